```python
import math
import jax, jax.numpy as jnp
from jax import lax
import numpy as np

D_MODEL = 2048
BATCH = 8
SEQ = 2048
DEPTH = 2

GATE_WIDTH = D_MODEL
SB_WIDTH = D_MODEL // 2
SB_HEAD_DIM = 128
SB_HEADS = SB_WIDTH // SB_HEAD_DIM
SB_BLOCK = 128
POOL_WIDTH = D_MODEL - SB_WIDTH
POOL_WINDOWS = (2, 4, 8, 16)
POOL_GROUPS = len(POOL_WINDOWS)
POOL_GROUP_DIM = POOL_WIDTH // POOL_GROUPS
EVEN_IN = 3 * SB_WIDTH + POOL_WIDTH + GATE_WIDTH
SCONV_WIDTH = D_MODEL // 2
SCONV_K = 3
CONF_WIDTH = D_MODEL - SCONV_WIDTH
CONF_K = 31
ODD_IN = 3 * SCONV_WIDTH + 2 * CONF_WIDTH + GATE_WIDTH
N_EVEN = (DEPTH + 1) // 2
N_ODD = DEPTH // 2
EPS = 1e-6

kernel_name = "hybrid_stickbreak_pool_shortconv_conformer"


def rms_norm(x, g):
    xf = x.astype(jnp.float32)
    y = xf * lax.rsqrt(jnp.mean(xf * xf, axis=-1, keepdims=True) + EPS)
    return (y * g.astype(jnp.float32)).astype(x.dtype)


def layer_norm(x, g, b):
    xf = x.astype(jnp.float32)
    mu = jnp.mean(xf, axis=-1, keepdims=True)
    var = jnp.mean(jnp.square(xf - mu), axis=-1, keepdims=True)
    y = (xf - mu) * lax.rsqrt(var + EPS)
    return (y * g.astype(jnp.float32) + b.astype(jnp.float32)).astype(x.dtype)


def causal_depthwise_conv(x, w):
    k, c = w.shape
    return lax.conv_general_dilated(
        x, w.astype(x.dtype)[:, None, :], window_strides=(1,), padding=[(k - 1, 0)],
        dimension_numbers=("NWC", "WIO", "NWC"), feature_group_count=c)


def stick_breaking_attention(q, k, v):
    b, s_len, h, dh = q.shape
    scale = 1.0 / math.sqrt(dh)
    outs = []
    for qb in range(s_len // SB_BLOCK):
        q0 = qb * SB_BLOCK
        kend = q0 + SB_BLOCK
        z = jnp.einsum("bqhd,bkhd->bhqk", q[:, q0:kend], k[:, :kend]).astype(jnp.float32) * scale
        t_idx = q0 + jnp.arange(SB_BLOCK)[:, None]
        s_idx = jnp.arange(kend)[None, :]
        mask = s_idx < t_idx
        log_beta = jax.nn.log_sigmoid(z)
        log_1m = jnp.where(mask, jax.nn.log_sigmoid(-z), 0.0)
        log_stay = lax.cumsum(log_1m, axis=3, reverse=True) - log_1m
        wts = jnp.where(mask, jnp.exp(log_beta + log_stay), 0.0)
        outs.append(jnp.einsum("bhqk,bkhd->bqhd", wts.astype(v.dtype), v[:, :kend]))
    return jnp.concatenate(outs, axis=1)


def multiscale_pool(u, pool_w, pool_scale):
    b, s_len, _ = u.shape
    ug = u.reshape(b, s_len, POOL_GROUPS, POOL_GROUP_DIM)
    cs = jnp.cumsum(ug.astype(jnp.float32), axis=1)
    pos1 = jnp.arange(1, s_len + 1)
    pooled = []
    for gi, win in enumerate(POOL_WINDOWS):
        c = cs[:, :, gi]
        prev = jnp.pad(c, ((0, 0), (win, 0), (0, 0)))[:, :s_len]
        count = jnp.minimum(win, pos1).astype(jnp.float32)[None, :, None]
        pooled.append(((c - prev) / count).astype(u.dtype) - ug[:, :, gi])
    pooled = jnp.stack(pooled, axis=2)
    y = jnp.einsum("bsgc,gcd->bsgd", pooled, pool_w).reshape(b, s_len, POOL_WIDTH)
    return y * pool_scale


def even_mixer(h, w_in, pool_w, pool_scale, w_out):
    b, s_len, _ = h.shape
    p = h @ w_in
    q, k, v, u, g = jnp.split(p, np.cumsum([SB_WIDTH, SB_WIDTH, SB_WIDTH, POOL_WIDTH]).tolist(), axis=-1)
    hs = (b, s_len, SB_HEADS, SB_HEAD_DIM)
    a = stick_breaking_attention(q.reshape(hs), k.reshape(hs), v.reshape(hs)).reshape(b, s_len, SB_WIDTH)
    po = multiscale_pool(u, pool_w, pool_scale)
    y = jnp.concatenate([a, po], axis=-1) * jax.nn.silu(g)
    return y @ w_out


def odd_mixer(h, w_in, sconv_w, dconv_w, dconv_b, cnorm_g, cnorm_b, w_out):
    p = h @ w_in
    hc, bc, cc, ga, gb, g = jnp.split(
        p, np.cumsum([SCONV_WIDTH, SCONV_WIDTH, SCONV_WIDTH, CONF_WIDTH, CONF_WIDTH]).tolist(), axis=-1)
    c_out = bc * causal_depthwise_conv(cc * hc, sconv_w)
    d = ga * jax.nn.sigmoid(gb)
    d = causal_depthwise_conv(d, dconv_w) + dconv_b
    d = jax.nn.silu(layer_norm(d, cnorm_g, cnorm_b))
    y = jnp.concatenate([c_out, d], axis=-1) * jax.nn.silu(g)
    return y @ w_out


def _fwd_setup_inputs(seed: int = 0) -> dict:
    key = jax.random.key(seed)
    ks = jax.random.split(key, 20)
    f32 = jnp.float32
    nrm = lambda k, shape, s: jax.random.normal(k, shape, f32) * s
    return {
        "x": jax.random.normal(ks[0], (BATCH, SEQ, D_MODEL), f32),
        "ln_pre_even": 1.0 + nrm(ks[1], (N_EVEN, D_MODEL), 0.05),
        "w_in_even": nrm(ks[2], (N_EVEN, D_MODEL, EVEN_IN), D_MODEL ** -0.5),
        "pool_w": nrm(ks[3], (N_EVEN, POOL_GROUPS, POOL_GROUP_DIM, POOL_GROUP_DIM), POOL_GROUP_DIM ** -0.5),
        "pool_scale": 1.0 + nrm(ks[4], (N_EVEN, POOL_WIDTH), 0.1),
        "w_out_even": nrm(ks[5], (N_EVEN, D_MODEL, D_MODEL), D_MODEL ** -0.5),
        "ln_post_even": 1.0 + nrm(ks[6], (N_EVEN, D_MODEL), 0.05),
        "ln_pre_odd": 1.0 + nrm(ks[7], (N_ODD, D_MODEL), 0.05),
        "w_in_odd": nrm(ks[8], (N_ODD, D_MODEL, ODD_IN), D_MODEL ** -0.5),
        "sconv_w": nrm(ks[9], (N_ODD, SCONV_K, SCONV_WIDTH), SCONV_K ** -0.5),
        "dconv_w": nrm(ks[10], (N_ODD, CONF_K, CONF_WIDTH), CONF_K ** -0.5),
        "dconv_b": nrm(ks[11], (N_ODD, CONF_WIDTH), 0.02),
        "cnorm_g": 1.0 + nrm(ks[12], (N_ODD, CONF_WIDTH), 0.05),
        "cnorm_b": nrm(ks[13], (N_ODD, CONF_WIDTH), 0.02),
        "w_out_odd": nrm(ks[14], (N_ODD, D_MODEL, D_MODEL), D_MODEL ** -0.5),
        "ln_post_odd": 1.0 + nrm(ks[15], (N_ODD, D_MODEL), 0.05),
    }


def _fwd_reference(x, ln_pre_even, w_in_even, pool_w, pool_scale, w_out_even, ln_post_even,
              ln_pre_odd, w_in_odd, sconv_w, dconv_w, dconv_b, cnorm_g, cnorm_b, w_out_odd, ln_post_odd):
    for layer in range(DEPTH):
        i = layer // 2
        if layer % 2 == 0:
            h = rms_norm(x, ln_pre_even[i])
            o = even_mixer(h, w_in_even[i], pool_w[i], pool_scale[i], w_out_even[i])
            x = x + rms_norm(o, ln_post_even[i])
        else:
            h = rms_norm(x, ln_pre_odd[i])
            o = odd_mixer(h, w_in_odd[i], sconv_w[i], dconv_w[i], dconv_b[i], cnorm_g[i], cnorm_b[i], w_out_odd[i])
            x = x + rms_norm(o, ln_post_odd[i])
    return x


import jax as _jax
import jax.numpy as _jnp

TWIN_FORMAT = 'train_step'
FWD_PARAMS = ['x', 'ln_pre_even', 'w_in_even', 'pool_w', 'pool_scale', 'w_out_even', 'ln_post_even', 'ln_pre_odd', 'w_in_odd', 'sconv_w', 'dconv_w', 'dconv_b', 'cnorm_g', 'cnorm_b', 'w_out_odd', 'ln_post_odd']
TWIN_WEIGHTS = ['ln_pre_even', 'w_in_even', 'pool_w', 'pool_scale', 'w_out_even', 'ln_post_even', 'ln_pre_odd', 'w_in_odd', 'sconv_w', 'dconv_w', 'dconv_b', 'cnorm_g', 'cnorm_b', 'w_out_odd', 'ln_post_odd']
TWIN_DIFF_INPUT = 'x'
TWIN_INPUTS = ['x', 'ln_pre_even', 'w_in_even', 'pool_w', 'pool_scale', 'w_out_even', 'ln_post_even', 'ln_pre_odd', 'w_in_odd', 'sconv_w', 'dconv_w', 'dconv_b', 'cnorm_g', 'cnorm_b', 'w_out_odd', 'ln_post_odd', 'loss_target', 'm_ln_pre_even', 'm_w_in_even', 'm_pool_w', 'm_pool_scale', 'm_w_out_even', 'm_ln_post_even', 'm_ln_pre_odd', 'm_w_in_odd', 'm_sconv_w', 'm_dconv_w', 'm_dconv_b', 'm_cnorm_g', 'm_cnorm_b', 'm_w_out_odd', 'm_ln_post_odd', 'v_ln_pre_even', 'v_w_in_even', 'v_pool_w', 'v_pool_scale', 'v_w_out_even', 'v_ln_post_even', 'v_ln_pre_odd', 'v_w_in_odd', 'v_sconv_w', 'v_dconv_w', 'v_dconv_b', 'v_cnorm_g', 'v_cnorm_b', 'v_w_out_odd', 'v_ln_post_odd']
TWIN_OUTPUTS = ['loss', 'grad_x', 'grad_ln_pre_even', 'grad_w_in_even', 'grad_pool_w', 'grad_pool_scale', 'grad_w_out_even', 'grad_ln_post_even', 'grad_ln_pre_odd', 'grad_w_in_odd', 'grad_sconv_w', 'grad_dconv_w', 'grad_dconv_b', 'grad_cnorm_g', 'grad_cnorm_b', 'grad_w_out_odd', 'grad_ln_post_odd', 'delta_ln_pre_even', 'delta_w_in_even', 'delta_pool_w', 'delta_pool_scale', 'delta_w_out_even', 'delta_ln_post_even', 'delta_ln_pre_odd', 'delta_w_in_odd', 'delta_sconv_w', 'delta_dconv_w', 'delta_dconv_b', 'delta_cnorm_g', 'delta_cnorm_b', 'delta_w_out_odd', 'delta_ln_post_odd', 'new_m_ln_pre_even', 'new_m_w_in_even', 'new_m_pool_w', 'new_m_pool_scale', 'new_m_w_out_even', 'new_m_ln_post_even', 'new_m_ln_pre_odd', 'new_m_w_in_odd', 'new_m_sconv_w', 'new_m_dconv_w', 'new_m_dconv_b', 'new_m_cnorm_g', 'new_m_cnorm_b', 'new_m_w_out_odd', 'new_m_ln_post_odd', 'new_v_ln_pre_even', 'new_v_w_in_even', 'new_v_pool_w', 'new_v_pool_scale', 'new_v_w_out_even', 'new_v_ln_post_even', 'new_v_ln_pre_odd', 'new_v_w_in_odd', 'new_v_sconv_w', 'new_v_dconv_w', 'new_v_dconv_b', 'new_v_cnorm_g', 'new_v_cnorm_b', 'new_v_w_out_odd', 'new_v_ln_post_odd']
TWIN_LEAF_KINDS = {'loss': 'loss', 'grad_x': 'grad_x', 'grad_ln_pre_even': 'grad_w', 'grad_w_in_even': 'grad_w', 'grad_pool_w': 'grad_w', 'grad_pool_scale': 'grad_w', 'grad_w_out_even': 'grad_w', 'grad_ln_post_even': 'grad_w', 'grad_ln_pre_odd': 'grad_w', 'grad_w_in_odd': 'grad_w', 'grad_sconv_w': 'grad_w', 'grad_dconv_w': 'grad_w', 'grad_dconv_b': 'grad_w', 'grad_cnorm_g': 'grad_w', 'grad_cnorm_b': 'grad_w', 'grad_w_out_odd': 'grad_w', 'grad_ln_post_odd': 'grad_w', 'delta_ln_pre_even': 'delta_w', 'delta_w_in_even': 'delta_w', 'delta_pool_w': 'delta_w', 'delta_pool_scale': 'delta_w', 'delta_w_out_even': 'delta_w', 'delta_ln_post_even': 'delta_w', 'delta_ln_pre_odd': 'delta_w', 'delta_w_in_odd': 'delta_w', 'delta_sconv_w': 'delta_w', 'delta_dconv_w': 'delta_w', 'delta_dconv_b': 'delta_w', 'delta_cnorm_g': 'delta_w', 'delta_cnorm_b': 'delta_w', 'delta_w_out_odd': 'delta_w', 'delta_ln_post_odd': 'delta_w', 'new_m_ln_pre_even': 'new_m', 'new_m_w_in_even': 'new_m', 'new_m_pool_w': 'new_m', 'new_m_pool_scale': 'new_m', 'new_m_w_out_even': 'new_m', 'new_m_ln_post_even': 'new_m', 'new_m_ln_pre_odd': 'new_m', 'new_m_w_in_odd': 'new_m', 'new_m_sconv_w': 'new_m', 'new_m_dconv_w': 'new_m', 'new_m_dconv_b': 'new_m', 'new_m_cnorm_g': 'new_m', 'new_m_cnorm_b': 'new_m', 'new_m_w_out_odd': 'new_m', 'new_m_ln_post_odd': 'new_m', 'new_v_ln_pre_even': 'new_v', 'new_v_w_in_even': 'new_v', 'new_v_pool_w': 'new_v', 'new_v_pool_scale': 'new_v', 'new_v_w_out_even': 'new_v', 'new_v_ln_post_even': 'new_v', 'new_v_ln_pre_odd': 'new_v', 'new_v_w_in_odd': 'new_v', 'new_v_sconv_w': 'new_v', 'new_v_dconv_w': 'new_v', 'new_v_dconv_b': 'new_v', 'new_v_cnorm_g': 'new_v', 'new_v_cnorm_b': 'new_v', 'new_v_w_out_odd': 'new_v', 'new_v_ln_post_odd': 'new_v'}


def _forward(args):
    return _fwd_reference(*[args[k] for k in FWD_PARAMS])


def _output_shape():
    out = _jax.eval_shape(lambda: _forward(_fwd_setup_inputs(0)))
    return out.shape, out.dtype

N_MICROBATCH = 1
ADAM_LR = 0.001
ADAM_B1 = 0.9
ADAM_B2 = 0.999
ADAM_EPS = 1e-08
ADAM_WD = 0.01
ADAM_STEP = 10
PER_EXAMPLE_BATCH_AXIS = {'x': 0, 'loss_target': 0}
SHARED_INPUTS = []
_WEIGHT_DTYPES = {'ln_pre_even': _jnp.float32, 'w_in_even': _jnp.float32, 'pool_w': _jnp.float32, 'pool_scale': _jnp.float32, 'w_out_even': _jnp.float32, 'ln_post_even': _jnp.float32, 'ln_pre_odd': _jnp.float32, 'w_in_odd': _jnp.float32, 'sconv_w': _jnp.float32, 'dconv_w': _jnp.float32, 'dconv_b': _jnp.float32, 'cnorm_g': _jnp.float32, 'cnorm_b': _jnp.float32, 'w_out_odd': _jnp.float32, 'ln_post_odd': _jnp.float32}
MOMENT_SCALE = {'ln_pre_even': 2.722233e-01, 'w_in_even': 1.582575e-01, 'pool_w': 2.134645e-01, 'pool_scale': 2.084527e-01, 'w_out_even': 1.866823e-01, 'ln_post_even': 8.025156e+00, 'ln_pre_odd': 2.098717e-01, 'w_in_odd': 1.120392e-01, 'sconv_w': 1.327373e-01, 'dconv_w': 8.399220e-02, 'dconv_b': 2.189121e-01, 'cnorm_g': 1.167427e-01, 'cnorm_b': 1.301269e-01, 'w_out_odd': 1.138713e-01, 'ln_post_odd': 8.015676e+00}


def _to_microbatches(a, axis):
    t = _jnp.moveaxis(a, axis, 0)
    t = t.reshape((N_MICROBATCH, t.shape[0] // N_MICROBATCH) + t.shape[1:])
    return _jnp.moveaxis(t, 1, axis + 1)


def setup_inputs(seed: int = 0) -> dict:
    inp = _fwd_setup_inputs(seed)
    key = _jax.random.fold_in(_jax.random.key(seed), 7919)
    shape, _ = _output_shape()
    out = dict(inp)
    out["loss_target"] = _jax.random.normal(_jax.random.fold_in(key, 0), shape, _jnp.float32)
    for i, name in enumerate(TWIN_WEIGHTS):
        w = inp[name].astype(_jnp.float32)
        if MOMENT_SCALE is None:
            s = _jnp.sqrt(_jnp.mean(_jnp.square(w)) + 1e-30)
        else:
            s = MOMENT_SCALE[name]
        km, kv = _jax.random.split(_jax.random.fold_in(key, i + 1))
        out[name] = w
        out["m_" + name] = s * _jax.random.normal(km, w.shape, _jnp.float32)
        out["v_" + name] = (s * s) * _jax.random.uniform(kv, w.shape, _jnp.float32, 0.5, 1.5)
    if N_MICROBATCH > 1:
        for name, axis in PER_EXAMPLE_BATCH_AXIS.items():
            out[name] = _to_microbatches(out[name], axis)
    return {'x': out['x'], 'ln_pre_even': out['ln_pre_even'], 'w_in_even': out['w_in_even'], 'pool_w': out['pool_w'], 'pool_scale': out['pool_scale'], 'w_out_even': out['w_out_even'], 'ln_post_even': out['ln_post_even'], 'ln_pre_odd': out['ln_pre_odd'], 'w_in_odd': out['w_in_odd'], 'sconv_w': out['sconv_w'], 'dconv_w': out['dconv_w'], 'dconv_b': out['dconv_b'], 'cnorm_g': out['cnorm_g'], 'cnorm_b': out['cnorm_b'], 'w_out_odd': out['w_out_odd'], 'ln_post_odd': out['ln_post_odd'], 'loss_target': out['loss_target'], 'm_ln_pre_even': out['m_ln_pre_even'], 'm_w_in_even': out['m_w_in_even'], 'm_pool_w': out['m_pool_w'], 'm_pool_scale': out['m_pool_scale'], 'm_w_out_even': out['m_w_out_even'], 'm_ln_post_even': out['m_ln_post_even'], 'm_ln_pre_odd': out['m_ln_pre_odd'], 'm_w_in_odd': out['m_w_in_odd'], 'm_sconv_w': out['m_sconv_w'], 'm_dconv_w': out['m_dconv_w'], 'm_dconv_b': out['m_dconv_b'], 'm_cnorm_g': out['m_cnorm_g'], 'm_cnorm_b': out['m_cnorm_b'], 'm_w_out_odd': out['m_w_out_odd'], 'm_ln_post_odd': out['m_ln_post_odd'], 'v_ln_pre_even': out['v_ln_pre_even'], 'v_w_in_even': out['v_w_in_even'], 'v_pool_w': out['v_pool_w'], 'v_pool_scale': out['v_pool_scale'], 'v_w_out_even': out['v_w_out_even'], 'v_ln_post_even': out['v_ln_post_even'], 'v_ln_pre_odd': out['v_ln_pre_odd'], 'v_w_in_odd': out['v_w_in_odd'], 'v_sconv_w': out['v_sconv_w'], 'v_dconv_w': out['v_dconv_w'], 'v_dconv_b': out['v_dconv_b'], 'v_cnorm_g': out['v_cnorm_g'], 'v_cnorm_b': out['v_cnorm_b'], 'v_w_out_odd': out['v_w_out_odd'], 'v_ln_post_odd': out['v_ln_post_odd']}


def _loss(weights, diff, rest, loss_target):
    with _jax.named_scope("forward"):
        args = {**rest, TWIN_DIFF_INPUT: diff, **{k: w.astype(_WEIGHT_DTYPES[k]) for k, w in weights.items()}}
        y = _forward(args)
    with _jax.named_scope("loss_head"):
        err = _jnp.square(y.astype(_jnp.float32) - loss_target)
        return 0.5 * _jnp.sum(_jnp.mean(err, axis=-1)) if err.ndim else 0.5 * err


def _adamw(w, g, m, v):
    m = ADAM_B1 * m + (1.0 - ADAM_B1) * g
    v = ADAM_B2 * v + (1.0 - ADAM_B2) * _jnp.square(g)
    m_hat = m / (1.0 - ADAM_B1 ** ADAM_STEP)
    v_hat = v / (1.0 - ADAM_B2 ** ADAM_STEP)
    delta = -ADAM_LR * (m_hat / (_jnp.sqrt(v_hat) + ADAM_EPS) + ADAM_WD * w)
    return delta, m, v


def reference(x, ln_pre_even, w_in_even, pool_w, pool_scale, w_out_even, ln_post_even, ln_pre_odd, w_in_odd, sconv_w, dconv_w, dconv_b, cnorm_g, cnorm_b, w_out_odd, ln_post_odd, loss_target, m_ln_pre_even, m_w_in_even, m_pool_w, m_pool_scale, m_w_out_even, m_ln_post_even, m_ln_pre_odd, m_w_in_odd, m_sconv_w, m_dconv_w, m_dconv_b, m_cnorm_g, m_cnorm_b, m_w_out_odd, m_ln_post_odd, v_ln_pre_even, v_w_in_even, v_pool_w, v_pool_scale, v_w_out_even, v_ln_post_even, v_ln_pre_odd, v_w_in_odd, v_sconv_w, v_dconv_w, v_dconv_b, v_cnorm_g, v_cnorm_b, v_w_out_odd, v_ln_post_odd):
    given = dict(x=x, ln_pre_even=ln_pre_even, w_in_even=w_in_even, pool_w=pool_w, pool_scale=pool_scale, w_out_even=w_out_even, ln_post_even=ln_post_even, ln_pre_odd=ln_pre_odd, w_in_odd=w_in_odd, sconv_w=sconv_w, dconv_w=dconv_w, dconv_b=dconv_b, cnorm_g=cnorm_g, cnorm_b=cnorm_b, w_out_odd=w_out_odd, ln_post_odd=ln_post_odd, loss_target=loss_target, m_ln_pre_even=m_ln_pre_even, m_w_in_even=m_w_in_even, m_pool_w=m_pool_w, m_pool_scale=m_pool_scale, m_w_out_even=m_w_out_even, m_ln_post_even=m_ln_post_even, m_ln_pre_odd=m_ln_pre_odd, m_w_in_odd=m_w_in_odd, m_sconv_w=m_sconv_w, m_dconv_w=m_dconv_w, m_dconv_b=m_dconv_b, m_cnorm_g=m_cnorm_g, m_cnorm_b=m_cnorm_b, m_w_out_odd=m_w_out_odd, m_ln_post_odd=m_ln_post_odd, v_ln_pre_even=v_ln_pre_even, v_w_in_even=v_w_in_even, v_pool_w=v_pool_w, v_pool_scale=v_pool_scale, v_w_out_even=v_w_out_even, v_ln_post_even=v_ln_post_even, v_ln_pre_odd=v_ln_pre_odd, v_w_in_odd=v_w_in_odd, v_sconv_w=v_sconv_w, v_dconv_w=v_dconv_w, v_dconv_b=v_dconv_b, v_cnorm_g=v_cnorm_g, v_cnorm_b=v_cnorm_b, v_w_out_odd=v_w_out_odd, v_ln_post_odd=v_ln_post_odd)
    weights = {n: given[n] for n in TWIN_WEIGHTS}
    shared = {n: given[n] for n in SHARED_INPUTS}
    per_example = {n: given[n] for n in ['x']}
    grad_fn = _jax.value_and_grad(_loss, argnums=(0, 1))

    def one_microbatch(ex, loss_target):
        ex = dict(ex)
        diff = ex.pop(TWIN_DIFF_INPUT)
        return grad_fn(weights, diff, {**shared, **ex}, loss_target)

    if N_MICROBATCH == 1:
        loss, (grad_w, grad_x) = one_microbatch(per_example, given["loss_target"])
    else:
        def body(carry, xs):
            loss_sum, grad_sum = carry
            l_k, (gw_k, gx_k) = one_microbatch(xs[0], xs[1])
            with _jax.named_scope("update"):
                return (loss_sum + l_k, _jax.tree.map(_jnp.add, grad_sum, gw_k)), gx_k

        init = (_jnp.zeros((), _jnp.float32), _jax.tree.map(_jnp.zeros_like, weights))
        (loss, grad_w), grad_x = _jax.lax.scan(body, init, (per_example, given["loss_target"]))
    with _jax.named_scope("update"):
        delta_w, new_m, new_v = {}, {}, {}
        for n in TWIN_WEIGHTS:
            delta_w[n], new_m[n], new_v[n] = _adamw(weights[n], grad_w[n], given["m_" + n], given["v_" + n])
    return (loss, grad_x, *[grad_w[n] for n in TWIN_WEIGHTS], *[delta_w[n] for n in TWIN_WEIGHTS],
            *[new_m[n] for n in TWIN_WEIGHTS], *[new_v[n] for n in TWIN_WEIGHTS])
```

```python
import functools
import math

import jax
import jax.numpy as jnp
from jax import lax
from jax.experimental import pallas as pl
from jax.experimental.pallas import tpu as pltpu

F32 = jnp.float32
BF16 = jnp.bfloat16
EPS = 1e-6
N_CHIPS = 4
VMEM_LIMIT_V7X = 56 << 20
HEAD_DIM = 128
ATT_BLOCK = 256
POOL_WINDOWS = (2, 4, 8, 16)
ROW_TILE = 256
POOL_HALO = 16
CONV_HALO = 32
LANES = 128
ADAM_LR, ADAM_B1, ADAM_B2, ADAM_EPS, ADAM_WD, ADAM_STEP = 0.001, 0.9, 0.999, 1e-08, 0.01, 10
MESH_ID = pl.DeviceIdType.MESH
ANY = pl.BlockSpec(memory_space=pl.ANY)


def _cp(*sem):
    return pltpu.CompilerParams(dimension_semantics=sem or None, vmem_limit_bytes=VMEM_LIMIT_V7X)


def _pick_tile(n, cap):
    best = None
    for t in range(LANES, min(n, cap) + 1, LANES):
        if n % t == 0:
            best = t
    assert best is not None, (n, cap)
    return best


def _sigmoid(x):
    return 1.0 / (1.0 + jnp.exp(-x))


def _silu(x):
    return x * _sigmoid(x)


def _dsilu(x):
    s = _sigmoid(x)
    return s * (1.0 + x * (1.0 - s))


def _log_sigmoid(z):
    return jnp.minimum(z, 0.0) - jnp.log(1.0 + jnp.exp(-jnp.abs(z)))


def _rms_stats(x):
    r = lax.rsqrt(jnp.mean(x * x, axis=-1, keepdims=True) + EPS)
    return x * r, r


def _rms_bwd(dh, xhat, r, g):
    dxh = dh * g
    dx = r * (dxh - xhat * jnp.mean(dxh * xhat, axis=-1, keepdims=True))
    return dx, jnp.sum(dh * xhat, axis=0, keepdims=True)


def _acc_rows(ref, first, val):
    @pl.when(first)
    def _():
        ref[...] = val

    @pl.when(jnp.logical_not(first))
    def _():
        ref[...] += val


def _cast_bf16(x, name):
    r, c = x.shape
    tr = ROW_TILE if r % ROW_TILE == 0 else r

    def body(x_ref, o_ref):
        o_ref[...] = x_ref[...].astype(BF16)

    return pl.pallas_call(
        body, name=name, grid=(r // tr,),
        in_specs=[pl.BlockSpec((tr, c), lambda i: (i, 0))],
        out_specs=pl.BlockSpec((tr, c), lambda i: (i, 0)),
        out_shape=jax.ShapeDtypeStruct((r, c), BF16), compiler_params=_cp("parallel"))(x)


def _rms_fwd(x, g, name):
    s, d = x.shape

    def body(x_ref, g_ref, h_ref):
        xhat, _ = _rms_stats(x_ref[...])
        h_ref[...] = (xhat * g_ref[...]).astype(BF16)

    return pl.pallas_call(
        body, name=name, grid=(s // ROW_TILE,),
        in_specs=[pl.BlockSpec((ROW_TILE, d), lambda i: (i, 0)), pl.BlockSpec((1, d), lambda i: (0, 0))],
        out_specs=pl.BlockSpec((ROW_TILE, d), lambda i: (i, 0)),
        out_shape=jax.ShapeDtypeStruct((s, d), BF16), compiler_params=_cp("parallel"))(x, g)


def _mm_nn(a, w3, name):
    m, k = a.shape
    nsh, _, ns = w3.shape
    tm = 512 if m % 512 == 0 else ROW_TILE
    tn = _pick_tile(ns, 1024)
    per = ns // tn

    def body(a_ref, w_ref, o_ref):
        o_ref[...] = jnp.dot(a_ref[...], w_ref[...], preferred_element_type=F32)

    return pl.pallas_call(
        body, name=name, grid=(nsh * per, m // tm),
        in_specs=[pl.BlockSpec((tm, k), lambda n, i: (i, 0)),
                  pl.BlockSpec((None, k, tn), lambda n, i: (n // per, 0, n % per))],
        out_specs=pl.BlockSpec((tm, tn), lambda n, i: (i, n)),
        out_shape=jax.ShapeDtypeStruct((m, nsh * ns), F32), compiler_params=_cp("parallel", "parallel"))(a, w3)


def _mm_nt(a, b, name):
    m, k = a.shape
    n = b.shape[0]
    tm = 512 if m % 512 == 0 else ROW_TILE

    def body(a_ref, b_ref, o_ref):
        o_ref[...] = lax.dot_general(a_ref[...], b_ref[...], (((1,), (1,)), ((), ())), preferred_element_type=F32)

    return pl.pallas_call(
        body, name=name, grid=(m // tm,),
        in_specs=[pl.BlockSpec((tm, k), lambda i: (i, 0)), pl.BlockSpec((n, k), lambda i: (0, 0))],
        out_specs=pl.BlockSpec((tm, n), lambda i: (i, 0)),
        out_shape=jax.ShapeDtypeStruct((m, n), F32), compiler_params=_cp("parallel"))(a, b)


def _mm_tn(a, b, nsh, name):
    s, m = a.shape
    n = b.shape[1]
    ns = n // nsh
    tm = 512 if m % 512 == 0 else ROW_TILE
    tn = _pick_tile(ns, 1024)
    per = ns // tn

    def body(a_ref, b_ref, o_ref):
        o_ref[...] = lax.dot_general(a_ref[...], b_ref[...], (((0,), (0,)), ((), ())),
                                     preferred_element_type=F32).astype(BF16)

    return pl.pallas_call(
        body, name=name, grid=(nsh * per, m // tm),
        in_specs=[pl.BlockSpec((s, tm), lambda j, i: (0, i)), pl.BlockSpec((s, tn), lambda j, i: (0, j))],
        out_specs=pl.BlockSpec((None, tm, tn), lambda j, i: (j // per, i, j % per)),
        out_shape=jax.ShapeDtypeStruct((nsh, m, ns), BF16), compiler_params=_cp("parallel", "parallel"))(a, b)


def _tri(n, rel):
    row = lax.broadcasted_iota(jnp.int32, (n, n), 0)
    col = lax.broadcasted_iota(jnp.int32, (n, n), 1)
    return jnp.where(rel(row, col), 1.0, 0.0).astype(BF16)


def _dot_split(x, tri):
    hi = x.astype(BF16)
    lo = (x - hi.astype(F32)).astype(BF16)
    return jnp.dot(hi, tri, preferred_element_type=F32) + jnp.dot(lo, tri, preferred_element_type=F32)


def _nt(a, b):
    return lax.dot_general(a, b, (((1,), (1,)), ((), ())), preferred_element_type=F32)


def _tn(a, b):
    return lax.dot_general(a, b, (((0,), (0,)), ((), ())), preferred_element_type=F32)


def _sba_fwd(p, sbw, name):
    s = p.shape[0]
    nh = sbw // HEAD_DIM
    blk = ATT_BLOCK
    scale = 1.0 / math.sqrt(HEAD_DIM)

    def body(q_ref, k_ref, v_ref, o_ref, lt_ref, kb_ref, vb_ref):
        i = pl.program_id(1)

        @pl.when(i == 0)
        def _():
            kb_ref[...] = k_ref[...].astype(BF16)
            vb_ref[...] = v_ref[...].astype(BF16)

        q = q_ref[...].astype(BF16)
        later = _tri(blk, lambda r, c: r > c)
        t_idx = i * blk + lax.broadcasted_iota(jnp.int32, (blk, blk), 0)
        col = lax.broadcasted_iota(jnp.int32, (blk, blk), 1)

        def step(n, carry):
            acc, run = carry
            j = i - n
            rows = pl.ds(pl.multiple_of(j * blk, blk), blk)
            mask = (j * blk + col) < t_idx
            z = _nt(q, kb_ref[rows, :]) * scale
            ls = _log_sigmoid(z)
            lm = jnp.where(mask, ls - z, 0.0)
            stay = _dot_split(lm, later) + run
            w = jnp.where(mask, jnp.exp(ls + stay), 0.0)
            acc = acc + jnp.dot(w.astype(BF16), vb_ref[rows, :], preferred_element_type=F32)
            return acc, run + jnp.sum(lm, axis=1, keepdims=True)

        acc, run = lax.fori_loop(0, i + 1, step, (jnp.zeros((blk, HEAD_DIM), F32), jnp.zeros((blk, 1), F32)))
        o_ref[...] = acc
        lt_ref[...] = jnp.broadcast_to(run, (blk, HEAD_DIM))

    return pl.pallas_call(
        body, name=name, grid=(nh, s // blk),
        in_specs=[pl.BlockSpec((blk, HEAD_DIM), lambda h, i: (i, h)),
                  pl.BlockSpec((s, HEAD_DIM), lambda h, i: (0, nh + h)),
                  pl.BlockSpec((s, HEAD_DIM), lambda h, i: (0, 2 * nh + h))],
        out_specs=[pl.BlockSpec((blk, HEAD_DIM), lambda h, i: (i, h))] * 2,
        out_shape=[jax.ShapeDtypeStruct((s, sbw), F32)] * 2,
        scratch_shapes=[pltpu.VMEM((s, HEAD_DIM), BF16)] * 2,
        compiler_params=_cp("parallel", "arbitrary"))(p, p, p)


def _sba_bwd(p, out, ltot, dout, sbw, name):
    s = p.shape[0]
    nh = sbw // HEAD_DIM
    blk = ATT_BLOCK
    nq = s // blk
    scale = 1.0 / math.sqrt(HEAD_DIM)

    def body(q_ref, k_ref, v_ref, lt_ref, do_ref, dq_ref, dk_ref, dv_ref, kb_ref, vb_ref, dka_ref, dva_ref):
        i = pl.program_id(1)

        @pl.when(i == 0)
        def _():
            kb_ref[...] = k_ref[...].astype(BF16)
            vb_ref[...] = v_ref[...].astype(BF16)
            dka_ref[...] = jnp.zeros_like(dka_ref)
            dva_ref[...] = jnp.zeros_like(dva_ref)

        q = q_ref[...].astype(BF16)
        do = do_ref[...].astype(BF16)
        ltot_row = lt_ref[:, 0:1]
        upto = _tri(blk, lambda r, c: r <= c)
        before = _tri(blk, lambda r, c: r < c)
        t_idx = i * blk + lax.broadcasted_iota(jnp.int32, (blk, blk), 0)
        col = lax.broadcasted_iota(jnp.int32, (blk, blk), 1)

        def step(j, carry):
            dq, lsum, asum = carry
            rows = pl.ds(pl.multiple_of(j * blk, blk), blk)
            mask = (j * blk + col) < t_idx
            kj = kb_ref[rows, :]
            vj = vb_ref[rows, :]
            z = _nt(q, kj) * scale
            ls = _log_sigmoid(z)
            lm = jnp.where(mask, ls - z, 0.0)
            stay = ltot_row - lsum - _dot_split(lm, upto)
            w = jnp.where(mask, jnp.exp(ls + stay), 0.0)
            da = _nt(do, vj) * w
            e = asum + _dot_split(da, before)
            sig = jnp.exp(ls)
            dz = (da * (1.0 - sig) - jnp.where(mask, sig * e, 0.0)) * scale
            dzb = dz.astype(BF16)
            dq = dq + jnp.dot(dzb, kj, preferred_element_type=F32)
            dka_ref[rows, :] += _tn(dzb, q)
            dva_ref[rows, :] += _tn(w.astype(BF16), do)
            return dq, lsum + jnp.sum(lm, axis=1, keepdims=True), asum + jnp.sum(da, axis=1, keepdims=True)

        zero = jnp.zeros((blk, 1), F32)
        dq, _, _ = lax.fori_loop(0, i + 1, step, (jnp.zeros((blk, HEAD_DIM), F32), zero, zero))
        dq_ref[...] = dq.astype(BF16)

        @pl.when(i == nq - 1)
        def _():
            dk_ref[...] = dka_ref[...].astype(BF16)
            dv_ref[...] = dva_ref[...].astype(BF16)

    blk_spec = pl.BlockSpec((blk, HEAD_DIM), lambda h, i: (i, h))
    col_spec = pl.BlockSpec((s, HEAD_DIM), lambda h, i: (0, h))
    return pl.pallas_call(
        body, name=name, grid=(nh, nq),
        in_specs=[blk_spec,
                  pl.BlockSpec((s, HEAD_DIM), lambda h, i: (0, nh + h)),
                  pl.BlockSpec((s, HEAD_DIM), lambda h, i: (0, 2 * nh + h)),
                  blk_spec, blk_spec],
        out_specs=[blk_spec, col_spec, col_spec],
        out_shape=[jax.ShapeDtypeStruct((s, sbw), BF16)] * 3,
        scratch_shapes=[pltpu.VMEM((s, HEAD_DIM), BF16)] * 2 + [pltpu.VMEM((s, HEAD_DIM), F32)] * 2,
        compiler_params=_cp("parallel", "arbitrary"))(p, p, p, ltot, dout)


def _pool_groups(pad_ref, tile, row0, gd, halo):
    row = row0 + lax.broadcasted_iota(jnp.int32, (tile, 1), 0)
    out = []
    for gi, win in enumerate(POOL_WINDOWS):
        cs = slice(gi * gd, (gi + 1) * gd)
        tok = pad_ref[halo:halo + tile, cs]
        acc = tok
        for j in range(1, win):
            acc = acc + pad_ref[halo - j:halo - j + tile, cs]
        cnt = jnp.minimum(win, row + 1).astype(F32)
        out.append(acc / cnt - tok)
    return out


def _even_mix_fwd(p, att, pool_w, pool_scale, d, name):
    s = p.shape[0]
    half = d // 2
    gd = half // len(POOL_WINDOWS)
    t, hb = ROW_TILE, POOL_HALO

    def body(u_ref, uh_ref, g_ref, a_ref, pw_ref, sc_ref, y_ref, pad_ref):
        i = pl.program_id(0)
        pad_ref[0:hb, :] = jnp.where(i > 0, uh_ref[...], 0.0)
        pad_ref[hb:, :] = u_ref[...]
        pooled = _pool_groups(pad_ref, t, i * t, gd, hb)
        for gi in range(len(POOL_WINDOWS)):
            cs = slice(gi * gd, (gi + 1) * gd)
            po = jnp.dot(pooled[gi].astype(BF16), pw_ref[gi], preferred_element_type=F32) * sc_ref[:, cs]
            y_ref[:, half + gi * gd:half + (gi + 1) * gd] = (po * _silu(g_ref[:, half + gi * gd:half + (gi + 1) * gd])).astype(BF16)
        y_ref[:, :half] = (a_ref[...] * _silu(g_ref[:, :half])).astype(BF16)

    return pl.pallas_call(
        body, name=name, grid=(s // t,),
        in_specs=[pl.BlockSpec((t, half), lambda i: (i, 3)),
                  pl.BlockSpec((hb, half), lambda i: (jnp.maximum(i * (t // hb) - 1, 0), 3)),
                  pl.BlockSpec((t, d), lambda i: (i, 2)),
                  pl.BlockSpec((t, half), lambda i: (i, 0)),
                  pl.BlockSpec(pool_w.shape, lambda i: (0, 0, 0)),
                  pl.BlockSpec((1, half), lambda i: (0, 0))],
        out_specs=pl.BlockSpec((t, d), lambda i: (i, 0)),
        out_shape=jax.ShapeDtypeStruct((s, d), BF16),
        scratch_shapes=[pltpu.VMEM((hb + t, half), F32)],
        compiler_params=_cp("parallel"))(p, p, p, att, pool_w, pool_scale)


def _even_mix_bwd(p, att, dy, pool_w, pool_scale, d, name):
    s = p.shape[0]
    half = d // 2
    ng = len(POOL_WINDOWS)
    gd = half // ng
    t, hb = ROW_TILE, POOL_HALO
    nt = s // t

    def body(u_ref, uh_ref, g_ref, gh_ref, a_ref, dy_ref, dyh_ref, pw_ref, sc_ref,
             da_ref, du_ref, dg_ref, dsc_ref, dpw_ref, pad_ref, dn_ref):
        i = pl.program_id(0)
        first = i == 0
        pad_ref[0:hb, :] = jnp.where(i > 0, uh_ref[...], 0.0)
        pad_ref[hb:, :] = u_ref[...]
        pooled = _pool_groups(pad_ref, t, i * t, gd, hb)
        g1 = g_ref[:, :half]
        dy1 = dy_ref[:, :half]
        da_ref[...] = dy1 * _silu(g1)
        dg_ref[:, :half] = (dy1 * a_ref[...] * _dsilu(g1)).astype(BF16)
        row = i * t + lax.broadcasted_iota(jnp.int32, (t + hb, 1), 0)
        for gi, win in enumerate(POOL_WINDOWS):
            cs = slice(gi * gd, (gi + 1) * gd)
            cs2 = slice(half + gi * gd, half + (gi + 1) * gd)
            w = pw_ref[gi]
            pb = pooled[gi].astype(BF16)
            zp = jnp.dot(pb, w, preferred_element_type=F32)
            g2 = g_ref[:, cs2]
            dy2 = dy_ref[:, cs2]
            dg_ref[:, cs2] = (dy2 * zp * sc_ref[:, cs] * _dsilu(g2)).astype(BF16)
            dpo = dy2 * _silu(g2)
            _acc_rows(dsc_ref.at[:, cs], first, jnp.sum(dpo * zp, axis=0, keepdims=True))
            dz = (dpo * sc_ref[:, cs]).astype(BF16)
            _acc_rows(dpw_ref.at[gi], first, _tn(pb, dz))
            dzh = jnp.where(i < nt - 1, dyh_ref[:, cs] * _silu(gh_ref[:, cs]) * sc_ref[:, cs], 0.0).astype(BF16)
            dpool = _nt(dz, w)
            dpool_h = _nt(dzh, w)
            cnt = jnp.minimum(win, row + 1).astype(F32)
            dn_ref[0:t, cs] = dpool / cnt[0:t]
            dn_ref[t:, cs] = dpool_h / cnt[t:]
            acc = dn_ref[0:t, cs]
            for j in range(1, win):
                acc = acc + dn_ref[j:j + t, cs]
            du_ref[:, cs] = (acc - dpool).astype(BF16)

    return pl.pallas_call(
        body, name=name, grid=(nt,),
        in_specs=[pl.BlockSpec((t, half), lambda i: (i, 3)),
                  pl.BlockSpec((hb, half), lambda i: (jnp.maximum(i * (t // hb) - 1, 0), 3)),
                  pl.BlockSpec((t, d), lambda i: (i, 2)),
                  pl.BlockSpec((hb, half), lambda i: (jnp.minimum((i + 1) * (t // hb), s // hb - 1), 5)),
                  pl.BlockSpec((t, half), lambda i: (i, 0)),
                  pl.BlockSpec((t, d), lambda i: (i, 0)),
                  pl.BlockSpec((hb, half), lambda i: (jnp.minimum((i + 1) * (t // hb), s // hb - 1), 1)),
                  pl.BlockSpec(pool_w.shape, lambda i: (0, 0, 0)),
                  pl.BlockSpec((1, half), lambda i: (0, 0))],
        out_specs=[pl.BlockSpec((t, half), lambda i: (i, 0)),
                   pl.BlockSpec((t, half), lambda i: (i, 0)),
                   pl.BlockSpec((t, d), lambda i: (i, 0)),
                   pl.BlockSpec((1, half), lambda i: (0, 0)),
                   pl.BlockSpec((ng, gd, gd), lambda i: (0, 0, 0))],
        out_shape=[jax.ShapeDtypeStruct((s, half), F32), jax.ShapeDtypeStruct((s, half), BF16),
                   jax.ShapeDtypeStruct((s, d), BF16), jax.ShapeDtypeStruct((1, half), F32),
                   jax.ShapeDtypeStruct((ng, gd, gd), F32)],
        scratch_shapes=[pltpu.VMEM((hb + t, half), F32), pltpu.VMEM((t + hb, half), F32)],
        compiler_params=_cp("arbitrary"))(p, p, p, p, att, dy, dy, pool_w, pool_scale)


def _mm_out_even(y, w, x, g_post, g_pre_next, name):
    s, k = y.shape
    d = w.shape[1]
    t = ROW_TILE

    def body(y_ref, w_ref, x_ref, gp_ref, gn_ref, o_ref, x1_ref, h1_ref):
        o = jnp.dot(y_ref[...], w_ref[...], preferred_element_type=F32)
        o_ref[...] = o
        ohat, _ = _rms_stats(o)
        x1 = x_ref[...] + ohat * gp_ref[...]
        x1_ref[...] = x1
        xhat, _ = _rms_stats(x1)
        h1_ref[...] = (xhat * gn_ref[...]).astype(BF16)

    row = lambda c: pl.BlockSpec((t, c), lambda i: (i, 0))
    vec = pl.BlockSpec((1, d), lambda i: (0, 0))
    return pl.pallas_call(
        body, name=name, grid=(s // t,),
        in_specs=[row(k), pl.BlockSpec((k, d), lambda i: (0, 0)), row(d), vec, vec],
        out_specs=[row(d), row(d), row(d)],
        out_shape=[jax.ShapeDtypeStruct((s, d), F32), jax.ShapeDtypeStruct((s, d), F32),
                   jax.ShapeDtypeStruct((s, d), BF16)],
        compiler_params=_cp("parallel"))(y, w, x, g_post, g_pre_next)


def _mm_out_odd(y, w, x1, g_post, target, name):
    s, k = y.shape
    d = w.shape[1]
    t = ROW_TILE

    def body(y_ref, w_ref, x_ref, gp_ref, tg_ref, do_ref, dx_ref, loss_ref, dgp_ref):
        first = pl.program_id(0) == 0
        o = jnp.dot(y_ref[...], w_ref[...], preferred_element_type=F32)
        ohat, r = _rms_stats(o)
        gp = gp_ref[...]
        diff = x_ref[...] + ohat * gp - tg_ref[...]
        part = 0.5 * jnp.sum(jnp.mean(diff * diff, axis=-1, keepdims=True), axis=0, keepdims=True)
        _acc_rows(loss_ref, first, jnp.broadcast_to(part, loss_ref.shape))
        dx2 = diff * (1.0 / d)
        dx_ref[...] = dx2
        do, dgp = _rms_bwd(dx2, ohat, r, gp)
        do_ref[...] = do.astype(BF16)
        _acc_rows(dgp_ref, first, dgp)

    row = lambda c: pl.BlockSpec((t, c), lambda i: (i, 0))
    vec = pl.BlockSpec((1, d), lambda i: (0, 0))
    return pl.pallas_call(
        body, name=name, grid=(s // t,),
        in_specs=[row(k), pl.BlockSpec((k, d), lambda i: (0, 0)), row(d), vec, row(d)],
        out_specs=[row(d), row(d), pl.BlockSpec((8, LANES), lambda i: (0, 0)), vec],
        out_shape=[jax.ShapeDtypeStruct((s, d), BF16), jax.ShapeDtypeStruct((s, d), F32),
                   jax.ShapeDtypeStruct((8, LANES), F32), jax.ShapeDtypeStruct((1, d), F32)],
        compiler_params=_cp("arbitrary"))(y, w, x1, g_post, target)


def _layer_norm(d1, cg, cb):
    mu = jnp.mean(d1, axis=-1, keepdims=True)
    cen = d1 - mu
    rstd = lax.rsqrt(jnp.mean(cen * cen, axis=-1, keepdims=True) + EPS)
    n = cen * rstd
    return n, rstd, n * cg + cb


def _odd_mix_fwd(p, sconv_w, dconv_w, dconv_b, cnorm_g, cnorm_b, d, name):
    s = p.shape[0]
    w = d // 2
    k3, k31 = sconv_w.shape[0], dconv_w.shape[0]
    t, hb = ROW_TILE, CONV_HALO
    assert hb >= k31 - 1 and w % LANES == 0

    def body(p_ref, ph_ref, w3_ref, w31_ref, b31_ref, cg_ref, cb_ref, y_ref, s3_ref, d1_ref, mpad, dpad):
        i = pl.program_id(0)
        mpad[0:hb, :] = jnp.where(i > 0, ph_ref[:, 2 * w:3 * w] * ph_ref[:, 0:w], 0.0)
        mpad[hb:, :] = p_ref[:, 2 * w:3 * w] * p_ref[:, 0:w]
        dpad[0:hb, :] = jnp.where(i > 0, ph_ref[:, 3 * w:4 * w] * _sigmoid(ph_ref[:, 4 * w:5 * w]), 0.0)
        dpad[hb:, :] = p_ref[:, 3 * w:4 * w] * _sigmoid(p_ref[:, 4 * w:5 * w])
        for c0 in range(0, w, LANES):
            cs = slice(c0, c0 + LANES)
            acc = jnp.zeros((t, LANES), F32)
            for kk in range(k3):
                acc = acc + w3_ref[kk:kk + 1, cs] * mpad[hb - (k3 - 1) + kk:hb - (k3 - 1) + kk + t, cs]
            s3_ref[:, cs] = acc
            acc = jnp.zeros((t, LANES), F32)
            for kk in range(k31):
                acc = acc + w31_ref[kk:kk + 1, cs] * dpad[hb - (k31 - 1) + kk:hb - (k31 - 1) + kk + t, cs]
            d1_ref[:, cs] = acc + b31_ref[:, cs]
        _, _, d2 = _layer_norm(d1_ref[...], cg_ref[...], cb_ref[...])
        y_ref[:, :w] = (p_ref[:, w:2 * w] * s3_ref[...] * _silu(p_ref[:, 5 * w:6 * w])).astype(BF16)
        y_ref[:, w:] = (_silu(d2) * _silu(p_ref[:, 6 * w:7 * w])).astype(BF16)

    row = lambda c: pl.BlockSpec((t, c), lambda i: (i, 0))
    full = lambda a: pl.BlockSpec(a.shape, lambda i: (0, 0))
    return pl.pallas_call(
        body, name=name, grid=(s // t,),
        in_specs=[row(7 * w),
                  pl.BlockSpec((hb, 5 * w), lambda i: (jnp.maximum(i * (t // hb) - 1, 0), 0)),
                  full(sconv_w), full(dconv_w), full(dconv_b), full(cnorm_g), full(cnorm_b)],
        out_specs=[row(d), row(w), row(w)],
        out_shape=[jax.ShapeDtypeStruct((s, d), BF16), jax.ShapeDtypeStruct((s, w), F32),
                   jax.ShapeDtypeStruct((s, w), F32)],
        scratch_shapes=[pltpu.VMEM((hb + t, w), F32)] * 2,
        compiler_params=_cp("parallel"))(p, p, sconv_w, dconv_w, dconv_b, cnorm_g, cnorm_b)


def _odd_bwd_rows(p, s3, d1, dy, cnorm_g, cnorm_b, d, name):
    s = p.shape[0]
    w = d // 2
    t = ROW_TILE

    def body(bc_ref, g1_ref, g2_ref, s3_ref, d1_ref, dy_ref, cg_ref, cb_ref,
             dbc_ref, dg_ref, ds3_ref, dd1_ref, dcg_ref, dcb_ref, db_ref):
        first = pl.program_id(0) == 0
        g1, g2 = g1_ref[...], g2_ref[...]
        bc, s3v = bc_ref[...], s3_ref[...]
        dy1, dy2 = dy_ref[:, :w], dy_ref[:, w:]
        n, rstd, d2 = _layer_norm(d1_ref[...], cg_ref[...], cb_ref[...])
        dg_ref[:, :w] = (dy1 * bc * s3v * _dsilu(g1)).astype(BF16)
        dg_ref[:, w:] = (dy2 * _silu(d2) * _dsilu(g2)).astype(BF16)
        dco = dy1 * _silu(g1)
        dbc_ref[...] = (dco * s3v).astype(BF16)
        ds3_ref[...] = dco * bc
        dd2 = dy2 * _silu(g2) * _dsilu(d2)
        _acc_rows(dcb_ref, first, jnp.sum(dd2, axis=0, keepdims=True))
        _acc_rows(dcg_ref, first, jnp.sum(dd2 * n, axis=0, keepdims=True))
        dn = dd2 * cg_ref[...]
        dd1 = rstd * (dn - jnp.mean(dn, axis=-1, keepdims=True) - n * jnp.mean(dn * n, axis=-1, keepdims=True))
        dd1_ref[...] = dd1
        _acc_rows(db_ref, first, jnp.sum(dd1, axis=0, keepdims=True))

    col = lambda j: pl.BlockSpec((t, w), lambda i: (i, j))
    row = lambda c: pl.BlockSpec((t, c), lambda i: (i, 0))
    vec = pl.BlockSpec((1, w), lambda i: (0, 0))
    return pl.pallas_call(
        body, name=name, grid=(s // t,),
        in_specs=[col(1), col(5), col(6), row(w), row(w), row(d), vec, vec],
        out_specs=[row(w), row(d), row(w), row(w), vec, vec, vec],
        out_shape=[jax.ShapeDtypeStruct((s, w), BF16), jax.ShapeDtypeStruct((s, d), BF16),
                   jax.ShapeDtypeStruct((s, w), F32), jax.ShapeDtypeStruct((s, w), F32)]
        + [jax.ShapeDtypeStruct((1, w), F32)] * 3,
        compiler_params=_cp("arbitrary"))(p, p, p, s3, d1, dy, cnorm_g, cnorm_b)


def _odd_bwd_conv(p, ds3, dd1, sconv_w, dconv_w, d, name):
    s = p.shape[0]
    w = d // 2
    k3, k31 = sconv_w.shape[0], dconv_w.shape[0]
    t, hb, ha = ROW_TILE, CONV_HALO, 8
    nt = s // t
    assert hb >= k31 - 1 and ha >= k3 - 1 and k3 <= 8 and k31 <= 32

    def body(hc_ref, cc_ref, ga_ref, gb_ref, hch_ref, cch_ref, gah_ref, gbh_ref, ds3_ref, ds3h_ref, dd1_ref, dd1h_ref,
             w3_ref, w31_ref, dhc_ref, dcc_ref, dga_ref, dgb_ref, dw3_ref, dw31_ref, mpad, dpad, s3pad, d1pad):
        i = pl.program_id(0)
        first = i == 0
        last = i == nt - 1
        mpad[0:hb, :] = jnp.where(i > 0, cch_ref[...] * hch_ref[...], 0.0)
        mpad[hb:, :] = cc_ref[...] * hc_ref[...]
        dpad[0:hb, :] = jnp.where(i > 0, gah_ref[...] * _sigmoid(gbh_ref[...]), 0.0)
        dpad[hb:, :] = ga_ref[...] * _sigmoid(gb_ref[...])
        s3pad[0:t, :] = ds3_ref[...]
        s3pad[t:, :] = jnp.where(last, 0.0, ds3h_ref[...])
        d1pad[0:t, :] = dd1_ref[...]
        d1pad[t:, :] = jnp.where(last, 0.0, dd1h_ref[...])

        @pl.when(first)
        def _():
            dw3_ref[...] = jnp.zeros_like(dw3_ref)
            dw31_ref[...] = jnp.zeros_like(dw31_ref)

        for c0 in range(0, w, LANES):
            cs = slice(c0, c0 + LANES)
            ds3v = s3pad[0:t, cs]
            dd1v = d1pad[0:t, cs]
            dm = jnp.zeros((t, LANES), F32)
            for kk in range(k3):
                dm = dm + w3_ref[kk:kk + 1, cs] * s3pad[k3 - 1 - kk:k3 - 1 - kk + t, cs]
                off = hb - (k3 - 1) + kk
                dw3_ref[kk:kk + 1, cs] += jnp.sum(ds3v * mpad[off:off + t, cs], axis=0, keepdims=True)
            dd0 = jnp.zeros((t, LANES), F32)
            for kk in range(k31):
                dd0 = dd0 + w31_ref[kk:kk + 1, cs] * d1pad[k31 - 1 - kk:k31 - 1 - kk + t, cs]
                off = hb - (k31 - 1) + kk
                dw31_ref[kk:kk + 1, cs] += jnp.sum(dd1v * dpad[off:off + t, cs], axis=0, keepdims=True)
            dcc_ref[:, cs] = (dm * hc_ref[:, cs]).astype(BF16)
            dhc_ref[:, cs] = (dm * cc_ref[:, cs]).astype(BF16)
            sgb = _sigmoid(gb_ref[:, cs])
            dga_ref[:, cs] = (dd0 * sgb).astype(BF16)
            dgb_ref[:, cs] = (dd0 * ga_ref[:, cs] * sgb * (1.0 - sgb)).astype(BF16)

    col = lambda j: pl.BlockSpec((t, w), lambda i: (i, j))
    pre = lambda j: pl.BlockSpec((hb, w), lambda i: (jnp.maximum(i * (t // hb) - 1, 0), j))
    row = pl.BlockSpec((t, w), lambda i: (i, 0))
    post = lambda h: pl.BlockSpec((h, w), lambda i: (jnp.minimum((i + 1) * (t // h), s // h - 1), 0))
    full = lambda a: pl.BlockSpec(a.shape, lambda i: (0, 0))
    return pl.pallas_call(
        body, name=name, grid=(nt,),
        in_specs=[col(0), col(2), col(3), col(4), pre(0), pre(2), pre(3), pre(4),
                  row, post(ha), row, post(hb), full(sconv_w), full(dconv_w)],
        out_specs=[row, row, row, row, pl.BlockSpec((8, w), lambda i: (0, 0)), pl.BlockSpec((32, w), lambda i: (0, 0))],
        out_shape=[jax.ShapeDtypeStruct((s, w), BF16)] * 4
        + [jax.ShapeDtypeStruct((8, w), F32), jax.ShapeDtypeStruct((32, w), F32)],
        scratch_shapes=[pltpu.VMEM((hb + t, w), F32)] * 2 + [pltpu.VMEM((t + ha, w), F32), pltpu.VMEM((t + hb, w), F32)],
        compiler_params=_cp("arbitrary"))(p, p, p, p, p, p, p, p, ds3, ds3, dd1, dd1, sconv_w, dconv_w)


def _mm_in_bwd(dp, w3, x, g_pre, dres, post, name):
    s = dp.shape[0]
    nsh, d, ns = w3.shape
    t = ROW_TILE

    def body(*refs):
        if post is None:
            dp_ref, w_ref, x_ref, g_ref, dr_ref, dx_ref, dg_ref, acc_ref = refs
        else:
            dp_ref, w_ref, x_ref, g_ref, dr_ref, o_ref, gp_ref, dx_ref, dg_ref, do_ref, dgp_ref, acc_ref = refs
        kk = pl.program_id(1)
        first = pl.program_id(0) == 0
        part = _nt(dp_ref[...], w_ref[...])

        @pl.when(kk == 0)
        def _():
            acc_ref[...] = part

        @pl.when(kk > 0)
        def _():
            acc_ref[...] += part

        @pl.when(kk == nsh - 1)
        def _():
            xhat, r = _rms_stats(x_ref[...])
            dxn, dg = _rms_bwd(acc_ref[...], xhat, r, g_ref[...])
            dx = dr_ref[...] + dxn
            dx_ref[...] = dx
            _acc_rows(dg_ref, first, dg)
            if post is not None:
                ohat, ro = _rms_stats(o_ref[...])
                do, dgp = _rms_bwd(dx, ohat, ro, gp_ref[...])
                do_ref[...] = do.astype(BF16)
                _acc_rows(dgp_ref, first, dgp)

    row = pl.BlockSpec((t, d), lambda i, k: (i, 0))
    vec = pl.BlockSpec((1, d), lambda i, k: (0, 0))
    in_specs = [pl.BlockSpec((t, ns), lambda i, k: (i, k)), pl.BlockSpec((None, d, ns), lambda i, k: (k, 0, 0)), row, vec, row]
    out_specs = [row, vec]
    out_shape = [jax.ShapeDtypeStruct((s, d), F32), jax.ShapeDtypeStruct((1, d), F32)]
    args = [dp, w3, x, g_pre, dres]
    if post is not None:
        in_specs += [row, vec]
        out_specs += [row, vec]
        out_shape += [jax.ShapeDtypeStruct((s, d), BF16), jax.ShapeDtypeStruct((1, d), F32)]
        args += list(post)
    return pl.pallas_call(
        body, name=name, grid=(s // t, nsh), in_specs=in_specs, out_specs=out_specs, out_shape=out_shape,
        scratch_shapes=[pltpu.VMEM((t, d), F32)],
        compiler_params=_cp("arbitrary", "arbitrary"))(*args)


def _half_add(g, r1, c_arr, name):
    nsh, rows, ns = g.shape
    h = rows // 2
    tr = min(ROW_TILE, h)
    per = h // tr

    def body(c_ref, g_ref, r_ref, o_ref):
        o_ref[...] = (g_ref[...].astype(F32) + r_ref[...].astype(F32)).astype(BF16)

    spec = pl.BlockSpec((None, tr, ns), lambda s, r, c: (s, r, 0))
    return pl.pallas_call(
        body, name=name,
        grid_spec=pltpu.PrefetchScalarGridSpec(
            num_scalar_prefetch=1, grid=(nsh, per),
            in_specs=[pl.BlockSpec((None, tr, ns), lambda s, r, c: (s, c[0] * per + r, 0)), spec], out_specs=spec),
        out_shape=jax.ShapeDtypeStruct((nsh, h, ns), BF16), compiler_params=_cp("parallel", "parallel"))(c_arr, g, r1)


def _sum_chips(hh, r2, me_arr, name):
    _, h, ns = hh.shape
    tr = min(ROW_TILE, h)

    def body(me_ref, h_ref, a_ref, b_ref, c_ref, o_ref):
        o_ref[...] = ((h_ref[...].astype(F32) + a_ref[...].astype(F32)) + b_ref[...].astype(F32)) + c_ref[...].astype(F32)

    got = lambda k: pl.BlockSpec((None, tr, ns), lambda r, me: (k, r, 0))
    return pl.pallas_call(
        body, name=name,
        grid_spec=pltpu.PrefetchScalarGridSpec(
            num_scalar_prefetch=1, grid=(h // tr,),
            in_specs=[pl.BlockSpec((None, tr, ns), lambda r, me: (me[0], r, 0)), got(0), got(1), got(2)],
            out_specs=pl.BlockSpec((tr, ns), lambda r, me: (r, 0))),
        out_shape=jax.ShapeDtypeStruct((h, ns), F32), compiler_params=_cp("parallel"))(me_arr, hh, r2, r2, r2)


def _add2(a, b, name):
    def body(a_ref, b_ref, o_ref):
        o_ref[...] = a_ref[...] + b_ref[...]

    return pl.pallas_call(body, name=name, out_shape=jax.ShapeDtypeStruct(a.shape, a.dtype), compiler_params=_cp())(a, b)


def _sum_chips_ordered(s2, r2, mc_arr, name):
    rows, w = s2.shape
    rh = rows // 2

    def body(mc_ref, s_ref, a_ref, b_ref, c_ref, o_ref):
        me = mc_ref[0]
        acc = None
        for j in range(N_CHIPS):
            rel = jnp.bitwise_xor(me, j)
            v = jnp.where(rel == 0, s_ref[...], jnp.where(rel == 2, a_ref[...], jnp.where(rel == 1, b_ref[...], c_ref[...])))
            acc = v if acc is None else acc + v
        o_ref[...] = acc

    got = lambda k: pl.BlockSpec((None, rh, w), lambda i, mc: (k, 0, 0))
    return pl.pallas_call(
        body, name=name,
        grid_spec=pltpu.PrefetchScalarGridSpec(
            num_scalar_prefetch=1, grid=(1,),
            in_specs=[pl.BlockSpec((rh, w), lambda i, mc: (mc[1], 0)), got(0), got(1), got(2)],
            out_specs=pl.BlockSpec((rh, w), lambda i, mc: (0, 0))),
        out_shape=jax.ShapeDtypeStruct((rh, w), F32), compiler_params=_cp("arbitrary"))(mc_arr, s2, r2, r2, r2)


def _adamw(w, g, m, v, name):
    r, c = w.shape
    tr = ROW_TILE if r % ROW_TILE == 0 else r
    c1 = 1.0 / (1.0 - ADAM_B1 ** ADAM_STEP)
    c2 = 1.0 / (1.0 - ADAM_B2 ** ADAM_STEP)

    def body(w_ref, g_ref, m_ref, v_ref, d_ref, nm_ref, nv_ref):
        gv = g_ref[...]
        nm = ADAM_B1 * m_ref[...] + (1.0 - ADAM_B1) * gv
        nv = ADAM_B2 * v_ref[...] + (1.0 - ADAM_B2) * (gv * gv)
        nm_ref[...] = nm
        nv_ref[...] = nv
        d_ref[...] = -ADAM_LR * ((nm * c1) / (jnp.sqrt(nv * c2) + ADAM_EPS) + ADAM_WD * w_ref[...])

    spec = pl.BlockSpec((tr, c), lambda i: (i, 0))
    return pl.pallas_call(
        body, name=name, grid=(r // tr,), in_specs=[spec] * 4, out_specs=[spec] * 3,
        out_shape=[jax.ShapeDtypeStruct((r, c), F32)] * 3, compiler_params=_cp("parallel"))(w, g, m, v)


def _rcopy(src, dst, ssem, rsem, dev):
    return pltpu.make_async_remote_copy(src_ref=src, dst_ref=dst, send_sem=ssem, recv_sem=rsem,
                                        device_id=dev, device_id_type=MESH_ID)


def _place():
    x, y, c = lax.axis_index("x"), lax.axis_index("y"), lax.axis_index("c")
    chips = [(1 - x, y), (x, 1 - y), (1 - x, 1 - y)]
    return x, y, c, 2 * x + y, chips, (x, y, 1 - c)


def _gather_weights(bigs, pool_w, pack_w, pack_d, name):
    nb = len(bigs)
    smalls = [pool_w, pack_w, pack_d]
    q, cw, cd = pool_w.shape[1], pack_w.shape[1], pack_d.shape[1]
    halves = [b.shape[0] // 2 for b in bigs]

    def body(*refs):
        srcs, dsts = refs[:nb + 3], refs[nb + 3:2 * (nb + 3)]
        ssem, rsem, lsem = refs[2 * (nb + 3):]
        x, y, c, me, chips, sib = _place()

        def big_dst(a, chip, half):
            return dsts[a].at[chip, pl.ds(half * halves[a], halves[a])]

        def small_dst(n, chip):
            if n == 0:
                return dsts[nb].at[:, pl.ds(chip * q, q), :]
            return dsts[nb + n].at[:, pl.ds(chip * (cw if n == 1 else cd), cw if n == 1 else cd)]

        local = [pltpu.make_async_copy(srcs[a], dsts[a].at[me], lsem.at[a]) for a in range(nb)]
        local += [pltpu.make_async_copy(srcs[nb + n], small_dst(n, me), lsem.at[nb + n]) for n in range(3)]
        for cp in local:
            cp.start()
        sends = []
        for a in range(nb):
            for k, chip in enumerate(chips):
                cp = _rcopy(srcs[a].at[pl.ds(c * halves[a], halves[a])], big_dst(a, me, c),
                            ssem.at[6 * a + k], rsem.at[6 * a + k], (*chip, c))
                cp.start()
                sends.append(cp)
        for n in range(3):
            for k, chip in enumerate(chips):
                cp = _rcopy(srcs[nb + n], small_dst(n, me), ssem.at[6 * nb + 3 * n + k], rsem.at[6 * nb + 3 * n + k], (*chip, c))
                cp.start()
                sends.append(cp)
        for a in range(nb):
            for k, chip in enumerate(chips):
                ref = big_dst(a, 2 * chip[0] + chip[1], c)
                _rcopy(ref, ref, ssem.at[6 * a + k], rsem.at[6 * a + k], (*chip, c)).wait_recv()
                cp = _rcopy(ref, ref, ssem.at[6 * a + 3 + k], rsem.at[6 * a + 3 + k], sib)
                cp.start()
                sends.append(cp)
        for a in range(nb):
            for k, chip in enumerate(chips):
                ref = big_dst(a, 2 * chip[0] + chip[1], 1 - c)
                _rcopy(ref, ref, ssem.at[6 * a + 3 + k], rsem.at[6 * a + 3 + k], sib).wait_recv()
        for n in range(3):
            for k, chip in enumerate(chips):
                ref = small_dst(n, 2 * chip[0] + chip[1])
                _rcopy(ref, ref, ssem.at[6 * nb + 3 * n + k], rsem.at[6 * nb + 3 * n + k], (*chip, c)).wait_recv()
        for cp in sends:
            cp.wait_send()
        for cp in local:
            cp.wait()

    nsem = 6 * nb + 9
    out_shape = [jax.ShapeDtypeStruct((N_CHIPS,) + b.shape, b.dtype) for b in bigs]
    out_shape += [jax.ShapeDtypeStruct((pool_w.shape[0], N_CHIPS * q, pool_w.shape[2]), pool_w.dtype),
                  jax.ShapeDtypeStruct((pack_w.shape[0], N_CHIPS * cw), pack_w.dtype),
                  jax.ShapeDtypeStruct((pack_d.shape[0], N_CHIPS * cd), pack_d.dtype)]
    return pl.pallas_call(
        body, name=name, in_specs=[ANY] * (nb + 3), out_specs=[ANY] * (nb + 3), out_shape=out_shape,
        scratch_shapes=[pltpu.SemaphoreType.DMA((nsem,)), pltpu.SemaphoreType.DMA((nsem,)), pltpu.SemaphoreType.DMA((nb + 3,))],
        compiler_params=pltpu.CompilerParams(has_side_effects=True))(*bigs, *smalls)


def _swap_with_sibling(grads, small, name):
    n = len(grads)
    halves = [g.shape[1] // 2 for g in grads]

    def body(*refs):
        srcs, dsts = refs[:n + 1], refs[n + 1:2 * (n + 1)]
        ssem, rsem = refs[2 * (n + 1):]
        x, y, c, me, chips, sib = _place()
        cps = [_rcopy(srcs[a].at[:, pl.ds((1 - c) * halves[a], halves[a]), :], dsts[a], ssem.at[a], rsem.at[a], sib)
               for a in range(n)]
        cps.append(_rcopy(srcs[n], dsts[n], ssem.at[n], rsem.at[n], sib))
        for cp in cps:
            cp.start()
        for cp in cps:
            cp.wait_recv()
        for cp in cps:
            cp.wait_send()

    out_shape = [jax.ShapeDtypeStruct((g.shape[0], h, g.shape[2]), g.dtype) for g, h in zip(grads, halves)]
    out_shape.append(jax.ShapeDtypeStruct(small.shape, small.dtype))
    return pl.pallas_call(
        body, name=name, in_specs=[ANY] * (n + 1), out_specs=[ANY] * (n + 1), out_shape=out_shape,
        scratch_shapes=[pltpu.SemaphoreType.DMA((n + 1,)), pltpu.SemaphoreType.DMA((n + 1,))],
        compiler_params=pltpu.CompilerParams(has_side_effects=True))(*grads, small)


def _scatter_to_chips(halves_in, small, name):
    n = len(halves_in)
    rh = small.shape[0] // 2

    def body(*refs):
        srcs, dsts = refs[:n + 1], refs[n + 1:2 * (n + 1)]
        ssem, rsem = refs[2 * (n + 1):]
        x, y, c, me, chips, sib = _place()
        cps = []
        for a in range(n + 1):
            for k, chip in enumerate(chips):
                src = srcs[a].at[2 * chip[0] + chip[1]] if a < n else srcs[a].at[pl.ds(c * rh, rh)]
                cps.append(_rcopy(src, dsts[a].at[k], ssem.at[3 * a + k], rsem.at[3 * a + k], (*chip, c)))
        for cp in cps:
            cp.start()
        for cp in cps:
            cp.wait_recv()
        for cp in cps:
            cp.wait_send()

    out_shape = [jax.ShapeDtypeStruct((3,) + h.shape[1:], h.dtype) for h in halves_in]
    out_shape.append(jax.ShapeDtypeStruct((3, rh, small.shape[1]), small.dtype))
    return pl.pallas_call(
        body, name=name, in_specs=[ANY] * (n + 1), out_specs=[ANY] * (n + 1), out_shape=out_shape,
        scratch_shapes=[pltpu.SemaphoreType.DMA((3 * (n + 1),)), pltpu.SemaphoreType.DMA((3 * (n + 1),))],
        compiler_params=pltpu.CompilerParams(has_side_effects=True))(*halves_in, small)


def _join_halves(parts, name):
    n = len(parts)

    def body(*refs):
        srcs, dsts = refs[:n], refs[n:2 * n]
        ssem, rsem, lsem = refs[2 * n:]
        x, y, c, me, chips, sib = _place()
        local, cps = [], []
        for a in range(n):
            h = srcs[a].shape[0]
            mine = dsts[a].at[pl.ds(c * h, h)]
            local.append(pltpu.make_async_copy(srcs[a], mine, lsem.at[a]))
            cps.append(_rcopy(srcs[a], mine, ssem.at[a], rsem.at[a], sib))
        for cp in local + cps:
            cp.start()
        for a in range(n):
            h = srcs[a].shape[0]
            theirs = dsts[a].at[pl.ds((1 - c) * h, h)]
            _rcopy(theirs, theirs, ssem.at[a], rsem.at[a], sib).wait_recv()
        for cp in cps:
            cp.wait_send()
        for cp in local:
            cp.wait()

    out_shape = [jax.ShapeDtypeStruct((2 * p.shape[0], p.shape[1]), p.dtype) for p in parts]
    return pl.pallas_call(
        body, name=name, in_specs=[ANY] * n, out_specs=[ANY] * n, out_shape=out_shape,
        scratch_shapes=[pltpu.SemaphoreType.DMA((n,)), pltpu.SemaphoreType.DMA((n,)), pltpu.SemaphoreType.DMA((n,))],
        compiler_params=pltpu.CompilerParams(has_side_effects=True))(*parts)


def _pad_rows(a, rows):
    return jnp.pad(a, ((0, rows - a.shape[0]), (0, 0)))


def _stack_rows(parts, multiple):
    padded = [_pad_rows(p, -(-p.shape[0] // 8) * 8) for p in parts]
    starts, at = [], 0
    for p in padded:
        starts.append(at)
        at += p.shape[0]
    total = -(-at // multiple) * multiple
    if total > at:
        padded.append(jnp.zeros((total - at, parts[0].shape[1]), parts[0].dtype))
    return jnp.concatenate(padded, axis=0), starts


def _forward_backward(x2d, tgt, ln_pre_even, win_e, pool_w_f, pool_scale, wout_e, ln_post_even, ln_pre_odd, win_o,
                      sconv, dconv, dconv_b, cnorm_g, cnorm_b, wout_o, ln_post_odd):
    d = x2d.shape[1]
    half = d // 2
    gd = pool_w_f.shape[2]
    h0 = _rms_fwd(x2d, ln_pre_even, "rms_pre_even")
    p_e = _mm_nn(h0, win_e, "proj_in_even")
    att, ltot = _sba_fwd(p_e, half, "sba_fwd")
    y_e = _even_mix_fwd(p_e, att, pool_w_f, pool_scale, d, "even_mix_fwd")
    o_e, x1, h1 = _mm_out_even(y_e, wout_e, x2d, ln_post_even, ln_pre_odd, "proj_out_even")
    p_o = _mm_nn(h1, win_o, "proj_in_odd")
    y_o, s3, d1 = _odd_mix_fwd(p_o, sconv, dconv, dconv_b, cnorm_g, cnorm_b, d, "odd_mix_fwd")
    do_o, dx2, loss_blk, dln_post_odd = _mm_out_odd(y_o, wout_o, x1, ln_post_odd, tgt, "proj_out_odd_loss")

    dy_o = _mm_nt(do_o, wout_o, "dy_odd")
    g_wout_o = _mm_tn(y_o, do_o, 1, "dw_out_odd").reshape(N_CHIPS, d // N_CHIPS, d)
    dbc, dgate_o, ds3, dd1, dcnorm_g, dcnorm_b, ddconv_b = _odd_bwd_rows(p_o, s3, d1, dy_o, cnorm_g, cnorm_b, d, "odd_bwd_rows")
    dhc, dcc, dga, dgb, dsconv, ddconv = _odd_bwd_conv(p_o, ds3, dd1, sconv, dconv, d, "odd_bwd_conv")
    dp_o = jnp.concatenate([dhc, dbc, dcc, dga, dgb, dgate_o], axis=1)
    g_win_o = _mm_tn(h1, dp_o, N_CHIPS, "dw_in_odd")
    dx1, dln_pre_odd, do_e, dln_post_even = _mm_in_bwd(dp_o, win_o, x1, ln_pre_odd, dx2, (o_e, ln_post_even), "dx_odd")

    dy_e = _mm_nt(do_e, wout_e, "dy_even")
    g_wout_e = _mm_tn(y_e, do_e, 1, "dw_out_even").reshape(N_CHIPS, d // N_CHIPS, d)
    datt, du, dgate_e, dpool_scale, dpool_w = _even_mix_bwd(p_e, att, dy_e, pool_w_f, pool_scale, d, "even_mix_bwd")
    dq, dk, dv = _sba_bwd(p_e, att, ltot, datt, half, "sba_bwd")
    dp_e = jnp.concatenate([dq, dk, dv, du, dgate_e], axis=1)
    g_win_e = _mm_tn(h0, dp_e, N_CHIPS, "dw_in_even")
    grad_x, dln_pre_even = _mm_in_bwd(dp_e, win_e, x2d, ln_pre_even, dx1, None, "dx_even")

    two = lambda v: v.reshape(2, half)
    small_parts = [two(dln_pre_even), dpool_scale, two(dln_post_even), two(dln_pre_odd), two(dln_post_odd),
                   dsconv, ddconv, ddconv_b, dcnorm_g, dcnorm_b, dpool_w.reshape(gd, half)]
    return loss_blk, grad_x, [g_win_e, g_wout_e, g_win_o, g_wout_o], small_parts


def kernel(x, ln_pre_even, w_in_even, pool_w, pool_scale, w_out_even, ln_post_even, ln_pre_odd, w_in_odd, sconv_w, dconv_w, dconv_b, cnorm_g, cnorm_b, w_out_odd, ln_post_odd, loss_target, m_ln_pre_even, m_w_in_even, m_pool_w, m_pool_scale, m_w_out_even, m_ln_post_even, m_ln_pre_odd, m_w_in_odd, m_sconv_w, m_dconv_w, m_dconv_b, m_cnorm_g, m_cnorm_b, m_w_out_odd, m_ln_post_odd, v_ln_pre_even, v_w_in_even, v_pool_w, v_pool_scale, v_w_out_even, v_ln_post_even, v_ln_pre_odd, v_w_in_odd, v_sconv_w, v_dconv_w, v_dconv_b, v_cnorm_g, v_cnorm_b, v_w_out_odd, v_ln_post_odd):
    _, s, d = x.shape
    half = d // 2
    cw = half // N_CHIPS
    ng, q, gd = pool_w.shape[1:]
    k3, k31 = sconv_w.shape[1], dconv_w.shape[1]
    x2d, tgt = x[0], loss_target[0]
    me = 2 * lax.axis_index("x") + lax.axis_index("y")
    core = lax.axis_index("c")
    c_arr = jnp.reshape(core, (1,)).astype(jnp.int32)
    me_arr = jnp.reshape(me, (1,)).astype(jnp.int32)
    mc_arr = jnp.stack([me, core]).astype(jnp.int32)

    shards = [w_in_even[0], w_out_even[0], w_in_odd[0], w_out_odd[0]]
    bigs = [_cast_bf16(w, f"cast_w{n}") for n, w in enumerate(shards)]
    pool_w_b = _cast_bf16(pool_w[0].reshape(ng * q, gd), "cast_pool_w").reshape(ng, q, gd)
    pack_w, at_w = _stack_rows([sconv_w[0], dconv_w[0], dconv_b, cnorm_g, cnorm_b], 8)
    pack_d, at_d = _stack_rows([ln_pre_odd, ln_post_odd], 8)
    win_e, wout_e, win_o, wout_o, pool_w_f, pack_w_f, pack_d_f = _gather_weights(bigs, pool_w_b, pack_w, pack_d, "gather_weights")
    wout_e = wout_e.reshape(d, d)
    wout_o = wout_o.reshape(d, d)
    sconv_f = pack_w_f[at_w[0]:at_w[0] + k3]
    dconv_f = pack_w_f[at_w[1]:at_w[1] + k31]
    dconv_b_f, cnorm_g_f, cnorm_b_f = (pack_w_f[at_w[n]:at_w[n] + 1] for n in (2, 3, 4))
    ln_pre_odd_f = pack_d_f[at_d[0]:at_d[0] + 1]
    ln_post_odd_f = pack_d_f[at_d[1]:at_d[1] + 1]

    loss_blk, grad_x, grads, small_parts = _forward_backward(
        x2d, tgt, ln_pre_even, win_e, pool_w_f, pool_scale, wout_e, ln_post_even, ln_pre_odd_f, win_o, sconv_f, dconv_f,
        dconv_b_f, cnorm_g_f, cnorm_b_f, wout_o, ln_post_odd_f)
    loss = lax.psum(loss_blk[0, 0], ("x", "y", "c"))

    small, at_s = _stack_rows(small_parts, 16)
    *got1, small1 = _swap_with_sibling(grads, small, "swap_with_sibling")
    halves = [_half_add(g, r, c_arr, f"half_add{n}") for n, (g, r) in enumerate(zip(grads, got1))]
    small2 = _add2(small, small1, "small_add")
    *got2, small_got = _scatter_to_chips(halves, small2, "scatter_to_chips")
    parts = [_sum_chips(h, r, me_arr, f"sum_chips{n}") for n, (h, r) in enumerate(zip(halves, got2))]
    parts.append(_sum_chips_ordered(small2, small_got, mc_arr, "small_sum"))
    gw_in_e, gw_out_e, gw_in_o, gw_out_o, red = _join_halves(parts, "join_halves")

    def rows(n, cnt):
        return red[at_s[n]:at_s[n] + cnt]

    def mine(a, width):
        return lax.dynamic_slice_in_dim(a, me * width, width, axis=1)

    quarter = d // N_CHIPS
    g_small = {
        "ln_pre_even": rows(0, 2).reshape(1, d),
        "pool_scale": rows(1, 1),
        "ln_post_even": rows(2, 2).reshape(1, d),
        "ln_pre_odd": mine(rows(3, 2).reshape(1, d), quarter),
        "ln_post_odd": mine(rows(4, 2).reshape(1, d), quarter),
        "sconv_w": mine(rows(5, k3), cw),
        "dconv_w": mine(rows(6, k31), cw),
        "dconv_b": mine(rows(7, 1), cw),
        "cnorm_g": mine(rows(8, 1), cw),
        "cnorm_b": mine(rows(9, 1), cw),
        "pool_w": lax.dynamic_slice_in_dim(rows(10, gd).reshape(ng, gd, gd), me * q, q, axis=1).reshape(ng * q, gd),
    }
    w2d = {
        "ln_pre_even": ln_pre_even, "w_in_even": w_in_even[0], "pool_w": pool_w[0].reshape(ng * q, gd),
        "pool_scale": pool_scale, "w_out_even": w_out_even[0], "ln_post_even": ln_post_even, "ln_pre_odd": ln_pre_odd,
        "w_in_odd": w_in_odd[0], "sconv_w": sconv_w[0], "dconv_w": dconv_w[0], "dconv_b": dconv_b, "cnorm_g": cnorm_g,
        "cnorm_b": cnorm_b, "w_out_odd": w_out_odd[0], "ln_post_odd": ln_post_odd,
    }
    moments = {
        "ln_pre_even": (m_ln_pre_even, v_ln_pre_even), "w_in_even": (m_w_in_even, v_w_in_even),
        "pool_w": (m_pool_w, v_pool_w), "pool_scale": (m_pool_scale, v_pool_scale),
        "w_out_even": (m_w_out_even, v_w_out_even), "ln_post_even": (m_ln_post_even, v_ln_post_even),
        "ln_pre_odd": (m_ln_pre_odd, v_ln_pre_odd), "w_in_odd": (m_w_in_odd, v_w_in_odd),
        "sconv_w": (m_sconv_w, v_sconv_w), "dconv_w": (m_dconv_w, v_dconv_w), "dconv_b": (m_dconv_b, v_dconv_b),
        "cnorm_g": (m_cnorm_g, v_cnorm_g), "cnorm_b": (m_cnorm_b, v_cnorm_b),
        "w_out_odd": (m_w_out_odd, v_w_out_odd), "ln_post_odd": (m_ln_post_odd, v_ln_post_odd),
    }
    g2d = dict(g_small, w_in_even=gw_in_e, w_out_even=gw_out_e, w_in_odd=gw_in_o, w_out_odd=gw_out_o)
    grads_out, deltas, new_m, new_v = [], [], [], []
    for name, w in w2d.items():
        m_in, v_in = moments[name]
        shape = m_in.shape
        delta, nm, nv = _adamw(w, g2d[name], m_in.reshape(w.shape), v_in.reshape(w.shape), "adamw_" + name)
        grads_out.append(g2d[name].reshape(shape))
        deltas.append(delta.reshape(shape))
        new_m.append(nm.reshape(shape))
        new_v.append(nv.reshape(shape))
    return (loss, grad_x.reshape(x.shape), *grads_out, *deltas, *new_m, *new_v)
```

```python
import functools
import math

import jax
import jax.numpy as jnp
from jax import lax
from jax.experimental import pallas as pl
from jax.experimental.pallas import tpu as pltpu

F32 = jnp.float32
BF16 = jnp.bfloat16
EPS = 1e-6
N_CHIPS = 4
VMEM_LIMIT_V7X = 56 << 20
HEAD_DIM = 128
ATT_BLOCK = 256
POOL_WINDOWS = (2, 4, 8, 16)
ROW_TILE = 256
POOL_HALO = 16
CONV_HALO = 32
LANES = 128
ADAM_LR, ADAM_B1, ADAM_B2, ADAM_EPS, ADAM_WD, ADAM_STEP = 0.001, 0.9, 0.999, 1e-08, 0.01, 10
MESH_ID = pl.DeviceIdType.MESH
ANY = pl.BlockSpec(memory_space=pl.ANY)


def _cp(*sem):
    return pltpu.CompilerParams(dimension_semantics=sem or None, vmem_limit_bytes=VMEM_LIMIT_V7X)


def _pick_tile(n, cap):
    best = None
    for t in range(LANES, min(n, cap) + 1, LANES):
        if n % t == 0:
            best = t
    assert best is not None, (n, cap)
    return best


def _sigmoid(x):
    return 1.0 / (1.0 + jnp.exp(-x))


def _silu(x):
    return x * _sigmoid(x)


def _dsilu(x):
    s = _sigmoid(x)
    return s * (1.0 + x * (1.0 - s))


def _log_sigmoid(z):
    return jnp.minimum(z, 0.0) - jnp.log(1.0 + jnp.exp(-jnp.abs(z)))


def _rms_stats(x):
    r = lax.rsqrt(jnp.mean(x * x, axis=-1, keepdims=True) + EPS)
    return x * r, r


def _rms_bwd(dh, xhat, r, g):
    dxh = dh * g
    dx = r * (dxh - xhat * jnp.mean(dxh * xhat, axis=-1, keepdims=True))
    return dx, jnp.sum(dh * xhat, axis=0, keepdims=True)


def _acc_rows(ref, first, val):
    @pl.when(first)
    def _():
        ref[...] = val

    @pl.when(jnp.logical_not(first))
    def _():
        ref[...] += val


def _cast_bf16(x, name):
    r, c = x.shape
    tr = ROW_TILE if r % ROW_TILE == 0 else r

    def body(x_ref, o_ref):
        o_ref[...] = x_ref[...].astype(BF16)

    return pl.pallas_call(
        body, name=name, grid=(r // tr,),
        in_specs=[pl.BlockSpec((tr, c), lambda i: (i, 0))],
        out_specs=pl.BlockSpec((tr, c), lambda i: (i, 0)),
        out_shape=jax.ShapeDtypeStruct((r, c), BF16), compiler_params=_cp("parallel"))(x)


def _cast_bf16_own_slab(x, me_arr, name):
    r, c = x.shape
    tr = ROW_TILE if r % ROW_TILE == 0 else r

    def body(me_ref, x_ref, o_ref):
        o_ref[...] = x_ref[...].astype(BF16)

    return pl.pallas_call(
        body, name=name,
        grid_spec=pltpu.PrefetchScalarGridSpec(
            num_scalar_prefetch=1, grid=(r // tr,),
            in_specs=[pl.BlockSpec((tr, c), lambda i, me: (i, 0))],
            out_specs=pl.BlockSpec((None, tr, c), lambda i, me: (me[0], i, 0))),
        out_shape=jax.ShapeDtypeStruct((N_CHIPS, r, c), BF16), compiler_params=_cp("parallel"))(me_arr, x)


def _rms_fwd(x, g, name):
    s, d = x.shape

    def body(x_ref, g_ref, h_ref):
        xhat, _ = _rms_stats(x_ref[...])
        h_ref[...] = (xhat * g_ref[...]).astype(BF16)

    return pl.pallas_call(
        body, name=name, grid=(s // ROW_TILE,),
        in_specs=[pl.BlockSpec((ROW_TILE, d), lambda i: (i, 0)), pl.BlockSpec((1, d), lambda i: (0, 0))],
        out_specs=pl.BlockSpec((ROW_TILE, d), lambda i: (i, 0)),
        out_shape=jax.ShapeDtypeStruct((s, d), BF16), compiler_params=_cp("parallel"))(x, g)


def _mm_nn(a, w3, name):
    m, k = a.shape
    nsh, _, ns = w3.shape
    tm = 512 if m % 512 == 0 else ROW_TILE
    tn = _pick_tile(ns, 1024)
    per = ns // tn

    def body(a_ref, w_ref, o_ref):
        o_ref[...] = jnp.dot(a_ref[...], w_ref[...], preferred_element_type=F32)

    return pl.pallas_call(
        body, name=name, grid=(nsh * per, m // tm),
        in_specs=[pl.BlockSpec((tm, k), lambda n, i: (i, 0)),
                  pl.BlockSpec((None, k, tn), lambda n, i: (n // per, 0, n % per))],
        out_specs=pl.BlockSpec((tm, tn), lambda n, i: (i, n)),
        out_shape=jax.ShapeDtypeStruct((m, nsh * ns), F32), compiler_params=_cp("parallel", "parallel"))(a, w3)


def _mm_nt(a, b, name):
    m, k = a.shape
    n = b.shape[0]
    tm = 512 if m % 512 == 0 else ROW_TILE

    def body(a_ref, b_ref, o_ref):
        o_ref[...] = lax.dot_general(a_ref[...], b_ref[...], (((1,), (1,)), ((), ())), preferred_element_type=F32)

    return pl.pallas_call(
        body, name=name, grid=(m // tm,),
        in_specs=[pl.BlockSpec((tm, k), lambda i: (i, 0)), pl.BlockSpec((n, k), lambda i: (0, 0))],
        out_specs=pl.BlockSpec((tm, n), lambda i: (i, 0)),
        out_shape=jax.ShapeDtypeStruct((m, n), F32), compiler_params=_cp("parallel"))(a, b)


def _mm_tn(a, b, nsh, name):
    s, m = a.shape
    n = b.shape[1]
    ns = n // nsh
    tm = 512 if m % 512 == 0 else ROW_TILE
    tn = _pick_tile(ns, 1024)
    per = ns // tn

    def body(a_ref, b_ref, o_ref):
        o_ref[...] = lax.dot_general(a_ref[...], b_ref[...], (((0,), (0,)), ((), ())),
                                     preferred_element_type=F32).astype(BF16)

    return pl.pallas_call(
        body, name=name, grid=(nsh * per, m // tm),
        in_specs=[pl.BlockSpec((s, tm), lambda j, i: (0, i)), pl.BlockSpec((s, tn), lambda j, i: (0, j))],
        out_specs=pl.BlockSpec((None, tm, tn), lambda j, i: (j // per, i, j % per)),
        out_shape=jax.ShapeDtypeStruct((nsh, m, ns), BF16), compiler_params=_cp("parallel", "parallel"))(a, b)


def _tri(n, rel):
    row = lax.broadcasted_iota(jnp.int32, (n, n), 0)
    col = lax.broadcasted_iota(jnp.int32, (n, n), 1)
    return jnp.where(rel(row, col), 1.0, 0.0).astype(BF16)


def _dot_split(x, tri):
    hi = x.astype(BF16)
    lo = (x - hi.astype(F32)).astype(BF16)
    return jnp.dot(hi, tri, preferred_element_type=F32) + jnp.dot(lo, tri, preferred_element_type=F32)


def _nt(a, b):
    return lax.dot_general(a, b, (((1,), (1,)), ((), ())), preferred_element_type=F32)


def _tn(a, b):
    return lax.dot_general(a, b, (((0,), (0,)), ((), ())), preferred_element_type=F32)


def _sba_fwd(p, sbw, name):
    s = p.shape[0]
    nh = sbw // HEAD_DIM
    blk = ATT_BLOCK
    scale = 1.0 / math.sqrt(HEAD_DIM)

    def body(q_ref, k_ref, v_ref, o_ref, lt_ref, kb_ref, vb_ref):
        i = pl.program_id(1)

        @pl.when(i == 0)
        def _():
            kb_ref[...] = k_ref[...].astype(BF16)
            vb_ref[...] = v_ref[...].astype(BF16)

        q = q_ref[...].astype(BF16)
        later = _tri(blk, lambda r, c: r > c)
        t_idx = i * blk + lax.broadcasted_iota(jnp.int32, (blk, blk), 0)
        col = lax.broadcasted_iota(jnp.int32, (blk, blk), 1)

        def step(n, carry):
            acc, run = carry
            j = i - n
            rows = pl.ds(pl.multiple_of(j * blk, blk), blk)
            mask = (j * blk + col) < t_idx
            z = _nt(q, kb_ref[rows, :]) * scale
            ls = _log_sigmoid(z)
            lm = jnp.where(mask, ls - z, 0.0)
            stay = _dot_split(lm, later) + run
            w = jnp.where(mask, jnp.exp(ls + stay), 0.0)
            acc = acc + jnp.dot(w.astype(BF16), vb_ref[rows, :], preferred_element_type=F32)
            return acc, run + jnp.sum(lm, axis=1, keepdims=True)

        acc, run = lax.fori_loop(0, i + 1, step, (jnp.zeros((blk, HEAD_DIM), F32), jnp.zeros((blk, 1), F32)))
        o_ref[...] = acc
        lt_ref[...] = jnp.broadcast_to(run, (blk, HEAD_DIM))

    return pl.pallas_call(
        body, name=name, grid=(nh, s // blk),
        in_specs=[pl.BlockSpec((blk, HEAD_DIM), lambda h, i: (i, h)),
                  pl.BlockSpec((s, HEAD_DIM), lambda h, i: (0, nh + h)),
                  pl.BlockSpec((s, HEAD_DIM), lambda h, i: (0, 2 * nh + h))],
        out_specs=[pl.BlockSpec((blk, HEAD_DIM), lambda h, i: (i, h))] * 2,
        out_shape=[jax.ShapeDtypeStruct((s, sbw), F32)] * 2,
        scratch_shapes=[pltpu.VMEM((s, HEAD_DIM), BF16)] * 2,
        compiler_params=_cp("parallel", "arbitrary"))(p, p, p)


def _sba_bwd(p, out, ltot, dout, sbw, name):
    s = p.shape[0]
    nh = sbw // HEAD_DIM
    blk = ATT_BLOCK
    nq = s // blk
    scale = 1.0 / math.sqrt(HEAD_DIM)

    def body(q_ref, k_ref, v_ref, lt_ref, do_ref, dq_ref, dk_ref, dv_ref, kb_ref, vb_ref, dka_ref, dva_ref):
        i = pl.program_id(1)

        @pl.when(i == 0)
        def _():
            kb_ref[...] = k_ref[...].astype(BF16)
            vb_ref[...] = v_ref[...].astype(BF16)
            dka_ref[...] = jnp.zeros_like(dka_ref)
            dva_ref[...] = jnp.zeros_like(dva_ref)

        q = q_ref[...].astype(BF16)
        do = do_ref[...].astype(BF16)
        ltot_row = lt_ref[:, 0:1]
        upto = _tri(blk, lambda r, c: r <= c)
        before = _tri(blk, lambda r, c: r < c)
        t_idx = i * blk + lax.broadcasted_iota(jnp.int32, (blk, blk), 0)
        col = lax.broadcasted_iota(jnp.int32, (blk, blk), 1)

        def step(j, carry):
            dq, lsum, asum = carry
            rows = pl.ds(pl.multiple_of(j * blk, blk), blk)
            mask = (j * blk + col) < t_idx
            kj = kb_ref[rows, :]
            vj = vb_ref[rows, :]
            z = _nt(q, kj) * scale
            ls = _log_sigmoid(z)
            lm = jnp.where(mask, ls - z, 0.0)
            stay = ltot_row - lsum - _dot_split(lm, upto)
            w = jnp.where(mask, jnp.exp(ls + stay), 0.0)
            da = _nt(do, vj) * w
            e = asum + _dot_split(da, before)
            sig = jnp.exp(ls)
            dz = (da * (1.0 - sig) - jnp.where(mask, sig * e, 0.0)) * scale
            dzb = dz.astype(BF16)
            dq = dq + jnp.dot(dzb, kj, preferred_element_type=F32)
            dka_ref[rows, :] += _tn(dzb, q)
            dva_ref[rows, :] += _tn(w.astype(BF16), do)
            return dq, lsum + jnp.sum(lm, axis=1, keepdims=True), asum + jnp.sum(da, axis=1, keepdims=True)

        zero = jnp.zeros((blk, 1), F32)
        dq, _, _ = lax.fori_loop(0, i + 1, step, (jnp.zeros((blk, HEAD_DIM), F32), zero, zero))
        dq_ref[...] = dq.astype(BF16)

        @pl.when(i == nq - 1)
        def _():
            dk_ref[...] = dka_ref[...].astype(BF16)
            dv_ref[...] = dva_ref[...].astype(BF16)

    blk_spec = pl.BlockSpec((blk, HEAD_DIM), lambda h, i: (i, h))
    col_spec = pl.BlockSpec((s, HEAD_DIM), lambda h, i: (0, h))
    return pl.pallas_call(
        body, name=name, grid=(nh, nq),
        in_specs=[blk_spec,
                  pl.BlockSpec((s, HEAD_DIM), lambda h, i: (0, nh + h)),
                  pl.BlockSpec((s, HEAD_DIM), lambda h, i: (0, 2 * nh + h)),
                  blk_spec, blk_spec],
        out_specs=[blk_spec, col_spec, col_spec],
        out_shape=[jax.ShapeDtypeStruct((s, sbw), BF16)] * 3,
        scratch_shapes=[pltpu.VMEM((s, HEAD_DIM), BF16)] * 2 + [pltpu.VMEM((s, HEAD_DIM), F32)] * 2,
        compiler_params=_cp("parallel", "arbitrary"))(p, p, p, ltot, dout)


def _pool_groups(pad_ref, tile, row0, gd, halo):
    row = row0 + lax.broadcasted_iota(jnp.int32, (tile, 1), 0)
    out = []
    for gi, win in enumerate(POOL_WINDOWS):
        cs = slice(gi * gd, (gi + 1) * gd)
        tok = pad_ref[halo:halo + tile, cs]
        acc = tok
        for j in range(1, win):
            acc = acc + pad_ref[halo - j:halo - j + tile, cs]
        cnt = jnp.minimum(win, row + 1).astype(F32)
        out.append(acc / cnt - tok)
    return out


def _even_mix_fwd(p, att, pool_w, pool_scale, d, name):
    s = p.shape[0]
    half = d // 2
    gd = half // len(POOL_WINDOWS)
    t, hb = ROW_TILE, POOL_HALO

    def body(u_ref, uh_ref, g_ref, a_ref, pw_ref, sc_ref, y_ref, pad_ref):
        i = pl.program_id(0)
        pad_ref[0:hb, :] = jnp.where(i > 0, uh_ref[...], 0.0)
        pad_ref[hb:, :] = u_ref[...]
        pooled = _pool_groups(pad_ref, t, i * t, gd, hb)
        for gi in range(len(POOL_WINDOWS)):
            cs = slice(gi * gd, (gi + 1) * gd)
            po = jnp.dot(pooled[gi].astype(BF16), pw_ref[gi], preferred_element_type=F32) * sc_ref[:, cs]
            y_ref[:, half + gi * gd:half + (gi + 1) * gd] = (po * _silu(g_ref[:, half + gi * gd:half + (gi + 1) * gd])).astype(BF16)
        y_ref[:, :half] = (a_ref[...] * _silu(g_ref[:, :half])).astype(BF16)

    return pl.pallas_call(
        body, name=name, grid=(s // t,),
        in_specs=[pl.BlockSpec((t, half), lambda i: (i, 3)),
                  pl.BlockSpec((hb, half), lambda i: (jnp.maximum(i * (t // hb) - 1, 0), 3)),
                  pl.BlockSpec((t, d), lambda i: (i, 2)),
                  pl.BlockSpec((t, half), lambda i: (i, 0)),
                  pl.BlockSpec(pool_w.shape, lambda i: (0, 0, 0)),
                  pl.BlockSpec((1, half), lambda i: (0, 0))],
        out_specs=pl.BlockSpec((t, d), lambda i: (i, 0)),
        out_shape=jax.ShapeDtypeStruct((s, d), BF16),
        scratch_shapes=[pltpu.VMEM((hb + t, half), F32)],
        compiler_params=_cp("parallel"))(p, p, p, att, pool_w, pool_scale)


def _even_mix_bwd(p, att, dy, pool_w, pool_scale, d, name):
    s = p.shape[0]
    half = d // 2
    ng = len(POOL_WINDOWS)
    gd = half // ng
    t, hb = ROW_TILE, POOL_HALO
    nt = s // t

    def body(u_ref, uh_ref, g_ref, gh_ref, a_ref, dy_ref, dyh_ref, pw_ref, sc_ref,
             da_ref, du_ref, dg_ref, dsc_ref, dpw_ref, pad_ref, dn_ref):
        i = pl.program_id(0)
        first = i == 0
        pad_ref[0:hb, :] = jnp.where(i > 0, uh_ref[...], 0.0)
        pad_ref[hb:, :] = u_ref[...]
        pooled = _pool_groups(pad_ref, t, i * t, gd, hb)
        g1 = g_ref[:, :half]
        dy1 = dy_ref[:, :half]
        da_ref[...] = dy1 * _silu(g1)
        dg_ref[:, :half] = (dy1 * a_ref[...] * _dsilu(g1)).astype(BF16)
        row = i * t + lax.broadcasted_iota(jnp.int32, (t + hb, 1), 0)
        for gi, win in enumerate(POOL_WINDOWS):
            cs = slice(gi * gd, (gi + 1) * gd)
            cs2 = slice(half + gi * gd, half + (gi + 1) * gd)
            w = pw_ref[gi]
            pb = pooled[gi].astype(BF16)
            zp = jnp.dot(pb, w, preferred_element_type=F32)
            g2 = g_ref[:, cs2]
            dy2 = dy_ref[:, cs2]
            dg_ref[:, cs2] = (dy2 * zp * sc_ref[:, cs] * _dsilu(g2)).astype(BF16)
            dpo = dy2 * _silu(g2)
            _acc_rows(dsc_ref.at[:, cs], first, jnp.sum(dpo * zp, axis=0, keepdims=True))
            dz = (dpo * sc_ref[:, cs]).astype(BF16)
            _acc_rows(dpw_ref.at[gi], first, _tn(pb, dz))
            dzh = jnp.where(i < nt - 1, dyh_ref[:, cs] * _silu(gh_ref[:, cs]) * sc_ref[:, cs], 0.0).astype(BF16)
            dpool = _nt(dz, w)
            dpool_h = _nt(dzh, w)
            cnt = jnp.minimum(win, row + 1).astype(F32)
            dn_ref[0:t, cs] = dpool / cnt[0:t]
            dn_ref[t:, cs] = dpool_h / cnt[t:]
            acc = dn_ref[0:t, cs]
            for j in range(1, win):
                acc = acc + dn_ref[j:j + t, cs]
            du_ref[:, cs] = (acc - dpool).astype(BF16)

    return pl.pallas_call(
        body, name=name, grid=(nt,),
        in_specs=[pl.BlockSpec((t, half), lambda i: (i, 3)),
                  pl.BlockSpec((hb, half), lambda i: (jnp.maximum(i * (t // hb) - 1, 0), 3)),
                  pl.BlockSpec((t, d), lambda i: (i, 2)),
                  pl.BlockSpec((hb, half), lambda i: (jnp.minimum((i + 1) * (t // hb), s // hb - 1), 5)),
                  pl.BlockSpec((t, half), lambda i: (i, 0)),
                  pl.BlockSpec((t, d), lambda i: (i, 0)),
                  pl.BlockSpec((hb, half), lambda i: (jnp.minimum((i + 1) * (t // hb), s // hb - 1), 1)),
                  pl.BlockSpec(pool_w.shape, lambda i: (0, 0, 0)),
                  pl.BlockSpec((1, half), lambda i: (0, 0))],
        out_specs=[pl.BlockSpec((t, half), lambda i: (i, 0)),
                   pl.BlockSpec((t, half), lambda i: (i, 0)),
                   pl.BlockSpec((t, d), lambda i: (i, 0)),
                   pl.BlockSpec((1, half), lambda i: (0, 0)),
                   pl.BlockSpec((ng, gd, gd), lambda i: (0, 0, 0))],
        out_shape=[jax.ShapeDtypeStruct((s, half), F32), jax.ShapeDtypeStruct((s, half), BF16),
                   jax.ShapeDtypeStruct((s, d), BF16), jax.ShapeDtypeStruct((1, half), F32),
                   jax.ShapeDtypeStruct((ng, gd, gd), F32)],
        scratch_shapes=[pltpu.VMEM((hb + t, half), F32), pltpu.VMEM((t + hb, half), F32)],
        compiler_params=_cp("arbitrary"))(p, p, p, p, att, dy, dy, pool_w, pool_scale)


def _mm_out_even(y, w, x, g_post, g_pre_next, name):
    s, k = y.shape
    d = w.shape[1]
    t = ROW_TILE

    def body(y_ref, w_ref, x_ref, gp_ref, gn_ref, o_ref, x1_ref, h1_ref):
        o = jnp.dot(y_ref[...], w_ref[...], preferred_element_type=F32)
        o_ref[...] = o
        ohat, _ = _rms_stats(o)
        x1 = x_ref[...] + ohat * gp_ref[...]
        x1_ref[...] = x1
        xhat, _ = _rms_stats(x1)
        h1_ref[...] = (xhat * gn_ref[...]).astype(BF16)

    row = lambda c: pl.BlockSpec((t, c), lambda i: (i, 0))
    vec = pl.BlockSpec((1, d), lambda i: (0, 0))
    return pl.pallas_call(
        body, name=name, grid=(s // t,),
        in_specs=[row(k), pl.BlockSpec((k, d), lambda i: (0, 0)), row(d), vec, vec],
        out_specs=[row(d), row(d), row(d)],
        out_shape=[jax.ShapeDtypeStruct((s, d), F32), jax.ShapeDtypeStruct((s, d), F32),
                   jax.ShapeDtypeStruct((s, d), BF16)],
        compiler_params=_cp("parallel"))(y, w, x, g_post, g_pre_next)


def _mm_out_odd(y, w, x1, g_post, target, name):
    s, k = y.shape
    d = w.shape[1]
    t = ROW_TILE

    def body(y_ref, w_ref, x_ref, gp_ref, tg_ref, do_ref, dx_ref, loss_ref, dgp_ref):
        first = pl.program_id(0) == 0
        o = jnp.dot(y_ref[...], w_ref[...], preferred_element_type=F32)
        ohat, r = _rms_stats(o)
        gp = gp_ref[...]
        diff = x_ref[...] + ohat * gp - tg_ref[...]
        part = 0.5 * jnp.sum(jnp.mean(diff * diff, axis=-1, keepdims=True), axis=0, keepdims=True)
        _acc_rows(loss_ref, first, jnp.broadcast_to(part, loss_ref.shape))
        dx2 = diff * (1.0 / d)
        dx_ref[...] = dx2
        do, dgp = _rms_bwd(dx2, ohat, r, gp)
        do_ref[...] = do.astype(BF16)
        _acc_rows(dgp_ref, first, dgp)

    row = lambda c: pl.BlockSpec((t, c), lambda i: (i, 0))
    vec = pl.BlockSpec((1, d), lambda i: (0, 0))
    return pl.pallas_call(
        body, name=name, grid=(s // t,),
        in_specs=[row(k), pl.BlockSpec((k, d), lambda i: (0, 0)), row(d), vec, row(d)],
        out_specs=[row(d), row(d), pl.BlockSpec((8, LANES), lambda i: (0, 0)), vec],
        out_shape=[jax.ShapeDtypeStruct((s, d), BF16), jax.ShapeDtypeStruct((s, d), F32),
                   jax.ShapeDtypeStruct((8, LANES), F32), jax.ShapeDtypeStruct((1, d), F32)],
        compiler_params=_cp("arbitrary"))(y, w, x1, g_post, target)


def _layer_norm(d1, cg, cb):
    mu = jnp.mean(d1, axis=-1, keepdims=True)
    cen = d1 - mu
    rstd = lax.rsqrt(jnp.mean(cen * cen, axis=-1, keepdims=True) + EPS)
    n = cen * rstd
    return n, rstd, n * cg + cb


def _odd_mix_fwd(p, sconv_w, dconv_w, dconv_b, cnorm_g, cnorm_b, d, name):
    s = p.shape[0]
    w = d // 2
    k3, k31 = sconv_w.shape[0], dconv_w.shape[0]
    t, hb = ROW_TILE, CONV_HALO
    assert hb >= k31 - 1 and w % LANES == 0

    def body(p_ref, ph_ref, w3_ref, w31_ref, b31_ref, cg_ref, cb_ref, y_ref, s3_ref, d1_ref, mpad, dpad):
        i = pl.program_id(0)
        mpad[0:hb, :] = jnp.where(i > 0, ph_ref[:, 2 * w:3 * w] * ph_ref[:, 0:w], 0.0)
        mpad[hb:, :] = p_ref[:, 2 * w:3 * w] * p_ref[:, 0:w]
        dpad[0:hb, :] = jnp.where(i > 0, ph_ref[:, 3 * w:4 * w] * _sigmoid(ph_ref[:, 4 * w:5 * w]), 0.0)
        dpad[hb:, :] = p_ref[:, 3 * w:4 * w] * _sigmoid(p_ref[:, 4 * w:5 * w])
        for c0 in range(0, w, LANES):
            cs = slice(c0, c0 + LANES)
            acc = jnp.zeros((t, LANES), F32)
            for kk in range(k3):
                acc = acc + w3_ref[kk:kk + 1, cs] * mpad[hb - (k3 - 1) + kk:hb - (k3 - 1) + kk + t, cs]
            s3_ref[:, cs] = acc
            acc = jnp.zeros((t, LANES), F32)
            for kk in range(k31):
                acc = acc + w31_ref[kk:kk + 1, cs] * dpad[hb - (k31 - 1) + kk:hb - (k31 - 1) + kk + t, cs]
            d1_ref[:, cs] = acc + b31_ref[:, cs]
        _, _, d2 = _layer_norm(d1_ref[...], cg_ref[...], cb_ref[...])
        y_ref[:, :w] = (p_ref[:, w:2 * w] * s3_ref[...] * _silu(p_ref[:, 5 * w:6 * w])).astype(BF16)
        y_ref[:, w:] = (_silu(d2) * _silu(p_ref[:, 6 * w:7 * w])).astype(BF16)

    row = lambda c: pl.BlockSpec((t, c), lambda i: (i, 0))
    full = lambda a: pl.BlockSpec(a.shape, lambda i: (0, 0))
    return pl.pallas_call(
        body, name=name, grid=(s // t,),
        in_specs=[row(7 * w),
                  pl.BlockSpec((hb, 5 * w), lambda i: (jnp.maximum(i * (t // hb) - 1, 0), 0)),
                  full(sconv_w), full(dconv_w), full(dconv_b), full(cnorm_g), full(cnorm_b)],
        out_specs=[row(d), row(w), row(w)],
        out_shape=[jax.ShapeDtypeStruct((s, d), BF16), jax.ShapeDtypeStruct((s, w), F32),
                   jax.ShapeDtypeStruct((s, w), F32)],
        scratch_shapes=[pltpu.VMEM((hb + t, w), F32)] * 2,
        compiler_params=_cp("parallel"))(p, p, sconv_w, dconv_w, dconv_b, cnorm_g, cnorm_b)


def _odd_bwd_rows(p, s3, d1, dy, cnorm_g, cnorm_b, d, name):
    s = p.shape[0]
    w = d // 2
    t = ROW_TILE

    def body(bc_ref, g1_ref, g2_ref, s3_ref, d1_ref, dy_ref, cg_ref, cb_ref,
             dbc_ref, dg_ref, ds3_ref, dd1_ref, dcg_ref, dcb_ref, db_ref):
        first = pl.program_id(0) == 0
        g1, g2 = g1_ref[...], g2_ref[...]
        bc, s3v = bc_ref[...], s3_ref[...]
        dy1, dy2 = dy_ref[:, :w], dy_ref[:, w:]
        n, rstd, d2 = _layer_norm(d1_ref[...], cg_ref[...], cb_ref[...])
        dg_ref[:, :w] = (dy1 * bc * s3v * _dsilu(g1)).astype(BF16)
        dg_ref[:, w:] = (dy2 * _silu(d2) * _dsilu(g2)).astype(BF16)
        dco = dy1 * _silu(g1)
        dbc_ref[...] = (dco * s3v).astype(BF16)
        ds3_ref[...] = dco * bc
        dd2 = dy2 * _silu(g2) * _dsilu(d2)
        _acc_rows(dcb_ref, first, jnp.sum(dd2, axis=0, keepdims=True))
        _acc_rows(dcg_ref, first, jnp.sum(dd2 * n, axis=0, keepdims=True))
        dn = dd2 * cg_ref[...]
        dd1 = rstd * (dn - jnp.mean(dn, axis=-1, keepdims=True) - n * jnp.mean(dn * n, axis=-1, keepdims=True))
        dd1_ref[...] = dd1
        _acc_rows(db_ref, first, jnp.sum(dd1, axis=0, keepdims=True))

    col = lambda j: pl.BlockSpec((t, w), lambda i: (i, j))
    row = lambda c: pl.BlockSpec((t, c), lambda i: (i, 0))
    vec = pl.BlockSpec((1, w), lambda i: (0, 0))
    return pl.pallas_call(
        body, name=name, grid=(s // t,),
        in_specs=[col(1), col(5), col(6), row(w), row(w), row(d), vec, vec],
        out_specs=[row(w), row(d), row(w), row(w), vec, vec, vec],
        out_shape=[jax.ShapeDtypeStruct((s, w), BF16), jax.ShapeDtypeStruct((s, d), BF16),
                   jax.ShapeDtypeStruct((s, w), F32), jax.ShapeDtypeStruct((s, w), F32)]
        + [jax.ShapeDtypeStruct((1, w), F32)] * 3,
        compiler_params=_cp("arbitrary"))(p, p, p, s3, d1, dy, cnorm_g, cnorm_b)


def _odd_bwd_conv(p, ds3, dd1, sconv_w, dconv_w, d, name):
    s = p.shape[0]
    w = d // 2
    k3, k31 = sconv_w.shape[0], dconv_w.shape[0]
    t, hb, ha = ROW_TILE, CONV_HALO, 8
    nt = s // t
    assert hb >= k31 - 1 and ha >= k3 - 1 and k3 <= 8 and k31 <= 32

    def body(hc_ref, cc_ref, ga_ref, gb_ref, hch_ref, cch_ref, gah_ref, gbh_ref, ds3_ref, ds3h_ref, dd1_ref, dd1h_ref,
             w3_ref, w31_ref, dhc_ref, dcc_ref, dga_ref, dgb_ref, dw3_ref, dw31_ref, mpad, dpad, s3pad, d1pad):
        i = pl.program_id(0)
        first = i == 0
        last = i == nt - 1
        mpad[0:hb, :] = jnp.where(i > 0, cch_ref[...] * hch_ref[...], 0.0)
        mpad[hb:, :] = cc_ref[...] * hc_ref[...]
        dpad[0:hb, :] = jnp.where(i > 0, gah_ref[...] * _sigmoid(gbh_ref[...]), 0.0)
        dpad[hb:, :] = ga_ref[...] * _sigmoid(gb_ref[...])
        s3pad[0:t, :] = ds3_ref[...]
        s3pad[t:, :] = jnp.where(last, 0.0, ds3h_ref[...])
        d1pad[0:t, :] = dd1_ref[...]
        d1pad[t:, :] = jnp.where(last, 0.0, dd1h_ref[...])

        @pl.when(first)
        def _():
            dw3_ref[...] = jnp.zeros_like(dw3_ref)
            dw31_ref[...] = jnp.zeros_like(dw31_ref)

        for c0 in range(0, w, LANES):
            cs = slice(c0, c0 + LANES)
            ds3v = s3pad[0:t, cs]
            dd1v = d1pad[0:t, cs]
            dm = jnp.zeros((t, LANES), F32)
            for kk in range(k3):
                dm = dm + w3_ref[kk:kk + 1, cs] * s3pad[k3 - 1 - kk:k3 - 1 - kk + t, cs]
                off = hb - (k3 - 1) + kk
                dw3_ref[kk:kk + 1, cs] += jnp.sum(ds3v * mpad[off:off + t, cs], axis=0, keepdims=True)
            dd0 = jnp.zeros((t, LANES), F32)
            for kk in range(k31):
                dd0 = dd0 + w31_ref[kk:kk + 1, cs] * d1pad[k31 - 1 - kk:k31 - 1 - kk + t, cs]
                off = hb - (k31 - 1) + kk
                dw31_ref[kk:kk + 1, cs] += jnp.sum(dd1v * dpad[off:off + t, cs], axis=0, keepdims=True)
            dcc_ref[:, cs] = (dm * hc_ref[:, cs]).astype(BF16)
            dhc_ref[:, cs] = (dm * cc_ref[:, cs]).astype(BF16)
            sgb = _sigmoid(gb_ref[:, cs])
            dga_ref[:, cs] = (dd0 * sgb).astype(BF16)
            dgb_ref[:, cs] = (dd0 * ga_ref[:, cs] * sgb * (1.0 - sgb)).astype(BF16)

    col = lambda j: pl.BlockSpec((t, w), lambda i: (i, j))
    pre = lambda j: pl.BlockSpec((hb, w), lambda i: (jnp.maximum(i * (t // hb) - 1, 0), j))
    row = pl.BlockSpec((t, w), lambda i: (i, 0))
    post = lambda h: pl.BlockSpec((h, w), lambda i: (jnp.minimum((i + 1) * (t // h), s // h - 1), 0))
    full = lambda a: pl.BlockSpec(a.shape, lambda i: (0, 0))
    return pl.pallas_call(
        body, name=name, grid=(nt,),
        in_specs=[col(0), col(2), col(3), col(4), pre(0), pre(2), pre(3), pre(4),
                  row, post(ha), row, post(hb), full(sconv_w), full(dconv_w)],
        out_specs=[row, row, row, row, pl.BlockSpec((8, w), lambda i: (0, 0)), pl.BlockSpec((32, w), lambda i: (0, 0))],
        out_shape=[jax.ShapeDtypeStruct((s, w), BF16)] * 4
        + [jax.ShapeDtypeStruct((8, w), F32), jax.ShapeDtypeStruct((32, w), F32)],
        scratch_shapes=[pltpu.VMEM((hb + t, w), F32)] * 2 + [pltpu.VMEM((t + ha, w), F32), pltpu.VMEM((t + hb, w), F32)],
        compiler_params=_cp("arbitrary"))(p, p, p, p, p, p, p, p, ds3, ds3, dd1, dd1, sconv_w, dconv_w)


def _mm_in_bwd(dp, w3, x, g_pre, dres, post, name):
    s = dp.shape[0]
    nsh, d, ns = w3.shape
    t = ROW_TILE

    def body(*refs):
        if post is None:
            dp_ref, w_ref, x_ref, g_ref, dr_ref, dx_ref, dg_ref, acc_ref = refs
        else:
            dp_ref, w_ref, x_ref, g_ref, dr_ref, o_ref, gp_ref, dx_ref, dg_ref, do_ref, dgp_ref, acc_ref = refs
        kk = pl.program_id(1)
        first = pl.program_id(0) == 0
        part = _nt(dp_ref[...], w_ref[...])

        @pl.when(kk == 0)
        def _():
            acc_ref[...] = part

        @pl.when(kk > 0)
        def _():
            acc_ref[...] += part

        @pl.when(kk == nsh - 1)
        def _():
            xhat, r = _rms_stats(x_ref[...])
            dxn, dg = _rms_bwd(acc_ref[...], xhat, r, g_ref[...])
            dx = dr_ref[...] + dxn
            dx_ref[...] = dx
            _acc_rows(dg_ref, first, dg)
            if post is not None:
                ohat, ro = _rms_stats(o_ref[...])
                do, dgp = _rms_bwd(dx, ohat, ro, gp_ref[...])
                do_ref[...] = do.astype(BF16)
                _acc_rows(dgp_ref, first, dgp)

    row = pl.BlockSpec((t, d), lambda i, k: (i, 0))
    vec = pl.BlockSpec((1, d), lambda i, k: (0, 0))
    in_specs = [pl.BlockSpec((t, ns), lambda i, k: (i, k)), pl.BlockSpec((None, d, ns), lambda i, k: (k, 0, 0)), row, vec, row]
    out_specs = [row, vec]
    out_shape = [jax.ShapeDtypeStruct((s, d), F32), jax.ShapeDtypeStruct((1, d), F32)]
    args = [dp, w3, x, g_pre, dres]
    if post is not None:
        in_specs += [row, vec]
        out_specs += [row, vec]
        out_shape += [jax.ShapeDtypeStruct((s, d), BF16), jax.ShapeDtypeStruct((1, d), F32)]
        args += list(post)
    return pl.pallas_call(
        body, name=name, grid=(s // t, nsh), in_specs=in_specs, out_specs=out_specs, out_shape=out_shape,
        scratch_shapes=[pltpu.VMEM((t, d), F32)],
        compiler_params=_cp("arbitrary", "arbitrary"))(*args)


def _half_add(g, r1, c_arr, name):
    nsh, rows, ns = g.shape
    h = rows // 2
    tr = min(ROW_TILE, h)
    per = h // tr

    def body(c_ref, g_ref, r_ref, o_ref):
        o_ref[...] = (g_ref[...].astype(F32) + r_ref[...].astype(F32)).astype(BF16)

    spec = pl.BlockSpec((None, tr, ns), lambda s, r, c: (s, r, 0))
    return pl.pallas_call(
        body, name=name,
        grid_spec=pltpu.PrefetchScalarGridSpec(
            num_scalar_prefetch=1, grid=(nsh, per),
            in_specs=[pl.BlockSpec((None, tr, ns), lambda s, r, c: (s, c[0] * per + r, 0)), spec], out_specs=spec),
        out_shape=jax.ShapeDtypeStruct((nsh, h, ns), BF16), compiler_params=_cp("parallel", "parallel"))(c_arr, g, r1)


def _sum_chips(hh, r2, mc_arr, name):
    _, h, ns = hh.shape
    tr = min(ROW_TILE, h)
    per = h // tr

    def body(mc_ref, h_ref, a_ref, b_ref, c_ref, o_ref):
        o_ref[...] = ((h_ref[...].astype(F32) + a_ref[...].astype(F32)) + b_ref[...].astype(F32)) + c_ref[...].astype(F32)

    got = lambda k: pl.BlockSpec((None, tr, ns), lambda r, mc: (k, r, 0))
    return pl.pallas_call(
        body, name=name,
        grid_spec=pltpu.PrefetchScalarGridSpec(
            num_scalar_prefetch=1, grid=(per,),
            in_specs=[pl.BlockSpec((None, tr, ns), lambda r, mc: (mc[0], r, 0)), got(0), got(1), got(2)],
            out_specs=pl.BlockSpec((tr, ns), lambda r, mc: (mc[1] * per + r, 0))),
        out_shape=jax.ShapeDtypeStruct((2 * h, ns), F32), compiler_params=_cp("parallel"))(mc_arr, hh, r2, r2, r2)


def _add2(a, b, name):
    def body(a_ref, b_ref, o_ref):
        o_ref[...] = a_ref[...] + b_ref[...]

    return pl.pallas_call(body, name=name, out_shape=jax.ShapeDtypeStruct(a.shape, a.dtype), compiler_params=_cp())(a, b)


def _sum_chips_ordered(s2, r2, mc_arr, name):
    rows, w = s2.shape
    rh = rows // 2

    def body(mc_ref, s_ref, a_ref, b_ref, c_ref, o_ref):
        me = mc_ref[0]
        acc = None
        for j in range(N_CHIPS):
            rel = jnp.bitwise_xor(me, j)
            v = jnp.where(rel == 0, s_ref[...], jnp.where(rel == 2, a_ref[...], jnp.where(rel == 1, b_ref[...], c_ref[...])))
            acc = v if acc is None else acc + v
        o_ref[...] = acc

    got = lambda k: pl.BlockSpec((None, rh, w), lambda i, mc: (k, 0, 0))
    return pl.pallas_call(
        body, name=name,
        grid_spec=pltpu.PrefetchScalarGridSpec(
            num_scalar_prefetch=1, grid=(1,),
            in_specs=[pl.BlockSpec((rh, w), lambda i, mc: (mc[1], 0)), got(0), got(1), got(2)],
            out_specs=pl.BlockSpec((rh, w), lambda i, mc: (mc[1], 0))),
        out_shape=jax.ShapeDtypeStruct((rows, w), F32), compiler_params=_cp("arbitrary"))(mc_arr, s2, r2, r2, r2)


def _adamw(w, g, m, v, name):
    r, c = w.shape
    tr = ROW_TILE if r % ROW_TILE == 0 else r
    c1 = 1.0 / (1.0 - ADAM_B1 ** ADAM_STEP)
    c2 = 1.0 / (1.0 - ADAM_B2 ** ADAM_STEP)

    def body(w_ref, g_ref, m_ref, v_ref, d_ref, nm_ref, nv_ref):
        gv = g_ref[...]
        nm = ADAM_B1 * m_ref[...] + (1.0 - ADAM_B1) * gv
        nv = ADAM_B2 * v_ref[...] + (1.0 - ADAM_B2) * (gv * gv)
        nm_ref[...] = nm
        nv_ref[...] = nv
        d_ref[...] = -ADAM_LR * ((nm * c1) / (jnp.sqrt(nv * c2) + ADAM_EPS) + ADAM_WD * w_ref[...])

    spec = pl.BlockSpec((tr, c), lambda i: (i, 0))
    return pl.pallas_call(
        body, name=name, grid=(r // tr,), in_specs=[spec] * 4, out_specs=[spec] * 3,
        out_shape=[jax.ShapeDtypeStruct((r, c), F32)] * 3, compiler_params=_cp("parallel"))(w, g, m, v)


def _rcopy(src, dst, ssem, rsem, dev):
    return pltpu.make_async_remote_copy(src_ref=src, dst_ref=dst, send_sem=ssem, recv_sem=rsem,
                                        device_id=dev, device_id_type=MESH_ID)


def _place():
    x, y, c = lax.axis_index("x"), lax.axis_index("y"), lax.axis_index("c")
    chips = [(1 - x, y), (x, 1 - y), (1 - x, 1 - y)]
    return x, y, c, 2 * x + y, chips, (x, y, 1 - c)


def _gather_weights(bigs, pool_w, pack_w, pack_d, name):
    nb = len(bigs)
    smalls = [pool_w, pack_w, pack_d]
    q, cw, cd = pool_w.shape[1], pack_w.shape[1], pack_d.shape[1]
    halves = [b.shape[1] // 2 for b in bigs]

    def body(*refs):
        srcs, dsts = refs[:nb + 3], refs[nb + 3:2 * (nb + 3)]
        ssem, rsem, lsem = refs[2 * (nb + 3):]
        x, y, c, me, chips, sib = _place()

        def big_dst(a, chip, half):
            return dsts[a].at[chip, pl.ds(half * halves[a], halves[a])]

        def small_dst(n, chip):
            if n == 0:
                return dsts[nb].at[:, pl.ds(chip * q, q), :]
            return dsts[nb + n].at[:, pl.ds(chip * (cw if n == 1 else cd), cw if n == 1 else cd)]

        local = [pltpu.make_async_copy(srcs[nb + n], small_dst(n, me), lsem.at[n]) for n in range(3)]
        for cp in local:
            cp.start()
        sends = []
        for a in range(nb):
            for k, chip in enumerate(chips):
                cp = _rcopy(srcs[a].at[me, pl.ds(c * halves[a], halves[a])], big_dst(a, me, c),
                            ssem.at[6 * a + k], rsem.at[6 * a + k], (*chip, c))
                cp.start()
                sends.append(cp)
        for n in range(3):
            for k, chip in enumerate(chips):
                cp = _rcopy(srcs[nb + n], small_dst(n, me), ssem.at[6 * nb + 3 * n + k], rsem.at[6 * nb + 3 * n + k], (*chip, c))
                cp.start()
                sends.append(cp)
        for a in range(nb):
            for k, chip in enumerate(chips):
                ref = big_dst(a, 2 * chip[0] + chip[1], c)
                _rcopy(ref, ref, ssem.at[6 * a + k], rsem.at[6 * a + k], (*chip, c)).wait_recv()
                cp = _rcopy(ref, ref, ssem.at[6 * a + 3 + k], rsem.at[6 * a + 3 + k], sib)
                cp.start()
                sends.append(cp)
        for a in range(nb):
            for k, chip in enumerate(chips):
                ref = big_dst(a, 2 * chip[0] + chip[1], 1 - c)
                _rcopy(ref, ref, ssem.at[6 * a + 3 + k], rsem.at[6 * a + 3 + k], sib).wait_recv()
        for n in range(3):
            for k, chip in enumerate(chips):
                ref = small_dst(n, 2 * chip[0] + chip[1])
                _rcopy(ref, ref, ssem.at[6 * nb + 3 * n + k], rsem.at[6 * nb + 3 * n + k], (*chip, c)).wait_recv()
        for cp in sends:
            cp.wait_send()
        for cp in local:
            cp.wait()

    nsem = 6 * nb + 9
    out_shape = [jax.ShapeDtypeStruct(b.shape, b.dtype) for b in bigs]
    out_shape += [jax.ShapeDtypeStruct((pool_w.shape[0], N_CHIPS * q, pool_w.shape[2]), pool_w.dtype),
                  jax.ShapeDtypeStruct((pack_w.shape[0], N_CHIPS * cw), pack_w.dtype),
                  jax.ShapeDtypeStruct((pack_d.shape[0], N_CHIPS * cd), pack_d.dtype)]
    return pl.pallas_call(
        body, name=name, in_specs=[ANY] * (nb + 3), out_specs=[ANY] * (nb + 3), out_shape=out_shape,
        input_output_aliases={a: a for a in range(nb)},
        scratch_shapes=[pltpu.SemaphoreType.DMA((nsem,)), pltpu.SemaphoreType.DMA((nsem,)), pltpu.SemaphoreType.DMA((3,))],
        compiler_params=pltpu.CompilerParams(has_side_effects=True))(*bigs, *smalls)


def _swap_with_sibling(grads, small, name):
    n = len(grads)
    halves = [g.shape[1] // 2 for g in grads]

    def body(*refs):
        srcs, dsts = refs[:n + 1], refs[n + 1:2 * (n + 1)]
        ssem, rsem = refs[2 * (n + 1):]
        x, y, c, me, chips, sib = _place()
        cps = [_rcopy(srcs[a].at[:, pl.ds((1 - c) * halves[a], halves[a]), :], dsts[a], ssem.at[a], rsem.at[a], sib)
               for a in range(n)]
        cps.append(_rcopy(srcs[n], dsts[n], ssem.at[n], rsem.at[n], sib))
        for cp in cps:
            cp.start()
        for cp in cps:
            cp.wait_recv()
        for cp in cps:
            cp.wait_send()

    out_shape = [jax.ShapeDtypeStruct((g.shape[0], h, g.shape[2]), g.dtype) for g, h in zip(grads, halves)]
    out_shape.append(jax.ShapeDtypeStruct(small.shape, small.dtype))
    return pl.pallas_call(
        body, name=name, in_specs=[ANY] * (n + 1), out_specs=[ANY] * (n + 1), out_shape=out_shape,
        scratch_shapes=[pltpu.SemaphoreType.DMA((n + 1,)), pltpu.SemaphoreType.DMA((n + 1,))],
        compiler_params=pltpu.CompilerParams(has_side_effects=True))(*grads, small)


def _scatter_to_chips(halves_in, small, name):
    n = len(halves_in)
    rh = small.shape[0] // 2

    def body(*refs):
        srcs, dsts = refs[:n + 1], refs[n + 1:2 * (n + 1)]
        ssem, rsem = refs[2 * (n + 1):]
        x, y, c, me, chips, sib = _place()
        cps = []
        for a in range(n + 1):
            for k, chip in enumerate(chips):
                src = srcs[a].at[2 * chip[0] + chip[1]] if a < n else srcs[a].at[pl.ds(c * rh, rh)]
                cps.append(_rcopy(src, dsts[a].at[k], ssem.at[3 * a + k], rsem.at[3 * a + k], (*chip, c)))
        for cp in cps:
            cp.start()
        for cp in cps:
            cp.wait_recv()
        for cp in cps:
            cp.wait_send()

    out_shape = [jax.ShapeDtypeStruct((3,) + h.shape[1:], h.dtype) for h in halves_in]
    out_shape.append(jax.ShapeDtypeStruct((3, rh, small.shape[1]), small.dtype))
    return pl.pallas_call(
        body, name=name, in_specs=[ANY] * (n + 1), out_specs=[ANY] * (n + 1), out_shape=out_shape,
        scratch_shapes=[pltpu.SemaphoreType.DMA((3 * (n + 1),)), pltpu.SemaphoreType.DMA((3 * (n + 1),))],
        compiler_params=pltpu.CompilerParams(has_side_effects=True))(*halves_in, small)


def _join_halves(parts, name):
    n = len(parts)

    def body(*refs):
        srcs, dsts = refs[:n], refs[n:2 * n]
        ssem, rsem = refs[2 * n:]
        x, y, c, me, chips, sib = _place()
        cps = []
        for a in range(n):
            h = srcs[a].shape[0] // 2
            cps.append(_rcopy(srcs[a].at[pl.ds(c * h, h)], dsts[a].at[pl.ds(c * h, h)], ssem.at[a], rsem.at[a], sib))
        for cp in cps:
            cp.start()
        for a in range(n):
            h = srcs[a].shape[0] // 2
            theirs = dsts[a].at[pl.ds((1 - c) * h, h)]
            _rcopy(theirs, theirs, ssem.at[a], rsem.at[a], sib).wait_recv()
        for cp in cps:
            cp.wait_send()

    out_shape = [jax.ShapeDtypeStruct(p.shape, p.dtype) for p in parts]
    return pl.pallas_call(
        body, name=name, in_specs=[ANY] * n, out_specs=[ANY] * n, out_shape=out_shape,
        input_output_aliases={a: a for a in range(n)},
        scratch_shapes=[pltpu.SemaphoreType.DMA((n,)), pltpu.SemaphoreType.DMA((n,))],
        compiler_params=pltpu.CompilerParams(has_side_effects=True))(*parts)


def _pad_rows(a, rows):
    return jnp.pad(a, ((0, rows - a.shape[0]), (0, 0)))


def _stack_rows(parts, multiple):
    padded = [_pad_rows(p, -(-p.shape[0] // 8) * 8) for p in parts]
    starts, at = [], 0
    for p in padded:
        starts.append(at)
        at += p.shape[0]
    total = -(-at // multiple) * multiple
    if total > at:
        padded.append(jnp.zeros((total - at, parts[0].shape[1]), parts[0].dtype))
    return jnp.concatenate(padded, axis=0), starts


def _forward_backward(x2d, tgt, ln_pre_even, win_e, pool_w_f, pool_scale, wout_e, ln_post_even, ln_pre_odd, win_o,
                      sconv, dconv, dconv_b, cnorm_g, cnorm_b, wout_o, ln_post_odd):
    d = x2d.shape[1]
    half = d // 2
    gd = pool_w_f.shape[2]
    h0 = _rms_fwd(x2d, ln_pre_even, "rms_pre_even")
    p_e = _mm_nn(h0, win_e, "proj_in_even")
    att, ltot = _sba_fwd(p_e, half, "sba_fwd")
    y_e = _even_mix_fwd(p_e, att, pool_w_f, pool_scale, d, "even_mix_fwd")
    o_e, x1, h1 = _mm_out_even(y_e, wout_e, x2d, ln_post_even, ln_pre_odd, "proj_out_even")
    p_o = _mm_nn(h1, win_o, "proj_in_odd")
    y_o, s3, d1 = _odd_mix_fwd(p_o, sconv, dconv, dconv_b, cnorm_g, cnorm_b, d, "odd_mix_fwd")
    do_o, dx2, loss_blk, dln_post_odd = _mm_out_odd(y_o, wout_o, x1, ln_post_odd, tgt, "proj_out_odd_loss")

    dy_o = _mm_nt(do_o, wout_o, "dy_odd")
    g_wout_o = _mm_tn(y_o, do_o, 1, "dw_out_odd").reshape(N_CHIPS, d // N_CHIPS, d)
    dbc, dgate_o, ds3, dd1, dcnorm_g, dcnorm_b, ddconv_b = _odd_bwd_rows(p_o, s3, d1, dy_o, cnorm_g, cnorm_b, d, "odd_bwd_rows")
    dhc, dcc, dga, dgb, dsconv, ddconv = _odd_bwd_conv(p_o, ds3, dd1, sconv, dconv, d, "odd_bwd_conv")
    dp_o = jnp.concatenate([dhc, dbc, dcc, dga, dgb, dgate_o], axis=1)
    g_win_o = _mm_tn(h1, dp_o, N_CHIPS, "dw_in_odd")
    dx1, dln_pre_odd, do_e, dln_post_even = _mm_in_bwd(dp_o, win_o, x1, ln_pre_odd, dx2, (o_e, ln_post_even), "dx_odd")

    dy_e = _mm_nt(do_e, wout_e, "dy_even")
    g_wout_e = _mm_tn(y_e, do_e, 1, "dw_out_even").reshape(N_CHIPS, d // N_CHIPS, d)
    datt, du, dgate_e, dpool_scale, dpool_w = _even_mix_bwd(p_e, att, dy_e, pool_w_f, pool_scale, d, "even_mix_bwd")
    dq, dk, dv = _sba_bwd(p_e, att, ltot, datt, half, "sba_bwd")
    dp_e = jnp.concatenate([dq, dk, dv, du, dgate_e], axis=1)
    g_win_e = _mm_tn(h0, dp_e, N_CHIPS, "dw_in_even")
    grad_x, dln_pre_even = _mm_in_bwd(dp_e, win_e, x2d, ln_pre_even, dx1, None, "dx_even")

    two = lambda v: v.reshape(2, half)
    small_parts = [two(dln_pre_even), dpool_scale, two(dln_post_even), two(dln_pre_odd), two(dln_post_odd),
                   dsconv, ddconv, ddconv_b, dcnorm_g, dcnorm_b, dpool_w.reshape(gd, half)]
    return loss_blk, grad_x, [g_win_e, g_wout_e, g_win_o, g_wout_o], small_parts


def kernel(x, ln_pre_even, w_in_even, pool_w, pool_scale, w_out_even, ln_post_even, ln_pre_odd, w_in_odd, sconv_w, dconv_w, dconv_b, cnorm_g, cnorm_b, w_out_odd, ln_post_odd, loss_target, m_ln_pre_even, m_w_in_even, m_pool_w, m_pool_scale, m_w_out_even, m_ln_post_even, m_ln_pre_odd, m_w_in_odd, m_sconv_w, m_dconv_w, m_dconv_b, m_cnorm_g, m_cnorm_b, m_w_out_odd, m_ln_post_odd, v_ln_pre_even, v_w_in_even, v_pool_w, v_pool_scale, v_w_out_even, v_ln_post_even, v_ln_pre_odd, v_w_in_odd, v_sconv_w, v_dconv_w, v_dconv_b, v_cnorm_g, v_cnorm_b, v_w_out_odd, v_ln_post_odd):
    _, s, d = x.shape
    half = d // 2
    cw = half // N_CHIPS
    ng, q, gd = pool_w.shape[1:]
    k3, k31 = sconv_w.shape[1], dconv_w.shape[1]
    x2d, tgt = x[0], loss_target[0]
    me = 2 * lax.axis_index("x") + lax.axis_index("y")
    core = lax.axis_index("c")
    c_arr = jnp.reshape(core, (1,)).astype(jnp.int32)
    me_arr = jnp.reshape(me, (1,)).astype(jnp.int32)
    mc_arr = jnp.stack([me, core]).astype(jnp.int32)

    shards = [w_in_even[0], w_out_even[0], w_in_odd[0], w_out_odd[0]]
    bigs = [_cast_bf16_own_slab(w, me_arr, f"cast_w{n}") for n, w in enumerate(shards)]
    pool_w_b = _cast_bf16(pool_w[0].reshape(ng * q, gd), "cast_pool_w").reshape(ng, q, gd)
    pack_w, at_w = _stack_rows([sconv_w[0], dconv_w[0], dconv_b, cnorm_g, cnorm_b], 8)
    pack_d, at_d = _stack_rows([ln_pre_odd, ln_post_odd], 8)
    win_e, wout_e, win_o, wout_o, pool_w_f, pack_w_f, pack_d_f = _gather_weights(bigs, pool_w_b, pack_w, pack_d, "gather_weights")
    wout_e = wout_e.reshape(d, d)
    wout_o = wout_o.reshape(d, d)
    sconv_f = pack_w_f[at_w[0]:at_w[0] + k3]
    dconv_f = pack_w_f[at_w[1]:at_w[1] + k31]
    dconv_b_f, cnorm_g_f, cnorm_b_f = (pack_w_f[at_w[n]:at_w[n] + 1] for n in (2, 3, 4))
    ln_pre_odd_f = pack_d_f[at_d[0]:at_d[0] + 1]
    ln_post_odd_f = pack_d_f[at_d[1]:at_d[1] + 1]

    loss_blk, grad_x, grads, small_parts = _forward_backward(
        x2d, tgt, ln_pre_even, win_e, pool_w_f, pool_scale, wout_e, ln_post_even, ln_pre_odd_f, win_o, sconv_f, dconv_f,
        dconv_b_f, cnorm_g_f, cnorm_b_f, wout_o, ln_post_odd_f)
    loss = lax.psum(loss_blk[0, 0], ("x", "y", "c"))

    small, at_s = _stack_rows(small_parts, 16)
    *got1, small1 = _swap_with_sibling(grads, small, "swap_with_sibling")
    halves = [_half_add(g, r, c_arr, f"half_add{n}") for n, (g, r) in enumerate(zip(grads, got1))]
    small2 = _add2(small, small1, "small_add")
    *got2, small_got = _scatter_to_chips(halves, small2, "scatter_to_chips")
    parts = [_sum_chips(h, r, mc_arr, f"sum_chips{n}") for n, (h, r) in enumerate(zip(halves, got2))]
    parts.append(_sum_chips_ordered(small2, small_got, mc_arr, "small_sum"))
    gw_in_e, gw_out_e, gw_in_o, gw_out_o, red = _join_halves(parts, "join_halves")

    def rows(n, cnt):
        return red[at_s[n]:at_s[n] + cnt]

    def mine(a, width):
        return lax.dynamic_slice_in_dim(a, me * width, width, axis=1)

    quarter = d // N_CHIPS
    g_small = {
        "ln_pre_even": rows(0, 2).reshape(1, d),
        "pool_scale": rows(1, 1),
        "ln_post_even": rows(2, 2).reshape(1, d),
        "ln_pre_odd": mine(rows(3, 2).reshape(1, d), quarter),
        "ln_post_odd": mine(rows(4, 2).reshape(1, d), quarter),
        "sconv_w": mine(rows(5, k3), cw),
        "dconv_w": mine(rows(6, k31), cw),
        "dconv_b": mine(rows(7, 1), cw),
        "cnorm_g": mine(rows(8, 1), cw),
        "cnorm_b": mine(rows(9, 1), cw),
        "pool_w": lax.dynamic_slice_in_dim(rows(10, gd).reshape(ng, gd, gd), me * q, q, axis=1).reshape(ng * q, gd),
    }
    w2d = {
        "ln_pre_even": ln_pre_even, "w_in_even": w_in_even[0], "pool_w": pool_w[0].reshape(ng * q, gd),
        "pool_scale": pool_scale, "w_out_even": w_out_even[0], "ln_post_even": ln_post_even, "ln_pre_odd": ln_pre_odd,
        "w_in_odd": w_in_odd[0], "sconv_w": sconv_w[0], "dconv_w": dconv_w[0], "dconv_b": dconv_b, "cnorm_g": cnorm_g,
        "cnorm_b": cnorm_b, "w_out_odd": w_out_odd[0], "ln_post_odd": ln_post_odd,
    }
    moments = {
        "ln_pre_even": (m_ln_pre_even, v_ln_pre_even), "w_in_even": (m_w_in_even, v_w_in_even),
        "pool_w": (m_pool_w, v_pool_w), "pool_scale": (m_pool_scale, v_pool_scale),
        "w_out_even": (m_w_out_even, v_w_out_even), "ln_post_even": (m_ln_post_even, v_ln_post_even),
        "ln_pre_odd": (m_ln_pre_odd, v_ln_pre_odd), "w_in_odd": (m_w_in_odd, v_w_in_odd),
        "sconv_w": (m_sconv_w, v_sconv_w), "dconv_w": (m_dconv_w, v_dconv_w), "dconv_b": (m_dconv_b, v_dconv_b),
        "cnorm_g": (m_cnorm_g, v_cnorm_g), "cnorm_b": (m_cnorm_b, v_cnorm_b),
        "w_out_odd": (m_w_out_odd, v_w_out_odd), "ln_post_odd": (m_ln_post_odd, v_ln_post_odd),
    }
    g2d = dict(g_small, w_in_even=gw_in_e, w_out_even=gw_out_e, w_in_odd=gw_in_o, w_out_odd=gw_out_o)
    grads_out, deltas, new_m, new_v = [], [], [], []
    for name, w in w2d.items():
        m_in, v_in = moments[name]
        shape = m_in.shape
        delta, nm, nv = _adamw(w, g2d[name], m_in.reshape(w.shape), v_in.reshape(w.shape), "adamw_" + name)
        grads_out.append(g2d[name].reshape(shape))
        deltas.append(delta.reshape(shape))
        new_m.append(nm.reshape(shape))
        new_v.append(nv.reshape(shape))
    return (loss, grad_x.reshape(x.shape), *grads_out, *deltas, *new_m, *new_v)
```

```python
import functools
import math

import jax
import jax.numpy as jnp
from jax import lax
from jax.experimental import pallas as pl
from jax.experimental.pallas import tpu as pltpu

F32 = jnp.float32
BF16 = jnp.bfloat16
EPS = 1e-6
N_CHIPS = 4
VMEM_LIMIT_V7X = 56 << 20
HEAD_DIM = 128
ATT_BLOCK = 256
POOL_WINDOWS = (2, 4, 8, 16)
ROW_TILE = 256
POOL_HALO = 16
CONV_HALO = 32
LANES = 128
ADAM_LR, ADAM_B1, ADAM_B2, ADAM_EPS, ADAM_WD, ADAM_STEP = 0.001, 0.9, 0.999, 1e-08, 0.01, 10
MESH_ID = pl.DeviceIdType.MESH
ANY = pl.BlockSpec(memory_space=pl.ANY)


def _cp(*sem):
    return pltpu.CompilerParams(dimension_semantics=sem or None, vmem_limit_bytes=VMEM_LIMIT_V7X)


def _pick_tile(n, cap):
    best = None
    for t in range(LANES, min(n, cap) + 1, LANES):
        if n % t == 0:
            best = t
    assert best is not None, (n, cap)
    return best


def _sigmoid(x):
    return 1.0 / (1.0 + jnp.exp(-x))


def _silu(x):
    return x * _sigmoid(x)


def _dsilu(x):
    s = _sigmoid(x)
    return s * (1.0 + x * (1.0 - s))


def _log_sigmoid(z):
    return jnp.minimum(z, 0.0) - jnp.log(1.0 + jnp.exp(-jnp.abs(z)))


def _rms_stats(x):
    r = lax.rsqrt(jnp.mean(x * x, axis=-1, keepdims=True) + EPS)
    return x * r, r


def _rms_bwd(dh, xhat, r, g):
    dxh = dh * g
    dx = r * (dxh - xhat * jnp.mean(dxh * xhat, axis=-1, keepdims=True))
    return dx, jnp.sum(dh * xhat, axis=0, keepdims=True)


def _acc_rows(ref, first, val):
    @pl.when(first)
    def _():
        ref[...] = val

    @pl.when(jnp.logical_not(first))
    def _():
        ref[...] += val


def _rcopy(src, dst, ssem, rsem, dev):
    return pltpu.make_async_remote_copy(src_ref=src, dst_ref=dst, send_sem=ssem, recv_sem=rsem,
                                        device_id=dev, device_id_type=MESH_ID)


def _place():
    x, y, c = lax.axis_index("x"), lax.axis_index("y"), lax.axis_index("c")
    chips = [(1 - x, y), (x, 1 - y), (1 - x, 1 - y)]
    return x, y, c, 2 * x + y, chips, (x, y, 1 - c)


class _GatherPlan:
    def __init__(self, arrays):
        self.operands = list(arrays)
        self.out_shapes = [jax.ShapeDtypeStruct(a.shape, a.dtype) for a in arrays]
        self.aliases = {i: i for i in range(len(arrays))}
        self.nsems = 6 * len(arrays)
        self.halves = [a.shape[1] // 2 for a in arrays]

    def _slab(self, ref, a, chip, half):
        return ref.at[chip, pl.ds(half * self.halves[a], self.halves[a])]

    def _sends(self, ins, outs, ssem, rsem):
        x, y, c, me, chips, sib = _place()
        return [_rcopy(self._slab(ins[a], a, me, c), self._slab(outs[a], a, me, c), ssem.at[6 * a + k], rsem.at[6 * a + k], (*chip, c))
                for a in range(len(ins)) for k, chip in enumerate(chips)]

    def _onward(self, outs, ssem, rsem, half_of):
        x, y, c, me, chips, sib = _place()
        out = []
        for a in range(len(outs)):
            for k, chip in enumerate(chips):
                ref = self._slab(outs[a], a, 2 * chip[0] + chip[1], half_of(c))
                out.append(_rcopy(ref, ref, ssem.at[6 * a + 3 + k], rsem.at[6 * a + 3 + k], sib))
        return out

    def start(self, ins, outs, ssem, rsem):
        for cp in self._sends(ins, outs, ssem, rsem):
            cp.start()

    def mid(self, ins, outs, ssem, rsem):
        x, y, c, me, chips, sib = _place()
        landed = [_rcopy(self._slab(outs[a], a, 2 * chip[0] + chip[1], c), self._slab(outs[a], a, 2 * chip[0] + chip[1], c),
                         ssem.at[6 * a + k], rsem.at[6 * a + k], (*chip, c))
                  for a in range(len(outs)) for k, chip in enumerate(chips)]
        for got, cp in zip(landed, self._onward(outs, ssem, rsem, lambda c: c)):
            got.wait_recv()
            cp.start()

    def finish(self, ins, outs, ssem, rsem):
        for cp in self._onward(outs, ssem, rsem, lambda c: 1 - c):
            cp.wait_recv()
        for cp in self._sends(ins, outs, ssem, rsem) + self._onward(outs, ssem, rsem, lambda c: c):
            cp.wait_send()


class _ScatterPlan:
    def __init__(self, arrays):
        self.operands = list(arrays)
        self.out_shapes = [jax.ShapeDtypeStruct((3,) + a.shape[1:], a.dtype) for a in arrays]
        self.aliases = {}
        self.nsems = 3 * len(arrays)

    def _copies(self, ins, outs, ssem, rsem):
        x, y, c, me, chips, sib = _place()
        return [_rcopy(ins[a].at[2 * chip[0] + chip[1]], outs[a].at[k], ssem.at[3 * a + k], rsem.at[3 * a + k], (*chip, c))
                for a in range(len(ins)) for k, chip in enumerate(chips)]

    def start(self, ins, outs, ssem, rsem):
        for cp in self._copies(ins, outs, ssem, rsem):
            cp.start()

    def mid(self, ins, outs, ssem, rsem):
        pass

    def finish(self, ins, outs, ssem, rsem):
        cps = self._copies(ins, outs, ssem, rsem)
        for cp in cps:
            cp.wait_recv()
        for cp in cps:
            cp.wait_send()


class _Host:
    def __init__(self, comm, in_specs, out_specs, out_shape, scratch):
        self.comm = comm
        self.n_in, self.n_out = len(in_specs), len(out_specs)
        self.in_specs, self.out_specs, self.out_shape, self.scratch = list(in_specs), list(out_specs), list(out_shape), list(scratch)
        self.aliases = {}
        self.args = []
        if comm is not None:
            self.in_specs += [ANY] * len(comm.operands)
            self.out_specs += [ANY] * len(comm.out_shapes)
            self.out_shape += comm.out_shapes
            self.scratch += [pltpu.SemaphoreType.DMA((comm.nsems,)), pltpu.SemaphoreType.DMA((comm.nsems,))]
            self.aliases = {self.n_in + i: self.n_out + j for i, j in comm.aliases.items()}
            self.args = list(comm.operands)

    def split(self, refs):
        nc = len(self.args)
        nco = len(self.out_shape) - self.n_out
        ins, p = refs[:self.n_in], self.n_in + nc
        outs, rest = refs[p:p + self.n_out], refs[p + self.n_out + nco:]
        self._cargs = None
        if self.comm is not None:
            self._cargs = (refs[self.n_in:p], refs[p + self.n_out:p + self.n_out + nco], rest[-2], rest[-1])
            rest = rest[:-2]
        return ins, outs, rest

    def before(self, step, total):
        if self.comm is None:
            return

        @pl.when(step == 0)
        def _():
            self.comm.start(*self._cargs)

        @pl.when(step == (3 * total) // 4)
        def _():
            self.comm.mid(*self._cargs)

    def after(self, step, total):
        if self.comm is None:
            return

        @pl.when(step == total - 1)
        def _():
            self.comm.finish(*self._cargs)

    def results(self, outs):
        return outs[:self.n_out], outs[self.n_out:]


def _cast_bf16(x, name):
    r, c = x.shape
    tr = ROW_TILE if r % ROW_TILE == 0 else r

    def body(x_ref, o_ref):
        o_ref[...] = x_ref[...].astype(BF16)

    return pl.pallas_call(
        body, name=name, grid=(r // tr,),
        in_specs=[pl.BlockSpec((tr, c), lambda i: (i, 0))],
        out_specs=pl.BlockSpec((tr, c), lambda i: (i, 0)),
        out_shape=jax.ShapeDtypeStruct((r, c), BF16), compiler_params=_cp("parallel"))(x)


def _cast_bf16_own_slab(x, me_arr, name):
    r, c = x.shape
    tr = ROW_TILE if r % ROW_TILE == 0 else r

    def body(me_ref, x_ref, o_ref):
        o_ref[...] = x_ref[...].astype(BF16)

    return pl.pallas_call(
        body, name=name,
        grid_spec=pltpu.PrefetchScalarGridSpec(
            num_scalar_prefetch=1, grid=(r // tr,),
            in_specs=[pl.BlockSpec((tr, c), lambda i, me: (i, 0))],
            out_specs=pl.BlockSpec((None, tr, c), lambda i, me: (me[0], i, 0))),
        out_shape=jax.ShapeDtypeStruct((N_CHIPS, r, c), BF16), compiler_params=_cp("parallel"))(me_arr, x)


def _rms_fwd(x, g, name):
    s, d = x.shape

    def body(x_ref, g_ref, h_ref):
        xhat, _ = _rms_stats(x_ref[...])
        h_ref[...] = (xhat * g_ref[...]).astype(BF16)

    return pl.pallas_call(
        body, name=name, grid=(s // ROW_TILE,),
        in_specs=[pl.BlockSpec((ROW_TILE, d), lambda i: (i, 0)), pl.BlockSpec((1, d), lambda i: (0, 0))],
        out_specs=pl.BlockSpec((ROW_TILE, d), lambda i: (i, 0)),
        out_shape=jax.ShapeDtypeStruct((s, d), BF16), compiler_params=_cp("parallel"))(x, g)


def _mm_nn(a, w3, name, comm=None):
    m, k = a.shape
    nsh, _, ns = w3.shape
    tm = 512 if m % 512 == 0 else ROW_TILE
    tn = _pick_tile(ns, 1024)
    per = ns // tn
    grid = (nsh * per, m // tm)
    host = _Host(comm,
                 [pl.BlockSpec((tm, k), lambda n, i: (i, 0)), pl.BlockSpec((None, k, tn), lambda n, i: (n // per, 0, n % per))],
                 [pl.BlockSpec((tm, tn), lambda n, i: (i, n))], [jax.ShapeDtypeStruct((m, nsh * ns), F32)], [])

    def body(*refs):
        (a_ref, w_ref), (o_ref,), _ = host.split(refs)
        step = pl.program_id(0) * grid[1] + pl.program_id(1)
        host.before(step, grid[0] * grid[1])
        o_ref[...] = jnp.dot(a_ref[...], w_ref[...], preferred_element_type=F32)
        host.after(step, grid[0] * grid[1])

    outs = pl.pallas_call(
        body, name=name, grid=grid, in_specs=host.in_specs, out_specs=host.out_specs, out_shape=host.out_shape,
        scratch_shapes=host.scratch, input_output_aliases=host.aliases,
        compiler_params=_cp("arbitrary", "arbitrary"))(a, w3, *host.args)
    (out,), extra = host.results(outs)
    return out, extra


def _mm_nt(a, b, name):
    m, k = a.shape
    n = b.shape[0]
    tm = 512 if m % 512 == 0 else ROW_TILE

    def body(a_ref, b_ref, o_ref):
        o_ref[...] = lax.dot_general(a_ref[...], b_ref[...], (((1,), (1,)), ((), ())), preferred_element_type=F32)

    return pl.pallas_call(
        body, name=name, grid=(m // tm,),
        in_specs=[pl.BlockSpec((tm, k), lambda i: (i, 0)), pl.BlockSpec((n, k), lambda i: (0, 0))],
        out_specs=pl.BlockSpec((tm, n), lambda i: (i, 0)),
        out_shape=jax.ShapeDtypeStruct((m, n), F32), compiler_params=_cp("parallel"))(a, b)


def _mm_tn(a, b, nsh, name):
    s, m = a.shape
    n = b.shape[1]
    ns = n // nsh
    tm = 512 if m % 512 == 0 else ROW_TILE
    tn = _pick_tile(ns, 1024)
    per = ns // tn

    def body(a_ref, b_ref, o_ref):
        o_ref[...] = lax.dot_general(a_ref[...], b_ref[...], (((0,), (0,)), ((), ())),
                                     preferred_element_type=F32).astype(BF16)

    return pl.pallas_call(
        body, name=name, grid=(nsh * per, m // tm),
        in_specs=[pl.BlockSpec((s, tm), lambda j, i: (0, i)), pl.BlockSpec((s, tn), lambda j, i: (0, j))],
        out_specs=pl.BlockSpec((None, tm, tn), lambda j, i: (j // per, i, j % per)),
        out_shape=jax.ShapeDtypeStruct((nsh, m, ns), BF16), compiler_params=_cp("parallel", "parallel"))(a, b)


def _tri(n, rel):
    row = lax.broadcasted_iota(jnp.int32, (n, n), 0)
    col = lax.broadcasted_iota(jnp.int32, (n, n), 1)
    return jnp.where(rel(row, col), 1.0, 0.0).astype(BF16)


def _dot_split(x, tri):
    hi = x.astype(BF16)
    lo = (x - hi.astype(F32)).astype(BF16)
    return jnp.dot(hi, tri, preferred_element_type=F32) + jnp.dot(lo, tri, preferred_element_type=F32)


def _nt(a, b):
    return lax.dot_general(a, b, (((1,), (1,)), ((), ())), preferred_element_type=F32)


def _tn(a, b):
    return lax.dot_general(a, b, (((0,), (0,)), ((), ())), preferred_element_type=F32)


def _sba_fwd(p, sbw, name, comm=None):
    s = p.shape[0]
    nh = sbw // HEAD_DIM
    blk = ATT_BLOCK
    nq = s // blk
    scale = 1.0 / math.sqrt(HEAD_DIM)
    host = _Host(comm,
                 [pl.BlockSpec((blk, HEAD_DIM), lambda h, i: (i, h)),
                  pl.BlockSpec((s, HEAD_DIM), lambda h, i: (0, nh + h)),
                  pl.BlockSpec((s, HEAD_DIM), lambda h, i: (0, 2 * nh + h))],
                 [pl.BlockSpec((blk, HEAD_DIM), lambda h, i: (i, h))] * 2,
                 [jax.ShapeDtypeStruct((s, sbw), F32)] * 2,
                 [pltpu.VMEM((s, HEAD_DIM), BF16)] * 2)

    def body(*refs):
        (q_ref, k_ref, v_ref), (o_ref, lt_ref), (kb_ref, vb_ref) = host.split(refs)
        i = pl.program_id(1)
        step = pl.program_id(0) * nq + i
        host.before(step, nh * nq)

        @pl.when(i == 0)
        def _():
            kb_ref[...] = k_ref[...].astype(BF16)
            vb_ref[...] = v_ref[...].astype(BF16)

        q = q_ref[...].astype(BF16)
        later = _tri(blk, lambda r, c: r > c)
        t_idx = i * blk + lax.broadcasted_iota(jnp.int32, (blk, blk), 0)
        col = lax.broadcasted_iota(jnp.int32, (blk, blk), 1)

        def key_block(n, carry):
            acc, run = carry
            j = i - n
            rows = pl.ds(pl.multiple_of(j * blk, blk), blk)
            mask = (j * blk + col) < t_idx
            z = _nt(q, kb_ref[rows, :]) * scale
            ls = _log_sigmoid(z)
            lm = jnp.where(mask, ls - z, 0.0)
            stay = _dot_split(lm, later) + run
            w = jnp.where(mask, jnp.exp(ls + stay), 0.0)
            acc = acc + jnp.dot(w.astype(BF16), vb_ref[rows, :], preferred_element_type=F32)
            return acc, run + jnp.sum(lm, axis=1, keepdims=True)

        acc, run = lax.fori_loop(0, i + 1, key_block, (jnp.zeros((blk, HEAD_DIM), F32), jnp.zeros((blk, 1), F32)))
        o_ref[...] = acc
        lt_ref[...] = jnp.broadcast_to(run, (blk, HEAD_DIM))
        host.after(step, nh * nq)

    outs = pl.pallas_call(
        body, name=name, grid=(nh, nq), in_specs=host.in_specs, out_specs=host.out_specs, out_shape=host.out_shape,
        scratch_shapes=host.scratch, input_output_aliases=host.aliases,
        compiler_params=_cp("arbitrary", "arbitrary"))(p, p, p, *host.args)
    (out, ltot), extra = host.results(outs)
    return out, ltot, extra


def _sba_bwd(p, ltot, dout, sbw, name, comm=None):
    s = p.shape[0]
    nh = sbw // HEAD_DIM
    blk = ATT_BLOCK
    nq = s // blk
    scale = 1.0 / math.sqrt(HEAD_DIM)
    blk_spec = pl.BlockSpec((blk, HEAD_DIM), lambda h, i: (i, h))
    col_spec = pl.BlockSpec((s, HEAD_DIM), lambda h, i: (0, h))
    host = _Host(comm,
                 [blk_spec, pl.BlockSpec((s, HEAD_DIM), lambda h, i: (0, nh + h)),
                  pl.BlockSpec((s, HEAD_DIM), lambda h, i: (0, 2 * nh + h)), blk_spec, blk_spec],
                 [blk_spec, col_spec, col_spec], [jax.ShapeDtypeStruct((s, sbw), BF16)] * 3,
                 [pltpu.VMEM((s, HEAD_DIM), BF16)] * 2 + [pltpu.VMEM((s, HEAD_DIM), F32)] * 2)

    def body(*refs):
        (q_ref, k_ref, v_ref, lt_ref, do_ref), (dq_ref, dk_ref, dv_ref), (kb_ref, vb_ref, dka_ref, dva_ref) = host.split(refs)
        i = pl.program_id(1)
        step = pl.program_id(0) * nq + i
        host.before(step, nh * nq)

        @pl.when(i == 0)
        def _():
            kb_ref[...] = k_ref[...].astype(BF16)
            vb_ref[...] = v_ref[...].astype(BF16)
            dka_ref[...] = jnp.zeros_like(dka_ref)
            dva_ref[...] = jnp.zeros_like(dva_ref)

        q = q_ref[...].astype(BF16)
        do = do_ref[...].astype(BF16)
        ltot_row = lt_ref[:, 0:1]
        upto = _tri(blk, lambda r, c: r <= c)
        before = _tri(blk, lambda r, c: r < c)
        t_idx = i * blk + lax.broadcasted_iota(jnp.int32, (blk, blk), 0)
        col = lax.broadcasted_iota(jnp.int32, (blk, blk), 1)

        def key_block(j, carry):
            dq, lsum, asum = carry
            rows = pl.ds(pl.multiple_of(j * blk, blk), blk)
            mask = (j * blk + col) < t_idx
            kj = kb_ref[rows, :]
            vj = vb_ref[rows, :]
            z = _nt(q, kj) * scale
            ls = _log_sigmoid(z)
            lm = jnp.where(mask, ls - z, 0.0)
            stay = ltot_row - lsum - _dot_split(lm, upto)
            w = jnp.where(mask, jnp.exp(ls + stay), 0.0)
            da = _nt(do, vj) * w
            e = asum + _dot_split(da, before)
            sig = jnp.exp(ls)
            dz = (da * (1.0 - sig) - jnp.where(mask, sig * e, 0.0)) * scale
            dzb = dz.astype(BF16)
            dq = dq + jnp.dot(dzb, kj, preferred_element_type=F32)
            dka_ref[rows, :] += _tn(dzb, q)
            dva_ref[rows, :] += _tn(w.astype(BF16), do)
            return dq, lsum + jnp.sum(lm, axis=1, keepdims=True), asum + jnp.sum(da, axis=1, keepdims=True)

        zero = jnp.zeros((blk, 1), F32)
        dq, _, _ = lax.fori_loop(0, i + 1, key_block, (jnp.zeros((blk, HEAD_DIM), F32), zero, zero))
        dq_ref[...] = dq.astype(BF16)

        @pl.when(i == nq - 1)
        def _():
            dk_ref[...] = dka_ref[...].astype(BF16)
            dv_ref[...] = dva_ref[...].astype(BF16)

        host.after(step, nh * nq)

    outs = pl.pallas_call(
        body, name=name, grid=(nh, nq), in_specs=host.in_specs, out_specs=host.out_specs, out_shape=host.out_shape,
        scratch_shapes=host.scratch, input_output_aliases=host.aliases,
        compiler_params=_cp("arbitrary", "arbitrary"))(p, p, p, ltot, dout, *host.args)
    (dq, dk, dv), extra = host.results(outs)
    return dq, dk, dv, extra


def _pool_groups(pad_ref, tile, row0, gd, halo):
    row = row0 + lax.broadcasted_iota(jnp.int32, (tile, 1), 0)
    out = []
    for gi, win in enumerate(POOL_WINDOWS):
        cs = slice(gi * gd, (gi + 1) * gd)
        tok = pad_ref[halo:halo + tile, cs]
        acc = tok
        for j in range(1, win):
            acc = acc + pad_ref[halo - j:halo - j + tile, cs]
        cnt = jnp.minimum(win, row + 1).astype(F32)
        out.append(acc / cnt - tok)
    return out


def _even_mix_fwd(p, att, pool_w, pool_scale, d, name):
    s = p.shape[0]
    half = d // 2
    gd = half // len(POOL_WINDOWS)
    t, hb = ROW_TILE, POOL_HALO

    def body(u_ref, uh_ref, g_ref, a_ref, pw_ref, sc_ref, y_ref, pad_ref):
        i = pl.program_id(0)
        pad_ref[0:hb, :] = jnp.where(i > 0, uh_ref[...], 0.0)
        pad_ref[hb:, :] = u_ref[...]
        pooled = _pool_groups(pad_ref, t, i * t, gd, hb)
        for gi in range(len(POOL_WINDOWS)):
            cs = slice(gi * gd, (gi + 1) * gd)
            po = jnp.dot(pooled[gi].astype(BF16), pw_ref[gi], preferred_element_type=F32) * sc_ref[:, cs]
            y_ref[:, half + gi * gd:half + (gi + 1) * gd] = (po * _silu(g_ref[:, half + gi * gd:half + (gi + 1) * gd])).astype(BF16)
        y_ref[:, :half] = (a_ref[...] * _silu(g_ref[:, :half])).astype(BF16)

    return pl.pallas_call(
        body, name=name, grid=(s // t,),
        in_specs=[pl.BlockSpec((t, half), lambda i: (i, 3)),
                  pl.BlockSpec((hb, half), lambda i: (jnp.maximum(i * (t // hb) - 1, 0), 3)),
                  pl.BlockSpec((t, d), lambda i: (i, 2)),
                  pl.BlockSpec((t, half), lambda i: (i, 0)),
                  pl.BlockSpec(pool_w.shape, lambda i: (0, 0, 0)),
                  pl.BlockSpec((1, half), lambda i: (0, 0))],
        out_specs=pl.BlockSpec((t, d), lambda i: (i, 0)),
        out_shape=jax.ShapeDtypeStruct((s, d), BF16),
        scratch_shapes=[pltpu.VMEM((hb + t, half), F32)],
        compiler_params=_cp("parallel"))(p, p, p, att, pool_w, pool_scale)


def _even_mix_bwd(p, att, dy, pool_w, pool_scale, d, name):
    s = p.shape[0]
    half = d // 2
    ng = len(POOL_WINDOWS)
    gd = half // ng
    t, hb = ROW_TILE, POOL_HALO
    nt = s // t

    def body(u_ref, uh_ref, g_ref, gh_ref, a_ref, dy_ref, dyh_ref, pw_ref, sc_ref,
             da_ref, du_ref, dg_ref, dsc_ref, dpw_ref, pad_ref, dn_ref):
        i = pl.program_id(0)
        first = i == 0
        pad_ref[0:hb, :] = jnp.where(i > 0, uh_ref[...], 0.0)
        pad_ref[hb:, :] = u_ref[...]
        pooled = _pool_groups(pad_ref, t, i * t, gd, hb)
        g1 = g_ref[:, :half]
        dy1 = dy_ref[:, :half]
        da_ref[...] = dy1 * _silu(g1)
        dg_ref[:, :half] = (dy1 * a_ref[...] * _dsilu(g1)).astype(BF16)
        row = i * t + lax.broadcasted_iota(jnp.int32, (t + hb, 1), 0)
        for gi, win in enumerate(POOL_WINDOWS):
            cs = slice(gi * gd, (gi + 1) * gd)
            cs2 = slice(half + gi * gd, half + (gi + 1) * gd)
            w = pw_ref[gi]
            pb = pooled[gi].astype(BF16)
            zp = jnp.dot(pb, w, preferred_element_type=F32)
            g2 = g_ref[:, cs2]
            dy2 = dy_ref[:, cs2]
            dg_ref[:, cs2] = (dy2 * zp * sc_ref[:, cs] * _dsilu(g2)).astype(BF16)
            dpo = dy2 * _silu(g2)
            _acc_rows(dsc_ref.at[:, cs], first, jnp.sum(dpo * zp, axis=0, keepdims=True))
            dz = (dpo * sc_ref[:, cs]).astype(BF16)
            _acc_rows(dpw_ref.at[gi], first, _tn(pb, dz))
            dzh = jnp.where(i < nt - 1, dyh_ref[:, cs] * _silu(gh_ref[:, cs]) * sc_ref[:, cs], 0.0).astype(BF16)
            dpool = _nt(dz, w)
            dpool_h = _nt(dzh, w)
            cnt = jnp.minimum(win, row + 1).astype(F32)
            dn_ref[0:t, cs] = dpool / cnt[0:t]
            dn_ref[t:, cs] = dpool_h / cnt[t:]
            acc = dn_ref[0:t, cs]
            for j in range(1, win):
                acc = acc + dn_ref[j:j + t, cs]
            du_ref[:, cs] = (acc - dpool).astype(BF16)

    return pl.pallas_call(
        body, name=name, grid=(nt,),
        in_specs=[pl.BlockSpec((t, half), lambda i: (i, 3)),
                  pl.BlockSpec((hb, half), lambda i: (jnp.maximum(i * (t // hb) - 1, 0), 3)),
                  pl.BlockSpec((t, d), lambda i: (i, 2)),
                  pl.BlockSpec((hb, half), lambda i: (jnp.minimum((i + 1) * (t // hb), s // hb - 1), 5)),
                  pl.BlockSpec((t, half), lambda i: (i, 0)),
                  pl.BlockSpec((t, d), lambda i: (i, 0)),
                  pl.BlockSpec((hb, half), lambda i: (jnp.minimum((i + 1) * (t // hb), s // hb - 1), 1)),
                  pl.BlockSpec(pool_w.shape, lambda i: (0, 0, 0)),
                  pl.BlockSpec((1, half), lambda i: (0, 0))],
        out_specs=[pl.BlockSpec((t, half), lambda i: (i, 0)),
                   pl.BlockSpec((t, half), lambda i: (i, 0)),
                   pl.BlockSpec((t, d), lambda i: (i, 0)),
                   pl.BlockSpec((1, half), lambda i: (0, 0)),
                   pl.BlockSpec((ng, gd, gd), lambda i: (0, 0, 0))],
        out_shape=[jax.ShapeDtypeStruct((s, half), F32), jax.ShapeDtypeStruct((s, half), BF16),
                   jax.ShapeDtypeStruct((s, d), BF16), jax.ShapeDtypeStruct((1, half), F32),
                   jax.ShapeDtypeStruct((ng, gd, gd), F32)],
        scratch_shapes=[pltpu.VMEM((hb + t, half), F32), pltpu.VMEM((t + hb, half), F32)],
        compiler_params=_cp("arbitrary"))(p, p, p, p, att, dy, dy, pool_w, pool_scale)


def _mm_out_even(y, w, x, g_post, g_pre_next, name):
    s, k = y.shape
    d = w.shape[1]
    t = ROW_TILE

    def body(y_ref, w_ref, x_ref, gp_ref, gn_ref, o_ref, x1_ref, h1_ref):
        o = jnp.dot(y_ref[...], w_ref[...], preferred_element_type=F32)
        o_ref[...] = o
        ohat, _ = _rms_stats(o)
        x1 = x_ref[...] + ohat * gp_ref[...]
        x1_ref[...] = x1
        xhat, _ = _rms_stats(x1)
        h1_ref[...] = (xhat * gn_ref[...]).astype(BF16)

    row = lambda c: pl.BlockSpec((t, c), lambda i: (i, 0))
    vec = pl.BlockSpec((1, d), lambda i: (0, 0))
    return pl.pallas_call(
        body, name=name, grid=(s // t,),
        in_specs=[row(k), pl.BlockSpec((k, d), lambda i: (0, 0)), row(d), vec, vec],
        out_specs=[row(d), row(d), row(d)],
        out_shape=[jax.ShapeDtypeStruct((s, d), F32), jax.ShapeDtypeStruct((s, d), F32),
                   jax.ShapeDtypeStruct((s, d), BF16)],
        compiler_params=_cp("parallel"))(y, w, x, g_post, g_pre_next)


def _mm_out_odd(y, w, x1, g_post, target, name):
    s, k = y.shape
    d = w.shape[1]
    t = ROW_TILE

    def body(y_ref, w_ref, x_ref, gp_ref, tg_ref, do_ref, dx_ref, loss_ref, dgp_ref):
        first = pl.program_id(0) == 0
        o = jnp.dot(y_ref[...], w_ref[...], preferred_element_type=F32)
        ohat, r = _rms_stats(o)
        gp = gp_ref[...]
        diff = x_ref[...] + ohat * gp - tg_ref[...]
        part = 0.5 * jnp.sum(jnp.mean(diff * diff, axis=-1, keepdims=True), axis=0, keepdims=True)
        _acc_rows(loss_ref, first, jnp.broadcast_to(part, loss_ref.shape))
        dx2 = diff * (1.0 / d)
        dx_ref[...] = dx2
        do, dgp = _rms_bwd(dx2, ohat, r, gp)
        do_ref[...] = do.astype(BF16)
        _acc_rows(dgp_ref, first, dgp)

    row = lambda c: pl.BlockSpec((t, c), lambda i: (i, 0))
    vec = pl.BlockSpec((1, d), lambda i: (0, 0))
    return pl.pallas_call(
        body, name=name, grid=(s // t,),
        in_specs=[row(k), pl.BlockSpec((k, d), lambda i: (0, 0)), row(d), vec, row(d)],
        out_specs=[row(d), row(d), pl.BlockSpec((8, LANES), lambda i: (0, 0)), vec],
        out_shape=[jax.ShapeDtypeStruct((s, d), BF16), jax.ShapeDtypeStruct((s, d), F32),
                   jax.ShapeDtypeStruct((8, LANES), F32), jax.ShapeDtypeStruct((1, d), F32)],
        compiler_params=_cp("arbitrary"))(y, w, x1, g_post, target)


def _layer_norm(d1, cg, cb):
    mu = jnp.mean(d1, axis=-1, keepdims=True)
    cen = d1 - mu
    rstd = lax.rsqrt(jnp.mean(cen * cen, axis=-1, keepdims=True) + EPS)
    n = cen * rstd
    return n, rstd, n * cg + cb


def _odd_mix_fwd(p, sconv_w, dconv_w, dconv_b, cnorm_g, cnorm_b, d, name):
    s = p.shape[0]
    w = d // 2
    k3, k31 = sconv_w.shape[0], dconv_w.shape[0]
    t, hb = ROW_TILE, CONV_HALO
    assert hb >= k31 - 1 and w % LANES == 0

    def body(p_ref, ph_ref, w3_ref, w31_ref, b31_ref, cg_ref, cb_ref, y_ref, s3_ref, d1_ref, mpad, dpad):
        i = pl.program_id(0)
        mpad[0:hb, :] = jnp.where(i > 0, ph_ref[:, 2 * w:3 * w] * ph_ref[:, 0:w], 0.0)
        mpad[hb:, :] = p_ref[:, 2 * w:3 * w] * p_ref[:, 0:w]
        dpad[0:hb, :] = jnp.where(i > 0, ph_ref[:, 3 * w:4 * w] * _sigmoid(ph_ref[:, 4 * w:5 * w]), 0.0)
        dpad[hb:, :] = p_ref[:, 3 * w:4 * w] * _sigmoid(p_ref[:, 4 * w:5 * w])
        for c0 in range(0, w, LANES):
            cs = slice(c0, c0 + LANES)
            acc = jnp.zeros((t, LANES), F32)
            for kk in range(k3):
                acc = acc + w3_ref[kk:kk + 1, cs] * mpad[hb - (k3 - 1) + kk:hb - (k3 - 1) + kk + t, cs]
            s3_ref[:, cs] = acc
            acc = jnp.zeros((t, LANES), F32)
            for kk in range(k31):
                acc = acc + w31_ref[kk:kk + 1, cs] * dpad[hb - (k31 - 1) + kk:hb - (k31 - 1) + kk + t, cs]
            d1_ref[:, cs] = acc + b31_ref[:, cs]
        _, _, d2 = _layer_norm(d1_ref[...], cg_ref[...], cb_ref[...])
        y_ref[:, :w] = (p_ref[:, w:2 * w] * s3_ref[...] * _silu(p_ref[:, 5 * w:6 * w])).astype(BF16)
        y_ref[:, w:] = (_silu(d2) * _silu(p_ref[:, 6 * w:7 * w])).astype(BF16)

    row = lambda c: pl.BlockSpec((t, c), lambda i: (i, 0))
    full = lambda a: pl.BlockSpec(a.shape, lambda i: (0, 0))
    return pl.pallas_call(
        body, name=name, grid=(s // t,),
        in_specs=[row(7 * w),
                  pl.BlockSpec((hb, 5 * w), lambda i: (jnp.maximum(i * (t // hb) - 1, 0), 0)),
                  full(sconv_w), full(dconv_w), full(dconv_b), full(cnorm_g), full(cnorm_b)],
        out_specs=[row(d), row(w), row(w)],
        out_shape=[jax.ShapeDtypeStruct((s, d), BF16), jax.ShapeDtypeStruct((s, w), F32),
                   jax.ShapeDtypeStruct((s, w), F32)],
        scratch_shapes=[pltpu.VMEM((hb + t, w), F32)] * 2,
        compiler_params=_cp("parallel"))(p, p, sconv_w, dconv_w, dconv_b, cnorm_g, cnorm_b)


def _odd_bwd_rows(p, s3, d1, dy, cnorm_g, cnorm_b, d, name):
    s = p.shape[0]
    w = d // 2
    t = ROW_TILE

    def body(bc_ref, g1_ref, g2_ref, s3_ref, d1_ref, dy_ref, cg_ref, cb_ref,
             dbc_ref, dg_ref, ds3_ref, dd1_ref, dcg_ref, dcb_ref, db_ref):
        first = pl.program_id(0) == 0
        g1, g2 = g1_ref[...], g2_ref[...]
        bc, s3v = bc_ref[...], s3_ref[...]
        dy1, dy2 = dy_ref[:, :w], dy_ref[:, w:]
        n, rstd, d2 = _layer_norm(d1_ref[...], cg_ref[...], cb_ref[...])
        dg_ref[:, :w] = (dy1 * bc * s3v * _dsilu(g1)).astype(BF16)
        dg_ref[:, w:] = (dy2 * _silu(d2) * _dsilu(g2)).astype(BF16)
        dco = dy1 * _silu(g1)
        dbc_ref[...] = (dco * s3v).astype(BF16)
        ds3_ref[...] = dco * bc
        dd2 = dy2 * _silu(g2) * _dsilu(d2)
        _acc_rows(dcb_ref, first, jnp.sum(dd2, axis=0, keepdims=True))
        _acc_rows(dcg_ref, first, jnp.sum(dd2 * n, axis=0, keepdims=True))
        dn = dd2 * cg_ref[...]
        dd1 = rstd * (dn - jnp.mean(dn, axis=-1, keepdims=True) - n * jnp.mean(dn * n, axis=-1, keepdims=True))
        dd1_ref[...] = dd1
        _acc_rows(db_ref, first, jnp.sum(dd1, axis=0, keepdims=True))

    col = lambda j: pl.BlockSpec((t, w), lambda i: (i, j))
    row = lambda c: pl.BlockSpec((t, c), lambda i: (i, 0))
    vec = pl.BlockSpec((1, w), lambda i: (0, 0))
    return pl.pallas_call(
        body, name=name, grid=(s // t,),
        in_specs=[col(1), col(5), col(6), row(w), row(w), row(d), vec, vec],
        out_specs=[row(w), row(d), row(w), row(w), vec, vec, vec],
        out_shape=[jax.ShapeDtypeStruct((s, w), BF16), jax.ShapeDtypeStruct((s, d), BF16),
                   jax.ShapeDtypeStruct((s, w), F32), jax.ShapeDtypeStruct((s, w), F32)]
        + [jax.ShapeDtypeStruct((1, w), F32)] * 3,
        compiler_params=_cp("arbitrary"))(p, p, p, s3, d1, dy, cnorm_g, cnorm_b)


def _odd_bwd_conv(p, ds3, dd1, sconv_w, dconv_w, d, name):
    s = p.shape[0]
    w = d // 2
    k3, k31 = sconv_w.shape[0], dconv_w.shape[0]
    t, hb, ha = ROW_TILE, CONV_HALO, 8
    nt = s // t
    assert hb >= k31 - 1 and ha >= k3 - 1 and k3 <= 8 and k31 <= 32

    def body(hc_ref, cc_ref, ga_ref, gb_ref, hch_ref, cch_ref, gah_ref, gbh_ref, ds3_ref, ds3h_ref, dd1_ref, dd1h_ref,
             w3_ref, w31_ref, dhc_ref, dcc_ref, dga_ref, dgb_ref, dw3_ref, dw31_ref, mpad, dpad, s3pad, d1pad):
        i = pl.program_id(0)
        first = i == 0
        last = i == nt - 1
        mpad[0:hb, :] = jnp.where(i > 0, cch_ref[...] * hch_ref[...], 0.0)
        mpad[hb:, :] = cc_ref[...] * hc_ref[...]
        dpad[0:hb, :] = jnp.where(i > 0, gah_ref[...] * _sigmoid(gbh_ref[...]), 0.0)
        dpad[hb:, :] = ga_ref[...] * _sigmoid(gb_ref[...])
        s3pad[0:t, :] = ds3_ref[...]
        s3pad[t:, :] = jnp.where(last, 0.0, ds3h_ref[...])
        d1pad[0:t, :] = dd1_ref[...]
        d1pad[t:, :] = jnp.where(last, 0.0, dd1h_ref[...])

        @pl.when(first)
        def _():
            dw3_ref[...] = jnp.zeros_like(dw3_ref)
            dw31_ref[...] = jnp.zeros_like(dw31_ref)

        for c0 in range(0, w, LANES):
            cs = slice(c0, c0 + LANES)
            ds3v = s3pad[0:t, cs]
            dd1v = d1pad[0:t, cs]
            dm = jnp.zeros((t, LANES), F32)
            for kk in range(k3):
                dm = dm + w3_ref[kk:kk + 1, cs] * s3pad[k3 - 1 - kk:k3 - 1 - kk + t, cs]
                off = hb - (k3 - 1) + kk
                dw3_ref[kk:kk + 1, cs] += jnp.sum(ds3v * mpad[off:off + t, cs], axis=0, keepdims=True)
            dd0 = jnp.zeros((t, LANES), F32)
            for kk in range(k31):
                dd0 = dd0 + w31_ref[kk:kk + 1, cs] * d1pad[k31 - 1 - kk:k31 - 1 - kk + t, cs]
                off = hb - (k31 - 1) + kk
                dw31_ref[kk:kk + 1, cs] += jnp.sum(dd1v * dpad[off:off + t, cs], axis=0, keepdims=True)
            dcc_ref[:, cs] = (dm * hc_ref[:, cs]).astype(BF16)
            dhc_ref[:, cs] = (dm * cc_ref[:, cs]).astype(BF16)
            sgb = _sigmoid(gb_ref[:, cs])
            dga_ref[:, cs] = (dd0 * sgb).astype(BF16)
            dgb_ref[:, cs] = (dd0 * ga_ref[:, cs] * sgb * (1.0 - sgb)).astype(BF16)

    col = lambda j: pl.BlockSpec((t, w), lambda i: (i, j))
    pre = lambda j: pl.BlockSpec((hb, w), lambda i: (jnp.maximum(i * (t // hb) - 1, 0), j))
    row = pl.BlockSpec((t, w), lambda i: (i, 0))
    post = lambda h: pl.BlockSpec((h, w), lambda i: (jnp.minimum((i + 1) * (t // h), s // h - 1), 0))
    full = lambda a: pl.BlockSpec(a.shape, lambda i: (0, 0))
    return pl.pallas_call(
        body, name=name, grid=(nt,),
        in_specs=[col(0), col(2), col(3), col(4), pre(0), pre(2), pre(3), pre(4),
                  row, post(ha), row, post(hb), full(sconv_w), full(dconv_w)],
        out_specs=[row, row, row, row, pl.BlockSpec((8, w), lambda i: (0, 0)), pl.BlockSpec((32, w), lambda i: (0, 0))],
        out_shape=[jax.ShapeDtypeStruct((s, w), BF16)] * 4
        + [jax.ShapeDtypeStruct((8, w), F32), jax.ShapeDtypeStruct((32, w), F32)],
        scratch_shapes=[pltpu.VMEM((hb + t, w), F32)] * 2 + [pltpu.VMEM((t + ha, w), F32), pltpu.VMEM((t + hb, w), F32)],
        compiler_params=_cp("arbitrary"))(p, p, p, p, p, p, p, p, ds3, ds3, dd1, dd1, sconv_w, dconv_w)


def _mm_in_bwd(dp, w3, x, g_pre, dres, post, name, comm=None):
    s = dp.shape[0]
    nsh, d, ns = w3.shape
    t = ROW_TILE
    nt = s // t
    row = pl.BlockSpec((t, d), lambda i, k: (i, 0))
    vec = pl.BlockSpec((1, d), lambda i, k: (0, 0))
    in_specs = [pl.BlockSpec((t, ns), lambda i, k: (i, k)), pl.BlockSpec((None, d, ns), lambda i, k: (k, 0, 0)), row, vec, row]
    out_specs = [row, vec]
    out_shape = [jax.ShapeDtypeStruct((s, d), F32), jax.ShapeDtypeStruct((1, d), F32)]
    args = [dp, w3, x, g_pre, dres]
    if post is not None:
        in_specs += [row, vec]
        out_specs += [row, vec]
        out_shape += [jax.ShapeDtypeStruct((s, d), BF16), jax.ShapeDtypeStruct((1, d), F32)]
        args += list(post)
    host = _Host(comm, in_specs, out_specs, out_shape, [pltpu.VMEM((t, d), F32)])

    def body(*refs):
        ins, outs, (acc_ref,) = host.split(refs)
        dp_ref, w_ref, x_ref, g_ref, dr_ref = ins[:5]
        dx_ref, dg_ref = outs[:2]
        kk = pl.program_id(1)
        first = pl.program_id(0) == 0
        step = pl.program_id(0) * nsh + kk
        host.before(step, nt * nsh)
        part = _nt(dp_ref[...], w_ref[...])

        @pl.when(kk == 0)
        def _():
            acc_ref[...] = part

        @pl.when(kk > 0)
        def _():
            acc_ref[...] += part

        @pl.when(kk == nsh - 1)
        def _():
            xhat, r = _rms_stats(x_ref[...])
            dxn, dg = _rms_bwd(acc_ref[...], xhat, r, g_ref[...])
            dx = dr_ref[...] + dxn
            dx_ref[...] = dx
            _acc_rows(dg_ref, first, dg)
            if post is not None:
                ohat, ro = _rms_stats(ins[5][...])
                do, dgp = _rms_bwd(dx, ohat, ro, ins[6][...])
                outs[2][...] = do.astype(BF16)
                _acc_rows(outs[3], first, dgp)

        host.after(step, nt * nsh)

    res = pl.pallas_call(
        body, name=name, grid=(nt, nsh), in_specs=host.in_specs, out_specs=host.out_specs, out_shape=host.out_shape,
        scratch_shapes=host.scratch, input_output_aliases=host.aliases,
        compiler_params=_cp("arbitrary", "arbitrary"))(*args, *host.args)
    return host.results(res)


def _half_add(g, r1, c_arr, name):
    nsh, rows, ns = g.shape
    h = rows // 2
    tr = min(ROW_TILE, h)
    per = h // tr

    def body(c_ref, g_ref, r_ref, o_ref):
        o_ref[...] = (g_ref[...].astype(F32) + r_ref[...].astype(F32)).astype(BF16)

    spec = pl.BlockSpec((None, tr, ns), lambda s, r, c: (s, r, 0))
    return pl.pallas_call(
        body, name=name,
        grid_spec=pltpu.PrefetchScalarGridSpec(
            num_scalar_prefetch=1, grid=(nsh, per),
            in_specs=[pl.BlockSpec((None, tr, ns), lambda s, r, c: (s, c[0] * per + r, 0)), spec], out_specs=spec),
        out_shape=jax.ShapeDtypeStruct((nsh, h, ns), BF16), compiler_params=_cp("parallel", "parallel"))(c_arr, g, r1)


def _sum_chips(hh, r2, mc_arr, name):
    _, h, ns = hh.shape
    tr = min(ROW_TILE, h)
    per = h // tr

    def body(mc_ref, h_ref, a_ref, b_ref, c_ref, o_ref):
        o_ref[...] = ((h_ref[...].astype(F32) + a_ref[...].astype(F32)) + b_ref[...].astype(F32)) + c_ref[...].astype(F32)

    got = lambda k: pl.BlockSpec((None, tr, ns), lambda r, mc: (k, r, 0))
    return pl.pallas_call(
        body, name=name,
        grid_spec=pltpu.PrefetchScalarGridSpec(
            num_scalar_prefetch=1, grid=(per,),
            in_specs=[pl.BlockSpec((None, tr, ns), lambda r, mc: (mc[0], r, 0)), got(0), got(1), got(2)],
            out_specs=pl.BlockSpec((tr, ns), lambda r, mc: (mc[1] * per + r, 0))),
        out_shape=jax.ShapeDtypeStruct((2 * h, ns), F32), compiler_params=_cp("parallel"))(mc_arr, hh, r2, r2, r2)


def _add2(a, b, name):
    def body(a_ref, b_ref, o_ref):
        o_ref[...] = a_ref[...] + b_ref[...]

    return pl.pallas_call(body, name=name, out_shape=jax.ShapeDtypeStruct(a.shape, a.dtype), compiler_params=_cp())(a, b)


def _sum_chips_ordered(s2, r2, mc_arr, name):
    rows, w = s2.shape
    rh = rows // 2

    def body(mc_ref, s_ref, a_ref, b_ref, c_ref, o_ref):
        me = mc_ref[0]
        acc = None
        for j in range(N_CHIPS):
            rel = jnp.bitwise_xor(me, j)
            v = jnp.where(rel == 0, s_ref[...], jnp.where(rel == 2, a_ref[...], jnp.where(rel == 1, b_ref[...], c_ref[...])))
            acc = v if acc is None else acc + v
        o_ref[...] = acc

    got = lambda k: pl.BlockSpec((None, rh, w), lambda i, mc: (k, 0, 0))
    return pl.pallas_call(
        body, name=name,
        grid_spec=pltpu.PrefetchScalarGridSpec(
            num_scalar_prefetch=1, grid=(1,),
            in_specs=[pl.BlockSpec((rh, w), lambda i, mc: (mc[1], 0)), got(0), got(1), got(2)],
            out_specs=pl.BlockSpec((rh, w), lambda i, mc: (mc[1], 0))),
        out_shape=jax.ShapeDtypeStruct((rows, w), F32), compiler_params=_cp("arbitrary"))(mc_arr, s2, r2, r2, r2)


def _adamw(w, g, m, v, name):
    r, c = w.shape
    tr = ROW_TILE if r % ROW_TILE == 0 else r
    c1 = 1.0 / (1.0 - ADAM_B1 ** ADAM_STEP)
    c2 = 1.0 / (1.0 - ADAM_B2 ** ADAM_STEP)

    def body(w_ref, g_ref, m_ref, v_ref, d_ref, nm_ref, nv_ref):
        gv = g_ref[...]
        nm = ADAM_B1 * m_ref[...] + (1.0 - ADAM_B1) * gv
        nv = ADAM_B2 * v_ref[...] + (1.0 - ADAM_B2) * (gv * gv)
        nm_ref[...] = nm
        nv_ref[...] = nv
        d_ref[...] = -ADAM_LR * ((nm * c1) / (jnp.sqrt(nv * c2) + ADAM_EPS) + ADAM_WD * w_ref[...])

    spec = pl.BlockSpec((tr, c), lambda i: (i, 0))
    return pl.pallas_call(
        body, name=name, grid=(r // tr,), in_specs=[spec] * 4, out_specs=[spec] * 3,
        out_shape=[jax.ShapeDtypeStruct((r, c), F32)] * 3, compiler_params=_cp("parallel"))(w, g, m, v)


def _gather_weights(bigs, pool_w, pack_w, pack_d, name):
    nb = len(bigs)
    smalls = [pool_w, pack_w, pack_d]
    q, cw, cd = pool_w.shape[1], pack_w.shape[1], pack_d.shape[1]
    halves = [b.shape[1] // 2 for b in bigs]

    def body(*refs):
        srcs, dsts = refs[:nb + 3], refs[nb + 3:2 * (nb + 3)]
        ssem, rsem, lsem = refs[2 * (nb + 3):]
        x, y, c, me, chips, sib = _place()

        def big_dst(a, chip, half):
            return dsts[a].at[chip, pl.ds(half * halves[a], halves[a])]

        def small_dst(n, chip):
            if n == 0:
                return dsts[nb].at[:, pl.ds(chip * q, q), :]
            return dsts[nb + n].at[:, pl.ds(chip * (cw if n == 1 else cd), cw if n == 1 else cd)]

        local = [pltpu.make_async_copy(srcs[nb + n], small_dst(n, me), lsem.at[n]) for n in range(3)]
        for cp in local:
            cp.start()
        sends = []
        for a in range(nb):
            for k, chip in enumerate(chips):
                cp = _rcopy(srcs[a].at[me, pl.ds(c * halves[a], halves[a])], big_dst(a, me, c),
                            ssem.at[6 * a + k], rsem.at[6 * a + k], (*chip, c))
                cp.start()
                sends.append(cp)
        for n in range(3):
            for k, chip in enumerate(chips):
                cp = _rcopy(srcs[nb + n], small_dst(n, me), ssem.at[6 * nb + 3 * n + k], rsem.at[6 * nb + 3 * n + k], (*chip, c))
                cp.start()
                sends.append(cp)
        for a in range(nb):
            for k, chip in enumerate(chips):
                ref = big_dst(a, 2 * chip[0] + chip[1], c)
                _rcopy(ref, ref, ssem.at[6 * a + k], rsem.at[6 * a + k], (*chip, c)).wait_recv()
                cp = _rcopy(ref, ref, ssem.at[6 * a + 3 + k], rsem.at[6 * a + 3 + k], sib)
                cp.start()
                sends.append(cp)
        for a in range(nb):
            for k, chip in enumerate(chips):
                ref = big_dst(a, 2 * chip[0] + chip[1], 1 - c)
                _rcopy(ref, ref, ssem.at[6 * a + 3 + k], rsem.at[6 * a + 3 + k], sib).wait_recv()
        for n in range(3):
            for k, chip in enumerate(chips):
                ref = small_dst(n, 2 * chip[0] + chip[1])
                _rcopy(ref, ref, ssem.at[6 * nb + 3 * n + k], rsem.at[6 * nb + 3 * n + k], (*chip, c)).wait_recv()
        for cp in sends:
            cp.wait_send()
        for cp in local:
            cp.wait()

    nsem = 6 * nb + 9
    out_shape = [jax.ShapeDtypeStruct(b.shape, b.dtype) for b in bigs]
    out_shape += [jax.ShapeDtypeStruct((pool_w.shape[0], N_CHIPS * q, pool_w.shape[2]), pool_w.dtype),
                  jax.ShapeDtypeStruct((pack_w.shape[0], N_CHIPS * cw), pack_w.dtype),
                  jax.ShapeDtypeStruct((pack_d.shape[0], N_CHIPS * cd), pack_d.dtype)]
    return pl.pallas_call(
        body, name=name, in_specs=[ANY] * (nb + 3), out_specs=[ANY] * (nb + 3), out_shape=out_shape,
        input_output_aliases={a: a for a in range(nb)},
        scratch_shapes=[pltpu.SemaphoreType.DMA((nsem,)), pltpu.SemaphoreType.DMA((nsem,)), pltpu.SemaphoreType.DMA((3,))],
        compiler_params=pltpu.CompilerParams(has_side_effects=True))(*bigs, *smalls)


def _swap_with_sibling(grads, wholes, name):
    n, nw = len(grads), len(wholes)
    halves = [g.shape[1] // 2 for g in grads]

    def body(*refs):
        srcs, dsts = refs[:n + nw], refs[n + nw:2 * (n + nw)]
        ssem, rsem = refs[2 * (n + nw):]
        x, y, c, me, chips, sib = _place()
        cps = [_rcopy(srcs[a].at[:, pl.ds((1 - c) * halves[a], halves[a]), :], dsts[a], ssem.at[a], rsem.at[a], sib)
               for a in range(n)]
        cps += [_rcopy(srcs[a], dsts[a], ssem.at[a], rsem.at[a], sib) for a in range(n, n + nw)]
        for cp in cps:
            cp.start()
        for cp in cps:
            cp.wait_recv()
        for cp in cps:
            cp.wait_send()

    out_shape = [jax.ShapeDtypeStruct((g.shape[0], h, g.shape[2]), g.dtype) for g, h in zip(grads, halves)]
    out_shape += [jax.ShapeDtypeStruct(w.shape, w.dtype) for w in wholes]
    return pl.pallas_call(
        body, name=name, in_specs=[ANY] * (n + nw), out_specs=[ANY] * (n + nw), out_shape=out_shape,
        scratch_shapes=[pltpu.SemaphoreType.DMA((n + nw,)), pltpu.SemaphoreType.DMA((n + nw,))],
        compiler_params=pltpu.CompilerParams(has_side_effects=True))(*grads, *wholes)


def _scatter_to_chips(halves_in, small, name):
    n = len(halves_in)
    rh = small.shape[0] // 2

    def body(*refs):
        srcs, dsts = refs[:n + 1], refs[n + 1:2 * (n + 1)]
        ssem, rsem = refs[2 * (n + 1):]
        x, y, c, me, chips, sib = _place()
        cps = []
        for a in range(n + 1):
            for k, chip in enumerate(chips):
                src = srcs[a].at[2 * chip[0] + chip[1]] if a < n else srcs[a].at[pl.ds(c * rh, rh)]
                cps.append(_rcopy(src, dsts[a].at[k], ssem.at[3 * a + k], rsem.at[3 * a + k], (*chip, c)))
        for cp in cps:
            cp.start()
        for cp in cps:
            cp.wait_recv()
        for cp in cps:
            cp.wait_send()

    out_shape = [jax.ShapeDtypeStruct((3,) + h.shape[1:], h.dtype) for h in halves_in]
    out_shape.append(jax.ShapeDtypeStruct((3, rh, small.shape[1]), small.dtype))
    return pl.pallas_call(
        body, name=name, in_specs=[ANY] * (n + 1), out_specs=[ANY] * (n + 1), out_shape=out_shape,
        scratch_shapes=[pltpu.SemaphoreType.DMA((3 * (n + 1),)), pltpu.SemaphoreType.DMA((3 * (n + 1),))],
        compiler_params=pltpu.CompilerParams(has_side_effects=True))(*halves_in, small)


def _join_halves(parts, name):
    n = len(parts)

    def body(*refs):
        srcs, dsts = refs[:n], refs[n:2 * n]
        ssem, rsem = refs[2 * n:]
        x, y, c, me, chips, sib = _place()
        cps = []
        for a in range(n):
            h = srcs[a].shape[0] // 2
            cps.append(_rcopy(srcs[a].at[pl.ds(c * h, h)], dsts[a].at[pl.ds(c * h, h)], ssem.at[a], rsem.at[a], sib))
        for cp in cps:
            cp.start()
        for a in range(n):
            h = srcs[a].shape[0] // 2
            theirs = dsts[a].at[pl.ds((1 - c) * h, h)]
            _rcopy(theirs, theirs, ssem.at[a], rsem.at[a], sib).wait_recv()
        for cp in cps:
            cp.wait_send()

    out_shape = [jax.ShapeDtypeStruct(p.shape, p.dtype) for p in parts]
    return pl.pallas_call(
        body, name=name, in_specs=[ANY] * n, out_specs=[ANY] * n, out_shape=out_shape,
        input_output_aliases={a: a for a in range(n)},
        scratch_shapes=[pltpu.SemaphoreType.DMA((n,)), pltpu.SemaphoreType.DMA((n,))],
        compiler_params=pltpu.CompilerParams(has_side_effects=True))(*parts)


def _pad_rows(a, rows):
    return jnp.pad(a, ((0, rows - a.shape[0]), (0, 0)))


def _stack_rows(parts, multiple):
    padded = [_pad_rows(p, -(-p.shape[0] // 8) * 8) for p in parts]
    starts, at = [], 0
    for p in padded:
        starts.append(at)
        at += p.shape[0]
    total = -(-at // multiple) * multiple
    if total > at:
        padded.append(jnp.zeros((total - at, parts[0].shape[1]), parts[0].dtype))
    return jnp.concatenate(padded, axis=0), starts


def kernel(x, ln_pre_even, w_in_even, pool_w, pool_scale, w_out_even, ln_post_even, ln_pre_odd, w_in_odd, sconv_w, dconv_w, dconv_b, cnorm_g, cnorm_b, w_out_odd, ln_post_odd, loss_target, m_ln_pre_even, m_w_in_even, m_pool_w, m_pool_scale, m_w_out_even, m_ln_post_even, m_ln_pre_odd, m_w_in_odd, m_sconv_w, m_dconv_w, m_dconv_b, m_cnorm_g, m_cnorm_b, m_w_out_odd, m_ln_post_odd, v_ln_pre_even, v_w_in_even, v_pool_w, v_pool_scale, v_w_out_even, v_ln_post_even, v_ln_pre_odd, v_w_in_odd, v_sconv_w, v_dconv_w, v_dconv_b, v_cnorm_g, v_cnorm_b, v_w_out_odd, v_ln_post_odd):
    _, s, d = x.shape
    half = d // 2
    cw = half // N_CHIPS
    ng, q, gd = pool_w.shape[1:]
    k3, k31 = sconv_w.shape[1], dconv_w.shape[1]
    x2d, tgt = x[0], loss_target[0]
    me = 2 * lax.axis_index("x") + lax.axis_index("y")
    core = lax.axis_index("c")
    c_arr = jnp.reshape(core, (1,)).astype(jnp.int32)
    me_arr = jnp.reshape(me, (1,)).astype(jnp.int32)
    mc_arr = jnp.stack([me, core]).astype(jnp.int32)

    shards = [w_in_even[0], w_out_even[0], w_in_odd[0], w_out_odd[0]]
    slabs = [_cast_bf16_own_slab(w, me_arr, f"cast_w{n}") for n, w in enumerate(shards)]
    pool_w_b = _cast_bf16(pool_w[0].reshape(ng * q, gd), "cast_pool_w").reshape(ng, q, gd)
    pack_w, at_w = _stack_rows([sconv_w[0], dconv_w[0], dconv_b, cnorm_g, cnorm_b], 8)
    pack_d, at_d = _stack_rows([ln_pre_odd, ln_post_odd], 8)
    win_e, pool_w_f, pack_w_f, pack_d_f = _gather_weights(slabs[:1], pool_w_b, pack_w, pack_d, "gather_first")
    sconv_f = pack_w_f[at_w[0]:at_w[0] + k3]
    dconv_f = pack_w_f[at_w[1]:at_w[1] + k31]
    dconv_b_f, cnorm_g_f, cnorm_b_f = (pack_w_f[at_w[n]:at_w[n] + 1] for n in (2, 3, 4))
    ln_pre_odd_f = pack_d_f[at_d[0]:at_d[0] + 1]
    ln_post_odd_f = pack_d_f[at_d[1]:at_d[1] + 1]

    def reduce_half(g, name):
        (got,) = _swap_with_sibling([g], [], "swap_" + name)
        return _half_add(g, got, c_arr, "half_add_" + name)

    h0 = _rms_fwd(x2d, ln_pre_even, "rms_pre_even")
    p_e, (wout_e,) = _mm_nn(h0, win_e, "proj_in_even", _GatherPlan([slabs[1]]))
    wout_e = wout_e.reshape(d, d)
    att, ltot, (win_o,) = _sba_fwd(p_e, half, "sba_fwd", _GatherPlan([slabs[2]]))
    y_e = _even_mix_fwd(p_e, att, pool_w_f, pool_scale, d, "even_mix_fwd")
    o_e, x1, h1 = _mm_out_even(y_e, wout_e, x2d, ln_post_even, ln_pre_odd_f, "proj_out_even")
    p_o, (wout_o,) = _mm_nn(h1, win_o, "proj_in_odd", _GatherPlan([slabs[3]]))
    wout_o = wout_o.reshape(d, d)
    y_o, s3, d1 = _odd_mix_fwd(p_o, sconv_f, dconv_f, dconv_b_f, cnorm_g_f, cnorm_b_f, d, "odd_mix_fwd")
    do_o, dx2, loss_blk, dln_post_odd = _mm_out_odd(y_o, wout_o, x1, ln_post_odd_f, tgt, "proj_out_odd_loss")
    loss = lax.psum(loss_blk[0, 0], ("x", "y", "c"))

    dy_o = _mm_nt(do_o, wout_o, "dy_odd")
    g_wout_o = _mm_tn(y_o, do_o, 1, "dw_out_odd").reshape(N_CHIPS, d // N_CHIPS, d)
    h_wout_o = reduce_half(g_wout_o, "out_odd")
    dbc, dgate_o, ds3, dd1, dcnorm_g, dcnorm_b, ddconv_b = _odd_bwd_rows(p_o, s3, d1, dy_o, cnorm_g_f, cnorm_b_f, d, "odd_bwd_rows")
    dhc, dcc, dga, dgb, dsconv, ddconv = _odd_bwd_conv(p_o, ds3, dd1, sconv_f, dconv_f, d, "odd_bwd_conv")
    dp_o = jnp.concatenate([dhc, dbc, dcc, dga, dgb, dgate_o], axis=1)
    g_win_o = _mm_tn(h1, dp_o, N_CHIPS, "dw_in_odd")
    h_win_o = reduce_half(g_win_o, "in_odd")
    (dx1, dln_pre_odd, do_e, dln_post_even), (s_wout_o,) = _mm_in_bwd(
        dp_o, win_o, x1, ln_pre_odd_f, dx2, (o_e, ln_post_even), "dx_odd", _ScatterPlan([h_wout_o]))

    dy_e = _mm_nt(do_e, wout_e, "dy_even")
    g_wout_e = _mm_tn(y_e, do_e, 1, "dw_out_even").reshape(N_CHIPS, d // N_CHIPS, d)
    h_wout_e = reduce_half(g_wout_e, "out_even")
    datt, du, dgate_e, dpool_scale, dpool_w = _even_mix_bwd(p_e, att, dy_e, pool_w_f, pool_scale, d, "even_mix_bwd")
    dq, dk, dv, (s_win_o, s_wout_e) = _sba_bwd(p_e, ltot, datt, half, "sba_bwd", _ScatterPlan([h_win_o, h_wout_e]))
    dp_e = jnp.concatenate([dq, dk, dv, du, dgate_e], axis=1)
    g_win_e = _mm_tn(h0, dp_e, N_CHIPS, "dw_in_even")
    h_win_e = reduce_half(g_win_e, "in_even")
    (grad_x, dln_pre_even), (s_win_e,) = _mm_in_bwd(dp_e, win_e, x2d, ln_pre_even, dx1, None, "dx_even", _ScatterPlan([h_win_e]))

    two = lambda v: v.reshape(2, half)
    small_parts = [two(dln_pre_even), dpool_scale, two(dln_post_even), two(dln_pre_odd), two(dln_post_odd),
                   dsconv, ddconv, ddconv_b, dcnorm_g, dcnorm_b, dpool_w.reshape(gd, half)]
    small, at_s = _stack_rows(small_parts, 16)
    (small1,) = _swap_with_sibling([], [small], "swap_small")
    small2 = _add2(small, small1, "small_add")
    (small_got,) = _scatter_to_chips([], small2, "scatter_small")
    pairs = [(h_win_e, s_win_e), (h_wout_e, s_wout_e), (h_win_o, s_win_o), (h_wout_o, s_wout_o)]
    parts = [_sum_chips(h, r, mc_arr, f"sum_chips{n}") for n, (h, r) in enumerate(pairs)]
    parts.append(_sum_chips_ordered(small2, small_got, mc_arr, "small_sum"))
    gw_in_e, gw_out_e, gw_in_o, gw_out_o, red = _join_halves(parts, "join_halves")

    def rows(n, cnt):
        return red[at_s[n]:at_s[n] + cnt]

    def mine(a, width):
        return lax.dynamic_slice_in_dim(a, me * width, width, axis=1)

    quarter = d // N_CHIPS
    g_small = {
        "ln_pre_even": rows(0, 2).reshape(1, d),
        "pool_scale": rows(1, 1),
        "ln_post_even": rows(2, 2).reshape(1, d),
        "ln_pre_odd": mine(rows(3, 2).reshape(1, d), quarter),
        "ln_post_odd": mine(rows(4, 2).reshape(1, d), quarter),
        "sconv_w": mine(rows(5, k3), cw),
        "dconv_w": mine(rows(6, k31), cw),
        "dconv_b": mine(rows(7, 1), cw),
        "cnorm_g": mine(rows(8, 1), cw),
        "cnorm_b": mine(rows(9, 1), cw),
        "pool_w": lax.dynamic_slice_in_dim(rows(10, gd).reshape(ng, gd, gd), me * q, q, axis=1).reshape(ng * q, gd),
    }
    w2d = {
        "ln_pre_even": ln_pre_even, "w_in_even": w_in_even[0], "pool_w": pool_w[0].reshape(ng * q, gd),
        "pool_scale": pool_scale, "w_out_even": w_out_even[0], "ln_post_even": ln_post_even, "ln_pre_odd": ln_pre_odd,
        "w_in_odd": w_in_odd[0], "sconv_w": sconv_w[0], "dconv_w": dconv_w[0], "dconv_b": dconv_b, "cnorm_g": cnorm_g,
        "cnorm_b": cnorm_b, "w_out_odd": w_out_odd[0], "ln_post_odd": ln_post_odd,
    }
    moments = {
        "ln_pre_even": (m_ln_pre_even, v_ln_pre_even), "w_in_even": (m_w_in_even, v_w_in_even),
        "pool_w": (m_pool_w, v_pool_w), "pool_scale": (m_pool_scale, v_pool_scale),
        "w_out_even": (m_w_out_even, v_w_out_even), "ln_post_even": (m_ln_post_even, v_ln_post_even),
        "ln_pre_odd": (m_ln_pre_odd, v_ln_pre_odd), "w_in_odd": (m_w_in_odd, v_w_in_odd),
        "sconv_w": (m_sconv_w, v_sconv_w), "dconv_w": (m_dconv_w, v_dconv_w), "dconv_b": (m_dconv_b, v_dconv_b),
        "cnorm_g": (m_cnorm_g, v_cnorm_g), "cnorm_b": (m_cnorm_b, v_cnorm_b),
        "w_out_odd": (m_w_out_odd, v_w_out_odd), "ln_post_odd": (m_ln_post_odd, v_ln_post_odd),
    }
    g2d = dict(g_small, w_in_even=gw_in_e, w_out_even=gw_out_e, w_in_odd=gw_in_o, w_out_odd=gw_out_o)
    grads_out, deltas, new_m, new_v = [], [], [], []
    for name, w in w2d.items():
        m_in, v_in = moments[name]
        shape = m_in.shape
        delta, nm, nv = _adamw(w, g2d[name], m_in.reshape(w.shape), v_in.reshape(w.shape), "adamw_" + name)
        grads_out.append(g2d[name].reshape(shape))
        deltas.append(delta.reshape(shape))
        new_m.append(nm.reshape(shape))
        new_v.append(nv.reshape(shape))
    return (loss, grad_x.reshape(x.shape), *grads_out, *deltas, *new_m, *new_v)
```

```python
import functools
import math

import jax
import jax.numpy as jnp
from jax import lax
from jax.experimental import pallas as pl
from jax.experimental.pallas import tpu as pltpu

F32 = jnp.float32
BF16 = jnp.bfloat16
EPS = 1e-6
N_CHIPS = 4
VMEM_LIMIT_V7X = 56 << 20
HEAD_DIM = 128
ATT_BLOCK = 256
POOL_WINDOWS = (2, 4, 8, 16)
ROW_TILE = 256
POOL_HALO = 16
CONV_HALO = 32
LANES = 128
ADAM_LR, ADAM_B1, ADAM_B2, ADAM_EPS, ADAM_WD, ADAM_STEP = 0.001, 0.9, 0.999, 1e-08, 0.01, 10
MESH_ID = pl.DeviceIdType.MESH
ANY = pl.BlockSpec(memory_space=pl.ANY)


def _cp(*sem):
    return pltpu.CompilerParams(dimension_semantics=sem or None, vmem_limit_bytes=VMEM_LIMIT_V7X)


def _pick_tile(n, cap):
    best = None
    for t in range(LANES, min(n, cap) + 1, LANES):
        if n % t == 0:
            best = t
    assert best is not None, (n, cap)
    return best


def _sigmoid(x):
    return 1.0 / (1.0 + jnp.exp(-x))


def _silu(x):
    return x * _sigmoid(x)


def _dsilu(x):
    s = _sigmoid(x)
    return s * (1.0 + x * (1.0 - s))


def _log_sigmoid(z):
    return jnp.minimum(z, 0.0) - jnp.log(1.0 + jnp.exp(-jnp.abs(z)))


def _rms_stats(x):
    r = lax.rsqrt(jnp.mean(x * x, axis=-1, keepdims=True) + EPS)
    return x * r, r


def _rms_bwd(dh, xhat, r, g):
    dxh = dh * g
    dx = r * (dxh - xhat * jnp.mean(dxh * xhat, axis=-1, keepdims=True))
    return dx, jnp.sum(dh * xhat, axis=0, keepdims=True)


def _acc_rows(ref, first, val):
    @pl.when(first)
    def _():
        ref[...] = val

    @pl.when(jnp.logical_not(first))
    def _():
        ref[...] += val


def _rcopy(src, dst, ssem, rsem, dev):
    return pltpu.make_async_remote_copy(src_ref=src, dst_ref=dst, send_sem=ssem, recv_sem=rsem,
                                        device_id=dev, device_id_type=MESH_ID)


def _place():
    x, y, c = lax.axis_index("x"), lax.axis_index("y"), lax.axis_index("c")
    chips = [(1 - x, y), (x, 1 - y), (1 - x, 1 - y)]
    return x, y, c, 2 * x + y, chips, (x, y, 1 - c)


class _GatherPlan:
    def __init__(self, arrays, part=(0, 1, 1)):
        self.operands = list(arrays)
        self.out_shapes = [jax.ShapeDtypeStruct(a.shape, a.dtype) for a in arrays]
        self.aliases = {i: i for i in range(len(arrays))}
        self.nsems = 6 * len(arrays)
        self.base = 0
        self.halves = [a.shape[1] // 2 for a in arrays]
        self.part = part

    def _slab(self, ref, a, chip, half):
        lo, hi, n = self.part
        h = self.halves[a]
        return ref.at[chip, pl.ds(half * h + lo * h // n, (hi - lo) * h // n)]

    def _sends(self, ins, outs, ssem, rsem):
        x, y, c, me, chips, sib = _place()
        return [_rcopy(self._slab(ins[a], a, me, c), self._slab(outs[a], a, me, c),
                       ssem.at[self.base + 6 * a + k], rsem.at[self.base + 6 * a + k], (*chip, c))
                for a in range(len(ins)) for k, chip in enumerate(chips)]

    def _onward(self, outs, ssem, rsem, half_of):
        x, y, c, me, chips, sib = _place()
        out = []
        for a in range(len(outs)):
            for k, chip in enumerate(chips):
                ref = self._slab(outs[a], a, 2 * chip[0] + chip[1], half_of(c))
                out.append(_rcopy(ref, ref, ssem.at[self.base + 6 * a + 3 + k], rsem.at[self.base + 6 * a + 3 + k], sib))
        return out

    def start(self, ins, outs, ssem, rsem):
        for cp in self._sends(ins, outs, ssem, rsem):
            cp.start()

    def mid(self, ins, outs, ssem, rsem):
        x, y, c, me, chips, sib = _place()
        landed = [_rcopy(self._slab(outs[a], a, 2 * chip[0] + chip[1], c), self._slab(outs[a], a, 2 * chip[0] + chip[1], c),
                         ssem.at[self.base + 6 * a + k], rsem.at[self.base + 6 * a + k], (*chip, c))
                  for a in range(len(outs)) for k, chip in enumerate(chips)]
        for got, cp in zip(landed, self._onward(outs, ssem, rsem, lambda c: c)):
            got.wait_recv()
            cp.start()

    def finish(self, ins, outs, ssem, rsem):
        for cp in self._onward(outs, ssem, rsem, lambda c: 1 - c):
            cp.wait_recv()
        for cp in self._sends(ins, outs, ssem, rsem) + self._onward(outs, ssem, rsem, lambda c: c):
            cp.wait_send()


class _ScatterPlan:
    def __init__(self, arrays, part=(0, 1, 1), into=None):
        self.n = len(arrays)
        self.operands = list(arrays) + list(into or [])
        self.out_shapes = [jax.ShapeDtypeStruct((3,) + a.shape[1:], a.dtype) for a in arrays]
        self.aliases = {self.n + i: i for i in range(self.n)} if into else {}
        self.nsems = 3 * self.n
        self.base = 0
        self.part = part

    def _copies(self, ins, outs, ssem, rsem):
        x, y, c, me, chips, sib = _place()
        lo, hi, n = self.part
        out = []
        for a in range(self.n):
            h = ins[a].shape[1]
            rows = pl.ds(lo * h // n, (hi - lo) * h // n)
            for k, chip in enumerate(chips):
                out.append(_rcopy(ins[a].at[2 * chip[0] + chip[1], rows], outs[a].at[k, rows],
                                  ssem.at[self.base + 3 * a + k], rsem.at[self.base + 3 * a + k], (*chip, c)))
        return out

    def start(self, ins, outs, ssem, rsem):
        for cp in self._copies(ins, outs, ssem, rsem):
            cp.start()

    def mid(self, ins, outs, ssem, rsem):
        pass

    def finish(self, ins, outs, ssem, rsem):
        cps = self._copies(ins, outs, ssem, rsem)
        for cp in cps:
            cp.wait_recv()
        for cp in cps:
            cp.wait_send()


class _Multi:
    def __init__(self, plans):
        self.plans = plans
        self.operands, self.out_shapes, self.aliases, self.nsems = [], [], {}, 0
        self.spans = []
        for p in plans:
            ni, no = len(self.operands), len(self.out_shapes)
            self.spans.append((ni, ni + len(p.operands), no, no + len(p.out_shapes)))
            self.aliases.update({ni + i: no + j for i, j in p.aliases.items()})
            p.base = self.nsems
            self.nsems += p.nsems
            self.operands += p.operands
            self.out_shapes += p.out_shapes

    def _each(self, what, ins, outs, ssem, rsem):
        for p, (i0, i1, o0, o1) in zip(self.plans, self.spans):
            getattr(p, what)(ins[i0:i1], outs[o0:o1], ssem, rsem)

    def start(self, *a):
        self._each("start", *a)

    def mid(self, *a):
        self._each("mid", *a)

    def finish(self, *a):
        self._each("finish", *a)

    def results(self, extra):
        return [list(extra[o0:o1]) for (_, _, o0, o1) in self.spans]


class _Host:
    def __init__(self, comm, in_specs, out_specs, out_shape, scratch):
        self.comm = comm
        self.n_in, self.n_out = len(in_specs), len(out_specs)
        self.in_specs, self.out_specs, self.out_shape, self.scratch = list(in_specs), list(out_specs), list(out_shape), list(scratch)
        self.aliases = {}
        self.args = []
        if comm is not None:
            self.in_specs += [ANY] * len(comm.operands)
            self.out_specs += [ANY] * len(comm.out_shapes)
            self.out_shape += comm.out_shapes
            self.scratch += [pltpu.SemaphoreType.DMA((comm.nsems,)), pltpu.SemaphoreType.DMA((comm.nsems,))]
            self.aliases = {self.n_in + i: self.n_out + j for i, j in comm.aliases.items()}
            self.args = list(comm.operands)

    def split(self, refs):
        nc = len(self.args)
        nco = len(self.out_shape) - self.n_out
        ins, p = refs[:self.n_in], self.n_in + nc
        outs, rest = refs[p:p + self.n_out], refs[p + self.n_out + nco:]
        self._cargs = None
        if self.comm is not None:
            self._cargs = (refs[self.n_in:p], refs[p + self.n_out:p + self.n_out + nco], rest[-2], rest[-1])
            rest = rest[:-2]
        return ins, outs, rest

    def before(self, step, total):
        if self.comm is None:
            return

        @pl.when(step == 0)
        def _():
            self.comm.start(*self._cargs)

        @pl.when(step == (3 * total) // 4)
        def _():
            self.comm.mid(*self._cargs)

    def after(self, step, total):
        if self.comm is None:
            return

        @pl.when(step == total - 1)
        def _():
            self.comm.finish(*self._cargs)

    def results(self, outs):
        return outs[:self.n_out], outs[self.n_out:]


def _cast_bf16(x, name):
    r, c = x.shape
    tr = ROW_TILE if r % ROW_TILE == 0 else r

    def body(x_ref, o_ref):
        o_ref[...] = x_ref[...].astype(BF16)

    return pl.pallas_call(
        body, name=name, grid=(r // tr,),
        in_specs=[pl.BlockSpec((tr, c), lambda i: (i, 0))],
        out_specs=pl.BlockSpec((tr, c), lambda i: (i, 0)),
        out_shape=jax.ShapeDtypeStruct((r, c), BF16), compiler_params=_cp("parallel"))(x)


def _cast_bf16_own_slab(x, me_arr, name):
    r, c = x.shape
    tr = ROW_TILE if r % ROW_TILE == 0 else r

    def body(me_ref, x_ref, o_ref):
        o_ref[...] = x_ref[...].astype(BF16)

    return pl.pallas_call(
        body, name=name,
        grid_spec=pltpu.PrefetchScalarGridSpec(
            num_scalar_prefetch=1, grid=(r // tr,),
            in_specs=[pl.BlockSpec((tr, c), lambda i, me: (i, 0))],
            out_specs=pl.BlockSpec((None, tr, c), lambda i, me: (me[0], i, 0))),
        out_shape=jax.ShapeDtypeStruct((N_CHIPS, r, c), BF16), compiler_params=_cp("parallel"))(me_arr, x)


def _rms_fwd(x, g, name):
    s, d = x.shape

    def body(x_ref, g_ref, h_ref):
        xhat, _ = _rms_stats(x_ref[...])
        h_ref[...] = (xhat * g_ref[...]).astype(BF16)

    return pl.pallas_call(
        body, name=name, grid=(s // ROW_TILE,),
        in_specs=[pl.BlockSpec((ROW_TILE, d), lambda i: (i, 0)), pl.BlockSpec((1, d), lambda i: (0, 0))],
        out_specs=pl.BlockSpec((ROW_TILE, d), lambda i: (i, 0)),
        out_shape=jax.ShapeDtypeStruct((s, d), BF16), compiler_params=_cp("parallel"))(x, g)


def _mm_nn(a, w3, name, comm=None):
    m, k = a.shape
    nsh, _, ns = w3.shape
    tm = 512 if m % 512 == 0 else ROW_TILE
    tn = _pick_tile(ns, 1024)
    per = ns // tn
    grid = (nsh * per, m // tm)
    host = _Host(comm,
                 [pl.BlockSpec((tm, k), lambda n, i: (i, 0)), pl.BlockSpec((None, k, tn), lambda n, i: (n // per, 0, n % per))],
                 [pl.BlockSpec((tm, tn), lambda n, i: (i, n))], [jax.ShapeDtypeStruct((m, nsh * ns), F32)], [])

    def body(*refs):
        (a_ref, w_ref), (o_ref,), _ = host.split(refs)
        step = pl.program_id(0) * grid[1] + pl.program_id(1)
        host.before(step, grid[0] * grid[1])
        o_ref[...] = jnp.dot(a_ref[...], w_ref[...], preferred_element_type=F32)
        host.after(step, grid[0] * grid[1])

    outs = pl.pallas_call(
        body, name=name, grid=grid, in_specs=host.in_specs, out_specs=host.out_specs, out_shape=host.out_shape,
        scratch_shapes=host.scratch, input_output_aliases=host.aliases,
        compiler_params=_cp("arbitrary", "arbitrary"))(a, w3, *host.args)
    (out,), extra = host.results(outs)
    return out, extra


def _mm_nt(a, b, name):
    m, k = a.shape
    n = b.shape[0]
    tm = 512 if m % 512 == 0 else ROW_TILE

    def body(a_ref, b_ref, o_ref):
        o_ref[...] = lax.dot_general(a_ref[...], b_ref[...], (((1,), (1,)), ((), ())), preferred_element_type=F32)

    return pl.pallas_call(
        body, name=name, grid=(m // tm,),
        in_specs=[pl.BlockSpec((tm, k), lambda i: (i, 0)), pl.BlockSpec((n, k), lambda i: (0, 0))],
        out_specs=pl.BlockSpec((tm, n), lambda i: (i, 0)),
        out_shape=jax.ShapeDtypeStruct((m, n), F32), compiler_params=_cp("parallel"))(a, b)


def _mm_tn(a, b, nsh, name):
    s, m = a.shape
    n = b.shape[1]
    ns = n // nsh
    tm = 512 if m % 512 == 0 else ROW_TILE
    tn = _pick_tile(ns, 1024)
    per = ns // tn

    def body(a_ref, b_ref, o_ref):
        o_ref[...] = lax.dot_general(a_ref[...], b_ref[...], (((0,), (0,)), ((), ())),
                                     preferred_element_type=F32).astype(BF16)

    return pl.pallas_call(
        body, name=name, grid=(nsh * per, m // tm),
        in_specs=[pl.BlockSpec((s, tm), lambda j, i: (0, i)), pl.BlockSpec((s, tn), lambda j, i: (0, j))],
        out_specs=pl.BlockSpec((None, tm, tn), lambda j, i: (j // per, i, j % per)),
        out_shape=jax.ShapeDtypeStruct((nsh, m, ns), BF16), compiler_params=_cp("parallel", "parallel"))(a, b)


def _tri(n, rel):
    row = lax.broadcasted_iota(jnp.int32, (2 * n, n), 0)
    col = lax.broadcasted_iota(jnp.int32, (2 * n, n), 1)
    return jnp.where(rel(jnp.where(row >= n, row - n, row), col), 1.0, 0.0).astype(BF16)


def _dot_split(x, tri2):
    hi = x.astype(BF16)
    lo = (x - hi.astype(F32)).astype(BF16)
    return jnp.dot(jnp.concatenate([hi, lo], axis=1), tri2, preferred_element_type=F32)


def _nt(a, b):
    return lax.dot_general(a, b, (((1,), (1,)), ((), ())), preferred_element_type=F32)


def _tn(a, b):
    return lax.dot_general(a, b, (((0,), (0,)), ((), ())), preferred_element_type=F32)


def _heads_per_step(nh):
    return max(h for h in (1, 2, 4) if nh % h == 0)


def _sba_fwd(p, sbw, name, comm=None):
    s = p.shape[0]
    nh = sbw // HEAD_DIM
    hp = _heads_per_step(nh)
    ngrp, hw = nh // hp, hp * HEAD_DIM
    blk = ATT_BLOCK
    nq = s // blk
    scale = 1.0 / math.sqrt(HEAD_DIM)
    host = _Host(comm,
                 [pl.BlockSpec((blk, hw), lambda g, i: (i, g)),
                  pl.BlockSpec((s, hw), lambda g, i: (0, ngrp + g)),
                  pl.BlockSpec((s, hw), lambda g, i: (0, 2 * ngrp + g))],
                 [pl.BlockSpec((blk, hw), lambda g, i: (i, g))] * 2,
                 [jax.ShapeDtypeStruct((s, sbw), F32)] * 2,
                 [pltpu.VMEM((s, hw), BF16)] * 2)

    def body(*refs):
        (q_ref, k_ref, v_ref), (o_ref, lt_ref), (kb_ref, vb_ref) = host.split(refs)
        i = pl.program_id(1)
        step = pl.program_id(0) * nq + i
        host.before(step, ngrp * nq)

        @pl.when(i == 0)
        def _():
            kb_ref[...] = k_ref[...].astype(BF16)
            vb_ref[...] = v_ref[...].astype(BF16)

        heads = [slice(h * HEAD_DIM, (h + 1) * HEAD_DIM) for h in range(hp)]
        qs = [q_ref[:, hd].astype(BF16) for hd in heads]
        later = _tri(blk, lambda r, c: r > c)
        causal = lax.broadcasted_iota(jnp.int32, (blk, blk), 1) < lax.broadcasted_iota(jnp.int32, (blk, blk), 0)

        def key_block(j, carry, diagonal):
            rows = pl.ds(pl.multiple_of(j * blk, blk), blk)
            hs = range(hp)
            z = [_nt(qs[h], kb_ref[rows, heads[h]]) * scale for h in hs]
            ls = [_log_sigmoid(z[h]) for h in hs]
            lm = [jnp.where(causal, ls[h] - z[h], 0.0) if diagonal else ls[h] - z[h] for h in hs]
            stay = [_dot_split(lm[h], later) for h in hs]
            w = [jnp.exp(ls[h] + stay[h] + carry[h][1]) for h in hs]
            if diagonal:
                w = [jnp.where(causal, w[h], 0.0) for h in hs]
            acc = [carry[h][0] + jnp.dot(w[h].astype(BF16), vb_ref[rows, heads[h]], preferred_element_type=F32) for h in hs]
            return tuple((acc[h], carry[h][1] + jnp.sum(lm[h], axis=1, keepdims=True)) for h in hs)

        init = tuple((jnp.zeros((blk, HEAD_DIM), F32), jnp.zeros((blk, 1), F32)) for _ in heads)
        carry = key_block(i, init, True)
        carry = lax.fori_loop(0, i, lambda n, c: key_block(i - 1 - n, c, False), carry)
        for h, hd in enumerate(heads):
            o_ref[:, hd] = carry[h][0]
            lt_ref[:, hd] = jnp.broadcast_to(carry[h][1], (blk, HEAD_DIM))
        host.after(step, ngrp * nq)

    outs = pl.pallas_call(
        body, name=name, grid=(ngrp, nq), in_specs=host.in_specs, out_specs=host.out_specs, out_shape=host.out_shape,
        scratch_shapes=host.scratch, input_output_aliases=host.aliases,
        compiler_params=_cp("arbitrary", "arbitrary"))(p, p, p, *host.args)
    (out, ltot), extra = host.results(outs)
    return out, ltot, extra


def _sba_bwd(p, ltot, dout, sbw, name, comm=None):
    s = p.shape[0]
    nh = sbw // HEAD_DIM
    hp = _heads_per_step(nh)
    ngrp, hw = nh // hp, hp * HEAD_DIM
    blk = ATT_BLOCK
    nq = s // blk
    scale = 1.0 / math.sqrt(HEAD_DIM)
    blk_spec = pl.BlockSpec((blk, hw), lambda g, i: (i, g))
    col_spec = pl.BlockSpec((s, hw), lambda g, i: (0, g))
    host = _Host(comm,
                 [blk_spec, pl.BlockSpec((s, hw), lambda g, i: (0, ngrp + g)),
                  pl.BlockSpec((s, hw), lambda g, i: (0, 2 * ngrp + g)), blk_spec, blk_spec],
                 [blk_spec, col_spec, col_spec], [jax.ShapeDtypeStruct((s, sbw), BF16)] * 3,
                 [pltpu.VMEM((s, hw), BF16)] * 2 + [pltpu.VMEM((s, hw), F32)] * 2)

    def body(*refs):
        (q_ref, k_ref, v_ref, lt_ref, do_ref), (dq_ref, dk_ref, dv_ref), (kb_ref, vb_ref, dka_ref, dva_ref) = host.split(refs)
        i = pl.program_id(1)
        step = pl.program_id(0) * nq + i
        host.before(step, ngrp * nq)

        @pl.when(i == 0)
        def _():
            kb_ref[...] = k_ref[...].astype(BF16)
            vb_ref[...] = v_ref[...].astype(BF16)
            dka_ref[...] = jnp.zeros_like(dka_ref)
            dva_ref[...] = jnp.zeros_like(dva_ref)

        heads = [slice(h * HEAD_DIM, (h + 1) * HEAD_DIM) for h in range(hp)]
        qs = [q_ref[:, hd].astype(BF16) for hd in heads]
        dos = [do_ref[:, hd].astype(BF16) for hd in heads]
        ltots = [lt_ref[:, h * HEAD_DIM:h * HEAD_DIM + 1] for h in range(hp)]
        upto = _tri(blk, lambda r, c: r <= c)
        before = _tri(blk, lambda r, c: r < c)
        causal = lax.broadcasted_iota(jnp.int32, (blk, blk), 1) < lax.broadcasted_iota(jnp.int32, (blk, blk), 0)

        def key_block(j, carry, diagonal):
            rows = pl.ds(pl.multiple_of(j * blk, blk), blk)
            hs = range(hp)
            kj = [kb_ref[rows, heads[h]] for h in hs]
            vj = [vb_ref[rows, heads[h]] for h in hs]
            z = [_nt(qs[h], kj[h]) * scale for h in hs]
            dw = [_nt(dos[h], vj[h]) for h in hs]
            ls = [_log_sigmoid(z[h]) for h in hs]
            lm = [jnp.where(causal, ls[h] - z[h], 0.0) if diagonal else ls[h] - z[h] for h in hs]
            stay = [ltots[h] - carry[h][1] - _dot_split(lm[h], upto) for h in hs]
            w = [jnp.exp(ls[h] + stay[h]) for h in hs]
            if diagonal:
                w = [jnp.where(causal, w[h], 0.0) for h in hs]
            da = [dw[h] * w[h] for h in hs]
            sig = [jnp.exp(ls[h]) for h in hs]
            chain = [sig[h] * (carry[h][2] + _dot_split(da[h], before)) for h in hs]
            if diagonal:
                chain = [jnp.where(causal, chain[h], 0.0) for h in hs]
            dzb = [((da[h] * (1.0 - sig[h]) - chain[h]) * scale).astype(BF16) for h in hs]
            dq = [carry[h][0] + jnp.dot(dzb[h], kj[h], preferred_element_type=F32) for h in hs]
            for h in hs:
                dka_ref[rows, heads[h]] += _tn(dzb[h], qs[h])
            for h in hs:
                dva_ref[rows, heads[h]] += _tn(w[h].astype(BF16), dos[h])
            return tuple((dq[h], carry[h][1] + jnp.sum(lm[h], axis=1, keepdims=True),
                          carry[h][2] + jnp.sum(da[h], axis=1, keepdims=True)) for h in hs)

        zero = jnp.zeros((blk, 1), F32)
        init = tuple((jnp.zeros((blk, HEAD_DIM), F32), zero, zero) for _ in heads)
        carry = lax.fori_loop(0, i, lambda j, c: key_block(j, c, False), init)
        carry = key_block(i, carry, True)
        for h, hd in enumerate(heads):
            dq_ref[:, hd] = carry[h][0].astype(BF16)

        @pl.when(i == nq - 1)
        def _():
            dk_ref[...] = dka_ref[...].astype(BF16)
            dv_ref[...] = dva_ref[...].astype(BF16)

        host.after(step, ngrp * nq)

    outs = pl.pallas_call(
        body, name=name, grid=(ngrp, nq), in_specs=host.in_specs, out_specs=host.out_specs, out_shape=host.out_shape,
        scratch_shapes=host.scratch, input_output_aliases=host.aliases,
        compiler_params=_cp("arbitrary", "arbitrary"))(p, p, p, ltot, dout, *host.args)
    (dq, dk, dv), extra = host.results(outs)
    return dq, dk, dv, extra


def _pool_groups(pad_ref, tile, row0, gd, halo):
    row = row0 + lax.broadcasted_iota(jnp.int32, (tile, 1), 0)
    out = []
    for gi, win in enumerate(POOL_WINDOWS):
        cs = slice(gi * gd, (gi + 1) * gd)
        tok = pad_ref[halo:halo + tile, cs]
        acc = tok
        for j in range(1, win):
            acc = acc + pad_ref[halo - j:halo - j + tile, cs]
        cnt = jnp.minimum(win, row + 1).astype(F32)
        out.append(acc / cnt - tok)
    return out


def _even_mix_fwd(p, att, pool_w, pool_scale, d, name):
    s = p.shape[0]
    half = d // 2
    gd = half // len(POOL_WINDOWS)
    t, hb = ROW_TILE, POOL_HALO

    def body(u_ref, uh_ref, g_ref, a_ref, pw_ref, sc_ref, y_ref, pad_ref):
        i = pl.program_id(0)
        pad_ref[0:hb, :] = jnp.where(i > 0, uh_ref[...], 0.0)
        pad_ref[hb:, :] = u_ref[...]
        pooled = _pool_groups(pad_ref, t, i * t, gd, hb)
        for gi in range(len(POOL_WINDOWS)):
            cs = slice(gi * gd, (gi + 1) * gd)
            po = jnp.dot(pooled[gi].astype(BF16), pw_ref[gi], preferred_element_type=F32) * sc_ref[:, cs]
            y_ref[:, half + gi * gd:half + (gi + 1) * gd] = (po * _silu(g_ref[:, half + gi * gd:half + (gi + 1) * gd])).astype(BF16)
        y_ref[:, :half] = (a_ref[...] * _silu(g_ref[:, :half])).astype(BF16)

    return pl.pallas_call(
        body, name=name, grid=(s // t,),
        in_specs=[pl.BlockSpec((t, half), lambda i: (i, 3)),
                  pl.BlockSpec((hb, half), lambda i: (jnp.maximum(i * (t // hb) - 1, 0), 3)),
                  pl.BlockSpec((t, d), lambda i: (i, 2)),
                  pl.BlockSpec((t, half), lambda i: (i, 0)),
                  pl.BlockSpec(pool_w.shape, lambda i: (0, 0, 0)),
                  pl.BlockSpec((1, half), lambda i: (0, 0))],
        out_specs=pl.BlockSpec((t, d), lambda i: (i, 0)),
        out_shape=jax.ShapeDtypeStruct((s, d), BF16),
        scratch_shapes=[pltpu.VMEM((hb + t, half), F32)],
        compiler_params=_cp("parallel"))(p, p, p, att, pool_w, pool_scale)


def _even_mix_bwd(p, att, dy, pool_w, pool_scale, d, name):
    s = p.shape[0]
    half = d // 2
    ng = len(POOL_WINDOWS)
    gd = half // ng
    t, hb = ROW_TILE, POOL_HALO
    nt = s // t

    def body(u_ref, uh_ref, g_ref, gh_ref, a_ref, dy_ref, dyh_ref, pw_ref, sc_ref,
             da_ref, du_ref, dg_ref, dsc_ref, dpw_ref, pad_ref, dn_ref):
        i = pl.program_id(0)
        first = i == 0
        pad_ref[0:hb, :] = jnp.where(i > 0, uh_ref[...], 0.0)
        pad_ref[hb:, :] = u_ref[...]
        pooled = _pool_groups(pad_ref, t, i * t, gd, hb)
        g1 = g_ref[:, :half]
        dy1 = dy_ref[:, :half]
        da_ref[...] = dy1 * _silu(g1)
        dg_ref[:, :half] = (dy1 * a_ref[...] * _dsilu(g1)).astype(BF16)
        row = i * t + lax.broadcasted_iota(jnp.int32, (t + hb, 1), 0)
        for gi, win in enumerate(POOL_WINDOWS):
            cs = slice(gi * gd, (gi + 1) * gd)
            cs2 = slice(half + gi * gd, half + (gi + 1) * gd)
            w = pw_ref[gi]
            pb = pooled[gi].astype(BF16)
            zp = jnp.dot(pb, w, preferred_element_type=F32)
            g2 = g_ref[:, cs2]
            dy2 = dy_ref[:, cs2]
            dg_ref[:, cs2] = (dy2 * zp * sc_ref[:, cs] * _dsilu(g2)).astype(BF16)
            dpo = dy2 * _silu(g2)
            _acc_rows(dsc_ref.at[:, cs], first, jnp.sum(dpo * zp, axis=0, keepdims=True))
            dz = (dpo * sc_ref[:, cs]).astype(BF16)
            _acc_rows(dpw_ref.at[gi], first, _tn(pb, dz))
            dzh = jnp.where(i < nt - 1, dyh_ref[:, cs] * _silu(gh_ref[:, cs]) * sc_ref[:, cs], 0.0).astype(BF16)
            dpool = _nt(dz, w)
            dpool_h = _nt(dzh, w)
            cnt = jnp.minimum(win, row + 1).astype(F32)
            dn_ref[0:t, cs] = dpool / cnt[0:t]
            dn_ref[t:, cs] = dpool_h / cnt[t:]
            acc = dn_ref[0:t, cs]
            for j in range(1, win):
                acc = acc + dn_ref[j:j + t, cs]
            du_ref[:, cs] = (acc - dpool).astype(BF16)

    return pl.pallas_call(
        body, name=name, grid=(nt,),
        in_specs=[pl.BlockSpec((t, half), lambda i: (i, 3)),
                  pl.BlockSpec((hb, half), lambda i: (jnp.maximum(i * (t // hb) - 1, 0), 3)),
                  pl.BlockSpec((t, d), lambda i: (i, 2)),
                  pl.BlockSpec((hb, half), lambda i: (jnp.minimum((i + 1) * (t // hb), s // hb - 1), 5)),
                  pl.BlockSpec((t, half), lambda i: (i, 0)),
                  pl.BlockSpec((t, d), lambda i: (i, 0)),
                  pl.BlockSpec((hb, half), lambda i: (jnp.minimum((i + 1) * (t // hb), s // hb - 1), 1)),
                  pl.BlockSpec(pool_w.shape, lambda i: (0, 0, 0)),
                  pl.BlockSpec((1, half), lambda i: (0, 0))],
        out_specs=[pl.BlockSpec((t, half), lambda i: (i, 0)),
                   pl.BlockSpec((t, half), lambda i: (i, 0)),
                   pl.BlockSpec((t, d), lambda i: (i, 0)),
                   pl.BlockSpec((1, half), lambda i: (0, 0)),
                   pl.BlockSpec((ng, gd, gd), lambda i: (0, 0, 0))],
        out_shape=[jax.ShapeDtypeStruct((s, half), F32), jax.ShapeDtypeStruct((s, half), BF16),
                   jax.ShapeDtypeStruct((s, d), BF16), jax.ShapeDtypeStruct((1, half), F32),
                   jax.ShapeDtypeStruct((ng, gd, gd), F32)],
        scratch_shapes=[pltpu.VMEM((hb + t, half), F32), pltpu.VMEM((t + hb, half), F32)],
        compiler_params=_cp("arbitrary"))(p, p, p, p, att, dy, dy, pool_w, pool_scale)


def _mm_out_even(y, w, x, g_post, g_pre_next, name):
    s, k = y.shape
    d = w.shape[1]
    t = ROW_TILE

    def body(y_ref, w_ref, x_ref, gp_ref, gn_ref, o_ref, x1_ref, h1_ref):
        o = jnp.dot(y_ref[...], w_ref[...], preferred_element_type=F32)
        o_ref[...] = o
        ohat, _ = _rms_stats(o)
        x1 = x_ref[...] + ohat * gp_ref[...]
        x1_ref[...] = x1
        xhat, _ = _rms_stats(x1)
        h1_ref[...] = (xhat * gn_ref[...]).astype(BF16)

    row = lambda c: pl.BlockSpec((t, c), lambda i: (i, 0))
    vec = pl.BlockSpec((1, d), lambda i: (0, 0))
    return pl.pallas_call(
        body, name=name, grid=(s // t,),
        in_specs=[row(k), pl.BlockSpec((k, d), lambda i: (0, 0)), row(d), vec, vec],
        out_specs=[row(d), row(d), row(d)],
        out_shape=[jax.ShapeDtypeStruct((s, d), F32), jax.ShapeDtypeStruct((s, d), F32),
                   jax.ShapeDtypeStruct((s, d), BF16)],
        compiler_params=_cp("parallel"))(y, w, x, g_post, g_pre_next)


def _mm_out_odd(y, w, x1, g_post, target, name):
    s, k = y.shape
    d = w.shape[1]
    t = ROW_TILE

    def body(y_ref, w_ref, x_ref, gp_ref, tg_ref, do_ref, dx_ref, loss_ref, dgp_ref):
        first = pl.program_id(0) == 0
        o = jnp.dot(y_ref[...], w_ref[...], preferred_element_type=F32)
        ohat, r = _rms_stats(o)
        gp = gp_ref[...]
        diff = x_ref[...] + ohat * gp - tg_ref[...]
        part = 0.5 * jnp.sum(jnp.mean(diff * diff, axis=-1, keepdims=True), axis=0, keepdims=True)
        _acc_rows(loss_ref, first, jnp.broadcast_to(part, loss_ref.shape))
        dx2 = diff * (1.0 / d)
        dx_ref[...] = dx2
        do, dgp = _rms_bwd(dx2, ohat, r, gp)
        do_ref[...] = do.astype(BF16)
        _acc_rows(dgp_ref, first, dgp)

    row = lambda c: pl.BlockSpec((t, c), lambda i: (i, 0))
    vec = pl.BlockSpec((1, d), lambda i: (0, 0))
    return pl.pallas_call(
        body, name=name, grid=(s // t,),
        in_specs=[row(k), pl.BlockSpec((k, d), lambda i: (0, 0)), row(d), vec, row(d)],
        out_specs=[row(d), row(d), pl.BlockSpec((8, LANES), lambda i: (0, 0)), vec],
        out_shape=[jax.ShapeDtypeStruct((s, d), BF16), jax.ShapeDtypeStruct((s, d), F32),
                   jax.ShapeDtypeStruct((8, LANES), F32), jax.ShapeDtypeStruct((1, d), F32)],
        compiler_params=_cp("arbitrary"))(y, w, x1, g_post, target)


def _layer_norm(d1, cg, cb):
    mu = jnp.mean(d1, axis=-1, keepdims=True)
    cen = d1 - mu
    rstd = lax.rsqrt(jnp.mean(cen * cen, axis=-1, keepdims=True) + EPS)
    n = cen * rstd
    return n, rstd, n * cg + cb


def _odd_mix_fwd(p, sconv_w, dconv_w, dconv_b, cnorm_g, cnorm_b, d, name):
    s = p.shape[0]
    w = d // 2
    k3, k31 = sconv_w.shape[0], dconv_w.shape[0]
    t, hb = ROW_TILE, CONV_HALO
    assert hb >= k31 - 1 and w % LANES == 0

    def body(p_ref, ph_ref, w3_ref, w31_ref, b31_ref, cg_ref, cb_ref, y_ref, s3_ref, d1_ref, mpad, dpad):
        i = pl.program_id(0)
        mpad[0:hb, :] = jnp.where(i > 0, ph_ref[:, 2 * w:3 * w] * ph_ref[:, 0:w], 0.0)
        mpad[hb:, :] = p_ref[:, 2 * w:3 * w] * p_ref[:, 0:w]
        dpad[0:hb, :] = jnp.where(i > 0, ph_ref[:, 3 * w:4 * w] * _sigmoid(ph_ref[:, 4 * w:5 * w]), 0.0)
        dpad[hb:, :] = p_ref[:, 3 * w:4 * w] * _sigmoid(p_ref[:, 4 * w:5 * w])
        for c0 in range(0, w, LANES):
            cs = slice(c0, c0 + LANES)
            acc = jnp.zeros((t, LANES), F32)
            for kk in range(k3):
                acc = acc + w3_ref[kk:kk + 1, cs] * mpad[hb - (k3 - 1) + kk:hb - (k3 - 1) + kk + t, cs]
            s3_ref[:, cs] = acc
            acc = jnp.zeros((t, LANES), F32)
            for kk in range(k31):
                acc = acc + w31_ref[kk:kk + 1, cs] * dpad[hb - (k31 - 1) + kk:hb - (k31 - 1) + kk + t, cs]
            d1_ref[:, cs] = acc + b31_ref[:, cs]
        _, _, d2 = _layer_norm(d1_ref[...], cg_ref[...], cb_ref[...])
        y_ref[:, :w] = (p_ref[:, w:2 * w] * s3_ref[...] * _silu(p_ref[:, 5 * w:6 * w])).astype(BF16)
        y_ref[:, w:] = (_silu(d2) * _silu(p_ref[:, 6 * w:7 * w])).astype(BF16)

    row = lambda c: pl.BlockSpec((t, c), lambda i: (i, 0))
    full = lambda a: pl.BlockSpec(a.shape, lambda i: (0, 0))
    return pl.pallas_call(
        body, name=name, grid=(s // t,),
        in_specs=[row(7 * w),
                  pl.BlockSpec((hb, 5 * w), lambda i: (jnp.maximum(i * (t // hb) - 1, 0), 0)),
                  full(sconv_w), full(dconv_w), full(dconv_b), full(cnorm_g), full(cnorm_b)],
        out_specs=[row(d), row(w), row(w)],
        out_shape=[jax.ShapeDtypeStruct((s, d), BF16), jax.ShapeDtypeStruct((s, w), F32),
                   jax.ShapeDtypeStruct((s, w), F32)],
        scratch_shapes=[pltpu.VMEM((hb + t, w), F32)] * 2,
        compiler_params=_cp("parallel"))(p, p, sconv_w, dconv_w, dconv_b, cnorm_g, cnorm_b)


def _odd_bwd_rows(p, s3, d1, dy, cnorm_g, cnorm_b, d, name):
    s = p.shape[0]
    w = d // 2
    t = ROW_TILE

    def body(bc_ref, g1_ref, g2_ref, s3_ref, d1_ref, dy_ref, cg_ref, cb_ref,
             dbc_ref, dg_ref, ds3_ref, dd1_ref, dcg_ref, dcb_ref, db_ref):
        first = pl.program_id(0) == 0
        g1, g2 = g1_ref[...], g2_ref[...]
        bc, s3v = bc_ref[...], s3_ref[...]
        dy1, dy2 = dy_ref[:, :w], dy_ref[:, w:]
        n, rstd, d2 = _layer_norm(d1_ref[...], cg_ref[...], cb_ref[...])
        dg_ref[:, :w] = (dy1 * bc * s3v * _dsilu(g1)).astype(BF16)
        dg_ref[:, w:] = (dy2 * _silu(d2) * _dsilu(g2)).astype(BF16)
        dco = dy1 * _silu(g1)
        dbc_ref[...] = (dco * s3v).astype(BF16)
        ds3_ref[...] = dco * bc
        dd2 = dy2 * _silu(g2) * _dsilu(d2)
        _acc_rows(dcb_ref, first, jnp.sum(dd2, axis=0, keepdims=True))
        _acc_rows(dcg_ref, first, jnp.sum(dd2 * n, axis=0, keepdims=True))
        dn = dd2 * cg_ref[...]
        dd1 = rstd * (dn - jnp.mean(dn, axis=-1, keepdims=True) - n * jnp.mean(dn * n, axis=-1, keepdims=True))
        dd1_ref[...] = dd1
        _acc_rows(db_ref, first, jnp.sum(dd1, axis=0, keepdims=True))

    col = lambda j: pl.BlockSpec((t, w), lambda i: (i, j))
    row = lambda c: pl.BlockSpec((t, c), lambda i: (i, 0))
    vec = pl.BlockSpec((1, w), lambda i: (0, 0))
    return pl.pallas_call(
        body, name=name, grid=(s // t,),
        in_specs=[col(1), col(5), col(6), row(w), row(w), row(d), vec, vec],
        out_specs=[row(w), row(d), row(w), row(w), vec, vec, vec],
        out_shape=[jax.ShapeDtypeStruct((s, w), BF16), jax.ShapeDtypeStruct((s, d), BF16),
                   jax.ShapeDtypeStruct((s, w), F32), jax.ShapeDtypeStruct((s, w), F32)]
        + [jax.ShapeDtypeStruct((1, w), F32)] * 3,
        compiler_params=_cp("arbitrary"))(p, p, p, s3, d1, dy, cnorm_g, cnorm_b)


def _odd_bwd_conv(p, ds3, dd1, sconv_w, dconv_w, d, name):
    s = p.shape[0]
    w = d // 2
    k3, k31 = sconv_w.shape[0], dconv_w.shape[0]
    t, hb, ha = ROW_TILE, CONV_HALO, 8
    nt = s // t
    assert hb >= k31 - 1 and ha >= k3 - 1 and k3 <= 8 and k31 <= 32

    def body(hc_ref, cc_ref, ga_ref, gb_ref, hch_ref, cch_ref, gah_ref, gbh_ref, ds3_ref, ds3h_ref, dd1_ref, dd1h_ref,
             w3_ref, w31_ref, dhc_ref, dcc_ref, dga_ref, dgb_ref, dw3_ref, dw31_ref, mpad, dpad, s3pad, d1pad):
        i = pl.program_id(0)
        first = i == 0
        last = i == nt - 1
        mpad[0:hb, :] = jnp.where(i > 0, cch_ref[...] * hch_ref[...], 0.0)
        mpad[hb:, :] = cc_ref[...] * hc_ref[...]
        dpad[0:hb, :] = jnp.where(i > 0, gah_ref[...] * _sigmoid(gbh_ref[...]), 0.0)
        dpad[hb:, :] = ga_ref[...] * _sigmoid(gb_ref[...])
        s3pad[0:t, :] = ds3_ref[...]
        s3pad[t:, :] = jnp.where(last, 0.0, ds3h_ref[...])
        d1pad[0:t, :] = dd1_ref[...]
        d1pad[t:, :] = jnp.where(last, 0.0, dd1h_ref[...])

        @pl.when(first)
        def _():
            dw3_ref[...] = jnp.zeros_like(dw3_ref)
            dw31_ref[...] = jnp.zeros_like(dw31_ref)

        for c0 in range(0, w, LANES):
            cs = slice(c0, c0 + LANES)
            ds3v = s3pad[0:t, cs]
            dd1v = d1pad[0:t, cs]
            dm = jnp.zeros((t, LANES), F32)
            for kk in range(k3):
                dm = dm + w3_ref[kk:kk + 1, cs] * s3pad[k3 - 1 - kk:k3 - 1 - kk + t, cs]
                off = hb - (k3 - 1) + kk
                dw3_ref[kk:kk + 1, cs] += jnp.sum(ds3v * mpad[off:off + t, cs], axis=0, keepdims=True)
            dd0 = jnp.zeros((t, LANES), F32)
            for kk in range(k31):
                dd0 = dd0 + w31_ref[kk:kk + 1, cs] * d1pad[k31 - 1 - kk:k31 - 1 - kk + t, cs]
                off = hb - (k31 - 1) + kk
                dw31_ref[kk:kk + 1, cs] += jnp.sum(dd1v * dpad[off:off + t, cs], axis=0, keepdims=True)
            dcc_ref[:, cs] = (dm * hc_ref[:, cs]).astype(BF16)
            dhc_ref[:, cs] = (dm * cc_ref[:, cs]).astype(BF16)
            sgb = _sigmoid(gb_ref[:, cs])
            dga_ref[:, cs] = (dd0 * sgb).astype(BF16)
            dgb_ref[:, cs] = (dd0 * ga_ref[:, cs] * sgb * (1.0 - sgb)).astype(BF16)

    col = lambda j: pl.BlockSpec((t, w), lambda i: (i, j))
    pre = lambda j: pl.BlockSpec((hb, w), lambda i: (jnp.maximum(i * (t // hb) - 1, 0), j))
    row = pl.BlockSpec((t, w), lambda i: (i, 0))
    post = lambda h: pl.BlockSpec((h, w), lambda i: (jnp.minimum((i + 1) * (t // h), s // h - 1), 0))
    full = lambda a: pl.BlockSpec(a.shape, lambda i: (0, 0))
    return pl.pallas_call(
        body, name=name, grid=(nt,),
        in_specs=[col(0), col(2), col(3), col(4), pre(0), pre(2), pre(3), pre(4),
                  row, post(ha), row, post(hb), full(sconv_w), full(dconv_w)],
        out_specs=[row, row, row, row, pl.BlockSpec((8, w), lambda i: (0, 0)), pl.BlockSpec((32, w), lambda i: (0, 0))],
        out_shape=[jax.ShapeDtypeStruct((s, w), BF16)] * 4
        + [jax.ShapeDtypeStruct((8, w), F32), jax.ShapeDtypeStruct((32, w), F32)],
        scratch_shapes=[pltpu.VMEM((hb + t, w), F32)] * 2 + [pltpu.VMEM((t + ha, w), F32), pltpu.VMEM((t + hb, w), F32)],
        compiler_params=_cp("arbitrary"))(p, p, p, p, p, p, p, p, ds3, ds3, dd1, dd1, sconv_w, dconv_w)


def _mm_in_bwd(dp, w3, x, g_pre, dres, post, name, comm=None):
    s = dp.shape[0]
    nsh, d, ns = w3.shape
    t = ROW_TILE
    nt = s // t
    row = pl.BlockSpec((t, d), lambda i, k: (i, 0))
    vec = pl.BlockSpec((1, d), lambda i, k: (0, 0))
    in_specs = [pl.BlockSpec((t, ns), lambda i, k: (i, k)), pl.BlockSpec((None, d, ns), lambda i, k: (k, 0, 0)), row, vec, row]
    out_specs = [row, vec]
    out_shape = [jax.ShapeDtypeStruct((s, d), F32), jax.ShapeDtypeStruct((1, d), F32)]
    args = [dp, w3, x, g_pre, dres]
    if post is not None:
        in_specs += [row, vec]
        out_specs += [row, vec]
        out_shape += [jax.ShapeDtypeStruct((s, d), BF16), jax.ShapeDtypeStruct((1, d), F32)]
        args += list(post)
    host = _Host(comm, in_specs, out_specs, out_shape, [pltpu.VMEM((t, d), F32)])

    def body(*refs):
        ins, outs, (acc_ref,) = host.split(refs)
        dp_ref, w_ref, x_ref, g_ref, dr_ref = ins[:5]
        dx_ref, dg_ref = outs[:2]
        kk = pl.program_id(1)
        first = pl.program_id(0) == 0
        step = pl.program_id(0) * nsh + kk
        host.before(step, nt * nsh)
        part = _nt(dp_ref[...], w_ref[...])

        @pl.when(kk == 0)
        def _():
            acc_ref[...] = part

        @pl.when(kk > 0)
        def _():
            acc_ref[...] += part

        @pl.when(kk == nsh - 1)
        def _():
            xhat, r = _rms_stats(x_ref[...])
            dxn, dg = _rms_bwd(acc_ref[...], xhat, r, g_ref[...])
            dx = dr_ref[...] + dxn
            dx_ref[...] = dx
            _acc_rows(dg_ref, first, dg)
            if post is not None:
                ohat, ro = _rms_stats(ins[5][...])
                do, dgp = _rms_bwd(dx, ohat, ro, ins[6][...])
                outs[2][...] = do.astype(BF16)
                _acc_rows(outs[3], first, dgp)

        host.after(step, nt * nsh)

    res = pl.pallas_call(
        body, name=name, grid=(nt, nsh), in_specs=host.in_specs, out_specs=host.out_specs, out_shape=host.out_shape,
        scratch_shapes=host.scratch, input_output_aliases=host.aliases,
        compiler_params=_cp("arbitrary", "arbitrary"))(*args, *host.args)
    return host.results(res)


def _half_add(g, r1, c_arr, name):
    nsh, rows, ns = g.shape
    h = rows // 2
    tr = min(ROW_TILE, h)
    per = h // tr

    def body(c_ref, g_ref, r_ref, o_ref):
        o_ref[...] = (g_ref[...].astype(F32) + r_ref[...].astype(F32)).astype(BF16)

    spec = pl.BlockSpec((None, tr, ns), lambda s, r, c: (s, r, 0))
    return pl.pallas_call(
        body, name=name,
        grid_spec=pltpu.PrefetchScalarGridSpec(
            num_scalar_prefetch=1, grid=(nsh, per),
            in_specs=[pl.BlockSpec((None, tr, ns), lambda s, r, c: (s, c[0] * per + r, 0)), spec], out_specs=spec),
        out_shape=jax.ShapeDtypeStruct((nsh, h, ns), BF16), compiler_params=_cp("parallel", "parallel"))(c_arr, g, r1)


def _sum_chips(hh, r2, mc_arr, name):
    _, h, ns = hh.shape
    tr = min(ROW_TILE, h)
    per = h // tr

    def body(mc_ref, h_ref, a_ref, b_ref, c_ref, o_ref):
        o_ref[...] = ((h_ref[...].astype(F32) + a_ref[...].astype(F32)) + b_ref[...].astype(F32)) + c_ref[...].astype(F32)

    got = lambda k: pl.BlockSpec((None, tr, ns), lambda r, mc: (k, r, 0))
    return pl.pallas_call(
        body, name=name,
        grid_spec=pltpu.PrefetchScalarGridSpec(
            num_scalar_prefetch=1, grid=(per,),
            in_specs=[pl.BlockSpec((None, tr, ns), lambda r, mc: (mc[0], r, 0)), got(0), got(1), got(2)],
            out_specs=pl.BlockSpec((tr, ns), lambda r, mc: (mc[1] * per + r, 0))),
        out_shape=jax.ShapeDtypeStruct((2 * h, ns), F32), compiler_params=_cp("parallel"))(mc_arr, hh, r2, r2, r2)


def _add2(a, b, name):
    def body(a_ref, b_ref, o_ref):
        o_ref[...] = a_ref[...] + b_ref[...]

    return pl.pallas_call(body, name=name, out_shape=jax.ShapeDtypeStruct(a.shape, a.dtype), compiler_params=_cp())(a, b)


def _sum_chips_ordered(s2, r2, mc_arr, name):
    rows, w = s2.shape
    rh = rows // 2

    def body(mc_ref, s_ref, a_ref, b_ref, c_ref, o_ref):
        me = mc_ref[0]
        acc = None
        for j in range(N_CHIPS):
            rel = jnp.bitwise_xor(me, j)
            v = jnp.where(rel == 0, s_ref[...], jnp.where(rel == 2, a_ref[...], jnp.where(rel == 1, b_ref[...], c_ref[...])))
            acc = v if acc is None else acc + v
        o_ref[...] = acc

    got = lambda k: pl.BlockSpec((None, rh, w), lambda i, mc: (k, 0, 0))
    return pl.pallas_call(
        body, name=name,
        grid_spec=pltpu.PrefetchScalarGridSpec(
            num_scalar_prefetch=1, grid=(1,),
            in_specs=[pl.BlockSpec((rh, w), lambda i, mc: (mc[1], 0)), got(0), got(1), got(2)],
            out_specs=pl.BlockSpec((rh, w), lambda i, mc: (mc[1], 0))),
        out_shape=jax.ShapeDtypeStruct((rows, w), F32), compiler_params=_cp("arbitrary"))(mc_arr, s2, r2, r2, r2)


def _adamw(w, g, m, v, name):
    r, c = w.shape
    tr = ROW_TILE if r % ROW_TILE == 0 else r
    c1 = 1.0 / (1.0 - ADAM_B1 ** ADAM_STEP)
    c2 = 1.0 / (1.0 - ADAM_B2 ** ADAM_STEP)

    def body(w_ref, g_ref, m_ref, v_ref, go_ref, d_ref, nm_ref, nv_ref):
        gv = g_ref[...]
        go_ref[...] = gv
        nm = ADAM_B1 * m_ref[...] + (1.0 - ADAM_B1) * gv
        nv = ADAM_B2 * v_ref[...] + (1.0 - ADAM_B2) * (gv * gv)
        nm_ref[...] = nm
        nv_ref[...] = nv
        d_ref[...] = -ADAM_LR * ((nm * c1) / (jnp.sqrt(nv * c2) + ADAM_EPS) + ADAM_WD * w_ref[...])

    spec = pl.BlockSpec((tr, c), lambda i: (i, 0))
    return pl.pallas_call(
        body, name=name, grid=(r // tr,), in_specs=[spec] * 4, out_specs=[spec] * 4,
        out_shape=[jax.ShapeDtypeStruct((r, c), F32)] * 4, compiler_params=_cp("parallel"))(w, g, m, v)


def _gather_weights(bigs, pool_w, pack_w, pack_d, name):
    nb = len(bigs)
    smalls = [pool_w, pack_w, pack_d]
    q, cw, cd = pool_w.shape[1], pack_w.shape[1], pack_d.shape[1]
    halves = [b.shape[1] // 2 for b in bigs]

    def body(*refs):
        srcs, dsts = refs[:nb + 3], refs[nb + 3:2 * (nb + 3)]
        ssem, rsem, lsem = refs[2 * (nb + 3):]
        x, y, c, me, chips, sib = _place()

        def big_dst(a, chip, half):
            return dsts[a].at[chip, pl.ds(half * halves[a], halves[a])]

        def small_dst(n, chip):
            if n == 0:
                return dsts[nb].at[:, pl.ds(chip * q, q), :]
            return dsts[nb + n].at[:, pl.ds(chip * (cw if n == 1 else cd), cw if n == 1 else cd)]

        local = [pltpu.make_async_copy(srcs[nb + n], small_dst(n, me), lsem.at[n]) for n in range(3)]
        for cp in local:
            cp.start()
        sends = []
        for a in range(nb):
            for k, chip in enumerate(chips):
                cp = _rcopy(srcs[a].at[me, pl.ds(c * halves[a], halves[a])], big_dst(a, me, c),
                            ssem.at[6 * a + k], rsem.at[6 * a + k], (*chip, c))
                cp.start()
                sends.append(cp)
        for n in range(3):
            for k, chip in enumerate(chips):
                cp = _rcopy(srcs[nb + n], small_dst(n, me), ssem.at[6 * nb + 3 * n + k], rsem.at[6 * nb + 3 * n + k], (*chip, c))
                cp.start()
                sends.append(cp)
        for a in range(nb):
            for k, chip in enumerate(chips):
                ref = big_dst(a, 2 * chip[0] + chip[1], c)
                _rcopy(ref, ref, ssem.at[6 * a + k], rsem.at[6 * a + k], (*chip, c)).wait_recv()
                cp = _rcopy(ref, ref, ssem.at[6 * a + 3 + k], rsem.at[6 * a + 3 + k], sib)
                cp.start()
                sends.append(cp)
        for a in range(nb):
            for k, chip in enumerate(chips):
                ref = big_dst(a, 2 * chip[0] + chip[1], 1 - c)
                _rcopy(ref, ref, ssem.at[6 * a + 3 + k], rsem.at[6 * a + 3 + k], sib).wait_recv()
        for n in range(3):
            for k, chip in enumerate(chips):
                ref = small_dst(n, 2 * chip[0] + chip[1])
                _rcopy(ref, ref, ssem.at[6 * nb + 3 * n + k], rsem.at[6 * nb + 3 * n + k], (*chip, c)).wait_recv()
        for cp in sends:
            cp.wait_send()
        for cp in local:
            cp.wait()

    nsem = 6 * nb + 9
    out_shape = [jax.ShapeDtypeStruct(b.shape, b.dtype) for b in bigs]
    out_shape += [jax.ShapeDtypeStruct((pool_w.shape[0], N_CHIPS * q, pool_w.shape[2]), pool_w.dtype),
                  jax.ShapeDtypeStruct((pack_w.shape[0], N_CHIPS * cw), pack_w.dtype),
                  jax.ShapeDtypeStruct((pack_d.shape[0], N_CHIPS * cd), pack_d.dtype)]
    return pl.pallas_call(
        body, name=name, in_specs=[ANY] * (nb + 3), out_specs=[ANY] * (nb + 3), out_shape=out_shape,
        input_output_aliases={a: a for a in range(nb)},
        scratch_shapes=[pltpu.SemaphoreType.DMA((nsem,)), pltpu.SemaphoreType.DMA((nsem,)), pltpu.SemaphoreType.DMA((3,))],
        compiler_params=pltpu.CompilerParams(has_side_effects=True))(*bigs, *smalls)


def _swap_with_sibling(grads, wholes, name):
    n, nw = len(grads), len(wholes)
    halves = [g.shape[1] // 2 for g in grads]

    def body(*refs):
        srcs, dsts = refs[:n + nw], refs[n + nw:2 * (n + nw)]
        ssem, rsem = refs[2 * (n + nw):]
        x, y, c, me, chips, sib = _place()
        cps = [_rcopy(srcs[a].at[:, pl.ds((1 - c) * halves[a], halves[a]), :], dsts[a], ssem.at[a], rsem.at[a], sib)
               for a in range(n)]
        cps += [_rcopy(srcs[a], dsts[a], ssem.at[a], rsem.at[a], sib) for a in range(n, n + nw)]
        for cp in cps:
            cp.start()
        for cp in cps:
            cp.wait_recv()
        for cp in cps:
            cp.wait_send()

    out_shape = [jax.ShapeDtypeStruct((g.shape[0], h, g.shape[2]), g.dtype) for g, h in zip(grads, halves)]
    out_shape += [jax.ShapeDtypeStruct(w.shape, w.dtype) for w in wholes]
    return pl.pallas_call(
        body, name=name, in_specs=[ANY] * (n + nw), out_specs=[ANY] * (n + nw), out_shape=out_shape,
        scratch_shapes=[pltpu.SemaphoreType.DMA((n + nw,)), pltpu.SemaphoreType.DMA((n + nw,))],
        compiler_params=pltpu.CompilerParams(has_side_effects=True))(*grads, *wholes)


def _scatter_to_chips(halves_in, small, name):
    n = len(halves_in)
    rh = small.shape[0] // 2

    def body(*refs):
        srcs, dsts = refs[:n + 1], refs[n + 1:2 * (n + 1)]
        ssem, rsem = refs[2 * (n + 1):]
        x, y, c, me, chips, sib = _place()
        cps = []
        for a in range(n + 1):
            for k, chip in enumerate(chips):
                src = srcs[a].at[2 * chip[0] + chip[1]] if a < n else srcs[a].at[pl.ds(c * rh, rh)]
                cps.append(_rcopy(src, dsts[a].at[k], ssem.at[3 * a + k], rsem.at[3 * a + k], (*chip, c)))
        for cp in cps:
            cp.start()
        for cp in cps:
            cp.wait_recv()
        for cp in cps:
            cp.wait_send()

    out_shape = [jax.ShapeDtypeStruct((3,) + h.shape[1:], h.dtype) for h in halves_in]
    out_shape.append(jax.ShapeDtypeStruct((3, rh, small.shape[1]), small.dtype))
    return pl.pallas_call(
        body, name=name, in_specs=[ANY] * (n + 1), out_specs=[ANY] * (n + 1), out_shape=out_shape,
        scratch_shapes=[pltpu.SemaphoreType.DMA((3 * (n + 1),)), pltpu.SemaphoreType.DMA((3 * (n + 1),))],
        compiler_params=pltpu.CompilerParams(has_side_effects=True))(*halves_in, small)


def _join_halves(parts, name):
    n = len(parts)

    def body(*refs):
        srcs, dsts = refs[:n], refs[n:2 * n]
        ssem, rsem = refs[2 * n:]
        x, y, c, me, chips, sib = _place()
        cps = []
        for a in range(n):
            h = srcs[a].shape[0] // 2
            cps.append(_rcopy(srcs[a].at[pl.ds(c * h, h)], dsts[a].at[pl.ds(c * h, h)], ssem.at[a], rsem.at[a], sib))
        for cp in cps:
            cp.start()
        for a in range(n):
            h = srcs[a].shape[0] // 2
            theirs = dsts[a].at[pl.ds((1 - c) * h, h)]
            _rcopy(theirs, theirs, ssem.at[a], rsem.at[a], sib).wait_recv()
        for cp in cps:
            cp.wait_send()

    out_shape = [jax.ShapeDtypeStruct(p.shape, p.dtype) for p in parts]
    return pl.pallas_call(
        body, name=name, in_specs=[ANY] * n, out_specs=[ANY] * n, out_shape=out_shape,
        input_output_aliases={a: a for a in range(n)},
        scratch_shapes=[pltpu.SemaphoreType.DMA((n,)), pltpu.SemaphoreType.DMA((n,))],
        compiler_params=pltpu.CompilerParams(has_side_effects=True))(*parts)


def _pad_rows(a, rows):
    return jnp.pad(a, ((0, rows - a.shape[0]), (0, 0)))


def _stack_rows(parts, multiple):
    padded = [_pad_rows(p, -(-p.shape[0] // 8) * 8) for p in parts]
    starts, at = [], 0
    for p in padded:
        starts.append(at)
        at += p.shape[0]
    total = -(-at // multiple) * multiple
    if total > at:
        padded.append(jnp.zeros((total - at, parts[0].shape[1]), parts[0].dtype))
    return jnp.concatenate(padded, axis=0), starts


def kernel(x, ln_pre_even, w_in_even, pool_w, pool_scale, w_out_even, ln_post_even, ln_pre_odd, w_in_odd, sconv_w, dconv_w, dconv_b, cnorm_g, cnorm_b, w_out_odd, ln_post_odd, loss_target, m_ln_pre_even, m_w_in_even, m_pool_w, m_pool_scale, m_w_out_even, m_ln_post_even, m_ln_pre_odd, m_w_in_odd, m_sconv_w, m_dconv_w, m_dconv_b, m_cnorm_g, m_cnorm_b, m_w_out_odd, m_ln_post_odd, v_ln_pre_even, v_w_in_even, v_pool_w, v_pool_scale, v_w_out_even, v_ln_post_even, v_ln_pre_odd, v_w_in_odd, v_sconv_w, v_dconv_w, v_dconv_b, v_cnorm_g, v_cnorm_b, v_w_out_odd, v_ln_post_odd):
    _, s, d = x.shape
    half = d // 2
    cw = half // N_CHIPS
    ng, q, gd = pool_w.shape[1:]
    k3, k31 = sconv_w.shape[1], dconv_w.shape[1]
    x2d, tgt = x[0], loss_target[0]
    me = 2 * lax.axis_index("x") + lax.axis_index("y")
    core = lax.axis_index("c")
    c_arr = jnp.reshape(core, (1,)).astype(jnp.int32)
    me_arr = jnp.reshape(me, (1,)).astype(jnp.int32)
    mc_arr = jnp.stack([me, core]).astype(jnp.int32)

    shards = [w_in_even[0], w_out_even[0], w_in_odd[0], w_out_odd[0]]
    slabs = [_cast_bf16_own_slab(w, me_arr, f"cast_w{n}") for n, w in enumerate(shards)]
    pool_w_b = _cast_bf16(pool_w[0].reshape(ng * q, gd), "cast_pool_w").reshape(ng, q, gd)
    pack_w, at_w = _stack_rows([sconv_w[0], dconv_w[0], dconv_b, cnorm_g, cnorm_b], 8)
    pack_d, at_d = _stack_rows([ln_pre_odd, ln_post_odd], 8)
    win_e, pool_w_f, pack_w_f, pack_d_f = _gather_weights(slabs[:1], pool_w_b, pack_w, pack_d, "gather_first")
    sconv_f = pack_w_f[at_w[0]:at_w[0] + k3]
    dconv_f = pack_w_f[at_w[1]:at_w[1] + k31]
    dconv_b_f, cnorm_g_f, cnorm_b_f = (pack_w_f[at_w[n]:at_w[n] + 1] for n in (2, 3, 4))
    ln_pre_odd_f = pack_d_f[at_d[0]:at_d[0] + 1]
    ln_post_odd_f = pack_d_f[at_d[1]:at_d[1] + 1]

    def reduce_half(g, name):
        (got,) = _swap_with_sibling([g], [], "swap_" + name)
        return _half_add(g, got, c_arr, "half_add_" + name)

    h0 = _rms_fwd(x2d, ln_pre_even, "rms_pre_even")
    plans = _Multi([_GatherPlan([slabs[1]]), _GatherPlan([slabs[2]], (0, 1, 4))])
    p_e, extra = _mm_nn(h0, win_e, "proj_in_even", plans)
    (wout_e,), (win_o,) = plans.results(extra)
    wout_e = wout_e.reshape(d, d)
    att, ltot, (win_o,) = _sba_fwd(p_e, half, "sba_fwd", _GatherPlan([win_o], (1, 4, 4)))
    y_e = _even_mix_fwd(p_e, att, pool_w_f, pool_scale, d, "even_mix_fwd")
    o_e, x1, h1 = _mm_out_even(y_e, wout_e, x2d, ln_post_even, ln_pre_odd_f, "proj_out_even")
    p_o, (wout_o,) = _mm_nn(h1, win_o, "proj_in_odd", _GatherPlan([slabs[3]]))
    wout_o = wout_o.reshape(d, d)
    y_o, s3, d1 = _odd_mix_fwd(p_o, sconv_f, dconv_f, dconv_b_f, cnorm_g_f, cnorm_b_f, d, "odd_mix_fwd")
    do_o, dx2, loss_blk, dln_post_odd = _mm_out_odd(y_o, wout_o, x1, ln_post_odd_f, tgt, "proj_out_odd_loss")
    loss = lax.psum(loss_blk[0, 0], ("x", "y", "c"))

    dy_o = _mm_nt(do_o, wout_o, "dy_odd")
    g_wout_o = _mm_tn(y_o, do_o, 1, "dw_out_odd").reshape(N_CHIPS, d // N_CHIPS, d)
    h_wout_o = reduce_half(g_wout_o, "out_odd")
    dbc, dgate_o, ds3, dd1, dcnorm_g, dcnorm_b, ddconv_b = _odd_bwd_rows(p_o, s3, d1, dy_o, cnorm_g_f, cnorm_b_f, d, "odd_bwd_rows")
    dhc, dcc, dga, dgb, dsconv, ddconv = _odd_bwd_conv(p_o, ds3, dd1, sconv_f, dconv_f, d, "odd_bwd_conv")
    dp_o = jnp.concatenate([dhc, dbc, dcc, dga, dgb, dgate_o], axis=1)
    g_win_o = _mm_tn(h1, dp_o, N_CHIPS, "dw_in_odd")
    h_win_o = reduce_half(g_win_o, "in_odd")
    plans = _Multi([_ScatterPlan([h_wout_o]), _ScatterPlan([h_win_o], (0, 1, 2))])
    (dx1, dln_pre_odd, do_e, dln_post_even), extra = _mm_in_bwd(
        dp_o, win_o, x1, ln_pre_odd_f, dx2, (o_e, ln_post_even), "dx_odd", plans)
    (s_wout_o,), (s_win_o,) = plans.results(extra)

    dy_e = _mm_nt(do_e, wout_e, "dy_even")
    g_wout_e = _mm_tn(y_e, do_e, 1, "dw_out_even").reshape(N_CHIPS, d // N_CHIPS, d)
    h_wout_e = reduce_half(g_wout_e, "out_even")
    datt, du, dgate_e, dpool_scale, dpool_w = _even_mix_bwd(p_e, att, dy_e, pool_w_f, pool_scale, d, "even_mix_bwd")
    plans = _Multi([_ScatterPlan([h_win_o], (1, 2, 2), into=[s_win_o]), _ScatterPlan([h_wout_e])])
    dq, dk, dv, extra = _sba_bwd(p_e, ltot, datt, half, "sba_bwd", plans)
    (s_win_o,), (s_wout_e,) = plans.results(extra)
    dp_e = jnp.concatenate([dq, dk, dv, du, dgate_e], axis=1)
    g_win_e = _mm_tn(h0, dp_e, N_CHIPS, "dw_in_even")
    h_win_e = reduce_half(g_win_e, "in_even")
    (grad_x, dln_pre_even), (s_win_e,) = _mm_in_bwd(dp_e, win_e, x2d, ln_pre_even, dx1, None, "dx_even", _ScatterPlan([h_win_e]))

    two = lambda v: v.reshape(2, half)
    small_parts = [two(dln_pre_even), dpool_scale, two(dln_post_even), two(dln_pre_odd), two(dln_post_odd),
                   dsconv, ddconv, ddconv_b, dcnorm_g, dcnorm_b, dpool_w.reshape(gd, half)]
    small, at_s = _stack_rows(small_parts, 16)
    (small1,) = _swap_with_sibling([], [small], "swap_small")
    small2 = _add2(small, small1, "small_add")
    (small_got,) = _scatter_to_chips([], small2, "scatter_small")
    pairs = [(h_win_e, s_win_e), (h_wout_e, s_wout_e), (h_win_o, s_win_o), (h_wout_o, s_wout_o)]
    parts = [_sum_chips(h, r, mc_arr, f"sum_chips{n}") for n, (h, r) in enumerate(pairs)]
    parts.append(_sum_chips_ordered(small2, small_got, mc_arr, "small_sum"))
    gw_in_e, gw_out_e, gw_in_o, gw_out_o, red = _join_halves(parts, "join_halves")

    def rows(n, cnt):
        return red[at_s[n]:at_s[n] + cnt]

    def mine(a, width):
        return lax.dynamic_slice_in_dim(a, me * width, width, axis=1)

    quarter = d // N_CHIPS
    g_small = {
        "ln_pre_even": rows(0, 2).reshape(1, d),
        "pool_scale": rows(1, 1),
        "ln_post_even": rows(2, 2).reshape(1, d),
        "ln_pre_odd": mine(rows(3, 2).reshape(1, d), quarter),
        "ln_post_odd": mine(rows(4, 2).reshape(1, d), quarter),
        "sconv_w": mine(rows(5, k3), cw),
        "dconv_w": mine(rows(6, k31), cw),
        "dconv_b": mine(rows(7, 1), cw),
        "cnorm_g": mine(rows(8, 1), cw),
        "cnorm_b": mine(rows(9, 1), cw),
        "pool_w": lax.dynamic_slice_in_dim(rows(10, gd).reshape(ng, gd, gd), me * q, q, axis=1).reshape(ng * q, gd),
    }
    w2d = {
        "ln_pre_even": ln_pre_even, "w_in_even": w_in_even[0], "pool_w": pool_w[0].reshape(ng * q, gd),
        "pool_scale": pool_scale, "w_out_even": w_out_even[0], "ln_post_even": ln_post_even, "ln_pre_odd": ln_pre_odd,
        "w_in_odd": w_in_odd[0], "sconv_w": sconv_w[0], "dconv_w": dconv_w[0], "dconv_b": dconv_b, "cnorm_g": cnorm_g,
        "cnorm_b": cnorm_b, "w_out_odd": w_out_odd[0], "ln_post_odd": ln_post_odd,
    }
    moments = {
        "ln_pre_even": (m_ln_pre_even, v_ln_pre_even), "w_in_even": (m_w_in_even, v_w_in_even),
        "pool_w": (m_pool_w, v_pool_w), "pool_scale": (m_pool_scale, v_pool_scale),
        "w_out_even": (m_w_out_even, v_w_out_even), "ln_post_even": (m_ln_post_even, v_ln_post_even),
        "ln_pre_odd": (m_ln_pre_odd, v_ln_pre_odd), "w_in_odd": (m_w_in_odd, v_w_in_odd),
        "sconv_w": (m_sconv_w, v_sconv_w), "dconv_w": (m_dconv_w, v_dconv_w), "dconv_b": (m_dconv_b, v_dconv_b),
        "cnorm_g": (m_cnorm_g, v_cnorm_g), "cnorm_b": (m_cnorm_b, v_cnorm_b),
        "w_out_odd": (m_w_out_odd, v_w_out_odd), "ln_post_odd": (m_ln_post_odd, v_ln_post_odd),
    }
    g2d = dict(g_small, w_in_even=gw_in_e, w_out_even=gw_out_e, w_in_odd=gw_in_o, w_out_odd=gw_out_o)
    grads_out, deltas, new_m, new_v = [], [], [], []
    for name, w in w2d.items():
        m_in, v_in = moments[name]
        shape = m_in.shape
        g_out, delta, nm, nv = _adamw(w, g2d[name], m_in.reshape(w.shape), v_in.reshape(w.shape), "adamw_" + name)
        grads_out.append(g_out.reshape(shape))
        deltas.append(delta.reshape(shape))
        new_m.append(nm.reshape(shape))
        new_v.append(nv.reshape(shape))
    return (loss, grad_x.reshape(x.shape), *grads_out, *deltas, *new_m, *new_v)
```

```python
import functools
import math

import jax
import jax.numpy as jnp
from jax import lax
from jax.experimental import pallas as pl
from jax.experimental.pallas import tpu as pltpu

F32 = jnp.float32
BF16 = jnp.bfloat16
EPS = 1e-6
N_CHIPS = 4
VMEM_LIMIT_V7X = 56 << 20
HEAD_DIM = 128
ATT_BLOCK = 256
POOL_WINDOWS = (2, 4, 8, 16)
ROW_TILE = 256
POOL_HALO = 16
CONV_HALO = 32
LANES = 128
ADAM_LR, ADAM_B1, ADAM_B2, ADAM_EPS, ADAM_WD, ADAM_STEP = 0.001, 0.9, 0.999, 1e-08, 0.01, 10
MESH_ID = pl.DeviceIdType.MESH
ANY = pl.BlockSpec(memory_space=pl.ANY)


def _cp(*sem):
    return pltpu.CompilerParams(dimension_semantics=sem or None, vmem_limit_bytes=VMEM_LIMIT_V7X)


def _pick_tile(n, cap):
    best = None
    for t in range(LANES, min(n, cap) + 1, LANES):
        if n % t == 0:
            best = t
    assert best is not None, (n, cap)
    return best


def _sigmoid(x):
    return 1.0 / (1.0 + jnp.exp(-x))


def _silu(x):
    return x * _sigmoid(x)


def _dsilu(x):
    s = _sigmoid(x)
    return s * (1.0 + x * (1.0 - s))


def _log_sigmoid(z):
    return jnp.minimum(z, 0.0) - jnp.log(1.0 + jnp.exp(-jnp.abs(z)))


def _rms_stats(x):
    r = lax.rsqrt(jnp.mean(x * x, axis=-1, keepdims=True) + EPS)
    return x * r, r


def _rms_bwd(dh, xhat, r, g):
    dxh = dh * g
    dx = r * (dxh - xhat * jnp.mean(dxh * xhat, axis=-1, keepdims=True))
    return dx, jnp.sum(dh * xhat, axis=0, keepdims=True)


def _acc_rows(ref, first, val):
    @pl.when(first)
    def _():
        ref[...] = val

    @pl.when(jnp.logical_not(first))
    def _():
        ref[...] += val


def _rcopy(src, dst, ssem, rsem, dev):
    return pltpu.make_async_remote_copy(src_ref=src, dst_ref=dst, send_sem=ssem, recv_sem=rsem,
                                        device_id=dev, device_id_type=MESH_ID)


def _place():
    x, y, c = lax.axis_index("x"), lax.axis_index("y"), lax.axis_index("c")
    chips = [(1 - x, y), (x, 1 - y), (1 - x, 1 - y)]
    return x, y, c, 2 * x + y, chips, (x, y, 1 - c)


class _GatherPlan:
    def __init__(self, arrays, part=(0, 1, 1)):
        self.operands = list(arrays)
        self.out_shapes = [jax.ShapeDtypeStruct(a.shape, a.dtype) for a in arrays]
        self.aliases = {i: i for i in range(len(arrays))}
        self.nsems = 6 * len(arrays)
        self.base = 0
        self.halves = [a.shape[1] // 2 for a in arrays]
        self.part = part

    def _slab(self, ref, a, chip, half):
        lo, hi, n = self.part
        h = self.halves[a]
        return ref.at[chip, pl.ds(half * h + lo * h // n, (hi - lo) * h // n)]

    def _sends(self, ins, outs, ssem, rsem):
        x, y, c, me, chips, sib = _place()
        return [_rcopy(self._slab(ins[a], a, me, c), self._slab(outs[a], a, me, c),
                       ssem.at[self.base + 6 * a + k], rsem.at[self.base + 6 * a + k], (*chip, c))
                for a in range(len(ins)) for k, chip in enumerate(chips)]

    def _onward(self, outs, ssem, rsem, half_of):
        x, y, c, me, chips, sib = _place()
        out = []
        for a in range(len(outs)):
            for k, chip in enumerate(chips):
                ref = self._slab(outs[a], a, 2 * chip[0] + chip[1], half_of(c))
                out.append(_rcopy(ref, ref, ssem.at[self.base + 6 * a + 3 + k], rsem.at[self.base + 6 * a + 3 + k], sib))
        return out

    def start(self, ins, outs, ssem, rsem):
        for cp in self._sends(ins, outs, ssem, rsem):
            cp.start()

    def mid(self, ins, outs, ssem, rsem):
        x, y, c, me, chips, sib = _place()
        landed = [_rcopy(self._slab(outs[a], a, 2 * chip[0] + chip[1], c), self._slab(outs[a], a, 2 * chip[0] + chip[1], c),
                         ssem.at[self.base + 6 * a + k], rsem.at[self.base + 6 * a + k], (*chip, c))
                  for a in range(len(outs)) for k, chip in enumerate(chips)]
        for got, cp in zip(landed, self._onward(outs, ssem, rsem, lambda c: c)):
            got.wait_recv()
            cp.start()

    def finish(self, ins, outs, ssem, rsem):
        for cp in self._onward(outs, ssem, rsem, lambda c: 1 - c):
            cp.wait_recv()
        for cp in self._sends(ins, outs, ssem, rsem) + self._onward(outs, ssem, rsem, lambda c: c):
            cp.wait_send()


class _ScatterPlan:
    def __init__(self, arrays, part=(0, 1, 1), into=None):
        self.n = len(arrays)
        self.operands = list(arrays) + list(into or [])
        self.out_shapes = [jax.ShapeDtypeStruct((3,) + a.shape[1:], a.dtype) for a in arrays]
        self.aliases = {self.n + i: i for i in range(self.n)} if into else {}
        self.nsems = 3 * self.n
        self.base = 0
        self.part = part

    def _copies(self, ins, outs, ssem, rsem):
        x, y, c, me, chips, sib = _place()
        lo, hi, n = self.part
        out = []
        for a in range(self.n):
            h = ins[a].shape[1]
            rows = pl.ds(lo * h // n, (hi - lo) * h // n)
            for k, chip in enumerate(chips):
                out.append(_rcopy(ins[a].at[2 * chip[0] + chip[1], rows], outs[a].at[k, rows],
                                  ssem.at[self.base + 3 * a + k], rsem.at[self.base + 3 * a + k], (*chip, c)))
        return out

    def start(self, ins, outs, ssem, rsem):
        for cp in self._copies(ins, outs, ssem, rsem):
            cp.start()

    def mid(self, ins, outs, ssem, rsem):
        pass

    def finish(self, ins, outs, ssem, rsem):
        cps = self._copies(ins, outs, ssem, rsem)
        for cp in cps:
            cp.wait_recv()
        for cp in cps:
            cp.wait_send()


class _ShareHalfPlan(_ScatterPlan):
    def __init__(self, arrays):
        super().__init__(arrays)
        self.out_shapes = [jax.ShapeDtypeStruct((3, a.shape[0] // 2, a.shape[1]), a.dtype) for a in arrays]

    def _copies(self, ins, outs, ssem, rsem):
        x, y, c, me, chips, sib = _place()
        out = []
        for a in range(self.n):
            rh = ins[a].shape[0] // 2
            for k, chip in enumerate(chips):
                out.append(_rcopy(ins[a].at[pl.ds(c * rh, rh)], outs[a].at[k],
                                  ssem.at[self.base + 3 * a + k], rsem.at[self.base + 3 * a + k], (*chip, c)))
        return out


class _Multi:
    def __init__(self, plans):
        self.plans = plans
        self.operands, self.out_shapes, self.aliases, self.nsems = [], [], {}, 0
        self.spans = []
        for p in plans:
            ni, no = len(self.operands), len(self.out_shapes)
            self.spans.append((ni, ni + len(p.operands), no, no + len(p.out_shapes)))
            self.aliases.update({ni + i: no + j for i, j in p.aliases.items()})
            p.base = self.nsems
            self.nsems += p.nsems
            self.operands += p.operands
            self.out_shapes += p.out_shapes

    def _each(self, what, ins, outs, ssem, rsem):
        for p, (i0, i1, o0, o1) in zip(self.plans, self.spans):
            getattr(p, what)(ins[i0:i1], outs[o0:o1], ssem, rsem)

    def start(self, *a):
        self._each("start", *a)

    def mid(self, *a):
        self._each("mid", *a)

    def finish(self, *a):
        self._each("finish", *a)

    def results(self, extra):
        return [list(extra[o0:o1]) for (_, _, o0, o1) in self.spans]


class _Host:
    def __init__(self, comm, in_specs, out_specs, out_shape, scratch):
        self.comm = comm
        self.late = getattr(comm, "late", False)
        self.n_in, self.n_out = len(in_specs), len(out_specs)
        self.in_specs, self.out_specs, self.out_shape, self.scratch = list(in_specs), list(out_specs), list(out_shape), list(scratch)
        self.aliases = {}
        self.args = []
        if comm is not None:
            self.in_specs += [ANY] * len(comm.operands)
            self.out_specs += [ANY] * len(comm.out_shapes)
            self.out_shape += comm.out_shapes
            self.scratch += [pltpu.SemaphoreType.DMA((comm.nsems,)), pltpu.SemaphoreType.DMA((comm.nsems,))]
            self.aliases = {self.n_in + i: self.n_out + j for i, j in comm.aliases.items()}
            self.args = list(comm.operands)

    def split(self, refs):
        nc = len(self.args)
        nco = len(self.out_shape) - self.n_out
        ins, p = refs[:self.n_in], self.n_in + nc
        outs, rest = refs[p:p + self.n_out], refs[p + self.n_out + nco:]
        self._cargs = None
        if self.comm is not None:
            self._cargs = (refs[self.n_in:p], refs[p + self.n_out:p + self.n_out + nco], rest[-2], rest[-1])
            rest = rest[:-2]
        return ins, outs, rest

    def before(self, step, total):
        if self.comm is None:
            return

        @pl.when(step == 0)
        def _():
            self.comm.start(*self._cargs)

        @pl.when(step == (total - 1 if self.late else (3 * total) // 4))
        def _():
            self.comm.mid(*self._cargs)

    def after(self, step, total):
        if self.comm is None:
            return

        @pl.when(step == total - 1)
        def _():
            self.comm.finish(*self._cargs)

    def results(self, outs):
        return outs[:self.n_out], outs[self.n_out:]


def _cast_bf16(x, name):
    r, c = x.shape
    tr = ROW_TILE if r % ROW_TILE == 0 else r

    def body(x_ref, o_ref):
        o_ref[...] = x_ref[...].astype(BF16)

    return pl.pallas_call(
        body, name=name, grid=(r // tr,),
        in_specs=[pl.BlockSpec((tr, c), lambda i: (i, 0))],
        out_specs=pl.BlockSpec((tr, c), lambda i: (i, 0)),
        out_shape=jax.ShapeDtypeStruct((r, c), BF16), compiler_params=_cp("parallel"))(x)


def _cast_bf16_own_slab(x, me_arr, name):
    r, c = x.shape
    tr = ROW_TILE if r % ROW_TILE == 0 else r

    def body(me_ref, x_ref, o_ref):
        o_ref[...] = x_ref[...].astype(BF16)

    return pl.pallas_call(
        body, name=name,
        grid_spec=pltpu.PrefetchScalarGridSpec(
            num_scalar_prefetch=1, grid=(r // tr,),
            in_specs=[pl.BlockSpec((tr, c), lambda i, me: (i, 0))],
            out_specs=pl.BlockSpec((None, tr, c), lambda i, me: (me[0], i, 0))),
        out_shape=jax.ShapeDtypeStruct((N_CHIPS, r, c), BF16), compiler_params=_cp("parallel"))(me_arr, x)


def _rms_fwd(x, g, name):
    s, d = x.shape

    def body(x_ref, g_ref, h_ref):
        xhat, _ = _rms_stats(x_ref[...])
        h_ref[...] = (xhat * g_ref[...]).astype(BF16)

    return pl.pallas_call(
        body, name=name, grid=(s // ROW_TILE,),
        in_specs=[pl.BlockSpec((ROW_TILE, d), lambda i: (i, 0)), pl.BlockSpec((1, d), lambda i: (0, 0))],
        out_specs=pl.BlockSpec((ROW_TILE, d), lambda i: (i, 0)),
        out_shape=jax.ShapeDtypeStruct((s, d), BF16), compiler_params=_cp("parallel"))(x, g)


def _mm_nn(a, w3, name, comm=None):
    m, k = a.shape
    nsh, _, ns = w3.shape
    tm = 512 if m % 512 == 0 else ROW_TILE
    tn = _pick_tile(ns, 1024)
    per = ns // tn
    grid = (nsh * per, m // tm)
    host = _Host(comm,
                 [pl.BlockSpec((tm, k), lambda n, i: (i, 0)), pl.BlockSpec((None, k, tn), lambda n, i: (n // per, 0, n % per))],
                 [pl.BlockSpec((tm, tn), lambda n, i: (i, n))], [jax.ShapeDtypeStruct((m, nsh * ns), F32)], [])

    def body(*refs):
        (a_ref, w_ref), (o_ref,), _ = host.split(refs)
        step = pl.program_id(0) * grid[1] + pl.program_id(1)
        host.before(step, grid[0] * grid[1])
        o_ref[...] = jnp.dot(a_ref[...], w_ref[...], preferred_element_type=F32)
        host.after(step, grid[0] * grid[1])

    outs = pl.pallas_call(
        body, name=name, grid=grid, in_specs=host.in_specs, out_specs=host.out_specs, out_shape=host.out_shape,
        scratch_shapes=host.scratch, input_output_aliases=host.aliases,
        compiler_params=_cp("arbitrary", "arbitrary"))(a, w3, *host.args)
    (out,), extra = host.results(outs)
    return out, extra


def _mm_nt(a, b, name):
    m, k = a.shape
    n = b.shape[0]
    tm = 512 if m % 512 == 0 else ROW_TILE

    def body(a_ref, b_ref, o_ref):
        o_ref[...] = lax.dot_general(a_ref[...], b_ref[...], (((1,), (1,)), ((), ())), preferred_element_type=F32)

    return pl.pallas_call(
        body, name=name, grid=(m // tm,),
        in_specs=[pl.BlockSpec((tm, k), lambda i: (i, 0)), pl.BlockSpec((n, k), lambda i: (0, 0))],
        out_specs=pl.BlockSpec((tm, n), lambda i: (i, 0)),
        out_shape=jax.ShapeDtypeStruct((m, n), F32), compiler_params=_cp("parallel"))(a, b)


def _mm_tn(a, b, nsh, name):
    s, m = a.shape
    n = b.shape[1]
    ns = n // nsh
    tm = 512 if m % 512 == 0 else ROW_TILE
    tn = _pick_tile(ns, 1024)
    per = ns // tn

    def body(a_ref, b_ref, o_ref):
        o_ref[...] = lax.dot_general(a_ref[...], b_ref[...], (((0,), (0,)), ((), ())),
                                     preferred_element_type=F32).astype(BF16)

    return pl.pallas_call(
        body, name=name, grid=(nsh * per, m // tm),
        in_specs=[pl.BlockSpec((s, tm), lambda j, i: (0, i)), pl.BlockSpec((s, tn), lambda j, i: (0, j))],
        out_specs=pl.BlockSpec((None, tm, tn), lambda j, i: (j // per, i, j % per)),
        out_shape=jax.ShapeDtypeStruct((nsh, m, ns), BF16), compiler_params=_cp("parallel", "parallel"))(a, b)


def _tri(n, rel):
    row = lax.broadcasted_iota(jnp.int32, (2 * n, n), 0)
    col = lax.broadcasted_iota(jnp.int32, (2 * n, n), 1)
    return jnp.where(rel(jnp.where(row >= n, row - n, row), col), 1.0, 0.0).astype(BF16)


def _dot_split(x, tri2):
    hi = x.astype(BF16)
    lo = (x - hi.astype(F32)).astype(BF16)
    return jnp.dot(jnp.concatenate([hi, lo], axis=1), tri2, preferred_element_type=F32)


def _nt(a, b):
    return lax.dot_general(a, b, (((1,), (1,)), ((), ())), preferred_element_type=F32)


def _tn(a, b):
    return lax.dot_general(a, b, (((0,), (0,)), ((), ())), preferred_element_type=F32)


def _heads_per_step(nh):
    return max(h for h in (1, 2, 4) if nh % h == 0)


def _sba_fwd(p, sbw, name, comm=None):
    s = p.shape[0]
    nh = sbw // HEAD_DIM
    hp = _heads_per_step(nh)
    ngrp, hw = nh // hp, hp * HEAD_DIM
    blk = ATT_BLOCK
    nq = s // blk
    scale = 1.0 / math.sqrt(HEAD_DIM)
    host = _Host(comm,
                 [pl.BlockSpec((blk, hw), lambda g, i: (i, g)),
                  pl.BlockSpec((s, hw), lambda g, i: (0, ngrp + g)),
                  pl.BlockSpec((s, hw), lambda g, i: (0, 2 * ngrp + g))],
                 [pl.BlockSpec((blk, hw), lambda g, i: (i, g))] * 2,
                 [jax.ShapeDtypeStruct((s, sbw), F32)] * 2,
                 [pltpu.VMEM((s, hw), BF16)] * 2)

    def body(*refs):
        (q_ref, k_ref, v_ref), (o_ref, lt_ref), (kb_ref, vb_ref) = host.split(refs)
        i = pl.program_id(1)
        step = pl.program_id(0) * nq + i
        host.before(step, ngrp * nq)

        @pl.when(i == 0)
        def _():
            kb_ref[...] = k_ref[...].astype(BF16)
            vb_ref[...] = v_ref[...].astype(BF16)

        heads = [slice(h * HEAD_DIM, (h + 1) * HEAD_DIM) for h in range(hp)]
        qs = [q_ref[:, hd].astype(BF16) for hd in heads]
        later = _tri(blk, lambda r, c: r > c)
        causal = lax.broadcasted_iota(jnp.int32, (blk, blk), 1) < lax.broadcasted_iota(jnp.int32, (blk, blk), 0)

        def key_block(j, carry, diagonal):
            rows = pl.ds(pl.multiple_of(j * blk, blk), blk)
            hs = range(hp)
            z = [_nt(qs[h], kb_ref[rows, heads[h]]) * scale for h in hs]
            ls = [_log_sigmoid(z[h]) for h in hs]
            lm = [jnp.where(causal, ls[h] - z[h], 0.0) if diagonal else ls[h] - z[h] for h in hs]
            stay = [_dot_split(lm[h], later) for h in hs]
            w = [jnp.exp(ls[h] + stay[h] + carry[h][1]) for h in hs]
            if diagonal:
                w = [jnp.where(causal, w[h], 0.0) for h in hs]
            acc = [carry[h][0] + jnp.dot(w[h].astype(BF16), vb_ref[rows, heads[h]], preferred_element_type=F32) for h in hs]
            return tuple((acc[h], carry[h][1] + jnp.sum(lm[h], axis=1, keepdims=True)) for h in hs)

        init = tuple((jnp.zeros((blk, HEAD_DIM), F32), jnp.zeros((blk, 1), F32)) for _ in heads)
        carry = key_block(i, init, True)
        carry = lax.fori_loop(0, i, lambda n, c: key_block(i - 1 - n, c, False), carry)
        for h, hd in enumerate(heads):
            o_ref[:, hd] = carry[h][0]
            lt_ref[:, hd] = jnp.broadcast_to(carry[h][1], (blk, HEAD_DIM))
        host.after(step, ngrp * nq)

    outs = pl.pallas_call(
        body, name=name, grid=(ngrp, nq), in_specs=host.in_specs, out_specs=host.out_specs, out_shape=host.out_shape,
        scratch_shapes=host.scratch, input_output_aliases=host.aliases,
        compiler_params=_cp("arbitrary", "arbitrary"))(p, p, p, *host.args)
    (out, ltot), extra = host.results(outs)
    return out, ltot, extra


def _sba_bwd(p, ltot, dout, sbw, name, comm=None):
    s = p.shape[0]
    nh = sbw // HEAD_DIM
    hp = _heads_per_step(nh)
    ngrp, hw = nh // hp, hp * HEAD_DIM
    blk = ATT_BLOCK
    nq = s // blk
    scale = 1.0 / math.sqrt(HEAD_DIM)
    blk_spec = pl.BlockSpec((blk, hw), lambda g, i: (i, g))
    col_spec = pl.BlockSpec((s, hw), lambda g, i: (0, g))
    host = _Host(comm,
                 [blk_spec, pl.BlockSpec((s, hw), lambda g, i: (0, ngrp + g)),
                  pl.BlockSpec((s, hw), lambda g, i: (0, 2 * ngrp + g)), blk_spec, blk_spec],
                 [blk_spec, col_spec, col_spec], [jax.ShapeDtypeStruct((s, sbw), BF16)] * 3,
                 [pltpu.VMEM((s, hw), BF16)] * 2 + [pltpu.VMEM((s, hw), F32)] * 2)

    def body(*refs):
        (q_ref, k_ref, v_ref, lt_ref, do_ref), (dq_ref, dk_ref, dv_ref), (kb_ref, vb_ref, dka_ref, dva_ref) = host.split(refs)
        i = pl.program_id(1)
        step = pl.program_id(0) * nq + i
        host.before(step, ngrp * nq)

        @pl.when(i == 0)
        def _():
            kb_ref[...] = k_ref[...].astype(BF16)
            vb_ref[...] = v_ref[...].astype(BF16)
            dka_ref[...] = jnp.zeros_like(dka_ref)
            dva_ref[...] = jnp.zeros_like(dva_ref)

        heads = [slice(h * HEAD_DIM, (h + 1) * HEAD_DIM) for h in range(hp)]
        qs = [q_ref[:, hd].astype(BF16) for hd in heads]
        dos = [do_ref[:, hd].astype(BF16) for hd in heads]
        ltots = [lt_ref[:, h * HEAD_DIM:h * HEAD_DIM + 1] for h in range(hp)]
        upto = _tri(blk, lambda r, c: r <= c)
        before = _tri(blk, lambda r, c: r < c)
        causal = lax.broadcasted_iota(jnp.int32, (blk, blk), 1) < lax.broadcasted_iota(jnp.int32, (blk, blk), 0)

        def key_block(j, carry, diagonal):
            rows = pl.ds(pl.multiple_of(j * blk, blk), blk)
            hs = range(hp)
            kj = [kb_ref[rows, heads[h]] for h in hs]
            vj = [vb_ref[rows, heads[h]] for h in hs]
            z = [_nt(qs[h], kj[h]) * scale for h in hs]
            dw = [_nt(dos[h], vj[h]) for h in hs]
            ls = [_log_sigmoid(z[h]) for h in hs]
            lm = [jnp.where(causal, ls[h] - z[h], 0.0) if diagonal else ls[h] - z[h] for h in hs]
            stay = [ltots[h] - carry[h][1] - _dot_split(lm[h], upto) for h in hs]
            w = [jnp.exp(ls[h] + stay[h]) for h in hs]
            if diagonal:
                w = [jnp.where(causal, w[h], 0.0) for h in hs]
            da = [dw[h] * w[h] for h in hs]
            sig = [jnp.exp(ls[h]) for h in hs]
            chain = [sig[h] * (carry[h][2] + _dot_split(da[h], before)) for h in hs]
            if diagonal:
                chain = [jnp.where(causal, chain[h], 0.0) for h in hs]
            dzb = [((da[h] * (1.0 - sig[h]) - chain[h]) * scale).astype(BF16) for h in hs]
            dq = [carry[h][0] + jnp.dot(dzb[h], kj[h], preferred_element_type=F32) for h in hs]
            for h in hs:
                dka_ref[rows, heads[h]] += _tn(dzb[h], qs[h])
            for h in hs:
                dva_ref[rows, heads[h]] += _tn(w[h].astype(BF16), dos[h])
            return tuple((dq[h], carry[h][1] + jnp.sum(lm[h], axis=1, keepdims=True),
                          carry[h][2] + jnp.sum(da[h], axis=1, keepdims=True)) for h in hs)

        zero = jnp.zeros((blk, 1), F32)
        init = tuple((jnp.zeros((blk, HEAD_DIM), F32), zero, zero) for _ in heads)
        carry = lax.fori_loop(0, i, lambda j, c: key_block(j, c, False), init)
        carry = key_block(i, carry, True)
        for h, hd in enumerate(heads):
            dq_ref[:, hd] = carry[h][0].astype(BF16)

        @pl.when(i == nq - 1)
        def _():
            dk_ref[...] = dka_ref[...].astype(BF16)
            dv_ref[...] = dva_ref[...].astype(BF16)

        host.after(step, ngrp * nq)

    outs = pl.pallas_call(
        body, name=name, grid=(ngrp, nq), in_specs=host.in_specs, out_specs=host.out_specs, out_shape=host.out_shape,
        scratch_shapes=host.scratch, input_output_aliases=host.aliases,
        compiler_params=_cp("arbitrary", "arbitrary"))(p, p, p, ltot, dout, *host.args)
    (dq, dk, dv), extra = host.results(outs)
    return dq, dk, dv, extra


def _pool_groups(pad_ref, tile, row0, gd, halo):
    row = row0 + lax.broadcasted_iota(jnp.int32, (tile, 1), 0)
    out = []
    for gi, win in enumerate(POOL_WINDOWS):
        cs = slice(gi * gd, (gi + 1) * gd)
        tok = pad_ref[halo:halo + tile, cs]
        acc = tok
        for j in range(1, win):
            acc = acc + pad_ref[halo - j:halo - j + tile, cs]
        cnt = jnp.minimum(win, row + 1).astype(F32)
        out.append(acc / cnt - tok)
    return out


def _even_mix_fwd(p, att, pool_w, pool_scale, d, name):
    s = p.shape[0]
    half = d // 2
    gd = half // len(POOL_WINDOWS)
    t, hb = ROW_TILE, POOL_HALO

    def body(u_ref, uh_ref, g_ref, a_ref, pw_ref, sc_ref, y_ref, pad_ref):
        i = pl.program_id(0)
        pad_ref[0:hb, :] = jnp.where(i > 0, uh_ref[...], 0.0)
        pad_ref[hb:, :] = u_ref[...]
        pooled = _pool_groups(pad_ref, t, i * t, gd, hb)
        for gi in range(len(POOL_WINDOWS)):
            cs = slice(gi * gd, (gi + 1) * gd)
            po = jnp.dot(pooled[gi].astype(BF16), pw_ref[gi], preferred_element_type=F32) * sc_ref[:, cs]
            y_ref[:, half + gi * gd:half + (gi + 1) * gd] = (po * _silu(g_ref[:, half + gi * gd:half + (gi + 1) * gd])).astype(BF16)
        y_ref[:, :half] = (a_ref[...] * _silu(g_ref[:, :half])).astype(BF16)

    return pl.pallas_call(
        body, name=name, grid=(s // t,),
        in_specs=[pl.BlockSpec((t, half), lambda i: (i, 3)),
                  pl.BlockSpec((hb, half), lambda i: (jnp.maximum(i * (t // hb) - 1, 0), 3)),
                  pl.BlockSpec((t, d), lambda i: (i, 2)),
                  pl.BlockSpec((t, half), lambda i: (i, 0)),
                  pl.BlockSpec(pool_w.shape, lambda i: (0, 0, 0)),
                  pl.BlockSpec((1, half), lambda i: (0, 0))],
        out_specs=pl.BlockSpec((t, d), lambda i: (i, 0)),
        out_shape=jax.ShapeDtypeStruct((s, d), BF16),
        scratch_shapes=[pltpu.VMEM((hb + t, half), F32)],
        compiler_params=_cp("parallel"))(p, p, p, att, pool_w, pool_scale)


def _even_mix_bwd(p, att, dy, pool_w, pool_scale, d, name):
    s = p.shape[0]
    half = d // 2
    ng = len(POOL_WINDOWS)
    gd = half // ng
    t, hb = ROW_TILE, POOL_HALO
    nt = s // t

    def body(u_ref, uh_ref, g_ref, gh_ref, a_ref, dy_ref, dyh_ref, pw_ref, sc_ref,
             da_ref, du_ref, dg_ref, dsc_ref, dpw_ref, pad_ref, dn_ref):
        i = pl.program_id(0)
        first = i == 0
        pad_ref[0:hb, :] = jnp.where(i > 0, uh_ref[...], 0.0)
        pad_ref[hb:, :] = u_ref[...]
        pooled = _pool_groups(pad_ref, t, i * t, gd, hb)
        g1 = g_ref[:, :half]
        dy1 = dy_ref[:, :half]
        da_ref[...] = dy1 * _silu(g1)
        dg_ref[:, :half] = (dy1 * a_ref[...] * _dsilu(g1)).astype(BF16)
        row = i * t + lax.broadcasted_iota(jnp.int32, (t + hb, 1), 0)
        for gi, win in enumerate(POOL_WINDOWS):
            cs = slice(gi * gd, (gi + 1) * gd)
            cs2 = slice(half + gi * gd, half + (gi + 1) * gd)
            w = pw_ref[gi]
            pb = pooled[gi].astype(BF16)
            zp = jnp.dot(pb, w, preferred_element_type=F32)
            g2 = g_ref[:, cs2]
            dy2 = dy_ref[:, cs2]
            dg_ref[:, cs2] = (dy2 * zp * sc_ref[:, cs] * _dsilu(g2)).astype(BF16)
            dpo = dy2 * _silu(g2)
            _acc_rows(dsc_ref.at[:, cs], first, jnp.sum(dpo * zp, axis=0, keepdims=True))
            dz = (dpo * sc_ref[:, cs]).astype(BF16)
            _acc_rows(dpw_ref.at[gi], first, _tn(pb, dz))
            dzh = jnp.where(i < nt - 1, dyh_ref[:, cs] * _silu(gh_ref[:, cs]) * sc_ref[:, cs], 0.0).astype(BF16)
            dpool = _nt(dz, w)
            dpool_h = _nt(dzh, w)
            cnt = jnp.minimum(win, row + 1).astype(F32)
            dn_ref[0:t, cs] = dpool / cnt[0:t]
            dn_ref[t:, cs] = dpool_h / cnt[t:]
            acc = dn_ref[0:t, cs]
            for j in range(1, win):
                acc = acc + dn_ref[j:j + t, cs]
            du_ref[:, cs] = (acc - dpool).astype(BF16)

    return pl.pallas_call(
        body, name=name, grid=(nt,),
        in_specs=[pl.BlockSpec((t, half), lambda i: (i, 3)),
                  pl.BlockSpec((hb, half), lambda i: (jnp.maximum(i * (t // hb) - 1, 0), 3)),
                  pl.BlockSpec((t, d), lambda i: (i, 2)),
                  pl.BlockSpec((hb, half), lambda i: (jnp.minimum((i + 1) * (t // hb), s // hb - 1), 5)),
                  pl.BlockSpec((t, half), lambda i: (i, 0)),
                  pl.BlockSpec((t, d), lambda i: (i, 0)),
                  pl.BlockSpec((hb, half), lambda i: (jnp.minimum((i + 1) * (t // hb), s // hb - 1), 1)),
                  pl.BlockSpec(pool_w.shape, lambda i: (0, 0, 0)),
                  pl.BlockSpec((1, half), lambda i: (0, 0))],
        out_specs=[pl.BlockSpec((t, half), lambda i: (i, 0)),
                   pl.BlockSpec((t, half), lambda i: (i, 0)),
                   pl.BlockSpec((t, d), lambda i: (i, 0)),
                   pl.BlockSpec((1, half), lambda i: (0, 0)),
                   pl.BlockSpec((ng, gd, gd), lambda i: (0, 0, 0))],
        out_shape=[jax.ShapeDtypeStruct((s, half), F32), jax.ShapeDtypeStruct((s, half), BF16),
                   jax.ShapeDtypeStruct((s, d), BF16), jax.ShapeDtypeStruct((1, half), F32),
                   jax.ShapeDtypeStruct((ng, gd, gd), F32)],
        scratch_shapes=[pltpu.VMEM((hb + t, half), F32), pltpu.VMEM((t + hb, half), F32)],
        compiler_params=_cp("arbitrary"))(p, p, p, p, att, dy, dy, pool_w, pool_scale)


def _mm_out_even(y, w, x, g_post, g_pre_next, name):
    s, k = y.shape
    d = w.shape[1]
    t = ROW_TILE

    def body(y_ref, w_ref, x_ref, gp_ref, gn_ref, o_ref, x1_ref, h1_ref):
        o = jnp.dot(y_ref[...], w_ref[...], preferred_element_type=F32)
        o_ref[...] = o
        ohat, _ = _rms_stats(o)
        x1 = x_ref[...] + ohat * gp_ref[...]
        x1_ref[...] = x1
        xhat, _ = _rms_stats(x1)
        h1_ref[...] = (xhat * gn_ref[...]).astype(BF16)

    row = lambda c: pl.BlockSpec((t, c), lambda i: (i, 0))
    vec = pl.BlockSpec((1, d), lambda i: (0, 0))
    return pl.pallas_call(
        body, name=name, grid=(s // t,),
        in_specs=[row(k), pl.BlockSpec((k, d), lambda i: (0, 0)), row(d), vec, vec],
        out_specs=[row(d), row(d), row(d)],
        out_shape=[jax.ShapeDtypeStruct((s, d), F32), jax.ShapeDtypeStruct((s, d), F32),
                   jax.ShapeDtypeStruct((s, d), BF16)],
        compiler_params=_cp("parallel"))(y, w, x, g_post, g_pre_next)


def _mm_out_odd(y, w, x1, g_post, target, name):
    s, k = y.shape
    d = w.shape[1]
    t = ROW_TILE

    def body(y_ref, w_ref, x_ref, gp_ref, tg_ref, do_ref, dx_ref, loss_ref, dgp_ref):
        first = pl.program_id(0) == 0
        o = jnp.dot(y_ref[...], w_ref[...], preferred_element_type=F32)
        ohat, r = _rms_stats(o)
        gp = gp_ref[...]
        diff = x_ref[...] + ohat * gp - tg_ref[...]
        part = 0.5 * jnp.sum(jnp.mean(diff * diff, axis=-1, keepdims=True), axis=0, keepdims=True)
        _acc_rows(loss_ref, first, jnp.broadcast_to(part, loss_ref.shape))
        dx2 = diff * (1.0 / d)
        dx_ref[...] = dx2
        do, dgp = _rms_bwd(dx2, ohat, r, gp)
        do_ref[...] = do.astype(BF16)
        _acc_rows(dgp_ref, first, dgp)

    row = lambda c: pl.BlockSpec((t, c), lambda i: (i, 0))
    vec = pl.BlockSpec((1, d), lambda i: (0, 0))
    return pl.pallas_call(
        body, name=name, grid=(s // t,),
        in_specs=[row(k), pl.BlockSpec((k, d), lambda i: (0, 0)), row(d), vec, row(d)],
        out_specs=[row(d), row(d), pl.BlockSpec((8, LANES), lambda i: (0, 0)), vec],
        out_shape=[jax.ShapeDtypeStruct((s, d), BF16), jax.ShapeDtypeStruct((s, d), F32),
                   jax.ShapeDtypeStruct((8, LANES), F32), jax.ShapeDtypeStruct((1, d), F32)],
        compiler_params=_cp("arbitrary"))(y, w, x1, g_post, target)


def _layer_norm(d1, cg, cb):
    mu = jnp.mean(d1, axis=-1, keepdims=True)
    cen = d1 - mu
    rstd = lax.rsqrt(jnp.mean(cen * cen, axis=-1, keepdims=True) + EPS)
    n = cen * rstd
    return n, rstd, n * cg + cb


SUBLANES = 8
CONV_ROWS = 64


def _make_shifts(pad_ref, cs, sh_ref):
    rows = sh_ref.shape[1]
    for r in range(1, SUBLANES):
        sh_ref[r - 1] = pad_ref[r:r + rows, cs]


def _by_shift(taps, base, sign=1):
    return sorted(range(taps), key=lambda k: ((sign * (base + k)) % SUBLANES, k))


def _window(pad_ref, cs, sh_ref, off, t):
    m, r = divmod(off, SUBLANES)
    if r == 0:
        return pad_ref[SUBLANES * m:SUBLANES * m + t, cs]
    return sh_ref[r - 1, SUBLANES * m:SUBLANES * m + t, :]


def _odd_mix_fwd(p, sconv_w, dconv_w, dconv_b, cnorm_g, cnorm_b, d, name):
    s = p.shape[0]
    w = d // 2
    k3, k31 = sconv_w.shape[0], dconv_w.shape[0]
    t, hb = ROW_TILE, CONV_HALO
    assert hb >= k31 - 1 and w % LANES == 0

    def body(p_ref, ph_ref, w3_ref, w31_ref, b31_ref, cg_ref, cb_ref, y_ref, s3_ref, d1_ref, mpad, dpad, sh_ref):
        i = pl.program_id(0)
        mpad[0:hb, :] = jnp.where(i > 0, ph_ref[:, 2 * w:3 * w] * ph_ref[:, 0:w], 0.0)
        mpad[hb:, :] = p_ref[:, 2 * w:3 * w] * p_ref[:, 0:w]
        dpad[0:hb, :] = jnp.where(i > 0, ph_ref[:, 3 * w:4 * w] * _sigmoid(ph_ref[:, 4 * w:5 * w]), 0.0)
        dpad[hb:, :] = p_ref[:, 3 * w:4 * w] * _sigmoid(p_ref[:, 4 * w:5 * w])
        for c0 in range(0, w, LANES):
            cs = slice(c0, c0 + LANES)
            acc = jnp.zeros((t, LANES), F32)
            for kk in range(k3):
                acc = acc + w3_ref[kk:kk + 1, cs] * mpad[hb - (k3 - 1) + kk:hb - (k3 - 1) + kk + t, cs]
            s3_ref[:, cs] = acc
            _make_shifts(dpad, cs, sh_ref)
            for r0 in range(0, t, CONV_ROWS):
                acc = jnp.zeros((CONV_ROWS, LANES), F32)
                for kk in _by_shift(k31, hb - (k31 - 1)):
                    acc = acc + w31_ref[kk:kk + 1, cs] * _window(dpad, cs, sh_ref, hb - (k31 - 1) + kk + r0, CONV_ROWS)
                d1_ref[r0:r0 + CONV_ROWS, cs] = acc + b31_ref[:, cs]
        _, _, d2 = _layer_norm(d1_ref[...], cg_ref[...], cb_ref[...])
        y_ref[:, :w] = (p_ref[:, w:2 * w] * s3_ref[...] * _silu(p_ref[:, 5 * w:6 * w])).astype(BF16)
        y_ref[:, w:] = (_silu(d2) * _silu(p_ref[:, 6 * w:7 * w])).astype(BF16)

    row = lambda c: pl.BlockSpec((t, c), lambda i: (i, 0))
    full = lambda a: pl.BlockSpec(a.shape, lambda i: (0, 0))
    return pl.pallas_call(
        body, name=name, grid=(s // t,),
        in_specs=[row(7 * w),
                  pl.BlockSpec((hb, 5 * w), lambda i: (jnp.maximum(i * (t // hb) - 1, 0), 0)),
                  full(sconv_w), full(dconv_w), full(dconv_b), full(cnorm_g), full(cnorm_b)],
        out_specs=[row(d), row(w), row(w)],
        out_shape=[jax.ShapeDtypeStruct((s, d), BF16), jax.ShapeDtypeStruct((s, w), F32),
                   jax.ShapeDtypeStruct((s, w), F32)],
        scratch_shapes=[pltpu.VMEM((hb + t, w), F32)] * 2 + [pltpu.VMEM((SUBLANES - 1, hb + t - SUBLANES, LANES), F32)],
        compiler_params=_cp("parallel"))(p, p, sconv_w, dconv_w, dconv_b, cnorm_g, cnorm_b)


def _odd_bwd_rows(p, s3, d1, dy, cnorm_g, cnorm_b, d, name):
    s = p.shape[0]
    w = d // 2
    t = ROW_TILE

    def body(bc_ref, g1_ref, g2_ref, s3_ref, d1_ref, dy_ref, cg_ref, cb_ref,
             dbc_ref, dg_ref, ds3_ref, dd1_ref, dcg_ref, dcb_ref, db_ref):
        first = pl.program_id(0) == 0
        g1, g2 = g1_ref[...], g2_ref[...]
        bc, s3v = bc_ref[...], s3_ref[...]
        dy1, dy2 = dy_ref[:, :w], dy_ref[:, w:]
        n, rstd, d2 = _layer_norm(d1_ref[...], cg_ref[...], cb_ref[...])
        dg_ref[:, :w] = (dy1 * bc * s3v * _dsilu(g1)).astype(BF16)
        dg_ref[:, w:] = (dy2 * _silu(d2) * _dsilu(g2)).astype(BF16)
        dco = dy1 * _silu(g1)
        dbc_ref[...] = (dco * s3v).astype(BF16)
        ds3_ref[...] = dco * bc
        dd2 = dy2 * _silu(g2) * _dsilu(d2)
        _acc_rows(dcb_ref, first, jnp.sum(dd2, axis=0, keepdims=True))
        _acc_rows(dcg_ref, first, jnp.sum(dd2 * n, axis=0, keepdims=True))
        dn = dd2 * cg_ref[...]
        dd1 = rstd * (dn - jnp.mean(dn, axis=-1, keepdims=True) - n * jnp.mean(dn * n, axis=-1, keepdims=True))
        dd1_ref[...] = dd1
        _acc_rows(db_ref, first, jnp.sum(dd1, axis=0, keepdims=True))

    col = lambda j: pl.BlockSpec((t, w), lambda i: (i, j))
    row = lambda c: pl.BlockSpec((t, c), lambda i: (i, 0))
    vec = pl.BlockSpec((1, w), lambda i: (0, 0))
    return pl.pallas_call(
        body, name=name, grid=(s // t,),
        in_specs=[col(1), col(5), col(6), row(w), row(w), row(d), vec, vec],
        out_specs=[row(w), row(d), row(w), row(w), vec, vec, vec],
        out_shape=[jax.ShapeDtypeStruct((s, w), BF16), jax.ShapeDtypeStruct((s, d), BF16),
                   jax.ShapeDtypeStruct((s, w), F32), jax.ShapeDtypeStruct((s, w), F32)]
        + [jax.ShapeDtypeStruct((1, w), F32)] * 3,
        compiler_params=_cp("arbitrary"))(p, p, p, s3, d1, dy, cnorm_g, cnorm_b)


def _odd_bwd_conv(p, ds3, dd1, sconv_w, dconv_w, d, name):
    s = p.shape[0]
    w = d // 2
    k3, k31 = sconv_w.shape[0], dconv_w.shape[0]
    t, hb, ha = ROW_TILE, CONV_HALO, 8
    nt = s // t
    assert hb >= k31 - 1 and ha >= k3 - 1

    def body(hc_ref, cc_ref, ga_ref, gb_ref, hch_ref, cch_ref, gah_ref, gbh_ref, ds3_ref, ds3h_ref, dd1_ref, dd1h_ref,
             w3_ref, w31_ref, dhc_ref, dcc_ref, dga_ref, dgb_ref, dw3_ref, dw31_ref, mpad, dpad, s3pad, d1pad, sh_ref):
        i = pl.program_id(0)
        first = i == 0
        last = i == nt - 1
        mpad[0:hb, :] = jnp.where(i > 0, cch_ref[...] * hch_ref[...], 0.0)
        mpad[hb:, :] = cc_ref[...] * hc_ref[...]
        dpad[0:hb, :] = jnp.where(i > 0, gah_ref[...] * _sigmoid(gbh_ref[...]), 0.0)
        dpad[hb:, :] = ga_ref[...] * _sigmoid(gb_ref[...])
        s3pad[0:t, :] = ds3_ref[...]
        s3pad[t:, :] = jnp.where(last, 0.0, ds3h_ref[...])
        d1pad[0:t, :] = dd1_ref[...]
        d1pad[t:, :] = jnp.where(last, 0.0, dd1h_ref[...])

        @pl.when(first)
        def _():
            dw3_ref[...] = jnp.zeros_like(dw3_ref)
            dw31_ref[...] = jnp.zeros_like(dw31_ref)

        def fold(v):
            return jnp.sum(v.reshape(v.shape[0] // SUBLANES, SUBLANES, LANES), axis=0)

        groups = range(0, t, CONV_ROWS)
        for c0 in range(0, w, LANES):
            cs = slice(c0, c0 + LANES)
            ds3v = s3pad[0:t, cs]
            dm = jnp.zeros((t, LANES), F32)
            for kk in range(k3):
                dm = dm + w3_ref[kk:kk + 1, cs] * s3pad[k3 - 1 - kk:k3 - 1 - kk + t, cs]
                off = hb - (k3 - 1) + kk
                dw3_ref[SUBLANES * kk:SUBLANES * (kk + 1), cs] += fold(ds3v * mpad[off:off + t, cs])
            dcc_ref[:, cs] = (dm * hc_ref[:, cs]).astype(BF16)
            dhc_ref[:, cs] = (dm * cc_ref[:, cs]).astype(BF16)
            _make_shifts(d1pad, cs, sh_ref)
            for r0 in groups:
                rows = slice(r0, r0 + CONV_ROWS)
                dd0 = jnp.zeros((CONV_ROWS, LANES), F32)
                for kk in _by_shift(k31, -(k31 - 1), -1):
                    dd0 = dd0 + w31_ref[kk:kk + 1, cs] * _window(d1pad, cs, sh_ref, k31 - 1 - kk + r0, CONV_ROWS)
                sgb = _sigmoid(gb_ref[rows, cs])
                dga_ref[rows, cs] = (dd0 * sgb).astype(BF16)
                dgb_ref[rows, cs] = (dd0 * ga_ref[rows, cs] * sgb * (1.0 - sgb)).astype(BF16)
            _make_shifts(dpad, cs, sh_ref)
            for kk in _by_shift(k31, hb - (k31 - 1)):
                part = jnp.zeros((SUBLANES, LANES), F32)
                for r0 in groups:
                    part = part + fold(d1pad[r0:r0 + CONV_ROWS, cs]
                                       * _window(dpad, cs, sh_ref, hb - (k31 - 1) + kk + r0, CONV_ROWS))
                dw31_ref[SUBLANES * kk:SUBLANES * (kk + 1), cs] += part

    col = lambda j: pl.BlockSpec((t, w), lambda i: (i, j))
    pre = lambda j: pl.BlockSpec((hb, w), lambda i: (jnp.maximum(i * (t // hb) - 1, 0), j))
    row = pl.BlockSpec((t, w), lambda i: (i, 0))
    post = lambda h: pl.BlockSpec((h, w), lambda i: (jnp.minimum((i + 1) * (t // h), s // h - 1), 0))
    full = lambda a: pl.BlockSpec(a.shape, lambda i: (0, 0))
    dhc, dcc, dga, dgb, dw3, dw31 = pl.pallas_call(
        body, name=name, grid=(nt,),
        in_specs=[col(0), col(2), col(3), col(4), pre(0), pre(2), pre(3), pre(4),
                  row, post(ha), row, post(hb), full(sconv_w), full(dconv_w)],
        out_specs=[row, row, row, row, pl.BlockSpec((SUBLANES * k3, w), lambda i: (0, 0)),
                   pl.BlockSpec((SUBLANES * k31, w), lambda i: (0, 0))],
        out_shape=[jax.ShapeDtypeStruct((s, w), BF16)] * 4
        + [jax.ShapeDtypeStruct((SUBLANES * k3, w), F32), jax.ShapeDtypeStruct((SUBLANES * k31, w), F32)],
        scratch_shapes=[pltpu.VMEM((hb + t, w), F32)] * 2 + [pltpu.VMEM((t + ha, w), F32), pltpu.VMEM((t + hb, w), F32),
                                                             pltpu.VMEM((SUBLANES - 1, hb + t - SUBLANES, LANES), F32)],
        compiler_params=_cp("arbitrary"))(p, p, p, p, p, p, p, p, ds3, ds3, dd1, dd1, sconv_w, dconv_w)
    return dhc, dcc, dga, dgb, jnp.sum(dw3.reshape(k3, SUBLANES, w), axis=1), jnp.sum(dw31.reshape(k31, SUBLANES, w), axis=1)


def _mm_in_bwd(dp, w3, x, g_pre, dres, post, name, comm=None):
    s = dp.shape[0]
    nsh, d, ns = w3.shape
    t = ROW_TILE
    nt = s // t
    row = pl.BlockSpec((t, d), lambda i, k: (i, 0))
    vec = pl.BlockSpec((1, d), lambda i, k: (0, 0))
    in_specs = [pl.BlockSpec((t, ns), lambda i, k: (i, k)), pl.BlockSpec((None, d, ns), lambda i, k: (k, 0, 0)), row, vec, row]
    out_specs = [row, vec]
    out_shape = [jax.ShapeDtypeStruct((s, d), F32), jax.ShapeDtypeStruct((1, d), F32)]
    args = [dp, w3, x, g_pre, dres]
    if post is not None:
        in_specs += [row, vec]
        out_specs += [row, vec]
        out_shape += [jax.ShapeDtypeStruct((s, d), BF16), jax.ShapeDtypeStruct((1, d), F32)]
        args += list(post)
    host = _Host(comm, in_specs, out_specs, out_shape, [pltpu.VMEM((t, d), F32)])

    def body(*refs):
        ins, outs, (acc_ref,) = host.split(refs)
        dp_ref, w_ref, x_ref, g_ref, dr_ref = ins[:5]
        dx_ref, dg_ref = outs[:2]
        kk = pl.program_id(1)
        first = pl.program_id(0) == 0
        step = pl.program_id(0) * nsh + kk
        host.before(step, nt * nsh)
        part = _nt(dp_ref[...], w_ref[...])

        @pl.when(kk == 0)
        def _():
            acc_ref[...] = part

        @pl.when(kk > 0)
        def _():
            acc_ref[...] += part

        @pl.when(kk == nsh - 1)
        def _():
            xhat, r = _rms_stats(x_ref[...])
            dxn, dg = _rms_bwd(acc_ref[...], xhat, r, g_ref[...])
            dx = dr_ref[...] + dxn
            dx_ref[...] = dx
            _acc_rows(dg_ref, first, dg)
            if post is not None:
                ohat, ro = _rms_stats(ins[5][...])
                do, dgp = _rms_bwd(dx, ohat, ro, ins[6][...])
                outs[2][...] = do.astype(BF16)
                _acc_rows(outs[3], first, dgp)

        host.after(step, nt * nsh)

    res = pl.pallas_call(
        body, name=name, grid=(nt, nsh), in_specs=host.in_specs, out_specs=host.out_specs, out_shape=host.out_shape,
        scratch_shapes=host.scratch, input_output_aliases=host.aliases,
        compiler_params=_cp("arbitrary", "arbitrary"))(*args, *host.args)
    return host.results(res)


def _half_add(g, r1, c_arr, name):
    nsh, rows, ns = g.shape
    h = rows // 2
    tr = min(ROW_TILE, h)
    per = h // tr

    def body(c_ref, g_ref, r_ref, o_ref):
        o_ref[...] = (g_ref[...].astype(F32) + r_ref[...].astype(F32)).astype(BF16)

    spec = pl.BlockSpec((None, tr, ns), lambda s, r, c: (s, r, 0))
    return pl.pallas_call(
        body, name=name,
        grid_spec=pltpu.PrefetchScalarGridSpec(
            num_scalar_prefetch=1, grid=(nsh, per),
            in_specs=[pl.BlockSpec((None, tr, ns), lambda s, r, c: (s, c[0] * per + r, 0)), spec], out_specs=spec),
        out_shape=jax.ShapeDtypeStruct((nsh, h, ns), BF16), compiler_params=_cp("parallel", "parallel"))(c_arr, g, r1)


def _sum_chips(hh, r2, mc_arr, name):
    _, h, ns = hh.shape
    tr = min(ROW_TILE, h)
    per = h // tr

    def body(mc_ref, h_ref, a_ref, b_ref, c_ref, o_ref):
        o_ref[...] = ((h_ref[...].astype(F32) + a_ref[...].astype(F32)) + b_ref[...].astype(F32)) + c_ref[...].astype(F32)

    got = lambda k: pl.BlockSpec((None, tr, ns), lambda r, mc: (k, r, 0))
    return pl.pallas_call(
        body, name=name,
        grid_spec=pltpu.PrefetchScalarGridSpec(
            num_scalar_prefetch=1, grid=(per,),
            in_specs=[pl.BlockSpec((None, tr, ns), lambda r, mc: (mc[0], r, 0)), got(0), got(1), got(2)],
            out_specs=pl.BlockSpec((tr, ns), lambda r, mc: (mc[1] * per + r, 0))),
        out_shape=jax.ShapeDtypeStruct((2 * h, ns), F32), compiler_params=_cp("parallel"))(mc_arr, hh, r2, r2, r2)


def _add2(a, b, name):
    def body(a_ref, b_ref, o_ref):
        o_ref[...] = a_ref[...] + b_ref[...]

    return pl.pallas_call(body, name=name, out_shape=jax.ShapeDtypeStruct(a.shape, a.dtype), compiler_params=_cp())(a, b)


def _sum_chips_ordered(s2, r2, mc_arr, name):
    rows, w = s2.shape
    rh = rows // 2

    def body(mc_ref, s_ref, a_ref, b_ref, c_ref, o_ref):
        me = mc_ref[0]
        acc = None
        for j in range(N_CHIPS):
            rel = jnp.bitwise_xor(me, j)
            v = jnp.where(rel == 0, s_ref[...], jnp.where(rel == 2, a_ref[...], jnp.where(rel == 1, b_ref[...], c_ref[...])))
            acc = v if acc is None else acc + v
        o_ref[...] = acc

    got = lambda k: pl.BlockSpec((None, rh, w), lambda i, mc: (k, 0, 0))
    return pl.pallas_call(
        body, name=name,
        grid_spec=pltpu.PrefetchScalarGridSpec(
            num_scalar_prefetch=1, grid=(1,),
            in_specs=[pl.BlockSpec((rh, w), lambda i, mc: (mc[1], 0)), got(0), got(1), got(2)],
            out_specs=pl.BlockSpec((rh, w), lambda i, mc: (mc[1], 0))),
        out_shape=jax.ShapeDtypeStruct((rows, w), F32), compiler_params=_cp("arbitrary"))(mc_arr, s2, r2, r2, r2)


def _adamw(w, g, m, v, name):
    r, c = w.shape
    tr = ROW_TILE if r % ROW_TILE == 0 else r
    c1 = 1.0 / (1.0 - ADAM_B1 ** ADAM_STEP)
    c2 = 1.0 / (1.0 - ADAM_B2 ** ADAM_STEP)

    def body(w_ref, g_ref, m_ref, v_ref, go_ref, d_ref, nm_ref, nv_ref):
        gv = g_ref[...]
        go_ref[...] = gv
        nm = ADAM_B1 * m_ref[...] + (1.0 - ADAM_B1) * gv
        nv = ADAM_B2 * v_ref[...] + (1.0 - ADAM_B2) * (gv * gv)
        nm_ref[...] = nm
        nv_ref[...] = nv
        d_ref[...] = -ADAM_LR * ((nm * c1) / (jnp.sqrt(nv * c2) + ADAM_EPS) + ADAM_WD * w_ref[...])

    spec = pl.BlockSpec((tr, c), lambda i: (i, 0))
    return pl.pallas_call(
        body, name=name, grid=(r // tr,), in_specs=[spec] * 4, out_specs=[spec] * 4,
        out_shape=[jax.ShapeDtypeStruct((r, c), F32)] * 4, compiler_params=_cp("parallel"))(w, g, m, v)


def _gather_weights(bigs, pool_w, pack_w, pack_d, name):
    nb = len(bigs)
    smalls = [pool_w, pack_w, pack_d]
    q, cw, cd = pool_w.shape[1], pack_w.shape[1], pack_d.shape[1]
    halves = [b.shape[1] // 2 for b in bigs]

    def body(*refs):
        srcs, dsts = refs[:nb + 3], refs[nb + 3:2 * (nb + 3)]
        ssem, rsem, lsem = refs[2 * (nb + 3):]
        x, y, c, me, chips, sib = _place()

        def big_dst(a, chip, half):
            return dsts[a].at[chip, pl.ds(half * halves[a], halves[a])]

        def small_dst(n, chip):
            if n == 0:
                return dsts[nb].at[:, pl.ds(chip * q, q), :]
            return dsts[nb + n].at[:, pl.ds(chip * (cw if n == 1 else cd), cw if n == 1 else cd)]

        local = [pltpu.make_async_copy(srcs[nb + n], small_dst(n, me), lsem.at[n]) for n in range(3)]
        for cp in local:
            cp.start()
        sends = []
        for a in range(nb):
            for k, chip in enumerate(chips):
                cp = _rcopy(srcs[a].at[me, pl.ds(c * halves[a], halves[a])], big_dst(a, me, c),
                            ssem.at[6 * a + k], rsem.at[6 * a + k], (*chip, c))
                cp.start()
                sends.append(cp)
        for n in range(3):
            for k, chip in enumerate(chips):
                cp = _rcopy(srcs[nb + n], small_dst(n, me), ssem.at[6 * nb + 3 * n + k], rsem.at[6 * nb + 3 * n + k], (*chip, c))
                cp.start()
                sends.append(cp)
        for a in range(nb):
            for k, chip in enumerate(chips):
                ref = big_dst(a, 2 * chip[0] + chip[1], c)
                _rcopy(ref, ref, ssem.at[6 * a + k], rsem.at[6 * a + k], (*chip, c)).wait_recv()
                cp = _rcopy(ref, ref, ssem.at[6 * a + 3 + k], rsem.at[6 * a + 3 + k], sib)
                cp.start()
                sends.append(cp)
        for a in range(nb):
            for k, chip in enumerate(chips):
                ref = big_dst(a, 2 * chip[0] + chip[1], 1 - c)
                _rcopy(ref, ref, ssem.at[6 * a + 3 + k], rsem.at[6 * a + 3 + k], sib).wait_recv()
        for n in range(3):
            for k, chip in enumerate(chips):
                ref = small_dst(n, 2 * chip[0] + chip[1])
                _rcopy(ref, ref, ssem.at[6 * nb + 3 * n + k], rsem.at[6 * nb + 3 * n + k], (*chip, c)).wait_recv()
        for cp in sends:
            cp.wait_send()
        for cp in local:
            cp.wait()

    nsem = 6 * nb + 9
    out_shape = [jax.ShapeDtypeStruct(b.shape, b.dtype) for b in bigs]
    out_shape += [jax.ShapeDtypeStruct((pool_w.shape[0], N_CHIPS * q, pool_w.shape[2]), pool_w.dtype),
                  jax.ShapeDtypeStruct((pack_w.shape[0], N_CHIPS * cw), pack_w.dtype),
                  jax.ShapeDtypeStruct((pack_d.shape[0], N_CHIPS * cd), pack_d.dtype)]
    return pl.pallas_call(
        body, name=name, in_specs=[ANY] * (nb + 3), out_specs=[ANY] * (nb + 3), out_shape=out_shape,
        input_output_aliases={a: a for a in range(nb)},
        scratch_shapes=[pltpu.SemaphoreType.DMA((nsem,)), pltpu.SemaphoreType.DMA((nsem,)), pltpu.SemaphoreType.DMA((3,))],
        compiler_params=pltpu.CompilerParams(has_side_effects=True))(*bigs, *smalls)


def _swap_with_sibling(grads, wholes, name):
    n, nw = len(grads), len(wholes)
    halves = [g.shape[1] // 2 for g in grads]

    def body(*refs):
        srcs, dsts = refs[:n + nw], refs[n + nw:2 * (n + nw)]
        ssem, rsem = refs[2 * (n + nw):]
        x, y, c, me, chips, sib = _place()
        cps = [_rcopy(srcs[a].at[:, pl.ds((1 - c) * halves[a], halves[a]), :], dsts[a], ssem.at[a], rsem.at[a], sib)
               for a in range(n)]
        cps += [_rcopy(srcs[a], dsts[a], ssem.at[a], rsem.at[a], sib) for a in range(n, n + nw)]
        for cp in cps:
            cp.start()
        for cp in cps:
            cp.wait_recv()
        for cp in cps:
            cp.wait_send()

    out_shape = [jax.ShapeDtypeStruct((g.shape[0], h, g.shape[2]), g.dtype) for g, h in zip(grads, halves)]
    out_shape += [jax.ShapeDtypeStruct(w.shape, w.dtype) for w in wholes]
    return pl.pallas_call(
        body, name=name, in_specs=[ANY] * (n + nw), out_specs=[ANY] * (n + nw), out_shape=out_shape,
        scratch_shapes=[pltpu.SemaphoreType.DMA((n + nw,)), pltpu.SemaphoreType.DMA((n + nw,))],
        compiler_params=pltpu.CompilerParams(has_side_effects=True))(*grads, *wholes)


def _scatter_to_chips(halves_in, small, name):
    n = len(halves_in)
    rh = small.shape[0] // 2

    def body(*refs):
        srcs, dsts = refs[:n + 1], refs[n + 1:2 * (n + 1)]
        ssem, rsem = refs[2 * (n + 1):]
        x, y, c, me, chips, sib = _place()
        cps = []
        for a in range(n + 1):
            for k, chip in enumerate(chips):
                src = srcs[a].at[2 * chip[0] + chip[1]] if a < n else srcs[a].at[pl.ds(c * rh, rh)]
                cps.append(_rcopy(src, dsts[a].at[k], ssem.at[3 * a + k], rsem.at[3 * a + k], (*chip, c)))
        for cp in cps:
            cp.start()
        for cp in cps:
            cp.wait_recv()
        for cp in cps:
            cp.wait_send()

    out_shape = [jax.ShapeDtypeStruct((3,) + h.shape[1:], h.dtype) for h in halves_in]
    out_shape.append(jax.ShapeDtypeStruct((3, rh, small.shape[1]), small.dtype))
    return pl.pallas_call(
        body, name=name, in_specs=[ANY] * (n + 1), out_specs=[ANY] * (n + 1), out_shape=out_shape,
        scratch_shapes=[pltpu.SemaphoreType.DMA((3 * (n + 1),)), pltpu.SemaphoreType.DMA((3 * (n + 1),))],
        compiler_params=pltpu.CompilerParams(has_side_effects=True))(*halves_in, small)


def _join_halves(parts, name):
    n = len(parts)

    def body(*refs):
        srcs, dsts = refs[:n], refs[n:2 * n]
        ssem, rsem = refs[2 * n:]
        x, y, c, me, chips, sib = _place()
        cps = []
        for a in range(n):
            h = srcs[a].shape[0] // 2
            cps.append(_rcopy(srcs[a].at[pl.ds(c * h, h)], dsts[a].at[pl.ds(c * h, h)], ssem.at[a], rsem.at[a], sib))
        for cp in cps:
            cp.start()
        for a in range(n):
            h = srcs[a].shape[0] // 2
            theirs = dsts[a].at[pl.ds((1 - c) * h, h)]
            _rcopy(theirs, theirs, ssem.at[a], rsem.at[a], sib).wait_recv()
        for cp in cps:
            cp.wait_send()

    out_shape = [jax.ShapeDtypeStruct(p.shape, p.dtype) for p in parts]
    return pl.pallas_call(
        body, name=name, in_specs=[ANY] * n, out_specs=[ANY] * n, out_shape=out_shape,
        input_output_aliases={a: a for a in range(n)},
        scratch_shapes=[pltpu.SemaphoreType.DMA((n,)), pltpu.SemaphoreType.DMA((n,))],
        compiler_params=pltpu.CompilerParams(has_side_effects=True))(*parts)


def _pad_rows(a, rows):
    return jnp.pad(a, ((0, rows - a.shape[0]), (0, 0)))


def _stack_rows(parts, multiple):
    padded = [_pad_rows(p, -(-p.shape[0] // 8) * 8) for p in parts]
    starts, at = [], 0
    for p in padded:
        starts.append(at)
        at += p.shape[0]
    total = -(-at // multiple) * multiple
    if total > at:
        padded.append(jnp.zeros((total - at, parts[0].shape[1]), parts[0].dtype))
    return jnp.concatenate(padded, axis=0), starts


def kernel(x, ln_pre_even, w_in_even, pool_w, pool_scale, w_out_even, ln_post_even, ln_pre_odd, w_in_odd, sconv_w, dconv_w, dconv_b, cnorm_g, cnorm_b, w_out_odd, ln_post_odd, loss_target, m_ln_pre_even, m_w_in_even, m_pool_w, m_pool_scale, m_w_out_even, m_ln_post_even, m_ln_pre_odd, m_w_in_odd, m_sconv_w, m_dconv_w, m_dconv_b, m_cnorm_g, m_cnorm_b, m_w_out_odd, m_ln_post_odd, v_ln_pre_even, v_w_in_even, v_pool_w, v_pool_scale, v_w_out_even, v_ln_post_even, v_ln_pre_odd, v_w_in_odd, v_sconv_w, v_dconv_w, v_dconv_b, v_cnorm_g, v_cnorm_b, v_w_out_odd, v_ln_post_odd):
    _, s, d = x.shape
    half = d // 2
    cw = half // N_CHIPS
    ng, q, gd = pool_w.shape[1:]
    k3, k31 = sconv_w.shape[1], dconv_w.shape[1]
    x2d, tgt = x[0], loss_target[0]
    me = 2 * lax.axis_index("x") + lax.axis_index("y")
    core = lax.axis_index("c")
    c_arr = jnp.reshape(core, (1,)).astype(jnp.int32)
    me_arr = jnp.reshape(me, (1,)).astype(jnp.int32)
    mc_arr = jnp.stack([me, core]).astype(jnp.int32)

    shards = [w_in_even[0], w_out_even[0], w_in_odd[0], w_out_odd[0]]
    slabs = [_cast_bf16_own_slab(w, me_arr, f"cast_w{n}") for n, w in enumerate(shards)]
    pool_w_b = _cast_bf16(pool_w[0].reshape(ng * q, gd), "cast_pool_w").reshape(ng, q, gd)
    pack_w, at_w = _stack_rows([sconv_w[0], dconv_w[0], dconv_b, cnorm_g, cnorm_b], 8)
    pack_d, at_d = _stack_rows([ln_pre_odd, ln_post_odd], 8)
    win_e, pool_w_f, pack_w_f, pack_d_f = _gather_weights(slabs[:1], pool_w_b, pack_w, pack_d, "gather_first")
    sconv_f = pack_w_f[at_w[0]:at_w[0] + k3]
    dconv_f = pack_w_f[at_w[1]:at_w[1] + k31]
    dconv_b_f, cnorm_g_f, cnorm_b_f = (pack_w_f[at_w[n]:at_w[n] + 1] for n in (2, 3, 4))
    ln_pre_odd_f = pack_d_f[at_d[0]:at_d[0] + 1]
    ln_post_odd_f = pack_d_f[at_d[1]:at_d[1] + 1]

    def reduce_half(g, name):
        (got,) = _swap_with_sibling([g], [], "swap_" + name)
        return _half_add(g, got, c_arr, "half_add_" + name)

    h0 = _rms_fwd(x2d, ln_pre_even, "rms_pre_even")
    plans = _Multi([_GatherPlan([slabs[1]]), _GatherPlan([slabs[2]], (0, 1, 4))])
    plans.late = True
    p_e, extra = _mm_nn(h0, win_e, "proj_in_even", plans)
    (wout_e,), (win_o,) = plans.results(extra)
    wout_e = wout_e.reshape(d, d)
    plans = _GatherPlan([win_o], (1, 4, 4))
    plans.late = True
    att, ltot, (win_o,) = _sba_fwd(p_e, half, "sba_fwd", plans)
    y_e = _even_mix_fwd(p_e, att, pool_w_f, pool_scale, d, "even_mix_fwd")
    o_e, x1, h1 = _mm_out_even(y_e, wout_e, x2d, ln_post_even, ln_pre_odd_f, "proj_out_even")
    p_o, (wout_o,) = _mm_nn(h1, win_o, "proj_in_odd", _GatherPlan([slabs[3]]))
    wout_o = wout_o.reshape(d, d)
    y_o, s3, d1 = _odd_mix_fwd(p_o, sconv_f, dconv_f, dconv_b_f, cnorm_g_f, cnorm_b_f, d, "odd_mix_fwd")
    do_o, dx2, loss_blk, dln_post_odd = _mm_out_odd(y_o, wout_o, x1, ln_post_odd_f, tgt, "proj_out_odd_loss")

    dy_o = _mm_nt(do_o, wout_o, "dy_odd")
    g_wout_o = _mm_tn(y_o, do_o, 1, "dw_out_odd").reshape(N_CHIPS, d // N_CHIPS, d)
    h_wout_o = reduce_half(g_wout_o, "out_odd")
    dbc, dgate_o, ds3, dd1, dcnorm_g, dcnorm_b, ddconv_b = _odd_bwd_rows(p_o, s3, d1, dy_o, cnorm_g_f, cnorm_b_f, d, "odd_bwd_rows")
    dhc, dcc, dga, dgb, dsconv, ddconv = _odd_bwd_conv(p_o, ds3, dd1, sconv_f, dconv_f, d, "odd_bwd_conv")
    dp_o = jnp.concatenate([dhc, dbc, dcc, dga, dgb, dgate_o], axis=1)
    g_win_o = _mm_tn(h1, dp_o, N_CHIPS, "dw_in_odd")
    h_win_o = reduce_half(g_win_o, "in_odd")
    plans = _Multi([_ScatterPlan([h_wout_o]), _ScatterPlan([h_win_o], (0, 1, 2))])
    (dx1, dln_pre_odd, do_e, dln_post_even), extra = _mm_in_bwd(
        dp_o, win_o, x1, ln_pre_odd_f, dx2, (o_e, ln_post_even), "dx_odd", plans)
    (s_wout_o,), (s_win_o,) = plans.results(extra)

    dy_e = _mm_nt(do_e, wout_e, "dy_even")
    g_wout_e = _mm_tn(y_e, do_e, 1, "dw_out_even").reshape(N_CHIPS, d // N_CHIPS, d)
    h_wout_e = reduce_half(g_wout_e, "out_even")
    datt, du, dgate_e, dpool_scale, dpool_w = _even_mix_bwd(p_e, att, dy_e, pool_w_f, pool_scale, d, "even_mix_bwd")
    two = lambda v: v.reshape(2, half)
    small_parts = [dpool_scale, two(dln_post_even), two(dln_pre_odd), two(dln_post_odd),
                   dsconv, ddconv, ddconv_b, dcnorm_g, dcnorm_b, dpool_w.reshape(gd, half)]
    small, at_s = _stack_rows(small_parts, 16)
    (small1,) = _swap_with_sibling([], [small], "swap_small")
    small2 = _add2(small, small1, "small_add")
    plans = _Multi([_ScatterPlan([h_win_o], (1, 2, 2), into=[s_win_o]), _ScatterPlan([h_wout_e]), _ShareHalfPlan([small2])])
    dq, dk, dv, extra = _sba_bwd(p_e, ltot, datt, half, "sba_bwd", plans)
    (s_win_o,), (s_wout_e,), (small_got,) = plans.results(extra)
    dp_e = jnp.concatenate([dq, dk, dv, du, dgate_e], axis=1)
    g_win_e = _mm_tn(h0, dp_e, N_CHIPS, "dw_in_even")
    h_win_e = reduce_half(g_win_e, "in_even")
    (grad_x, dln_pre_even), (s_win_e,) = _mm_in_bwd(dp_e, win_e, x2d, ln_pre_even, dx1, None, "dx_even", _ScatterPlan([h_win_e]))

    last, at_l = _stack_rows([two(dln_pre_even), jnp.pad(loss_blk[0:1], ((0, 0), (0, half - LANES)))], 16)
    (last1,) = _swap_with_sibling([], [last], "swap_last")
    last2 = _add2(last, last1, "last_add")
    (last_got,) = _scatter_to_chips([], last2, "scatter_last")
    pairs = [(h_win_e, s_win_e), (h_wout_e, s_wout_e), (h_win_o, s_win_o), (h_wout_o, s_wout_o)]
    parts = [_sum_chips(h, r, mc_arr, f"sum_chips{n}") for n, (h, r) in enumerate(pairs)]
    parts.append(_sum_chips_ordered(small2, small_got, mc_arr, "small_sum"))
    parts.append(_sum_chips_ordered(last2, last_got, mc_arr, "last_sum"))
    gw_in_e, gw_out_e, gw_in_o, gw_out_o, red, red_last = _join_halves(parts, "join_halves")
    loss = red_last[at_l[1], 0]

    def rows(n, cnt):
        return red[at_s[n]:at_s[n] + cnt]

    def mine(a, width):
        return lax.dynamic_slice_in_dim(a, me * width, width, axis=1)

    quarter = d // N_CHIPS
    g_small = {
        "ln_pre_even": red_last[at_l[0]:at_l[0] + 2].reshape(1, d),
        "pool_scale": rows(0, 1),
        "ln_post_even": rows(1, 2).reshape(1, d),
        "ln_pre_odd": mine(rows(2, 2).reshape(1, d), quarter),
        "ln_post_odd": mine(rows(3, 2).reshape(1, d), quarter),
        "sconv_w": mine(rows(4, k3), cw),
        "dconv_w": mine(rows(5, k31), cw),
        "dconv_b": mine(rows(6, 1), cw),
        "cnorm_g": mine(rows(7, 1), cw),
        "cnorm_b": mine(rows(8, 1), cw),
        "pool_w": lax.dynamic_slice_in_dim(rows(9, gd).reshape(ng, gd, gd), me * q, q, axis=1).reshape(ng * q, gd),
    }
    w2d = {
        "ln_pre_even": ln_pre_even, "w_in_even": w_in_even[0], "pool_w": pool_w[0].reshape(ng * q, gd),
        "pool_scale": pool_scale, "w_out_even": w_out_even[0], "ln_post_even": ln_post_even, "ln_pre_odd": ln_pre_odd,
        "w_in_odd": w_in_odd[0], "sconv_w": sconv_w[0], "dconv_w": dconv_w[0], "dconv_b": dconv_b, "cnorm_g": cnorm_g,
        "cnorm_b": cnorm_b, "w_out_odd": w_out_odd[0], "ln_post_odd": ln_post_odd,
    }
    moments = {
        "ln_pre_even": (m_ln_pre_even, v_ln_pre_even), "w_in_even": (m_w_in_even, v_w_in_even),
        "pool_w": (m_pool_w, v_pool_w), "pool_scale": (m_pool_scale, v_pool_scale),
        "w_out_even": (m_w_out_even, v_w_out_even), "ln_post_even": (m_ln_post_even, v_ln_post_even),
        "ln_pre_odd": (m_ln_pre_odd, v_ln_pre_odd), "w_in_odd": (m_w_in_odd, v_w_in_odd),
        "sconv_w": (m_sconv_w, v_sconv_w), "dconv_w": (m_dconv_w, v_dconv_w), "dconv_b": (m_dconv_b, v_dconv_b),
        "cnorm_g": (m_cnorm_g, v_cnorm_g), "cnorm_b": (m_cnorm_b, v_cnorm_b),
        "w_out_odd": (m_w_out_odd, v_w_out_odd), "ln_post_odd": (m_ln_post_odd, v_ln_post_odd),
    }
    g2d = dict(g_small, w_in_even=gw_in_e, w_out_even=gw_out_e, w_in_odd=gw_in_o, w_out_odd=gw_out_o)
    grads_out, deltas, new_m, new_v = [], [], [], []
    for name, w in w2d.items():
        m_in, v_in = moments[name]
        shape = m_in.shape
        g_out, delta, nm, nv = _adamw(w, g2d[name], m_in.reshape(w.shape), v_in.reshape(w.shape), "adamw_" + name)
        grads_out.append(g_out.reshape(shape))
        deltas.append(delta.reshape(shape))
        new_m.append(nm.reshape(shape))
        new_v.append(nv.reshape(shape))
    return (loss, grad_x.reshape(x.shape), *grads_out, *deltas, *new_m, *new_v)
```

```python
import functools
import math

import jax
import jax.numpy as jnp
from jax import lax
from jax.experimental import pallas as pl
from jax.experimental.pallas import tpu as pltpu

F32 = jnp.float32
BF16 = jnp.bfloat16
EPS = 1e-6
N_CHIPS = 4
VMEM_LIMIT_V7X = 56 << 20
HEAD_DIM = 128
ATT_BLOCK = 256
POOL_WINDOWS = (2, 4, 8, 16)
ROW_TILE = 256
POOL_HALO = 16
CONV_HALO = 32
LANES = 128
ADAM_LR, ADAM_B1, ADAM_B2, ADAM_EPS, ADAM_WD, ADAM_STEP = 0.001, 0.9, 0.999, 1e-08, 0.01, 10
MESH_ID = pl.DeviceIdType.MESH
ANY = pl.BlockSpec(memory_space=pl.ANY)


def _cp(*sem):
    return pltpu.CompilerParams(dimension_semantics=sem or None, vmem_limit_bytes=VMEM_LIMIT_V7X)


def _pick_tile(n, cap):
    best = None
    for t in range(LANES, min(n, cap) + 1, LANES):
        if n % t == 0:
            best = t
    assert best is not None, (n, cap)
    return best


def _sigmoid(x):
    return 1.0 / (1.0 + jnp.exp(-x))


def _silu(x):
    return x * _sigmoid(x)


def _dsilu(x):
    s = _sigmoid(x)
    return s * (1.0 + x * (1.0 - s))


def _log_sigmoid(z):
    return jnp.minimum(z, 0.0) - jnp.log(1.0 + jnp.exp(-jnp.abs(z)))


def _rms_stats(x):
    r = lax.rsqrt(jnp.mean(x * x, axis=-1, keepdims=True) + EPS)
    return x * r, r


def _rms_bwd(dh, xhat, r, g):
    dxh = dh * g
    dx = r * (dxh - xhat * jnp.mean(dxh * xhat, axis=-1, keepdims=True))
    return dx, jnp.sum(dh * xhat, axis=0, keepdims=True)


def _acc_rows(ref, first, val):
    @pl.when(first)
    def _():
        ref[...] = val

    @pl.when(jnp.logical_not(first))
    def _():
        ref[...] += val


def _rcopy(src, dst, ssem, rsem, dev):
    return pltpu.make_async_remote_copy(src_ref=src, dst_ref=dst, send_sem=ssem, recv_sem=rsem,
                                        device_id=dev, device_id_type=MESH_ID)


def _place():
    x, y, c = lax.axis_index("x"), lax.axis_index("y"), lax.axis_index("c")
    chips = [(1 - x, y), (x, 1 - y), (1 - x, 1 - y)]
    return x, y, c, 2 * x + y, chips, (x, y, 1 - c)


class _GatherPlan:
    PER_ARRAY = 7

    def __init__(self, arrays, part=(0, 1, 1), at=(0.5, 0.8)):
        self.operands = list(arrays)
        self.out_shapes = [jax.ShapeDtypeStruct(a.shape, a.dtype) for a in arrays]
        self.aliases = {i: i for i in range(len(arrays))}
        self.nsems = self.PER_ARRAY * len(arrays)
        self.base = 0
        self.halves = [a.shape[1] // 2 for a in arrays]
        self.part = part
        self.at = at

    def schedule(self):
        return [(0.0, self.start), (self.at[0], self.relay), (self.at[1], self.relay_far)]

    def _rows(self, ref, a, chip, half, quarter=None):
        lo, hi, n = self.part
        h = self.halves[a]
        first, size = half * h + lo * h // n, (hi - lo) * h // n
        if quarter is not None:
            first, size = first + quarter * (size // 2), size // 2
        return ref.at[chip, pl.ds(first, size)]

    def _copy(self, src, dst, a, n, ssem, rsem, dev):
        return _rcopy(src, dst, ssem.at[self.base + self.PER_ARRAY * a + n], rsem.at[self.base + self.PER_ARRAY * a + n], dev)

    def _own(self, ins, outs, ssem, rsem):
        x, y, c, me, chips, sib = _place()
        return [self._copy(self._rows(ins[a], a, me, c), self._rows(outs[a], a, me, c), a, k, ssem, rsem, (*chips[k], c))
                for a in range(len(ins)) for k in (0, 1)]

    def _relays(self, outs, ssem, rsem, a, k):
        x, y, c, me, chips, sib = _place()
        chip = 2 * chips[k][0] + chips[k][1]
        whole, quarter = self._rows(outs[a], a, chip, c), self._rows(outs[a], a, chip, c, k)
        return (self._copy(whole, whole, a, k, ssem, rsem, (*chips[k], c)),
                self._copy(quarter, quarter, a, 2 + k, ssem, rsem, (*chips[1 - k], c)),
                self._copy(whole, whole, a, 4 + k, ssem, rsem, sib))

    def _far(self, outs, ssem, rsem, a):
        x, y, c, me, chips, sib = _place()
        chip = 2 * chips[2][0] + chips[2][1]
        whole = self._rows(outs[a], a, chip, c)
        got = [self._copy(q, q, a, 2 + k, ssem, rsem, (*chips[1 - k], c))
               for k, q in enumerate([self._rows(outs[a], a, chip, c, 0), self._rows(outs[a], a, chip, c, 1)])]
        return got, self._copy(whole, whole, a, 6, ssem, rsem, sib)

    def start(self, ins, outs, ssem, rsem):
        for cp in self._own(ins, outs, ssem, rsem):
            cp.start()

    def relay(self, ins, outs, ssem, rsem):
        for a in range(len(outs)):
            for k in (0, 1):
                landed, onward, to_sibling = self._relays(outs, ssem, rsem, a, k)
                landed.wait_recv()
                onward.start()
                to_sibling.start()

    def relay_far(self, ins, outs, ssem, rsem):
        for a in range(len(outs)):
            got, to_sibling = self._far(outs, ssem, rsem, a)
            for cp in got:
                cp.wait_recv()
            to_sibling.start()

    def finish(self, ins, outs, ssem, rsem):
        x, y, c, me, chips, sib = _place()
        for a in range(len(outs)):
            for k in range(3):
                ref = self._rows(outs[a], a, 2 * chips[k][0] + chips[k][1], 1 - c)
                self._copy(ref, ref, a, 4 + k, ssem, rsem, sib).wait_recv()
        for cp in self._own(ins, outs, ssem, rsem):
            cp.wait_send()
        for a in range(len(outs)):
            for k in (0, 1):
                _, onward, to_sibling = self._relays(outs, ssem, rsem, a, k)
                onward.wait_send()
                to_sibling.wait_send()
            self._far(outs, ssem, rsem, a)[1].wait_send()


class _ScatterPlan:
    def __init__(self, arrays, part=(0, 1, 1), into=None):
        self.n = len(arrays)
        self.operands = list(arrays) + list(into or [])
        self.out_shapes = [jax.ShapeDtypeStruct((3,) + a.shape[1:], a.dtype) for a in arrays]
        self.aliases = {self.n + i: i for i in range(self.n)} if into else {}
        self.nsems = 3 * self.n
        self.base = 0
        self.part = part

    def _copies(self, ins, outs, ssem, rsem):
        x, y, c, me, chips, sib = _place()
        lo, hi, n = self.part
        out = []
        for a in range(self.n):
            h = ins[a].shape[1]
            rows = pl.ds(lo * h // n, (hi - lo) * h // n)
            for k, chip in enumerate(chips):
                out.append(_rcopy(ins[a].at[2 * chip[0] + chip[1], rows], outs[a].at[k, rows],
                                  ssem.at[self.base + 3 * a + k], rsem.at[self.base + 3 * a + k], (*chip, c)))
        return out

    def schedule(self):
        return [(0.0, self.start)]

    def start(self, ins, outs, ssem, rsem):
        for cp in self._copies(ins, outs, ssem, rsem):
            cp.start()

    def finish(self, ins, outs, ssem, rsem):
        cps = self._copies(ins, outs, ssem, rsem)
        for cp in cps:
            cp.wait_recv()
        for cp in cps:
            cp.wait_send()


class _ShareHalfPlan(_ScatterPlan):
    def __init__(self, arrays):
        super().__init__(arrays)
        self.out_shapes = [jax.ShapeDtypeStruct((3, a.shape[0] // 2, a.shape[1]), a.dtype) for a in arrays]

    def _copies(self, ins, outs, ssem, rsem):
        x, y, c, me, chips, sib = _place()
        out = []
        for a in range(self.n):
            rh = ins[a].shape[0] // 2
            for k, chip in enumerate(chips):
                out.append(_rcopy(ins[a].at[pl.ds(c * rh, rh)], outs[a].at[k],
                                  ssem.at[self.base + 3 * a + k], rsem.at[self.base + 3 * a + k], (*chip, c)))
        return out


class _Multi:
    def __init__(self, plans):
        self.plans = plans
        self.operands, self.out_shapes, self.aliases, self.nsems = [], [], {}, 0
        self.spans = []
        for p in plans:
            ni, no = len(self.operands), len(self.out_shapes)
            self.spans.append((ni, ni + len(p.operands), no, no + len(p.out_shapes)))
            self.aliases.update({ni + i: no + j for i, j in p.aliases.items()})
            p.base = self.nsems
            self.nsems += p.nsems
            self.operands += p.operands
            self.out_shapes += p.out_shapes

    def schedule(self):
        def bound(fn, span):
            i0, i1, o0, o1 = span
            return lambda ins, outs, ssem, rsem: fn(ins[i0:i1], outs[o0:o1], ssem, rsem)

        stages = [(at, bound(fn, span)) for p, span in zip(self.plans, self.spans) for at, fn in p.schedule()]
        return sorted(stages, key=lambda s: s[0])

    def finish(self, ins, outs, ssem, rsem):
        for p, (i0, i1, o0, o1) in zip(self.plans, self.spans):
            p.finish(ins[i0:i1], outs[o0:o1], ssem, rsem)

    def results(self, extra):
        return [list(extra[o0:o1]) for (_, _, o0, o1) in self.spans]


class _Host:
    def __init__(self, comm, in_specs, out_specs, out_shape, scratch):
        self.comm = comm
        self.n_in, self.n_out = len(in_specs), len(out_specs)
        self.in_specs, self.out_specs, self.out_shape, self.scratch = list(in_specs), list(out_specs), list(out_shape), list(scratch)
        self.aliases = {}
        self.args = []
        if comm is not None:
            self.in_specs += [ANY] * len(comm.operands)
            self.out_specs += [ANY] * len(comm.out_shapes)
            self.out_shape += comm.out_shapes
            self.scratch += [pltpu.SemaphoreType.DMA((comm.nsems,)), pltpu.SemaphoreType.DMA((comm.nsems,))]
            self.aliases = {self.n_in + i: self.n_out + j for i, j in comm.aliases.items()}
            self.args = list(comm.operands)

    def split(self, refs):
        nc = len(self.args)
        nco = len(self.out_shape) - self.n_out
        ins, p = refs[:self.n_in], self.n_in + nc
        outs, rest = refs[p:p + self.n_out], refs[p + self.n_out + nco:]
        self._cargs = None
        if self.comm is not None:
            self._cargs = (refs[self.n_in:p], refs[p + self.n_out:p + self.n_out + nco], rest[-2], rest[-1])
            rest = rest[:-2]
        return ins, outs, rest

    def before(self, step, total):
        if self.comm is None:
            return

        for at, stage in self.comm.schedule():
            pl.when(step == min(total - 1, int(at * total)))(functools.partial(stage, *self._cargs))

    def after(self, step, total):
        if self.comm is None:
            return

        @pl.when(step == total - 1)
        def _():
            self.comm.finish(*self._cargs)

    def results(self, outs):
        return outs[:self.n_out], outs[self.n_out:]


def _cast_bf16(x, name):
    r, c = x.shape
    tr = ROW_TILE if r % ROW_TILE == 0 else r

    def body(x_ref, o_ref):
        o_ref[...] = x_ref[...].astype(BF16)

    return pl.pallas_call(
        body, name=name, grid=(r // tr,),
        in_specs=[pl.BlockSpec((tr, c), lambda i: (i, 0))],
        out_specs=pl.BlockSpec((tr, c), lambda i: (i, 0)),
        out_shape=jax.ShapeDtypeStruct((r, c), BF16), compiler_params=_cp("parallel"))(x)


def _cast_bf16_own_slab(x, me_arr, name):
    r, c = x.shape
    tr = ROW_TILE if r % ROW_TILE == 0 else r

    def body(me_ref, x_ref, o_ref):
        o_ref[...] = x_ref[...].astype(BF16)

    return pl.pallas_call(
        body, name=name,
        grid_spec=pltpu.PrefetchScalarGridSpec(
            num_scalar_prefetch=1, grid=(r // tr,),
            in_specs=[pl.BlockSpec((tr, c), lambda i, me: (i, 0))],
            out_specs=pl.BlockSpec((None, tr, c), lambda i, me: (me[0], i, 0))),
        out_shape=jax.ShapeDtypeStruct((N_CHIPS, r, c), BF16), compiler_params=_cp("parallel"))(me_arr, x)


def _rms_fwd(x, g, name):
    s, d = x.shape

    def body(x_ref, g_ref, h_ref):
        xhat, _ = _rms_stats(x_ref[...])
        h_ref[...] = (xhat * g_ref[...]).astype(BF16)

    return pl.pallas_call(
        body, name=name, grid=(s // ROW_TILE,),
        in_specs=[pl.BlockSpec((ROW_TILE, d), lambda i: (i, 0)), pl.BlockSpec((1, d), lambda i: (0, 0))],
        out_specs=pl.BlockSpec((ROW_TILE, d), lambda i: (i, 0)),
        out_shape=jax.ShapeDtypeStruct((s, d), BF16), compiler_params=_cp("parallel"))(x, g)


def _mm_nn(a, w3, name, comm=None):
    m, k = a.shape
    nsh, _, ns = w3.shape
    tm = 512 if m % 512 == 0 else ROW_TILE
    tn = _pick_tile(ns, 1024)
    per = ns // tn
    grid = (nsh * per, m // tm)
    host = _Host(comm,
                 [pl.BlockSpec((tm, k), lambda n, i: (i, 0)), pl.BlockSpec((None, k, tn), lambda n, i: (n // per, 0, n % per))],
                 [pl.BlockSpec((tm, tn), lambda n, i: (i, n))], [jax.ShapeDtypeStruct((m, nsh * ns), F32)], [])

    def body(*refs):
        (a_ref, w_ref), (o_ref,), _ = host.split(refs)
        step = pl.program_id(0) * grid[1] + pl.program_id(1)
        host.before(step, grid[0] * grid[1])
        o_ref[...] = jnp.dot(a_ref[...], w_ref[...], preferred_element_type=F32)
        host.after(step, grid[0] * grid[1])

    outs = pl.pallas_call(
        body, name=name, grid=grid, in_specs=host.in_specs, out_specs=host.out_specs, out_shape=host.out_shape,
        scratch_shapes=host.scratch, input_output_aliases=host.aliases,
        compiler_params=_cp("arbitrary", "arbitrary"))(a, w3, *host.args)
    (out,), extra = host.results(outs)
    return out, extra


def _mm_nt(a, b, name):
    m, k = a.shape
    n = b.shape[0]
    tm = 512 if m % 512 == 0 else ROW_TILE

    def body(a_ref, b_ref, o_ref):
        o_ref[...] = lax.dot_general(a_ref[...], b_ref[...], (((1,), (1,)), ((), ())), preferred_element_type=F32)

    return pl.pallas_call(
        body, name=name, grid=(m // tm,),
        in_specs=[pl.BlockSpec((tm, k), lambda i: (i, 0)), pl.BlockSpec((n, k), lambda i: (0, 0))],
        out_specs=pl.BlockSpec((tm, n), lambda i: (i, 0)),
        out_shape=jax.ShapeDtypeStruct((m, n), F32), compiler_params=_cp("parallel"))(a, b)


def _mm_tn(a, b, nsh, name):
    s, m = a.shape
    n = b.shape[1]
    ns = n // nsh
    tm = 512 if m % 512 == 0 else ROW_TILE
    tn = _pick_tile(ns, 1024)
    per = ns // tn

    def body(a_ref, b_ref, o_ref):
        o_ref[...] = lax.dot_general(a_ref[...], b_ref[...], (((0,), (0,)), ((), ())),
                                     preferred_element_type=F32).astype(BF16)

    return pl.pallas_call(
        body, name=name, grid=(nsh * per, m // tm),
        in_specs=[pl.BlockSpec((s, tm), lambda j, i: (0, i)), pl.BlockSpec((s, tn), lambda j, i: (0, j))],
        out_specs=pl.BlockSpec((None, tm, tn), lambda j, i: (j // per, i, j % per)),
        out_shape=jax.ShapeDtypeStruct((nsh, m, ns), BF16), compiler_params=_cp("parallel", "parallel"))(a, b)


def _tri(n, rel):
    row = lax.broadcasted_iota(jnp.int32, (2 * n, n), 0)
    col = lax.broadcasted_iota(jnp.int32, (2 * n, n), 1)
    return jnp.where(rel(jnp.where(row >= n, row - n, row), col), 1.0, 0.0).astype(BF16)


def _dot_split(x, tri2):
    hi = x.astype(BF16)
    lo = (x - hi.astype(F32)).astype(BF16)
    return jnp.dot(jnp.concatenate([hi, lo], axis=1), tri2, preferred_element_type=F32)


def _nt(a, b):
    return lax.dot_general(a, b, (((1,), (1,)), ((), ())), preferred_element_type=F32)


def _tn(a, b):
    return lax.dot_general(a, b, (((0,), (0,)), ((), ())), preferred_element_type=F32)


def _heads_per_step(nh):
    return max(h for h in (1, 2, 4) if nh % h == 0)


def _sba_fwd(p, sbw, name, comm=None):
    s = p.shape[0]
    nh = sbw // HEAD_DIM
    hp = _heads_per_step(nh)
    ngrp, hw = nh // hp, hp * HEAD_DIM
    blk = ATT_BLOCK
    nq = s // blk
    scale = 1.0 / math.sqrt(HEAD_DIM)
    host = _Host(comm,
                 [pl.BlockSpec((blk, hw), lambda g, i: (i, g)),
                  pl.BlockSpec((s, hw), lambda g, i: (0, ngrp + g)),
                  pl.BlockSpec((s, hw), lambda g, i: (0, 2 * ngrp + g))],
                 [pl.BlockSpec((blk, hw), lambda g, i: (i, g))] * 2,
                 [jax.ShapeDtypeStruct((s, sbw), F32)] * 2,
                 [pltpu.VMEM((s, hw), BF16)] * 2)

    def body(*refs):
        (q_ref, k_ref, v_ref), (o_ref, lt_ref), (kb_ref, vb_ref) = host.split(refs)
        i = pl.program_id(1)
        step = pl.program_id(0) * nq + i
        host.before(step, ngrp * nq)

        @pl.when(i == 0)
        def _():
            kb_ref[...] = k_ref[...].astype(BF16)
            vb_ref[...] = v_ref[...].astype(BF16)

        heads = [slice(h * HEAD_DIM, (h + 1) * HEAD_DIM) for h in range(hp)]
        qs = [q_ref[:, hd].astype(BF16) for hd in heads]
        later = _tri(blk, lambda r, c: r > c)
        causal = lax.broadcasted_iota(jnp.int32, (blk, blk), 1) < lax.broadcasted_iota(jnp.int32, (blk, blk), 0)

        def key_block(j, carry, diagonal):
            rows = pl.ds(pl.multiple_of(j * blk, blk), blk)
            hs = range(hp)
            z = [_nt(qs[h], kb_ref[rows, heads[h]]) * scale for h in hs]
            ls = [_log_sigmoid(z[h]) for h in hs]
            lm = [jnp.where(causal, ls[h] - z[h], 0.0) if diagonal else ls[h] - z[h] for h in hs]
            stay = [_dot_split(lm[h], later) for h in hs]
            w = [jnp.exp(ls[h] + stay[h] + carry[h][1]) for h in hs]
            if diagonal:
                w = [jnp.where(causal, w[h], 0.0) for h in hs]
            acc = [carry[h][0] + jnp.dot(w[h].astype(BF16), vb_ref[rows, heads[h]], preferred_element_type=F32) for h in hs]
            return tuple((acc[h], carry[h][1] + jnp.sum(lm[h], axis=1, keepdims=True)) for h in hs)

        init = tuple((jnp.zeros((blk, HEAD_DIM), F32), jnp.zeros((blk, 1), F32)) for _ in heads)
        carry = key_block(i, init, True)
        carry = lax.fori_loop(0, i, lambda n, c: key_block(i - 1 - n, c, False), carry)
        for h, hd in enumerate(heads):
            o_ref[:, hd] = carry[h][0]
            lt_ref[:, hd] = jnp.broadcast_to(carry[h][1], (blk, HEAD_DIM))
        host.after(step, ngrp * nq)

    outs = pl.pallas_call(
        body, name=name, grid=(ngrp, nq), in_specs=host.in_specs, out_specs=host.out_specs, out_shape=host.out_shape,
        scratch_shapes=host.scratch, input_output_aliases=host.aliases,
        compiler_params=_cp("arbitrary", "arbitrary"))(p, p, p, *host.args)
    (out, ltot), extra = host.results(outs)
    return out, ltot, extra


def _sba_bwd(p, ltot, dout, sbw, name, comm=None):
    s = p.shape[0]
    nh = sbw // HEAD_DIM
    hp = _heads_per_step(nh)
    ngrp, hw = nh // hp, hp * HEAD_DIM
    blk = ATT_BLOCK
    nq = s // blk
    scale = 1.0 / math.sqrt(HEAD_DIM)
    blk_spec = pl.BlockSpec((blk, hw), lambda g, i: (i, g))
    col_spec = pl.BlockSpec((s, hw), lambda g, i: (0, g))
    host = _Host(comm,
                 [blk_spec, pl.BlockSpec((s, hw), lambda g, i: (0, ngrp + g)),
                  pl.BlockSpec((s, hw), lambda g, i: (0, 2 * ngrp + g)), blk_spec, blk_spec],
                 [blk_spec, col_spec, col_spec], [jax.ShapeDtypeStruct((s, sbw), BF16)] * 3,
                 [pltpu.VMEM((s, hw), BF16)] * 2 + [pltpu.VMEM((s, hw), F32)] * 2)

    def body(*refs):
        (q_ref, k_ref, v_ref, lt_ref, do_ref), (dq_ref, dk_ref, dv_ref), (kb_ref, vb_ref, dka_ref, dva_ref) = host.split(refs)
        i = pl.program_id(1)
        step = pl.program_id(0) * nq + i
        host.before(step, ngrp * nq)

        @pl.when(i == 0)
        def _():
            kb_ref[...] = k_ref[...].astype(BF16)
            vb_ref[...] = v_ref[...].astype(BF16)
            dka_ref[...] = jnp.zeros_like(dka_ref)
            dva_ref[...] = jnp.zeros_like(dva_ref)

        heads = [slice(h * HEAD_DIM, (h + 1) * HEAD_DIM) for h in range(hp)]
        qs = [q_ref[:, hd].astype(BF16) for hd in heads]
        dos = [do_ref[:, hd].astype(BF16) for hd in heads]
        ltots = [lt_ref[:, h * HEAD_DIM:h * HEAD_DIM + 1] for h in range(hp)]
        upto = _tri(blk, lambda r, c: r <= c)
        before = _tri(blk, lambda r, c: r < c)
        causal = lax.broadcasted_iota(jnp.int32, (blk, blk), 1) < lax.broadcasted_iota(jnp.int32, (blk, blk), 0)

        def key_block(j, carry, diagonal):
            rows = pl.ds(pl.multiple_of(j * blk, blk), blk)
            hs = range(hp)
            kj = [kb_ref[rows, heads[h]] for h in hs]
            vj = [vb_ref[rows, heads[h]] for h in hs]
            z = [_nt(qs[h], kj[h]) * scale for h in hs]
            dw = [_nt(dos[h], vj[h]) for h in hs]
            ls = [_log_sigmoid(z[h]) for h in hs]
            lm = [jnp.where(causal, ls[h] - z[h], 0.0) if diagonal else ls[h] - z[h] for h in hs]
            stay = [ltots[h] - carry[h][1] - _dot_split(lm[h], upto) for h in hs]
            w = [jnp.exp(ls[h] + stay[h]) for h in hs]
            if diagonal:
                w = [jnp.where(causal, w[h], 0.0) for h in hs]
            da = [dw[h] * w[h] for h in hs]
            sig = [jnp.exp(ls[h]) for h in hs]
            chain = [sig[h] * (carry[h][2] + _dot_split(da[h], before)) for h in hs]
            if diagonal:
                chain = [jnp.where(causal, chain[h], 0.0) for h in hs]
            dzb = [((da[h] * (1.0 - sig[h]) - chain[h]) * scale).astype(BF16) for h in hs]
            dq = [carry[h][0] + jnp.dot(dzb[h], kj[h], preferred_element_type=F32) for h in hs]
            for h in hs:
                dka_ref[rows, heads[h]] += _tn(dzb[h], qs[h])
            for h in hs:
                dva_ref[rows, heads[h]] += _tn(w[h].astype(BF16), dos[h])
            return tuple((dq[h], carry[h][1] + jnp.sum(lm[h], axis=1, keepdims=True),
                          carry[h][2] + jnp.sum(da[h], axis=1, keepdims=True)) for h in hs)

        zero = jnp.zeros((blk, 1), F32)
        init = tuple((jnp.zeros((blk, HEAD_DIM), F32), zero, zero) for _ in heads)
        carry = lax.fori_loop(0, i, lambda j, c: key_block(j, c, False), init)
        carry = key_block(i, carry, True)
        for h, hd in enumerate(heads):
            dq_ref[:, hd] = carry[h][0].astype(BF16)

        @pl.when(i == nq - 1)
        def _():
            dk_ref[...] = dka_ref[...].astype(BF16)
            dv_ref[...] = dva_ref[...].astype(BF16)

        host.after(step, ngrp * nq)

    outs = pl.pallas_call(
        body, name=name, grid=(ngrp, nq), in_specs=host.in_specs, out_specs=host.out_specs, out_shape=host.out_shape,
        scratch_shapes=host.scratch, input_output_aliases=host.aliases,
        compiler_params=_cp("arbitrary", "arbitrary"))(p, p, p, ltot, dout, *host.args)
    (dq, dk, dv), extra = host.results(outs)
    return dq, dk, dv, extra


def _pool_groups(pad_ref, tile, row0, gd, halo):
    row = row0 + lax.broadcasted_iota(jnp.int32, (tile, 1), 0)
    out = []
    for gi, win in enumerate(POOL_WINDOWS):
        cs = slice(gi * gd, (gi + 1) * gd)
        tok = pad_ref[halo:halo + tile, cs]
        acc = tok
        for j in range(1, win):
            acc = acc + pad_ref[halo - j:halo - j + tile, cs]
        cnt = jnp.minimum(win, row + 1).astype(F32)
        out.append(acc / cnt - tok)
    return out


def _even_mix_fwd(p, att, pool_w, pool_scale, d, name):
    s = p.shape[0]
    half = d // 2
    gd = half // len(POOL_WINDOWS)
    t, hb = ROW_TILE, POOL_HALO

    def body(u_ref, uh_ref, g_ref, a_ref, pw_ref, sc_ref, y_ref, pad_ref):
        i = pl.program_id(0)
        pad_ref[0:hb, :] = jnp.where(i > 0, uh_ref[...], 0.0)
        pad_ref[hb:, :] = u_ref[...]
        pooled = _pool_groups(pad_ref, t, i * t, gd, hb)
        for gi in range(len(POOL_WINDOWS)):
            cs = slice(gi * gd, (gi + 1) * gd)
            po = jnp.dot(pooled[gi].astype(BF16), pw_ref[gi], preferred_element_type=F32) * sc_ref[:, cs]
            y_ref[:, half + gi * gd:half + (gi + 1) * gd] = (po * _silu(g_ref[:, half + gi * gd:half + (gi + 1) * gd])).astype(BF16)
        y_ref[:, :half] = (a_ref[...] * _silu(g_ref[:, :half])).astype(BF16)

    return pl.pallas_call(
        body, name=name, grid=(s // t,),
        in_specs=[pl.BlockSpec((t, half), lambda i: (i, 3)),
                  pl.BlockSpec((hb, half), lambda i: (jnp.maximum(i * (t // hb) - 1, 0), 3)),
                  pl.BlockSpec((t, d), lambda i: (i, 2)),
                  pl.BlockSpec((t, half), lambda i: (i, 0)),
                  pl.BlockSpec(pool_w.shape, lambda i: (0, 0, 0)),
                  pl.BlockSpec((1, half), lambda i: (0, 0))],
        out_specs=pl.BlockSpec((t, d), lambda i: (i, 0)),
        out_shape=jax.ShapeDtypeStruct((s, d), BF16),
        scratch_shapes=[pltpu.VMEM((hb + t, half), F32)],
        compiler_params=_cp("parallel"))(p, p, p, att, pool_w, pool_scale)


def _even_mix_bwd(p, att, dy, pool_w, pool_scale, d, name):
    s = p.shape[0]
    half = d // 2
    ng = len(POOL_WINDOWS)
    gd = half // ng
    t, hb = ROW_TILE, POOL_HALO
    nt = s // t

    def body(u_ref, uh_ref, g_ref, gh_ref, a_ref, dy_ref, dyh_ref, pw_ref, sc_ref,
             da_ref, du_ref, dg_ref, dsc_ref, dpw_ref, pad_ref, dn_ref):
        i = pl.program_id(0)
        first = i == 0
        pad_ref[0:hb, :] = jnp.where(i > 0, uh_ref[...], 0.0)
        pad_ref[hb:, :] = u_ref[...]
        pooled = _pool_groups(pad_ref, t, i * t, gd, hb)
        g1 = g_ref[:, :half]
        dy1 = dy_ref[:, :half]
        da_ref[...] = dy1 * _silu(g1)
        dg_ref[:, :half] = (dy1 * a_ref[...] * _dsilu(g1)).astype(BF16)
        row = i * t + lax.broadcasted_iota(jnp.int32, (t + hb, 1), 0)
        for gi, win in enumerate(POOL_WINDOWS):
            cs = slice(gi * gd, (gi + 1) * gd)
            cs2 = slice(half + gi * gd, half + (gi + 1) * gd)
            w = pw_ref[gi]
            pb = pooled[gi].astype(BF16)
            zp = jnp.dot(pb, w, preferred_element_type=F32)
            g2 = g_ref[:, cs2]
            dy2 = dy_ref[:, cs2]
            dg_ref[:, cs2] = (dy2 * zp * sc_ref[:, cs] * _dsilu(g2)).astype(BF16)
            dpo = dy2 * _silu(g2)
            _acc_rows(dsc_ref.at[:, cs], first, jnp.sum(dpo * zp, axis=0, keepdims=True))
            dz = (dpo * sc_ref[:, cs]).astype(BF16)
            _acc_rows(dpw_ref.at[gi], first, _tn(pb, dz))
            dzh = jnp.where(i < nt - 1, dyh_ref[:, cs] * _silu(gh_ref[:, cs]) * sc_ref[:, cs], 0.0).astype(BF16)
            dpool = _nt(dz, w)
            dpool_h = _nt(dzh, w)
            cnt = jnp.minimum(win, row + 1).astype(F32)
            dn_ref[0:t, cs] = dpool / cnt[0:t]
            dn_ref[t:, cs] = dpool_h / cnt[t:]
            acc = dn_ref[0:t, cs]
            for j in range(1, win):
                acc = acc + dn_ref[j:j + t, cs]
            du_ref[:, cs] = (acc - dpool).astype(BF16)

    return pl.pallas_call(
        body, name=name, grid=(nt,),
        in_specs=[pl.BlockSpec((t, half), lambda i: (i, 3)),
                  pl.BlockSpec((hb, half), lambda i: (jnp.maximum(i * (t // hb) - 1, 0), 3)),
                  pl.BlockSpec((t, d), lambda i: (i, 2)),
                  pl.BlockSpec((hb, half), lambda i: (jnp.minimum((i + 1) * (t // hb), s // hb - 1), 5)),
                  pl.BlockSpec((t, half), lambda i: (i, 0)),
                  pl.BlockSpec((t, d), lambda i: (i, 0)),
                  pl.BlockSpec((hb, half), lambda i: (jnp.minimum((i + 1) * (t // hb), s // hb - 1), 1)),
                  pl.BlockSpec(pool_w.shape, lambda i: (0, 0, 0)),
                  pl.BlockSpec((1, half), lambda i: (0, 0))],
        out_specs=[pl.BlockSpec((t, half), lambda i: (i, 0)),
                   pl.BlockSpec((t, half), lambda i: (i, 0)),
                   pl.BlockSpec((t, d), lambda i: (i, 0)),
                   pl.BlockSpec((1, half), lambda i: (0, 0)),
                   pl.BlockSpec((ng, gd, gd), lambda i: (0, 0, 0))],
        out_shape=[jax.ShapeDtypeStruct((s, half), F32), jax.ShapeDtypeStruct((s, half), BF16),
                   jax.ShapeDtypeStruct((s, d), BF16), jax.ShapeDtypeStruct((1, half), F32),
                   jax.ShapeDtypeStruct((ng, gd, gd), F32)],
        scratch_shapes=[pltpu.VMEM((hb + t, half), F32), pltpu.VMEM((t + hb, half), F32)],
        compiler_params=_cp("arbitrary"))(p, p, p, p, att, dy, dy, pool_w, pool_scale)


def _mm_out_even(y, w, x, g_post, g_pre_next, name):
    s, k = y.shape
    d = w.shape[1]
    t = ROW_TILE

    def body(y_ref, w_ref, x_ref, gp_ref, gn_ref, o_ref, x1_ref, h1_ref):
        o = jnp.dot(y_ref[...], w_ref[...], preferred_element_type=F32)
        o_ref[...] = o
        ohat, _ = _rms_stats(o)
        x1 = x_ref[...] + ohat * gp_ref[...]
        x1_ref[...] = x1
        xhat, _ = _rms_stats(x1)
        h1_ref[...] = (xhat * gn_ref[...]).astype(BF16)

    row = lambda c: pl.BlockSpec((t, c), lambda i: (i, 0))
    vec = pl.BlockSpec((1, d), lambda i: (0, 0))
    return pl.pallas_call(
        body, name=name, grid=(s // t,),
        in_specs=[row(k), pl.BlockSpec((k, d), lambda i: (0, 0)), row(d), vec, vec],
        out_specs=[row(d), row(d), row(d)],
        out_shape=[jax.ShapeDtypeStruct((s, d), F32), jax.ShapeDtypeStruct((s, d), F32),
                   jax.ShapeDtypeStruct((s, d), BF16)],
        compiler_params=_cp("parallel"))(y, w, x, g_post, g_pre_next)


def _mm_out_odd(y, w, x1, g_post, target, name):
    s, k = y.shape
    d = w.shape[1]
    t = ROW_TILE

    def body(y_ref, w_ref, x_ref, gp_ref, tg_ref, do_ref, dx_ref, loss_ref, dgp_ref):
        first = pl.program_id(0) == 0
        o = jnp.dot(y_ref[...], w_ref[...], preferred_element_type=F32)
        ohat, r = _rms_stats(o)
        gp = gp_ref[...]
        diff = x_ref[...] + ohat * gp - tg_ref[...]
        part = 0.5 * jnp.sum(jnp.mean(diff * diff, axis=-1, keepdims=True), axis=0, keepdims=True)
        _acc_rows(loss_ref, first, jnp.broadcast_to(part, loss_ref.shape))
        dx2 = diff * (1.0 / d)
        dx_ref[...] = dx2
        do, dgp = _rms_bwd(dx2, ohat, r, gp)
        do_ref[...] = do.astype(BF16)
        _acc_rows(dgp_ref, first, dgp)

    row = lambda c: pl.BlockSpec((t, c), lambda i: (i, 0))
    vec = pl.BlockSpec((1, d), lambda i: (0, 0))
    return pl.pallas_call(
        body, name=name, grid=(s // t,),
        in_specs=[row(k), pl.BlockSpec((k, d), lambda i: (0, 0)), row(d), vec, row(d)],
        out_specs=[row(d), row(d), pl.BlockSpec((8, LANES), lambda i: (0, 0)), vec],
        out_shape=[jax.ShapeDtypeStruct((s, d), BF16), jax.ShapeDtypeStruct((s, d), F32),
                   jax.ShapeDtypeStruct((8, LANES), F32), jax.ShapeDtypeStruct((1, d), F32)],
        compiler_params=_cp("arbitrary"))(y, w, x1, g_post, target)


def _layer_norm(d1, cg, cb):
    mu = jnp.mean(d1, axis=-1, keepdims=True)
    cen = d1 - mu
    rstd = lax.rsqrt(jnp.mean(cen * cen, axis=-1, keepdims=True) + EPS)
    n = cen * rstd
    return n, rstd, n * cg + cb


SUBLANES = 8
GATHER_PIECES = 4
CONV_ROWS = 64


def _make_shifts(pad_ref, cs, sh_ref):
    rows = sh_ref.shape[1]
    for r in range(1, SUBLANES):
        sh_ref[r - 1] = pad_ref[r:r + rows, cs]


def _by_shift(taps, base, sign=1):
    return sorted(range(taps), key=lambda k: ((sign * (base + k)) % SUBLANES, k))


def _window(pad_ref, cs, sh_ref, off, t):
    m, r = divmod(off, SUBLANES)
    if r == 0:
        return pad_ref[SUBLANES * m:SUBLANES * m + t, cs]
    return sh_ref[r - 1, SUBLANES * m:SUBLANES * m + t, :]


def _odd_mix_fwd(p, sconv_w, dconv_w, dconv_b, cnorm_g, cnorm_b, d, name):
    s = p.shape[0]
    w = d // 2
    k3, k31 = sconv_w.shape[0], dconv_w.shape[0]
    t, hb = ROW_TILE, CONV_HALO
    assert hb >= k31 - 1 and w % LANES == 0

    def body(p_ref, ph_ref, w3_ref, w31_ref, b31_ref, cg_ref, cb_ref, y_ref, s3_ref, d1_ref, mpad, dpad, sh_ref):
        i = pl.program_id(0)
        mpad[0:hb, :] = jnp.where(i > 0, ph_ref[:, 2 * w:3 * w] * ph_ref[:, 0:w], 0.0)
        mpad[hb:, :] = p_ref[:, 2 * w:3 * w] * p_ref[:, 0:w]
        dpad[0:hb, :] = jnp.where(i > 0, ph_ref[:, 3 * w:4 * w] * _sigmoid(ph_ref[:, 4 * w:5 * w]), 0.0)
        dpad[hb:, :] = p_ref[:, 3 * w:4 * w] * _sigmoid(p_ref[:, 4 * w:5 * w])
        for c0 in range(0, w, LANES):
            cs = slice(c0, c0 + LANES)
            acc = jnp.zeros((t, LANES), F32)
            for kk in range(k3):
                acc = acc + w3_ref[kk:kk + 1, cs] * mpad[hb - (k3 - 1) + kk:hb - (k3 - 1) + kk + t, cs]
            s3_ref[:, cs] = acc
            _make_shifts(dpad, cs, sh_ref)
            for r0 in range(0, t, CONV_ROWS):
                acc = jnp.zeros((CONV_ROWS, LANES), F32)
                for kk in _by_shift(k31, hb - (k31 - 1)):
                    acc = acc + w31_ref[kk:kk + 1, cs] * _window(dpad, cs, sh_ref, hb - (k31 - 1) + kk + r0, CONV_ROWS)
                d1_ref[r0:r0 + CONV_ROWS, cs] = acc + b31_ref[:, cs]
        _, _, d2 = _layer_norm(d1_ref[...], cg_ref[...], cb_ref[...])
        y_ref[:, :w] = (p_ref[:, w:2 * w] * s3_ref[...] * _silu(p_ref[:, 5 * w:6 * w])).astype(BF16)
        y_ref[:, w:] = (_silu(d2) * _silu(p_ref[:, 6 * w:7 * w])).astype(BF16)

    row = lambda c: pl.BlockSpec((t, c), lambda i: (i, 0))
    full = lambda a: pl.BlockSpec(a.shape, lambda i: (0, 0))
    return pl.pallas_call(
        body, name=name, grid=(s // t,),
        in_specs=[row(7 * w),
                  pl.BlockSpec((hb, 5 * w), lambda i: (jnp.maximum(i * (t // hb) - 1, 0), 0)),
                  full(sconv_w), full(dconv_w), full(dconv_b), full(cnorm_g), full(cnorm_b)],
        out_specs=[row(d), row(w), row(w)],
        out_shape=[jax.ShapeDtypeStruct((s, d), BF16), jax.ShapeDtypeStruct((s, w), F32),
                   jax.ShapeDtypeStruct((s, w), F32)],
        scratch_shapes=[pltpu.VMEM((hb + t, w), F32)] * 2 + [pltpu.VMEM((SUBLANES - 1, hb + t - SUBLANES, LANES), F32)],
        compiler_params=_cp("parallel"))(p, p, sconv_w, dconv_w, dconv_b, cnorm_g, cnorm_b)


def _odd_bwd_rows(p, s3, d1, dy, cnorm_g, cnorm_b, d, name):
    s = p.shape[0]
    w = d // 2
    t = ROW_TILE

    def body(bc_ref, g1_ref, g2_ref, s3_ref, d1_ref, dy_ref, cg_ref, cb_ref,
             dbc_ref, dg_ref, ds3_ref, dd1_ref, dcg_ref, dcb_ref, db_ref):
        first = pl.program_id(0) == 0
        g1, g2 = g1_ref[...], g2_ref[...]
        bc, s3v = bc_ref[...], s3_ref[...]
        dy1, dy2 = dy_ref[:, :w], dy_ref[:, w:]
        n, rstd, d2 = _layer_norm(d1_ref[...], cg_ref[...], cb_ref[...])
        dg_ref[:, :w] = (dy1 * bc * s3v * _dsilu(g1)).astype(BF16)
        dg_ref[:, w:] = (dy2 * _silu(d2) * _dsilu(g2)).astype(BF16)
        dco = dy1 * _silu(g1)
        dbc_ref[...] = (dco * s3v).astype(BF16)
        ds3_ref[...] = dco * bc
        dd2 = dy2 * _silu(g2) * _dsilu(d2)
        _acc_rows(dcb_ref, first, jnp.sum(dd2, axis=0, keepdims=True))
        _acc_rows(dcg_ref, first, jnp.sum(dd2 * n, axis=0, keepdims=True))
        dn = dd2 * cg_ref[...]
        dd1 = rstd * (dn - jnp.mean(dn, axis=-1, keepdims=True) - n * jnp.mean(dn * n, axis=-1, keepdims=True))
        dd1_ref[...] = dd1
        _acc_rows(db_ref, first, jnp.sum(dd1, axis=0, keepdims=True))

    col = lambda j: pl.BlockSpec((t, w), lambda i: (i, j))
    row = lambda c: pl.BlockSpec((t, c), lambda i: (i, 0))
    vec = pl.BlockSpec((1, w), lambda i: (0, 0))
    return pl.pallas_call(
        body, name=name, grid=(s // t,),
        in_specs=[col(1), col(5), col(6), row(w), row(w), row(d), vec, vec],
        out_specs=[row(w), row(d), row(w), row(w), vec, vec, vec],
        out_shape=[jax.ShapeDtypeStruct((s, w), BF16), jax.ShapeDtypeStruct((s, d), BF16),
                   jax.ShapeDtypeStruct((s, w), F32), jax.ShapeDtypeStruct((s, w), F32)]
        + [jax.ShapeDtypeStruct((1, w), F32)] * 3,
        compiler_params=_cp("arbitrary"))(p, p, p, s3, d1, dy, cnorm_g, cnorm_b)


def _odd_bwd_conv(p, ds3, dd1, sconv_w, dconv_w, d, name):
    s = p.shape[0]
    w = d // 2
    k3, k31 = sconv_w.shape[0], dconv_w.shape[0]
    t, hb, ha = ROW_TILE, CONV_HALO, 8
    nt = s // t
    assert hb >= k31 - 1 and ha >= k3 - 1

    def body(hc_ref, cc_ref, ga_ref, gb_ref, hch_ref, cch_ref, gah_ref, gbh_ref, ds3_ref, ds3h_ref, dd1_ref, dd1h_ref,
             w3_ref, w31_ref, dhc_ref, dcc_ref, dga_ref, dgb_ref, dw3_ref, dw31_ref, mpad, dpad, s3pad, d1pad, sh_ref):
        i = pl.program_id(0)
        first = i == 0
        last = i == nt - 1
        mpad[0:hb, :] = jnp.where(i > 0, cch_ref[...] * hch_ref[...], 0.0)
        mpad[hb:, :] = cc_ref[...] * hc_ref[...]
        dpad[0:hb, :] = jnp.where(i > 0, gah_ref[...] * _sigmoid(gbh_ref[...]), 0.0)
        dpad[hb:, :] = ga_ref[...] * _sigmoid(gb_ref[...])
        s3pad[0:t, :] = ds3_ref[...]
        s3pad[t:, :] = jnp.where(last, 0.0, ds3h_ref[...])
        d1pad[0:t, :] = dd1_ref[...]
        d1pad[t:, :] = jnp.where(last, 0.0, dd1h_ref[...])

        @pl.when(first)
        def _():
            dw3_ref[...] = jnp.zeros_like(dw3_ref)
            dw31_ref[...] = jnp.zeros_like(dw31_ref)

        def fold(v):
            return jnp.sum(v.reshape(v.shape[0] // SUBLANES, SUBLANES, LANES), axis=0)

        groups = range(0, t, CONV_ROWS)
        for c0 in range(0, w, LANES):
            cs = slice(c0, c0 + LANES)
            ds3v = s3pad[0:t, cs]
            dm = jnp.zeros((t, LANES), F32)
            for kk in range(k3):
                dm = dm + w3_ref[kk:kk + 1, cs] * s3pad[k3 - 1 - kk:k3 - 1 - kk + t, cs]
                off = hb - (k3 - 1) + kk
                dw3_ref[SUBLANES * kk:SUBLANES * (kk + 1), cs] += fold(ds3v * mpad[off:off + t, cs])
            dcc_ref[:, cs] = (dm * hc_ref[:, cs]).astype(BF16)
            dhc_ref[:, cs] = (dm * cc_ref[:, cs]).astype(BF16)
            _make_shifts(d1pad, cs, sh_ref)
            for r0 in groups:
                rows = slice(r0, r0 + CONV_ROWS)
                dd0 = jnp.zeros((CONV_ROWS, LANES), F32)
                for kk in _by_shift(k31, -(k31 - 1), -1):
                    dd0 = dd0 + w31_ref[kk:kk + 1, cs] * _window(d1pad, cs, sh_ref, k31 - 1 - kk + r0, CONV_ROWS)
                sgb = _sigmoid(gb_ref[rows, cs])
                dga_ref[rows, cs] = (dd0 * sgb).astype(BF16)
                dgb_ref[rows, cs] = (dd0 * ga_ref[rows, cs] * sgb * (1.0 - sgb)).astype(BF16)
            _make_shifts(dpad, cs, sh_ref)
            for kk in _by_shift(k31, hb - (k31 - 1)):
                part = jnp.zeros((SUBLANES, LANES), F32)
                for r0 in groups:
                    part = part + fold(d1pad[r0:r0 + CONV_ROWS, cs]
                                       * _window(dpad, cs, sh_ref, hb - (k31 - 1) + kk + r0, CONV_ROWS))
                dw31_ref[SUBLANES * kk:SUBLANES * (kk + 1), cs] += part

    col = lambda j: pl.BlockSpec((t, w), lambda i: (i, j))
    pre = lambda j: pl.BlockSpec((hb, w), lambda i: (jnp.maximum(i * (t // hb) - 1, 0), j))
    row = pl.BlockSpec((t, w), lambda i: (i, 0))
    post = lambda h: pl.BlockSpec((h, w), lambda i: (jnp.minimum((i + 1) * (t // h), s // h - 1), 0))
    full = lambda a: pl.BlockSpec(a.shape, lambda i: (0, 0))
    dhc, dcc, dga, dgb, dw3, dw31 = pl.pallas_call(
        body, name=name, grid=(nt,),
        in_specs=[col(0), col(2), col(3), col(4), pre(0), pre(2), pre(3), pre(4),
                  row, post(ha), row, post(hb), full(sconv_w), full(dconv_w)],
        out_specs=[row, row, row, row, pl.BlockSpec((SUBLANES * k3, w), lambda i: (0, 0)),
                   pl.BlockSpec((SUBLANES * k31, w), lambda i: (0, 0))],
        out_shape=[jax.ShapeDtypeStruct((s, w), BF16)] * 4
        + [jax.ShapeDtypeStruct((SUBLANES * k3, w), F32), jax.ShapeDtypeStruct((SUBLANES * k31, w), F32)],
        scratch_shapes=[pltpu.VMEM((hb + t, w), F32)] * 2 + [pltpu.VMEM((t + ha, w), F32), pltpu.VMEM((t + hb, w), F32),
                                                             pltpu.VMEM((SUBLANES - 1, hb + t - SUBLANES, LANES), F32)],
        compiler_params=_cp("arbitrary"))(p, p, p, p, p, p, p, p, ds3, ds3, dd1, dd1, sconv_w, dconv_w)
    return dhc, dcc, dga, dgb, jnp.sum(dw3.reshape(k3, SUBLANES, w), axis=1), jnp.sum(dw31.reshape(k31, SUBLANES, w), axis=1)


def _mm_in_bwd(dp, w3, x, g_pre, dres, post, name, comm=None):
    s = dp.shape[0]
    nsh, d, ns = w3.shape
    t = ROW_TILE
    nt = s // t
    row = pl.BlockSpec((t, d), lambda i, k: (i, 0))
    vec = pl.BlockSpec((1, d), lambda i, k: (0, 0))
    in_specs = [pl.BlockSpec((t, ns), lambda i, k: (i, k)), pl.BlockSpec((None, d, ns), lambda i, k: (k, 0, 0)), row, vec, row]
    out_specs = [row, vec]
    out_shape = [jax.ShapeDtypeStruct((s, d), F32), jax.ShapeDtypeStruct((1, d), F32)]
    args = [dp, w3, x, g_pre, dres]
    if post is not None:
        in_specs += [row, vec]
        out_specs += [row, vec]
        out_shape += [jax.ShapeDtypeStruct((s, d), BF16), jax.ShapeDtypeStruct((1, d), F32)]
        args += list(post)
    host = _Host(comm, in_specs, out_specs, out_shape, [pltpu.VMEM((t, d), F32)])

    def body(*refs):
        ins, outs, (acc_ref,) = host.split(refs)
        dp_ref, w_ref, x_ref, g_ref, dr_ref = ins[:5]
        dx_ref, dg_ref = outs[:2]
        kk = pl.program_id(1)
        first = pl.program_id(0) == 0
        step = pl.program_id(0) * nsh + kk
        host.before(step, nt * nsh)
        part = _nt(dp_ref[...], w_ref[...])

        @pl.when(kk == 0)
        def _():
            acc_ref[...] = part

        @pl.when(kk > 0)
        def _():
            acc_ref[...] += part

        @pl.when(kk == nsh - 1)
        def _():
            xhat, r = _rms_stats(x_ref[...])
            dxn, dg = _rms_bwd(acc_ref[...], xhat, r, g_ref[...])
            dx = dr_ref[...] + dxn
            dx_ref[...] = dx
            _acc_rows(dg_ref, first, dg)
            if post is not None:
                ohat, ro = _rms_stats(ins[5][...])
                do, dgp = _rms_bwd(dx, ohat, ro, ins[6][...])
                outs[2][...] = do.astype(BF16)
                _acc_rows(outs[3], first, dgp)

        host.after(step, nt * nsh)

    res = pl.pallas_call(
        body, name=name, grid=(nt, nsh), in_specs=host.in_specs, out_specs=host.out_specs, out_shape=host.out_shape,
        scratch_shapes=host.scratch, input_output_aliases=host.aliases,
        compiler_params=_cp("arbitrary", "arbitrary"))(*args, *host.args)
    return host.results(res)


def _half_add(g, r1, c_arr, name):
    nsh, rows, ns = g.shape
    h = rows // 2
    tr = min(ROW_TILE, h)
    per = h // tr

    def body(c_ref, g_ref, r_ref, o_ref):
        o_ref[...] = (g_ref[...].astype(F32) + r_ref[...].astype(F32)).astype(BF16)

    spec = pl.BlockSpec((None, tr, ns), lambda s, r, c: (s, r, 0))
    return pl.pallas_call(
        body, name=name,
        grid_spec=pltpu.PrefetchScalarGridSpec(
            num_scalar_prefetch=1, grid=(nsh, per),
            in_specs=[pl.BlockSpec((None, tr, ns), lambda s, r, c: (s, c[0] * per + r, 0)), spec], out_specs=spec),
        out_shape=jax.ShapeDtypeStruct((nsh, h, ns), BF16), compiler_params=_cp("parallel", "parallel"))(c_arr, g, r1)


def _sum_chips(hh, r2, mc_arr, name):
    _, h, ns = hh.shape
    tr = min(ROW_TILE, h)
    per = h // tr

    def body(mc_ref, h_ref, a_ref, b_ref, c_ref, o_ref):
        o_ref[...] = ((h_ref[...].astype(F32) + a_ref[...].astype(F32)) + b_ref[...].astype(F32)) + c_ref[...].astype(F32)

    got = lambda k: pl.BlockSpec((None, tr, ns), lambda r, mc: (k, r, 0))
    return pl.pallas_call(
        body, name=name,
        grid_spec=pltpu.PrefetchScalarGridSpec(
            num_scalar_prefetch=1, grid=(per,),
            in_specs=[pl.BlockSpec((None, tr, ns), lambda r, mc: (mc[0], r, 0)), got(0), got(1), got(2)],
            out_specs=pl.BlockSpec((tr, ns), lambda r, mc: (mc[1] * per + r, 0))),
        out_shape=jax.ShapeDtypeStruct((2 * h, ns), F32), compiler_params=_cp("parallel"))(mc_arr, hh, r2, r2, r2)


def _add2(a, b, name):
    def body(a_ref, b_ref, o_ref):
        o_ref[...] = a_ref[...] + b_ref[...]

    return pl.pallas_call(body, name=name, out_shape=jax.ShapeDtypeStruct(a.shape, a.dtype), compiler_params=_cp())(a, b)


def _sum_chips_ordered(s2, r2, mc_arr, name):
    rows, w = s2.shape
    rh = rows // 2

    def body(mc_ref, s_ref, a_ref, b_ref, c_ref, o_ref):
        me = mc_ref[0]
        acc = None
        for j in range(N_CHIPS):
            rel = jnp.bitwise_xor(me, j)
            v = jnp.where(rel == 0, s_ref[...], jnp.where(rel == 2, a_ref[...], jnp.where(rel == 1, b_ref[...], c_ref[...])))
            acc = v if acc is None else acc + v
        o_ref[...] = acc

    got = lambda k: pl.BlockSpec((None, rh, w), lambda i, mc: (k, 0, 0))
    return pl.pallas_call(
        body, name=name,
        grid_spec=pltpu.PrefetchScalarGridSpec(
            num_scalar_prefetch=1, grid=(1,),
            in_specs=[pl.BlockSpec((rh, w), lambda i, mc: (mc[1], 0)), got(0), got(1), got(2)],
            out_specs=pl.BlockSpec((rh, w), lambda i, mc: (mc[1], 0))),
        out_shape=jax.ShapeDtypeStruct((rows, w), F32), compiler_params=_cp("arbitrary"))(mc_arr, s2, r2, r2, r2)


def _adamw(w, g, m, v, name):
    r, c = w.shape
    tr = ROW_TILE if r % ROW_TILE == 0 else r
    c1 = 1.0 / (1.0 - ADAM_B1 ** ADAM_STEP)
    c2 = 1.0 / (1.0 - ADAM_B2 ** ADAM_STEP)

    def body(w_ref, g_ref, m_ref, v_ref, go_ref, d_ref, nm_ref, nv_ref):
        gv = g_ref[...]
        go_ref[...] = gv
        nm = ADAM_B1 * m_ref[...] + (1.0 - ADAM_B1) * gv
        nv = ADAM_B2 * v_ref[...] + (1.0 - ADAM_B2) * (gv * gv)
        nm_ref[...] = nm
        nv_ref[...] = nv
        d_ref[...] = -ADAM_LR * ((nm * c1) / (jnp.sqrt(nv * c2) + ADAM_EPS) + ADAM_WD * w_ref[...])

    spec = pl.BlockSpec((tr, c), lambda i: (i, 0))
    return pl.pallas_call(
        body, name=name, grid=(r // tr,), in_specs=[spec] * 4, out_specs=[spec] * 4,
        out_shape=[jax.ShapeDtypeStruct((r, c), F32)] * 4, compiler_params=_cp("parallel"))(w, g, m, v)


def _gather_weights(bigs, pool_w, pack_w, pack_d, name):
    nb = len(bigs)
    smalls = [pool_w, pack_w, pack_d]
    q, cw, cd = pool_w.shape[1], pack_w.shape[1], pack_d.shape[1]
    pieces = [_GatherPlan(bigs, (j, j + 1, GATHER_PIECES)) for j in range(GATHER_PIECES)]
    for j, piece in enumerate(pieces):
        piece.base = 9 + j * piece.nsems

    def body(*refs):
        srcs, dsts = refs[:nb + 3], refs[nb + 3:2 * (nb + 3)]
        ssem, rsem, lsem = refs[2 * (nb + 3):]
        x, y, c, me, chips, sib = _place()

        def small_dst(n, chip):
            if n == 0:
                return dsts[nb].at[:, pl.ds(chip * q, q), :]
            return dsts[nb + n].at[:, pl.ds(chip * (cw if n == 1 else cd), cw if n == 1 else cd)]

        local = [pltpu.make_async_copy(srcs[nb + n], small_dst(n, me), lsem.at[n]) for n in range(3)]
        for cp in local:
            cp.start()
        sends = []
        for n in range(3):
            for k, chip in enumerate(chips):
                cp = _rcopy(srcs[nb + n], small_dst(n, me), ssem.at[3 * n + k], rsem.at[3 * n + k], (*chip, c))
                cp.start()
                sends.append(cp)
        big = (srcs[:nb], dsts[:nb], ssem, rsem)
        for stage in ("start", "relay", "relay_far", "finish"):
            for piece in pieces:
                getattr(piece, stage)(*big)
        for n in range(3):
            for k, chip in enumerate(chips):
                ref = small_dst(n, 2 * chip[0] + chip[1])
                _rcopy(ref, ref, ssem.at[3 * n + k], rsem.at[3 * n + k], (*chip, c)).wait_recv()
        for cp in sends:
            cp.wait_send()
        for cp in local:
            cp.wait()

    nsem = 9 + sum(piece.nsems for piece in pieces)
    out_shape = [jax.ShapeDtypeStruct(b.shape, b.dtype) for b in bigs]
    out_shape += [jax.ShapeDtypeStruct((pool_w.shape[0], N_CHIPS * q, pool_w.shape[2]), pool_w.dtype),
                  jax.ShapeDtypeStruct((pack_w.shape[0], N_CHIPS * cw), pack_w.dtype),
                  jax.ShapeDtypeStruct((pack_d.shape[0], N_CHIPS * cd), pack_d.dtype)]
    return pl.pallas_call(
        body, name=name, in_specs=[ANY] * (nb + 3), out_specs=[ANY] * (nb + 3), out_shape=out_shape,
        input_output_aliases={a: a for a in range(nb)},
        scratch_shapes=[pltpu.SemaphoreType.DMA((nsem,)), pltpu.SemaphoreType.DMA((nsem,)), pltpu.SemaphoreType.DMA((3,))],
        compiler_params=pltpu.CompilerParams(has_side_effects=True))(*bigs, *smalls)


def _swap_with_sibling(grads, wholes, name):
    n, nw = len(grads), len(wholes)
    halves = [g.shape[1] // 2 for g in grads]

    def body(*refs):
        srcs, dsts = refs[:n + nw], refs[n + nw:2 * (n + nw)]
        ssem, rsem = refs[2 * (n + nw):]
        x, y, c, me, chips, sib = _place()
        cps = [_rcopy(srcs[a].at[:, pl.ds((1 - c) * halves[a], halves[a]), :], dsts[a], ssem.at[a], rsem.at[a], sib)
               for a in range(n)]
        cps += [_rcopy(srcs[a], dsts[a], ssem.at[a], rsem.at[a], sib) for a in range(n, n + nw)]
        for cp in cps:
            cp.start()
        for cp in cps:
            cp.wait_recv()
        for cp in cps:
            cp.wait_send()

    out_shape = [jax.ShapeDtypeStruct((g.shape[0], h, g.shape[2]), g.dtype) for g, h in zip(grads, halves)]
    out_shape += [jax.ShapeDtypeStruct(w.shape, w.dtype) for w in wholes]
    return pl.pallas_call(
        body, name=name, in_specs=[ANY] * (n + nw), out_specs=[ANY] * (n + nw), out_shape=out_shape,
        scratch_shapes=[pltpu.SemaphoreType.DMA((n + nw,)), pltpu.SemaphoreType.DMA((n + nw,))],
        compiler_params=pltpu.CompilerParams(has_side_effects=True))(*grads, *wholes)


def _scatter_to_chips(halves_in, small, name):
    n = len(halves_in)
    rh = small.shape[0] // 2

    def body(*refs):
        srcs, dsts = refs[:n + 1], refs[n + 1:2 * (n + 1)]
        ssem, rsem = refs[2 * (n + 1):]
        x, y, c, me, chips, sib = _place()
        cps = []
        for a in range(n + 1):
            for k, chip in enumerate(chips):
                src = srcs[a].at[2 * chip[0] + chip[1]] if a < n else srcs[a].at[pl.ds(c * rh, rh)]
                cps.append(_rcopy(src, dsts[a].at[k], ssem.at[3 * a + k], rsem.at[3 * a + k], (*chip, c)))
        for cp in cps:
            cp.start()
        for cp in cps:
            cp.wait_recv()
        for cp in cps:
            cp.wait_send()

    out_shape = [jax.ShapeDtypeStruct((3,) + h.shape[1:], h.dtype) for h in halves_in]
    out_shape.append(jax.ShapeDtypeStruct((3, rh, small.shape[1]), small.dtype))
    return pl.pallas_call(
        body, name=name, in_specs=[ANY] * (n + 1), out_specs=[ANY] * (n + 1), out_shape=out_shape,
        scratch_shapes=[pltpu.SemaphoreType.DMA((3 * (n + 1),)), pltpu.SemaphoreType.DMA((3 * (n + 1),))],
        compiler_params=pltpu.CompilerParams(has_side_effects=True))(*halves_in, small)


def _join_halves(parts, name):
    n = len(parts)

    def body(*refs):
        srcs, dsts = refs[:n], refs[n:2 * n]
        ssem, rsem = refs[2 * n:]
        x, y, c, me, chips, sib = _place()
        cps = []
        for a in range(n):
            h = srcs[a].shape[0] // 2
            cps.append(_rcopy(srcs[a].at[pl.ds(c * h, h)], dsts[a].at[pl.ds(c * h, h)], ssem.at[a], rsem.at[a], sib))
        for cp in cps:
            cp.start()
        for a in range(n):
            h = srcs[a].shape[0] // 2
            theirs = dsts[a].at[pl.ds((1 - c) * h, h)]
            _rcopy(theirs, theirs, ssem.at[a], rsem.at[a], sib).wait_recv()
        for cp in cps:
            cp.wait_send()

    out_shape = [jax.ShapeDtypeStruct(p.shape, p.dtype) for p in parts]
    return pl.pallas_call(
        body, name=name, in_specs=[ANY] * n, out_specs=[ANY] * n, out_shape=out_shape,
        input_output_aliases={a: a for a in range(n)},
        scratch_shapes=[pltpu.SemaphoreType.DMA((n,)), pltpu.SemaphoreType.DMA((n,))],
        compiler_params=pltpu.CompilerParams(has_side_effects=True))(*parts)


def _pad_rows(a, rows):
    return jnp.pad(a, ((0, rows - a.shape[0]), (0, 0)))


def _stack_rows(parts, multiple):
    padded = [_pad_rows(p, -(-p.shape[0] // 8) * 8) for p in parts]
    starts, at = [], 0
    for p in padded:
        starts.append(at)
        at += p.shape[0]
    total = -(-at // multiple) * multiple
    if total > at:
        padded.append(jnp.zeros((total - at, parts[0].shape[1]), parts[0].dtype))
    return jnp.concatenate(padded, axis=0), starts


def kernel(x, ln_pre_even, w_in_even, pool_w, pool_scale, w_out_even, ln_post_even, ln_pre_odd, w_in_odd, sconv_w, dconv_w, dconv_b, cnorm_g, cnorm_b, w_out_odd, ln_post_odd, loss_target, m_ln_pre_even, m_w_in_even, m_pool_w, m_pool_scale, m_w_out_even, m_ln_post_even, m_ln_pre_odd, m_w_in_odd, m_sconv_w, m_dconv_w, m_dconv_b, m_cnorm_g, m_cnorm_b, m_w_out_odd, m_ln_post_odd, v_ln_pre_even, v_w_in_even, v_pool_w, v_pool_scale, v_w_out_even, v_ln_post_even, v_ln_pre_odd, v_w_in_odd, v_sconv_w, v_dconv_w, v_dconv_b, v_cnorm_g, v_cnorm_b, v_w_out_odd, v_ln_post_odd):
    _, s, d = x.shape
    half = d // 2
    cw = half // N_CHIPS
    ng, q, gd = pool_w.shape[1:]
    k3, k31 = sconv_w.shape[1], dconv_w.shape[1]
    x2d, tgt = x[0], loss_target[0]
    me = 2 * lax.axis_index("x") + lax.axis_index("y")
    core = lax.axis_index("c")
    c_arr = jnp.reshape(core, (1,)).astype(jnp.int32)
    me_arr = jnp.reshape(me, (1,)).astype(jnp.int32)
    mc_arr = jnp.stack([me, core]).astype(jnp.int32)

    shards = [w_in_even[0], w_out_even[0], w_in_odd[0], w_out_odd[0]]
    slabs = [_cast_bf16_own_slab(w, me_arr, f"cast_w{n}") for n, w in enumerate(shards)]
    pool_w_b = _cast_bf16(pool_w[0].reshape(ng * q, gd), "cast_pool_w").reshape(ng, q, gd)
    pack_w, at_w = _stack_rows([sconv_w[0], dconv_w[0], dconv_b, cnorm_g, cnorm_b], 8)
    pack_d, at_d = _stack_rows([ln_pre_odd, ln_post_odd], 8)
    win_e, pool_w_f, pack_w_f, pack_d_f = _gather_weights(slabs[:1], pool_w_b, pack_w, pack_d, "gather_first")
    sconv_f = pack_w_f[at_w[0]:at_w[0] + k3]
    dconv_f = pack_w_f[at_w[1]:at_w[1] + k31]
    dconv_b_f, cnorm_g_f, cnorm_b_f = (pack_w_f[at_w[n]:at_w[n] + 1] for n in (2, 3, 4))
    ln_pre_odd_f = pack_d_f[at_d[0]:at_d[0] + 1]
    ln_post_odd_f = pack_d_f[at_d[1]:at_d[1] + 1]

    def reduce_half(g, name):
        (got,) = _swap_with_sibling([g], [], "swap_" + name)
        return _half_add(g, got, c_arr, "half_add_" + name)

    h0 = _rms_fwd(x2d, ln_pre_even, "rms_pre_even")
    plans = _Multi([_GatherPlan([slabs[1]], at=(0.55, 0.85)), _GatherPlan([slabs[2]], (0, 1, 4), at=(0.55, 0.85))])
    p_e, extra = _mm_nn(h0, win_e, "proj_in_even", plans)
    (wout_e,), (win_o,) = plans.results(extra)
    wout_e = wout_e.reshape(d, d)
    att, ltot, (win_o,) = _sba_fwd(p_e, half, "sba_fwd", _GatherPlan([win_o], (1, 4, 4), at=(0.55, 0.85)))
    y_e = _even_mix_fwd(p_e, att, pool_w_f, pool_scale, d, "even_mix_fwd")
    o_e, x1, h1 = _mm_out_even(y_e, wout_e, x2d, ln_post_even, ln_pre_odd_f, "proj_out_even")
    p_o, (wout_o,) = _mm_nn(h1, win_o, "proj_in_odd", _GatherPlan([slabs[3]]))
    wout_o = wout_o.reshape(d, d)
    y_o, s3, d1 = _odd_mix_fwd(p_o, sconv_f, dconv_f, dconv_b_f, cnorm_g_f, cnorm_b_f, d, "odd_mix_fwd")
    do_o, dx2, loss_blk, dln_post_odd = _mm_out_odd(y_o, wout_o, x1, ln_post_odd_f, tgt, "proj_out_odd_loss")

    dy_o = _mm_nt(do_o, wout_o, "dy_odd")
    g_wout_o = _mm_tn(y_o, do_o, 1, "dw_out_odd").reshape(N_CHIPS, d // N_CHIPS, d)
    h_wout_o = reduce_half(g_wout_o, "out_odd")
    dbc, dgate_o, ds3, dd1, dcnorm_g, dcnorm_b, ddconv_b = _odd_bwd_rows(p_o, s3, d1, dy_o, cnorm_g_f, cnorm_b_f, d, "odd_bwd_rows")
    dhc, dcc, dga, dgb, dsconv, ddconv = _odd_bwd_conv(p_o, ds3, dd1, sconv_f, dconv_f, d, "odd_bwd_conv")
    dp_o = jnp.concatenate([dhc, dbc, dcc, dga, dgb, dgate_o], axis=1)
    g_win_o = _mm_tn(h1, dp_o, N_CHIPS, "dw_in_odd")
    h_win_o = reduce_half(g_win_o, "in_odd")
    plans = _Multi([_ScatterPlan([h_wout_o]), _ScatterPlan([h_win_o], (0, 1, 2))])
    (dx1, dln_pre_odd, do_e, dln_post_even), extra = _mm_in_bwd(
        dp_o, win_o, x1, ln_pre_odd_f, dx2, (o_e, ln_post_even), "dx_odd", plans)
    (s_wout_o,), (s_win_o,) = plans.results(extra)

    dy_e = _mm_nt(do_e, wout_e, "dy_even")
    g_wout_e = _mm_tn(y_e, do_e, 1, "dw_out_even").reshape(N_CHIPS, d // N_CHIPS, d)
    h_wout_e = reduce_half(g_wout_e, "out_even")
    datt, du, dgate_e, dpool_scale, dpool_w = _even_mix_bwd(p_e, att, dy_e, pool_w_f, pool_scale, d, "even_mix_bwd")
    two = lambda v: v.reshape(2, half)
    small_parts = [dpool_scale, two(dln_post_even), two(dln_pre_odd), two(dln_post_odd),
                   dsconv, ddconv, ddconv_b, dcnorm_g, dcnorm_b, dpool_w.reshape(gd, half)]
    small, at_s = _stack_rows(small_parts, 16)
    (small1,) = _swap_with_sibling([], [small], "swap_small")
    small2 = _add2(small, small1, "small_add")
    plans = _Multi([_ScatterPlan([h_win_o], (1, 2, 2), into=[s_win_o]), _ScatterPlan([h_wout_e]), _ShareHalfPlan([small2])])
    dq, dk, dv, extra = _sba_bwd(p_e, ltot, datt, half, "sba_bwd", plans)
    (s_win_o,), (s_wout_e,), (small_got,) = plans.results(extra)
    dp_e = jnp.concatenate([dq, dk, dv, du, dgate_e], axis=1)
    g_win_e = _mm_tn(h0, dp_e, N_CHIPS, "dw_in_even")
    h_win_e = reduce_half(g_win_e, "in_even")
    (grad_x, dln_pre_even), (s_win_e,) = _mm_in_bwd(dp_e, win_e, x2d, ln_pre_even, dx1, None, "dx_even", _ScatterPlan([h_win_e]))

    last, at_l = _stack_rows([two(dln_pre_even), jnp.pad(loss_blk[0:1], ((0, 0), (0, half - LANES)))], 16)
    (last1,) = _swap_with_sibling([], [last], "swap_last")
    last2 = _add2(last, last1, "last_add")
    (last_got,) = _scatter_to_chips([], last2, "scatter_last")
    pairs = [(h_win_e, s_win_e), (h_wout_e, s_wout_e), (h_win_o, s_win_o), (h_wout_o, s_wout_o)]
    parts = [_sum_chips(h, r, mc_arr, f"sum_chips{n}") for n, (h, r) in enumerate(pairs)]
    parts.append(_sum_chips_ordered(small2, small_got, mc_arr, "small_sum"))
    parts.append(_sum_chips_ordered(last2, last_got, mc_arr, "last_sum"))
    gw_in_e, gw_out_e, gw_in_o, gw_out_o, red, red_last = _join_halves(parts, "join_halves")
    loss = red_last[at_l[1], 0]

    def rows(n, cnt):
        return red[at_s[n]:at_s[n] + cnt]

    def mine(a, width):
        return lax.dynamic_slice_in_dim(a, me * width, width, axis=1)

    quarter = d // N_CHIPS
    g_small = {
        "ln_pre_even": red_last[at_l[0]:at_l[0] + 2].reshape(1, d),
        "pool_scale": rows(0, 1),
        "ln_post_even": rows(1, 2).reshape(1, d),
        "ln_pre_odd": mine(rows(2, 2).reshape(1, d), quarter),
        "ln_post_odd": mine(rows(3, 2).reshape(1, d), quarter),
        "sconv_w": mine(rows(4, k3), cw),
        "dconv_w": mine(rows(5, k31), cw),
        "dconv_b": mine(rows(6, 1), cw),
        "cnorm_g": mine(rows(7, 1), cw),
        "cnorm_b": mine(rows(8, 1), cw),
        "pool_w": lax.dynamic_slice_in_dim(rows(9, gd).reshape(ng, gd, gd), me * q, q, axis=1).reshape(ng * q, gd),
    }
    w2d = {
        "ln_pre_even": ln_pre_even, "w_in_even": w_in_even[0], "pool_w": pool_w[0].reshape(ng * q, gd),
        "pool_scale": pool_scale, "w_out_even": w_out_even[0], "ln_post_even": ln_post_even, "ln_pre_odd": ln_pre_odd,
        "w_in_odd": w_in_odd[0], "sconv_w": sconv_w[0], "dconv_w": dconv_w[0], "dconv_b": dconv_b, "cnorm_g": cnorm_g,
        "cnorm_b": cnorm_b, "w_out_odd": w_out_odd[0], "ln_post_odd": ln_post_odd,
    }
    moments = {
        "ln_pre_even": (m_ln_pre_even, v_ln_pre_even), "w_in_even": (m_w_in_even, v_w_in_even),
        "pool_w": (m_pool_w, v_pool_w), "pool_scale": (m_pool_scale, v_pool_scale),
        "w_out_even": (m_w_out_even, v_w_out_even), "ln_post_even": (m_ln_post_even, v_ln_post_even),
        "ln_pre_odd": (m_ln_pre_odd, v_ln_pre_odd), "w_in_odd": (m_w_in_odd, v_w_in_odd),
        "sconv_w": (m_sconv_w, v_sconv_w), "dconv_w": (m_dconv_w, v_dconv_w), "dconv_b": (m_dconv_b, v_dconv_b),
        "cnorm_g": (m_cnorm_g, v_cnorm_g), "cnorm_b": (m_cnorm_b, v_cnorm_b),
        "w_out_odd": (m_w_out_odd, v_w_out_odd), "ln_post_odd": (m_ln_post_odd, v_ln_post_odd),
    }
    g2d = dict(g_small, w_in_even=gw_in_e, w_out_even=gw_out_e, w_in_odd=gw_in_o, w_out_odd=gw_out_o)
    grads_out, deltas, new_m, new_v = [], [], [], []
    for name, w in w2d.items():
        m_in, v_in = moments[name]
        shape = m_in.shape
        g_out, delta, nm, nv = _adamw(w, g2d[name], m_in.reshape(w.shape), v_in.reshape(w.shape), "adamw_" + name)
        grads_out.append(g_out.reshape(shape))
        deltas.append(delta.reshape(shape))
        new_m.append(nm.reshape(shape))
        new_v.append(nv.reshape(shape))
    return (loss, grad_x.reshape(x.shape), *grads_out, *deltas, *new_m, *new_v)
```

```python
import functools
import math

import jax
import jax.numpy as jnp
from jax import lax
from jax.experimental import pallas as pl
from jax.experimental.pallas import tpu as pltpu

F32 = jnp.float32
BF16 = jnp.bfloat16
EPS = 1e-6
N_CHIPS = 4
VMEM_LIMIT_V7X = 56 << 20
HEAD_DIM = 128
ATT_BLOCK = 256
POOL_WINDOWS = (2, 4, 8, 16)
ROW_TILE = 256
POOL_HALO = 16
CONV_HALO = 32
LANES = 128
ADAM_LR, ADAM_B1, ADAM_B2, ADAM_EPS, ADAM_WD, ADAM_STEP = 0.001, 0.9, 0.999, 1e-08, 0.01, 10
MESH_ID = pl.DeviceIdType.MESH
ANY = pl.BlockSpec(memory_space=pl.ANY)


def _cp(*sem):
    return pltpu.CompilerParams(dimension_semantics=sem or None, vmem_limit_bytes=VMEM_LIMIT_V7X)


def _pick_tile(n, cap):
    best = None
    for t in range(LANES, min(n, cap) + 1, LANES):
        if n % t == 0:
            best = t
    assert best is not None, (n, cap)
    return best


def _sigmoid(x):
    return 1.0 / (1.0 + jnp.exp(-x))


def _silu(x):
    return x * _sigmoid(x)


def _dsilu(x):
    s = _sigmoid(x)
    return s * (1.0 + x * (1.0 - s))


def _log_sigmoid(z):
    return jnp.minimum(z, 0.0) - jnp.log(1.0 + jnp.exp(-jnp.abs(z)))


def _rms_stats(x):
    r = lax.rsqrt(jnp.mean(x * x, axis=-1, keepdims=True) + EPS)
    return x * r, r


def _rms_bwd(dh, xhat, r, g):
    dxh = dh * g
    dx = r * (dxh - xhat * jnp.mean(dxh * xhat, axis=-1, keepdims=True))
    return dx, jnp.sum(dh * xhat, axis=0, keepdims=True)


def _acc_rows(ref, first, val):
    @pl.when(first)
    def _():
        ref[...] = val

    @pl.when(jnp.logical_not(first))
    def _():
        ref[...] += val


def _rcopy(src, dst, ssem, rsem, dev):
    return pltpu.make_async_remote_copy(src_ref=src, dst_ref=dst, send_sem=ssem, recv_sem=rsem,
                                        device_id=dev, device_id_type=MESH_ID)


def _place():
    x, y, c = lax.axis_index("x"), lax.axis_index("y"), lax.axis_index("c")
    chips = [(1 - x, y), (x, 1 - y), (1 - x, 1 - y)]
    return x, y, c, 2 * x + y, chips, (x, y, 1 - c)


class _GatherPlan:
    PER_ARRAY = 7

    def __init__(self, arrays, part=(0, 1, 1), at=(0.5, 0.8)):
        self.operands = list(arrays)
        self.out_shapes = [jax.ShapeDtypeStruct(a.shape, a.dtype) for a in arrays]
        self.aliases = {i: i for i in range(len(arrays))}
        self.nsems = self.PER_ARRAY * len(arrays)
        self.base = 0
        self.halves = [a.shape[1] // 2 for a in arrays]
        self.part = part
        self.at = at

    def schedule(self):
        return [(0.0, self.start), (self.at[0], self.relay), (self.at[1], self.relay_far)]

    def _rows(self, ref, a, chip, half, quarter=None):
        lo, hi, n = self.part
        h = self.halves[a]
        first, size = half * h + lo * h // n, (hi - lo) * h // n
        if quarter is not None:
            first, size = first + quarter * (size // 2), size // 2
        return ref.at[chip, pl.ds(first, size)]

    def _copy(self, src, dst, a, n, ssem, rsem, dev):
        return _rcopy(src, dst, ssem.at[self.base + self.PER_ARRAY * a + n], rsem.at[self.base + self.PER_ARRAY * a + n], dev)

    def _own(self, ins, outs, ssem, rsem):
        x, y, c, me, chips, sib = _place()
        return [self._copy(self._rows(ins[a], a, me, c), self._rows(outs[a], a, me, c), a, k, ssem, rsem, (*chips[k], c))
                for a in range(len(ins)) for k in (0, 1)]

    def _relays(self, outs, ssem, rsem, a, k):
        x, y, c, me, chips, sib = _place()
        chip = 2 * chips[k][0] + chips[k][1]
        whole, quarter = self._rows(outs[a], a, chip, c), self._rows(outs[a], a, chip, c, k)
        return (self._copy(whole, whole, a, k, ssem, rsem, (*chips[k], c)),
                self._copy(quarter, quarter, a, 2 + k, ssem, rsem, (*chips[1 - k], c)),
                self._copy(whole, whole, a, 4 + k, ssem, rsem, sib))

    def _far(self, outs, ssem, rsem, a):
        x, y, c, me, chips, sib = _place()
        chip = 2 * chips[2][0] + chips[2][1]
        whole = self._rows(outs[a], a, chip, c)
        got = [self._copy(q, q, a, 2 + k, ssem, rsem, (*chips[1 - k], c))
               for k, q in enumerate([self._rows(outs[a], a, chip, c, 0), self._rows(outs[a], a, chip, c, 1)])]
        return got, self._copy(whole, whole, a, 6, ssem, rsem, sib)

    def start(self, ins, outs, ssem, rsem):
        for cp in self._own(ins, outs, ssem, rsem):
            cp.start()

    def relay(self, ins, outs, ssem, rsem):
        for a in range(len(outs)):
            for k in (0, 1):
                landed, onward, to_sibling = self._relays(outs, ssem, rsem, a, k)
                landed.wait_recv()
                onward.start()
                to_sibling.start()

    def relay_far(self, ins, outs, ssem, rsem):
        for a in range(len(outs)):
            got, to_sibling = self._far(outs, ssem, rsem, a)
            for cp in got:
                cp.wait_recv()
            to_sibling.start()

    def finish(self, ins, outs, ssem, rsem):
        x, y, c, me, chips, sib = _place()
        for a in range(len(outs)):
            for k in range(3):
                ref = self._rows(outs[a], a, 2 * chips[k][0] + chips[k][1], 1 - c)
                self._copy(ref, ref, a, 4 + k, ssem, rsem, sib).wait_recv()
        for cp in self._own(ins, outs, ssem, rsem):
            cp.wait_send()
        for a in range(len(outs)):
            for k in (0, 1):
                _, onward, to_sibling = self._relays(outs, ssem, rsem, a, k)
                onward.wait_send()
                to_sibling.wait_send()
            self._far(outs, ssem, rsem, a)[1].wait_send()


class _ScatterPlan:
    def __init__(self, arrays, part=(0, 1, 1), into=None):
        self.n = len(arrays)
        self.operands = list(arrays) + list(into or [])
        self.out_shapes = [jax.ShapeDtypeStruct((3,) + a.shape[1:], a.dtype) for a in arrays]
        self.aliases = {self.n + i: i for i in range(self.n)} if into else {}
        self.nsems = 3 * self.n
        self.base = 0
        self.part = part

    def _copies(self, ins, outs, ssem, rsem):
        x, y, c, me, chips, sib = _place()
        lo, hi, n = self.part
        out = []
        for a in range(self.n):
            h = ins[a].shape[1]
            rows = pl.ds(lo * h // n, (hi - lo) * h // n)
            for k, chip in enumerate(chips):
                out.append(_rcopy(ins[a].at[2 * chip[0] + chip[1], rows], outs[a].at[k, rows],
                                  ssem.at[self.base + 3 * a + k], rsem.at[self.base + 3 * a + k], (*chip, c)))
        return out

    def schedule(self):
        return [(0.0, self.start)]

    def start(self, ins, outs, ssem, rsem):
        for cp in self._copies(ins, outs, ssem, rsem):
            cp.start()

    def finish(self, ins, outs, ssem, rsem):
        cps = self._copies(ins, outs, ssem, rsem)
        for cp in cps:
            cp.wait_recv()
        for cp in cps:
            cp.wait_send()


class _ShareHalfPlan(_ScatterPlan):
    def __init__(self, arrays):
        super().__init__(arrays)
        self.out_shapes = [jax.ShapeDtypeStruct((3, a.shape[0] // 2, a.shape[1]), a.dtype) for a in arrays]

    def _copies(self, ins, outs, ssem, rsem):
        x, y, c, me, chips, sib = _place()
        out = []
        for a in range(self.n):
            rh = ins[a].shape[0] // 2
            for k, chip in enumerate(chips):
                out.append(_rcopy(ins[a].at[pl.ds(c * rh, rh)], outs[a].at[k],
                                  ssem.at[self.base + 3 * a + k], rsem.at[self.base + 3 * a + k], (*chip, c)))
        return out


class _Multi:
    def __init__(self, plans):
        self.plans = plans
        self.operands, self.out_shapes, self.aliases, self.nsems = [], [], {}, 0
        self.spans = []
        for p in plans:
            ni, no = len(self.operands), len(self.out_shapes)
            self.spans.append((ni, ni + len(p.operands), no, no + len(p.out_shapes)))
            self.aliases.update({ni + i: no + j for i, j in p.aliases.items()})
            p.base = self.nsems
            self.nsems += p.nsems
            self.operands += p.operands
            self.out_shapes += p.out_shapes

    def schedule(self):
        def bound(fn, span):
            i0, i1, o0, o1 = span
            return lambda ins, outs, ssem, rsem: fn(ins[i0:i1], outs[o0:o1], ssem, rsem)

        stages = [(at, bound(fn, span)) for p, span in zip(self.plans, self.spans) for at, fn in p.schedule()]
        return sorted(stages, key=lambda s: s[0])

    def finish(self, ins, outs, ssem, rsem):
        for p, (i0, i1, o0, o1) in zip(self.plans, self.spans):
            p.finish(ins[i0:i1], outs[o0:o1], ssem, rsem)

    def results(self, extra):
        return [list(extra[o0:o1]) for (_, _, o0, o1) in self.spans]


class _Host:
    def __init__(self, comm, in_specs, out_specs, out_shape, scratch):
        self.comm = comm
        self.n_in, self.n_out = len(in_specs), len(out_specs)
        self.in_specs, self.out_specs, self.out_shape, self.scratch = list(in_specs), list(out_specs), list(out_shape), list(scratch)
        self.aliases = {}
        self.args = []
        if comm is not None:
            self.in_specs += [ANY] * len(comm.operands)
            self.out_specs += [ANY] * len(comm.out_shapes)
            self.out_shape += comm.out_shapes
            self.scratch += [pltpu.SemaphoreType.DMA((comm.nsems,)), pltpu.SemaphoreType.DMA((comm.nsems,))]
            self.aliases = {self.n_in + i: self.n_out + j for i, j in comm.aliases.items()}
            self.args = list(comm.operands)

    def split(self, refs):
        nc = len(self.args)
        nco = len(self.out_shape) - self.n_out
        ins, p = refs[:self.n_in], self.n_in + nc
        outs, rest = refs[p:p + self.n_out], refs[p + self.n_out + nco:]
        self._cargs = None
        if self.comm is not None:
            self._cargs = (refs[self.n_in:p], refs[p + self.n_out:p + self.n_out + nco], rest[-2], rest[-1])
            rest = rest[:-2]
        return ins, outs, rest

    def before(self, step, total):
        if self.comm is None:
            return

        for at, stage in self.comm.schedule():
            pl.when(step == min(total - 1, int(at * total)))(functools.partial(stage, *self._cargs))

    def after(self, step, total):
        if self.comm is None:
            return

        @pl.when(step == total - 1)
        def _():
            self.comm.finish(*self._cargs)

    def results(self, outs):
        return outs[:self.n_out], outs[self.n_out:]


def _cast_bf16(x, name):
    r, c = x.shape
    tr = ROW_TILE if r % ROW_TILE == 0 else r

    def body(x_ref, o_ref):
        o_ref[...] = x_ref[...].astype(BF16)

    return pl.pallas_call(
        body, name=name, grid=(r // tr,),
        in_specs=[pl.BlockSpec((tr, c), lambda i: (i, 0))],
        out_specs=pl.BlockSpec((tr, c), lambda i: (i, 0)),
        out_shape=jax.ShapeDtypeStruct((r, c), BF16), compiler_params=_cp("parallel"))(x)


def _cast_bf16_own_slab(x, me_arr, name):
    r, c = x.shape
    tr = ROW_TILE if r % ROW_TILE == 0 else r

    def body(me_ref, x_ref, o_ref):
        o_ref[...] = x_ref[...].astype(BF16)

    return pl.pallas_call(
        body, name=name,
        grid_spec=pltpu.PrefetchScalarGridSpec(
            num_scalar_prefetch=1, grid=(r // tr,),
            in_specs=[pl.BlockSpec((tr, c), lambda i, me: (i, 0))],
            out_specs=pl.BlockSpec((None, tr, c), lambda i, me: (me[0], i, 0))),
        out_shape=jax.ShapeDtypeStruct((N_CHIPS, r, c), BF16), compiler_params=_cp("parallel"))(me_arr, x)


def _rms_fwd(x, g, name):
    s, d = x.shape

    def body(x_ref, g_ref, h_ref):
        xhat, _ = _rms_stats(x_ref[...])
        h_ref[...] = (xhat * g_ref[...]).astype(BF16)

    return pl.pallas_call(
        body, name=name, grid=(s // ROW_TILE,),
        in_specs=[pl.BlockSpec((ROW_TILE, d), lambda i: (i, 0)), pl.BlockSpec((1, d), lambda i: (0, 0))],
        out_specs=pl.BlockSpec((ROW_TILE, d), lambda i: (i, 0)),
        out_shape=jax.ShapeDtypeStruct((s, d), BF16), compiler_params=_cp("parallel"))(x, g)


def _mm_nn(a, w3, name, comm=None):
    m, k = a.shape
    nsh, _, ns = w3.shape
    tm = 512 if m % 512 == 0 else ROW_TILE
    tn = _pick_tile(ns, 1024)
    per = ns // tn
    grid = (nsh * per, m // tm)
    host = _Host(comm,
                 [pl.BlockSpec((tm, k), lambda n, i: (i, 0)), pl.BlockSpec((None, k, tn), lambda n, i: (n // per, 0, n % per))],
                 [pl.BlockSpec((tm, tn), lambda n, i: (i, n))], [jax.ShapeDtypeStruct((m, nsh * ns), F32)], [])

    def body(*refs):
        (a_ref, w_ref), (o_ref,), _ = host.split(refs)
        step = pl.program_id(0) * grid[1] + pl.program_id(1)
        host.before(step, grid[0] * grid[1])
        o_ref[...] = jnp.dot(a_ref[...], w_ref[...], preferred_element_type=F32)
        host.after(step, grid[0] * grid[1])

    outs = pl.pallas_call(
        body, name=name, grid=grid, in_specs=host.in_specs, out_specs=host.out_specs, out_shape=host.out_shape,
        scratch_shapes=host.scratch, input_output_aliases=host.aliases,
        compiler_params=_cp("arbitrary", "arbitrary"))(a, w3, *host.args)
    (out,), extra = host.results(outs)
    return out, extra


def _mm_nt(a, b, name):
    m, k = a.shape
    n = b.shape[0]
    tm = 512 if m % 512 == 0 else ROW_TILE

    def body(a_ref, b_ref, o_ref):
        o_ref[...] = lax.dot_general(a_ref[...], b_ref[...], (((1,), (1,)), ((), ())), preferred_element_type=F32)

    return pl.pallas_call(
        body, name=name, grid=(m // tm,),
        in_specs=[pl.BlockSpec((tm, k), lambda i: (i, 0)), pl.BlockSpec((n, k), lambda i: (0, 0))],
        out_specs=pl.BlockSpec((tm, n), lambda i: (i, 0)),
        out_shape=jax.ShapeDtypeStruct((m, n), F32), compiler_params=_cp("parallel"))(a, b)


def _mm_tn(a, b, nsh, name, comm=None):
    s, m = a.shape
    n = b.shape[1]
    ns = n // nsh
    tm = 512 if m % 512 == 0 else ROW_TILE
    tn = _pick_tile(ns, 1024)
    per = ns // tn
    grid = (nsh * per, m // tm)
    host = _Host(comm, [pl.BlockSpec((s, tm), lambda j, i: (0, i)), pl.BlockSpec((s, tn), lambda j, i: (0, j))],
                 [pl.BlockSpec((None, tm, tn), lambda j, i: (j // per, i, j % per))],
                 [jax.ShapeDtypeStruct((nsh, m, ns), BF16)], [])

    def body(*refs):
        (a_ref, b_ref), (o_ref,), _ = host.split(refs)
        step = pl.program_id(0) * grid[1] + pl.program_id(1)
        host.before(step, grid[0] * grid[1])
        o_ref[...] = lax.dot_general(a_ref[...], b_ref[...], (((0,), (0,)), ((), ())),
                                     preferred_element_type=F32).astype(BF16)
        host.after(step, grid[0] * grid[1])

    outs = pl.pallas_call(
        body, name=name, grid=grid, in_specs=host.in_specs, out_specs=host.out_specs, out_shape=host.out_shape,
        scratch_shapes=host.scratch, input_output_aliases=host.aliases,
        compiler_params=_cp("arbitrary", "arbitrary"))(a, b, *host.args)
    (out,), extra = host.results(outs)
    return out, extra


def _tri(n, rel):
    row = lax.broadcasted_iota(jnp.int32, (2 * n, n), 0)
    col = lax.broadcasted_iota(jnp.int32, (2 * n, n), 1)
    return jnp.where(rel(jnp.where(row >= n, row - n, row), col), 1.0, 0.0).astype(BF16)


def _dot_split(x, tri2):
    hi = x.astype(BF16)
    lo = (x - hi.astype(F32)).astype(BF16)
    return jnp.dot(jnp.concatenate([hi, lo], axis=1), tri2, preferred_element_type=F32)


def _nt(a, b):
    return lax.dot_general(a, b, (((1,), (1,)), ((), ())), preferred_element_type=F32)


def _tn(a, b):
    return lax.dot_general(a, b, (((0,), (0,)), ((), ())), preferred_element_type=F32)


def _heads_per_step(nh):
    return max(h for h in (1, 2, 4) if nh % h == 0)


def _sba_fwd(p, sbw, name, comm=None):
    s = p.shape[0]
    nh = sbw // HEAD_DIM
    hp = _heads_per_step(nh)
    ngrp, hw = nh // hp, hp * HEAD_DIM
    blk = ATT_BLOCK
    nq = s // blk
    scale = 1.0 / math.sqrt(HEAD_DIM)
    host = _Host(comm,
                 [pl.BlockSpec((blk, hw), lambda g, i: (i, g)),
                  pl.BlockSpec((s, hw), lambda g, i: (0, ngrp + g)),
                  pl.BlockSpec((s, hw), lambda g, i: (0, 2 * ngrp + g))],
                 [pl.BlockSpec((blk, hw), lambda g, i: (i, g))] * 2,
                 [jax.ShapeDtypeStruct((s, sbw), F32)] * 2,
                 [pltpu.VMEM((s, hw), BF16)] * 2)

    def body(*refs):
        (q_ref, k_ref, v_ref), (o_ref, lt_ref), (kb_ref, vb_ref) = host.split(refs)
        i = pl.program_id(1)
        step = pl.program_id(0) * nq + i
        host.before(step, ngrp * nq)

        @pl.when(i == 0)
        def _():
            kb_ref[...] = k_ref[...].astype(BF16)
            vb_ref[...] = v_ref[...].astype(BF16)

        heads = [slice(h * HEAD_DIM, (h + 1) * HEAD_DIM) for h in range(hp)]
        qs = [q_ref[:, hd].astype(BF16) for hd in heads]
        later = _tri(blk, lambda r, c: r > c)
        causal = lax.broadcasted_iota(jnp.int32, (blk, blk), 1) < lax.broadcasted_iota(jnp.int32, (blk, blk), 0)

        def key_block(j, carry, diagonal):
            rows = pl.ds(pl.multiple_of(j * blk, blk), blk)
            hs = range(hp)
            z = [_nt(qs[h], kb_ref[rows, heads[h]]) * scale for h in hs]
            ls = [_log_sigmoid(z[h]) for h in hs]
            lm = [jnp.where(causal, ls[h] - z[h], 0.0) if diagonal else ls[h] - z[h] for h in hs]
            stay = [_dot_split(lm[h], later) for h in hs]
            w = [jnp.exp(ls[h] + stay[h] + carry[h][1]) for h in hs]
            if diagonal:
                w = [jnp.where(causal, w[h], 0.0) for h in hs]
            acc = [carry[h][0] + jnp.dot(w[h].astype(BF16), vb_ref[rows, heads[h]], preferred_element_type=F32) for h in hs]
            return tuple((acc[h], carry[h][1] + jnp.sum(lm[h], axis=1, keepdims=True)) for h in hs)

        init = tuple((jnp.zeros((blk, HEAD_DIM), F32), jnp.zeros((blk, 1), F32)) for _ in heads)
        carry = key_block(i, init, True)
        carry = lax.fori_loop(0, i, lambda n, c: key_block(i - 1 - n, c, False), carry)
        for h, hd in enumerate(heads):
            o_ref[:, hd] = carry[h][0]
            lt_ref[:, hd] = jnp.broadcast_to(carry[h][1], (blk, HEAD_DIM))
        host.after(step, ngrp * nq)

    outs = pl.pallas_call(
        body, name=name, grid=(ngrp, nq), in_specs=host.in_specs, out_specs=host.out_specs, out_shape=host.out_shape,
        scratch_shapes=host.scratch, input_output_aliases=host.aliases,
        compiler_params=_cp("arbitrary", "arbitrary"))(p, p, p, *host.args)
    (out, ltot), extra = host.results(outs)
    return out, ltot, extra


def _sba_bwd(p, ltot, dout, sbw, name, comm=None):
    s = p.shape[0]
    nh = sbw // HEAD_DIM
    hp = _heads_per_step(nh)
    ngrp, hw = nh // hp, hp * HEAD_DIM
    blk = ATT_BLOCK
    nq = s // blk
    scale = 1.0 / math.sqrt(HEAD_DIM)
    blk_spec = pl.BlockSpec((blk, hw), lambda g, i: (i, g))
    col_spec = pl.BlockSpec((s, hw), lambda g, i: (0, g))
    host = _Host(comm,
                 [blk_spec, pl.BlockSpec((s, hw), lambda g, i: (0, ngrp + g)),
                  pl.BlockSpec((s, hw), lambda g, i: (0, 2 * ngrp + g)), blk_spec, blk_spec],
                 [blk_spec, col_spec, col_spec], [jax.ShapeDtypeStruct((s, sbw), BF16)] * 3,
                 [pltpu.VMEM((s, hw), BF16)] * 2 + [pltpu.VMEM((s, hw), F32)] * 2)

    def body(*refs):
        (q_ref, k_ref, v_ref, lt_ref, do_ref), (dq_ref, dk_ref, dv_ref), (kb_ref, vb_ref, dka_ref, dva_ref) = host.split(refs)
        i = pl.program_id(1)
        step = pl.program_id(0) * nq + i
        host.before(step, ngrp * nq)

        @pl.when(i == 0)
        def _():
            kb_ref[...] = k_ref[...].astype(BF16)
            vb_ref[...] = v_ref[...].astype(BF16)
            dka_ref[...] = jnp.zeros_like(dka_ref)
            dva_ref[...] = jnp.zeros_like(dva_ref)

        heads = [slice(h * HEAD_DIM, (h + 1) * HEAD_DIM) for h in range(hp)]
        qs = [q_ref[:, hd].astype(BF16) for hd in heads]
        dos = [do_ref[:, hd].astype(BF16) for hd in heads]
        ltots = [lt_ref[:, h * HEAD_DIM:h * HEAD_DIM + 1] for h in range(hp)]
        upto = _tri(blk, lambda r, c: r <= c)
        before = _tri(blk, lambda r, c: r < c)
        causal = lax.broadcasted_iota(jnp.int32, (blk, blk), 1) < lax.broadcasted_iota(jnp.int32, (blk, blk), 0)

        def key_block(j, carry, diagonal):
            rows = pl.ds(pl.multiple_of(j * blk, blk), blk)
            hs = range(hp)
            kj = [kb_ref[rows, heads[h]] for h in hs]
            vj = [vb_ref[rows, heads[h]] for h in hs]
            z = [_nt(qs[h], kj[h]) * scale for h in hs]
            dw = [_nt(dos[h], vj[h]) for h in hs]
            ls = [_log_sigmoid(z[h]) for h in hs]
            lm = [jnp.where(causal, ls[h] - z[h], 0.0) if diagonal else ls[h] - z[h] for h in hs]
            stay = [ltots[h] - carry[h][1] - _dot_split(lm[h], upto) for h in hs]
            w = [jnp.exp(ls[h] + stay[h]) for h in hs]
            if diagonal:
                w = [jnp.where(causal, w[h], 0.0) for h in hs]
            da = [dw[h] * w[h] for h in hs]
            sig = [jnp.exp(ls[h]) for h in hs]
            chain = [sig[h] * (carry[h][2] + _dot_split(da[h], before)) for h in hs]
            if diagonal:
                chain = [jnp.where(causal, chain[h], 0.0) for h in hs]
            dzb = [((da[h] * (1.0 - sig[h]) - chain[h]) * scale).astype(BF16) for h in hs]
            dq = [carry[h][0] + jnp.dot(dzb[h], kj[h], preferred_element_type=F32) for h in hs]
            for h in hs:
                dka_ref[rows, heads[h]] += _tn(dzb[h], qs[h])
            for h in hs:
                dva_ref[rows, heads[h]] += _tn(w[h].astype(BF16), dos[h])
            return tuple((dq[h], carry[h][1] + jnp.sum(lm[h], axis=1, keepdims=True),
                          carry[h][2] + jnp.sum(da[h], axis=1, keepdims=True)) for h in hs)

        zero = jnp.zeros((blk, 1), F32)
        init = tuple((jnp.zeros((blk, HEAD_DIM), F32), zero, zero) for _ in heads)
        carry = lax.fori_loop(0, i, lambda j, c: key_block(j, c, False), init)
        carry = key_block(i, carry, True)
        for h, hd in enumerate(heads):
            dq_ref[:, hd] = carry[h][0].astype(BF16)

        @pl.when(i == nq - 1)
        def _():
            dk_ref[...] = dka_ref[...].astype(BF16)
            dv_ref[...] = dva_ref[...].astype(BF16)

        host.after(step, ngrp * nq)

    outs = pl.pallas_call(
        body, name=name, grid=(ngrp, nq), in_specs=host.in_specs, out_specs=host.out_specs, out_shape=host.out_shape,
        scratch_shapes=host.scratch, input_output_aliases=host.aliases,
        compiler_params=_cp("arbitrary", "arbitrary"))(p, p, p, ltot, dout, *host.args)
    (dq, dk, dv), extra = host.results(outs)
    return dq, dk, dv, extra


def _pool_groups(pad_ref, tile, row0, gd, halo):
    row = row0 + lax.broadcasted_iota(jnp.int32, (tile, 1), 0)
    out = []
    for gi, win in enumerate(POOL_WINDOWS):
        cs = slice(gi * gd, (gi + 1) * gd)
        tok = pad_ref[halo:halo + tile, cs]
        acc = tok
        for j in range(1, win):
            acc = acc + pad_ref[halo - j:halo - j + tile, cs]
        cnt = jnp.minimum(win, row + 1).astype(F32)
        out.append(acc / cnt - tok)
    return out


def _even_mix_fwd(p, att, pool_w, pool_scale, d, name):
    s = p.shape[0]
    half = d // 2
    gd = half // len(POOL_WINDOWS)
    t, hb = ROW_TILE, POOL_HALO

    def body(u_ref, uh_ref, g_ref, a_ref, pw_ref, sc_ref, y_ref, pad_ref):
        i = pl.program_id(0)
        pad_ref[0:hb, :] = jnp.where(i > 0, uh_ref[...], 0.0)
        pad_ref[hb:, :] = u_ref[...]
        pooled = _pool_groups(pad_ref, t, i * t, gd, hb)
        for gi in range(len(POOL_WINDOWS)):
            cs = slice(gi * gd, (gi + 1) * gd)
            po = jnp.dot(pooled[gi].astype(BF16), pw_ref[gi], preferred_element_type=F32) * sc_ref[:, cs]
            y_ref[:, half + gi * gd:half + (gi + 1) * gd] = (po * _silu(g_ref[:, half + gi * gd:half + (gi + 1) * gd])).astype(BF16)
        y_ref[:, :half] = (a_ref[...] * _silu(g_ref[:, :half])).astype(BF16)

    return pl.pallas_call(
        body, name=name, grid=(s // t,),
        in_specs=[pl.BlockSpec((t, half), lambda i: (i, 3)),
                  pl.BlockSpec((hb, half), lambda i: (jnp.maximum(i * (t // hb) - 1, 0), 3)),
                  pl.BlockSpec((t, d), lambda i: (i, 2)),
                  pl.BlockSpec((t, half), lambda i: (i, 0)),
                  pl.BlockSpec(pool_w.shape, lambda i: (0, 0, 0)),
                  pl.BlockSpec((1, half), lambda i: (0, 0))],
        out_specs=pl.BlockSpec((t, d), lambda i: (i, 0)),
        out_shape=jax.ShapeDtypeStruct((s, d), BF16),
        scratch_shapes=[pltpu.VMEM((hb + t, half), F32)],
        compiler_params=_cp("parallel"))(p, p, p, att, pool_w, pool_scale)


def _even_mix_bwd(p, att, dy, pool_w, pool_scale, d, name):
    s = p.shape[0]
    half = d // 2
    ng = len(POOL_WINDOWS)
    gd = half // ng
    t, hb = ROW_TILE, POOL_HALO
    nt = s // t

    def body(u_ref, uh_ref, g_ref, gh_ref, a_ref, dy_ref, dyh_ref, pw_ref, sc_ref,
             da_ref, du_ref, dg_ref, dsc_ref, dpw_ref, pad_ref, dn_ref):
        i = pl.program_id(0)
        first = i == 0
        pad_ref[0:hb, :] = jnp.where(i > 0, uh_ref[...], 0.0)
        pad_ref[hb:, :] = u_ref[...]
        pooled = _pool_groups(pad_ref, t, i * t, gd, hb)
        g1 = g_ref[:, :half]
        dy1 = dy_ref[:, :half]
        da_ref[...] = dy1 * _silu(g1)
        dg_ref[:, :half] = (dy1 * a_ref[...] * _dsilu(g1)).astype(BF16)
        row = i * t + lax.broadcasted_iota(jnp.int32, (t + hb, 1), 0)
        for gi, win in enumerate(POOL_WINDOWS):
            cs = slice(gi * gd, (gi + 1) * gd)
            cs2 = slice(half + gi * gd, half + (gi + 1) * gd)
            w = pw_ref[gi]
            pb = pooled[gi].astype(BF16)
            zp = jnp.dot(pb, w, preferred_element_type=F32)
            g2 = g_ref[:, cs2]
            dy2 = dy_ref[:, cs2]
            dg_ref[:, cs2] = (dy2 * zp * sc_ref[:, cs] * _dsilu(g2)).astype(BF16)
            dpo = dy2 * _silu(g2)
            _acc_rows(dsc_ref.at[:, cs], first, jnp.sum(dpo * zp, axis=0, keepdims=True))
            dz = (dpo * sc_ref[:, cs]).astype(BF16)
            _acc_rows(dpw_ref.at[gi], first, _tn(pb, dz))
            dzh = jnp.where(i < nt - 1, dyh_ref[:, cs] * _silu(gh_ref[:, cs]) * sc_ref[:, cs], 0.0).astype(BF16)
            dpool = _nt(dz, w)
            dpool_h = _nt(dzh, w)
            cnt = jnp.minimum(win, row + 1).astype(F32)
            dn_ref[0:t, cs] = dpool / cnt[0:t]
            dn_ref[t:, cs] = dpool_h / cnt[t:]
            acc = dn_ref[0:t, cs]
            for j in range(1, win):
                acc = acc + dn_ref[j:j + t, cs]
            du_ref[:, cs] = (acc - dpool).astype(BF16)

    return pl.pallas_call(
        body, name=name, grid=(nt,),
        in_specs=[pl.BlockSpec((t, half), lambda i: (i, 3)),
                  pl.BlockSpec((hb, half), lambda i: (jnp.maximum(i * (t // hb) - 1, 0), 3)),
                  pl.BlockSpec((t, d), lambda i: (i, 2)),
                  pl.BlockSpec((hb, half), lambda i: (jnp.minimum((i + 1) * (t // hb), s // hb - 1), 5)),
                  pl.BlockSpec((t, half), lambda i: (i, 0)),
                  pl.BlockSpec((t, d), lambda i: (i, 0)),
                  pl.BlockSpec((hb, half), lambda i: (jnp.minimum((i + 1) * (t // hb), s // hb - 1), 1)),
                  pl.BlockSpec(pool_w.shape, lambda i: (0, 0, 0)),
                  pl.BlockSpec((1, half), lambda i: (0, 0))],
        out_specs=[pl.BlockSpec((t, half), lambda i: (i, 0)),
                   pl.BlockSpec((t, half), lambda i: (i, 0)),
                   pl.BlockSpec((t, d), lambda i: (i, 0)),
                   pl.BlockSpec((1, half), lambda i: (0, 0)),
                   pl.BlockSpec((ng, gd, gd), lambda i: (0, 0, 0))],
        out_shape=[jax.ShapeDtypeStruct((s, half), F32), jax.ShapeDtypeStruct((s, half), BF16),
                   jax.ShapeDtypeStruct((s, d), BF16), jax.ShapeDtypeStruct((1, half), F32),
                   jax.ShapeDtypeStruct((ng, gd, gd), F32)],
        scratch_shapes=[pltpu.VMEM((hb + t, half), F32), pltpu.VMEM((t + hb, half), F32)],
        compiler_params=_cp("arbitrary"))(p, p, p, p, att, dy, dy, pool_w, pool_scale)


def _mm_out_even(y, w, x, g_post, g_pre_next, name):
    s, k = y.shape
    d = w.shape[1]
    t = ROW_TILE

    def body(y_ref, w_ref, x_ref, gp_ref, gn_ref, o_ref, x1_ref, h1_ref):
        o = jnp.dot(y_ref[...], w_ref[...], preferred_element_type=F32)
        o_ref[...] = o
        ohat, _ = _rms_stats(o)
        x1 = x_ref[...] + ohat * gp_ref[...]
        x1_ref[...] = x1
        xhat, _ = _rms_stats(x1)
        h1_ref[...] = (xhat * gn_ref[...]).astype(BF16)

    row = lambda c: pl.BlockSpec((t, c), lambda i: (i, 0))
    vec = pl.BlockSpec((1, d), lambda i: (0, 0))
    return pl.pallas_call(
        body, name=name, grid=(s // t,),
        in_specs=[row(k), pl.BlockSpec((k, d), lambda i: (0, 0)), row(d), vec, vec],
        out_specs=[row(d), row(d), row(d)],
        out_shape=[jax.ShapeDtypeStruct((s, d), F32), jax.ShapeDtypeStruct((s, d), F32),
                   jax.ShapeDtypeStruct((s, d), BF16)],
        compiler_params=_cp("parallel"))(y, w, x, g_post, g_pre_next)


def _mm_out_odd(y, w, x1, g_post, target, name):
    s, k = y.shape
    d = w.shape[1]
    t = ROW_TILE

    def body(y_ref, w_ref, x_ref, gp_ref, tg_ref, do_ref, dx_ref, loss_ref, dgp_ref):
        first = pl.program_id(0) == 0
        o = jnp.dot(y_ref[...], w_ref[...], preferred_element_type=F32)
        ohat, r = _rms_stats(o)
        gp = gp_ref[...]
        diff = x_ref[...] + ohat * gp - tg_ref[...]
        part = 0.5 * jnp.sum(jnp.mean(diff * diff, axis=-1, keepdims=True), axis=0, keepdims=True)
        _acc_rows(loss_ref, first, jnp.broadcast_to(part, loss_ref.shape))
        dx2 = diff * (1.0 / d)
        dx_ref[...] = dx2
        do, dgp = _rms_bwd(dx2, ohat, r, gp)
        do_ref[...] = do.astype(BF16)
        _acc_rows(dgp_ref, first, dgp)

    row = lambda c: pl.BlockSpec((t, c), lambda i: (i, 0))
    vec = pl.BlockSpec((1, d), lambda i: (0, 0))
    return pl.pallas_call(
        body, name=name, grid=(s // t,),
        in_specs=[row(k), pl.BlockSpec((k, d), lambda i: (0, 0)), row(d), vec, row(d)],
        out_specs=[row(d), row(d), pl.BlockSpec((8, LANES), lambda i: (0, 0)), vec],
        out_shape=[jax.ShapeDtypeStruct((s, d), BF16), jax.ShapeDtypeStruct((s, d), F32),
                   jax.ShapeDtypeStruct((8, LANES), F32), jax.ShapeDtypeStruct((1, d), F32)],
        compiler_params=_cp("arbitrary"))(y, w, x1, g_post, target)


def _layer_norm(d1, cg, cb):
    mu = jnp.mean(d1, axis=-1, keepdims=True)
    cen = d1 - mu
    rstd = lax.rsqrt(jnp.mean(cen * cen, axis=-1, keepdims=True) + EPS)
    n = cen * rstd
    return n, rstd, n * cg + cb


SUBLANES = 8
GATHER_PIECES = 4
CONV_ROWS = 64


def _make_shifts(pad_ref, cs, sh_ref):
    rows = sh_ref.shape[1]
    for r in range(1, SUBLANES):
        sh_ref[r - 1] = pad_ref[r:r + rows, cs]


def _by_shift(taps, base, sign=1):
    return sorted(range(taps), key=lambda k: ((sign * (base + k)) % SUBLANES, k))


def _window(pad_ref, cs, sh_ref, off, t):
    m, r = divmod(off, SUBLANES)
    if r == 0:
        return pad_ref[SUBLANES * m:SUBLANES * m + t, cs]
    return sh_ref[r - 1, SUBLANES * m:SUBLANES * m + t, :]


def _odd_mix_fwd(p, sconv_w, dconv_w, dconv_b, cnorm_g, cnorm_b, d, name):
    s = p.shape[0]
    w = d // 2
    k3, k31 = sconv_w.shape[0], dconv_w.shape[0]
    t, hb = ROW_TILE, CONV_HALO
    assert hb >= k31 - 1 and w % LANES == 0

    def body(p_ref, ph_ref, w3_ref, w31_ref, b31_ref, cg_ref, cb_ref, y_ref, s3_ref, d1_ref, mpad, dpad, sh_ref):
        i = pl.program_id(0)
        mpad[0:hb, :] = jnp.where(i > 0, ph_ref[:, 2 * w:3 * w] * ph_ref[:, 0:w], 0.0)
        mpad[hb:, :] = p_ref[:, 2 * w:3 * w] * p_ref[:, 0:w]
        dpad[0:hb, :] = jnp.where(i > 0, ph_ref[:, 3 * w:4 * w] * _sigmoid(ph_ref[:, 4 * w:5 * w]), 0.0)
        dpad[hb:, :] = p_ref[:, 3 * w:4 * w] * _sigmoid(p_ref[:, 4 * w:5 * w])
        for c0 in range(0, w, LANES):
            cs = slice(c0, c0 + LANES)
            acc = jnp.zeros((t, LANES), F32)
            for kk in range(k3):
                acc = acc + w3_ref[kk:kk + 1, cs] * mpad[hb - (k3 - 1) + kk:hb - (k3 - 1) + kk + t, cs]
            s3_ref[:, cs] = acc
            _make_shifts(dpad, cs, sh_ref)
            for r0 in range(0, t, CONV_ROWS):
                acc = jnp.zeros((CONV_ROWS, LANES), F32)
                for kk in _by_shift(k31, hb - (k31 - 1)):
                    acc = acc + w31_ref[kk:kk + 1, cs] * _window(dpad, cs, sh_ref, hb - (k31 - 1) + kk + r0, CONV_ROWS)
                d1_ref[r0:r0 + CONV_ROWS, cs] = acc + b31_ref[:, cs]
        _, _, d2 = _layer_norm(d1_ref[...], cg_ref[...], cb_ref[...])
        y_ref[:, :w] = (p_ref[:, w:2 * w] * s3_ref[...] * _silu(p_ref[:, 5 * w:6 * w])).astype(BF16)
        y_ref[:, w:] = (_silu(d2) * _silu(p_ref[:, 6 * w:7 * w])).astype(BF16)

    row = lambda c: pl.BlockSpec((t, c), lambda i: (i, 0))
    full = lambda a: pl.BlockSpec(a.shape, lambda i: (0, 0))
    return pl.pallas_call(
        body, name=name, grid=(s // t,),
        in_specs=[row(7 * w),
                  pl.BlockSpec((hb, 5 * w), lambda i: (jnp.maximum(i * (t // hb) - 1, 0), 0)),
                  full(sconv_w), full(dconv_w), full(dconv_b), full(cnorm_g), full(cnorm_b)],
        out_specs=[row(d), row(w), row(w)],
        out_shape=[jax.ShapeDtypeStruct((s, d), BF16), jax.ShapeDtypeStruct((s, w), F32),
                   jax.ShapeDtypeStruct((s, w), F32)],
        scratch_shapes=[pltpu.VMEM((hb + t, w), F32)] * 2 + [pltpu.VMEM((SUBLANES - 1, hb + t - SUBLANES, LANES), F32)],
        compiler_params=_cp("parallel"))(p, p, sconv_w, dconv_w, dconv_b, cnorm_g, cnorm_b)


def _odd_bwd_rows(p, s3, d1, dy, cnorm_g, cnorm_b, d, name):
    s = p.shape[0]
    w = d // 2
    t = ROW_TILE

    def body(bc_ref, g1_ref, g2_ref, s3_ref, d1_ref, dy_ref, cg_ref, cb_ref,
             dbc_ref, dg_ref, ds3_ref, dd1_ref, dcg_ref, dcb_ref, db_ref):
        first = pl.program_id(0) == 0
        g1, g2 = g1_ref[...], g2_ref[...]
        bc, s3v = bc_ref[...], s3_ref[...]
        dy1, dy2 = dy_ref[:, :w], dy_ref[:, w:]
        n, rstd, d2 = _layer_norm(d1_ref[...], cg_ref[...], cb_ref[...])
        dg_ref[:, :w] = (dy1 * bc * s3v * _dsilu(g1)).astype(BF16)
        dg_ref[:, w:] = (dy2 * _silu(d2) * _dsilu(g2)).astype(BF16)
        dco = dy1 * _silu(g1)
        dbc_ref[...] = (dco * s3v).astype(BF16)
        ds3_ref[...] = dco * bc
        dd2 = dy2 * _silu(g2) * _dsilu(d2)
        _acc_rows(dcb_ref, first, jnp.sum(dd2, axis=0, keepdims=True))
        _acc_rows(dcg_ref, first, jnp.sum(dd2 * n, axis=0, keepdims=True))
        dn = dd2 * cg_ref[...]
        dd1 = rstd * (dn - jnp.mean(dn, axis=-1, keepdims=True) - n * jnp.mean(dn * n, axis=-1, keepdims=True))
        dd1_ref[...] = dd1
        _acc_rows(db_ref, first, jnp.sum(dd1, axis=0, keepdims=True))

    col = lambda j: pl.BlockSpec((t, w), lambda i: (i, j))
    row = lambda c: pl.BlockSpec((t, c), lambda i: (i, 0))
    vec = pl.BlockSpec((1, w), lambda i: (0, 0))
    return pl.pallas_call(
        body, name=name, grid=(s // t,),
        in_specs=[col(1), col(5), col(6), row(w), row(w), row(d), vec, vec],
        out_specs=[row(w), row(d), row(w), row(w), vec, vec, vec],
        out_shape=[jax.ShapeDtypeStruct((s, w), BF16), jax.ShapeDtypeStruct((s, d), BF16),
                   jax.ShapeDtypeStruct((s, w), F32), jax.ShapeDtypeStruct((s, w), F32)]
        + [jax.ShapeDtypeStruct((1, w), F32)] * 3,
        compiler_params=_cp("arbitrary"))(p, p, p, s3, d1, dy, cnorm_g, cnorm_b)


def _odd_bwd_conv(p, ds3, dd1, sconv_w, dconv_w, d, name):
    s = p.shape[0]
    w = d // 2
    k3, k31 = sconv_w.shape[0], dconv_w.shape[0]
    t, hb, ha = ROW_TILE, CONV_HALO, 8
    nt = s // t
    assert hb >= k31 - 1 and ha >= k3 - 1

    def body(hc_ref, cc_ref, ga_ref, gb_ref, hch_ref, cch_ref, gah_ref, gbh_ref, ds3_ref, ds3h_ref, dd1_ref, dd1h_ref,
             w3_ref, w31_ref, dhc_ref, dcc_ref, dga_ref, dgb_ref, dw3_ref, dw31_ref, mpad, dpad, s3pad, d1pad, sh_ref):
        i = pl.program_id(0)
        first = i == 0
        last = i == nt - 1
        mpad[0:hb, :] = jnp.where(i > 0, cch_ref[...] * hch_ref[...], 0.0)
        mpad[hb:, :] = cc_ref[...] * hc_ref[...]
        dpad[0:hb, :] = jnp.where(i > 0, gah_ref[...] * _sigmoid(gbh_ref[...]), 0.0)
        dpad[hb:, :] = ga_ref[...] * _sigmoid(gb_ref[...])
        s3pad[0:t, :] = ds3_ref[...]
        s3pad[t:, :] = jnp.where(last, 0.0, ds3h_ref[...])
        d1pad[0:t, :] = dd1_ref[...]
        d1pad[t:, :] = jnp.where(last, 0.0, dd1h_ref[...])

        @pl.when(first)
        def _():
            dw3_ref[...] = jnp.zeros_like(dw3_ref)
            dw31_ref[...] = jnp.zeros_like(dw31_ref)

        def fold(v):
            return jnp.sum(v.reshape(v.shape[0] // SUBLANES, SUBLANES, LANES), axis=0)

        groups = range(0, t, CONV_ROWS)
        for c0 in range(0, w, LANES):
            cs = slice(c0, c0 + LANES)
            ds3v = s3pad[0:t, cs]
            dm = jnp.zeros((t, LANES), F32)
            for kk in range(k3):
                dm = dm + w3_ref[kk:kk + 1, cs] * s3pad[k3 - 1 - kk:k3 - 1 - kk + t, cs]
                off = hb - (k3 - 1) + kk
                dw3_ref[SUBLANES * kk:SUBLANES * (kk + 1), cs] += fold(ds3v * mpad[off:off + t, cs])
            dcc_ref[:, cs] = (dm * hc_ref[:, cs]).astype(BF16)
            dhc_ref[:, cs] = (dm * cc_ref[:, cs]).astype(BF16)
            _make_shifts(d1pad, cs, sh_ref)
            for r0 in groups:
                rows = slice(r0, r0 + CONV_ROWS)
                dd0 = jnp.zeros((CONV_ROWS, LANES), F32)
                for kk in _by_shift(k31, -(k31 - 1), -1):
                    dd0 = dd0 + w31_ref[kk:kk + 1, cs] * _window(d1pad, cs, sh_ref, k31 - 1 - kk + r0, CONV_ROWS)
                sgb = _sigmoid(gb_ref[rows, cs])
                dga_ref[rows, cs] = (dd0 * sgb).astype(BF16)
                dgb_ref[rows, cs] = (dd0 * ga_ref[rows, cs] * sgb * (1.0 - sgb)).astype(BF16)
            _make_shifts(dpad, cs, sh_ref)
            for kk in _by_shift(k31, hb - (k31 - 1)):
                part = jnp.zeros((SUBLANES, LANES), F32)
                for r0 in groups:
                    part = part + fold(d1pad[r0:r0 + CONV_ROWS, cs]
                                       * _window(dpad, cs, sh_ref, hb - (k31 - 1) + kk + r0, CONV_ROWS))
                dw31_ref[SUBLANES * kk:SUBLANES * (kk + 1), cs] += part

    col = lambda j: pl.BlockSpec((t, w), lambda i: (i, j))
    pre = lambda j: pl.BlockSpec((hb, w), lambda i: (jnp.maximum(i * (t // hb) - 1, 0), j))
    row = pl.BlockSpec((t, w), lambda i: (i, 0))
    post = lambda h: pl.BlockSpec((h, w), lambda i: (jnp.minimum((i + 1) * (t // h), s // h - 1), 0))
    full = lambda a: pl.BlockSpec(a.shape, lambda i: (0, 0))
    dhc, dcc, dga, dgb, dw3, dw31 = pl.pallas_call(
        body, name=name, grid=(nt,),
        in_specs=[col(0), col(2), col(3), col(4), pre(0), pre(2), pre(3), pre(4),
                  row, post(ha), row, post(hb), full(sconv_w), full(dconv_w)],
        out_specs=[row, row, row, row, pl.BlockSpec((SUBLANES * k3, w), lambda i: (0, 0)),
                   pl.BlockSpec((SUBLANES * k31, w), lambda i: (0, 0))],
        out_shape=[jax.ShapeDtypeStruct((s, w), BF16)] * 4
        + [jax.ShapeDtypeStruct((SUBLANES * k3, w), F32), jax.ShapeDtypeStruct((SUBLANES * k31, w), F32)],
        scratch_shapes=[pltpu.VMEM((hb + t, w), F32)] * 2 + [pltpu.VMEM((t + ha, w), F32), pltpu.VMEM((t + hb, w), F32),
                                                             pltpu.VMEM((SUBLANES - 1, hb + t - SUBLANES, LANES), F32)],
        compiler_params=_cp("arbitrary"))(p, p, p, p, p, p, p, p, ds3, ds3, dd1, dd1, sconv_w, dconv_w)
    return dhc, dcc, dga, dgb, jnp.sum(dw3.reshape(k3, SUBLANES, w), axis=1), jnp.sum(dw31.reshape(k31, SUBLANES, w), axis=1)


def _mm_in_bwd(dp, w3, x, g_pre, dres, post, name, comm=None):
    s = dp.shape[0]
    nsh, d, ns = w3.shape
    t = 512 if s % 512 == 0 else ROW_TILE
    nt = s // t
    ks = 2 if (ns // 2) % LANES == 0 else 1
    nk, kw = nsh * ks, ns // ks
    chunk = 128
    nchunk = t // chunk
    row = pl.BlockSpec((t, d), lambda i, k: (i, 0))
    vec = pl.BlockSpec((1, d), lambda i, k: (0, 0))
    rowwise = [x, dres] + ([post[0]] if post is not None else [])
    in_specs = [pl.BlockSpec((t, kw), lambda i, k: (i, k)), pl.BlockSpec((None, d, kw), lambda i, k: (k // ks, 0, k % ks)), vec]
    out_specs = [row, vec]
    out_shape = [jax.ShapeDtypeStruct((s, d), F32), jax.ShapeDtypeStruct((1, d), F32)]
    args = [dp, w3, g_pre]
    if post is not None:
        in_specs += [vec]
        out_specs += [row, vec]
        out_shape += [jax.ShapeDtypeStruct((s, d), BF16), jax.ShapeDtypeStruct((1, d), F32)]
        args += [post[1]]
    n_blocked = len(in_specs)
    in_specs += [ANY] * len(rowwise)
    args += rowwise
    host = _Host(comm, in_specs, out_specs, out_shape,
                 [pltpu.VMEM((t, d), F32), pltpu.VMEM((len(rowwise), 2, chunk, d), F32), pltpu.SemaphoreType.DMA((len(rowwise), 2))])

    def body(*refs):
        ins, outs, (acc_ref, buf_ref, sem_ref) = host.split(refs)
        dp_ref, w_ref, g_ref = ins[:3]
        hbm = ins[n_blocked:]
        dx_ref, dg_ref = outs[:2]
        tile = pl.program_id(0)
        kk = pl.program_id(1)
        first = tile == 0
        step = tile * nk + kk
        host.before(step, nt * nk)
        part = _nt(dp_ref[...], w_ref[...])

        @pl.when(kk == 0)
        def _():
            acc_ref[...] = part

        @pl.when(kk > 0)
        def _():
            acc_ref[...] += part

        def fetch(ci, slot):
            return [pltpu.make_async_copy(src.at[pl.ds(tile * t + ci * chunk, chunk)], buf_ref.at[n, slot], sem_ref.at[n, slot])
                    for n, src in enumerate(hbm)]

        @pl.when(kk == nk - 1)
        def _():
            dg = dgp = None
            for cp in fetch(0, 0):
                cp.start()
            for ci in range(nchunk):
                slot = ci % 2
                if ci + 1 < nchunk:
                    for cp in fetch(ci + 1, 1 - slot):
                        cp.start()
                for cp in fetch(ci, slot):
                    cp.wait()
                rows = slice(ci * chunk, (ci + 1) * chunk)
                xhat, r = _rms_stats(buf_ref[0, slot])
                dxn, dg_part = _rms_bwd(acc_ref[rows, :], xhat, r, g_ref[...])
                dx = buf_ref[1, slot] + dxn
                dx_ref[rows, :] = dx
                dg = dg_part if dg is None else dg + dg_part
                if post is not None:
                    ohat, ro = _rms_stats(buf_ref[2, slot])
                    do, dgp_part = _rms_bwd(dx, ohat, ro, ins[3][...])
                    outs[2][rows, :] = do.astype(BF16)
                    dgp = dgp_part if dgp is None else dgp + dgp_part
            _acc_rows(dg_ref, first, dg)
            if post is not None:
                _acc_rows(outs[3], first, dgp)

        host.after(step, nt * nk)

    res = pl.pallas_call(
        body, name=name, grid=(nt, nk), in_specs=host.in_specs, out_specs=host.out_specs, out_shape=host.out_shape,
        scratch_shapes=host.scratch, input_output_aliases=host.aliases,
        compiler_params=_cp("arbitrary", "arbitrary"))(*args, *host.args)
    return host.results(res)


def _half_add(g, r1, c_arr, name):
    nsh, rows, ns = g.shape
    h = rows // 2
    tr = min(ROW_TILE, h)
    per = h // tr

    def body(c_ref, g_ref, r_ref, o_ref):
        o_ref[...] = (g_ref[...].astype(F32) + r_ref[...].astype(F32)).astype(BF16)

    spec = pl.BlockSpec((None, tr, ns), lambda s, r, c: (s, r, 0))
    return pl.pallas_call(
        body, name=name,
        grid_spec=pltpu.PrefetchScalarGridSpec(
            num_scalar_prefetch=1, grid=(nsh, per),
            in_specs=[pl.BlockSpec((None, tr, ns), lambda s, r, c: (s, c[0] * per + r, 0)), spec], out_specs=spec),
        out_shape=jax.ShapeDtypeStruct((nsh, h, ns), BF16), compiler_params=_cp("parallel", "parallel"))(c_arr, g, r1)


def _sum_chips(hh, r2, mc_arr, name):
    _, h, ns = hh.shape
    tr = min(ROW_TILE, h)
    per = h // tr

    def body(mc_ref, h_ref, a_ref, b_ref, c_ref, o_ref):
        o_ref[...] = ((h_ref[...].astype(F32) + a_ref[...].astype(F32)) + b_ref[...].astype(F32)) + c_ref[...].astype(F32)

    got = lambda k: pl.BlockSpec((None, tr, ns), lambda r, mc: (k, r, 0))
    return pl.pallas_call(
        body, name=name,
        grid_spec=pltpu.PrefetchScalarGridSpec(
            num_scalar_prefetch=1, grid=(per,),
            in_specs=[pl.BlockSpec((None, tr, ns), lambda r, mc: (mc[0], r, 0)), got(0), got(1), got(2)],
            out_specs=pl.BlockSpec((tr, ns), lambda r, mc: (mc[1] * per + r, 0))),
        out_shape=jax.ShapeDtypeStruct((2 * h, ns), F32), compiler_params=_cp("parallel"))(mc_arr, hh, r2, r2, r2)


def _add2(a, b, name):
    def body(a_ref, b_ref, o_ref):
        o_ref[...] = a_ref[...] + b_ref[...]

    return pl.pallas_call(body, name=name, out_shape=jax.ShapeDtypeStruct(a.shape, a.dtype), compiler_params=_cp())(a, b)


def _sum_chips_ordered(s2, r2, mc_arr, name):
    rows, w = s2.shape
    rh = rows // 2

    def body(mc_ref, s_ref, a_ref, b_ref, c_ref, o_ref):
        me = mc_ref[0]
        acc = None
        for j in range(N_CHIPS):
            rel = jnp.bitwise_xor(me, j)
            v = jnp.where(rel == 0, s_ref[...], jnp.where(rel == 2, a_ref[...], jnp.where(rel == 1, b_ref[...], c_ref[...])))
            acc = v if acc is None else acc + v
        o_ref[...] = acc

    got = lambda k: pl.BlockSpec((None, rh, w), lambda i, mc: (k, 0, 0))
    return pl.pallas_call(
        body, name=name,
        grid_spec=pltpu.PrefetchScalarGridSpec(
            num_scalar_prefetch=1, grid=(1,),
            in_specs=[pl.BlockSpec((rh, w), lambda i, mc: (mc[1], 0)), got(0), got(1), got(2)],
            out_specs=pl.BlockSpec((rh, w), lambda i, mc: (mc[1], 0))),
        out_shape=jax.ShapeDtypeStruct((rows, w), F32), compiler_params=_cp("arbitrary"))(mc_arr, s2, r2, r2, r2)


def _adamw(w, g, m, v, name, comm=None):
    r, c = w.shape
    tr = ROW_TILE if r % ROW_TILE == 0 else r
    c1 = 1.0 / (1.0 - ADAM_B1 ** ADAM_STEP)
    c2 = 1.0 / (1.0 - ADAM_B2 ** ADAM_STEP)
    spec = pl.BlockSpec((tr, c), lambda i: (i, 0))
    host = _Host(comm, [spec] * 4, [spec] * 4, [jax.ShapeDtypeStruct((r, c), F32)] * 4, [])

    def body(*refs):
        (w_ref, g_ref, m_ref, v_ref), (go_ref, d_ref, nm_ref, nv_ref), _ = host.split(refs)
        step = pl.program_id(0)
        host.before(step, r // tr)
        gv = g_ref[...]
        go_ref[...] = gv
        nm = ADAM_B1 * m_ref[...] + (1.0 - ADAM_B1) * gv
        nv = ADAM_B2 * v_ref[...] + (1.0 - ADAM_B2) * (gv * gv)
        nm_ref[...] = nm
        nv_ref[...] = nv
        d_ref[...] = -ADAM_LR * ((nm * c1) / (jnp.sqrt(nv * c2) + ADAM_EPS) + ADAM_WD * w_ref[...])
        host.after(step, r // tr)

    outs = pl.pallas_call(
        body, name=name, grid=(r // tr,), in_specs=host.in_specs, out_specs=host.out_specs, out_shape=host.out_shape,
        scratch_shapes=host.scratch, input_output_aliases=host.aliases,
        compiler_params=_cp("arbitrary"))(w, g, m, v, *host.args)
    return host.results(outs)


def _gather_weights(bigs, pool_w, pack_w, pack_d, name):
    nb = len(bigs)
    smalls = [pool_w, pack_w, pack_d]
    q, cw, cd = pool_w.shape[1], pack_w.shape[1], pack_d.shape[1]
    pieces = [_GatherPlan(bigs, (j, j + 1, GATHER_PIECES)) for j in range(GATHER_PIECES)]
    for j, piece in enumerate(pieces):
        piece.base = 9 + j * piece.nsems

    def body(*refs):
        srcs, dsts = refs[:nb + 3], refs[nb + 3:2 * (nb + 3)]
        ssem, rsem, lsem = refs[2 * (nb + 3):]
        x, y, c, me, chips, sib = _place()

        def small_dst(n, chip):
            if n == 0:
                return dsts[nb].at[:, pl.ds(chip * q, q), :]
            return dsts[nb + n].at[:, pl.ds(chip * (cw if n == 1 else cd), cw if n == 1 else cd)]

        local = [pltpu.make_async_copy(srcs[nb + n], small_dst(n, me), lsem.at[n]) for n in range(3)]
        for cp in local:
            cp.start()
        sends = []
        for n in range(3):
            for k, chip in enumerate(chips):
                cp = _rcopy(srcs[nb + n], small_dst(n, me), ssem.at[3 * n + k], rsem.at[3 * n + k], (*chip, c))
                cp.start()
                sends.append(cp)
        big = (srcs[:nb], dsts[:nb], ssem, rsem)
        for stage in ("start", "relay", "relay_far", "finish"):
            for piece in pieces:
                getattr(piece, stage)(*big)
        for n in range(3):
            for k, chip in enumerate(chips):
                ref = small_dst(n, 2 * chip[0] + chip[1])
                _rcopy(ref, ref, ssem.at[3 * n + k], rsem.at[3 * n + k], (*chip, c)).wait_recv()
        for cp in sends:
            cp.wait_send()
        for cp in local:
            cp.wait()

    nsem = 9 + sum(piece.nsems for piece in pieces)
    out_shape = [jax.ShapeDtypeStruct(b.shape, b.dtype) for b in bigs]
    out_shape += [jax.ShapeDtypeStruct((pool_w.shape[0], N_CHIPS * q, pool_w.shape[2]), pool_w.dtype),
                  jax.ShapeDtypeStruct((pack_w.shape[0], N_CHIPS * cw), pack_w.dtype),
                  jax.ShapeDtypeStruct((pack_d.shape[0], N_CHIPS * cd), pack_d.dtype)]
    return pl.pallas_call(
        body, name=name, in_specs=[ANY] * (nb + 3), out_specs=[ANY] * (nb + 3), out_shape=out_shape,
        input_output_aliases={a: a for a in range(nb)},
        scratch_shapes=[pltpu.SemaphoreType.DMA((nsem,)), pltpu.SemaphoreType.DMA((nsem,)), pltpu.SemaphoreType.DMA((3,))],
        compiler_params=pltpu.CompilerParams(has_side_effects=True))(*bigs, *smalls)


def _swap_with_sibling(grads, wholes, name):
    n, nw = len(grads), len(wholes)
    halves = [g.shape[1] // 2 for g in grads]

    def body(*refs):
        srcs, dsts = refs[:n + nw], refs[n + nw:2 * (n + nw)]
        ssem, rsem = refs[2 * (n + nw):]
        x, y, c, me, chips, sib = _place()
        cps = [_rcopy(srcs[a].at[:, pl.ds((1 - c) * halves[a], halves[a]), :], dsts[a], ssem.at[a], rsem.at[a], sib)
               for a in range(n)]
        cps += [_rcopy(srcs[a], dsts[a], ssem.at[a], rsem.at[a], sib) for a in range(n, n + nw)]
        for cp in cps:
            cp.start()
        for cp in cps:
            cp.wait_recv()
        for cp in cps:
            cp.wait_send()

    out_shape = [jax.ShapeDtypeStruct((g.shape[0], h, g.shape[2]), g.dtype) for g, h in zip(grads, halves)]
    out_shape += [jax.ShapeDtypeStruct(w.shape, w.dtype) for w in wholes]
    return pl.pallas_call(
        body, name=name, in_specs=[ANY] * (n + nw), out_specs=[ANY] * (n + nw), out_shape=out_shape,
        scratch_shapes=[pltpu.SemaphoreType.DMA((n + nw,)), pltpu.SemaphoreType.DMA((n + nw,))],
        compiler_params=pltpu.CompilerParams(has_side_effects=True))(*grads, *wholes)


def _scatter_to_chips(halves_in, small, name):
    n = len(halves_in)
    rh = small.shape[0] // 2

    def body(*refs):
        srcs, dsts = refs[:n + 1], refs[n + 1:2 * (n + 1)]
        ssem, rsem = refs[2 * (n + 1):]
        x, y, c, me, chips, sib = _place()
        cps = []
        for a in range(n + 1):
            for k, chip in enumerate(chips):
                src = srcs[a].at[2 * chip[0] + chip[1]] if a < n else srcs[a].at[pl.ds(c * rh, rh)]
                cps.append(_rcopy(src, dsts[a].at[k], ssem.at[3 * a + k], rsem.at[3 * a + k], (*chip, c)))
        for cp in cps:
            cp.start()
        for cp in cps:
            cp.wait_recv()
        for cp in cps:
            cp.wait_send()

    out_shape = [jax.ShapeDtypeStruct((3,) + h.shape[1:], h.dtype) for h in halves_in]
    out_shape.append(jax.ShapeDtypeStruct((3, rh, small.shape[1]), small.dtype))
    return pl.pallas_call(
        body, name=name, in_specs=[ANY] * (n + 1), out_specs=[ANY] * (n + 1), out_shape=out_shape,
        scratch_shapes=[pltpu.SemaphoreType.DMA((3 * (n + 1),)), pltpu.SemaphoreType.DMA((3 * (n + 1),))],
        compiler_params=pltpu.CompilerParams(has_side_effects=True))(*halves_in, small)


def _join_halves(parts, name):
    n = len(parts)

    def body(*refs):
        srcs, dsts = refs[:n], refs[n:2 * n]
        ssem, rsem = refs[2 * n:]
        x, y, c, me, chips, sib = _place()
        cps = []
        for a in range(n):
            h = srcs[a].shape[0] // 2
            cps.append(_rcopy(srcs[a].at[pl.ds(c * h, h)], dsts[a].at[pl.ds(c * h, h)], ssem.at[a], rsem.at[a], sib))
        for cp in cps:
            cp.start()
        for a in range(n):
            h = srcs[a].shape[0] // 2
            theirs = dsts[a].at[pl.ds((1 - c) * h, h)]
            _rcopy(theirs, theirs, ssem.at[a], rsem.at[a], sib).wait_recv()
        for cp in cps:
            cp.wait_send()

    out_shape = [jax.ShapeDtypeStruct(p.shape, p.dtype) for p in parts]
    return pl.pallas_call(
        body, name=name, in_specs=[ANY] * n, out_specs=[ANY] * n, out_shape=out_shape,
        input_output_aliases={a: a for a in range(n)},
        scratch_shapes=[pltpu.SemaphoreType.DMA((n,)), pltpu.SemaphoreType.DMA((n,))],
        compiler_params=pltpu.CompilerParams(has_side_effects=True))(*parts)


def _pad_rows(a, rows):
    return jnp.pad(a, ((0, rows - a.shape[0]), (0, 0)))


def _stack_rows(parts, multiple):
    padded = [_pad_rows(p, -(-p.shape[0] // 8) * 8) for p in parts]
    starts, at = [], 0
    for p in padded:
        starts.append(at)
        at += p.shape[0]
    total = -(-at // multiple) * multiple
    if total > at:
        padded.append(jnp.zeros((total - at, parts[0].shape[1]), parts[0].dtype))
    return jnp.concatenate(padded, axis=0), starts


def kernel(x, ln_pre_even, w_in_even, pool_w, pool_scale, w_out_even, ln_post_even, ln_pre_odd, w_in_odd, sconv_w, dconv_w, dconv_b, cnorm_g, cnorm_b, w_out_odd, ln_post_odd, loss_target, m_ln_pre_even, m_w_in_even, m_pool_w, m_pool_scale, m_w_out_even, m_ln_post_even, m_ln_pre_odd, m_w_in_odd, m_sconv_w, m_dconv_w, m_dconv_b, m_cnorm_g, m_cnorm_b, m_w_out_odd, m_ln_post_odd, v_ln_pre_even, v_w_in_even, v_pool_w, v_pool_scale, v_w_out_even, v_ln_post_even, v_ln_pre_odd, v_w_in_odd, v_sconv_w, v_dconv_w, v_dconv_b, v_cnorm_g, v_cnorm_b, v_w_out_odd, v_ln_post_odd):
    _, s, d = x.shape
    half = d // 2
    cw = half // N_CHIPS
    ng, q, gd = pool_w.shape[1:]
    k3, k31 = sconv_w.shape[1], dconv_w.shape[1]
    x2d, tgt = x[0], loss_target[0]
    me = 2 * lax.axis_index("x") + lax.axis_index("y")
    core = lax.axis_index("c")
    c_arr = jnp.reshape(core, (1,)).astype(jnp.int32)
    me_arr = jnp.reshape(me, (1,)).astype(jnp.int32)
    mc_arr = jnp.stack([me, core]).astype(jnp.int32)

    shards = [w_in_even[0], w_out_even[0], w_in_odd[0], w_out_odd[0]]
    slabs = [_cast_bf16_own_slab(w, me_arr, f"cast_w{n}") for n, w in enumerate(shards)]
    pool_w_b = _cast_bf16(pool_w[0].reshape(ng * q, gd), "cast_pool_w").reshape(ng, q, gd)
    pack_w, at_w = _stack_rows([sconv_w[0], dconv_w[0], dconv_b, cnorm_g, cnorm_b], 8)
    pack_d, at_d = _stack_rows([ln_pre_odd, ln_post_odd], 8)
    win_e, pool_w_f, pack_w_f, pack_d_f = _gather_weights(slabs[:1], pool_w_b, pack_w, pack_d, "gather_first")
    sconv_f = pack_w_f[at_w[0]:at_w[0] + k3]
    dconv_f = pack_w_f[at_w[1]:at_w[1] + k31]
    dconv_b_f, cnorm_g_f, cnorm_b_f = (pack_w_f[at_w[n]:at_w[n] + 1] for n in (2, 3, 4))
    ln_pre_odd_f = pack_d_f[at_d[0]:at_d[0] + 1]
    ln_post_odd_f = pack_d_f[at_d[1]:at_d[1] + 1]

    def reduce_half(g, name):
        (got,) = _swap_with_sibling([g], [], "swap_" + name)
        return _half_add(g, got, c_arr, "half_add_" + name)

    h0 = _rms_fwd(x2d, ln_pre_even, "rms_pre_even")
    plans = _Multi([_GatherPlan([slabs[1]], at=(0.6, 0.88)), _GatherPlan([slabs[2]], (0, 1, 4), at=(0.6, 0.88))])
    p_e, extra = _mm_nn(h0, win_e, "proj_in_even", plans)
    (wout_e,), (win_o,) = plans.results(extra)
    wout_e = wout_e.reshape(d, d)
    att, ltot, (win_o,) = _sba_fwd(p_e, half, "sba_fwd", _GatherPlan([win_o], (1, 4, 4), at=(0.69, 0.94)))
    y_e = _even_mix_fwd(p_e, att, pool_w_f, pool_scale, d, "even_mix_fwd")
    o_e, x1, h1 = _mm_out_even(y_e, wout_e, x2d, ln_post_even, ln_pre_odd_f, "proj_out_even")
    p_o, (wout_o,) = _mm_nn(h1, win_o, "proj_in_odd", _GatherPlan([slabs[3]]))
    wout_o = wout_o.reshape(d, d)
    y_o, s3, d1 = _odd_mix_fwd(p_o, sconv_f, dconv_f, dconv_b_f, cnorm_g_f, cnorm_b_f, d, "odd_mix_fwd")
    do_o, dx2, loss_blk, dln_post_odd = _mm_out_odd(y_o, wout_o, x1, ln_post_odd_f, tgt, "proj_out_odd_loss")

    dy_o = _mm_nt(do_o, wout_o, "dy_odd")
    g_wout_o = _mm_tn(y_o, do_o, 1, "dw_out_odd")[0].reshape(N_CHIPS, d // N_CHIPS, d)
    h_wout_o = reduce_half(g_wout_o, "out_odd")
    dbc, dgate_o, ds3, dd1, dcnorm_g, dcnorm_b, ddconv_b = _odd_bwd_rows(p_o, s3, d1, dy_o, cnorm_g_f, cnorm_b_f, d, "odd_bwd_rows")
    dhc, dcc, dga, dgb, dsconv, ddconv = _odd_bwd_conv(p_o, ds3, dd1, sconv_f, dconv_f, d, "odd_bwd_conv")
    dp_o = jnp.concatenate([dhc, dbc, dcc, dga, dgb, dgate_o], axis=1)
    g_win_o, (s_wout_o,) = _mm_tn(h1, dp_o, N_CHIPS, "dw_in_odd", _ScatterPlan([h_wout_o]))
    h_win_o = reduce_half(g_win_o, "in_odd")
    (dx1, dln_pre_odd, do_e, dln_post_even), (s_win_o,) = _mm_in_bwd(
        dp_o, win_o, x1, ln_pre_odd_f, dx2, (o_e, ln_post_even), "dx_odd", _ScatterPlan([h_win_o], (0, 1, 2)))

    dy_e = _mm_nt(do_e, wout_e, "dy_even")
    g_wout_e = _mm_tn(y_e, do_e, 1, "dw_out_even")[0].reshape(N_CHIPS, d // N_CHIPS, d)
    h_wout_e = reduce_half(g_wout_e, "out_even")
    datt, du, dgate_e, dpool_scale, dpool_w = _even_mix_bwd(p_e, att, dy_e, pool_w_f, pool_scale, d, "even_mix_bwd")
    two = lambda v: v.reshape(2, half)
    small_parts = [dpool_scale, two(dln_post_even), two(dln_pre_odd), two(dln_post_odd),
                   dsconv, ddconv, ddconv_b, dcnorm_g, dcnorm_b, dpool_w.reshape(gd, half)]
    small, at_s = _stack_rows(small_parts, 16)
    (small1,) = _swap_with_sibling([], [small], "swap_small")
    small2 = _add2(small, small1, "small_add")
    plans = _Multi([_ScatterPlan([h_win_o], (1, 2, 2), into=[s_win_o]), _ScatterPlan([h_wout_e]), _ShareHalfPlan([small2])])
    dq, dk, dv, extra = _sba_bwd(p_e, ltot, datt, half, "sba_bwd", plans)
    (s_win_o,), (s_wout_e,), (small_got,) = plans.results(extra)
    dp_e = jnp.concatenate([dq, dk, dv, du, dgate_e], axis=1)
    g_win_e, _ = _mm_tn(h0, dp_e, N_CHIPS, "dw_in_even")
    h_win_e = reduce_half(g_win_e, "in_even")
    (grad_x, dln_pre_even), (s_win_e,) = _mm_in_bwd(dp_e, win_e, x2d, ln_pre_even, dx1, None, "dx_even",
                                                    _ScatterPlan([h_win_e], (0, 3, 4)))

    last, at_l = _stack_rows([two(dln_pre_even), jnp.pad(loss_blk[0:1], ((0, 0), (0, half - LANES)))], 16)
    (last1,) = _swap_with_sibling([], [last], "swap_last")
    last2 = _add2(last, last1, "last_add")
    (last_got,) = _scatter_to_chips([], last2, "scatter_last")
    pairs = [(h_wout_e, s_wout_e), (h_win_o, s_win_o), (h_wout_o, s_wout_o)]
    parts = [_sum_chips(h, r, mc_arr, f"sum_chips{n + 1}") for n, (h, r) in enumerate(pairs)]
    parts.append(_sum_chips_ordered(small2, small_got, mc_arr, "small_sum"))
    parts.append(_sum_chips_ordered(last2, last_got, mc_arr, "last_sum"))
    gw_out_e, gw_in_o, gw_out_o, red, red_last = _join_halves(parts, "join_first")
    loss = red_last[at_l[1], 0]

    def rows(n, cnt):
        return red[at_s[n]:at_s[n] + cnt]

    def mine(a, width):
        return lax.dynamic_slice_in_dim(a, me * width, width, axis=1)

    quarter = d // N_CHIPS
    g_small = {
        "ln_pre_even": red_last[at_l[0]:at_l[0] + 2].reshape(1, d),
        "pool_scale": rows(0, 1),
        "ln_post_even": rows(1, 2).reshape(1, d),
        "ln_pre_odd": mine(rows(2, 2).reshape(1, d), quarter),
        "ln_post_odd": mine(rows(3, 2).reshape(1, d), quarter),
        "sconv_w": mine(rows(4, k3), cw),
        "dconv_w": mine(rows(5, k31), cw),
        "dconv_b": mine(rows(6, 1), cw),
        "cnorm_g": mine(rows(7, 1), cw),
        "cnorm_b": mine(rows(8, 1), cw),
        "pool_w": lax.dynamic_slice_in_dim(rows(9, gd).reshape(ng, gd, gd), me * q, q, axis=1).reshape(ng * q, gd),
    }
    w2d = {
        "ln_pre_even": ln_pre_even, "w_in_even": w_in_even[0], "pool_w": pool_w[0].reshape(ng * q, gd),
        "pool_scale": pool_scale, "w_out_even": w_out_even[0], "ln_post_even": ln_post_even, "ln_pre_odd": ln_pre_odd,
        "w_in_odd": w_in_odd[0], "sconv_w": sconv_w[0], "dconv_w": dconv_w[0], "dconv_b": dconv_b, "cnorm_g": cnorm_g,
        "cnorm_b": cnorm_b, "w_out_odd": w_out_odd[0], "ln_post_odd": ln_post_odd,
    }
    moments = {
        "ln_pre_even": (m_ln_pre_even, v_ln_pre_even), "w_in_even": (m_w_in_even, v_w_in_even),
        "pool_w": (m_pool_w, v_pool_w), "pool_scale": (m_pool_scale, v_pool_scale),
        "w_out_even": (m_w_out_even, v_w_out_even), "ln_post_even": (m_ln_post_even, v_ln_post_even),
        "ln_pre_odd": (m_ln_pre_odd, v_ln_pre_odd), "w_in_odd": (m_w_in_odd, v_w_in_odd),
        "sconv_w": (m_sconv_w, v_sconv_w), "dconv_w": (m_dconv_w, v_dconv_w), "dconv_b": (m_dconv_b, v_dconv_b),
        "cnorm_g": (m_cnorm_g, v_cnorm_g), "cnorm_b": (m_cnorm_b, v_cnorm_b),
        "w_out_odd": (m_w_out_odd, v_w_out_odd), "ln_post_odd": (m_ln_post_odd, v_ln_post_odd),
    }
    def update(name, g, comm=None):
        m_in, v_in = moments[name]
        w = w2d[name]
        return _adamw(w, g, m_in.reshape(w.shape), v_in.reshape(w.shape), "adamw_" + name, comm)

    updates = {}
    updates["w_in_odd"], (s_win_e,) = update("w_in_odd", gw_in_o, _ScatterPlan([h_win_e], (3, 4, 4), into=[s_win_e]))
    (gw_in_e,) = _join_halves([_sum_chips(h_win_e, s_win_e, mc_arr, "sum_chips0")], "join_last")
    g2d = dict(g_small, w_in_even=gw_in_e, w_out_even=gw_out_e, w_out_odd=gw_out_o)
    for name, g in g2d.items():
        updates[name], _ = update(name, g)
    outs = [[u.reshape(moments[name][0].shape) for u in updates[name]] for name in w2d]
    grads_out, deltas, new_m, new_v = zip(*outs)
    return (loss, grad_x.reshape(x.shape), *grads_out, *deltas, *new_m, *new_v)
```

```python
import functools
import math

import jax
import jax.numpy as jnp
from jax import lax
from jax.experimental import pallas as pl
from jax.experimental.pallas import tpu as pltpu

F32 = jnp.float32
BF16 = jnp.bfloat16
EPS = 1e-6
N_CHIPS = 4
VMEM_LIMIT_V7X = 56 << 20
HEAD_DIM = 128
ATT_BLOCK = 256
POOL_WINDOWS = (2, 4, 8, 16)
ROW_TILE = 256
POOL_HALO = 16
CONV_HALO = 32
LANES = 128
ADAM_LR, ADAM_B1, ADAM_B2, ADAM_EPS, ADAM_WD, ADAM_STEP = 0.001, 0.9, 0.999, 1e-08, 0.01, 10
MESH_ID = pl.DeviceIdType.MESH
ANY = pl.BlockSpec(memory_space=pl.ANY)


def _cp(*sem):
    return pltpu.CompilerParams(dimension_semantics=sem or None, vmem_limit_bytes=VMEM_LIMIT_V7X)


def _pick_tile(n, cap):
    best = None
    for t in range(LANES, min(n, cap) + 1, LANES):
        if n % t == 0:
            best = t
    assert best is not None, (n, cap)
    return best


def _sigmoid(x):
    return 1.0 / (1.0 + jnp.exp(-x))


def _silu(x):
    return x * _sigmoid(x)


def _dsilu(x):
    s = _sigmoid(x)
    return s * (1.0 + x * (1.0 - s))


def _log_sigmoid(z):
    return jnp.minimum(z, 0.0) - jnp.log(1.0 + jnp.exp(-jnp.abs(z)))


def _rms_stats(x):
    r = lax.rsqrt(jnp.mean(x * x, axis=-1, keepdims=True) + EPS)
    return x * r, r


def _rms_bwd(dh, xhat, r, g):
    dxh = dh * g
    dx = r * (dxh - xhat * jnp.mean(dxh * xhat, axis=-1, keepdims=True))
    return dx, jnp.sum(dh * xhat, axis=0, keepdims=True)


def _acc_rows(ref, first, val):
    @pl.when(first)
    def _():
        ref[...] = val

    @pl.when(jnp.logical_not(first))
    def _():
        ref[...] += val


def _rcopy(src, dst, ssem, rsem, dev):
    return pltpu.make_async_remote_copy(src_ref=src, dst_ref=dst, send_sem=ssem, recv_sem=rsem,
                                        device_id=dev, device_id_type=MESH_ID)


def _place():
    x, y, c = lax.axis_index("x"), lax.axis_index("y"), lax.axis_index("c")
    chips = [(1 - x, y), (x, 1 - y), (1 - x, 1 - y)]
    return x, y, c, 2 * x + y, chips, (x, y, 1 - c)


class _GatherPlan:
    PER_ARRAY = 7

    def __init__(self, arrays, part=(0, 1, 1), at=(0.5, 0.8)):
        self.operands = list(arrays)
        self.out_shapes = [jax.ShapeDtypeStruct(a.shape, a.dtype) for a in arrays]
        self.aliases = {i: i for i in range(len(arrays))}
        self.nsems = self.PER_ARRAY * len(arrays)
        self.base = 0
        self.halves = [a.shape[1] // 2 for a in arrays]
        self.part = part
        self.at = at

    def schedule(self):
        return [(0.0, self.start), (self.at[0], self.relay), (self.at[1], self.relay_far)]

    def _rows(self, ref, a, chip, half, quarter=None):
        lo, hi, n = self.part
        h = self.halves[a]
        first, size = half * h + lo * h // n, (hi - lo) * h // n
        if quarter is not None:
            first, size = first + quarter * (size // 2), size // 2
        return ref.at[chip, pl.ds(first, size)]

    def _copy(self, src, dst, a, n, ssem, rsem, dev):
        return _rcopy(src, dst, ssem.at[self.base + self.PER_ARRAY * a + n], rsem.at[self.base + self.PER_ARRAY * a + n], dev)

    def _own(self, ins, outs, ssem, rsem):
        x, y, c, me, chips, sib = _place()
        return [self._copy(self._rows(ins[a], a, me, c), self._rows(outs[a], a, me, c), a, k, ssem, rsem, (*chips[k], c))
                for a in range(len(ins)) for k in (0, 1)]

    def _relays(self, outs, ssem, rsem, a, k):
        x, y, c, me, chips, sib = _place()
        chip = 2 * chips[k][0] + chips[k][1]
        whole, quarter = self._rows(outs[a], a, chip, c), self._rows(outs[a], a, chip, c, k)
        return (self._copy(whole, whole, a, k, ssem, rsem, (*chips[k], c)),
                self._copy(quarter, quarter, a, 2 + k, ssem, rsem, (*chips[1 - k], c)),
                self._copy(whole, whole, a, 4 + k, ssem, rsem, sib))

    def _far(self, outs, ssem, rsem, a):
        x, y, c, me, chips, sib = _place()
        chip = 2 * chips[2][0] + chips[2][1]
        whole = self._rows(outs[a], a, chip, c)
        got = [self._copy(q, q, a, 2 + k, ssem, rsem, (*chips[1 - k], c))
               for k, q in enumerate([self._rows(outs[a], a, chip, c, 0), self._rows(outs[a], a, chip, c, 1)])]
        return got, self._copy(whole, whole, a, 6, ssem, rsem, sib)

    def start(self, ins, outs, ssem, rsem):
        for cp in self._own(ins, outs, ssem, rsem):
            cp.start()

    def relay(self, ins, outs, ssem, rsem):
        for a in range(len(outs)):
            for k in (0, 1):
                landed, onward, to_sibling = self._relays(outs, ssem, rsem, a, k)
                landed.wait_recv()
                onward.start()
                to_sibling.start()

    def relay_far(self, ins, outs, ssem, rsem):
        for a in range(len(outs)):
            got, to_sibling = self._far(outs, ssem, rsem, a)
            for cp in got:
                cp.wait_recv()
            to_sibling.start()

    def finish(self, ins, outs, ssem, rsem):
        x, y, c, me, chips, sib = _place()
        for a in range(len(outs)):
            for k in range(3):
                ref = self._rows(outs[a], a, 2 * chips[k][0] + chips[k][1], 1 - c)
                self._copy(ref, ref, a, 4 + k, ssem, rsem, sib).wait_recv()
        for cp in self._own(ins, outs, ssem, rsem):
            cp.wait_send()
        for a in range(len(outs)):
            for k in (0, 1):
                _, onward, to_sibling = self._relays(outs, ssem, rsem, a, k)
                onward.wait_send()
                to_sibling.wait_send()
            self._far(outs, ssem, rsem, a)[1].wait_send()


class _ScatterPlan:
    def __init__(self, arrays, part=(0, 1, 1), into=None):
        self.n = len(arrays)
        self.operands = list(arrays) + list(into or [])
        self.out_shapes = [jax.ShapeDtypeStruct((3,) + a.shape[1:], a.dtype) for a in arrays]
        self.aliases = {self.n + i: i for i in range(self.n)} if into else {}
        self.nsems = 3 * self.n
        self.base = 0
        self.part = part

    def _copies(self, ins, outs, ssem, rsem):
        x, y, c, me, chips, sib = _place()
        lo, hi, n = self.part
        out = []
        for a in range(self.n):
            h = ins[a].shape[1]
            rows = pl.ds(lo * h // n, (hi - lo) * h // n)
            for k, chip in enumerate(chips):
                out.append(_rcopy(ins[a].at[2 * chip[0] + chip[1], rows], outs[a].at[k, rows],
                                  ssem.at[self.base + 3 * a + k], rsem.at[self.base + 3 * a + k], (*chip, c)))
        return out

    def schedule(self):
        return [(0.0, self.start)]

    def start(self, ins, outs, ssem, rsem):
        for cp in self._copies(ins, outs, ssem, rsem):
            cp.start()

    def finish(self, ins, outs, ssem, rsem):
        cps = self._copies(ins, outs, ssem, rsem)
        for cp in cps:
            cp.wait_recv()
        for cp in cps:
            cp.wait_send()


class _ShareHalfPlan(_ScatterPlan):
    def __init__(self, arrays):
        super().__init__(arrays)
        self.out_shapes = [jax.ShapeDtypeStruct((3, a.shape[0] // 2, a.shape[1]), a.dtype) for a in arrays]

    def _copies(self, ins, outs, ssem, rsem):
        x, y, c, me, chips, sib = _place()
        out = []
        for a in range(self.n):
            rh = ins[a].shape[0] // 2
            for k, chip in enumerate(chips):
                out.append(_rcopy(ins[a].at[pl.ds(c * rh, rh)], outs[a].at[k],
                                  ssem.at[self.base + 3 * a + k], rsem.at[self.base + 3 * a + k], (*chip, c)))
        return out


class _Multi:
    def __init__(self, plans):
        self.plans = plans
        self.operands, self.out_shapes, self.aliases, self.nsems = [], [], {}, 0
        self.spans = []
        for p in plans:
            ni, no = len(self.operands), len(self.out_shapes)
            self.spans.append((ni, ni + len(p.operands), no, no + len(p.out_shapes)))
            self.aliases.update({ni + i: no + j for i, j in p.aliases.items()})
            p.base = self.nsems
            self.nsems += p.nsems
            self.operands += p.operands
            self.out_shapes += p.out_shapes

    def schedule(self):
        def bound(fn, span):
            i0, i1, o0, o1 = span
            return lambda ins, outs, ssem, rsem: fn(ins[i0:i1], outs[o0:o1], ssem, rsem)

        stages = [(at, bound(fn, span)) for p, span in zip(self.plans, self.spans) for at, fn in p.schedule()]
        return sorted(stages, key=lambda s: s[0])

    def finish(self, ins, outs, ssem, rsem):
        for p, (i0, i1, o0, o1) in zip(self.plans, self.spans):
            p.finish(ins[i0:i1], outs[o0:o1], ssem, rsem)

    def results(self, extra):
        return [list(extra[o0:o1]) for (_, _, o0, o1) in self.spans]


class _Host:
    def __init__(self, comm, in_specs, out_specs, out_shape, scratch):
        self.comm = comm
        self.n_in, self.n_out = len(in_specs), len(out_specs)
        self.in_specs, self.out_specs, self.out_shape, self.scratch = list(in_specs), list(out_specs), list(out_shape), list(scratch)
        self.aliases = {}
        self.args = []
        if comm is not None:
            self.in_specs += [ANY] * len(comm.operands)
            self.out_specs += [ANY] * len(comm.out_shapes)
            self.out_shape += comm.out_shapes
            self.scratch += [pltpu.SemaphoreType.DMA((comm.nsems,)), pltpu.SemaphoreType.DMA((comm.nsems,))]
            self.aliases = {self.n_in + i: self.n_out + j for i, j in comm.aliases.items()}
            self.args = list(comm.operands)

    def split(self, refs):
        nc = len(self.args)
        nco = len(self.out_shape) - self.n_out
        ins, p = refs[:self.n_in], self.n_in + nc
        outs, rest = refs[p:p + self.n_out], refs[p + self.n_out + nco:]
        self._cargs = None
        if self.comm is not None:
            self._cargs = (refs[self.n_in:p], refs[p + self.n_out:p + self.n_out + nco], rest[-2], rest[-1])
            rest = rest[:-2]
        return ins, outs, rest

    def before(self, step, total):
        if self.comm is None:
            return

        for at, stage in self.comm.schedule():
            pl.when(step == min(total - 1, int(at * total)))(functools.partial(stage, *self._cargs))

    def after(self, step, total):
        if self.comm is None:
            return

        @pl.when(step == total - 1)
        def _():
            self.comm.finish(*self._cargs)

    def results(self, outs):
        return outs[:self.n_out], outs[self.n_out:]


def _cast_bf16(x, name):
    r, c = x.shape
    tr = ROW_TILE if r % ROW_TILE == 0 else r

    def body(x_ref, o_ref):
        o_ref[...] = x_ref[...].astype(BF16)

    return pl.pallas_call(
        body, name=name, grid=(r // tr,),
        in_specs=[pl.BlockSpec((tr, c), lambda i: (i, 0))],
        out_specs=pl.BlockSpec((tr, c), lambda i: (i, 0)),
        out_shape=jax.ShapeDtypeStruct((r, c), BF16), compiler_params=_cp("parallel"))(x)


def _cast_bf16_own_slab(x, me_arr, name):
    r, c = x.shape
    tr = ROW_TILE if r % ROW_TILE == 0 else r

    def body(me_ref, x_ref, o_ref):
        o_ref[...] = x_ref[...].astype(BF16)

    return pl.pallas_call(
        body, name=name,
        grid_spec=pltpu.PrefetchScalarGridSpec(
            num_scalar_prefetch=1, grid=(r // tr,),
            in_specs=[pl.BlockSpec((tr, c), lambda i, me: (i, 0))],
            out_specs=pl.BlockSpec((None, tr, c), lambda i, me: (me[0], i, 0))),
        out_shape=jax.ShapeDtypeStruct((N_CHIPS, r, c), BF16), compiler_params=_cp("parallel"))(me_arr, x)


def _rms_fwd(x, g, name):
    s, d = x.shape

    def body(x_ref, g_ref, h_ref):
        xhat, _ = _rms_stats(x_ref[...])
        h_ref[...] = (xhat * g_ref[...]).astype(BF16)

    return pl.pallas_call(
        body, name=name, grid=(s // ROW_TILE,),
        in_specs=[pl.BlockSpec((ROW_TILE, d), lambda i: (i, 0)), pl.BlockSpec((1, d), lambda i: (0, 0))],
        out_specs=pl.BlockSpec((ROW_TILE, d), lambda i: (i, 0)),
        out_shape=jax.ShapeDtypeStruct((s, d), BF16), compiler_params=_cp("parallel"))(x, g)


def _mm_nn(a, w3, name, comm=None):
    m, k = a.shape
    nsh, _, ns = w3.shape
    tm = 512 if m % 512 == 0 else ROW_TILE
    tn = _pick_tile(ns, 1024)
    per = ns // tn
    grid = (nsh * per, m // tm)
    host = _Host(comm,
                 [pl.BlockSpec((tm, k), lambda n, i: (i, 0)), pl.BlockSpec((None, k, tn), lambda n, i: (n // per, 0, n % per))],
                 [pl.BlockSpec((tm, tn), lambda n, i: (i, n))], [jax.ShapeDtypeStruct((m, nsh * ns), F32)], [])

    def body(*refs):
        (a_ref, w_ref), (o_ref,), _ = host.split(refs)
        step = pl.program_id(0) * grid[1] + pl.program_id(1)
        host.before(step, grid[0] * grid[1])
        o_ref[...] = jnp.dot(a_ref[...], w_ref[...], preferred_element_type=F32)
        host.after(step, grid[0] * grid[1])

    outs = pl.pallas_call(
        body, name=name, grid=grid, in_specs=host.in_specs, out_specs=host.out_specs, out_shape=host.out_shape,
        scratch_shapes=host.scratch, input_output_aliases=host.aliases,
        compiler_params=_cp("arbitrary", "arbitrary"))(a, w3, *host.args)
    (out,), extra = host.results(outs)
    return out, extra


def _mm_nt(a, b, name):
    m, k = a.shape
    n = b.shape[0]
    tm = 512 if m % 512 == 0 else ROW_TILE

    def body(a_ref, b_ref, o_ref):
        o_ref[...] = lax.dot_general(a_ref[...], b_ref[...], (((1,), (1,)), ((), ())), preferred_element_type=F32)

    return pl.pallas_call(
        body, name=name, grid=(m // tm,),
        in_specs=[pl.BlockSpec((tm, k), lambda i: (i, 0)), pl.BlockSpec((n, k), lambda i: (0, 0))],
        out_specs=pl.BlockSpec((tm, n), lambda i: (i, 0)),
        out_shape=jax.ShapeDtypeStruct((m, n), F32), compiler_params=_cp("parallel"))(a, b)


def _mm_tn(a, b, nsh, name, comm=None):
    s, m = a.shape
    n = b.shape[1]
    ns = n // nsh
    tm = 512 if m % 512 == 0 else ROW_TILE
    tn = _pick_tile(ns, 1024)
    per = ns // tn
    grid = (nsh * per, m // tm)
    host = _Host(comm, [pl.BlockSpec((s, tm), lambda j, i: (0, i)), pl.BlockSpec((s, tn), lambda j, i: (0, j))],
                 [pl.BlockSpec((None, tm, tn), lambda j, i: (j // per, i, j % per))],
                 [jax.ShapeDtypeStruct((nsh, m, ns), BF16)], [])

    def body(*refs):
        (a_ref, b_ref), (o_ref,), _ = host.split(refs)
        step = pl.program_id(0) * grid[1] + pl.program_id(1)
        host.before(step, grid[0] * grid[1])
        o_ref[...] = lax.dot_general(a_ref[...], b_ref[...], (((0,), (0,)), ((), ())),
                                     preferred_element_type=F32).astype(BF16)
        host.after(step, grid[0] * grid[1])

    outs = pl.pallas_call(
        body, name=name, grid=grid, in_specs=host.in_specs, out_specs=host.out_specs, out_shape=host.out_shape,
        scratch_shapes=host.scratch, input_output_aliases=host.aliases,
        compiler_params=_cp("arbitrary", "arbitrary"))(a, b, *host.args)
    (out,), extra = host.results(outs)
    return out, extra


def _tri(n, rel):
    row = lax.broadcasted_iota(jnp.int32, (2 * n, n), 0)
    col = lax.broadcasted_iota(jnp.int32, (2 * n, n), 1)
    return jnp.where(rel(jnp.where(row >= n, row - n, row), col), 1.0, 0.0).astype(BF16)


def _dot_split(x, tri2):
    hi = x.astype(BF16)
    lo = (x - hi.astype(F32)).astype(BF16)
    return jnp.dot(jnp.concatenate([hi, lo], axis=1), tri2, preferred_element_type=F32)


def _nt(a, b):
    return lax.dot_general(a, b, (((1,), (1,)), ((), ())), preferred_element_type=F32)


def _tn(a, b):
    return lax.dot_general(a, b, (((0,), (0,)), ((), ())), preferred_element_type=F32)


def _heads_per_step(nh):
    return max(h for h in (1, 2, 4) if nh % h == 0)


def _sba_fwd(p, sbw, name, comm=None):
    s = p.shape[0]
    nh = sbw // HEAD_DIM
    hp = _heads_per_step(nh)
    ngrp, hw = nh // hp, hp * HEAD_DIM
    blk = ATT_BLOCK
    nq = s // blk
    scale = 1.0 / math.sqrt(HEAD_DIM)
    host = _Host(comm,
                 [pl.BlockSpec((blk, hw), lambda g, i: (i, g)),
                  pl.BlockSpec((s, hw), lambda g, i: (0, ngrp + g)),
                  pl.BlockSpec((s, hw), lambda g, i: (0, 2 * ngrp + g))],
                 [pl.BlockSpec((blk, hw), lambda g, i: (i, g))] * 2,
                 [jax.ShapeDtypeStruct((s, sbw), F32)] * 2,
                 [pltpu.VMEM((s, hw), BF16)] * 2)

    def body(*refs):
        (q_ref, k_ref, v_ref), (o_ref, lt_ref), (kb_ref, vb_ref) = host.split(refs)
        i = pl.program_id(1)
        step = pl.program_id(0) * nq + i
        host.before(step, ngrp * nq)

        @pl.when(i == 0)
        def _():
            kb_ref[...] = k_ref[...].astype(BF16)
            vb_ref[...] = v_ref[...].astype(BF16)

        heads = [slice(h * HEAD_DIM, (h + 1) * HEAD_DIM) for h in range(hp)]
        qs = [q_ref[:, hd].astype(BF16) for hd in heads]
        later = _tri(blk, lambda r, c: r > c)
        causal = lax.broadcasted_iota(jnp.int32, (blk, blk), 1) < lax.broadcasted_iota(jnp.int32, (blk, blk), 0)

        def key_block(j, carry, diagonal):
            rows = pl.ds(pl.multiple_of(j * blk, blk), blk)
            hs = range(hp)
            z = [_nt(qs[h], kb_ref[rows, heads[h]]) * scale for h in hs]
            ls = [_log_sigmoid(z[h]) for h in hs]
            lm = [jnp.where(causal, ls[h] - z[h], 0.0) if diagonal else ls[h] - z[h] for h in hs]
            stay = [_dot_split(lm[h], later) for h in hs]
            w = [jnp.exp(ls[h] + stay[h] + carry[h][1]) for h in hs]
            if diagonal:
                w = [jnp.where(causal, w[h], 0.0) for h in hs]
            acc = [carry[h][0] + jnp.dot(w[h].astype(BF16), vb_ref[rows, heads[h]], preferred_element_type=F32) for h in hs]
            return tuple((acc[h], carry[h][1] + jnp.sum(lm[h], axis=1, keepdims=True)) for h in hs)

        init = tuple((jnp.zeros((blk, HEAD_DIM), F32), jnp.zeros((blk, 1), F32)) for _ in heads)
        carry = key_block(i, init, True)
        carry = lax.fori_loop(0, i, lambda n, c: key_block(i - 1 - n, c, False), carry)
        for h, hd in enumerate(heads):
            o_ref[:, hd] = carry[h][0]
            lt_ref[:, hd] = jnp.broadcast_to(carry[h][1], (blk, HEAD_DIM))
        host.after(step, ngrp * nq)

    outs = pl.pallas_call(
        body, name=name, grid=(ngrp, nq), in_specs=host.in_specs, out_specs=host.out_specs, out_shape=host.out_shape,
        scratch_shapes=host.scratch, input_output_aliases=host.aliases,
        compiler_params=_cp("arbitrary", "arbitrary"))(p, p, p, *host.args)
    (out, ltot), extra = host.results(outs)
    return out, ltot, extra


def _sba_bwd(p, ltot, dout, sbw, name, comm=None):
    s = p.shape[0]
    nh = sbw // HEAD_DIM
    hp = _heads_per_step(nh)
    ngrp, hw = nh // hp, hp * HEAD_DIM
    blk = ATT_BLOCK
    nq = s // blk
    scale = 1.0 / math.sqrt(HEAD_DIM)
    blk_spec = pl.BlockSpec((blk, hw), lambda g, i: (i, g))
    col_spec = pl.BlockSpec((s, hw), lambda g, i: (0, g))
    host = _Host(comm,
                 [blk_spec, pl.BlockSpec((s, hw), lambda g, i: (0, ngrp + g)),
                  pl.BlockSpec((s, hw), lambda g, i: (0, 2 * ngrp + g)), blk_spec, blk_spec],
                 [blk_spec, col_spec, col_spec], [jax.ShapeDtypeStruct((s, sbw), BF16)] * 3,
                 [pltpu.VMEM((s, hw), BF16)] * 2 + [pltpu.VMEM((s, hw), F32)] * 2)

    def body(*refs):
        (q_ref, k_ref, v_ref, lt_ref, do_ref), (dq_ref, dk_ref, dv_ref), (kb_ref, vb_ref, dka_ref, dva_ref) = host.split(refs)
        i = pl.program_id(1)
        step = pl.program_id(0) * nq + i
        host.before(step, ngrp * nq)

        @pl.when(i == 0)
        def _():
            kb_ref[...] = k_ref[...].astype(BF16)
            vb_ref[...] = v_ref[...].astype(BF16)
            dka_ref[...] = jnp.zeros_like(dka_ref)
            dva_ref[...] = jnp.zeros_like(dva_ref)

        heads = [slice(h * HEAD_DIM, (h + 1) * HEAD_DIM) for h in range(hp)]
        qs = [q_ref[:, hd].astype(BF16) for hd in heads]
        dos = [do_ref[:, hd].astype(BF16) for hd in heads]
        ltots = [lt_ref[:, h * HEAD_DIM:h * HEAD_DIM + 1] for h in range(hp)]
        upto = _tri(blk, lambda r, c: r <= c)
        before = _tri(blk, lambda r, c: r < c)
        causal = lax.broadcasted_iota(jnp.int32, (blk, blk), 1) < lax.broadcasted_iota(jnp.int32, (blk, blk), 0)

        def key_block(j, carry, diagonal):
            rows = pl.ds(pl.multiple_of(j * blk, blk), blk)
            hs = range(hp)
            kj = [kb_ref[rows, heads[h]] for h in hs]
            vj = [vb_ref[rows, heads[h]] for h in hs]
            z = [_nt(qs[h], kj[h]) * scale for h in hs]
            dw = [_nt(dos[h], vj[h]) for h in hs]
            ls = [_log_sigmoid(z[h]) for h in hs]
            lm = [jnp.where(causal, ls[h] - z[h], 0.0) if diagonal else ls[h] - z[h] for h in hs]
            stay = [ltots[h] - carry[h][1] - _dot_split(lm[h], upto) for h in hs]
            w = [jnp.exp(ls[h] + stay[h]) for h in hs]
            if diagonal:
                w = [jnp.where(causal, w[h], 0.0) for h in hs]
            da = [dw[h] * w[h] for h in hs]
            sig = [jnp.exp(ls[h]) for h in hs]
            chain = [sig[h] * (carry[h][2] + _dot_split(da[h], before)) for h in hs]
            if diagonal:
                chain = [jnp.where(causal, chain[h], 0.0) for h in hs]
            dzb = [((da[h] * (1.0 - sig[h]) - chain[h]) * scale).astype(BF16) for h in hs]
            dq = [carry[h][0] + jnp.dot(dzb[h], kj[h], preferred_element_type=F32) for h in hs]
            for h in hs:
                dka_ref[rows, heads[h]] += _tn(dzb[h], qs[h])
            for h in hs:
                dva_ref[rows, heads[h]] += _tn(w[h].astype(BF16), dos[h])
            return tuple((dq[h], carry[h][1] + jnp.sum(lm[h], axis=1, keepdims=True),
                          carry[h][2] + jnp.sum(da[h], axis=1, keepdims=True)) for h in hs)

        zero = jnp.zeros((blk, 1), F32)
        init = tuple((jnp.zeros((blk, HEAD_DIM), F32), zero, zero) for _ in heads)
        carry = lax.fori_loop(0, i, lambda j, c: key_block(j, c, False), init)
        carry = key_block(i, carry, True)
        for h, hd in enumerate(heads):
            dq_ref[:, hd] = carry[h][0].astype(BF16)

        @pl.when(i == nq - 1)
        def _():
            dk_ref[...] = dka_ref[...].astype(BF16)
            dv_ref[...] = dva_ref[...].astype(BF16)

        host.after(step, ngrp * nq)

    outs = pl.pallas_call(
        body, name=name, grid=(ngrp, nq), in_specs=host.in_specs, out_specs=host.out_specs, out_shape=host.out_shape,
        scratch_shapes=host.scratch, input_output_aliases=host.aliases,
        compiler_params=_cp("arbitrary", "arbitrary"))(p, p, p, ltot, dout, *host.args)
    (dq, dk, dv), extra = host.results(outs)
    return dq, dk, dv, extra


def _pool_groups(pad_ref, tile, row0, gd, halo):
    row = row0 + lax.broadcasted_iota(jnp.int32, (tile, 1), 0)
    out = []
    for gi, win in enumerate(POOL_WINDOWS):
        cs = slice(gi * gd, (gi + 1) * gd)
        tok = pad_ref[halo:halo + tile, cs]
        acc = tok
        for j in range(1, win):
            acc = acc + pad_ref[halo - j:halo - j + tile, cs]
        cnt = jnp.minimum(win, row + 1).astype(F32)
        out.append(acc / cnt - tok)
    return out


def _even_mix_fwd(p, att, pool_w, pool_scale, d, name):
    s = p.shape[0]
    half = d // 2
    gd = half // len(POOL_WINDOWS)
    t, hb = ROW_TILE, POOL_HALO

    def body(u_ref, uh_ref, g_ref, a_ref, pw_ref, sc_ref, y_ref, pad_ref):
        i = pl.program_id(0)
        pad_ref[0:hb, :] = jnp.where(i > 0, uh_ref[...], 0.0)
        pad_ref[hb:, :] = u_ref[...]
        pooled = _pool_groups(pad_ref, t, i * t, gd, hb)
        for gi in range(len(POOL_WINDOWS)):
            cs = slice(gi * gd, (gi + 1) * gd)
            po = jnp.dot(pooled[gi].astype(BF16), pw_ref[gi], preferred_element_type=F32) * sc_ref[:, cs]
            y_ref[:, half + gi * gd:half + (gi + 1) * gd] = (po * _silu(g_ref[:, half + gi * gd:half + (gi + 1) * gd])).astype(BF16)
        y_ref[:, :half] = (a_ref[...] * _silu(g_ref[:, :half])).astype(BF16)

    return pl.pallas_call(
        body, name=name, grid=(s // t,),
        in_specs=[pl.BlockSpec((t, half), lambda i: (i, 3)),
                  pl.BlockSpec((hb, half), lambda i: (jnp.maximum(i * (t // hb) - 1, 0), 3)),
                  pl.BlockSpec((t, d), lambda i: (i, 2)),
                  pl.BlockSpec((t, half), lambda i: (i, 0)),
                  pl.BlockSpec(pool_w.shape, lambda i: (0, 0, 0)),
                  pl.BlockSpec((1, half), lambda i: (0, 0))],
        out_specs=pl.BlockSpec((t, d), lambda i: (i, 0)),
        out_shape=jax.ShapeDtypeStruct((s, d), BF16),
        scratch_shapes=[pltpu.VMEM((hb + t, half), F32)],
        compiler_params=_cp("parallel"))(p, p, p, att, pool_w, pool_scale)


def _even_mix_bwd(p, att, dy, pool_w, pool_scale, d, name):
    s = p.shape[0]
    half = d // 2
    ng = len(POOL_WINDOWS)
    gd = half // ng
    t, hb = ROW_TILE, POOL_HALO
    nt = s // t

    def body(u_ref, uh_ref, g_ref, gh_ref, a_ref, dy_ref, dyh_ref, pw_ref, sc_ref,
             da_ref, du_ref, dg_ref, dsc_ref, dpw_ref, pad_ref, dn_ref):
        i = pl.program_id(0)
        first = i == 0
        pad_ref[0:hb, :] = jnp.where(i > 0, uh_ref[...], 0.0)
        pad_ref[hb:, :] = u_ref[...]
        pooled = _pool_groups(pad_ref, t, i * t, gd, hb)
        g1 = g_ref[:, :half]
        dy1 = dy_ref[:, :half]
        da_ref[...] = dy1 * _silu(g1)
        dg_ref[:, :half] = (dy1 * a_ref[...] * _dsilu(g1)).astype(BF16)
        row = i * t + lax.broadcasted_iota(jnp.int32, (t + hb, 1), 0)
        for gi, win in enumerate(POOL_WINDOWS):
            cs = slice(gi * gd, (gi + 1) * gd)
            cs2 = slice(half + gi * gd, half + (gi + 1) * gd)
            w = pw_ref[gi]
            pb = pooled[gi].astype(BF16)
            zp = jnp.dot(pb, w, preferred_element_type=F32)
            g2 = g_ref[:, cs2]
            dy2 = dy_ref[:, cs2]
            dg_ref[:, cs2] = (dy2 * zp * sc_ref[:, cs] * _dsilu(g2)).astype(BF16)
            dpo = dy2 * _silu(g2)
            _acc_rows(dsc_ref.at[:, cs], first, jnp.sum(dpo * zp, axis=0, keepdims=True))
            dz = (dpo * sc_ref[:, cs]).astype(BF16)
            _acc_rows(dpw_ref.at[gi], first, _tn(pb, dz))
            dzh = jnp.where(i < nt - 1, dyh_ref[:, cs] * _silu(gh_ref[:, cs]) * sc_ref[:, cs], 0.0).astype(BF16)
            dpool = _nt(dz, w)
            dpool_h = _nt(dzh, w)
            cnt = jnp.minimum(win, row + 1).astype(F32)
            dn_ref[0:t, cs] = dpool / cnt[0:t]
            dn_ref[t:, cs] = dpool_h / cnt[t:]
            acc = dn_ref[0:t, cs]
            for j in range(1, win):
                acc = acc + dn_ref[j:j + t, cs]
            du_ref[:, cs] = (acc - dpool).astype(BF16)

    return pl.pallas_call(
        body, name=name, grid=(nt,),
        in_specs=[pl.BlockSpec((t, half), lambda i: (i, 3)),
                  pl.BlockSpec((hb, half), lambda i: (jnp.maximum(i * (t // hb) - 1, 0), 3)),
                  pl.BlockSpec((t, d), lambda i: (i, 2)),
                  pl.BlockSpec((hb, half), lambda i: (jnp.minimum((i + 1) * (t // hb), s // hb - 1), 5)),
                  pl.BlockSpec((t, half), lambda i: (i, 0)),
                  pl.BlockSpec((t, d), lambda i: (i, 0)),
                  pl.BlockSpec((hb, half), lambda i: (jnp.minimum((i + 1) * (t // hb), s // hb - 1), 1)),
                  pl.BlockSpec(pool_w.shape, lambda i: (0, 0, 0)),
                  pl.BlockSpec((1, half), lambda i: (0, 0))],
        out_specs=[pl.BlockSpec((t, half), lambda i: (i, 0)),
                   pl.BlockSpec((t, half), lambda i: (i, 0)),
                   pl.BlockSpec((t, d), lambda i: (i, 0)),
                   pl.BlockSpec((1, half), lambda i: (0, 0)),
                   pl.BlockSpec((ng, gd, gd), lambda i: (0, 0, 0))],
        out_shape=[jax.ShapeDtypeStruct((s, half), F32), jax.ShapeDtypeStruct((s, half), BF16),
                   jax.ShapeDtypeStruct((s, d), BF16), jax.ShapeDtypeStruct((1, half), F32),
                   jax.ShapeDtypeStruct((ng, gd, gd), F32)],
        scratch_shapes=[pltpu.VMEM((hb + t, half), F32), pltpu.VMEM((t + hb, half), F32)],
        compiler_params=_cp("arbitrary"))(p, p, p, p, att, dy, dy, pool_w, pool_scale)


def _mm_out_even(y, w, x, g_post, g_pre_next, name):
    s, k = y.shape
    d = w.shape[1]
    t = ROW_TILE

    def body(y_ref, w_ref, x_ref, gp_ref, gn_ref, o_ref, x1_ref, h1_ref):
        o = jnp.dot(y_ref[...], w_ref[...], preferred_element_type=F32)
        o_ref[...] = o
        ohat, _ = _rms_stats(o)
        x1 = x_ref[...] + ohat * gp_ref[...]
        x1_ref[...] = x1
        xhat, _ = _rms_stats(x1)
        h1_ref[...] = (xhat * gn_ref[...]).astype(BF16)

    row = lambda c: pl.BlockSpec((t, c), lambda i: (i, 0))
    vec = pl.BlockSpec((1, d), lambda i: (0, 0))
    return pl.pallas_call(
        body, name=name, grid=(s // t,),
        in_specs=[row(k), pl.BlockSpec((k, d), lambda i: (0, 0)), row(d), vec, vec],
        out_specs=[row(d), row(d), row(d)],
        out_shape=[jax.ShapeDtypeStruct((s, d), F32), jax.ShapeDtypeStruct((s, d), F32),
                   jax.ShapeDtypeStruct((s, d), BF16)],
        compiler_params=_cp("parallel"))(y, w, x, g_post, g_pre_next)


def _mm_out_odd(y, w, x1, g_post, target, name):
    s, k = y.shape
    d = w.shape[1]
    t = ROW_TILE

    def body(y_ref, w_ref, x_ref, gp_ref, tg_ref, do_ref, dx_ref, loss_ref, dgp_ref):
        first = pl.program_id(0) == 0
        o = jnp.dot(y_ref[...], w_ref[...], preferred_element_type=F32)
        ohat, r = _rms_stats(o)
        gp = gp_ref[...]
        diff = x_ref[...] + ohat * gp - tg_ref[...]
        part = 0.5 * jnp.sum(jnp.mean(diff * diff, axis=-1, keepdims=True), axis=0, keepdims=True)
        _acc_rows(loss_ref, first, jnp.broadcast_to(part, loss_ref.shape))
        dx2 = diff * (1.0 / d)
        dx_ref[...] = dx2
        do, dgp = _rms_bwd(dx2, ohat, r, gp)
        do_ref[...] = do.astype(BF16)
        _acc_rows(dgp_ref, first, dgp)

    row = lambda c: pl.BlockSpec((t, c), lambda i: (i, 0))
    vec = pl.BlockSpec((1, d), lambda i: (0, 0))
    return pl.pallas_call(
        body, name=name, grid=(s // t,),
        in_specs=[row(k), pl.BlockSpec((k, d), lambda i: (0, 0)), row(d), vec, row(d)],
        out_specs=[row(d), row(d), pl.BlockSpec((8, LANES), lambda i: (0, 0)), vec],
        out_shape=[jax.ShapeDtypeStruct((s, d), BF16), jax.ShapeDtypeStruct((s, d), F32),
                   jax.ShapeDtypeStruct((8, LANES), F32), jax.ShapeDtypeStruct((1, d), F32)],
        compiler_params=_cp("arbitrary"))(y, w, x1, g_post, target)


def _layer_norm(d1, cg, cb):
    mu = jnp.mean(d1, axis=-1, keepdims=True)
    cen = d1 - mu
    rstd = lax.rsqrt(jnp.mean(cen * cen, axis=-1, keepdims=True) + EPS)
    n = cen * rstd
    return n, rstd, n * cg + cb


SUBLANES = 8
GATHER_PIECES = 4
CONV_ROWS = 64


def _make_shifts(pad_ref, cs, sh_ref):
    rows = sh_ref.shape[1]
    for r in range(1, SUBLANES):
        sh_ref[r - 1] = pad_ref[r:r + rows, cs]


def _by_shift(taps, base, sign=1):
    return sorted(range(taps), key=lambda k: ((sign * (base + k)) % SUBLANES, k))


def _window(pad_ref, cs, sh_ref, off, t):
    m, r = divmod(off, SUBLANES)
    if r == 0:
        return pad_ref[SUBLANES * m:SUBLANES * m + t, cs]
    return sh_ref[r - 1, SUBLANES * m:SUBLANES * m + t, :]


def _odd_mix_fwd(p, sconv_w, dconv_w, dconv_b, cnorm_g, cnorm_b, d, name):
    s = p.shape[0]
    w = d // 2
    k3, k31 = sconv_w.shape[0], dconv_w.shape[0]
    t, hb = ROW_TILE, CONV_HALO
    assert hb >= k31 - 1 and w % LANES == 0

    def body(p_ref, ph_ref, w3_ref, w31_ref, b31_ref, cg_ref, cb_ref, y_ref, s3_ref, d1_ref, mpad, dpad, sh_ref):
        i = pl.program_id(0)
        mpad[0:hb, :] = jnp.where(i > 0, ph_ref[:, 2 * w:3 * w] * ph_ref[:, 0:w], 0.0)
        mpad[hb:, :] = p_ref[:, 2 * w:3 * w] * p_ref[:, 0:w]
        dpad[0:hb, :] = jnp.where(i > 0, ph_ref[:, 3 * w:4 * w] * _sigmoid(ph_ref[:, 4 * w:5 * w]), 0.0)
        dpad[hb:, :] = p_ref[:, 3 * w:4 * w] * _sigmoid(p_ref[:, 4 * w:5 * w])
        for c0 in range(0, w, LANES):
            cs = slice(c0, c0 + LANES)
            acc = jnp.zeros((t, LANES), F32)
            for kk in range(k3):
                acc = acc + w3_ref[kk:kk + 1, cs] * mpad[hb - (k3 - 1) + kk:hb - (k3 - 1) + kk + t, cs]
            s3_ref[:, cs] = acc
            _make_shifts(dpad, cs, sh_ref)
            for r0 in range(0, t, CONV_ROWS):
                acc = jnp.zeros((CONV_ROWS, LANES), F32)
                for kk in _by_shift(k31, hb - (k31 - 1)):
                    acc = acc + w31_ref[kk:kk + 1, cs] * _window(dpad, cs, sh_ref, hb - (k31 - 1) + kk + r0, CONV_ROWS)
                d1_ref[r0:r0 + CONV_ROWS, cs] = acc + b31_ref[:, cs]
        _, _, d2 = _layer_norm(d1_ref[...], cg_ref[...], cb_ref[...])
        y_ref[:, :w] = (p_ref[:, w:2 * w] * s3_ref[...] * _silu(p_ref[:, 5 * w:6 * w])).astype(BF16)
        y_ref[:, w:] = (_silu(d2) * _silu(p_ref[:, 6 * w:7 * w])).astype(BF16)

    row = lambda c: pl.BlockSpec((t, c), lambda i: (i, 0))
    full = lambda a: pl.BlockSpec(a.shape, lambda i: (0, 0))
    return pl.pallas_call(
        body, name=name, grid=(s // t,),
        in_specs=[row(7 * w),
                  pl.BlockSpec((hb, 5 * w), lambda i: (jnp.maximum(i * (t // hb) - 1, 0), 0)),
                  full(sconv_w), full(dconv_w), full(dconv_b), full(cnorm_g), full(cnorm_b)],
        out_specs=[row(d), row(w), row(w)],
        out_shape=[jax.ShapeDtypeStruct((s, d), BF16), jax.ShapeDtypeStruct((s, w), F32),
                   jax.ShapeDtypeStruct((s, w), F32)],
        scratch_shapes=[pltpu.VMEM((hb + t, w), F32)] * 2 + [pltpu.VMEM((SUBLANES - 1, hb + t - SUBLANES, LANES), F32)],
        compiler_params=_cp("parallel"))(p, p, sconv_w, dconv_w, dconv_b, cnorm_g, cnorm_b)


def _odd_bwd_rows(p, s3, d1, dy, cnorm_g, cnorm_b, d, name):
    s = p.shape[0]
    w = d // 2
    t = ROW_TILE

    def body(bc_ref, g1_ref, g2_ref, s3_ref, d1_ref, dy_ref, cg_ref, cb_ref,
             dbc_ref, dg_ref, ds3_ref, dd1_ref, dcg_ref, dcb_ref, db_ref):
        first = pl.program_id(0) == 0
        g1, g2 = g1_ref[...], g2_ref[...]
        bc, s3v = bc_ref[...], s3_ref[...]
        dy1, dy2 = dy_ref[:, :w], dy_ref[:, w:]
        n, rstd, d2 = _layer_norm(d1_ref[...], cg_ref[...], cb_ref[...])
        dg_ref[:, :w] = (dy1 * bc * s3v * _dsilu(g1)).astype(BF16)
        dg_ref[:, w:] = (dy2 * _silu(d2) * _dsilu(g2)).astype(BF16)
        dco = dy1 * _silu(g1)
        dbc_ref[...] = (dco * s3v).astype(BF16)
        ds3_ref[...] = dco * bc
        dd2 = dy2 * _silu(g2) * _dsilu(d2)
        _acc_rows(dcb_ref, first, jnp.sum(dd2, axis=0, keepdims=True))
        _acc_rows(dcg_ref, first, jnp.sum(dd2 * n, axis=0, keepdims=True))
        dn = dd2 * cg_ref[...]
        dd1 = rstd * (dn - jnp.mean(dn, axis=-1, keepdims=True) - n * jnp.mean(dn * n, axis=-1, keepdims=True))
        dd1_ref[...] = dd1
        _acc_rows(db_ref, first, jnp.sum(dd1, axis=0, keepdims=True))

    col = lambda j: pl.BlockSpec((t, w), lambda i: (i, j))
    row = lambda c: pl.BlockSpec((t, c), lambda i: (i, 0))
    vec = pl.BlockSpec((1, w), lambda i: (0, 0))
    return pl.pallas_call(
        body, name=name, grid=(s // t,),
        in_specs=[col(1), col(5), col(6), row(w), row(w), row(d), vec, vec],
        out_specs=[row(w), row(d), row(w), row(w), vec, vec, vec],
        out_shape=[jax.ShapeDtypeStruct((s, w), BF16), jax.ShapeDtypeStruct((s, d), BF16),
                   jax.ShapeDtypeStruct((s, w), F32), jax.ShapeDtypeStruct((s, w), F32)]
        + [jax.ShapeDtypeStruct((1, w), F32)] * 3,
        compiler_params=_cp("arbitrary"))(p, p, p, s3, d1, dy, cnorm_g, cnorm_b)


def _odd_bwd_conv(p, ds3, dd1, sconv_w, dconv_w, d, name):
    s = p.shape[0]
    w = d // 2
    k3, k31 = sconv_w.shape[0], dconv_w.shape[0]
    t, hb, ha = ROW_TILE, CONV_HALO, 8
    nt = s // t
    assert hb >= k31 - 1 and ha >= k3 - 1

    def body(hc_ref, cc_ref, ga_ref, gb_ref, hch_ref, cch_ref, gah_ref, gbh_ref, ds3_ref, ds3h_ref, dd1_ref, dd1h_ref,
             w3_ref, w31_ref, dhc_ref, dcc_ref, dga_ref, dgb_ref, dw3_ref, dw31_ref, mpad, dpad, s3pad, d1pad, sh_ref):
        i = pl.program_id(0)
        first = i == 0
        last = i == nt - 1
        mpad[0:hb, :] = jnp.where(i > 0, cch_ref[...] * hch_ref[...], 0.0)
        mpad[hb:, :] = cc_ref[...] * hc_ref[...]
        dpad[0:hb, :] = jnp.where(i > 0, gah_ref[...] * _sigmoid(gbh_ref[...]), 0.0)
        dpad[hb:, :] = ga_ref[...] * _sigmoid(gb_ref[...])
        s3pad[0:t, :] = ds3_ref[...]
        s3pad[t:, :] = jnp.where(last, 0.0, ds3h_ref[...])
        d1pad[0:t, :] = dd1_ref[...]
        d1pad[t:, :] = jnp.where(last, 0.0, dd1h_ref[...])

        @pl.when(first)
        def _():
            dw3_ref[...] = jnp.zeros_like(dw3_ref)
            dw31_ref[...] = jnp.zeros_like(dw31_ref)

        def fold(v):
            return jnp.sum(v.reshape(v.shape[0] // SUBLANES, SUBLANES, LANES), axis=0)

        groups = range(0, t, CONV_ROWS)
        for c0 in range(0, w, LANES):
            cs = slice(c0, c0 + LANES)
            ds3v = s3pad[0:t, cs]
            dm = jnp.zeros((t, LANES), F32)
            for kk in range(k3):
                dm = dm + w3_ref[kk:kk + 1, cs] * s3pad[k3 - 1 - kk:k3 - 1 - kk + t, cs]
                off = hb - (k3 - 1) + kk
                dw3_ref[SUBLANES * kk:SUBLANES * (kk + 1), cs] += fold(ds3v * mpad[off:off + t, cs])
            dcc_ref[:, cs] = (dm * hc_ref[:, cs]).astype(BF16)
            dhc_ref[:, cs] = (dm * cc_ref[:, cs]).astype(BF16)
            _make_shifts(d1pad, cs, sh_ref)
            for r0 in groups:
                rows = slice(r0, r0 + CONV_ROWS)
                dd0 = jnp.zeros((CONV_ROWS, LANES), F32)
                for kk in _by_shift(k31, -(k31 - 1), -1):
                    dd0 = dd0 + w31_ref[kk:kk + 1, cs] * _window(d1pad, cs, sh_ref, k31 - 1 - kk + r0, CONV_ROWS)
                sgb = _sigmoid(gb_ref[rows, cs])
                dga_ref[rows, cs] = (dd0 * sgb).astype(BF16)
                dgb_ref[rows, cs] = (dd0 * ga_ref[rows, cs] * sgb * (1.0 - sgb)).astype(BF16)
            _make_shifts(dpad, cs, sh_ref)
            for kk in _by_shift(k31, hb - (k31 - 1)):
                part = jnp.zeros((SUBLANES, LANES), F32)
                for r0 in groups:
                    part = part + fold(d1pad[r0:r0 + CONV_ROWS, cs]
                                       * _window(dpad, cs, sh_ref, hb - (k31 - 1) + kk + r0, CONV_ROWS))
                dw31_ref[SUBLANES * kk:SUBLANES * (kk + 1), cs] += part

    col = lambda j: pl.BlockSpec((t, w), lambda i: (i, j))
    pre = lambda j: pl.BlockSpec((hb, w), lambda i: (jnp.maximum(i * (t // hb) - 1, 0), j))
    row = pl.BlockSpec((t, w), lambda i: (i, 0))
    post = lambda h: pl.BlockSpec((h, w), lambda i: (jnp.minimum((i + 1) * (t // h), s // h - 1), 0))
    full = lambda a: pl.BlockSpec(a.shape, lambda i: (0, 0))
    dhc, dcc, dga, dgb, dw3, dw31 = pl.pallas_call(
        body, name=name, grid=(nt,),
        in_specs=[col(0), col(2), col(3), col(4), pre(0), pre(2), pre(3), pre(4),
                  row, post(ha), row, post(hb), full(sconv_w), full(dconv_w)],
        out_specs=[row, row, row, row, pl.BlockSpec((SUBLANES * k3, w), lambda i: (0, 0)),
                   pl.BlockSpec((SUBLANES * k31, w), lambda i: (0, 0))],
        out_shape=[jax.ShapeDtypeStruct((s, w), BF16)] * 4
        + [jax.ShapeDtypeStruct((SUBLANES * k3, w), F32), jax.ShapeDtypeStruct((SUBLANES * k31, w), F32)],
        scratch_shapes=[pltpu.VMEM((hb + t, w), F32)] * 2 + [pltpu.VMEM((t + ha, w), F32), pltpu.VMEM((t + hb, w), F32),
                                                             pltpu.VMEM((SUBLANES - 1, hb + t - SUBLANES, LANES), F32)],
        compiler_params=_cp("arbitrary"))(p, p, p, p, p, p, p, p, ds3, ds3, dd1, dd1, sconv_w, dconv_w)
    return dhc, dcc, dga, dgb, jnp.sum(dw3.reshape(k3, SUBLANES, w), axis=1), jnp.sum(dw31.reshape(k31, SUBLANES, w), axis=1)


def _mm_in_bwd(dp, w3, x, g_pre, dres, post, name, comm=None):
    s = dp.shape[0]
    nsh, d, ns = w3.shape
    t = 512 if s % 512 == 0 else ROW_TILE
    nt = s // t
    ks = 2 if (ns // 2) % LANES == 0 else 1
    nk, kw = nsh * ks, ns // ks
    chunk = 128
    nchunk = t // chunk
    row = pl.BlockSpec((t, d), lambda i, k: (i, 0))
    vec = pl.BlockSpec((1, d), lambda i, k: (0, 0))
    rowwise = [x, dres] + ([post[0]] if post is not None else [])
    in_specs = [pl.BlockSpec((t, kw), lambda i, k: (i, k)), pl.BlockSpec((None, d, kw), lambda i, k: (k // ks, 0, k % ks)), vec]
    out_specs = [row, vec]
    out_shape = [jax.ShapeDtypeStruct((s, d), F32), jax.ShapeDtypeStruct((1, d), F32)]
    args = [dp, w3, g_pre]
    if post is not None:
        in_specs += [vec]
        out_specs += [row, vec]
        out_shape += [jax.ShapeDtypeStruct((s, d), BF16), jax.ShapeDtypeStruct((1, d), F32)]
        args += [post[1]]
    n_blocked = len(in_specs)
    in_specs += [ANY] * len(rowwise)
    args += rowwise
    host = _Host(comm, in_specs, out_specs, out_shape,
                 [pltpu.VMEM((t, d), F32), pltpu.VMEM((len(rowwise), 2, chunk, d), F32), pltpu.SemaphoreType.DMA((len(rowwise), 2))])

    def body(*refs):
        ins, outs, (acc_ref, buf_ref, sem_ref) = host.split(refs)
        dp_ref, w_ref, g_ref = ins[:3]
        hbm = ins[n_blocked:]
        dx_ref, dg_ref = outs[:2]
        tile = pl.program_id(0)
        kk = pl.program_id(1)
        first = tile == 0
        step = tile * nk + kk
        host.before(step, nt * nk)
        part = _nt(dp_ref[...], w_ref[...])

        @pl.when(kk == 0)
        def _():
            acc_ref[...] = part

        @pl.when(kk > 0)
        def _():
            acc_ref[...] += part

        def fetch(ci, slot):
            return [pltpu.make_async_copy(src.at[pl.ds(tile * t + ci * chunk, chunk)], buf_ref.at[n, slot], sem_ref.at[n, slot])
                    for n, src in enumerate(hbm)]

        @pl.when(kk == nk - 1)
        def _():
            dg = dgp = None
            for cp in fetch(0, 0):
                cp.start()
            for ci in range(nchunk):
                slot = ci % 2
                if ci + 1 < nchunk:
                    for cp in fetch(ci + 1, 1 - slot):
                        cp.start()
                for cp in fetch(ci, slot):
                    cp.wait()
                rows = slice(ci * chunk, (ci + 1) * chunk)
                xhat, r = _rms_stats(buf_ref[0, slot])
                dxn, dg_part = _rms_bwd(acc_ref[rows, :], xhat, r, g_ref[...])
                dx = buf_ref[1, slot] + dxn
                dx_ref[rows, :] = dx
                dg = dg_part if dg is None else dg + dg_part
                if post is not None:
                    ohat, ro = _rms_stats(buf_ref[2, slot])
                    do, dgp_part = _rms_bwd(dx, ohat, ro, ins[3][...])
                    outs[2][rows, :] = do.astype(BF16)
                    dgp = dgp_part if dgp is None else dgp + dgp_part
            _acc_rows(dg_ref, first, dg)
            if post is not None:
                _acc_rows(outs[3], first, dgp)

        host.after(step, nt * nk)

    res = pl.pallas_call(
        body, name=name, grid=(nt, nk), in_specs=host.in_specs, out_specs=host.out_specs, out_shape=host.out_shape,
        scratch_shapes=host.scratch, input_output_aliases=host.aliases,
        compiler_params=_cp("arbitrary", "arbitrary"))(*args, *host.args)
    return host.results(res)


def _half_add(g, r1, c_arr, name):
    nsh, rows, ns = g.shape
    h = rows // 2
    tr = min(ROW_TILE, h)
    per = h // tr

    def body(c_ref, g_ref, r_ref, o_ref):
        o_ref[...] = (g_ref[...].astype(F32) + r_ref[...].astype(F32)).astype(BF16)

    spec = pl.BlockSpec((None, tr, ns), lambda s, r, c: (s, r, 0))
    return pl.pallas_call(
        body, name=name,
        grid_spec=pltpu.PrefetchScalarGridSpec(
            num_scalar_prefetch=1, grid=(nsh, per),
            in_specs=[pl.BlockSpec((None, tr, ns), lambda s, r, c: (s, c[0] * per + r, 0)), spec], out_specs=spec),
        out_shape=jax.ShapeDtypeStruct((nsh, h, ns), BF16), compiler_params=_cp("parallel", "parallel"))(c_arr, g, r1)


def _sum_chips(hh, r2, mc_arr, name):
    _, h, ns = hh.shape
    tr = min(ROW_TILE, h)
    per = h // tr

    def body(mc_ref, h_ref, a_ref, b_ref, c_ref, o_ref):
        o_ref[...] = ((h_ref[...].astype(F32) + a_ref[...].astype(F32)) + b_ref[...].astype(F32)) + c_ref[...].astype(F32)

    got = lambda k: pl.BlockSpec((None, tr, ns), lambda r, mc: (k, r, 0))
    return pl.pallas_call(
        body, name=name,
        grid_spec=pltpu.PrefetchScalarGridSpec(
            num_scalar_prefetch=1, grid=(per,),
            in_specs=[pl.BlockSpec((None, tr, ns), lambda r, mc: (mc[0], r, 0)), got(0), got(1), got(2)],
            out_specs=pl.BlockSpec((tr, ns), lambda r, mc: (mc[1] * per + r, 0))),
        out_shape=jax.ShapeDtypeStruct((2 * h, ns), F32), compiler_params=_cp("parallel"))(mc_arr, hh, r2, r2, r2)


def _add2(a, b, name):
    def body(a_ref, b_ref, o_ref):
        o_ref[...] = a_ref[...] + b_ref[...]

    return pl.pallas_call(body, name=name, out_shape=jax.ShapeDtypeStruct(a.shape, a.dtype), compiler_params=_cp())(a, b)


def _sum_chips_ordered(s2, r2, mc_arr, name):
    rows, w = s2.shape
    rh = rows // 2

    def body(mc_ref, s_ref, a_ref, b_ref, c_ref, o_ref):
        me = mc_ref[0]
        acc = None
        for j in range(N_CHIPS):
            rel = jnp.bitwise_xor(me, j)
            v = jnp.where(rel == 0, s_ref[...], jnp.where(rel == 2, a_ref[...], jnp.where(rel == 1, b_ref[...], c_ref[...])))
            acc = v if acc is None else acc + v
        o_ref[...] = acc

    got = lambda k: pl.BlockSpec((None, rh, w), lambda i, mc: (k, 0, 0))
    return pl.pallas_call(
        body, name=name,
        grid_spec=pltpu.PrefetchScalarGridSpec(
            num_scalar_prefetch=1, grid=(1,),
            in_specs=[pl.BlockSpec((rh, w), lambda i, mc: (mc[1], 0)), got(0), got(1), got(2)],
            out_specs=pl.BlockSpec((rh, w), lambda i, mc: (mc[1], 0))),
        out_shape=jax.ShapeDtypeStruct((rows, w), F32), compiler_params=_cp("arbitrary"))(mc_arr, s2, r2, r2, r2)


def _adamw(w, g, m, v, name, comm=None):
    r, c = w.shape
    tr = ROW_TILE if r % ROW_TILE == 0 else r
    c1 = 1.0 / (1.0 - ADAM_B1 ** ADAM_STEP)
    c2 = 1.0 / (1.0 - ADAM_B2 ** ADAM_STEP)
    spec = pl.BlockSpec((tr, c), lambda i: (i, 0))
    host = _Host(comm, [spec] * 4, [spec] * 4, [jax.ShapeDtypeStruct((r, c), F32)] * 4, [])

    def body(*refs):
        (w_ref, g_ref, m_ref, v_ref), (go_ref, d_ref, nm_ref, nv_ref), _ = host.split(refs)
        step = pl.program_id(0)
        host.before(step, r // tr)
        gv = g_ref[...]
        go_ref[...] = gv
        nm = ADAM_B1 * m_ref[...] + (1.0 - ADAM_B1) * gv
        nv = ADAM_B2 * v_ref[...] + (1.0 - ADAM_B2) * (gv * gv)
        nm_ref[...] = nm
        nv_ref[...] = nv
        d_ref[...] = -ADAM_LR * ((nm * c1) / (jnp.sqrt(nv * c2) + ADAM_EPS) + ADAM_WD * w_ref[...])
        host.after(step, r // tr)

    outs = pl.pallas_call(
        body, name=name, grid=(r // tr,), in_specs=host.in_specs, out_specs=host.out_specs, out_shape=host.out_shape,
        scratch_shapes=host.scratch, input_output_aliases=host.aliases,
        compiler_params=_cp("arbitrary"))(w, g, m, v, *host.args)
    return host.results(outs)


def _gather_weights(bigs, pool_w, pack_w, pack_d, name):
    nb = len(bigs)
    smalls = [pool_w, pack_w, pack_d]
    q, cw, cd = pool_w.shape[1], pack_w.shape[1], pack_d.shape[1]
    pieces = [_GatherPlan(bigs, (j, j + 1, GATHER_PIECES)) for j in range(GATHER_PIECES)]
    for j, piece in enumerate(pieces):
        piece.base = 9 + j * piece.nsems

    def body(*refs):
        srcs, dsts = refs[:nb + 3], refs[nb + 3:2 * (nb + 3)]
        ssem, rsem, lsem = refs[2 * (nb + 3):]
        x, y, c, me, chips, sib = _place()

        def small_dst(n, chip):
            if n == 0:
                return dsts[nb].at[:, pl.ds(chip * q, q), :]
            return dsts[nb + n].at[:, pl.ds(chip * (cw if n == 1 else cd), cw if n == 1 else cd)]

        local = [pltpu.make_async_copy(srcs[nb + n], small_dst(n, me), lsem.at[n]) for n in range(3)]
        for cp in local:
            cp.start()
        sends = []
        for n in range(3):
            for k, chip in enumerate(chips):
                cp = _rcopy(srcs[nb + n], small_dst(n, me), ssem.at[3 * n + k], rsem.at[3 * n + k], (*chip, c))
                cp.start()
                sends.append(cp)
        big = (srcs[:nb], dsts[:nb], ssem, rsem)
        for stage in ("start", "relay", "relay_far", "finish"):
            for piece in pieces:
                getattr(piece, stage)(*big)
        for n in range(3):
            for k, chip in enumerate(chips):
                ref = small_dst(n, 2 * chip[0] + chip[1])
                _rcopy(ref, ref, ssem.at[3 * n + k], rsem.at[3 * n + k], (*chip, c)).wait_recv()
        for cp in sends:
            cp.wait_send()
        for cp in local:
            cp.wait()

    nsem = 9 + sum(piece.nsems for piece in pieces)
    out_shape = [jax.ShapeDtypeStruct(b.shape, b.dtype) for b in bigs]
    out_shape += [jax.ShapeDtypeStruct((pool_w.shape[0], N_CHIPS * q, pool_w.shape[2]), pool_w.dtype),
                  jax.ShapeDtypeStruct((pack_w.shape[0], N_CHIPS * cw), pack_w.dtype),
                  jax.ShapeDtypeStruct((pack_d.shape[0], N_CHIPS * cd), pack_d.dtype)]
    return pl.pallas_call(
        body, name=name, in_specs=[ANY] * (nb + 3), out_specs=[ANY] * (nb + 3), out_shape=out_shape,
        input_output_aliases={a: a for a in range(nb)},
        scratch_shapes=[pltpu.SemaphoreType.DMA((nsem,)), pltpu.SemaphoreType.DMA((nsem,)), pltpu.SemaphoreType.DMA((3,))],
        compiler_params=pltpu.CompilerParams(has_side_effects=True))(*bigs, *smalls)


def _swap_with_sibling(grads, wholes, name):
    n, nw = len(grads), len(wholes)
    halves = [g.shape[1] // 2 for g in grads]

    def body(*refs):
        srcs, dsts = refs[:n + nw], refs[n + nw:2 * (n + nw)]
        ssem, rsem = refs[2 * (n + nw):]
        x, y, c, me, chips, sib = _place()
        cps = [_rcopy(srcs[a].at[:, pl.ds((1 - c) * halves[a], halves[a]), :], dsts[a], ssem.at[a], rsem.at[a], sib)
               for a in range(n)]
        cps += [_rcopy(srcs[a], dsts[a], ssem.at[a], rsem.at[a], sib) for a in range(n, n + nw)]
        for cp in cps:
            cp.start()
        for cp in cps:
            cp.wait_recv()
        for cp in cps:
            cp.wait_send()

    out_shape = [jax.ShapeDtypeStruct((g.shape[0], h, g.shape[2]), g.dtype) for g, h in zip(grads, halves)]
    out_shape += [jax.ShapeDtypeStruct(w.shape, w.dtype) for w in wholes]
    return pl.pallas_call(
        body, name=name, in_specs=[ANY] * (n + nw), out_specs=[ANY] * (n + nw), out_shape=out_shape,
        scratch_shapes=[pltpu.SemaphoreType.DMA((n + nw,)), pltpu.SemaphoreType.DMA((n + nw,))],
        compiler_params=pltpu.CompilerParams(has_side_effects=True))(*grads, *wholes)


def _scatter_to_chips(halves_in, small, name):
    n = len(halves_in)
    rh = small.shape[0] // 2

    def body(*refs):
        srcs, dsts = refs[:n + 1], refs[n + 1:2 * (n + 1)]
        ssem, rsem = refs[2 * (n + 1):]
        x, y, c, me, chips, sib = _place()
        cps = []
        for a in range(n + 1):
            for k, chip in enumerate(chips):
                src = srcs[a].at[2 * chip[0] + chip[1]] if a < n else srcs[a].at[pl.ds(c * rh, rh)]
                cps.append(_rcopy(src, dsts[a].at[k], ssem.at[3 * a + k], rsem.at[3 * a + k], (*chip, c)))
        for cp in cps:
            cp.start()
        for cp in cps:
            cp.wait_recv()
        for cp in cps:
            cp.wait_send()

    out_shape = [jax.ShapeDtypeStruct((3,) + h.shape[1:], h.dtype) for h in halves_in]
    out_shape.append(jax.ShapeDtypeStruct((3, rh, small.shape[1]), small.dtype))
    return pl.pallas_call(
        body, name=name, in_specs=[ANY] * (n + 1), out_specs=[ANY] * (n + 1), out_shape=out_shape,
        scratch_shapes=[pltpu.SemaphoreType.DMA((3 * (n + 1),)), pltpu.SemaphoreType.DMA((3 * (n + 1),))],
        compiler_params=pltpu.CompilerParams(has_side_effects=True))(*halves_in, small)


def _join_halves(parts, name):
    n = len(parts)

    def body(*refs):
        srcs, dsts = refs[:n], refs[n:2 * n]
        ssem, rsem = refs[2 * n:]
        x, y, c, me, chips, sib = _place()
        cps = []
        for a in range(n):
            h = srcs[a].shape[0] // 2
            cps.append(_rcopy(srcs[a].at[pl.ds(c * h, h)], dsts[a].at[pl.ds(c * h, h)], ssem.at[a], rsem.at[a], sib))
        for cp in cps:
            cp.start()
        for a in range(n):
            h = srcs[a].shape[0] // 2
            theirs = dsts[a].at[pl.ds((1 - c) * h, h)]
            _rcopy(theirs, theirs, ssem.at[a], rsem.at[a], sib).wait_recv()
        for cp in cps:
            cp.wait_send()

    out_shape = [jax.ShapeDtypeStruct(p.shape, p.dtype) for p in parts]
    return pl.pallas_call(
        body, name=name, in_specs=[ANY] * n, out_specs=[ANY] * n, out_shape=out_shape,
        input_output_aliases={a: a for a in range(n)},
        scratch_shapes=[pltpu.SemaphoreType.DMA((n,)), pltpu.SemaphoreType.DMA((n,))],
        compiler_params=pltpu.CompilerParams(has_side_effects=True))(*parts)


def _pad_rows(a, rows):
    return jnp.pad(a, ((0, rows - a.shape[0]), (0, 0)))


def _stack_rows(parts, multiple):
    padded = [_pad_rows(p, -(-p.shape[0] // 8) * 8) for p in parts]
    starts, at = [], 0
    for p in padded:
        starts.append(at)
        at += p.shape[0]
    total = -(-at // multiple) * multiple
    if total > at:
        padded.append(jnp.zeros((total - at, parts[0].shape[1]), parts[0].dtype))
    return jnp.concatenate(padded, axis=0), starts


def kernel(x, ln_pre_even, w_in_even, pool_w, pool_scale, w_out_even, ln_post_even, ln_pre_odd, w_in_odd, sconv_w, dconv_w, dconv_b, cnorm_g, cnorm_b, w_out_odd, ln_post_odd, loss_target, m_ln_pre_even, m_w_in_even, m_pool_w, m_pool_scale, m_w_out_even, m_ln_post_even, m_ln_pre_odd, m_w_in_odd, m_sconv_w, m_dconv_w, m_dconv_b, m_cnorm_g, m_cnorm_b, m_w_out_odd, m_ln_post_odd, v_ln_pre_even, v_w_in_even, v_pool_w, v_pool_scale, v_w_out_even, v_ln_post_even, v_ln_pre_odd, v_w_in_odd, v_sconv_w, v_dconv_w, v_dconv_b, v_cnorm_g, v_cnorm_b, v_w_out_odd, v_ln_post_odd):
    _, s, d = x.shape
    half = d // 2
    cw = half // N_CHIPS
    ng, q, gd = pool_w.shape[1:]
    k3, k31 = sconv_w.shape[1], dconv_w.shape[1]
    x2d, tgt = x[0], loss_target[0]
    me = 2 * lax.axis_index("x") + lax.axis_index("y")
    core = lax.axis_index("c")
    c_arr = jnp.reshape(core, (1,)).astype(jnp.int32)
    me_arr = jnp.reshape(me, (1,)).astype(jnp.int32)
    mc_arr = jnp.stack([me, core]).astype(jnp.int32)

    shards = [w_in_even[0], w_out_even[0], w_in_odd[0], w_out_odd[0]]
    slabs = [_cast_bf16_own_slab(w, me_arr, f"cast_w{n}") for n, w in enumerate(shards)]
    pool_w_b = _cast_bf16(pool_w[0].reshape(ng * q, gd), "cast_pool_w").reshape(ng, q, gd)
    pack_w, at_w = _stack_rows([sconv_w[0], dconv_w[0], dconv_b, cnorm_g, cnorm_b], 8)
    pack_d, at_d = _stack_rows([ln_pre_odd, ln_post_odd], 8)
    win_e, pool_w_f, pack_w_f, pack_d_f = _gather_weights(slabs[:1], pool_w_b, pack_w, pack_d, "gather_first")
    sconv_f = pack_w_f[at_w[0]:at_w[0] + k3]
    dconv_f = pack_w_f[at_w[1]:at_w[1] + k31]
    dconv_b_f, cnorm_g_f, cnorm_b_f = (pack_w_f[at_w[n]:at_w[n] + 1] for n in (2, 3, 4))
    ln_pre_odd_f = pack_d_f[at_d[0]:at_d[0] + 1]
    ln_post_odd_f = pack_d_f[at_d[1]:at_d[1] + 1]

    def reduce_half(g, name):
        (got,) = _swap_with_sibling([g], [], "swap_" + name)
        return _half_add(g, got, c_arr, "half_add_" + name)

    h0 = _rms_fwd(x2d, ln_pre_even, "rms_pre_even")
    plans = _Multi([_GatherPlan([slabs[1]], at=(0.6, 0.88)), _GatherPlan([slabs[2]], (0, 1, 4), at=(0.6, 0.88))])
    p_e, extra = _mm_nn(h0, win_e, "proj_in_even", plans)
    (wout_e,), (win_o,) = plans.results(extra)
    wout_e = wout_e.reshape(d, d)
    att, ltot, (win_o,) = _sba_fwd(p_e, half, "sba_fwd", _GatherPlan([win_o], (1, 4, 4), at=(0.69, 0.94)))
    y_e = _even_mix_fwd(p_e, att, pool_w_f, pool_scale, d, "even_mix_fwd")
    o_e, x1, h1 = _mm_out_even(y_e, wout_e, x2d, ln_post_even, ln_pre_odd_f, "proj_out_even")
    p_o, (wout_o,) = _mm_nn(h1, win_o, "proj_in_odd", _GatherPlan([slabs[3]]))
    wout_o = wout_o.reshape(d, d)
    y_o, s3, d1 = _odd_mix_fwd(p_o, sconv_f, dconv_f, dconv_b_f, cnorm_g_f, cnorm_b_f, d, "odd_mix_fwd")
    do_o, dx2, loss_blk, dln_post_odd = _mm_out_odd(y_o, wout_o, x1, ln_post_odd_f, tgt, "proj_out_odd_loss")

    dy_o = _mm_nt(do_o, wout_o, "dy_odd")
    g_wout_o = _mm_tn(y_o, do_o, 1, "dw_out_odd")[0].reshape(N_CHIPS, d // N_CHIPS, d)
    h_wout_o = reduce_half(g_wout_o, "out_odd")
    dbc, dgate_o, ds3, dd1, dcnorm_g, dcnorm_b, ddconv_b = _odd_bwd_rows(p_o, s3, d1, dy_o, cnorm_g_f, cnorm_b_f, d, "odd_bwd_rows")
    dhc, dcc, dga, dgb, dsconv, ddconv = _odd_bwd_conv(p_o, ds3, dd1, sconv_f, dconv_f, d, "odd_bwd_conv")
    dp_o = jnp.concatenate([dhc, dbc, dcc, dga, dgb, dgate_o], axis=1)
    g_win_o, (s_wout_o,) = _mm_tn(h1, dp_o, N_CHIPS, "dw_in_odd", _ScatterPlan([h_wout_o]))
    h_win_o = reduce_half(g_win_o, "in_odd")
    (dx1, dln_pre_odd, do_e, dln_post_even), (s_win_o,) = _mm_in_bwd(
        dp_o, win_o, x1, ln_pre_odd_f, dx2, (o_e, ln_post_even), "dx_odd", _ScatterPlan([h_win_o], (0, 1, 2)))

    dy_e = _mm_nt(do_e, wout_e, "dy_even")
    g_wout_e = _mm_tn(y_e, do_e, 1, "dw_out_even")[0].reshape(N_CHIPS, d // N_CHIPS, d)
    h_wout_e = reduce_half(g_wout_e, "out_even")
    datt, du, dgate_e, dpool_scale, dpool_w = _even_mix_bwd(p_e, att, dy_e, pool_w_f, pool_scale, d, "even_mix_bwd")
    two = lambda v: v.reshape(2, half)
    small_parts = [dpool_scale, two(dln_post_even), two(dln_pre_odd), two(dln_post_odd),
                   dsconv, ddconv, ddconv_b, dcnorm_g, dcnorm_b, dpool_w.reshape(gd, half)]
    small, at_s = _stack_rows(small_parts, 16)
    (small1,) = _swap_with_sibling([], [small], "swap_small")
    small2 = _add2(small, small1, "small_add")
    plans = _Multi([_ScatterPlan([h_win_o], (1, 2, 2), into=[s_win_o]), _ScatterPlan([h_wout_e]), _ShareHalfPlan([small2])])
    dq, dk, dv, extra = _sba_bwd(p_e, ltot, datt, half, "sba_bwd", plans)
    (s_win_o,), (s_wout_e,), (small_got,) = plans.results(extra)
    dp_e = jnp.concatenate([dq, dk, dv, du, dgate_e], axis=1)
    g_win_e, _ = _mm_tn(h0, dp_e, N_CHIPS, "dw_in_even")
    h_win_e = reduce_half(g_win_e, "in_even")
    (grad_x, dln_pre_even), (s_win_e,) = _mm_in_bwd(dp_e, win_e, x2d, ln_pre_even, dx1, None, "dx_even", _ScatterPlan([h_win_e]))

    last, at_l = _stack_rows([two(dln_pre_even), jnp.pad(loss_blk[0:1], ((0, 0), (0, half - LANES)))], 16)
    (last1,) = _swap_with_sibling([], [last], "swap_last")
    last2 = _add2(last, last1, "last_add")
    (last_got,) = _scatter_to_chips([], last2, "scatter_last")
    pairs = [(h_win_e, s_win_e), (h_wout_e, s_wout_e), (h_win_o, s_win_o), (h_wout_o, s_wout_o)]
    parts = [_sum_chips(h, r, mc_arr, f"sum_chips{n}") for n, (h, r) in enumerate(pairs)]
    parts.append(_sum_chips_ordered(small2, small_got, mc_arr, "small_sum"))
    parts.append(_sum_chips_ordered(last2, last_got, mc_arr, "last_sum"))
    gw_in_e, gw_out_e, gw_in_o, gw_out_o, red, red_last = _join_halves(parts, "join_halves")
    loss = red_last[at_l[1], 0]

    def rows(n, cnt):
        return red[at_s[n]:at_s[n] + cnt]

    def mine(a, width):
        return lax.dynamic_slice_in_dim(a, me * width, width, axis=1)

    quarter = d // N_CHIPS
    g_small = {
        "ln_pre_even": red_last[at_l[0]:at_l[0] + 2].reshape(1, d),
        "pool_scale": rows(0, 1),
        "ln_post_even": rows(1, 2).reshape(1, d),
        "ln_pre_odd": mine(rows(2, 2).reshape(1, d), quarter),
        "ln_post_odd": mine(rows(3, 2).reshape(1, d), quarter),
        "sconv_w": mine(rows(4, k3), cw),
        "dconv_w": mine(rows(5, k31), cw),
        "dconv_b": mine(rows(6, 1), cw),
        "cnorm_g": mine(rows(7, 1), cw),
        "cnorm_b": mine(rows(8, 1), cw),
        "pool_w": lax.dynamic_slice_in_dim(rows(9, gd).reshape(ng, gd, gd), me * q, q, axis=1).reshape(ng * q, gd),
    }
    w2d = {
        "ln_pre_even": ln_pre_even, "w_in_even": w_in_even[0], "pool_w": pool_w[0].reshape(ng * q, gd),
        "pool_scale": pool_scale, "w_out_even": w_out_even[0], "ln_post_even": ln_post_even, "ln_pre_odd": ln_pre_odd,
        "w_in_odd": w_in_odd[0], "sconv_w": sconv_w[0], "dconv_w": dconv_w[0], "dconv_b": dconv_b, "cnorm_g": cnorm_g,
        "cnorm_b": cnorm_b, "w_out_odd": w_out_odd[0], "ln_post_odd": ln_post_odd,
    }
    moments = {
        "ln_pre_even": (m_ln_pre_even, v_ln_pre_even), "w_in_even": (m_w_in_even, v_w_in_even),
        "pool_w": (m_pool_w, v_pool_w), "pool_scale": (m_pool_scale, v_pool_scale),
        "w_out_even": (m_w_out_even, v_w_out_even), "ln_post_even": (m_ln_post_even, v_ln_post_even),
        "ln_pre_odd": (m_ln_pre_odd, v_ln_pre_odd), "w_in_odd": (m_w_in_odd, v_w_in_odd),
        "sconv_w": (m_sconv_w, v_sconv_w), "dconv_w": (m_dconv_w, v_dconv_w), "dconv_b": (m_dconv_b, v_dconv_b),
        "cnorm_g": (m_cnorm_g, v_cnorm_g), "cnorm_b": (m_cnorm_b, v_cnorm_b),
        "w_out_odd": (m_w_out_odd, v_w_out_odd), "ln_post_odd": (m_ln_post_odd, v_ln_post_odd),
    }
    g2d = dict(g_small, w_in_even=gw_in_e, w_out_even=gw_out_e, w_in_odd=gw_in_o, w_out_odd=gw_out_o)
    updates = {}
    for name, w in w2d.items():
        m_in, v_in = moments[name]
        updates[name], _ = _adamw(w, g2d[name], m_in.reshape(w.shape), v_in.reshape(w.shape), "adamw_" + name)
    outs = [[u.reshape(moments[name][0].shape) for u in updates[name]] for name in w2d]
    grads_out, deltas, new_m, new_v = zip(*outs)
    return (loss, grad_x.reshape(x.shape), *grads_out, *deltas, *new_m, *new_v)
```

```python
import functools
import math

import jax
import jax.numpy as jnp
from jax import lax
from jax.experimental import pallas as pl
from jax.experimental.pallas import tpu as pltpu

F32 = jnp.float32
BF16 = jnp.bfloat16
EPS = 1e-6
N_CHIPS = 4
VMEM_LIMIT_V7X = 56 << 20
HEAD_DIM = 128
ATT_BLOCK = 256
POOL_WINDOWS = (2, 4, 8, 16)
ROW_TILE = 256
POOL_HALO = 16
CONV_HALO = 32
LANES = 128
ADAM_LR, ADAM_B1, ADAM_B2, ADAM_EPS, ADAM_WD, ADAM_STEP = 0.001, 0.9, 0.999, 1e-08, 0.01, 10
MESH_ID = pl.DeviceIdType.MESH
ANY = pl.BlockSpec(memory_space=pl.ANY)


def _cp(*sem):
    return pltpu.CompilerParams(dimension_semantics=sem or None, vmem_limit_bytes=VMEM_LIMIT_V7X)


def _pick_tile(n, cap):
    best = None
    for t in range(LANES, min(n, cap) + 1, LANES):
        if n % t == 0:
            best = t
    assert best is not None, (n, cap)
    return best


def _sigmoid(x):
    return 1.0 / (1.0 + jnp.exp(-x))


def _silu(x):
    return x * _sigmoid(x)


def _dsilu(x):
    s = _sigmoid(x)
    return s * (1.0 + x * (1.0 - s))


def _log_sigmoid(z):
    return jnp.minimum(z, 0.0) - jnp.log(1.0 + jnp.exp(-jnp.abs(z)))


def _rms_stats(x):
    r = lax.rsqrt(jnp.mean(x * x, axis=-1, keepdims=True) + EPS)
    return x * r, r


def _rms_bwd(dh, xhat, r, g):
    dxh = dh * g
    dx = r * (dxh - xhat * jnp.mean(dxh * xhat, axis=-1, keepdims=True))
    return dx, jnp.sum(dh * xhat, axis=0, keepdims=True)


def _acc_rows(ref, first, val):
    @pl.when(first)
    def _():
        ref[...] = val

    @pl.when(jnp.logical_not(first))
    def _():
        ref[...] += val


def _rcopy(src, dst, ssem, rsem, dev):
    return pltpu.make_async_remote_copy(src_ref=src, dst_ref=dst, send_sem=ssem, recv_sem=rsem,
                                        device_id=dev, device_id_type=MESH_ID)


def _place():
    x, y, c = lax.axis_index("x"), lax.axis_index("y"), lax.axis_index("c")
    chips = [(1 - x, y), (x, 1 - y), (1 - x, 1 - y)]
    return x, y, c, 2 * x + y, chips, (x, y, 1 - c)


class _GatherPlan:
    PER_ARRAY = 7

    def __init__(self, arrays, part=(0, 1, 1), at=(0.5, 0.8)):
        self.operands = list(arrays)
        self.out_shapes = [jax.ShapeDtypeStruct(a.shape, a.dtype) for a in arrays]
        self.aliases = {i: i for i in range(len(arrays))}
        self.nsems = self.PER_ARRAY * len(arrays)
        self.base = 0
        self.halves = [a.shape[1] // 2 for a in arrays]
        self.part = part
        self.at = at

    def schedule(self):
        return [(0.0, self.start), (self.at[0], self.relay), (self.at[1], self.relay_far)]

    def _rows(self, ref, a, chip, half, quarter=None):
        lo, hi, n = self.part
        h = self.halves[a]
        first, size = half * h + lo * h // n, (hi - lo) * h // n
        if quarter is not None:
            first, size = first + quarter * (size // 2), size // 2
        return ref.at[chip, pl.ds(first, size)]

    def _copy(self, src, dst, a, n, ssem, rsem, dev):
        return _rcopy(src, dst, ssem.at[self.base + self.PER_ARRAY * a + n], rsem.at[self.base + self.PER_ARRAY * a + n], dev)

    def _own(self, ins, outs, ssem, rsem):
        x, y, c, me, chips, sib = _place()
        return [self._copy(self._rows(ins[a], a, me, c), self._rows(outs[a], a, me, c), a, k, ssem, rsem, (*chips[k], c))
                for a in range(len(ins)) for k in (0, 1)]

    def _relays(self, outs, ssem, rsem, a, k):
        x, y, c, me, chips, sib = _place()
        chip = 2 * chips[k][0] + chips[k][1]
        whole, quarter = self._rows(outs[a], a, chip, c), self._rows(outs[a], a, chip, c, k)
        return (self._copy(whole, whole, a, k, ssem, rsem, (*chips[k], c)),
                self._copy(quarter, quarter, a, 2 + k, ssem, rsem, (*chips[1 - k], c)),
                self._copy(whole, whole, a, 4 + k, ssem, rsem, sib))

    def _far(self, outs, ssem, rsem, a):
        x, y, c, me, chips, sib = _place()
        chip = 2 * chips[2][0] + chips[2][1]
        whole = self._rows(outs[a], a, chip, c)
        got = [self._copy(q, q, a, 2 + k, ssem, rsem, (*chips[1 - k], c))
               for k, q in enumerate([self._rows(outs[a], a, chip, c, 0), self._rows(outs[a], a, chip, c, 1)])]
        return got, self._copy(whole, whole, a, 6, ssem, rsem, sib)

    def start(self, ins, outs, ssem, rsem):
        for cp in self._own(ins, outs, ssem, rsem):
            cp.start()

    def relay(self, ins, outs, ssem, rsem):
        for a in range(len(outs)):
            for k in (0, 1):
                landed, onward, to_sibling = self._relays(outs, ssem, rsem, a, k)
                landed.wait_recv()
                onward.start()
                to_sibling.start()

    def relay_far(self, ins, outs, ssem, rsem):
        for a in range(len(outs)):
            got, to_sibling = self._far(outs, ssem, rsem, a)
            for cp in got:
                cp.wait_recv()
            to_sibling.start()

    def finish(self, ins, outs, ssem, rsem):
        x, y, c, me, chips, sib = _place()
        for a in range(len(outs)):
            for k in range(3):
                ref = self._rows(outs[a], a, 2 * chips[k][0] + chips[k][1], 1 - c)
                self._copy(ref, ref, a, 4 + k, ssem, rsem, sib).wait_recv()
        for cp in self._own(ins, outs, ssem, rsem):
            cp.wait_send()
        for a in range(len(outs)):
            for k in (0, 1):
                _, onward, to_sibling = self._relays(outs, ssem, rsem, a, k)
                onward.wait_send()
                to_sibling.wait_send()
            self._far(outs, ssem, rsem, a)[1].wait_send()


class _ScatterPlan:
    def __init__(self, arrays, part=(0, 1, 1), into=None):
        self.n = len(arrays)
        self.operands = list(arrays) + list(into or [])
        self.out_shapes = [jax.ShapeDtypeStruct((3,) + a.shape[1:], a.dtype) for a in arrays]
        self.aliases = {self.n + i: i for i in range(self.n)} if into else {}
        self.nsems = 3 * self.n
        self.base = 0
        self.part = part

    def _copies(self, ins, outs, ssem, rsem):
        x, y, c, me, chips, sib = _place()
        lo, hi, n = self.part
        out = []
        for a in range(self.n):
            h = ins[a].shape[1]
            rows = pl.ds(lo * h // n, (hi - lo) * h // n)
            for k, chip in enumerate(chips):
                out.append(_rcopy(ins[a].at[2 * chip[0] + chip[1], rows], outs[a].at[k, rows],
                                  ssem.at[self.base + 3 * a + k], rsem.at[self.base + 3 * a + k], (*chip, c)))
        return out

    def schedule(self):
        return [(0.0, self.start)]

    def start(self, ins, outs, ssem, rsem):
        for cp in self._copies(ins, outs, ssem, rsem):
            cp.start()

    def finish(self, ins, outs, ssem, rsem):
        cps = self._copies(ins, outs, ssem, rsem)
        for cp in cps:
            cp.wait_recv()
        for cp in cps:
            cp.wait_send()


class _ShareHalfPlan(_ScatterPlan):
    def __init__(self, arrays):
        super().__init__(arrays)
        self.out_shapes = [jax.ShapeDtypeStruct((3, a.shape[0] // 2, a.shape[1]), a.dtype) for a in arrays]

    def _copies(self, ins, outs, ssem, rsem):
        x, y, c, me, chips, sib = _place()
        out = []
        for a in range(self.n):
            rh = ins[a].shape[0] // 2
            for k, chip in enumerate(chips):
                out.append(_rcopy(ins[a].at[pl.ds(c * rh, rh)], outs[a].at[k],
                                  ssem.at[self.base + 3 * a + k], rsem.at[self.base + 3 * a + k], (*chip, c)))
        return out


class _SwapPlan:
    def __init__(self, grads):
        self.operands = list(grads)
        self.out_shapes = [jax.ShapeDtypeStruct((g.shape[0], g.shape[1] // 2, g.shape[2]), g.dtype) for g in grads]
        self.aliases = {}
        self.nsems = len(grads)
        self.base = 0

    def _copies(self, ins, outs, ssem, rsem):
        x, y, c, me, chips, sib = _place()
        out = []
        for a, src in enumerate(ins):
            h = src.shape[1] // 2
            out.append(_rcopy(src.at[:, pl.ds((1 - c) * h, h), :], outs[a], ssem.at[self.base + a], rsem.at[self.base + a], sib))
        return out

    def schedule(self):
        return [(0.0, self.start)]

    def start(self, ins, outs, ssem, rsem):
        for cp in self._copies(ins, outs, ssem, rsem):
            cp.start()

    def finish(self, ins, outs, ssem, rsem):
        cps = self._copies(ins, outs, ssem, rsem)
        for cp in cps:
            cp.wait_recv()
        for cp in cps:
            cp.wait_send()


class _Multi:
    def __init__(self, plans):
        self.plans = plans
        self.operands, self.out_shapes, self.aliases, self.nsems = [], [], {}, 0
        self.spans = []
        for p in plans:
            ni, no = len(self.operands), len(self.out_shapes)
            self.spans.append((ni, ni + len(p.operands), no, no + len(p.out_shapes)))
            self.aliases.update({ni + i: no + j for i, j in p.aliases.items()})
            p.base = self.nsems
            self.nsems += p.nsems
            self.operands += p.operands
            self.out_shapes += p.out_shapes

    def schedule(self):
        def bound(fn, span):
            i0, i1, o0, o1 = span
            return lambda ins, outs, ssem, rsem: fn(ins[i0:i1], outs[o0:o1], ssem, rsem)

        stages = [(at, bound(fn, span)) for p, span in zip(self.plans, self.spans) for at, fn in p.schedule()]
        return sorted(stages, key=lambda s: s[0])

    def finish(self, ins, outs, ssem, rsem):
        for p, (i0, i1, o0, o1) in zip(self.plans, self.spans):
            p.finish(ins[i0:i1], outs[o0:o1], ssem, rsem)

    def results(self, extra):
        return [list(extra[o0:o1]) for (_, _, o0, o1) in self.spans]


class _Host:
    def __init__(self, comm, in_specs, out_specs, out_shape, scratch):
        self.comm = comm
        self.n_in, self.n_out = len(in_specs), len(out_specs)
        self.in_specs, self.out_specs, self.out_shape, self.scratch = list(in_specs), list(out_specs), list(out_shape), list(scratch)
        self.aliases = {}
        self.args = []
        if comm is not None:
            self.in_specs += [ANY] * len(comm.operands)
            self.out_specs += [ANY] * len(comm.out_shapes)
            self.out_shape += comm.out_shapes
            self.scratch += [pltpu.SemaphoreType.DMA((comm.nsems,)), pltpu.SemaphoreType.DMA((comm.nsems,))]
            self.aliases = {self.n_in + i: self.n_out + j for i, j in comm.aliases.items()}
            self.args = list(comm.operands)

    def split(self, refs):
        nc = len(self.args)
        nco = len(self.out_shape) - self.n_out
        ins, p = refs[:self.n_in], self.n_in + nc
        outs, rest = refs[p:p + self.n_out], refs[p + self.n_out + nco:]
        self._cargs = None
        if self.comm is not None:
            self._cargs = (refs[self.n_in:p], refs[p + self.n_out:p + self.n_out + nco], rest[-2], rest[-1])
            rest = rest[:-2]
        return ins, outs, rest

    def before(self, step, total):
        if self.comm is None:
            return

        for at, stage in self.comm.schedule():
            pl.when(step == min(total - 1, int(at * total)))(functools.partial(stage, *self._cargs))

    def after(self, step, total):
        if self.comm is None:
            return

        @pl.when(step == total - 1)
        def _():
            self.comm.finish(*self._cargs)

    def results(self, outs):
        return outs[:self.n_out], outs[self.n_out:]


def _cast_bf16(x, name):
    r, c = x.shape
    tr = ROW_TILE if r % ROW_TILE == 0 else r

    def body(x_ref, o_ref):
        o_ref[...] = x_ref[...].astype(BF16)

    return pl.pallas_call(
        body, name=name, grid=(r // tr,),
        in_specs=[pl.BlockSpec((tr, c), lambda i: (i, 0))],
        out_specs=pl.BlockSpec((tr, c), lambda i: (i, 0)),
        out_shape=jax.ShapeDtypeStruct((r, c), BF16), compiler_params=_cp("parallel"))(x)


def _cast_bf16_own_slab(x, me_arr, name):
    r, c = x.shape
    tr = ROW_TILE if r % ROW_TILE == 0 else r

    def body(me_ref, x_ref, o_ref):
        o_ref[...] = x_ref[...].astype(BF16)

    return pl.pallas_call(
        body, name=name,
        grid_spec=pltpu.PrefetchScalarGridSpec(
            num_scalar_prefetch=1, grid=(r // tr,),
            in_specs=[pl.BlockSpec((tr, c), lambda i, me: (i, 0))],
            out_specs=pl.BlockSpec((None, tr, c), lambda i, me: (me[0], i, 0))),
        out_shape=jax.ShapeDtypeStruct((N_CHIPS, r, c), BF16), compiler_params=_cp("parallel"))(me_arr, x)


def _rms_fwd(x, g, name):
    s, d = x.shape

    def body(x_ref, g_ref, h_ref):
        xhat, _ = _rms_stats(x_ref[...])
        h_ref[...] = (xhat * g_ref[...]).astype(BF16)

    return pl.pallas_call(
        body, name=name, grid=(s // ROW_TILE,),
        in_specs=[pl.BlockSpec((ROW_TILE, d), lambda i: (i, 0)), pl.BlockSpec((1, d), lambda i: (0, 0))],
        out_specs=pl.BlockSpec((ROW_TILE, d), lambda i: (i, 0)),
        out_shape=jax.ShapeDtypeStruct((s, d), BF16), compiler_params=_cp("parallel"))(x, g)


def _mm_nn(a, w3, name, comm=None):
    m, k = a.shape
    nsh, _, ns = w3.shape
    tm = 512 if m % 512 == 0 else ROW_TILE
    tn = _pick_tile(ns, 1024)
    per = ns // tn
    grid = (nsh * per, m // tm)
    host = _Host(comm,
                 [pl.BlockSpec((tm, k), lambda n, i: (i, 0)), pl.BlockSpec((None, k, tn), lambda n, i: (n // per, 0, n % per))],
                 [pl.BlockSpec((tm, tn), lambda n, i: (i, n))], [jax.ShapeDtypeStruct((m, nsh * ns), F32)], [])

    def body(*refs):
        (a_ref, w_ref), (o_ref,), _ = host.split(refs)
        step = pl.program_id(0) * grid[1] + pl.program_id(1)
        host.before(step, grid[0] * grid[1])
        o_ref[...] = jnp.dot(a_ref[...], w_ref[...], preferred_element_type=F32)
        host.after(step, grid[0] * grid[1])

    outs = pl.pallas_call(
        body, name=name, grid=grid, in_specs=host.in_specs, out_specs=host.out_specs, out_shape=host.out_shape,
        scratch_shapes=host.scratch, input_output_aliases=host.aliases,
        compiler_params=_cp("arbitrary", "arbitrary"))(a, w3, *host.args)
    (out,), extra = host.results(outs)
    return out, extra


def _mm_nt(a, b, name):
    m, k = a.shape
    n = b.shape[0]
    tm = 512 if m % 512 == 0 else ROW_TILE

    def body(a_ref, b_ref, o_ref):
        o_ref[...] = lax.dot_general(a_ref[...], b_ref[...], (((1,), (1,)), ((), ())), preferred_element_type=F32)

    return pl.pallas_call(
        body, name=name, grid=(m // tm,),
        in_specs=[pl.BlockSpec((tm, k), lambda i: (i, 0)), pl.BlockSpec((n, k), lambda i: (0, 0))],
        out_specs=pl.BlockSpec((tm, n), lambda i: (i, 0)),
        out_shape=jax.ShapeDtypeStruct((m, n), F32), compiler_params=_cp("parallel"))(a, b)


def _mm_tn(a, b, nsh, name, comm=None):
    s, m = a.shape
    n = b.shape[1]
    ns = n // nsh
    tm = 512 if m % 512 == 0 else ROW_TILE
    tn = _pick_tile(ns, 1024)
    per = ns // tn
    grid = (nsh * per, m // tm)
    host = _Host(comm, [pl.BlockSpec((s, tm), lambda j, i: (0, i)), pl.BlockSpec((s, tn), lambda j, i: (0, j))],
                 [pl.BlockSpec((None, tm, tn), lambda j, i: (j // per, i, j % per))],
                 [jax.ShapeDtypeStruct((nsh, m, ns), BF16)], [])

    def body(*refs):
        (a_ref, b_ref), (o_ref,), _ = host.split(refs)
        step = pl.program_id(0) * grid[1] + pl.program_id(1)
        host.before(step, grid[0] * grid[1])
        o_ref[...] = lax.dot_general(a_ref[...], b_ref[...], (((0,), (0,)), ((), ())),
                                     preferred_element_type=F32).astype(BF16)
        host.after(step, grid[0] * grid[1])

    outs = pl.pallas_call(
        body, name=name, grid=grid, in_specs=host.in_specs, out_specs=host.out_specs, out_shape=host.out_shape,
        scratch_shapes=host.scratch, input_output_aliases=host.aliases,
        compiler_params=_cp("arbitrary", "arbitrary"))(a, b, *host.args)
    (out,), extra = host.results(outs)
    return out, extra


def _tri(n, rel):
    row = lax.broadcasted_iota(jnp.int32, (2 * n, n), 0)
    col = lax.broadcasted_iota(jnp.int32, (2 * n, n), 1)
    return jnp.where(rel(jnp.where(row >= n, row - n, row), col), 1.0, 0.0).astype(BF16)


def _dot_split(x, tri2):
    hi = x.astype(BF16)
    lo = (x - hi.astype(F32)).astype(BF16)
    return jnp.dot(jnp.concatenate([hi, lo], axis=1), tri2, preferred_element_type=F32)


def _nt(a, b):
    return lax.dot_general(a, b, (((1,), (1,)), ((), ())), preferred_element_type=F32)


def _tn(a, b):
    return lax.dot_general(a, b, (((0,), (0,)), ((), ())), preferred_element_type=F32)


def _heads_per_step(nh):
    return max(h for h in (1, 2, 4) if nh % h == 0)


def _sba_fwd(p, sbw, name, comm=None):
    s = p.shape[0]
    nh = sbw // HEAD_DIM
    hp = _heads_per_step(nh)
    ngrp, hw = nh // hp, hp * HEAD_DIM
    blk = ATT_BLOCK
    nq = s // blk
    scale = 1.0 / math.sqrt(HEAD_DIM)
    host = _Host(comm,
                 [pl.BlockSpec((blk, hw), lambda g, i: (i, g)),
                  pl.BlockSpec((s, hw), lambda g, i: (0, ngrp + g)),
                  pl.BlockSpec((s, hw), lambda g, i: (0, 2 * ngrp + g))],
                 [pl.BlockSpec((blk, hw), lambda g, i: (i, g))] * 2,
                 [jax.ShapeDtypeStruct((s, sbw), F32)] * 2,
                 [pltpu.VMEM((s, hw), BF16)] * 2)

    def body(*refs):
        (q_ref, k_ref, v_ref), (o_ref, lt_ref), (kb_ref, vb_ref) = host.split(refs)
        i = pl.program_id(1)
        step = pl.program_id(0) * nq + i
        host.before(step, ngrp * nq)

        @pl.when(i == 0)
        def _():
            kb_ref[...] = k_ref[...].astype(BF16)
            vb_ref[...] = v_ref[...].astype(BF16)

        heads = [slice(h * HEAD_DIM, (h + 1) * HEAD_DIM) for h in range(hp)]
        qs = [q_ref[:, hd].astype(BF16) for hd in heads]
        later = _tri(blk, lambda r, c: r > c)
        causal = lax.broadcasted_iota(jnp.int32, (blk, blk), 1) < lax.broadcasted_iota(jnp.int32, (blk, blk), 0)

        def key_block(j, carry, diagonal):
            rows = pl.ds(pl.multiple_of(j * blk, blk), blk)
            hs = range(hp)
            z = [_nt(qs[h], kb_ref[rows, heads[h]]) * scale for h in hs]
            ls = [_log_sigmoid(z[h]) for h in hs]
            lm = [jnp.where(causal, ls[h] - z[h], 0.0) if diagonal else ls[h] - z[h] for h in hs]
            stay = [_dot_split(lm[h], later) for h in hs]
            w = [jnp.exp(ls[h] + stay[h] + carry[h][1]) for h in hs]
            if diagonal:
                w = [jnp.where(causal, w[h], 0.0) for h in hs]
            acc = [carry[h][0] + jnp.dot(w[h].astype(BF16), vb_ref[rows, heads[h]], preferred_element_type=F32) for h in hs]
            return tuple((acc[h], carry[h][1] + jnp.sum(lm[h], axis=1, keepdims=True)) for h in hs)

        init = tuple((jnp.zeros((blk, HEAD_DIM), F32), jnp.zeros((blk, 1), F32)) for _ in heads)
        carry = key_block(i, init, True)
        carry = lax.fori_loop(0, i, lambda n, c: key_block(i - 1 - n, c, False), carry)
        for h, hd in enumerate(heads):
            o_ref[:, hd] = carry[h][0]
            lt_ref[:, hd] = jnp.broadcast_to(carry[h][1], (blk, HEAD_DIM))
        host.after(step, ngrp * nq)

    outs = pl.pallas_call(
        body, name=name, grid=(ngrp, nq), in_specs=host.in_specs, out_specs=host.out_specs, out_shape=host.out_shape,
        scratch_shapes=host.scratch, input_output_aliases=host.aliases,
        compiler_params=_cp("arbitrary", "arbitrary"))(p, p, p, *host.args)
    (out, ltot), extra = host.results(outs)
    return out, ltot, extra


def _sba_bwd(p, ltot, dout, sbw, name, comm=None):
    s = p.shape[0]
    nh = sbw // HEAD_DIM
    hp = _heads_per_step(nh)
    ngrp, hw = nh // hp, hp * HEAD_DIM
    blk = ATT_BLOCK
    nq = s // blk
    scale = 1.0 / math.sqrt(HEAD_DIM)
    blk_spec = pl.BlockSpec((blk, hw), lambda g, i: (i, g))
    col_spec = pl.BlockSpec((s, hw), lambda g, i: (0, g))
    host = _Host(comm,
                 [blk_spec, pl.BlockSpec((s, hw), lambda g, i: (0, ngrp + g)),
                  pl.BlockSpec((s, hw), lambda g, i: (0, 2 * ngrp + g)), blk_spec, blk_spec],
                 [blk_spec, col_spec, col_spec], [jax.ShapeDtypeStruct((s, sbw), BF16)] * 3,
                 [pltpu.VMEM((s, hw), BF16)] * 2 + [pltpu.VMEM((s, hw), F32)] * 2)

    def body(*refs):
        (q_ref, k_ref, v_ref, lt_ref, do_ref), (dq_ref, dk_ref, dv_ref), (kb_ref, vb_ref, dka_ref, dva_ref) = host.split(refs)
        i = pl.program_id(1)
        step = pl.program_id(0) * nq + i
        host.before(step, ngrp * nq)

        @pl.when(i == 0)
        def _():
            kb_ref[...] = k_ref[...].astype(BF16)
            vb_ref[...] = v_ref[...].astype(BF16)
            dka_ref[...] = jnp.zeros_like(dka_ref)
            dva_ref[...] = jnp.zeros_like(dva_ref)

        heads = [slice(h * HEAD_DIM, (h + 1) * HEAD_DIM) for h in range(hp)]
        qs = [q_ref[:, hd].astype(BF16) for hd in heads]
        dos = [do_ref[:, hd].astype(BF16) for hd in heads]
        ltots = [lt_ref[:, h * HEAD_DIM:h * HEAD_DIM + 1] for h in range(hp)]
        upto = _tri(blk, lambda r, c: r <= c)
        before = _tri(blk, lambda r, c: r < c)
        causal = lax.broadcasted_iota(jnp.int32, (blk, blk), 1) < lax.broadcasted_iota(jnp.int32, (blk, blk), 0)

        def key_block(j, carry, diagonal):
            rows = pl.ds(pl.multiple_of(j * blk, blk), blk)
            hs = range(hp)
            kj = [kb_ref[rows, heads[h]] for h in hs]
            vj = [vb_ref[rows, heads[h]] for h in hs]
            z = [_nt(qs[h], kj[h]) * scale for h in hs]
            dw = [_nt(dos[h], vj[h]) for h in hs]
            ls = [_log_sigmoid(z[h]) for h in hs]
            lm = [jnp.where(causal, ls[h] - z[h], 0.0) if diagonal else ls[h] - z[h] for h in hs]
            stay = [ltots[h] - carry[h][1] - _dot_split(lm[h], upto) for h in hs]
            w = [jnp.exp(ls[h] + stay[h]) for h in hs]
            if diagonal:
                w = [jnp.where(causal, w[h], 0.0) for h in hs]
            da = [dw[h] * w[h] for h in hs]
            sig = [jnp.exp(ls[h]) for h in hs]
            chain = [sig[h] * (carry[h][2] + _dot_split(da[h], before)) for h in hs]
            if diagonal:
                chain = [jnp.where(causal, chain[h], 0.0) for h in hs]
            dzb = [((da[h] * (1.0 - sig[h]) - chain[h]) * scale).astype(BF16) for h in hs]
            dq = [carry[h][0] + jnp.dot(dzb[h], kj[h], preferred_element_type=F32) for h in hs]
            for h in hs:
                dka_ref[rows, heads[h]] += _tn(dzb[h], qs[h])
            for h in hs:
                dva_ref[rows, heads[h]] += _tn(w[h].astype(BF16), dos[h])
            return tuple((dq[h], carry[h][1] + jnp.sum(lm[h], axis=1, keepdims=True),
                          carry[h][2] + jnp.sum(da[h], axis=1, keepdims=True)) for h in hs)

        zero = jnp.zeros((blk, 1), F32)
        init = tuple((jnp.zeros((blk, HEAD_DIM), F32), zero, zero) for _ in heads)
        carry = lax.fori_loop(0, i, lambda j, c: key_block(j, c, False), init)
        carry = key_block(i, carry, True)
        for h, hd in enumerate(heads):
            dq_ref[:, hd] = carry[h][0].astype(BF16)

        @pl.when(i == nq - 1)
        def _():
            dk_ref[...] = dka_ref[...].astype(BF16)
            dv_ref[...] = dva_ref[...].astype(BF16)

        host.after(step, ngrp * nq)

    outs = pl.pallas_call(
        body, name=name, grid=(ngrp, nq), in_specs=host.in_specs, out_specs=host.out_specs, out_shape=host.out_shape,
        scratch_shapes=host.scratch, input_output_aliases=host.aliases,
        compiler_params=_cp("arbitrary", "arbitrary"))(p, p, p, ltot, dout, *host.args)
    (dq, dk, dv), extra = host.results(outs)
    return dq, dk, dv, extra


def _pool_groups(pad_ref, tile, row0, gd, halo):
    row = row0 + lax.broadcasted_iota(jnp.int32, (tile, 1), 0)
    out = []
    for gi, win in enumerate(POOL_WINDOWS):
        cs = slice(gi * gd, (gi + 1) * gd)
        tok = pad_ref[halo:halo + tile, cs]
        acc = tok
        for j in range(1, win):
            acc = acc + pad_ref[halo - j:halo - j + tile, cs]
        cnt = jnp.minimum(win, row + 1).astype(F32)
        out.append(acc / cnt - tok)
    return out


def _even_mix_fwd(p, att, pool_w, pool_scale, d, name):
    s = p.shape[0]
    half = d // 2
    gd = half // len(POOL_WINDOWS)
    t, hb = ROW_TILE, POOL_HALO

    def body(u_ref, uh_ref, g_ref, a_ref, pw_ref, sc_ref, y_ref, pad_ref):
        i = pl.program_id(0)
        pad_ref[0:hb, :] = jnp.where(i > 0, uh_ref[...], 0.0)
        pad_ref[hb:, :] = u_ref[...]
        pooled = _pool_groups(pad_ref, t, i * t, gd, hb)
        for gi in range(len(POOL_WINDOWS)):
            cs = slice(gi * gd, (gi + 1) * gd)
            po = jnp.dot(pooled[gi].astype(BF16), pw_ref[gi], preferred_element_type=F32) * sc_ref[:, cs]
            y_ref[:, half + gi * gd:half + (gi + 1) * gd] = (po * _silu(g_ref[:, half + gi * gd:half + (gi + 1) * gd])).astype(BF16)
        y_ref[:, :half] = (a_ref[...] * _silu(g_ref[:, :half])).astype(BF16)

    return pl.pallas_call(
        body, name=name, grid=(s // t,),
        in_specs=[pl.BlockSpec((t, half), lambda i: (i, 3)),
                  pl.BlockSpec((hb, half), lambda i: (jnp.maximum(i * (t // hb) - 1, 0), 3)),
                  pl.BlockSpec((t, d), lambda i: (i, 2)),
                  pl.BlockSpec((t, half), lambda i: (i, 0)),
                  pl.BlockSpec(pool_w.shape, lambda i: (0, 0, 0)),
                  pl.BlockSpec((1, half), lambda i: (0, 0))],
        out_specs=pl.BlockSpec((t, d), lambda i: (i, 0)),
        out_shape=jax.ShapeDtypeStruct((s, d), BF16),
        scratch_shapes=[pltpu.VMEM((hb + t, half), F32)],
        compiler_params=_cp("parallel"))(p, p, p, att, pool_w, pool_scale)


def _even_mix_bwd(p, att, dy, pool_w, pool_scale, d, name, comm=None):
    s = p.shape[0]
    half = d // 2
    ng = len(POOL_WINDOWS)
    gd = half // ng
    t, hb = ROW_TILE, POOL_HALO
    nt = s // t
    host = _Host(
        comm,
        [pl.BlockSpec((t, half), lambda i: (i, 3)),
         pl.BlockSpec((hb, half), lambda i: (jnp.maximum(i * (t // hb) - 1, 0), 3)),
         pl.BlockSpec((t, d), lambda i: (i, 2)),
         pl.BlockSpec((hb, half), lambda i: (jnp.minimum((i + 1) * (t // hb), s // hb - 1), 5)),
         pl.BlockSpec((t, half), lambda i: (i, 0)),
         pl.BlockSpec((t, d), lambda i: (i, 0)),
         pl.BlockSpec((hb, half), lambda i: (jnp.minimum((i + 1) * (t // hb), s // hb - 1), 1)),
         pl.BlockSpec(pool_w.shape, lambda i: (0, 0, 0)),
         pl.BlockSpec((1, half), lambda i: (0, 0))],
        [pl.BlockSpec((t, half), lambda i: (i, 0)),
         pl.BlockSpec((t, half), lambda i: (i, 0)),
         pl.BlockSpec((t, d), lambda i: (i, 0)),
         pl.BlockSpec((1, half), lambda i: (0, 0)),
         pl.BlockSpec((ng, gd, gd), lambda i: (0, 0, 0))],
        [jax.ShapeDtypeStruct((s, half), F32), jax.ShapeDtypeStruct((s, half), BF16),
         jax.ShapeDtypeStruct((s, d), BF16), jax.ShapeDtypeStruct((1, half), F32),
         jax.ShapeDtypeStruct((ng, gd, gd), F32)],
        [pltpu.VMEM((hb + t, half), F32), pltpu.VMEM((t + hb, half), F32)])

    def body(*refs):
        ((u_ref, uh_ref, g_ref, gh_ref, a_ref, dy_ref, dyh_ref, pw_ref, sc_ref),
         (da_ref, du_ref, dg_ref, dsc_ref, dpw_ref), (pad_ref, dn_ref)) = host.split(refs)
        i = pl.program_id(0)
        host.before(i, nt)
        first = i == 0
        pad_ref[0:hb, :] = jnp.where(i > 0, uh_ref[...], 0.0)
        pad_ref[hb:, :] = u_ref[...]
        pooled = _pool_groups(pad_ref, t, i * t, gd, hb)
        g1 = g_ref[:, :half]
        dy1 = dy_ref[:, :half]
        da_ref[...] = dy1 * _silu(g1)
        dg_ref[:, :half] = (dy1 * a_ref[...] * _dsilu(g1)).astype(BF16)
        row = i * t + lax.broadcasted_iota(jnp.int32, (t + hb, 1), 0)
        for gi, win in enumerate(POOL_WINDOWS):
            cs = slice(gi * gd, (gi + 1) * gd)
            cs2 = slice(half + gi * gd, half + (gi + 1) * gd)
            w = pw_ref[gi]
            pb = pooled[gi].astype(BF16)
            zp = jnp.dot(pb, w, preferred_element_type=F32)
            g2 = g_ref[:, cs2]
            dy2 = dy_ref[:, cs2]
            dg_ref[:, cs2] = (dy2 * zp * sc_ref[:, cs] * _dsilu(g2)).astype(BF16)
            dpo = dy2 * _silu(g2)
            _acc_rows(dsc_ref.at[:, cs], first, jnp.sum(dpo * zp, axis=0, keepdims=True))
            dz = (dpo * sc_ref[:, cs]).astype(BF16)
            _acc_rows(dpw_ref.at[gi], first, _tn(pb, dz))
            dzh = jnp.where(i < nt - 1, dyh_ref[:, cs] * _silu(gh_ref[:, cs]) * sc_ref[:, cs], 0.0).astype(BF16)
            dpool = _nt(dz, w)
            dpool_h = _nt(dzh, w)
            cnt = jnp.minimum(win, row + 1).astype(F32)
            dn_ref[0:t, cs] = dpool / cnt[0:t]
            dn_ref[t:, cs] = dpool_h / cnt[t:]
            acc = dn_ref[0:t, cs]
            for j in range(1, win):
                acc = acc + dn_ref[j:j + t, cs]
            du_ref[:, cs] = (acc - dpool).astype(BF16)
        host.after(i, nt)

    outs = pl.pallas_call(
        body, name=name, grid=(nt,), in_specs=host.in_specs, out_specs=host.out_specs, out_shape=host.out_shape,
        scratch_shapes=host.scratch, input_output_aliases=host.aliases,
        compiler_params=_cp("arbitrary"))(p, p, p, p, att, dy, dy, pool_w, pool_scale, *host.args)
    return host.results(outs)


def _mm_out_even(y, w, x, g_post, g_pre_next, name):
    s, k = y.shape
    d = w.shape[1]
    t = ROW_TILE

    def body(y_ref, w_ref, x_ref, gp_ref, gn_ref, o_ref, x1_ref, h1_ref):
        o = jnp.dot(y_ref[...], w_ref[...], preferred_element_type=F32)
        o_ref[...] = o
        ohat, _ = _rms_stats(o)
        x1 = x_ref[...] + ohat * gp_ref[...]
        x1_ref[...] = x1
        xhat, _ = _rms_stats(x1)
        h1_ref[...] = (xhat * gn_ref[...]).astype(BF16)

    row = lambda c: pl.BlockSpec((t, c), lambda i: (i, 0))
    vec = pl.BlockSpec((1, d), lambda i: (0, 0))
    return pl.pallas_call(
        body, name=name, grid=(s // t,),
        in_specs=[row(k), pl.BlockSpec((k, d), lambda i: (0, 0)), row(d), vec, vec],
        out_specs=[row(d), row(d), row(d)],
        out_shape=[jax.ShapeDtypeStruct((s, d), F32), jax.ShapeDtypeStruct((s, d), F32),
                   jax.ShapeDtypeStruct((s, d), BF16)],
        compiler_params=_cp("parallel"))(y, w, x, g_post, g_pre_next)


def _mm_out_odd(y, w, x1, g_post, target, name):
    s, k = y.shape
    d = w.shape[1]
    t = ROW_TILE

    def body(y_ref, w_ref, x_ref, gp_ref, tg_ref, do_ref, dx_ref, loss_ref, dgp_ref):
        first = pl.program_id(0) == 0
        o = jnp.dot(y_ref[...], w_ref[...], preferred_element_type=F32)
        ohat, r = _rms_stats(o)
        gp = gp_ref[...]
        diff = x_ref[...] + ohat * gp - tg_ref[...]
        part = 0.5 * jnp.sum(jnp.mean(diff * diff, axis=-1, keepdims=True), axis=0, keepdims=True)
        _acc_rows(loss_ref, first, jnp.broadcast_to(part, loss_ref.shape))
        dx2 = diff * (1.0 / d)
        dx_ref[...] = dx2
        do, dgp = _rms_bwd(dx2, ohat, r, gp)
        do_ref[...] = do.astype(BF16)
        _acc_rows(dgp_ref, first, dgp)

    row = lambda c: pl.BlockSpec((t, c), lambda i: (i, 0))
    vec = pl.BlockSpec((1, d), lambda i: (0, 0))
    return pl.pallas_call(
        body, name=name, grid=(s // t,),
        in_specs=[row(k), pl.BlockSpec((k, d), lambda i: (0, 0)), row(d), vec, row(d)],
        out_specs=[row(d), row(d), pl.BlockSpec((8, LANES), lambda i: (0, 0)), vec],
        out_shape=[jax.ShapeDtypeStruct((s, d), BF16), jax.ShapeDtypeStruct((s, d), F32),
                   jax.ShapeDtypeStruct((8, LANES), F32), jax.ShapeDtypeStruct((1, d), F32)],
        compiler_params=_cp("arbitrary"))(y, w, x1, g_post, target)


def _layer_norm(d1, cg, cb):
    mu = jnp.mean(d1, axis=-1, keepdims=True)
    cen = d1 - mu
    rstd = lax.rsqrt(jnp.mean(cen * cen, axis=-1, keepdims=True) + EPS)
    n = cen * rstd
    return n, rstd, n * cg + cb


SUBLANES = 8
GATHER_PIECES = 4
CONV_ROWS = 64


def _make_shifts(pad_ref, cs, sh_ref):
    rows = sh_ref.shape[1]
    for r in range(1, SUBLANES):
        sh_ref[r - 1] = pad_ref[r:r + rows, cs]


def _by_shift(taps, base, sign=1):
    return sorted(range(taps), key=lambda k: ((sign * (base + k)) % SUBLANES, k))


def _window(pad_ref, cs, sh_ref, off, t):
    m, r = divmod(off, SUBLANES)
    if r == 0:
        return pad_ref[SUBLANES * m:SUBLANES * m + t, cs]
    return sh_ref[r - 1, SUBLANES * m:SUBLANES * m + t, :]


def _odd_mix_fwd(p, sconv_w, dconv_w, dconv_b, cnorm_g, cnorm_b, d, name):
    s = p.shape[0]
    w = d // 2
    k3, k31 = sconv_w.shape[0], dconv_w.shape[0]
    t, hb = ROW_TILE, CONV_HALO
    assert hb >= k31 - 1 and w % LANES == 0

    def body(p_ref, ph_ref, w3_ref, w31_ref, b31_ref, cg_ref, cb_ref, y_ref, s3_ref, d1_ref, mpad, dpad, sh_ref):
        i = pl.program_id(0)
        mpad[0:hb, :] = jnp.where(i > 0, ph_ref[:, 2 * w:3 * w] * ph_ref[:, 0:w], 0.0)
        mpad[hb:, :] = p_ref[:, 2 * w:3 * w] * p_ref[:, 0:w]
        dpad[0:hb, :] = jnp.where(i > 0, ph_ref[:, 3 * w:4 * w] * _sigmoid(ph_ref[:, 4 * w:5 * w]), 0.0)
        dpad[hb:, :] = p_ref[:, 3 * w:4 * w] * _sigmoid(p_ref[:, 4 * w:5 * w])
        for c0 in range(0, w, LANES):
            cs = slice(c0, c0 + LANES)
            acc = jnp.zeros((t, LANES), F32)
            for kk in range(k3):
                acc = acc + w3_ref[kk:kk + 1, cs] * mpad[hb - (k3 - 1) + kk:hb - (k3 - 1) + kk + t, cs]
            s3_ref[:, cs] = acc
            _make_shifts(dpad, cs, sh_ref)
            for r0 in range(0, t, CONV_ROWS):
                acc = jnp.zeros((CONV_ROWS, LANES), F32)
                for kk in _by_shift(k31, hb - (k31 - 1)):
                    acc = acc + w31_ref[kk:kk + 1, cs] * _window(dpad, cs, sh_ref, hb - (k31 - 1) + kk + r0, CONV_ROWS)
                d1_ref[r0:r0 + CONV_ROWS, cs] = acc + b31_ref[:, cs]
        _, _, d2 = _layer_norm(d1_ref[...], cg_ref[...], cb_ref[...])
        y_ref[:, :w] = (p_ref[:, w:2 * w] * s3_ref[...] * _silu(p_ref[:, 5 * w:6 * w])).astype(BF16)
        y_ref[:, w:] = (_silu(d2) * _silu(p_ref[:, 6 * w:7 * w])).astype(BF16)

    row = lambda c: pl.BlockSpec((t, c), lambda i: (i, 0))
    full = lambda a: pl.BlockSpec(a.shape, lambda i: (0, 0))
    return pl.pallas_call(
        body, name=name, grid=(s // t,),
        in_specs=[row(7 * w),
                  pl.BlockSpec((hb, 5 * w), lambda i: (jnp.maximum(i * (t // hb) - 1, 0), 0)),
                  full(sconv_w), full(dconv_w), full(dconv_b), full(cnorm_g), full(cnorm_b)],
        out_specs=[row(d), row(w), row(w)],
        out_shape=[jax.ShapeDtypeStruct((s, d), BF16), jax.ShapeDtypeStruct((s, w), F32),
                   jax.ShapeDtypeStruct((s, w), F32)],
        scratch_shapes=[pltpu.VMEM((hb + t, w), F32)] * 2 + [pltpu.VMEM((SUBLANES - 1, hb + t - SUBLANES, LANES), F32)],
        compiler_params=_cp("parallel"))(p, p, sconv_w, dconv_w, dconv_b, cnorm_g, cnorm_b)


def _odd_bwd_rows(p, s3, d1, dy, cnorm_g, cnorm_b, d, name, comm=None):
    s = p.shape[0]
    w = d // 2
    t = ROW_TILE
    col = lambda j: pl.BlockSpec((t, w), lambda i: (i, j))
    row = lambda c: pl.BlockSpec((t, c), lambda i: (i, 0))
    vec = pl.BlockSpec((1, w), lambda i: (0, 0))
    host = _Host(comm, [col(1), col(5), col(6), row(w), row(w), row(d), vec, vec],
                 [row(w), row(d), row(w), row(w), vec, vec, vec],
                 [jax.ShapeDtypeStruct((s, w), BF16), jax.ShapeDtypeStruct((s, d), BF16),
                  jax.ShapeDtypeStruct((s, w), F32), jax.ShapeDtypeStruct((s, w), F32)] + [jax.ShapeDtypeStruct((1, w), F32)] * 3, [])

    def body(*refs):
        ((bc_ref, g1_ref, g2_ref, s3_ref, d1_ref, dy_ref, cg_ref, cb_ref),
         (dbc_ref, dg_ref, ds3_ref, dd1_ref, dcg_ref, dcb_ref, db_ref), _) = host.split(refs)
        step = pl.program_id(0)
        host.before(step, s // t)
        first = step == 0
        g1, g2 = g1_ref[...], g2_ref[...]
        bc, s3v = bc_ref[...], s3_ref[...]
        dy1, dy2 = dy_ref[:, :w], dy_ref[:, w:]
        n, rstd, d2 = _layer_norm(d1_ref[...], cg_ref[...], cb_ref[...])
        dg_ref[:, :w] = (dy1 * bc * s3v * _dsilu(g1)).astype(BF16)
        dg_ref[:, w:] = (dy2 * _silu(d2) * _dsilu(g2)).astype(BF16)
        dco = dy1 * _silu(g1)
        dbc_ref[...] = (dco * s3v).astype(BF16)
        ds3_ref[...] = dco * bc
        dd2 = dy2 * _silu(g2) * _dsilu(d2)
        _acc_rows(dcb_ref, first, jnp.sum(dd2, axis=0, keepdims=True))
        _acc_rows(dcg_ref, first, jnp.sum(dd2 * n, axis=0, keepdims=True))
        dn = dd2 * cg_ref[...]
        dd1 = rstd * (dn - jnp.mean(dn, axis=-1, keepdims=True) - n * jnp.mean(dn * n, axis=-1, keepdims=True))
        dd1_ref[...] = dd1
        _acc_rows(db_ref, first, jnp.sum(dd1, axis=0, keepdims=True))
        host.after(step, s // t)

    outs = pl.pallas_call(
        body, name=name, grid=(s // t,), in_specs=host.in_specs, out_specs=host.out_specs, out_shape=host.out_shape,
        scratch_shapes=host.scratch, input_output_aliases=host.aliases,
        compiler_params=_cp("arbitrary"))(p, p, p, s3, d1, dy, cnorm_g, cnorm_b, *host.args)
    return host.results(outs)


def _odd_bwd_conv(p, ds3, dd1, sconv_w, dconv_w, d, name):
    s = p.shape[0]
    w = d // 2
    k3, k31 = sconv_w.shape[0], dconv_w.shape[0]
    t, hb, ha = ROW_TILE, CONV_HALO, 8
    nt = s // t
    assert hb >= k31 - 1 and ha >= k3 - 1

    def body(hc_ref, cc_ref, ga_ref, gb_ref, hch_ref, cch_ref, gah_ref, gbh_ref, ds3_ref, ds3h_ref, dd1_ref, dd1h_ref,
             w3_ref, w31_ref, dhc_ref, dcc_ref, dga_ref, dgb_ref, dw3_ref, dw31_ref, mpad, dpad, s3pad, d1pad, sh_ref):
        i = pl.program_id(0)
        first = i == 0
        last = i == nt - 1
        mpad[0:hb, :] = jnp.where(i > 0, cch_ref[...] * hch_ref[...], 0.0)
        mpad[hb:, :] = cc_ref[...] * hc_ref[...]
        dpad[0:hb, :] = jnp.where(i > 0, gah_ref[...] * _sigmoid(gbh_ref[...]), 0.0)
        dpad[hb:, :] = ga_ref[...] * _sigmoid(gb_ref[...])
        s3pad[0:t, :] = ds3_ref[...]
        s3pad[t:, :] = jnp.where(last, 0.0, ds3h_ref[...])
        d1pad[0:t, :] = dd1_ref[...]
        d1pad[t:, :] = jnp.where(last, 0.0, dd1h_ref[...])

        @pl.when(first)
        def _():
            dw3_ref[...] = jnp.zeros_like(dw3_ref)
            dw31_ref[...] = jnp.zeros_like(dw31_ref)

        def fold(v):
            return jnp.sum(v.reshape(v.shape[0] // SUBLANES, SUBLANES, LANES), axis=0)

        groups = range(0, t, CONV_ROWS)
        for c0 in range(0, w, LANES):
            cs = slice(c0, c0 + LANES)
            ds3v = s3pad[0:t, cs]
            dm = jnp.zeros((t, LANES), F32)
            for kk in range(k3):
                dm = dm + w3_ref[kk:kk + 1, cs] * s3pad[k3 - 1 - kk:k3 - 1 - kk + t, cs]
                off = hb - (k3 - 1) + kk
                dw3_ref[SUBLANES * kk:SUBLANES * (kk + 1), cs] += fold(ds3v * mpad[off:off + t, cs])
            dcc_ref[:, cs] = (dm * hc_ref[:, cs]).astype(BF16)
            dhc_ref[:, cs] = (dm * cc_ref[:, cs]).astype(BF16)
            _make_shifts(d1pad, cs, sh_ref)
            for r0 in groups:
                rows = slice(r0, r0 + CONV_ROWS)
                dd0 = jnp.zeros((CONV_ROWS, LANES), F32)
                for kk in _by_shift(k31, -(k31 - 1), -1):
                    dd0 = dd0 + w31_ref[kk:kk + 1, cs] * _window(d1pad, cs, sh_ref, k31 - 1 - kk + r0, CONV_ROWS)
                sgb = _sigmoid(gb_ref[rows, cs])
                dga_ref[rows, cs] = (dd0 * sgb).astype(BF16)
                dgb_ref[rows, cs] = (dd0 * ga_ref[rows, cs] * sgb * (1.0 - sgb)).astype(BF16)
            _make_shifts(dpad, cs, sh_ref)
            for kk in _by_shift(k31, hb - (k31 - 1)):
                part = jnp.zeros((SUBLANES, LANES), F32)
                for r0 in groups:
                    part = part + fold(d1pad[r0:r0 + CONV_ROWS, cs]
                                       * _window(dpad, cs, sh_ref, hb - (k31 - 1) + kk + r0, CONV_ROWS))
                dw31_ref[SUBLANES * kk:SUBLANES * (kk + 1), cs] += part

    col = lambda j: pl.BlockSpec((t, w), lambda i: (i, j))
    pre = lambda j: pl.BlockSpec((hb, w), lambda i: (jnp.maximum(i * (t // hb) - 1, 0), j))
    row = pl.BlockSpec((t, w), lambda i: (i, 0))
    post = lambda h: pl.BlockSpec((h, w), lambda i: (jnp.minimum((i + 1) * (t // h), s // h - 1), 0))
    full = lambda a: pl.BlockSpec(a.shape, lambda i: (0, 0))
    dhc, dcc, dga, dgb, dw3, dw31 = pl.pallas_call(
        body, name=name, grid=(nt,),
        in_specs=[col(0), col(2), col(3), col(4), pre(0), pre(2), pre(3), pre(4),
                  row, post(ha), row, post(hb), full(sconv_w), full(dconv_w)],
        out_specs=[row, row, row, row, pl.BlockSpec((SUBLANES * k3, w), lambda i: (0, 0)),
                   pl.BlockSpec((SUBLANES * k31, w), lambda i: (0, 0))],
        out_shape=[jax.ShapeDtypeStruct((s, w), BF16)] * 4
        + [jax.ShapeDtypeStruct((SUBLANES * k3, w), F32), jax.ShapeDtypeStruct((SUBLANES * k31, w), F32)],
        scratch_shapes=[pltpu.VMEM((hb + t, w), F32)] * 2 + [pltpu.VMEM((t + ha, w), F32), pltpu.VMEM((t + hb, w), F32),
                                                             pltpu.VMEM((SUBLANES - 1, hb + t - SUBLANES, LANES), F32)],
        compiler_params=_cp("arbitrary"))(p, p, p, p, p, p, p, p, ds3, ds3, dd1, dd1, sconv_w, dconv_w)
    return dhc, dcc, dga, dgb, jnp.sum(dw3.reshape(k3, SUBLANES, w), axis=1), jnp.sum(dw31.reshape(k31, SUBLANES, w), axis=1)


def _mm_in_bwd(dp, w3, x, g_pre, dres, post, name, comm=None):
    s = dp.shape[0]
    nsh, d, ns = w3.shape
    t = 512 if s % 512 == 0 else ROW_TILE
    nt = s // t
    ks = 2 if (ns // 2) % LANES == 0 else 1
    nk, kw = nsh * ks, ns // ks
    chunk = 128
    nchunk = t // chunk
    row = pl.BlockSpec((t, d), lambda i, k: (i, 0))
    vec = pl.BlockSpec((1, d), lambda i, k: (0, 0))
    rowwise = [x, dres] + ([post[0]] if post is not None else [])
    in_specs = [pl.BlockSpec((t, kw), lambda i, k: (i, k)), pl.BlockSpec((None, d, kw), lambda i, k: (k // ks, 0, k % ks)), vec]
    out_specs = [row, vec]
    out_shape = [jax.ShapeDtypeStruct((s, d), F32), jax.ShapeDtypeStruct((1, d), F32)]
    args = [dp, w3, g_pre]
    if post is not None:
        in_specs += [vec]
        out_specs += [row, vec]
        out_shape += [jax.ShapeDtypeStruct((s, d), BF16), jax.ShapeDtypeStruct((1, d), F32)]
        args += [post[1]]
    n_blocked = len(in_specs)
    in_specs += [ANY] * len(rowwise)
    args += rowwise
    host = _Host(comm, in_specs, out_specs, out_shape,
                 [pltpu.VMEM((t, d), F32), pltpu.VMEM((len(rowwise), 2, chunk, d), F32), pltpu.SemaphoreType.DMA((len(rowwise), 2))])

    def body(*refs):
        ins, outs, (acc_ref, buf_ref, sem_ref) = host.split(refs)
        dp_ref, w_ref, g_ref = ins[:3]
        hbm = ins[n_blocked:]
        dx_ref, dg_ref = outs[:2]
        tile = pl.program_id(0)
        kk = pl.program_id(1)
        first = tile == 0
        step = tile * nk + kk
        host.before(step, nt * nk)
        part = _nt(dp_ref[...], w_ref[...])

        @pl.when(kk == 0)
        def _():
            acc_ref[...] = part

        @pl.when(kk > 0)
        def _():
            acc_ref[...] += part

        def fetch(ci, slot):
            return [pltpu.make_async_copy(src.at[pl.ds(tile * t + ci * chunk, chunk)], buf_ref.at[n, slot], sem_ref.at[n, slot])
                    for n, src in enumerate(hbm)]

        @pl.when(kk == nk - 1)
        def _():
            dg = dgp = None
            for cp in fetch(0, 0):
                cp.start()
            for ci in range(nchunk):
                slot = ci % 2
                if ci + 1 < nchunk:
                    for cp in fetch(ci + 1, 1 - slot):
                        cp.start()
                for cp in fetch(ci, slot):
                    cp.wait()
                rows = slice(ci * chunk, (ci + 1) * chunk)
                xhat, r = _rms_stats(buf_ref[0, slot])
                dxn, dg_part = _rms_bwd(acc_ref[rows, :], xhat, r, g_ref[...])
                dx = buf_ref[1, slot] + dxn
                dx_ref[rows, :] = dx
                dg = dg_part if dg is None else dg + dg_part
                if post is not None:
                    ohat, ro = _rms_stats(buf_ref[2, slot])
                    do, dgp_part = _rms_bwd(dx, ohat, ro, ins[3][...])
                    outs[2][rows, :] = do.astype(BF16)
                    dgp = dgp_part if dgp is None else dgp + dgp_part
            _acc_rows(dg_ref, first, dg)
            if post is not None:
                _acc_rows(outs[3], first, dgp)

        host.after(step, nt * nk)

    res = pl.pallas_call(
        body, name=name, grid=(nt, nk), in_specs=host.in_specs, out_specs=host.out_specs, out_shape=host.out_shape,
        scratch_shapes=host.scratch, input_output_aliases=host.aliases,
        compiler_params=_cp("arbitrary", "arbitrary"))(*args, *host.args)
    return host.results(res)


def _half_add(g, r1, c_arr, name):
    nsh, rows, ns = g.shape
    h = rows // 2
    tr = min(ROW_TILE, h)
    per = h // tr

    def body(c_ref, g_ref, r_ref, o_ref):
        o_ref[...] = (g_ref[...].astype(F32) + r_ref[...].astype(F32)).astype(BF16)

    spec = pl.BlockSpec((None, tr, ns), lambda s, r, c: (s, r, 0))
    return pl.pallas_call(
        body, name=name,
        grid_spec=pltpu.PrefetchScalarGridSpec(
            num_scalar_prefetch=1, grid=(nsh, per),
            in_specs=[pl.BlockSpec((None, tr, ns), lambda s, r, c: (s, c[0] * per + r, 0)), spec], out_specs=spec),
        out_shape=jax.ShapeDtypeStruct((nsh, h, ns), BF16), compiler_params=_cp("parallel", "parallel"))(c_arr, g, r1)


def _sum_chips(hh, r2, mc_arr, name):
    _, h, ns = hh.shape
    tr = min(ROW_TILE, h)
    per = h // tr

    def body(mc_ref, h_ref, a_ref, b_ref, c_ref, o_ref):
        o_ref[...] = ((h_ref[...].astype(F32) + a_ref[...].astype(F32)) + b_ref[...].astype(F32)) + c_ref[...].astype(F32)

    got = lambda k: pl.BlockSpec((None, tr, ns), lambda r, mc: (k, r, 0))
    return pl.pallas_call(
        body, name=name,
        grid_spec=pltpu.PrefetchScalarGridSpec(
            num_scalar_prefetch=1, grid=(per,),
            in_specs=[pl.BlockSpec((None, tr, ns), lambda r, mc: (mc[0], r, 0)), got(0), got(1), got(2)],
            out_specs=pl.BlockSpec((tr, ns), lambda r, mc: (mc[1] * per + r, 0))),
        out_shape=jax.ShapeDtypeStruct((2 * h, ns), F32), compiler_params=_cp("parallel"))(mc_arr, hh, r2, r2, r2)


def _add2(a, b, name):
    def body(a_ref, b_ref, o_ref):
        o_ref[...] = a_ref[...] + b_ref[...]

    return pl.pallas_call(body, name=name, out_shape=jax.ShapeDtypeStruct(a.shape, a.dtype), compiler_params=_cp())(a, b)


def _sum_chips_ordered(s2, r2, mc_arr, name):
    rows, w = s2.shape
    rh = rows // 2

    def body(mc_ref, s_ref, a_ref, b_ref, c_ref, o_ref):
        me = mc_ref[0]
        acc = None
        for j in range(N_CHIPS):
            rel = jnp.bitwise_xor(me, j)
            v = jnp.where(rel == 0, s_ref[...], jnp.where(rel == 2, a_ref[...], jnp.where(rel == 1, b_ref[...], c_ref[...])))
            acc = v if acc is None else acc + v
        o_ref[...] = acc

    got = lambda k: pl.BlockSpec((None, rh, w), lambda i, mc: (k, 0, 0))
    return pl.pallas_call(
        body, name=name,
        grid_spec=pltpu.PrefetchScalarGridSpec(
            num_scalar_prefetch=1, grid=(1,),
            in_specs=[pl.BlockSpec((rh, w), lambda i, mc: (mc[1], 0)), got(0), got(1), got(2)],
            out_specs=pl.BlockSpec((rh, w), lambda i, mc: (mc[1], 0))),
        out_shape=jax.ShapeDtypeStruct((rows, w), F32), compiler_params=_cp("arbitrary"))(mc_arr, s2, r2, r2, r2)


def _adamw(w, g, m, v, name, comm=None):
    r, c = w.shape
    tr = ROW_TILE if r % ROW_TILE == 0 else r
    c1 = 1.0 / (1.0 - ADAM_B1 ** ADAM_STEP)
    c2 = 1.0 / (1.0 - ADAM_B2 ** ADAM_STEP)
    spec = pl.BlockSpec((tr, c), lambda i: (i, 0))
    host = _Host(comm, [spec] * 4, [spec] * 4, [jax.ShapeDtypeStruct((r, c), F32)] * 4, [])

    def body(*refs):
        (w_ref, g_ref, m_ref, v_ref), (go_ref, d_ref, nm_ref, nv_ref), _ = host.split(refs)
        step = pl.program_id(0)
        host.before(step, r // tr)
        gv = g_ref[...]
        go_ref[...] = gv
        nm = ADAM_B1 * m_ref[...] + (1.0 - ADAM_B1) * gv
        nv = ADAM_B2 * v_ref[...] + (1.0 - ADAM_B2) * (gv * gv)
        nm_ref[...] = nm
        nv_ref[...] = nv
        d_ref[...] = -ADAM_LR * ((nm * c1) / (jnp.sqrt(nv * c2) + ADAM_EPS) + ADAM_WD * w_ref[...])
        host.after(step, r // tr)

    outs = pl.pallas_call(
        body, name=name, grid=(r // tr,), in_specs=host.in_specs, out_specs=host.out_specs, out_shape=host.out_shape,
        scratch_shapes=host.scratch, input_output_aliases=host.aliases,
        compiler_params=_cp("arbitrary"))(w, g, m, v, *host.args)
    return host.results(outs)


def _gather_weights(bigs, pool_w, pack_w, pack_d, name):
    nb = len(bigs)
    smalls = [pool_w, pack_w, pack_d]
    q, cw, cd = pool_w.shape[1], pack_w.shape[1], pack_d.shape[1]
    pieces = [_GatherPlan(bigs, (j, j + 1, GATHER_PIECES)) for j in range(GATHER_PIECES)]
    for j, piece in enumerate(pieces):
        piece.base = 9 + j * piece.nsems

    def body(*refs):
        srcs, dsts = refs[:nb + 3], refs[nb + 3:2 * (nb + 3)]
        ssem, rsem, lsem = refs[2 * (nb + 3):]
        x, y, c, me, chips, sib = _place()

        def small_dst(n, chip):
            if n == 0:
                return dsts[nb].at[:, pl.ds(chip * q, q), :]
            return dsts[nb + n].at[:, pl.ds(chip * (cw if n == 1 else cd), cw if n == 1 else cd)]

        local = [pltpu.make_async_copy(srcs[nb + n], small_dst(n, me), lsem.at[n]) for n in range(3)]
        for cp in local:
            cp.start()
        sends = []
        for n in range(3):
            for k, chip in enumerate(chips):
                cp = _rcopy(srcs[nb + n], small_dst(n, me), ssem.at[3 * n + k], rsem.at[3 * n + k], (*chip, c))
                cp.start()
                sends.append(cp)
        big = (srcs[:nb], dsts[:nb], ssem, rsem)
        for stage in ("start", "relay", "relay_far", "finish"):
            for piece in pieces:
                getattr(piece, stage)(*big)
        for n in range(3):
            for k, chip in enumerate(chips):
                ref = small_dst(n, 2 * chip[0] + chip[1])
                _rcopy(ref, ref, ssem.at[3 * n + k], rsem.at[3 * n + k], (*chip, c)).wait_recv()
        for cp in sends:
            cp.wait_send()
        for cp in local:
            cp.wait()

    nsem = 9 + sum(piece.nsems for piece in pieces)
    out_shape = [jax.ShapeDtypeStruct(b.shape, b.dtype) for b in bigs]
    out_shape += [jax.ShapeDtypeStruct((pool_w.shape[0], N_CHIPS * q, pool_w.shape[2]), pool_w.dtype),
                  jax.ShapeDtypeStruct((pack_w.shape[0], N_CHIPS * cw), pack_w.dtype),
                  jax.ShapeDtypeStruct((pack_d.shape[0], N_CHIPS * cd), pack_d.dtype)]
    return pl.pallas_call(
        body, name=name, in_specs=[ANY] * (nb + 3), out_specs=[ANY] * (nb + 3), out_shape=out_shape,
        input_output_aliases={a: a for a in range(nb)},
        scratch_shapes=[pltpu.SemaphoreType.DMA((nsem,)), pltpu.SemaphoreType.DMA((nsem,)), pltpu.SemaphoreType.DMA((3,))],
        compiler_params=pltpu.CompilerParams(has_side_effects=True))(*bigs, *smalls)


def _swap_with_sibling(grads, wholes, name):
    n, nw = len(grads), len(wholes)
    halves = [g.shape[1] // 2 for g in grads]

    def body(*refs):
        srcs, dsts = refs[:n + nw], refs[n + nw:2 * (n + nw)]
        ssem, rsem = refs[2 * (n + nw):]
        x, y, c, me, chips, sib = _place()
        cps = [_rcopy(srcs[a].at[:, pl.ds((1 - c) * halves[a], halves[a]), :], dsts[a], ssem.at[a], rsem.at[a], sib)
               for a in range(n)]
        cps += [_rcopy(srcs[a], dsts[a], ssem.at[a], rsem.at[a], sib) for a in range(n, n + nw)]
        for cp in cps:
            cp.start()
        for cp in cps:
            cp.wait_recv()
        for cp in cps:
            cp.wait_send()

    out_shape = [jax.ShapeDtypeStruct((g.shape[0], h, g.shape[2]), g.dtype) for g, h in zip(grads, halves)]
    out_shape += [jax.ShapeDtypeStruct(w.shape, w.dtype) for w in wholes]
    return pl.pallas_call(
        body, name=name, in_specs=[ANY] * (n + nw), out_specs=[ANY] * (n + nw), out_shape=out_shape,
        scratch_shapes=[pltpu.SemaphoreType.DMA((n + nw,)), pltpu.SemaphoreType.DMA((n + nw,))],
        compiler_params=pltpu.CompilerParams(has_side_effects=True))(*grads, *wholes)


def _scatter_to_chips(halves_in, small, name):
    n = len(halves_in)
    rh = small.shape[0] // 2

    def body(*refs):
        srcs, dsts = refs[:n + 1], refs[n + 1:2 * (n + 1)]
        ssem, rsem = refs[2 * (n + 1):]
        x, y, c, me, chips, sib = _place()
        cps = []
        for a in range(n + 1):
            for k, chip in enumerate(chips):
                src = srcs[a].at[2 * chip[0] + chip[1]] if a < n else srcs[a].at[pl.ds(c * rh, rh)]
                cps.append(_rcopy(src, dsts[a].at[k], ssem.at[3 * a + k], rsem.at[3 * a + k], (*chip, c)))
        for cp in cps:
            cp.start()
        for cp in cps:
            cp.wait_recv()
        for cp in cps:
            cp.wait_send()

    out_shape = [jax.ShapeDtypeStruct((3,) + h.shape[1:], h.dtype) for h in halves_in]
    out_shape.append(jax.ShapeDtypeStruct((3, rh, small.shape[1]), small.dtype))
    return pl.pallas_call(
        body, name=name, in_specs=[ANY] * (n + 1), out_specs=[ANY] * (n + 1), out_shape=out_shape,
        scratch_shapes=[pltpu.SemaphoreType.DMA((3 * (n + 1),)), pltpu.SemaphoreType.DMA((3 * (n + 1),))],
        compiler_params=pltpu.CompilerParams(has_side_effects=True))(*halves_in, small)


def _join_halves(parts, name):
    n = len(parts)

    def body(*refs):
        srcs, dsts = refs[:n], refs[n:2 * n]
        ssem, rsem = refs[2 * n:]
        x, y, c, me, chips, sib = _place()
        cps = []
        for a in range(n):
            h = srcs[a].shape[0] // 2
            cps.append(_rcopy(srcs[a].at[pl.ds(c * h, h)], dsts[a].at[pl.ds(c * h, h)], ssem.at[a], rsem.at[a], sib))
        for cp in cps:
            cp.start()
        for a in range(n):
            h = srcs[a].shape[0] // 2
            theirs = dsts[a].at[pl.ds((1 - c) * h, h)]
            _rcopy(theirs, theirs, ssem.at[a], rsem.at[a], sib).wait_recv()
        for cp in cps:
            cp.wait_send()

    out_shape = [jax.ShapeDtypeStruct(p.shape, p.dtype) for p in parts]
    return pl.pallas_call(
        body, name=name, in_specs=[ANY] * n, out_specs=[ANY] * n, out_shape=out_shape,
        input_output_aliases={a: a for a in range(n)},
        scratch_shapes=[pltpu.SemaphoreType.DMA((n,)), pltpu.SemaphoreType.DMA((n,))],
        compiler_params=pltpu.CompilerParams(has_side_effects=True))(*parts)


def _pad_rows(a, rows):
    return jnp.pad(a, ((0, rows - a.shape[0]), (0, 0)))


def _stack_rows(parts, multiple):
    padded = [_pad_rows(p, -(-p.shape[0] // 8) * 8) for p in parts]
    starts, at = [], 0
    for p in padded:
        starts.append(at)
        at += p.shape[0]
    total = -(-at // multiple) * multiple
    if total > at:
        padded.append(jnp.zeros((total - at, parts[0].shape[1]), parts[0].dtype))
    return jnp.concatenate(padded, axis=0), starts


def kernel(x, ln_pre_even, w_in_even, pool_w, pool_scale, w_out_even, ln_post_even, ln_pre_odd, w_in_odd, sconv_w, dconv_w, dconv_b, cnorm_g, cnorm_b, w_out_odd, ln_post_odd, loss_target, m_ln_pre_even, m_w_in_even, m_pool_w, m_pool_scale, m_w_out_even, m_ln_post_even, m_ln_pre_odd, m_w_in_odd, m_sconv_w, m_dconv_w, m_dconv_b, m_cnorm_g, m_cnorm_b, m_w_out_odd, m_ln_post_odd, v_ln_pre_even, v_w_in_even, v_pool_w, v_pool_scale, v_w_out_even, v_ln_post_even, v_ln_pre_odd, v_w_in_odd, v_sconv_w, v_dconv_w, v_dconv_b, v_cnorm_g, v_cnorm_b, v_w_out_odd, v_ln_post_odd):
    _, s, d = x.shape
    half = d // 2
    cw = half // N_CHIPS
    ng, q, gd = pool_w.shape[1:]
    k3, k31 = sconv_w.shape[1], dconv_w.shape[1]
    x2d, tgt = x[0], loss_target[0]
    me = 2 * lax.axis_index("x") + lax.axis_index("y")
    core = lax.axis_index("c")
    c_arr = jnp.reshape(core, (1,)).astype(jnp.int32)
    me_arr = jnp.reshape(me, (1,)).astype(jnp.int32)
    mc_arr = jnp.stack([me, core]).astype(jnp.int32)

    shards = [w_in_even[0], w_out_even[0], w_in_odd[0], w_out_odd[0]]
    slabs = [_cast_bf16_own_slab(w, me_arr, f"cast_w{n}") for n, w in enumerate(shards)]
    pool_w_b = _cast_bf16(pool_w[0].reshape(ng * q, gd), "cast_pool_w").reshape(ng, q, gd)
    pack_w, at_w = _stack_rows([sconv_w[0], dconv_w[0], dconv_b, cnorm_g, cnorm_b], 8)
    pack_d, at_d = _stack_rows([ln_pre_odd, ln_post_odd], 8)
    win_e, pool_w_f, pack_w_f, pack_d_f = _gather_weights(slabs[:1], pool_w_b, pack_w, pack_d, "gather_first")
    sconv_f = pack_w_f[at_w[0]:at_w[0] + k3]
    dconv_f = pack_w_f[at_w[1]:at_w[1] + k31]
    dconv_b_f, cnorm_g_f, cnorm_b_f = (pack_w_f[at_w[n]:at_w[n] + 1] for n in (2, 3, 4))
    ln_pre_odd_f = pack_d_f[at_d[0]:at_d[0] + 1]
    ln_post_odd_f = pack_d_f[at_d[1]:at_d[1] + 1]

    def reduce_half(g, name):
        (got,) = _swap_with_sibling([g], [], "swap_" + name)
        return _half_add(g, got, c_arr, "half_add_" + name)

    h0 = _rms_fwd(x2d, ln_pre_even, "rms_pre_even")
    plans = _Multi([_GatherPlan([slabs[1]], at=(0.6, 0.88)), _GatherPlan([slabs[2]], (0, 1, 4), at=(0.6, 0.88))])
    p_e, extra = _mm_nn(h0, win_e, "proj_in_even", plans)
    (wout_e,), (win_o,) = plans.results(extra)
    wout_e = wout_e.reshape(d, d)
    att, ltot, (win_o,) = _sba_fwd(p_e, half, "sba_fwd", _GatherPlan([win_o], (1, 4, 4), at=(0.69, 0.94)))
    y_e = _even_mix_fwd(p_e, att, pool_w_f, pool_scale, d, "even_mix_fwd")
    o_e, x1, h1 = _mm_out_even(y_e, wout_e, x2d, ln_post_even, ln_pre_odd_f, "proj_out_even")
    p_o, (wout_o,) = _mm_nn(h1, win_o, "proj_in_odd", _GatherPlan([slabs[3]]))
    wout_o = wout_o.reshape(d, d)
    y_o, s3, d1 = _odd_mix_fwd(p_o, sconv_f, dconv_f, dconv_b_f, cnorm_g_f, cnorm_b_f, d, "odd_mix_fwd")
    do_o, dx2, loss_blk, dln_post_odd = _mm_out_odd(y_o, wout_o, x1, ln_post_odd_f, tgt, "proj_out_odd_loss")

    dy_o = _mm_nt(do_o, wout_o, "dy_odd")
    g_wout_o = _mm_tn(y_o, do_o, 1, "dw_out_odd")[0].reshape(N_CHIPS, d // N_CHIPS, d)
    (dbc, dgate_o, ds3, dd1, dcnorm_g, dcnorm_b, ddconv_b), (got,) = _odd_bwd_rows(
        p_o, s3, d1, dy_o, cnorm_g_f, cnorm_b_f, d, "odd_bwd_rows", _SwapPlan([g_wout_o]))
    h_wout_o = _half_add(g_wout_o, got, c_arr, "half_add_out_odd")
    dhc, dcc, dga, dgb, dsconv, ddconv = _odd_bwd_conv(p_o, ds3, dd1, sconv_f, dconv_f, d, "odd_bwd_conv")
    dp_o = jnp.concatenate([dhc, dbc, dcc, dga, dgb, dgate_o], axis=1)
    g_win_o, (s_wout_o,) = _mm_tn(h1, dp_o, N_CHIPS, "dw_in_odd", _ScatterPlan([h_wout_o]))
    (dx1, dln_pre_odd, do_e, dln_post_even), (got,) = _mm_in_bwd(
        dp_o, win_o, x1, ln_pre_odd_f, dx2, (o_e, ln_post_even), "dx_odd", _SwapPlan([g_win_o]))
    h_win_o = _half_add(g_win_o, got, c_arr, "half_add_in_odd")

    dy_e = _mm_nt(do_e, wout_e, "dy_even")
    g_wout_e = _mm_tn(y_e, do_e, 1, "dw_out_even")[0].reshape(N_CHIPS, d // N_CHIPS, d)
    (datt, du, dgate_e, dpool_scale, dpool_w), (got,) = _even_mix_bwd(
        p_e, att, dy_e, pool_w_f, pool_scale, d, "even_mix_bwd", _SwapPlan([g_wout_e]))
    h_wout_e = _half_add(g_wout_e, got, c_arr, "half_add_out_even")
    two = lambda v: v.reshape(2, half)
    small_parts = [dpool_scale, two(dln_post_even), two(dln_pre_odd), two(dln_post_odd),
                   dsconv, ddconv, ddconv_b, dcnorm_g, dcnorm_b, dpool_w.reshape(gd, half)]
    small, at_s = _stack_rows(small_parts, 16)
    (small1,) = _swap_with_sibling([], [small], "swap_small")
    small2 = _add2(small, small1, "small_add")
    plans = _Multi([_ScatterPlan([h_win_o]), _ShareHalfPlan([small2])])
    dq, dk, dv, extra = _sba_bwd(p_e, ltot, datt, half, "sba_bwd", plans)
    (s_win_o,), (small_got,) = plans.results(extra)
    dp_e = jnp.concatenate([dq, dk, dv, du, dgate_e], axis=1)
    g_win_e, (s_wout_e,) = _mm_tn(h0, dp_e, N_CHIPS, "dw_in_even", _ScatterPlan([h_wout_e]))
    h_win_e = reduce_half(g_win_e, "in_even")
    (grad_x, dln_pre_even), (s_win_e,) = _mm_in_bwd(dp_e, win_e, x2d, ln_pre_even, dx1, None, "dx_even", _ScatterPlan([h_win_e]))

    last, at_l = _stack_rows([two(dln_pre_even), jnp.pad(loss_blk[0:1], ((0, 0), (0, half - LANES)))], 16)
    (last1,) = _swap_with_sibling([], [last], "swap_last")
    last2 = _add2(last, last1, "last_add")
    (last_got,) = _scatter_to_chips([], last2, "scatter_last")
    pairs = [(h_win_e, s_win_e), (h_wout_e, s_wout_e), (h_win_o, s_win_o), (h_wout_o, s_wout_o)]
    parts = [_sum_chips(h, r, mc_arr, f"sum_chips{n}") for n, (h, r) in enumerate(pairs)]
    parts.append(_sum_chips_ordered(small2, small_got, mc_arr, "small_sum"))
    parts.append(_sum_chips_ordered(last2, last_got, mc_arr, "last_sum"))
    gw_in_e, gw_out_e, gw_in_o, gw_out_o, red, red_last = _join_halves(parts, "join_halves")
    loss = red_last[at_l[1], 0]

    def rows(n, cnt):
        return red[at_s[n]:at_s[n] + cnt]

    def mine(a, width):
        return lax.dynamic_slice_in_dim(a, me * width, width, axis=1)

    quarter = d // N_CHIPS
    g_small = {
        "ln_pre_even": red_last[at_l[0]:at_l[0] + 2].reshape(1, d),
        "pool_scale": rows(0, 1),
        "ln_post_even": rows(1, 2).reshape(1, d),
        "ln_pre_odd": mine(rows(2, 2).reshape(1, d), quarter),
        "ln_post_odd": mine(rows(3, 2).reshape(1, d), quarter),
        "sconv_w": mine(rows(4, k3), cw),
        "dconv_w": mine(rows(5, k31), cw),
        "dconv_b": mine(rows(6, 1), cw),
        "cnorm_g": mine(rows(7, 1), cw),
        "cnorm_b": mine(rows(8, 1), cw),
        "pool_w": lax.dynamic_slice_in_dim(rows(9, gd).reshape(ng, gd, gd), me * q, q, axis=1).reshape(ng * q, gd),
    }
    w2d = {
        "ln_pre_even": ln_pre_even, "w_in_even": w_in_even[0], "pool_w": pool_w[0].reshape(ng * q, gd),
        "pool_scale": pool_scale, "w_out_even": w_out_even[0], "ln_post_even": ln_post_even, "ln_pre_odd": ln_pre_odd,
        "w_in_odd": w_in_odd[0], "sconv_w": sconv_w[0], "dconv_w": dconv_w[0], "dconv_b": dconv_b, "cnorm_g": cnorm_g,
        "cnorm_b": cnorm_b, "w_out_odd": w_out_odd[0], "ln_post_odd": ln_post_odd,
    }
    moments = {
        "ln_pre_even": (m_ln_pre_even, v_ln_pre_even), "w_in_even": (m_w_in_even, v_w_in_even),
        "pool_w": (m_pool_w, v_pool_w), "pool_scale": (m_pool_scale, v_pool_scale),
        "w_out_even": (m_w_out_even, v_w_out_even), "ln_post_even": (m_ln_post_even, v_ln_post_even),
        "ln_pre_odd": (m_ln_pre_odd, v_ln_pre_odd), "w_in_odd": (m_w_in_odd, v_w_in_odd),
        "sconv_w": (m_sconv_w, v_sconv_w), "dconv_w": (m_dconv_w, v_dconv_w), "dconv_b": (m_dconv_b, v_dconv_b),
        "cnorm_g": (m_cnorm_g, v_cnorm_g), "cnorm_b": (m_cnorm_b, v_cnorm_b),
        "w_out_odd": (m_w_out_odd, v_w_out_odd), "ln_post_odd": (m_ln_post_odd, v_ln_post_odd),
    }
    g2d = dict(g_small, w_in_even=gw_in_e, w_out_even=gw_out_e, w_in_odd=gw_in_o, w_out_odd=gw_out_o)
    updates = {}
    for name, w in w2d.items():
        m_in, v_in = moments[name]
        updates[name], _ = _adamw(w, g2d[name], m_in.reshape(w.shape), v_in.reshape(w.shape), "adamw_" + name)
    outs = [[u.reshape(moments[name][0].shape) for u in updates[name]] for name in w2d]
    grads_out, deltas, new_m, new_v = zip(*outs)
    return (loss, grad_x.reshape(x.shape), *grads_out, *deltas, *new_m, *new_v)
```

```python
import functools
import math

import jax
import jax.numpy as jnp
from jax import lax
from jax.experimental import pallas as pl
from jax.experimental.pallas import tpu as pltpu

F32 = jnp.float32
BF16 = jnp.bfloat16
EPS = 1e-6
N_CHIPS = 4
VMEM_LIMIT_V7X = 56 << 20
HEAD_DIM = 128
ATT_BLOCK = 256
POOL_WINDOWS = (2, 4, 8, 16)
ROW_TILE = 256
POOL_HALO = 16
CONV_HALO = 32
LANES = 128
ADAM_LR, ADAM_B1, ADAM_B2, ADAM_EPS, ADAM_WD, ADAM_STEP = 0.001, 0.9, 0.999, 1e-08, 0.01, 10
MESH_ID = pl.DeviceIdType.MESH
ANY = pl.BlockSpec(memory_space=pl.ANY)


def _cp(*sem):
    return pltpu.CompilerParams(dimension_semantics=sem or None, vmem_limit_bytes=VMEM_LIMIT_V7X)


def _pick_tile(n, cap):
    best = None
    for t in range(LANES, min(n, cap) + 1, LANES):
        if n % t == 0:
            best = t
    assert best is not None, (n, cap)
    return best


def _sigmoid(x):
    return 1.0 / (1.0 + jnp.exp(-x))


def _silu(x):
    return x * _sigmoid(x)


def _dsilu(x):
    s = _sigmoid(x)
    return s * (1.0 + x * (1.0 - s))


def _log_sigmoid(z):
    return jnp.minimum(z, 0.0) - jnp.log(1.0 + jnp.exp(-jnp.abs(z)))


def _rms_stats(x):
    r = lax.rsqrt(jnp.mean(x * x, axis=-1, keepdims=True) + EPS)
    return x * r, r


def _rms_bwd(dh, xhat, r, g):
    dxh = dh * g
    dx = r * (dxh - xhat * jnp.mean(dxh * xhat, axis=-1, keepdims=True))
    return dx, jnp.sum(dh * xhat, axis=0, keepdims=True)


def _acc_rows(ref, first, val):
    @pl.when(first)
    def _():
        ref[...] = val

    @pl.when(jnp.logical_not(first))
    def _():
        ref[...] += val


def _rcopy(src, dst, ssem, rsem, dev):
    return pltpu.make_async_remote_copy(src_ref=src, dst_ref=dst, send_sem=ssem, recv_sem=rsem,
                                        device_id=dev, device_id_type=MESH_ID)


def _place():
    x, y, c = lax.axis_index("x"), lax.axis_index("y"), lax.axis_index("c")
    chips = [(1 - x, y), (x, 1 - y), (1 - x, 1 - y)]
    return x, y, c, 2 * x + y, chips, (x, y, 1 - c)


class _GatherPlan:
    PER_ARRAY = 7

    def __init__(self, arrays, part=(0, 1, 1), at=(0.5, 0.8)):
        self.operands = list(arrays)
        self.out_shapes = [jax.ShapeDtypeStruct(a.shape, a.dtype) for a in arrays]
        self.aliases = {i: i for i in range(len(arrays))}
        self.nsems = self.PER_ARRAY * len(arrays)
        self.base = 0
        self.halves = [a.shape[1] // 2 for a in arrays]
        self.part = part
        self.at = at

    def schedule(self):
        return [(0.0, self.start), (self.at[0], self.relay), (self.at[1], self.relay_far)]

    def _rows(self, ref, a, chip, half, quarter=None):
        lo, hi, n = self.part
        h = self.halves[a]
        first, size = half * h + lo * h // n, (hi - lo) * h // n
        if quarter is not None:
            first, size = first + quarter * (size // 2), size // 2
        return ref.at[chip, pl.ds(first, size)]

    def _copy(self, src, dst, a, n, ssem, rsem, dev):
        return _rcopy(src, dst, ssem.at[self.base + self.PER_ARRAY * a + n], rsem.at[self.base + self.PER_ARRAY * a + n], dev)

    def _own(self, ins, outs, ssem, rsem):
        x, y, c, me, chips, sib = _place()
        return [self._copy(self._rows(ins[a], a, me, c), self._rows(outs[a], a, me, c), a, k, ssem, rsem, (*chips[k], c))
                for a in range(len(ins)) for k in (0, 1)]

    def _relays(self, outs, ssem, rsem, a, k):
        x, y, c, me, chips, sib = _place()
        chip = 2 * chips[k][0] + chips[k][1]
        whole, quarter = self._rows(outs[a], a, chip, c), self._rows(outs[a], a, chip, c, k)
        return (self._copy(whole, whole, a, k, ssem, rsem, (*chips[k], c)),
                self._copy(quarter, quarter, a, 2 + k, ssem, rsem, (*chips[1 - k], c)),
                self._copy(whole, whole, a, 4 + k, ssem, rsem, sib))

    def _far(self, outs, ssem, rsem, a):
        x, y, c, me, chips, sib = _place()
        chip = 2 * chips[2][0] + chips[2][1]
        whole = self._rows(outs[a], a, chip, c)
        got = [self._copy(q, q, a, 2 + k, ssem, rsem, (*chips[1 - k], c))
               for k, q in enumerate([self._rows(outs[a], a, chip, c, 0), self._rows(outs[a], a, chip, c, 1)])]
        return got, self._copy(whole, whole, a, 6, ssem, rsem, sib)

    def start(self, ins, outs, ssem, rsem):
        for cp in self._own(ins, outs, ssem, rsem):
            cp.start()

    def relay(self, ins, outs, ssem, rsem):
        for a in range(len(outs)):
            for k in (0, 1):
                landed, onward, to_sibling = self._relays(outs, ssem, rsem, a, k)
                landed.wait_recv()
                onward.start()
                to_sibling.start()

    def relay_far(self, ins, outs, ssem, rsem):
        for a in range(len(outs)):
            got, to_sibling = self._far(outs, ssem, rsem, a)
            for cp in got:
                cp.wait_recv()
            to_sibling.start()

    def finish(self, ins, outs, ssem, rsem):
        x, y, c, me, chips, sib = _place()
        for a in range(len(outs)):
            for k in range(3):
                ref = self._rows(outs[a], a, 2 * chips[k][0] + chips[k][1], 1 - c)
                self._copy(ref, ref, a, 4 + k, ssem, rsem, sib).wait_recv()
        for cp in self._own(ins, outs, ssem, rsem):
            cp.wait_send()
        for a in range(len(outs)):
            for k in (0, 1):
                _, onward, to_sibling = self._relays(outs, ssem, rsem, a, k)
                onward.wait_send()
                to_sibling.wait_send()
            self._far(outs, ssem, rsem, a)[1].wait_send()


class _ScatterPlan:
    def __init__(self, arrays, part=(0, 1, 1), into=None):
        self.n = len(arrays)
        self.operands = list(arrays) + list(into or [])
        self.out_shapes = [jax.ShapeDtypeStruct((3,) + a.shape[1:], a.dtype) for a in arrays]
        self.aliases = {self.n + i: i for i in range(self.n)} if into else {}
        self.nsems = 3 * self.n
        self.base = 0
        self.part = part

    def _copies(self, ins, outs, ssem, rsem):
        x, y, c, me, chips, sib = _place()
        lo, hi, n = self.part
        out = []
        for a in range(self.n):
            h = ins[a].shape[1]
            rows = pl.ds(lo * h // n, (hi - lo) * h // n)
            for k, chip in enumerate(chips):
                out.append(_rcopy(ins[a].at[2 * chip[0] + chip[1], rows], outs[a].at[k, rows],
                                  ssem.at[self.base + 3 * a + k], rsem.at[self.base + 3 * a + k], (*chip, c)))
        return out

    def schedule(self):
        return [(0.0, self.start)]

    def start(self, ins, outs, ssem, rsem):
        for cp in self._copies(ins, outs, ssem, rsem):
            cp.start()

    def finish(self, ins, outs, ssem, rsem):
        cps = self._copies(ins, outs, ssem, rsem)
        for cp in cps:
            cp.wait_recv()
        for cp in cps:
            cp.wait_send()


class _ShareHalfPlan(_ScatterPlan):
    def __init__(self, arrays):
        super().__init__(arrays)
        self.out_shapes = [jax.ShapeDtypeStruct((3, a.shape[0] // 2, a.shape[1]), a.dtype) for a in arrays]

    def _copies(self, ins, outs, ssem, rsem):
        x, y, c, me, chips, sib = _place()
        out = []
        for a in range(self.n):
            rh = ins[a].shape[0] // 2
            for k, chip in enumerate(chips):
                out.append(_rcopy(ins[a].at[pl.ds(c * rh, rh)], outs[a].at[k],
                                  ssem.at[self.base + 3 * a + k], rsem.at[self.base + 3 * a + k], (*chip, c)))
        return out


class _SwapPlan:
    def __init__(self, grads):
        self.operands = list(grads)
        self.out_shapes = [jax.ShapeDtypeStruct((g.shape[0], g.shape[1] // 2, g.shape[2]), g.dtype) for g in grads]
        self.aliases = {}
        self.nsems = len(grads)
        self.base = 0

    def _copies(self, ins, outs, ssem, rsem):
        x, y, c, me, chips, sib = _place()
        out = []
        for a, src in enumerate(ins):
            h = src.shape[1] // 2
            out.append(_rcopy(src.at[:, pl.ds((1 - c) * h, h), :], outs[a], ssem.at[self.base + a], rsem.at[self.base + a], sib))
        return out

    def schedule(self):
        return [(0.0, self.start)]

    def start(self, ins, outs, ssem, rsem):
        for cp in self._copies(ins, outs, ssem, rsem):
            cp.start()

    def finish(self, ins, outs, ssem, rsem):
        cps = self._copies(ins, outs, ssem, rsem)
        for cp in cps:
            cp.wait_recv()
        for cp in cps:
            cp.wait_send()


class _Multi:
    def __init__(self, plans):
        self.plans = plans
        self.operands, self.out_shapes, self.aliases, self.nsems = [], [], {}, 0
        self.spans = []
        for p in plans:
            ni, no = len(self.operands), len(self.out_shapes)
            self.spans.append((ni, ni + len(p.operands), no, no + len(p.out_shapes)))
            self.aliases.update({ni + i: no + j for i, j in p.aliases.items()})
            p.base = self.nsems
            self.nsems += p.nsems
            self.operands += p.operands
            self.out_shapes += p.out_shapes

    def schedule(self):
        def bound(fn, span):
            i0, i1, o0, o1 = span
            return lambda ins, outs, ssem, rsem: fn(ins[i0:i1], outs[o0:o1], ssem, rsem)

        stages = [(at, bound(fn, span)) for p, span in zip(self.plans, self.spans) for at, fn in p.schedule()]
        return sorted(stages, key=lambda s: s[0])

    def finish(self, ins, outs, ssem, rsem):
        for p, (i0, i1, o0, o1) in zip(self.plans, self.spans):
            p.finish(ins[i0:i1], outs[o0:o1], ssem, rsem)

    def results(self, extra):
        return [list(extra[o0:o1]) for (_, _, o0, o1) in self.spans]


class _Host:
    def __init__(self, comm, in_specs, out_specs, out_shape, scratch):
        self.comm = comm
        self.n_in, self.n_out = len(in_specs), len(out_specs)
        self.in_specs, self.out_specs, self.out_shape, self.scratch = list(in_specs), list(out_specs), list(out_shape), list(scratch)
        self.aliases = {}
        self.args = []
        if comm is not None:
            self.in_specs += [ANY] * len(comm.operands)
            self.out_specs += [ANY] * len(comm.out_shapes)
            self.out_shape += comm.out_shapes
            self.scratch += [pltpu.SemaphoreType.DMA((comm.nsems,)), pltpu.SemaphoreType.DMA((comm.nsems,))]
            self.aliases = {self.n_in + i: self.n_out + j for i, j in comm.aliases.items()}
            self.args = list(comm.operands)

    def split(self, refs):
        nc = len(self.args)
        nco = len(self.out_shape) - self.n_out
        ins, p = refs[:self.n_in], self.n_in + nc
        outs, rest = refs[p:p + self.n_out], refs[p + self.n_out + nco:]
        self._cargs = None
        if self.comm is not None:
            self._cargs = (refs[self.n_in:p], refs[p + self.n_out:p + self.n_out + nco], rest[-2], rest[-1])
            rest = rest[:-2]
        return ins, outs, rest

    def before(self, step, total):
        if self.comm is None:
            return

        for at, stage in self.comm.schedule():
            pl.when(step == min(total - 1, int(at * total)))(functools.partial(stage, *self._cargs))

    def after(self, step, total):
        if self.comm is None:
            return

        @pl.when(step == total - 1)
        def _():
            self.comm.finish(*self._cargs)

    def results(self, outs):
        return outs[:self.n_out], outs[self.n_out:]


def _cast_bf16(x, name):
    r, c = x.shape
    tr = ROW_TILE if r % ROW_TILE == 0 else r

    def body(x_ref, o_ref):
        o_ref[...] = x_ref[...].astype(BF16)

    return pl.pallas_call(
        body, name=name, grid=(r // tr,),
        in_specs=[pl.BlockSpec((tr, c), lambda i: (i, 0))],
        out_specs=pl.BlockSpec((tr, c), lambda i: (i, 0)),
        out_shape=jax.ShapeDtypeStruct((r, c), BF16), compiler_params=_cp("parallel"))(x)


def _cast_bf16_own_slab(x, me_arr, name):
    r, c = x.shape
    tr = ROW_TILE if r % ROW_TILE == 0 else r

    def body(me_ref, x_ref, o_ref):
        o_ref[...] = x_ref[...].astype(BF16)

    return pl.pallas_call(
        body, name=name,
        grid_spec=pltpu.PrefetchScalarGridSpec(
            num_scalar_prefetch=1, grid=(r // tr,),
            in_specs=[pl.BlockSpec((tr, c), lambda i, me: (i, 0))],
            out_specs=pl.BlockSpec((None, tr, c), lambda i, me: (me[0], i, 0))),
        out_shape=jax.ShapeDtypeStruct((N_CHIPS, r, c), BF16), compiler_params=_cp("parallel"))(me_arr, x)


def _rms_fwd(x, g, name):
    s, d = x.shape

    def body(x_ref, g_ref, h_ref):
        xhat, _ = _rms_stats(x_ref[...])
        h_ref[...] = (xhat * g_ref[...]).astype(BF16)

    return pl.pallas_call(
        body, name=name, grid=(s // ROW_TILE,),
        in_specs=[pl.BlockSpec((ROW_TILE, d), lambda i: (i, 0)), pl.BlockSpec((1, d), lambda i: (0, 0))],
        out_specs=pl.BlockSpec((ROW_TILE, d), lambda i: (i, 0)),
        out_shape=jax.ShapeDtypeStruct((s, d), BF16), compiler_params=_cp("parallel"))(x, g)


def _mm_nn(a, w3, name, comm=None):
    m, k = a.shape
    nsh, _, ns = w3.shape
    tm = 512 if m % 512 == 0 else ROW_TILE
    tn = _pick_tile(ns, 1024)
    per = ns // tn
    grid = (nsh * per, m // tm)
    host = _Host(comm,
                 [pl.BlockSpec((tm, k), lambda n, i: (i, 0)), pl.BlockSpec((None, k, tn), lambda n, i: (n // per, 0, n % per))],
                 [pl.BlockSpec((tm, tn), lambda n, i: (i, n))], [jax.ShapeDtypeStruct((m, nsh * ns), F32)], [])

    def body(*refs):
        (a_ref, w_ref), (o_ref,), _ = host.split(refs)
        step = pl.program_id(0) * grid[1] + pl.program_id(1)
        host.before(step, grid[0] * grid[1])
        o_ref[...] = jnp.dot(a_ref[...], w_ref[...], preferred_element_type=F32)
        host.after(step, grid[0] * grid[1])

    outs = pl.pallas_call(
        body, name=name, grid=grid, in_specs=host.in_specs, out_specs=host.out_specs, out_shape=host.out_shape,
        scratch_shapes=host.scratch, input_output_aliases=host.aliases,
        compiler_params=_cp("arbitrary", "arbitrary"))(a, w3, *host.args)
    (out,), extra = host.results(outs)
    return out, extra


def _mm_nt(a, b, name):
    m, k = a.shape
    n = b.shape[0]
    tm = 512 if m % 512 == 0 else ROW_TILE

    def body(a_ref, b_ref, o_ref):
        o_ref[...] = lax.dot_general(a_ref[...], b_ref[...], (((1,), (1,)), ((), ())), preferred_element_type=F32)

    return pl.pallas_call(
        body, name=name, grid=(m // tm,),
        in_specs=[pl.BlockSpec((tm, k), lambda i: (i, 0)), pl.BlockSpec((n, k), lambda i: (0, 0))],
        out_specs=pl.BlockSpec((tm, n), lambda i: (i, 0)),
        out_shape=jax.ShapeDtypeStruct((m, n), F32), compiler_params=_cp("parallel"))(a, b)


def _mm_tn(a, b, nsh, name, comm=None):
    s, m = a.shape
    n = b.shape[1]
    ns = n // nsh
    tm = 512 if m % 512 == 0 else ROW_TILE
    tn = _pick_tile(ns, 1024)
    per = ns // tn
    grid = (nsh * per, m // tm)
    host = _Host(comm, [pl.BlockSpec((s, tm), lambda j, i: (0, i)), pl.BlockSpec((s, tn), lambda j, i: (0, j))],
                 [pl.BlockSpec((None, tm, tn), lambda j, i: (j // per, i, j % per))],
                 [jax.ShapeDtypeStruct((nsh, m, ns), BF16)], [])

    def body(*refs):
        (a_ref, b_ref), (o_ref,), _ = host.split(refs)
        step = pl.program_id(0) * grid[1] + pl.program_id(1)
        host.before(step, grid[0] * grid[1])
        o_ref[...] = lax.dot_general(a_ref[...], b_ref[...], (((0,), (0,)), ((), ())),
                                     preferred_element_type=F32).astype(BF16)
        host.after(step, grid[0] * grid[1])

    outs = pl.pallas_call(
        body, name=name, grid=grid, in_specs=host.in_specs, out_specs=host.out_specs, out_shape=host.out_shape,
        scratch_shapes=host.scratch, input_output_aliases=host.aliases,
        compiler_params=_cp("arbitrary", "arbitrary"))(a, b, *host.args)
    (out,), extra = host.results(outs)
    return out, extra


def _tri(n, rel):
    row = lax.broadcasted_iota(jnp.int32, (2 * n, n), 0)
    col = lax.broadcasted_iota(jnp.int32, (2 * n, n), 1)
    return jnp.where(rel(jnp.where(row >= n, row - n, row), col), 1.0, 0.0).astype(BF16)


def _dot_split(x, tri2):
    hi = x.astype(BF16)
    lo = (x - hi.astype(F32)).astype(BF16)
    return jnp.dot(jnp.concatenate([hi, lo], axis=1), tri2, preferred_element_type=F32)


def _nt(a, b):
    return lax.dot_general(a, b, (((1,), (1,)), ((), ())), preferred_element_type=F32)


def _tn(a, b):
    return lax.dot_general(a, b, (((0,), (0,)), ((), ())), preferred_element_type=F32)


def _heads_per_step(nh):
    return max(h for h in (1, 2, 4) if nh % h == 0)


def _sba_fwd(p, sbw, name, comm=None):
    s = p.shape[0]
    nh = sbw // HEAD_DIM
    hp = _heads_per_step(nh)
    ngrp, hw = nh // hp, hp * HEAD_DIM
    blk = ATT_BLOCK
    nq = s // blk
    scale = 1.0 / math.sqrt(HEAD_DIM)
    host = _Host(comm,
                 [pl.BlockSpec((blk, hw), lambda g, i: (i, g)),
                  pl.BlockSpec((s, hw), lambda g, i: (0, ngrp + g)),
                  pl.BlockSpec((s, hw), lambda g, i: (0, 2 * ngrp + g))],
                 [pl.BlockSpec((blk, hw), lambda g, i: (i, g))] * 2,
                 [jax.ShapeDtypeStruct((s, sbw), F32)] * 2,
                 [pltpu.VMEM((s, hw), BF16)] * 2)

    def body(*refs):
        (q_ref, k_ref, v_ref), (o_ref, lt_ref), (kb_ref, vb_ref) = host.split(refs)
        i = pl.program_id(1)
        step = pl.program_id(0) * nq + i
        host.before(step, ngrp * nq)

        @pl.when(i == 0)
        def _():
            kb_ref[...] = k_ref[...].astype(BF16)
            vb_ref[...] = v_ref[...].astype(BF16)

        heads = [slice(h * HEAD_DIM, (h + 1) * HEAD_DIM) for h in range(hp)]
        qs = [q_ref[:, hd].astype(BF16) for hd in heads]
        later = _tri(blk, lambda r, c: r > c)
        causal = lax.broadcasted_iota(jnp.int32, (blk, blk), 1) < lax.broadcasted_iota(jnp.int32, (blk, blk), 0)

        def key_block(j, carry, diagonal):
            rows = pl.ds(pl.multiple_of(j * blk, blk), blk)
            hs = range(hp)
            z = [_nt(qs[h], kb_ref[rows, heads[h]]) * scale for h in hs]
            ls = [_log_sigmoid(z[h]) for h in hs]
            lm = [jnp.where(causal, ls[h] - z[h], 0.0) if diagonal else ls[h] - z[h] for h in hs]
            stay = [_dot_split(lm[h], later) for h in hs]
            w = [jnp.exp(ls[h] + stay[h] + carry[h][1]) for h in hs]
            if diagonal:
                w = [jnp.where(causal, w[h], 0.0) for h in hs]
            acc = [carry[h][0] + jnp.dot(w[h].astype(BF16), vb_ref[rows, heads[h]], preferred_element_type=F32) for h in hs]
            return tuple((acc[h], carry[h][1] + jnp.sum(lm[h], axis=1, keepdims=True)) for h in hs)

        init = tuple((jnp.zeros((blk, HEAD_DIM), F32), jnp.zeros((blk, 1), F32)) for _ in heads)
        carry = key_block(i, init, True)
        carry = lax.fori_loop(0, i, lambda n, c: key_block(i - 1 - n, c, False), carry)
        for h, hd in enumerate(heads):
            o_ref[:, hd] = carry[h][0]
            lt_ref[:, hd] = jnp.broadcast_to(carry[h][1], (blk, HEAD_DIM))
        host.after(step, ngrp * nq)

    outs = pl.pallas_call(
        body, name=name, grid=(ngrp, nq), in_specs=host.in_specs, out_specs=host.out_specs, out_shape=host.out_shape,
        scratch_shapes=host.scratch, input_output_aliases=host.aliases,
        compiler_params=_cp("arbitrary", "arbitrary"))(p, p, p, *host.args)
    (out, ltot), extra = host.results(outs)
    return out, ltot, extra


def _sba_bwd(p, ltot, dout, sbw, name, comm=None):
    s = p.shape[0]
    nh = sbw // HEAD_DIM
    hp = _heads_per_step(nh)
    ngrp, hw = nh // hp, hp * HEAD_DIM
    blk = ATT_BLOCK
    nq = s // blk
    scale = 1.0 / math.sqrt(HEAD_DIM)
    blk_spec = pl.BlockSpec((blk, hw), lambda g, i: (i, g))
    col_spec = pl.BlockSpec((s, hw), lambda g, i: (0, g))
    host = _Host(comm,
                 [blk_spec, pl.BlockSpec((s, hw), lambda g, i: (0, ngrp + g)),
                  pl.BlockSpec((s, hw), lambda g, i: (0, 2 * ngrp + g)), blk_spec, blk_spec],
                 [blk_spec, col_spec, col_spec], [jax.ShapeDtypeStruct((s, sbw), BF16)] * 3,
                 [pltpu.VMEM((s, hw), BF16)] * 2 + [pltpu.VMEM((s, hw), F32)] * 2)

    def body(*refs):
        (q_ref, k_ref, v_ref, lt_ref, do_ref), (dq_ref, dk_ref, dv_ref), (kb_ref, vb_ref, dka_ref, dva_ref) = host.split(refs)
        i = pl.program_id(1)
        step = pl.program_id(0) * nq + i
        host.before(step, ngrp * nq)

        @pl.when(i == 0)
        def _():
            kb_ref[...] = k_ref[...].astype(BF16)
            vb_ref[...] = v_ref[...].astype(BF16)
            dka_ref[...] = jnp.zeros_like(dka_ref)
            dva_ref[...] = jnp.zeros_like(dva_ref)

        heads = [slice(h * HEAD_DIM, (h + 1) * HEAD_DIM) for h in range(hp)]
        qs = [q_ref[:, hd].astype(BF16) for hd in heads]
        dos = [do_ref[:, hd].astype(BF16) for hd in heads]
        ltots = [lt_ref[:, h * HEAD_DIM:h * HEAD_DIM + 1] for h in range(hp)]
        upto = _tri(blk, lambda r, c: r <= c)
        before = _tri(blk, lambda r, c: r < c)
        causal = lax.broadcasted_iota(jnp.int32, (blk, blk), 1) < lax.broadcasted_iota(jnp.int32, (blk, blk), 0)

        def key_block(j, carry, diagonal):
            rows = pl.ds(pl.multiple_of(j * blk, blk), blk)
            hs = range(hp)
            kj = [kb_ref[rows, heads[h]] for h in hs]
            vj = [vb_ref[rows, heads[h]] for h in hs]
            z = [_nt(qs[h], kj[h]) * scale for h in hs]
            dw = [_nt(dos[h], vj[h]) for h in hs]
            ls = [_log_sigmoid(z[h]) for h in hs]
            lm = [jnp.where(causal, ls[h] - z[h], 0.0) if diagonal else ls[h] - z[h] for h in hs]
            stay = [ltots[h] - carry[h][1] - _dot_split(lm[h], upto) for h in hs]
            w = [jnp.exp(ls[h] + stay[h]) for h in hs]
            if diagonal:
                w = [jnp.where(causal, w[h], 0.0) for h in hs]
            da = [dw[h] * w[h] for h in hs]
            sig = [jnp.exp(ls[h]) for h in hs]
            chain = [sig[h] * (carry[h][2] + _dot_split(da[h], before)) for h in hs]
            if diagonal:
                chain = [jnp.where(causal, chain[h], 0.0) for h in hs]
            dzb = [((da[h] * (1.0 - sig[h]) - chain[h]) * scale).astype(BF16) for h in hs]
            dq = [carry[h][0] + jnp.dot(dzb[h], kj[h], preferred_element_type=F32) for h in hs]
            for h in hs:
                dka_ref[rows, heads[h]] += _tn(dzb[h], qs[h])
            for h in hs:
                dva_ref[rows, heads[h]] += _tn(w[h].astype(BF16), dos[h])
            return tuple((dq[h], carry[h][1] + jnp.sum(lm[h], axis=1, keepdims=True),
                          carry[h][2] + jnp.sum(da[h], axis=1, keepdims=True)) for h in hs)

        zero = jnp.zeros((blk, 1), F32)
        init = tuple((jnp.zeros((blk, HEAD_DIM), F32), zero, zero) for _ in heads)
        carry = lax.fori_loop(0, i, lambda j, c: key_block(j, c, False), init)
        carry = key_block(i, carry, True)
        for h, hd in enumerate(heads):
            dq_ref[:, hd] = carry[h][0].astype(BF16)

        @pl.when(i == nq - 1)
        def _():
            dk_ref[...] = dka_ref[...].astype(BF16)
            dv_ref[...] = dva_ref[...].astype(BF16)

        host.after(step, ngrp * nq)

    outs = pl.pallas_call(
        body, name=name, grid=(ngrp, nq), in_specs=host.in_specs, out_specs=host.out_specs, out_shape=host.out_shape,
        scratch_shapes=host.scratch, input_output_aliases=host.aliases,
        compiler_params=_cp("arbitrary", "arbitrary"))(p, p, p, ltot, dout, *host.args)
    (dq, dk, dv), extra = host.results(outs)
    return dq, dk, dv, extra


def _pool_groups(pad_ref, tile, row0, gd, halo):
    row = row0 + lax.broadcasted_iota(jnp.int32, (tile, 1), 0)
    out = []
    for gi, win in enumerate(POOL_WINDOWS):
        cs = slice(gi * gd, (gi + 1) * gd)
        tok = pad_ref[halo:halo + tile, cs]
        acc = tok
        for j in range(1, win):
            acc = acc + pad_ref[halo - j:halo - j + tile, cs]
        cnt = jnp.minimum(win, row + 1).astype(F32)
        out.append(acc / cnt - tok)
    return out


def _even_mix_fwd(p, att, pool_w, pool_scale, d, name):
    s = p.shape[0]
    half = d // 2
    gd = half // len(POOL_WINDOWS)
    t, hb = ROW_TILE, POOL_HALO

    def body(u_ref, uh_ref, g_ref, a_ref, pw_ref, sc_ref, y_ref, pad_ref):
        i = pl.program_id(0)
        pad_ref[0:hb, :] = jnp.where(i > 0, uh_ref[...], 0.0)
        pad_ref[hb:, :] = u_ref[...]
        pooled = _pool_groups(pad_ref, t, i * t, gd, hb)
        for gi in range(len(POOL_WINDOWS)):
            cs = slice(gi * gd, (gi + 1) * gd)
            po = jnp.dot(pooled[gi].astype(BF16), pw_ref[gi], preferred_element_type=F32) * sc_ref[:, cs]
            y_ref[:, half + gi * gd:half + (gi + 1) * gd] = (po * _silu(g_ref[:, half + gi * gd:half + (gi + 1) * gd])).astype(BF16)
        y_ref[:, :half] = (a_ref[...] * _silu(g_ref[:, :half])).astype(BF16)

    return pl.pallas_call(
        body, name=name, grid=(s // t,),
        in_specs=[pl.BlockSpec((t, half), lambda i: (i, 3)),
                  pl.BlockSpec((hb, half), lambda i: (jnp.maximum(i * (t // hb) - 1, 0), 3)),
                  pl.BlockSpec((t, d), lambda i: (i, 2)),
                  pl.BlockSpec((t, half), lambda i: (i, 0)),
                  pl.BlockSpec(pool_w.shape, lambda i: (0, 0, 0)),
                  pl.BlockSpec((1, half), lambda i: (0, 0))],
        out_specs=pl.BlockSpec((t, d), lambda i: (i, 0)),
        out_shape=jax.ShapeDtypeStruct((s, d), BF16),
        scratch_shapes=[pltpu.VMEM((hb + t, half), F32)],
        compiler_params=_cp("parallel"))(p, p, p, att, pool_w, pool_scale)


def _even_mix_bwd(p, att, dy, pool_w, pool_scale, d, name, comm=None):
    s = p.shape[0]
    half = d // 2
    ng = len(POOL_WINDOWS)
    gd = half // ng
    t, hb = ROW_TILE, POOL_HALO
    nt = s // t
    host = _Host(
        comm,
        [pl.BlockSpec((t, half), lambda i: (i, 3)),
         pl.BlockSpec((hb, half), lambda i: (jnp.maximum(i * (t // hb) - 1, 0), 3)),
         pl.BlockSpec((t, d), lambda i: (i, 2)),
         pl.BlockSpec((hb, half), lambda i: (jnp.minimum((i + 1) * (t // hb), s // hb - 1), 5)),
         pl.BlockSpec((t, half), lambda i: (i, 0)),
         pl.BlockSpec((t, d), lambda i: (i, 0)),
         pl.BlockSpec((hb, half), lambda i: (jnp.minimum((i + 1) * (t // hb), s // hb - 1), 1)),
         pl.BlockSpec(pool_w.shape, lambda i: (0, 0, 0)),
         pl.BlockSpec((1, half), lambda i: (0, 0))],
        [pl.BlockSpec((t, half), lambda i: (i, 0)),
         pl.BlockSpec((t, half), lambda i: (i, 0)),
         pl.BlockSpec((t, d), lambda i: (i, 0)),
         pl.BlockSpec((1, half), lambda i: (0, 0)),
         pl.BlockSpec((ng, gd, gd), lambda i: (0, 0, 0))],
        [jax.ShapeDtypeStruct((s, half), F32), jax.ShapeDtypeStruct((s, half), BF16),
         jax.ShapeDtypeStruct((s, d), BF16), jax.ShapeDtypeStruct((1, half), F32),
         jax.ShapeDtypeStruct((ng, gd, gd), F32)],
        [pltpu.VMEM((hb + t, half), F32), pltpu.VMEM((t + hb, half), F32)])

    def body(*refs):
        ((u_ref, uh_ref, g_ref, gh_ref, a_ref, dy_ref, dyh_ref, pw_ref, sc_ref),
         (da_ref, du_ref, dg_ref, dsc_ref, dpw_ref), (pad_ref, dn_ref)) = host.split(refs)
        i = pl.program_id(0)
        host.before(i, nt)
        first = i == 0
        pad_ref[0:hb, :] = jnp.where(i > 0, uh_ref[...], 0.0)
        pad_ref[hb:, :] = u_ref[...]
        pooled = _pool_groups(pad_ref, t, i * t, gd, hb)
        g1 = g_ref[:, :half]
        dy1 = dy_ref[:, :half]
        da_ref[...] = dy1 * _silu(g1)
        dg_ref[:, :half] = (dy1 * a_ref[...] * _dsilu(g1)).astype(BF16)
        row = i * t + lax.broadcasted_iota(jnp.int32, (t + hb, 1), 0)
        for gi, win in enumerate(POOL_WINDOWS):
            cs = slice(gi * gd, (gi + 1) * gd)
            cs2 = slice(half + gi * gd, half + (gi + 1) * gd)
            w = pw_ref[gi]
            pb = pooled[gi].astype(BF16)
            zp = jnp.dot(pb, w, preferred_element_type=F32)
            g2 = g_ref[:, cs2]
            dy2 = dy_ref[:, cs2]
            dg_ref[:, cs2] = (dy2 * zp * sc_ref[:, cs] * _dsilu(g2)).astype(BF16)
            dpo = dy2 * _silu(g2)
            _acc_rows(dsc_ref.at[:, cs], first, jnp.sum(dpo * zp, axis=0, keepdims=True))
            dz = (dpo * sc_ref[:, cs]).astype(BF16)
            _acc_rows(dpw_ref.at[gi], first, _tn(pb, dz))
            dzh = jnp.where(i < nt - 1, dyh_ref[:, cs] * _silu(gh_ref[:, cs]) * sc_ref[:, cs], 0.0).astype(BF16)
            dpool = _nt(dz, w)
            dpool_h = _nt(dzh, w)
            cnt = jnp.minimum(win, row + 1).astype(F32)
            dn_ref[0:t, cs] = dpool / cnt[0:t]
            dn_ref[t:, cs] = dpool_h / cnt[t:]
            acc = dn_ref[0:t, cs]
            for j in range(1, win):
                acc = acc + dn_ref[j:j + t, cs]
            du_ref[:, cs] = (acc - dpool).astype(BF16)
        host.after(i, nt)

    outs = pl.pallas_call(
        body, name=name, grid=(nt,), in_specs=host.in_specs, out_specs=host.out_specs, out_shape=host.out_shape,
        scratch_shapes=host.scratch, input_output_aliases=host.aliases,
        compiler_params=_cp("arbitrary"))(p, p, p, p, att, dy, dy, pool_w, pool_scale, *host.args)
    return host.results(outs)


def _mm_out_even(y, w, x, g_post, g_pre_next, name):
    s, k = y.shape
    d = w.shape[1]
    t = ROW_TILE

    def body(y_ref, w_ref, x_ref, gp_ref, gn_ref, o_ref, x1_ref, h1_ref):
        o = jnp.dot(y_ref[...], w_ref[...], preferred_element_type=F32)
        o_ref[...] = o
        ohat, _ = _rms_stats(o)
        x1 = x_ref[...] + ohat * gp_ref[...]
        x1_ref[...] = x1
        xhat, _ = _rms_stats(x1)
        h1_ref[...] = (xhat * gn_ref[...]).astype(BF16)

    row = lambda c: pl.BlockSpec((t, c), lambda i: (i, 0))
    vec = pl.BlockSpec((1, d), lambda i: (0, 0))
    return pl.pallas_call(
        body, name=name, grid=(s // t,),
        in_specs=[row(k), pl.BlockSpec((k, d), lambda i: (0, 0)), row(d), vec, vec],
        out_specs=[row(d), row(d), row(d)],
        out_shape=[jax.ShapeDtypeStruct((s, d), F32), jax.ShapeDtypeStruct((s, d), F32),
                   jax.ShapeDtypeStruct((s, d), BF16)],
        compiler_params=_cp("parallel"))(y, w, x, g_post, g_pre_next)


def _mm_out_odd(y, w, x1, g_post, target, name):
    s, k = y.shape
    d = w.shape[1]
    t = ROW_TILE

    def body(y_ref, w_ref, x_ref, gp_ref, tg_ref, do_ref, dx_ref, loss_ref, dgp_ref):
        first = pl.program_id(0) == 0
        o = jnp.dot(y_ref[...], w_ref[...], preferred_element_type=F32)
        ohat, r = _rms_stats(o)
        gp = gp_ref[...]
        diff = x_ref[...] + ohat * gp - tg_ref[...]
        part = 0.5 * jnp.sum(jnp.mean(diff * diff, axis=-1, keepdims=True), axis=0, keepdims=True)
        _acc_rows(loss_ref, first, jnp.broadcast_to(part, loss_ref.shape))
        dx2 = diff * (1.0 / d)
        dx_ref[...] = dx2
        do, dgp = _rms_bwd(dx2, ohat, r, gp)
        do_ref[...] = do.astype(BF16)
        _acc_rows(dgp_ref, first, dgp)

    row = lambda c: pl.BlockSpec((t, c), lambda i: (i, 0))
    vec = pl.BlockSpec((1, d), lambda i: (0, 0))
    return pl.pallas_call(
        body, name=name, grid=(s // t,),
        in_specs=[row(k), pl.BlockSpec((k, d), lambda i: (0, 0)), row(d), vec, row(d)],
        out_specs=[row(d), row(d), pl.BlockSpec((8, LANES), lambda i: (0, 0)), vec],
        out_shape=[jax.ShapeDtypeStruct((s, d), BF16), jax.ShapeDtypeStruct((s, d), F32),
                   jax.ShapeDtypeStruct((8, LANES), F32), jax.ShapeDtypeStruct((1, d), F32)],
        compiler_params=_cp("arbitrary"))(y, w, x1, g_post, target)


def _layer_norm(d1, cg, cb):
    mu = jnp.mean(d1, axis=-1, keepdims=True)
    cen = d1 - mu
    rstd = lax.rsqrt(jnp.mean(cen * cen, axis=-1, keepdims=True) + EPS)
    n = cen * rstd
    return n, rstd, n * cg + cb


SUBLANES = 8
GATHER_PIECES = 4
CONV_ROWS = 64


def _make_shifts(pad_ref, cs, sh_ref):
    rows = sh_ref.shape[1]
    for r in range(1, SUBLANES):
        sh_ref[r - 1] = pad_ref[r:r + rows, cs]


def _by_shift(taps, base, sign=1):
    return sorted(range(taps), key=lambda k: ((sign * (base + k)) % SUBLANES, k))


def _window(pad_ref, cs, sh_ref, off, t):
    m, r = divmod(off, SUBLANES)
    if r == 0:
        return pad_ref[SUBLANES * m:SUBLANES * m + t, cs]
    return sh_ref[r - 1, SUBLANES * m:SUBLANES * m + t, :]


def _odd_mix_fwd(p, sconv_w, dconv_w, dconv_b, cnorm_g, cnorm_b, d, name):
    s = p.shape[0]
    w = d // 2
    k3, k31 = sconv_w.shape[0], dconv_w.shape[0]
    t, hb = ROW_TILE, CONV_HALO
    assert hb >= k31 - 1 and w % LANES == 0

    def body(p_ref, ph_ref, w3_ref, w31_ref, b31_ref, cg_ref, cb_ref, y_ref, s3_ref, d1_ref, mpad, dpad, sh_ref):
        i = pl.program_id(0)
        mpad[0:hb, :] = jnp.where(i > 0, ph_ref[:, 2 * w:3 * w] * ph_ref[:, 0:w], 0.0)
        mpad[hb:, :] = p_ref[:, 2 * w:3 * w] * p_ref[:, 0:w]
        dpad[0:hb, :] = jnp.where(i > 0, ph_ref[:, 3 * w:4 * w] * _sigmoid(ph_ref[:, 4 * w:5 * w]), 0.0)
        dpad[hb:, :] = p_ref[:, 3 * w:4 * w] * _sigmoid(p_ref[:, 4 * w:5 * w])
        for c0 in range(0, w, LANES):
            cs = slice(c0, c0 + LANES)
            acc = jnp.zeros((t, LANES), F32)
            for kk in range(k3):
                acc = acc + w3_ref[kk:kk + 1, cs] * mpad[hb - (k3 - 1) + kk:hb - (k3 - 1) + kk + t, cs]
            s3_ref[:, cs] = acc
            _make_shifts(dpad, cs, sh_ref)
            for r0 in range(0, t, CONV_ROWS):
                acc = jnp.zeros((CONV_ROWS, LANES), F32)
                for kk in _by_shift(k31, hb - (k31 - 1)):
                    acc = acc + w31_ref[kk:kk + 1, cs] * _window(dpad, cs, sh_ref, hb - (k31 - 1) + kk + r0, CONV_ROWS)
                d1_ref[r0:r0 + CONV_ROWS, cs] = acc + b31_ref[:, cs]
        _, _, d2 = _layer_norm(d1_ref[...], cg_ref[...], cb_ref[...])
        y_ref[:, :w] = (p_ref[:, w:2 * w] * s3_ref[...] * _silu(p_ref[:, 5 * w:6 * w])).astype(BF16)
        y_ref[:, w:] = (_silu(d2) * _silu(p_ref[:, 6 * w:7 * w])).astype(BF16)

    row = lambda c: pl.BlockSpec((t, c), lambda i: (i, 0))
    full = lambda a: pl.BlockSpec(a.shape, lambda i: (0, 0))
    return pl.pallas_call(
        body, name=name, grid=(s // t,),
        in_specs=[row(7 * w),
                  pl.BlockSpec((hb, 5 * w), lambda i: (jnp.maximum(i * (t // hb) - 1, 0), 0)),
                  full(sconv_w), full(dconv_w), full(dconv_b), full(cnorm_g), full(cnorm_b)],
        out_specs=[row(d), row(w), row(w)],
        out_shape=[jax.ShapeDtypeStruct((s, d), BF16), jax.ShapeDtypeStruct((s, w), F32),
                   jax.ShapeDtypeStruct((s, w), F32)],
        scratch_shapes=[pltpu.VMEM((hb + t, w), F32)] * 2 + [pltpu.VMEM((SUBLANES - 1, hb + t - SUBLANES, LANES), F32)],
        compiler_params=_cp("parallel"))(p, p, sconv_w, dconv_w, dconv_b, cnorm_g, cnorm_b)


def _odd_bwd_rows(p, s3, d1, dy, cnorm_g, cnorm_b, d, name, comm=None):
    s = p.shape[0]
    w = d // 2
    t = ROW_TILE
    col = lambda j: pl.BlockSpec((t, w), lambda i: (i, j))
    row = lambda c: pl.BlockSpec((t, c), lambda i: (i, 0))
    vec = pl.BlockSpec((1, w), lambda i: (0, 0))
    host = _Host(comm, [col(1), col(5), col(6), row(w), row(w), row(d), vec, vec],
                 [row(w), row(d), row(w), row(w), vec, vec, vec],
                 [jax.ShapeDtypeStruct((s, w), BF16), jax.ShapeDtypeStruct((s, d), BF16),
                  jax.ShapeDtypeStruct((s, w), F32), jax.ShapeDtypeStruct((s, w), F32)] + [jax.ShapeDtypeStruct((1, w), F32)] * 3, [])

    def body(*refs):
        ((bc_ref, g1_ref, g2_ref, s3_ref, d1_ref, dy_ref, cg_ref, cb_ref),
         (dbc_ref, dg_ref, ds3_ref, dd1_ref, dcg_ref, dcb_ref, db_ref), _) = host.split(refs)
        step = pl.program_id(0)
        host.before(step, s // t)
        first = step == 0
        g1, g2 = g1_ref[...], g2_ref[...]
        bc, s3v = bc_ref[...], s3_ref[...]
        dy1, dy2 = dy_ref[:, :w], dy_ref[:, w:]
        n, rstd, d2 = _layer_norm(d1_ref[...], cg_ref[...], cb_ref[...])
        dg_ref[:, :w] = (dy1 * bc * s3v * _dsilu(g1)).astype(BF16)
        dg_ref[:, w:] = (dy2 * _silu(d2) * _dsilu(g2)).astype(BF16)
        dco = dy1 * _silu(g1)
        dbc_ref[...] = (dco * s3v).astype(BF16)
        ds3_ref[...] = dco * bc
        dd2 = dy2 * _silu(g2) * _dsilu(d2)
        _acc_rows(dcb_ref, first, jnp.sum(dd2, axis=0, keepdims=True))
        _acc_rows(dcg_ref, first, jnp.sum(dd2 * n, axis=0, keepdims=True))
        dn = dd2 * cg_ref[...]
        dd1 = rstd * (dn - jnp.mean(dn, axis=-1, keepdims=True) - n * jnp.mean(dn * n, axis=-1, keepdims=True))
        dd1_ref[...] = dd1
        _acc_rows(db_ref, first, jnp.sum(dd1, axis=0, keepdims=True))
        host.after(step, s // t)

    outs = pl.pallas_call(
        body, name=name, grid=(s // t,), in_specs=host.in_specs, out_specs=host.out_specs, out_shape=host.out_shape,
        scratch_shapes=host.scratch, input_output_aliases=host.aliases,
        compiler_params=_cp("arbitrary"))(p, p, p, s3, d1, dy, cnorm_g, cnorm_b, *host.args)
    return host.results(outs)


def _odd_bwd_conv(p, ds3, dd1, sconv_w, dconv_w, d, name):
    s = p.shape[0]
    w = d // 2
    k3, k31 = sconv_w.shape[0], dconv_w.shape[0]
    t, hb, ha = ROW_TILE, CONV_HALO, 8
    nt = s // t
    assert hb >= k31 - 1 and ha >= k3 - 1

    def body(hc_ref, cc_ref, ga_ref, gb_ref, hch_ref, cch_ref, gah_ref, gbh_ref, ds3_ref, ds3h_ref, dd1_ref, dd1h_ref,
             w3_ref, w31_ref, dhc_ref, dcc_ref, dga_ref, dgb_ref, dw3_ref, dw31_ref, mpad, dpad, s3pad, d1pad, sh_ref):
        i = pl.program_id(0)
        first = i == 0
        last = i == nt - 1
        mpad[0:hb, :] = jnp.where(i > 0, cch_ref[...] * hch_ref[...], 0.0)
        mpad[hb:, :] = cc_ref[...] * hc_ref[...]
        dpad[0:hb, :] = jnp.where(i > 0, gah_ref[...] * _sigmoid(gbh_ref[...]), 0.0)
        dpad[hb:, :] = ga_ref[...] * _sigmoid(gb_ref[...])
        s3pad[0:t, :] = ds3_ref[...]
        s3pad[t:, :] = jnp.where(last, 0.0, ds3h_ref[...])
        d1pad[0:t, :] = dd1_ref[...]
        d1pad[t:, :] = jnp.where(last, 0.0, dd1h_ref[...])

        @pl.when(first)
        def _():
            dw3_ref[...] = jnp.zeros_like(dw3_ref)
            dw31_ref[...] = jnp.zeros_like(dw31_ref)

        def fold(v):
            return jnp.sum(v.reshape(v.shape[0] // SUBLANES, SUBLANES, LANES), axis=0)

        groups = range(0, t, CONV_ROWS)
        for c0 in range(0, w, LANES):
            cs = slice(c0, c0 + LANES)
            ds3v = s3pad[0:t, cs]
            dm = jnp.zeros((t, LANES), F32)
            for kk in range(k3):
                dm = dm + w3_ref[kk:kk + 1, cs] * s3pad[k3 - 1 - kk:k3 - 1 - kk + t, cs]
                off = hb - (k3 - 1) + kk
                dw3_ref[SUBLANES * kk:SUBLANES * (kk + 1), cs] += fold(ds3v * mpad[off:off + t, cs])
            dcc_ref[:, cs] = (dm * hc_ref[:, cs]).astype(BF16)
            dhc_ref[:, cs] = (dm * cc_ref[:, cs]).astype(BF16)
            _make_shifts(d1pad, cs, sh_ref)
            for r0 in groups:
                rows = slice(r0, r0 + CONV_ROWS)
                dd0 = jnp.zeros((CONV_ROWS, LANES), F32)
                for kk in _by_shift(k31, -(k31 - 1), -1):
                    dd0 = dd0 + w31_ref[kk:kk + 1, cs] * _window(d1pad, cs, sh_ref, k31 - 1 - kk + r0, CONV_ROWS)
                sgb = _sigmoid(gb_ref[rows, cs])
                dga_ref[rows, cs] = (dd0 * sgb).astype(BF16)
                dgb_ref[rows, cs] = (dd0 * ga_ref[rows, cs] * sgb * (1.0 - sgb)).astype(BF16)
            _make_shifts(dpad, cs, sh_ref)
            for kk in _by_shift(k31, hb - (k31 - 1)):
                part = jnp.zeros((SUBLANES, LANES), F32)
                for r0 in groups:
                    part = part + fold(d1pad[r0:r0 + CONV_ROWS, cs]
                                       * _window(dpad, cs, sh_ref, hb - (k31 - 1) + kk + r0, CONV_ROWS))
                dw31_ref[SUBLANES * kk:SUBLANES * (kk + 1), cs] += part

    col = lambda j: pl.BlockSpec((t, w), lambda i: (i, j))
    pre = lambda j: pl.BlockSpec((hb, w), lambda i: (jnp.maximum(i * (t // hb) - 1, 0), j))
    row = pl.BlockSpec((t, w), lambda i: (i, 0))
    post = lambda h: pl.BlockSpec((h, w), lambda i: (jnp.minimum((i + 1) * (t // h), s // h - 1), 0))
    full = lambda a: pl.BlockSpec(a.shape, lambda i: (0, 0))
    dhc, dcc, dga, dgb, dw3, dw31 = pl.pallas_call(
        body, name=name, grid=(nt,),
        in_specs=[col(0), col(2), col(3), col(4), pre(0), pre(2), pre(3), pre(4),
                  row, post(ha), row, post(hb), full(sconv_w), full(dconv_w)],
        out_specs=[row, row, row, row, pl.BlockSpec((SUBLANES * k3, w), lambda i: (0, 0)),
                   pl.BlockSpec((SUBLANES * k31, w), lambda i: (0, 0))],
        out_shape=[jax.ShapeDtypeStruct((s, w), BF16)] * 4
        + [jax.ShapeDtypeStruct((SUBLANES * k3, w), F32), jax.ShapeDtypeStruct((SUBLANES * k31, w), F32)],
        scratch_shapes=[pltpu.VMEM((hb + t, w), F32)] * 2 + [pltpu.VMEM((t + ha, w), F32), pltpu.VMEM((t + hb, w), F32),
                                                             pltpu.VMEM((SUBLANES - 1, hb + t - SUBLANES, LANES), F32)],
        compiler_params=_cp("arbitrary"))(p, p, p, p, p, p, p, p, ds3, ds3, dd1, dd1, sconv_w, dconv_w)
    return dhc, dcc, dga, dgb, jnp.sum(dw3.reshape(k3, SUBLANES, w), axis=1), jnp.sum(dw31.reshape(k31, SUBLANES, w), axis=1)


def _mm_in_bwd(dp, w3, x, g_pre, dres, post, name, comm=None):
    s = dp.shape[0]
    nsh, d, ns = w3.shape
    t = 512 if s % 512 == 0 else ROW_TILE
    nt = s // t
    ks = 2 if (ns // 2) % LANES == 0 else 1
    nk, kw = nsh * ks, ns // ks
    chunk = 128
    nchunk = t // chunk
    row = pl.BlockSpec((t, d), lambda i, k: (i, 0))
    vec = pl.BlockSpec((1, d), lambda i, k: (0, 0))
    rowwise = [x, dres] + ([post[0]] if post is not None else [])
    in_specs = [pl.BlockSpec((t, kw), lambda i, k: (i, k)), pl.BlockSpec((None, d, kw), lambda i, k: (k // ks, 0, k % ks)), vec]
    out_specs = [row, vec]
    out_shape = [jax.ShapeDtypeStruct((s, d), F32), jax.ShapeDtypeStruct((1, d), F32)]
    args = [dp, w3, g_pre]
    if post is not None:
        in_specs += [vec]
        out_specs += [row, vec]
        out_shape += [jax.ShapeDtypeStruct((s, d), BF16), jax.ShapeDtypeStruct((1, d), F32)]
        args += [post[1]]
    n_blocked = len(in_specs)
    in_specs += [ANY] * len(rowwise)
    args += rowwise
    host = _Host(comm, in_specs, out_specs, out_shape,
                 [pltpu.VMEM((d, t), F32), pltpu.VMEM((len(rowwise), 2, chunk, d), F32), pltpu.SemaphoreType.DMA((len(rowwise), 2))])

    def body(*refs):
        ins, outs, (acc_ref, buf_ref, sem_ref) = host.split(refs)
        dp_ref, w_ref, g_ref = ins[:3]
        hbm = ins[n_blocked:]
        dx_ref, dg_ref = outs[:2]
        tile = pl.program_id(0)
        kk = pl.program_id(1)
        first = tile == 0
        step = tile * nk + kk
        host.before(step, nt * nk)
        part = _nt(w_ref[...], dp_ref[...])

        @pl.when(kk == 0)
        def _():
            acc_ref[...] = part

        @pl.when(kk > 0)
        def _():
            acc_ref[...] += part

        def fetch(ci, slot):
            return [pltpu.make_async_copy(src.at[pl.ds(tile * t + ci * chunk, chunk)], buf_ref.at[n, slot], sem_ref.at[n, slot])
                    for n, src in enumerate(hbm)]

        @pl.when(kk == nk - 1)
        def _():
            dg = dgp = None
            for cp in fetch(0, 0):
                cp.start()
            for ci in range(nchunk):
                slot = ci % 2
                if ci + 1 < nchunk:
                    for cp in fetch(ci + 1, 1 - slot):
                        cp.start()
                for cp in fetch(ci, slot):
                    cp.wait()
                rows = slice(ci * chunk, (ci + 1) * chunk)
                xhat, r = _rms_stats(buf_ref[0, slot])
                dxn, dg_part = _rms_bwd(acc_ref[:, rows].T, xhat, r, g_ref[...])
                dx = buf_ref[1, slot] + dxn
                dx_ref[rows, :] = dx
                dg = dg_part if dg is None else dg + dg_part
                if post is not None:
                    ohat, ro = _rms_stats(buf_ref[2, slot])
                    do, dgp_part = _rms_bwd(dx, ohat, ro, ins[3][...])
                    outs[2][rows, :] = do.astype(BF16)
                    dgp = dgp_part if dgp is None else dgp + dgp_part
            _acc_rows(dg_ref, first, dg)
            if post is not None:
                _acc_rows(outs[3], first, dgp)

        host.after(step, nt * nk)

    res = pl.pallas_call(
        body, name=name, grid=(nt, nk), in_specs=host.in_specs, out_specs=host.out_specs, out_shape=host.out_shape,
        scratch_shapes=host.scratch, input_output_aliases=host.aliases,
        compiler_params=_cp("arbitrary", "arbitrary"))(*args, *host.args)
    return host.results(res)


def _half_add(g, r1, c_arr, name):
    nsh, rows, ns = g.shape
    h = rows // 2
    tr = min(ROW_TILE, h)
    per = h // tr

    def body(c_ref, g_ref, r_ref, o_ref):
        o_ref[...] = (g_ref[...].astype(F32) + r_ref[...].astype(F32)).astype(BF16)

    spec = pl.BlockSpec((None, tr, ns), lambda s, r, c: (s, r, 0))
    return pl.pallas_call(
        body, name=name,
        grid_spec=pltpu.PrefetchScalarGridSpec(
            num_scalar_prefetch=1, grid=(nsh, per),
            in_specs=[pl.BlockSpec((None, tr, ns), lambda s, r, c: (s, c[0] * per + r, 0)), spec], out_specs=spec),
        out_shape=jax.ShapeDtypeStruct((nsh, h, ns), BF16), compiler_params=_cp("parallel", "parallel"))(c_arr, g, r1)


def _sum_chips(hh, r2, mc_arr, name):
    _, h, ns = hh.shape
    tr = min(ROW_TILE, h)
    per = h // tr

    def body(mc_ref, h_ref, a_ref, b_ref, c_ref, o_ref):
        o_ref[...] = ((h_ref[...].astype(F32) + a_ref[...].astype(F32)) + b_ref[...].astype(F32)) + c_ref[...].astype(F32)

    got = lambda k: pl.BlockSpec((None, tr, ns), lambda r, mc: (k, r, 0))
    return pl.pallas_call(
        body, name=name,
        grid_spec=pltpu.PrefetchScalarGridSpec(
            num_scalar_prefetch=1, grid=(per,),
            in_specs=[pl.BlockSpec((None, tr, ns), lambda r, mc: (mc[0], r, 0)), got(0), got(1), got(2)],
            out_specs=pl.BlockSpec((tr, ns), lambda r, mc: (mc[1] * per + r, 0))),
        out_shape=jax.ShapeDtypeStruct((2 * h, ns), F32), compiler_params=_cp("parallel"))(mc_arr, hh, r2, r2, r2)


def _add2(a, b, name):
    def body(a_ref, b_ref, o_ref):
        o_ref[...] = a_ref[...] + b_ref[...]

    return pl.pallas_call(body, name=name, out_shape=jax.ShapeDtypeStruct(a.shape, a.dtype), compiler_params=_cp())(a, b)


def _sum_chips_ordered(s2, r2, mc_arr, name):
    rows, w = s2.shape
    rh = rows // 2

    def body(mc_ref, s_ref, a_ref, b_ref, c_ref, o_ref):
        me = mc_ref[0]
        acc = None
        for j in range(N_CHIPS):
            rel = jnp.bitwise_xor(me, j)
            v = jnp.where(rel == 0, s_ref[...], jnp.where(rel == 2, a_ref[...], jnp.where(rel == 1, b_ref[...], c_ref[...])))
            acc = v if acc is None else acc + v
        o_ref[...] = acc

    got = lambda k: pl.BlockSpec((None, rh, w), lambda i, mc: (k, 0, 0))
    return pl.pallas_call(
        body, name=name,
        grid_spec=pltpu.PrefetchScalarGridSpec(
            num_scalar_prefetch=1, grid=(1,),
            in_specs=[pl.BlockSpec((rh, w), lambda i, mc: (mc[1], 0)), got(0), got(1), got(2)],
            out_specs=pl.BlockSpec((rh, w), lambda i, mc: (mc[1], 0))),
        out_shape=jax.ShapeDtypeStruct((rows, w), F32), compiler_params=_cp("arbitrary"))(mc_arr, s2, r2, r2, r2)


def _adamw(w, g, m, v, name, comm=None):
    r, c = w.shape
    tr = ROW_TILE if r % ROW_TILE == 0 else r
    c1 = 1.0 / (1.0 - ADAM_B1 ** ADAM_STEP)
    c2 = 1.0 / (1.0 - ADAM_B2 ** ADAM_STEP)
    spec = pl.BlockSpec((tr, c), lambda i: (i, 0))
    host = _Host(comm, [spec] * 4, [spec] * 4, [jax.ShapeDtypeStruct((r, c), F32)] * 4, [])

    def body(*refs):
        (w_ref, g_ref, m_ref, v_ref), (go_ref, d_ref, nm_ref, nv_ref), _ = host.split(refs)
        step = pl.program_id(0)
        host.before(step, r // tr)
        gv = g_ref[...]
        go_ref[...] = gv
        nm = ADAM_B1 * m_ref[...] + (1.0 - ADAM_B1) * gv
        nv = ADAM_B2 * v_ref[...] + (1.0 - ADAM_B2) * (gv * gv)
        nm_ref[...] = nm
        nv_ref[...] = nv
        d_ref[...] = -ADAM_LR * ((nm * c1) / (jnp.sqrt(nv * c2) + ADAM_EPS) + ADAM_WD * w_ref[...])
        host.after(step, r // tr)

    outs = pl.pallas_call(
        body, name=name, grid=(r // tr,), in_specs=host.in_specs, out_specs=host.out_specs, out_shape=host.out_shape,
        scratch_shapes=host.scratch, input_output_aliases=host.aliases,
        compiler_params=_cp("arbitrary"))(w, g, m, v, *host.args)
    return host.results(outs)


def _gather_weights(bigs, pool_w, pack_w, pack_d, name):
    nb = len(bigs)
    smalls = [pool_w, pack_w, pack_d]
    q, cw, cd = pool_w.shape[1], pack_w.shape[1], pack_d.shape[1]
    pieces = [_GatherPlan(bigs, (j, j + 1, GATHER_PIECES)) for j in range(GATHER_PIECES)]
    for j, piece in enumerate(pieces):
        piece.base = 9 + j * piece.nsems

    def body(*refs):
        srcs, dsts = refs[:nb + 3], refs[nb + 3:2 * (nb + 3)]
        ssem, rsem, lsem = refs[2 * (nb + 3):]
        x, y, c, me, chips, sib = _place()

        def small_dst(n, chip):
            if n == 0:
                return dsts[nb].at[:, pl.ds(chip * q, q), :]
            return dsts[nb + n].at[:, pl.ds(chip * (cw if n == 1 else cd), cw if n == 1 else cd)]

        local = [pltpu.make_async_copy(srcs[nb + n], small_dst(n, me), lsem.at[n]) for n in range(3)]
        for cp in local:
            cp.start()
        sends = []
        for n in range(3):
            for k, chip in enumerate(chips):
                cp = _rcopy(srcs[nb + n], small_dst(n, me), ssem.at[3 * n + k], rsem.at[3 * n + k], (*chip, c))
                cp.start()
                sends.append(cp)
        big = (srcs[:nb], dsts[:nb], ssem, rsem)
        for stage in ("start", "relay", "relay_far", "finish"):
            for piece in pieces:
                getattr(piece, stage)(*big)
        for n in range(3):
            for k, chip in enumerate(chips):
                ref = small_dst(n, 2 * chip[0] + chip[1])
                _rcopy(ref, ref, ssem.at[3 * n + k], rsem.at[3 * n + k], (*chip, c)).wait_recv()
        for cp in sends:
            cp.wait_send()
        for cp in local:
            cp.wait()

    nsem = 9 + sum(piece.nsems for piece in pieces)
    out_shape = [jax.ShapeDtypeStruct(b.shape, b.dtype) for b in bigs]
    out_shape += [jax.ShapeDtypeStruct((pool_w.shape[0], N_CHIPS * q, pool_w.shape[2]), pool_w.dtype),
                  jax.ShapeDtypeStruct((pack_w.shape[0], N_CHIPS * cw), pack_w.dtype),
                  jax.ShapeDtypeStruct((pack_d.shape[0], N_CHIPS * cd), pack_d.dtype)]
    return pl.pallas_call(
        body, name=name, in_specs=[ANY] * (nb + 3), out_specs=[ANY] * (nb + 3), out_shape=out_shape,
        input_output_aliases={a: a for a in range(nb)},
        scratch_shapes=[pltpu.SemaphoreType.DMA((nsem,)), pltpu.SemaphoreType.DMA((nsem,)), pltpu.SemaphoreType.DMA((3,))],
        compiler_params=pltpu.CompilerParams(has_side_effects=True))(*bigs, *smalls)


def _swap_with_sibling(grads, wholes, name):
    n, nw = len(grads), len(wholes)
    halves = [g.shape[1] // 2 for g in grads]

    def body(*refs):
        srcs, dsts = refs[:n + nw], refs[n + nw:2 * (n + nw)]
        ssem, rsem = refs[2 * (n + nw):]
        x, y, c, me, chips, sib = _place()
        cps = [_rcopy(srcs[a].at[:, pl.ds((1 - c) * halves[a], halves[a]), :], dsts[a], ssem.at[a], rsem.at[a], sib)
               for a in range(n)]
        cps += [_rcopy(srcs[a], dsts[a], ssem.at[a], rsem.at[a], sib) for a in range(n, n + nw)]
        for cp in cps:
            cp.start()
        for cp in cps:
            cp.wait_recv()
        for cp in cps:
            cp.wait_send()

    out_shape = [jax.ShapeDtypeStruct((g.shape[0], h, g.shape[2]), g.dtype) for g, h in zip(grads, halves)]
    out_shape += [jax.ShapeDtypeStruct(w.shape, w.dtype) for w in wholes]
    return pl.pallas_call(
        body, name=name, in_specs=[ANY] * (n + nw), out_specs=[ANY] * (n + nw), out_shape=out_shape,
        scratch_shapes=[pltpu.SemaphoreType.DMA((n + nw,)), pltpu.SemaphoreType.DMA((n + nw,))],
        compiler_params=pltpu.CompilerParams(has_side_effects=True))(*grads, *wholes)


def _scatter_to_chips(halves_in, small, name):
    n = len(halves_in)
    rh = small.shape[0] // 2

    def body(*refs):
        srcs, dsts = refs[:n + 1], refs[n + 1:2 * (n + 1)]
        ssem, rsem = refs[2 * (n + 1):]
        x, y, c, me, chips, sib = _place()
        cps = []
        for a in range(n + 1):
            for k, chip in enumerate(chips):
                src = srcs[a].at[2 * chip[0] + chip[1]] if a < n else srcs[a].at[pl.ds(c * rh, rh)]
                cps.append(_rcopy(src, dsts[a].at[k], ssem.at[3 * a + k], rsem.at[3 * a + k], (*chip, c)))
        for cp in cps:
            cp.start()
        for cp in cps:
            cp.wait_recv()
        for cp in cps:
            cp.wait_send()

    out_shape = [jax.ShapeDtypeStruct((3,) + h.shape[1:], h.dtype) for h in halves_in]
    out_shape.append(jax.ShapeDtypeStruct((3, rh, small.shape[1]), small.dtype))
    return pl.pallas_call(
        body, name=name, in_specs=[ANY] * (n + 1), out_specs=[ANY] * (n + 1), out_shape=out_shape,
        scratch_shapes=[pltpu.SemaphoreType.DMA((3 * (n + 1),)), pltpu.SemaphoreType.DMA((3 * (n + 1),))],
        compiler_params=pltpu.CompilerParams(has_side_effects=True))(*halves_in, small)


def _join_halves(parts, name):
    n = len(parts)

    def body(*refs):
        srcs, dsts = refs[:n], refs[n:2 * n]
        ssem, rsem = refs[2 * n:]
        x, y, c, me, chips, sib = _place()
        cps = []
        for a in range(n):
            h = srcs[a].shape[0] // 2
            cps.append(_rcopy(srcs[a].at[pl.ds(c * h, h)], dsts[a].at[pl.ds(c * h, h)], ssem.at[a], rsem.at[a], sib))
        for cp in cps:
            cp.start()
        for a in range(n):
            h = srcs[a].shape[0] // 2
            theirs = dsts[a].at[pl.ds((1 - c) * h, h)]
            _rcopy(theirs, theirs, ssem.at[a], rsem.at[a], sib).wait_recv()
        for cp in cps:
            cp.wait_send()

    out_shape = [jax.ShapeDtypeStruct(p.shape, p.dtype) for p in parts]
    return pl.pallas_call(
        body, name=name, in_specs=[ANY] * n, out_specs=[ANY] * n, out_shape=out_shape,
        input_output_aliases={a: a for a in range(n)},
        scratch_shapes=[pltpu.SemaphoreType.DMA((n,)), pltpu.SemaphoreType.DMA((n,))],
        compiler_params=pltpu.CompilerParams(has_side_effects=True))(*parts)


def _pad_rows(a, rows):
    return jnp.pad(a, ((0, rows - a.shape[0]), (0, 0)))


def _stack_rows(parts, multiple):
    padded = [_pad_rows(p, -(-p.shape[0] // 8) * 8) for p in parts]
    starts, at = [], 0
    for p in padded:
        starts.append(at)
        at += p.shape[0]
    total = -(-at // multiple) * multiple
    if total > at:
        padded.append(jnp.zeros((total - at, parts[0].shape[1]), parts[0].dtype))
    return jnp.concatenate(padded, axis=0), starts


def kernel(x, ln_pre_even, w_in_even, pool_w, pool_scale, w_out_even, ln_post_even, ln_pre_odd, w_in_odd, sconv_w, dconv_w, dconv_b, cnorm_g, cnorm_b, w_out_odd, ln_post_odd, loss_target, m_ln_pre_even, m_w_in_even, m_pool_w, m_pool_scale, m_w_out_even, m_ln_post_even, m_ln_pre_odd, m_w_in_odd, m_sconv_w, m_dconv_w, m_dconv_b, m_cnorm_g, m_cnorm_b, m_w_out_odd, m_ln_post_odd, v_ln_pre_even, v_w_in_even, v_pool_w, v_pool_scale, v_w_out_even, v_ln_post_even, v_ln_pre_odd, v_w_in_odd, v_sconv_w, v_dconv_w, v_dconv_b, v_cnorm_g, v_cnorm_b, v_w_out_odd, v_ln_post_odd):
    _, s, d = x.shape
    half = d // 2
    cw = half // N_CHIPS
    ng, q, gd = pool_w.shape[1:]
    k3, k31 = sconv_w.shape[1], dconv_w.shape[1]
    x2d, tgt = x[0], loss_target[0]
    me = 2 * lax.axis_index("x") + lax.axis_index("y")
    core = lax.axis_index("c")
    c_arr = jnp.reshape(core, (1,)).astype(jnp.int32)
    me_arr = jnp.reshape(me, (1,)).astype(jnp.int32)
    mc_arr = jnp.stack([me, core]).astype(jnp.int32)

    shards = [w_in_even[0], w_out_even[0], w_in_odd[0], w_out_odd[0]]
    slabs = [_cast_bf16_own_slab(w, me_arr, f"cast_w{n}") for n, w in enumerate(shards)]
    pool_w_b = _cast_bf16(pool_w[0].reshape(ng * q, gd), "cast_pool_w").reshape(ng, q, gd)
    pack_w, at_w = _stack_rows([sconv_w[0], dconv_w[0], dconv_b, cnorm_g, cnorm_b], 8)
    pack_d, at_d = _stack_rows([ln_pre_odd, ln_post_odd], 8)
    win_e, pool_w_f, pack_w_f, pack_d_f = _gather_weights(slabs[:1], pool_w_b, pack_w, pack_d, "gather_first")
    sconv_f = pack_w_f[at_w[0]:at_w[0] + k3]
    dconv_f = pack_w_f[at_w[1]:at_w[1] + k31]
    dconv_b_f, cnorm_g_f, cnorm_b_f = (pack_w_f[at_w[n]:at_w[n] + 1] for n in (2, 3, 4))
    ln_pre_odd_f = pack_d_f[at_d[0]:at_d[0] + 1]
    ln_post_odd_f = pack_d_f[at_d[1]:at_d[1] + 1]

    def reduce_half(g, name):
        (got,) = _swap_with_sibling([g], [], "swap_" + name)
        return _half_add(g, got, c_arr, "half_add_" + name)

    h0 = _rms_fwd(x2d, ln_pre_even, "rms_pre_even")
    plans = _Multi([_GatherPlan([slabs[1]], at=(0.6, 0.88)), _GatherPlan([slabs[2]], (0, 1, 4), at=(0.6, 0.88))])
    p_e, extra = _mm_nn(h0, win_e, "proj_in_even", plans)
    (wout_e,), (win_o,) = plans.results(extra)
    wout_e = wout_e.reshape(d, d)
    att, ltot, (win_o,) = _sba_fwd(p_e, half, "sba_fwd", _GatherPlan([win_o], (1, 4, 4), at=(0.69, 0.94)))
    y_e = _even_mix_fwd(p_e, att, pool_w_f, pool_scale, d, "even_mix_fwd")
    o_e, x1, h1 = _mm_out_even(y_e, wout_e, x2d, ln_post_even, ln_pre_odd_f, "proj_out_even")
    p_o, (wout_o,) = _mm_nn(h1, win_o, "proj_in_odd", _GatherPlan([slabs[3]]))
    wout_o = wout_o.reshape(d, d)
    y_o, s3, d1 = _odd_mix_fwd(p_o, sconv_f, dconv_f, dconv_b_f, cnorm_g_f, cnorm_b_f, d, "odd_mix_fwd")
    do_o, dx2, loss_blk, dln_post_odd = _mm_out_odd(y_o, wout_o, x1, ln_post_odd_f, tgt, "proj_out_odd_loss")

    dy_o = _mm_nt(do_o, wout_o, "dy_odd")
    g_wout_o = _mm_tn(y_o, do_o, 1, "dw_out_odd")[0].reshape(N_CHIPS, d // N_CHIPS, d)
    (dbc, dgate_o, ds3, dd1, dcnorm_g, dcnorm_b, ddconv_b), (got,) = _odd_bwd_rows(
        p_o, s3, d1, dy_o, cnorm_g_f, cnorm_b_f, d, "odd_bwd_rows", _SwapPlan([g_wout_o]))
    h_wout_o = _half_add(g_wout_o, got, c_arr, "half_add_out_odd")
    dhc, dcc, dga, dgb, dsconv, ddconv = _odd_bwd_conv(p_o, ds3, dd1, sconv_f, dconv_f, d, "odd_bwd_conv")
    dp_o = jnp.concatenate([dhc, dbc, dcc, dga, dgb, dgate_o], axis=1)
    g_win_o, (s_wout_o,) = _mm_tn(h1, dp_o, N_CHIPS, "dw_in_odd", _ScatterPlan([h_wout_o]))
    (dx1, dln_pre_odd, do_e, dln_post_even), (got,) = _mm_in_bwd(
        dp_o, win_o, x1, ln_pre_odd_f, dx2, (o_e, ln_post_even), "dx_odd", _SwapPlan([g_win_o]))
    h_win_o = _half_add(g_win_o, got, c_arr, "half_add_in_odd")

    dy_e = _mm_nt(do_e, wout_e, "dy_even")
    g_wout_e = _mm_tn(y_e, do_e, 1, "dw_out_even")[0].reshape(N_CHIPS, d // N_CHIPS, d)
    (datt, du, dgate_e, dpool_scale, dpool_w), (got,) = _even_mix_bwd(
        p_e, att, dy_e, pool_w_f, pool_scale, d, "even_mix_bwd", _SwapPlan([g_wout_e]))
    h_wout_e = _half_add(g_wout_e, got, c_arr, "half_add_out_even")
    two = lambda v: v.reshape(2, half)
    small_parts = [dpool_scale, two(dln_post_even), two(dln_pre_odd), two(dln_post_odd),
                   dsconv, ddconv, ddconv_b, dcnorm_g, dcnorm_b, dpool_w.reshape(gd, half)]
    small, at_s = _stack_rows(small_parts, 16)
    (small1,) = _swap_with_sibling([], [small], "swap_small")
    small2 = _add2(small, small1, "small_add")
    plans = _Multi([_ScatterPlan([h_win_o]), _ShareHalfPlan([small2])])
    dq, dk, dv, extra = _sba_bwd(p_e, ltot, datt, half, "sba_bwd", plans)
    (s_win_o,), (small_got,) = plans.results(extra)
    dp_e = jnp.concatenate([dq, dk, dv, du, dgate_e], axis=1)
    g_win_e, (s_wout_e,) = _mm_tn(h0, dp_e, N_CHIPS, "dw_in_even", _ScatterPlan([h_wout_e]))
    h_win_e = reduce_half(g_win_e, "in_even")
    (grad_x, dln_pre_even), (s_win_e,) = _mm_in_bwd(dp_e, win_e, x2d, ln_pre_even, dx1, None, "dx_even", _ScatterPlan([h_win_e]))

    last, at_l = _stack_rows([two(dln_pre_even), jnp.pad(loss_blk[0:1], ((0, 0), (0, half - LANES)))], 16)
    (last1,) = _swap_with_sibling([], [last], "swap_last")
    last2 = _add2(last, last1, "last_add")
    (last_got,) = _scatter_to_chips([], last2, "scatter_last")
    pairs = [(h_win_e, s_win_e), (h_wout_e, s_wout_e), (h_win_o, s_win_o), (h_wout_o, s_wout_o)]
    parts = [_sum_chips(h, r, mc_arr, f"sum_chips{n}") for n, (h, r) in enumerate(pairs)]
    parts.append(_sum_chips_ordered(small2, small_got, mc_arr, "small_sum"))
    parts.append(_sum_chips_ordered(last2, last_got, mc_arr, "last_sum"))
    gw_in_e, gw_out_e, gw_in_o, gw_out_o, red, red_last = _join_halves(parts, "join_halves")
    loss = red_last[at_l[1], 0]

    def rows(n, cnt):
        return red[at_s[n]:at_s[n] + cnt]

    def mine(a, width):
        return lax.dynamic_slice_in_dim(a, me * width, width, axis=1)

    quarter = d // N_CHIPS
    g_small = {
        "ln_pre_even": red_last[at_l[0]:at_l[0] + 2].reshape(1, d),
        "pool_scale": rows(0, 1),
        "ln_post_even": rows(1, 2).reshape(1, d),
        "ln_pre_odd": mine(rows(2, 2).reshape(1, d), quarter),
        "ln_post_odd": mine(rows(3, 2).reshape(1, d), quarter),
        "sconv_w": mine(rows(4, k3), cw),
        "dconv_w": mine(rows(5, k31), cw),
        "dconv_b": mine(rows(6, 1), cw),
        "cnorm_g": mine(rows(7, 1), cw),
        "cnorm_b": mine(rows(8, 1), cw),
        "pool_w": lax.dynamic_slice_in_dim(rows(9, gd).reshape(ng, gd, gd), me * q, q, axis=1).reshape(ng * q, gd),
    }
    w2d = {
        "ln_pre_even": ln_pre_even, "w_in_even": w_in_even[0], "pool_w": pool_w[0].reshape(ng * q, gd),
        "pool_scale": pool_scale, "w_out_even": w_out_even[0], "ln_post_even": ln_post_even, "ln_pre_odd": ln_pre_odd,
        "w_in_odd": w_in_odd[0], "sconv_w": sconv_w[0], "dconv_w": dconv_w[0], "dconv_b": dconv_b, "cnorm_g": cnorm_g,
        "cnorm_b": cnorm_b, "w_out_odd": w_out_odd[0], "ln_post_odd": ln_post_odd,
    }
    moments = {
        "ln_pre_even": (m_ln_pre_even, v_ln_pre_even), "w_in_even": (m_w_in_even, v_w_in_even),
        "pool_w": (m_pool_w, v_pool_w), "pool_scale": (m_pool_scale, v_pool_scale),
        "w_out_even": (m_w_out_even, v_w_out_even), "ln_post_even": (m_ln_post_even, v_ln_post_even),
        "ln_pre_odd": (m_ln_pre_odd, v_ln_pre_odd), "w_in_odd": (m_w_in_odd, v_w_in_odd),
        "sconv_w": (m_sconv_w, v_sconv_w), "dconv_w": (m_dconv_w, v_dconv_w), "dconv_b": (m_dconv_b, v_dconv_b),
        "cnorm_g": (m_cnorm_g, v_cnorm_g), "cnorm_b": (m_cnorm_b, v_cnorm_b),
        "w_out_odd": (m_w_out_odd, v_w_out_odd), "ln_post_odd": (m_ln_post_odd, v_ln_post_odd),
    }
    g2d = dict(g_small, w_in_even=gw_in_e, w_out_even=gw_out_e, w_in_odd=gw_in_o, w_out_odd=gw_out_o)
    updates = {}
    for name, w in w2d.items():
        m_in, v_in = moments[name]
        updates[name], _ = _adamw(w, g2d[name], m_in.reshape(w.shape), v_in.reshape(w.shape), "adamw_" + name)
    outs = [[u.reshape(moments[name][0].shape) for u in updates[name]] for name in w2d]
    grads_out, deltas, new_m, new_v = zip(*outs)
    return (loss, grad_x.reshape(x.shape), *grads_out, *deltas, *new_m, *new_v)
```

```python
import functools
import math

import jax
import jax.numpy as jnp
from jax import lax
from jax.experimental import pallas as pl
from jax.experimental.pallas import tpu as pltpu

F32 = jnp.float32
BF16 = jnp.bfloat16
EPS = 1e-6
N_CHIPS = 4
VMEM_LIMIT_V7X = 56 << 20
HEAD_DIM = 128
ATT_BLOCK = 256
POOL_WINDOWS = (2, 4, 8, 16)
ROW_TILE = 256
POOL_HALO = 16
CONV_HALO = 32
LANES = 128
ADAM_LR, ADAM_B1, ADAM_B2, ADAM_EPS, ADAM_WD, ADAM_STEP = 0.001, 0.9, 0.999, 1e-08, 0.01, 10
MESH_ID = pl.DeviceIdType.MESH
ANY = pl.BlockSpec(memory_space=pl.ANY)


def _cp(*sem):
    return pltpu.CompilerParams(dimension_semantics=sem or None, vmem_limit_bytes=VMEM_LIMIT_V7X)


def _pick_tile(n, cap):
    best = None
    for t in range(LANES, min(n, cap) + 1, LANES):
        if n % t == 0:
            best = t
    assert best is not None, (n, cap)
    return best


def _sigmoid(x):
    return 1.0 / (1.0 + jnp.exp(-x))


def _silu(x):
    return x * _sigmoid(x)


def _dsilu(x):
    s = _sigmoid(x)
    return s * (1.0 + x * (1.0 - s))


def _log_sigmoid(z):
    return jnp.minimum(z, 0.0) - jnp.log(1.0 + jnp.exp(-jnp.abs(z)))


def _rms_stats(x):
    r = lax.rsqrt(jnp.mean(x * x, axis=-1, keepdims=True) + EPS)
    return x * r, r


def _rms_bwd(dh, xhat, r, g):
    dxh = dh * g
    dx = r * (dxh - xhat * jnp.mean(dxh * xhat, axis=-1, keepdims=True))
    return dx, jnp.sum(dh * xhat, axis=0, keepdims=True)


def _acc_rows(ref, first, val):
    @pl.when(first)
    def _():
        ref[...] = val

    @pl.when(jnp.logical_not(first))
    def _():
        ref[...] += val


def _rcopy(src, dst, ssem, rsem, dev):
    return pltpu.make_async_remote_copy(src_ref=src, dst_ref=dst, send_sem=ssem, recv_sem=rsem,
                                        device_id=dev, device_id_type=MESH_ID)


def _place():
    x, y, c = lax.axis_index("x"), lax.axis_index("y"), lax.axis_index("c")
    chips = [(1 - x, y), (x, 1 - y), (1 - x, 1 - y)]
    return x, y, c, 2 * x + y, chips, (x, y, 1 - c)


class _GatherPlan:
    PER_ARRAY = 7

    def __init__(self, arrays, part=(0, 1, 1), at=(0.5, 0.8)):
        self.operands = list(arrays)
        self.out_shapes = [jax.ShapeDtypeStruct(a.shape, a.dtype) for a in arrays]
        self.aliases = {i: i for i in range(len(arrays))}
        self.nsems = self.PER_ARRAY * len(arrays)
        self.base = 0
        self.halves = [a.shape[1] // 2 for a in arrays]
        self.part = part
        self.at = at

    def schedule(self):
        return [(0.0, self.start), (self.at[0], self.relay), (self.at[1], self.relay_far)]

    def _rows(self, ref, a, chip, half, quarter=None):
        lo, hi, n = self.part
        h = self.halves[a]
        first, size = half * h + lo * h // n, (hi - lo) * h // n
        if quarter is not None:
            first, size = first + quarter * (size // 2), size // 2
        return ref.at[chip, pl.ds(first, size)]

    def _copy(self, src, dst, a, n, ssem, rsem, dev):
        return _rcopy(src, dst, ssem.at[self.base + self.PER_ARRAY * a + n], rsem.at[self.base + self.PER_ARRAY * a + n], dev)

    def _own(self, ins, outs, ssem, rsem):
        x, y, c, me, chips, sib = _place()
        return [self._copy(self._rows(ins[a], a, me, c), self._rows(outs[a], a, me, c), a, k, ssem, rsem, (*chips[k], c))
                for a in range(len(ins)) for k in (0, 1)]

    def _relays(self, outs, ssem, rsem, a, k):
        x, y, c, me, chips, sib = _place()
        chip = 2 * chips[k][0] + chips[k][1]
        whole, quarter = self._rows(outs[a], a, chip, c), self._rows(outs[a], a, chip, c, k)
        return (self._copy(whole, whole, a, k, ssem, rsem, (*chips[k], c)),
                self._copy(quarter, quarter, a, 2 + k, ssem, rsem, (*chips[1 - k], c)),
                self._copy(whole, whole, a, 4 + k, ssem, rsem, sib))

    def _far(self, outs, ssem, rsem, a):
        x, y, c, me, chips, sib = _place()
        chip = 2 * chips[2][0] + chips[2][1]
        whole = self._rows(outs[a], a, chip, c)
        got = [self._copy(q, q, a, 2 + k, ssem, rsem, (*chips[1 - k], c))
               for k, q in enumerate([self._rows(outs[a], a, chip, c, 0), self._rows(outs[a], a, chip, c, 1)])]
        return got, self._copy(whole, whole, a, 6, ssem, rsem, sib)

    def start(self, ins, outs, ssem, rsem):
        for cp in self._own(ins, outs, ssem, rsem):
            cp.start()

    def relay(self, ins, outs, ssem, rsem):
        for a in range(len(outs)):
            for k in (0, 1):
                landed, onward, to_sibling = self._relays(outs, ssem, rsem, a, k)
                landed.wait_recv()
                onward.start()
                to_sibling.start()

    def relay_far(self, ins, outs, ssem, rsem):
        for a in range(len(outs)):
            got, to_sibling = self._far(outs, ssem, rsem, a)
            for cp in got:
                cp.wait_recv()
            to_sibling.start()

    def finish(self, ins, outs, ssem, rsem):
        x, y, c, me, chips, sib = _place()
        for a in range(len(outs)):
            for k in range(3):
                ref = self._rows(outs[a], a, 2 * chips[k][0] + chips[k][1], 1 - c)
                self._copy(ref, ref, a, 4 + k, ssem, rsem, sib).wait_recv()
        for cp in self._own(ins, outs, ssem, rsem):
            cp.wait_send()
        for a in range(len(outs)):
            for k in (0, 1):
                _, onward, to_sibling = self._relays(outs, ssem, rsem, a, k)
                onward.wait_send()
                to_sibling.wait_send()
            self._far(outs, ssem, rsem, a)[1].wait_send()


class _ScatterPlan:
    def __init__(self, arrays, part=(0, 1, 1), into=None):
        self.n = len(arrays)
        self.operands = list(arrays) + list(into or [])
        self.out_shapes = [jax.ShapeDtypeStruct((3,) + a.shape[1:], a.dtype) for a in arrays]
        self.aliases = {self.n + i: i for i in range(self.n)} if into else {}
        self.nsems = 3 * self.n
        self.base = 0
        self.part = part

    def _copies(self, ins, outs, ssem, rsem):
        x, y, c, me, chips, sib = _place()
        lo, hi, n = self.part
        out = []
        for a in range(self.n):
            h = ins[a].shape[1]
            rows = pl.ds(lo * h // n, (hi - lo) * h // n)
            for k, chip in enumerate(chips):
                out.append(_rcopy(ins[a].at[2 * chip[0] + chip[1], rows], outs[a].at[k, rows],
                                  ssem.at[self.base + 3 * a + k], rsem.at[self.base + 3 * a + k], (*chip, c)))
        return out

    def schedule(self):
        return [(0.0, self.start)]

    def start(self, ins, outs, ssem, rsem):
        for cp in self._copies(ins, outs, ssem, rsem):
            cp.start()

    def finish(self, ins, outs, ssem, rsem):
        cps = self._copies(ins, outs, ssem, rsem)
        for cp in cps:
            cp.wait_recv()
        for cp in cps:
            cp.wait_send()


class _ShareHalfPlan(_ScatterPlan):
    def __init__(self, arrays):
        super().__init__(arrays)
        self.out_shapes = [jax.ShapeDtypeStruct((3, a.shape[0] // 2, a.shape[1]), a.dtype) for a in arrays]

    def _copies(self, ins, outs, ssem, rsem):
        x, y, c, me, chips, sib = _place()
        out = []
        for a in range(self.n):
            rh = ins[a].shape[0] // 2
            for k, chip in enumerate(chips):
                out.append(_rcopy(ins[a].at[pl.ds(c * rh, rh)], outs[a].at[k],
                                  ssem.at[self.base + 3 * a + k], rsem.at[self.base + 3 * a + k], (*chip, c)))
        return out


class _SwapPlan:
    def __init__(self, grads):
        self.operands = list(grads)
        self.out_shapes = [jax.ShapeDtypeStruct((g.shape[0], g.shape[1] // 2, g.shape[2]), g.dtype) for g in grads]
        self.aliases = {}
        self.nsems = len(grads)
        self.base = 0

    def _copies(self, ins, outs, ssem, rsem):
        x, y, c, me, chips, sib = _place()
        out = []
        for a, src in enumerate(ins):
            h = src.shape[1] // 2
            out.append(_rcopy(src.at[:, pl.ds((1 - c) * h, h), :], outs[a], ssem.at[self.base + a], rsem.at[self.base + a], sib))
        return out

    def schedule(self):
        return [(0.0, self.start)]

    def start(self, ins, outs, ssem, rsem):
        for cp in self._copies(ins, outs, ssem, rsem):
            cp.start()

    def finish(self, ins, outs, ssem, rsem):
        cps = self._copies(ins, outs, ssem, rsem)
        for cp in cps:
            cp.wait_recv()
        for cp in cps:
            cp.wait_send()


class _Multi:
    def __init__(self, plans):
        self.plans = plans
        self.operands, self.out_shapes, self.aliases, self.nsems = [], [], {}, 0
        self.spans = []
        for p in plans:
            ni, no = len(self.operands), len(self.out_shapes)
            self.spans.append((ni, ni + len(p.operands), no, no + len(p.out_shapes)))
            self.aliases.update({ni + i: no + j for i, j in p.aliases.items()})
            p.base = self.nsems
            self.nsems += p.nsems
            self.operands += p.operands
            self.out_shapes += p.out_shapes

    def schedule(self):
        def bound(fn, span):
            i0, i1, o0, o1 = span
            return lambda ins, outs, ssem, rsem: fn(ins[i0:i1], outs[o0:o1], ssem, rsem)

        stages = [(at, bound(fn, span)) for p, span in zip(self.plans, self.spans) for at, fn in p.schedule()]
        return sorted(stages, key=lambda s: s[0])

    def finish(self, ins, outs, ssem, rsem):
        for p, (i0, i1, o0, o1) in zip(self.plans, self.spans):
            p.finish(ins[i0:i1], outs[o0:o1], ssem, rsem)

    def results(self, extra):
        return [list(extra[o0:o1]) for (_, _, o0, o1) in self.spans]


class _Host:
    def __init__(self, comm, in_specs, out_specs, out_shape, scratch):
        self.comm = comm
        self.n_in, self.n_out = len(in_specs), len(out_specs)
        self.in_specs, self.out_specs, self.out_shape, self.scratch = list(in_specs), list(out_specs), list(out_shape), list(scratch)
        self.aliases = {}
        self.args = []
        if comm is not None:
            self.in_specs += [ANY] * len(comm.operands)
            self.out_specs += [ANY] * len(comm.out_shapes)
            self.out_shape += comm.out_shapes
            self.scratch += [pltpu.SemaphoreType.DMA((comm.nsems,)), pltpu.SemaphoreType.DMA((comm.nsems,))]
            self.aliases = {self.n_in + i: self.n_out + j for i, j in comm.aliases.items()}
            self.args = list(comm.operands)

    def split(self, refs):
        nc = len(self.args)
        nco = len(self.out_shape) - self.n_out
        ins, p = refs[:self.n_in], self.n_in + nc
        outs, rest = refs[p:p + self.n_out], refs[p + self.n_out + nco:]
        self._cargs = None
        if self.comm is not None:
            self._cargs = (refs[self.n_in:p], refs[p + self.n_out:p + self.n_out + nco], rest[-2], rest[-1])
            rest = rest[:-2]
        return ins, outs, rest

    def before(self, step, total):
        if self.comm is None:
            return

        for at, stage in self.comm.schedule():
            pl.when(step == min(total - 1, int(at * total)))(functools.partial(stage, *self._cargs))

    def after(self, step, total):
        if self.comm is None:
            return

        @pl.when(step == total - 1)
        def _():
            self.comm.finish(*self._cargs)

    def results(self, outs):
        return outs[:self.n_out], outs[self.n_out:]


def _cast_bf16(x, name):
    r, c = x.shape
    tr = ROW_TILE if r % ROW_TILE == 0 else r

    def body(x_ref, o_ref):
        o_ref[...] = x_ref[...].astype(BF16)

    return pl.pallas_call(
        body, name=name, grid=(r // tr,),
        in_specs=[pl.BlockSpec((tr, c), lambda i: (i, 0))],
        out_specs=pl.BlockSpec((tr, c), lambda i: (i, 0)),
        out_shape=jax.ShapeDtypeStruct((r, c), BF16), compiler_params=_cp("parallel"))(x)


def _cast_bf16_own_slab(x, me_arr, name):
    r, c = x.shape
    tr = ROW_TILE if r % ROW_TILE == 0 else r

    def body(me_ref, x_ref, o_ref):
        o_ref[...] = x_ref[...].astype(BF16)

    return pl.pallas_call(
        body, name=name,
        grid_spec=pltpu.PrefetchScalarGridSpec(
            num_scalar_prefetch=1, grid=(r // tr,),
            in_specs=[pl.BlockSpec((tr, c), lambda i, me: (i, 0))],
            out_specs=pl.BlockSpec((None, tr, c), lambda i, me: (me[0], i, 0))),
        out_shape=jax.ShapeDtypeStruct((N_CHIPS, r, c), BF16), compiler_params=_cp("parallel"))(me_arr, x)


def _rms_fwd(x, g, name):
    s, d = x.shape

    def body(x_ref, g_ref, h_ref):
        xhat, _ = _rms_stats(x_ref[...])
        h_ref[...] = (xhat * g_ref[...]).astype(BF16)

    return pl.pallas_call(
        body, name=name, grid=(s // ROW_TILE,),
        in_specs=[pl.BlockSpec((ROW_TILE, d), lambda i: (i, 0)), pl.BlockSpec((1, d), lambda i: (0, 0))],
        out_specs=pl.BlockSpec((ROW_TILE, d), lambda i: (i, 0)),
        out_shape=jax.ShapeDtypeStruct((s, d), BF16), compiler_params=_cp("parallel"))(x, g)


def _mm_nn(a, w3, name, comm=None):
    m, k = a.shape
    nsh, _, ns = w3.shape
    tm = 512 if m % 512 == 0 else ROW_TILE
    tn = _pick_tile(ns, 1024)
    per = ns // tn
    grid = (nsh * per, m // tm)
    host = _Host(comm,
                 [pl.BlockSpec((tm, k), lambda n, i: (i, 0)), pl.BlockSpec((None, k, tn), lambda n, i: (n // per, 0, n % per))],
                 [pl.BlockSpec((tm, tn), lambda n, i: (i, n))], [jax.ShapeDtypeStruct((m, nsh * ns), F32)], [])

    def body(*refs):
        (a_ref, w_ref), (o_ref,), _ = host.split(refs)
        step = pl.program_id(0) * grid[1] + pl.program_id(1)
        host.before(step, grid[0] * grid[1])
        o_ref[...] = jnp.dot(a_ref[...], w_ref[...], preferred_element_type=F32)
        host.after(step, grid[0] * grid[1])

    outs = pl.pallas_call(
        body, name=name, grid=grid, in_specs=host.in_specs, out_specs=host.out_specs, out_shape=host.out_shape,
        scratch_shapes=host.scratch, input_output_aliases=host.aliases,
        compiler_params=_cp("arbitrary", "arbitrary"))(a, w3, *host.args)
    (out,), extra = host.results(outs)
    return out, extra


def _mm_nt(a, b, name):
    m, k = a.shape
    n = b.shape[0]
    tm = 512 if m % 512 == 0 else ROW_TILE

    def body(a_ref, b_ref, o_ref):
        o_ref[...] = lax.dot_general(a_ref[...], b_ref[...], (((1,), (1,)), ((), ())), preferred_element_type=F32)

    return pl.pallas_call(
        body, name=name, grid=(m // tm,),
        in_specs=[pl.BlockSpec((tm, k), lambda i: (i, 0)), pl.BlockSpec((n, k), lambda i: (0, 0))],
        out_specs=pl.BlockSpec((tm, n), lambda i: (i, 0)),
        out_shape=jax.ShapeDtypeStruct((m, n), F32), compiler_params=_cp("parallel"))(a, b)


def _mm_tn(a, b, nsh, name, comm=None):
    s, m = a.shape
    n = b.shape[1]
    ns = n // nsh
    tm = 512 if m % 512 == 0 else ROW_TILE
    tn = _pick_tile(ns, 1024)
    per = ns // tn
    grid = (nsh * per, m // tm)
    host = _Host(comm, [pl.BlockSpec((s, tm), lambda j, i: (0, i)), pl.BlockSpec((s, tn), lambda j, i: (0, j))],
                 [pl.BlockSpec((None, tm, tn), lambda j, i: (j // per, i, j % per))],
                 [jax.ShapeDtypeStruct((nsh, m, ns), BF16)], [])

    def body(*refs):
        (a_ref, b_ref), (o_ref,), _ = host.split(refs)
        step = pl.program_id(0) * grid[1] + pl.program_id(1)
        host.before(step, grid[0] * grid[1])
        o_ref[...] = lax.dot_general(a_ref[...], b_ref[...], (((0,), (0,)), ((), ())),
                                     preferred_element_type=F32).astype(BF16)
        host.after(step, grid[0] * grid[1])

    outs = pl.pallas_call(
        body, name=name, grid=grid, in_specs=host.in_specs, out_specs=host.out_specs, out_shape=host.out_shape,
        scratch_shapes=host.scratch, input_output_aliases=host.aliases,
        compiler_params=_cp("arbitrary", "arbitrary"))(a, b, *host.args)
    (out,), extra = host.results(outs)
    return out, extra


def _tri(n, rel):
    row = lax.broadcasted_iota(jnp.int32, (2 * n, n), 0)
    col = lax.broadcasted_iota(jnp.int32, (2 * n, n), 1)
    return jnp.where(rel(jnp.where(row >= n, row - n, row), col), 1.0, 0.0).astype(BF16)


def _dot_split(x, tri2):
    hi = x.astype(BF16)
    lo = (x - hi.astype(F32)).astype(BF16)
    return jnp.dot(jnp.concatenate([hi, lo], axis=1), tri2, preferred_element_type=F32)


def _nt(a, b):
    return lax.dot_general(a, b, (((1,), (1,)), ((), ())), preferred_element_type=F32)


def _tn(a, b):
    return lax.dot_general(a, b, (((0,), (0,)), ((), ())), preferred_element_type=F32)


def _heads_per_step(nh):
    return max(h for h in (1, 2, 4) if nh % h == 0)


def _sba_fwd(p, sbw, name, comm=None):
    s = p.shape[0]
    nh = sbw // HEAD_DIM
    hp = _heads_per_step(nh)
    ngrp, hw = nh // hp, hp * HEAD_DIM
    blk = ATT_BLOCK
    nq = s // blk
    scale = 1.0 / math.sqrt(HEAD_DIM)
    host = _Host(comm,
                 [pl.BlockSpec((blk, hw), lambda g, i: (i, g)),
                  pl.BlockSpec((s, hw), lambda g, i: (0, ngrp + g)),
                  pl.BlockSpec((s, hw), lambda g, i: (0, 2 * ngrp + g))],
                 [pl.BlockSpec((blk, hw), lambda g, i: (i, g))] * 2,
                 [jax.ShapeDtypeStruct((s, sbw), F32)] * 2,
                 [pltpu.VMEM((s, hw), BF16)] * 2)

    def body(*refs):
        (q_ref, k_ref, v_ref), (o_ref, lt_ref), (kb_ref, vb_ref) = host.split(refs)
        i = pl.program_id(1)
        step = pl.program_id(0) * nq + i
        host.before(step, ngrp * nq)

        @pl.when(i == 0)
        def _():
            kb_ref[...] = k_ref[...].astype(BF16)
            vb_ref[...] = v_ref[...].astype(BF16)

        heads = [slice(h * HEAD_DIM, (h + 1) * HEAD_DIM) for h in range(hp)]
        qs = [q_ref[:, hd].astype(BF16) for hd in heads]
        later = _tri(blk, lambda r, c: r > c)
        causal = lax.broadcasted_iota(jnp.int32, (blk, blk), 1) < lax.broadcasted_iota(jnp.int32, (blk, blk), 0)

        def key_block(j, carry, diagonal):
            rows = pl.ds(pl.multiple_of(j * blk, blk), blk)
            hs = range(hp)
            z = [_nt(qs[h], kb_ref[rows, heads[h]]) * scale for h in hs]
            ls = [_log_sigmoid(z[h]) for h in hs]
            lm = [jnp.where(causal, ls[h] - z[h], 0.0) if diagonal else ls[h] - z[h] for h in hs]
            stay = [_dot_split(lm[h], later) for h in hs]
            w = [jnp.exp(ls[h] + stay[h] + carry[h][1]) for h in hs]
            if diagonal:
                w = [jnp.where(causal, w[h], 0.0) for h in hs]
            acc = [carry[h][0] + jnp.dot(w[h].astype(BF16), vb_ref[rows, heads[h]], preferred_element_type=F32) for h in hs]
            return tuple((acc[h], carry[h][1] + jnp.sum(lm[h], axis=1, keepdims=True)) for h in hs)

        init = tuple((jnp.zeros((blk, HEAD_DIM), F32), jnp.zeros((blk, 1), F32)) for _ in heads)
        carry = key_block(i, init, True)
        carry = lax.fori_loop(0, i, lambda n, c: key_block(i - 1 - n, c, False), carry)
        for h, hd in enumerate(heads):
            o_ref[:, hd] = carry[h][0]
            lt_ref[:, hd] = jnp.broadcast_to(carry[h][1], (blk, HEAD_DIM))
        host.after(step, ngrp * nq)

    outs = pl.pallas_call(
        body, name=name, grid=(ngrp, nq), in_specs=host.in_specs, out_specs=host.out_specs, out_shape=host.out_shape,
        scratch_shapes=host.scratch, input_output_aliases=host.aliases,
        compiler_params=_cp("arbitrary", "arbitrary"))(p, p, p, *host.args)
    (out, ltot), extra = host.results(outs)
    return out, ltot, extra


def _sba_bwd(p, ltot, dout, sbw, name, comm=None):
    s = p.shape[0]
    nh = sbw // HEAD_DIM
    hp = _heads_per_step(nh)
    ngrp, hw = nh // hp, hp * HEAD_DIM
    blk = ATT_BLOCK
    nq = s // blk
    scale = 1.0 / math.sqrt(HEAD_DIM)
    blk_spec = pl.BlockSpec((blk, hw), lambda g, i: (i, g))
    col_spec = pl.BlockSpec((s, hw), lambda g, i: (0, g))
    host = _Host(comm,
                 [blk_spec, pl.BlockSpec((s, hw), lambda g, i: (0, ngrp + g)),
                  pl.BlockSpec((s, hw), lambda g, i: (0, 2 * ngrp + g)), blk_spec, blk_spec],
                 [blk_spec, col_spec, col_spec], [jax.ShapeDtypeStruct((s, sbw), BF16)] * 3,
                 [pltpu.VMEM((s, hw), BF16)] * 2 + [pltpu.VMEM((s, hw), F32)] * 2)

    def body(*refs):
        (q_ref, k_ref, v_ref, lt_ref, do_ref), (dq_ref, dk_ref, dv_ref), (kb_ref, vb_ref, dka_ref, dva_ref) = host.split(refs)
        i = pl.program_id(1)
        step = pl.program_id(0) * nq + i
        host.before(step, ngrp * nq)

        @pl.when(i == 0)
        def _():
            kb_ref[...] = k_ref[...].astype(BF16)
            vb_ref[...] = v_ref[...].astype(BF16)
            dka_ref[...] = jnp.zeros_like(dka_ref)
            dva_ref[...] = jnp.zeros_like(dva_ref)

        heads = [slice(h * HEAD_DIM, (h + 1) * HEAD_DIM) for h in range(hp)]
        qs = [q_ref[:, hd].astype(BF16) for hd in heads]
        dos = [do_ref[:, hd].astype(BF16) for hd in heads]
        ltots = [lt_ref[:, h * HEAD_DIM:h * HEAD_DIM + 1] for h in range(hp)]
        upto = _tri(blk, lambda r, c: r <= c)
        before = _tri(blk, lambda r, c: r < c)
        causal = lax.broadcasted_iota(jnp.int32, (blk, blk), 1) < lax.broadcasted_iota(jnp.int32, (blk, blk), 0)

        def key_block(j, carry, diagonal):
            rows = pl.ds(pl.multiple_of(j * blk, blk), blk)
            hs = range(hp)
            kj = [kb_ref[rows, heads[h]] for h in hs]
            vj = [vb_ref[rows, heads[h]] for h in hs]
            z = [_nt(qs[h], kj[h]) * scale for h in hs]
            dw = [_nt(dos[h], vj[h]) for h in hs]
            ls = [_log_sigmoid(z[h]) for h in hs]
            lm = [jnp.where(causal, ls[h] - z[h], 0.0) if diagonal else ls[h] - z[h] for h in hs]
            stay = [ltots[h] - carry[h][1] - _dot_split(lm[h], upto) for h in hs]
            w = [jnp.exp(ls[h] + stay[h]) for h in hs]
            if diagonal:
                w = [jnp.where(causal, w[h], 0.0) for h in hs]
            da = [dw[h] * w[h] for h in hs]
            sig = [jnp.exp(ls[h]) for h in hs]
            chain = [sig[h] * (carry[h][2] + _dot_split(da[h], before)) for h in hs]
            if diagonal:
                chain = [jnp.where(causal, chain[h], 0.0) for h in hs]
            dzb = [((da[h] * (1.0 - sig[h]) - chain[h]) * scale).astype(BF16) for h in hs]
            dq = [carry[h][0] + jnp.dot(dzb[h], kj[h], preferred_element_type=F32) for h in hs]
            for h in hs:
                dka_ref[rows, heads[h]] += _tn(dzb[h], qs[h])
            for h in hs:
                dva_ref[rows, heads[h]] += _tn(w[h].astype(BF16), dos[h])
            return tuple((dq[h], carry[h][1] + jnp.sum(lm[h], axis=1, keepdims=True),
                          carry[h][2] + jnp.sum(da[h], axis=1, keepdims=True)) for h in hs)

        zero = jnp.zeros((blk, 1), F32)
        init = tuple((jnp.zeros((blk, HEAD_DIM), F32), zero, zero) for _ in heads)
        carry = lax.fori_loop(0, i, lambda j, c: key_block(j, c, False), init)
        carry = key_block(i, carry, True)
        for h, hd in enumerate(heads):
            dq_ref[:, hd] = carry[h][0].astype(BF16)

        @pl.when(i == nq - 1)
        def _():
            dk_ref[...] = dka_ref[...].astype(BF16)
            dv_ref[...] = dva_ref[...].astype(BF16)

        host.after(step, ngrp * nq)

    outs = pl.pallas_call(
        body, name=name, grid=(ngrp, nq), in_specs=host.in_specs, out_specs=host.out_specs, out_shape=host.out_shape,
        scratch_shapes=host.scratch, input_output_aliases=host.aliases,
        compiler_params=_cp("arbitrary", "arbitrary"))(p, p, p, ltot, dout, *host.args)
    (dq, dk, dv), extra = host.results(outs)
    return dq, dk, dv, extra


def _pool_groups(pad_ref, tile, row0, gd, halo):
    row = row0 + lax.broadcasted_iota(jnp.int32, (tile, 1), 0)
    out = []
    for gi, win in enumerate(POOL_WINDOWS):
        cs = slice(gi * gd, (gi + 1) * gd)
        tok = pad_ref[halo:halo + tile, cs]
        acc = tok
        for j in range(1, win):
            acc = acc + pad_ref[halo - j:halo - j + tile, cs]
        cnt = jnp.minimum(win, row + 1).astype(F32)
        out.append(acc / cnt - tok)
    return out


def _even_mix_fwd(p, att, pool_w, pool_scale, d, name):
    s = p.shape[0]
    half = d // 2
    gd = half // len(POOL_WINDOWS)
    t, hb = ROW_TILE, POOL_HALO

    def body(u_ref, uh_ref, g_ref, a_ref, pw_ref, sc_ref, y_ref, pad_ref):
        i = pl.program_id(0)
        pad_ref[0:hb, :] = jnp.where(i > 0, uh_ref[...], 0.0)
        pad_ref[hb:, :] = u_ref[...]
        pooled = _pool_groups(pad_ref, t, i * t, gd, hb)
        for gi in range(len(POOL_WINDOWS)):
            cs = slice(gi * gd, (gi + 1) * gd)
            po = jnp.dot(pooled[gi].astype(BF16), pw_ref[gi], preferred_element_type=F32) * sc_ref[:, cs]
            y_ref[:, half + gi * gd:half + (gi + 1) * gd] = (po * _silu(g_ref[:, half + gi * gd:half + (gi + 1) * gd])).astype(BF16)
        y_ref[:, :half] = (a_ref[...] * _silu(g_ref[:, :half])).astype(BF16)

    return pl.pallas_call(
        body, name=name, grid=(s // t,),
        in_specs=[pl.BlockSpec((t, half), lambda i: (i, 3)),
                  pl.BlockSpec((hb, half), lambda i: (jnp.maximum(i * (t // hb) - 1, 0), 3)),
                  pl.BlockSpec((t, d), lambda i: (i, 2)),
                  pl.BlockSpec((t, half), lambda i: (i, 0)),
                  pl.BlockSpec(pool_w.shape, lambda i: (0, 0, 0)),
                  pl.BlockSpec((1, half), lambda i: (0, 0))],
        out_specs=pl.BlockSpec((t, d), lambda i: (i, 0)),
        out_shape=jax.ShapeDtypeStruct((s, d), BF16),
        scratch_shapes=[pltpu.VMEM((hb + t, half), F32)],
        compiler_params=_cp("parallel"))(p, p, p, att, pool_w, pool_scale)


def _even_mix_bwd(p, att, dy, pool_w, pool_scale, d, name, comm=None):
    s = p.shape[0]
    half = d // 2
    ng = len(POOL_WINDOWS)
    gd = half // ng
    t, hb = ROW_TILE, POOL_HALO
    nt = s // t
    host = _Host(
        comm,
        [pl.BlockSpec((t, half), lambda i: (i, 3)),
         pl.BlockSpec((hb, half), lambda i: (jnp.maximum(i * (t // hb) - 1, 0), 3)),
         pl.BlockSpec((t, d), lambda i: (i, 2)),
         pl.BlockSpec((hb, half), lambda i: (jnp.minimum((i + 1) * (t // hb), s // hb - 1), 5)),
         pl.BlockSpec((t, half), lambda i: (i, 0)),
         pl.BlockSpec((t, d), lambda i: (i, 0)),
         pl.BlockSpec((hb, half), lambda i: (jnp.minimum((i + 1) * (t // hb), s // hb - 1), 1)),
         pl.BlockSpec(pool_w.shape, lambda i: (0, 0, 0)),
         pl.BlockSpec((1, half), lambda i: (0, 0))],
        [pl.BlockSpec((t, half), lambda i: (i, 0)),
         pl.BlockSpec((t, half), lambda i: (i, 0)),
         pl.BlockSpec((t, d), lambda i: (i, 0)),
         pl.BlockSpec((1, half), lambda i: (0, 0)),
         pl.BlockSpec((ng, gd, gd), lambda i: (0, 0, 0))],
        [jax.ShapeDtypeStruct((s, half), F32), jax.ShapeDtypeStruct((s, half), BF16),
         jax.ShapeDtypeStruct((s, d), BF16), jax.ShapeDtypeStruct((1, half), F32),
         jax.ShapeDtypeStruct((ng, gd, gd), F32)],
        [pltpu.VMEM((hb + t, half), F32), pltpu.VMEM((t + hb, half), F32)])

    def body(*refs):
        ((u_ref, uh_ref, g_ref, gh_ref, a_ref, dy_ref, dyh_ref, pw_ref, sc_ref),
         (da_ref, du_ref, dg_ref, dsc_ref, dpw_ref), (pad_ref, dn_ref)) = host.split(refs)
        i = pl.program_id(0)
        host.before(i, nt)
        first = i == 0
        pad_ref[0:hb, :] = jnp.where(i > 0, uh_ref[...], 0.0)
        pad_ref[hb:, :] = u_ref[...]
        pooled = _pool_groups(pad_ref, t, i * t, gd, hb)
        g1 = g_ref[:, :half]
        dy1 = dy_ref[:, :half]
        da_ref[...] = dy1 * _silu(g1)
        dg_ref[:, :half] = (dy1 * a_ref[...] * _dsilu(g1)).astype(BF16)
        row = i * t + lax.broadcasted_iota(jnp.int32, (t + hb, 1), 0)
        for gi, win in enumerate(POOL_WINDOWS):
            cs = slice(gi * gd, (gi + 1) * gd)
            cs2 = slice(half + gi * gd, half + (gi + 1) * gd)
            w = pw_ref[gi]
            pb = pooled[gi].astype(BF16)
            zp = jnp.dot(pb, w, preferred_element_type=F32)
            g2 = g_ref[:, cs2]
            dy2 = dy_ref[:, cs2]
            dg_ref[:, cs2] = (dy2 * zp * sc_ref[:, cs] * _dsilu(g2)).astype(BF16)
            dpo = dy2 * _silu(g2)
            _acc_rows(dsc_ref.at[:, cs], first, jnp.sum(dpo * zp, axis=0, keepdims=True))
            dz = (dpo * sc_ref[:, cs]).astype(BF16)
            _acc_rows(dpw_ref.at[gi], first, _tn(pb, dz))
            dzh = jnp.where(i < nt - 1, dyh_ref[:, cs] * _silu(gh_ref[:, cs]) * sc_ref[:, cs], 0.0).astype(BF16)
            dpool = _nt(dz, w)
            dpool_h = _nt(dzh, w)
            cnt = jnp.minimum(win, row + 1).astype(F32)
            dn_ref[0:t, cs] = dpool / cnt[0:t]
            dn_ref[t:, cs] = dpool_h / cnt[t:]
            acc = dn_ref[0:t, cs]
            for j in range(1, win):
                acc = acc + dn_ref[j:j + t, cs]
            du_ref[:, cs] = (acc - dpool).astype(BF16)
        host.after(i, nt)

    outs = pl.pallas_call(
        body, name=name, grid=(nt,), in_specs=host.in_specs, out_specs=host.out_specs, out_shape=host.out_shape,
        scratch_shapes=host.scratch, input_output_aliases=host.aliases,
        compiler_params=_cp("arbitrary"))(p, p, p, p, att, dy, dy, pool_w, pool_scale, *host.args)
    return host.results(outs)


def _mm_out_even(y, w, x, g_post, g_pre_next, name):
    s, k = y.shape
    d = w.shape[1]
    t = ROW_TILE

    def body(y_ref, w_ref, x_ref, gp_ref, gn_ref, o_ref, x1_ref, h1_ref):
        for r0 in range(0, t, t // 2):
            rows = slice(r0, r0 + t // 2)
            o = jnp.dot(y_ref[rows, :], w_ref[...], preferred_element_type=F32)
            o_ref[rows, :] = o
            ohat, _ = _rms_stats(o)
            x1 = x_ref[rows, :] + ohat * gp_ref[...]
            x1_ref[rows, :] = x1
            xhat, _ = _rms_stats(x1)
            h1_ref[rows, :] = (xhat * gn_ref[...]).astype(BF16)

    row = lambda c: pl.BlockSpec((t, c), lambda i: (i, 0))
    vec = pl.BlockSpec((1, d), lambda i: (0, 0))
    return pl.pallas_call(
        body, name=name, grid=(s // t,),
        in_specs=[row(k), pl.BlockSpec((k, d), lambda i: (0, 0)), row(d), vec, vec],
        out_specs=[row(d), row(d), row(d)],
        out_shape=[jax.ShapeDtypeStruct((s, d), F32), jax.ShapeDtypeStruct((s, d), F32),
                   jax.ShapeDtypeStruct((s, d), BF16)],
        compiler_params=_cp("parallel"))(y, w, x, g_post, g_pre_next)


def _mm_out_odd(y, w, x1, g_post, target, name):
    s, k = y.shape
    d = w.shape[1]
    t = ROW_TILE

    def body(y_ref, w_ref, x_ref, gp_ref, tg_ref, do_ref, dx_ref, loss_ref, dgp_ref):
        first = pl.program_id(0) == 0
        gp = gp_ref[...]
        part = dgp = None
        for r0 in range(0, t, t // 2):
            rows = slice(r0, r0 + t // 2)
            o = jnp.dot(y_ref[rows, :], w_ref[...], preferred_element_type=F32)
            ohat, r = _rms_stats(o)
            diff = x_ref[rows, :] + ohat * gp - tg_ref[rows, :]
            part_half = 0.5 * jnp.sum(jnp.mean(diff * diff, axis=-1, keepdims=True), axis=0, keepdims=True)
            dx2 = diff * (1.0 / d)
            dx_ref[rows, :] = dx2
            do, dgp_half = _rms_bwd(dx2, ohat, r, gp)
            do_ref[rows, :] = do.astype(BF16)
            part = part_half if part is None else part + part_half
            dgp = dgp_half if dgp is None else dgp + dgp_half
        _acc_rows(loss_ref, first, jnp.broadcast_to(part, loss_ref.shape))
        _acc_rows(dgp_ref, first, dgp)

    row = lambda c: pl.BlockSpec((t, c), lambda i: (i, 0))
    vec = pl.BlockSpec((1, d), lambda i: (0, 0))
    return pl.pallas_call(
        body, name=name, grid=(s // t,),
        in_specs=[row(k), pl.BlockSpec((k, d), lambda i: (0, 0)), row(d), vec, row(d)],
        out_specs=[row(d), row(d), pl.BlockSpec((8, LANES), lambda i: (0, 0)), vec],
        out_shape=[jax.ShapeDtypeStruct((s, d), BF16), jax.ShapeDtypeStruct((s, d), F32),
                   jax.ShapeDtypeStruct((8, LANES), F32), jax.ShapeDtypeStruct((1, d), F32)],
        compiler_params=_cp("arbitrary"))(y, w, x1, g_post, target)


def _layer_norm(d1, cg, cb):
    mu = jnp.mean(d1, axis=-1, keepdims=True)
    cen = d1 - mu
    rstd = lax.rsqrt(jnp.mean(cen * cen, axis=-1, keepdims=True) + EPS)
    n = cen * rstd
    return n, rstd, n * cg + cb


SUBLANES = 8
ROW_STRIP = 64
GATHER_PIECES = 8
CONV_ROWS = 64


def _make_shifts(pad_ref, cs, sh_ref):
    rows = sh_ref.shape[1]
    for r in range(1, SUBLANES):
        sh_ref[r - 1] = pad_ref[r:r + rows, cs]


def _by_shift(taps, base, sign=1):
    return sorted(range(taps), key=lambda k: ((sign * (base + k)) % SUBLANES, k))


def _window(pad_ref, cs, sh_ref, off, t):
    m, r = divmod(off, SUBLANES)
    if r == 0:
        return pad_ref[SUBLANES * m:SUBLANES * m + t, cs]
    return sh_ref[r - 1, SUBLANES * m:SUBLANES * m + t, :]


def _odd_mix_fwd(p, sconv_w, dconv_w, dconv_b, cnorm_g, cnorm_b, d, name):
    s = p.shape[0]
    w = d // 2
    k3, k31 = sconv_w.shape[0], dconv_w.shape[0]
    t, hb = ROW_TILE, CONV_HALO
    assert hb >= k31 - 1 and w % LANES == 0

    def body(p_ref, ph_ref, w3_ref, w31_ref, b31_ref, cg_ref, cb_ref, y_ref, s3_ref, d1_ref, mpad, dpad, sh_ref):
        i = pl.program_id(0)
        mpad[0:hb, :] = jnp.where(i > 0, ph_ref[:, 2 * w:3 * w] * ph_ref[:, 0:w], 0.0)
        mpad[hb:, :] = p_ref[:, 2 * w:3 * w] * p_ref[:, 0:w]
        dpad[0:hb, :] = jnp.where(i > 0, ph_ref[:, 3 * w:4 * w] * _sigmoid(ph_ref[:, 4 * w:5 * w]), 0.0)
        dpad[hb:, :] = p_ref[:, 3 * w:4 * w] * _sigmoid(p_ref[:, 4 * w:5 * w])
        for c0 in range(0, w, LANES):
            cs = slice(c0, c0 + LANES)
            acc = jnp.zeros((t, LANES), F32)
            for kk in range(k3):
                acc = acc + w3_ref[kk:kk + 1, cs] * mpad[hb - (k3 - 1) + kk:hb - (k3 - 1) + kk + t, cs]
            s3_ref[:, cs] = acc
            _make_shifts(dpad, cs, sh_ref)
            for r0 in range(0, t, CONV_ROWS):
                acc = jnp.zeros((CONV_ROWS, LANES), F32)
                for kk in _by_shift(k31, hb - (k31 - 1)):
                    acc = acc + w31_ref[kk:kk + 1, cs] * _window(dpad, cs, sh_ref, hb - (k31 - 1) + kk + r0, CONV_ROWS)
                d1_ref[r0:r0 + CONV_ROWS, cs] = acc + b31_ref[:, cs]
        _, _, d2 = _layer_norm(d1_ref[...], cg_ref[...], cb_ref[...])
        y_ref[:, :w] = (p_ref[:, w:2 * w] * s3_ref[...] * _silu(p_ref[:, 5 * w:6 * w])).astype(BF16)
        y_ref[:, w:] = (_silu(d2) * _silu(p_ref[:, 6 * w:7 * w])).astype(BF16)

    row = lambda c: pl.BlockSpec((t, c), lambda i: (i, 0))
    full = lambda a: pl.BlockSpec(a.shape, lambda i: (0, 0))
    return pl.pallas_call(
        body, name=name, grid=(s // t,),
        in_specs=[row(7 * w),
                  pl.BlockSpec((hb, 5 * w), lambda i: (jnp.maximum(i * (t // hb) - 1, 0), 0)),
                  full(sconv_w), full(dconv_w), full(dconv_b), full(cnorm_g), full(cnorm_b)],
        out_specs=[row(d), row(w), row(w)],
        out_shape=[jax.ShapeDtypeStruct((s, d), BF16), jax.ShapeDtypeStruct((s, w), F32),
                   jax.ShapeDtypeStruct((s, w), F32)],
        scratch_shapes=[pltpu.VMEM((hb + t, w), F32)] * 2 + [pltpu.VMEM((SUBLANES - 1, hb + t - SUBLANES, LANES), F32)],
        compiler_params=_cp("parallel"))(p, p, sconv_w, dconv_w, dconv_b, cnorm_g, cnorm_b)


def _odd_bwd_rows(p, s3, d1, dy, cnorm_g, cnorm_b, d, name, comm=None):
    s = p.shape[0]
    w = d // 2
    t = ROW_TILE
    col = lambda j: pl.BlockSpec((t, w), lambda i: (i, j))
    row = lambda c: pl.BlockSpec((t, c), lambda i: (i, 0))
    vec = pl.BlockSpec((1, w), lambda i: (0, 0))
    host = _Host(comm, [col(1), col(5), col(6), row(w), row(w), row(d), vec, vec],
                 [row(w), row(d), row(w), row(w), vec, vec, vec],
                 [jax.ShapeDtypeStruct((s, w), BF16), jax.ShapeDtypeStruct((s, d), BF16),
                  jax.ShapeDtypeStruct((s, w), F32), jax.ShapeDtypeStruct((s, w), F32)] + [jax.ShapeDtypeStruct((1, w), F32)] * 3, [])

    def body(*refs):
        ((bc_ref, g1_ref, g2_ref, s3_ref, d1_ref, dy_ref, cg_ref, cb_ref),
         (dbc_ref, dg_ref, ds3_ref, dd1_ref, dcg_ref, dcb_ref, db_ref), _) = host.split(refs)
        step = pl.program_id(0)
        host.before(step, s // t)
        first = step == 0

        def strip(j, sums):
            rows = slice(j * ROW_STRIP, (j + 1) * ROW_STRIP)
            g1, g2 = g1_ref[rows, :], g2_ref[rows, :]
            bc, s3v = bc_ref[rows, :], s3_ref[rows, :]
            dy1, dy2 = dy_ref[rows, :w], dy_ref[rows, w:]
            n, rstd, d2 = _layer_norm(d1_ref[rows, :], cg_ref[...], cb_ref[...])
            dg_ref[rows, :w] = (dy1 * bc * s3v * _dsilu(g1)).astype(BF16)
            dg_ref[rows, w:] = (dy2 * _silu(d2) * _dsilu(g2)).astype(BF16)
            dco = dy1 * _silu(g1)
            dbc_ref[rows, :] = (dco * s3v).astype(BF16)
            ds3_ref[rows, :] = dco * bc
            dd2 = dy2 * _silu(g2) * _dsilu(d2)
            dn = dd2 * cg_ref[...]
            dd1 = rstd * (dn - jnp.mean(dn, axis=-1, keepdims=True) - n * jnp.mean(dn * n, axis=-1, keepdims=True))
            dd1_ref[rows, :] = dd1
            dcb, dcg, db = sums
            return (dcb + jnp.sum(dd2, axis=0, keepdims=True), dcg + jnp.sum(dd2 * n, axis=0, keepdims=True),
                    db + jnp.sum(dd1, axis=0, keepdims=True))

        zero = jnp.zeros((1, w), F32)
        sums = (zero, zero, zero)
        for j in range(t // ROW_STRIP):
            sums = strip(j, sums)
        dcb, dcg, db = sums
        _acc_rows(dcb_ref, first, dcb)
        _acc_rows(dcg_ref, first, dcg)
        _acc_rows(db_ref, first, db)
        host.after(step, s // t)

    outs = pl.pallas_call(
        body, name=name, grid=(s // t,), in_specs=host.in_specs, out_specs=host.out_specs, out_shape=host.out_shape,
        scratch_shapes=host.scratch, input_output_aliases=host.aliases,
        compiler_params=_cp("arbitrary"))(p, p, p, s3, d1, dy, cnorm_g, cnorm_b, *host.args)
    return host.results(outs)


def _odd_bwd_conv(p, ds3, dd1, sconv_w, dconv_w, d, name):
    s = p.shape[0]
    w = d // 2
    k3, k31 = sconv_w.shape[0], dconv_w.shape[0]
    t, hb, ha = ROW_TILE, CONV_HALO, 8
    nt = s // t
    assert hb >= k31 - 1 and ha >= k3 - 1

    def body(hc_ref, cc_ref, ga_ref, gb_ref, hch_ref, cch_ref, gah_ref, gbh_ref, ds3_ref, ds3h_ref, dd1_ref, dd1h_ref,
             w3_ref, w31_ref, dhc_ref, dcc_ref, dga_ref, dgb_ref, dw3_ref, dw31_ref, mpad, dpad, s3pad, d1pad, sh_ref):
        i = pl.program_id(0)
        first = i == 0
        last = i == nt - 1
        mpad[0:hb, :] = jnp.where(i > 0, cch_ref[...] * hch_ref[...], 0.0)
        mpad[hb:, :] = cc_ref[...] * hc_ref[...]
        dpad[0:hb, :] = jnp.where(i > 0, gah_ref[...] * _sigmoid(gbh_ref[...]), 0.0)
        dpad[hb:, :] = ga_ref[...] * _sigmoid(gb_ref[...])
        s3pad[0:t, :] = ds3_ref[...]
        s3pad[t:, :] = jnp.where(last, 0.0, ds3h_ref[...])
        d1pad[0:t, :] = dd1_ref[...]
        d1pad[t:, :] = jnp.where(last, 0.0, dd1h_ref[...])

        @pl.when(first)
        def _():
            dw3_ref[...] = jnp.zeros_like(dw3_ref)
            dw31_ref[...] = jnp.zeros_like(dw31_ref)

        def fold(v):
            return jnp.sum(v.reshape(v.shape[0] // SUBLANES, SUBLANES, LANES), axis=0)

        groups = range(0, t, CONV_ROWS)
        for c0 in range(0, w, LANES):
            cs = slice(c0, c0 + LANES)
            ds3v = s3pad[0:t, cs]
            dm = jnp.zeros((t, LANES), F32)
            for kk in range(k3):
                dm = dm + w3_ref[kk:kk + 1, cs] * s3pad[k3 - 1 - kk:k3 - 1 - kk + t, cs]
                off = hb - (k3 - 1) + kk
                dw3_ref[SUBLANES * kk:SUBLANES * (kk + 1), cs] += fold(ds3v * mpad[off:off + t, cs])
            dcc_ref[:, cs] = (dm * hc_ref[:, cs]).astype(BF16)
            dhc_ref[:, cs] = (dm * cc_ref[:, cs]).astype(BF16)
            _make_shifts(d1pad, cs, sh_ref)
            for r0 in groups:
                rows = slice(r0, r0 + CONV_ROWS)
                dd0 = jnp.zeros((CONV_ROWS, LANES), F32)
                for kk in _by_shift(k31, -(k31 - 1), -1):
                    dd0 = dd0 + w31_ref[kk:kk + 1, cs] * _window(d1pad, cs, sh_ref, k31 - 1 - kk + r0, CONV_ROWS)
                sgb = _sigmoid(gb_ref[rows, cs])
                dga_ref[rows, cs] = (dd0 * sgb).astype(BF16)
                dgb_ref[rows, cs] = (dd0 * ga_ref[rows, cs] * sgb * (1.0 - sgb)).astype(BF16)
            _make_shifts(dpad, cs, sh_ref)
            for kk in _by_shift(k31, hb - (k31 - 1)):
                part = jnp.zeros((SUBLANES, LANES), F32)
                for r0 in groups:
                    part = part + fold(d1pad[r0:r0 + CONV_ROWS, cs]
                                       * _window(dpad, cs, sh_ref, hb - (k31 - 1) + kk + r0, CONV_ROWS))
                dw31_ref[SUBLANES * kk:SUBLANES * (kk + 1), cs] += part

    col = lambda j: pl.BlockSpec((t, w), lambda i: (i, j))
    pre = lambda j: pl.BlockSpec((hb, w), lambda i: (jnp.maximum(i * (t // hb) - 1, 0), j))
    row = pl.BlockSpec((t, w), lambda i: (i, 0))
    post = lambda h: pl.BlockSpec((h, w), lambda i: (jnp.minimum((i + 1) * (t // h), s // h - 1), 0))
    full = lambda a: pl.BlockSpec(a.shape, lambda i: (0, 0))
    dhc, dcc, dga, dgb, dw3, dw31 = pl.pallas_call(
        body, name=name, grid=(nt,),
        in_specs=[col(0), col(2), col(3), col(4), pre(0), pre(2), pre(3), pre(4),
                  row, post(ha), row, post(hb), full(sconv_w), full(dconv_w)],
        out_specs=[row, row, row, row, pl.BlockSpec((SUBLANES * k3, w), lambda i: (0, 0)),
                   pl.BlockSpec((SUBLANES * k31, w), lambda i: (0, 0))],
        out_shape=[jax.ShapeDtypeStruct((s, w), BF16)] * 4
        + [jax.ShapeDtypeStruct((SUBLANES * k3, w), F32), jax.ShapeDtypeStruct((SUBLANES * k31, w), F32)],
        scratch_shapes=[pltpu.VMEM((hb + t, w), F32)] * 2 + [pltpu.VMEM((t + ha, w), F32), pltpu.VMEM((t + hb, w), F32),
                                                             pltpu.VMEM((SUBLANES - 1, hb + t - SUBLANES, LANES), F32)],
        compiler_params=_cp("arbitrary"))(p, p, p, p, p, p, p, p, ds3, ds3, dd1, dd1, sconv_w, dconv_w)
    return dhc, dcc, dga, dgb, jnp.sum(dw3.reshape(k3, SUBLANES, w), axis=1), jnp.sum(dw31.reshape(k31, SUBLANES, w), axis=1)


def _mm_in_bwd(dp, w3, x, g_pre, dres, post, name, comm=None):
    s = dp.shape[0]
    nsh, d, ns = w3.shape
    t = 512 if s % 512 == 0 else ROW_TILE
    nt = s // t
    ks = 2 if (ns // 2) % LANES == 0 else 1
    nk, kw = nsh * ks, ns // ks
    chunk = 128
    nchunk = t // chunk
    row = pl.BlockSpec((t, d), lambda i, k: (i, 0))
    vec = pl.BlockSpec((1, d), lambda i, k: (0, 0))
    rowwise = [x, dres] + ([post[0]] if post is not None else [])
    in_specs = [pl.BlockSpec((t, kw), lambda i, k: (i, k)), pl.BlockSpec((None, d, kw), lambda i, k: (k // ks, 0, k % ks)), vec]
    out_specs = [row, vec]
    out_shape = [jax.ShapeDtypeStruct((s, d), F32), jax.ShapeDtypeStruct((1, d), F32)]
    args = [dp, w3, g_pre]
    if post is not None:
        in_specs += [vec]
        out_specs += [row, vec]
        out_shape += [jax.ShapeDtypeStruct((s, d), BF16), jax.ShapeDtypeStruct((1, d), F32)]
        args += [post[1]]
    n_blocked = len(in_specs)
    in_specs += [ANY] * len(rowwise)
    args += rowwise
    host = _Host(comm, in_specs, out_specs, out_shape,
                 [pltpu.VMEM((t, d), F32), pltpu.VMEM((len(rowwise), 2, chunk, d), F32), pltpu.SemaphoreType.DMA((len(rowwise), 2))])

    def body(*refs):
        ins, outs, (acc_ref, buf_ref, sem_ref) = host.split(refs)
        dp_ref, w_ref, g_ref = ins[:3]
        hbm = ins[n_blocked:]
        dx_ref, dg_ref = outs[:2]
        tile = pl.program_id(0)
        kk = pl.program_id(1)
        first = tile == 0
        step = tile * nk + kk
        host.before(step, nt * nk)
        part = _nt(dp_ref[...], w_ref[...])

        @pl.when(kk == 0)
        def _():
            acc_ref[...] = part

        @pl.when(kk > 0)
        def _():
            acc_ref[...] += part

        def fetch(ci, slot):
            return [pltpu.make_async_copy(src.at[pl.ds(tile * t + ci * chunk, chunk)], buf_ref.at[n, slot], sem_ref.at[n, slot])
                    for n, src in enumerate(hbm)]

        @pl.when(kk == nk - 1)
        def _():
            dg = dgp = None
            for cp in fetch(0, 0):
                cp.start()
            for ci in range(nchunk):
                slot = ci % 2
                if ci + 1 < nchunk:
                    for cp in fetch(ci + 1, 1 - slot):
                        cp.start()
                for cp in fetch(ci, slot):
                    cp.wait()
                rows = slice(ci * chunk, (ci + 1) * chunk)
                xhat, r = _rms_stats(buf_ref[0, slot])
                dxn, dg_part = _rms_bwd(acc_ref[rows, :], xhat, r, g_ref[...])
                dx = buf_ref[1, slot] + dxn
                dx_ref[rows, :] = dx
                dg = dg_part if dg is None else dg + dg_part
                if post is not None:
                    ohat, ro = _rms_stats(buf_ref[2, slot])
                    do, dgp_part = _rms_bwd(dx, ohat, ro, ins[3][...])
                    outs[2][rows, :] = do.astype(BF16)
                    dgp = dgp_part if dgp is None else dgp + dgp_part
            _acc_rows(dg_ref, first, dg)
            if post is not None:
                _acc_rows(outs[3], first, dgp)

        host.after(step, nt * nk)

    res = pl.pallas_call(
        body, name=name, grid=(nt, nk), in_specs=host.in_specs, out_specs=host.out_specs, out_shape=host.out_shape,
        scratch_shapes=host.scratch, input_output_aliases=host.aliases,
        compiler_params=_cp("arbitrary", "arbitrary"))(*args, *host.args)
    return host.results(res)


def _half_add(g, r1, c_arr, name):
    nsh, rows, ns = g.shape
    h = rows // 2
    tr = min(ROW_TILE, h)
    per = h // tr

    def body(c_ref, g_ref, r_ref, o_ref):
        o_ref[...] = (g_ref[...].astype(F32) + r_ref[...].astype(F32)).astype(BF16)

    spec = pl.BlockSpec((None, tr, ns), lambda s, r, c: (s, r, 0))
    return pl.pallas_call(
        body, name=name,
        grid_spec=pltpu.PrefetchScalarGridSpec(
            num_scalar_prefetch=1, grid=(nsh, per),
            in_specs=[pl.BlockSpec((None, tr, ns), lambda s, r, c: (s, c[0] * per + r, 0)), spec], out_specs=spec),
        out_shape=jax.ShapeDtypeStruct((nsh, h, ns), BF16), compiler_params=_cp("parallel", "parallel"))(c_arr, g, r1)


def _sum_chips(hh, r2, mc_arr, name):
    _, h, ns = hh.shape
    tr = min(ROW_TILE, h)
    per = h // tr

    def body(mc_ref, h_ref, a_ref, b_ref, c_ref, o_ref):
        o_ref[...] = ((h_ref[...].astype(F32) + a_ref[...].astype(F32)) + b_ref[...].astype(F32)) + c_ref[...].astype(F32)

    got = lambda k: pl.BlockSpec((None, tr, ns), lambda r, mc: (k, r, 0))
    return pl.pallas_call(
        body, name=name,
        grid_spec=pltpu.PrefetchScalarGridSpec(
            num_scalar_prefetch=1, grid=(per,),
            in_specs=[pl.BlockSpec((None, tr, ns), lambda r, mc: (mc[0], r, 0)), got(0), got(1), got(2)],
            out_specs=pl.BlockSpec((tr, ns), lambda r, mc: (mc[1] * per + r, 0))),
        out_shape=jax.ShapeDtypeStruct((2 * h, ns), F32), compiler_params=_cp("parallel"))(mc_arr, hh, r2, r2, r2)


def _add2(a, b, name):
    def body(a_ref, b_ref, o_ref):
        o_ref[...] = a_ref[...] + b_ref[...]

    return pl.pallas_call(body, name=name, out_shape=jax.ShapeDtypeStruct(a.shape, a.dtype), compiler_params=_cp())(a, b)


def _sum_chips_ordered(s2, r2, mc_arr, name):
    rows, w = s2.shape
    rh = rows // 2

    def body(mc_ref, s_ref, a_ref, b_ref, c_ref, o_ref):
        me = mc_ref[0]
        acc = None
        for j in range(N_CHIPS):
            rel = jnp.bitwise_xor(me, j)
            v = jnp.where(rel == 0, s_ref[...], jnp.where(rel == 2, a_ref[...], jnp.where(rel == 1, b_ref[...], c_ref[...])))
            acc = v if acc is None else acc + v
        o_ref[...] = acc

    got = lambda k: pl.BlockSpec((None, rh, w), lambda i, mc: (k, 0, 0))
    return pl.pallas_call(
        body, name=name,
        grid_spec=pltpu.PrefetchScalarGridSpec(
            num_scalar_prefetch=1, grid=(1,),
            in_specs=[pl.BlockSpec((rh, w), lambda i, mc: (mc[1], 0)), got(0), got(1), got(2)],
            out_specs=pl.BlockSpec((rh, w), lambda i, mc: (mc[1], 0))),
        out_shape=jax.ShapeDtypeStruct((rows, w), F32), compiler_params=_cp("arbitrary"))(mc_arr, s2, r2, r2, r2)


def _adamw(w, g, m, v, name, comm=None):
    r, c = w.shape
    tr = ROW_TILE if r % ROW_TILE == 0 else r
    c1 = 1.0 / (1.0 - ADAM_B1 ** ADAM_STEP)
    c2 = 1.0 / (1.0 - ADAM_B2 ** ADAM_STEP)
    spec = pl.BlockSpec((tr, c), lambda i: (i, 0))
    host = _Host(comm, [spec] * 4, [spec] * 4, [jax.ShapeDtypeStruct((r, c), F32)] * 4, [])

    def body(*refs):
        (w_ref, g_ref, m_ref, v_ref), (go_ref, d_ref, nm_ref, nv_ref), _ = host.split(refs)
        step = pl.program_id(0)
        host.before(step, r // tr)
        gv = g_ref[...]
        go_ref[...] = gv
        nm = ADAM_B1 * m_ref[...] + (1.0 - ADAM_B1) * gv
        nv = ADAM_B2 * v_ref[...] + (1.0 - ADAM_B2) * (gv * gv)
        nm_ref[...] = nm
        nv_ref[...] = nv
        d_ref[...] = -ADAM_LR * ((nm * c1) / (jnp.sqrt(nv * c2) + ADAM_EPS) + ADAM_WD * w_ref[...])
        host.after(step, r // tr)

    outs = pl.pallas_call(
        body, name=name, grid=(r // tr,), in_specs=host.in_specs, out_specs=host.out_specs, out_shape=host.out_shape,
        scratch_shapes=host.scratch, input_output_aliases=host.aliases,
        compiler_params=_cp("arbitrary"))(w, g, m, v, *host.args)
    return host.results(outs)


def _gather_weights(bigs, pool_w, pack_w, pack_d, name):
    nb = len(bigs)
    smalls = [pool_w, pack_w, pack_d]
    q, cw, cd = pool_w.shape[1], pack_w.shape[1], pack_d.shape[1]
    pieces = [_GatherPlan(bigs, (j, j + 1, GATHER_PIECES)) for j in range(GATHER_PIECES)]
    for j, piece in enumerate(pieces):
        piece.base = 9 + j * piece.nsems

    def body(*refs):
        srcs, dsts = refs[:nb + 3], refs[nb + 3:2 * (nb + 3)]
        ssem, rsem, lsem = refs[2 * (nb + 3):]
        x, y, c, me, chips, sib = _place()

        def small_dst(n, chip):
            if n == 0:
                return dsts[nb].at[:, pl.ds(chip * q, q), :]
            return dsts[nb + n].at[:, pl.ds(chip * (cw if n == 1 else cd), cw if n == 1 else cd)]

        local = [pltpu.make_async_copy(srcs[nb + n], small_dst(n, me), lsem.at[n]) for n in range(3)]
        for cp in local:
            cp.start()
        sends = []
        for n in range(3):
            for k, chip in enumerate(chips):
                cp = _rcopy(srcs[nb + n], small_dst(n, me), ssem.at[3 * n + k], rsem.at[3 * n + k], (*chip, c))
                cp.start()
                sends.append(cp)
        big = (srcs[:nb], dsts[:nb], ssem, rsem)
        for stage in ("start", "relay", "relay_far", "finish"):
            for piece in pieces:
                getattr(piece, stage)(*big)
        for n in range(3):
            for k, chip in enumerate(chips):
                ref = small_dst(n, 2 * chip[0] + chip[1])
                _rcopy(ref, ref, ssem.at[3 * n + k], rsem.at[3 * n + k], (*chip, c)).wait_recv()
        for cp in sends:
            cp.wait_send()
        for cp in local:
            cp.wait()

    nsem = 9 + sum(piece.nsems for piece in pieces)
    out_shape = [jax.ShapeDtypeStruct(b.shape, b.dtype) for b in bigs]
    out_shape += [jax.ShapeDtypeStruct((pool_w.shape[0], N_CHIPS * q, pool_w.shape[2]), pool_w.dtype),
                  jax.ShapeDtypeStruct((pack_w.shape[0], N_CHIPS * cw), pack_w.dtype),
                  jax.ShapeDtypeStruct((pack_d.shape[0], N_CHIPS * cd), pack_d.dtype)]
    return pl.pallas_call(
        body, name=name, in_specs=[ANY] * (nb + 3), out_specs=[ANY] * (nb + 3), out_shape=out_shape,
        input_output_aliases={a: a for a in range(nb)},
        scratch_shapes=[pltpu.SemaphoreType.DMA((nsem,)), pltpu.SemaphoreType.DMA((nsem,)), pltpu.SemaphoreType.DMA((3,))],
        compiler_params=pltpu.CompilerParams(has_side_effects=True))(*bigs, *smalls)


def _swap_with_sibling(grads, wholes, name):
    n, nw = len(grads), len(wholes)
    halves = [g.shape[1] // 2 for g in grads]

    def body(*refs):
        srcs, dsts = refs[:n + nw], refs[n + nw:2 * (n + nw)]
        ssem, rsem = refs[2 * (n + nw):]
        x, y, c, me, chips, sib = _place()
        cps = [_rcopy(srcs[a].at[:, pl.ds((1 - c) * halves[a], halves[a]), :], dsts[a], ssem.at[a], rsem.at[a], sib)
               for a in range(n)]
        cps += [_rcopy(srcs[a], dsts[a], ssem.at[a], rsem.at[a], sib) for a in range(n, n + nw)]
        for cp in cps:
            cp.start()
        for cp in cps:
            cp.wait_recv()
        for cp in cps:
            cp.wait_send()

    out_shape = [jax.ShapeDtypeStruct((g.shape[0], h, g.shape[2]), g.dtype) for g, h in zip(grads, halves)]
    out_shape += [jax.ShapeDtypeStruct(w.shape, w.dtype) for w in wholes]
    return pl.pallas_call(
        body, name=name, in_specs=[ANY] * (n + nw), out_specs=[ANY] * (n + nw), out_shape=out_shape,
        scratch_shapes=[pltpu.SemaphoreType.DMA((n + nw,)), pltpu.SemaphoreType.DMA((n + nw,))],
        compiler_params=pltpu.CompilerParams(has_side_effects=True))(*grads, *wholes)


def _scatter_to_chips(halves_in, small, name):
    n = len(halves_in)
    rh = small.shape[0] // 2

    def body(*refs):
        srcs, dsts = refs[:n + 1], refs[n + 1:2 * (n + 1)]
        ssem, rsem = refs[2 * (n + 1):]
        x, y, c, me, chips, sib = _place()
        cps = []
        for a in range(n + 1):
            for k, chip in enumerate(chips):
                src = srcs[a].at[2 * chip[0] + chip[1]] if a < n else srcs[a].at[pl.ds(c * rh, rh)]
                cps.append(_rcopy(src, dsts[a].at[k], ssem.at[3 * a + k], rsem.at[3 * a + k], (*chip, c)))
        for cp in cps:
            cp.start()
        for cp in cps:
            cp.wait_recv()
        for cp in cps:
            cp.wait_send()

    out_shape = [jax.ShapeDtypeStruct((3,) + h.shape[1:], h.dtype) for h in halves_in]
    out_shape.append(jax.ShapeDtypeStruct((3, rh, small.shape[1]), small.dtype))
    return pl.pallas_call(
        body, name=name, in_specs=[ANY] * (n + 1), out_specs=[ANY] * (n + 1), out_shape=out_shape,
        scratch_shapes=[pltpu.SemaphoreType.DMA((3 * (n + 1),)), pltpu.SemaphoreType.DMA((3 * (n + 1),))],
        compiler_params=pltpu.CompilerParams(has_side_effects=True))(*halves_in, small)


def _join_halves(parts, name):
    n = len(parts)

    def body(*refs):
        srcs, dsts = refs[:n], refs[n:2 * n]
        ssem, rsem = refs[2 * n:]
        x, y, c, me, chips, sib = _place()
        cps = []
        for a in range(n):
            h = srcs[a].shape[0] // 2
            cps.append(_rcopy(srcs[a].at[pl.ds(c * h, h)], dsts[a].at[pl.ds(c * h, h)], ssem.at[a], rsem.at[a], sib))
        for cp in cps:
            cp.start()
        for a in range(n):
            h = srcs[a].shape[0] // 2
            theirs = dsts[a].at[pl.ds((1 - c) * h, h)]
            _rcopy(theirs, theirs, ssem.at[a], rsem.at[a], sib).wait_recv()
        for cp in cps:
            cp.wait_send()

    out_shape = [jax.ShapeDtypeStruct(p.shape, p.dtype) for p in parts]
    return pl.pallas_call(
        body, name=name, in_specs=[ANY] * n, out_specs=[ANY] * n, out_shape=out_shape,
        input_output_aliases={a: a for a in range(n)},
        scratch_shapes=[pltpu.SemaphoreType.DMA((n,)), pltpu.SemaphoreType.DMA((n,))],
        compiler_params=pltpu.CompilerParams(has_side_effects=True))(*parts)


def _pad_rows(a, rows):
    return jnp.pad(a, ((0, rows - a.shape[0]), (0, 0)))


def _stack_rows(parts, multiple):
    padded = [_pad_rows(p, -(-p.shape[0] // 8) * 8) for p in parts]
    starts, at = [], 0
    for p in padded:
        starts.append(at)
        at += p.shape[0]
    total = -(-at // multiple) * multiple
    if total > at:
        padded.append(jnp.zeros((total - at, parts[0].shape[1]), parts[0].dtype))
    return jnp.concatenate(padded, axis=0), starts


def kernel(x, ln_pre_even, w_in_even, pool_w, pool_scale, w_out_even, ln_post_even, ln_pre_odd, w_in_odd, sconv_w, dconv_w, dconv_b, cnorm_g, cnorm_b, w_out_odd, ln_post_odd, loss_target, m_ln_pre_even, m_w_in_even, m_pool_w, m_pool_scale, m_w_out_even, m_ln_post_even, m_ln_pre_odd, m_w_in_odd, m_sconv_w, m_dconv_w, m_dconv_b, m_cnorm_g, m_cnorm_b, m_w_out_odd, m_ln_post_odd, v_ln_pre_even, v_w_in_even, v_pool_w, v_pool_scale, v_w_out_even, v_ln_post_even, v_ln_pre_odd, v_w_in_odd, v_sconv_w, v_dconv_w, v_dconv_b, v_cnorm_g, v_cnorm_b, v_w_out_odd, v_ln_post_odd):
    _, s, d = x.shape
    half = d // 2
    cw = half // N_CHIPS
    ng, q, gd = pool_w.shape[1:]
    k3, k31 = sconv_w.shape[1], dconv_w.shape[1]
    x2d, tgt = x[0], loss_target[0]
    me = 2 * lax.axis_index("x") + lax.axis_index("y")
    core = lax.axis_index("c")
    c_arr = jnp.reshape(core, (1,)).astype(jnp.int32)
    me_arr = jnp.reshape(me, (1,)).astype(jnp.int32)
    mc_arr = jnp.stack([me, core]).astype(jnp.int32)

    shards = [w_in_even[0], w_out_even[0], w_in_odd[0], w_out_odd[0]]
    slabs = [_cast_bf16_own_slab(w, me_arr, f"cast_w{n}") for n, w in enumerate(shards)]
    pool_w_b = _cast_bf16(pool_w[0].reshape(ng * q, gd), "cast_pool_w").reshape(ng, q, gd)
    pack_w, at_w = _stack_rows([sconv_w[0], dconv_w[0], dconv_b, cnorm_g, cnorm_b], 8)
    pack_d, at_d = _stack_rows([ln_pre_odd, ln_post_odd], 8)
    win_e, pool_w_f, pack_w_f, pack_d_f = _gather_weights(slabs[:1], pool_w_b, pack_w, pack_d, "gather_first")
    sconv_f = pack_w_f[at_w[0]:at_w[0] + k3]
    dconv_f = pack_w_f[at_w[1]:at_w[1] + k31]
    dconv_b_f, cnorm_g_f, cnorm_b_f = (pack_w_f[at_w[n]:at_w[n] + 1] for n in (2, 3, 4))
    ln_pre_odd_f = pack_d_f[at_d[0]:at_d[0] + 1]
    ln_post_odd_f = pack_d_f[at_d[1]:at_d[1] + 1]

    def reduce_half(g, name):
        (got,) = _swap_with_sibling([g], [], "swap_" + name)
        return _half_add(g, got, c_arr, "half_add_" + name)

    h0 = _rms_fwd(x2d, ln_pre_even, "rms_pre_even")
    plans = _Multi([_GatherPlan([slabs[1]], at=(0.6, 0.88)), _GatherPlan([slabs[2]], (0, 1, 4), at=(0.6, 0.88))])
    p_e, extra = _mm_nn(h0, win_e, "proj_in_even", plans)
    (wout_e,), (win_o,) = plans.results(extra)
    wout_e = wout_e.reshape(d, d)
    att, ltot, (win_o,) = _sba_fwd(p_e, half, "sba_fwd", _GatherPlan([win_o], (1, 4, 4), at=(0.69, 0.94)))
    y_e = _even_mix_fwd(p_e, att, pool_w_f, pool_scale, d, "even_mix_fwd")
    o_e, x1, h1 = _mm_out_even(y_e, wout_e, x2d, ln_post_even, ln_pre_odd_f, "proj_out_even")
    p_o, (wout_o,) = _mm_nn(h1, win_o, "proj_in_odd", _GatherPlan([slabs[3]]))
    wout_o = wout_o.reshape(d, d)
    y_o, s3, d1 = _odd_mix_fwd(p_o, sconv_f, dconv_f, dconv_b_f, cnorm_g_f, cnorm_b_f, d, "odd_mix_fwd")
    do_o, dx2, loss_blk, dln_post_odd = _mm_out_odd(y_o, wout_o, x1, ln_post_odd_f, tgt, "proj_out_odd_loss")

    dy_o = _mm_nt(do_o, wout_o, "dy_odd")
    g_wout_o = _mm_tn(y_o, do_o, 1, "dw_out_odd")[0].reshape(N_CHIPS, d // N_CHIPS, d)
    (dbc, dgate_o, ds3, dd1, dcnorm_g, dcnorm_b, ddconv_b), (got,) = _odd_bwd_rows(
        p_o, s3, d1, dy_o, cnorm_g_f, cnorm_b_f, d, "odd_bwd_rows", _SwapPlan([g_wout_o]))
    h_wout_o = _half_add(g_wout_o, got, c_arr, "half_add_out_odd")
    dhc, dcc, dga, dgb, dsconv, ddconv = _odd_bwd_conv(p_o, ds3, dd1, sconv_f, dconv_f, d, "odd_bwd_conv")
    dp_o = jnp.concatenate([dhc, dbc, dcc, dga, dgb, dgate_o], axis=1)
    g_win_o, (s_wout_o,) = _mm_tn(h1, dp_o, N_CHIPS, "dw_in_odd", _ScatterPlan([h_wout_o]))
    (dx1, dln_pre_odd, do_e, dln_post_even), (got,) = _mm_in_bwd(
        dp_o, win_o, x1, ln_pre_odd_f, dx2, (o_e, ln_post_even), "dx_odd", _SwapPlan([g_win_o]))
    h_win_o = _half_add(g_win_o, got, c_arr, "half_add_in_odd")

    dy_e = _mm_nt(do_e, wout_e, "dy_even")
    g_wout_e = _mm_tn(y_e, do_e, 1, "dw_out_even")[0].reshape(N_CHIPS, d // N_CHIPS, d)
    (datt, du, dgate_e, dpool_scale, dpool_w), (got,) = _even_mix_bwd(
        p_e, att, dy_e, pool_w_f, pool_scale, d, "even_mix_bwd", _SwapPlan([g_wout_e]))
    h_wout_e = _half_add(g_wout_e, got, c_arr, "half_add_out_even")
    two = lambda v: v.reshape(2, half)
    small_parts = [dpool_scale, two(dln_post_even), two(dln_pre_odd), two(dln_post_odd),
                   dsconv, ddconv, ddconv_b, dcnorm_g, dcnorm_b, dpool_w.reshape(gd, half)]
    small, at_s = _stack_rows(small_parts, 16)
    (small1,) = _swap_with_sibling([], [small], "swap_small")
    small2 = _add2(small, small1, "small_add")
    plans = _Multi([_ScatterPlan([h_win_o]), _ShareHalfPlan([small2])])
    dq, dk, dv, extra = _sba_bwd(p_e, ltot, datt, half, "sba_bwd", plans)
    (s_win_o,), (small_got,) = plans.results(extra)
    dp_e = jnp.concatenate([dq, dk, dv, du, dgate_e], axis=1)
    g_win_e, (s_wout_e,) = _mm_tn(h0, dp_e, N_CHIPS, "dw_in_even", _ScatterPlan([h_wout_e]))
    h_win_e = reduce_half(g_win_e, "in_even")
    (grad_x, dln_pre_even), (s_win_e,) = _mm_in_bwd(dp_e, win_e, x2d, ln_pre_even, dx1, None, "dx_even", _ScatterPlan([h_win_e]))

    last, at_l = _stack_rows([two(dln_pre_even), jnp.pad(loss_blk[0:1], ((0, 0), (0, half - LANES)))], 16)
    (last1,) = _swap_with_sibling([], [last], "swap_last")
    last2 = _add2(last, last1, "last_add")
    (last_got,) = _scatter_to_chips([], last2, "scatter_last")
    pairs = [(h_win_e, s_win_e), (h_wout_e, s_wout_e), (h_win_o, s_win_o), (h_wout_o, s_wout_o)]
    parts = [_sum_chips(h, r, mc_arr, f"sum_chips{n}") for n, (h, r) in enumerate(pairs)]
    parts.append(_sum_chips_ordered(small2, small_got, mc_arr, "small_sum"))
    parts.append(_sum_chips_ordered(last2, last_got, mc_arr, "last_sum"))
    gw_in_e, gw_out_e, gw_in_o, gw_out_o, red, red_last = _join_halves(parts, "join_halves")
    loss = red_last[at_l[1], 0]

    def rows(n, cnt):
        return red[at_s[n]:at_s[n] + cnt]

    def mine(a, width):
        return lax.dynamic_slice_in_dim(a, me * width, width, axis=1)

    quarter = d // N_CHIPS
    g_small = {
        "ln_pre_even": red_last[at_l[0]:at_l[0] + 2].reshape(1, d),
        "pool_scale": rows(0, 1),
        "ln_post_even": rows(1, 2).reshape(1, d),
        "ln_pre_odd": mine(rows(2, 2).reshape(1, d), quarter),
        "ln_post_odd": mine(rows(3, 2).reshape(1, d), quarter),
        "sconv_w": mine(rows(4, k3), cw),
        "dconv_w": mine(rows(5, k31), cw),
        "dconv_b": mine(rows(6, 1), cw),
        "cnorm_g": mine(rows(7, 1), cw),
        "cnorm_b": mine(rows(8, 1), cw),
        "pool_w": lax.dynamic_slice_in_dim(rows(9, gd).reshape(ng, gd, gd), me * q, q, axis=1).reshape(ng * q, gd),
    }
    w2d = {
        "ln_pre_even": ln_pre_even, "w_in_even": w_in_even[0], "pool_w": pool_w[0].reshape(ng * q, gd),
        "pool_scale": pool_scale, "w_out_even": w_out_even[0], "ln_post_even": ln_post_even, "ln_pre_odd": ln_pre_odd,
        "w_in_odd": w_in_odd[0], "sconv_w": sconv_w[0], "dconv_w": dconv_w[0], "dconv_b": dconv_b, "cnorm_g": cnorm_g,
        "cnorm_b": cnorm_b, "w_out_odd": w_out_odd[0], "ln_post_odd": ln_post_odd,
    }
    moments = {
        "ln_pre_even": (m_ln_pre_even, v_ln_pre_even), "w_in_even": (m_w_in_even, v_w_in_even),
        "pool_w": (m_pool_w, v_pool_w), "pool_scale": (m_pool_scale, v_pool_scale),
        "w_out_even": (m_w_out_even, v_w_out_even), "ln_post_even": (m_ln_post_even, v_ln_post_even),
        "ln_pre_odd": (m_ln_pre_odd, v_ln_pre_odd), "w_in_odd": (m_w_in_odd, v_w_in_odd),
        "sconv_w": (m_sconv_w, v_sconv_w), "dconv_w": (m_dconv_w, v_dconv_w), "dconv_b": (m_dconv_b, v_dconv_b),
        "cnorm_g": (m_cnorm_g, v_cnorm_g), "cnorm_b": (m_cnorm_b, v_cnorm_b),
        "w_out_odd": (m_w_out_odd, v_w_out_odd), "ln_post_odd": (m_ln_post_odd, v_ln_post_odd),
    }
    g2d = dict(g_small, w_in_even=gw_in_e, w_out_even=gw_out_e, w_in_odd=gw_in_o, w_out_odd=gw_out_o)
    updates = {}
    for name, w in w2d.items():
        m_in, v_in = moments[name]
        updates[name], _ = _adamw(w, g2d[name], m_in.reshape(w.shape), v_in.reshape(w.shape), "adamw_" + name)
    outs = [[u.reshape(moments[name][0].shape) for u in updates[name]] for name in w2d]
    grads_out, deltas, new_m, new_v = zip(*outs)
    return (loss, grad_x.reshape(x.shape), *grads_out, *deltas, *new_m, *new_v)
```

```python
import functools
import math

import jax
import jax.numpy as jnp
from jax import lax
from jax.experimental import pallas as pl
from jax.experimental.pallas import tpu as pltpu

F32 = jnp.float32
BF16 = jnp.bfloat16
EPS = 1e-6
N_CHIPS = 4
VMEM_LIMIT_V7X = 56 << 20
HEAD_DIM = 128
ATT_BLOCK = 256
POOL_WINDOWS = (2, 4, 8, 16)
ROW_TILE = 256
POOL_HALO = 16
CONV_HALO = 32
LANES = 128
ADAM_LR, ADAM_B1, ADAM_B2, ADAM_EPS, ADAM_WD, ADAM_STEP = 0.001, 0.9, 0.999, 1e-08, 0.01, 10
MESH_ID = pl.DeviceIdType.MESH
ANY = pl.BlockSpec(memory_space=pl.ANY)


def _cp(*sem):
    return pltpu.CompilerParams(dimension_semantics=sem or None, vmem_limit_bytes=VMEM_LIMIT_V7X)


def _pick_tile(n, cap):
    best = None
    for t in range(LANES, min(n, cap) + 1, LANES):
        if n % t == 0:
            best = t
    assert best is not None, (n, cap)
    return best


def _sigmoid(x):
    return 1.0 / (1.0 + jnp.exp(-x))


def _silu(x):
    return x * _sigmoid(x)


def _dsilu(x):
    s = _sigmoid(x)
    return s * (1.0 + x * (1.0 - s))


def _log_sigmoid(z):
    return jnp.minimum(z, 0.0) - jnp.log(1.0 + jnp.exp(-jnp.abs(z)))


def _rms_stats(x):
    r = lax.rsqrt(jnp.mean(x * x, axis=-1, keepdims=True) + EPS)
    return x * r, r


def _rms_bwd(dh, xhat, r, g):
    dxh = dh * g
    dx = r * (dxh - xhat * jnp.mean(dxh * xhat, axis=-1, keepdims=True))
    return dx, jnp.sum(dh * xhat, axis=0, keepdims=True)


def _acc_rows(ref, first, val):
    @pl.when(first)
    def _():
        ref[...] = val

    @pl.when(jnp.logical_not(first))
    def _():
        ref[...] += val


def _rcopy(src, dst, ssem, rsem, dev):
    return pltpu.make_async_remote_copy(src_ref=src, dst_ref=dst, send_sem=ssem, recv_sem=rsem,
                                        device_id=dev, device_id_type=MESH_ID)


def _place():
    x, y, c = lax.axis_index("x"), lax.axis_index("y"), lax.axis_index("c")
    chips = [(1 - x, y), (x, 1 - y), (1 - x, 1 - y)]
    return x, y, c, 2 * x + y, chips, (x, y, 1 - c)


class _GatherPlan:
    PER_ARRAY = 7

    def __init__(self, arrays, part=(0, 1, 1), at=(0.5, 0.8)):
        self.operands = list(arrays)
        self.out_shapes = [jax.ShapeDtypeStruct(a.shape, a.dtype) for a in arrays]
        self.aliases = {i: i for i in range(len(arrays))}
        self.nsems = self.PER_ARRAY * len(arrays)
        self.base = 0
        self.halves = [a.shape[1] // 2 for a in arrays]
        self.part = part
        self.at = at

    def schedule(self):
        return [(0.0, self.start), (self.at[0], self.relay), (self.at[1], self.relay_far)]

    def _rows(self, ref, a, chip, half, quarter=None):
        lo, hi, n = self.part
        h = self.halves[a]
        first, size = half * h + lo * h // n, (hi - lo) * h // n
        if quarter is not None:
            first, size = first + quarter * (size // 2), size // 2
        return ref.at[chip, pl.ds(first, size)]

    def _copy(self, src, dst, a, n, ssem, rsem, dev):
        return _rcopy(src, dst, ssem.at[self.base + self.PER_ARRAY * a + n], rsem.at[self.base + self.PER_ARRAY * a + n], dev)

    def _own(self, ins, outs, ssem, rsem):
        x, y, c, me, chips, sib = _place()
        return [self._copy(self._rows(ins[a], a, me, c), self._rows(outs[a], a, me, c), a, k, ssem, rsem, (*chips[k], c))
                for a in range(len(ins)) for k in (0, 1)]

    def _relays(self, outs, ssem, rsem, a, k):
        x, y, c, me, chips, sib = _place()
        chip = 2 * chips[k][0] + chips[k][1]
        whole, quarter = self._rows(outs[a], a, chip, c), self._rows(outs[a], a, chip, c, k)
        return (self._copy(whole, whole, a, k, ssem, rsem, (*chips[k], c)),
                self._copy(quarter, quarter, a, 2 + k, ssem, rsem, (*chips[1 - k], c)),
                self._copy(whole, whole, a, 4 + k, ssem, rsem, sib))

    def _far(self, outs, ssem, rsem, a):
        x, y, c, me, chips, sib = _place()
        chip = 2 * chips[2][0] + chips[2][1]
        whole = self._rows(outs[a], a, chip, c)
        got = [self._copy(q, q, a, 2 + k, ssem, rsem, (*chips[1 - k], c))
               for k, q in enumerate([self._rows(outs[a], a, chip, c, 0), self._rows(outs[a], a, chip, c, 1)])]
        return got, self._copy(whole, whole, a, 6, ssem, rsem, sib)

    def start(self, ins, outs, ssem, rsem):
        for cp in self._own(ins, outs, ssem, rsem):
            cp.start()

    def relay(self, ins, outs, ssem, rsem):
        for a in range(len(outs)):
            for k in (0, 1):
                landed, onward, to_sibling = self._relays(outs, ssem, rsem, a, k)
                landed.wait_recv()
                onward.start()
                to_sibling.start()

    def relay_far(self, ins, outs, ssem, rsem):
        for a in range(len(outs)):
            got, to_sibling = self._far(outs, ssem, rsem, a)
            for cp in got:
                cp.wait_recv()
            to_sibling.start()

    def finish(self, ins, outs, ssem, rsem):
        x, y, c, me, chips, sib = _place()
        for a in range(len(outs)):
            for k in range(3):
                ref = self._rows(outs[a], a, 2 * chips[k][0] + chips[k][1], 1 - c)
                self._copy(ref, ref, a, 4 + k, ssem, rsem, sib).wait_recv()
        for cp in self._own(ins, outs, ssem, rsem):
            cp.wait_send()
        for a in range(len(outs)):
            for k in (0, 1):
                _, onward, to_sibling = self._relays(outs, ssem, rsem, a, k)
                onward.wait_send()
                to_sibling.wait_send()
            self._far(outs, ssem, rsem, a)[1].wait_send()


class _ScatterPlan:
    def __init__(self, arrays, part=(0, 1, 1), into=None):
        self.n = len(arrays)
        self.operands = list(arrays) + list(into or [])
        self.out_shapes = [jax.ShapeDtypeStruct((3,) + a.shape[1:], a.dtype) for a in arrays]
        self.aliases = {self.n + i: i for i in range(self.n)} if into else {}
        self.nsems = 3 * self.n
        self.base = 0
        self.part = part

    def _copies(self, ins, outs, ssem, rsem):
        x, y, c, me, chips, sib = _place()
        lo, hi, n = self.part
        out = []
        for a in range(self.n):
            h = ins[a].shape[1]
            rows = pl.ds(lo * h // n, (hi - lo) * h // n)
            for k, chip in enumerate(chips):
                out.append(_rcopy(ins[a].at[2 * chip[0] + chip[1], rows], outs[a].at[k, rows],
                                  ssem.at[self.base + 3 * a + k], rsem.at[self.base + 3 * a + k], (*chip, c)))
        return out

    def schedule(self):
        return [(0.0, self.start)]

    def start(self, ins, outs, ssem, rsem):
        for cp in self._copies(ins, outs, ssem, rsem):
            cp.start()

    def finish(self, ins, outs, ssem, rsem):
        cps = self._copies(ins, outs, ssem, rsem)
        for cp in cps:
            cp.wait_recv()
        for cp in cps:
            cp.wait_send()


class _ShareHalfPlan(_ScatterPlan):
    def __init__(self, arrays):
        super().__init__(arrays)
        self.out_shapes = [jax.ShapeDtypeStruct((3, a.shape[0] // 2, a.shape[1]), a.dtype) for a in arrays]

    def _copies(self, ins, outs, ssem, rsem):
        x, y, c, me, chips, sib = _place()
        out = []
        for a in range(self.n):
            rh = ins[a].shape[0] // 2
            for k, chip in enumerate(chips):
                out.append(_rcopy(ins[a].at[pl.ds(c * rh, rh)], outs[a].at[k],
                                  ssem.at[self.base + 3 * a + k], rsem.at[self.base + 3 * a + k], (*chip, c)))
        return out


class _SwapPlan:
    def __init__(self, grads):
        self.operands = list(grads)
        self.out_shapes = [jax.ShapeDtypeStruct((g.shape[0], g.shape[1] // 2, g.shape[2]), g.dtype) for g in grads]
        self.aliases = {}
        self.nsems = len(grads)
        self.base = 0

    def _copies(self, ins, outs, ssem, rsem):
        x, y, c, me, chips, sib = _place()
        out = []
        for a, src in enumerate(ins):
            h = src.shape[1] // 2
            out.append(_rcopy(src.at[:, pl.ds((1 - c) * h, h), :], outs[a], ssem.at[self.base + a], rsem.at[self.base + a], sib))
        return out

    def schedule(self):
        return [(0.0, self.start)]

    def start(self, ins, outs, ssem, rsem):
        for cp in self._copies(ins, outs, ssem, rsem):
            cp.start()

    def finish(self, ins, outs, ssem, rsem):
        cps = self._copies(ins, outs, ssem, rsem)
        for cp in cps:
            cp.wait_recv()
        for cp in cps:
            cp.wait_send()


class _Multi:
    def __init__(self, plans):
        self.plans = plans
        self.operands, self.out_shapes, self.aliases, self.nsems = [], [], {}, 0
        self.spans = []
        for p in plans:
            ni, no = len(self.operands), len(self.out_shapes)
            self.spans.append((ni, ni + len(p.operands), no, no + len(p.out_shapes)))
            self.aliases.update({ni + i: no + j for i, j in p.aliases.items()})
            p.base = self.nsems
            self.nsems += p.nsems
            self.operands += p.operands
            self.out_shapes += p.out_shapes

    def schedule(self):
        def bound(fn, span):
            i0, i1, o0, o1 = span
            return lambda ins, outs, ssem, rsem: fn(ins[i0:i1], outs[o0:o1], ssem, rsem)

        stages = [(at, bound(fn, span)) for p, span in zip(self.plans, self.spans) for at, fn in p.schedule()]
        return sorted(stages, key=lambda s: s[0])

    def finish(self, ins, outs, ssem, rsem):
        for p, (i0, i1, o0, o1) in zip(self.plans, self.spans):
            p.finish(ins[i0:i1], outs[o0:o1], ssem, rsem)

    def results(self, extra):
        return [list(extra[o0:o1]) for (_, _, o0, o1) in self.spans]


class _Host:
    def __init__(self, comm, in_specs, out_specs, out_shape, scratch):
        self.comm = comm
        self.n_in, self.n_out = len(in_specs), len(out_specs)
        self.in_specs, self.out_specs, self.out_shape, self.scratch = list(in_specs), list(out_specs), list(out_shape), list(scratch)
        self.aliases = {}
        self.args = []
        if comm is not None:
            self.in_specs += [ANY] * len(comm.operands)
            self.out_specs += [ANY] * len(comm.out_shapes)
            self.out_shape += comm.out_shapes
            self.scratch += [pltpu.SemaphoreType.DMA((comm.nsems,)), pltpu.SemaphoreType.DMA((comm.nsems,))]
            self.aliases = {self.n_in + i: self.n_out + j for i, j in comm.aliases.items()}
            self.args = list(comm.operands)

    def split(self, refs):
        nc = len(self.args)
        nco = len(self.out_shape) - self.n_out
        ins, p = refs[:self.n_in], self.n_in + nc
        outs, rest = refs[p:p + self.n_out], refs[p + self.n_out + nco:]
        self._cargs = None
        if self.comm is not None:
            self._cargs = (refs[self.n_in:p], refs[p + self.n_out:p + self.n_out + nco], rest[-2], rest[-1])
            rest = rest[:-2]
        return ins, outs, rest

    def before(self, step, total):
        if self.comm is None:
            return

        for at, stage in self.comm.schedule():
            pl.when(step == min(total - 1, int(at * total)))(functools.partial(stage, *self._cargs))

    def after(self, step, total):
        if self.comm is None:
            return

        @pl.when(step == total - 1)
        def _():
            self.comm.finish(*self._cargs)

    def results(self, outs):
        return outs[:self.n_out], outs[self.n_out:]


def _cast_bf16(x, name):
    r, c = x.shape
    tr = ROW_TILE if r % ROW_TILE == 0 else r

    def body(x_ref, o_ref):
        o_ref[...] = x_ref[...].astype(BF16)

    return pl.pallas_call(
        body, name=name, grid=(r // tr,),
        in_specs=[pl.BlockSpec((tr, c), lambda i: (i, 0))],
        out_specs=pl.BlockSpec((tr, c), lambda i: (i, 0)),
        out_shape=jax.ShapeDtypeStruct((r, c), BF16), compiler_params=_cp("parallel"))(x)


def _cast_bf16_own_slab(x, me_arr, name):
    r, c = x.shape
    tr = ROW_TILE if r % ROW_TILE == 0 else r

    def body(me_ref, x_ref, o_ref):
        o_ref[...] = x_ref[...].astype(BF16)

    return pl.pallas_call(
        body, name=name,
        grid_spec=pltpu.PrefetchScalarGridSpec(
            num_scalar_prefetch=1, grid=(r // tr,),
            in_specs=[pl.BlockSpec((tr, c), lambda i, me: (i, 0))],
            out_specs=pl.BlockSpec((None, tr, c), lambda i, me: (me[0], i, 0))),
        out_shape=jax.ShapeDtypeStruct((N_CHIPS, r, c), BF16), compiler_params=_cp("parallel"))(me_arr, x)


def _rms_fwd(x, g, name):
    s, d = x.shape

    def body(x_ref, g_ref, h_ref):
        xhat, _ = _rms_stats(x_ref[...])
        h_ref[...] = (xhat * g_ref[...]).astype(BF16)

    return pl.pallas_call(
        body, name=name, grid=(s // ROW_TILE,),
        in_specs=[pl.BlockSpec((ROW_TILE, d), lambda i: (i, 0)), pl.BlockSpec((1, d), lambda i: (0, 0))],
        out_specs=pl.BlockSpec((ROW_TILE, d), lambda i: (i, 0)),
        out_shape=jax.ShapeDtypeStruct((s, d), BF16), compiler_params=_cp("parallel"))(x, g)


def _mm_nn(a, w3, name, comm=None):
    m, k = a.shape
    nsh, _, ns = w3.shape
    tm = 512 if m % 512 == 0 else ROW_TILE
    tn = _pick_tile(ns, 1024)
    per = ns // tn
    grid = (nsh * per, m // tm)
    host = _Host(comm,
                 [pl.BlockSpec((tm, k), lambda n, i: (i, 0)), pl.BlockSpec((None, k, tn), lambda n, i: (n // per, 0, n % per))],
                 [pl.BlockSpec((tm, tn), lambda n, i: (i, n))], [jax.ShapeDtypeStruct((m, nsh * ns), F32)], [])

    def body(*refs):
        (a_ref, w_ref), (o_ref,), _ = host.split(refs)
        step = pl.program_id(0) * grid[1] + pl.program_id(1)
        host.before(step, grid[0] * grid[1])
        o_ref[...] = jnp.dot(a_ref[...], w_ref[...], preferred_element_type=F32)
        host.after(step, grid[0] * grid[1])

    outs = pl.pallas_call(
        body, name=name, grid=grid, in_specs=host.in_specs, out_specs=host.out_specs, out_shape=host.out_shape,
        scratch_shapes=host.scratch, input_output_aliases=host.aliases,
        compiler_params=_cp("arbitrary", "arbitrary"))(a, w3, *host.args)
    (out,), extra = host.results(outs)
    return out, extra


def _mm_nt(a, b, name):
    m, k = a.shape
    n = b.shape[0]
    tm = 512 if m % 512 == 0 else ROW_TILE

    def body(a_ref, b_ref, o_ref):
        o_ref[...] = lax.dot_general(a_ref[...], b_ref[...], (((1,), (1,)), ((), ())), preferred_element_type=F32)

    return pl.pallas_call(
        body, name=name, grid=(m // tm,),
        in_specs=[pl.BlockSpec((tm, k), lambda i: (i, 0)), pl.BlockSpec((n, k), lambda i: (0, 0))],
        out_specs=pl.BlockSpec((tm, n), lambda i: (i, 0)),
        out_shape=jax.ShapeDtypeStruct((m, n), F32), compiler_params=_cp("parallel"))(a, b)


def _mm_tn(a, b, nsh, name, comm=None):
    s, m = a.shape
    n = b.shape[1]
    ns = n // nsh
    tm = 512 if m % 512 == 0 else ROW_TILE
    tn = _pick_tile(ns, 1024)
    per = ns // tn
    grid = (nsh * per, m // tm)
    host = _Host(comm, [pl.BlockSpec((s, tm), lambda j, i: (0, i)), pl.BlockSpec((s, tn), lambda j, i: (0, j))],
                 [pl.BlockSpec((None, tm, tn), lambda j, i: (j // per, i, j % per))],
                 [jax.ShapeDtypeStruct((nsh, m, ns), BF16)], [])

    def body(*refs):
        (a_ref, b_ref), (o_ref,), _ = host.split(refs)
        step = pl.program_id(0) * grid[1] + pl.program_id(1)
        host.before(step, grid[0] * grid[1])
        o_ref[...] = lax.dot_general(a_ref[...], b_ref[...], (((0,), (0,)), ((), ())),
                                     preferred_element_type=F32).astype(BF16)
        host.after(step, grid[0] * grid[1])

    outs = pl.pallas_call(
        body, name=name, grid=grid, in_specs=host.in_specs, out_specs=host.out_specs, out_shape=host.out_shape,
        scratch_shapes=host.scratch, input_output_aliases=host.aliases,
        compiler_params=_cp("arbitrary", "arbitrary"))(a, b, *host.args)
    (out,), extra = host.results(outs)
    return out, extra


def _tri(n, rel):
    row = lax.broadcasted_iota(jnp.int32, (2 * n, n), 0)
    col = lax.broadcasted_iota(jnp.int32, (2 * n, n), 1)
    return jnp.where(rel(jnp.where(row >= n, row - n, row), col), 1.0, 0.0).astype(BF16)


def _dot_split(x, tri2):
    hi = x.astype(BF16)
    lo = (x - hi.astype(F32)).astype(BF16)
    return jnp.dot(jnp.concatenate([hi, lo], axis=1), tri2, preferred_element_type=F32)


def _nt(a, b):
    return lax.dot_general(a, b, (((1,), (1,)), ((), ())), preferred_element_type=F32)


def _tn(a, b):
    return lax.dot_general(a, b, (((0,), (0,)), ((), ())), preferred_element_type=F32)


def _heads_per_step(nh):
    return max(h for h in (1, 2, 4) if nh % h == 0)


def _sba_fwd(p, sbw, name, comm=None):
    s = p.shape[0]
    nh = sbw // HEAD_DIM
    hp = _heads_per_step(nh)
    ngrp, hw = nh // hp, hp * HEAD_DIM
    blk = ATT_BLOCK
    nq = s // blk
    scale = 1.0 / math.sqrt(HEAD_DIM)
    host = _Host(comm,
                 [pl.BlockSpec((blk, hw), lambda g, i: (i, g)),
                  pl.BlockSpec((s, hw), lambda g, i: (0, ngrp + g)),
                  pl.BlockSpec((s, hw), lambda g, i: (0, 2 * ngrp + g))],
                 [pl.BlockSpec((blk, hw), lambda g, i: (i, g))] * 2,
                 [jax.ShapeDtypeStruct((s, sbw), F32)] * 2,
                 [pltpu.VMEM((s, hw), BF16)] * 2)

    def body(*refs):
        (q_ref, k_ref, v_ref), (o_ref, lt_ref), (kb_ref, vb_ref) = host.split(refs)
        i = pl.program_id(1)
        step = pl.program_id(0) * nq + i
        host.before(step, ngrp * nq)

        @pl.when(i == 0)
        def _():
            kb_ref[...] = k_ref[...].astype(BF16)
            vb_ref[...] = v_ref[...].astype(BF16)

        heads = [slice(h * HEAD_DIM, (h + 1) * HEAD_DIM) for h in range(hp)]
        qs = [q_ref[:, hd].astype(BF16) for hd in heads]
        later = _tri(blk, lambda r, c: r > c)
        causal = lax.broadcasted_iota(jnp.int32, (blk, blk), 1) < lax.broadcasted_iota(jnp.int32, (blk, blk), 0)

        def key_block(j, carry, diagonal):
            rows = pl.ds(pl.multiple_of(j * blk, blk), blk)
            hs = range(hp)
            z = [_nt(qs[h], kb_ref[rows, heads[h]]) * scale for h in hs]
            ls = [_log_sigmoid(z[h]) for h in hs]
            lm = [jnp.where(causal, ls[h] - z[h], 0.0) if diagonal else ls[h] - z[h] for h in hs]
            stay = [_dot_split(lm[h], later) for h in hs]
            w = [jnp.exp(ls[h] + stay[h] + carry[h][1]) for h in hs]
            if diagonal:
                w = [jnp.where(causal, w[h], 0.0) for h in hs]
            acc = [carry[h][0] + jnp.dot(w[h].astype(BF16), vb_ref[rows, heads[h]], preferred_element_type=F32) for h in hs]
            return tuple((acc[h], carry[h][1] + jnp.sum(lm[h], axis=1, keepdims=True)) for h in hs)

        init = tuple((jnp.zeros((blk, HEAD_DIM), F32), jnp.zeros((blk, 1), F32)) for _ in heads)
        carry = key_block(i, init, True)
        carry = lax.fori_loop(0, i, lambda n, c: key_block(i - 1 - n, c, False), carry)
        for h, hd in enumerate(heads):
            o_ref[:, hd] = carry[h][0]
            lt_ref[:, hd] = jnp.broadcast_to(carry[h][1], (blk, HEAD_DIM))
        host.after(step, ngrp * nq)

    outs = pl.pallas_call(
        body, name=name, grid=(ngrp, nq), in_specs=host.in_specs, out_specs=host.out_specs, out_shape=host.out_shape,
        scratch_shapes=host.scratch, input_output_aliases=host.aliases,
        compiler_params=_cp("arbitrary", "arbitrary"))(p, p, p, *host.args)
    (out, ltot), extra = host.results(outs)
    return out, ltot, extra


def _sba_bwd(p, ltot, dout, sbw, name, comm=None):
    s = p.shape[0]
    nh = sbw // HEAD_DIM
    hp = _heads_per_step(nh)
    ngrp, hw = nh // hp, hp * HEAD_DIM
    blk = ATT_BLOCK
    nq = s // blk
    scale = 1.0 / math.sqrt(HEAD_DIM)
    blk_spec = pl.BlockSpec((blk, hw), lambda g, i: (i, g))
    col_spec = pl.BlockSpec((s, hw), lambda g, i: (0, g))
    host = _Host(comm,
                 [blk_spec, pl.BlockSpec((s, hw), lambda g, i: (0, ngrp + g)),
                  pl.BlockSpec((s, hw), lambda g, i: (0, 2 * ngrp + g)), blk_spec, blk_spec],
                 [blk_spec, col_spec, col_spec], [jax.ShapeDtypeStruct((s, sbw), BF16)] * 3,
                 [pltpu.VMEM((s, hw), BF16)] * 2 + [pltpu.VMEM((s, hw), F32)] * 2)

    def body(*refs):
        (q_ref, k_ref, v_ref, lt_ref, do_ref), (dq_ref, dk_ref, dv_ref), (kb_ref, vb_ref, dka_ref, dva_ref) = host.split(refs)
        i = pl.program_id(1)
        step = pl.program_id(0) * nq + i
        host.before(step, ngrp * nq)

        @pl.when(i == 0)
        def _():
            kb_ref[...] = k_ref[...].astype(BF16)
            vb_ref[...] = v_ref[...].astype(BF16)
            dka_ref[...] = jnp.zeros_like(dka_ref)
            dva_ref[...] = jnp.zeros_like(dva_ref)

        heads = [slice(h * HEAD_DIM, (h + 1) * HEAD_DIM) for h in range(hp)]
        qs = [q_ref[:, hd].astype(BF16) for hd in heads]
        dos = [do_ref[:, hd].astype(BF16) for hd in heads]
        ltots = [lt_ref[:, h * HEAD_DIM:h * HEAD_DIM + 1] for h in range(hp)]
        upto = _tri(blk, lambda r, c: r <= c)
        before = _tri(blk, lambda r, c: r < c)
        causal = lax.broadcasted_iota(jnp.int32, (blk, blk), 1) < lax.broadcasted_iota(jnp.int32, (blk, blk), 0)

        def key_block(j, carry, diagonal):
            rows = pl.ds(pl.multiple_of(j * blk, blk), blk)
            hs = range(hp)
            kj = [kb_ref[rows, heads[h]] for h in hs]
            vj = [vb_ref[rows, heads[h]] for h in hs]
            z = [_nt(qs[h], kj[h]) * scale for h in hs]
            dw = [_nt(dos[h], vj[h]) for h in hs]
            ls = [_log_sigmoid(z[h]) for h in hs]
            lm = [jnp.where(causal, ls[h] - z[h], 0.0) if diagonal else ls[h] - z[h] for h in hs]
            stay = [ltots[h] - carry[h][1] - _dot_split(lm[h], upto) for h in hs]
            w = [jnp.exp(ls[h] + stay[h]) for h in hs]
            if diagonal:
                w = [jnp.where(causal, w[h], 0.0) for h in hs]
            da = [dw[h] * w[h] for h in hs]
            sig = [jnp.exp(ls[h]) for h in hs]
            chain = [sig[h] * (carry[h][2] + _dot_split(da[h], before)) for h in hs]
            if diagonal:
                chain = [jnp.where(causal, chain[h], 0.0) for h in hs]
            dzb = [((da[h] * (1.0 - sig[h]) - chain[h]) * scale).astype(BF16) for h in hs]
            dq = [carry[h][0] + jnp.dot(dzb[h], kj[h], preferred_element_type=F32) for h in hs]
            for h in hs:
                dka_ref[rows, heads[h]] += _tn(dzb[h], qs[h])
            for h in hs:
                dva_ref[rows, heads[h]] += _tn(w[h].astype(BF16), dos[h])
            return tuple((dq[h], carry[h][1] + jnp.sum(lm[h], axis=1, keepdims=True),
                          carry[h][2] + jnp.sum(da[h], axis=1, keepdims=True)) for h in hs)

        zero = jnp.zeros((blk, 1), F32)
        init = tuple((jnp.zeros((blk, HEAD_DIM), F32), zero, zero) for _ in heads)
        carry = lax.fori_loop(0, i, lambda j, c: key_block(j, c, False), init)
        carry = key_block(i, carry, True)
        for h, hd in enumerate(heads):
            dq_ref[:, hd] = carry[h][0].astype(BF16)

        @pl.when(i == nq - 1)
        def _():
            dk_ref[...] = dka_ref[...].astype(BF16)
            dv_ref[...] = dva_ref[...].astype(BF16)

        host.after(step, ngrp * nq)

    outs = pl.pallas_call(
        body, name=name, grid=(ngrp, nq), in_specs=host.in_specs, out_specs=host.out_specs, out_shape=host.out_shape,
        scratch_shapes=host.scratch, input_output_aliases=host.aliases,
        compiler_params=_cp("arbitrary", "arbitrary"))(p, p, p, ltot, dout, *host.args)
    (dq, dk, dv), extra = host.results(outs)
    return dq, dk, dv, extra


def _pool_groups(pad_ref, tile, row0, gd, halo):
    row = row0 + lax.broadcasted_iota(jnp.int32, (tile, 1), 0)
    out = []
    for gi, win in enumerate(POOL_WINDOWS):
        cs = slice(gi * gd, (gi + 1) * gd)
        tok = pad_ref[halo:halo + tile, cs]
        acc = tok
        for j in range(1, win):
            acc = acc + pad_ref[halo - j:halo - j + tile, cs]
        cnt = jnp.minimum(win, row + 1).astype(F32)
        out.append(acc / cnt - tok)
    return out


def _even_mix_fwd(p, att, pool_w, pool_scale, d, name):
    s = p.shape[0]
    half = d // 2
    gd = half // len(POOL_WINDOWS)
    t, hb = ROW_TILE, POOL_HALO

    def body(u_ref, uh_ref, g_ref, a_ref, pw_ref, sc_ref, y_ref, pad_ref):
        i = pl.program_id(0)
        pad_ref[0:hb, :] = jnp.where(i > 0, uh_ref[...], 0.0)
        pad_ref[hb:, :] = u_ref[...]
        pooled = _pool_groups(pad_ref, t, i * t, gd, hb)
        for gi in range(len(POOL_WINDOWS)):
            cs = slice(gi * gd, (gi + 1) * gd)
            po = jnp.dot(pooled[gi].astype(BF16), pw_ref[gi], preferred_element_type=F32) * sc_ref[:, cs]
            y_ref[:, half + gi * gd:half + (gi + 1) * gd] = (po * _silu(g_ref[:, half + gi * gd:half + (gi + 1) * gd])).astype(BF16)
        y_ref[:, :half] = (a_ref[...] * _silu(g_ref[:, :half])).astype(BF16)

    return pl.pallas_call(
        body, name=name, grid=(s // t,),
        in_specs=[pl.BlockSpec((t, half), lambda i: (i, 3)),
                  pl.BlockSpec((hb, half), lambda i: (jnp.maximum(i * (t // hb) - 1, 0), 3)),
                  pl.BlockSpec((t, d), lambda i: (i, 2)),
                  pl.BlockSpec((t, half), lambda i: (i, 0)),
                  pl.BlockSpec(pool_w.shape, lambda i: (0, 0, 0)),
                  pl.BlockSpec((1, half), lambda i: (0, 0))],
        out_specs=pl.BlockSpec((t, d), lambda i: (i, 0)),
        out_shape=jax.ShapeDtypeStruct((s, d), BF16),
        scratch_shapes=[pltpu.VMEM((hb + t, half), F32)],
        compiler_params=_cp("parallel"))(p, p, p, att, pool_w, pool_scale)


def _even_mix_bwd(p, att, dy, pool_w, pool_scale, d, name, comm=None):
    s = p.shape[0]
    half = d // 2
    ng = len(POOL_WINDOWS)
    gd = half // ng
    t, hb = ROW_TILE, POOL_HALO
    nt = s // t
    host = _Host(
        comm,
        [pl.BlockSpec((t, half), lambda i: (i, 3)),
         pl.BlockSpec((hb, half), lambda i: (jnp.maximum(i * (t // hb) - 1, 0), 3)),
         pl.BlockSpec((t, d), lambda i: (i, 2)),
         pl.BlockSpec((hb, half), lambda i: (jnp.minimum((i + 1) * (t // hb), s // hb - 1), 5)),
         pl.BlockSpec((t, half), lambda i: (i, 0)),
         pl.BlockSpec((t, d), lambda i: (i, 0)),
         pl.BlockSpec((hb, half), lambda i: (jnp.minimum((i + 1) * (t // hb), s // hb - 1), 1)),
         pl.BlockSpec(pool_w.shape, lambda i: (0, 0, 0)),
         pl.BlockSpec((1, half), lambda i: (0, 0))],
        [pl.BlockSpec((t, half), lambda i: (i, 0)),
         pl.BlockSpec((t, half), lambda i: (i, 0)),
         pl.BlockSpec((t, d), lambda i: (i, 0)),
         pl.BlockSpec((1, half), lambda i: (0, 0)),
         pl.BlockSpec((ng, gd, gd), lambda i: (0, 0, 0))],
        [jax.ShapeDtypeStruct((s, half), F32), jax.ShapeDtypeStruct((s, half), BF16),
         jax.ShapeDtypeStruct((s, d), BF16), jax.ShapeDtypeStruct((1, half), F32),
         jax.ShapeDtypeStruct((ng, gd, gd), F32)],
        [pltpu.VMEM((hb + t, half), F32), pltpu.VMEM((t + hb, half), F32)])

    def body(*refs):
        ((u_ref, uh_ref, g_ref, gh_ref, a_ref, dy_ref, dyh_ref, pw_ref, sc_ref),
         (da_ref, du_ref, dg_ref, dsc_ref, dpw_ref), (pad_ref, dn_ref)) = host.split(refs)
        i = pl.program_id(0)
        host.before(i, nt)
        first = i == 0
        pad_ref[0:hb, :] = jnp.where(i > 0, uh_ref[...], 0.0)
        pad_ref[hb:, :] = u_ref[...]
        pooled = _pool_groups(pad_ref, t, i * t, gd, hb)
        g1 = g_ref[:, :half]
        dy1 = dy_ref[:, :half]
        da_ref[...] = dy1 * _silu(g1)
        dg_ref[:, :half] = (dy1 * a_ref[...] * _dsilu(g1)).astype(BF16)
        row = i * t + lax.broadcasted_iota(jnp.int32, (t + hb, 1), 0)
        for gi, win in enumerate(POOL_WINDOWS):
            cs = slice(gi * gd, (gi + 1) * gd)
            cs2 = slice(half + gi * gd, half + (gi + 1) * gd)
            w = pw_ref[gi]
            pb = pooled[gi].astype(BF16)
            zp = jnp.dot(pb, w, preferred_element_type=F32)
            g2 = g_ref[:, cs2]
            dy2 = dy_ref[:, cs2]
            dg_ref[:, cs2] = (dy2 * zp * sc_ref[:, cs] * _dsilu(g2)).astype(BF16)
            dpo = dy2 * _silu(g2)
            _acc_rows(dsc_ref.at[:, cs], first, jnp.sum(dpo * zp, axis=0, keepdims=True))
            dz = (dpo * sc_ref[:, cs]).astype(BF16)
            _acc_rows(dpw_ref.at[gi], first, _tn(pb, dz))
            dzh = jnp.where(i < nt - 1, dyh_ref[:, cs] * _silu(gh_ref[:, cs]) * sc_ref[:, cs], 0.0).astype(BF16)
            dpool = _nt(dz, w)
            dpool_h = _nt(dzh, w)
            cnt = jnp.minimum(win, row + 1).astype(F32)
            dn_ref[0:t, cs] = dpool / cnt[0:t]
            dn_ref[t:, cs] = dpool_h / cnt[t:]
            acc = dn_ref[0:t, cs]
            for j in range(1, win):
                acc = acc + dn_ref[j:j + t, cs]
            du_ref[:, cs] = (acc - dpool).astype(BF16)
        host.after(i, nt)

    outs = pl.pallas_call(
        body, name=name, grid=(nt,), in_specs=host.in_specs, out_specs=host.out_specs, out_shape=host.out_shape,
        scratch_shapes=host.scratch, input_output_aliases=host.aliases,
        compiler_params=_cp("arbitrary"))(p, p, p, p, att, dy, dy, pool_w, pool_scale, *host.args)
    return host.results(outs)


def _mm_out_even(y, w, x, g_post, g_pre_next, name):
    s, k = y.shape
    d = w.shape[1]
    t = ROW_TILE

    def body(y_ref, w_ref, x_ref, gp_ref, gn_ref, o_ref, x1_ref, h1_ref):
        for r0 in range(0, t, t // 2):
            rows = slice(r0, r0 + t // 2)
            o = jnp.dot(y_ref[rows, :], w_ref[...], preferred_element_type=F32)
            o_ref[rows, :] = o
            ohat, _ = _rms_stats(o)
            x1 = x_ref[rows, :] + ohat * gp_ref[...]
            x1_ref[rows, :] = x1
            xhat, _ = _rms_stats(x1)
            h1_ref[rows, :] = (xhat * gn_ref[...]).astype(BF16)

    row = lambda c: pl.BlockSpec((t, c), lambda i: (i, 0))
    vec = pl.BlockSpec((1, d), lambda i: (0, 0))
    return pl.pallas_call(
        body, name=name, grid=(s // t,),
        in_specs=[row(k), pl.BlockSpec((k, d), lambda i: (0, 0)), row(d), vec, vec],
        out_specs=[row(d), row(d), row(d)],
        out_shape=[jax.ShapeDtypeStruct((s, d), F32), jax.ShapeDtypeStruct((s, d), F32),
                   jax.ShapeDtypeStruct((s, d), BF16)],
        compiler_params=_cp("parallel"))(y, w, x, g_post, g_pre_next)


def _mm_out_odd(y, w, x1, g_post, target, name):
    s, k = y.shape
    d = w.shape[1]
    t = ROW_TILE

    def body(y_ref, w_ref, x_ref, gp_ref, tg_ref, do_ref, dx_ref, loss_ref, dgp_ref):
        first = pl.program_id(0) == 0
        gp = gp_ref[...]
        part = dgp = None
        for r0 in range(0, t, t // 2):
            rows = slice(r0, r0 + t // 2)
            o = jnp.dot(y_ref[rows, :], w_ref[...], preferred_element_type=F32)
            ohat, r = _rms_stats(o)
            diff = x_ref[rows, :] + ohat * gp - tg_ref[rows, :]
            part_half = 0.5 * jnp.sum(jnp.mean(diff * diff, axis=-1, keepdims=True), axis=0, keepdims=True)
            dx2 = diff * (1.0 / d)
            dx_ref[rows, :] = dx2
            do, dgp_half = _rms_bwd(dx2, ohat, r, gp)
            do_ref[rows, :] = do.astype(BF16)
            part = part_half if part is None else part + part_half
            dgp = dgp_half if dgp is None else dgp + dgp_half
        _acc_rows(loss_ref, first, jnp.broadcast_to(part, loss_ref.shape))
        _acc_rows(dgp_ref, first, dgp)

    row = lambda c: pl.BlockSpec((t, c), lambda i: (i, 0))
    vec = pl.BlockSpec((1, d), lambda i: (0, 0))
    return pl.pallas_call(
        body, name=name, grid=(s // t,),
        in_specs=[row(k), pl.BlockSpec((k, d), lambda i: (0, 0)), row(d), vec, row(d)],
        out_specs=[row(d), row(d), pl.BlockSpec((8, LANES), lambda i: (0, 0)), vec],
        out_shape=[jax.ShapeDtypeStruct((s, d), BF16), jax.ShapeDtypeStruct((s, d), F32),
                   jax.ShapeDtypeStruct((8, LANES), F32), jax.ShapeDtypeStruct((1, d), F32)],
        compiler_params=_cp("arbitrary"))(y, w, x1, g_post, target)


def _layer_norm(d1, cg, cb):
    mu = jnp.mean(d1, axis=-1, keepdims=True)
    cen = d1 - mu
    rstd = lax.rsqrt(jnp.mean(cen * cen, axis=-1, keepdims=True) + EPS)
    n = cen * rstd
    return n, rstd, n * cg + cb


SUBLANES = 8
ROW_STRIP = 64
GATHER_PIECES = 8
CONV_ROWS = 64


def _make_shifts(pad_ref, cs, sh_ref):
    rows = sh_ref.shape[1]
    for r in range(1, SUBLANES):
        sh_ref[r - 1] = pad_ref[r:r + rows, cs]


def _by_shift(taps, base, sign=1):
    return sorted(range(taps), key=lambda k: ((sign * (base + k)) % SUBLANES, k))


def _window(pad_ref, cs, sh_ref, off, t):
    m, r = divmod(off, SUBLANES)
    if r == 0:
        return pad_ref[SUBLANES * m:SUBLANES * m + t, cs]
    return sh_ref[r - 1, SUBLANES * m:SUBLANES * m + t, :]


def _odd_mix_fwd(p, sconv_w, dconv_w, dconv_b, cnorm_g, cnorm_b, d, name):
    s = p.shape[0]
    w = d // 2
    k3, k31 = sconv_w.shape[0], dconv_w.shape[0]
    t, hb = ROW_TILE, CONV_HALO
    assert hb >= k31 - 1 and w % LANES == 0

    def body(p_ref, ph_ref, w3_ref, w31_ref, b31_ref, cg_ref, cb_ref, y_ref, s3_ref, d1_ref, mpad, dpad, sh_ref):
        i = pl.program_id(0)
        mpad[0:hb, :] = jnp.where(i > 0, ph_ref[:, 2 * w:3 * w] * ph_ref[:, 0:w], 0.0)
        mpad[hb:, :] = p_ref[:, 2 * w:3 * w] * p_ref[:, 0:w]
        dpad[0:hb, :] = jnp.where(i > 0, ph_ref[:, 3 * w:4 * w] * _sigmoid(ph_ref[:, 4 * w:5 * w]), 0.0)
        dpad[hb:, :] = p_ref[:, 3 * w:4 * w] * _sigmoid(p_ref[:, 4 * w:5 * w])
        for c0 in range(0, w, LANES):
            cs = slice(c0, c0 + LANES)
            acc = jnp.zeros((t, LANES), F32)
            for kk in range(k3):
                acc = acc + w3_ref[kk:kk + 1, cs] * mpad[hb - (k3 - 1) + kk:hb - (k3 - 1) + kk + t, cs]
            s3_ref[:, cs] = acc
            _make_shifts(dpad, cs, sh_ref)
            for r0 in range(0, t, CONV_ROWS):
                acc = jnp.zeros((CONV_ROWS, LANES), F32)
                for kk in _by_shift(k31, hb - (k31 - 1)):
                    acc = acc + w31_ref[kk:kk + 1, cs] * _window(dpad, cs, sh_ref, hb - (k31 - 1) + kk + r0, CONV_ROWS)
                d1_ref[r0:r0 + CONV_ROWS, cs] = acc + b31_ref[:, cs]
        _, _, d2 = _layer_norm(d1_ref[...], cg_ref[...], cb_ref[...])
        y_ref[:, :w] = (p_ref[:, w:2 * w] * s3_ref[...] * _silu(p_ref[:, 5 * w:6 * w])).astype(BF16)
        y_ref[:, w:] = (_silu(d2) * _silu(p_ref[:, 6 * w:7 * w])).astype(BF16)

    row = lambda c: pl.BlockSpec((t, c), lambda i: (i, 0))
    full = lambda a: pl.BlockSpec(a.shape, lambda i: (0, 0))
    return pl.pallas_call(
        body, name=name, grid=(s // t,),
        in_specs=[row(7 * w),
                  pl.BlockSpec((hb, 5 * w), lambda i: (jnp.maximum(i * (t // hb) - 1, 0), 0)),
                  full(sconv_w), full(dconv_w), full(dconv_b), full(cnorm_g), full(cnorm_b)],
        out_specs=[row(d), row(w), row(w)],
        out_shape=[jax.ShapeDtypeStruct((s, d), BF16), jax.ShapeDtypeStruct((s, w), F32),
                   jax.ShapeDtypeStruct((s, w), F32)],
        scratch_shapes=[pltpu.VMEM((hb + t, w), F32)] * 2 + [pltpu.VMEM((SUBLANES - 1, hb + t - SUBLANES, LANES), F32)],
        compiler_params=_cp("parallel"))(p, p, sconv_w, dconv_w, dconv_b, cnorm_g, cnorm_b)


def _odd_bwd_rows(p, s3, d1, dy, cnorm_g, cnorm_b, d, name, comm=None):
    s = p.shape[0]
    w = d // 2
    t = ROW_TILE
    col = lambda j: pl.BlockSpec((t, w), lambda i: (i, j))
    row = lambda c: pl.BlockSpec((t, c), lambda i: (i, 0))
    vec = pl.BlockSpec((1, w), lambda i: (0, 0))
    host = _Host(comm, [col(1), col(5), col(6), row(w), row(w), row(d), vec, vec],
                 [row(w), row(d), row(w), row(w), vec, vec, vec],
                 [jax.ShapeDtypeStruct((s, w), BF16), jax.ShapeDtypeStruct((s, d), BF16),
                  jax.ShapeDtypeStruct((s, w), F32), jax.ShapeDtypeStruct((s, w), F32)] + [jax.ShapeDtypeStruct((1, w), F32)] * 3, [])

    def body(*refs):
        ((bc_ref, g1_ref, g2_ref, s3_ref, d1_ref, dy_ref, cg_ref, cb_ref),
         (dbc_ref, dg_ref, ds3_ref, dd1_ref, dcg_ref, dcb_ref, db_ref), _) = host.split(refs)
        step = pl.program_id(0)
        host.before(step, s // t)
        first = step == 0

        def strip(j, sums):
            rows = slice(j * ROW_STRIP, (j + 1) * ROW_STRIP)
            g1, g2 = g1_ref[rows, :], g2_ref[rows, :]
            bc, s3v = bc_ref[rows, :], s3_ref[rows, :]
            dy1, dy2 = dy_ref[rows, :w], dy_ref[rows, w:]
            n, rstd, d2 = _layer_norm(d1_ref[rows, :], cg_ref[...], cb_ref[...])
            dg_ref[rows, :w] = (dy1 * bc * s3v * _dsilu(g1)).astype(BF16)
            dg_ref[rows, w:] = (dy2 * _silu(d2) * _dsilu(g2)).astype(BF16)
            dco = dy1 * _silu(g1)
            dbc_ref[rows, :] = (dco * s3v).astype(BF16)
            ds3_ref[rows, :] = dco * bc
            dd2 = dy2 * _silu(g2) * _dsilu(d2)
            dn = dd2 * cg_ref[...]
            dd1 = rstd * (dn - jnp.mean(dn, axis=-1, keepdims=True) - n * jnp.mean(dn * n, axis=-1, keepdims=True))
            dd1_ref[rows, :] = dd1
            dcb, dcg, db = sums
            return (dcb + jnp.sum(dd2, axis=0, keepdims=True), dcg + jnp.sum(dd2 * n, axis=0, keepdims=True),
                    db + jnp.sum(dd1, axis=0, keepdims=True))

        zero = jnp.zeros((1, w), F32)
        sums = (zero, zero, zero)
        for j in range(t // ROW_STRIP):
            sums = strip(j, sums)
        dcb, dcg, db = sums
        _acc_rows(dcb_ref, first, dcb)
        _acc_rows(dcg_ref, first, dcg)
        _acc_rows(db_ref, first, db)
        host.after(step, s // t)

    outs = pl.pallas_call(
        body, name=name, grid=(s // t,), in_specs=host.in_specs, out_specs=host.out_specs, out_shape=host.out_shape,
        scratch_shapes=host.scratch, input_output_aliases=host.aliases,
        compiler_params=_cp("arbitrary"))(p, p, p, s3, d1, dy, cnorm_g, cnorm_b, *host.args)
    return host.results(outs)


def _odd_bwd_conv(p, ds3, dd1, sconv_w, dconv_w, d, name):
    s = p.shape[0]
    w = d // 2
    k3, k31 = sconv_w.shape[0], dconv_w.shape[0]
    t, hb, ha = ROW_TILE, CONV_HALO, 8
    nt = s // t
    assert hb >= k31 - 1 and ha >= k3 - 1

    def body(hc_ref, cc_ref, ga_ref, gb_ref, hch_ref, cch_ref, gah_ref, gbh_ref, ds3_ref, ds3h_ref, dd1_ref, dd1h_ref,
             w3_ref, w31_ref, dhc_ref, dcc_ref, dga_ref, dgb_ref, dw3_ref, dw31_ref, mpad, dpad, s3pad, d1pad, sh_ref):
        i = pl.program_id(0)
        first = i == 0
        last = i == nt - 1
        mpad[0:hb, :] = jnp.where(i > 0, cch_ref[...] * hch_ref[...], 0.0)
        mpad[hb:, :] = cc_ref[...] * hc_ref[...]
        dpad[0:hb, :] = jnp.where(i > 0, gah_ref[...] * _sigmoid(gbh_ref[...]), 0.0)
        dpad[hb:, :] = ga_ref[...] * _sigmoid(gb_ref[...])
        s3pad[0:t, :] = ds3_ref[...]
        s3pad[t:, :] = jnp.where(last, 0.0, ds3h_ref[...])
        d1pad[0:t, :] = dd1_ref[...]
        d1pad[t:, :] = jnp.where(last, 0.0, dd1h_ref[...])

        @pl.when(first)
        def _():
            dw3_ref[...] = jnp.zeros_like(dw3_ref)
            dw31_ref[...] = jnp.zeros_like(dw31_ref)

        def fold(v):
            return jnp.sum(v.reshape(v.shape[0] // SUBLANES, SUBLANES, LANES), axis=0)

        groups = range(0, t, CONV_ROWS)
        for c0 in range(0, w, LANES):
            cs = slice(c0, c0 + LANES)
            ds3v = s3pad[0:t, cs]
            dm = jnp.zeros((t, LANES), F32)
            for kk in range(k3):
                dm = dm + w3_ref[kk:kk + 1, cs] * s3pad[k3 - 1 - kk:k3 - 1 - kk + t, cs]
                off = hb - (k3 - 1) + kk
                dw3_ref[SUBLANES * kk:SUBLANES * (kk + 1), cs] += fold(ds3v * mpad[off:off + t, cs])
            dcc_ref[:, cs] = (dm * hc_ref[:, cs]).astype(BF16)
            dhc_ref[:, cs] = (dm * cc_ref[:, cs]).astype(BF16)
            _make_shifts(d1pad, cs, sh_ref)
            for r0 in groups:
                rows = slice(r0, r0 + CONV_ROWS)
                dd0 = jnp.zeros((CONV_ROWS, LANES), F32)
                for kk in _by_shift(k31, -(k31 - 1), -1):
                    dd0 = dd0 + w31_ref[kk:kk + 1, cs] * _window(d1pad, cs, sh_ref, k31 - 1 - kk + r0, CONV_ROWS)
                sgb = _sigmoid(gb_ref[rows, cs])
                dga_ref[rows, cs] = (dd0 * sgb).astype(BF16)
                dgb_ref[rows, cs] = (dd0 * ga_ref[rows, cs] * sgb * (1.0 - sgb)).astype(BF16)
            _make_shifts(dpad, cs, sh_ref)
            for kk in _by_shift(k31, hb - (k31 - 1)):
                part = jnp.zeros((SUBLANES, LANES), F32)
                for r0 in groups:
                    part = part + fold(d1pad[r0:r0 + CONV_ROWS, cs]
                                       * _window(dpad, cs, sh_ref, hb - (k31 - 1) + kk + r0, CONV_ROWS))
                dw31_ref[SUBLANES * kk:SUBLANES * (kk + 1), cs] += part

    col = lambda j: pl.BlockSpec((t, w), lambda i: (i, j))
    pre = lambda j: pl.BlockSpec((hb, w), lambda i: (jnp.maximum(i * (t // hb) - 1, 0), j))
    row = pl.BlockSpec((t, w), lambda i: (i, 0))
    post = lambda h: pl.BlockSpec((h, w), lambda i: (jnp.minimum((i + 1) * (t // h), s // h - 1), 0))
    full = lambda a: pl.BlockSpec(a.shape, lambda i: (0, 0))
    dhc, dcc, dga, dgb, dw3, dw31 = pl.pallas_call(
        body, name=name, grid=(nt,),
        in_specs=[col(0), col(2), col(3), col(4), pre(0), pre(2), pre(3), pre(4),
                  row, post(ha), row, post(hb), full(sconv_w), full(dconv_w)],
        out_specs=[row, row, row, row, pl.BlockSpec((SUBLANES * k3, w), lambda i: (0, 0)),
                   pl.BlockSpec((SUBLANES * k31, w), lambda i: (0, 0))],
        out_shape=[jax.ShapeDtypeStruct((s, w), BF16)] * 4
        + [jax.ShapeDtypeStruct((SUBLANES * k3, w), F32), jax.ShapeDtypeStruct((SUBLANES * k31, w), F32)],
        scratch_shapes=[pltpu.VMEM((hb + t, w), F32)] * 2 + [pltpu.VMEM((t + ha, w), F32), pltpu.VMEM((t + hb, w), F32),
                                                             pltpu.VMEM((SUBLANES - 1, hb + t - SUBLANES, LANES), F32)],
        compiler_params=_cp("arbitrary"))(p, p, p, p, p, p, p, p, ds3, ds3, dd1, dd1, sconv_w, dconv_w)
    return dhc, dcc, dga, dgb, jnp.sum(dw3.reshape(k3, SUBLANES, w), axis=1), jnp.sum(dw31.reshape(k31, SUBLANES, w), axis=1)


def _mm_in_bwd(dp, w3, x, g_pre, dres, post, name, comm=None):
    s = dp.shape[0]
    nsh, d, ns = w3.shape
    t = 512 if s % 512 == 0 else ROW_TILE
    nt = s // t
    ks = 2 if (ns // 2) % LANES == 0 else 1
    nk, kw = nsh * ks, ns // ks
    chunk = 128
    nchunk = t // chunk
    row = pl.BlockSpec((t, d), lambda i, k: (i, 0))
    vec = pl.BlockSpec((1, d), lambda i, k: (0, 0))
    rowwise = [x, dres] + ([post[0]] if post is not None else [])
    in_specs = [pl.BlockSpec((t, kw), lambda i, k: (i, k)), pl.BlockSpec((None, d, kw), lambda i, k: (k // ks, 0, k % ks)), vec]
    out_specs = [row, vec]
    out_shape = [jax.ShapeDtypeStruct((s, d), F32), jax.ShapeDtypeStruct((1, d), F32)]
    args = [dp, w3, g_pre]
    if post is not None:
        in_specs += [vec]
        out_specs += [row, vec]
        out_shape += [jax.ShapeDtypeStruct((s, d), BF16), jax.ShapeDtypeStruct((1, d), F32)]
        args += [post[1]]
    n_blocked = len(in_specs)
    in_specs += [ANY] * len(rowwise)
    args += rowwise
    host = _Host(comm, in_specs, out_specs, out_shape,
                 [pltpu.VMEM((t, d), F32), pltpu.VMEM((len(rowwise), 2, chunk, d), F32), pltpu.SemaphoreType.DMA((len(rowwise), 2))])

    def body(*refs):
        ins, outs, (acc_ref, buf_ref, sem_ref) = host.split(refs)
        dp_ref, w_ref, g_ref = ins[:3]
        hbm = ins[n_blocked:]
        dx_ref, dg_ref = outs[:2]
        tile = pl.program_id(0)
        kk = pl.program_id(1)
        first = tile == 0
        step = tile * nk + kk
        host.before(step, nt * nk)
        part = _nt(dp_ref[...], w_ref[...])

        @pl.when(kk == 0)
        def _():
            acc_ref[...] = part

        @pl.when(kk > 0)
        def _():
            acc_ref[...] += part

        def fetch(ci, slot):
            return [pltpu.make_async_copy(src.at[pl.ds(tile * t + ci * chunk, chunk)], buf_ref.at[n, slot], sem_ref.at[n, slot])
                    for n, src in enumerate(hbm)]

        @pl.when(kk == nk - 1)
        def _():
            dg = dgp = None
            for cp in fetch(0, 0):
                cp.start()
            for ci in range(nchunk):
                slot = ci % 2
                if ci + 1 < nchunk:
                    for cp in fetch(ci + 1, 1 - slot):
                        cp.start()
                for cp in fetch(ci, slot):
                    cp.wait()
                rows = slice(ci * chunk, (ci + 1) * chunk)
                xhat, r = _rms_stats(buf_ref[0, slot])
                dxn, dg_part = _rms_bwd(acc_ref[rows, :], xhat, r, g_ref[...])
                dx = buf_ref[1, slot] + dxn
                dx_ref[rows, :] = dx
                dg = dg_part if dg is None else dg + dg_part
                if post is not None:
                    ohat, ro = _rms_stats(buf_ref[2, slot])
                    do, dgp_part = _rms_bwd(dx, ohat, ro, ins[3][...])
                    outs[2][rows, :] = do.astype(BF16)
                    dgp = dgp_part if dgp is None else dgp + dgp_part
            _acc_rows(dg_ref, first, dg)
            if post is not None:
                _acc_rows(outs[3], first, dgp)

        host.after(step, nt * nk)

    res = pl.pallas_call(
        body, name=name, grid=(nt, nk), in_specs=host.in_specs, out_specs=host.out_specs, out_shape=host.out_shape,
        scratch_shapes=host.scratch, input_output_aliases=host.aliases,
        compiler_params=_cp("arbitrary", "arbitrary"))(*args, *host.args)
    return host.results(res)


def _half_add(g, r1, c_arr, name):
    nsh, rows, ns = g.shape
    h = rows // 2
    tr = min(ROW_TILE, h)
    per = h // tr

    def body(c_ref, g_ref, r_ref, o_ref):
        o_ref[...] = (g_ref[...].astype(F32) + r_ref[...].astype(F32)).astype(BF16)

    spec = pl.BlockSpec((None, tr, ns), lambda s, r, c: (s, r, 0))
    return pl.pallas_call(
        body, name=name,
        grid_spec=pltpu.PrefetchScalarGridSpec(
            num_scalar_prefetch=1, grid=(nsh, per),
            in_specs=[pl.BlockSpec((None, tr, ns), lambda s, r, c: (s, c[0] * per + r, 0)), spec], out_specs=spec),
        out_shape=jax.ShapeDtypeStruct((nsh, h, ns), BF16), compiler_params=_cp("parallel", "parallel"))(c_arr, g, r1)


def _sum_chips(hh, r2, mc_arr, name):
    _, h, ns = hh.shape
    tr = min(ROW_TILE, h)
    per = h // tr

    def body(mc_ref, h_ref, a_ref, b_ref, c_ref, o_ref):
        o_ref[...] = ((h_ref[...].astype(F32) + a_ref[...].astype(F32)) + b_ref[...].astype(F32)) + c_ref[...].astype(F32)

    got = lambda k: pl.BlockSpec((None, tr, ns), lambda r, mc: (k, r, 0))
    return pl.pallas_call(
        body, name=name,
        grid_spec=pltpu.PrefetchScalarGridSpec(
            num_scalar_prefetch=1, grid=(per,),
            in_specs=[pl.BlockSpec((None, tr, ns), lambda r, mc: (mc[0], r, 0)), got(0), got(1), got(2)],
            out_specs=pl.BlockSpec((tr, ns), lambda r, mc: (mc[1] * per + r, 0))),
        out_shape=jax.ShapeDtypeStruct((2 * h, ns), F32), compiler_params=_cp("parallel"))(mc_arr, hh, r2, r2, r2)


def _add2(a, b, name):
    def body(a_ref, b_ref, o_ref):
        o_ref[...] = a_ref[...] + b_ref[...]

    return pl.pallas_call(body, name=name, out_shape=jax.ShapeDtypeStruct(a.shape, a.dtype), compiler_params=_cp())(a, b)


def _sum_chips_ordered(s2, r2, mc_arr, name):
    rows, w = s2.shape
    rh = rows // 2

    def body(mc_ref, s_ref, a_ref, b_ref, c_ref, o_ref):
        me = mc_ref[0]
        acc = None
        for j in range(N_CHIPS):
            rel = jnp.bitwise_xor(me, j)
            v = jnp.where(rel == 0, s_ref[...], jnp.where(rel == 2, a_ref[...], jnp.where(rel == 1, b_ref[...], c_ref[...])))
            acc = v if acc is None else acc + v
        o_ref[...] = acc

    got = lambda k: pl.BlockSpec((None, rh, w), lambda i, mc: (k, 0, 0))
    return pl.pallas_call(
        body, name=name,
        grid_spec=pltpu.PrefetchScalarGridSpec(
            num_scalar_prefetch=1, grid=(1,),
            in_specs=[pl.BlockSpec((rh, w), lambda i, mc: (mc[1], 0)), got(0), got(1), got(2)],
            out_specs=pl.BlockSpec((rh, w), lambda i, mc: (mc[1], 0))),
        out_shape=jax.ShapeDtypeStruct((rows, w), F32), compiler_params=_cp("arbitrary"))(mc_arr, s2, r2, r2, r2)


def _adamw(w, g, m, v, name, comm=None):
    r, c = w.shape
    tr = ROW_TILE if r % ROW_TILE == 0 else r
    c1 = 1.0 / (1.0 - ADAM_B1 ** ADAM_STEP)
    c2 = 1.0 / (1.0 - ADAM_B2 ** ADAM_STEP)
    spec = pl.BlockSpec((tr, c), lambda i: (i, 0))
    host = _Host(comm, [spec] * 4, [spec] * 4, [jax.ShapeDtypeStruct((r, c), F32)] * 4, [])

    def body(*refs):
        (w_ref, g_ref, m_ref, v_ref), (go_ref, d_ref, nm_ref, nv_ref), _ = host.split(refs)
        step = pl.program_id(0)
        host.before(step, r // tr)
        gv = g_ref[...]
        go_ref[...] = gv
        nm = ADAM_B1 * m_ref[...] + (1.0 - ADAM_B1) * gv
        nv = ADAM_B2 * v_ref[...] + (1.0 - ADAM_B2) * (gv * gv)
        nm_ref[...] = nm
        nv_ref[...] = nv
        d_ref[...] = -ADAM_LR * ((nm * c1) / (jnp.sqrt(nv * c2) + ADAM_EPS) + ADAM_WD * w_ref[...])
        host.after(step, r // tr)

    outs = pl.pallas_call(
        body, name=name, grid=(r // tr,), in_specs=host.in_specs, out_specs=host.out_specs, out_shape=host.out_shape,
        scratch_shapes=host.scratch, input_output_aliases=host.aliases,
        compiler_params=_cp("arbitrary"))(w, g, m, v, *host.args)
    return host.results(outs)


def _gather_weights(bigs, pool_w, pack_w, pack_d, name):
    nb = len(bigs)
    smalls = [pool_w, pack_w, pack_d]
    q, cw, cd = pool_w.shape[1], pack_w.shape[1], pack_d.shape[1]
    pieces = [_GatherPlan(bigs, (j, j + 1, GATHER_PIECES)) for j in range(GATHER_PIECES)]
    for j, piece in enumerate(pieces):
        piece.base = 9 + j * piece.nsems

    def body(*refs):
        srcs, dsts = refs[:nb + 3], refs[nb + 3:2 * (nb + 3)]
        ssem, rsem, lsem = refs[2 * (nb + 3):]
        x, y, c, me, chips, sib = _place()

        def small_dst(n, chip):
            if n == 0:
                return dsts[nb].at[:, pl.ds(chip * q, q), :]
            return dsts[nb + n].at[:, pl.ds(chip * (cw if n == 1 else cd), cw if n == 1 else cd)]

        local = [pltpu.make_async_copy(srcs[nb + n], small_dst(n, me), lsem.at[n]) for n in range(3)]
        for cp in local:
            cp.start()
        sends = []
        for n in range(3):
            for k, chip in enumerate(chips):
                cp = _rcopy(srcs[nb + n], small_dst(n, me), ssem.at[3 * n + k], rsem.at[3 * n + k], (*chip, c))
                cp.start()
                sends.append(cp)
        big = (srcs[:nb], dsts[:nb], ssem, rsem)
        for stage in ("start", "relay", "relay_far", "finish"):
            for piece in pieces:
                getattr(piece, stage)(*big)
        for n in range(3):
            for k, chip in enumerate(chips):
                ref = small_dst(n, 2 * chip[0] + chip[1])
                _rcopy(ref, ref, ssem.at[3 * n + k], rsem.at[3 * n + k], (*chip, c)).wait_recv()
        for cp in sends:
            cp.wait_send()
        for cp in local:
            cp.wait()

    nsem = 9 + sum(piece.nsems for piece in pieces)
    out_shape = [jax.ShapeDtypeStruct(b.shape, b.dtype) for b in bigs]
    out_shape += [jax.ShapeDtypeStruct((pool_w.shape[0], N_CHIPS * q, pool_w.shape[2]), pool_w.dtype),
                  jax.ShapeDtypeStruct((pack_w.shape[0], N_CHIPS * cw), pack_w.dtype),
                  jax.ShapeDtypeStruct((pack_d.shape[0], N_CHIPS * cd), pack_d.dtype)]
    return pl.pallas_call(
        body, name=name, in_specs=[ANY] * (nb + 3), out_specs=[ANY] * (nb + 3), out_shape=out_shape,
        input_output_aliases={a: a for a in range(nb)},
        scratch_shapes=[pltpu.SemaphoreType.DMA((nsem,)), pltpu.SemaphoreType.DMA((nsem,)), pltpu.SemaphoreType.DMA((3,))],
        compiler_params=pltpu.CompilerParams(has_side_effects=True))(*bigs, *smalls)


def _swap_with_sibling(grads, wholes, name):
    n, nw = len(grads), len(wholes)
    halves = [g.shape[1] // 2 for g in grads]

    def body(*refs):
        srcs, dsts = refs[:n + nw], refs[n + nw:2 * (n + nw)]
        ssem, rsem = refs[2 * (n + nw):]
        x, y, c, me, chips, sib = _place()
        cps = [_rcopy(srcs[a].at[:, pl.ds((1 - c) * halves[a], halves[a]), :], dsts[a], ssem.at[a], rsem.at[a], sib)
               for a in range(n)]
        cps += [_rcopy(srcs[a], dsts[a], ssem.at[a], rsem.at[a], sib) for a in range(n, n + nw)]
        for cp in cps:
            cp.start()
        for cp in cps:
            cp.wait_recv()
        for cp in cps:
            cp.wait_send()

    out_shape = [jax.ShapeDtypeStruct((g.shape[0], h, g.shape[2]), g.dtype) for g, h in zip(grads, halves)]
    out_shape += [jax.ShapeDtypeStruct(w.shape, w.dtype) for w in wholes]
    return pl.pallas_call(
        body, name=name, in_specs=[ANY] * (n + nw), out_specs=[ANY] * (n + nw), out_shape=out_shape,
        scratch_shapes=[pltpu.SemaphoreType.DMA((n + nw,)), pltpu.SemaphoreType.DMA((n + nw,))],
        compiler_params=pltpu.CompilerParams(has_side_effects=True))(*grads, *wholes)


def _scatter_to_chips(halves_in, small, name):
    n = len(halves_in)
    rh = small.shape[0] // 2

    def body(*refs):
        srcs, dsts = refs[:n + 1], refs[n + 1:2 * (n + 1)]
        ssem, rsem = refs[2 * (n + 1):]
        x, y, c, me, chips, sib = _place()
        cps = []
        for a in range(n + 1):
            for k, chip in enumerate(chips):
                src = srcs[a].at[2 * chip[0] + chip[1]] if a < n else srcs[a].at[pl.ds(c * rh, rh)]
                cps.append(_rcopy(src, dsts[a].at[k], ssem.at[3 * a + k], rsem.at[3 * a + k], (*chip, c)))
        for cp in cps:
            cp.start()
        for cp in cps:
            cp.wait_recv()
        for cp in cps:
            cp.wait_send()

    out_shape = [jax.ShapeDtypeStruct((3,) + h.shape[1:], h.dtype) for h in halves_in]
    out_shape.append(jax.ShapeDtypeStruct((3, rh, small.shape[1]), small.dtype))
    return pl.pallas_call(
        body, name=name, in_specs=[ANY] * (n + 1), out_specs=[ANY] * (n + 1), out_shape=out_shape,
        scratch_shapes=[pltpu.SemaphoreType.DMA((3 * (n + 1),)), pltpu.SemaphoreType.DMA((3 * (n + 1),))],
        compiler_params=pltpu.CompilerParams(has_side_effects=True))(*halves_in, small)


def _join_halves(parts, name):
    n = len(parts)

    def body(*refs):
        srcs, dsts = refs[:n], refs[n:2 * n]
        ssem, rsem = refs[2 * n:]
        x, y, c, me, chips, sib = _place()
        cps = []
        for a in range(n):
            h = srcs[a].shape[0] // 2
            cps.append(_rcopy(srcs[a].at[pl.ds(c * h, h)], dsts[a].at[pl.ds(c * h, h)], ssem.at[a], rsem.at[a], sib))
        for cp in cps:
            cp.start()
        for a in range(n):
            h = srcs[a].shape[0] // 2
            theirs = dsts[a].at[pl.ds((1 - c) * h, h)]
            _rcopy(theirs, theirs, ssem.at[a], rsem.at[a], sib).wait_recv()
        for cp in cps:
            cp.wait_send()

    out_shape = [jax.ShapeDtypeStruct(p.shape, p.dtype) for p in parts]
    return pl.pallas_call(
        body, name=name, in_specs=[ANY] * n, out_specs=[ANY] * n, out_shape=out_shape,
        input_output_aliases={a: a for a in range(n)},
        scratch_shapes=[pltpu.SemaphoreType.DMA((n,)), pltpu.SemaphoreType.DMA((n,))],
        compiler_params=pltpu.CompilerParams(has_side_effects=True))(*parts)


def _scatter_start(h, name):
    land = (3,) + h.shape[1:]

    def body(h_ref, land_ref, send_sems, recv_sems, h_thru, land_thru, token):
        x, y, c, me, chips, sib = _place()
        for k, chip in enumerate(chips):
            _rcopy(h_ref.at[2 * chip[0] + chip[1]], land_ref.at[k], send_sems.at[k], recv_sems.at[k], (*chip, c)).start()
        token[...] = jnp.zeros_like(token)

    hbm = pl.BlockSpec(memory_space=pltpu.HBM)
    sem = pl.BlockSpec(memory_space=pltpu.SEMAPHORE)
    return pl.pallas_call(
        body, name=name,
        out_shape=(pltpu.SemaphoreType.DMA((3,)), pltpu.SemaphoreType.DMA((3,)), pltpu.HBM(h.shape, h.dtype),
                   pltpu.HBM(land, h.dtype), jax.ShapeDtypeStruct((8, LANES), F32)),
        in_specs=(hbm, hbm), out_specs=(sem, sem, hbm, hbm, pl.BlockSpec(memory_space=pltpu.VMEM)),
        input_output_aliases={0: 2, 1: 3},
        compiler_params=pltpu.CompilerParams(has_side_effects=pltpu.SideEffectType.DATAFLOW_SIDE_EFFECTING))(
            pltpu.with_memory_space_constraint(h, pltpu.HBM),
            pltpu.with_memory_space_constraint(lax.empty(land, h.dtype), pltpu.HBM))


def _scatter_wait(send_sems, recv_sems, h_thru, land_thru, after, name):
    def body(h_ref, land_ref, send_sems, recv_sems, after_ref, h_dead, got_ref):
        x, y, c, me, chips, sib = _place()
        for k, chip in enumerate(chips):
            cp = _rcopy(h_ref.at[2 * chip[0] + chip[1]], land_ref.at[k], send_sems.at[k], recv_sems.at[k], (*chip, c))
            cp.wait_send()
            cp.wait_recv()

    hbm = pl.BlockSpec(memory_space=pltpu.HBM)
    sem = pl.BlockSpec(memory_space=pltpu.SEMAPHORE)
    return pl.pallas_call(
        body, name=name,
        out_shape=(pltpu.HBM(h_thru.shape, h_thru.dtype), pltpu.HBM(land_thru.shape, land_thru.dtype)),
        in_specs=(hbm, hbm, sem, sem, ANY), out_specs=(hbm, hbm), input_output_aliases={0: 0, 1: 1},
        compiler_params=pltpu.CompilerParams(has_side_effects=pltpu.SideEffectType.DATAFLOW_SIDE_EFFECTING))(
            h_thru, land_thru, send_sems, recv_sems, after)


def _pad_rows(a, rows):
    return jnp.pad(a, ((0, rows - a.shape[0]), (0, 0)))


def _stack_rows(parts, multiple):
    padded = [_pad_rows(p, -(-p.shape[0] // 8) * 8) for p in parts]
    starts, at = [], 0
    for p in padded:
        starts.append(at)
        at += p.shape[0]
    total = -(-at // multiple) * multiple
    if total > at:
        padded.append(jnp.zeros((total - at, parts[0].shape[1]), parts[0].dtype))
    return jnp.concatenate(padded, axis=0), starts


def kernel(x, ln_pre_even, w_in_even, pool_w, pool_scale, w_out_even, ln_post_even, ln_pre_odd, w_in_odd, sconv_w, dconv_w, dconv_b, cnorm_g, cnorm_b, w_out_odd, ln_post_odd, loss_target, m_ln_pre_even, m_w_in_even, m_pool_w, m_pool_scale, m_w_out_even, m_ln_post_even, m_ln_pre_odd, m_w_in_odd, m_sconv_w, m_dconv_w, m_dconv_b, m_cnorm_g, m_cnorm_b, m_w_out_odd, m_ln_post_odd, v_ln_pre_even, v_w_in_even, v_pool_w, v_pool_scale, v_w_out_even, v_ln_post_even, v_ln_pre_odd, v_w_in_odd, v_sconv_w, v_dconv_w, v_dconv_b, v_cnorm_g, v_cnorm_b, v_w_out_odd, v_ln_post_odd):
    _, s, d = x.shape
    half = d // 2
    cw = half // N_CHIPS
    ng, q, gd = pool_w.shape[1:]
    k3, k31 = sconv_w.shape[1], dconv_w.shape[1]
    x2d, tgt = x[0], loss_target[0]
    me = 2 * lax.axis_index("x") + lax.axis_index("y")
    core = lax.axis_index("c")
    c_arr = jnp.reshape(core, (1,)).astype(jnp.int32)
    me_arr = jnp.reshape(me, (1,)).astype(jnp.int32)
    mc_arr = jnp.stack([me, core]).astype(jnp.int32)

    shards = [w_in_even[0], w_out_even[0], w_in_odd[0], w_out_odd[0]]
    slabs = [_cast_bf16_own_slab(w, me_arr, f"cast_w{n}") for n, w in enumerate(shards)]
    pool_w_b = _cast_bf16(pool_w[0].reshape(ng * q, gd), "cast_pool_w").reshape(ng, q, gd)
    pack_w, at_w = _stack_rows([sconv_w[0], dconv_w[0], dconv_b, cnorm_g, cnorm_b], 8)
    pack_d, at_d = _stack_rows([ln_pre_odd, ln_post_odd], 8)
    win_e, pool_w_f, pack_w_f, pack_d_f = _gather_weights(slabs[:1], pool_w_b, pack_w, pack_d, "gather_first")
    sconv_f = pack_w_f[at_w[0]:at_w[0] + k3]
    dconv_f = pack_w_f[at_w[1]:at_w[1] + k31]
    dconv_b_f, cnorm_g_f, cnorm_b_f = (pack_w_f[at_w[n]:at_w[n] + 1] for n in (2, 3, 4))
    ln_pre_odd_f = pack_d_f[at_d[0]:at_d[0] + 1]
    ln_post_odd_f = pack_d_f[at_d[1]:at_d[1] + 1]

    def reduce_half(g, name):
        (got,) = _swap_with_sibling([g], [], "swap_" + name)
        return _half_add(g, got, c_arr, "half_add_" + name)

    h0 = _rms_fwd(x2d, ln_pre_even, "rms_pre_even")
    plans = _Multi([_GatherPlan([slabs[1]], at=(0.6, 0.88)), _GatherPlan([slabs[2]], (0, 1, 4), at=(0.6, 0.88))])
    p_e, extra = _mm_nn(h0, win_e, "proj_in_even", plans)
    (wout_e,), (win_o,) = plans.results(extra)
    wout_e = wout_e.reshape(d, d)
    att, ltot, (win_o,) = _sba_fwd(p_e, half, "sba_fwd", _GatherPlan([win_o], (1, 4, 4), at=(0.69, 0.94)))
    y_e = _even_mix_fwd(p_e, att, pool_w_f, pool_scale, d, "even_mix_fwd")
    o_e, x1, h1 = _mm_out_even(y_e, wout_e, x2d, ln_post_even, ln_pre_odd_f, "proj_out_even")
    p_o, (wout_o,) = _mm_nn(h1, win_o, "proj_in_odd", _GatherPlan([slabs[3]]))
    wout_o = wout_o.reshape(d, d)
    y_o, s3, d1 = _odd_mix_fwd(p_o, sconv_f, dconv_f, dconv_b_f, cnorm_g_f, cnorm_b_f, d, "odd_mix_fwd")
    do_o, dx2, loss_blk, dln_post_odd = _mm_out_odd(y_o, wout_o, x1, ln_post_odd_f, tgt, "proj_out_odd_loss")

    dy_o = _mm_nt(do_o, wout_o, "dy_odd")
    g_wout_o = _mm_tn(y_o, do_o, 1, "dw_out_odd")[0].reshape(N_CHIPS, d // N_CHIPS, d)
    (dbc, dgate_o, ds3, dd1, dcnorm_g, dcnorm_b, ddconv_b), (got,) = _odd_bwd_rows(
        p_o, s3, d1, dy_o, cnorm_g_f, cnorm_b_f, d, "odd_bwd_rows", _SwapPlan([g_wout_o]))
    h_wout_o = _half_add(g_wout_o, got, c_arr, "half_add_out_odd")
    dhc, dcc, dga, dgb, dsconv, ddconv = _odd_bwd_conv(p_o, ds3, dd1, sconv_f, dconv_f, d, "odd_bwd_conv")
    dp_o = jnp.concatenate([dhc, dbc, dcc, dga, dgb, dgate_o], axis=1)
    g_win_o, (s_wout_o,) = _mm_tn(h1, dp_o, N_CHIPS, "dw_in_odd", _ScatterPlan([h_wout_o]))
    (dx1, dln_pre_odd, do_e, dln_post_even), (got,) = _mm_in_bwd(
        dp_o, win_o, x1, ln_pre_odd_f, dx2, (o_e, ln_post_even), "dx_odd", _SwapPlan([g_win_o]))
    h_win_o = _half_add(g_win_o, got, c_arr, "half_add_in_odd")

    dy_e = _mm_nt(do_e, wout_e, "dy_even")
    g_wout_e = _mm_tn(y_e, do_e, 1, "dw_out_even")[0].reshape(N_CHIPS, d // N_CHIPS, d)
    (datt, du, dgate_e, dpool_scale, dpool_w), (got,) = _even_mix_bwd(
        p_e, att, dy_e, pool_w_f, pool_scale, d, "even_mix_bwd", _SwapPlan([g_wout_e]))
    h_wout_e = _half_add(g_wout_e, got, c_arr, "half_add_out_even")
    two = lambda v: v.reshape(2, half)
    small_parts = [dpool_scale, two(dln_post_even), two(dln_pre_odd), two(dln_post_odd),
                   dsconv, ddconv, ddconv_b, dcnorm_g, dcnorm_b, dpool_w.reshape(gd, half)]
    small, at_s = _stack_rows(small_parts, 16)
    (small1,) = _swap_with_sibling([], [small], "swap_small")
    small2 = _add2(small, small1, "small_add")
    plans = _Multi([_ScatterPlan([h_win_o]), _ShareHalfPlan([small2])])
    dq, dk, dv, extra = _sba_bwd(p_e, ltot, datt, half, "sba_bwd", plans)
    (s_win_o,), (small_got,) = plans.results(extra)
    dp_e = jnp.concatenate([dq, dk, dv, du, dgate_e], axis=1)
    g_win_e, (s_wout_e,) = _mm_tn(h0, dp_e, N_CHIPS, "dw_in_even", _ScatterPlan([h_wout_e]))
    h_win_e = reduce_half(g_win_e, "in_even")
    send_sems, recv_sems, h_win_e, landing, token = _scatter_start(h_win_e, "scatter_in_even_start")
    (grad_x, dln_pre_even), _ = _mm_in_bwd(dp_e, win_e, x2d, ln_pre_even + token[0:1, 0:1], dx1, None, "dx_even")

    last, at_l = _stack_rows([two(dln_pre_even), jnp.pad(loss_blk[0:1], ((0, 0), (0, half - LANES)))], 16)
    (last1,) = _swap_with_sibling([], [last], "swap_last")
    last2 = _add2(last, last1, "last_add")
    (last_got,) = _scatter_to_chips([], last2, "scatter_last")
    pairs = [(h_wout_e, s_wout_e), (h_win_o, s_win_o), (h_wout_o, s_wout_o)]
    parts = [_sum_chips(h, r, mc_arr, f"sum_chips{n + 1}") for n, (h, r) in enumerate(pairs)]
    parts.append(_sum_chips_ordered(small2, small_got, mc_arr, "small_sum"))
    parts.append(_sum_chips_ordered(last2, last_got, mc_arr, "last_sum"))
    h_win_e, s_win_e = _scatter_wait(send_sems, recv_sems, h_win_e, landing, parts[-1], "scatter_in_even_wait")
    parts.insert(0, _sum_chips(h_win_e, s_win_e, mc_arr, "sum_chips0"))
    gw_in_e, gw_out_e, gw_in_o, gw_out_o, red, red_last = _join_halves(parts, "join_halves")
    loss = red_last[at_l[1], 0]

    def rows(n, cnt):
        return red[at_s[n]:at_s[n] + cnt]

    def mine(a, width):
        return lax.dynamic_slice_in_dim(a, me * width, width, axis=1)

    quarter = d // N_CHIPS
    g_small = {
        "ln_pre_even": red_last[at_l[0]:at_l[0] + 2].reshape(1, d),
        "pool_scale": rows(0, 1),
        "ln_post_even": rows(1, 2).reshape(1, d),
        "ln_pre_odd": mine(rows(2, 2).reshape(1, d), quarter),
        "ln_post_odd": mine(rows(3, 2).reshape(1, d), quarter),
        "sconv_w": mine(rows(4, k3), cw),
        "dconv_w": mine(rows(5, k31), cw),
        "dconv_b": mine(rows(6, 1), cw),
        "cnorm_g": mine(rows(7, 1), cw),
        "cnorm_b": mine(rows(8, 1), cw),
        "pool_w": lax.dynamic_slice_in_dim(rows(9, gd).reshape(ng, gd, gd), me * q, q, axis=1).reshape(ng * q, gd),
    }
    w2d = {
        "ln_pre_even": ln_pre_even, "w_in_even": w_in_even[0], "pool_w": pool_w[0].reshape(ng * q, gd),
        "pool_scale": pool_scale, "w_out_even": w_out_even[0], "ln_post_even": ln_post_even, "ln_pre_odd": ln_pre_odd,
        "w_in_odd": w_in_odd[0], "sconv_w": sconv_w[0], "dconv_w": dconv_w[0], "dconv_b": dconv_b, "cnorm_g": cnorm_g,
        "cnorm_b": cnorm_b, "w_out_odd": w_out_odd[0], "ln_post_odd": ln_post_odd,
    }
    moments = {
        "ln_pre_even": (m_ln_pre_even, v_ln_pre_even), "w_in_even": (m_w_in_even, v_w_in_even),
        "pool_w": (m_pool_w, v_pool_w), "pool_scale": (m_pool_scale, v_pool_scale),
        "w_out_even": (m_w_out_even, v_w_out_even), "ln_post_even": (m_ln_post_even, v_ln_post_even),
        "ln_pre_odd": (m_ln_pre_odd, v_ln_pre_odd), "w_in_odd": (m_w_in_odd, v_w_in_odd),
        "sconv_w": (m_sconv_w, v_sconv_w), "dconv_w": (m_dconv_w, v_dconv_w), "dconv_b": (m_dconv_b, v_dconv_b),
        "cnorm_g": (m_cnorm_g, v_cnorm_g), "cnorm_b": (m_cnorm_b, v_cnorm_b),
        "w_out_odd": (m_w_out_odd, v_w_out_odd), "ln_post_odd": (m_ln_post_odd, v_ln_post_odd),
    }
    g2d = dict(g_small, w_in_even=gw_in_e, w_out_even=gw_out_e, w_in_odd=gw_in_o, w_out_odd=gw_out_o)
    updates = {}
    for name, w in w2d.items():
        m_in, v_in = moments[name]
        updates[name], _ = _adamw(w, g2d[name], m_in.reshape(w.shape), v_in.reshape(w.shape), "adamw_" + name)
    outs = [[u.reshape(moments[name][0].shape) for u in updates[name]] for name in w2d]
    grads_out, deltas, new_m, new_v = zip(*outs)
    return (loss, grad_x.reshape(x.shape), *grads_out, *deltas, *new_m, *new_v)
```

```python
import functools
import math

import jax
import jax.numpy as jnp
from jax import lax
from jax.experimental import pallas as pl
from jax.experimental.pallas import tpu as pltpu

F32 = jnp.float32
BF16 = jnp.bfloat16
EPS = 1e-6
N_CHIPS = 4
VMEM_LIMIT_V7X = 56 << 20
HEAD_DIM = 128
ATT_BLOCK = 256
POOL_WINDOWS = (2, 4, 8, 16)
ROW_TILE = 256
POOL_HALO = 16
CONV_HALO = 32
LANES = 128
ADAM_LR, ADAM_B1, ADAM_B2, ADAM_EPS, ADAM_WD, ADAM_STEP = 0.001, 0.9, 0.999, 1e-08, 0.01, 10
MESH_ID = pl.DeviceIdType.MESH
ANY = pl.BlockSpec(memory_space=pl.ANY)


def _cp(*sem):
    return pltpu.CompilerParams(dimension_semantics=sem or None, vmem_limit_bytes=VMEM_LIMIT_V7X)


def _pick_tile(n, cap):
    best = None
    for t in range(LANES, min(n, cap) + 1, LANES):
        if n % t == 0:
            best = t
    assert best is not None, (n, cap)
    return best


def _sigmoid(x):
    return 1.0 / (1.0 + jnp.exp(-x))


def _silu(x):
    return x * _sigmoid(x)


def _dsilu(x):
    s = _sigmoid(x)
    return s * (1.0 + x * (1.0 - s))


def _log_sigmoid(z):
    return jnp.minimum(z, 0.0) - jnp.log(1.0 + jnp.exp(-jnp.abs(z)))


def _rms_stats(x):
    r = lax.rsqrt(jnp.mean(x * x, axis=-1, keepdims=True) + EPS)
    return x * r, r


def _rms_bwd(dh, xhat, r, g):
    dxh = dh * g
    dx = r * (dxh - xhat * jnp.mean(dxh * xhat, axis=-1, keepdims=True))
    return dx, jnp.sum(dh * xhat, axis=0, keepdims=True)


def _acc_rows(ref, first, val):
    @pl.when(first)
    def _():
        ref[...] = val

    @pl.when(jnp.logical_not(first))
    def _():
        ref[...] += val


def _rcopy(src, dst, ssem, rsem, dev):
    return pltpu.make_async_remote_copy(src_ref=src, dst_ref=dst, send_sem=ssem, recv_sem=rsem,
                                        device_id=dev, device_id_type=MESH_ID)


def _place():
    x, y, c = lax.axis_index("x"), lax.axis_index("y"), lax.axis_index("c")
    chips = [(1 - x, y), (x, 1 - y), (1 - x, 1 - y)]
    return x, y, c, 2 * x + y, chips, (x, y, 1 - c)


class _GatherPlan:
    PER_ARRAY = 7

    def __init__(self, arrays, part=(0, 1, 1), at=(0.5, 0.8)):
        self.operands = list(arrays)
        self.out_shapes = [jax.ShapeDtypeStruct(a.shape, a.dtype) for a in arrays]
        self.aliases = {i: i for i in range(len(arrays))}
        self.nsems = self.PER_ARRAY * len(arrays)
        self.base = 0
        self.halves = [a.shape[1] // 2 for a in arrays]
        self.part = part
        self.at = at

    def schedule(self):
        return [(0.0, self.start), (self.at[0], self.relay), (self.at[1], self.relay_far)]

    def _rows(self, ref, a, chip, half, quarter=None):
        lo, hi, n = self.part
        h = self.halves[a]
        first, size = half * h + lo * h // n, (hi - lo) * h // n
        if quarter is not None:
            first, size = first + quarter * (size // 2), size // 2
        return ref.at[chip, pl.ds(first, size)]

    def _copy(self, src, dst, a, n, ssem, rsem, dev):
        return _rcopy(src, dst, ssem.at[self.base + self.PER_ARRAY * a + n], rsem.at[self.base + self.PER_ARRAY * a + n], dev)

    def _own(self, ins, outs, ssem, rsem):
        x, y, c, me, chips, sib = _place()
        return [self._copy(self._rows(ins[a], a, me, c), self._rows(outs[a], a, me, c), a, k, ssem, rsem, (*chips[k], c))
                for a in range(len(ins)) for k in (0, 1)]

    def _relays(self, outs, ssem, rsem, a, k):
        x, y, c, me, chips, sib = _place()
        chip = 2 * chips[k][0] + chips[k][1]
        whole, quarter = self._rows(outs[a], a, chip, c), self._rows(outs[a], a, chip, c, k)
        return (self._copy(whole, whole, a, k, ssem, rsem, (*chips[k], c)),
                self._copy(quarter, quarter, a, 2 + k, ssem, rsem, (*chips[1 - k], c)),
                self._copy(whole, whole, a, 4 + k, ssem, rsem, sib))

    def _far(self, outs, ssem, rsem, a):
        x, y, c, me, chips, sib = _place()
        chip = 2 * chips[2][0] + chips[2][1]
        whole = self._rows(outs[a], a, chip, c)
        got = [self._copy(q, q, a, 2 + k, ssem, rsem, (*chips[1 - k], c))
               for k, q in enumerate([self._rows(outs[a], a, chip, c, 0), self._rows(outs[a], a, chip, c, 1)])]
        return got, self._copy(whole, whole, a, 6, ssem, rsem, sib)

    def start(self, ins, outs, ssem, rsem):
        for cp in self._own(ins, outs, ssem, rsem):
            cp.start()

    def relay(self, ins, outs, ssem, rsem):
        for a in range(len(outs)):
            for k in (0, 1):
                landed, onward, to_sibling = self._relays(outs, ssem, rsem, a, k)
                landed.wait_recv()
                onward.start()
                to_sibling.start()

    def relay_far(self, ins, outs, ssem, rsem):
        for a in range(len(outs)):
            got, to_sibling = self._far(outs, ssem, rsem, a)
            for cp in got:
                cp.wait_recv()
            to_sibling.start()

    def finish(self, ins, outs, ssem, rsem):
        x, y, c, me, chips, sib = _place()
        for a in range(len(outs)):
            for k in range(3):
                ref = self._rows(outs[a], a, 2 * chips[k][0] + chips[k][1], 1 - c)
                self._copy(ref, ref, a, 4 + k, ssem, rsem, sib).wait_recv()
        for cp in self._own(ins, outs, ssem, rsem):
            cp.wait_send()
        for a in range(len(outs)):
            for k in (0, 1):
                _, onward, to_sibling = self._relays(outs, ssem, rsem, a, k)
                onward.wait_send()
                to_sibling.wait_send()
            self._far(outs, ssem, rsem, a)[1].wait_send()


class _ScatterPlan:
    def __init__(self, arrays, part=(0, 1, 1), into=None):
        self.n = len(arrays)
        self.operands = list(arrays) + list(into or [])
        self.out_shapes = [jax.ShapeDtypeStruct((3,) + a.shape[1:], a.dtype) for a in arrays]
        self.aliases = {self.n + i: i for i in range(self.n)} if into else {}
        self.nsems = 3 * self.n
        self.base = 0
        self.part = part

    def _copies(self, ins, outs, ssem, rsem):
        x, y, c, me, chips, sib = _place()
        lo, hi, n = self.part
        out = []
        for a in range(self.n):
            h = ins[a].shape[1]
            rows = pl.ds(lo * h // n, (hi - lo) * h // n)
            for k, chip in enumerate(chips):
                out.append(_rcopy(ins[a].at[2 * chip[0] + chip[1], rows], outs[a].at[k, rows],
                                  ssem.at[self.base + 3 * a + k], rsem.at[self.base + 3 * a + k], (*chip, c)))
        return out

    def schedule(self):
        return [(0.0, self.start)]

    def start(self, ins, outs, ssem, rsem):
        for cp in self._copies(ins, outs, ssem, rsem):
            cp.start()

    def finish(self, ins, outs, ssem, rsem):
        cps = self._copies(ins, outs, ssem, rsem)
        for cp in cps:
            cp.wait_recv()
        for cp in cps:
            cp.wait_send()


class _ShareHalfPlan(_ScatterPlan):
    def __init__(self, arrays):
        super().__init__(arrays)
        self.out_shapes = [jax.ShapeDtypeStruct((3, a.shape[0] // 2, a.shape[1]), a.dtype) for a in arrays]

    def _copies(self, ins, outs, ssem, rsem):
        x, y, c, me, chips, sib = _place()
        out = []
        for a in range(self.n):
            rh = ins[a].shape[0] // 2
            for k, chip in enumerate(chips):
                out.append(_rcopy(ins[a].at[pl.ds(c * rh, rh)], outs[a].at[k],
                                  ssem.at[self.base + 3 * a + k], rsem.at[self.base + 3 * a + k], (*chip, c)))
        return out


class _SwapPlan:
    def __init__(self, grads):
        self.operands = list(grads)
        self.out_shapes = [jax.ShapeDtypeStruct((g.shape[0], g.shape[1] // 2, g.shape[2]), g.dtype) for g in grads]
        self.aliases = {}
        self.nsems = len(grads)
        self.base = 0

    def _copies(self, ins, outs, ssem, rsem):
        x, y, c, me, chips, sib = _place()
        out = []
        for a, src in enumerate(ins):
            h = src.shape[1] // 2
            out.append(_rcopy(src.at[:, pl.ds((1 - c) * h, h), :], outs[a], ssem.at[self.base + a], rsem.at[self.base + a], sib))
        return out

    def schedule(self):
        return [(0.0, self.start)]

    def start(self, ins, outs, ssem, rsem):
        for cp in self._copies(ins, outs, ssem, rsem):
            cp.start()

    def finish(self, ins, outs, ssem, rsem):
        cps = self._copies(ins, outs, ssem, rsem)
        for cp in cps:
            cp.wait_recv()
        for cp in cps:
            cp.wait_send()


class _Multi:
    def __init__(self, plans):
        self.plans = plans
        self.operands, self.out_shapes, self.aliases, self.nsems = [], [], {}, 0
        self.spans = []
        for p in plans:
            ni, no = len(self.operands), len(self.out_shapes)
            self.spans.append((ni, ni + len(p.operands), no, no + len(p.out_shapes)))
            self.aliases.update({ni + i: no + j for i, j in p.aliases.items()})
            p.base = self.nsems
            self.nsems += p.nsems
            self.operands += p.operands
            self.out_shapes += p.out_shapes

    def schedule(self):
        def bound(fn, span):
            i0, i1, o0, o1 = span
            return lambda ins, outs, ssem, rsem: fn(ins[i0:i1], outs[o0:o1], ssem, rsem)

        stages = [(at, bound(fn, span)) for p, span in zip(self.plans, self.spans) for at, fn in p.schedule()]
        return sorted(stages, key=lambda s: s[0])

    def finish(self, ins, outs, ssem, rsem):
        for p, (i0, i1, o0, o1) in zip(self.plans, self.spans):
            p.finish(ins[i0:i1], outs[o0:o1], ssem, rsem)

    def results(self, extra):
        return [list(extra[o0:o1]) for (_, _, o0, o1) in self.spans]


class _Host:
    def __init__(self, comm, in_specs, out_specs, out_shape, scratch):
        self.comm = comm
        self.n_in, self.n_out = len(in_specs), len(out_specs)
        self.in_specs, self.out_specs, self.out_shape, self.scratch = list(in_specs), list(out_specs), list(out_shape), list(scratch)
        self.aliases = {}
        self.args = []
        if comm is not None:
            self.in_specs += [ANY] * len(comm.operands)
            self.out_specs += [ANY] * len(comm.out_shapes)
            self.out_shape += comm.out_shapes
            self.scratch += [pltpu.SemaphoreType.DMA((comm.nsems,)), pltpu.SemaphoreType.DMA((comm.nsems,))]
            self.aliases = {self.n_in + i: self.n_out + j for i, j in comm.aliases.items()}
            self.args = list(comm.operands)

    def split(self, refs):
        nc = len(self.args)
        nco = len(self.out_shape) - self.n_out
        ins, p = refs[:self.n_in], self.n_in + nc
        outs, rest = refs[p:p + self.n_out], refs[p + self.n_out + nco:]
        self._cargs = None
        if self.comm is not None:
            self._cargs = (refs[self.n_in:p], refs[p + self.n_out:p + self.n_out + nco], rest[-2], rest[-1])
            rest = rest[:-2]
        return ins, outs, rest

    def before(self, step, total):
        if self.comm is None:
            return

        for at, stage in self.comm.schedule():
            pl.when(step == min(total - 1, int(at * total)))(functools.partial(stage, *self._cargs))

    def after(self, step, total):
        if self.comm is None:
            return

        @pl.when(step == total - 1)
        def _():
            self.comm.finish(*self._cargs)

    def results(self, outs):
        return outs[:self.n_out], outs[self.n_out:]


def _cast_bf16(x, name):
    r, c = x.shape
    tr = ROW_TILE if r % ROW_TILE == 0 else r

    def body(x_ref, o_ref):
        o_ref[...] = x_ref[...].astype(BF16)

    return pl.pallas_call(
        body, name=name, grid=(r // tr,),
        in_specs=[pl.BlockSpec((tr, c), lambda i: (i, 0))],
        out_specs=pl.BlockSpec((tr, c), lambda i: (i, 0)),
        out_shape=jax.ShapeDtypeStruct((r, c), BF16), compiler_params=_cp("parallel"))(x)


def _cast_bf16_own_slab(x, me_arr, name):
    r, c = x.shape
    tr = ROW_TILE if r % ROW_TILE == 0 else r

    def body(me_ref, x_ref, o_ref):
        o_ref[...] = x_ref[...].astype(BF16)

    return pl.pallas_call(
        body, name=name,
        grid_spec=pltpu.PrefetchScalarGridSpec(
            num_scalar_prefetch=1, grid=(r // tr,),
            in_specs=[pl.BlockSpec((tr, c), lambda i, me: (i, 0))],
            out_specs=pl.BlockSpec((None, tr, c), lambda i, me: (me[0], i, 0))),
        out_shape=jax.ShapeDtypeStruct((N_CHIPS, r, c), BF16), compiler_params=_cp("parallel"))(me_arr, x)


def _rms_fwd(x, g, name):
    s, d = x.shape

    def body(x_ref, g_ref, h_ref):
        xhat, _ = _rms_stats(x_ref[...])
        h_ref[...] = (xhat * g_ref[...]).astype(BF16)

    return pl.pallas_call(
        body, name=name, grid=(s // ROW_TILE,),
        in_specs=[pl.BlockSpec((ROW_TILE, d), lambda i: (i, 0)), pl.BlockSpec((1, d), lambda i: (0, 0))],
        out_specs=pl.BlockSpec((ROW_TILE, d), lambda i: (i, 0)),
        out_shape=jax.ShapeDtypeStruct((s, d), BF16), compiler_params=_cp("parallel"))(x, g)


def _mm_nn(a, w3, name, comm=None):
    m, k = a.shape
    nsh, _, ns = w3.shape
    tm = 512 if m % 512 == 0 else ROW_TILE
    tn = _pick_tile(ns, 1024)
    per = ns // tn
    grid = (nsh * per, m // tm)
    host = _Host(comm,
                 [pl.BlockSpec((tm, k), lambda n, i: (i, 0)), pl.BlockSpec((None, k, tn), lambda n, i: (n // per, 0, n % per))],
                 [pl.BlockSpec((tm, tn), lambda n, i: (i, n))], [jax.ShapeDtypeStruct((m, nsh * ns), F32)], [])

    def body(*refs):
        (a_ref, w_ref), (o_ref,), _ = host.split(refs)
        step = pl.program_id(0) * grid[1] + pl.program_id(1)
        host.before(step, grid[0] * grid[1])
        o_ref[...] = jnp.dot(a_ref[...], w_ref[...], preferred_element_type=F32)
        host.after(step, grid[0] * grid[1])

    outs = pl.pallas_call(
        body, name=name, grid=grid, in_specs=host.in_specs, out_specs=host.out_specs, out_shape=host.out_shape,
        scratch_shapes=host.scratch, input_output_aliases=host.aliases,
        compiler_params=_cp("arbitrary", "arbitrary"))(a, w3, *host.args)
    (out,), extra = host.results(outs)
    return out, extra


def _mm_nt(a, b, name):
    m, k = a.shape
    n = b.shape[0]
    tm = 512 if m % 512 == 0 else ROW_TILE

    def body(a_ref, b_ref, o_ref):
        o_ref[...] = lax.dot_general(a_ref[...], b_ref[...], (((1,), (1,)), ((), ())), preferred_element_type=F32)

    return pl.pallas_call(
        body, name=name, grid=(m // tm,),
        in_specs=[pl.BlockSpec((tm, k), lambda i: (i, 0)), pl.BlockSpec((n, k), lambda i: (0, 0))],
        out_specs=pl.BlockSpec((tm, n), lambda i: (i, 0)),
        out_shape=jax.ShapeDtypeStruct((m, n), F32), compiler_params=_cp("parallel"))(a, b)


def _mm_tn(a, b, nsh, name, comm=None):
    s, m = a.shape
    n = b.shape[1]
    ns = n // nsh
    tm = 512 if m % 512 == 0 else ROW_TILE
    tn = _pick_tile(ns, 1024)
    per = ns // tn
    grid = (nsh * per, m // tm)
    host = _Host(comm, [pl.BlockSpec((s, tm), lambda j, i: (0, i)), pl.BlockSpec((s, tn), lambda j, i: (0, j))],
                 [pl.BlockSpec((None, tm, tn), lambda j, i: (j // per, i, j % per))],
                 [jax.ShapeDtypeStruct((nsh, m, ns), BF16)], [])

    def body(*refs):
        (a_ref, b_ref), (o_ref,), _ = host.split(refs)
        step = pl.program_id(0) * grid[1] + pl.program_id(1)
        host.before(step, grid[0] * grid[1])
        o_ref[...] = lax.dot_general(a_ref[...], b_ref[...], (((0,), (0,)), ((), ())),
                                     preferred_element_type=F32).astype(BF16)
        host.after(step, grid[0] * grid[1])

    outs = pl.pallas_call(
        body, name=name, grid=grid, in_specs=host.in_specs, out_specs=host.out_specs, out_shape=host.out_shape,
        scratch_shapes=host.scratch, input_output_aliases=host.aliases,
        compiler_params=_cp("arbitrary", "arbitrary"))(a, b, *host.args)
    (out,), extra = host.results(outs)
    return out, extra


def _tri(n, rel):
    row = lax.broadcasted_iota(jnp.int32, (2 * n, n), 0)
    col = lax.broadcasted_iota(jnp.int32, (2 * n, n), 1)
    return jnp.where(rel(jnp.where(row >= n, row - n, row), col), 1.0, 0.0).astype(BF16)


def _dot_split(x, tri2):
    hi = x.astype(BF16)
    lo = (x - hi.astype(F32)).astype(BF16)
    return jnp.dot(jnp.concatenate([hi, lo], axis=1), tri2, preferred_element_type=F32)


def _nt(a, b):
    return lax.dot_general(a, b, (((1,), (1,)), ((), ())), preferred_element_type=F32)


def _tn(a, b):
    return lax.dot_general(a, b, (((0,), (0,)), ((), ())), preferred_element_type=F32)


def _heads_per_step(nh):
    return max(h for h in (1, 2, 4) if nh % h == 0)


def _sba_fwd(p, sbw, name, comm=None):
    s = p.shape[0]
    nh = sbw // HEAD_DIM
    hp = _heads_per_step(nh)
    ngrp, hw = nh // hp, hp * HEAD_DIM
    blk = ATT_BLOCK
    nq = s // blk
    scale = 1.0 / math.sqrt(HEAD_DIM)
    host = _Host(comm,
                 [pl.BlockSpec((blk, hw), lambda g, i: (i, g)),
                  pl.BlockSpec((s, hw), lambda g, i: (0, ngrp + g)),
                  pl.BlockSpec((s, hw), lambda g, i: (0, 2 * ngrp + g))],
                 [pl.BlockSpec((blk, hw), lambda g, i: (i, g))] * 2,
                 [jax.ShapeDtypeStruct((s, sbw), F32)] * 2,
                 [pltpu.VMEM((s, hw), BF16)] * 2)

    def body(*refs):
        (q_ref, k_ref, v_ref), (o_ref, lt_ref), (kb_ref, vb_ref) = host.split(refs)
        i = pl.program_id(1)
        step = pl.program_id(0) * nq + i
        host.before(step, ngrp * nq)

        @pl.when(i == 0)
        def _():
            kb_ref[...] = k_ref[...].astype(BF16)
            vb_ref[...] = v_ref[...].astype(BF16)

        heads = [slice(h * HEAD_DIM, (h + 1) * HEAD_DIM) for h in range(hp)]
        qs = [q_ref[:, hd].astype(BF16) for hd in heads]
        later = _tri(blk, lambda r, c: r > c)
        causal = lax.broadcasted_iota(jnp.int32, (blk, blk), 1) < lax.broadcasted_iota(jnp.int32, (blk, blk), 0)

        def key_block(j, carry, diagonal):
            rows = pl.ds(pl.multiple_of(j * blk, blk), blk)
            hs = range(hp)
            z = [_nt(qs[h], kb_ref[rows, heads[h]]) * scale for h in hs]
            ls = [_log_sigmoid(z[h]) for h in hs]
            lm = [jnp.where(causal, ls[h] - z[h], 0.0) if diagonal else ls[h] - z[h] for h in hs]
            stay = [_dot_split(lm[h], later) for h in hs]
            w = [jnp.exp(ls[h] + stay[h] + carry[h][1]) for h in hs]
            if diagonal:
                w = [jnp.where(causal, w[h], 0.0) for h in hs]
            acc = [carry[h][0] + jnp.dot(w[h].astype(BF16), vb_ref[rows, heads[h]], preferred_element_type=F32) for h in hs]
            return tuple((acc[h], carry[h][1] + jnp.sum(lm[h], axis=1, keepdims=True)) for h in hs)

        init = tuple((jnp.zeros((blk, HEAD_DIM), F32), jnp.zeros((blk, 1), F32)) for _ in heads)
        carry = key_block(i, init, True)
        carry = lax.fori_loop(0, i, lambda n, c: key_block(i - 1 - n, c, False), carry)
        for h, hd in enumerate(heads):
            o_ref[:, hd] = carry[h][0]
            lt_ref[:, hd] = jnp.broadcast_to(carry[h][1], (blk, HEAD_DIM))
        host.after(step, ngrp * nq)

    outs = pl.pallas_call(
        body, name=name, grid=(ngrp, nq), in_specs=host.in_specs, out_specs=host.out_specs, out_shape=host.out_shape,
        scratch_shapes=host.scratch, input_output_aliases=host.aliases,
        compiler_params=_cp("arbitrary", "arbitrary"))(p, p, p, *host.args)
    (out, ltot), extra = host.results(outs)
    return out, ltot, extra


def _sba_bwd(p, ltot, dout, sbw, name, comm=None):
    s = p.shape[0]
    nh = sbw // HEAD_DIM
    hp = _heads_per_step(nh)
    ngrp, hw = nh // hp, hp * HEAD_DIM
    blk = ATT_BLOCK
    nq = s // blk
    scale = 1.0 / math.sqrt(HEAD_DIM)
    blk_spec = pl.BlockSpec((blk, hw), lambda g, i: (i, g))
    col_spec = pl.BlockSpec((s, hw), lambda g, i: (0, g))
    host = _Host(comm,
                 [blk_spec, pl.BlockSpec((s, hw), lambda g, i: (0, ngrp + g)),
                  pl.BlockSpec((s, hw), lambda g, i: (0, 2 * ngrp + g)), blk_spec, blk_spec],
                 [blk_spec, col_spec, col_spec], [jax.ShapeDtypeStruct((s, sbw), BF16)] * 3,
                 [pltpu.VMEM((s, hw), BF16)] * 2 + [pltpu.VMEM((s, hw), F32)] * 2)

    def body(*refs):
        (q_ref, k_ref, v_ref, lt_ref, do_ref), (dq_ref, dk_ref, dv_ref), (kb_ref, vb_ref, dka_ref, dva_ref) = host.split(refs)
        i = pl.program_id(1)
        step = pl.program_id(0) * nq + i
        host.before(step, ngrp * nq)

        @pl.when(i == 0)
        def _():
            kb_ref[...] = k_ref[...].astype(BF16)
            vb_ref[...] = v_ref[...].astype(BF16)
            dka_ref[...] = jnp.zeros_like(dka_ref)
            dva_ref[...] = jnp.zeros_like(dva_ref)

        heads = [slice(h * HEAD_DIM, (h + 1) * HEAD_DIM) for h in range(hp)]
        qs = [q_ref[:, hd].astype(BF16) for hd in heads]
        dos = [do_ref[:, hd].astype(BF16) for hd in heads]
        ltots = [lt_ref[:, h * HEAD_DIM:h * HEAD_DIM + 1] for h in range(hp)]
        upto = _tri(blk, lambda r, c: r <= c)
        before = _tri(blk, lambda r, c: r < c)
        causal = lax.broadcasted_iota(jnp.int32, (blk, blk), 1) < lax.broadcasted_iota(jnp.int32, (blk, blk), 0)

        def key_block(j, carry, diagonal):
            rows = pl.ds(pl.multiple_of(j * blk, blk), blk)
            hs = range(hp)
            kj = [kb_ref[rows, heads[h]] for h in hs]
            vj = [vb_ref[rows, heads[h]] for h in hs]
            z = [_nt(qs[h], kj[h]) * scale for h in hs]
            dw = [_nt(dos[h], vj[h]) for h in hs]
            ls = [_log_sigmoid(z[h]) for h in hs]
            lm = [jnp.where(causal, ls[h] - z[h], 0.0) if diagonal else ls[h] - z[h] for h in hs]
            stay = [ltots[h] - carry[h][1] - _dot_split(lm[h], upto) for h in hs]
            w = [jnp.exp(ls[h] + stay[h]) for h in hs]
            if diagonal:
                w = [jnp.where(causal, w[h], 0.0) for h in hs]
            da = [dw[h] * w[h] for h in hs]
            sig = [jnp.exp(ls[h]) for h in hs]
            chain = [sig[h] * (carry[h][2] + _dot_split(da[h], before)) for h in hs]
            if diagonal:
                chain = [jnp.where(causal, chain[h], 0.0) for h in hs]
            dzb = [((da[h] * (1.0 - sig[h]) - chain[h]) * scale).astype(BF16) for h in hs]
            dq = [carry[h][0] + jnp.dot(dzb[h], kj[h], preferred_element_type=F32) for h in hs]
            for h in hs:
                dka_ref[rows, heads[h]] += _tn(dzb[h], qs[h])
            for h in hs:
                dva_ref[rows, heads[h]] += _tn(w[h].astype(BF16), dos[h])
            return tuple((dq[h], carry[h][1] + jnp.sum(lm[h], axis=1, keepdims=True),
                          carry[h][2] + jnp.sum(da[h], axis=1, keepdims=True)) for h in hs)

        zero = jnp.zeros((blk, 1), F32)
        init = tuple((jnp.zeros((blk, HEAD_DIM), F32), zero, zero) for _ in heads)
        carry = lax.fori_loop(0, i, lambda j, c: key_block(j, c, False), init)
        carry = key_block(i, carry, True)
        for h, hd in enumerate(heads):
            dq_ref[:, hd] = carry[h][0].astype(BF16)

        @pl.when(i == nq - 1)
        def _():
            dk_ref[...] = dka_ref[...].astype(BF16)
            dv_ref[...] = dva_ref[...].astype(BF16)

        host.after(step, ngrp * nq)

    outs = pl.pallas_call(
        body, name=name, grid=(ngrp, nq), in_specs=host.in_specs, out_specs=host.out_specs, out_shape=host.out_shape,
        scratch_shapes=host.scratch, input_output_aliases=host.aliases,
        compiler_params=_cp("arbitrary", "arbitrary"))(p, p, p, ltot, dout, *host.args)
    (dq, dk, dv), extra = host.results(outs)
    return dq, dk, dv, extra


def _pool_groups(pad_ref, tile, row0, gd, halo):
    row = row0 + lax.broadcasted_iota(jnp.int32, (tile, 1), 0)
    out = []
    for gi, win in enumerate(POOL_WINDOWS):
        cs = slice(gi * gd, (gi + 1) * gd)
        tok = pad_ref[halo:halo + tile, cs]
        acc = tok
        for j in range(1, win):
            acc = acc + pad_ref[halo - j:halo - j + tile, cs]
        cnt = jnp.minimum(win, row + 1).astype(F32)
        out.append(acc / cnt - tok)
    return out


def _even_mix_fwd(p, att, pool_w, pool_scale, d, name):
    s = p.shape[0]
    half = d // 2
    gd = half // len(POOL_WINDOWS)
    t, hb = ROW_TILE, POOL_HALO

    def body(u_ref, uh_ref, g_ref, a_ref, pw_ref, sc_ref, y_ref, pad_ref):
        i = pl.program_id(0)
        pad_ref[0:hb, :] = jnp.where(i > 0, uh_ref[...], 0.0)
        pad_ref[hb:, :] = u_ref[...]
        pooled = _pool_groups(pad_ref, t, i * t, gd, hb)
        for gi in range(len(POOL_WINDOWS)):
            cs = slice(gi * gd, (gi + 1) * gd)
            po = jnp.dot(pooled[gi].astype(BF16), pw_ref[gi], preferred_element_type=F32) * sc_ref[:, cs]
            y_ref[:, half + gi * gd:half + (gi + 1) * gd] = (po * _silu(g_ref[:, half + gi * gd:half + (gi + 1) * gd])).astype(BF16)
        y_ref[:, :half] = (a_ref[...] * _silu(g_ref[:, :half])).astype(BF16)

    return pl.pallas_call(
        body, name=name, grid=(s // t,),
        in_specs=[pl.BlockSpec((t, half), lambda i: (i, 3)),
                  pl.BlockSpec((hb, half), lambda i: (jnp.maximum(i * (t // hb) - 1, 0), 3)),
                  pl.BlockSpec((t, d), lambda i: (i, 2)),
                  pl.BlockSpec((t, half), lambda i: (i, 0)),
                  pl.BlockSpec(pool_w.shape, lambda i: (0, 0, 0)),
                  pl.BlockSpec((1, half), lambda i: (0, 0))],
        out_specs=pl.BlockSpec((t, d), lambda i: (i, 0)),
        out_shape=jax.ShapeDtypeStruct((s, d), BF16),
        scratch_shapes=[pltpu.VMEM((hb + t, half), F32)],
        compiler_params=_cp("parallel"))(p, p, p, att, pool_w, pool_scale)


def _even_mix_bwd(p, att, dy, pool_w, pool_scale, d, name, comm=None):
    s = p.shape[0]
    half = d // 2
    ng = len(POOL_WINDOWS)
    gd = half // ng
    t, hb = ROW_TILE, POOL_HALO
    nt = s // t
    host = _Host(
        comm,
        [pl.BlockSpec((t, half), lambda i: (i, 3)),
         pl.BlockSpec((hb, half), lambda i: (jnp.maximum(i * (t // hb) - 1, 0), 3)),
         pl.BlockSpec((t, d), lambda i: (i, 2)),
         pl.BlockSpec((hb, half), lambda i: (jnp.minimum((i + 1) * (t // hb), s // hb - 1), 5)),
         pl.BlockSpec((t, half), lambda i: (i, 0)),
         pl.BlockSpec((t, d), lambda i: (i, 0)),
         pl.BlockSpec((hb, half), lambda i: (jnp.minimum((i + 1) * (t // hb), s // hb - 1), 1)),
         pl.BlockSpec(pool_w.shape, lambda i: (0, 0, 0)),
         pl.BlockSpec((1, half), lambda i: (0, 0))],
        [pl.BlockSpec((t, half), lambda i: (i, 0)),
         pl.BlockSpec((t, half), lambda i: (i, 0)),
         pl.BlockSpec((t, d), lambda i: (i, 0)),
         pl.BlockSpec((1, half), lambda i: (0, 0)),
         pl.BlockSpec((ng, gd, gd), lambda i: (0, 0, 0))],
        [jax.ShapeDtypeStruct((s, half), F32), jax.ShapeDtypeStruct((s, half), BF16),
         jax.ShapeDtypeStruct((s, d), BF16), jax.ShapeDtypeStruct((1, half), F32),
         jax.ShapeDtypeStruct((ng, gd, gd), F32)],
        [pltpu.VMEM((hb + t, half), F32), pltpu.VMEM((t + hb, half), F32)])

    def body(*refs):
        ((u_ref, uh_ref, g_ref, gh_ref, a_ref, dy_ref, dyh_ref, pw_ref, sc_ref),
         (da_ref, du_ref, dg_ref, dsc_ref, dpw_ref), (pad_ref, dn_ref)) = host.split(refs)
        i = pl.program_id(0)
        host.before(i, nt)
        first = i == 0
        pad_ref[0:hb, :] = jnp.where(i > 0, uh_ref[...], 0.0)
        pad_ref[hb:, :] = u_ref[...]
        pooled = _pool_groups(pad_ref, t, i * t, gd, hb)
        g1 = g_ref[:, :half]
        dy1 = dy_ref[:, :half]
        da_ref[...] = dy1 * _silu(g1)
        dg_ref[:, :half] = (dy1 * a_ref[...] * _dsilu(g1)).astype(BF16)
        row = i * t + lax.broadcasted_iota(jnp.int32, (t + hb, 1), 0)
        for gi, win in enumerate(POOL_WINDOWS):
            cs = slice(gi * gd, (gi + 1) * gd)
            cs2 = slice(half + gi * gd, half + (gi + 1) * gd)
            w = pw_ref[gi]
            pb = pooled[gi].astype(BF16)
            zp = jnp.dot(pb, w, preferred_element_type=F32)
            g2 = g_ref[:, cs2]
            dy2 = dy_ref[:, cs2]
            dg_ref[:, cs2] = (dy2 * zp * sc_ref[:, cs] * _dsilu(g2)).astype(BF16)
            dpo = dy2 * _silu(g2)
            _acc_rows(dsc_ref.at[:, cs], first, jnp.sum(dpo * zp, axis=0, keepdims=True))
            dz = (dpo * sc_ref[:, cs]).astype(BF16)
            _acc_rows(dpw_ref.at[gi], first, _tn(pb, dz))
            dzh = jnp.where(i < nt - 1, dyh_ref[:, cs] * _silu(gh_ref[:, cs]) * sc_ref[:, cs], 0.0).astype(BF16)
            dpool = _nt(dz, w)
            dpool_h = _nt(dzh, w)
            cnt = jnp.minimum(win, row + 1).astype(F32)
            dn_ref[0:t, cs] = dpool / cnt[0:t]
            dn_ref[t:, cs] = dpool_h / cnt[t:]
            acc = dn_ref[0:t, cs]
            for j in range(1, win):
                acc = acc + dn_ref[j:j + t, cs]
            du_ref[:, cs] = (acc - dpool).astype(BF16)
        host.after(i, nt)

    outs = pl.pallas_call(
        body, name=name, grid=(nt,), in_specs=host.in_specs, out_specs=host.out_specs, out_shape=host.out_shape,
        scratch_shapes=host.scratch, input_output_aliases=host.aliases,
        compiler_params=_cp("arbitrary"))(p, p, p, p, att, dy, dy, pool_w, pool_scale, *host.args)
    return host.results(outs)


def _mm_out_even(y, w, x, g_post, g_pre_next, name):
    s, k = y.shape
    d = w.shape[1]
    t = ROW_TILE

    def body(y_ref, w_ref, x_ref, gp_ref, gn_ref, o_ref, x1_ref, h1_ref):
        for r0 in range(0, t, t // 2):
            rows = slice(r0, r0 + t // 2)
            o = jnp.dot(y_ref[rows, :], w_ref[...], preferred_element_type=F32)
            o_ref[rows, :] = o
            ohat, _ = _rms_stats(o)
            x1 = x_ref[rows, :] + ohat * gp_ref[...]
            x1_ref[rows, :] = x1
            xhat, _ = _rms_stats(x1)
            h1_ref[rows, :] = (xhat * gn_ref[...]).astype(BF16)

    row = lambda c: pl.BlockSpec((t, c), lambda i: (i, 0))
    vec = pl.BlockSpec((1, d), lambda i: (0, 0))
    return pl.pallas_call(
        body, name=name, grid=(s // t,),
        in_specs=[row(k), pl.BlockSpec((k, d), lambda i: (0, 0)), row(d), vec, vec],
        out_specs=[row(d), row(d), row(d)],
        out_shape=[jax.ShapeDtypeStruct((s, d), F32), jax.ShapeDtypeStruct((s, d), F32),
                   jax.ShapeDtypeStruct((s, d), BF16)],
        compiler_params=_cp("parallel"))(y, w, x, g_post, g_pre_next)


def _mm_out_odd(y, w, x1, g_post, target, name):
    s, k = y.shape
    d = w.shape[1]
    t = ROW_TILE

    def body(y_ref, w_ref, x_ref, gp_ref, tg_ref, do_ref, dx_ref, loss_ref, dgp_ref):
        first = pl.program_id(0) == 0
        gp = gp_ref[...]
        part = dgp = None
        for r0 in range(0, t, t // 2):
            rows = slice(r0, r0 + t // 2)
            o = jnp.dot(y_ref[rows, :], w_ref[...], preferred_element_type=F32)
            ohat, r = _rms_stats(o)
            diff = x_ref[rows, :] + ohat * gp - tg_ref[rows, :]
            part_half = 0.5 * jnp.sum(jnp.mean(diff * diff, axis=-1, keepdims=True), axis=0, keepdims=True)
            dx2 = diff * (1.0 / d)
            dx_ref[rows, :] = dx2
            do, dgp_half = _rms_bwd(dx2, ohat, r, gp)
            do_ref[rows, :] = do.astype(BF16)
            part = part_half if part is None else part + part_half
            dgp = dgp_half if dgp is None else dgp + dgp_half
        _acc_rows(loss_ref, first, jnp.broadcast_to(part, loss_ref.shape))
        _acc_rows(dgp_ref, first, dgp)

    row = lambda c: pl.BlockSpec((t, c), lambda i: (i, 0))
    vec = pl.BlockSpec((1, d), lambda i: (0, 0))
    return pl.pallas_call(
        body, name=name, grid=(s // t,),
        in_specs=[row(k), pl.BlockSpec((k, d), lambda i: (0, 0)), row(d), vec, row(d)],
        out_specs=[row(d), row(d), pl.BlockSpec((8, LANES), lambda i: (0, 0)), vec],
        out_shape=[jax.ShapeDtypeStruct((s, d), BF16), jax.ShapeDtypeStruct((s, d), F32),
                   jax.ShapeDtypeStruct((8, LANES), F32), jax.ShapeDtypeStruct((1, d), F32)],
        compiler_params=_cp("arbitrary"))(y, w, x1, g_post, target)


def _layer_norm(d1, cg, cb):
    mu = jnp.mean(d1, axis=-1, keepdims=True)
    cen = d1 - mu
    rstd = lax.rsqrt(jnp.mean(cen * cen, axis=-1, keepdims=True) + EPS)
    n = cen * rstd
    return n, rstd, n * cg + cb


SUBLANES = 8
ROW_STRIP = 64
GATHER_PIECES = 8
CONV_ROWS = 64


def _make_shifts(pad_ref, cs, sh_ref):
    rows = sh_ref.shape[1]
    for r in range(1, SUBLANES):
        sh_ref[r - 1] = pad_ref[r:r + rows, cs]


def _by_shift(taps, base, sign=1):
    return sorted(range(taps), key=lambda k: ((sign * (base + k)) % SUBLANES, k))


def _window(pad_ref, cs, sh_ref, off, t):
    m, r = divmod(off, SUBLANES)
    if r == 0:
        return pad_ref[SUBLANES * m:SUBLANES * m + t, cs]
    return sh_ref[r - 1, SUBLANES * m:SUBLANES * m + t, :]


def _odd_mix_fwd(p, sconv_w, dconv_w, dconv_b, cnorm_g, cnorm_b, d, name):
    s = p.shape[0]
    w = d // 2
    k3, k31 = sconv_w.shape[0], dconv_w.shape[0]
    t, hb = ROW_TILE, CONV_HALO
    assert hb >= k31 - 1 and w % LANES == 0

    def body(p_ref, ph_ref, w3_ref, w31_ref, b31_ref, cg_ref, cb_ref, y_ref, s3_ref, d1_ref, mpad, dpad, sh_ref):
        i = pl.program_id(0)
        mpad[0:hb, :] = jnp.where(i > 0, ph_ref[:, 2 * w:3 * w] * ph_ref[:, 0:w], 0.0)
        mpad[hb:, :] = p_ref[:, 2 * w:3 * w] * p_ref[:, 0:w]
        dpad[0:hb, :] = jnp.where(i > 0, ph_ref[:, 3 * w:4 * w] * _sigmoid(ph_ref[:, 4 * w:5 * w]), 0.0)
        dpad[hb:, :] = p_ref[:, 3 * w:4 * w] * _sigmoid(p_ref[:, 4 * w:5 * w])
        for c0 in range(0, w, LANES):
            cs = slice(c0, c0 + LANES)
            acc = jnp.zeros((t, LANES), F32)
            for kk in range(k3):
                acc = acc + w3_ref[kk:kk + 1, cs] * mpad[hb - (k3 - 1) + kk:hb - (k3 - 1) + kk + t, cs]
            s3_ref[:, cs] = acc
            _make_shifts(dpad, cs, sh_ref)
            for r0 in range(0, t, CONV_ROWS):
                acc = jnp.zeros((CONV_ROWS, LANES), F32)
                for kk in _by_shift(k31, hb - (k31 - 1)):
                    acc = acc + w31_ref[kk:kk + 1, cs] * _window(dpad, cs, sh_ref, hb - (k31 - 1) + kk + r0, CONV_ROWS)
                d1_ref[r0:r0 + CONV_ROWS, cs] = acc + b31_ref[:, cs]
        _, _, d2 = _layer_norm(d1_ref[...], cg_ref[...], cb_ref[...])
        y_ref[:, :w] = (p_ref[:, w:2 * w] * s3_ref[...] * _silu(p_ref[:, 5 * w:6 * w])).astype(BF16)
        y_ref[:, w:] = (_silu(d2) * _silu(p_ref[:, 6 * w:7 * w])).astype(BF16)

    row = lambda c: pl.BlockSpec((t, c), lambda i: (i, 0))
    full = lambda a: pl.BlockSpec(a.shape, lambda i: (0, 0))
    return pl.pallas_call(
        body, name=name, grid=(s // t,),
        in_specs=[row(7 * w),
                  pl.BlockSpec((hb, 5 * w), lambda i: (jnp.maximum(i * (t // hb) - 1, 0), 0)),
                  full(sconv_w), full(dconv_w), full(dconv_b), full(cnorm_g), full(cnorm_b)],
        out_specs=[row(d), row(w), row(w)],
        out_shape=[jax.ShapeDtypeStruct((s, d), BF16), jax.ShapeDtypeStruct((s, w), F32),
                   jax.ShapeDtypeStruct((s, w), F32)],
        scratch_shapes=[pltpu.VMEM((hb + t, w), F32)] * 2 + [pltpu.VMEM((SUBLANES - 1, hb + t - SUBLANES, LANES), F32)],
        compiler_params=_cp("parallel"))(p, p, sconv_w, dconv_w, dconv_b, cnorm_g, cnorm_b)


def _odd_bwd_rows(p, s3, d1, dy, cnorm_g, cnorm_b, d, name, comm=None):
    s = p.shape[0]
    w = d // 2
    t = ROW_TILE
    col = lambda j: pl.BlockSpec((t, w), lambda i: (i, j))
    row = lambda c: pl.BlockSpec((t, c), lambda i: (i, 0))
    vec = pl.BlockSpec((1, w), lambda i: (0, 0))
    host = _Host(comm, [col(1), col(5), col(6), row(w), row(w), row(d), vec, vec],
                 [row(w), row(d), row(w), row(w), vec, vec, vec],
                 [jax.ShapeDtypeStruct((s, w), BF16), jax.ShapeDtypeStruct((s, d), BF16),
                  jax.ShapeDtypeStruct((s, w), F32), jax.ShapeDtypeStruct((s, w), F32)] + [jax.ShapeDtypeStruct((1, w), F32)] * 3, [])

    def body(*refs):
        ((bc_ref, g1_ref, g2_ref, s3_ref, d1_ref, dy_ref, cg_ref, cb_ref),
         (dbc_ref, dg_ref, ds3_ref, dd1_ref, dcg_ref, dcb_ref, db_ref), _) = host.split(refs)
        step = pl.program_id(0)
        host.before(step, s // t)
        first = step == 0

        def strip(j, sums):
            rows = slice(j * ROW_STRIP, (j + 1) * ROW_STRIP)
            g1, g2 = g1_ref[rows, :], g2_ref[rows, :]
            bc, s3v = bc_ref[rows, :], s3_ref[rows, :]
            dy1, dy2 = dy_ref[rows, :w], dy_ref[rows, w:]
            n, rstd, d2 = _layer_norm(d1_ref[rows, :], cg_ref[...], cb_ref[...])
            dg_ref[rows, :w] = (dy1 * bc * s3v * _dsilu(g1)).astype(BF16)
            dg_ref[rows, w:] = (dy2 * _silu(d2) * _dsilu(g2)).astype(BF16)
            dco = dy1 * _silu(g1)
            dbc_ref[rows, :] = (dco * s3v).astype(BF16)
            ds3_ref[rows, :] = dco * bc
            dd2 = dy2 * _silu(g2) * _dsilu(d2)
            dn = dd2 * cg_ref[...]
            dd1 = rstd * (dn - jnp.mean(dn, axis=-1, keepdims=True) - n * jnp.mean(dn * n, axis=-1, keepdims=True))
            dd1_ref[rows, :] = dd1
            dcb, dcg, db = sums
            return (dcb + jnp.sum(dd2, axis=0, keepdims=True), dcg + jnp.sum(dd2 * n, axis=0, keepdims=True),
                    db + jnp.sum(dd1, axis=0, keepdims=True))

        zero = jnp.zeros((1, w), F32)
        sums = (zero, zero, zero)
        for j in range(t // ROW_STRIP):
            sums = strip(j, sums)
        dcb, dcg, db = sums
        _acc_rows(dcb_ref, first, dcb)
        _acc_rows(dcg_ref, first, dcg)
        _acc_rows(db_ref, first, db)
        host.after(step, s // t)

    outs = pl.pallas_call(
        body, name=name, grid=(s // t,), in_specs=host.in_specs, out_specs=host.out_specs, out_shape=host.out_shape,
        scratch_shapes=host.scratch, input_output_aliases=host.aliases,
        compiler_params=_cp("arbitrary"))(p, p, p, s3, d1, dy, cnorm_g, cnorm_b, *host.args)
    return host.results(outs)


def _odd_bwd_conv(p, ds3, dd1, sconv_w, dconv_w, d, name):
    s = p.shape[0]
    w = d // 2
    k3, k31 = sconv_w.shape[0], dconv_w.shape[0]
    t, hb, ha = ROW_TILE, CONV_HALO, 8
    nt = s // t
    assert hb >= k31 - 1 and ha >= k3 - 1

    def body(hc_ref, cc_ref, ga_ref, gb_ref, hch_ref, cch_ref, gah_ref, gbh_ref, ds3_ref, ds3h_ref, dd1_ref, dd1h_ref,
             w3_ref, w31_ref, dhc_ref, dcc_ref, dga_ref, dgb_ref, dw3_ref, dw31_ref, mpad, dpad, s3pad, d1pad, sh_ref):
        i = pl.program_id(0)
        first = i == 0
        last = i == nt - 1
        mpad[0:hb, :] = jnp.where(i > 0, cch_ref[...] * hch_ref[...], 0.0)
        mpad[hb:, :] = cc_ref[...] * hc_ref[...]
        dpad[0:hb, :] = jnp.where(i > 0, gah_ref[...] * _sigmoid(gbh_ref[...]), 0.0)
        dpad[hb:, :] = ga_ref[...] * _sigmoid(gb_ref[...])
        s3pad[0:t, :] = ds3_ref[...]
        s3pad[t:, :] = jnp.where(last, 0.0, ds3h_ref[...])
        d1pad[0:t, :] = dd1_ref[...]
        d1pad[t:, :] = jnp.where(last, 0.0, dd1h_ref[...])

        @pl.when(first)
        def _():
            dw3_ref[...] = jnp.zeros_like(dw3_ref)
            dw31_ref[...] = jnp.zeros_like(dw31_ref)

        def fold(v):
            return jnp.sum(v.reshape(v.shape[0] // SUBLANES, SUBLANES, LANES), axis=0)

        groups = range(0, t, CONV_ROWS)
        for c0 in range(0, w, LANES):
            cs = slice(c0, c0 + LANES)
            ds3v = s3pad[0:t, cs]
            dm = jnp.zeros((t, LANES), F32)
            for kk in range(k3):
                dm = dm + w3_ref[kk:kk + 1, cs] * s3pad[k3 - 1 - kk:k3 - 1 - kk + t, cs]
                off = hb - (k3 - 1) + kk
                dw3_ref[SUBLANES * kk:SUBLANES * (kk + 1), cs] += fold(ds3v * mpad[off:off + t, cs])
            dcc_ref[:, cs] = (dm * hc_ref[:, cs]).astype(BF16)
            dhc_ref[:, cs] = (dm * cc_ref[:, cs]).astype(BF16)
            _make_shifts(d1pad, cs, sh_ref)
            for r0 in groups:
                rows = slice(r0, r0 + CONV_ROWS)
                dd0 = jnp.zeros((CONV_ROWS, LANES), F32)
                for kk in _by_shift(k31, -(k31 - 1), -1):
                    dd0 = dd0 + w31_ref[kk:kk + 1, cs] * _window(d1pad, cs, sh_ref, k31 - 1 - kk + r0, CONV_ROWS)
                sgb = _sigmoid(gb_ref[rows, cs])
                dga_ref[rows, cs] = (dd0 * sgb).astype(BF16)
                dgb_ref[rows, cs] = (dd0 * ga_ref[rows, cs] * sgb * (1.0 - sgb)).astype(BF16)
            _make_shifts(dpad, cs, sh_ref)
            for kk in _by_shift(k31, hb - (k31 - 1)):
                part = jnp.zeros((SUBLANES, LANES), F32)
                for r0 in groups:
                    part = part + fold(d1pad[r0:r0 + CONV_ROWS, cs]
                                       * _window(dpad, cs, sh_ref, hb - (k31 - 1) + kk + r0, CONV_ROWS))
                dw31_ref[SUBLANES * kk:SUBLANES * (kk + 1), cs] += part

    col = lambda j: pl.BlockSpec((t, w), lambda i: (i, j))
    pre = lambda j: pl.BlockSpec((hb, w), lambda i: (jnp.maximum(i * (t // hb) - 1, 0), j))
    row = pl.BlockSpec((t, w), lambda i: (i, 0))
    post = lambda h: pl.BlockSpec((h, w), lambda i: (jnp.minimum((i + 1) * (t // h), s // h - 1), 0))
    full = lambda a: pl.BlockSpec(a.shape, lambda i: (0, 0))
    dhc, dcc, dga, dgb, dw3, dw31 = pl.pallas_call(
        body, name=name, grid=(nt,),
        in_specs=[col(0), col(2), col(3), col(4), pre(0), pre(2), pre(3), pre(4),
                  row, post(ha), row, post(hb), full(sconv_w), full(dconv_w)],
        out_specs=[row, row, row, row, pl.BlockSpec((SUBLANES * k3, w), lambda i: (0, 0)),
                   pl.BlockSpec((SUBLANES * k31, w), lambda i: (0, 0))],
        out_shape=[jax.ShapeDtypeStruct((s, w), BF16)] * 4
        + [jax.ShapeDtypeStruct((SUBLANES * k3, w), F32), jax.ShapeDtypeStruct((SUBLANES * k31, w), F32)],
        scratch_shapes=[pltpu.VMEM((hb + t, w), F32)] * 2 + [pltpu.VMEM((t + ha, w), F32), pltpu.VMEM((t + hb, w), F32),
                                                             pltpu.VMEM((SUBLANES - 1, hb + t - SUBLANES, LANES), F32)],
        compiler_params=_cp("arbitrary"))(p, p, p, p, p, p, p, p, ds3, ds3, dd1, dd1, sconv_w, dconv_w)
    return dhc, dcc, dga, dgb, jnp.sum(dw3.reshape(k3, SUBLANES, w), axis=1), jnp.sum(dw31.reshape(k31, SUBLANES, w), axis=1)


def _mm_in_bwd(dp, w3, x, g_pre, dres, post, name, comm=None):
    s = dp.shape[0]
    nsh, d, ns = w3.shape
    t = 512 if s % 512 == 0 else ROW_TILE
    nt = s // t
    ks = 2 if (ns // 2) % LANES == 0 else 1
    nk, kw = nsh * ks, ns // ks
    chunk = 128
    nchunk = t // chunk
    row = pl.BlockSpec((t, d), lambda i, k: (i, 0))
    vec = pl.BlockSpec((1, d), lambda i, k: (0, 0))
    rowwise = [x, dres] + ([post[0]] if post is not None else [])
    in_specs = [pl.BlockSpec((t, kw), lambda i, k: (i, k)), pl.BlockSpec((None, d, kw), lambda i, k: (k // ks, 0, k % ks)), vec]
    out_specs = [row, vec]
    out_shape = [jax.ShapeDtypeStruct((s, d), F32), jax.ShapeDtypeStruct((1, d), F32)]
    args = [dp, w3, g_pre]
    if post is not None:
        in_specs += [vec]
        out_specs += [row, vec]
        out_shape += [jax.ShapeDtypeStruct((s, d), BF16), jax.ShapeDtypeStruct((1, d), F32)]
        args += [post[1]]
    n_blocked = len(in_specs)
    in_specs += [ANY] * len(rowwise)
    args += rowwise
    host = _Host(comm, in_specs, out_specs, out_shape,
                 [pltpu.VMEM((t, d), F32), pltpu.VMEM((len(rowwise), 2, chunk, d), F32), pltpu.SemaphoreType.DMA((len(rowwise), 2))])

    def body(*refs):
        ins, outs, (acc_ref, buf_ref, sem_ref) = host.split(refs)
        dp_ref, w_ref, g_ref = ins[:3]
        hbm = ins[n_blocked:]
        dx_ref, dg_ref = outs[:2]
        tile = pl.program_id(0)
        kk = pl.program_id(1)
        first = tile == 0
        step = tile * nk + kk
        host.before(step, nt * nk)
        part = _nt(dp_ref[...], w_ref[...])

        @pl.when(kk == 0)
        def _():
            acc_ref[...] = part

        @pl.when(kk > 0)
        def _():
            acc_ref[...] += part

        def fetch(ci, slot):
            return [pltpu.make_async_copy(src.at[pl.ds(tile * t + ci * chunk, chunk)], buf_ref.at[n, slot], sem_ref.at[n, slot])
                    for n, src in enumerate(hbm)]

        @pl.when(kk == nk - 1)
        def _():
            dg = dgp = None
            for cp in fetch(0, 0):
                cp.start()
            for ci in range(nchunk):
                slot = ci % 2
                if ci + 1 < nchunk:
                    for cp in fetch(ci + 1, 1 - slot):
                        cp.start()
                for cp in fetch(ci, slot):
                    cp.wait()
                rows = slice(ci * chunk, (ci + 1) * chunk)
                xhat, r = _rms_stats(buf_ref[0, slot])
                dxn, dg_part = _rms_bwd(acc_ref[rows, :], xhat, r, g_ref[...])
                dx = buf_ref[1, slot] + dxn
                dx_ref[rows, :] = dx
                dg = dg_part if dg is None else dg + dg_part
                if post is not None:
                    ohat, ro = _rms_stats(buf_ref[2, slot])
                    do, dgp_part = _rms_bwd(dx, ohat, ro, ins[3][...])
                    outs[2][rows, :] = do.astype(BF16)
                    dgp = dgp_part if dgp is None else dgp + dgp_part
            _acc_rows(dg_ref, first, dg)
            if post is not None:
                _acc_rows(outs[3], first, dgp)

        host.after(step, nt * nk)

    res = pl.pallas_call(
        body, name=name, grid=(nt, nk), in_specs=host.in_specs, out_specs=host.out_specs, out_shape=host.out_shape,
        scratch_shapes=host.scratch, input_output_aliases=host.aliases,
        compiler_params=_cp("arbitrary", "arbitrary"))(*args, *host.args)
    return host.results(res)


def _half_add(g, r1, c_arr, name):
    nsh, rows, ns = g.shape
    h = rows // 2
    tr = min(ROW_TILE, h)
    per = h // tr

    def body(c_ref, g_ref, r_ref, o_ref):
        o_ref[...] = (g_ref[...].astype(F32) + r_ref[...].astype(F32)).astype(BF16)

    spec = pl.BlockSpec((None, tr, ns), lambda s, r, c: (s, r, 0))
    return pl.pallas_call(
        body, name=name,
        grid_spec=pltpu.PrefetchScalarGridSpec(
            num_scalar_prefetch=1, grid=(nsh, per),
            in_specs=[pl.BlockSpec((None, tr, ns), lambda s, r, c: (s, c[0] * per + r, 0)), spec], out_specs=spec),
        out_shape=jax.ShapeDtypeStruct((nsh, h, ns), BF16), compiler_params=_cp("parallel", "parallel"))(c_arr, g, r1)


def _sum_chips(hh, r2, mc_arr, name):
    _, h, ns = hh.shape
    tr = min(ROW_TILE, h)
    per = h // tr

    def body(mc_ref, h_ref, a_ref, b_ref, c_ref, o_ref):
        o_ref[...] = ((h_ref[...].astype(F32) + a_ref[...].astype(F32)) + b_ref[...].astype(F32)) + c_ref[...].astype(F32)

    got = lambda k: pl.BlockSpec((None, tr, ns), lambda r, mc: (k, r, 0))
    return pl.pallas_call(
        body, name=name,
        grid_spec=pltpu.PrefetchScalarGridSpec(
            num_scalar_prefetch=1, grid=(per,),
            in_specs=[pl.BlockSpec((None, tr, ns), lambda r, mc: (mc[0], r, 0)), got(0), got(1), got(2)],
            out_specs=pl.BlockSpec((tr, ns), lambda r, mc: (mc[1] * per + r, 0))),
        out_shape=jax.ShapeDtypeStruct((2 * h, ns), F32), compiler_params=_cp("parallel"))(mc_arr, hh, r2, r2, r2)


def _add2(a, b, name):
    def body(a_ref, b_ref, o_ref):
        o_ref[...] = a_ref[...] + b_ref[...]

    return pl.pallas_call(body, name=name, out_shape=jax.ShapeDtypeStruct(a.shape, a.dtype), compiler_params=_cp())(a, b)


def _sum_chips_ordered(s2, r2, mc_arr, name):
    rows, w = s2.shape
    rh = rows // 2

    def body(mc_ref, s_ref, a_ref, b_ref, c_ref, o_ref):
        me = mc_ref[0]
        acc = None
        for j in range(N_CHIPS):
            rel = jnp.bitwise_xor(me, j)
            v = jnp.where(rel == 0, s_ref[...], jnp.where(rel == 2, a_ref[...], jnp.where(rel == 1, b_ref[...], c_ref[...])))
            acc = v if acc is None else acc + v
        o_ref[...] = acc

    got = lambda k: pl.BlockSpec((None, rh, w), lambda i, mc: (k, 0, 0))
    return pl.pallas_call(
        body, name=name,
        grid_spec=pltpu.PrefetchScalarGridSpec(
            num_scalar_prefetch=1, grid=(1,),
            in_specs=[pl.BlockSpec((rh, w), lambda i, mc: (mc[1], 0)), got(0), got(1), got(2)],
            out_specs=pl.BlockSpec((rh, w), lambda i, mc: (mc[1], 0))),
        out_shape=jax.ShapeDtypeStruct((rows, w), F32), compiler_params=_cp("arbitrary"))(mc_arr, s2, r2, r2, r2)


def _adamw(w, g, m, v, name, comm=None):
    r, c = w.shape
    tr = ROW_TILE if r % ROW_TILE == 0 else r
    c1 = 1.0 / (1.0 - ADAM_B1 ** ADAM_STEP)
    c2 = 1.0 / (1.0 - ADAM_B2 ** ADAM_STEP)
    spec = pl.BlockSpec((tr, c), lambda i: (i, 0))
    host = _Host(comm, [spec] * 4, [spec] * 4, [jax.ShapeDtypeStruct((r, c), F32)] * 4, [])

    def body(*refs):
        (w_ref, g_ref, m_ref, v_ref), (go_ref, d_ref, nm_ref, nv_ref), _ = host.split(refs)
        step = pl.program_id(0)
        host.before(step, r // tr)
        gv = g_ref[...]
        go_ref[...] = gv
        nm = ADAM_B1 * m_ref[...] + (1.0 - ADAM_B1) * gv
        nv = ADAM_B2 * v_ref[...] + (1.0 - ADAM_B2) * (gv * gv)
        nm_ref[...] = nm
        nv_ref[...] = nv
        d_ref[...] = -ADAM_LR * ((nm * c1) / (jnp.sqrt(nv * c2) + ADAM_EPS) + ADAM_WD * w_ref[...])
        host.after(step, r // tr)

    outs = pl.pallas_call(
        body, name=name, grid=(r // tr,), in_specs=host.in_specs, out_specs=host.out_specs, out_shape=host.out_shape,
        scratch_shapes=host.scratch, input_output_aliases=host.aliases,
        compiler_params=_cp("arbitrary"))(w, g, m, v, *host.args)
    return host.results(outs)


def _gather_weights(bigs, pool_w, pack_w, pack_d, name):
    nb = len(bigs)
    smalls = [pool_w, pack_w, pack_d]
    q, cw, cd = pool_w.shape[1], pack_w.shape[1], pack_d.shape[1]
    pieces = [_GatherPlan(bigs, (j, j + 1, GATHER_PIECES)) for j in range(GATHER_PIECES)]
    for j, piece in enumerate(pieces):
        piece.base = 9 + j * piece.nsems

    def body(*refs):
        srcs, dsts = refs[:nb + 3], refs[nb + 3:2 * (nb + 3)]
        ssem, rsem, lsem = refs[2 * (nb + 3):]
        x, y, c, me, chips, sib = _place()

        def small_dst(n, chip):
            if n == 0:
                return dsts[nb].at[:, pl.ds(chip * q, q), :]
            return dsts[nb + n].at[:, pl.ds(chip * (cw if n == 1 else cd), cw if n == 1 else cd)]

        local = [pltpu.make_async_copy(srcs[nb + n], small_dst(n, me), lsem.at[n]) for n in range(3)]
        for cp in local:
            cp.start()
        sends = []
        for n in range(3):
            for k, chip in enumerate(chips):
                cp = _rcopy(srcs[nb + n], small_dst(n, me), ssem.at[3 * n + k], rsem.at[3 * n + k], (*chip, c))
                cp.start()
                sends.append(cp)
        big = (srcs[:nb], dsts[:nb], ssem, rsem)
        for stage in ("start", "relay", "relay_far", "finish"):
            for piece in pieces:
                getattr(piece, stage)(*big)
        for n in range(3):
            for k, chip in enumerate(chips):
                ref = small_dst(n, 2 * chip[0] + chip[1])
                _rcopy(ref, ref, ssem.at[3 * n + k], rsem.at[3 * n + k], (*chip, c)).wait_recv()
        for cp in sends:
            cp.wait_send()
        for cp in local:
            cp.wait()

    nsem = 9 + sum(piece.nsems for piece in pieces)
    out_shape = [jax.ShapeDtypeStruct(b.shape, b.dtype) for b in bigs]
    out_shape += [jax.ShapeDtypeStruct((pool_w.shape[0], N_CHIPS * q, pool_w.shape[2]), pool_w.dtype),
                  jax.ShapeDtypeStruct((pack_w.shape[0], N_CHIPS * cw), pack_w.dtype),
                  jax.ShapeDtypeStruct((pack_d.shape[0], N_CHIPS * cd), pack_d.dtype)]
    return pl.pallas_call(
        body, name=name, in_specs=[ANY] * (nb + 3), out_specs=[ANY] * (nb + 3), out_shape=out_shape,
        input_output_aliases={a: a for a in range(nb)},
        scratch_shapes=[pltpu.SemaphoreType.DMA((nsem,)), pltpu.SemaphoreType.DMA((nsem,)), pltpu.SemaphoreType.DMA((3,))],
        compiler_params=pltpu.CompilerParams(has_side_effects=True))(*bigs, *smalls)


def _swap_with_sibling(grads, wholes, name):
    n, nw = len(grads), len(wholes)
    halves = [g.shape[1] // 2 for g in grads]

    def body(*refs):
        srcs, dsts = refs[:n + nw], refs[n + nw:2 * (n + nw)]
        ssem, rsem = refs[2 * (n + nw):]
        x, y, c, me, chips, sib = _place()
        cps = [_rcopy(srcs[a].at[:, pl.ds((1 - c) * halves[a], halves[a]), :], dsts[a], ssem.at[a], rsem.at[a], sib)
               for a in range(n)]
        cps += [_rcopy(srcs[a], dsts[a], ssem.at[a], rsem.at[a], sib) for a in range(n, n + nw)]
        for cp in cps:
            cp.start()
        for cp in cps:
            cp.wait_recv()
        for cp in cps:
            cp.wait_send()

    out_shape = [jax.ShapeDtypeStruct((g.shape[0], h, g.shape[2]), g.dtype) for g, h in zip(grads, halves)]
    out_shape += [jax.ShapeDtypeStruct(w.shape, w.dtype) for w in wholes]
    return pl.pallas_call(
        body, name=name, in_specs=[ANY] * (n + nw), out_specs=[ANY] * (n + nw), out_shape=out_shape,
        scratch_shapes=[pltpu.SemaphoreType.DMA((n + nw,)), pltpu.SemaphoreType.DMA((n + nw,))],
        compiler_params=pltpu.CompilerParams(has_side_effects=True))(*grads, *wholes)


def _scatter_to_chips(halves_in, small, name):
    n = len(halves_in)
    rh = small.shape[0] // 2

    def body(*refs):
        srcs, dsts = refs[:n + 1], refs[n + 1:2 * (n + 1)]
        ssem, rsem = refs[2 * (n + 1):]
        x, y, c, me, chips, sib = _place()
        cps = []
        for a in range(n + 1):
            for k, chip in enumerate(chips):
                src = srcs[a].at[2 * chip[0] + chip[1]] if a < n else srcs[a].at[pl.ds(c * rh, rh)]
                cps.append(_rcopy(src, dsts[a].at[k], ssem.at[3 * a + k], rsem.at[3 * a + k], (*chip, c)))
        for cp in cps:
            cp.start()
        for cp in cps:
            cp.wait_recv()
        for cp in cps:
            cp.wait_send()

    out_shape = [jax.ShapeDtypeStruct((3,) + h.shape[1:], h.dtype) for h in halves_in]
    out_shape.append(jax.ShapeDtypeStruct((3, rh, small.shape[1]), small.dtype))
    return pl.pallas_call(
        body, name=name, in_specs=[ANY] * (n + 1), out_specs=[ANY] * (n + 1), out_shape=out_shape,
        scratch_shapes=[pltpu.SemaphoreType.DMA((3 * (n + 1),)), pltpu.SemaphoreType.DMA((3 * (n + 1),))],
        compiler_params=pltpu.CompilerParams(has_side_effects=True))(*halves_in, small)


def _join_halves(parts, name):
    n = len(parts)

    def body(*refs):
        srcs, dsts = refs[:n], refs[n:2 * n]
        ssem, rsem = refs[2 * n:]
        x, y, c, me, chips, sib = _place()
        cps = []
        for a in range(n):
            h = srcs[a].shape[0] // 2
            cps.append(_rcopy(srcs[a].at[pl.ds(c * h, h)], dsts[a].at[pl.ds(c * h, h)], ssem.at[a], rsem.at[a], sib))
        for cp in cps:
            cp.start()
        for a in range(n):
            h = srcs[a].shape[0] // 2
            theirs = dsts[a].at[pl.ds((1 - c) * h, h)]
            _rcopy(theirs, theirs, ssem.at[a], rsem.at[a], sib).wait_recv()
        for cp in cps:
            cp.wait_send()

    out_shape = [jax.ShapeDtypeStruct(p.shape, p.dtype) for p in parts]
    return pl.pallas_call(
        body, name=name, in_specs=[ANY] * n, out_specs=[ANY] * n, out_shape=out_shape,
        input_output_aliases={a: a for a in range(n)},
        scratch_shapes=[pltpu.SemaphoreType.DMA((n,)), pltpu.SemaphoreType.DMA((n,))],
        compiler_params=pltpu.CompilerParams(has_side_effects=True))(*parts)


def _scatter_start(h, name):
    land = (3,) + h.shape[1:]

    def body(h_ref, land_ref, send_sems, recv_sems, h_thru, land_thru, token):
        x, y, c, me, chips, sib = _place()
        for k, chip in enumerate(chips):
            _rcopy(h_ref.at[2 * chip[0] + chip[1]], land_ref.at[k], send_sems.at[k], recv_sems.at[k], (*chip, c)).start()
        token[...] = jnp.zeros_like(token)

    hbm = pl.BlockSpec(memory_space=pltpu.HBM)
    sem = pl.BlockSpec(memory_space=pltpu.SEMAPHORE)
    return pl.pallas_call(
        body, name=name,
        out_shape=(pltpu.SemaphoreType.DMA((3,)), pltpu.SemaphoreType.DMA((3,)), pltpu.HBM(h.shape, h.dtype),
                   pltpu.HBM(land, h.dtype), jax.ShapeDtypeStruct((8, LANES), F32)),
        in_specs=(hbm, hbm), out_specs=(sem, sem, hbm, hbm, pl.BlockSpec(memory_space=pltpu.VMEM)),
        input_output_aliases={0: 2, 1: 3},
        compiler_params=pltpu.CompilerParams(has_side_effects=pltpu.SideEffectType.DATAFLOW_SIDE_EFFECTING))(
            pltpu.with_memory_space_constraint(h, pltpu.HBM),
            pltpu.with_memory_space_constraint(lax.empty(land, h.dtype), pltpu.HBM))


def _scatter_wait(send_sems, recv_sems, h_thru, land_thru, after, name):
    def body(h_ref, land_ref, send_sems, recv_sems, after_ref, h_dead, got_ref):
        x, y, c, me, chips, sib = _place()
        for k, chip in enumerate(chips):
            cp = _rcopy(h_ref.at[2 * chip[0] + chip[1]], land_ref.at[k], send_sems.at[k], recv_sems.at[k], (*chip, c))
            cp.wait_send()
            cp.wait_recv()

    hbm = pl.BlockSpec(memory_space=pltpu.HBM)
    sem = pl.BlockSpec(memory_space=pltpu.SEMAPHORE)
    return pl.pallas_call(
        body, name=name,
        out_shape=(pltpu.HBM(h_thru.shape, h_thru.dtype), pltpu.HBM(land_thru.shape, land_thru.dtype)),
        in_specs=(hbm, hbm, sem, sem, ANY), out_specs=(hbm, hbm), input_output_aliases={0: 0, 1: 1},
        compiler_params=pltpu.CompilerParams(has_side_effects=pltpu.SideEffectType.DATAFLOW_SIDE_EFFECTING))(
            h_thru, land_thru, send_sems, recv_sems, after)


def _join_start(parts, name):
    n = len(parts)

    def body(*refs):
        srcs, (send_sems, recv_sems), token = refs[:n], refs[n:n + 2], refs[-1]
        x, y, c, me, chips, sib = _place()
        for a, src in enumerate(srcs):
            h = src.shape[0] // 2
            mine = src.at[pl.ds(c * h, h)]
            _rcopy(mine, mine, send_sems.at[a], recv_sems.at[a], sib).start()
        token[...] = jnp.zeros_like(token)

    hbm = pl.BlockSpec(memory_space=pltpu.HBM)
    sem = pl.BlockSpec(memory_space=pltpu.SEMAPHORE)
    outs = pl.pallas_call(
        body, name=name,
        out_shape=(pltpu.SemaphoreType.DMA((n,)), pltpu.SemaphoreType.DMA((n,)))
        + tuple(pltpu.HBM(p.shape, p.dtype) for p in parts) + (jax.ShapeDtypeStruct((8, LANES), F32),),
        in_specs=(hbm,) * n, out_specs=(sem, sem) + (hbm,) * n + (pl.BlockSpec(memory_space=pltpu.VMEM),),
        input_output_aliases={a: 2 + a for a in range(n)},
        compiler_params=pltpu.CompilerParams(has_side_effects=pltpu.SideEffectType.DATAFLOW_SIDE_EFFECTING))(
            *[pltpu.with_memory_space_constraint(p, pltpu.HBM) for p in parts])
    return outs[0], outs[1], list(outs[2:2 + n]), outs[-1]


def _join_wait(send_sems, recv_sems, parts, after, name):
    n = len(parts)

    def body(*refs):
        srcs, (send_sems, recv_sems) = refs[:n], refs[n:n + 2]
        x, y, c, me, chips, sib = _place()
        for a, src in enumerate(srcs):
            h = src.shape[0] // 2
            mine, theirs = src.at[pl.ds(c * h, h)], src.at[pl.ds((1 - c) * h, h)]
            _rcopy(mine, theirs, send_sems.at[a], recv_sems.at[a], sib).wait_send()
            _rcopy(theirs, theirs, send_sems.at[a], recv_sems.at[a], sib).wait_recv()

    hbm = pl.BlockSpec(memory_space=pltpu.HBM)
    sem = pl.BlockSpec(memory_space=pltpu.SEMAPHORE)
    return pl.pallas_call(
        body, name=name, out_shape=tuple(pltpu.HBM(p.shape, p.dtype) for p in parts),
        in_specs=(hbm,) * n + (sem, sem, ANY), out_specs=(hbm,) * n, input_output_aliases={a: a for a in range(n)},
        compiler_params=pltpu.CompilerParams(has_side_effects=pltpu.SideEffectType.DATAFLOW_SIDE_EFFECTING))(
            *parts, send_sems, recv_sems, after)


def _pad_rows(a, rows):
    return jnp.pad(a, ((0, rows - a.shape[0]), (0, 0)))


def _stack_rows(parts, multiple):
    padded = [_pad_rows(p, -(-p.shape[0] // 8) * 8) for p in parts]
    starts, at = [], 0
    for p in padded:
        starts.append(at)
        at += p.shape[0]
    total = -(-at // multiple) * multiple
    if total > at:
        padded.append(jnp.zeros((total - at, parts[0].shape[1]), parts[0].dtype))
    return jnp.concatenate(padded, axis=0), starts


def kernel(x, ln_pre_even, w_in_even, pool_w, pool_scale, w_out_even, ln_post_even, ln_pre_odd, w_in_odd, sconv_w, dconv_w, dconv_b, cnorm_g, cnorm_b, w_out_odd, ln_post_odd, loss_target, m_ln_pre_even, m_w_in_even, m_pool_w, m_pool_scale, m_w_out_even, m_ln_post_even, m_ln_pre_odd, m_w_in_odd, m_sconv_w, m_dconv_w, m_dconv_b, m_cnorm_g, m_cnorm_b, m_w_out_odd, m_ln_post_odd, v_ln_pre_even, v_w_in_even, v_pool_w, v_pool_scale, v_w_out_even, v_ln_post_even, v_ln_pre_odd, v_w_in_odd, v_sconv_w, v_dconv_w, v_dconv_b, v_cnorm_g, v_cnorm_b, v_w_out_odd, v_ln_post_odd):
    _, s, d = x.shape
    half = d // 2
    cw = half // N_CHIPS
    ng, q, gd = pool_w.shape[1:]
    k3, k31 = sconv_w.shape[1], dconv_w.shape[1]
    x2d, tgt = x[0], loss_target[0]
    me = 2 * lax.axis_index("x") + lax.axis_index("y")
    core = lax.axis_index("c")
    c_arr = jnp.reshape(core, (1,)).astype(jnp.int32)
    me_arr = jnp.reshape(me, (1,)).astype(jnp.int32)
    mc_arr = jnp.stack([me, core]).astype(jnp.int32)

    shards = [w_in_even[0], w_out_even[0], w_in_odd[0], w_out_odd[0]]
    slabs = [_cast_bf16_own_slab(w, me_arr, f"cast_w{n}") for n, w in enumerate(shards)]
    pool_w_b = _cast_bf16(pool_w[0].reshape(ng * q, gd), "cast_pool_w").reshape(ng, q, gd)
    pack_w, at_w = _stack_rows([sconv_w[0], dconv_w[0], dconv_b, cnorm_g, cnorm_b], 8)
    pack_d, at_d = _stack_rows([ln_pre_odd, ln_post_odd], 8)
    win_e, pool_w_f, pack_w_f, pack_d_f = _gather_weights(slabs[:1], pool_w_b, pack_w, pack_d, "gather_first")
    sconv_f = pack_w_f[at_w[0]:at_w[0] + k3]
    dconv_f = pack_w_f[at_w[1]:at_w[1] + k31]
    dconv_b_f, cnorm_g_f, cnorm_b_f = (pack_w_f[at_w[n]:at_w[n] + 1] for n in (2, 3, 4))
    ln_pre_odd_f = pack_d_f[at_d[0]:at_d[0] + 1]
    ln_post_odd_f = pack_d_f[at_d[1]:at_d[1] + 1]

    def reduce_half(g, name):
        (got,) = _swap_with_sibling([g], [], "swap_" + name)
        return _half_add(g, got, c_arr, "half_add_" + name)

    h0 = _rms_fwd(x2d, ln_pre_even, "rms_pre_even")
    plans = _Multi([_GatherPlan([slabs[1]], at=(0.6, 0.88)), _GatherPlan([slabs[2]], (0, 1, 4), at=(0.6, 0.88))])
    p_e, extra = _mm_nn(h0, win_e, "proj_in_even", plans)
    (wout_e,), (win_o,) = plans.results(extra)
    wout_e = wout_e.reshape(d, d)
    att, ltot, (win_o,) = _sba_fwd(p_e, half, "sba_fwd", _GatherPlan([win_o], (1, 4, 4), at=(0.69, 0.94)))
    y_e = _even_mix_fwd(p_e, att, pool_w_f, pool_scale, d, "even_mix_fwd")
    o_e, x1, h1 = _mm_out_even(y_e, wout_e, x2d, ln_post_even, ln_pre_odd_f, "proj_out_even")
    p_o, (wout_o,) = _mm_nn(h1, win_o, "proj_in_odd", _GatherPlan([slabs[3]]))
    wout_o = wout_o.reshape(d, d)
    y_o, s3, d1 = _odd_mix_fwd(p_o, sconv_f, dconv_f, dconv_b_f, cnorm_g_f, cnorm_b_f, d, "odd_mix_fwd")
    do_o, dx2, loss_blk, dln_post_odd = _mm_out_odd(y_o, wout_o, x1, ln_post_odd_f, tgt, "proj_out_odd_loss")

    dy_o = _mm_nt(do_o, wout_o, "dy_odd")
    g_wout_o = _mm_tn(y_o, do_o, 1, "dw_out_odd")[0].reshape(N_CHIPS, d // N_CHIPS, d)
    (dbc, dgate_o, ds3, dd1, dcnorm_g, dcnorm_b, ddconv_b), (got,) = _odd_bwd_rows(
        p_o, s3, d1, dy_o, cnorm_g_f, cnorm_b_f, d, "odd_bwd_rows", _SwapPlan([g_wout_o]))
    h_wout_o = _half_add(g_wout_o, got, c_arr, "half_add_out_odd")
    dhc, dcc, dga, dgb, dsconv, ddconv = _odd_bwd_conv(p_o, ds3, dd1, sconv_f, dconv_f, d, "odd_bwd_conv")
    dp_o = jnp.concatenate([dhc, dbc, dcc, dga, dgb, dgate_o], axis=1)
    g_win_o, (s_wout_o,) = _mm_tn(h1, dp_o, N_CHIPS, "dw_in_odd", _ScatterPlan([h_wout_o]))
    (dx1, dln_pre_odd, do_e, dln_post_even), (got,) = _mm_in_bwd(
        dp_o, win_o, x1, ln_pre_odd_f, dx2, (o_e, ln_post_even), "dx_odd", _SwapPlan([g_win_o]))
    h_win_o = _half_add(g_win_o, got, c_arr, "half_add_in_odd")

    dy_e = _mm_nt(do_e, wout_e, "dy_even")
    g_wout_e = _mm_tn(y_e, do_e, 1, "dw_out_even")[0].reshape(N_CHIPS, d // N_CHIPS, d)
    (datt, du, dgate_e, dpool_scale, dpool_w), (got,) = _even_mix_bwd(
        p_e, att, dy_e, pool_w_f, pool_scale, d, "even_mix_bwd", _SwapPlan([g_wout_e]))
    h_wout_e = _half_add(g_wout_e, got, c_arr, "half_add_out_even")
    two = lambda v: v.reshape(2, half)
    small_parts = [dpool_scale, two(dln_post_even), two(dln_pre_odd), two(dln_post_odd),
                   dsconv, ddconv, ddconv_b, dcnorm_g, dcnorm_b, dpool_w.reshape(gd, half)]
    small, at_s = _stack_rows(small_parts, 16)
    (small1,) = _swap_with_sibling([], [small], "swap_small")
    small2 = _add2(small, small1, "small_add")
    plans = _Multi([_ScatterPlan([h_win_o]), _ShareHalfPlan([small2])])
    dq, dk, dv, extra = _sba_bwd(p_e, ltot, datt, half, "sba_bwd", plans)
    (s_win_o,), (small_got,) = plans.results(extra)
    dp_e = jnp.concatenate([dq, dk, dv, du, dgate_e], axis=1)
    g_win_e, (s_wout_e,) = _mm_tn(h0, dp_e, N_CHIPS, "dw_in_even", _ScatterPlan([h_wout_e]))
    h_win_e = reduce_half(g_win_e, "in_even")
    send_sems, recv_sems, h_win_e, landing, token = _scatter_start(h_win_e, "scatter_in_even_start")
    (grad_x, dln_pre_even), _ = _mm_in_bwd(dp_e, win_e, x2d, ln_pre_even + token[0:1, 0:1], dx1, None, "dx_even")

    last, at_l = _stack_rows([two(dln_pre_even), jnp.pad(loss_blk[0:1], ((0, 0), (0, half - LANES)))], 16)
    (last1,) = _swap_with_sibling([], [last], "swap_last")
    last2 = _add2(last, last1, "last_add")
    (last_got,) = _scatter_to_chips([], last2, "scatter_last")
    pairs = [(h_wout_e, s_wout_e), (h_win_o, s_win_o), (h_wout_o, s_wout_o)]
    parts = [_sum_chips(h, r, mc_arr, f"sum_chips{n + 1}") for n, (h, r) in enumerate(pairs)]
    parts.append(_sum_chips_ordered(small2, small_got, mc_arr, "small_sum"))
    parts.append(_sum_chips_ordered(last2, last_got, mc_arr, "last_sum"))
    join_sems = _join_start(parts, "join_first_start")
    h_win_e, s_win_e = _scatter_wait(send_sems, recv_sems, h_win_e, landing, join_sems[3], "scatter_in_even_wait")
    last_part = _sum_chips(h_win_e, s_win_e, mc_arr, "sum_chips0")
    last_sems = _join_start([last_part], "join_last_start")
    gw_out_e, gw_in_o, gw_out_o, red, red_last = _join_wait(*join_sems[:3], last_sems[3], "join_first_wait")
    loss = red_last[at_l[1], 0]

    def rows(n, cnt):
        return red[at_s[n]:at_s[n] + cnt]

    def mine(a, width):
        return lax.dynamic_slice_in_dim(a, me * width, width, axis=1)

    quarter = d // N_CHIPS
    g_small = {
        "ln_pre_even": red_last[at_l[0]:at_l[0] + 2].reshape(1, d),
        "pool_scale": rows(0, 1),
        "ln_post_even": rows(1, 2).reshape(1, d),
        "ln_pre_odd": mine(rows(2, 2).reshape(1, d), quarter),
        "ln_post_odd": mine(rows(3, 2).reshape(1, d), quarter),
        "sconv_w": mine(rows(4, k3), cw),
        "dconv_w": mine(rows(5, k31), cw),
        "dconv_b": mine(rows(6, 1), cw),
        "cnorm_g": mine(rows(7, 1), cw),
        "cnorm_b": mine(rows(8, 1), cw),
        "pool_w": lax.dynamic_slice_in_dim(rows(9, gd).reshape(ng, gd, gd), me * q, q, axis=1).reshape(ng * q, gd),
    }
    w2d = {
        "ln_pre_even": ln_pre_even, "w_in_even": w_in_even[0], "pool_w": pool_w[0].reshape(ng * q, gd),
        "pool_scale": pool_scale, "w_out_even": w_out_even[0], "ln_post_even": ln_post_even, "ln_pre_odd": ln_pre_odd,
        "w_in_odd": w_in_odd[0], "sconv_w": sconv_w[0], "dconv_w": dconv_w[0], "dconv_b": dconv_b, "cnorm_g": cnorm_g,
        "cnorm_b": cnorm_b, "w_out_odd": w_out_odd[0], "ln_post_odd": ln_post_odd,
    }
    moments = {
        "ln_pre_even": (m_ln_pre_even, v_ln_pre_even), "w_in_even": (m_w_in_even, v_w_in_even),
        "pool_w": (m_pool_w, v_pool_w), "pool_scale": (m_pool_scale, v_pool_scale),
        "w_out_even": (m_w_out_even, v_w_out_even), "ln_post_even": (m_ln_post_even, v_ln_post_even),
        "ln_pre_odd": (m_ln_pre_odd, v_ln_pre_odd), "w_in_odd": (m_w_in_odd, v_w_in_odd),
        "sconv_w": (m_sconv_w, v_sconv_w), "dconv_w": (m_dconv_w, v_dconv_w), "dconv_b": (m_dconv_b, v_dconv_b),
        "cnorm_g": (m_cnorm_g, v_cnorm_g), "cnorm_b": (m_cnorm_b, v_cnorm_b),
        "w_out_odd": (m_w_out_odd, v_w_out_odd), "ln_post_odd": (m_ln_post_odd, v_ln_post_odd),
    }
    def update(name, g):
        m_in, v_in = moments[name]
        w = w2d[name]
        return _adamw(w, g, m_in.reshape(w.shape), v_in.reshape(w.shape), "adamw_" + name)[0]

    updates = {"w_in_odd": update("w_in_odd", gw_in_o)}
    (gw_in_e,) = _join_wait(*last_sems[:3], updates["w_in_odd"][1], "join_last_wait")
    for name, g in dict(g_small, w_in_even=gw_in_e, w_out_even=gw_out_e, w_out_odd=gw_out_o).items():
        updates[name] = update(name, g)
    outs = [[u.reshape(moments[name][0].shape) for u in updates[name]] for name in w2d]
    grads_out, deltas, new_m, new_v = zip(*outs)
    return (loss, grad_x.reshape(x.shape), *grads_out, *deltas, *new_m, *new_v)
```

```python
import functools
import math

import jax
import jax.numpy as jnp
from jax import lax
from jax.experimental import pallas as pl
from jax.experimental.pallas import tpu as pltpu

F32 = jnp.float32
BF16 = jnp.bfloat16
EPS = 1e-6
N_CHIPS = 4
VMEM_LIMIT_V7X = 56 << 20
HEAD_DIM = 128
ATT_BLOCK = 256
POOL_WINDOWS = (2, 4, 8, 16)
ROW_TILE = 256
POOL_HALO = 16
CONV_HALO = 32
LANES = 128
ADAM_LR, ADAM_B1, ADAM_B2, ADAM_EPS, ADAM_WD, ADAM_STEP = 0.001, 0.9, 0.999, 1e-08, 0.01, 10
MESH_ID = pl.DeviceIdType.MESH
ANY = pl.BlockSpec(memory_space=pl.ANY)


def _cp(*sem):
    return pltpu.CompilerParams(dimension_semantics=sem or None, vmem_limit_bytes=VMEM_LIMIT_V7X)


def _pick_tile(n, cap):
    best = None
    for t in range(LANES, min(n, cap) + 1, LANES):
        if n % t == 0:
            best = t
    assert best is not None, (n, cap)
    return best


def _sigmoid(x):
    return 1.0 / (1.0 + jnp.exp(-x))


def _silu(x):
    return x * _sigmoid(x)


def _dsilu(x):
    s = _sigmoid(x)
    return s * (1.0 + x * (1.0 - s))


def _log_sigmoid(z):
    return jnp.minimum(z, 0.0) - jnp.log(1.0 + jnp.exp(-jnp.abs(z)))


def _rms_stats(x):
    r = lax.rsqrt(jnp.mean(x * x, axis=-1, keepdims=True) + EPS)
    return x * r, r


def _rms_bwd(dh, xhat, r, g):
    dxh = dh * g
    dx = r * (dxh - xhat * jnp.mean(dxh * xhat, axis=-1, keepdims=True))
    return dx, jnp.sum(dh * xhat, axis=0, keepdims=True)


def _acc_rows(ref, first, val):
    @pl.when(first)
    def _():
        ref[...] = val

    @pl.when(jnp.logical_not(first))
    def _():
        ref[...] += val


def _rcopy(src, dst, ssem, rsem, dev):
    return pltpu.make_async_remote_copy(src_ref=src, dst_ref=dst, send_sem=ssem, recv_sem=rsem,
                                        device_id=dev, device_id_type=MESH_ID)


def _place():
    x, y, c = lax.axis_index("x"), lax.axis_index("y"), lax.axis_index("c")
    chips = [(1 - x, y), (x, 1 - y), (1 - x, 1 - y)]
    return x, y, c, 2 * x + y, chips, (x, y, 1 - c)


class _GatherPlan:
    PER_ARRAY = 7

    def __init__(self, arrays, part=(0, 1, 1), at=(0.5, 0.8)):
        self.operands = list(arrays)
        self.out_shapes = [jax.ShapeDtypeStruct(a.shape, a.dtype) for a in arrays]
        self.aliases = {i: i for i in range(len(arrays))}
        self.nsems = self.PER_ARRAY * len(arrays)
        self.base = 0
        self.halves = [a.shape[1] // 2 for a in arrays]
        self.part = part
        self.at = at

    def schedule(self):
        return [(0.0, self.start), (self.at[0], self.relay), (self.at[1], self.relay_far)]

    def _rows(self, ref, a, chip, half, quarter=None):
        lo, hi, n = self.part
        h = self.halves[a]
        first, size = half * h + lo * h // n, (hi - lo) * h // n
        if quarter is not None:
            first, size = first + quarter * (size // 2), size // 2
        return ref.at[chip, pl.ds(first, size)]

    def _copy(self, src, dst, a, n, ssem, rsem, dev):
        return _rcopy(src, dst, ssem.at[self.base + self.PER_ARRAY * a + n], rsem.at[self.base + self.PER_ARRAY * a + n], dev)

    def _own(self, ins, outs, ssem, rsem):
        x, y, c, me, chips, sib = _place()
        return [self._copy(self._rows(ins[a], a, me, c), self._rows(outs[a], a, me, c), a, k, ssem, rsem, (*chips[k], c))
                for a in range(len(ins)) for k in (0, 1)]

    def _relays(self, outs, ssem, rsem, a, k):
        x, y, c, me, chips, sib = _place()
        chip = 2 * chips[k][0] + chips[k][1]
        whole, quarter = self._rows(outs[a], a, chip, c), self._rows(outs[a], a, chip, c, k)
        return (self._copy(whole, whole, a, k, ssem, rsem, (*chips[k], c)),
                self._copy(quarter, quarter, a, 2 + k, ssem, rsem, (*chips[1 - k], c)),
                self._copy(whole, whole, a, 4 + k, ssem, rsem, sib))

    def _far(self, outs, ssem, rsem, a):
        x, y, c, me, chips, sib = _place()
        chip = 2 * chips[2][0] + chips[2][1]
        whole = self._rows(outs[a], a, chip, c)
        got = [self._copy(q, q, a, 2 + k, ssem, rsem, (*chips[1 - k], c))
               for k, q in enumerate([self._rows(outs[a], a, chip, c, 0), self._rows(outs[a], a, chip, c, 1)])]
        return got, self._copy(whole, whole, a, 6, ssem, rsem, sib)

    def start(self, ins, outs, ssem, rsem):
        for cp in self._own(ins, outs, ssem, rsem):
            cp.start()

    def relay(self, ins, outs, ssem, rsem):
        for a in range(len(outs)):
            for k in (0, 1):
                landed, onward, to_sibling = self._relays(outs, ssem, rsem, a, k)
                landed.wait_recv()
                onward.start()
                to_sibling.start()

    def relay_far(self, ins, outs, ssem, rsem):
        for a in range(len(outs)):
            got, to_sibling = self._far(outs, ssem, rsem, a)
            for cp in got:
                cp.wait_recv()
            to_sibling.start()

    def finish(self, ins, outs, ssem, rsem):
        x, y, c, me, chips, sib = _place()
        for a in range(len(outs)):
            for k in range(3):
                ref = self._rows(outs[a], a, 2 * chips[k][0] + chips[k][1], 1 - c)
                self._copy(ref, ref, a, 4 + k, ssem, rsem, sib).wait_recv()
        for cp in self._own(ins, outs, ssem, rsem):
            cp.wait_send()
        for a in range(len(outs)):
            for k in (0, 1):
                _, onward, to_sibling = self._relays(outs, ssem, rsem, a, k)
                onward.wait_send()
                to_sibling.wait_send()
            self._far(outs, ssem, rsem, a)[1].wait_send()


class _ScatterPlan:
    def __init__(self, arrays, part=(0, 1, 1), into=None):
        self.n = len(arrays)
        self.operands = list(arrays) + list(into or [])
        self.out_shapes = [jax.ShapeDtypeStruct((3,) + a.shape[1:], a.dtype) for a in arrays]
        self.aliases = {self.n + i: i for i in range(self.n)} if into else {}
        self.nsems = 3 * self.n
        self.base = 0
        self.part = part

    def _copies(self, ins, outs, ssem, rsem):
        x, y, c, me, chips, sib = _place()
        lo, hi, n = self.part
        out = []
        for a in range(self.n):
            h = ins[a].shape[1]
            rows = pl.ds(lo * h // n, (hi - lo) * h // n)
            for k, chip in enumerate(chips):
                out.append(_rcopy(ins[a].at[2 * chip[0] + chip[1], rows], outs[a].at[k, rows],
                                  ssem.at[self.base + 3 * a + k], rsem.at[self.base + 3 * a + k], (*chip, c)))
        return out

    def schedule(self):
        return [(0.0, self.start)]

    def start(self, ins, outs, ssem, rsem):
        for cp in self._copies(ins, outs, ssem, rsem):
            cp.start()

    def finish(self, ins, outs, ssem, rsem):
        cps = self._copies(ins, outs, ssem, rsem)
        for cp in cps:
            cp.wait_recv()
        for cp in cps:
            cp.wait_send()


class _ShareHalfPlan(_ScatterPlan):
    def __init__(self, arrays):
        super().__init__(arrays)
        self.out_shapes = [jax.ShapeDtypeStruct((3, a.shape[0] // 2, a.shape[1]), a.dtype) for a in arrays]

    def _copies(self, ins, outs, ssem, rsem):
        x, y, c, me, chips, sib = _place()
        out = []
        for a in range(self.n):
            rh = ins[a].shape[0] // 2
            for k, chip in enumerate(chips):
                out.append(_rcopy(ins[a].at[pl.ds(c * rh, rh)], outs[a].at[k],
                                  ssem.at[self.base + 3 * a + k], rsem.at[self.base + 3 * a + k], (*chip, c)))
        return out


class _SwapPlan:
    def __init__(self, grads):
        self.operands = list(grads)
        self.out_shapes = [jax.ShapeDtypeStruct((g.shape[0], g.shape[1] // 2, g.shape[2]), g.dtype) for g in grads]
        self.aliases = {}
        self.nsems = len(grads)
        self.base = 0

    def _copies(self, ins, outs, ssem, rsem):
        x, y, c, me, chips, sib = _place()
        out = []
        for a, src in enumerate(ins):
            h = src.shape[1] // 2
            out.append(_rcopy(src.at[:, pl.ds((1 - c) * h, h), :], outs[a], ssem.at[self.base + a], rsem.at[self.base + a], sib))
        return out

    def schedule(self):
        return [(0.0, self.start)]

    def start(self, ins, outs, ssem, rsem):
        for cp in self._copies(ins, outs, ssem, rsem):
            cp.start()

    def finish(self, ins, outs, ssem, rsem):
        cps = self._copies(ins, outs, ssem, rsem)
        for cp in cps:
            cp.wait_recv()
        for cp in cps:
            cp.wait_send()


class _Multi:
    def __init__(self, plans):
        self.plans = plans
        self.operands, self.out_shapes, self.aliases, self.nsems = [], [], {}, 0
        self.spans = []
        for p in plans:
            ni, no = len(self.operands), len(self.out_shapes)
            self.spans.append((ni, ni + len(p.operands), no, no + len(p.out_shapes)))
            self.aliases.update({ni + i: no + j for i, j in p.aliases.items()})
            p.base = self.nsems
            self.nsems += p.nsems
            self.operands += p.operands
            self.out_shapes += p.out_shapes

    def schedule(self):
        def bound(fn, span):
            i0, i1, o0, o1 = span
            return lambda ins, outs, ssem, rsem: fn(ins[i0:i1], outs[o0:o1], ssem, rsem)

        stages = [(at, bound(fn, span)) for p, span in zip(self.plans, self.spans) for at, fn in p.schedule()]
        return sorted(stages, key=lambda s: s[0])

    def finish(self, ins, outs, ssem, rsem):
        for p, (i0, i1, o0, o1) in zip(self.plans, self.spans):
            p.finish(ins[i0:i1], outs[o0:o1], ssem, rsem)

    def results(self, extra):
        return [list(extra[o0:o1]) for (_, _, o0, o1) in self.spans]


class _Host:
    def __init__(self, comm, in_specs, out_specs, out_shape, scratch):
        self.comm = comm
        self.n_in, self.n_out = len(in_specs), len(out_specs)
        self.in_specs, self.out_specs, self.out_shape, self.scratch = list(in_specs), list(out_specs), list(out_shape), list(scratch)
        self.aliases = {}
        self.args = []
        if comm is not None:
            self.in_specs += [ANY] * len(comm.operands)
            self.out_specs += [ANY] * len(comm.out_shapes)
            self.out_shape += comm.out_shapes
            self.scratch += [pltpu.SemaphoreType.DMA((comm.nsems,)), pltpu.SemaphoreType.DMA((comm.nsems,))]
            self.aliases = {self.n_in + i: self.n_out + j for i, j in comm.aliases.items()}
            self.args = list(comm.operands)

    def split(self, refs):
        nc = len(self.args)
        nco = len(self.out_shape) - self.n_out
        ins, p = refs[:self.n_in], self.n_in + nc
        outs, rest = refs[p:p + self.n_out], refs[p + self.n_out + nco:]
        self._cargs = None
        if self.comm is not None:
            self._cargs = (refs[self.n_in:p], refs[p + self.n_out:p + self.n_out + nco], rest[-2], rest[-1])
            rest = rest[:-2]
        return ins, outs, rest

    def before(self, step, total):
        if self.comm is None:
            return

        for at, stage in self.comm.schedule():
            pl.when(step == min(total - 1, int(at * total)))(functools.partial(stage, *self._cargs))

    def after(self, step, total):
        if self.comm is None:
            return

        @pl.when(step == total - 1)
        def _():
            self.comm.finish(*self._cargs)

    def results(self, outs):
        return outs[:self.n_out], outs[self.n_out:]


def _cast_bf16(x, name):
    r, c = x.shape
    tr = ROW_TILE if r % ROW_TILE == 0 else r

    def body(x_ref, o_ref):
        o_ref[...] = x_ref[...].astype(BF16)

    return pl.pallas_call(
        body, name=name, grid=(r // tr,),
        in_specs=[pl.BlockSpec((tr, c), lambda i: (i, 0))],
        out_specs=pl.BlockSpec((tr, c), lambda i: (i, 0)),
        out_shape=jax.ShapeDtypeStruct((r, c), BF16), compiler_params=_cp("parallel"))(x)


def _cast_bf16_own_slab(x, me_arr, name):
    r, c = x.shape
    tr = ROW_TILE if r % ROW_TILE == 0 else r

    def body(me_ref, x_ref, o_ref):
        o_ref[...] = x_ref[...].astype(BF16)

    return pl.pallas_call(
        body, name=name,
        grid_spec=pltpu.PrefetchScalarGridSpec(
            num_scalar_prefetch=1, grid=(r // tr,),
            in_specs=[pl.BlockSpec((tr, c), lambda i, me: (i, 0))],
            out_specs=pl.BlockSpec((None, tr, c), lambda i, me: (me[0], i, 0))),
        out_shape=jax.ShapeDtypeStruct((N_CHIPS, r, c), BF16), compiler_params=_cp("parallel"))(me_arr, x)


def _rms_fwd(x, g, name):
    s, d = x.shape

    def body(x_ref, g_ref, h_ref):
        xhat, _ = _rms_stats(x_ref[...])
        h_ref[...] = (xhat * g_ref[...]).astype(BF16)

    return pl.pallas_call(
        body, name=name, grid=(s // ROW_TILE,),
        in_specs=[pl.BlockSpec((ROW_TILE, d), lambda i: (i, 0)), pl.BlockSpec((1, d), lambda i: (0, 0))],
        out_specs=pl.BlockSpec((ROW_TILE, d), lambda i: (i, 0)),
        out_shape=jax.ShapeDtypeStruct((s, d), BF16), compiler_params=_cp("parallel"))(x, g)


def _mm_nn(a, w3, name, comm=None):
    m, k = a.shape
    nsh, _, ns = w3.shape
    tm = 512 if m % 512 == 0 else ROW_TILE
    tn = _pick_tile(ns, 1024)
    per = ns // tn
    grid = (nsh * per, m // tm)
    host = _Host(comm,
                 [pl.BlockSpec((tm, k), lambda n, i: (i, 0)), pl.BlockSpec((None, k, tn), lambda n, i: (n // per, 0, n % per))],
                 [pl.BlockSpec((tm, tn), lambda n, i: (i, n))], [jax.ShapeDtypeStruct((m, nsh * ns), F32)], [])

    def body(*refs):
        (a_ref, w_ref), (o_ref,), _ = host.split(refs)
        step = pl.program_id(0) * grid[1] + pl.program_id(1)
        host.before(step, grid[0] * grid[1])
        o_ref[...] = jnp.dot(a_ref[...], w_ref[...], preferred_element_type=F32)
        host.after(step, grid[0] * grid[1])

    outs = pl.pallas_call(
        body, name=name, grid=grid, in_specs=host.in_specs, out_specs=host.out_specs, out_shape=host.out_shape,
        scratch_shapes=host.scratch, input_output_aliases=host.aliases,
        compiler_params=_cp("arbitrary", "arbitrary"))(a, w3, *host.args)
    (out,), extra = host.results(outs)
    return out, extra


def _mm_nt(a, b, name):
    m, k = a.shape
    n = b.shape[0]
    tm = 512 if m % 512 == 0 else ROW_TILE

    def body(a_ref, b_ref, o_ref):
        o_ref[...] = lax.dot_general(a_ref[...], b_ref[...], (((1,), (1,)), ((), ())), preferred_element_type=F32)

    return pl.pallas_call(
        body, name=name, grid=(m // tm,),
        in_specs=[pl.BlockSpec((tm, k), lambda i: (i, 0)), pl.BlockSpec((n, k), lambda i: (0, 0))],
        out_specs=pl.BlockSpec((tm, n), lambda i: (i, 0)),
        out_shape=jax.ShapeDtypeStruct((m, n), F32), compiler_params=_cp("parallel"))(a, b)


def _mm_tn(a, b, nsh, name, comm=None):
    s, m = a.shape
    n = b.shape[1]
    ns = n // nsh
    tm = 512 if m % 512 == 0 else ROW_TILE
    tn = _pick_tile(ns, 1024)
    per = ns // tn
    grid = (nsh * per, m // tm)
    host = _Host(comm, [pl.BlockSpec((s, tm), lambda j, i: (0, i)), pl.BlockSpec((s, tn), lambda j, i: (0, j))],
                 [pl.BlockSpec((None, tm, tn), lambda j, i: (j // per, i, j % per))],
                 [jax.ShapeDtypeStruct((nsh, m, ns), BF16)], [])

    def body(*refs):
        (a_ref, b_ref), (o_ref,), _ = host.split(refs)
        step = pl.program_id(0) * grid[1] + pl.program_id(1)
        host.before(step, grid[0] * grid[1])
        o_ref[...] = lax.dot_general(a_ref[...], b_ref[...], (((0,), (0,)), ((), ())),
                                     preferred_element_type=F32).astype(BF16)
        host.after(step, grid[0] * grid[1])

    outs = pl.pallas_call(
        body, name=name, grid=grid, in_specs=host.in_specs, out_specs=host.out_specs, out_shape=host.out_shape,
        scratch_shapes=host.scratch, input_output_aliases=host.aliases,
        compiler_params=_cp("arbitrary", "arbitrary"))(a, b, *host.args)
    (out,), extra = host.results(outs)
    return out, extra


def _tri(n, rel):
    row = lax.broadcasted_iota(jnp.int32, (2 * n, n), 0)
    col = lax.broadcasted_iota(jnp.int32, (2 * n, n), 1)
    return jnp.where(rel(jnp.where(row >= n, row - n, row), col), 1.0, 0.0).astype(BF16)


def _dot_split(x, tri2):
    hi = x.astype(BF16)
    lo = (x - hi.astype(F32)).astype(BF16)
    return jnp.dot(jnp.concatenate([hi, lo], axis=1), tri2, preferred_element_type=F32)


def _nt(a, b):
    return lax.dot_general(a, b, (((1,), (1,)), ((), ())), preferred_element_type=F32)


def _tn(a, b):
    return lax.dot_general(a, b, (((0,), (0,)), ((), ())), preferred_element_type=F32)


def _heads_per_step(nh):
    return max(h for h in (1, 2, 4) if nh % h == 0)


def _sba_fwd(p, sbw, name, comm=None):
    s = p.shape[0]
    nh = sbw // HEAD_DIM
    hp = _heads_per_step(nh)
    ngrp, hw = nh // hp, hp * HEAD_DIM
    blk = ATT_BLOCK
    nq = s // blk
    scale = 1.0 / math.sqrt(HEAD_DIM)
    host = _Host(comm,
                 [pl.BlockSpec((blk, hw), lambda g, i: (i, g)),
                  pl.BlockSpec((s, hw), lambda g, i: (0, ngrp + g)),
                  pl.BlockSpec((s, hw), lambda g, i: (0, 2 * ngrp + g))],
                 [pl.BlockSpec((blk, hw), lambda g, i: (i, g))] * 2,
                 [jax.ShapeDtypeStruct((s, sbw), F32)] * 2,
                 [pltpu.VMEM((s, hw), BF16)] * 2)

    def body(*refs):
        (q_ref, k_ref, v_ref), (o_ref, lt_ref), (kb_ref, vb_ref) = host.split(refs)
        i = pl.program_id(1)
        step = pl.program_id(0) * nq + i
        host.before(step, ngrp * nq)

        @pl.when(i == 0)
        def _():
            kb_ref[...] = k_ref[...].astype(BF16)
            vb_ref[...] = v_ref[...].astype(BF16)

        heads = [slice(h * HEAD_DIM, (h + 1) * HEAD_DIM) for h in range(hp)]
        qs = [q_ref[:, hd].astype(BF16) for hd in heads]
        later = _tri(blk, lambda r, c: r > c)
        causal = lax.broadcasted_iota(jnp.int32, (blk, blk), 1) < lax.broadcasted_iota(jnp.int32, (blk, blk), 0)

        def key_block(j, carry, diagonal):
            rows = pl.ds(pl.multiple_of(j * blk, blk), blk)
            hs = range(hp)
            z = [_nt(qs[h], kb_ref[rows, heads[h]]) * scale for h in hs]
            ls = [_log_sigmoid(z[h]) for h in hs]
            lm = [jnp.where(causal, ls[h] - z[h], 0.0) if diagonal else ls[h] - z[h] for h in hs]
            stay = [_dot_split(lm[h], later) for h in hs]
            w = [jnp.exp(ls[h] + stay[h] + carry[h][1]) for h in hs]
            if diagonal:
                w = [jnp.where(causal, w[h], 0.0) for h in hs]
            acc = [carry[h][0] + jnp.dot(w[h].astype(BF16), vb_ref[rows, heads[h]], preferred_element_type=F32) for h in hs]
            return tuple((acc[h], carry[h][1] + jnp.sum(lm[h], axis=1, keepdims=True)) for h in hs)

        init = tuple((jnp.zeros((blk, HEAD_DIM), F32), jnp.zeros((blk, 1), F32)) for _ in heads)
        carry = key_block(i, init, True)
        carry = lax.fori_loop(0, i, lambda n, c: key_block(i - 1 - n, c, False), carry)
        for h, hd in enumerate(heads):
            o_ref[:, hd] = carry[h][0]
            lt_ref[:, hd] = jnp.broadcast_to(carry[h][1], (blk, HEAD_DIM))
        host.after(step, ngrp * nq)

    outs = pl.pallas_call(
        body, name=name, grid=(ngrp, nq), in_specs=host.in_specs, out_specs=host.out_specs, out_shape=host.out_shape,
        scratch_shapes=host.scratch, input_output_aliases=host.aliases,
        compiler_params=_cp("arbitrary", "arbitrary"))(p, p, p, *host.args)
    (out, ltot), extra = host.results(outs)
    return out, ltot, extra


def _sba_bwd(p, ltot, dout, sbw, name, comm=None):
    s = p.shape[0]
    nh = sbw // HEAD_DIM
    hp = _heads_per_step(nh)
    ngrp, hw = nh // hp, hp * HEAD_DIM
    blk = ATT_BLOCK
    nq = s // blk
    scale = 1.0 / math.sqrt(HEAD_DIM)
    blk_spec = pl.BlockSpec((blk, hw), lambda g, i: (i, g))
    col_spec = pl.BlockSpec((s, hw), lambda g, i: (0, g))
    host = _Host(comm,
                 [blk_spec, pl.BlockSpec((s, hw), lambda g, i: (0, ngrp + g)),
                  pl.BlockSpec((s, hw), lambda g, i: (0, 2 * ngrp + g)), blk_spec, blk_spec],
                 [blk_spec, col_spec, col_spec], [jax.ShapeDtypeStruct((s, sbw), BF16)] * 3,
                 [pltpu.VMEM((s, hw), BF16)] * 2 + [pltpu.VMEM((s, hw), F32)] * 2)

    def body(*refs):
        (q_ref, k_ref, v_ref, lt_ref, do_ref), (dq_ref, dk_ref, dv_ref), (kb_ref, vb_ref, dka_ref, dva_ref) = host.split(refs)
        i = pl.program_id(1)
        step = pl.program_id(0) * nq + i
        host.before(step, ngrp * nq)

        @pl.when(i == 0)
        def _():
            kb_ref[...] = k_ref[...].astype(BF16)
            vb_ref[...] = v_ref[...].astype(BF16)
            dka_ref[...] = jnp.zeros_like(dka_ref)
            dva_ref[...] = jnp.zeros_like(dva_ref)

        heads = [slice(h * HEAD_DIM, (h + 1) * HEAD_DIM) for h in range(hp)]
        qs = [q_ref[:, hd].astype(BF16) for hd in heads]
        dos = [do_ref[:, hd].astype(BF16) for hd in heads]
        ltots = [lt_ref[:, h * HEAD_DIM:h * HEAD_DIM + 1] for h in range(hp)]
        upto = _tri(blk, lambda r, c: r <= c)
        before = _tri(blk, lambda r, c: r < c)
        causal = lax.broadcasted_iota(jnp.int32, (blk, blk), 1) < lax.broadcasted_iota(jnp.int32, (blk, blk), 0)

        def key_block(j, carry, diagonal):
            rows = pl.ds(pl.multiple_of(j * blk, blk), blk)
            hs = range(hp)
            kj = [kb_ref[rows, heads[h]] for h in hs]
            vj = [vb_ref[rows, heads[h]] for h in hs]
            z = [_nt(qs[h], kj[h]) * scale for h in hs]
            dw = [_nt(dos[h], vj[h]) for h in hs]
            ls = [_log_sigmoid(z[h]) for h in hs]
            lm = [jnp.where(causal, ls[h] - z[h], 0.0) if diagonal else ls[h] - z[h] for h in hs]
            stay = [ltots[h] - carry[h][1] - _dot_split(lm[h], upto) for h in hs]
            w = [jnp.exp(ls[h] + stay[h]) for h in hs]
            if diagonal:
                w = [jnp.where(causal, w[h], 0.0) for h in hs]
            da = [dw[h] * w[h] for h in hs]
            sig = [jnp.exp(ls[h]) for h in hs]
            chain = [sig[h] * (carry[h][2] + _dot_split(da[h], before)) for h in hs]
            if diagonal:
                chain = [jnp.where(causal, chain[h], 0.0) for h in hs]
            dzb = [((da[h] * (1.0 - sig[h]) - chain[h]) * scale).astype(BF16) for h in hs]
            dq = [carry[h][0] + jnp.dot(dzb[h], kj[h], preferred_element_type=F32) for h in hs]
            for h in hs:
                dka_ref[rows, heads[h]] += _tn(dzb[h], qs[h])
            for h in hs:
                dva_ref[rows, heads[h]] += _tn(w[h].astype(BF16), dos[h])
            return tuple((dq[h], carry[h][1] + jnp.sum(lm[h], axis=1, keepdims=True),
                          carry[h][2] + jnp.sum(da[h], axis=1, keepdims=True)) for h in hs)

        zero = jnp.zeros((blk, 1), F32)
        init = tuple((jnp.zeros((blk, HEAD_DIM), F32), zero, zero) for _ in heads)
        carry = lax.fori_loop(0, i, lambda j, c: key_block(j, c, False), init)
        carry = key_block(i, carry, True)
        for h, hd in enumerate(heads):
            dq_ref[:, hd] = carry[h][0].astype(BF16)

        @pl.when(i == nq - 1)
        def _():
            dk_ref[...] = dka_ref[...].astype(BF16)
            dv_ref[...] = dva_ref[...].astype(BF16)

        host.after(step, ngrp * nq)

    outs = pl.pallas_call(
        body, name=name, grid=(ngrp, nq), in_specs=host.in_specs, out_specs=host.out_specs, out_shape=host.out_shape,
        scratch_shapes=host.scratch, input_output_aliases=host.aliases,
        compiler_params=_cp("arbitrary", "arbitrary"))(p, p, p, ltot, dout, *host.args)
    (dq, dk, dv), extra = host.results(outs)
    return dq, dk, dv, extra


def _pool_groups(pad_ref, tile, row0, gd, halo):
    row = row0 + lax.broadcasted_iota(jnp.int32, (tile, 1), 0)
    out = []
    for gi, win in enumerate(POOL_WINDOWS):
        cs = slice(gi * gd, (gi + 1) * gd)
        tok = pad_ref[halo:halo + tile, cs]
        acc = tok
        for j in range(1, win):
            acc = acc + pad_ref[halo - j:halo - j + tile, cs]
        cnt = jnp.minimum(win, row + 1).astype(F32)
        out.append(acc / cnt - tok)
    return out


def _even_mix_fwd(p, att, pool_w, pool_scale, d, name):
    s = p.shape[0]
    half = d // 2
    gd = half // len(POOL_WINDOWS)
    t, hb = ROW_TILE, POOL_HALO

    def body(u_ref, uh_ref, g_ref, a_ref, pw_ref, sc_ref, y_ref, pad_ref):
        i = pl.program_id(0)
        pad_ref[0:hb, :] = jnp.where(i > 0, uh_ref[...], 0.0)
        pad_ref[hb:, :] = u_ref[...]
        pooled = _pool_groups(pad_ref, t, i * t, gd, hb)
        for gi in range(len(POOL_WINDOWS)):
            cs = slice(gi * gd, (gi + 1) * gd)
            po = jnp.dot(pooled[gi].astype(BF16), pw_ref[gi], preferred_element_type=F32) * sc_ref[:, cs]
            y_ref[:, half + gi * gd:half + (gi + 1) * gd] = (po * _silu(g_ref[:, half + gi * gd:half + (gi + 1) * gd])).astype(BF16)
        y_ref[:, :half] = (a_ref[...] * _silu(g_ref[:, :half])).astype(BF16)

    return pl.pallas_call(
        body, name=name, grid=(s // t,),
        in_specs=[pl.BlockSpec((t, half), lambda i: (i, 3)),
                  pl.BlockSpec((hb, half), lambda i: (jnp.maximum(i * (t // hb) - 1, 0), 3)),
                  pl.BlockSpec((t, d), lambda i: (i, 2)),
                  pl.BlockSpec((t, half), lambda i: (i, 0)),
                  pl.BlockSpec(pool_w.shape, lambda i: (0, 0, 0)),
                  pl.BlockSpec((1, half), lambda i: (0, 0))],
        out_specs=pl.BlockSpec((t, d), lambda i: (i, 0)),
        out_shape=jax.ShapeDtypeStruct((s, d), BF16),
        scratch_shapes=[pltpu.VMEM((hb + t, half), F32)],
        compiler_params=_cp("parallel"))(p, p, p, att, pool_w, pool_scale)


def _even_mix_bwd(p, att, dy, pool_w, pool_scale, d, name, comm=None):
    s = p.shape[0]
    half = d // 2
    ng = len(POOL_WINDOWS)
    gd = half // ng
    t, hb = ROW_TILE, POOL_HALO
    nt = s // t
    host = _Host(
        comm,
        [pl.BlockSpec((t, half), lambda i: (i, 3)),
         pl.BlockSpec((hb, half), lambda i: (jnp.maximum(i * (t // hb) - 1, 0), 3)),
         pl.BlockSpec((t, d), lambda i: (i, 2)),
         pl.BlockSpec((hb, half), lambda i: (jnp.minimum((i + 1) * (t // hb), s // hb - 1), 5)),
         pl.BlockSpec((t, half), lambda i: (i, 0)),
         pl.BlockSpec((t, d), lambda i: (i, 0)),
         pl.BlockSpec((hb, half), lambda i: (jnp.minimum((i + 1) * (t // hb), s // hb - 1), 1)),
         pl.BlockSpec(pool_w.shape, lambda i: (0, 0, 0)),
         pl.BlockSpec((1, half), lambda i: (0, 0))],
        [pl.BlockSpec((t, half), lambda i: (i, 0)),
         pl.BlockSpec((t, half), lambda i: (i, 0)),
         pl.BlockSpec((t, d), lambda i: (i, 0)),
         pl.BlockSpec((1, half), lambda i: (0, 0)),
         pl.BlockSpec((ng, gd, gd), lambda i: (0, 0, 0))],
        [jax.ShapeDtypeStruct((s, half), F32), jax.ShapeDtypeStruct((s, half), BF16),
         jax.ShapeDtypeStruct((s, d), BF16), jax.ShapeDtypeStruct((1, half), F32),
         jax.ShapeDtypeStruct((ng, gd, gd), F32)],
        [pltpu.VMEM((hb + t, half), F32), pltpu.VMEM((t + hb, half), F32)])

    def body(*refs):
        ((u_ref, uh_ref, g_ref, gh_ref, a_ref, dy_ref, dyh_ref, pw_ref, sc_ref),
         (da_ref, du_ref, dg_ref, dsc_ref, dpw_ref), (pad_ref, dn_ref)) = host.split(refs)
        i = pl.program_id(0)
        host.before(i, nt)
        first = i == 0
        pad_ref[0:hb, :] = jnp.where(i > 0, uh_ref[...], 0.0)
        pad_ref[hb:, :] = u_ref[...]
        pooled = _pool_groups(pad_ref, t, i * t, gd, hb)
        g1 = g_ref[:, :half]
        dy1 = dy_ref[:, :half]
        da_ref[...] = dy1 * _silu(g1)
        dg_ref[:, :half] = (dy1 * a_ref[...] * _dsilu(g1)).astype(BF16)
        row = i * t + lax.broadcasted_iota(jnp.int32, (t + hb, 1), 0)
        for gi, win in enumerate(POOL_WINDOWS):
            cs = slice(gi * gd, (gi + 1) * gd)
            cs2 = slice(half + gi * gd, half + (gi + 1) * gd)
            w = pw_ref[gi]
            pb = pooled[gi].astype(BF16)
            zp = jnp.dot(pb, w, preferred_element_type=F32)
            g2 = g_ref[:, cs2]
            dy2 = dy_ref[:, cs2]
            dg_ref[:, cs2] = (dy2 * zp * sc_ref[:, cs] * _dsilu(g2)).astype(BF16)
            dpo = dy2 * _silu(g2)
            _acc_rows(dsc_ref.at[:, cs], first, jnp.sum(dpo * zp, axis=0, keepdims=True))
            dz = (dpo * sc_ref[:, cs]).astype(BF16)
            _acc_rows(dpw_ref.at[gi], first, _tn(pb, dz))
            dzh = jnp.where(i < nt - 1, dyh_ref[:, cs] * _silu(gh_ref[:, cs]) * sc_ref[:, cs], 0.0).astype(BF16)
            dpool = _nt(dz, w)
            dpool_h = _nt(dzh, w)
            cnt = jnp.minimum(win, row + 1).astype(F32)
            dn_ref[0:t, cs] = dpool / cnt[0:t]
            dn_ref[t:, cs] = dpool_h / cnt[t:]
            acc = dn_ref[0:t, cs]
            for j in range(1, win):
                acc = acc + dn_ref[j:j + t, cs]
            du_ref[:, cs] = (acc - dpool).astype(BF16)
        host.after(i, nt)

    outs = pl.pallas_call(
        body, name=name, grid=(nt,), in_specs=host.in_specs, out_specs=host.out_specs, out_shape=host.out_shape,
        scratch_shapes=host.scratch, input_output_aliases=host.aliases,
        compiler_params=_cp("arbitrary"))(p, p, p, p, att, dy, dy, pool_w, pool_scale, *host.args)
    return host.results(outs)


def _mm_out_even(y, w, x, g_post, g_pre_next, name):
    s, k = y.shape
    d = w.shape[1]
    t = ROW_TILE

    def body(y_ref, w_ref, x_ref, gp_ref, gn_ref, o_ref, x1_ref, h1_ref):
        for r0 in range(0, t, t // 2):
            rows = slice(r0, r0 + t // 2)
            o = jnp.dot(y_ref[rows, :], w_ref[...], preferred_element_type=F32)
            o_ref[rows, :] = o
            ohat, _ = _rms_stats(o)
            x1 = x_ref[rows, :] + ohat * gp_ref[...]
            x1_ref[rows, :] = x1
            xhat, _ = _rms_stats(x1)
            h1_ref[rows, :] = (xhat * gn_ref[...]).astype(BF16)

    row = lambda c: pl.BlockSpec((t, c), lambda i: (i, 0))
    vec = pl.BlockSpec((1, d), lambda i: (0, 0))
    return pl.pallas_call(
        body, name=name, grid=(s // t,),
        in_specs=[row(k), pl.BlockSpec((k, d), lambda i: (0, 0)), row(d), vec, vec],
        out_specs=[row(d), row(d), row(d)],
        out_shape=[jax.ShapeDtypeStruct((s, d), F32), jax.ShapeDtypeStruct((s, d), F32),
                   jax.ShapeDtypeStruct((s, d), BF16)],
        compiler_params=_cp("parallel"))(y, w, x, g_post, g_pre_next)


def _mm_out_odd(y, w, x1, g_post, target, name):
    s, k = y.shape
    d = w.shape[1]
    t = ROW_TILE

    def body(y_ref, w_ref, x_ref, gp_ref, tg_ref, do_ref, dx_ref, loss_ref, dgp_ref):
        first = pl.program_id(0) == 0
        gp = gp_ref[...]
        part = dgp = None
        for r0 in range(0, t, t // 2):
            rows = slice(r0, r0 + t // 2)
            o = jnp.dot(y_ref[rows, :], w_ref[...], preferred_element_type=F32)
            ohat, r = _rms_stats(o)
            diff = x_ref[rows, :] + ohat * gp - tg_ref[rows, :]
            part_half = 0.5 * jnp.sum(jnp.mean(diff * diff, axis=-1, keepdims=True), axis=0, keepdims=True)
            dx2 = diff * (1.0 / d)
            dx_ref[rows, :] = dx2
            do, dgp_half = _rms_bwd(dx2, ohat, r, gp)
            do_ref[rows, :] = do.astype(BF16)
            part = part_half if part is None else part + part_half
            dgp = dgp_half if dgp is None else dgp + dgp_half
        _acc_rows(loss_ref, first, jnp.broadcast_to(part, loss_ref.shape))
        _acc_rows(dgp_ref, first, dgp)

    row = lambda c: pl.BlockSpec((t, c), lambda i: (i, 0))
    vec = pl.BlockSpec((1, d), lambda i: (0, 0))
    return pl.pallas_call(
        body, name=name, grid=(s // t,),
        in_specs=[row(k), pl.BlockSpec((k, d), lambda i: (0, 0)), row(d), vec, row(d)],
        out_specs=[row(d), row(d), pl.BlockSpec((8, LANES), lambda i: (0, 0)), vec],
        out_shape=[jax.ShapeDtypeStruct((s, d), BF16), jax.ShapeDtypeStruct((s, d), F32),
                   jax.ShapeDtypeStruct((8, LANES), F32), jax.ShapeDtypeStruct((1, d), F32)],
        compiler_params=_cp("arbitrary"))(y, w, x1, g_post, target)


def _layer_norm(d1, cg, cb):
    mu = jnp.mean(d1, axis=-1, keepdims=True)
    cen = d1 - mu
    rstd = lax.rsqrt(jnp.mean(cen * cen, axis=-1, keepdims=True) + EPS)
    n = cen * rstd
    return n, rstd, n * cg + cb


SUBLANES = 8
ROW_STRIP = 64
GATHER_PIECES = 8
CONV_ROWS = 64


def _make_shifts(pad_ref, cs, sh_ref):
    rows = sh_ref.shape[1]
    for r in range(1, SUBLANES):
        sh_ref[r - 1] = pad_ref[r:r + rows, cs]


def _by_shift(taps, base, sign=1):
    return sorted(range(taps), key=lambda k: ((sign * (base + k)) % SUBLANES, k))


def _window(pad_ref, cs, sh_ref, off, t):
    m, r = divmod(off, SUBLANES)
    if r == 0:
        return pad_ref[SUBLANES * m:SUBLANES * m + t, cs]
    return sh_ref[r - 1, SUBLANES * m:SUBLANES * m + t, :]


def _odd_mix_fwd(p, sconv_w, dconv_w, dconv_b, cnorm_g, cnorm_b, d, name):
    s = p.shape[0]
    w = d // 2
    k3, k31 = sconv_w.shape[0], dconv_w.shape[0]
    t, hb = ROW_TILE, CONV_HALO
    assert hb >= k31 - 1 and w % LANES == 0

    def body(p_ref, ph_ref, w3_ref, w31_ref, b31_ref, cg_ref, cb_ref, y_ref, s3_ref, d1_ref, mpad, dpad, sh_ref):
        i = pl.program_id(0)
        mpad[0:hb, :] = jnp.where(i > 0, ph_ref[:, 2 * w:3 * w] * ph_ref[:, 0:w], 0.0)
        mpad[hb:, :] = p_ref[:, 2 * w:3 * w] * p_ref[:, 0:w]
        dpad[0:hb, :] = jnp.where(i > 0, ph_ref[:, 3 * w:4 * w] * _sigmoid(ph_ref[:, 4 * w:5 * w]), 0.0)
        dpad[hb:, :] = p_ref[:, 3 * w:4 * w] * _sigmoid(p_ref[:, 4 * w:5 * w])
        for c0 in range(0, w, LANES):
            cs = slice(c0, c0 + LANES)
            acc = jnp.zeros((t, LANES), F32)
            for kk in range(k3):
                acc = acc + w3_ref[kk:kk + 1, cs] * mpad[hb - (k3 - 1) + kk:hb - (k3 - 1) + kk + t, cs]
            s3_ref[:, cs] = acc
            _make_shifts(dpad, cs, sh_ref)
            for r0 in range(0, t, CONV_ROWS):
                acc = jnp.zeros((CONV_ROWS, LANES), F32)
                for kk in _by_shift(k31, hb - (k31 - 1)):
                    acc = acc + w31_ref[kk:kk + 1, cs] * _window(dpad, cs, sh_ref, hb - (k31 - 1) + kk + r0, CONV_ROWS)
                d1_ref[r0:r0 + CONV_ROWS, cs] = acc + b31_ref[:, cs]
        _, _, d2 = _layer_norm(d1_ref[...], cg_ref[...], cb_ref[...])
        y_ref[:, :w] = (p_ref[:, w:2 * w] * s3_ref[...] * _silu(p_ref[:, 5 * w:6 * w])).astype(BF16)
        y_ref[:, w:] = (_silu(d2) * _silu(p_ref[:, 6 * w:7 * w])).astype(BF16)

    row = lambda c: pl.BlockSpec((t, c), lambda i: (i, 0))
    full = lambda a: pl.BlockSpec(a.shape, lambda i: (0, 0))
    return pl.pallas_call(
        body, name=name, grid=(s // t,),
        in_specs=[row(7 * w),
                  pl.BlockSpec((hb, 5 * w), lambda i: (jnp.maximum(i * (t // hb) - 1, 0), 0)),
                  full(sconv_w), full(dconv_w), full(dconv_b), full(cnorm_g), full(cnorm_b)],
        out_specs=[row(d), row(w), row(w)],
        out_shape=[jax.ShapeDtypeStruct((s, d), BF16), jax.ShapeDtypeStruct((s, w), F32),
                   jax.ShapeDtypeStruct((s, w), F32)],
        scratch_shapes=[pltpu.VMEM((hb + t, w), F32)] * 2 + [pltpu.VMEM((SUBLANES - 1, hb + t - SUBLANES, LANES), F32)],
        compiler_params=_cp("parallel"))(p, p, sconv_w, dconv_w, dconv_b, cnorm_g, cnorm_b)


def _odd_bwd_rows(p, s3, d1, dy, cnorm_g, cnorm_b, d, name, comm=None):
    s = p.shape[0]
    w = d // 2
    t = ROW_TILE
    col = lambda j: pl.BlockSpec((t, w), lambda i: (i, j))
    row = lambda c: pl.BlockSpec((t, c), lambda i: (i, 0))
    vec = pl.BlockSpec((1, w), lambda i: (0, 0))
    host = _Host(comm, [col(1), col(5), col(6), row(w), row(w), row(d), vec, vec],
                 [row(w), row(d), row(w), row(w), vec, vec, vec],
                 [jax.ShapeDtypeStruct((s, w), BF16), jax.ShapeDtypeStruct((s, d), BF16),
                  jax.ShapeDtypeStruct((s, w), F32), jax.ShapeDtypeStruct((s, w), F32)] + [jax.ShapeDtypeStruct((1, w), F32)] * 3, [])

    def body(*refs):
        ((bc_ref, g1_ref, g2_ref, s3_ref, d1_ref, dy_ref, cg_ref, cb_ref),
         (dbc_ref, dg_ref, ds3_ref, dd1_ref, dcg_ref, dcb_ref, db_ref), _) = host.split(refs)
        step = pl.program_id(0)
        host.before(step, s // t)
        first = step == 0

        def strip(j, sums):
            rows = slice(j * ROW_STRIP, (j + 1) * ROW_STRIP)
            g1, g2 = g1_ref[rows, :], g2_ref[rows, :]
            bc, s3v = bc_ref[rows, :], s3_ref[rows, :]
            dy1, dy2 = dy_ref[rows, :w], dy_ref[rows, w:]
            n, rstd, d2 = _layer_norm(d1_ref[rows, :], cg_ref[...], cb_ref[...])
            dg_ref[rows, :w] = (dy1 * bc * s3v * _dsilu(g1)).astype(BF16)
            dg_ref[rows, w:] = (dy2 * _silu(d2) * _dsilu(g2)).astype(BF16)
            dco = dy1 * _silu(g1)
            dbc_ref[rows, :] = (dco * s3v).astype(BF16)
            ds3_ref[rows, :] = dco * bc
            dd2 = dy2 * _silu(g2) * _dsilu(d2)
            dn = dd2 * cg_ref[...]
            dd1 = rstd * (dn - jnp.mean(dn, axis=-1, keepdims=True) - n * jnp.mean(dn * n, axis=-1, keepdims=True))
            dd1_ref[rows, :] = dd1
            dcb, dcg, db = sums
            return (dcb + jnp.sum(dd2, axis=0, keepdims=True), dcg + jnp.sum(dd2 * n, axis=0, keepdims=True),
                    db + jnp.sum(dd1, axis=0, keepdims=True))

        zero = jnp.zeros((1, w), F32)
        sums = (zero, zero, zero)
        for j in range(t // ROW_STRIP):
            sums = strip(j, sums)
        dcb, dcg, db = sums
        _acc_rows(dcb_ref, first, dcb)
        _acc_rows(dcg_ref, first, dcg)
        _acc_rows(db_ref, first, db)
        host.after(step, s // t)

    outs = pl.pallas_call(
        body, name=name, grid=(s // t,), in_specs=host.in_specs, out_specs=host.out_specs, out_shape=host.out_shape,
        scratch_shapes=host.scratch, input_output_aliases=host.aliases,
        compiler_params=_cp("arbitrary"))(p, p, p, s3, d1, dy, cnorm_g, cnorm_b, *host.args)
    return host.results(outs)


def _odd_bwd_conv(p, ds3, dd1, sconv_w, dconv_w, d, name):
    s = p.shape[0]
    w = d // 2
    k3, k31 = sconv_w.shape[0], dconv_w.shape[0]
    t, hb, ha = ROW_TILE, CONV_HALO, 8
    nt = s // t
    assert hb >= k31 - 1 and ha >= k3 - 1

    def body(hc_ref, cc_ref, ga_ref, gb_ref, hch_ref, cch_ref, gah_ref, gbh_ref, ds3_ref, ds3h_ref, dd1_ref, dd1h_ref,
             w3_ref, w31_ref, dhc_ref, dcc_ref, dga_ref, dgb_ref, dw3_ref, dw31_ref, mpad, dpad, s3pad, d1pad, sh_ref):
        i = pl.program_id(0)
        first = i == 0
        last = i == nt - 1
        mpad[0:hb, :] = jnp.where(i > 0, cch_ref[...] * hch_ref[...], 0.0)
        mpad[hb:, :] = cc_ref[...] * hc_ref[...]
        dpad[0:hb, :] = jnp.where(i > 0, gah_ref[...] * _sigmoid(gbh_ref[...]), 0.0)
        dpad[hb:, :] = ga_ref[...] * _sigmoid(gb_ref[...])
        s3pad[0:t, :] = ds3_ref[...]
        s3pad[t:, :] = jnp.where(last, 0.0, ds3h_ref[...])
        d1pad[0:t, :] = dd1_ref[...]
        d1pad[t:, :] = jnp.where(last, 0.0, dd1h_ref[...])

        @pl.when(first)
        def _():
            dw3_ref[...] = jnp.zeros_like(dw3_ref)
            dw31_ref[...] = jnp.zeros_like(dw31_ref)

        def fold(v):
            return jnp.sum(v.reshape(v.shape[0] // SUBLANES, SUBLANES, LANES), axis=0)

        groups = range(0, t, CONV_ROWS)
        for c0 in range(0, w, LANES):
            cs = slice(c0, c0 + LANES)
            ds3v = s3pad[0:t, cs]
            dm = jnp.zeros((t, LANES), F32)
            for kk in range(k3):
                dm = dm + w3_ref[kk:kk + 1, cs] * s3pad[k3 - 1 - kk:k3 - 1 - kk + t, cs]
                off = hb - (k3 - 1) + kk
                dw3_ref[SUBLANES * kk:SUBLANES * (kk + 1), cs] += fold(ds3v * mpad[off:off + t, cs])
            dcc_ref[:, cs] = (dm * hc_ref[:, cs]).astype(BF16)
            dhc_ref[:, cs] = (dm * cc_ref[:, cs]).astype(BF16)
            _make_shifts(d1pad, cs, sh_ref)
            for r0 in groups:
                rows = slice(r0, r0 + CONV_ROWS)
                dd0 = jnp.zeros((CONV_ROWS, LANES), F32)
                for kk in _by_shift(k31, -(k31 - 1), -1):
                    dd0 = dd0 + w31_ref[kk:kk + 1, cs] * _window(d1pad, cs, sh_ref, k31 - 1 - kk + r0, CONV_ROWS)
                sgb = _sigmoid(gb_ref[rows, cs])
                dga_ref[rows, cs] = (dd0 * sgb).astype(BF16)
                dgb_ref[rows, cs] = (dd0 * ga_ref[rows, cs] * sgb * (1.0 - sgb)).astype(BF16)
            _make_shifts(dpad, cs, sh_ref)
            for kk in _by_shift(k31, hb - (k31 - 1)):
                part = jnp.zeros((SUBLANES, LANES), F32)
                for r0 in groups:
                    part = part + fold(d1pad[r0:r0 + CONV_ROWS, cs]
                                       * _window(dpad, cs, sh_ref, hb - (k31 - 1) + kk + r0, CONV_ROWS))
                dw31_ref[SUBLANES * kk:SUBLANES * (kk + 1), cs] += part

    col = lambda j: pl.BlockSpec((t, w), lambda i: (i, j))
    pre = lambda j: pl.BlockSpec((hb, w), lambda i: (jnp.maximum(i * (t // hb) - 1, 0), j))
    row = pl.BlockSpec((t, w), lambda i: (i, 0))
    post = lambda h: pl.BlockSpec((h, w), lambda i: (jnp.minimum((i + 1) * (t // h), s // h - 1), 0))
    full = lambda a: pl.BlockSpec(a.shape, lambda i: (0, 0))
    dhc, dcc, dga, dgb, dw3, dw31 = pl.pallas_call(
        body, name=name, grid=(nt,),
        in_specs=[col(0), col(2), col(3), col(4), pre(0), pre(2), pre(3), pre(4),
                  row, post(ha), row, post(hb), full(sconv_w), full(dconv_w)],
        out_specs=[row, row, row, row, pl.BlockSpec((SUBLANES * k3, w), lambda i: (0, 0)),
                   pl.BlockSpec((SUBLANES * k31, w), lambda i: (0, 0))],
        out_shape=[jax.ShapeDtypeStruct((s, w), BF16)] * 4
        + [jax.ShapeDtypeStruct((SUBLANES * k3, w), F32), jax.ShapeDtypeStruct((SUBLANES * k31, w), F32)],
        scratch_shapes=[pltpu.VMEM((hb + t, w), F32)] * 2 + [pltpu.VMEM((t + ha, w), F32), pltpu.VMEM((t + hb, w), F32),
                                                             pltpu.VMEM((SUBLANES - 1, hb + t - SUBLANES, LANES), F32)],
        compiler_params=_cp("arbitrary"))(p, p, p, p, p, p, p, p, ds3, ds3, dd1, dd1, sconv_w, dconv_w)
    return dhc, dcc, dga, dgb, jnp.sum(dw3.reshape(k3, SUBLANES, w), axis=1), jnp.sum(dw31.reshape(k31, SUBLANES, w), axis=1)


def _mm_in_bwd(dp, w3, x, g_pre, dres, post, name, comm=None):
    s = dp.shape[0]
    nsh, d, ns = w3.shape
    t = 512 if s % 512 == 0 else ROW_TILE
    nt = s // t
    ks = 2 if (ns // 2) % LANES == 0 else 1
    nk, kw = nsh * ks, ns // ks
    chunk = 128
    nchunk = t // chunk
    row = pl.BlockSpec((t, d), lambda i, k: (i, 0))
    vec = pl.BlockSpec((1, d), lambda i, k: (0, 0))
    rowwise = [x, dres] + ([post[0]] if post is not None else [])
    in_specs = [pl.BlockSpec((t, kw), lambda i, k: (i, k)), pl.BlockSpec((None, d, kw), lambda i, k: (k // ks, 0, k % ks)), vec]
    out_specs = [row, vec]
    out_shape = [jax.ShapeDtypeStruct((s, d), F32), jax.ShapeDtypeStruct((1, d), F32)]
    args = [dp, w3, g_pre]
    if post is not None:
        in_specs += [vec]
        out_specs += [row, vec]
        out_shape += [jax.ShapeDtypeStruct((s, d), BF16), jax.ShapeDtypeStruct((1, d), F32)]
        args += [post[1]]
    n_blocked = len(in_specs)
    in_specs += [ANY] * len(rowwise)
    args += rowwise
    host = _Host(comm, in_specs, out_specs, out_shape,
                 [pltpu.VMEM((t, d), F32), pltpu.VMEM((len(rowwise), 2, chunk, d), F32), pltpu.SemaphoreType.DMA((len(rowwise), 2))])

    def body(*refs):
        ins, outs, (acc_ref, buf_ref, sem_ref) = host.split(refs)
        dp_ref, w_ref, g_ref = ins[:3]
        hbm = ins[n_blocked:]
        dx_ref, dg_ref = outs[:2]
        tile = pl.program_id(0)
        kk = pl.program_id(1)
        first = tile == 0
        step = tile * nk + kk
        host.before(step, nt * nk)
        part = _nt(dp_ref[...], w_ref[...])

        @pl.when(kk == 0)
        def _():
            acc_ref[...] = part

        @pl.when(kk > 0)
        def _():
            acc_ref[...] += part

        def fetch(ci, slot):
            return [pltpu.make_async_copy(src.at[pl.ds(tile * t + ci * chunk, chunk)], buf_ref.at[n, slot], sem_ref.at[n, slot])
                    for n, src in enumerate(hbm)]

        @pl.when(kk == nk - 1)
        def _():
            dg = dgp = None
            for cp in fetch(0, 0):
                cp.start()
            for ci in range(nchunk):
                slot = ci % 2
                if ci + 1 < nchunk:
                    for cp in fetch(ci + 1, 1 - slot):
                        cp.start()
                for cp in fetch(ci, slot):
                    cp.wait()
                rows = slice(ci * chunk, (ci + 1) * chunk)
                xhat, r = _rms_stats(buf_ref[0, slot])
                dxn, dg_part = _rms_bwd(acc_ref[rows, :], xhat, r, g_ref[...])
                dx = buf_ref[1, slot] + dxn
                dx_ref[rows, :] = dx
                dg = dg_part if dg is None else dg + dg_part
                if post is not None:
                    ohat, ro = _rms_stats(buf_ref[2, slot])
                    do, dgp_part = _rms_bwd(dx, ohat, ro, ins[3][...])
                    outs[2][rows, :] = do.astype(BF16)
                    dgp = dgp_part if dgp is None else dgp + dgp_part
            _acc_rows(dg_ref, first, dg)
            if post is not None:
                _acc_rows(outs[3], first, dgp)

        host.after(step, nt * nk)

    res = pl.pallas_call(
        body, name=name, grid=(nt, nk), in_specs=host.in_specs, out_specs=host.out_specs, out_shape=host.out_shape,
        scratch_shapes=host.scratch, input_output_aliases=host.aliases,
        compiler_params=_cp("arbitrary", "arbitrary"))(*args, *host.args)
    return host.results(res)


def _half_add(g, r1, c_arr, name):
    nsh, rows, ns = g.shape
    h = rows // 2
    tr = min(ROW_TILE, h)
    per = h // tr

    def body(c_ref, g_ref, r_ref, o_ref):
        o_ref[...] = (g_ref[...].astype(F32) + r_ref[...].astype(F32)).astype(BF16)

    spec = pl.BlockSpec((None, tr, ns), lambda s, r, c: (s, r, 0))
    return pl.pallas_call(
        body, name=name,
        grid_spec=pltpu.PrefetchScalarGridSpec(
            num_scalar_prefetch=1, grid=(nsh, per),
            in_specs=[pl.BlockSpec((None, tr, ns), lambda s, r, c: (s, c[0] * per + r, 0)), spec], out_specs=spec),
        out_shape=jax.ShapeDtypeStruct((nsh, h, ns), BF16), compiler_params=_cp("parallel", "parallel"))(c_arr, g, r1)


def _sum_chips(hh, r2, mc_arr, name):
    _, h, ns = hh.shape
    tr = min(ROW_TILE, h)
    per = h // tr

    def body(mc_ref, h_ref, a_ref, b_ref, c_ref, o_ref):
        o_ref[...] = ((h_ref[...].astype(F32) + a_ref[...].astype(F32)) + b_ref[...].astype(F32)) + c_ref[...].astype(F32)

    got = lambda k: pl.BlockSpec((None, tr, ns), lambda r, mc: (k, r, 0))
    return pl.pallas_call(
        body, name=name,
        grid_spec=pltpu.PrefetchScalarGridSpec(
            num_scalar_prefetch=1, grid=(per,),
            in_specs=[pl.BlockSpec((None, tr, ns), lambda r, mc: (mc[0], r, 0)), got(0), got(1), got(2)],
            out_specs=pl.BlockSpec((tr, ns), lambda r, mc: (mc[1] * per + r, 0))),
        out_shape=jax.ShapeDtypeStruct((2 * h, ns), F32), compiler_params=_cp("parallel"))(mc_arr, hh, r2, r2, r2)


def _add2(a, b, name):
    def body(a_ref, b_ref, o_ref):
        o_ref[...] = a_ref[...] + b_ref[...]

    return pl.pallas_call(body, name=name, out_shape=jax.ShapeDtypeStruct(a.shape, a.dtype), compiler_params=_cp())(a, b)


def _sum_chips_ordered(s2, r2, mc_arr, name):
    rows, w = s2.shape
    rh = rows // 2

    def body(mc_ref, s_ref, a_ref, b_ref, c_ref, o_ref):
        me = mc_ref[0]
        acc = None
        for j in range(N_CHIPS):
            rel = jnp.bitwise_xor(me, j)
            v = jnp.where(rel == 0, s_ref[...], jnp.where(rel == 2, a_ref[...], jnp.where(rel == 1, b_ref[...], c_ref[...])))
            acc = v if acc is None else acc + v
        o_ref[...] = acc

    got = lambda k: pl.BlockSpec((None, rh, w), lambda i, mc: (k, 0, 0))
    return pl.pallas_call(
        body, name=name,
        grid_spec=pltpu.PrefetchScalarGridSpec(
            num_scalar_prefetch=1, grid=(1,),
            in_specs=[pl.BlockSpec((rh, w), lambda i, mc: (mc[1], 0)), got(0), got(1), got(2)],
            out_specs=pl.BlockSpec((rh, w), lambda i, mc: (mc[1], 0))),
        out_shape=jax.ShapeDtypeStruct((rows, w), F32), compiler_params=_cp("arbitrary"))(mc_arr, s2, r2, r2, r2)


def _adamw(w, g, m, v, name, comm=None):
    r, c = w.shape
    tr = ROW_TILE if r % ROW_TILE == 0 else r
    c1 = 1.0 / (1.0 - ADAM_B1 ** ADAM_STEP)
    c2 = 1.0 / (1.0 - ADAM_B2 ** ADAM_STEP)
    spec = pl.BlockSpec((tr, c), lambda i: (i, 0))
    host = _Host(comm, [spec] * 4, [spec] * 4, [jax.ShapeDtypeStruct((r, c), F32)] * 4, [])

    def body(*refs):
        (w_ref, g_ref, m_ref, v_ref), (go_ref, d_ref, nm_ref, nv_ref), _ = host.split(refs)
        step = pl.program_id(0)
        host.before(step, r // tr)
        gv = g_ref[...]
        go_ref[...] = gv
        nm = ADAM_B1 * m_ref[...] + (1.0 - ADAM_B1) * gv
        nv = ADAM_B2 * v_ref[...] + (1.0 - ADAM_B2) * (gv * gv)
        nm_ref[...] = nm
        nv_ref[...] = nv
        d_ref[...] = -ADAM_LR * ((nm * c1) / (jnp.sqrt(nv * c2) + ADAM_EPS) + ADAM_WD * w_ref[...])
        host.after(step, r // tr)

    outs = pl.pallas_call(
        body, name=name, grid=(r // tr,), in_specs=host.in_specs, out_specs=host.out_specs, out_shape=host.out_shape,
        scratch_shapes=host.scratch, input_output_aliases=host.aliases,
        compiler_params=_cp("arbitrary"))(w, g, m, v, *host.args)
    return host.results(outs)


def _gather_weights(bigs, pool_w, pack_w, pack_d, name):
    nb = len(bigs)
    smalls = [pool_w, pack_w, pack_d]
    q, cw, cd = pool_w.shape[1], pack_w.shape[1], pack_d.shape[1]
    pieces = [_GatherPlan(bigs, (j, j + 1, GATHER_PIECES)) for j in range(GATHER_PIECES)]
    for j, piece in enumerate(pieces):
        piece.base = 9 + j * piece.nsems

    def body(*refs):
        srcs, dsts = refs[:nb + 3], refs[nb + 3:2 * (nb + 3)]
        ssem, rsem, lsem = refs[2 * (nb + 3):]
        x, y, c, me, chips, sib = _place()

        def small_dst(n, chip):
            if n == 0:
                return dsts[nb].at[:, pl.ds(chip * q, q), :]
            return dsts[nb + n].at[:, pl.ds(chip * (cw if n == 1 else cd), cw if n == 1 else cd)]

        local = [pltpu.make_async_copy(srcs[nb + n], small_dst(n, me), lsem.at[n]) for n in range(3)]
        for cp in local:
            cp.start()
        sends = []
        for n in range(3):
            for k, chip in enumerate(chips):
                cp = _rcopy(srcs[nb + n], small_dst(n, me), ssem.at[3 * n + k], rsem.at[3 * n + k], (*chip, c))
                cp.start()
                sends.append(cp)
        big = (srcs[:nb], dsts[:nb], ssem, rsem)
        for stage in ("start", "relay", "relay_far", "finish"):
            for piece in pieces:
                getattr(piece, stage)(*big)
        for n in range(3):
            for k, chip in enumerate(chips):
                ref = small_dst(n, 2 * chip[0] + chip[1])
                _rcopy(ref, ref, ssem.at[3 * n + k], rsem.at[3 * n + k], (*chip, c)).wait_recv()
        for cp in sends:
            cp.wait_send()
        for cp in local:
            cp.wait()

    nsem = 9 + sum(piece.nsems for piece in pieces)
    out_shape = [jax.ShapeDtypeStruct(b.shape, b.dtype) for b in bigs]
    out_shape += [jax.ShapeDtypeStruct((pool_w.shape[0], N_CHIPS * q, pool_w.shape[2]), pool_w.dtype),
                  jax.ShapeDtypeStruct((pack_w.shape[0], N_CHIPS * cw), pack_w.dtype),
                  jax.ShapeDtypeStruct((pack_d.shape[0], N_CHIPS * cd), pack_d.dtype)]
    return pl.pallas_call(
        body, name=name, in_specs=[ANY] * (nb + 3), out_specs=[ANY] * (nb + 3), out_shape=out_shape,
        input_output_aliases={a: a for a in range(nb)},
        scratch_shapes=[pltpu.SemaphoreType.DMA((nsem,)), pltpu.SemaphoreType.DMA((nsem,)), pltpu.SemaphoreType.DMA((3,))],
        compiler_params=pltpu.CompilerParams(has_side_effects=True))(*bigs, *smalls)


def _swap_with_sibling(grads, wholes, name):
    n, nw = len(grads), len(wholes)
    halves = [g.shape[1] // 2 for g in grads]

    def body(*refs):
        srcs, dsts = refs[:n + nw], refs[n + nw:2 * (n + nw)]
        ssem, rsem = refs[2 * (n + nw):]
        x, y, c, me, chips, sib = _place()
        cps = [_rcopy(srcs[a].at[:, pl.ds((1 - c) * halves[a], halves[a]), :], dsts[a], ssem.at[a], rsem.at[a], sib)
               for a in range(n)]
        cps += [_rcopy(srcs[a], dsts[a], ssem.at[a], rsem.at[a], sib) for a in range(n, n + nw)]
        for cp in cps:
            cp.start()
        for cp in cps:
            cp.wait_recv()
        for cp in cps:
            cp.wait_send()

    out_shape = [jax.ShapeDtypeStruct((g.shape[0], h, g.shape[2]), g.dtype) for g, h in zip(grads, halves)]
    out_shape += [jax.ShapeDtypeStruct(w.shape, w.dtype) for w in wholes]
    return pl.pallas_call(
        body, name=name, in_specs=[ANY] * (n + nw), out_specs=[ANY] * (n + nw), out_shape=out_shape,
        scratch_shapes=[pltpu.SemaphoreType.DMA((n + nw,)), pltpu.SemaphoreType.DMA((n + nw,))],
        compiler_params=pltpu.CompilerParams(has_side_effects=True))(*grads, *wholes)


def _scatter_to_chips(halves_in, small, name):
    n = len(halves_in)
    rh = small.shape[0] // 2

    def body(*refs):
        srcs, dsts = refs[:n + 1], refs[n + 1:2 * (n + 1)]
        ssem, rsem = refs[2 * (n + 1):]
        x, y, c, me, chips, sib = _place()
        cps = []
        for a in range(n + 1):
            for k, chip in enumerate(chips):
                src = srcs[a].at[2 * chip[0] + chip[1]] if a < n else srcs[a].at[pl.ds(c * rh, rh)]
                cps.append(_rcopy(src, dsts[a].at[k], ssem.at[3 * a + k], rsem.at[3 * a + k], (*chip, c)))
        for cp in cps:
            cp.start()
        for cp in cps:
            cp.wait_recv()
        for cp in cps:
            cp.wait_send()

    out_shape = [jax.ShapeDtypeStruct((3,) + h.shape[1:], h.dtype) for h in halves_in]
    out_shape.append(jax.ShapeDtypeStruct((3, rh, small.shape[1]), small.dtype))
    return pl.pallas_call(
        body, name=name, in_specs=[ANY] * (n + 1), out_specs=[ANY] * (n + 1), out_shape=out_shape,
        scratch_shapes=[pltpu.SemaphoreType.DMA((3 * (n + 1),)), pltpu.SemaphoreType.DMA((3 * (n + 1),))],
        compiler_params=pltpu.CompilerParams(has_side_effects=True))(*halves_in, small)


def _join_halves(parts, name):
    n = len(parts)

    def body(*refs):
        srcs, dsts = refs[:n], refs[n:2 * n]
        ssem, rsem = refs[2 * n:]
        x, y, c, me, chips, sib = _place()
        cps = []
        for a in range(n):
            h = srcs[a].shape[0] // 2
            cps.append(_rcopy(srcs[a].at[pl.ds(c * h, h)], dsts[a].at[pl.ds(c * h, h)], ssem.at[a], rsem.at[a], sib))
        for cp in cps:
            cp.start()
        for a in range(n):
            h = srcs[a].shape[0] // 2
            theirs = dsts[a].at[pl.ds((1 - c) * h, h)]
            _rcopy(theirs, theirs, ssem.at[a], rsem.at[a], sib).wait_recv()
        for cp in cps:
            cp.wait_send()

    out_shape = [jax.ShapeDtypeStruct(p.shape, p.dtype) for p in parts]
    return pl.pallas_call(
        body, name=name, in_specs=[ANY] * n, out_specs=[ANY] * n, out_shape=out_shape,
        input_output_aliases={a: a for a in range(n)},
        scratch_shapes=[pltpu.SemaphoreType.DMA((n,)), pltpu.SemaphoreType.DMA((n,))],
        compiler_params=pltpu.CompilerParams(has_side_effects=True))(*parts)


def _scatter_start(h, name):
    land = (3,) + h.shape[1:]

    def body(h_ref, land_ref, send_sems, recv_sems, h_thru, land_thru, token):
        x, y, c, me, chips, sib = _place()
        for k, chip in enumerate(chips):
            _rcopy(h_ref.at[2 * chip[0] + chip[1]], land_ref.at[k], send_sems.at[k], recv_sems.at[k], (*chip, c)).start()
        token[...] = jnp.zeros_like(token)

    hbm = pl.BlockSpec(memory_space=pltpu.HBM)
    sem = pl.BlockSpec(memory_space=pltpu.SEMAPHORE)
    return pl.pallas_call(
        body, name=name,
        out_shape=(pltpu.SemaphoreType.DMA((3,)), pltpu.SemaphoreType.DMA((3,)), pltpu.HBM(h.shape, h.dtype),
                   pltpu.HBM(land, h.dtype), jax.ShapeDtypeStruct((8, LANES), F32)),
        in_specs=(hbm, hbm), out_specs=(sem, sem, hbm, hbm, pl.BlockSpec(memory_space=pltpu.VMEM)),
        input_output_aliases={0: 2, 1: 3},
        compiler_params=pltpu.CompilerParams(has_side_effects=pltpu.SideEffectType.DATAFLOW_SIDE_EFFECTING))(
            pltpu.with_memory_space_constraint(h, pltpu.HBM),
            pltpu.with_memory_space_constraint(lax.empty(land, h.dtype), pltpu.HBM))


def _scatter_wait(send_sems, recv_sems, h_thru, land_thru, after, name):
    def body(h_ref, land_ref, send_sems, recv_sems, after_ref, h_dead, got_ref):
        x, y, c, me, chips, sib = _place()
        for k, chip in enumerate(chips):
            cp = _rcopy(h_ref.at[2 * chip[0] + chip[1]], land_ref.at[k], send_sems.at[k], recv_sems.at[k], (*chip, c))
            cp.wait_send()
            cp.wait_recv()

    hbm = pl.BlockSpec(memory_space=pltpu.HBM)
    sem = pl.BlockSpec(memory_space=pltpu.SEMAPHORE)
    return pl.pallas_call(
        body, name=name,
        out_shape=(pltpu.HBM(h_thru.shape, h_thru.dtype), pltpu.HBM(land_thru.shape, land_thru.dtype)),
        in_specs=(hbm, hbm, sem, sem, ANY), out_specs=(hbm, hbm), input_output_aliases={0: 0, 1: 1},
        compiler_params=pltpu.CompilerParams(has_side_effects=pltpu.SideEffectType.DATAFLOW_SIDE_EFFECTING))(
            h_thru, land_thru, send_sems, recv_sems, after)


def _share_half_start(small, name):
    rh = small.shape[0] // 2
    land = (3, rh, small.shape[1])

    def body(s_ref, land_ref, send_sems, recv_sems, s_thru, land_thru, token):
        x, y, c, me, chips, sib = _place()
        for k, chip in enumerate(chips):
            _rcopy(s_ref.at[pl.ds(c * rh, rh)], land_ref.at[k], send_sems.at[k], recv_sems.at[k], (*chip, c)).start()
        token[...] = jnp.zeros_like(token)

    hbm = pl.BlockSpec(memory_space=pltpu.HBM)
    sem = pl.BlockSpec(memory_space=pltpu.SEMAPHORE)
    return pl.pallas_call(
        body, name=name,
        out_shape=(pltpu.SemaphoreType.DMA((3,)), pltpu.SemaphoreType.DMA((3,)), pltpu.HBM(small.shape, small.dtype),
                   pltpu.HBM(land, small.dtype), jax.ShapeDtypeStruct((8, LANES), F32)),
        in_specs=(hbm, hbm), out_specs=(sem, sem, hbm, hbm, pl.BlockSpec(memory_space=pltpu.VMEM)),
        input_output_aliases={0: 2, 1: 3},
        compiler_params=pltpu.CompilerParams(has_side_effects=pltpu.SideEffectType.DATAFLOW_SIDE_EFFECTING))(
            pltpu.with_memory_space_constraint(small, pltpu.HBM),
            pltpu.with_memory_space_constraint(lax.empty(land, small.dtype), pltpu.HBM))


def _share_half_wait(send_sems, recv_sems, s_thru, land_thru, after, name):
    rh = s_thru.shape[0] // 2

    def body(s_ref, land_ref, send_sems, recv_sems, after_ref, s_dead, got_ref):
        x, y, c, me, chips, sib = _place()
        for k, chip in enumerate(chips):
            cp = _rcopy(s_ref.at[pl.ds(c * rh, rh)], land_ref.at[k], send_sems.at[k], recv_sems.at[k], (*chip, c))
            cp.wait_send()
            cp.wait_recv()

    hbm = pl.BlockSpec(memory_space=pltpu.HBM)
    sem = pl.BlockSpec(memory_space=pltpu.SEMAPHORE)
    return pl.pallas_call(
        body, name=name,
        out_shape=(pltpu.HBM(s_thru.shape, s_thru.dtype), pltpu.HBM(land_thru.shape, land_thru.dtype)),
        in_specs=(hbm, hbm, sem, sem, ANY), out_specs=(hbm, hbm), input_output_aliases={0: 0, 1: 1},
        compiler_params=pltpu.CompilerParams(has_side_effects=pltpu.SideEffectType.DATAFLOW_SIDE_EFFECTING))(
            s_thru, land_thru, send_sems, recv_sems, after)


def _join_start(parts, name):
    n = len(parts)

    def body(*refs):
        srcs, (send_sems, recv_sems), token = refs[:n], refs[n:n + 2], refs[-1]
        x, y, c, me, chips, sib = _place()
        for a, src in enumerate(srcs):
            h = src.shape[0] // 2
            mine = src.at[pl.ds(c * h, h)]
            _rcopy(mine, mine, send_sems.at[a], recv_sems.at[a], sib).start()
        token[...] = jnp.zeros_like(token)

    hbm = pl.BlockSpec(memory_space=pltpu.HBM)
    sem = pl.BlockSpec(memory_space=pltpu.SEMAPHORE)
    outs = pl.pallas_call(
        body, name=name,
        out_shape=(pltpu.SemaphoreType.DMA((n,)), pltpu.SemaphoreType.DMA((n,)))
        + tuple(pltpu.HBM(p.shape, p.dtype) for p in parts) + (jax.ShapeDtypeStruct((8, LANES), F32),),
        in_specs=(hbm,) * n, out_specs=(sem, sem) + (hbm,) * n + (pl.BlockSpec(memory_space=pltpu.VMEM),),
        input_output_aliases={a: 2 + a for a in range(n)},
        compiler_params=pltpu.CompilerParams(has_side_effects=pltpu.SideEffectType.DATAFLOW_SIDE_EFFECTING))(
            *[pltpu.with_memory_space_constraint(p, pltpu.HBM) for p in parts])
    return outs[0], outs[1], list(outs[2:2 + n]), outs[-1]


def _join_wait(send_sems, recv_sems, parts, after, name):
    n = len(parts)

    def body(*refs):
        srcs, (send_sems, recv_sems) = refs[:n], refs[n:n + 2]
        x, y, c, me, chips, sib = _place()
        for a, src in enumerate(srcs):
            h = src.shape[0] // 2
            mine, theirs = src.at[pl.ds(c * h, h)], src.at[pl.ds((1 - c) * h, h)]
            _rcopy(mine, theirs, send_sems.at[a], recv_sems.at[a], sib).wait_send()
            _rcopy(theirs, theirs, send_sems.at[a], recv_sems.at[a], sib).wait_recv()

    hbm = pl.BlockSpec(memory_space=pltpu.HBM)
    sem = pl.BlockSpec(memory_space=pltpu.SEMAPHORE)
    return pl.pallas_call(
        body, name=name, out_shape=tuple(pltpu.HBM(p.shape, p.dtype) for p in parts),
        in_specs=(hbm,) * n + (sem, sem, ANY), out_specs=(hbm,) * n, input_output_aliases={a: a for a in range(n)},
        compiler_params=pltpu.CompilerParams(has_side_effects=pltpu.SideEffectType.DATAFLOW_SIDE_EFFECTING))(
            *parts, send_sems, recv_sems, after)


def _pad_rows(a, rows):
    return jnp.pad(a, ((0, rows - a.shape[0]), (0, 0)))


def _stack_rows(parts, multiple):
    padded = [_pad_rows(p, -(-p.shape[0] // 8) * 8) for p in parts]
    starts, at = [], 0
    for p in padded:
        starts.append(at)
        at += p.shape[0]
    total = -(-at // multiple) * multiple
    if total > at:
        padded.append(jnp.zeros((total - at, parts[0].shape[1]), parts[0].dtype))
    return jnp.concatenate(padded, axis=0), starts


def kernel(x, ln_pre_even, w_in_even, pool_w, pool_scale, w_out_even, ln_post_even, ln_pre_odd, w_in_odd, sconv_w, dconv_w, dconv_b, cnorm_g, cnorm_b, w_out_odd, ln_post_odd, loss_target, m_ln_pre_even, m_w_in_even, m_pool_w, m_pool_scale, m_w_out_even, m_ln_post_even, m_ln_pre_odd, m_w_in_odd, m_sconv_w, m_dconv_w, m_dconv_b, m_cnorm_g, m_cnorm_b, m_w_out_odd, m_ln_post_odd, v_ln_pre_even, v_w_in_even, v_pool_w, v_pool_scale, v_w_out_even, v_ln_post_even, v_ln_pre_odd, v_w_in_odd, v_sconv_w, v_dconv_w, v_dconv_b, v_cnorm_g, v_cnorm_b, v_w_out_odd, v_ln_post_odd):
    _, s, d = x.shape
    half = d // 2
    cw = half // N_CHIPS
    ng, q, gd = pool_w.shape[1:]
    k3, k31 = sconv_w.shape[1], dconv_w.shape[1]
    x2d, tgt = x[0], loss_target[0]
    me = 2 * lax.axis_index("x") + lax.axis_index("y")
    core = lax.axis_index("c")
    c_arr = jnp.reshape(core, (1,)).astype(jnp.int32)
    me_arr = jnp.reshape(me, (1,)).astype(jnp.int32)
    mc_arr = jnp.stack([me, core]).astype(jnp.int32)

    shards = [w_in_even[0], w_out_even[0], w_in_odd[0], w_out_odd[0]]
    slabs = [_cast_bf16_own_slab(w, me_arr, f"cast_w{n}") for n, w in enumerate(shards)]
    pool_w_b = _cast_bf16(pool_w[0].reshape(ng * q, gd), "cast_pool_w").reshape(ng, q, gd)
    pack_w, at_w = _stack_rows([sconv_w[0], dconv_w[0], dconv_b, cnorm_g, cnorm_b], 8)
    pack_d, at_d = _stack_rows([ln_pre_odd, ln_post_odd], 8)
    win_e, pool_w_f, pack_w_f, pack_d_f = _gather_weights(slabs[:1], pool_w_b, pack_w, pack_d, "gather_first")
    sconv_f = pack_w_f[at_w[0]:at_w[0] + k3]
    dconv_f = pack_w_f[at_w[1]:at_w[1] + k31]
    dconv_b_f, cnorm_g_f, cnorm_b_f = (pack_w_f[at_w[n]:at_w[n] + 1] for n in (2, 3, 4))
    ln_pre_odd_f = pack_d_f[at_d[0]:at_d[0] + 1]
    ln_post_odd_f = pack_d_f[at_d[1]:at_d[1] + 1]

    def reduce_half(g, name):
        (got,) = _swap_with_sibling([g], [], "swap_" + name)
        return _half_add(g, got, c_arr, "half_add_" + name)

    h0 = _rms_fwd(x2d, ln_pre_even, "rms_pre_even")
    plans = _Multi([_GatherPlan([slabs[1]], at=(0.6, 0.88)), _GatherPlan([slabs[2]], (0, 1, 4), at=(0.6, 0.88))])
    p_e, extra = _mm_nn(h0, win_e, "proj_in_even", plans)
    (wout_e,), (win_o,) = plans.results(extra)
    wout_e = wout_e.reshape(d, d)
    att, ltot, (win_o,) = _sba_fwd(p_e, half, "sba_fwd", _GatherPlan([win_o], (1, 4, 4), at=(0.69, 0.94)))
    y_e = _even_mix_fwd(p_e, att, pool_w_f, pool_scale, d, "even_mix_fwd")
    o_e, x1, h1 = _mm_out_even(y_e, wout_e, x2d, ln_post_even, ln_pre_odd_f, "proj_out_even")
    p_o, (wout_o,) = _mm_nn(h1, win_o, "proj_in_odd", _GatherPlan([slabs[3]]))
    wout_o = wout_o.reshape(d, d)
    y_o, s3, d1 = _odd_mix_fwd(p_o, sconv_f, dconv_f, dconv_b_f, cnorm_g_f, cnorm_b_f, d, "odd_mix_fwd")
    do_o, dx2, loss_blk, dln_post_odd = _mm_out_odd(y_o, wout_o, x1, ln_post_odd_f, tgt, "proj_out_odd_loss")

    dy_o = _mm_nt(do_o, wout_o, "dy_odd")
    g_wout_o = _mm_tn(y_o, do_o, 1, "dw_out_odd")[0].reshape(N_CHIPS, d // N_CHIPS, d)
    (dbc, dgate_o, ds3, dd1, dcnorm_g, dcnorm_b, ddconv_b), (got,) = _odd_bwd_rows(
        p_o, s3, d1, dy_o, cnorm_g_f, cnorm_b_f, d, "odd_bwd_rows", _SwapPlan([g_wout_o]))
    h_wout_o = _half_add(g_wout_o, got, c_arr, "half_add_out_odd")
    dhc, dcc, dga, dgb, dsconv, ddconv = _odd_bwd_conv(p_o, ds3, dd1, sconv_f, dconv_f, d, "odd_bwd_conv")
    dp_o = jnp.concatenate([dhc, dbc, dcc, dga, dgb, dgate_o], axis=1)
    g_win_o, (s_wout_o,) = _mm_tn(h1, dp_o, N_CHIPS, "dw_in_odd", _ScatterPlan([h_wout_o]))
    (dx1, dln_pre_odd, do_e, dln_post_even), (got,) = _mm_in_bwd(
        dp_o, win_o, x1, ln_pre_odd_f, dx2, (o_e, ln_post_even), "dx_odd", _SwapPlan([g_win_o]))
    h_win_o = _half_add(g_win_o, got, c_arr, "half_add_in_odd")

    dy_e = _mm_nt(do_e, wout_e, "dy_even")
    g_wout_e = _mm_tn(y_e, do_e, 1, "dw_out_even")[0].reshape(N_CHIPS, d // N_CHIPS, d)
    (datt, du, dgate_e, dpool_scale, dpool_w), (got,) = _even_mix_bwd(
        p_e, att, dy_e, pool_w_f, pool_scale, d, "even_mix_bwd", _SwapPlan([g_wout_e]))
    h_wout_e = _half_add(g_wout_e, got, c_arr, "half_add_out_even")
    two = lambda v: v.reshape(2, half)
    small_parts = [dpool_scale, two(dln_post_even), two(dln_pre_odd), two(dln_post_odd),
                   dsconv, ddconv, ddconv_b, dcnorm_g, dcnorm_b, dpool_w.reshape(gd, half)]
    small, at_s = _stack_rows(small_parts, 16)
    (small1,) = _swap_with_sibling([], [small], "swap_small")
    small2 = _add2(small, small1, "small_add")
    plans = _Multi([_ScatterPlan([h_win_o]), _ShareHalfPlan([small2])])
    dq, dk, dv, extra = _sba_bwd(p_e, ltot, datt, half, "sba_bwd", plans)
    (s_win_o,), (small_got,) = plans.results(extra)
    dp_e = jnp.concatenate([dq, dk, dv, du, dgate_e], axis=1)
    g_win_e, (s_wout_e,) = _mm_tn(h0, dp_e, N_CHIPS, "dw_in_even", _ScatterPlan([h_wout_e]))
    h_win_e = reduce_half(g_win_e, "in_even")
    send_sems, recv_sems, h_win_e, landing, token = _scatter_start(h_win_e, "scatter_in_even_start")
    (grad_x, dln_pre_even), _ = _mm_in_bwd(dp_e, win_e, x2d, ln_pre_even + token[0:1, 0:1], dx1, None, "dx_even")

    last, at_l = _stack_rows([two(dln_pre_even), jnp.pad(loss_blk[0:1], ((0, 0), (0, half - LANES)))], 16)
    (last1,) = _swap_with_sibling([], [last], "swap_last")
    last2 = _add2(last, last1, "last_add")
    share = _share_half_start(last2, "share_last_start")
    pairs = [(h_wout_e, s_wout_e), (h_win_o, s_win_o), (h_wout_o, s_wout_o)]
    parts = [_sum_chips(h, r, mc_arr, f"sum_chips{n + 1}") for n, (h, r) in enumerate(pairs)]
    parts.append(_sum_chips_ordered(small2, small_got, mc_arr, "small_sum"))
    last2, last_got = _share_half_wait(*share[:4], parts[-1], "share_last_wait")
    parts.append(_sum_chips_ordered(last2, last_got, mc_arr, "last_sum"))
    join_sems = _join_start(parts, "join_first_start")
    h_win_e, s_win_e = _scatter_wait(send_sems, recv_sems, h_win_e, landing, join_sems[3], "scatter_in_even_wait")
    last_part = _sum_chips(h_win_e, s_win_e, mc_arr, "sum_chips0")
    last_sems = _join_start([last_part], "join_last_start")
    gw_out_e, gw_in_o, gw_out_o, red, red_last = _join_wait(*join_sems[:3], last_sems[3], "join_first_wait")
    loss = red_last[at_l[1], 0]

    def rows(n, cnt):
        return red[at_s[n]:at_s[n] + cnt]

    def mine(a, width):
        return lax.dynamic_slice_in_dim(a, me * width, width, axis=1)

    quarter = d // N_CHIPS
    g_small = {
        "ln_pre_even": red_last[at_l[0]:at_l[0] + 2].reshape(1, d),
        "pool_scale": rows(0, 1),
        "ln_post_even": rows(1, 2).reshape(1, d),
        "ln_pre_odd": mine(rows(2, 2).reshape(1, d), quarter),
        "ln_post_odd": mine(rows(3, 2).reshape(1, d), quarter),
        "sconv_w": mine(rows(4, k3), cw),
        "dconv_w": mine(rows(5, k31), cw),
        "dconv_b": mine(rows(6, 1), cw),
        "cnorm_g": mine(rows(7, 1), cw),
        "cnorm_b": mine(rows(8, 1), cw),
        "pool_w": lax.dynamic_slice_in_dim(rows(9, gd).reshape(ng, gd, gd), me * q, q, axis=1).reshape(ng * q, gd),
    }
    w2d = {
        "ln_pre_even": ln_pre_even, "w_in_even": w_in_even[0], "pool_w": pool_w[0].reshape(ng * q, gd),
        "pool_scale": pool_scale, "w_out_even": w_out_even[0], "ln_post_even": ln_post_even, "ln_pre_odd": ln_pre_odd,
        "w_in_odd": w_in_odd[0], "sconv_w": sconv_w[0], "dconv_w": dconv_w[0], "dconv_b": dconv_b, "cnorm_g": cnorm_g,
        "cnorm_b": cnorm_b, "w_out_odd": w_out_odd[0], "ln_post_odd": ln_post_odd,
    }
    moments = {
        "ln_pre_even": (m_ln_pre_even, v_ln_pre_even), "w_in_even": (m_w_in_even, v_w_in_even),
        "pool_w": (m_pool_w, v_pool_w), "pool_scale": (m_pool_scale, v_pool_scale),
        "w_out_even": (m_w_out_even, v_w_out_even), "ln_post_even": (m_ln_post_even, v_ln_post_even),
        "ln_pre_odd": (m_ln_pre_odd, v_ln_pre_odd), "w_in_odd": (m_w_in_odd, v_w_in_odd),
        "sconv_w": (m_sconv_w, v_sconv_w), "dconv_w": (m_dconv_w, v_dconv_w), "dconv_b": (m_dconv_b, v_dconv_b),
        "cnorm_g": (m_cnorm_g, v_cnorm_g), "cnorm_b": (m_cnorm_b, v_cnorm_b),
        "w_out_odd": (m_w_out_odd, v_w_out_odd), "ln_post_odd": (m_ln_post_odd, v_ln_post_odd),
    }
    def update(name, g):
        m_in, v_in = moments[name]
        w = w2d[name]
        return _adamw(w, g, m_in.reshape(w.shape), v_in.reshape(w.shape), "adamw_" + name)[0]

    updates = {"w_in_odd": update("w_in_odd", gw_in_o)}
    (gw_in_e,) = _join_wait(*last_sems[:3], updates["w_in_odd"][1], "join_last_wait")
    for name, g in dict(g_small, w_in_even=gw_in_e, w_out_even=gw_out_e, w_out_odd=gw_out_o).items():
        updates[name] = update(name, g)
    outs = [[u.reshape(moments[name][0].shape) for u in updates[name]] for name in w2d]
    grads_out, deltas, new_m, new_v = zip(*outs)
    return (loss, grad_x.reshape(x.shape), *grads_out, *deltas, *new_m, *new_v)
```

```python
import functools
import math

import jax
import jax.numpy as jnp
from jax import lax
from jax.experimental import pallas as pl
from jax.experimental.pallas import tpu as pltpu

F32 = jnp.float32
BF16 = jnp.bfloat16
EPS = 1e-6
N_CHIPS = 4
VMEM_LIMIT_V7X = 56 << 20
HEAD_DIM = 128
ATT_BLOCK = 256
POOL_WINDOWS = (2, 4, 8, 16)
ROW_TILE = 256
POOL_HALO = 16
CONV_HALO = 32
LANES = 128
ADAM_LR, ADAM_B1, ADAM_B2, ADAM_EPS, ADAM_WD, ADAM_STEP = 0.001, 0.9, 0.999, 1e-08, 0.01, 10
MESH_ID = pl.DeviceIdType.MESH
ANY = pl.BlockSpec(memory_space=pl.ANY)


def _cp(*sem):
    return pltpu.CompilerParams(dimension_semantics=sem or None, vmem_limit_bytes=VMEM_LIMIT_V7X)


def _pick_tile(n, cap):
    best = None
    for t in range(LANES, min(n, cap) + 1, LANES):
        if n % t == 0:
            best = t
    assert best is not None, (n, cap)
    return best


def _sigmoid(x):
    return 1.0 / (1.0 + jnp.exp(-x))


def _silu(x):
    return x * _sigmoid(x)


def _dsilu(x):
    s = _sigmoid(x)
    return s * (1.0 + x * (1.0 - s))


def _log_sigmoid(z):
    return jnp.minimum(z, 0.0) - jnp.log(1.0 + jnp.exp(-jnp.abs(z)))


def _rms_stats(x):
    r = lax.rsqrt(jnp.mean(x * x, axis=-1, keepdims=True) + EPS)
    return x * r, r


def _rms_bwd(dh, xhat, r, g):
    dxh = dh * g
    dx = r * (dxh - xhat * jnp.mean(dxh * xhat, axis=-1, keepdims=True))
    return dx, jnp.sum(dh * xhat, axis=0, keepdims=True)


def _acc_rows(ref, first, val):
    @pl.when(first)
    def _():
        ref[...] = val

    @pl.when(jnp.logical_not(first))
    def _():
        ref[...] += val


def _rcopy(src, dst, ssem, rsem, dev):
    return pltpu.make_async_remote_copy(src_ref=src, dst_ref=dst, send_sem=ssem, recv_sem=rsem,
                                        device_id=dev, device_id_type=MESH_ID)


def _place():
    x, y, c = lax.axis_index("x"), lax.axis_index("y"), lax.axis_index("c")
    chips = [(1 - x, y), (x, 1 - y), (1 - x, 1 - y)]
    return x, y, c, 2 * x + y, chips, (x, y, 1 - c)


class _GatherPlan:
    PER_ARRAY = 7

    def __init__(self, arrays, part=(0, 1, 1), at=(0.5, 0.8)):
        self.operands = list(arrays)
        self.out_shapes = [jax.ShapeDtypeStruct(a.shape, a.dtype) for a in arrays]
        self.aliases = {i: i for i in range(len(arrays))}
        self.nsems = self.PER_ARRAY * len(arrays)
        self.base = 0
        self.halves = [a.shape[1] // 2 for a in arrays]
        self.part = part
        self.at = at

    def schedule(self):
        return [(0.0, self.start), (self.at[0], self.relay), (self.at[1], self.relay_far)]

    def _rows(self, ref, a, chip, half, quarter=None):
        lo, hi, n = self.part
        h = self.halves[a]
        first, size = half * h + lo * h // n, (hi - lo) * h // n
        if quarter is not None:
            first, size = first + quarter * (size // 2), size // 2
        return ref.at[chip, pl.ds(first, size)]

    def _copy(self, src, dst, a, n, ssem, rsem, dev):
        return _rcopy(src, dst, ssem.at[self.base + self.PER_ARRAY * a + n], rsem.at[self.base + self.PER_ARRAY * a + n], dev)

    def _own(self, ins, outs, ssem, rsem):
        x, y, c, me, chips, sib = _place()
        return [self._copy(self._rows(ins[a], a, me, c), self._rows(outs[a], a, me, c), a, k, ssem, rsem, (*chips[k], c))
                for a in range(len(ins)) for k in (0, 1)]

    def _relays(self, outs, ssem, rsem, a, k):
        x, y, c, me, chips, sib = _place()
        chip = 2 * chips[k][0] + chips[k][1]
        whole, quarter = self._rows(outs[a], a, chip, c), self._rows(outs[a], a, chip, c, k)
        return (self._copy(whole, whole, a, k, ssem, rsem, (*chips[k], c)),
                self._copy(quarter, quarter, a, 2 + k, ssem, rsem, (*chips[1 - k], c)),
                self._copy(whole, whole, a, 4 + k, ssem, rsem, sib))

    def _far(self, outs, ssem, rsem, a):
        x, y, c, me, chips, sib = _place()
        chip = 2 * chips[2][0] + chips[2][1]
        whole = self._rows(outs[a], a, chip, c)
        got = [self._copy(q, q, a, 2 + k, ssem, rsem, (*chips[1 - k], c))
               for k, q in enumerate([self._rows(outs[a], a, chip, c, 0), self._rows(outs[a], a, chip, c, 1)])]
        return got, self._copy(whole, whole, a, 6, ssem, rsem, sib)

    def start(self, ins, outs, ssem, rsem):
        for cp in self._own(ins, outs, ssem, rsem):
            cp.start()

    def relay(self, ins, outs, ssem, rsem):
        for a in range(len(outs)):
            for k in (0, 1):
                landed, onward, to_sibling = self._relays(outs, ssem, rsem, a, k)
                landed.wait_recv()
                onward.start()
                to_sibling.start()

    def relay_far(self, ins, outs, ssem, rsem):
        for a in range(len(outs)):
            got, to_sibling = self._far(outs, ssem, rsem, a)
            for cp in got:
                cp.wait_recv()
            to_sibling.start()

    def finish(self, ins, outs, ssem, rsem):
        x, y, c, me, chips, sib = _place()
        for a in range(len(outs)):
            for k in range(3):
                ref = self._rows(outs[a], a, 2 * chips[k][0] + chips[k][1], 1 - c)
                self._copy(ref, ref, a, 4 + k, ssem, rsem, sib).wait_recv()
        for cp in self._own(ins, outs, ssem, rsem):
            cp.wait_send()
        for a in range(len(outs)):
            for k in (0, 1):
                _, onward, to_sibling = self._relays(outs, ssem, rsem, a, k)
                onward.wait_send()
                to_sibling.wait_send()
            self._far(outs, ssem, rsem, a)[1].wait_send()


class _ScatterPlan:
    def __init__(self, arrays, part=(0, 1, 1), into=None):
        self.n = len(arrays)
        self.operands = list(arrays) + list(into or [])
        self.out_shapes = [jax.ShapeDtypeStruct((3,) + a.shape[1:], a.dtype) for a in arrays]
        self.aliases = {self.n + i: i for i in range(self.n)} if into else {}
        self.nsems = 3 * self.n
        self.base = 0
        self.part = part

    def _copies(self, ins, outs, ssem, rsem):
        x, y, c, me, chips, sib = _place()
        lo, hi, n = self.part
        out = []
        for a in range(self.n):
            h = ins[a].shape[1]
            rows = pl.ds(lo * h // n, (hi - lo) * h // n)
            for k, chip in enumerate(chips):
                out.append(_rcopy(ins[a].at[2 * chip[0] + chip[1], rows], outs[a].at[k, rows],
                                  ssem.at[self.base + 3 * a + k], rsem.at[self.base + 3 * a + k], (*chip, c)))
        return out

    def schedule(self):
        return [(0.0, self.start)]

    def start(self, ins, outs, ssem, rsem):
        for cp in self._copies(ins, outs, ssem, rsem):
            cp.start()

    def finish(self, ins, outs, ssem, rsem):
        cps = self._copies(ins, outs, ssem, rsem)
        for cp in cps:
            cp.wait_recv()
        for cp in cps:
            cp.wait_send()


class _ShareHalfPlan(_ScatterPlan):
    def __init__(self, arrays):
        super().__init__(arrays)
        self.out_shapes = [jax.ShapeDtypeStruct((3, a.shape[0] // 2, a.shape[1]), a.dtype) for a in arrays]

    def _copies(self, ins, outs, ssem, rsem):
        x, y, c, me, chips, sib = _place()
        out = []
        for a in range(self.n):
            rh = ins[a].shape[0] // 2
            for k, chip in enumerate(chips):
                out.append(_rcopy(ins[a].at[pl.ds(c * rh, rh)], outs[a].at[k],
                                  ssem.at[self.base + 3 * a + k], rsem.at[self.base + 3 * a + k], (*chip, c)))
        return out


class _SwapPlan:
    def __init__(self, grads):
        self.operands = list(grads)
        self.out_shapes = [jax.ShapeDtypeStruct((g.shape[0], g.shape[1] // 2, g.shape[2]), g.dtype) for g in grads]
        self.aliases = {}
        self.nsems = len(grads)
        self.base = 0

    def _copies(self, ins, outs, ssem, rsem):
        x, y, c, me, chips, sib = _place()
        out = []
        for a, src in enumerate(ins):
            h = src.shape[1] // 2
            out.append(_rcopy(src.at[:, pl.ds((1 - c) * h, h), :], outs[a], ssem.at[self.base + a], rsem.at[self.base + a], sib))
        return out

    def schedule(self):
        return [(0.0, self.start)]

    def start(self, ins, outs, ssem, rsem):
        for cp in self._copies(ins, outs, ssem, rsem):
            cp.start()

    def finish(self, ins, outs, ssem, rsem):
        cps = self._copies(ins, outs, ssem, rsem)
        for cp in cps:
            cp.wait_recv()
        for cp in cps:
            cp.wait_send()


class _SendWholePlan(_SwapPlan):
    def __init__(self, arrays):
        self.operands = list(arrays)
        self.out_shapes = [jax.ShapeDtypeStruct(a.shape, a.dtype) for a in arrays]
        self.aliases = {}
        self.nsems = len(arrays)
        self.base = 0

    def _copies(self, ins, outs, ssem, rsem):
        x, y, c, me, chips, sib = _place()
        return [_rcopy(src, outs[a], ssem.at[self.base + a], rsem.at[self.base + a], sib) for a, src in enumerate(ins)]


class _Multi:
    def __init__(self, plans):
        self.plans = plans
        self.operands, self.out_shapes, self.aliases, self.nsems = [], [], {}, 0
        self.spans = []
        for p in plans:
            ni, no = len(self.operands), len(self.out_shapes)
            self.spans.append((ni, ni + len(p.operands), no, no + len(p.out_shapes)))
            self.aliases.update({ni + i: no + j for i, j in p.aliases.items()})
            p.base = self.nsems
            self.nsems += p.nsems
            self.operands += p.operands
            self.out_shapes += p.out_shapes

    def schedule(self):
        def bound(fn, span):
            i0, i1, o0, o1 = span
            return lambda ins, outs, ssem, rsem: fn(ins[i0:i1], outs[o0:o1], ssem, rsem)

        stages = [(at, bound(fn, span)) for p, span in zip(self.plans, self.spans) for at, fn in p.schedule()]
        return sorted(stages, key=lambda s: s[0])

    def finish(self, ins, outs, ssem, rsem):
        for p, (i0, i1, o0, o1) in zip(self.plans, self.spans):
            p.finish(ins[i0:i1], outs[o0:o1], ssem, rsem)

    def results(self, extra):
        return [list(extra[o0:o1]) for (_, _, o0, o1) in self.spans]


class _Host:
    def __init__(self, comm, in_specs, out_specs, out_shape, scratch):
        self.comm = comm
        self.n_in, self.n_out = len(in_specs), len(out_specs)
        self.in_specs, self.out_specs, self.out_shape, self.scratch = list(in_specs), list(out_specs), list(out_shape), list(scratch)
        self.aliases = {}
        self.args = []
        if comm is not None:
            self.in_specs += [ANY] * len(comm.operands)
            self.out_specs += [ANY] * len(comm.out_shapes)
            self.out_shape += comm.out_shapes
            self.scratch += [pltpu.SemaphoreType.DMA((comm.nsems,)), pltpu.SemaphoreType.DMA((comm.nsems,))]
            self.aliases = {self.n_in + i: self.n_out + j for i, j in comm.aliases.items()}
            self.args = list(comm.operands)

    def split(self, refs):
        nc = len(self.args)
        nco = len(self.out_shape) - self.n_out
        ins, p = refs[:self.n_in], self.n_in + nc
        outs, rest = refs[p:p + self.n_out], refs[p + self.n_out + nco:]
        self._cargs = None
        if self.comm is not None:
            self._cargs = (refs[self.n_in:p], refs[p + self.n_out:p + self.n_out + nco], rest[-2], rest[-1])
            rest = rest[:-2]
        return ins, outs, rest

    def before(self, step, total):
        if self.comm is None:
            return

        for at, stage in self.comm.schedule():
            pl.when(step == min(total - 1, int(at * total)))(functools.partial(stage, *self._cargs))

    def after(self, step, total):
        if self.comm is None:
            return

        @pl.when(step == total - 1)
        def _():
            self.comm.finish(*self._cargs)

    def results(self, outs):
        return outs[:self.n_out], outs[self.n_out:]


def _cast_bf16(x, name):
    r, c = x.shape
    tr = ROW_TILE if r % ROW_TILE == 0 else r

    def body(x_ref, o_ref):
        o_ref[...] = x_ref[...].astype(BF16)

    return pl.pallas_call(
        body, name=name, grid=(r // tr,),
        in_specs=[pl.BlockSpec((tr, c), lambda i: (i, 0))],
        out_specs=pl.BlockSpec((tr, c), lambda i: (i, 0)),
        out_shape=jax.ShapeDtypeStruct((r, c), BF16), compiler_params=_cp("parallel"))(x)


def _cast_bf16_own_slab(x, me_arr, name):
    r, c = x.shape
    tr = ROW_TILE if r % ROW_TILE == 0 else r

    def body(me_ref, x_ref, o_ref):
        o_ref[...] = x_ref[...].astype(BF16)

    return pl.pallas_call(
        body, name=name,
        grid_spec=pltpu.PrefetchScalarGridSpec(
            num_scalar_prefetch=1, grid=(r // tr,),
            in_specs=[pl.BlockSpec((tr, c), lambda i, me: (i, 0))],
            out_specs=pl.BlockSpec((None, tr, c), lambda i, me: (me[0], i, 0))),
        out_shape=jax.ShapeDtypeStruct((N_CHIPS, r, c), BF16), compiler_params=_cp("parallel"))(me_arr, x)


def _rms_fwd(x, g, name):
    s, d = x.shape

    def body(x_ref, g_ref, h_ref):
        xhat, _ = _rms_stats(x_ref[...])
        h_ref[...] = (xhat * g_ref[...]).astype(BF16)

    return pl.pallas_call(
        body, name=name, grid=(s // ROW_TILE,),
        in_specs=[pl.BlockSpec((ROW_TILE, d), lambda i: (i, 0)), pl.BlockSpec((1, d), lambda i: (0, 0))],
        out_specs=pl.BlockSpec((ROW_TILE, d), lambda i: (i, 0)),
        out_shape=jax.ShapeDtypeStruct((s, d), BF16), compiler_params=_cp("parallel"))(x, g)


def _mm_nn(a, w3, name, comm=None):
    m, k = a.shape
    nsh, _, ns = w3.shape
    tm = 512 if m % 512 == 0 else ROW_TILE
    tn = _pick_tile(ns, 1024)
    per = ns // tn
    grid = (nsh * per, m // tm)
    host = _Host(comm,
                 [pl.BlockSpec((tm, k), lambda n, i: (i, 0)), pl.BlockSpec((None, k, tn), lambda n, i: (n // per, 0, n % per))],
                 [pl.BlockSpec((tm, tn), lambda n, i: (i, n))], [jax.ShapeDtypeStruct((m, nsh * ns), F32)], [])

    def body(*refs):
        (a_ref, w_ref), (o_ref,), _ = host.split(refs)
        step = pl.program_id(0) * grid[1] + pl.program_id(1)
        host.before(step, grid[0] * grid[1])
        o_ref[...] = jnp.dot(a_ref[...], w_ref[...], preferred_element_type=F32)
        host.after(step, grid[0] * grid[1])

    outs = pl.pallas_call(
        body, name=name, grid=grid, in_specs=host.in_specs, out_specs=host.out_specs, out_shape=host.out_shape,
        scratch_shapes=host.scratch, input_output_aliases=host.aliases,
        compiler_params=_cp("arbitrary", "arbitrary"))(a, w3, *host.args)
    (out,), extra = host.results(outs)
    return out, extra


def _mm_nt(a, b, name):
    m, k = a.shape
    n = b.shape[0]
    tm = 512 if m % 512 == 0 else ROW_TILE

    def body(a_ref, b_ref, o_ref):
        o_ref[...] = lax.dot_general(a_ref[...], b_ref[...], (((1,), (1,)), ((), ())), preferred_element_type=F32)

    return pl.pallas_call(
        body, name=name, grid=(m // tm,),
        in_specs=[pl.BlockSpec((tm, k), lambda i: (i, 0)), pl.BlockSpec((n, k), lambda i: (0, 0))],
        out_specs=pl.BlockSpec((tm, n), lambda i: (i, 0)),
        out_shape=jax.ShapeDtypeStruct((m, n), F32), compiler_params=_cp("parallel"))(a, b)


def _mm_tn(a, b, nsh, name, comm=None):
    s, m = a.shape
    n = b.shape[1]
    ns = n // nsh
    tm = 512 if m % 512 == 0 else ROW_TILE
    tn = _pick_tile(ns, 1024)
    per = ns // tn
    grid = (nsh * per, m // tm)
    host = _Host(comm, [pl.BlockSpec((s, tm), lambda j, i: (0, i)), pl.BlockSpec((s, tn), lambda j, i: (0, j))],
                 [pl.BlockSpec((None, tm, tn), lambda j, i: (j // per, i, j % per))],
                 [jax.ShapeDtypeStruct((nsh, m, ns), BF16)], [])

    def body(*refs):
        (a_ref, b_ref), (o_ref,), _ = host.split(refs)
        step = pl.program_id(0) * grid[1] + pl.program_id(1)
        host.before(step, grid[0] * grid[1])
        o_ref[...] = lax.dot_general(a_ref[...], b_ref[...], (((0,), (0,)), ((), ())),
                                     preferred_element_type=F32).astype(BF16)
        host.after(step, grid[0] * grid[1])

    outs = pl.pallas_call(
        body, name=name, grid=grid, in_specs=host.in_specs, out_specs=host.out_specs, out_shape=host.out_shape,
        scratch_shapes=host.scratch, input_output_aliases=host.aliases,
        compiler_params=_cp("arbitrary", "arbitrary"))(a, b, *host.args)
    (out,), extra = host.results(outs)
    return out, extra


def _tri(n, rel):
    row = lax.broadcasted_iota(jnp.int32, (2 * n, n), 0)
    col = lax.broadcasted_iota(jnp.int32, (2 * n, n), 1)
    return jnp.where(rel(jnp.where(row >= n, row - n, row), col), 1.0, 0.0).astype(BF16)


def _dot_split(x, tri2):
    hi = x.astype(BF16)
    lo = (x - hi.astype(F32)).astype(BF16)
    return jnp.dot(jnp.concatenate([hi, lo], axis=1), tri2, preferred_element_type=F32)


def _nt(a, b):
    return lax.dot_general(a, b, (((1,), (1,)), ((), ())), preferred_element_type=F32)


def _tn(a, b):
    return lax.dot_general(a, b, (((0,), (0,)), ((), ())), preferred_element_type=F32)


def _heads_per_step(nh):
    return max(h for h in (1, 2, 4) if nh % h == 0)


def _sba_fwd(p, sbw, name, comm=None):
    s = p.shape[0]
    nh = sbw // HEAD_DIM
    hp = _heads_per_step(nh)
    ngrp, hw = nh // hp, hp * HEAD_DIM
    blk = ATT_BLOCK
    nq = s // blk
    scale = 1.0 / math.sqrt(HEAD_DIM)
    host = _Host(comm,
                 [pl.BlockSpec((blk, hw), lambda g, i: (i, g)),
                  pl.BlockSpec((s, hw), lambda g, i: (0, ngrp + g)),
                  pl.BlockSpec((s, hw), lambda g, i: (0, 2 * ngrp + g))],
                 [pl.BlockSpec((blk, hw), lambda g, i: (i, g))] * 2,
                 [jax.ShapeDtypeStruct((s, sbw), F32)] * 2,
                 [pltpu.VMEM((s, hw), BF16)] * 2)

    def body(*refs):
        (q_ref, k_ref, v_ref), (o_ref, lt_ref), (kb_ref, vb_ref) = host.split(refs)
        i = pl.program_id(1)
        step = pl.program_id(0) * nq + i
        host.before(step, ngrp * nq)

        @pl.when(i == 0)
        def _():
            kb_ref[...] = k_ref[...].astype(BF16)
            vb_ref[...] = v_ref[...].astype(BF16)

        heads = [slice(h * HEAD_DIM, (h + 1) * HEAD_DIM) for h in range(hp)]
        qs = [q_ref[:, hd].astype(BF16) for hd in heads]
        later = _tri(blk, lambda r, c: r > c)
        causal = lax.broadcasted_iota(jnp.int32, (blk, blk), 1) < lax.broadcasted_iota(jnp.int32, (blk, blk), 0)

        def key_block(j, carry, diagonal):
            rows = pl.ds(pl.multiple_of(j * blk, blk), blk)
            hs = range(hp)
            z = [_nt(qs[h], kb_ref[rows, heads[h]]) * scale for h in hs]
            ls = [_log_sigmoid(z[h]) for h in hs]
            lm = [jnp.where(causal, ls[h] - z[h], 0.0) if diagonal else ls[h] - z[h] for h in hs]
            stay = [_dot_split(lm[h], later) for h in hs]
            w = [jnp.exp(ls[h] + stay[h] + carry[h][1]) for h in hs]
            if diagonal:
                w = [jnp.where(causal, w[h], 0.0) for h in hs]
            acc = [carry[h][0] + jnp.dot(w[h].astype(BF16), vb_ref[rows, heads[h]], preferred_element_type=F32) for h in hs]
            return tuple((acc[h], carry[h][1] + jnp.sum(lm[h], axis=1, keepdims=True)) for h in hs)

        init = tuple((jnp.zeros((blk, HEAD_DIM), F32), jnp.zeros((blk, 1), F32)) for _ in heads)
        carry = key_block(i, init, True)
        carry = lax.fori_loop(0, i, lambda n, c: key_block(i - 1 - n, c, False), carry)
        for h, hd in enumerate(heads):
            o_ref[:, hd] = carry[h][0]
            lt_ref[:, hd] = jnp.broadcast_to(carry[h][1], (blk, HEAD_DIM))
        host.after(step, ngrp * nq)

    outs = pl.pallas_call(
        body, name=name, grid=(ngrp, nq), in_specs=host.in_specs, out_specs=host.out_specs, out_shape=host.out_shape,
        scratch_shapes=host.scratch, input_output_aliases=host.aliases,
        compiler_params=_cp("arbitrary", "arbitrary"))(p, p, p, *host.args)
    (out, ltot), extra = host.results(outs)
    return out, ltot, extra


def _sba_bwd(p, ltot, dout, sbw, name, comm=None):
    s = p.shape[0]
    nh = sbw // HEAD_DIM
    hp = _heads_per_step(nh)
    ngrp, hw = nh // hp, hp * HEAD_DIM
    blk = ATT_BLOCK
    nq = s // blk
    scale = 1.0 / math.sqrt(HEAD_DIM)
    blk_spec = pl.BlockSpec((blk, hw), lambda g, i: (i, g))
    col_spec = pl.BlockSpec((s, hw), lambda g, i: (0, g))
    host = _Host(comm,
                 [blk_spec, pl.BlockSpec((s, hw), lambda g, i: (0, ngrp + g)),
                  pl.BlockSpec((s, hw), lambda g, i: (0, 2 * ngrp + g)), blk_spec, blk_spec],
                 [blk_spec, col_spec, col_spec], [jax.ShapeDtypeStruct((s, sbw), BF16)] * 3,
                 [pltpu.VMEM((s, hw), BF16)] * 2 + [pltpu.VMEM((s, hw), F32)] * 2)

    def body(*refs):
        (q_ref, k_ref, v_ref, lt_ref, do_ref), (dq_ref, dk_ref, dv_ref), (kb_ref, vb_ref, dka_ref, dva_ref) = host.split(refs)
        i = pl.program_id(1)
        step = pl.program_id(0) * nq + i
        host.before(step, ngrp * nq)

        @pl.when(i == 0)
        def _():
            kb_ref[...] = k_ref[...].astype(BF16)
            vb_ref[...] = v_ref[...].astype(BF16)
            dka_ref[...] = jnp.zeros_like(dka_ref)
            dva_ref[...] = jnp.zeros_like(dva_ref)

        heads = [slice(h * HEAD_DIM, (h + 1) * HEAD_DIM) for h in range(hp)]
        qs = [q_ref[:, hd].astype(BF16) for hd in heads]
        dos = [do_ref[:, hd].astype(BF16) for hd in heads]
        ltots = [lt_ref[:, h * HEAD_DIM:h * HEAD_DIM + 1] for h in range(hp)]
        upto = _tri(blk, lambda r, c: r <= c)
        before = _tri(blk, lambda r, c: r < c)
        causal = lax.broadcasted_iota(jnp.int32, (blk, blk), 1) < lax.broadcasted_iota(jnp.int32, (blk, blk), 0)

        def key_block(j, carry, diagonal):
            rows = pl.ds(pl.multiple_of(j * blk, blk), blk)
            hs = range(hp)
            kj = [kb_ref[rows, heads[h]] for h in hs]
            vj = [vb_ref[rows, heads[h]] for h in hs]
            z = [_nt(qs[h], kj[h]) * scale for h in hs]
            dw = [_nt(dos[h], vj[h]) for h in hs]
            ls = [_log_sigmoid(z[h]) for h in hs]
            lm = [jnp.where(causal, ls[h] - z[h], 0.0) if diagonal else ls[h] - z[h] for h in hs]
            stay = [ltots[h] - carry[h][1] - _dot_split(lm[h], upto) for h in hs]
            w = [jnp.exp(ls[h] + stay[h]) for h in hs]
            if diagonal:
                w = [jnp.where(causal, w[h], 0.0) for h in hs]
            da = [dw[h] * w[h] for h in hs]
            sig = [jnp.exp(ls[h]) for h in hs]
            chain = [sig[h] * (carry[h][2] + _dot_split(da[h], before)) for h in hs]
            if diagonal:
                chain = [jnp.where(causal, chain[h], 0.0) for h in hs]
            dzb = [((da[h] * (1.0 - sig[h]) - chain[h]) * scale).astype(BF16) for h in hs]
            dq = [carry[h][0] + jnp.dot(dzb[h], kj[h], preferred_element_type=F32) for h in hs]
            for h in hs:
                dka_ref[rows, heads[h]] += _tn(dzb[h], qs[h])
            for h in hs:
                dva_ref[rows, heads[h]] += _tn(w[h].astype(BF16), dos[h])
            return tuple((dq[h], carry[h][1] + jnp.sum(lm[h], axis=1, keepdims=True),
                          carry[h][2] + jnp.sum(da[h], axis=1, keepdims=True)) for h in hs)

        zero = jnp.zeros((blk, 1), F32)
        init = tuple((jnp.zeros((blk, HEAD_DIM), F32), zero, zero) for _ in heads)
        carry = lax.fori_loop(0, i, lambda j, c: key_block(j, c, False), init)
        carry = key_block(i, carry, True)
        for h, hd in enumerate(heads):
            dq_ref[:, hd] = carry[h][0].astype(BF16)

        @pl.when(i == nq - 1)
        def _():
            dk_ref[...] = dka_ref[...].astype(BF16)
            dv_ref[...] = dva_ref[...].astype(BF16)

        host.after(step, ngrp * nq)

    outs = pl.pallas_call(
        body, name=name, grid=(ngrp, nq), in_specs=host.in_specs, out_specs=host.out_specs, out_shape=host.out_shape,
        scratch_shapes=host.scratch, input_output_aliases=host.aliases,
        compiler_params=_cp("arbitrary", "arbitrary"))(p, p, p, ltot, dout, *host.args)
    (dq, dk, dv), extra = host.results(outs)
    return dq, dk, dv, extra


def _pool_groups(pad_ref, tile, row0, gd, halo):
    row = row0 + lax.broadcasted_iota(jnp.int32, (tile, 1), 0)
    out = []
    for gi, win in enumerate(POOL_WINDOWS):
        cs = slice(gi * gd, (gi + 1) * gd)
        tok = pad_ref[halo:halo + tile, cs]
        acc = tok
        for j in range(1, win):
            acc = acc + pad_ref[halo - j:halo - j + tile, cs]
        cnt = jnp.minimum(win, row + 1).astype(F32)
        out.append(acc / cnt - tok)
    return out


def _even_mix_fwd(p, att, pool_w, pool_scale, d, name):
    s = p.shape[0]
    half = d // 2
    gd = half // len(POOL_WINDOWS)
    t, hb = ROW_TILE, POOL_HALO

    def body(u_ref, uh_ref, g_ref, a_ref, pw_ref, sc_ref, y_ref, pad_ref):
        i = pl.program_id(0)
        pad_ref[0:hb, :] = jnp.where(i > 0, uh_ref[...], 0.0)
        pad_ref[hb:, :] = u_ref[...]
        pooled = _pool_groups(pad_ref, t, i * t, gd, hb)
        for gi in range(len(POOL_WINDOWS)):
            cs = slice(gi * gd, (gi + 1) * gd)
            po = jnp.dot(pooled[gi].astype(BF16), pw_ref[gi], preferred_element_type=F32) * sc_ref[:, cs]
            y_ref[:, half + gi * gd:half + (gi + 1) * gd] = (po * _silu(g_ref[:, half + gi * gd:half + (gi + 1) * gd])).astype(BF16)
        y_ref[:, :half] = (a_ref[...] * _silu(g_ref[:, :half])).astype(BF16)

    return pl.pallas_call(
        body, name=name, grid=(s // t,),
        in_specs=[pl.BlockSpec((t, half), lambda i: (i, 3)),
                  pl.BlockSpec((hb, half), lambda i: (jnp.maximum(i * (t // hb) - 1, 0), 3)),
                  pl.BlockSpec((t, d), lambda i: (i, 2)),
                  pl.BlockSpec((t, half), lambda i: (i, 0)),
                  pl.BlockSpec(pool_w.shape, lambda i: (0, 0, 0)),
                  pl.BlockSpec((1, half), lambda i: (0, 0))],
        out_specs=pl.BlockSpec((t, d), lambda i: (i, 0)),
        out_shape=jax.ShapeDtypeStruct((s, d), BF16),
        scratch_shapes=[pltpu.VMEM((hb + t, half), F32)],
        compiler_params=_cp("parallel"))(p, p, p, att, pool_w, pool_scale)


def _even_mix_bwd(p, att, dy, pool_w, pool_scale, d, name, comm=None):
    s = p.shape[0]
    half = d // 2
    ng = len(POOL_WINDOWS)
    gd = half // ng
    t, hb = ROW_TILE, POOL_HALO
    nt = s // t
    host = _Host(
        comm,
        [pl.BlockSpec((t, half), lambda i: (i, 3)),
         pl.BlockSpec((hb, half), lambda i: (jnp.maximum(i * (t // hb) - 1, 0), 3)),
         pl.BlockSpec((t, d), lambda i: (i, 2)),
         pl.BlockSpec((hb, half), lambda i: (jnp.minimum((i + 1) * (t // hb), s // hb - 1), 5)),
         pl.BlockSpec((t, half), lambda i: (i, 0)),
         pl.BlockSpec((t, d), lambda i: (i, 0)),
         pl.BlockSpec((hb, half), lambda i: (jnp.minimum((i + 1) * (t // hb), s // hb - 1), 1)),
         pl.BlockSpec(pool_w.shape, lambda i: (0, 0, 0)),
         pl.BlockSpec((1, half), lambda i: (0, 0))],
        [pl.BlockSpec((t, half), lambda i: (i, 0)),
         pl.BlockSpec((t, half), lambda i: (i, 0)),
         pl.BlockSpec((t, d), lambda i: (i, 0)),
         pl.BlockSpec((1, half), lambda i: (0, 0)),
         pl.BlockSpec((ng, gd, gd), lambda i: (0, 0, 0))],
        [jax.ShapeDtypeStruct((s, half), F32), jax.ShapeDtypeStruct((s, half), BF16),
         jax.ShapeDtypeStruct((s, d), BF16), jax.ShapeDtypeStruct((1, half), F32),
         jax.ShapeDtypeStruct((ng, gd, gd), F32)],
        [pltpu.VMEM((hb + t, half), F32), pltpu.VMEM((t + hb, half), F32)])

    def body(*refs):
        ((u_ref, uh_ref, g_ref, gh_ref, a_ref, dy_ref, dyh_ref, pw_ref, sc_ref),
         (da_ref, du_ref, dg_ref, dsc_ref, dpw_ref), (pad_ref, dn_ref)) = host.split(refs)
        i = pl.program_id(0)
        host.before(i, nt)
        first = i == 0
        pad_ref[0:hb, :] = jnp.where(i > 0, uh_ref[...], 0.0)
        pad_ref[hb:, :] = u_ref[...]
        pooled = _pool_groups(pad_ref, t, i * t, gd, hb)
        g1 = g_ref[:, :half]
        dy1 = dy_ref[:, :half]
        da_ref[...] = dy1 * _silu(g1)
        dg_ref[:, :half] = (dy1 * a_ref[...] * _dsilu(g1)).astype(BF16)
        row = i * t + lax.broadcasted_iota(jnp.int32, (t + hb, 1), 0)
        for gi, win in enumerate(POOL_WINDOWS):
            cs = slice(gi * gd, (gi + 1) * gd)
            cs2 = slice(half + gi * gd, half + (gi + 1) * gd)
            w = pw_ref[gi]
            pb = pooled[gi].astype(BF16)
            zp = jnp.dot(pb, w, preferred_element_type=F32)
            g2 = g_ref[:, cs2]
            dy2 = dy_ref[:, cs2]
            dg_ref[:, cs2] = (dy2 * zp * sc_ref[:, cs] * _dsilu(g2)).astype(BF16)
            dpo = dy2 * _silu(g2)
            _acc_rows(dsc_ref.at[:, cs], first, jnp.sum(dpo * zp, axis=0, keepdims=True))
            dz = (dpo * sc_ref[:, cs]).astype(BF16)
            _acc_rows(dpw_ref.at[gi], first, _tn(pb, dz))
            dzh = jnp.where(i < nt - 1, dyh_ref[:, cs] * _silu(gh_ref[:, cs]) * sc_ref[:, cs], 0.0).astype(BF16)
            dpool = _nt(dz, w)
            dpool_h = _nt(dzh, w)
            cnt = jnp.minimum(win, row + 1).astype(F32)
            dn_ref[0:t, cs] = dpool / cnt[0:t]
            dn_ref[t:, cs] = dpool_h / cnt[t:]
            acc = dn_ref[0:t, cs]
            for j in range(1, win):
                acc = acc + dn_ref[j:j + t, cs]
            du_ref[:, cs] = (acc - dpool).astype(BF16)
        host.after(i, nt)

    outs = pl.pallas_call(
        body, name=name, grid=(nt,), in_specs=host.in_specs, out_specs=host.out_specs, out_shape=host.out_shape,
        scratch_shapes=host.scratch, input_output_aliases=host.aliases,
        compiler_params=_cp("arbitrary"))(p, p, p, p, att, dy, dy, pool_w, pool_scale, *host.args)
    return host.results(outs)


def _mm_out_even(y, w, x, g_post, g_pre_next, name):
    s, k = y.shape
    d = w.shape[1]
    t = ROW_TILE

    def body(y_ref, w_ref, x_ref, gp_ref, gn_ref, o_ref, x1_ref, h1_ref):
        for r0 in range(0, t, t // 2):
            rows = slice(r0, r0 + t // 2)
            o = jnp.dot(y_ref[rows, :], w_ref[...], preferred_element_type=F32)
            o_ref[rows, :] = o
            ohat, _ = _rms_stats(o)
            x1 = x_ref[rows, :] + ohat * gp_ref[...]
            x1_ref[rows, :] = x1
            xhat, _ = _rms_stats(x1)
            h1_ref[rows, :] = (xhat * gn_ref[...]).astype(BF16)

    row = lambda c: pl.BlockSpec((t, c), lambda i: (i, 0))
    vec = pl.BlockSpec((1, d), lambda i: (0, 0))
    return pl.pallas_call(
        body, name=name, grid=(s // t,),
        in_specs=[row(k), pl.BlockSpec((k, d), lambda i: (0, 0)), row(d), vec, vec],
        out_specs=[row(d), row(d), row(d)],
        out_shape=[jax.ShapeDtypeStruct((s, d), F32), jax.ShapeDtypeStruct((s, d), F32),
                   jax.ShapeDtypeStruct((s, d), BF16)],
        compiler_params=_cp("parallel"))(y, w, x, g_post, g_pre_next)


def _mm_out_odd(y, w, x1, g_post, target, name):
    s, k = y.shape
    d = w.shape[1]
    t = ROW_TILE

    def body(y_ref, w_ref, x_ref, gp_ref, tg_ref, do_ref, dx_ref, loss_ref, dgp_ref):
        first = pl.program_id(0) == 0
        gp = gp_ref[...]
        part = dgp = None
        for r0 in range(0, t, t // 2):
            rows = slice(r0, r0 + t // 2)
            o = jnp.dot(y_ref[rows, :], w_ref[...], preferred_element_type=F32)
            ohat, r = _rms_stats(o)
            diff = x_ref[rows, :] + ohat * gp - tg_ref[rows, :]
            part_half = 0.5 * jnp.sum(jnp.mean(diff * diff, axis=-1, keepdims=True), axis=0, keepdims=True)
            dx2 = diff * (1.0 / d)
            dx_ref[rows, :] = dx2
            do, dgp_half = _rms_bwd(dx2, ohat, r, gp)
            do_ref[rows, :] = do.astype(BF16)
            part = part_half if part is None else part + part_half
            dgp = dgp_half if dgp is None else dgp + dgp_half
        _acc_rows(loss_ref, first, jnp.broadcast_to(part, loss_ref.shape))
        _acc_rows(dgp_ref, first, dgp)

    row = lambda c: pl.BlockSpec((t, c), lambda i: (i, 0))
    vec = pl.BlockSpec((1, d), lambda i: (0, 0))
    return pl.pallas_call(
        body, name=name, grid=(s // t,),
        in_specs=[row(k), pl.BlockSpec((k, d), lambda i: (0, 0)), row(d), vec, row(d)],
        out_specs=[row(d), row(d), pl.BlockSpec((8, LANES), lambda i: (0, 0)), vec],
        out_shape=[jax.ShapeDtypeStruct((s, d), BF16), jax.ShapeDtypeStruct((s, d), F32),
                   jax.ShapeDtypeStruct((8, LANES), F32), jax.ShapeDtypeStruct((1, d), F32)],
        compiler_params=_cp("arbitrary"))(y, w, x1, g_post, target)


def _layer_norm(d1, cg, cb):
    mu = jnp.mean(d1, axis=-1, keepdims=True)
    cen = d1 - mu
    rstd = lax.rsqrt(jnp.mean(cen * cen, axis=-1, keepdims=True) + EPS)
    n = cen * rstd
    return n, rstd, n * cg + cb


SUBLANES = 8
ROW_STRIP = 64
GATHER_PIECES = 8
CONV_ROWS = 64


def _make_shifts(pad_ref, cs, sh_ref):
    rows = sh_ref.shape[1]
    for r in range(1, SUBLANES):
        sh_ref[r - 1] = pad_ref[r:r + rows, cs]


def _by_shift(taps, base, sign=1):
    return sorted(range(taps), key=lambda k: ((sign * (base + k)) % SUBLANES, k))


def _window(pad_ref, cs, sh_ref, off, t):
    m, r = divmod(off, SUBLANES)
    if r == 0:
        return pad_ref[SUBLANES * m:SUBLANES * m + t, cs]
    return sh_ref[r - 1, SUBLANES * m:SUBLANES * m + t, :]


def _odd_mix_fwd(p, sconv_w, dconv_w, dconv_b, cnorm_g, cnorm_b, d, name):
    s = p.shape[0]
    w = d // 2
    k3, k31 = sconv_w.shape[0], dconv_w.shape[0]
    t, hb = ROW_TILE, CONV_HALO
    assert hb >= k31 - 1 and w % LANES == 0

    def body(p_ref, ph_ref, w3_ref, w31_ref, b31_ref, cg_ref, cb_ref, y_ref, s3_ref, d1_ref, mpad, dpad, sh_ref):
        i = pl.program_id(0)
        mpad[0:hb, :] = jnp.where(i > 0, ph_ref[:, 2 * w:3 * w] * ph_ref[:, 0:w], 0.0)
        mpad[hb:, :] = p_ref[:, 2 * w:3 * w] * p_ref[:, 0:w]
        dpad[0:hb, :] = jnp.where(i > 0, ph_ref[:, 3 * w:4 * w] * _sigmoid(ph_ref[:, 4 * w:5 * w]), 0.0)
        dpad[hb:, :] = p_ref[:, 3 * w:4 * w] * _sigmoid(p_ref[:, 4 * w:5 * w])
        for c0 in range(0, w, LANES):
            cs = slice(c0, c0 + LANES)
            acc = jnp.zeros((t, LANES), F32)
            for kk in range(k3):
                acc = acc + w3_ref[kk:kk + 1, cs] * mpad[hb - (k3 - 1) + kk:hb - (k3 - 1) + kk + t, cs]
            s3_ref[:, cs] = acc
            _make_shifts(dpad, cs, sh_ref)
            for r0 in range(0, t, CONV_ROWS):
                acc = jnp.zeros((CONV_ROWS, LANES), F32)
                for kk in _by_shift(k31, hb - (k31 - 1)):
                    acc = acc + w31_ref[kk:kk + 1, cs] * _window(dpad, cs, sh_ref, hb - (k31 - 1) + kk + r0, CONV_ROWS)
                d1_ref[r0:r0 + CONV_ROWS, cs] = acc + b31_ref[:, cs]
        _, _, d2 = _layer_norm(d1_ref[...], cg_ref[...], cb_ref[...])
        y_ref[:, :w] = (p_ref[:, w:2 * w] * s3_ref[...] * _silu(p_ref[:, 5 * w:6 * w])).astype(BF16)
        y_ref[:, w:] = (_silu(d2) * _silu(p_ref[:, 6 * w:7 * w])).astype(BF16)

    row = lambda c: pl.BlockSpec((t, c), lambda i: (i, 0))
    full = lambda a: pl.BlockSpec(a.shape, lambda i: (0, 0))
    return pl.pallas_call(
        body, name=name, grid=(s // t,),
        in_specs=[row(7 * w),
                  pl.BlockSpec((hb, 5 * w), lambda i: (jnp.maximum(i * (t // hb) - 1, 0), 0)),
                  full(sconv_w), full(dconv_w), full(dconv_b), full(cnorm_g), full(cnorm_b)],
        out_specs=[row(d), row(w), row(w)],
        out_shape=[jax.ShapeDtypeStruct((s, d), BF16), jax.ShapeDtypeStruct((s, w), F32),
                   jax.ShapeDtypeStruct((s, w), F32)],
        scratch_shapes=[pltpu.VMEM((hb + t, w), F32)] * 2 + [pltpu.VMEM((SUBLANES - 1, hb + t - SUBLANES, LANES), F32)],
        compiler_params=_cp("parallel"))(p, p, sconv_w, dconv_w, dconv_b, cnorm_g, cnorm_b)


def _odd_bwd_rows(p, s3, d1, dy, cnorm_g, cnorm_b, d, name, comm=None):
    s = p.shape[0]
    w = d // 2
    t = ROW_TILE
    col = lambda j: pl.BlockSpec((t, w), lambda i: (i, j))
    row = lambda c: pl.BlockSpec((t, c), lambda i: (i, 0))
    vec = pl.BlockSpec((1, w), lambda i: (0, 0))
    host = _Host(comm, [col(1), col(5), col(6), row(w), row(w), row(d), vec, vec],
                 [row(w), row(d), row(w), row(w), vec, vec, vec],
                 [jax.ShapeDtypeStruct((s, w), BF16), jax.ShapeDtypeStruct((s, d), BF16),
                  jax.ShapeDtypeStruct((s, w), F32), jax.ShapeDtypeStruct((s, w), F32)] + [jax.ShapeDtypeStruct((1, w), F32)] * 3, [])

    def body(*refs):
        ((bc_ref, g1_ref, g2_ref, s3_ref, d1_ref, dy_ref, cg_ref, cb_ref),
         (dbc_ref, dg_ref, ds3_ref, dd1_ref, dcg_ref, dcb_ref, db_ref), _) = host.split(refs)
        step = pl.program_id(0)
        host.before(step, s // t)
        first = step == 0

        def strip(j, sums):
            rows = slice(j * ROW_STRIP, (j + 1) * ROW_STRIP)
            g1, g2 = g1_ref[rows, :], g2_ref[rows, :]
            bc, s3v = bc_ref[rows, :], s3_ref[rows, :]
            dy1, dy2 = dy_ref[rows, :w], dy_ref[rows, w:]
            n, rstd, d2 = _layer_norm(d1_ref[rows, :], cg_ref[...], cb_ref[...])
            dg_ref[rows, :w] = (dy1 * bc * s3v * _dsilu(g1)).astype(BF16)
            dg_ref[rows, w:] = (dy2 * _silu(d2) * _dsilu(g2)).astype(BF16)
            dco = dy1 * _silu(g1)
            dbc_ref[rows, :] = (dco * s3v).astype(BF16)
            ds3_ref[rows, :] = dco * bc
            dd2 = dy2 * _silu(g2) * _dsilu(d2)
            dn = dd2 * cg_ref[...]
            dd1 = rstd * (dn - jnp.mean(dn, axis=-1, keepdims=True) - n * jnp.mean(dn * n, axis=-1, keepdims=True))
            dd1_ref[rows, :] = dd1
            dcb, dcg, db = sums
            return (dcb + jnp.sum(dd2, axis=0, keepdims=True), dcg + jnp.sum(dd2 * n, axis=0, keepdims=True),
                    db + jnp.sum(dd1, axis=0, keepdims=True))

        zero = jnp.zeros((1, w), F32)
        sums = (zero, zero, zero)
        for j in range(t // ROW_STRIP):
            sums = strip(j, sums)
        dcb, dcg, db = sums
        _acc_rows(dcb_ref, first, dcb)
        _acc_rows(dcg_ref, first, dcg)
        _acc_rows(db_ref, first, db)
        host.after(step, s // t)

    outs = pl.pallas_call(
        body, name=name, grid=(s // t,), in_specs=host.in_specs, out_specs=host.out_specs, out_shape=host.out_shape,
        scratch_shapes=host.scratch, input_output_aliases=host.aliases,
        compiler_params=_cp("arbitrary"))(p, p, p, s3, d1, dy, cnorm_g, cnorm_b, *host.args)
    return host.results(outs)


def _odd_bwd_conv(p, ds3, dd1, sconv_w, dconv_w, d, name):
    s = p.shape[0]
    w = d // 2
    k3, k31 = sconv_w.shape[0], dconv_w.shape[0]
    t, hb, ha = ROW_TILE, CONV_HALO, 8
    nt = s // t
    assert hb >= k31 - 1 and ha >= k3 - 1

    def body(hc_ref, cc_ref, ga_ref, gb_ref, hch_ref, cch_ref, gah_ref, gbh_ref, ds3_ref, ds3h_ref, dd1_ref, dd1h_ref,
             w3_ref, w31_ref, dhc_ref, dcc_ref, dga_ref, dgb_ref, dw3_ref, dw31_ref, mpad, dpad, s3pad, d1pad, sh_ref):
        i = pl.program_id(0)
        first = i == 0
        last = i == nt - 1
        mpad[0:hb, :] = jnp.where(i > 0, cch_ref[...] * hch_ref[...], 0.0)
        mpad[hb:, :] = cc_ref[...] * hc_ref[...]
        dpad[0:hb, :] = jnp.where(i > 0, gah_ref[...] * _sigmoid(gbh_ref[...]), 0.0)
        dpad[hb:, :] = ga_ref[...] * _sigmoid(gb_ref[...])
        s3pad[0:t, :] = ds3_ref[...]
        s3pad[t:, :] = jnp.where(last, 0.0, ds3h_ref[...])
        d1pad[0:t, :] = dd1_ref[...]
        d1pad[t:, :] = jnp.where(last, 0.0, dd1h_ref[...])

        @pl.when(first)
        def _():
            dw3_ref[...] = jnp.zeros_like(dw3_ref)
            dw31_ref[...] = jnp.zeros_like(dw31_ref)

        def fold(v):
            return jnp.sum(v.reshape(v.shape[0] // SUBLANES, SUBLANES, LANES), axis=0)

        groups = range(0, t, CONV_ROWS)
        for c0 in range(0, w, LANES):
            cs = slice(c0, c0 + LANES)
            ds3v = s3pad[0:t, cs]
            dm = jnp.zeros((t, LANES), F32)
            for kk in range(k3):
                dm = dm + w3_ref[kk:kk + 1, cs] * s3pad[k3 - 1 - kk:k3 - 1 - kk + t, cs]
                off = hb - (k3 - 1) + kk
                dw3_ref[SUBLANES * kk:SUBLANES * (kk + 1), cs] += fold(ds3v * mpad[off:off + t, cs])
            dcc_ref[:, cs] = (dm * hc_ref[:, cs]).astype(BF16)
            dhc_ref[:, cs] = (dm * cc_ref[:, cs]).astype(BF16)
            _make_shifts(d1pad, cs, sh_ref)
            for r0 in groups:
                rows = slice(r0, r0 + CONV_ROWS)
                dd0 = jnp.zeros((CONV_ROWS, LANES), F32)
                for kk in _by_shift(k31, -(k31 - 1), -1):
                    dd0 = dd0 + w31_ref[kk:kk + 1, cs] * _window(d1pad, cs, sh_ref, k31 - 1 - kk + r0, CONV_ROWS)
                sgb = _sigmoid(gb_ref[rows, cs])
                dga_ref[rows, cs] = (dd0 * sgb).astype(BF16)
                dgb_ref[rows, cs] = (dd0 * ga_ref[rows, cs] * sgb * (1.0 - sgb)).astype(BF16)
            _make_shifts(dpad, cs, sh_ref)
            for kk in _by_shift(k31, hb - (k31 - 1)):
                part = jnp.zeros((SUBLANES, LANES), F32)
                for r0 in groups:
                    part = part + fold(d1pad[r0:r0 + CONV_ROWS, cs]
                                       * _window(dpad, cs, sh_ref, hb - (k31 - 1) + kk + r0, CONV_ROWS))
                dw31_ref[SUBLANES * kk:SUBLANES * (kk + 1), cs] += part

    col = lambda j: pl.BlockSpec((t, w), lambda i: (i, j))
    pre = lambda j: pl.BlockSpec((hb, w), lambda i: (jnp.maximum(i * (t // hb) - 1, 0), j))
    row = pl.BlockSpec((t, w), lambda i: (i, 0))
    post = lambda h: pl.BlockSpec((h, w), lambda i: (jnp.minimum((i + 1) * (t // h), s // h - 1), 0))
    full = lambda a: pl.BlockSpec(a.shape, lambda i: (0, 0))
    dhc, dcc, dga, dgb, dw3, dw31 = pl.pallas_call(
        body, name=name, grid=(nt,),
        in_specs=[col(0), col(2), col(3), col(4), pre(0), pre(2), pre(3), pre(4),
                  row, post(ha), row, post(hb), full(sconv_w), full(dconv_w)],
        out_specs=[row, row, row, row, pl.BlockSpec((SUBLANES * k3, w), lambda i: (0, 0)),
                   pl.BlockSpec((SUBLANES * k31, w), lambda i: (0, 0))],
        out_shape=[jax.ShapeDtypeStruct((s, w), BF16)] * 4
        + [jax.ShapeDtypeStruct((SUBLANES * k3, w), F32), jax.ShapeDtypeStruct((SUBLANES * k31, w), F32)],
        scratch_shapes=[pltpu.VMEM((hb + t, w), F32)] * 2 + [pltpu.VMEM((t + ha, w), F32), pltpu.VMEM((t + hb, w), F32),
                                                             pltpu.VMEM((SUBLANES - 1, hb + t - SUBLANES, LANES), F32)],
        compiler_params=_cp("arbitrary"))(p, p, p, p, p, p, p, p, ds3, ds3, dd1, dd1, sconv_w, dconv_w)
    return dhc, dcc, dga, dgb, jnp.sum(dw3.reshape(k3, SUBLANES, w), axis=1), jnp.sum(dw31.reshape(k31, SUBLANES, w), axis=1)


def _mm_in_bwd(dp, w3, x, g_pre, dres, post, name, comm=None):
    s = dp.shape[0]
    nsh, d, ns = w3.shape
    t = 512 if s % 512 == 0 else ROW_TILE
    nt = s // t
    ks = 2 if (ns // 2) % LANES == 0 else 1
    nk, kw = nsh * ks, ns // ks
    chunk = 128
    nchunk = t // chunk
    row = pl.BlockSpec((t, d), lambda i, k: (i, 0))
    vec = pl.BlockSpec((1, d), lambda i, k: (0, 0))
    rowwise = [x, dres] + ([post[0]] if post is not None else [])
    in_specs = [pl.BlockSpec((t, kw), lambda i, k: (i, k)), pl.BlockSpec((None, d, kw), lambda i, k: (k // ks, 0, k % ks)), vec]
    out_specs = [row, vec]
    out_shape = [jax.ShapeDtypeStruct((s, d), F32), jax.ShapeDtypeStruct((1, d), F32)]
    args = [dp, w3, g_pre]
    if post is not None:
        in_specs += [vec]
        out_specs += [row, vec]
        out_shape += [jax.ShapeDtypeStruct((s, d), BF16), jax.ShapeDtypeStruct((1, d), F32)]
        args += [post[1]]
    n_blocked = len(in_specs)
    in_specs += [ANY] * len(rowwise)
    args += rowwise
    host = _Host(comm, in_specs, out_specs, out_shape,
                 [pltpu.VMEM((t, d), F32), pltpu.VMEM((len(rowwise), 2, chunk, d), F32), pltpu.SemaphoreType.DMA((len(rowwise), 2))])

    def body(*refs):
        ins, outs, (acc_ref, buf_ref, sem_ref) = host.split(refs)
        dp_ref, w_ref, g_ref = ins[:3]
        hbm = ins[n_blocked:]
        dx_ref, dg_ref = outs[:2]
        tile = pl.program_id(0)
        kk = pl.program_id(1)
        first = tile == 0
        step = tile * nk + kk
        host.before(step, nt * nk)
        part = _nt(dp_ref[...], w_ref[...])

        @pl.when(kk == 0)
        def _():
            acc_ref[...] = part

        @pl.when(kk > 0)
        def _():
            acc_ref[...] += part

        def fetch(ci, slot):
            return [pltpu.make_async_copy(src.at[pl.ds(tile * t + ci * chunk, chunk)], buf_ref.at[n, slot], sem_ref.at[n, slot])
                    for n, src in enumerate(hbm)]

        @pl.when(kk == nk - 1)
        def _():
            dg = dgp = None
            for cp in fetch(0, 0):
                cp.start()
            for ci in range(nchunk):
                slot = ci % 2
                if ci + 1 < nchunk:
                    for cp in fetch(ci + 1, 1 - slot):
                        cp.start()
                for cp in fetch(ci, slot):
                    cp.wait()
                rows = slice(ci * chunk, (ci + 1) * chunk)
                xhat, r = _rms_stats(buf_ref[0, slot])
                dxn, dg_part = _rms_bwd(acc_ref[rows, :], xhat, r, g_ref[...])
                dx = buf_ref[1, slot] + dxn
                dx_ref[rows, :] = dx
                dg = dg_part if dg is None else dg + dg_part
                if post is not None:
                    ohat, ro = _rms_stats(buf_ref[2, slot])
                    do, dgp_part = _rms_bwd(dx, ohat, ro, ins[3][...])
                    outs[2][rows, :] = do.astype(BF16)
                    dgp = dgp_part if dgp is None else dgp + dgp_part
            _acc_rows(dg_ref, first, dg)
            if post is not None:
                _acc_rows(outs[3], first, dgp)

        host.after(step, nt * nk)

    res = pl.pallas_call(
        body, name=name, grid=(nt, nk), in_specs=host.in_specs, out_specs=host.out_specs, out_shape=host.out_shape,
        scratch_shapes=host.scratch, input_output_aliases=host.aliases,
        compiler_params=_cp("arbitrary", "arbitrary"))(*args, *host.args)
    return host.results(res)


def _half_add(g, r1, c_arr, name):
    nsh, rows, ns = g.shape
    h = rows // 2
    tr = min(ROW_TILE, h)
    per = h // tr

    def body(c_ref, g_ref, r_ref, o_ref):
        o_ref[...] = (g_ref[...].astype(F32) + r_ref[...].astype(F32)).astype(BF16)

    spec = pl.BlockSpec((None, tr, ns), lambda s, r, c: (s, r, 0))
    return pl.pallas_call(
        body, name=name,
        grid_spec=pltpu.PrefetchScalarGridSpec(
            num_scalar_prefetch=1, grid=(nsh, per),
            in_specs=[pl.BlockSpec((None, tr, ns), lambda s, r, c: (s, c[0] * per + r, 0)), spec], out_specs=spec),
        out_shape=jax.ShapeDtypeStruct((nsh, h, ns), BF16), compiler_params=_cp("parallel", "parallel"))(c_arr, g, r1)


def _sum_chips(hh, r2, mc_arr, name):
    _, h, ns = hh.shape
    tr = min(ROW_TILE, h)
    per = h // tr

    def body(mc_ref, h_ref, a_ref, b_ref, c_ref, o_ref):
        o_ref[...] = ((h_ref[...].astype(F32) + a_ref[...].astype(F32)) + b_ref[...].astype(F32)) + c_ref[...].astype(F32)

    got = lambda k: pl.BlockSpec((None, tr, ns), lambda r, mc: (k, r, 0))
    return pl.pallas_call(
        body, name=name,
        grid_spec=pltpu.PrefetchScalarGridSpec(
            num_scalar_prefetch=1, grid=(per,),
            in_specs=[pl.BlockSpec((None, tr, ns), lambda r, mc: (mc[0], r, 0)), got(0), got(1), got(2)],
            out_specs=pl.BlockSpec((tr, ns), lambda r, mc: (mc[1] * per + r, 0))),
        out_shape=jax.ShapeDtypeStruct((2 * h, ns), F32), compiler_params=_cp("parallel"))(mc_arr, hh, r2, r2, r2)


def _add2(a, b, name):
    def body(a_ref, b_ref, o_ref):
        o_ref[...] = a_ref[...] + b_ref[...]

    return pl.pallas_call(body, name=name, out_shape=jax.ShapeDtypeStruct(a.shape, a.dtype), compiler_params=_cp())(a, b)


def _sum_chips_ordered(s2, r2, mc_arr, name):
    rows, w = s2.shape
    rh = rows // 2

    def body(mc_ref, s_ref, a_ref, b_ref, c_ref, o_ref):
        me = mc_ref[0]
        acc = None
        for j in range(N_CHIPS):
            rel = jnp.bitwise_xor(me, j)
            v = jnp.where(rel == 0, s_ref[...], jnp.where(rel == 2, a_ref[...], jnp.where(rel == 1, b_ref[...], c_ref[...])))
            acc = v if acc is None else acc + v
        o_ref[...] = acc

    got = lambda k: pl.BlockSpec((None, rh, w), lambda i, mc: (k, 0, 0))
    return pl.pallas_call(
        body, name=name,
        grid_spec=pltpu.PrefetchScalarGridSpec(
            num_scalar_prefetch=1, grid=(1,),
            in_specs=[pl.BlockSpec((rh, w), lambda i, mc: (mc[1], 0)), got(0), got(1), got(2)],
            out_specs=pl.BlockSpec((rh, w), lambda i, mc: (mc[1], 0))),
        out_shape=jax.ShapeDtypeStruct((rows, w), F32), compiler_params=_cp("arbitrary"))(mc_arr, s2, r2, r2, r2)


def _adamw(w, g, m, v, name, comm=None):
    r, c = w.shape
    tr = ROW_TILE if r % ROW_TILE == 0 else r
    c1 = 1.0 / (1.0 - ADAM_B1 ** ADAM_STEP)
    c2 = 1.0 / (1.0 - ADAM_B2 ** ADAM_STEP)
    spec = pl.BlockSpec((tr, c), lambda i: (i, 0))
    host = _Host(comm, [spec] * 4, [spec] * 4, [jax.ShapeDtypeStruct((r, c), F32)] * 4, [])

    def body(*refs):
        (w_ref, g_ref, m_ref, v_ref), (go_ref, d_ref, nm_ref, nv_ref), _ = host.split(refs)
        step = pl.program_id(0)
        host.before(step, r // tr)
        gv = g_ref[...]
        go_ref[...] = gv
        nm = ADAM_B1 * m_ref[...] + (1.0 - ADAM_B1) * gv
        nv = ADAM_B2 * v_ref[...] + (1.0 - ADAM_B2) * (gv * gv)
        nm_ref[...] = nm
        nv_ref[...] = nv
        d_ref[...] = -ADAM_LR * ((nm * c1) / (jnp.sqrt(nv * c2) + ADAM_EPS) + ADAM_WD * w_ref[...])
        host.after(step, r // tr)

    outs = pl.pallas_call(
        body, name=name, grid=(r // tr,), in_specs=host.in_specs, out_specs=host.out_specs, out_shape=host.out_shape,
        scratch_shapes=host.scratch, input_output_aliases=host.aliases,
        compiler_params=_cp("arbitrary"))(w, g, m, v, *host.args)
    return host.results(outs)


def _gather_weights(bigs, pool_w, pack_w, pack_d, name):
    nb = len(bigs)
    smalls = [pool_w, pack_w, pack_d]
    q, cw, cd = pool_w.shape[1], pack_w.shape[1], pack_d.shape[1]
    pieces = [_GatherPlan(bigs, (j, j + 1, GATHER_PIECES)) for j in range(GATHER_PIECES)]
    for j, piece in enumerate(pieces):
        piece.base = 9 + j * piece.nsems

    def body(*refs):
        srcs, dsts = refs[:nb + 3], refs[nb + 3:2 * (nb + 3)]
        ssem, rsem, lsem = refs[2 * (nb + 3):]
        x, y, c, me, chips, sib = _place()

        def small_dst(n, chip):
            if n == 0:
                return dsts[nb].at[:, pl.ds(chip * q, q), :]
            return dsts[nb + n].at[:, pl.ds(chip * (cw if n == 1 else cd), cw if n == 1 else cd)]

        local = [pltpu.make_async_copy(srcs[nb + n], small_dst(n, me), lsem.at[n]) for n in range(3)]
        for cp in local:
            cp.start()
        sends = []
        for n in range(3):
            for k, chip in enumerate(chips):
                cp = _rcopy(srcs[nb + n], small_dst(n, me), ssem.at[3 * n + k], rsem.at[3 * n + k], (*chip, c))
                cp.start()
                sends.append(cp)
        big = (srcs[:nb], dsts[:nb], ssem, rsem)
        for stage in ("start", "relay", "relay_far", "finish"):
            for piece in pieces:
                getattr(piece, stage)(*big)
        for n in range(3):
            for k, chip in enumerate(chips):
                ref = small_dst(n, 2 * chip[0] + chip[1])
                _rcopy(ref, ref, ssem.at[3 * n + k], rsem.at[3 * n + k], (*chip, c)).wait_recv()
        for cp in sends:
            cp.wait_send()
        for cp in local:
            cp.wait()

    nsem = 9 + sum(piece.nsems for piece in pieces)
    out_shape = [jax.ShapeDtypeStruct(b.shape, b.dtype) for b in bigs]
    out_shape += [jax.ShapeDtypeStruct((pool_w.shape[0], N_CHIPS * q, pool_w.shape[2]), pool_w.dtype),
                  jax.ShapeDtypeStruct((pack_w.shape[0], N_CHIPS * cw), pack_w.dtype),
                  jax.ShapeDtypeStruct((pack_d.shape[0], N_CHIPS * cd), pack_d.dtype)]
    return pl.pallas_call(
        body, name=name, in_specs=[ANY] * (nb + 3), out_specs=[ANY] * (nb + 3), out_shape=out_shape,
        input_output_aliases={a: a for a in range(nb)},
        scratch_shapes=[pltpu.SemaphoreType.DMA((nsem,)), pltpu.SemaphoreType.DMA((nsem,)), pltpu.SemaphoreType.DMA((3,))],
        compiler_params=pltpu.CompilerParams(has_side_effects=True))(*bigs, *smalls)


def _swap_with_sibling(grads, wholes, name):
    n, nw = len(grads), len(wholes)
    halves = [g.shape[1] // 2 for g in grads]

    def body(*refs):
        srcs, dsts = refs[:n + nw], refs[n + nw:2 * (n + nw)]
        ssem, rsem = refs[2 * (n + nw):]
        x, y, c, me, chips, sib = _place()
        cps = [_rcopy(srcs[a].at[:, pl.ds((1 - c) * halves[a], halves[a]), :], dsts[a], ssem.at[a], rsem.at[a], sib)
               for a in range(n)]
        cps += [_rcopy(srcs[a], dsts[a], ssem.at[a], rsem.at[a], sib) for a in range(n, n + nw)]
        for cp in cps:
            cp.start()
        for cp in cps:
            cp.wait_recv()
        for cp in cps:
            cp.wait_send()

    out_shape = [jax.ShapeDtypeStruct((g.shape[0], h, g.shape[2]), g.dtype) for g, h in zip(grads, halves)]
    out_shape += [jax.ShapeDtypeStruct(w.shape, w.dtype) for w in wholes]
    return pl.pallas_call(
        body, name=name, in_specs=[ANY] * (n + nw), out_specs=[ANY] * (n + nw), out_shape=out_shape,
        scratch_shapes=[pltpu.SemaphoreType.DMA((n + nw,)), pltpu.SemaphoreType.DMA((n + nw,))],
        compiler_params=pltpu.CompilerParams(has_side_effects=True))(*grads, *wholes)


def _scatter_to_chips(halves_in, small, name):
    n = len(halves_in)
    rh = small.shape[0] // 2

    def body(*refs):
        srcs, dsts = refs[:n + 1], refs[n + 1:2 * (n + 1)]
        ssem, rsem = refs[2 * (n + 1):]
        x, y, c, me, chips, sib = _place()
        cps = []
        for a in range(n + 1):
            for k, chip in enumerate(chips):
                src = srcs[a].at[2 * chip[0] + chip[1]] if a < n else srcs[a].at[pl.ds(c * rh, rh)]
                cps.append(_rcopy(src, dsts[a].at[k], ssem.at[3 * a + k], rsem.at[3 * a + k], (*chip, c)))
        for cp in cps:
            cp.start()
        for cp in cps:
            cp.wait_recv()
        for cp in cps:
            cp.wait_send()

    out_shape = [jax.ShapeDtypeStruct((3,) + h.shape[1:], h.dtype) for h in halves_in]
    out_shape.append(jax.ShapeDtypeStruct((3, rh, small.shape[1]), small.dtype))
    return pl.pallas_call(
        body, name=name, in_specs=[ANY] * (n + 1), out_specs=[ANY] * (n + 1), out_shape=out_shape,
        scratch_shapes=[pltpu.SemaphoreType.DMA((3 * (n + 1),)), pltpu.SemaphoreType.DMA((3 * (n + 1),))],
        compiler_params=pltpu.CompilerParams(has_side_effects=True))(*halves_in, small)


def _join_halves(parts, name):
    n = len(parts)

    def body(*refs):
        srcs, dsts = refs[:n], refs[n:2 * n]
        ssem, rsem = refs[2 * n:]
        x, y, c, me, chips, sib = _place()
        cps = []
        for a in range(n):
            h = srcs[a].shape[0] // 2
            cps.append(_rcopy(srcs[a].at[pl.ds(c * h, h)], dsts[a].at[pl.ds(c * h, h)], ssem.at[a], rsem.at[a], sib))
        for cp in cps:
            cp.start()
        for a in range(n):
            h = srcs[a].shape[0] // 2
            theirs = dsts[a].at[pl.ds((1 - c) * h, h)]
            _rcopy(theirs, theirs, ssem.at[a], rsem.at[a], sib).wait_recv()
        for cp in cps:
            cp.wait_send()

    out_shape = [jax.ShapeDtypeStruct(p.shape, p.dtype) for p in parts]
    return pl.pallas_call(
        body, name=name, in_specs=[ANY] * n, out_specs=[ANY] * n, out_shape=out_shape,
        input_output_aliases={a: a for a in range(n)},
        scratch_shapes=[pltpu.SemaphoreType.DMA((n,)), pltpu.SemaphoreType.DMA((n,))],
        compiler_params=pltpu.CompilerParams(has_side_effects=True))(*parts)


def _scatter_start(h, name):
    land = (3,) + h.shape[1:]

    def body(h_ref, land_ref, send_sems, recv_sems, h_thru, land_thru, token):
        x, y, c, me, chips, sib = _place()
        for k, chip in enumerate(chips):
            _rcopy(h_ref.at[2 * chip[0] + chip[1]], land_ref.at[k], send_sems.at[k], recv_sems.at[k], (*chip, c)).start()
        token[...] = jnp.zeros_like(token)

    hbm = pl.BlockSpec(memory_space=pltpu.HBM)
    sem = pl.BlockSpec(memory_space=pltpu.SEMAPHORE)
    return pl.pallas_call(
        body, name=name,
        out_shape=(pltpu.SemaphoreType.DMA((3,)), pltpu.SemaphoreType.DMA((3,)), pltpu.HBM(h.shape, h.dtype),
                   pltpu.HBM(land, h.dtype), jax.ShapeDtypeStruct((8, LANES), F32)),
        in_specs=(hbm, hbm), out_specs=(sem, sem, hbm, hbm, pl.BlockSpec(memory_space=pltpu.VMEM)),
        input_output_aliases={0: 2, 1: 3},
        compiler_params=pltpu.CompilerParams(has_side_effects=pltpu.SideEffectType.DATAFLOW_SIDE_EFFECTING))(
            pltpu.with_memory_space_constraint(h, pltpu.HBM),
            pltpu.with_memory_space_constraint(lax.empty(land, h.dtype), pltpu.HBM))


def _scatter_wait(send_sems, recv_sems, h_thru, land_thru, after, name):
    def body(h_ref, land_ref, send_sems, recv_sems, after_ref, h_dead, got_ref):
        x, y, c, me, chips, sib = _place()
        for k, chip in enumerate(chips):
            cp = _rcopy(h_ref.at[2 * chip[0] + chip[1]], land_ref.at[k], send_sems.at[k], recv_sems.at[k], (*chip, c))
            cp.wait_send()
            cp.wait_recv()

    hbm = pl.BlockSpec(memory_space=pltpu.HBM)
    sem = pl.BlockSpec(memory_space=pltpu.SEMAPHORE)
    return pl.pallas_call(
        body, name=name,
        out_shape=(pltpu.HBM(h_thru.shape, h_thru.dtype), pltpu.HBM(land_thru.shape, land_thru.dtype)),
        in_specs=(hbm, hbm, sem, sem, ANY), out_specs=(hbm, hbm), input_output_aliases={0: 0, 1: 1},
        compiler_params=pltpu.CompilerParams(has_side_effects=pltpu.SideEffectType.DATAFLOW_SIDE_EFFECTING))(
            h_thru, land_thru, send_sems, recv_sems, after)


def _share_half_start(small, name):
    rh = small.shape[0] // 2
    land = (3, rh, small.shape[1])

    def body(s_ref, land_ref, send_sems, recv_sems, s_thru, land_thru, token):
        x, y, c, me, chips, sib = _place()
        for k, chip in enumerate(chips):
            _rcopy(s_ref.at[pl.ds(c * rh, rh)], land_ref.at[k], send_sems.at[k], recv_sems.at[k], (*chip, c)).start()
        token[...] = jnp.zeros_like(token)

    hbm = pl.BlockSpec(memory_space=pltpu.HBM)
    sem = pl.BlockSpec(memory_space=pltpu.SEMAPHORE)
    return pl.pallas_call(
        body, name=name,
        out_shape=(pltpu.SemaphoreType.DMA((3,)), pltpu.SemaphoreType.DMA((3,)), pltpu.HBM(small.shape, small.dtype),
                   pltpu.HBM(land, small.dtype), jax.ShapeDtypeStruct((8, LANES), F32)),
        in_specs=(hbm, hbm), out_specs=(sem, sem, hbm, hbm, pl.BlockSpec(memory_space=pltpu.VMEM)),
        input_output_aliases={0: 2, 1: 3},
        compiler_params=pltpu.CompilerParams(has_side_effects=pltpu.SideEffectType.DATAFLOW_SIDE_EFFECTING))(
            pltpu.with_memory_space_constraint(small, pltpu.HBM),
            pltpu.with_memory_space_constraint(lax.empty(land, small.dtype), pltpu.HBM))


def _share_half_wait(send_sems, recv_sems, s_thru, land_thru, after, name):
    rh = s_thru.shape[0] // 2

    def body(s_ref, land_ref, send_sems, recv_sems, after_ref, s_dead, got_ref):
        x, y, c, me, chips, sib = _place()
        for k, chip in enumerate(chips):
            cp = _rcopy(s_ref.at[pl.ds(c * rh, rh)], land_ref.at[k], send_sems.at[k], recv_sems.at[k], (*chip, c))
            cp.wait_send()
            cp.wait_recv()

    hbm = pl.BlockSpec(memory_space=pltpu.HBM)
    sem = pl.BlockSpec(memory_space=pltpu.SEMAPHORE)
    return pl.pallas_call(
        body, name=name,
        out_shape=(pltpu.HBM(s_thru.shape, s_thru.dtype), pltpu.HBM(land_thru.shape, land_thru.dtype)),
        in_specs=(hbm, hbm, sem, sem, ANY), out_specs=(hbm, hbm), input_output_aliases={0: 0, 1: 1},
        compiler_params=pltpu.CompilerParams(has_side_effects=pltpu.SideEffectType.DATAFLOW_SIDE_EFFECTING))(
            s_thru, land_thru, send_sems, recv_sems, after)


def _join_start(parts, name):
    n = len(parts)

    def body(*refs):
        srcs, (send_sems, recv_sems), token = refs[:n], refs[n:n + 2], refs[-1]
        x, y, c, me, chips, sib = _place()
        for a, src in enumerate(srcs):
            h = src.shape[0] // 2
            mine = src.at[pl.ds(c * h, h)]
            _rcopy(mine, mine, send_sems.at[a], recv_sems.at[a], sib).start()
        token[...] = jnp.zeros_like(token)

    hbm = pl.BlockSpec(memory_space=pltpu.HBM)
    sem = pl.BlockSpec(memory_space=pltpu.SEMAPHORE)
    outs = pl.pallas_call(
        body, name=name,
        out_shape=(pltpu.SemaphoreType.DMA((n,)), pltpu.SemaphoreType.DMA((n,)))
        + tuple(pltpu.HBM(p.shape, p.dtype) for p in parts) + (jax.ShapeDtypeStruct((8, LANES), F32),),
        in_specs=(hbm,) * n, out_specs=(sem, sem) + (hbm,) * n + (pl.BlockSpec(memory_space=pltpu.VMEM),),
        input_output_aliases={a: 2 + a for a in range(n)},
        compiler_params=pltpu.CompilerParams(has_side_effects=pltpu.SideEffectType.DATAFLOW_SIDE_EFFECTING))(
            *[pltpu.with_memory_space_constraint(p, pltpu.HBM) for p in parts])
    return outs[0], outs[1], list(outs[2:2 + n]), outs[-1]


def _join_wait(send_sems, recv_sems, parts, after, name):
    n = len(parts)

    def body(*refs):
        srcs, (send_sems, recv_sems) = refs[:n], refs[n:n + 2]
        x, y, c, me, chips, sib = _place()
        for a, src in enumerate(srcs):
            h = src.shape[0] // 2
            mine, theirs = src.at[pl.ds(c * h, h)], src.at[pl.ds((1 - c) * h, h)]
            _rcopy(mine, theirs, send_sems.at[a], recv_sems.at[a], sib).wait_send()
            _rcopy(theirs, theirs, send_sems.at[a], recv_sems.at[a], sib).wait_recv()

    hbm = pl.BlockSpec(memory_space=pltpu.HBM)
    sem = pl.BlockSpec(memory_space=pltpu.SEMAPHORE)
    return pl.pallas_call(
        body, name=name, out_shape=tuple(pltpu.HBM(p.shape, p.dtype) for p in parts),
        in_specs=(hbm,) * n + (sem, sem, ANY), out_specs=(hbm,) * n, input_output_aliases={a: a for a in range(n)},
        compiler_params=pltpu.CompilerParams(has_side_effects=pltpu.SideEffectType.DATAFLOW_SIDE_EFFECTING))(
            *parts, send_sems, recv_sems, after)


def _pad_rows(a, rows):
    return jnp.pad(a, ((0, rows - a.shape[0]), (0, 0)))


def _stack_rows(parts, multiple):
    padded = [_pad_rows(p, -(-p.shape[0] // 8) * 8) for p in parts]
    starts, at = [], 0
    for p in padded:
        starts.append(at)
        at += p.shape[0]
    total = -(-at // multiple) * multiple
    if total > at:
        padded.append(jnp.zeros((total - at, parts[0].shape[1]), parts[0].dtype))
    return jnp.concatenate(padded, axis=0), starts


def kernel(x, ln_pre_even, w_in_even, pool_w, pool_scale, w_out_even, ln_post_even, ln_pre_odd, w_in_odd, sconv_w, dconv_w, dconv_b, cnorm_g, cnorm_b, w_out_odd, ln_post_odd, loss_target, m_ln_pre_even, m_w_in_even, m_pool_w, m_pool_scale, m_w_out_even, m_ln_post_even, m_ln_pre_odd, m_w_in_odd, m_sconv_w, m_dconv_w, m_dconv_b, m_cnorm_g, m_cnorm_b, m_w_out_odd, m_ln_post_odd, v_ln_pre_even, v_w_in_even, v_pool_w, v_pool_scale, v_w_out_even, v_ln_post_even, v_ln_pre_odd, v_w_in_odd, v_sconv_w, v_dconv_w, v_dconv_b, v_cnorm_g, v_cnorm_b, v_w_out_odd, v_ln_post_odd):
    _, s, d = x.shape
    half = d // 2
    cw = half // N_CHIPS
    ng, q, gd = pool_w.shape[1:]
    k3, k31 = sconv_w.shape[1], dconv_w.shape[1]
    x2d, tgt = x[0], loss_target[0]
    me = 2 * lax.axis_index("x") + lax.axis_index("y")
    core = lax.axis_index("c")
    c_arr = jnp.reshape(core, (1,)).astype(jnp.int32)
    me_arr = jnp.reshape(me, (1,)).astype(jnp.int32)
    mc_arr = jnp.stack([me, core]).astype(jnp.int32)

    shards = [w_in_even[0], w_out_even[0], w_in_odd[0], w_out_odd[0]]
    slabs = [_cast_bf16_own_slab(w, me_arr, f"cast_w{n}") for n, w in enumerate(shards)]
    pool_w_b = _cast_bf16(pool_w[0].reshape(ng * q, gd), "cast_pool_w").reshape(ng, q, gd)
    pack_w, at_w = _stack_rows([sconv_w[0], dconv_w[0], dconv_b, cnorm_g, cnorm_b], 8)
    pack_d, at_d = _stack_rows([ln_pre_odd, ln_post_odd], 8)
    win_e, pool_w_f, pack_w_f, pack_d_f = _gather_weights(slabs[:1], pool_w_b, pack_w, pack_d, "gather_first")
    sconv_f = pack_w_f[at_w[0]:at_w[0] + k3]
    dconv_f = pack_w_f[at_w[1]:at_w[1] + k31]
    dconv_b_f, cnorm_g_f, cnorm_b_f = (pack_w_f[at_w[n]:at_w[n] + 1] for n in (2, 3, 4))
    ln_pre_odd_f = pack_d_f[at_d[0]:at_d[0] + 1]
    ln_post_odd_f = pack_d_f[at_d[1]:at_d[1] + 1]

    def reduce_half(g, name):
        (got,) = _swap_with_sibling([g], [], "swap_" + name)
        return _half_add(g, got, c_arr, "half_add_" + name)

    h0 = _rms_fwd(x2d, ln_pre_even, "rms_pre_even")
    plans = _Multi([_GatherPlan([slabs[1]], at=(0.6, 0.88)), _GatherPlan([slabs[2]], (0, 1, 4), at=(0.6, 0.88))])
    p_e, extra = _mm_nn(h0, win_e, "proj_in_even", plans)
    (wout_e,), (win_o,) = plans.results(extra)
    wout_e = wout_e.reshape(d, d)
    att, ltot, (win_o,) = _sba_fwd(p_e, half, "sba_fwd", _GatherPlan([win_o], (1, 4, 4), at=(0.69, 0.94)))
    y_e = _even_mix_fwd(p_e, att, pool_w_f, pool_scale, d, "even_mix_fwd")
    o_e, x1, h1 = _mm_out_even(y_e, wout_e, x2d, ln_post_even, ln_pre_odd_f, "proj_out_even")
    p_o, (wout_o,) = _mm_nn(h1, win_o, "proj_in_odd", _GatherPlan([slabs[3]]))
    wout_o = wout_o.reshape(d, d)
    y_o, s3, d1 = _odd_mix_fwd(p_o, sconv_f, dconv_f, dconv_b_f, cnorm_g_f, cnorm_b_f, d, "odd_mix_fwd")
    do_o, dx2, loss_blk, dln_post_odd = _mm_out_odd(y_o, wout_o, x1, ln_post_odd_f, tgt, "proj_out_odd_loss")

    dy_o = _mm_nt(do_o, wout_o, "dy_odd")
    g_wout_o = _mm_tn(y_o, do_o, 1, "dw_out_odd")[0].reshape(N_CHIPS, d // N_CHIPS, d)
    (dbc, dgate_o, ds3, dd1, dcnorm_g, dcnorm_b, ddconv_b), (got,) = _odd_bwd_rows(
        p_o, s3, d1, dy_o, cnorm_g_f, cnorm_b_f, d, "odd_bwd_rows", _SwapPlan([g_wout_o]))
    h_wout_o = _half_add(g_wout_o, got, c_arr, "half_add_out_odd")
    dhc, dcc, dga, dgb, dsconv, ddconv = _odd_bwd_conv(p_o, ds3, dd1, sconv_f, dconv_f, d, "odd_bwd_conv")
    dp_o = jnp.concatenate([dhc, dbc, dcc, dga, dgb, dgate_o], axis=1)
    g_win_o, (s_wout_o,) = _mm_tn(h1, dp_o, N_CHIPS, "dw_in_odd", _ScatterPlan([h_wout_o]))
    (dx1, dln_pre_odd, do_e, dln_post_even), (got,) = _mm_in_bwd(
        dp_o, win_o, x1, ln_pre_odd_f, dx2, (o_e, ln_post_even), "dx_odd", _SwapPlan([g_win_o]))
    h_win_o = _half_add(g_win_o, got, c_arr, "half_add_in_odd")

    dy_e = _mm_nt(do_e, wout_e, "dy_even")
    g_wout_e = _mm_tn(y_e, do_e, 1, "dw_out_even")[0].reshape(N_CHIPS, d // N_CHIPS, d)
    (datt, du, dgate_e, dpool_scale, dpool_w), (got,) = _even_mix_bwd(
        p_e, att, dy_e, pool_w_f, pool_scale, d, "even_mix_bwd", _SwapPlan([g_wout_e]))
    h_wout_e = _half_add(g_wout_e, got, c_arr, "half_add_out_even")
    two = lambda v: v.reshape(2, half)
    small_parts = [dpool_scale, two(dln_post_even), two(dln_pre_odd), two(dln_post_odd),
                   dsconv, ddconv, ddconv_b, dcnorm_g, dcnorm_b, dpool_w.reshape(gd, half)]
    small, at_s = _stack_rows(small_parts, 16)
    plans = _Multi([_ScatterPlan([h_win_o]), _SendWholePlan([small])])
    dq, dk, dv, extra = _sba_bwd(p_e, ltot, datt, half, "sba_bwd", plans)
    (s_win_o,), (small1,) = plans.results(extra)
    small2 = _add2(small, small1, "small_add")
    dp_e = jnp.concatenate([dq, dk, dv, du, dgate_e], axis=1)
    plans = _Multi([_ScatterPlan([h_wout_e]), _ShareHalfPlan([small2])])
    g_win_e, extra = _mm_tn(h0, dp_e, N_CHIPS, "dw_in_even", plans)
    (s_wout_e,), (small_got,) = plans.results(extra)
    h_win_e = reduce_half(g_win_e, "in_even")
    send_sems, recv_sems, h_win_e, landing, token = _scatter_start(h_win_e, "scatter_in_even_start")
    (grad_x, dln_pre_even), _ = _mm_in_bwd(dp_e, win_e, x2d, ln_pre_even + token[0:1, 0:1], dx1, None, "dx_even")

    last, at_l = _stack_rows([two(dln_pre_even), jnp.pad(loss_blk[0:1], ((0, 0), (0, half - LANES)))], 16)
    (last1,) = _swap_with_sibling([], [last], "swap_last")
    last2 = _add2(last, last1, "last_add")
    share = _share_half_start(last2, "share_last_start")
    pairs = [(h_wout_e, s_wout_e), (h_win_o, s_win_o), (h_wout_o, s_wout_o)]
    parts = [_sum_chips(h, r, mc_arr, f"sum_chips{n + 1}") for n, (h, r) in enumerate(pairs)]
    parts.append(_sum_chips_ordered(small2, small_got, mc_arr, "small_sum"))
    last2, last_got = _share_half_wait(*share[:4], parts[-1], "share_last_wait")
    parts.append(_sum_chips_ordered(last2, last_got, mc_arr, "last_sum"))
    join_sems = _join_start(parts, "join_first_start")
    h_win_e, s_win_e = _scatter_wait(send_sems, recv_sems, h_win_e, landing, join_sems[3], "scatter_in_even_wait")
    last_part = _sum_chips(h_win_e, s_win_e, mc_arr, "sum_chips0")
    last_sems = _join_start([last_part], "join_last_start")
    gw_out_e, gw_in_o, gw_out_o, red, red_last = _join_wait(*join_sems[:3], last_sems[3], "join_first_wait")
    loss = red_last[at_l[1], 0]

    def rows(n, cnt):
        return red[at_s[n]:at_s[n] + cnt]

    def mine(a, width):
        return lax.dynamic_slice_in_dim(a, me * width, width, axis=1)

    quarter = d // N_CHIPS
    g_small = {
        "ln_pre_even": red_last[at_l[0]:at_l[0] + 2].reshape(1, d),
        "pool_scale": rows(0, 1),
        "ln_post_even": rows(1, 2).reshape(1, d),
        "ln_pre_odd": mine(rows(2, 2).reshape(1, d), quarter),
        "ln_post_odd": mine(rows(3, 2).reshape(1, d), quarter),
        "sconv_w": mine(rows(4, k3), cw),
        "dconv_w": mine(rows(5, k31), cw),
        "dconv_b": mine(rows(6, 1), cw),
        "cnorm_g": mine(rows(7, 1), cw),
        "cnorm_b": mine(rows(8, 1), cw),
        "pool_w": lax.dynamic_slice_in_dim(rows(9, gd).reshape(ng, gd, gd), me * q, q, axis=1).reshape(ng * q, gd),
    }
    w2d = {
        "ln_pre_even": ln_pre_even, "w_in_even": w_in_even[0], "pool_w": pool_w[0].reshape(ng * q, gd),
        "pool_scale": pool_scale, "w_out_even": w_out_even[0], "ln_post_even": ln_post_even, "ln_pre_odd": ln_pre_odd,
        "w_in_odd": w_in_odd[0], "sconv_w": sconv_w[0], "dconv_w": dconv_w[0], "dconv_b": dconv_b, "cnorm_g": cnorm_g,
        "cnorm_b": cnorm_b, "w_out_odd": w_out_odd[0], "ln_post_odd": ln_post_odd,
    }
    moments = {
        "ln_pre_even": (m_ln_pre_even, v_ln_pre_even), "w_in_even": (m_w_in_even, v_w_in_even),
        "pool_w": (m_pool_w, v_pool_w), "pool_scale": (m_pool_scale, v_pool_scale),
        "w_out_even": (m_w_out_even, v_w_out_even), "ln_post_even": (m_ln_post_even, v_ln_post_even),
        "ln_pre_odd": (m_ln_pre_odd, v_ln_pre_odd), "w_in_odd": (m_w_in_odd, v_w_in_odd),
        "sconv_w": (m_sconv_w, v_sconv_w), "dconv_w": (m_dconv_w, v_dconv_w), "dconv_b": (m_dconv_b, v_dconv_b),
        "cnorm_g": (m_cnorm_g, v_cnorm_g), "cnorm_b": (m_cnorm_b, v_cnorm_b),
        "w_out_odd": (m_w_out_odd, v_w_out_odd), "ln_post_odd": (m_ln_post_odd, v_ln_post_odd),
    }
    def update(name, g):
        m_in, v_in = moments[name]
        w = w2d[name]
        return _adamw(w, g, m_in.reshape(w.shape), v_in.reshape(w.shape), "adamw_" + name)[0]

    updates = {"w_in_odd": update("w_in_odd", gw_in_o)}
    (gw_in_e,) = _join_wait(*last_sems[:3], updates["w_in_odd"][1], "join_last_wait")
    for name, g in dict(g_small, w_in_even=gw_in_e, w_out_even=gw_out_e, w_out_odd=gw_out_o).items():
        updates[name] = update(name, g)
    outs = [[u.reshape(moments[name][0].shape) for u in updates[name]] for name in w2d]
    grads_out, deltas, new_m, new_v = zip(*outs)
    return (loss, grad_x.reshape(x.shape), *grads_out, *deltas, *new_m, *new_v)
```

```python
import functools
import math

import jax
import jax.numpy as jnp
from jax import lax
from jax.experimental import pallas as pl
from jax.experimental.pallas import tpu as pltpu

F32 = jnp.float32
BF16 = jnp.bfloat16
EPS = 1e-6
N_CHIPS = 4
VMEM_LIMIT_V7X = 56 << 20
HEAD_DIM = 128
ATT_BLOCK = 256
POOL_WINDOWS = (2, 4, 8, 16)
ROW_TILE = 256
POOL_HALO = 16
CONV_HALO = 32
LANES = 128
ADAM_LR, ADAM_B1, ADAM_B2, ADAM_EPS, ADAM_WD, ADAM_STEP = 0.001, 0.9, 0.999, 1e-08, 0.01, 10
MESH_ID = pl.DeviceIdType.MESH
ANY = pl.BlockSpec(memory_space=pl.ANY)


def _cp(*sem):
    return pltpu.CompilerParams(dimension_semantics=sem or None, vmem_limit_bytes=VMEM_LIMIT_V7X)


def _pick_tile(n, cap):
    best = None
    for t in range(LANES, min(n, cap) + 1, LANES):
        if n % t == 0:
            best = t
    assert best is not None, (n, cap)
    return best


def _sigmoid(x):
    return 1.0 / (1.0 + jnp.exp(-x))


def _silu(x):
    return x * _sigmoid(x)


def _dsilu(x):
    s = _sigmoid(x)
    return s * (1.0 + x * (1.0 - s))


def _log_sigmoid(z):
    return jnp.minimum(z, 0.0) - jnp.log(1.0 + jnp.exp(-jnp.abs(z)))


def _rms_stats(x):
    r = lax.rsqrt(jnp.mean(x * x, axis=-1, keepdims=True) + EPS)
    return x * r, r


def _rms_bwd(dh, xhat, r, g):
    dxh = dh * g
    dx = r * (dxh - xhat * jnp.mean(dxh * xhat, axis=-1, keepdims=True))
    return dx, jnp.sum(dh * xhat, axis=0, keepdims=True)


def _acc_rows(ref, first, val):
    @pl.when(first)
    def _():
        ref[...] = val

    @pl.when(jnp.logical_not(first))
    def _():
        ref[...] += val


def _rcopy(src, dst, ssem, rsem, dev):
    return pltpu.make_async_remote_copy(src_ref=src, dst_ref=dst, send_sem=ssem, recv_sem=rsem,
                                        device_id=dev, device_id_type=MESH_ID)


def _place():
    x, y, c = lax.axis_index("x"), lax.axis_index("y"), lax.axis_index("c")
    chips = [(1 - x, y), (x, 1 - y), (1 - x, 1 - y)]
    return x, y, c, 2 * x + y, chips, (x, y, 1 - c)


class _GatherPlan:
    PER_ARRAY = 7

    def __init__(self, arrays, part=(0, 1, 1), at=(0.5, 0.8)):
        self.operands = list(arrays)
        self.out_shapes = [jax.ShapeDtypeStruct(a.shape, a.dtype) for a in arrays]
        self.aliases = {i: i for i in range(len(arrays))}
        self.nsems = self.PER_ARRAY * len(arrays)
        self.base = 0
        self.halves = [a.shape[1] // 2 for a in arrays]
        self.part = part
        self.at = at

    def schedule(self):
        return [(0.0, self.start), (self.at[0], self.relay), (self.at[1], self.relay_far)]

    def _rows(self, ref, a, chip, half, quarter=None):
        lo, hi, n = self.part
        h = self.halves[a]
        first, size = half * h + lo * h // n, (hi - lo) * h // n
        if quarter is not None:
            first, size = first + quarter * (size // 2), size // 2
        return ref.at[chip, pl.ds(first, size)]

    def _copy(self, src, dst, a, n, ssem, rsem, dev):
        return _rcopy(src, dst, ssem.at[self.base + self.PER_ARRAY * a + n], rsem.at[self.base + self.PER_ARRAY * a + n], dev)

    def _own(self, ins, outs, ssem, rsem):
        x, y, c, me, chips, sib = _place()
        return [self._copy(self._rows(ins[a], a, me, c), self._rows(outs[a], a, me, c), a, k, ssem, rsem, (*chips[k], c))
                for a in range(len(ins)) for k in (0, 1)]

    def _relays(self, outs, ssem, rsem, a, k):
        x, y, c, me, chips, sib = _place()
        chip = 2 * chips[k][0] + chips[k][1]
        whole, quarter = self._rows(outs[a], a, chip, c), self._rows(outs[a], a, chip, c, k)
        return (self._copy(whole, whole, a, k, ssem, rsem, (*chips[k], c)),
                self._copy(quarter, quarter, a, 2 + k, ssem, rsem, (*chips[1 - k], c)),
                self._copy(whole, whole, a, 4 + k, ssem, rsem, sib))

    def _far(self, outs, ssem, rsem, a):
        x, y, c, me, chips, sib = _place()
        chip = 2 * chips[2][0] + chips[2][1]
        whole = self._rows(outs[a], a, chip, c)
        got = [self._copy(q, q, a, 2 + k, ssem, rsem, (*chips[1 - k], c))
               for k, q in enumerate([self._rows(outs[a], a, chip, c, 0), self._rows(outs[a], a, chip, c, 1)])]
        return got, self._copy(whole, whole, a, 6, ssem, rsem, sib)

    def start(self, ins, outs, ssem, rsem):
        for cp in self._own(ins, outs, ssem, rsem):
            cp.start()

    def relay(self, ins, outs, ssem, rsem):
        for a in range(len(outs)):
            for k in (0, 1):
                landed, onward, to_sibling = self._relays(outs, ssem, rsem, a, k)
                landed.wait_recv()
                onward.start()
                to_sibling.start()

    def relay_far(self, ins, outs, ssem, rsem):
        for a in range(len(outs)):
            got, to_sibling = self._far(outs, ssem, rsem, a)
            for cp in got:
                cp.wait_recv()
            to_sibling.start()

    def finish(self, ins, outs, ssem, rsem):
        x, y, c, me, chips, sib = _place()
        for a in range(len(outs)):
            for k in range(3):
                ref = self._rows(outs[a], a, 2 * chips[k][0] + chips[k][1], 1 - c)
                self._copy(ref, ref, a, 4 + k, ssem, rsem, sib).wait_recv()
        for cp in self._own(ins, outs, ssem, rsem):
            cp.wait_send()
        for a in range(len(outs)):
            for k in (0, 1):
                _, onward, to_sibling = self._relays(outs, ssem, rsem, a, k)
                onward.wait_send()
                to_sibling.wait_send()
            self._far(outs, ssem, rsem, a)[1].wait_send()


class _ScatterPlan:
    def __init__(self, arrays, part=(0, 1, 1), into=None):
        self.n = len(arrays)
        self.operands = list(arrays) + list(into or [])
        self.out_shapes = [jax.ShapeDtypeStruct((3,) + a.shape[1:], a.dtype) for a in arrays]
        self.aliases = {self.n + i: i for i in range(self.n)} if into else {}
        self.nsems = 3 * self.n
        self.base = 0
        self.part = part

    def _copies(self, ins, outs, ssem, rsem):
        x, y, c, me, chips, sib = _place()
        lo, hi, n = self.part
        out = []
        for a in range(self.n):
            h = ins[a].shape[1]
            rows = pl.ds(lo * h // n, (hi - lo) * h // n)
            for k, chip in enumerate(chips):
                out.append(_rcopy(ins[a].at[2 * chip[0] + chip[1], rows], outs[a].at[k, rows],
                                  ssem.at[self.base + 3 * a + k], rsem.at[self.base + 3 * a + k], (*chip, c)))
        return out

    def schedule(self):
        return [(0.0, self.start)]

    def start(self, ins, outs, ssem, rsem):
        for cp in self._copies(ins, outs, ssem, rsem):
            cp.start()

    def finish(self, ins, outs, ssem, rsem):
        cps = self._copies(ins, outs, ssem, rsem)
        for cp in cps:
            cp.wait_recv()
        for cp in cps:
            cp.wait_send()


class _ShareHalfPlan(_ScatterPlan):
    def __init__(self, arrays):
        super().__init__(arrays)
        self.out_shapes = [jax.ShapeDtypeStruct((3, a.shape[0] // 2, a.shape[1]), a.dtype) for a in arrays]

    def _copies(self, ins, outs, ssem, rsem):
        x, y, c, me, chips, sib = _place()
        out = []
        for a in range(self.n):
            rh = ins[a].shape[0] // 2
            for k, chip in enumerate(chips):
                out.append(_rcopy(ins[a].at[pl.ds(c * rh, rh)], outs[a].at[k],
                                  ssem.at[self.base + 3 * a + k], rsem.at[self.base + 3 * a + k], (*chip, c)))
        return out


class _SwapPlan:
    def __init__(self, grads):
        self.operands = list(grads)
        self.out_shapes = [jax.ShapeDtypeStruct((g.shape[0], g.shape[1] // 2, g.shape[2]), g.dtype) for g in grads]
        self.aliases = {}
        self.nsems = len(grads)
        self.base = 0

    def _copies(self, ins, outs, ssem, rsem):
        x, y, c, me, chips, sib = _place()
        out = []
        for a, src in enumerate(ins):
            h = src.shape[1] // 2
            out.append(_rcopy(src.at[:, pl.ds((1 - c) * h, h), :], outs[a], ssem.at[self.base + a], rsem.at[self.base + a], sib))
        return out

    def schedule(self):
        return [(0.0, self.start)]

    def start(self, ins, outs, ssem, rsem):
        for cp in self._copies(ins, outs, ssem, rsem):
            cp.start()

    def finish(self, ins, outs, ssem, rsem):
        cps = self._copies(ins, outs, ssem, rsem)
        for cp in cps:
            cp.wait_recv()
        for cp in cps:
            cp.wait_send()


class _SendWholePlan(_SwapPlan):
    def __init__(self, arrays):
        self.operands = list(arrays)
        self.out_shapes = [jax.ShapeDtypeStruct(a.shape, a.dtype) for a in arrays]
        self.aliases = {}
        self.nsems = len(arrays)
        self.base = 0

    def _copies(self, ins, outs, ssem, rsem):
        x, y, c, me, chips, sib = _place()
        return [_rcopy(src, outs[a], ssem.at[self.base + a], rsem.at[self.base + a], sib) for a, src in enumerate(ins)]


class _Multi:
    def __init__(self, plans):
        self.plans = plans
        self.operands, self.out_shapes, self.aliases, self.nsems = [], [], {}, 0
        self.spans = []
        for p in plans:
            ni, no = len(self.operands), len(self.out_shapes)
            self.spans.append((ni, ni + len(p.operands), no, no + len(p.out_shapes)))
            self.aliases.update({ni + i: no + j for i, j in p.aliases.items()})
            p.base = self.nsems
            self.nsems += p.nsems
            self.operands += p.operands
            self.out_shapes += p.out_shapes

    def schedule(self):
        def bound(fn, span):
            i0, i1, o0, o1 = span
            return lambda ins, outs, ssem, rsem: fn(ins[i0:i1], outs[o0:o1], ssem, rsem)

        stages = [(at, bound(fn, span)) for p, span in zip(self.plans, self.spans) for at, fn in p.schedule()]
        return sorted(stages, key=lambda s: s[0])

    def finish(self, ins, outs, ssem, rsem):
        for p, (i0, i1, o0, o1) in zip(self.plans, self.spans):
            p.finish(ins[i0:i1], outs[o0:o1], ssem, rsem)

    def results(self, extra):
        return [list(extra[o0:o1]) for (_, _, o0, o1) in self.spans]


class _Host:
    def __init__(self, comm, in_specs, out_specs, out_shape, scratch):
        self.comm = comm
        self.n_in, self.n_out = len(in_specs), len(out_specs)
        self.in_specs, self.out_specs, self.out_shape, self.scratch = list(in_specs), list(out_specs), list(out_shape), list(scratch)
        self.aliases = {}
        self.args = []
        if comm is not None:
            self.in_specs += [ANY] * len(comm.operands)
            self.out_specs += [ANY] * len(comm.out_shapes)
            self.out_shape += comm.out_shapes
            self.scratch += [pltpu.SemaphoreType.DMA((comm.nsems,)), pltpu.SemaphoreType.DMA((comm.nsems,))]
            self.aliases = {self.n_in + i: self.n_out + j for i, j in comm.aliases.items()}
            self.args = list(comm.operands)

    def split(self, refs):
        nc = len(self.args)
        nco = len(self.out_shape) - self.n_out
        ins, p = refs[:self.n_in], self.n_in + nc
        outs, rest = refs[p:p + self.n_out], refs[p + self.n_out + nco:]
        self._cargs = None
        if self.comm is not None:
            self._cargs = (refs[self.n_in:p], refs[p + self.n_out:p + self.n_out + nco], rest[-2], rest[-1])
            rest = rest[:-2]
        return ins, outs, rest

    def before(self, step, total):
        if self.comm is None:
            return

        for at, stage in self.comm.schedule():
            pl.when(step == min(total - 1, int(at * total)))(functools.partial(stage, *self._cargs))

    def after(self, step, total):
        if self.comm is None:
            return

        @pl.when(step == total - 1)
        def _():
            self.comm.finish(*self._cargs)

    def results(self, outs):
        return outs[:self.n_out], outs[self.n_out:]


def _cast_bf16(x, name):
    r, c = x.shape
    tr = ROW_TILE if r % ROW_TILE == 0 else r

    def body(x_ref, o_ref):
        o_ref[...] = x_ref[...].astype(BF16)

    return pl.pallas_call(
        body, name=name, grid=(r // tr,),
        in_specs=[pl.BlockSpec((tr, c), lambda i: (i, 0))],
        out_specs=pl.BlockSpec((tr, c), lambda i: (i, 0)),
        out_shape=jax.ShapeDtypeStruct((r, c), BF16), compiler_params=_cp("parallel"))(x)


def _cast_bf16_own_slab(x, me_arr, name):
    r, c = x.shape
    tr = ROW_TILE if r % ROW_TILE == 0 else r

    def body(me_ref, x_ref, o_ref):
        o_ref[...] = x_ref[...].astype(BF16)

    return pl.pallas_call(
        body, name=name,
        grid_spec=pltpu.PrefetchScalarGridSpec(
            num_scalar_prefetch=1, grid=(r // tr,),
            in_specs=[pl.BlockSpec((tr, c), lambda i, me: (i, 0))],
            out_specs=pl.BlockSpec((None, tr, c), lambda i, me: (me[0], i, 0))),
        out_shape=jax.ShapeDtypeStruct((N_CHIPS, r, c), BF16), compiler_params=_cp("parallel"))(me_arr, x)


def _rms_fwd(x, g, name):
    s, d = x.shape

    def body(x_ref, g_ref, h_ref):
        xhat, _ = _rms_stats(x_ref[...])
        h_ref[...] = (xhat * g_ref[...]).astype(BF16)

    return pl.pallas_call(
        body, name=name, grid=(s // ROW_TILE,),
        in_specs=[pl.BlockSpec((ROW_TILE, d), lambda i: (i, 0)), pl.BlockSpec((1, d), lambda i: (0, 0))],
        out_specs=pl.BlockSpec((ROW_TILE, d), lambda i: (i, 0)),
        out_shape=jax.ShapeDtypeStruct((s, d), BF16), compiler_params=_cp("parallel"))(x, g)


def _mm_nn(a, w3, name, comm=None):
    m, k = a.shape
    nsh, _, ns = w3.shape
    tm = 512 if m % 512 == 0 else ROW_TILE
    tn = _pick_tile(ns, 1024)
    per = ns // tn
    grid = (nsh * per, m // tm)
    host = _Host(comm,
                 [pl.BlockSpec((tm, k), lambda n, i: (i, 0)), pl.BlockSpec((None, k, tn), lambda n, i: (n // per, 0, n % per))],
                 [pl.BlockSpec((tm, tn), lambda n, i: (i, n))], [jax.ShapeDtypeStruct((m, nsh * ns), F32)], [])

    def body(*refs):
        (a_ref, w_ref), (o_ref,), _ = host.split(refs)
        step = pl.program_id(0) * grid[1] + pl.program_id(1)
        host.before(step, grid[0] * grid[1])
        o_ref[...] = jnp.dot(a_ref[...], w_ref[...], preferred_element_type=F32)
        host.after(step, grid[0] * grid[1])

    outs = pl.pallas_call(
        body, name=name, grid=grid, in_specs=host.in_specs, out_specs=host.out_specs, out_shape=host.out_shape,
        scratch_shapes=host.scratch, input_output_aliases=host.aliases,
        compiler_params=_cp("arbitrary", "arbitrary"))(a, w3, *host.args)
    (out,), extra = host.results(outs)
    return out, extra


def _mm_nt(a, b, name):
    m, k = a.shape
    n = b.shape[0]
    tm = 512 if m % 512 == 0 else ROW_TILE

    def body(a_ref, b_ref, o_ref):
        o_ref[...] = lax.dot_general(a_ref[...], b_ref[...], (((1,), (1,)), ((), ())), preferred_element_type=F32)

    return pl.pallas_call(
        body, name=name, grid=(m // tm,),
        in_specs=[pl.BlockSpec((tm, k), lambda i: (i, 0)), pl.BlockSpec((n, k), lambda i: (0, 0))],
        out_specs=pl.BlockSpec((tm, n), lambda i: (i, 0)),
        out_shape=jax.ShapeDtypeStruct((m, n), F32), compiler_params=_cp("parallel"))(a, b)


def _mm_tn(a, b, nsh, name, comm=None):
    s, m = a.shape
    n = b.shape[1]
    ns = n // nsh
    tm = 512 if m % 512 == 0 else ROW_TILE
    tn = _pick_tile(ns, 1024)
    per = ns // tn
    grid = (nsh * per, m // tm)
    host = _Host(comm, [pl.BlockSpec((s, tm), lambda j, i: (0, i)), pl.BlockSpec((s, tn), lambda j, i: (0, j))],
                 [pl.BlockSpec((None, tm, tn), lambda j, i: (j // per, i, j % per))],
                 [jax.ShapeDtypeStruct((nsh, m, ns), BF16)], [])

    def body(*refs):
        (a_ref, b_ref), (o_ref,), _ = host.split(refs)
        step = pl.program_id(0) * grid[1] + pl.program_id(1)
        host.before(step, grid[0] * grid[1])
        o_ref[...] = lax.dot_general(a_ref[...], b_ref[...], (((0,), (0,)), ((), ())),
                                     preferred_element_type=F32).astype(BF16)
        host.after(step, grid[0] * grid[1])

    outs = pl.pallas_call(
        body, name=name, grid=grid, in_specs=host.in_specs, out_specs=host.out_specs, out_shape=host.out_shape,
        scratch_shapes=host.scratch, input_output_aliases=host.aliases,
        compiler_params=_cp("arbitrary", "arbitrary"))(a, b, *host.args)
    (out,), extra = host.results(outs)
    return out, extra


def _tri(n, rel):
    row = lax.broadcasted_iota(jnp.int32, (2 * n, n), 0)
    col = lax.broadcasted_iota(jnp.int32, (2 * n, n), 1)
    return jnp.where(rel(jnp.where(row >= n, row - n, row), col), 1.0, 0.0).astype(BF16)


def _dot_split(x, tri2):
    hi = x.astype(BF16)
    lo = (x - hi.astype(F32)).astype(BF16)
    return jnp.dot(jnp.concatenate([hi, lo], axis=1), tri2, preferred_element_type=F32)


def _nt(a, b):
    return lax.dot_general(a, b, (((1,), (1,)), ((), ())), preferred_element_type=F32)


def _tn(a, b):
    return lax.dot_general(a, b, (((0,), (0,)), ((), ())), preferred_element_type=F32)


def _heads_per_step(nh):
    return max(h for h in (1, 2, 4) if nh % h == 0)


def _sba_fwd(p, sbw, name, comm=None):
    s = p.shape[0]
    nh = sbw // HEAD_DIM
    hp = _heads_per_step(nh)
    ngrp, hw = nh // hp, hp * HEAD_DIM
    blk = ATT_BLOCK
    nq = s // blk
    scale = 1.0 / math.sqrt(HEAD_DIM)
    host = _Host(comm,
                 [pl.BlockSpec((blk, hw), lambda g, i: (i, g)),
                  pl.BlockSpec((s, hw), lambda g, i: (0, ngrp + g)),
                  pl.BlockSpec((s, hw), lambda g, i: (0, 2 * ngrp + g))],
                 [pl.BlockSpec((blk, hw), lambda g, i: (i, g))] * 2,
                 [jax.ShapeDtypeStruct((s, sbw), F32)] * 2,
                 [pltpu.VMEM((s, hw), BF16)] * 2)

    def body(*refs):
        (q_ref, k_ref, v_ref), (o_ref, lt_ref), (kb_ref, vb_ref) = host.split(refs)
        i = pl.program_id(1)
        step = pl.program_id(0) * nq + i
        host.before(step, ngrp * nq)

        @pl.when(i == 0)
        def _():
            kb_ref[...] = k_ref[...].astype(BF16)
            vb_ref[...] = v_ref[...].astype(BF16)

        heads = [slice(h * HEAD_DIM, (h + 1) * HEAD_DIM) for h in range(hp)]
        qs = [q_ref[:, hd].astype(BF16) for hd in heads]
        later = _tri(blk, lambda r, c: r > c)
        causal = lax.broadcasted_iota(jnp.int32, (blk, blk), 1) < lax.broadcasted_iota(jnp.int32, (blk, blk), 0)

        def key_block(j, carry, diagonal):
            rows = pl.ds(pl.multiple_of(j * blk, blk), blk)
            hs = range(hp)
            z = [_nt(qs[h], kb_ref[rows, heads[h]]) * scale for h in hs]
            ls = [_log_sigmoid(z[h]) for h in hs]
            lm = [jnp.where(causal, ls[h] - z[h], 0.0) if diagonal else ls[h] - z[h] for h in hs]
            stay = [_dot_split(lm[h], later) for h in hs]
            w = [jnp.exp(ls[h] + stay[h] + carry[h][1]) for h in hs]
            if diagonal:
                w = [jnp.where(causal, w[h], 0.0) for h in hs]
            acc = [carry[h][0] + jnp.dot(w[h].astype(BF16), vb_ref[rows, heads[h]], preferred_element_type=F32) for h in hs]
            return tuple((acc[h], carry[h][1] + jnp.sum(lm[h], axis=1, keepdims=True)) for h in hs)

        init = tuple((jnp.zeros((blk, HEAD_DIM), F32), jnp.zeros((blk, 1), F32)) for _ in heads)
        carry = key_block(i, init, True)
        carry = lax.fori_loop(0, i, lambda n, c: key_block(i - 1 - n, c, False), carry)
        for h, hd in enumerate(heads):
            o_ref[:, hd] = carry[h][0]
            lt_ref[:, hd] = jnp.broadcast_to(carry[h][1], (blk, HEAD_DIM))
        host.after(step, ngrp * nq)

    outs = pl.pallas_call(
        body, name=name, grid=(ngrp, nq), in_specs=host.in_specs, out_specs=host.out_specs, out_shape=host.out_shape,
        scratch_shapes=host.scratch, input_output_aliases=host.aliases,
        compiler_params=_cp("arbitrary", "arbitrary"))(p, p, p, *host.args)
    (out, ltot), extra = host.results(outs)
    return out, ltot, extra


def _sba_bwd(p, ltot, dout, sbw, name, comm=None):
    s = p.shape[0]
    nh = sbw // HEAD_DIM
    hp = _heads_per_step(nh)
    ngrp, hw = nh // hp, hp * HEAD_DIM
    blk = ATT_BLOCK
    nq = s // blk
    scale = 1.0 / math.sqrt(HEAD_DIM)
    blk_spec = pl.BlockSpec((blk, hw), lambda g, i: (i, g))
    col_spec = pl.BlockSpec((s, hw), lambda g, i: (0, g))
    host = _Host(comm,
                 [blk_spec, pl.BlockSpec((s, hw), lambda g, i: (0, ngrp + g)),
                  pl.BlockSpec((s, hw), lambda g, i: (0, 2 * ngrp + g)), blk_spec, blk_spec],
                 [blk_spec, col_spec, col_spec], [jax.ShapeDtypeStruct((s, sbw), BF16)] * 3,
                 [pltpu.VMEM((s, hw), BF16)] * 2 + [pltpu.VMEM((s, hw), F32)] * 2)

    def body(*refs):
        (q_ref, k_ref, v_ref, lt_ref, do_ref), (dq_ref, dk_ref, dv_ref), (kb_ref, vb_ref, dka_ref, dva_ref) = host.split(refs)
        i = pl.program_id(1)
        step = pl.program_id(0) * nq + i
        host.before(step, ngrp * nq)

        @pl.when(i == 0)
        def _():
            kb_ref[...] = k_ref[...].astype(BF16)
            vb_ref[...] = v_ref[...].astype(BF16)
            dka_ref[...] = jnp.zeros_like(dka_ref)
            dva_ref[...] = jnp.zeros_like(dva_ref)

        heads = [slice(h * HEAD_DIM, (h + 1) * HEAD_DIM) for h in range(hp)]
        qs = [q_ref[:, hd].astype(BF16) for hd in heads]
        dos = [do_ref[:, hd].astype(BF16) for hd in heads]
        ltots = [lt_ref[:, h * HEAD_DIM:h * HEAD_DIM + 1] for h in range(hp)]
        upto = _tri(blk, lambda r, c: r <= c)
        before = _tri(blk, lambda r, c: r < c)
        causal = lax.broadcasted_iota(jnp.int32, (blk, blk), 1) < lax.broadcasted_iota(jnp.int32, (blk, blk), 0)

        def key_block(j, carry, diagonal):
            rows = pl.ds(pl.multiple_of(j * blk, blk), blk)
            hs = range(hp)
            kj = [kb_ref[rows, heads[h]] for h in hs]
            vj = [vb_ref[rows, heads[h]] for h in hs]
            z = [_nt(qs[h], kj[h]) * scale for h in hs]
            dw = [_nt(dos[h], vj[h]) for h in hs]
            ls = [_log_sigmoid(z[h]) for h in hs]
            lm = [jnp.where(causal, ls[h] - z[h], 0.0) if diagonal else ls[h] - z[h] for h in hs]
            stay = [ltots[h] - carry[h][1] - _dot_split(lm[h], upto) for h in hs]
            w = [jnp.exp(ls[h] + stay[h]) for h in hs]
            if diagonal:
                w = [jnp.where(causal, w[h], 0.0) for h in hs]
            da = [dw[h] * w[h] for h in hs]
            sig = [jnp.exp(ls[h]) for h in hs]
            chain = [sig[h] * (carry[h][2] + _dot_split(da[h], before)) for h in hs]
            if diagonal:
                chain = [jnp.where(causal, chain[h], 0.0) for h in hs]
            dzb = [((da[h] * (1.0 - sig[h]) - chain[h]) * scale).astype(BF16) for h in hs]
            dq = [carry[h][0] + jnp.dot(dzb[h], kj[h], preferred_element_type=F32) for h in hs]
            for h in hs:
                dka_ref[rows, heads[h]] += _tn(dzb[h], qs[h])
            for h in hs:
                dva_ref[rows, heads[h]] += _tn(w[h].astype(BF16), dos[h])
            return tuple((dq[h], carry[h][1] + jnp.sum(lm[h], axis=1, keepdims=True),
                          carry[h][2] + jnp.sum(da[h], axis=1, keepdims=True)) for h in hs)

        zero = jnp.zeros((blk, 1), F32)
        init = tuple((jnp.zeros((blk, HEAD_DIM), F32), zero, zero) for _ in heads)
        carry = lax.fori_loop(0, i, lambda j, c: key_block(j, c, False), init)
        carry = key_block(i, carry, True)
        for h, hd in enumerate(heads):
            dq_ref[:, hd] = carry[h][0].astype(BF16)

        @pl.when(i == nq - 1)
        def _():
            dk_ref[...] = dka_ref[...].astype(BF16)
            dv_ref[...] = dva_ref[...].astype(BF16)

        host.after(step, ngrp * nq)

    outs = pl.pallas_call(
        body, name=name, grid=(ngrp, nq), in_specs=host.in_specs, out_specs=host.out_specs, out_shape=host.out_shape,
        scratch_shapes=host.scratch, input_output_aliases=host.aliases,
        compiler_params=_cp("arbitrary", "arbitrary"))(p, p, p, ltot, dout, *host.args)
    (dq, dk, dv), extra = host.results(outs)
    return dq, dk, dv, extra


def _pool_groups(pad_ref, tile, row0, gd, halo):
    row = row0 + lax.broadcasted_iota(jnp.int32, (tile, 1), 0)
    out = []
    for gi, win in enumerate(POOL_WINDOWS):
        cs = slice(gi * gd, (gi + 1) * gd)
        tok = pad_ref[halo:halo + tile, cs]
        acc = tok
        for j in range(1, win):
            acc = acc + pad_ref[halo - j:halo - j + tile, cs]
        cnt = jnp.minimum(win, row + 1).astype(F32)
        out.append(acc / cnt - tok)
    return out


def _even_mix_fwd(p, att, pool_w, pool_scale, d, name):
    s = p.shape[0]
    half = d // 2
    gd = half // len(POOL_WINDOWS)
    t, hb = ROW_TILE, POOL_HALO

    def body(u_ref, uh_ref, g_ref, a_ref, pw_ref, sc_ref, y_ref, pad_ref):
        i = pl.program_id(0)
        pad_ref[0:hb, :] = jnp.where(i > 0, uh_ref[...], 0.0)
        pad_ref[hb:, :] = u_ref[...]
        pooled = _pool_groups(pad_ref, t, i * t, gd, hb)
        for gi in range(len(POOL_WINDOWS)):
            cs = slice(gi * gd, (gi + 1) * gd)
            po = jnp.dot(pooled[gi].astype(BF16), pw_ref[gi], preferred_element_type=F32) * sc_ref[:, cs]
            y_ref[:, half + gi * gd:half + (gi + 1) * gd] = (po * _silu(g_ref[:, half + gi * gd:half + (gi + 1) * gd])).astype(BF16)
        y_ref[:, :half] = (a_ref[...] * _silu(g_ref[:, :half])).astype(BF16)

    return pl.pallas_call(
        body, name=name, grid=(s // t,),
        in_specs=[pl.BlockSpec((t, half), lambda i: (i, 3)),
                  pl.BlockSpec((hb, half), lambda i: (jnp.maximum(i * (t // hb) - 1, 0), 3)),
                  pl.BlockSpec((t, d), lambda i: (i, 2)),
                  pl.BlockSpec((t, half), lambda i: (i, 0)),
                  pl.BlockSpec(pool_w.shape, lambda i: (0, 0, 0)),
                  pl.BlockSpec((1, half), lambda i: (0, 0))],
        out_specs=pl.BlockSpec((t, d), lambda i: (i, 0)),
        out_shape=jax.ShapeDtypeStruct((s, d), BF16),
        scratch_shapes=[pltpu.VMEM((hb + t, half), F32)],
        compiler_params=_cp("parallel"))(p, p, p, att, pool_w, pool_scale)


def _even_mix_bwd(p, att, dy, pool_w, pool_scale, d, name, comm=None):
    s = p.shape[0]
    half = d // 2
    ng = len(POOL_WINDOWS)
    gd = half // ng
    t, hb = ROW_TILE, POOL_HALO
    nt = s // t
    host = _Host(
        comm,
        [pl.BlockSpec((t, half), lambda i: (i, 3)),
         pl.BlockSpec((hb, half), lambda i: (jnp.maximum(i * (t // hb) - 1, 0), 3)),
         pl.BlockSpec((t, d), lambda i: (i, 2)),
         pl.BlockSpec((hb, half), lambda i: (jnp.minimum((i + 1) * (t // hb), s // hb - 1), 5)),
         pl.BlockSpec((t, half), lambda i: (i, 0)),
         pl.BlockSpec((t, d), lambda i: (i, 0)),
         pl.BlockSpec((hb, half), lambda i: (jnp.minimum((i + 1) * (t // hb), s // hb - 1), 1)),
         pl.BlockSpec(pool_w.shape, lambda i: (0, 0, 0)),
         pl.BlockSpec((1, half), lambda i: (0, 0))],
        [pl.BlockSpec((t, half), lambda i: (i, 0)),
         pl.BlockSpec((t, half), lambda i: (i, 0)),
         pl.BlockSpec((t, d), lambda i: (i, 0)),
         pl.BlockSpec((1, half), lambda i: (0, 0)),
         pl.BlockSpec((ng, gd, gd), lambda i: (0, 0, 0))],
        [jax.ShapeDtypeStruct((s, half), F32), jax.ShapeDtypeStruct((s, half), BF16),
         jax.ShapeDtypeStruct((s, d), BF16), jax.ShapeDtypeStruct((1, half), F32),
         jax.ShapeDtypeStruct((ng, gd, gd), F32)],
        [pltpu.VMEM((hb + t, half), F32), pltpu.VMEM((t + hb, half), F32)])

    def body(*refs):
        ((u_ref, uh_ref, g_ref, gh_ref, a_ref, dy_ref, dyh_ref, pw_ref, sc_ref),
         (da_ref, du_ref, dg_ref, dsc_ref, dpw_ref), (pad_ref, dn_ref)) = host.split(refs)
        i = pl.program_id(0)
        host.before(i, nt)
        first = i == 0
        pad_ref[0:hb, :] = jnp.where(i > 0, uh_ref[...], 0.0)
        pad_ref[hb:, :] = u_ref[...]
        pooled = _pool_groups(pad_ref, t, i * t, gd, hb)
        g1 = g_ref[:, :half]
        dy1 = dy_ref[:, :half]
        da_ref[...] = dy1 * _silu(g1)
        dg_ref[:, :half] = (dy1 * a_ref[...] * _dsilu(g1)).astype(BF16)
        row = i * t + lax.broadcasted_iota(jnp.int32, (t + hb, 1), 0)
        for gi, win in enumerate(POOL_WINDOWS):
            cs = slice(gi * gd, (gi + 1) * gd)
            cs2 = slice(half + gi * gd, half + (gi + 1) * gd)
            w = pw_ref[gi]
            pb = pooled[gi].astype(BF16)
            zp = jnp.dot(pb, w, preferred_element_type=F32)
            g2 = g_ref[:, cs2]
            dy2 = dy_ref[:, cs2]
            dg_ref[:, cs2] = (dy2 * zp * sc_ref[:, cs] * _dsilu(g2)).astype(BF16)
            dpo = dy2 * _silu(g2)
            _acc_rows(dsc_ref.at[:, cs], first, jnp.sum(dpo * zp, axis=0, keepdims=True))
            dz = (dpo * sc_ref[:, cs]).astype(BF16)
            _acc_rows(dpw_ref.at[gi], first, _tn(pb, dz))
            dzh = jnp.where(i < nt - 1, dyh_ref[:, cs] * _silu(gh_ref[:, cs]) * sc_ref[:, cs], 0.0).astype(BF16)
            dpool = _nt(dz, w)
            dpool_h = _nt(dzh, w)
            cnt = jnp.minimum(win, row + 1).astype(F32)
            dn_ref[0:t, cs] = dpool / cnt[0:t]
            dn_ref[t:, cs] = dpool_h / cnt[t:]
            acc = dn_ref[0:t, cs]
            for j in range(1, win):
                acc = acc + dn_ref[j:j + t, cs]
            du_ref[:, cs] = (acc - dpool).astype(BF16)
        host.after(i, nt)

    outs = pl.pallas_call(
        body, name=name, grid=(nt,), in_specs=host.in_specs, out_specs=host.out_specs, out_shape=host.out_shape,
        scratch_shapes=host.scratch, input_output_aliases=host.aliases,
        compiler_params=_cp("arbitrary"))(p, p, p, p, att, dy, dy, pool_w, pool_scale, *host.args)
    return host.results(outs)


def _mm_out_even(y, w, x, g_post, g_pre_next, name):
    s, k = y.shape
    d = w.shape[1]
    t = ROW_TILE

    def body(y_ref, w_ref, x_ref, gp_ref, gn_ref, o_ref, x1_ref, h1_ref):
        for r0 in range(0, t, t // 2):
            rows = slice(r0, r0 + t // 2)
            o = jnp.dot(y_ref[rows, :], w_ref[...], preferred_element_type=F32)
            o_ref[rows, :] = o
            ohat, _ = _rms_stats(o)
            x1 = x_ref[rows, :] + ohat * gp_ref[...]
            x1_ref[rows, :] = x1
            xhat, _ = _rms_stats(x1)
            h1_ref[rows, :] = (xhat * gn_ref[...]).astype(BF16)

    row = lambda c: pl.BlockSpec((t, c), lambda i: (i, 0))
    vec = pl.BlockSpec((1, d), lambda i: (0, 0))
    return pl.pallas_call(
        body, name=name, grid=(s // t,),
        in_specs=[row(k), pl.BlockSpec((k, d), lambda i: (0, 0)), row(d), vec, vec],
        out_specs=[row(d), row(d), row(d)],
        out_shape=[jax.ShapeDtypeStruct((s, d), F32), jax.ShapeDtypeStruct((s, d), F32),
                   jax.ShapeDtypeStruct((s, d), BF16)],
        compiler_params=_cp("parallel"))(y, w, x, g_post, g_pre_next)


def _mm_out_odd(y, w, x1, g_post, target, name):
    s, k = y.shape
    d = w.shape[1]
    t = ROW_TILE

    def body(y_ref, w_ref, x_ref, gp_ref, tg_ref, do_ref, dx_ref, loss_ref, dgp_ref):
        first = pl.program_id(0) == 0
        gp = gp_ref[...]
        part = dgp = None
        for r0 in range(0, t, t // 2):
            rows = slice(r0, r0 + t // 2)
            o = jnp.dot(y_ref[rows, :], w_ref[...], preferred_element_type=F32)
            ohat, r = _rms_stats(o)
            diff = x_ref[rows, :] + ohat * gp - tg_ref[rows, :]
            part_half = 0.5 * jnp.sum(jnp.mean(diff * diff, axis=-1, keepdims=True), axis=0, keepdims=True)
            dx2 = diff * (1.0 / d)
            dx_ref[rows, :] = dx2
            do, dgp_half = _rms_bwd(dx2, ohat, r, gp)
            do_ref[rows, :] = do.astype(BF16)
            part = part_half if part is None else part + part_half
            dgp = dgp_half if dgp is None else dgp + dgp_half
        _acc_rows(loss_ref, first, jnp.broadcast_to(part, loss_ref.shape))
        _acc_rows(dgp_ref, first, dgp)

    row = lambda c: pl.BlockSpec((t, c), lambda i: (i, 0))
    vec = pl.BlockSpec((1, d), lambda i: (0, 0))
    return pl.pallas_call(
        body, name=name, grid=(s // t,),
        in_specs=[row(k), pl.BlockSpec((k, d), lambda i: (0, 0)), row(d), vec, row(d)],
        out_specs=[row(d), row(d), pl.BlockSpec((8, LANES), lambda i: (0, 0)), vec],
        out_shape=[jax.ShapeDtypeStruct((s, d), BF16), jax.ShapeDtypeStruct((s, d), F32),
                   jax.ShapeDtypeStruct((8, LANES), F32), jax.ShapeDtypeStruct((1, d), F32)],
        compiler_params=_cp("arbitrary"))(y, w, x1, g_post, target)


def _layer_norm(d1, cg, cb):
    mu = jnp.mean(d1, axis=-1, keepdims=True)
    cen = d1 - mu
    rstd = lax.rsqrt(jnp.mean(cen * cen, axis=-1, keepdims=True) + EPS)
    n = cen * rstd
    return n, rstd, n * cg + cb


SUBLANES = 8
ROW_STRIP = 64
GATHER_PIECES = 8
CONV_ROWS = 64


def _make_shifts(pad_ref, cs, sh_ref):
    rows = sh_ref.shape[1]
    for r in range(1, SUBLANES):
        sh_ref[r - 1] = pad_ref[r:r + rows, cs]


def _by_shift(taps, base, sign=1):
    return sorted(range(taps), key=lambda k: ((sign * (base + k)) % SUBLANES, k))


def _window(pad_ref, cs, sh_ref, off, t):
    m, r = divmod(off, SUBLANES)
    if r == 0:
        return pad_ref[SUBLANES * m:SUBLANES * m + t, cs]
    return sh_ref[r - 1, SUBLANES * m:SUBLANES * m + t, :]


def _odd_mix_fwd(p, sconv_w, dconv_w, dconv_b, cnorm_g, cnorm_b, d, name):
    s = p.shape[0]
    w = d // 2
    k3, k31 = sconv_w.shape[0], dconv_w.shape[0]
    t, hb = ROW_TILE, CONV_HALO
    assert hb >= k31 - 1 and w % LANES == 0

    def body(p_ref, ph_ref, w3_ref, w31_ref, b31_ref, cg_ref, cb_ref, y_ref, s3_ref, d1_ref, mpad, dpad, sh_ref):
        i = pl.program_id(0)
        mpad[0:hb, :] = jnp.where(i > 0, ph_ref[:, 2 * w:3 * w] * ph_ref[:, 0:w], 0.0)
        mpad[hb:, :] = p_ref[:, 2 * w:3 * w] * p_ref[:, 0:w]
        dpad[0:hb, :] = jnp.where(i > 0, ph_ref[:, 3 * w:4 * w] * _sigmoid(ph_ref[:, 4 * w:5 * w]), 0.0)
        dpad[hb:, :] = p_ref[:, 3 * w:4 * w] * _sigmoid(p_ref[:, 4 * w:5 * w])
        for c0 in range(0, w, LANES):
            cs = slice(c0, c0 + LANES)
            acc = jnp.zeros((t, LANES), F32)
            for kk in range(k3):
                acc = acc + w3_ref[kk:kk + 1, cs] * mpad[hb - (k3 - 1) + kk:hb - (k3 - 1) + kk + t, cs]
            s3_ref[:, cs] = acc
            _make_shifts(dpad, cs, sh_ref)
            for r0 in range(0, t, CONV_ROWS):
                acc = jnp.zeros((CONV_ROWS, LANES), F32)
                for kk in _by_shift(k31, hb - (k31 - 1)):
                    acc = acc + w31_ref[kk:kk + 1, cs] * _window(dpad, cs, sh_ref, hb - (k31 - 1) + kk + r0, CONV_ROWS)
                d1_ref[r0:r0 + CONV_ROWS, cs] = acc + b31_ref[:, cs]
        _, _, d2 = _layer_norm(d1_ref[...], cg_ref[...], cb_ref[...])
        y_ref[:, :w] = (p_ref[:, w:2 * w] * s3_ref[...] * _silu(p_ref[:, 5 * w:6 * w])).astype(BF16)
        y_ref[:, w:] = (_silu(d2) * _silu(p_ref[:, 6 * w:7 * w])).astype(BF16)

    row = lambda c: pl.BlockSpec((t, c), lambda i: (i, 0))
    full = lambda a: pl.BlockSpec(a.shape, lambda i: (0, 0))
    return pl.pallas_call(
        body, name=name, grid=(s // t,),
        in_specs=[row(7 * w),
                  pl.BlockSpec((hb, 5 * w), lambda i: (jnp.maximum(i * (t // hb) - 1, 0), 0)),
                  full(sconv_w), full(dconv_w), full(dconv_b), full(cnorm_g), full(cnorm_b)],
        out_specs=[row(d), row(w), row(w)],
        out_shape=[jax.ShapeDtypeStruct((s, d), BF16), jax.ShapeDtypeStruct((s, w), F32),
                   jax.ShapeDtypeStruct((s, w), F32)],
        scratch_shapes=[pltpu.VMEM((hb + t, w), F32)] * 2 + [pltpu.VMEM((SUBLANES - 1, hb + t - SUBLANES, LANES), F32)],
        compiler_params=_cp("parallel"))(p, p, sconv_w, dconv_w, dconv_b, cnorm_g, cnorm_b)


def _odd_bwd_rows(p, s3, d1, dy, cnorm_g, cnorm_b, d, name, comm=None):
    s = p.shape[0]
    w = d // 2
    t = ROW_TILE
    col = lambda j: pl.BlockSpec((t, w), lambda i: (i, j))
    row = lambda c: pl.BlockSpec((t, c), lambda i: (i, 0))
    vec = pl.BlockSpec((1, w), lambda i: (0, 0))
    host = _Host(comm, [col(1), col(5), col(6), row(w), row(w), row(d), vec, vec],
                 [row(w), row(d), row(w), row(w), vec, vec, vec],
                 [jax.ShapeDtypeStruct((s, w), BF16), jax.ShapeDtypeStruct((s, d), BF16),
                  jax.ShapeDtypeStruct((s, w), F32), jax.ShapeDtypeStruct((s, w), F32)] + [jax.ShapeDtypeStruct((1, w), F32)] * 3, [])

    def body(*refs):
        ((bc_ref, g1_ref, g2_ref, s3_ref, d1_ref, dy_ref, cg_ref, cb_ref),
         (dbc_ref, dg_ref, ds3_ref, dd1_ref, dcg_ref, dcb_ref, db_ref), _) = host.split(refs)
        step = pl.program_id(0)
        host.before(step, s // t)
        first = step == 0

        def strip(j, sums):
            rows = slice(j * ROW_STRIP, (j + 1) * ROW_STRIP)
            g1, g2 = g1_ref[rows, :], g2_ref[rows, :]
            bc, s3v = bc_ref[rows, :], s3_ref[rows, :]
            dy1, dy2 = dy_ref[rows, :w], dy_ref[rows, w:]
            n, rstd, d2 = _layer_norm(d1_ref[rows, :], cg_ref[...], cb_ref[...])
            dg_ref[rows, :w] = (dy1 * bc * s3v * _dsilu(g1)).astype(BF16)
            dg_ref[rows, w:] = (dy2 * _silu(d2) * _dsilu(g2)).astype(BF16)
            dco = dy1 * _silu(g1)
            dbc_ref[rows, :] = (dco * s3v).astype(BF16)
            ds3_ref[rows, :] = dco * bc
            dd2 = dy2 * _silu(g2) * _dsilu(d2)
            dn = dd2 * cg_ref[...]
            dd1 = rstd * (dn - jnp.mean(dn, axis=-1, keepdims=True) - n * jnp.mean(dn * n, axis=-1, keepdims=True))
            dd1_ref[rows, :] = dd1
            dcb, dcg, db = sums
            return (dcb + jnp.sum(dd2, axis=0, keepdims=True), dcg + jnp.sum(dd2 * n, axis=0, keepdims=True),
                    db + jnp.sum(dd1, axis=0, keepdims=True))

        zero = jnp.zeros((1, w), F32)
        sums = (zero, zero, zero)
        for j in range(t // ROW_STRIP):
            sums = strip(j, sums)
        dcb, dcg, db = sums
        _acc_rows(dcb_ref, first, dcb)
        _acc_rows(dcg_ref, first, dcg)
        _acc_rows(db_ref, first, db)
        host.after(step, s // t)

    outs = pl.pallas_call(
        body, name=name, grid=(s // t,), in_specs=host.in_specs, out_specs=host.out_specs, out_shape=host.out_shape,
        scratch_shapes=host.scratch, input_output_aliases=host.aliases,
        compiler_params=_cp("arbitrary"))(p, p, p, s3, d1, dy, cnorm_g, cnorm_b, *host.args)
    return host.results(outs)


def _odd_bwd_conv(p, ds3, dd1, sconv_w, dconv_w, d, name):
    s = p.shape[0]
    w = d // 2
    k3, k31 = sconv_w.shape[0], dconv_w.shape[0]
    t, hb, ha = ROW_TILE, CONV_HALO, 8
    nt = s // t
    assert hb >= k31 - 1 and ha >= k3 - 1

    def body(hc_ref, cc_ref, ga_ref, gb_ref, hch_ref, cch_ref, gah_ref, gbh_ref, ds3_ref, ds3h_ref, dd1_ref, dd1h_ref,
             w3_ref, w31_ref, dhc_ref, dcc_ref, dga_ref, dgb_ref, dw3_ref, dw31_ref, mpad, dpad, s3pad, d1pad, sh_ref):
        i = pl.program_id(0)
        first = i == 0
        last = i == nt - 1
        mpad[0:hb, :] = jnp.where(i > 0, cch_ref[...] * hch_ref[...], 0.0)
        mpad[hb:, :] = cc_ref[...] * hc_ref[...]
        dpad[0:hb, :] = jnp.where(i > 0, gah_ref[...] * _sigmoid(gbh_ref[...]), 0.0)
        dpad[hb:, :] = ga_ref[...] * _sigmoid(gb_ref[...])
        s3pad[0:t, :] = ds3_ref[...]
        s3pad[t:, :] = jnp.where(last, 0.0, ds3h_ref[...])
        d1pad[0:t, :] = dd1_ref[...]
        d1pad[t:, :] = jnp.where(last, 0.0, dd1h_ref[...])

        @pl.when(first)
        def _():
            dw3_ref[...] = jnp.zeros_like(dw3_ref)
            dw31_ref[...] = jnp.zeros_like(dw31_ref)

        def fold(v):
            return jnp.sum(v.reshape(v.shape[0] // SUBLANES, SUBLANES, LANES), axis=0)

        groups = range(0, t, CONV_ROWS)
        for c0 in range(0, w, LANES):
            cs = slice(c0, c0 + LANES)
            ds3v = s3pad[0:t, cs]
            dm = jnp.zeros((t, LANES), F32)
            for kk in range(k3):
                dm = dm + w3_ref[kk:kk + 1, cs] * s3pad[k3 - 1 - kk:k3 - 1 - kk + t, cs]
                off = hb - (k3 - 1) + kk
                dw3_ref[SUBLANES * kk:SUBLANES * (kk + 1), cs] += fold(ds3v * mpad[off:off + t, cs])
            dcc_ref[:, cs] = (dm * hc_ref[:, cs]).astype(BF16)
            dhc_ref[:, cs] = (dm * cc_ref[:, cs]).astype(BF16)
            _make_shifts(d1pad, cs, sh_ref)
            for r0 in groups:
                rows = slice(r0, r0 + CONV_ROWS)
                dd0 = jnp.zeros((CONV_ROWS, LANES), F32)
                for kk in _by_shift(k31, -(k31 - 1), -1):
                    dd0 = dd0 + w31_ref[kk:kk + 1, cs] * _window(d1pad, cs, sh_ref, k31 - 1 - kk + r0, CONV_ROWS)
                sgb = _sigmoid(gb_ref[rows, cs])
                dga_ref[rows, cs] = (dd0 * sgb).astype(BF16)
                dgb_ref[rows, cs] = (dd0 * ga_ref[rows, cs] * sgb * (1.0 - sgb)).astype(BF16)
            _make_shifts(dpad, cs, sh_ref)
            for kk in _by_shift(k31, hb - (k31 - 1)):
                part = jnp.zeros((SUBLANES, LANES), F32)
                for r0 in groups:
                    part = part + fold(d1pad[r0:r0 + CONV_ROWS, cs]
                                       * _window(dpad, cs, sh_ref, hb - (k31 - 1) + kk + r0, CONV_ROWS))
                dw31_ref[SUBLANES * kk:SUBLANES * (kk + 1), cs] += part

    col = lambda j: pl.BlockSpec((t, w), lambda i: (i, j))
    pre = lambda j: pl.BlockSpec((hb, w), lambda i: (jnp.maximum(i * (t // hb) - 1, 0), j))
    row = pl.BlockSpec((t, w), lambda i: (i, 0))
    post = lambda h: pl.BlockSpec((h, w), lambda i: (jnp.minimum((i + 1) * (t // h), s // h - 1), 0))
    full = lambda a: pl.BlockSpec(a.shape, lambda i: (0, 0))
    dhc, dcc, dga, dgb, dw3, dw31 = pl.pallas_call(
        body, name=name, grid=(nt,),
        in_specs=[col(0), col(2), col(3), col(4), pre(0), pre(2), pre(3), pre(4),
                  row, post(ha), row, post(hb), full(sconv_w), full(dconv_w)],
        out_specs=[row, row, row, row, pl.BlockSpec((SUBLANES * k3, w), lambda i: (0, 0)),
                   pl.BlockSpec((SUBLANES * k31, w), lambda i: (0, 0))],
        out_shape=[jax.ShapeDtypeStruct((s, w), BF16)] * 4
        + [jax.ShapeDtypeStruct((SUBLANES * k3, w), F32), jax.ShapeDtypeStruct((SUBLANES * k31, w), F32)],
        scratch_shapes=[pltpu.VMEM((hb + t, w), F32)] * 2 + [pltpu.VMEM((t + ha, w), F32), pltpu.VMEM((t + hb, w), F32),
                                                             pltpu.VMEM((SUBLANES - 1, hb + t - SUBLANES, LANES), F32)],
        compiler_params=_cp("arbitrary"))(p, p, p, p, p, p, p, p, ds3, ds3, dd1, dd1, sconv_w, dconv_w)
    return dhc, dcc, dga, dgb, jnp.sum(dw3.reshape(k3, SUBLANES, w), axis=1), jnp.sum(dw31.reshape(k31, SUBLANES, w), axis=1)


def _mm_in_bwd(dp, w3, x, g_pre, dres, post, name, comm=None):
    s = dp.shape[0]
    nsh, d, ns = w3.shape
    t = 512 if s % 512 == 0 else ROW_TILE
    nt = s // t
    ks = 2 if (ns // 2) % LANES == 0 else 1
    nk, kw = nsh * ks, ns // ks
    chunk = 128
    nchunk = t // chunk
    row = pl.BlockSpec((t, d), lambda i, k: (i, 0))
    vec = pl.BlockSpec((1, d), lambda i, k: (0, 0))
    rowwise = [x, dres] + ([post[0]] if post is not None else [])
    in_specs = [pl.BlockSpec((t, kw), lambda i, k: (i, k)), pl.BlockSpec((None, d, kw), lambda i, k: (k // ks, 0, k % ks)), vec]
    out_specs = [row, vec]
    out_shape = [jax.ShapeDtypeStruct((s, d), F32), jax.ShapeDtypeStruct((1, d), F32)]
    args = [dp, w3, g_pre]
    if post is not None:
        in_specs += [vec]
        out_specs += [row, vec]
        out_shape += [jax.ShapeDtypeStruct((s, d), BF16), jax.ShapeDtypeStruct((1, d), F32)]
        args += [post[1]]
    n_blocked = len(in_specs)
    in_specs += [ANY] * len(rowwise)
    args += rowwise
    host = _Host(comm, in_specs, out_specs, out_shape,
                 [pltpu.VMEM((t, d), F32), pltpu.VMEM((len(rowwise), 2, chunk, d), F32), pltpu.SemaphoreType.DMA((len(rowwise), 2))])

    def body(*refs):
        ins, outs, (acc_ref, buf_ref, sem_ref) = host.split(refs)
        dp_ref, w_ref, g_ref = ins[:3]
        hbm = ins[n_blocked:]
        dx_ref, dg_ref = outs[:2]
        tile = pl.program_id(0)
        kk = pl.program_id(1)
        first = tile == 0
        step = tile * nk + kk
        host.before(step, nt * nk)
        part = _nt(dp_ref[...], w_ref[...])

        @pl.when(kk == 0)
        def _():
            acc_ref[...] = part

        @pl.when(kk > 0)
        def _():
            acc_ref[...] += part

        def fetch(ci, slot):
            return [pltpu.make_async_copy(src.at[pl.ds(tile * t + ci * chunk, chunk)], buf_ref.at[n, slot], sem_ref.at[n, slot])
                    for n, src in enumerate(hbm)]

        @pl.when(kk == nk - 1)
        def _():
            dg = dgp = None
            for cp in fetch(0, 0):
                cp.start()
            for ci in range(nchunk):
                slot = ci % 2
                if ci + 1 < nchunk:
                    for cp in fetch(ci + 1, 1 - slot):
                        cp.start()
                for cp in fetch(ci, slot):
                    cp.wait()
                rows = slice(ci * chunk, (ci + 1) * chunk)
                xhat, r = _rms_stats(buf_ref[0, slot])
                dxn, dg_part = _rms_bwd(acc_ref[rows, :], xhat, r, g_ref[...])
                dx = buf_ref[1, slot] + dxn
                dx_ref[rows, :] = dx
                dg = dg_part if dg is None else dg + dg_part
                if post is not None:
                    ohat, ro = _rms_stats(buf_ref[2, slot])
                    do, dgp_part = _rms_bwd(dx, ohat, ro, ins[3][...])
                    outs[2][rows, :] = do.astype(BF16)
                    dgp = dgp_part if dgp is None else dgp + dgp_part
            _acc_rows(dg_ref, first, dg)
            if post is not None:
                _acc_rows(outs[3], first, dgp)

        host.after(step, nt * nk)

    res = pl.pallas_call(
        body, name=name, grid=(nt, nk), in_specs=host.in_specs, out_specs=host.out_specs, out_shape=host.out_shape,
        scratch_shapes=host.scratch, input_output_aliases=host.aliases,
        compiler_params=_cp("arbitrary", "arbitrary"))(*args, *host.args)
    return host.results(res)


def _half_add(g, r1, c_arr, name):
    nsh, rows, ns = g.shape
    h = rows // 2
    tr = min(ROW_TILE, h)
    per = h // tr

    def body(c_ref, g_ref, r_ref, o_ref):
        o_ref[...] = (g_ref[...].astype(F32) + r_ref[...].astype(F32)).astype(BF16)

    spec = pl.BlockSpec((None, tr, ns), lambda s, r, c: (s, r, 0))
    return pl.pallas_call(
        body, name=name,
        grid_spec=pltpu.PrefetchScalarGridSpec(
            num_scalar_prefetch=1, grid=(nsh, per),
            in_specs=[pl.BlockSpec((None, tr, ns), lambda s, r, c: (s, c[0] * per + r, 0)), spec], out_specs=spec),
        out_shape=jax.ShapeDtypeStruct((nsh, h, ns), BF16), compiler_params=_cp("parallel", "parallel"))(c_arr, g, r1)


def _sum_chips(hh, r2, mc_arr, name):
    _, h, ns = hh.shape
    tr = min(ROW_TILE, h)
    per = h // tr

    def body(mc_ref, h_ref, a_ref, b_ref, c_ref, o_ref):
        o_ref[...] = ((h_ref[...].astype(F32) + a_ref[...].astype(F32)) + b_ref[...].astype(F32)) + c_ref[...].astype(F32)

    got = lambda k: pl.BlockSpec((None, tr, ns), lambda r, mc: (k, r, 0))
    return pl.pallas_call(
        body, name=name,
        grid_spec=pltpu.PrefetchScalarGridSpec(
            num_scalar_prefetch=1, grid=(per,),
            in_specs=[pl.BlockSpec((None, tr, ns), lambda r, mc: (mc[0], r, 0)), got(0), got(1), got(2)],
            out_specs=pl.BlockSpec((tr, ns), lambda r, mc: (mc[1] * per + r, 0))),
        out_shape=jax.ShapeDtypeStruct((2 * h, ns), F32), compiler_params=_cp("parallel"))(mc_arr, hh, r2, r2, r2)


def _add2(a, b, name):
    def body(a_ref, b_ref, o_ref):
        o_ref[...] = a_ref[...] + b_ref[...]

    return pl.pallas_call(body, name=name, out_shape=jax.ShapeDtypeStruct(a.shape, a.dtype), compiler_params=_cp())(a, b)


def _sum_chips_ordered(s2, r2, mc_arr, name):
    rows, w = s2.shape
    rh = rows // 2

    def body(mc_ref, s_ref, a_ref, b_ref, c_ref, o_ref):
        me = mc_ref[0]
        acc = None
        for j in range(N_CHIPS):
            rel = jnp.bitwise_xor(me, j)
            v = jnp.where(rel == 0, s_ref[...], jnp.where(rel == 2, a_ref[...], jnp.where(rel == 1, b_ref[...], c_ref[...])))
            acc = v if acc is None else acc + v
        o_ref[...] = acc

    got = lambda k: pl.BlockSpec((None, rh, w), lambda i, mc: (k, 0, 0))
    return pl.pallas_call(
        body, name=name,
        grid_spec=pltpu.PrefetchScalarGridSpec(
            num_scalar_prefetch=1, grid=(1,),
            in_specs=[pl.BlockSpec((rh, w), lambda i, mc: (mc[1], 0)), got(0), got(1), got(2)],
            out_specs=pl.BlockSpec((rh, w), lambda i, mc: (mc[1], 0))),
        out_shape=jax.ShapeDtypeStruct((rows, w), F32), compiler_params=_cp("arbitrary"))(mc_arr, s2, r2, r2, r2)


def _adamw(w, g, m, v, name, comm=None):
    r, c = w.shape
    tr = ROW_TILE if r % ROW_TILE == 0 else r
    c1 = 1.0 / (1.0 - ADAM_B1 ** ADAM_STEP)
    c2 = 1.0 / (1.0 - ADAM_B2 ** ADAM_STEP)
    spec = pl.BlockSpec((tr, c), lambda i: (i, 0))
    host = _Host(comm, [spec] * 4, [spec] * 4, [jax.ShapeDtypeStruct((r, c), F32)] * 4, [])

    def body(*refs):
        (w_ref, g_ref, m_ref, v_ref), (go_ref, d_ref, nm_ref, nv_ref), _ = host.split(refs)
        step = pl.program_id(0)
        host.before(step, r // tr)
        gv = g_ref[...]
        go_ref[...] = gv
        nm = ADAM_B1 * m_ref[...] + (1.0 - ADAM_B1) * gv
        nv = ADAM_B2 * v_ref[...] + (1.0 - ADAM_B2) * (gv * gv)
        nm_ref[...] = nm
        nv_ref[...] = nv
        d_ref[...] = -ADAM_LR * ((nm * c1) / (jnp.sqrt(nv * c2) + ADAM_EPS) + ADAM_WD * w_ref[...])
        host.after(step, r // tr)

    outs = pl.pallas_call(
        body, name=name, grid=(r // tr,), in_specs=host.in_specs, out_specs=host.out_specs, out_shape=host.out_shape,
        scratch_shapes=host.scratch, input_output_aliases=host.aliases,
        compiler_params=_cp("arbitrary"))(w, g, m, v, *host.args)
    return host.results(outs)


def _gather_weights(bigs, pool_w, pack_w, pack_d, name):
    nb = len(bigs)
    smalls = [pool_w, pack_w, pack_d]
    q, cw, cd = pool_w.shape[1], pack_w.shape[1], pack_d.shape[1]
    pieces = [_GatherPlan(bigs, (j, j + 1, GATHER_PIECES)) for j in range(GATHER_PIECES)]
    for j, piece in enumerate(pieces):
        piece.base = 9 + j * piece.nsems

    def body(*refs):
        srcs, dsts = refs[:nb + 3], refs[nb + 3:2 * (nb + 3)]
        ssem, rsem, lsem = refs[2 * (nb + 3):]
        x, y, c, me, chips, sib = _place()

        def small_dst(n, chip):
            if n == 0:
                return dsts[nb].at[:, pl.ds(chip * q, q), :]
            return dsts[nb + n].at[:, pl.ds(chip * (cw if n == 1 else cd), cw if n == 1 else cd)]

        local = [pltpu.make_async_copy(srcs[nb + n], small_dst(n, me), lsem.at[n]) for n in range(3)]
        for cp in local:
            cp.start()
        sends = []
        for n in range(3):
            for k, chip in enumerate(chips):
                cp = _rcopy(srcs[nb + n], small_dst(n, me), ssem.at[3 * n + k], rsem.at[3 * n + k], (*chip, c))
                cp.start()
                sends.append(cp)
        big = (srcs[:nb], dsts[:nb], ssem, rsem)
        for stage in ("start", "relay", "relay_far", "finish"):
            for piece in pieces:
                getattr(piece, stage)(*big)
        for n in range(3):
            for k, chip in enumerate(chips):
                ref = small_dst(n, 2 * chip[0] + chip[1])
                _rcopy(ref, ref, ssem.at[3 * n + k], rsem.at[3 * n + k], (*chip, c)).wait_recv()
        for cp in sends:
            cp.wait_send()
        for cp in local:
            cp.wait()

    nsem = 9 + sum(piece.nsems for piece in pieces)
    out_shape = [jax.ShapeDtypeStruct(b.shape, b.dtype) for b in bigs]
    out_shape += [jax.ShapeDtypeStruct((pool_w.shape[0], N_CHIPS * q, pool_w.shape[2]), pool_w.dtype),
                  jax.ShapeDtypeStruct((pack_w.shape[0], N_CHIPS * cw), pack_w.dtype),
                  jax.ShapeDtypeStruct((pack_d.shape[0], N_CHIPS * cd), pack_d.dtype)]
    return pl.pallas_call(
        body, name=name, in_specs=[ANY] * (nb + 3), out_specs=[ANY] * (nb + 3), out_shape=out_shape,
        input_output_aliases={a: a for a in range(nb)},
        scratch_shapes=[pltpu.SemaphoreType.DMA((nsem,)), pltpu.SemaphoreType.DMA((nsem,)), pltpu.SemaphoreType.DMA((3,))],
        compiler_params=pltpu.CompilerParams(has_side_effects=True))(*bigs, *smalls)


def _swap_with_sibling(grads, wholes, name):
    n, nw = len(grads), len(wholes)
    halves = [g.shape[1] // 2 for g in grads]

    def body(*refs):
        srcs, dsts = refs[:n + nw], refs[n + nw:2 * (n + nw)]
        ssem, rsem = refs[2 * (n + nw):]
        x, y, c, me, chips, sib = _place()
        cps = [_rcopy(srcs[a].at[:, pl.ds((1 - c) * halves[a], halves[a]), :], dsts[a], ssem.at[a], rsem.at[a], sib)
               for a in range(n)]
        cps += [_rcopy(srcs[a], dsts[a], ssem.at[a], rsem.at[a], sib) for a in range(n, n + nw)]
        for cp in cps:
            cp.start()
        for cp in cps:
            cp.wait_recv()
        for cp in cps:
            cp.wait_send()

    out_shape = [jax.ShapeDtypeStruct((g.shape[0], h, g.shape[2]), g.dtype) for g, h in zip(grads, halves)]
    out_shape += [jax.ShapeDtypeStruct(w.shape, w.dtype) for w in wholes]
    return pl.pallas_call(
        body, name=name, in_specs=[ANY] * (n + nw), out_specs=[ANY] * (n + nw), out_shape=out_shape,
        scratch_shapes=[pltpu.SemaphoreType.DMA((n + nw,)), pltpu.SemaphoreType.DMA((n + nw,))],
        compiler_params=pltpu.CompilerParams(has_side_effects=True))(*grads, *wholes)


def _scatter_to_chips(halves_in, small, name):
    n = len(halves_in)
    rh = small.shape[0] // 2

    def body(*refs):
        srcs, dsts = refs[:n + 1], refs[n + 1:2 * (n + 1)]
        ssem, rsem = refs[2 * (n + 1):]
        x, y, c, me, chips, sib = _place()
        cps = []
        for a in range(n + 1):
            for k, chip in enumerate(chips):
                src = srcs[a].at[2 * chip[0] + chip[1]] if a < n else srcs[a].at[pl.ds(c * rh, rh)]
                cps.append(_rcopy(src, dsts[a].at[k], ssem.at[3 * a + k], rsem.at[3 * a + k], (*chip, c)))
        for cp in cps:
            cp.start()
        for cp in cps:
            cp.wait_recv()
        for cp in cps:
            cp.wait_send()

    out_shape = [jax.ShapeDtypeStruct((3,) + h.shape[1:], h.dtype) for h in halves_in]
    out_shape.append(jax.ShapeDtypeStruct((3, rh, small.shape[1]), small.dtype))
    return pl.pallas_call(
        body, name=name, in_specs=[ANY] * (n + 1), out_specs=[ANY] * (n + 1), out_shape=out_shape,
        scratch_shapes=[pltpu.SemaphoreType.DMA((3 * (n + 1),)), pltpu.SemaphoreType.DMA((3 * (n + 1),))],
        compiler_params=pltpu.CompilerParams(has_side_effects=True))(*halves_in, small)


def _join_halves(parts, name):
    n = len(parts)

    def body(*refs):
        srcs, dsts = refs[:n], refs[n:2 * n]
        ssem, rsem = refs[2 * n:]
        x, y, c, me, chips, sib = _place()
        cps = []
        for a in range(n):
            h = srcs[a].shape[0] // 2
            cps.append(_rcopy(srcs[a].at[pl.ds(c * h, h)], dsts[a].at[pl.ds(c * h, h)], ssem.at[a], rsem.at[a], sib))
        for cp in cps:
            cp.start()
        for a in range(n):
            h = srcs[a].shape[0] // 2
            theirs = dsts[a].at[pl.ds((1 - c) * h, h)]
            _rcopy(theirs, theirs, ssem.at[a], rsem.at[a], sib).wait_recv()
        for cp in cps:
            cp.wait_send()

    out_shape = [jax.ShapeDtypeStruct(p.shape, p.dtype) for p in parts]
    return pl.pallas_call(
        body, name=name, in_specs=[ANY] * n, out_specs=[ANY] * n, out_shape=out_shape,
        input_output_aliases={a: a for a in range(n)},
        scratch_shapes=[pltpu.SemaphoreType.DMA((n,)), pltpu.SemaphoreType.DMA((n,))],
        compiler_params=pltpu.CompilerParams(has_side_effects=True))(*parts)


def _scatter_start(h, name):
    land = (3,) + h.shape[1:]

    def body(h_ref, land_ref, send_sems, recv_sems, h_thru, land_thru, token):
        x, y, c, me, chips, sib = _place()
        for k, chip in enumerate(chips):
            _rcopy(h_ref.at[2 * chip[0] + chip[1]], land_ref.at[k], send_sems.at[k], recv_sems.at[k], (*chip, c)).start()
        token[...] = jnp.zeros_like(token)

    hbm = pl.BlockSpec(memory_space=pltpu.HBM)
    sem = pl.BlockSpec(memory_space=pltpu.SEMAPHORE)
    return pl.pallas_call(
        body, name=name,
        out_shape=(pltpu.SemaphoreType.DMA((3,)), pltpu.SemaphoreType.DMA((3,)), pltpu.HBM(h.shape, h.dtype),
                   pltpu.HBM(land, h.dtype), jax.ShapeDtypeStruct((8, LANES), F32)),
        in_specs=(hbm, hbm), out_specs=(sem, sem, hbm, hbm, pl.BlockSpec(memory_space=pltpu.VMEM)),
        input_output_aliases={0: 2, 1: 3},
        compiler_params=pltpu.CompilerParams(has_side_effects=pltpu.SideEffectType.DATAFLOW_SIDE_EFFECTING))(
            pltpu.with_memory_space_constraint(h, pltpu.HBM),
            pltpu.with_memory_space_constraint(lax.empty(land, h.dtype), pltpu.HBM))


def _scatter_wait(send_sems, recv_sems, h_thru, land_thru, after, name):
    def body(h_ref, land_ref, send_sems, recv_sems, after_ref, h_dead, got_ref):
        x, y, c, me, chips, sib = _place()
        for k, chip in enumerate(chips):
            cp = _rcopy(h_ref.at[2 * chip[0] + chip[1]], land_ref.at[k], send_sems.at[k], recv_sems.at[k], (*chip, c))
            cp.wait_send()
            cp.wait_recv()

    hbm = pl.BlockSpec(memory_space=pltpu.HBM)
    sem = pl.BlockSpec(memory_space=pltpu.SEMAPHORE)
    return pl.pallas_call(
        body, name=name,
        out_shape=(pltpu.HBM(h_thru.shape, h_thru.dtype), pltpu.HBM(land_thru.shape, land_thru.dtype)),
        in_specs=(hbm, hbm, sem, sem, ANY), out_specs=(hbm, hbm), input_output_aliases={0: 0, 1: 1},
        compiler_params=pltpu.CompilerParams(has_side_effects=pltpu.SideEffectType.DATAFLOW_SIDE_EFFECTING))(
            h_thru, land_thru, send_sems, recv_sems, after)


def _swap_start(g, name):
    h = g.shape[1] // 2
    land = (g.shape[0], h, g.shape[2])

    def body(g_ref, land_ref, send_sem, recv_sem, g_thru, land_thru, token):
        x, y, c, me, chips, sib = _place()
        _rcopy(g_ref.at[:, pl.ds((1 - c) * h, h), :], land_ref, send_sem.at[0], recv_sem.at[0], sib).start()
        token[...] = jnp.zeros_like(token)

    hbm = pl.BlockSpec(memory_space=pltpu.HBM)
    sem = pl.BlockSpec(memory_space=pltpu.SEMAPHORE)
    return pl.pallas_call(
        body, name=name,
        out_shape=(pltpu.SemaphoreType.DMA((1,)), pltpu.SemaphoreType.DMA((1,)), pltpu.HBM(g.shape, g.dtype),
                   pltpu.HBM(land, g.dtype), jax.ShapeDtypeStruct((8, LANES), F32)),
        in_specs=(hbm, hbm), out_specs=(sem, sem, hbm, hbm, pl.BlockSpec(memory_space=pltpu.VMEM)),
        input_output_aliases={0: 2, 1: 3},
        compiler_params=pltpu.CompilerParams(has_side_effects=pltpu.SideEffectType.DATAFLOW_SIDE_EFFECTING))(
            pltpu.with_memory_space_constraint(g, pltpu.HBM),
            pltpu.with_memory_space_constraint(lax.empty(land, g.dtype), pltpu.HBM))


def _swap_wait(send_sem, recv_sem, g_thru, land_thru, after, name):
    h = g_thru.shape[1] // 2

    def body(g_ref, land_ref, send_sem, recv_sem, after_ref, g_dead, got_ref):
        x, y, c, me, chips, sib = _place()
        cp = _rcopy(g_ref.at[:, pl.ds((1 - c) * h, h), :], land_ref, send_sem.at[0], recv_sem.at[0], sib)
        cp.wait_send()
        cp.wait_recv()

    hbm = pl.BlockSpec(memory_space=pltpu.HBM)
    sem = pl.BlockSpec(memory_space=pltpu.SEMAPHORE)
    return pl.pallas_call(
        body, name=name,
        out_shape=(pltpu.HBM(g_thru.shape, g_thru.dtype), pltpu.HBM(land_thru.shape, land_thru.dtype)),
        in_specs=(hbm, hbm, sem, sem, ANY), out_specs=(hbm, hbm), input_output_aliases={0: 0, 1: 1},
        compiler_params=pltpu.CompilerParams(has_side_effects=pltpu.SideEffectType.DATAFLOW_SIDE_EFFECTING))(
            g_thru, land_thru, send_sem, recv_sem, after)


def _join_start(parts, name):
    n = len(parts)

    def body(*refs):
        srcs, (send_sems, recv_sems), token = refs[:n], refs[n:n + 2], refs[-1]
        x, y, c, me, chips, sib = _place()
        for a, src in enumerate(srcs):
            h = src.shape[0] // 2
            mine = src.at[pl.ds(c * h, h)]
            _rcopy(mine, mine, send_sems.at[a], recv_sems.at[a], sib).start()
        token[...] = jnp.zeros_like(token)

    hbm = pl.BlockSpec(memory_space=pltpu.HBM)
    sem = pl.BlockSpec(memory_space=pltpu.SEMAPHORE)
    outs = pl.pallas_call(
        body, name=name,
        out_shape=(pltpu.SemaphoreType.DMA((n,)), pltpu.SemaphoreType.DMA((n,)))
        + tuple(pltpu.HBM(p.shape, p.dtype) for p in parts) + (jax.ShapeDtypeStruct((8, LANES), F32),),
        in_specs=(hbm,) * n, out_specs=(sem, sem) + (hbm,) * n + (pl.BlockSpec(memory_space=pltpu.VMEM),),
        input_output_aliases={a: 2 + a for a in range(n)},
        compiler_params=pltpu.CompilerParams(has_side_effects=pltpu.SideEffectType.DATAFLOW_SIDE_EFFECTING))(
            *[pltpu.with_memory_space_constraint(p, pltpu.HBM) for p in parts])
    return outs[0], outs[1], list(outs[2:2 + n]), outs[-1]


def _join_wait(send_sems, recv_sems, parts, after, name):
    n = len(parts)

    def body(*refs):
        srcs, (send_sems, recv_sems) = refs[:n], refs[n:n + 2]
        x, y, c, me, chips, sib = _place()
        for a, src in enumerate(srcs):
            h = src.shape[0] // 2
            mine, theirs = src.at[pl.ds(c * h, h)], src.at[pl.ds((1 - c) * h, h)]
            _rcopy(mine, theirs, send_sems.at[a], recv_sems.at[a], sib).wait_send()
            _rcopy(theirs, theirs, send_sems.at[a], recv_sems.at[a], sib).wait_recv()

    hbm = pl.BlockSpec(memory_space=pltpu.HBM)
    sem = pl.BlockSpec(memory_space=pltpu.SEMAPHORE)
    return pl.pallas_call(
        body, name=name, out_shape=tuple(pltpu.HBM(p.shape, p.dtype) for p in parts),
        in_specs=(hbm,) * n + (sem, sem, ANY), out_specs=(hbm,) * n, input_output_aliases={a: a for a in range(n)},
        compiler_params=pltpu.CompilerParams(has_side_effects=pltpu.SideEffectType.DATAFLOW_SIDE_EFFECTING))(
            *parts, send_sems, recv_sems, after)


def _pad_rows(a, rows):
    return jnp.pad(a, ((0, rows - a.shape[0]), (0, 0)))


def _stack_rows(parts, multiple):
    padded = [_pad_rows(p, -(-p.shape[0] // 8) * 8) for p in parts]
    starts, at = [], 0
    for p in padded:
        starts.append(at)
        at += p.shape[0]
    total = -(-at // multiple) * multiple
    if total > at:
        padded.append(jnp.zeros((total - at, parts[0].shape[1]), parts[0].dtype))
    return jnp.concatenate(padded, axis=0), starts


def kernel(x, ln_pre_even, w_in_even, pool_w, pool_scale, w_out_even, ln_post_even, ln_pre_odd, w_in_odd, sconv_w, dconv_w, dconv_b, cnorm_g, cnorm_b, w_out_odd, ln_post_odd, loss_target, m_ln_pre_even, m_w_in_even, m_pool_w, m_pool_scale, m_w_out_even, m_ln_post_even, m_ln_pre_odd, m_w_in_odd, m_sconv_w, m_dconv_w, m_dconv_b, m_cnorm_g, m_cnorm_b, m_w_out_odd, m_ln_post_odd, v_ln_pre_even, v_w_in_even, v_pool_w, v_pool_scale, v_w_out_even, v_ln_post_even, v_ln_pre_odd, v_w_in_odd, v_sconv_w, v_dconv_w, v_dconv_b, v_cnorm_g, v_cnorm_b, v_w_out_odd, v_ln_post_odd):
    _, s, d = x.shape
    half = d // 2
    cw = half // N_CHIPS
    ng, q, gd = pool_w.shape[1:]
    k3, k31 = sconv_w.shape[1], dconv_w.shape[1]
    x2d, tgt = x[0], loss_target[0]
    me = 2 * lax.axis_index("x") + lax.axis_index("y")
    core = lax.axis_index("c")
    c_arr = jnp.reshape(core, (1,)).astype(jnp.int32)
    me_arr = jnp.reshape(me, (1,)).astype(jnp.int32)
    mc_arr = jnp.stack([me, core]).astype(jnp.int32)

    shards = [w_in_even[0], w_out_even[0], w_in_odd[0], w_out_odd[0]]
    slabs = [_cast_bf16_own_slab(w, me_arr, f"cast_w{n}") for n, w in enumerate(shards)]
    pool_w_b = _cast_bf16(pool_w[0].reshape(ng * q, gd), "cast_pool_w").reshape(ng, q, gd)
    pack_w, at_w = _stack_rows([sconv_w[0], dconv_w[0], dconv_b, cnorm_g, cnorm_b], 8)
    pack_d, at_d = _stack_rows([ln_pre_odd, ln_post_odd], 8)
    win_e, pool_w_f, pack_w_f, pack_d_f = _gather_weights(slabs[:1], pool_w_b, pack_w, pack_d, "gather_first")
    sconv_f = pack_w_f[at_w[0]:at_w[0] + k3]
    dconv_f = pack_w_f[at_w[1]:at_w[1] + k31]
    dconv_b_f, cnorm_g_f, cnorm_b_f = (pack_w_f[at_w[n]:at_w[n] + 1] for n in (2, 3, 4))
    ln_pre_odd_f = pack_d_f[at_d[0]:at_d[0] + 1]
    ln_post_odd_f = pack_d_f[at_d[1]:at_d[1] + 1]

    h0 = _rms_fwd(x2d, ln_pre_even, "rms_pre_even")
    plans = _Multi([_GatherPlan([slabs[1]], at=(0.6, 0.88)), _GatherPlan([slabs[2]], (0, 1, 4), at=(0.6, 0.88))])
    p_e, extra = _mm_nn(h0, win_e, "proj_in_even", plans)
    (wout_e,), (win_o,) = plans.results(extra)
    wout_e = wout_e.reshape(d, d)
    att, ltot, (win_o,) = _sba_fwd(p_e, half, "sba_fwd", _GatherPlan([win_o], (1, 4, 4), at=(0.69, 0.94)))
    y_e = _even_mix_fwd(p_e, att, pool_w_f, pool_scale, d, "even_mix_fwd")
    o_e, x1, h1 = _mm_out_even(y_e, wout_e, x2d, ln_post_even, ln_pre_odd_f, "proj_out_even")
    p_o, (wout_o,) = _mm_nn(h1, win_o, "proj_in_odd", _GatherPlan([slabs[3]]))
    wout_o = wout_o.reshape(d, d)
    y_o, s3, d1 = _odd_mix_fwd(p_o, sconv_f, dconv_f, dconv_b_f, cnorm_g_f, cnorm_b_f, d, "odd_mix_fwd")
    do_o, dx2, loss_blk, dln_post_odd = _mm_out_odd(y_o, wout_o, x1, ln_post_odd_f, tgt, "proj_out_odd_loss")

    dy_o = _mm_nt(do_o, wout_o, "dy_odd")
    g_wout_o = _mm_tn(y_o, do_o, 1, "dw_out_odd")[0].reshape(N_CHIPS, d // N_CHIPS, d)
    (dbc, dgate_o, ds3, dd1, dcnorm_g, dcnorm_b, ddconv_b), (got,) = _odd_bwd_rows(
        p_o, s3, d1, dy_o, cnorm_g_f, cnorm_b_f, d, "odd_bwd_rows", _SwapPlan([g_wout_o]))
    h_wout_o = _half_add(g_wout_o, got, c_arr, "half_add_out_odd")
    dhc, dcc, dga, dgb, dsconv, ddconv = _odd_bwd_conv(p_o, ds3, dd1, sconv_f, dconv_f, d, "odd_bwd_conv")
    dp_o = jnp.concatenate([dhc, dbc, dcc, dga, dgb, dgate_o], axis=1)
    g_win_o, (s_wout_o,) = _mm_tn(h1, dp_o, N_CHIPS, "dw_in_odd", _ScatterPlan([h_wout_o]))
    (dx1, dln_pre_odd, do_e, dln_post_even), (got,) = _mm_in_bwd(
        dp_o, win_o, x1, ln_pre_odd_f, dx2, (o_e, ln_post_even), "dx_odd", _SwapPlan([g_win_o]))
    h_win_o = _half_add(g_win_o, got, c_arr, "half_add_in_odd")

    dy_e = _mm_nt(do_e, wout_e, "dy_even")
    g_wout_e = _mm_tn(y_e, do_e, 1, "dw_out_even")[0].reshape(N_CHIPS, d // N_CHIPS, d)
    (datt, du, dgate_e, dpool_scale, dpool_w), (got,) = _even_mix_bwd(
        p_e, att, dy_e, pool_w_f, pool_scale, d, "even_mix_bwd", _SwapPlan([g_wout_e]))
    h_wout_e = _half_add(g_wout_e, got, c_arr, "half_add_out_even")
    two = lambda v: v.reshape(2, half)
    small_parts = [dpool_scale, two(dln_post_even), two(dln_pre_odd), two(dln_post_odd),
                   dsconv, ddconv, ddconv_b, dcnorm_g, dcnorm_b, dpool_w.reshape(gd, half)]
    small, at_s = _stack_rows(small_parts, 16)
    plans = _Multi([_ScatterPlan([h_win_o]), _SendWholePlan([small])])
    dq, dk, dv, extra = _sba_bwd(p_e, ltot, datt, half, "sba_bwd", plans)
    (s_win_o,), (small1,) = plans.results(extra)
    small2 = _add2(small, small1, "small_add")
    dp_e = jnp.concatenate([dq, dk, dv, du, dgate_e], axis=1)
    plans = _Multi([_ScatterPlan([h_wout_e]), _ShareHalfPlan([small2])])
    g_win_e, extra = _mm_tn(h0, dp_e, N_CHIPS, "dw_in_even", plans)
    (s_wout_e,), (small_got,) = plans.results(extra)
    swap = _swap_start(g_win_e, "swap_in_even_start")
    pairs = [(h_wout_e, s_wout_e), (h_win_o, s_win_o), (h_wout_o, s_wout_o)]
    parts = [_sum_chips(h, r, mc_arr, f"sum_chips{n + 1}") for n, (h, r) in enumerate(pairs)]
    parts.append(_sum_chips_ordered(small2, small_got, mc_arr, "small_sum"))
    join_sems = _join_start(parts, "join_first_start")
    g_win_e, got = _swap_wait(*swap[:4], join_sems[3], "swap_in_even_wait")
    h_win_e = _half_add(g_win_e, got, c_arr, "half_add_in_even")
    send_sems, recv_sems, h_win_e, landing, token = _scatter_start(h_win_e, "scatter_in_even_start")
    (grad_x, dln_pre_even), _ = _mm_in_bwd(dp_e, win_e, x2d, ln_pre_even + token[0:1, 0:1], dx1, None, "dx_even")

    last, at_l = _stack_rows([two(dln_pre_even), jnp.pad(loss_blk[0:1], ((0, 0), (0, half - LANES)))], 16)
    (last1,) = _swap_with_sibling([], [last], "swap_last")
    last2 = _add2(last, last1, "last_add")
    (last_got,) = _scatter_to_chips([], last2, "scatter_last")
    last_sum = _sum_chips_ordered(last2, last_got, mc_arr, "last_sum")
    h_win_e, s_win_e = _scatter_wait(send_sems, recv_sems, h_win_e, landing, last_sum, "scatter_in_even_wait")
    last_sems = _join_start([_sum_chips(h_win_e, s_win_e, mc_arr, "sum_chips0"), last_sum], "join_last_start")
    gw_out_e, gw_in_o, gw_out_o, red = _join_wait(*join_sems[:3], last_sems[3], "join_first_wait")

    def rows(n, cnt):
        return red[at_s[n]:at_s[n] + cnt]

    def mine(a, width):
        return lax.dynamic_slice_in_dim(a, me * width, width, axis=1)

    quarter = d // N_CHIPS
    g_small = {
        "pool_scale": rows(0, 1),
        "ln_post_even": rows(1, 2).reshape(1, d),
        "ln_pre_odd": mine(rows(2, 2).reshape(1, d), quarter),
        "ln_post_odd": mine(rows(3, 2).reshape(1, d), quarter),
        "sconv_w": mine(rows(4, k3), cw),
        "dconv_w": mine(rows(5, k31), cw),
        "dconv_b": mine(rows(6, 1), cw),
        "cnorm_g": mine(rows(7, 1), cw),
        "cnorm_b": mine(rows(8, 1), cw),
        "pool_w": lax.dynamic_slice_in_dim(rows(9, gd).reshape(ng, gd, gd), me * q, q, axis=1).reshape(ng * q, gd),
    }
    w2d = {
        "ln_pre_even": ln_pre_even, "w_in_even": w_in_even[0], "pool_w": pool_w[0].reshape(ng * q, gd),
        "pool_scale": pool_scale, "w_out_even": w_out_even[0], "ln_post_even": ln_post_even, "ln_pre_odd": ln_pre_odd,
        "w_in_odd": w_in_odd[0], "sconv_w": sconv_w[0], "dconv_w": dconv_w[0], "dconv_b": dconv_b, "cnorm_g": cnorm_g,
        "cnorm_b": cnorm_b, "w_out_odd": w_out_odd[0], "ln_post_odd": ln_post_odd,
    }
    moments = {
        "ln_pre_even": (m_ln_pre_even, v_ln_pre_even), "w_in_even": (m_w_in_even, v_w_in_even),
        "pool_w": (m_pool_w, v_pool_w), "pool_scale": (m_pool_scale, v_pool_scale),
        "w_out_even": (m_w_out_even, v_w_out_even), "ln_post_even": (m_ln_post_even, v_ln_post_even),
        "ln_pre_odd": (m_ln_pre_odd, v_ln_pre_odd), "w_in_odd": (m_w_in_odd, v_w_in_odd),
        "sconv_w": (m_sconv_w, v_sconv_w), "dconv_w": (m_dconv_w, v_dconv_w), "dconv_b": (m_dconv_b, v_dconv_b),
        "cnorm_g": (m_cnorm_g, v_cnorm_g), "cnorm_b": (m_cnorm_b, v_cnorm_b),
        "w_out_odd": (m_w_out_odd, v_w_out_odd), "ln_post_odd": (m_ln_post_odd, v_ln_post_odd),
    }
    def update(name, g):
        m_in, v_in = moments[name]
        w = w2d[name]
        return _adamw(w, g, m_in.reshape(w.shape), v_in.reshape(w.shape), "adamw_" + name)[0]

    updates = {"w_in_odd": update("w_in_odd", gw_in_o)}
    gw_in_e, red_last = _join_wait(*last_sems[:3], updates["w_in_odd"][1], "join_last_wait")
    loss = red_last[at_l[1], 0]
    g_small["ln_pre_even"] = red_last[at_l[0]:at_l[0] + 2].reshape(1, d)
    for name, g in dict(g_small, w_in_even=gw_in_e, w_out_even=gw_out_e, w_out_odd=gw_out_o).items():
        updates[name] = update(name, g)
    outs = [[u.reshape(moments[name][0].shape) for u in updates[name]] for name in w2d]
    grads_out, deltas, new_m, new_v = zip(*outs)
    return (loss, grad_x.reshape(x.shape), *grads_out, *deltas, *new_m, *new_v)
```

```python
import functools
import math

import jax
import jax.numpy as jnp
from jax import lax
from jax.experimental import pallas as pl
from jax.experimental.pallas import tpu as pltpu

F32 = jnp.float32
BF16 = jnp.bfloat16
EPS = 1e-6
N_CHIPS = 4
VMEM_LIMIT_V7X = 56 << 20
HEAD_DIM = 128
ATT_BLOCK = 256
POOL_WINDOWS = (2, 4, 8, 16)
ROW_TILE = 256
POOL_HALO = 16
CONV_HALO = 32
LANES = 128
ADAM_LR, ADAM_B1, ADAM_B2, ADAM_EPS, ADAM_WD, ADAM_STEP = 0.001, 0.9, 0.999, 1e-08, 0.01, 10
MESH_ID = pl.DeviceIdType.MESH
ANY = pl.BlockSpec(memory_space=pl.ANY)


def _cp(*sem):
    return pltpu.CompilerParams(dimension_semantics=sem or None, vmem_limit_bytes=VMEM_LIMIT_V7X)


def _pick_tile(n, cap):
    best = None
    for t in range(LANES, min(n, cap) + 1, LANES):
        if n % t == 0:
            best = t
    assert best is not None, (n, cap)
    return best


def _sigmoid(x):
    return 1.0 / (1.0 + jnp.exp(-x))


def _silu(x):
    return x * _sigmoid(x)


def _dsilu(x):
    s = _sigmoid(x)
    return s * (1.0 + x * (1.0 - s))


def _log_sigmoid(z):
    return jnp.minimum(z, 0.0) - jnp.log(1.0 + jnp.exp(-jnp.abs(z)))


def _rms_stats(x):
    r = lax.rsqrt(jnp.mean(x * x, axis=-1, keepdims=True) + EPS)
    return x * r, r


def _rms_bwd(dh, xhat, r, g):
    dxh = dh * g
    dx = r * (dxh - xhat * jnp.mean(dxh * xhat, axis=-1, keepdims=True))
    return dx, jnp.sum(dh * xhat, axis=0, keepdims=True)


def _acc_rows(ref, first, val):
    @pl.when(first)
    def _():
        ref[...] = val

    @pl.when(jnp.logical_not(first))
    def _():
        ref[...] += val


def _rcopy(src, dst, ssem, rsem, dev):
    return pltpu.make_async_remote_copy(src_ref=src, dst_ref=dst, send_sem=ssem, recv_sem=rsem,
                                        device_id=dev, device_id_type=MESH_ID)


def _place():
    x, y, c = lax.axis_index("x"), lax.axis_index("y"), lax.axis_index("c")
    chips = [(1 - x, y), (x, 1 - y), (1 - x, 1 - y)]
    return x, y, c, 2 * x + y, chips, (x, y, 1 - c)


class _GatherPlan:
    PER_ARRAY = 7

    def __init__(self, arrays, part=(0, 1, 1), at=(0.5, 0.8)):
        self.operands = list(arrays)
        self.out_shapes = [jax.ShapeDtypeStruct(a.shape, a.dtype) for a in arrays]
        self.aliases = {i: i for i in range(len(arrays))}
        self.nsems = self.PER_ARRAY * len(arrays)
        self.base = 0
        self.halves = [a.shape[1] // 2 for a in arrays]
        self.part = part
        self.at = at

    def schedule(self):
        return [(0.0, self.start), (self.at[0], self.relay), (self.at[1], self.relay_far)]

    def _rows(self, ref, a, chip, half, quarter=None):
        lo, hi, n = self.part
        h = self.halves[a]
        first, size = half * h + lo * h // n, (hi - lo) * h // n
        if quarter is not None:
            first, size = first + quarter * (size // 2), size // 2
        return ref.at[chip, pl.ds(first, size)]

    def _copy(self, src, dst, a, n, ssem, rsem, dev):
        return _rcopy(src, dst, ssem.at[self.base + self.PER_ARRAY * a + n], rsem.at[self.base + self.PER_ARRAY * a + n], dev)

    def _own(self, ins, outs, ssem, rsem):
        x, y, c, me, chips, sib = _place()
        return [self._copy(self._rows(ins[a], a, me, c), self._rows(outs[a], a, me, c), a, k, ssem, rsem, (*chips[k], c))
                for a in range(len(ins)) for k in (0, 1)]

    def _relays(self, outs, ssem, rsem, a, k):
        x, y, c, me, chips, sib = _place()
        chip = 2 * chips[k][0] + chips[k][1]
        whole, quarter = self._rows(outs[a], a, chip, c), self._rows(outs[a], a, chip, c, k)
        return (self._copy(whole, whole, a, k, ssem, rsem, (*chips[k], c)),
                self._copy(quarter, quarter, a, 2 + k, ssem, rsem, (*chips[1 - k], c)),
                self._copy(whole, whole, a, 4 + k, ssem, rsem, sib))

    def _far(self, outs, ssem, rsem, a):
        x, y, c, me, chips, sib = _place()
        chip = 2 * chips[2][0] + chips[2][1]
        whole = self._rows(outs[a], a, chip, c)
        got = [self._copy(q, q, a, 2 + k, ssem, rsem, (*chips[1 - k], c))
               for k, q in enumerate([self._rows(outs[a], a, chip, c, 0), self._rows(outs[a], a, chip, c, 1)])]
        return got, self._copy(whole, whole, a, 6, ssem, rsem, sib)

    def start(self, ins, outs, ssem, rsem):
        for cp in self._own(ins, outs, ssem, rsem):
            cp.start()

    def relay(self, ins, outs, ssem, rsem):
        for a in range(len(outs)):
            for k in (0, 1):
                landed, onward, to_sibling = self._relays(outs, ssem, rsem, a, k)
                landed.wait_recv()
                onward.start()
                to_sibling.start()

    def relay_far(self, ins, outs, ssem, rsem):
        for a in range(len(outs)):
            got, to_sibling = self._far(outs, ssem, rsem, a)
            for cp in got:
                cp.wait_recv()
            to_sibling.start()

    def finish(self, ins, outs, ssem, rsem):
        x, y, c, me, chips, sib = _place()
        for a in range(len(outs)):
            for k in range(3):
                ref = self._rows(outs[a], a, 2 * chips[k][0] + chips[k][1], 1 - c)
                self._copy(ref, ref, a, 4 + k, ssem, rsem, sib).wait_recv()
        for cp in self._own(ins, outs, ssem, rsem):
            cp.wait_send()
        for a in range(len(outs)):
            for k in (0, 1):
                _, onward, to_sibling = self._relays(outs, ssem, rsem, a, k)
                onward.wait_send()
                to_sibling.wait_send()
            self._far(outs, ssem, rsem, a)[1].wait_send()


class _ScatterPlan:
    def __init__(self, arrays, part=(0, 1, 1), into=None):
        self.n = len(arrays)
        self.operands = list(arrays) + list(into or [])
        self.out_shapes = [jax.ShapeDtypeStruct((3,) + a.shape[1:], a.dtype) for a in arrays]
        self.aliases = {self.n + i: i for i in range(self.n)} if into else {}
        self.nsems = 3 * self.n
        self.base = 0
        self.part = part

    def _copies(self, ins, outs, ssem, rsem):
        x, y, c, me, chips, sib = _place()
        lo, hi, n = self.part
        out = []
        for a in range(self.n):
            h = ins[a].shape[1]
            rows = pl.ds(lo * h // n, (hi - lo) * h // n)
            for k, chip in enumerate(chips):
                out.append(_rcopy(ins[a].at[2 * chip[0] + chip[1], rows], outs[a].at[k, rows],
                                  ssem.at[self.base + 3 * a + k], rsem.at[self.base + 3 * a + k], (*chip, c)))
        return out

    def schedule(self):
        return [(0.0, self.start)]

    def start(self, ins, outs, ssem, rsem):
        for cp in self._copies(ins, outs, ssem, rsem):
            cp.start()

    def finish(self, ins, outs, ssem, rsem):
        cps = self._copies(ins, outs, ssem, rsem)
        for cp in cps:
            cp.wait_recv()
        for cp in cps:
            cp.wait_send()


class _ShareHalfPlan(_ScatterPlan):
    def __init__(self, arrays):
        super().__init__(arrays)
        self.out_shapes = [jax.ShapeDtypeStruct((3, a.shape[0] // 2, a.shape[1]), a.dtype) for a in arrays]

    def _copies(self, ins, outs, ssem, rsem):
        x, y, c, me, chips, sib = _place()
        out = []
        for a in range(self.n):
            rh = ins[a].shape[0] // 2
            for k, chip in enumerate(chips):
                out.append(_rcopy(ins[a].at[pl.ds(c * rh, rh)], outs[a].at[k],
                                  ssem.at[self.base + 3 * a + k], rsem.at[self.base + 3 * a + k], (*chip, c)))
        return out


class _SwapPlan:
    def __init__(self, grads):
        self.operands = list(grads)
        self.out_shapes = [jax.ShapeDtypeStruct((g.shape[0], g.shape[1] // 2, g.shape[2]), g.dtype) for g in grads]
        self.aliases = {}
        self.nsems = len(grads)
        self.base = 0

    def _copies(self, ins, outs, ssem, rsem):
        x, y, c, me, chips, sib = _place()
        out = []
        for a, src in enumerate(ins):
            h = src.shape[1] // 2
            out.append(_rcopy(src.at[:, pl.ds((1 - c) * h, h), :], outs[a], ssem.at[self.base + a], rsem.at[self.base + a], sib))
        return out

    def schedule(self):
        return [(0.0, self.start)]

    def start(self, ins, outs, ssem, rsem):
        for cp in self._copies(ins, outs, ssem, rsem):
            cp.start()

    def finish(self, ins, outs, ssem, rsem):
        cps = self._copies(ins, outs, ssem, rsem)
        for cp in cps:
            cp.wait_recv()
        for cp in cps:
            cp.wait_send()


class _SendWholePlan(_SwapPlan):
    def __init__(self, arrays):
        self.operands = list(arrays)
        self.out_shapes = [jax.ShapeDtypeStruct(a.shape, a.dtype) for a in arrays]
        self.aliases = {}
        self.nsems = len(arrays)
        self.base = 0

    def _copies(self, ins, outs, ssem, rsem):
        x, y, c, me, chips, sib = _place()
        return [_rcopy(src, outs[a], ssem.at[self.base + a], rsem.at[self.base + a], sib) for a, src in enumerate(ins)]


class _Multi:
    def __init__(self, plans):
        self.plans = plans
        self.operands, self.out_shapes, self.aliases, self.nsems = [], [], {}, 0
        self.spans = []
        for p in plans:
            ni, no = len(self.operands), len(self.out_shapes)
            self.spans.append((ni, ni + len(p.operands), no, no + len(p.out_shapes)))
            self.aliases.update({ni + i: no + j for i, j in p.aliases.items()})
            p.base = self.nsems
            self.nsems += p.nsems
            self.operands += p.operands
            self.out_shapes += p.out_shapes

    def schedule(self):
        def bound(fn, span):
            i0, i1, o0, o1 = span
            return lambda ins, outs, ssem, rsem: fn(ins[i0:i1], outs[o0:o1], ssem, rsem)

        stages = [(at, bound(fn, span)) for p, span in zip(self.plans, self.spans) for at, fn in p.schedule()]
        return sorted(stages, key=lambda s: s[0])

    def finish(self, ins, outs, ssem, rsem):
        for p, (i0, i1, o0, o1) in zip(self.plans, self.spans):
            p.finish(ins[i0:i1], outs[o0:o1], ssem, rsem)

    def results(self, extra):
        return [list(extra[o0:o1]) for (_, _, o0, o1) in self.spans]


class _Host:
    def __init__(self, comm, in_specs, out_specs, out_shape, scratch):
        self.comm = comm
        self.n_in, self.n_out = len(in_specs), len(out_specs)
        self.in_specs, self.out_specs, self.out_shape, self.scratch = list(in_specs), list(out_specs), list(out_shape), list(scratch)
        self.aliases = {}
        self.args = []
        if comm is not None:
            self.in_specs += [ANY] * len(comm.operands)
            self.out_specs += [ANY] * len(comm.out_shapes)
            self.out_shape += comm.out_shapes
            self.scratch += [pltpu.SemaphoreType.DMA((comm.nsems,)), pltpu.SemaphoreType.DMA((comm.nsems,))]
            self.aliases = {self.n_in + i: self.n_out + j for i, j in comm.aliases.items()}
            self.args = list(comm.operands)

    def split(self, refs):
        nc = len(self.args)
        nco = len(self.out_shape) - self.n_out
        ins, p = refs[:self.n_in], self.n_in + nc
        outs, rest = refs[p:p + self.n_out], refs[p + self.n_out + nco:]
        self._cargs = None
        if self.comm is not None:
            self._cargs = (refs[self.n_in:p], refs[p + self.n_out:p + self.n_out + nco], rest[-2], rest[-1])
            rest = rest[:-2]
        return ins, outs, rest

    def before(self, step, total):
        if self.comm is None:
            return

        for at, stage in self.comm.schedule():
            pl.when(step == min(total - 1, int(at * total)))(functools.partial(stage, *self._cargs))

    def after(self, step, total):
        if self.comm is None:
            return

        @pl.when(step == total - 1)
        def _():
            self.comm.finish(*self._cargs)

    def results(self, outs):
        return outs[:self.n_out], outs[self.n_out:]


def _cast_bf16(x, name):
    r, c = x.shape
    tr = ROW_TILE if r % ROW_TILE == 0 else r

    def body(x_ref, o_ref):
        o_ref[...] = x_ref[...].astype(BF16)

    return pl.pallas_call(
        body, name=name, grid=(r // tr,),
        in_specs=[pl.BlockSpec((tr, c), lambda i: (i, 0))],
        out_specs=pl.BlockSpec((tr, c), lambda i: (i, 0)),
        out_shape=jax.ShapeDtypeStruct((r, c), BF16), compiler_params=_cp("parallel"))(x)


def _cast_bf16_own_slab(x, me_arr, name):
    r, c = x.shape
    tr = ROW_TILE if r % ROW_TILE == 0 else r

    def body(me_ref, x_ref, o_ref):
        o_ref[...] = x_ref[...].astype(BF16)

    return pl.pallas_call(
        body, name=name,
        grid_spec=pltpu.PrefetchScalarGridSpec(
            num_scalar_prefetch=1, grid=(r // tr,),
            in_specs=[pl.BlockSpec((tr, c), lambda i, me: (i, 0))],
            out_specs=pl.BlockSpec((None, tr, c), lambda i, me: (me[0], i, 0))),
        out_shape=jax.ShapeDtypeStruct((N_CHIPS, r, c), BF16), compiler_params=_cp("parallel"))(me_arr, x)


def _rms_fwd(x, g, name):
    s, d = x.shape

    def body(x_ref, g_ref, h_ref):
        xhat, _ = _rms_stats(x_ref[...])
        h_ref[...] = (xhat * g_ref[...]).astype(BF16)

    return pl.pallas_call(
        body, name=name, grid=(s // ROW_TILE,),
        in_specs=[pl.BlockSpec((ROW_TILE, d), lambda i: (i, 0)), pl.BlockSpec((1, d), lambda i: (0, 0))],
        out_specs=pl.BlockSpec((ROW_TILE, d), lambda i: (i, 0)),
        out_shape=jax.ShapeDtypeStruct((s, d), BF16), compiler_params=_cp("parallel"))(x, g)


def _mm_nn(a, w3, name, comm=None):
    m, k = a.shape
    nsh, _, ns = w3.shape
    tm = 512 if m % 512 == 0 else ROW_TILE
    tn = _pick_tile(ns, 1024)
    per = ns // tn
    grid = (nsh * per, m // tm)
    host = _Host(comm,
                 [pl.BlockSpec((tm, k), lambda n, i: (i, 0)), pl.BlockSpec((None, k, tn), lambda n, i: (n // per, 0, n % per))],
                 [pl.BlockSpec((tm, tn), lambda n, i: (i, n))], [jax.ShapeDtypeStruct((m, nsh * ns), F32)], [])

    def body(*refs):
        (a_ref, w_ref), (o_ref,), _ = host.split(refs)
        step = pl.program_id(0) * grid[1] + pl.program_id(1)
        host.before(step, grid[0] * grid[1])
        o_ref[...] = jnp.dot(a_ref[...], w_ref[...], preferred_element_type=F32)
        host.after(step, grid[0] * grid[1])

    outs = pl.pallas_call(
        body, name=name, grid=grid, in_specs=host.in_specs, out_specs=host.out_specs, out_shape=host.out_shape,
        scratch_shapes=host.scratch, input_output_aliases=host.aliases,
        compiler_params=_cp("arbitrary", "arbitrary"))(a, w3, *host.args)
    (out,), extra = host.results(outs)
    return out, extra


def _mm_nt(a, b, name):
    m, k = a.shape
    n = b.shape[0]
    tm = 512 if m % 512 == 0 else ROW_TILE

    def body(a_ref, b_ref, o_ref):
        o_ref[...] = lax.dot_general(a_ref[...], b_ref[...], (((1,), (1,)), ((), ())), preferred_element_type=F32)

    return pl.pallas_call(
        body, name=name, grid=(m // tm,),
        in_specs=[pl.BlockSpec((tm, k), lambda i: (i, 0)), pl.BlockSpec((n, k), lambda i: (0, 0))],
        out_specs=pl.BlockSpec((tm, n), lambda i: (i, 0)),
        out_shape=jax.ShapeDtypeStruct((m, n), F32), compiler_params=_cp("parallel"))(a, b)


def _mm_tn(a, b, nsh, name, comm=None):
    s, m = a.shape
    n = b.shape[1]
    ns = n // nsh
    tm = 512 if m % 512 == 0 else ROW_TILE
    tn = _pick_tile(ns, 1024)
    per = ns // tn
    grid = (nsh * per, m // tm)
    host = _Host(comm, [pl.BlockSpec((s, tm), lambda j, i: (0, i)), pl.BlockSpec((s, tn), lambda j, i: (0, j))],
                 [pl.BlockSpec((None, tm, tn), lambda j, i: (j // per, i, j % per))],
                 [jax.ShapeDtypeStruct((nsh, m, ns), BF16)], [])

    def body(*refs):
        (a_ref, b_ref), (o_ref,), _ = host.split(refs)
        step = pl.program_id(0) * grid[1] + pl.program_id(1)
        host.before(step, grid[0] * grid[1])
        o_ref[...] = lax.dot_general(a_ref[...], b_ref[...], (((0,), (0,)), ((), ())),
                                     preferred_element_type=F32).astype(BF16)
        host.after(step, grid[0] * grid[1])

    outs = pl.pallas_call(
        body, name=name, grid=grid, in_specs=host.in_specs, out_specs=host.out_specs, out_shape=host.out_shape,
        scratch_shapes=host.scratch, input_output_aliases=host.aliases,
        compiler_params=_cp("arbitrary", "arbitrary"))(a, b, *host.args)
    (out,), extra = host.results(outs)
    return out, extra


def _tri(n, rel):
    row = lax.broadcasted_iota(jnp.int32, (2 * n, n), 0)
    col = lax.broadcasted_iota(jnp.int32, (2 * n, n), 1)
    return jnp.where(rel(jnp.where(row >= n, row - n, row), col), 1.0, 0.0).astype(BF16)


def _dot_split(x, tri2):
    hi = x.astype(BF16)
    lo = (x - hi.astype(F32)).astype(BF16)
    return jnp.dot(jnp.concatenate([hi, lo], axis=1), tri2, preferred_element_type=F32)


def _nt(a, b):
    return lax.dot_general(a, b, (((1,), (1,)), ((), ())), preferred_element_type=F32)


def _tn(a, b):
    return lax.dot_general(a, b, (((0,), (0,)), ((), ())), preferred_element_type=F32)


def _heads_per_step(nh):
    return max(h for h in (1, 2, 4) if nh % h == 0)


def _sba_fwd(p, sbw, name, comm=None):
    s = p.shape[0]
    nh = sbw // HEAD_DIM
    hp = _heads_per_step(nh)
    ngrp, hw = nh // hp, hp * HEAD_DIM
    blk = ATT_BLOCK
    nq = s // blk
    scale = 1.0 / math.sqrt(HEAD_DIM)
    host = _Host(comm,
                 [pl.BlockSpec((blk, hw), lambda g, i: (i, g)),
                  pl.BlockSpec((s, hw), lambda g, i: (0, ngrp + g)),
                  pl.BlockSpec((s, hw), lambda g, i: (0, 2 * ngrp + g))],
                 [pl.BlockSpec((blk, hw), lambda g, i: (i, g))] * 2,
                 [jax.ShapeDtypeStruct((s, sbw), F32)] * 2,
                 [pltpu.VMEM((s, hw), BF16)] * 2)

    def body(*refs):
        (q_ref, k_ref, v_ref), (o_ref, lt_ref), (kb_ref, vb_ref) = host.split(refs)
        i = pl.program_id(1)
        step = pl.program_id(0) * nq + i
        host.before(step, ngrp * nq)

        @pl.when(i == 0)
        def _():
            kb_ref[...] = k_ref[...].astype(BF16)
            vb_ref[...] = v_ref[...].astype(BF16)

        heads = [slice(h * HEAD_DIM, (h + 1) * HEAD_DIM) for h in range(hp)]
        qs = [q_ref[:, hd].astype(BF16) for hd in heads]
        later = _tri(blk, lambda r, c: r > c)
        causal = lax.broadcasted_iota(jnp.int32, (blk, blk), 1) < lax.broadcasted_iota(jnp.int32, (blk, blk), 0)

        def key_block(j, carry, diagonal):
            rows = pl.ds(pl.multiple_of(j * blk, blk), blk)
            hs = range(hp)
            z = [_nt(qs[h], kb_ref[rows, heads[h]]) * scale for h in hs]
            ls = [_log_sigmoid(z[h]) for h in hs]
            lm = [jnp.where(causal, ls[h] - z[h], 0.0) if diagonal else ls[h] - z[h] for h in hs]
            stay = [_dot_split(lm[h], later) for h in hs]
            w = [jnp.exp(ls[h] + stay[h] + carry[h][1]) for h in hs]
            if diagonal:
                w = [jnp.where(causal, w[h], 0.0) for h in hs]
            acc = [carry[h][0] + jnp.dot(w[h].astype(BF16), vb_ref[rows, heads[h]], preferred_element_type=F32) for h in hs]
            return tuple((acc[h], carry[h][1] + jnp.sum(lm[h], axis=1, keepdims=True)) for h in hs)

        init = tuple((jnp.zeros((blk, HEAD_DIM), F32), jnp.zeros((blk, 1), F32)) for _ in heads)
        carry = key_block(i, init, True)
        carry = lax.fori_loop(0, i, lambda n, c: key_block(i - 1 - n, c, False), carry)
        for h, hd in enumerate(heads):
            o_ref[:, hd] = carry[h][0]
            lt_ref[:, hd] = jnp.broadcast_to(carry[h][1], (blk, HEAD_DIM))
        host.after(step, ngrp * nq)

    outs = pl.pallas_call(
        body, name=name, grid=(ngrp, nq), in_specs=host.in_specs, out_specs=host.out_specs, out_shape=host.out_shape,
        scratch_shapes=host.scratch, input_output_aliases=host.aliases,
        compiler_params=_cp("arbitrary", "arbitrary"))(p, p, p, *host.args)
    (out, ltot), extra = host.results(outs)
    return out, ltot, extra


def _sba_bwd(p, ltot, dout, sbw, name, comm=None):
    s = p.shape[0]
    nh = sbw // HEAD_DIM
    hp = _heads_per_step(nh)
    ngrp, hw = nh // hp, hp * HEAD_DIM
    blk = ATT_BLOCK
    nq = s // blk
    scale = 1.0 / math.sqrt(HEAD_DIM)
    blk_spec = pl.BlockSpec((blk, hw), lambda g, i: (i, g))
    col_spec = pl.BlockSpec((s, hw), lambda g, i: (0, g))
    host = _Host(comm,
                 [blk_spec, pl.BlockSpec((s, hw), lambda g, i: (0, ngrp + g)),
                  pl.BlockSpec((s, hw), lambda g, i: (0, 2 * ngrp + g)), blk_spec, blk_spec],
                 [blk_spec, col_spec, col_spec], [jax.ShapeDtypeStruct((s, sbw), BF16)] * 3,
                 [pltpu.VMEM((s, hw), BF16)] * 2 + [pltpu.VMEM((s, hw), F32)] * 2)

    def body(*refs):
        (q_ref, k_ref, v_ref, lt_ref, do_ref), (dq_ref, dk_ref, dv_ref), (kb_ref, vb_ref, dka_ref, dva_ref) = host.split(refs)
        i = pl.program_id(1)
        step = pl.program_id(0) * nq + i
        host.before(step, ngrp * nq)

        @pl.when(i == 0)
        def _():
            kb_ref[...] = k_ref[...].astype(BF16)
            vb_ref[...] = v_ref[...].astype(BF16)
            dka_ref[...] = jnp.zeros_like(dka_ref)
            dva_ref[...] = jnp.zeros_like(dva_ref)

        heads = [slice(h * HEAD_DIM, (h + 1) * HEAD_DIM) for h in range(hp)]
        qs = [q_ref[:, hd].astype(BF16) for hd in heads]
        dos = [do_ref[:, hd].astype(BF16) for hd in heads]
        ltots = [lt_ref[:, h * HEAD_DIM:h * HEAD_DIM + 1] for h in range(hp)]
        upto = _tri(blk, lambda r, c: r <= c)
        before = _tri(blk, lambda r, c: r < c)
        causal = lax.broadcasted_iota(jnp.int32, (blk, blk), 1) < lax.broadcasted_iota(jnp.int32, (blk, blk), 0)

        def key_block(j, carry, diagonal):
            rows = pl.ds(pl.multiple_of(j * blk, blk), blk)
            hs = range(hp)
            kj = [kb_ref[rows, heads[h]] for h in hs]
            vj = [vb_ref[rows, heads[h]] for h in hs]
            z = [_nt(qs[h], kj[h]) * scale for h in hs]
            dw = [_nt(dos[h], vj[h]) for h in hs]
            ls = [_log_sigmoid(z[h]) for h in hs]
            lm = [jnp.where(causal, ls[h] - z[h], 0.0) if diagonal else ls[h] - z[h] for h in hs]
            stay = [ltots[h] - carry[h][1] - _dot_split(lm[h], upto) for h in hs]
            w = [jnp.exp(ls[h] + stay[h]) for h in hs]
            if diagonal:
                w = [jnp.where(causal, w[h], 0.0) for h in hs]
            da = [dw[h] * w[h] for h in hs]
            sig = [jnp.exp(ls[h]) for h in hs]
            chain = [sig[h] * (carry[h][2] + _dot_split(da[h], before)) for h in hs]
            if diagonal:
                chain = [jnp.where(causal, chain[h], 0.0) for h in hs]
            dzb = [((da[h] * (1.0 - sig[h]) - chain[h]) * scale).astype(BF16) for h in hs]
            dq = [carry[h][0] + jnp.dot(dzb[h], kj[h], preferred_element_type=F32) for h in hs]
            for h in hs:
                dka_ref[rows, heads[h]] += _tn(dzb[h], qs[h])
            for h in hs:
                dva_ref[rows, heads[h]] += _tn(w[h].astype(BF16), dos[h])
            return tuple((dq[h], carry[h][1] + jnp.sum(lm[h], axis=1, keepdims=True),
                          carry[h][2] + jnp.sum(da[h], axis=1, keepdims=True)) for h in hs)

        zero = jnp.zeros((blk, 1), F32)
        init = tuple((jnp.zeros((blk, HEAD_DIM), F32), zero, zero) for _ in heads)
        carry = lax.fori_loop(0, i, lambda j, c: key_block(j, c, False), init)
        carry = key_block(i, carry, True)
        for h, hd in enumerate(heads):
            dq_ref[:, hd] = carry[h][0].astype(BF16)

        @pl.when(i == nq - 1)
        def _():
            dk_ref[...] = dka_ref[...].astype(BF16)
            dv_ref[...] = dva_ref[...].astype(BF16)

        host.after(step, ngrp * nq)

    outs = pl.pallas_call(
        body, name=name, grid=(ngrp, nq), in_specs=host.in_specs, out_specs=host.out_specs, out_shape=host.out_shape,
        scratch_shapes=host.scratch, input_output_aliases=host.aliases,
        compiler_params=_cp("arbitrary", "arbitrary"))(p, p, p, ltot, dout, *host.args)
    (dq, dk, dv), extra = host.results(outs)
    return dq, dk, dv, extra


def _pool_groups(pad_ref, tile, row0, gd, halo):
    row = row0 + lax.broadcasted_iota(jnp.int32, (tile, 1), 0)
    out = []
    for gi, win in enumerate(POOL_WINDOWS):
        cs = slice(gi * gd, (gi + 1) * gd)
        tok = pad_ref[halo:halo + tile, cs]
        acc = tok
        for j in range(1, win):
            acc = acc + pad_ref[halo - j:halo - j + tile, cs]
        cnt = jnp.minimum(win, row + 1).astype(F32)
        out.append(acc / cnt - tok)
    return out


def _even_mix_fwd(p, att, pool_w, pool_scale, d, name):
    s = p.shape[0]
    half = d // 2
    gd = half // len(POOL_WINDOWS)
    t, hb = ROW_TILE, POOL_HALO

    def body(u_ref, uh_ref, g_ref, a_ref, pw_ref, sc_ref, y_ref, pad_ref):
        i = pl.program_id(0)
        pad_ref[0:hb, :] = jnp.where(i > 0, uh_ref[...], 0.0)
        pad_ref[hb:, :] = u_ref[...]
        pooled = _pool_groups(pad_ref, t, i * t, gd, hb)
        for gi in range(len(POOL_WINDOWS)):
            cs = slice(gi * gd, (gi + 1) * gd)
            po = jnp.dot(pooled[gi].astype(BF16), pw_ref[gi], preferred_element_type=F32) * sc_ref[:, cs]
            y_ref[:, half + gi * gd:half + (gi + 1) * gd] = (po * _silu(g_ref[:, half + gi * gd:half + (gi + 1) * gd])).astype(BF16)
        y_ref[:, :half] = (a_ref[...] * _silu(g_ref[:, :half])).astype(BF16)

    return pl.pallas_call(
        body, name=name, grid=(s // t,),
        in_specs=[pl.BlockSpec((t, half), lambda i: (i, 3)),
                  pl.BlockSpec((hb, half), lambda i: (jnp.maximum(i * (t // hb) - 1, 0), 3)),
                  pl.BlockSpec((t, d), lambda i: (i, 2)),
                  pl.BlockSpec((t, half), lambda i: (i, 0)),
                  pl.BlockSpec(pool_w.shape, lambda i: (0, 0, 0)),
                  pl.BlockSpec((1, half), lambda i: (0, 0))],
        out_specs=pl.BlockSpec((t, d), lambda i: (i, 0)),
        out_shape=jax.ShapeDtypeStruct((s, d), BF16),
        scratch_shapes=[pltpu.VMEM((hb + t, half), F32)],
        compiler_params=_cp("parallel"))(p, p, p, att, pool_w, pool_scale)


def _even_mix_bwd(p, att, dy, pool_w, pool_scale, d, name, comm=None):
    s = p.shape[0]
    half = d // 2
    ng = len(POOL_WINDOWS)
    gd = half // ng
    t, hb = ROW_TILE, POOL_HALO
    nt = s // t
    host = _Host(
        comm,
        [pl.BlockSpec((t, half), lambda i: (i, 3)),
         pl.BlockSpec((hb, half), lambda i: (jnp.maximum(i * (t // hb) - 1, 0), 3)),
         pl.BlockSpec((t, d), lambda i: (i, 2)),
         pl.BlockSpec((hb, half), lambda i: (jnp.minimum((i + 1) * (t // hb), s // hb - 1), 5)),
         pl.BlockSpec((t, half), lambda i: (i, 0)),
         pl.BlockSpec((t, d), lambda i: (i, 0)),
         pl.BlockSpec((hb, half), lambda i: (jnp.minimum((i + 1) * (t // hb), s // hb - 1), 1)),
         pl.BlockSpec(pool_w.shape, lambda i: (0, 0, 0)),
         pl.BlockSpec((1, half), lambda i: (0, 0))],
        [pl.BlockSpec((t, half), lambda i: (i, 0)),
         pl.BlockSpec((t, half), lambda i: (i, 0)),
         pl.BlockSpec((t, d), lambda i: (i, 0)),
         pl.BlockSpec((1, half), lambda i: (0, 0)),
         pl.BlockSpec((ng, gd, gd), lambda i: (0, 0, 0))],
        [jax.ShapeDtypeStruct((s, half), F32), jax.ShapeDtypeStruct((s, half), BF16),
         jax.ShapeDtypeStruct((s, d), BF16), jax.ShapeDtypeStruct((1, half), F32),
         jax.ShapeDtypeStruct((ng, gd, gd), F32)],
        [pltpu.VMEM((hb + t, half), F32), pltpu.VMEM((t + hb, half), F32)])

    def body(*refs):
        ((u_ref, uh_ref, g_ref, gh_ref, a_ref, dy_ref, dyh_ref, pw_ref, sc_ref),
         (da_ref, du_ref, dg_ref, dsc_ref, dpw_ref), (pad_ref, dn_ref)) = host.split(refs)
        i = pl.program_id(0)
        host.before(i, nt)
        first = i == 0
        pad_ref[0:hb, :] = jnp.where(i > 0, uh_ref[...], 0.0)
        pad_ref[hb:, :] = u_ref[...]
        pooled = _pool_groups(pad_ref, t, i * t, gd, hb)
        g1 = g_ref[:, :half]
        dy1 = dy_ref[:, :half]
        da_ref[...] = dy1 * _silu(g1)
        dg_ref[:, :half] = (dy1 * a_ref[...] * _dsilu(g1)).astype(BF16)
        row = i * t + lax.broadcasted_iota(jnp.int32, (t + hb, 1), 0)
        for gi, win in enumerate(POOL_WINDOWS):
            cs = slice(gi * gd, (gi + 1) * gd)
            cs2 = slice(half + gi * gd, half + (gi + 1) * gd)
            w = pw_ref[gi]
            pb = pooled[gi].astype(BF16)
            zp = jnp.dot(pb, w, preferred_element_type=F32)
            g2 = g_ref[:, cs2]
            dy2 = dy_ref[:, cs2]
            dg_ref[:, cs2] = (dy2 * zp * sc_ref[:, cs] * _dsilu(g2)).astype(BF16)
            dpo = dy2 * _silu(g2)
            _acc_rows(dsc_ref.at[:, cs], first, jnp.sum(dpo * zp, axis=0, keepdims=True))
            dz = (dpo * sc_ref[:, cs]).astype(BF16)
            _acc_rows(dpw_ref.at[gi], first, _tn(pb, dz))
            dzh = jnp.where(i < nt - 1, dyh_ref[:, cs] * _silu(gh_ref[:, cs]) * sc_ref[:, cs], 0.0).astype(BF16)
            dpool = _nt(dz, w)
            dpool_h = _nt(dzh, w)
            cnt = jnp.minimum(win, row + 1).astype(F32)
            dn_ref[0:t, cs] = dpool / cnt[0:t]
            dn_ref[t:, cs] = dpool_h / cnt[t:]
            acc = dn_ref[0:t, cs]
            for j in range(1, win):
                acc = acc + dn_ref[j:j + t, cs]
            du_ref[:, cs] = (acc - dpool).astype(BF16)
        host.after(i, nt)

    outs = pl.pallas_call(
        body, name=name, grid=(nt,), in_specs=host.in_specs, out_specs=host.out_specs, out_shape=host.out_shape,
        scratch_shapes=host.scratch, input_output_aliases=host.aliases,
        compiler_params=_cp("arbitrary"))(p, p, p, p, att, dy, dy, pool_w, pool_scale, *host.args)
    return host.results(outs)


def _mm_out_even(y, w, x, g_post, g_pre_next, name):
    s, k = y.shape
    d = w.shape[1]
    t = ROW_TILE

    def body(y_ref, w_ref, x_ref, gp_ref, gn_ref, o_ref, x1_ref, h1_ref):
        for r0 in range(0, t, t // 2):
            rows = slice(r0, r0 + t // 2)
            o = jnp.dot(y_ref[rows, :], w_ref[...], preferred_element_type=F32)
            o_ref[rows, :] = o
            ohat, _ = _rms_stats(o)
            x1 = x_ref[rows, :] + ohat * gp_ref[...]
            x1_ref[rows, :] = x1
            xhat, _ = _rms_stats(x1)
            h1_ref[rows, :] = (xhat * gn_ref[...]).astype(BF16)

    row = lambda c: pl.BlockSpec((t, c), lambda i: (i, 0))
    vec = pl.BlockSpec((1, d), lambda i: (0, 0))
    return pl.pallas_call(
        body, name=name, grid=(s // t,),
        in_specs=[row(k), pl.BlockSpec((k, d), lambda i: (0, 0)), row(d), vec, vec],
        out_specs=[row(d), row(d), row(d)],
        out_shape=[jax.ShapeDtypeStruct((s, d), F32), jax.ShapeDtypeStruct((s, d), F32),
                   jax.ShapeDtypeStruct((s, d), BF16)],
        compiler_params=_cp("parallel"))(y, w, x, g_post, g_pre_next)


def _mm_out_odd(y, w, x1, g_post, target, name):
    s, k = y.shape
    d = w.shape[1]
    t = ROW_TILE

    def body(y_ref, w_ref, x_ref, gp_ref, tg_ref, do_ref, dx_ref, loss_ref, dgp_ref):
        first = pl.program_id(0) == 0
        gp = gp_ref[...]
        part = dgp = None
        for r0 in range(0, t, t // 2):
            rows = slice(r0, r0 + t // 2)
            o = jnp.dot(y_ref[rows, :], w_ref[...], preferred_element_type=F32)
            ohat, r = _rms_stats(o)
            diff = x_ref[rows, :] + ohat * gp - tg_ref[rows, :]
            part_half = 0.5 * jnp.sum(jnp.mean(diff * diff, axis=-1, keepdims=True), axis=0, keepdims=True)
            dx2 = diff * (1.0 / d)
            dx_ref[rows, :] = dx2
            do, dgp_half = _rms_bwd(dx2, ohat, r, gp)
            do_ref[rows, :] = do.astype(BF16)
            part = part_half if part is None else part + part_half
            dgp = dgp_half if dgp is None else dgp + dgp_half
        _acc_rows(loss_ref, first, jnp.broadcast_to(part, loss_ref.shape))
        _acc_rows(dgp_ref, first, dgp)

    row = lambda c: pl.BlockSpec((t, c), lambda i: (i, 0))
    vec = pl.BlockSpec((1, d), lambda i: (0, 0))
    return pl.pallas_call(
        body, name=name, grid=(s // t,),
        in_specs=[row(k), pl.BlockSpec((k, d), lambda i: (0, 0)), row(d), vec, row(d)],
        out_specs=[row(d), row(d), pl.BlockSpec((8, LANES), lambda i: (0, 0)), vec],
        out_shape=[jax.ShapeDtypeStruct((s, d), BF16), jax.ShapeDtypeStruct((s, d), F32),
                   jax.ShapeDtypeStruct((8, LANES), F32), jax.ShapeDtypeStruct((1, d), F32)],
        compiler_params=_cp("arbitrary"))(y, w, x1, g_post, target)


def _layer_norm(d1, cg, cb):
    mu = jnp.mean(d1, axis=-1, keepdims=True)
    cen = d1 - mu
    rstd = lax.rsqrt(jnp.mean(cen * cen, axis=-1, keepdims=True) + EPS)
    n = cen * rstd
    return n, rstd, n * cg + cb


SUBLANES = 8
ROW_STRIP = 64
GATHER_PIECES = 8
CONV_ROWS = 64


def _make_shifts(pad_ref, cs, sh_ref):
    rows = sh_ref.shape[1]
    for r in range(1, SUBLANES):
        sh_ref[r - 1] = pad_ref[r:r + rows, cs]


def _by_shift(taps, base, sign=1):
    return sorted(range(taps), key=lambda k: ((sign * (base + k)) % SUBLANES, k))


def _window(pad_ref, cs, sh_ref, off, t):
    m, r = divmod(off, SUBLANES)
    if r == 0:
        return pad_ref[SUBLANES * m:SUBLANES * m + t, cs]
    return sh_ref[r - 1, SUBLANES * m:SUBLANES * m + t, :]


def _odd_mix_fwd(p, sconv_w, dconv_w, dconv_b, cnorm_g, cnorm_b, d, name):
    s = p.shape[0]
    w = d // 2
    k3, k31 = sconv_w.shape[0], dconv_w.shape[0]
    t, hb = ROW_TILE, CONV_HALO
    assert hb >= k31 - 1 and w % LANES == 0

    def body(p_ref, ph_ref, w3_ref, w31_ref, b31_ref, cg_ref, cb_ref, y_ref, s3_ref, d1_ref, mpad, dpad, sh_ref):
        i = pl.program_id(0)
        mpad[0:hb, :] = jnp.where(i > 0, ph_ref[:, 2 * w:3 * w] * ph_ref[:, 0:w], 0.0)
        mpad[hb:, :] = p_ref[:, 2 * w:3 * w] * p_ref[:, 0:w]
        dpad[0:hb, :] = jnp.where(i > 0, ph_ref[:, 3 * w:4 * w] * _sigmoid(ph_ref[:, 4 * w:5 * w]), 0.0)
        dpad[hb:, :] = p_ref[:, 3 * w:4 * w] * _sigmoid(p_ref[:, 4 * w:5 * w])
        for c0 in range(0, w, LANES):
            cs = slice(c0, c0 + LANES)
            acc = jnp.zeros((t, LANES), F32)
            for kk in range(k3):
                acc = acc + w3_ref[kk:kk + 1, cs] * mpad[hb - (k3 - 1) + kk:hb - (k3 - 1) + kk + t, cs]
            s3_ref[:, cs] = acc
            _make_shifts(dpad, cs, sh_ref)
            for r0 in range(0, t, CONV_ROWS):
                acc = jnp.zeros((CONV_ROWS, LANES), F32)
                for kk in _by_shift(k31, hb - (k31 - 1)):
                    acc = acc + w31_ref[kk:kk + 1, cs] * _window(dpad, cs, sh_ref, hb - (k31 - 1) + kk + r0, CONV_ROWS)
                d1_ref[r0:r0 + CONV_ROWS, cs] = acc + b31_ref[:, cs]
        _, _, d2 = _layer_norm(d1_ref[...], cg_ref[...], cb_ref[...])
        y_ref[:, :w] = (p_ref[:, w:2 * w] * s3_ref[...] * _silu(p_ref[:, 5 * w:6 * w])).astype(BF16)
        y_ref[:, w:] = (_silu(d2) * _silu(p_ref[:, 6 * w:7 * w])).astype(BF16)

    row = lambda c: pl.BlockSpec((t, c), lambda i: (i, 0))
    full = lambda a: pl.BlockSpec(a.shape, lambda i: (0, 0))
    return pl.pallas_call(
        body, name=name, grid=(s // t,),
        in_specs=[row(7 * w),
                  pl.BlockSpec((hb, 5 * w), lambda i: (jnp.maximum(i * (t // hb) - 1, 0), 0)),
                  full(sconv_w), full(dconv_w), full(dconv_b), full(cnorm_g), full(cnorm_b)],
        out_specs=[row(d), row(w), row(w)],
        out_shape=[jax.ShapeDtypeStruct((s, d), BF16), jax.ShapeDtypeStruct((s, w), F32),
                   jax.ShapeDtypeStruct((s, w), F32)],
        scratch_shapes=[pltpu.VMEM((hb + t, w), F32)] * 2 + [pltpu.VMEM((SUBLANES - 1, hb + t - SUBLANES, LANES), F32)],
        compiler_params=_cp("parallel"))(p, p, sconv_w, dconv_w, dconv_b, cnorm_g, cnorm_b)


def _odd_bwd_rows(p, s3, d1, dy, cnorm_g, cnorm_b, d, name, comm=None):
    s = p.shape[0]
    w = d // 2
    t = ROW_TILE
    col = lambda j: pl.BlockSpec((t, w), lambda i: (i, j))
    row = lambda c: pl.BlockSpec((t, c), lambda i: (i, 0))
    vec = pl.BlockSpec((1, w), lambda i: (0, 0))
    host = _Host(comm, [col(1), col(5), col(6), row(w), row(w), row(d), vec, vec],
                 [row(w), row(d), row(w), row(w), vec, vec, vec],
                 [jax.ShapeDtypeStruct((s, w), BF16), jax.ShapeDtypeStruct((s, d), BF16),
                  jax.ShapeDtypeStruct((s, w), F32), jax.ShapeDtypeStruct((s, w), F32)] + [jax.ShapeDtypeStruct((1, w), F32)] * 3, [])

    def body(*refs):
        ((bc_ref, g1_ref, g2_ref, s3_ref, d1_ref, dy_ref, cg_ref, cb_ref),
         (dbc_ref, dg_ref, ds3_ref, dd1_ref, dcg_ref, dcb_ref, db_ref), _) = host.split(refs)
        step = pl.program_id(0)
        host.before(step, s // t)
        first = step == 0

        def strip(j, sums):
            rows = slice(j * ROW_STRIP, (j + 1) * ROW_STRIP)
            g1, g2 = g1_ref[rows, :], g2_ref[rows, :]
            bc, s3v = bc_ref[rows, :], s3_ref[rows, :]
            dy1, dy2 = dy_ref[rows, :w], dy_ref[rows, w:]
            n, rstd, d2 = _layer_norm(d1_ref[rows, :], cg_ref[...], cb_ref[...])
            dg_ref[rows, :w] = (dy1 * bc * s3v * _dsilu(g1)).astype(BF16)
            dg_ref[rows, w:] = (dy2 * _silu(d2) * _dsilu(g2)).astype(BF16)
            dco = dy1 * _silu(g1)
            dbc_ref[rows, :] = (dco * s3v).astype(BF16)
            ds3_ref[rows, :] = dco * bc
            dd2 = dy2 * _silu(g2) * _dsilu(d2)
            dn = dd2 * cg_ref[...]
            dd1 = rstd * (dn - jnp.mean(dn, axis=-1, keepdims=True) - n * jnp.mean(dn * n, axis=-1, keepdims=True))
            dd1_ref[rows, :] = dd1
            dcb, dcg, db = sums
            return (dcb + jnp.sum(dd2, axis=0, keepdims=True), dcg + jnp.sum(dd2 * n, axis=0, keepdims=True),
                    db + jnp.sum(dd1, axis=0, keepdims=True))

        zero = jnp.zeros((1, w), F32)
        sums = (zero, zero, zero)
        for j in range(t // ROW_STRIP):
            sums = strip(j, sums)
        dcb, dcg, db = sums
        _acc_rows(dcb_ref, first, dcb)
        _acc_rows(dcg_ref, first, dcg)
        _acc_rows(db_ref, first, db)
        host.after(step, s // t)

    outs = pl.pallas_call(
        body, name=name, grid=(s // t,), in_specs=host.in_specs, out_specs=host.out_specs, out_shape=host.out_shape,
        scratch_shapes=host.scratch, input_output_aliases=host.aliases,
        compiler_params=_cp("arbitrary"))(p, p, p, s3, d1, dy, cnorm_g, cnorm_b, *host.args)
    return host.results(outs)


def _odd_bwd_conv(p, ds3, dd1, sconv_w, dconv_w, d, name):
    s = p.shape[0]
    w = d // 2
    k3, k31 = sconv_w.shape[0], dconv_w.shape[0]
    t, hb, ha = ROW_TILE, CONV_HALO, 8
    nt = s // t
    assert hb >= k31 - 1 and ha >= k3 - 1

    def body(hc_ref, cc_ref, ga_ref, gb_ref, hch_ref, cch_ref, gah_ref, gbh_ref, ds3_ref, ds3h_ref, dd1_ref, dd1h_ref,
             w3_ref, w31_ref, dhc_ref, dcc_ref, dga_ref, dgb_ref, dw3_ref, dw31_ref, mpad, dpad, s3pad, d1pad, sh_ref):
        i = pl.program_id(0)
        first = i == 0
        last = i == nt - 1
        mpad[0:hb, :] = jnp.where(i > 0, cch_ref[...] * hch_ref[...], 0.0)
        mpad[hb:, :] = cc_ref[...] * hc_ref[...]
        dpad[0:hb, :] = jnp.where(i > 0, gah_ref[...] * _sigmoid(gbh_ref[...]), 0.0)
        dpad[hb:, :] = ga_ref[...] * _sigmoid(gb_ref[...])
        s3pad[0:t, :] = ds3_ref[...]
        s3pad[t:, :] = jnp.where(last, 0.0, ds3h_ref[...])
        d1pad[0:t, :] = dd1_ref[...]
        d1pad[t:, :] = jnp.where(last, 0.0, dd1h_ref[...])

        @pl.when(first)
        def _():
            dw3_ref[...] = jnp.zeros_like(dw3_ref)
            dw31_ref[...] = jnp.zeros_like(dw31_ref)

        def fold(v):
            return jnp.sum(v.reshape(v.shape[0] // SUBLANES, SUBLANES, LANES), axis=0)

        groups = range(0, t, CONV_ROWS)
        for c0 in range(0, w, LANES):
            cs = slice(c0, c0 + LANES)
            ds3v = s3pad[0:t, cs]
            dm = jnp.zeros((t, LANES), F32)
            for kk in range(k3):
                dm = dm + w3_ref[kk:kk + 1, cs] * s3pad[k3 - 1 - kk:k3 - 1 - kk + t, cs]
                off = hb - (k3 - 1) + kk
                dw3_ref[SUBLANES * kk:SUBLANES * (kk + 1), cs] += fold(ds3v * mpad[off:off + t, cs])
            dcc_ref[:, cs] = (dm * hc_ref[:, cs]).astype(BF16)
            dhc_ref[:, cs] = (dm * cc_ref[:, cs]).astype(BF16)
            _make_shifts(d1pad, cs, sh_ref)
            for r0 in groups:
                rows = slice(r0, r0 + CONV_ROWS)
                dd0 = jnp.zeros((CONV_ROWS, LANES), F32)
                for kk in _by_shift(k31, -(k31 - 1), -1):
                    dd0 = dd0 + w31_ref[kk:kk + 1, cs] * _window(d1pad, cs, sh_ref, k31 - 1 - kk + r0, CONV_ROWS)
                sgb = _sigmoid(gb_ref[rows, cs])
                dga_ref[rows, cs] = (dd0 * sgb).astype(BF16)
                dgb_ref[rows, cs] = (dd0 * ga_ref[rows, cs] * sgb * (1.0 - sgb)).astype(BF16)
            _make_shifts(dpad, cs, sh_ref)
            for kk in _by_shift(k31, hb - (k31 - 1)):
                part = jnp.zeros((SUBLANES, LANES), F32)
                for r0 in groups:
                    part = part + fold(d1pad[r0:r0 + CONV_ROWS, cs]
                                       * _window(dpad, cs, sh_ref, hb - (k31 - 1) + kk + r0, CONV_ROWS))
                dw31_ref[SUBLANES * kk:SUBLANES * (kk + 1), cs] += part

    col = lambda j: pl.BlockSpec((t, w), lambda i: (i, j))
    pre = lambda j: pl.BlockSpec((hb, w), lambda i: (jnp.maximum(i * (t // hb) - 1, 0), j))
    row = pl.BlockSpec((t, w), lambda i: (i, 0))
    post = lambda h: pl.BlockSpec((h, w), lambda i: (jnp.minimum((i + 1) * (t // h), s // h - 1), 0))
    full = lambda a: pl.BlockSpec(a.shape, lambda i: (0, 0))
    dhc, dcc, dga, dgb, dw3, dw31 = pl.pallas_call(
        body, name=name, grid=(nt,),
        in_specs=[col(0), col(2), col(3), col(4), pre(0), pre(2), pre(3), pre(4),
                  row, post(ha), row, post(hb), full(sconv_w), full(dconv_w)],
        out_specs=[row, row, row, row, pl.BlockSpec((SUBLANES * k3, w), lambda i: (0, 0)),
                   pl.BlockSpec((SUBLANES * k31, w), lambda i: (0, 0))],
        out_shape=[jax.ShapeDtypeStruct((s, w), BF16)] * 4
        + [jax.ShapeDtypeStruct((SUBLANES * k3, w), F32), jax.ShapeDtypeStruct((SUBLANES * k31, w), F32)],
        scratch_shapes=[pltpu.VMEM((hb + t, w), F32)] * 2 + [pltpu.VMEM((t + ha, w), F32), pltpu.VMEM((t + hb, w), F32),
                                                             pltpu.VMEM((SUBLANES - 1, hb + t - SUBLANES, LANES), F32)],
        compiler_params=_cp("arbitrary"))(p, p, p, p, p, p, p, p, ds3, ds3, dd1, dd1, sconv_w, dconv_w)
    return dhc, dcc, dga, dgb, jnp.sum(dw3.reshape(k3, SUBLANES, w), axis=1), jnp.sum(dw31.reshape(k31, SUBLANES, w), axis=1)


def _mm_in_bwd(dp, w3, x, g_pre, dres, post, name, comm=None):
    s = dp.shape[0]
    nsh, d, ns = w3.shape
    t = 512 if s % 512 == 0 else ROW_TILE
    nt = s // t
    ks = 2 if (ns // 2) % LANES == 0 else 1
    nk, kw = nsh * ks, ns // ks
    chunk = 128
    nchunk = t // chunk
    row = pl.BlockSpec((t, d), lambda i, k: (i, 0))
    vec = pl.BlockSpec((1, d), lambda i, k: (0, 0))
    rowwise = [x, dres] + ([post[0]] if post is not None else [])
    in_specs = [pl.BlockSpec((t, kw), lambda i, k: (i, k)), pl.BlockSpec((None, d, kw), lambda i, k: (k // ks, 0, k % ks)), vec]
    out_specs = [row, vec]
    out_shape = [jax.ShapeDtypeStruct((s, d), F32), jax.ShapeDtypeStruct((1, d), F32)]
    args = [dp, w3, g_pre]
    if post is not None:
        in_specs += [vec]
        out_specs += [row, vec]
        out_shape += [jax.ShapeDtypeStruct((s, d), BF16), jax.ShapeDtypeStruct((1, d), F32)]
        args += [post[1]]
    n_blocked = len(in_specs)
    in_specs += [ANY] * len(rowwise)
    args += rowwise
    host = _Host(comm, in_specs, out_specs, out_shape,
                 [pltpu.VMEM((t, d), F32), pltpu.VMEM((len(rowwise), 2, chunk, d), F32), pltpu.SemaphoreType.DMA((len(rowwise), 2))])

    def body(*refs):
        ins, outs, (acc_ref, buf_ref, sem_ref) = host.split(refs)
        dp_ref, w_ref, g_ref = ins[:3]
        hbm = ins[n_blocked:]
        dx_ref, dg_ref = outs[:2]
        tile = pl.program_id(0)
        kk = pl.program_id(1)
        first = tile == 0
        step = tile * nk + kk
        host.before(step, nt * nk)
        part = _nt(dp_ref[...], w_ref[...])

        @pl.when(kk == 0)
        def _():
            acc_ref[...] = part

        @pl.when(kk > 0)
        def _():
            acc_ref[...] += part

        def fetch(ci, slot):
            return [pltpu.make_async_copy(src.at[pl.ds(tile * t + ci * chunk, chunk)], buf_ref.at[n, slot], sem_ref.at[n, slot])
                    for n, src in enumerate(hbm)]

        @pl.when(kk == nk - 1)
        def _():
            dg = dgp = None
            for cp in fetch(0, 0):
                cp.start()
            for ci in range(nchunk):
                slot = ci % 2
                if ci + 1 < nchunk:
                    for cp in fetch(ci + 1, 1 - slot):
                        cp.start()
                for cp in fetch(ci, slot):
                    cp.wait()
                rows = slice(ci * chunk, (ci + 1) * chunk)
                xhat, r = _rms_stats(buf_ref[0, slot])
                dxn, dg_part = _rms_bwd(acc_ref[rows, :], xhat, r, g_ref[...])
                dx = buf_ref[1, slot] + dxn
                dx_ref[rows, :] = dx
                dg = dg_part if dg is None else dg + dg_part
                if post is not None:
                    ohat, ro = _rms_stats(buf_ref[2, slot])
                    do, dgp_part = _rms_bwd(dx, ohat, ro, ins[3][...])
                    outs[2][rows, :] = do.astype(BF16)
                    dgp = dgp_part if dgp is None else dgp + dgp_part
            _acc_rows(dg_ref, first, dg)
            if post is not None:
                _acc_rows(outs[3], first, dgp)

        host.after(step, nt * nk)

    res = pl.pallas_call(
        body, name=name, grid=(nt, nk), in_specs=host.in_specs, out_specs=host.out_specs, out_shape=host.out_shape,
        scratch_shapes=host.scratch, input_output_aliases=host.aliases,
        compiler_params=_cp("arbitrary", "arbitrary"))(*args, *host.args)
    return host.results(res)


def _half_add(g, r1, c_arr, name):
    nsh, rows, ns = g.shape
    h = rows // 2
    tr = min(ROW_TILE, h)
    per = h // tr

    def body(c_ref, g_ref, r_ref, o_ref):
        o_ref[...] = (g_ref[...].astype(F32) + r_ref[...].astype(F32)).astype(BF16)

    spec = pl.BlockSpec((None, tr, ns), lambda s, r, c: (s, r, 0))
    return pl.pallas_call(
        body, name=name,
        grid_spec=pltpu.PrefetchScalarGridSpec(
            num_scalar_prefetch=1, grid=(nsh, per),
            in_specs=[pl.BlockSpec((None, tr, ns), lambda s, r, c: (s, c[0] * per + r, 0)), spec], out_specs=spec),
        out_shape=jax.ShapeDtypeStruct((nsh, h, ns), BF16), compiler_params=_cp("parallel", "parallel"))(c_arr, g, r1)


def _sum_chips(hh, r2, mc_arr, name):
    _, h, ns = hh.shape
    tr = min(ROW_TILE, h)
    per = h // tr

    def body(mc_ref, h_ref, a_ref, b_ref, c_ref, o_ref):
        o_ref[...] = ((h_ref[...].astype(F32) + a_ref[...].astype(F32)) + b_ref[...].astype(F32)) + c_ref[...].astype(F32)

    got = lambda k: pl.BlockSpec((None, tr, ns), lambda r, mc: (k, r, 0))
    return pl.pallas_call(
        body, name=name,
        grid_spec=pltpu.PrefetchScalarGridSpec(
            num_scalar_prefetch=1, grid=(per,),
            in_specs=[pl.BlockSpec((None, tr, ns), lambda r, mc: (mc[0], r, 0)), got(0), got(1), got(2)],
            out_specs=pl.BlockSpec((tr, ns), lambda r, mc: (mc[1] * per + r, 0))),
        out_shape=jax.ShapeDtypeStruct((2 * h, ns), F32), compiler_params=_cp("parallel"))(mc_arr, hh, r2, r2, r2)


def _add2(a, b, name):
    def body(a_ref, b_ref, o_ref):
        o_ref[...] = a_ref[...] + b_ref[...]

    return pl.pallas_call(body, name=name, out_shape=jax.ShapeDtypeStruct(a.shape, a.dtype), compiler_params=_cp())(a, b)


def _sum_chips_ordered(s2, r2, mc_arr, name):
    rows, w = s2.shape
    rh = rows // 2

    def body(mc_ref, s_ref, a_ref, b_ref, c_ref, o_ref):
        me = mc_ref[0]
        acc = None
        for j in range(N_CHIPS):
            rel = jnp.bitwise_xor(me, j)
            v = jnp.where(rel == 0, s_ref[...], jnp.where(rel == 2, a_ref[...], jnp.where(rel == 1, b_ref[...], c_ref[...])))
            acc = v if acc is None else acc + v
        o_ref[...] = acc

    got = lambda k: pl.BlockSpec((None, rh, w), lambda i, mc: (k, 0, 0))
    return pl.pallas_call(
        body, name=name,
        grid_spec=pltpu.PrefetchScalarGridSpec(
            num_scalar_prefetch=1, grid=(1,),
            in_specs=[pl.BlockSpec((rh, w), lambda i, mc: (mc[1], 0)), got(0), got(1), got(2)],
            out_specs=pl.BlockSpec((rh, w), lambda i, mc: (mc[1], 0))),
        out_shape=jax.ShapeDtypeStruct((rows, w), F32), compiler_params=_cp("arbitrary"))(mc_arr, s2, r2, r2, r2)


def _adamw(w, g, m, v, name, comm=None):
    r, c = w.shape
    tr = ROW_TILE if r % ROW_TILE == 0 else r
    c1 = 1.0 / (1.0 - ADAM_B1 ** ADAM_STEP)
    c2 = 1.0 / (1.0 - ADAM_B2 ** ADAM_STEP)
    spec = pl.BlockSpec((tr, c), lambda i: (i, 0))
    host = _Host(comm, [spec] * 4, [spec] * 4, [jax.ShapeDtypeStruct((r, c), F32)] * 4, [])

    def body(*refs):
        (w_ref, g_ref, m_ref, v_ref), (go_ref, d_ref, nm_ref, nv_ref), _ = host.split(refs)
        step = pl.program_id(0)
        host.before(step, r // tr)
        gv = g_ref[...]
        go_ref[...] = gv
        nm = ADAM_B1 * m_ref[...] + (1.0 - ADAM_B1) * gv
        nv = ADAM_B2 * v_ref[...] + (1.0 - ADAM_B2) * (gv * gv)
        nm_ref[...] = nm
        nv_ref[...] = nv
        d_ref[...] = -ADAM_LR * ((nm * c1) / (jnp.sqrt(nv * c2) + ADAM_EPS) + ADAM_WD * w_ref[...])
        host.after(step, r // tr)

    outs = pl.pallas_call(
        body, name=name, grid=(r // tr,), in_specs=host.in_specs, out_specs=host.out_specs, out_shape=host.out_shape,
        scratch_shapes=host.scratch, input_output_aliases=host.aliases,
        compiler_params=_cp("arbitrary"))(w, g, m, v, *host.args)
    return host.results(outs)


def _gather_weights(bigs, pool_w, pack_w, pack_d, name):
    nb = len(bigs)
    smalls = [pool_w, pack_w, pack_d]
    q, cw, cd = pool_w.shape[1], pack_w.shape[1], pack_d.shape[1]
    pieces = [_GatherPlan(bigs, (j, j + 1, GATHER_PIECES)) for j in range(GATHER_PIECES)]
    for j, piece in enumerate(pieces):
        piece.base = 9 + j * piece.nsems

    def body(*refs):
        srcs, dsts = refs[:nb + 3], refs[nb + 3:2 * (nb + 3)]
        ssem, rsem, lsem = refs[2 * (nb + 3):]
        x, y, c, me, chips, sib = _place()

        def small_dst(n, chip):
            if n == 0:
                return dsts[nb].at[:, pl.ds(chip * q, q), :]
            return dsts[nb + n].at[:, pl.ds(chip * (cw if n == 1 else cd), cw if n == 1 else cd)]

        local = [pltpu.make_async_copy(srcs[nb + n], small_dst(n, me), lsem.at[n]) for n in range(3)]
        for cp in local:
            cp.start()
        sends = []
        for n in range(3):
            for k, chip in enumerate(chips):
                cp = _rcopy(srcs[nb + n], small_dst(n, me), ssem.at[3 * n + k], rsem.at[3 * n + k], (*chip, c))
                cp.start()
                sends.append(cp)
        big = (srcs[:nb], dsts[:nb], ssem, rsem)
        for stage in ("start", "relay", "relay_far", "finish"):
            for piece in pieces:
                getattr(piece, stage)(*big)
        for n in range(3):
            for k, chip in enumerate(chips):
                ref = small_dst(n, 2 * chip[0] + chip[1])
                _rcopy(ref, ref, ssem.at[3 * n + k], rsem.at[3 * n + k], (*chip, c)).wait_recv()
        for cp in sends:
            cp.wait_send()
        for cp in local:
            cp.wait()

    nsem = 9 + sum(piece.nsems for piece in pieces)
    out_shape = [jax.ShapeDtypeStruct(b.shape, b.dtype) for b in bigs]
    out_shape += [jax.ShapeDtypeStruct((pool_w.shape[0], N_CHIPS * q, pool_w.shape[2]), pool_w.dtype),
                  jax.ShapeDtypeStruct((pack_w.shape[0], N_CHIPS * cw), pack_w.dtype),
                  jax.ShapeDtypeStruct((pack_d.shape[0], N_CHIPS * cd), pack_d.dtype)]
    return pl.pallas_call(
        body, name=name, in_specs=[ANY] * (nb + 3), out_specs=[ANY] * (nb + 3), out_shape=out_shape,
        input_output_aliases={a: a for a in range(nb)},
        scratch_shapes=[pltpu.SemaphoreType.DMA((nsem,)), pltpu.SemaphoreType.DMA((nsem,)), pltpu.SemaphoreType.DMA((3,))],
        compiler_params=pltpu.CompilerParams(has_side_effects=True))(*bigs, *smalls)


def _swap_with_sibling(grads, wholes, name):
    n, nw = len(grads), len(wholes)
    halves = [g.shape[1] // 2 for g in grads]

    def body(*refs):
        srcs, dsts = refs[:n + nw], refs[n + nw:2 * (n + nw)]
        ssem, rsem = refs[2 * (n + nw):]
        x, y, c, me, chips, sib = _place()
        cps = [_rcopy(srcs[a].at[:, pl.ds((1 - c) * halves[a], halves[a]), :], dsts[a], ssem.at[a], rsem.at[a], sib)
               for a in range(n)]
        cps += [_rcopy(srcs[a], dsts[a], ssem.at[a], rsem.at[a], sib) for a in range(n, n + nw)]
        for cp in cps:
            cp.start()
        for cp in cps:
            cp.wait_recv()
        for cp in cps:
            cp.wait_send()

    out_shape = [jax.ShapeDtypeStruct((g.shape[0], h, g.shape[2]), g.dtype) for g, h in zip(grads, halves)]
    out_shape += [jax.ShapeDtypeStruct(w.shape, w.dtype) for w in wholes]
    return pl.pallas_call(
        body, name=name, in_specs=[ANY] * (n + nw), out_specs=[ANY] * (n + nw), out_shape=out_shape,
        scratch_shapes=[pltpu.SemaphoreType.DMA((n + nw,)), pltpu.SemaphoreType.DMA((n + nw,))],
        compiler_params=pltpu.CompilerParams(has_side_effects=True))(*grads, *wholes)


def _scatter_to_chips(halves_in, small, name):
    n = len(halves_in)
    rh = small.shape[0] // 2

    def body(*refs):
        srcs, dsts = refs[:n + 1], refs[n + 1:2 * (n + 1)]
        ssem, rsem = refs[2 * (n + 1):]
        x, y, c, me, chips, sib = _place()
        cps = []
        for a in range(n + 1):
            for k, chip in enumerate(chips):
                src = srcs[a].at[2 * chip[0] + chip[1]] if a < n else srcs[a].at[pl.ds(c * rh, rh)]
                cps.append(_rcopy(src, dsts[a].at[k], ssem.at[3 * a + k], rsem.at[3 * a + k], (*chip, c)))
        for cp in cps:
            cp.start()
        for cp in cps:
            cp.wait_recv()
        for cp in cps:
            cp.wait_send()

    out_shape = [jax.ShapeDtypeStruct((3,) + h.shape[1:], h.dtype) for h in halves_in]
    out_shape.append(jax.ShapeDtypeStruct((3, rh, small.shape[1]), small.dtype))
    return pl.pallas_call(
        body, name=name, in_specs=[ANY] * (n + 1), out_specs=[ANY] * (n + 1), out_shape=out_shape,
        scratch_shapes=[pltpu.SemaphoreType.DMA((3 * (n + 1),)), pltpu.SemaphoreType.DMA((3 * (n + 1),))],
        compiler_params=pltpu.CompilerParams(has_side_effects=True))(*halves_in, small)


def _join_halves(parts, name):
    n = len(parts)

    def body(*refs):
        srcs, dsts = refs[:n], refs[n:2 * n]
        ssem, rsem = refs[2 * n:]
        x, y, c, me, chips, sib = _place()
        cps = []
        for a in range(n):
            h = srcs[a].shape[0] // 2
            cps.append(_rcopy(srcs[a].at[pl.ds(c * h, h)], dsts[a].at[pl.ds(c * h, h)], ssem.at[a], rsem.at[a], sib))
        for cp in cps:
            cp.start()
        for a in range(n):
            h = srcs[a].shape[0] // 2
            theirs = dsts[a].at[pl.ds((1 - c) * h, h)]
            _rcopy(theirs, theirs, ssem.at[a], rsem.at[a], sib).wait_recv()
        for cp in cps:
            cp.wait_send()

    out_shape = [jax.ShapeDtypeStruct(p.shape, p.dtype) for p in parts]
    return pl.pallas_call(
        body, name=name, in_specs=[ANY] * n, out_specs=[ANY] * n, out_shape=out_shape,
        input_output_aliases={a: a for a in range(n)},
        scratch_shapes=[pltpu.SemaphoreType.DMA((n,)), pltpu.SemaphoreType.DMA((n,))],
        compiler_params=pltpu.CompilerParams(has_side_effects=True))(*parts)


def _scatter_start(h, name):
    land = (3,) + h.shape[1:]

    def body(h_ref, land_ref, send_sems, recv_sems, h_thru, land_thru, token):
        x, y, c, me, chips, sib = _place()
        for k, chip in enumerate(chips):
            _rcopy(h_ref.at[2 * chip[0] + chip[1]], land_ref.at[k], send_sems.at[k], recv_sems.at[k], (*chip, c)).start()
        token[...] = jnp.zeros_like(token)

    hbm = pl.BlockSpec(memory_space=pltpu.HBM)
    sem = pl.BlockSpec(memory_space=pltpu.SEMAPHORE)
    return pl.pallas_call(
        body, name=name,
        out_shape=(pltpu.SemaphoreType.DMA((3,)), pltpu.SemaphoreType.DMA((3,)), pltpu.HBM(h.shape, h.dtype),
                   pltpu.HBM(land, h.dtype), jax.ShapeDtypeStruct((8, LANES), F32)),
        in_specs=(hbm, hbm), out_specs=(sem, sem, hbm, hbm, pl.BlockSpec(memory_space=pltpu.VMEM)),
        input_output_aliases={0: 2, 1: 3},
        compiler_params=pltpu.CompilerParams(has_side_effects=pltpu.SideEffectType.DATAFLOW_SIDE_EFFECTING))(
            pltpu.with_memory_space_constraint(h, pltpu.HBM),
            pltpu.with_memory_space_constraint(lax.empty(land, h.dtype), pltpu.HBM))


def _scatter_wait(send_sems, recv_sems, h_thru, land_thru, after, name):
    def body(h_ref, land_ref, send_sems, recv_sems, after_ref, h_dead, got_ref):
        x, y, c, me, chips, sib = _place()
        for k, chip in enumerate(chips):
            cp = _rcopy(h_ref.at[2 * chip[0] + chip[1]], land_ref.at[k], send_sems.at[k], recv_sems.at[k], (*chip, c))
            cp.wait_send()
            cp.wait_recv()

    hbm = pl.BlockSpec(memory_space=pltpu.HBM)
    sem = pl.BlockSpec(memory_space=pltpu.SEMAPHORE)
    return pl.pallas_call(
        body, name=name,
        out_shape=(pltpu.HBM(h_thru.shape, h_thru.dtype), pltpu.HBM(land_thru.shape, land_thru.dtype)),
        in_specs=(hbm, hbm, sem, sem, ANY), out_specs=(hbm, hbm), input_output_aliases={0: 0, 1: 1},
        compiler_params=pltpu.CompilerParams(has_side_effects=pltpu.SideEffectType.DATAFLOW_SIDE_EFFECTING))(
            h_thru, land_thru, send_sems, recv_sems, after)


def _swap_start(g, name):
    h = g.shape[1] // 2
    land = (g.shape[0], h, g.shape[2])

    def body(g_ref, land_ref, send_sem, recv_sem, g_thru, land_thru, token):
        x, y, c, me, chips, sib = _place()
        _rcopy(g_ref.at[:, pl.ds((1 - c) * h, h), :], land_ref, send_sem.at[0], recv_sem.at[0], sib).start()
        token[...] = jnp.zeros_like(token)

    hbm = pl.BlockSpec(memory_space=pltpu.HBM)
    sem = pl.BlockSpec(memory_space=pltpu.SEMAPHORE)
    return pl.pallas_call(
        body, name=name,
        out_shape=(pltpu.SemaphoreType.DMA((1,)), pltpu.SemaphoreType.DMA((1,)), pltpu.HBM(g.shape, g.dtype),
                   pltpu.HBM(land, g.dtype), jax.ShapeDtypeStruct((8, LANES), F32)),
        in_specs=(hbm, hbm), out_specs=(sem, sem, hbm, hbm, pl.BlockSpec(memory_space=pltpu.VMEM)),
        input_output_aliases={0: 2, 1: 3},
        compiler_params=pltpu.CompilerParams(has_side_effects=pltpu.SideEffectType.DATAFLOW_SIDE_EFFECTING))(
            pltpu.with_memory_space_constraint(g, pltpu.HBM),
            pltpu.with_memory_space_constraint(lax.empty(land, g.dtype), pltpu.HBM))


def _swap_wait(send_sem, recv_sem, g_thru, land_thru, after, name):
    h = g_thru.shape[1] // 2

    def body(g_ref, land_ref, send_sem, recv_sem, after_ref, g_dead, got_ref):
        x, y, c, me, chips, sib = _place()
        cp = _rcopy(g_ref.at[:, pl.ds((1 - c) * h, h), :], land_ref, send_sem.at[0], recv_sem.at[0], sib)
        cp.wait_send()
        cp.wait_recv()

    hbm = pl.BlockSpec(memory_space=pltpu.HBM)
    sem = pl.BlockSpec(memory_space=pltpu.SEMAPHORE)
    return pl.pallas_call(
        body, name=name,
        out_shape=(pltpu.HBM(g_thru.shape, g_thru.dtype), pltpu.HBM(land_thru.shape, land_thru.dtype)),
        in_specs=(hbm, hbm, sem, sem, ANY), out_specs=(hbm, hbm), input_output_aliases={0: 0, 1: 1},
        compiler_params=pltpu.CompilerParams(has_side_effects=pltpu.SideEffectType.DATAFLOW_SIDE_EFFECTING))(
            g_thru, land_thru, send_sem, recv_sem, after)


def _join_start(parts, name):
    n = len(parts)

    def body(*refs):
        srcs, (send_sems, recv_sems), token = refs[:n], refs[n:n + 2], refs[-1]
        x, y, c, me, chips, sib = _place()
        for a, src in enumerate(srcs):
            h = src.shape[0] // 2
            mine = src.at[pl.ds(c * h, h)]
            _rcopy(mine, mine, send_sems.at[a], recv_sems.at[a], sib).start()
        token[...] = jnp.zeros_like(token)

    hbm = pl.BlockSpec(memory_space=pltpu.HBM)
    sem = pl.BlockSpec(memory_space=pltpu.SEMAPHORE)
    outs = pl.pallas_call(
        body, name=name,
        out_shape=(pltpu.SemaphoreType.DMA((n,)), pltpu.SemaphoreType.DMA((n,)))
        + tuple(pltpu.HBM(p.shape, p.dtype) for p in parts) + (jax.ShapeDtypeStruct((8, LANES), F32),),
        in_specs=(hbm,) * n, out_specs=(sem, sem) + (hbm,) * n + (pl.BlockSpec(memory_space=pltpu.VMEM),),
        input_output_aliases={a: 2 + a for a in range(n)},
        compiler_params=pltpu.CompilerParams(has_side_effects=pltpu.SideEffectType.DATAFLOW_SIDE_EFFECTING))(
            *[pltpu.with_memory_space_constraint(p, pltpu.HBM) for p in parts])
    return outs[0], outs[1], list(outs[2:2 + n]), outs[-1]


def _join_wait(send_sems, recv_sems, parts, after, name):
    n = len(parts)

    def body(*refs):
        srcs, (send_sems, recv_sems) = refs[:n], refs[n:n + 2]
        x, y, c, me, chips, sib = _place()
        for a, src in enumerate(srcs):
            h = src.shape[0] // 2
            mine, theirs = src.at[pl.ds(c * h, h)], src.at[pl.ds((1 - c) * h, h)]
            _rcopy(mine, theirs, send_sems.at[a], recv_sems.at[a], sib).wait_send()
            _rcopy(theirs, theirs, send_sems.at[a], recv_sems.at[a], sib).wait_recv()

    hbm = pl.BlockSpec(memory_space=pltpu.HBM)
    sem = pl.BlockSpec(memory_space=pltpu.SEMAPHORE)
    return pl.pallas_call(
        body, name=name, out_shape=tuple(pltpu.HBM(p.shape, p.dtype) for p in parts),
        in_specs=(hbm,) * n + (sem, sem, ANY), out_specs=(hbm,) * n, input_output_aliases={a: a for a in range(n)},
        compiler_params=pltpu.CompilerParams(has_side_effects=pltpu.SideEffectType.DATAFLOW_SIDE_EFFECTING))(
            *parts, send_sems, recv_sems, after)


def _pad_rows(a, rows):
    return jnp.pad(a, ((0, rows - a.shape[0]), (0, 0)))


def _stack_rows(parts, multiple):
    padded = [_pad_rows(p, -(-p.shape[0] // 8) * 8) for p in parts]
    starts, at = [], 0
    for p in padded:
        starts.append(at)
        at += p.shape[0]
    total = -(-at // multiple) * multiple
    if total > at:
        padded.append(jnp.zeros((total - at, parts[0].shape[1]), parts[0].dtype))
    return jnp.concatenate(padded, axis=0), starts


def kernel(x, ln_pre_even, w_in_even, pool_w, pool_scale, w_out_even, ln_post_even, ln_pre_odd, w_in_odd, sconv_w, dconv_w, dconv_b, cnorm_g, cnorm_b, w_out_odd, ln_post_odd, loss_target, m_ln_pre_even, m_w_in_even, m_pool_w, m_pool_scale, m_w_out_even, m_ln_post_even, m_ln_pre_odd, m_w_in_odd, m_sconv_w, m_dconv_w, m_dconv_b, m_cnorm_g, m_cnorm_b, m_w_out_odd, m_ln_post_odd, v_ln_pre_even, v_w_in_even, v_pool_w, v_pool_scale, v_w_out_even, v_ln_post_even, v_ln_pre_odd, v_w_in_odd, v_sconv_w, v_dconv_w, v_dconv_b, v_cnorm_g, v_cnorm_b, v_w_out_odd, v_ln_post_odd):
    _, s, d = x.shape
    half = d // 2
    cw = half // N_CHIPS
    ng, q, gd = pool_w.shape[1:]
    k3, k31 = sconv_w.shape[1], dconv_w.shape[1]
    x2d, tgt = x[0], loss_target[0]
    me = 2 * lax.axis_index("x") + lax.axis_index("y")
    core = lax.axis_index("c")
    c_arr = jnp.reshape(core, (1,)).astype(jnp.int32)
    me_arr = jnp.reshape(me, (1,)).astype(jnp.int32)
    mc_arr = jnp.stack([me, core]).astype(jnp.int32)

    shards = [w_in_even[0], w_out_even[0], w_in_odd[0], w_out_odd[0]]
    slabs = [_cast_bf16_own_slab(w, me_arr, f"cast_w{n}") for n, w in enumerate(shards)]
    pool_w_b = _cast_bf16(pool_w[0].reshape(ng * q, gd), "cast_pool_w").reshape(ng, q, gd)
    pack_w, at_w = _stack_rows([sconv_w[0], dconv_w[0], dconv_b, cnorm_g, cnorm_b], 8)
    pack_d, at_d = _stack_rows([ln_pre_odd, ln_post_odd], 8)
    win_e, pool_w_f, pack_w_f, pack_d_f = _gather_weights(slabs[:1], pool_w_b, pack_w, pack_d, "gather_first")
    sconv_f = pack_w_f[at_w[0]:at_w[0] + k3]
    dconv_f = pack_w_f[at_w[1]:at_w[1] + k31]
    dconv_b_f, cnorm_g_f, cnorm_b_f = (pack_w_f[at_w[n]:at_w[n] + 1] for n in (2, 3, 4))
    ln_pre_odd_f = pack_d_f[at_d[0]:at_d[0] + 1]
    ln_post_odd_f = pack_d_f[at_d[1]:at_d[1] + 1]

    h0 = _rms_fwd(x2d, ln_pre_even, "rms_pre_even")
    plans = _Multi([_GatherPlan([slabs[1]], at=(0.6, 0.88)), _GatherPlan([slabs[2]], (0, 1, 4), at=(0.6, 0.88))])
    p_e, extra = _mm_nn(h0, win_e, "proj_in_even", plans)
    (wout_e,), (win_o,) = plans.results(extra)
    wout_e = wout_e.reshape(d, d)
    att, ltot, (win_o,) = _sba_fwd(p_e, half, "sba_fwd", _GatherPlan([win_o], (1, 4, 4), at=(0.69, 0.94)))
    y_e = _even_mix_fwd(p_e, att, pool_w_f, pool_scale, d, "even_mix_fwd")
    o_e, x1, h1 = _mm_out_even(y_e, wout_e, x2d, ln_post_even, ln_pre_odd_f, "proj_out_even")
    p_o, (wout_o,) = _mm_nn(h1, win_o, "proj_in_odd", _GatherPlan([slabs[3]]))
    wout_o = wout_o.reshape(d, d)
    y_o, s3, d1 = _odd_mix_fwd(p_o, sconv_f, dconv_f, dconv_b_f, cnorm_g_f, cnorm_b_f, d, "odd_mix_fwd")
    do_o, dx2, loss_blk, dln_post_odd = _mm_out_odd(y_o, wout_o, x1, ln_post_odd_f, tgt, "proj_out_odd_loss")

    dy_o = _mm_nt(do_o, wout_o, "dy_odd")
    g_wout_o = _mm_tn(y_o, do_o, 1, "dw_out_odd")[0].reshape(N_CHIPS, d // N_CHIPS, d)
    (dbc, dgate_o, ds3, dd1, dcnorm_g, dcnorm_b, ddconv_b), (got,) = _odd_bwd_rows(
        p_o, s3, d1, dy_o, cnorm_g_f, cnorm_b_f, d, "odd_bwd_rows", _SwapPlan([g_wout_o]))
    h_wout_o = _half_add(g_wout_o, got, c_arr, "half_add_out_odd")
    dhc, dcc, dga, dgb, dsconv, ddconv = _odd_bwd_conv(p_o, ds3, dd1, sconv_f, dconv_f, d, "odd_bwd_conv")
    dp_o = jnp.concatenate([dhc, dbc, dcc, dga, dgb, dgate_o], axis=1)
    g_win_o, (s_wout_o,) = _mm_tn(h1, dp_o, N_CHIPS, "dw_in_odd", _ScatterPlan([h_wout_o]))
    (dx1, dln_pre_odd, do_e, dln_post_even), (got,) = _mm_in_bwd(
        dp_o, win_o, x1, ln_pre_odd_f, dx2, (o_e, ln_post_even), "dx_odd", _SwapPlan([g_win_o]))
    h_win_o = _half_add(g_win_o, got, c_arr, "half_add_in_odd")

    dy_e = _mm_nt(do_e, wout_e, "dy_even")
    g_wout_e = _mm_tn(y_e, do_e, 1, "dw_out_even")[0].reshape(N_CHIPS, d // N_CHIPS, d)
    (datt, du, dgate_e, dpool_scale, dpool_w), (got,) = _even_mix_bwd(
        p_e, att, dy_e, pool_w_f, pool_scale, d, "even_mix_bwd", _SwapPlan([g_wout_e]))
    h_wout_e = _half_add(g_wout_e, got, c_arr, "half_add_out_even")
    two = lambda v: v.reshape(2, half)
    small_parts = [dpool_scale, two(dln_post_even), two(dln_pre_odd), two(dln_post_odd),
                   dsconv, ddconv, ddconv_b, dcnorm_g, dcnorm_b, dpool_w.reshape(gd, half)]
    small, at_s = _stack_rows(small_parts, 16)
    plans = _Multi([_ScatterPlan([h_win_o]), _SendWholePlan([small])])
    dq, dk, dv, extra = _sba_bwd(p_e, ltot, datt, half, "sba_bwd", plans)
    (s_win_o,), (small1,) = plans.results(extra)
    small2 = _add2(small, small1, "small_add")
    dp_e = jnp.concatenate([dq, dk, dv, du, dgate_e], axis=1)
    plans = _Multi([_ScatterPlan([h_wout_e]), _ShareHalfPlan([small2])])
    g_win_e, extra = _mm_tn(h0, dp_e, N_CHIPS, "dw_in_even", plans)
    (s_wout_e,), (small_got,) = plans.results(extra)
    swap = _swap_start(g_win_e, "swap_in_even_start")
    pairs = [(h_wout_e, s_wout_e), (h_win_o, s_win_o), (h_wout_o, s_wout_o)]
    parts = [_sum_chips(h, r, mc_arr, f"sum_chips{n + 1}") for n, (h, r) in enumerate(pairs)]
    parts.append(_sum_chips_ordered(small2, small_got, mc_arr, "small_sum"))
    g_win_e, got = _swap_wait(*swap[:4], parts[-1], "swap_in_even_wait")
    join_sems = _join_start(parts, "join_first_start")
    h_win_e = _half_add(g_win_e, got, c_arr, "half_add_in_even")
    send_sems, recv_sems, h_win_e, landing, token = _scatter_start(h_win_e, "scatter_in_even_start")
    (grad_x, dln_pre_even), _ = _mm_in_bwd(dp_e, win_e, x2d, ln_pre_even + token[0:1, 0:1], dx1, None, "dx_even")

    last, at_l = _stack_rows([two(dln_pre_even), jnp.pad(loss_blk[0:1], ((0, 0), (0, half - LANES)))], 16)
    (last1,) = _swap_with_sibling([], [last], "swap_last")
    last2 = _add2(last, last1, "last_add")
    (last_got,) = _scatter_to_chips([], last2, "scatter_last")
    last_sum = _sum_chips_ordered(last2, last_got, mc_arr, "last_sum")
    h_win_e, s_win_e = _scatter_wait(send_sems, recv_sems, h_win_e, landing, last_sum, "scatter_in_even_wait")
    last_sems = _join_start([_sum_chips(h_win_e, s_win_e, mc_arr, "sum_chips0"), last_sum], "join_last_start")
    gw_out_e, gw_in_o, gw_out_o, red = _join_wait(*join_sems[:3], last_sems[3], "join_first_wait")

    def rows(n, cnt):
        return red[at_s[n]:at_s[n] + cnt]

    def mine(a, width):
        return lax.dynamic_slice_in_dim(a, me * width, width, axis=1)

    quarter = d // N_CHIPS
    g_small = {
        "pool_scale": rows(0, 1),
        "ln_post_even": rows(1, 2).reshape(1, d),
        "ln_pre_odd": mine(rows(2, 2).reshape(1, d), quarter),
        "ln_post_odd": mine(rows(3, 2).reshape(1, d), quarter),
        "sconv_w": mine(rows(4, k3), cw),
        "dconv_w": mine(rows(5, k31), cw),
        "dconv_b": mine(rows(6, 1), cw),
        "cnorm_g": mine(rows(7, 1), cw),
        "cnorm_b": mine(rows(8, 1), cw),
        "pool_w": lax.dynamic_slice_in_dim(rows(9, gd).reshape(ng, gd, gd), me * q, q, axis=1).reshape(ng * q, gd),
    }
    w2d = {
        "ln_pre_even": ln_pre_even, "w_in_even": w_in_even[0], "pool_w": pool_w[0].reshape(ng * q, gd),
        "pool_scale": pool_scale, "w_out_even": w_out_even[0], "ln_post_even": ln_post_even, "ln_pre_odd": ln_pre_odd,
        "w_in_odd": w_in_odd[0], "sconv_w": sconv_w[0], "dconv_w": dconv_w[0], "dconv_b": dconv_b, "cnorm_g": cnorm_g,
        "cnorm_b": cnorm_b, "w_out_odd": w_out_odd[0], "ln_post_odd": ln_post_odd,
    }
    moments = {
        "ln_pre_even": (m_ln_pre_even, v_ln_pre_even), "w_in_even": (m_w_in_even, v_w_in_even),
        "pool_w": (m_pool_w, v_pool_w), "pool_scale": (m_pool_scale, v_pool_scale),
        "w_out_even": (m_w_out_even, v_w_out_even), "ln_post_even": (m_ln_post_even, v_ln_post_even),
        "ln_pre_odd": (m_ln_pre_odd, v_ln_pre_odd), "w_in_odd": (m_w_in_odd, v_w_in_odd),
        "sconv_w": (m_sconv_w, v_sconv_w), "dconv_w": (m_dconv_w, v_dconv_w), "dconv_b": (m_dconv_b, v_dconv_b),
        "cnorm_g": (m_cnorm_g, v_cnorm_g), "cnorm_b": (m_cnorm_b, v_cnorm_b),
        "w_out_odd": (m_w_out_odd, v_w_out_odd), "ln_post_odd": (m_ln_post_odd, v_ln_post_odd),
    }
    def update(name, g):
        m_in, v_in = moments[name]
        w = w2d[name]
        return _adamw(w, g, m_in.reshape(w.shape), v_in.reshape(w.shape), "adamw_" + name)[0]

    updates = {"w_in_odd": update("w_in_odd", gw_in_o)}
    gw_in_e, red_last = _join_wait(*last_sems[:3], updates["w_in_odd"][1], "join_last_wait")
    loss = red_last[at_l[1], 0]
    g_small["ln_pre_even"] = red_last[at_l[0]:at_l[0] + 2].reshape(1, d)
    for name, g in dict(g_small, w_in_even=gw_in_e, w_out_even=gw_out_e, w_out_odd=gw_out_o).items():
        updates[name] = update(name, g)
    outs = [[u.reshape(moments[name][0].shape) for u in updates[name]] for name in w2d]
    grads_out, deltas, new_m, new_v = zip(*outs)
    return (loss, grad_x.reshape(x.shape), *grads_out, *deltas, *new_m, *new_v)
```

```python
import functools
import math

import jax
import jax.numpy as jnp
from jax import lax
from jax.experimental import pallas as pl
from jax.experimental.pallas import tpu as pltpu

F32 = jnp.float32
BF16 = jnp.bfloat16
EPS = 1e-6
N_CHIPS = 4
VMEM_LIMIT_V7X = 56 << 20
HEAD_DIM = 128
ATT_BLOCK = 256
POOL_WINDOWS = (2, 4, 8, 16)
ROW_TILE = 256
POOL_HALO = 16
CONV_HALO = 32
LANES = 128
ADAM_LR, ADAM_B1, ADAM_B2, ADAM_EPS, ADAM_WD, ADAM_STEP = 0.001, 0.9, 0.999, 1e-08, 0.01, 10
MESH_ID = pl.DeviceIdType.MESH
ANY = pl.BlockSpec(memory_space=pl.ANY)


def _cp(*sem):
    return pltpu.CompilerParams(dimension_semantics=sem or None, vmem_limit_bytes=VMEM_LIMIT_V7X)


def _pick_tile(n, cap):
    best = None
    for t in range(LANES, min(n, cap) + 1, LANES):
        if n % t == 0:
            best = t
    assert best is not None, (n, cap)
    return best


def _sigmoid(x):
    return 1.0 / (1.0 + jnp.exp(-x))


def _silu(x):
    return x * _sigmoid(x)


def _dsilu(x):
    s = _sigmoid(x)
    return s * (1.0 + x * (1.0 - s))


def _log_sigmoid(z):
    return jnp.minimum(z, 0.0) - jnp.log(1.0 + jnp.exp(-jnp.abs(z)))


def _rms_stats(x):
    r = lax.rsqrt(jnp.mean(x * x, axis=-1, keepdims=True) + EPS)
    return x * r, r


def _rms_bwd(dh, xhat, r, g):
    dxh = dh * g
    dx = r * (dxh - xhat * jnp.mean(dxh * xhat, axis=-1, keepdims=True))
    return dx, jnp.sum(dh * xhat, axis=0, keepdims=True)


def _acc_rows(ref, first, val):
    @pl.when(first)
    def _():
        ref[...] = val

    @pl.when(jnp.logical_not(first))
    def _():
        ref[...] += val


def _rcopy(src, dst, ssem, rsem, dev):
    return pltpu.make_async_remote_copy(src_ref=src, dst_ref=dst, send_sem=ssem, recv_sem=rsem,
                                        device_id=dev, device_id_type=MESH_ID)


def _place():
    x, y, c = lax.axis_index("x"), lax.axis_index("y"), lax.axis_index("c")
    chips = [(1 - x, y), (x, 1 - y), (1 - x, 1 - y)]
    return x, y, c, 2 * x + y, chips, (x, y, 1 - c)


class _GatherPlan:
    PER_ARRAY = 7

    def __init__(self, arrays, part=(0, 1, 1), at=(0.5, 0.8)):
        self.operands = list(arrays)
        self.out_shapes = [jax.ShapeDtypeStruct(a.shape, a.dtype) for a in arrays]
        self.aliases = {i: i for i in range(len(arrays))}
        self.nsems = self.PER_ARRAY * len(arrays)
        self.base = 0
        self.halves = [a.shape[1] // 2 for a in arrays]
        self.part = part
        self.at = at

    def schedule(self):
        return [(0.0, self.start), (self.at[0], self.relay), (self.at[1], self.relay_far)]

    def _rows(self, ref, a, chip, half, quarter=None):
        lo, hi, n = self.part
        h = self.halves[a]
        first, size = half * h + lo * h // n, (hi - lo) * h // n
        if quarter is not None:
            first, size = first + quarter * (size // 2), size // 2
        return ref.at[chip, pl.ds(first, size)]

    def _copy(self, src, dst, a, n, ssem, rsem, dev):
        return _rcopy(src, dst, ssem.at[self.base + self.PER_ARRAY * a + n], rsem.at[self.base + self.PER_ARRAY * a + n], dev)

    def _own(self, ins, outs, ssem, rsem):
        x, y, c, me, chips, sib = _place()
        return [self._copy(self._rows(ins[a], a, me, c), self._rows(outs[a], a, me, c), a, k, ssem, rsem, (*chips[k], c))
                for a in range(len(ins)) for k in (0, 1)]

    def _relays(self, outs, ssem, rsem, a, k):
        x, y, c, me, chips, sib = _place()
        chip = 2 * chips[k][0] + chips[k][1]
        whole, quarter = self._rows(outs[a], a, chip, c), self._rows(outs[a], a, chip, c, k)
        return (self._copy(whole, whole, a, k, ssem, rsem, (*chips[k], c)),
                self._copy(quarter, quarter, a, 2 + k, ssem, rsem, (*chips[1 - k], c)),
                self._copy(whole, whole, a, 4 + k, ssem, rsem, sib))

    def _far(self, outs, ssem, rsem, a):
        x, y, c, me, chips, sib = _place()
        chip = 2 * chips[2][0] + chips[2][1]
        whole = self._rows(outs[a], a, chip, c)
        got = [self._copy(q, q, a, 2 + k, ssem, rsem, (*chips[1 - k], c))
               for k, q in enumerate([self._rows(outs[a], a, chip, c, 0), self._rows(outs[a], a, chip, c, 1)])]
        return got, self._copy(whole, whole, a, 6, ssem, rsem, sib)

    def start(self, ins, outs, ssem, rsem):
        for cp in self._own(ins, outs, ssem, rsem):
            cp.start()

    def relay(self, ins, outs, ssem, rsem):
        for a in range(len(outs)):
            for k in (0, 1):
                landed, onward, to_sibling = self._relays(outs, ssem, rsem, a, k)
                landed.wait_recv()
                onward.start()
                to_sibling.start()

    def relay_far(self, ins, outs, ssem, rsem):
        for a in range(len(outs)):
            got, to_sibling = self._far(outs, ssem, rsem, a)
            for cp in got:
                cp.wait_recv()
            to_sibling.start()

    def finish(self, ins, outs, ssem, rsem):
        x, y, c, me, chips, sib = _place()
        for a in range(len(outs)):
            for k in range(3):
                ref = self._rows(outs[a], a, 2 * chips[k][0] + chips[k][1], 1 - c)
                self._copy(ref, ref, a, 4 + k, ssem, rsem, sib).wait_recv()
        for cp in self._own(ins, outs, ssem, rsem):
            cp.wait_send()
        for a in range(len(outs)):
            for k in (0, 1):
                _, onward, to_sibling = self._relays(outs, ssem, rsem, a, k)
                onward.wait_send()
                to_sibling.wait_send()
            self._far(outs, ssem, rsem, a)[1].wait_send()


class _ScatterPlan:
    def __init__(self, arrays, part=(0, 1, 1), into=None):
        self.n = len(arrays)
        self.operands = list(arrays) + list(into or [])
        self.out_shapes = [jax.ShapeDtypeStruct((3,) + a.shape[1:], a.dtype) for a in arrays]
        self.aliases = {self.n + i: i for i in range(self.n)} if into else {}
        self.nsems = 3 * self.n
        self.base = 0
        self.part = part

    def _copies(self, ins, outs, ssem, rsem):
        x, y, c, me, chips, sib = _place()
        lo, hi, n = self.part
        out = []
        for a in range(self.n):
            h = ins[a].shape[1]
            rows = pl.ds(lo * h // n, (hi - lo) * h // n)
            for k, chip in enumerate(chips):
                out.append(_rcopy(ins[a].at[2 * chip[0] + chip[1], rows], outs[a].at[k, rows],
                                  ssem.at[self.base + 3 * a + k], rsem.at[self.base + 3 * a + k], (*chip, c)))
        return out

    def schedule(self):
        return [(0.0, self.start)]

    def start(self, ins, outs, ssem, rsem):
        for cp in self._copies(ins, outs, ssem, rsem):
            cp.start()

    def finish(self, ins, outs, ssem, rsem):
        cps = self._copies(ins, outs, ssem, rsem)
        for cp in cps:
            cp.wait_recv()
        for cp in cps:
            cp.wait_send()


class _ShareHalfPlan(_ScatterPlan):
    def __init__(self, arrays):
        super().__init__(arrays)
        self.out_shapes = [jax.ShapeDtypeStruct((3, a.shape[0] // 2, a.shape[1]), a.dtype) for a in arrays]

    def _copies(self, ins, outs, ssem, rsem):
        x, y, c, me, chips, sib = _place()
        out = []
        for a in range(self.n):
            rh = ins[a].shape[0] // 2
            for k, chip in enumerate(chips):
                out.append(_rcopy(ins[a].at[pl.ds(c * rh, rh)], outs[a].at[k],
                                  ssem.at[self.base + 3 * a + k], rsem.at[self.base + 3 * a + k], (*chip, c)))
        return out


class _SwapPlan:
    def __init__(self, grads):
        self.operands = list(grads)
        self.out_shapes = [jax.ShapeDtypeStruct((g.shape[0], g.shape[1] // 2, g.shape[2]), g.dtype) for g in grads]
        self.aliases = {}
        self.nsems = len(grads)
        self.base = 0

    def _copies(self, ins, outs, ssem, rsem):
        x, y, c, me, chips, sib = _place()
        out = []
        for a, src in enumerate(ins):
            h = src.shape[1] // 2
            out.append(_rcopy(src.at[:, pl.ds((1 - c) * h, h), :], outs[a], ssem.at[self.base + a], rsem.at[self.base + a], sib))
        return out

    def schedule(self):
        return [(0.0, self.start)]

    def start(self, ins, outs, ssem, rsem):
        for cp in self._copies(ins, outs, ssem, rsem):
            cp.start()

    def finish(self, ins, outs, ssem, rsem):
        cps = self._copies(ins, outs, ssem, rsem)
        for cp in cps:
            cp.wait_recv()
        for cp in cps:
            cp.wait_send()


class _SendWholePlan(_SwapPlan):
    def __init__(self, arrays):
        self.operands = list(arrays)
        self.out_shapes = [jax.ShapeDtypeStruct(a.shape, a.dtype) for a in arrays]
        self.aliases = {}
        self.nsems = len(arrays)
        self.base = 0

    def _copies(self, ins, outs, ssem, rsem):
        x, y, c, me, chips, sib = _place()
        return [_rcopy(src, outs[a], ssem.at[self.base + a], rsem.at[self.base + a], sib) for a, src in enumerate(ins)]


class _Multi:
    def __init__(self, plans):
        self.plans = plans
        self.operands, self.out_shapes, self.aliases, self.nsems = [], [], {}, 0
        self.spans = []
        for p in plans:
            ni, no = len(self.operands), len(self.out_shapes)
            self.spans.append((ni, ni + len(p.operands), no, no + len(p.out_shapes)))
            self.aliases.update({ni + i: no + j for i, j in p.aliases.items()})
            p.base = self.nsems
            self.nsems += p.nsems
            self.operands += p.operands
            self.out_shapes += p.out_shapes

    def schedule(self):
        def bound(fn, span):
            i0, i1, o0, o1 = span
            return lambda ins, outs, ssem, rsem: fn(ins[i0:i1], outs[o0:o1], ssem, rsem)

        stages = [(at, bound(fn, span)) for p, span in zip(self.plans, self.spans) for at, fn in p.schedule()]
        return sorted(stages, key=lambda s: s[0])

    def finish(self, ins, outs, ssem, rsem):
        for p, (i0, i1, o0, o1) in zip(self.plans, self.spans):
            p.finish(ins[i0:i1], outs[o0:o1], ssem, rsem)

    def results(self, extra):
        return [list(extra[o0:o1]) for (_, _, o0, o1) in self.spans]


class _Host:
    def __init__(self, comm, in_specs, out_specs, out_shape, scratch):
        self.comm = comm
        self.n_in, self.n_out = len(in_specs), len(out_specs)
        self.in_specs, self.out_specs, self.out_shape, self.scratch = list(in_specs), list(out_specs), list(out_shape), list(scratch)
        self.aliases = {}
        self.args = []
        if comm is not None:
            self.in_specs += [ANY] * len(comm.operands)
            self.out_specs += [ANY] * len(comm.out_shapes)
            self.out_shape += comm.out_shapes
            self.scratch += [pltpu.SemaphoreType.DMA((comm.nsems,)), pltpu.SemaphoreType.DMA((comm.nsems,))]
            self.aliases = {self.n_in + i: self.n_out + j for i, j in comm.aliases.items()}
            self.args = list(comm.operands)

    def split(self, refs):
        nc = len(self.args)
        nco = len(self.out_shape) - self.n_out
        ins, p = refs[:self.n_in], self.n_in + nc
        outs, rest = refs[p:p + self.n_out], refs[p + self.n_out + nco:]
        self._cargs = None
        if self.comm is not None:
            self._cargs = (refs[self.n_in:p], refs[p + self.n_out:p + self.n_out + nco], rest[-2], rest[-1])
            rest = rest[:-2]
        return ins, outs, rest

    def before(self, step, total):
        if self.comm is None:
            return

        for at, stage in self.comm.schedule():
            pl.when(step == min(total - 1, int(at * total)))(functools.partial(stage, *self._cargs))

    def after(self, step, total):
        if self.comm is None:
            return

        @pl.when(step == total - 1)
        def _():
            self.comm.finish(*self._cargs)

    def results(self, outs):
        return outs[:self.n_out], outs[self.n_out:]


def _cast_bf16(x, name):
    r, c = x.shape
    tr = ROW_TILE if r % ROW_TILE == 0 else r

    def body(x_ref, o_ref):
        o_ref[...] = x_ref[...].astype(BF16)

    return pl.pallas_call(
        body, name=name, grid=(r // tr,),
        in_specs=[pl.BlockSpec((tr, c), lambda i: (i, 0))],
        out_specs=pl.BlockSpec((tr, c), lambda i: (i, 0)),
        out_shape=jax.ShapeDtypeStruct((r, c), BF16), compiler_params=_cp("parallel"))(x)


def _cast_bf16_own_slab(x, me_arr, name):
    r, c = x.shape
    tr = ROW_TILE if r % ROW_TILE == 0 else r

    def body(me_ref, x_ref, o_ref):
        o_ref[...] = x_ref[...].astype(BF16)

    return pl.pallas_call(
        body, name=name,
        grid_spec=pltpu.PrefetchScalarGridSpec(
            num_scalar_prefetch=1, grid=(r // tr,),
            in_specs=[pl.BlockSpec((tr, c), lambda i, me: (i, 0))],
            out_specs=pl.BlockSpec((None, tr, c), lambda i, me: (me[0], i, 0))),
        out_shape=jax.ShapeDtypeStruct((N_CHIPS, r, c), BF16), compiler_params=_cp("parallel"))(me_arr, x)


def _rms_fwd(x, g, name):
    s, d = x.shape

    def body(x_ref, g_ref, h_ref):
        xhat, _ = _rms_stats(x_ref[...])
        h_ref[...] = (xhat * g_ref[...]).astype(BF16)

    return pl.pallas_call(
        body, name=name, grid=(s // ROW_TILE,),
        in_specs=[pl.BlockSpec((ROW_TILE, d), lambda i: (i, 0)), pl.BlockSpec((1, d), lambda i: (0, 0))],
        out_specs=pl.BlockSpec((ROW_TILE, d), lambda i: (i, 0)),
        out_shape=jax.ShapeDtypeStruct((s, d), BF16), compiler_params=_cp("parallel"))(x, g)


def _mm_nn(a, w3, name, comm=None):
    m, k = a.shape
    nsh, _, ns = w3.shape
    tm = 512 if m % 512 == 0 else ROW_TILE
    tn = _pick_tile(ns, 1024)
    per = ns // tn
    grid = (nsh * per, m // tm)
    host = _Host(comm,
                 [pl.BlockSpec((tm, k), lambda n, i: (i, 0)), pl.BlockSpec((None, k, tn), lambda n, i: (n // per, 0, n % per))],
                 [pl.BlockSpec((tm, tn), lambda n, i: (i, n))], [jax.ShapeDtypeStruct((m, nsh * ns), F32)], [])

    def body(*refs):
        (a_ref, w_ref), (o_ref,), _ = host.split(refs)
        step = pl.program_id(0) * grid[1] + pl.program_id(1)
        host.before(step, grid[0] * grid[1])
        o_ref[...] = jnp.dot(a_ref[...], w_ref[...], preferred_element_type=F32)
        host.after(step, grid[0] * grid[1])

    outs = pl.pallas_call(
        body, name=name, grid=grid, in_specs=host.in_specs, out_specs=host.out_specs, out_shape=host.out_shape,
        scratch_shapes=host.scratch, input_output_aliases=host.aliases,
        compiler_params=_cp("arbitrary", "arbitrary"))(a, w3, *host.args)
    (out,), extra = host.results(outs)
    return out, extra


def _mm_nt(a, b, name):
    m, k = a.shape
    n = b.shape[0]
    tm = 512 if m % 512 == 0 else ROW_TILE

    def body(a_ref, b_ref, o_ref):
        o_ref[...] = lax.dot_general(a_ref[...], b_ref[...], (((1,), (1,)), ((), ())), preferred_element_type=F32)

    return pl.pallas_call(
        body, name=name, grid=(m // tm,),
        in_specs=[pl.BlockSpec((tm, k), lambda i: (i, 0)), pl.BlockSpec((n, k), lambda i: (0, 0))],
        out_specs=pl.BlockSpec((tm, n), lambda i: (i, 0)),
        out_shape=jax.ShapeDtypeStruct((m, n), F32), compiler_params=_cp("parallel"))(a, b)


def _mm_tn(a, b, nsh, name, comm=None):
    s, m = a.shape
    n = b.shape[1]
    ns = n // nsh
    tm = 512 if m % 512 == 0 else ROW_TILE
    tn = _pick_tile(ns, 1024)
    per = ns // tn
    grid = (nsh * per, m // tm)
    host = _Host(comm, [pl.BlockSpec((s, tm), lambda j, i: (0, i)), pl.BlockSpec((s, tn), lambda j, i: (0, j))],
                 [pl.BlockSpec((None, tm, tn), lambda j, i: (j // per, i, j % per))],
                 [jax.ShapeDtypeStruct((nsh, m, ns), BF16)], [])

    def body(*refs):
        (a_ref, b_ref), (o_ref,), _ = host.split(refs)
        step = pl.program_id(0) * grid[1] + pl.program_id(1)
        host.before(step, grid[0] * grid[1])
        o_ref[...] = lax.dot_general(a_ref[...], b_ref[...], (((0,), (0,)), ((), ())),
                                     preferred_element_type=F32).astype(BF16)
        host.after(step, grid[0] * grid[1])

    outs = pl.pallas_call(
        body, name=name, grid=grid, in_specs=host.in_specs, out_specs=host.out_specs, out_shape=host.out_shape,
        scratch_shapes=host.scratch, input_output_aliases=host.aliases,
        compiler_params=_cp("arbitrary", "arbitrary"))(a, b, *host.args)
    (out,), extra = host.results(outs)
    return out, extra


def _tri(n, rel):
    row = lax.broadcasted_iota(jnp.int32, (2 * n, n), 0)
    col = lax.broadcasted_iota(jnp.int32, (2 * n, n), 1)
    return jnp.where(rel(jnp.where(row >= n, row - n, row), col), 1.0, 0.0).astype(BF16)


def _dot_split(x, tri2):
    hi = x.astype(BF16)
    lo = (x - hi.astype(F32)).astype(BF16)
    return jnp.dot(jnp.concatenate([hi, lo], axis=1), tri2, preferred_element_type=F32)


def _nt(a, b):
    return lax.dot_general(a, b, (((1,), (1,)), ((), ())), preferred_element_type=F32)


def _tn(a, b):
    return lax.dot_general(a, b, (((0,), (0,)), ((), ())), preferred_element_type=F32)


def _heads_per_step(nh):
    return max(h for h in (1, 2, 4) if nh % h == 0)


def _sba_fwd(p, sbw, name, comm=None):
    s = p.shape[0]
    nh = sbw // HEAD_DIM
    hp = _heads_per_step(nh)
    ngrp, hw = nh // hp, hp * HEAD_DIM
    blk = ATT_BLOCK
    nq = s // blk
    scale = 1.0 / math.sqrt(HEAD_DIM)
    host = _Host(comm,
                 [pl.BlockSpec((blk, hw), lambda g, i: (i, g)),
                  pl.BlockSpec((s, hw), lambda g, i: (0, ngrp + g)),
                  pl.BlockSpec((s, hw), lambda g, i: (0, 2 * ngrp + g))],
                 [pl.BlockSpec((blk, hw), lambda g, i: (i, g))] * 2,
                 [jax.ShapeDtypeStruct((s, sbw), F32)] * 2,
                 [pltpu.VMEM((s, hw), BF16)] * 2)

    def body(*refs):
        (q_ref, k_ref, v_ref), (o_ref, lt_ref), (kb_ref, vb_ref) = host.split(refs)
        i = pl.program_id(1)
        step = pl.program_id(0) * nq + i
        host.before(step, ngrp * nq)

        @pl.when(i == 0)
        def _():
            kb_ref[...] = k_ref[...].astype(BF16)
            vb_ref[...] = v_ref[...].astype(BF16)

        heads = [slice(h * HEAD_DIM, (h + 1) * HEAD_DIM) for h in range(hp)]
        qs = [q_ref[:, hd].astype(BF16) for hd in heads]
        later = _tri(blk, lambda r, c: r > c)
        causal = lax.broadcasted_iota(jnp.int32, (blk, blk), 1) < lax.broadcasted_iota(jnp.int32, (blk, blk), 0)

        def key_block(j, carry, diagonal):
            rows = pl.ds(pl.multiple_of(j * blk, blk), blk)
            hs = range(hp)
            z = [_nt(qs[h], kb_ref[rows, heads[h]]) * scale for h in hs]
            ls = [_log_sigmoid(z[h]) for h in hs]
            lm = [jnp.where(causal, ls[h] - z[h], 0.0) if diagonal else ls[h] - z[h] for h in hs]
            stay = [_dot_split(lm[h], later) for h in hs]
            w = [jnp.exp(ls[h] + stay[h] + carry[h][1]) for h in hs]
            if diagonal:
                w = [jnp.where(causal, w[h], 0.0) for h in hs]
            acc = [carry[h][0] + jnp.dot(w[h].astype(BF16), vb_ref[rows, heads[h]], preferred_element_type=F32) for h in hs]
            return tuple((acc[h], carry[h][1] + jnp.sum(lm[h], axis=1, keepdims=True)) for h in hs)

        init = tuple((jnp.zeros((blk, HEAD_DIM), F32), jnp.zeros((blk, 1), F32)) for _ in heads)
        carry = key_block(i, init, True)
        carry = lax.fori_loop(0, i, lambda n, c: key_block(i - 1 - n, c, False), carry)
        for h, hd in enumerate(heads):
            o_ref[:, hd] = carry[h][0]
            lt_ref[:, hd] = jnp.broadcast_to(carry[h][1], (blk, HEAD_DIM))
        host.after(step, ngrp * nq)

    outs = pl.pallas_call(
        body, name=name, grid=(ngrp, nq), in_specs=host.in_specs, out_specs=host.out_specs, out_shape=host.out_shape,
        scratch_shapes=host.scratch, input_output_aliases=host.aliases,
        compiler_params=_cp("arbitrary", "arbitrary"))(p, p, p, *host.args)
    (out, ltot), extra = host.results(outs)
    return out, ltot, extra


def _sba_bwd(p, ltot, dout, sbw, name, comm=None):
    s = p.shape[0]
    nh = sbw // HEAD_DIM
    hp = _heads_per_step(nh)
    ngrp, hw = nh // hp, hp * HEAD_DIM
    blk = ATT_BLOCK
    nq = s // blk
    scale = 1.0 / math.sqrt(HEAD_DIM)
    blk_spec = pl.BlockSpec((blk, hw), lambda g, i: (i, g))
    col_spec = pl.BlockSpec((s, hw), lambda g, i: (0, g))
    host = _Host(comm,
                 [blk_spec, pl.BlockSpec((s, hw), lambda g, i: (0, ngrp + g)),
                  pl.BlockSpec((s, hw), lambda g, i: (0, 2 * ngrp + g)), blk_spec, blk_spec],
                 [blk_spec, col_spec, col_spec], [jax.ShapeDtypeStruct((s, sbw), BF16)] * 3,
                 [pltpu.VMEM((s, hw), BF16)] * 2 + [pltpu.VMEM((s, hw), F32)] * 2)

    def body(*refs):
        (q_ref, k_ref, v_ref, lt_ref, do_ref), (dq_ref, dk_ref, dv_ref), (kb_ref, vb_ref, dka_ref, dva_ref) = host.split(refs)
        i = pl.program_id(1)
        step = pl.program_id(0) * nq + i
        host.before(step, ngrp * nq)

        @pl.when(i == 0)
        def _():
            kb_ref[...] = k_ref[...].astype(BF16)
            vb_ref[...] = v_ref[...].astype(BF16)
            dka_ref[...] = jnp.zeros_like(dka_ref)
            dva_ref[...] = jnp.zeros_like(dva_ref)

        heads = [slice(h * HEAD_DIM, (h + 1) * HEAD_DIM) for h in range(hp)]
        qs = [q_ref[:, hd].astype(BF16) for hd in heads]
        dos = [do_ref[:, hd].astype(BF16) for hd in heads]
        ltots = [lt_ref[:, h * HEAD_DIM:h * HEAD_DIM + 1] for h in range(hp)]
        upto = _tri(blk, lambda r, c: r <= c)
        before = _tri(blk, lambda r, c: r < c)
        causal = lax.broadcasted_iota(jnp.int32, (blk, blk), 1) < lax.broadcasted_iota(jnp.int32, (blk, blk), 0)

        def key_block(j, carry, diagonal):
            rows = pl.ds(pl.multiple_of(j * blk, blk), blk)
            hs = range(hp)
            kj = [kb_ref[rows, heads[h]] for h in hs]
            vj = [vb_ref[rows, heads[h]] for h in hs]
            z = [_nt(qs[h], kj[h]) * scale for h in hs]
            dw = [_nt(dos[h], vj[h]) for h in hs]
            ls = [_log_sigmoid(z[h]) for h in hs]
            lm = [jnp.where(causal, ls[h] - z[h], 0.0) if diagonal else ls[h] - z[h] for h in hs]
            stay = [ltots[h] - carry[h][1] - _dot_split(lm[h], upto) for h in hs]
            w = [jnp.exp(ls[h] + stay[h]) for h in hs]
            if diagonal:
                w = [jnp.where(causal, w[h], 0.0) for h in hs]
            da = [dw[h] * w[h] for h in hs]
            sig = [jnp.exp(ls[h]) for h in hs]
            chain = [sig[h] * (carry[h][2] + _dot_split(da[h], before)) for h in hs]
            if diagonal:
                chain = [jnp.where(causal, chain[h], 0.0) for h in hs]
            dzb = [((da[h] * (1.0 - sig[h]) - chain[h]) * scale).astype(BF16) for h in hs]
            dq = [carry[h][0] + jnp.dot(dzb[h], kj[h], preferred_element_type=F32) for h in hs]
            for h in hs:
                dka_ref[rows, heads[h]] += _tn(dzb[h], qs[h])
            for h in hs:
                dva_ref[rows, heads[h]] += _tn(w[h].astype(BF16), dos[h])
            return tuple((dq[h], carry[h][1] + jnp.sum(lm[h], axis=1, keepdims=True),
                          carry[h][2] + jnp.sum(da[h], axis=1, keepdims=True)) for h in hs)

        zero = jnp.zeros((blk, 1), F32)
        init = tuple((jnp.zeros((blk, HEAD_DIM), F32), zero, zero) for _ in heads)
        carry = lax.fori_loop(0, i, lambda j, c: key_block(j, c, False), init)
        carry = key_block(i, carry, True)
        for h, hd in enumerate(heads):
            dq_ref[:, hd] = carry[h][0].astype(BF16)

        @pl.when(i == nq - 1)
        def _():
            dk_ref[...] = dka_ref[...].astype(BF16)
            dv_ref[...] = dva_ref[...].astype(BF16)

        host.after(step, ngrp * nq)

    outs = pl.pallas_call(
        body, name=name, grid=(ngrp, nq), in_specs=host.in_specs, out_specs=host.out_specs, out_shape=host.out_shape,
        scratch_shapes=host.scratch, input_output_aliases=host.aliases,
        compiler_params=_cp("arbitrary", "arbitrary"))(p, p, p, ltot, dout, *host.args)
    (dq, dk, dv), extra = host.results(outs)
    return dq, dk, dv, extra


def _pool_groups(pad_ref, tile, row0, gd, halo):
    row = row0 + lax.broadcasted_iota(jnp.int32, (tile, 1), 0)
    out = []
    for gi, win in enumerate(POOL_WINDOWS):
        cs = slice(gi * gd, (gi + 1) * gd)
        tok = pad_ref[halo:halo + tile, cs]
        acc = tok
        for j in range(1, win):
            acc = acc + pad_ref[halo - j:halo - j + tile, cs]
        cnt = jnp.minimum(win, row + 1).astype(F32)
        out.append(acc / cnt - tok)
    return out


def _even_mix_fwd(p, att, pool_w, pool_scale, d, name):
    s = p.shape[0]
    half = d // 2
    gd = half // len(POOL_WINDOWS)
    t, hb = ROW_TILE, POOL_HALO

    def body(u_ref, uh_ref, g_ref, a_ref, pw_ref, sc_ref, y_ref, pad_ref):
        i = pl.program_id(0)
        pad_ref[0:hb, :] = jnp.where(i > 0, uh_ref[...], 0.0)
        pad_ref[hb:, :] = u_ref[...]
        pooled = _pool_groups(pad_ref, t, i * t, gd, hb)
        for gi in range(len(POOL_WINDOWS)):
            cs = slice(gi * gd, (gi + 1) * gd)
            po = jnp.dot(pooled[gi].astype(BF16), pw_ref[gi], preferred_element_type=F32) * sc_ref[:, cs]
            y_ref[:, half + gi * gd:half + (gi + 1) * gd] = (po * _silu(g_ref[:, half + gi * gd:half + (gi + 1) * gd])).astype(BF16)
        y_ref[:, :half] = (a_ref[...] * _silu(g_ref[:, :half])).astype(BF16)

    return pl.pallas_call(
        body, name=name, grid=(s // t,),
        in_specs=[pl.BlockSpec((t, half), lambda i: (i, 3)),
                  pl.BlockSpec((hb, half), lambda i: (jnp.maximum(i * (t // hb) - 1, 0), 3)),
                  pl.BlockSpec((t, d), lambda i: (i, 2)),
                  pl.BlockSpec((t, half), lambda i: (i, 0)),
                  pl.BlockSpec(pool_w.shape, lambda i: (0, 0, 0)),
                  pl.BlockSpec((1, half), lambda i: (0, 0))],
        out_specs=pl.BlockSpec((t, d), lambda i: (i, 0)),
        out_shape=jax.ShapeDtypeStruct((s, d), BF16),
        scratch_shapes=[pltpu.VMEM((hb + t, half), F32)],
        compiler_params=_cp("parallel"))(p, p, p, att, pool_w, pool_scale)


def _even_mix_bwd(p, att, dy, pool_w, pool_scale, d, name, comm=None):
    s = p.shape[0]
    half = d // 2
    ng = len(POOL_WINDOWS)
    gd = half // ng
    t, hb = ROW_TILE, POOL_HALO
    nt = s // t
    host = _Host(
        comm,
        [pl.BlockSpec((t, half), lambda i: (i, 3)),
         pl.BlockSpec((hb, half), lambda i: (jnp.maximum(i * (t // hb) - 1, 0), 3)),
         pl.BlockSpec((t, d), lambda i: (i, 2)),
         pl.BlockSpec((hb, half), lambda i: (jnp.minimum((i + 1) * (t // hb), s // hb - 1), 5)),
         pl.BlockSpec((t, half), lambda i: (i, 0)),
         pl.BlockSpec((t, d), lambda i: (i, 0)),
         pl.BlockSpec((hb, half), lambda i: (jnp.minimum((i + 1) * (t // hb), s // hb - 1), 1)),
         pl.BlockSpec(pool_w.shape, lambda i: (0, 0, 0)),
         pl.BlockSpec((1, half), lambda i: (0, 0))],
        [pl.BlockSpec((t, half), lambda i: (i, 0)),
         pl.BlockSpec((t, half), lambda i: (i, 0)),
         pl.BlockSpec((t, d), lambda i: (i, 0)),
         pl.BlockSpec((1, half), lambda i: (0, 0)),
         pl.BlockSpec((ng, gd, gd), lambda i: (0, 0, 0))],
        [jax.ShapeDtypeStruct((s, half), F32), jax.ShapeDtypeStruct((s, half), BF16),
         jax.ShapeDtypeStruct((s, d), BF16), jax.ShapeDtypeStruct((1, half), F32),
         jax.ShapeDtypeStruct((ng, gd, gd), F32)],
        [pltpu.VMEM((hb + t, half), F32), pltpu.VMEM((t + hb, half), F32)])

    def body(*refs):
        ((u_ref, uh_ref, g_ref, gh_ref, a_ref, dy_ref, dyh_ref, pw_ref, sc_ref),
         (da_ref, du_ref, dg_ref, dsc_ref, dpw_ref), (pad_ref, dn_ref)) = host.split(refs)
        i = pl.program_id(0)
        host.before(i, nt)
        first = i == 0
        pad_ref[0:hb, :] = jnp.where(i > 0, uh_ref[...], 0.0)
        pad_ref[hb:, :] = u_ref[...]
        pooled = _pool_groups(pad_ref, t, i * t, gd, hb)
        g1 = g_ref[:, :half]
        dy1 = dy_ref[:, :half]
        da_ref[...] = dy1 * _silu(g1)
        dg_ref[:, :half] = (dy1 * a_ref[...] * _dsilu(g1)).astype(BF16)
        row = i * t + lax.broadcasted_iota(jnp.int32, (t + hb, 1), 0)
        for gi, win in enumerate(POOL_WINDOWS):
            cs = slice(gi * gd, (gi + 1) * gd)
            cs2 = slice(half + gi * gd, half + (gi + 1) * gd)
            w = pw_ref[gi]
            pb = pooled[gi].astype(BF16)
            zp = jnp.dot(pb, w, preferred_element_type=F32)
            g2 = g_ref[:, cs2]
            dy2 = dy_ref[:, cs2]
            dg_ref[:, cs2] = (dy2 * zp * sc_ref[:, cs] * _dsilu(g2)).astype(BF16)
            dpo = dy2 * _silu(g2)
            _acc_rows(dsc_ref.at[:, cs], first, jnp.sum(dpo * zp, axis=0, keepdims=True))
            dz = (dpo * sc_ref[:, cs]).astype(BF16)
            _acc_rows(dpw_ref.at[gi], first, _tn(pb, dz))
            dzh = jnp.where(i < nt - 1, dyh_ref[:, cs] * _silu(gh_ref[:, cs]) * sc_ref[:, cs], 0.0).astype(BF16)
            dpool = _nt(dz, w)
            dpool_h = _nt(dzh, w)
            cnt = jnp.minimum(win, row + 1).astype(F32)
            dn_ref[0:t, cs] = dpool / cnt[0:t]
            dn_ref[t:, cs] = dpool_h / cnt[t:]
            acc = dn_ref[0:t, cs]
            for j in range(1, win):
                acc = acc + dn_ref[j:j + t, cs]
            du_ref[:, cs] = (acc - dpool).astype(BF16)
        host.after(i, nt)

    outs = pl.pallas_call(
        body, name=name, grid=(nt,), in_specs=host.in_specs, out_specs=host.out_specs, out_shape=host.out_shape,
        scratch_shapes=host.scratch, input_output_aliases=host.aliases,
        compiler_params=_cp("arbitrary"))(p, p, p, p, att, dy, dy, pool_w, pool_scale, *host.args)
    return host.results(outs)


def _mm_out_even(y, w, x, g_post, g_pre_next, name):
    s, k = y.shape
    d = w.shape[1]
    t = ROW_TILE

    def body(y_ref, w_ref, x_ref, gp_ref, gn_ref, o_ref, x1_ref, h1_ref):
        for r0 in range(0, t, t // 2):
            rows = slice(r0, r0 + t // 2)
            o = jnp.dot(y_ref[rows, :], w_ref[...], preferred_element_type=F32)
            o_ref[rows, :] = o
            ohat, _ = _rms_stats(o)
            x1 = x_ref[rows, :] + ohat * gp_ref[...]
            x1_ref[rows, :] = x1
            xhat, _ = _rms_stats(x1)
            h1_ref[rows, :] = (xhat * gn_ref[...]).astype(BF16)

    row = lambda c: pl.BlockSpec((t, c), lambda i: (i, 0))
    vec = pl.BlockSpec((1, d), lambda i: (0, 0))
    return pl.pallas_call(
        body, name=name, grid=(s // t,),
        in_specs=[row(k), pl.BlockSpec((k, d), lambda i: (0, 0)), row(d), vec, vec],
        out_specs=[row(d), row(d), row(d)],
        out_shape=[jax.ShapeDtypeStruct((s, d), F32), jax.ShapeDtypeStruct((s, d), F32),
                   jax.ShapeDtypeStruct((s, d), BF16)],
        compiler_params=_cp("parallel"))(y, w, x, g_post, g_pre_next)


def _mm_out_odd(y, w, x1, g_post, target, name):
    s, k = y.shape
    d = w.shape[1]
    t = ROW_TILE

    def body(y_ref, w_ref, x_ref, gp_ref, tg_ref, do_ref, dx_ref, loss_ref, dgp_ref):
        first = pl.program_id(0) == 0
        gp = gp_ref[...]
        part = dgp = None
        for r0 in range(0, t, t // 2):
            rows = slice(r0, r0 + t // 2)
            o = jnp.dot(y_ref[rows, :], w_ref[...], preferred_element_type=F32)
            ohat, r = _rms_stats(o)
            diff = x_ref[rows, :] + ohat * gp - tg_ref[rows, :]
            part_half = 0.5 * jnp.sum(jnp.mean(diff * diff, axis=-1, keepdims=True), axis=0, keepdims=True)
            dx2 = diff * (1.0 / d)
            dx_ref[rows, :] = dx2
            do, dgp_half = _rms_bwd(dx2, ohat, r, gp)
            do_ref[rows, :] = do.astype(BF16)
            part = part_half if part is None else part + part_half
            dgp = dgp_half if dgp is None else dgp + dgp_half
        _acc_rows(loss_ref, first, jnp.broadcast_to(part, loss_ref.shape))
        _acc_rows(dgp_ref, first, dgp)

    row = lambda c: pl.BlockSpec((t, c), lambda i: (i, 0))
    vec = pl.BlockSpec((1, d), lambda i: (0, 0))
    return pl.pallas_call(
        body, name=name, grid=(s // t,),
        in_specs=[row(k), pl.BlockSpec((k, d), lambda i: (0, 0)), row(d), vec, row(d)],
        out_specs=[row(d), row(d), pl.BlockSpec((8, LANES), lambda i: (0, 0)), vec],
        out_shape=[jax.ShapeDtypeStruct((s, d), BF16), jax.ShapeDtypeStruct((s, d), F32),
                   jax.ShapeDtypeStruct((8, LANES), F32), jax.ShapeDtypeStruct((1, d), F32)],
        compiler_params=_cp("arbitrary"))(y, w, x1, g_post, target)


def _layer_norm(d1, cg, cb):
    mu = jnp.mean(d1, axis=-1, keepdims=True)
    cen = d1 - mu
    rstd = lax.rsqrt(jnp.mean(cen * cen, axis=-1, keepdims=True) + EPS)
    n = cen * rstd
    return n, rstd, n * cg + cb


SUBLANES = 8
ROW_STRIP = 64
GATHER_PIECES = 8
CONV_ROWS = 64


def _make_shifts(pad_ref, cs, sh_ref):
    rows = sh_ref.shape[1]
    for r in range(1, SUBLANES):
        sh_ref[r - 1] = pad_ref[r:r + rows, cs]


def _by_shift(taps, base, sign=1):
    return sorted(range(taps), key=lambda k: ((sign * (base + k)) % SUBLANES, k))


def _window(pad_ref, cs, sh_ref, off, t):
    m, r = divmod(off, SUBLANES)
    if r == 0:
        return pad_ref[SUBLANES * m:SUBLANES * m + t, cs]
    return sh_ref[r - 1, SUBLANES * m:SUBLANES * m + t, :]


def _odd_mix_fwd(p, sconv_w, dconv_w, dconv_b, cnorm_g, cnorm_b, d, name):
    s = p.shape[0]
    w = d // 2
    k3, k31 = sconv_w.shape[0], dconv_w.shape[0]
    t, hb = ROW_TILE, CONV_HALO
    assert hb >= k31 - 1 and w % LANES == 0

    def body(p_ref, ph_ref, w3_ref, w31_ref, b31_ref, cg_ref, cb_ref, y_ref, s3_ref, d1_ref, mpad, dpad, sh_ref):
        i = pl.program_id(0)
        mpad[0:hb, :] = jnp.where(i > 0, ph_ref[:, 2 * w:3 * w] * ph_ref[:, 0:w], 0.0)
        mpad[hb:, :] = p_ref[:, 2 * w:3 * w] * p_ref[:, 0:w]
        dpad[0:hb, :] = jnp.where(i > 0, ph_ref[:, 3 * w:4 * w] * _sigmoid(ph_ref[:, 4 * w:5 * w]), 0.0)
        dpad[hb:, :] = p_ref[:, 3 * w:4 * w] * _sigmoid(p_ref[:, 4 * w:5 * w])
        for c0 in range(0, w, LANES):
            cs = slice(c0, c0 + LANES)
            acc = jnp.zeros((t, LANES), F32)
            for kk in range(k3):
                acc = acc + w3_ref[kk:kk + 1, cs] * mpad[hb - (k3 - 1) + kk:hb - (k3 - 1) + kk + t, cs]
            s3_ref[:, cs] = acc
            _make_shifts(dpad, cs, sh_ref)
            for r0 in range(0, t, CONV_ROWS):
                acc = jnp.zeros((CONV_ROWS, LANES), F32)
                for kk in _by_shift(k31, hb - (k31 - 1)):
                    acc = acc + w31_ref[kk:kk + 1, cs] * _window(dpad, cs, sh_ref, hb - (k31 - 1) + kk + r0, CONV_ROWS)
                d1_ref[r0:r0 + CONV_ROWS, cs] = acc + b31_ref[:, cs]
        _, _, d2 = _layer_norm(d1_ref[...], cg_ref[...], cb_ref[...])
        y_ref[:, :w] = (p_ref[:, w:2 * w] * s3_ref[...] * _silu(p_ref[:, 5 * w:6 * w])).astype(BF16)
        y_ref[:, w:] = (_silu(d2) * _silu(p_ref[:, 6 * w:7 * w])).astype(BF16)

    row = lambda c: pl.BlockSpec((t, c), lambda i: (i, 0))
    full = lambda a: pl.BlockSpec(a.shape, lambda i: (0, 0))
    return pl.pallas_call(
        body, name=name, grid=(s // t,),
        in_specs=[row(7 * w),
                  pl.BlockSpec((hb, 5 * w), lambda i: (jnp.maximum(i * (t // hb) - 1, 0), 0)),
                  full(sconv_w), full(dconv_w), full(dconv_b), full(cnorm_g), full(cnorm_b)],
        out_specs=[row(d), row(w), row(w)],
        out_shape=[jax.ShapeDtypeStruct((s, d), BF16), jax.ShapeDtypeStruct((s, w), F32),
                   jax.ShapeDtypeStruct((s, w), F32)],
        scratch_shapes=[pltpu.VMEM((hb + t, w), F32)] * 2 + [pltpu.VMEM((SUBLANES - 1, hb + t - SUBLANES, LANES), F32)],
        compiler_params=_cp("parallel"))(p, p, sconv_w, dconv_w, dconv_b, cnorm_g, cnorm_b)


def _odd_bwd_rows(p, s3, d1, dy, cnorm_g, cnorm_b, d, name, comm=None):
    s = p.shape[0]
    w = d // 2
    t = ROW_TILE
    col = lambda j: pl.BlockSpec((t, w), lambda i: (i, j))
    row = lambda c: pl.BlockSpec((t, c), lambda i: (i, 0))
    vec = pl.BlockSpec((1, w), lambda i: (0, 0))
    host = _Host(comm, [col(1), col(5), col(6), row(w), row(w), row(d), vec, vec],
                 [row(w), row(d), row(w), row(w), vec, vec, vec],
                 [jax.ShapeDtypeStruct((s, w), BF16), jax.ShapeDtypeStruct((s, d), BF16),
                  jax.ShapeDtypeStruct((s, w), F32), jax.ShapeDtypeStruct((s, w), F32)] + [jax.ShapeDtypeStruct((1, w), F32)] * 3, [])

    def body(*refs):
        ((bc_ref, g1_ref, g2_ref, s3_ref, d1_ref, dy_ref, cg_ref, cb_ref),
         (dbc_ref, dg_ref, ds3_ref, dd1_ref, dcg_ref, dcb_ref, db_ref), _) = host.split(refs)
        step = pl.program_id(0)
        host.before(step, s // t)
        first = step == 0

        def strip(j, sums):
            rows = slice(j * ROW_STRIP, (j + 1) * ROW_STRIP)
            g1, g2 = g1_ref[rows, :], g2_ref[rows, :]
            bc, s3v = bc_ref[rows, :], s3_ref[rows, :]
            dy1, dy2 = dy_ref[rows, :w], dy_ref[rows, w:]
            n, rstd, d2 = _layer_norm(d1_ref[rows, :], cg_ref[...], cb_ref[...])
            dg_ref[rows, :w] = (dy1 * bc * s3v * _dsilu(g1)).astype(BF16)
            dg_ref[rows, w:] = (dy2 * _silu(d2) * _dsilu(g2)).astype(BF16)
            dco = dy1 * _silu(g1)
            dbc_ref[rows, :] = (dco * s3v).astype(BF16)
            ds3_ref[rows, :] = dco * bc
            dd2 = dy2 * _silu(g2) * _dsilu(d2)
            dn = dd2 * cg_ref[...]
            dd1 = rstd * (dn - jnp.mean(dn, axis=-1, keepdims=True) - n * jnp.mean(dn * n, axis=-1, keepdims=True))
            dd1_ref[rows, :] = dd1
            dcb, dcg, db = sums
            return (dcb + jnp.sum(dd2, axis=0, keepdims=True), dcg + jnp.sum(dd2 * n, axis=0, keepdims=True),
                    db + jnp.sum(dd1, axis=0, keepdims=True))

        zero = jnp.zeros((1, w), F32)
        sums = (zero, zero, zero)
        for j in range(t // ROW_STRIP):
            sums = strip(j, sums)
        dcb, dcg, db = sums
        _acc_rows(dcb_ref, first, dcb)
        _acc_rows(dcg_ref, first, dcg)
        _acc_rows(db_ref, first, db)
        host.after(step, s // t)

    outs = pl.pallas_call(
        body, name=name, grid=(s // t,), in_specs=host.in_specs, out_specs=host.out_specs, out_shape=host.out_shape,
        scratch_shapes=host.scratch, input_output_aliases=host.aliases,
        compiler_params=_cp("arbitrary"))(p, p, p, s3, d1, dy, cnorm_g, cnorm_b, *host.args)
    return host.results(outs)


def _odd_bwd_conv(p, ds3, dd1, sconv_w, dconv_w, d, name):
    s = p.shape[0]
    w = d // 2
    k3, k31 = sconv_w.shape[0], dconv_w.shape[0]
    t, hb, ha = ROW_TILE, CONV_HALO, 8
    nt = s // t
    assert hb >= k31 - 1 and ha >= k3 - 1

    def body(hc_ref, cc_ref, ga_ref, gb_ref, hch_ref, cch_ref, gah_ref, gbh_ref, ds3_ref, ds3h_ref, dd1_ref, dd1h_ref,
             w3_ref, w31_ref, dhc_ref, dcc_ref, dga_ref, dgb_ref, dw3_ref, dw31_ref, mpad, dpad, s3pad, d1pad, sh_ref):
        i = pl.program_id(0)
        first = i == 0
        last = i == nt - 1
        mpad[0:hb, :] = jnp.where(i > 0, cch_ref[...] * hch_ref[...], 0.0)
        mpad[hb:, :] = cc_ref[...] * hc_ref[...]
        dpad[0:hb, :] = jnp.where(i > 0, gah_ref[...] * _sigmoid(gbh_ref[...]), 0.0)
        dpad[hb:, :] = ga_ref[...] * _sigmoid(gb_ref[...])
        s3pad[0:t, :] = ds3_ref[...]
        s3pad[t:, :] = jnp.where(last, 0.0, ds3h_ref[...])
        d1pad[0:t, :] = dd1_ref[...]
        d1pad[t:, :] = jnp.where(last, 0.0, dd1h_ref[...])

        @pl.when(first)
        def _():
            dw3_ref[...] = jnp.zeros_like(dw3_ref)
            dw31_ref[...] = jnp.zeros_like(dw31_ref)

        def fold(v):
            return jnp.sum(v.reshape(v.shape[0] // SUBLANES, SUBLANES, LANES), axis=0)

        groups = range(0, t, CONV_ROWS)
        for c0 in range(0, w, LANES):
            cs = slice(c0, c0 + LANES)
            ds3v = s3pad[0:t, cs]
            dm = jnp.zeros((t, LANES), F32)
            for kk in range(k3):
                dm = dm + w3_ref[kk:kk + 1, cs] * s3pad[k3 - 1 - kk:k3 - 1 - kk + t, cs]
                off = hb - (k3 - 1) + kk
                dw3_ref[SUBLANES * kk:SUBLANES * (kk + 1), cs] += fold(ds3v * mpad[off:off + t, cs])
            dcc_ref[:, cs] = (dm * hc_ref[:, cs]).astype(BF16)
            dhc_ref[:, cs] = (dm * cc_ref[:, cs]).astype(BF16)
            _make_shifts(d1pad, cs, sh_ref)
            for r0 in groups:
                rows = slice(r0, r0 + CONV_ROWS)
                dd0 = jnp.zeros((CONV_ROWS, LANES), F32)
                for kk in _by_shift(k31, -(k31 - 1), -1):
                    dd0 = dd0 + w31_ref[kk:kk + 1, cs] * _window(d1pad, cs, sh_ref, k31 - 1 - kk + r0, CONV_ROWS)
                sgb = _sigmoid(gb_ref[rows, cs])
                dga_ref[rows, cs] = (dd0 * sgb).astype(BF16)
                dgb_ref[rows, cs] = (dd0 * ga_ref[rows, cs] * sgb * (1.0 - sgb)).astype(BF16)
            _make_shifts(dpad, cs, sh_ref)
            for kk in _by_shift(k31, hb - (k31 - 1)):
                part = jnp.zeros((SUBLANES, LANES), F32)
                for r0 in groups:
                    part = part + fold(d1pad[r0:r0 + CONV_ROWS, cs]
                                       * _window(dpad, cs, sh_ref, hb - (k31 - 1) + kk + r0, CONV_ROWS))
                dw31_ref[SUBLANES * kk:SUBLANES * (kk + 1), cs] += part

    col = lambda j: pl.BlockSpec((t, w), lambda i: (i, j))
    pre = lambda j: pl.BlockSpec((hb, w), lambda i: (jnp.maximum(i * (t // hb) - 1, 0), j))
    row = pl.BlockSpec((t, w), lambda i: (i, 0))
    post = lambda h: pl.BlockSpec((h, w), lambda i: (jnp.minimum((i + 1) * (t // h), s // h - 1), 0))
    full = lambda a: pl.BlockSpec(a.shape, lambda i: (0, 0))
    dhc, dcc, dga, dgb, dw3, dw31 = pl.pallas_call(
        body, name=name, grid=(nt,),
        in_specs=[col(0), col(2), col(3), col(4), pre(0), pre(2), pre(3), pre(4),
                  row, post(ha), row, post(hb), full(sconv_w), full(dconv_w)],
        out_specs=[row, row, row, row, pl.BlockSpec((SUBLANES * k3, w), lambda i: (0, 0)),
                   pl.BlockSpec((SUBLANES * k31, w), lambda i: (0, 0))],
        out_shape=[jax.ShapeDtypeStruct((s, w), BF16)] * 4
        + [jax.ShapeDtypeStruct((SUBLANES * k3, w), F32), jax.ShapeDtypeStruct((SUBLANES * k31, w), F32)],
        scratch_shapes=[pltpu.VMEM((hb + t, w), F32)] * 2 + [pltpu.VMEM((t + ha, w), F32), pltpu.VMEM((t + hb, w), F32),
                                                             pltpu.VMEM((SUBLANES - 1, hb + t - SUBLANES, LANES), F32)],
        compiler_params=_cp("arbitrary"))(p, p, p, p, p, p, p, p, ds3, ds3, dd1, dd1, sconv_w, dconv_w)
    return dhc, dcc, dga, dgb, jnp.sum(dw3.reshape(k3, SUBLANES, w), axis=1), jnp.sum(dw31.reshape(k31, SUBLANES, w), axis=1)


def _mm_in_bwd(dp, w3, x, g_pre, dres, post, name, comm=None):
    s = dp.shape[0]
    nsh, d, ns = w3.shape
    t = 512 if s % 512 == 0 else ROW_TILE
    nt = s // t
    ks = 2 if (ns // 2) % LANES == 0 else 1
    nk, kw = nsh * ks, ns // ks
    chunk = 128
    nchunk = t // chunk
    row = pl.BlockSpec((t, d), lambda i, k: (i, 0))
    vec = pl.BlockSpec((1, d), lambda i, k: (0, 0))
    rowwise = [x, dres] + ([post[0]] if post is not None else [])
    in_specs = [pl.BlockSpec((t, kw), lambda i, k: (i, k)), pl.BlockSpec((None, d, kw), lambda i, k: (k // ks, 0, k % ks)), vec]
    out_specs = [row, vec]
    out_shape = [jax.ShapeDtypeStruct((s, d), F32), jax.ShapeDtypeStruct((1, d), F32)]
    args = [dp, w3, g_pre]
    if post is not None:
        in_specs += [vec]
        out_specs += [row, vec]
        out_shape += [jax.ShapeDtypeStruct((s, d), BF16), jax.ShapeDtypeStruct((1, d), F32)]
        args += [post[1]]
    n_blocked = len(in_specs)
    in_specs += [ANY] * len(rowwise)
    args += rowwise
    host = _Host(comm, in_specs, out_specs, out_shape,
                 [pltpu.VMEM((t, d), F32), pltpu.VMEM((len(rowwise), 2, chunk, d), F32), pltpu.SemaphoreType.DMA((len(rowwise), 2))])

    def body(*refs):
        ins, outs, (acc_ref, buf_ref, sem_ref) = host.split(refs)
        dp_ref, w_ref, g_ref = ins[:3]
        hbm = ins[n_blocked:]
        dx_ref, dg_ref = outs[:2]
        tile = pl.program_id(0)
        kk = pl.program_id(1)
        first = tile == 0
        step = tile * nk + kk
        host.before(step, nt * nk)
        part = _nt(dp_ref[...], w_ref[...])

        @pl.when(kk == 0)
        def _():
            acc_ref[...] = part

        @pl.when(kk > 0)
        def _():
            acc_ref[...] += part

        def fetch(ci, slot):
            return [pltpu.make_async_copy(src.at[pl.ds(tile * t + ci * chunk, chunk)], buf_ref.at[n, slot], sem_ref.at[n, slot])
                    for n, src in enumerate(hbm)]

        @pl.when(kk == nk - 1)
        def _():
            dg = dgp = None
            for cp in fetch(0, 0):
                cp.start()
            for ci in range(nchunk):
                slot = ci % 2
                if ci + 1 < nchunk:
                    for cp in fetch(ci + 1, 1 - slot):
                        cp.start()
                for cp in fetch(ci, slot):
                    cp.wait()
                rows = slice(ci * chunk, (ci + 1) * chunk)
                xhat, r = _rms_stats(buf_ref[0, slot])
                dxn, dg_part = _rms_bwd(acc_ref[rows, :], xhat, r, g_ref[...])
                dx = buf_ref[1, slot] + dxn
                dx_ref[rows, :] = dx
                dg = dg_part if dg is None else dg + dg_part
                if post is not None:
                    ohat, ro = _rms_stats(buf_ref[2, slot])
                    do, dgp_part = _rms_bwd(dx, ohat, ro, ins[3][...])
                    outs[2][rows, :] = do.astype(BF16)
                    dgp = dgp_part if dgp is None else dgp + dgp_part
            _acc_rows(dg_ref, first, dg)
            if post is not None:
                _acc_rows(outs[3], first, dgp)

        host.after(step, nt * nk)

    res = pl.pallas_call(
        body, name=name, grid=(nt, nk), in_specs=host.in_specs, out_specs=host.out_specs, out_shape=host.out_shape,
        scratch_shapes=host.scratch, input_output_aliases=host.aliases,
        compiler_params=_cp("arbitrary", "arbitrary"))(*args, *host.args)
    return host.results(res)


def _half_add(g, r1, c_arr, name):
    nsh, rows, ns = g.shape
    h = rows // 2
    tr = min(ROW_TILE, h)
    per = h // tr

    def body(c_ref, g_ref, r_ref, o_ref):
        o_ref[...] = (g_ref[...].astype(F32) + r_ref[...].astype(F32)).astype(BF16)

    spec = pl.BlockSpec((None, tr, ns), lambda s, r, c: (s, r, 0))
    return pl.pallas_call(
        body, name=name,
        grid_spec=pltpu.PrefetchScalarGridSpec(
            num_scalar_prefetch=1, grid=(nsh, per),
            in_specs=[pl.BlockSpec((None, tr, ns), lambda s, r, c: (s, c[0] * per + r, 0)), spec], out_specs=spec),
        out_shape=jax.ShapeDtypeStruct((nsh, h, ns), BF16), compiler_params=_cp("parallel", "parallel"))(c_arr, g, r1)


def _sum_chips(hh, r2, mc_arr, name, after=None):
    _, h, ns = hh.shape
    tr = min(ROW_TILE, h)
    per = h // tr

    def body(mc_ref, h_ref, a_ref, b_ref, c_ref, *rest):
        rest[-1][...] = ((h_ref[...].astype(F32) + a_ref[...].astype(F32)) + b_ref[...].astype(F32)) + c_ref[...].astype(F32)

    got = lambda k: pl.BlockSpec((None, tr, ns), lambda r, mc: (k, r, 0))
    ordering = [] if after is None else [after]
    return pl.pallas_call(
        body, name=name,
        grid_spec=pltpu.PrefetchScalarGridSpec(
            num_scalar_prefetch=1, grid=(per,),
            in_specs=[pl.BlockSpec((None, tr, ns), lambda r, mc: (mc[0], r, 0)), got(0), got(1), got(2)] + [ANY] * len(ordering),
            out_specs=pl.BlockSpec((tr, ns), lambda r, mc: (mc[1] * per + r, 0))),
        out_shape=jax.ShapeDtypeStruct((2 * h, ns), F32), compiler_params=_cp("parallel"))(mc_arr, hh, r2, r2, r2, *ordering)


def _add2(a, b, name):
    def body(a_ref, b_ref, o_ref):
        o_ref[...] = a_ref[...] + b_ref[...]

    return pl.pallas_call(body, name=name, out_shape=jax.ShapeDtypeStruct(a.shape, a.dtype), compiler_params=_cp())(a, b)


def _sum_chips_ordered(s2, r2, mc_arr, name):
    rows, w = s2.shape
    rh = rows // 2

    def body(mc_ref, s_ref, a_ref, b_ref, c_ref, o_ref):
        me = mc_ref[0]
        acc = None
        for j in range(N_CHIPS):
            rel = jnp.bitwise_xor(me, j)
            v = jnp.where(rel == 0, s_ref[...], jnp.where(rel == 2, a_ref[...], jnp.where(rel == 1, b_ref[...], c_ref[...])))
            acc = v if acc is None else acc + v
        o_ref[...] = acc

    got = lambda k: pl.BlockSpec((None, rh, w), lambda i, mc: (k, 0, 0))
    return pl.pallas_call(
        body, name=name,
        grid_spec=pltpu.PrefetchScalarGridSpec(
            num_scalar_prefetch=1, grid=(1,),
            in_specs=[pl.BlockSpec((rh, w), lambda i, mc: (mc[1], 0)), got(0), got(1), got(2)],
            out_specs=pl.BlockSpec((rh, w), lambda i, mc: (mc[1], 0))),
        out_shape=jax.ShapeDtypeStruct((rows, w), F32), compiler_params=_cp("arbitrary"))(mc_arr, s2, r2, r2, r2)


def _adamw(w, g, m, v, name, comm=None):
    r, c = w.shape
    tr = ROW_TILE if r % ROW_TILE == 0 else r
    c1 = 1.0 / (1.0 - ADAM_B1 ** ADAM_STEP)
    c2 = 1.0 / (1.0 - ADAM_B2 ** ADAM_STEP)
    spec = pl.BlockSpec((tr, c), lambda i: (i, 0))
    host = _Host(comm, [spec] * 4, [spec] * 4, [jax.ShapeDtypeStruct((r, c), F32)] * 4, [])

    def body(*refs):
        (w_ref, g_ref, m_ref, v_ref), (go_ref, d_ref, nm_ref, nv_ref), _ = host.split(refs)
        step = pl.program_id(0)
        host.before(step, r // tr)
        gv = g_ref[...]
        go_ref[...] = gv
        nm = ADAM_B1 * m_ref[...] + (1.0 - ADAM_B1) * gv
        nv = ADAM_B2 * v_ref[...] + (1.0 - ADAM_B2) * (gv * gv)
        nm_ref[...] = nm
        nv_ref[...] = nv
        d_ref[...] = -ADAM_LR * ((nm * c1) / (jnp.sqrt(nv * c2) + ADAM_EPS) + ADAM_WD * w_ref[...])
        host.after(step, r // tr)

    outs = pl.pallas_call(
        body, name=name, grid=(r // tr,), in_specs=host.in_specs, out_specs=host.out_specs, out_shape=host.out_shape,
        scratch_shapes=host.scratch, input_output_aliases=host.aliases,
        compiler_params=_cp("arbitrary"))(w, g, m, v, *host.args)
    return host.results(outs)


def _gather_weights(bigs, pool_w, pack_w, pack_d, name):
    nb = len(bigs)
    smalls = [pool_w, pack_w, pack_d]
    q, cw, cd = pool_w.shape[1], pack_w.shape[1], pack_d.shape[1]
    pieces = [_GatherPlan(bigs, (j, j + 1, GATHER_PIECES)) for j in range(GATHER_PIECES)]
    for j, piece in enumerate(pieces):
        piece.base = 9 + j * piece.nsems

    def body(*refs):
        srcs, dsts = refs[:nb + 3], refs[nb + 3:2 * (nb + 3)]
        ssem, rsem, lsem = refs[2 * (nb + 3):]
        x, y, c, me, chips, sib = _place()

        def small_dst(n, chip):
            if n == 0:
                return dsts[nb].at[:, pl.ds(chip * q, q), :]
            return dsts[nb + n].at[:, pl.ds(chip * (cw if n == 1 else cd), cw if n == 1 else cd)]

        local = [pltpu.make_async_copy(srcs[nb + n], small_dst(n, me), lsem.at[n]) for n in range(3)]
        for cp in local:
            cp.start()
        sends = []
        for n in range(3):
            for k, chip in enumerate(chips):
                cp = _rcopy(srcs[nb + n], small_dst(n, me), ssem.at[3 * n + k], rsem.at[3 * n + k], (*chip, c))
                cp.start()
                sends.append(cp)
        big = (srcs[:nb], dsts[:nb], ssem, rsem)
        for stage in ("start", "relay", "relay_far", "finish"):
            for piece in pieces:
                getattr(piece, stage)(*big)
        for n in range(3):
            for k, chip in enumerate(chips):
                ref = small_dst(n, 2 * chip[0] + chip[1])
                _rcopy(ref, ref, ssem.at[3 * n + k], rsem.at[3 * n + k], (*chip, c)).wait_recv()
        for cp in sends:
            cp.wait_send()
        for cp in local:
            cp.wait()

    nsem = 9 + sum(piece.nsems for piece in pieces)
    out_shape = [jax.ShapeDtypeStruct(b.shape, b.dtype) for b in bigs]
    out_shape += [jax.ShapeDtypeStruct((pool_w.shape[0], N_CHIPS * q, pool_w.shape[2]), pool_w.dtype),
                  jax.ShapeDtypeStruct((pack_w.shape[0], N_CHIPS * cw), pack_w.dtype),
                  jax.ShapeDtypeStruct((pack_d.shape[0], N_CHIPS * cd), pack_d.dtype)]
    return pl.pallas_call(
        body, name=name, in_specs=[ANY] * (nb + 3), out_specs=[ANY] * (nb + 3), out_shape=out_shape,
        input_output_aliases={a: a for a in range(nb)},
        scratch_shapes=[pltpu.SemaphoreType.DMA((nsem,)), pltpu.SemaphoreType.DMA((nsem,)), pltpu.SemaphoreType.DMA((3,))],
        compiler_params=pltpu.CompilerParams(has_side_effects=True))(*bigs, *smalls)


def _swap_with_sibling(grads, wholes, name):
    n, nw = len(grads), len(wholes)
    halves = [g.shape[1] // 2 for g in grads]

    def body(*refs):
        srcs, dsts = refs[:n + nw], refs[n + nw:2 * (n + nw)]
        ssem, rsem = refs[2 * (n + nw):]
        x, y, c, me, chips, sib = _place()
        cps = [_rcopy(srcs[a].at[:, pl.ds((1 - c) * halves[a], halves[a]), :], dsts[a], ssem.at[a], rsem.at[a], sib)
               for a in range(n)]
        cps += [_rcopy(srcs[a], dsts[a], ssem.at[a], rsem.at[a], sib) for a in range(n, n + nw)]
        for cp in cps:
            cp.start()
        for cp in cps:
            cp.wait_recv()
        for cp in cps:
            cp.wait_send()

    out_shape = [jax.ShapeDtypeStruct((g.shape[0], h, g.shape[2]), g.dtype) for g, h in zip(grads, halves)]
    out_shape += [jax.ShapeDtypeStruct(w.shape, w.dtype) for w in wholes]
    return pl.pallas_call(
        body, name=name, in_specs=[ANY] * (n + nw), out_specs=[ANY] * (n + nw), out_shape=out_shape,
        scratch_shapes=[pltpu.SemaphoreType.DMA((n + nw,)), pltpu.SemaphoreType.DMA((n + nw,))],
        compiler_params=pltpu.CompilerParams(has_side_effects=True))(*grads, *wholes)


def _scatter_to_chips(halves_in, small, name):
    n = len(halves_in)
    rh = small.shape[0] // 2

    def body(*refs):
        srcs, dsts = refs[:n + 1], refs[n + 1:2 * (n + 1)]
        ssem, rsem = refs[2 * (n + 1):]
        x, y, c, me, chips, sib = _place()
        cps = []
        for a in range(n + 1):
            for k, chip in enumerate(chips):
                src = srcs[a].at[2 * chip[0] + chip[1]] if a < n else srcs[a].at[pl.ds(c * rh, rh)]
                cps.append(_rcopy(src, dsts[a].at[k], ssem.at[3 * a + k], rsem.at[3 * a + k], (*chip, c)))
        for cp in cps:
            cp.start()
        for cp in cps:
            cp.wait_recv()
        for cp in cps:
            cp.wait_send()

    out_shape = [jax.ShapeDtypeStruct((3,) + h.shape[1:], h.dtype) for h in halves_in]
    out_shape.append(jax.ShapeDtypeStruct((3, rh, small.shape[1]), small.dtype))
    return pl.pallas_call(
        body, name=name, in_specs=[ANY] * (n + 1), out_specs=[ANY] * (n + 1), out_shape=out_shape,
        scratch_shapes=[pltpu.SemaphoreType.DMA((3 * (n + 1),)), pltpu.SemaphoreType.DMA((3 * (n + 1),))],
        compiler_params=pltpu.CompilerParams(has_side_effects=True))(*halves_in, small)


def _join_halves(parts, name):
    n = len(parts)

    def body(*refs):
        srcs, dsts = refs[:n], refs[n:2 * n]
        ssem, rsem = refs[2 * n:]
        x, y, c, me, chips, sib = _place()
        cps = []
        for a in range(n):
            h = srcs[a].shape[0] // 2
            cps.append(_rcopy(srcs[a].at[pl.ds(c * h, h)], dsts[a].at[pl.ds(c * h, h)], ssem.at[a], rsem.at[a], sib))
        for cp in cps:
            cp.start()
        for a in range(n):
            h = srcs[a].shape[0] // 2
            theirs = dsts[a].at[pl.ds((1 - c) * h, h)]
            _rcopy(theirs, theirs, ssem.at[a], rsem.at[a], sib).wait_recv()
        for cp in cps:
            cp.wait_send()

    out_shape = [jax.ShapeDtypeStruct(p.shape, p.dtype) for p in parts]
    return pl.pallas_call(
        body, name=name, in_specs=[ANY] * n, out_specs=[ANY] * n, out_shape=out_shape,
        input_output_aliases={a: a for a in range(n)},
        scratch_shapes=[pltpu.SemaphoreType.DMA((n,)), pltpu.SemaphoreType.DMA((n,))],
        compiler_params=pltpu.CompilerParams(has_side_effects=True))(*parts)


def _scatter_start(h, name):
    land = (3,) + h.shape[1:]

    def body(h_ref, land_ref, send_sems, recv_sems, h_thru, land_thru, token):
        x, y, c, me, chips, sib = _place()
        for k, chip in enumerate(chips):
            _rcopy(h_ref.at[2 * chip[0] + chip[1]], land_ref.at[k], send_sems.at[k], recv_sems.at[k], (*chip, c)).start()
        token[...] = jnp.zeros_like(token)

    hbm = pl.BlockSpec(memory_space=pltpu.HBM)
    sem = pl.BlockSpec(memory_space=pltpu.SEMAPHORE)
    return pl.pallas_call(
        body, name=name,
        out_shape=(pltpu.SemaphoreType.DMA((3,)), pltpu.SemaphoreType.DMA((3,)), pltpu.HBM(h.shape, h.dtype),
                   pltpu.HBM(land, h.dtype), jax.ShapeDtypeStruct((8, LANES), F32)),
        in_specs=(hbm, hbm), out_specs=(sem, sem, hbm, hbm, pl.BlockSpec(memory_space=pltpu.VMEM)),
        input_output_aliases={0: 2, 1: 3},
        compiler_params=pltpu.CompilerParams(has_side_effects=pltpu.SideEffectType.DATAFLOW_SIDE_EFFECTING))(
            pltpu.with_memory_space_constraint(h, pltpu.HBM),
            pltpu.with_memory_space_constraint(lax.empty(land, h.dtype), pltpu.HBM))


def _scatter_wait(send_sems, recv_sems, h_thru, land_thru, after, name):
    def body(h_ref, land_ref, send_sems, recv_sems, after_ref, h_dead, got_ref):
        x, y, c, me, chips, sib = _place()
        for k, chip in enumerate(chips):
            cp = _rcopy(h_ref.at[2 * chip[0] + chip[1]], land_ref.at[k], send_sems.at[k], recv_sems.at[k], (*chip, c))
            cp.wait_send()
            cp.wait_recv()

    hbm = pl.BlockSpec(memory_space=pltpu.HBM)
    sem = pl.BlockSpec(memory_space=pltpu.SEMAPHORE)
    return pl.pallas_call(
        body, name=name,
        out_shape=(pltpu.HBM(h_thru.shape, h_thru.dtype), pltpu.HBM(land_thru.shape, land_thru.dtype)),
        in_specs=(hbm, hbm, sem, sem, ANY), out_specs=(hbm, hbm), input_output_aliases={0: 0, 1: 1},
        compiler_params=pltpu.CompilerParams(has_side_effects=pltpu.SideEffectType.DATAFLOW_SIDE_EFFECTING))(
            h_thru, land_thru, send_sems, recv_sems, after)


def _swap_start(g, name):
    h = g.shape[1] // 2
    land = (g.shape[0], h, g.shape[2])

    def body(g_ref, land_ref, send_sem, recv_sem, g_thru, land_thru, token):
        x, y, c, me, chips, sib = _place()
        _rcopy(g_ref.at[:, pl.ds((1 - c) * h, h), :], land_ref, send_sem.at[0], recv_sem.at[0], sib).start()
        token[...] = jnp.zeros_like(token)

    hbm = pl.BlockSpec(memory_space=pltpu.HBM)
    sem = pl.BlockSpec(memory_space=pltpu.SEMAPHORE)
    return pl.pallas_call(
        body, name=name,
        out_shape=(pltpu.SemaphoreType.DMA((1,)), pltpu.SemaphoreType.DMA((1,)), pltpu.HBM(g.shape, g.dtype),
                   pltpu.HBM(land, g.dtype), jax.ShapeDtypeStruct((8, LANES), F32)),
        in_specs=(hbm, hbm), out_specs=(sem, sem, hbm, hbm, pl.BlockSpec(memory_space=pltpu.VMEM)),
        input_output_aliases={0: 2, 1: 3},
        compiler_params=pltpu.CompilerParams(has_side_effects=pltpu.SideEffectType.DATAFLOW_SIDE_EFFECTING))(
            pltpu.with_memory_space_constraint(g, pltpu.HBM),
            pltpu.with_memory_space_constraint(lax.empty(land, g.dtype), pltpu.HBM))


def _swap_wait(send_sem, recv_sem, g_thru, land_thru, after, name):
    h = g_thru.shape[1] // 2

    def body(g_ref, land_ref, send_sem, recv_sem, after_ref, g_dead, got_ref):
        x, y, c, me, chips, sib = _place()
        cp = _rcopy(g_ref.at[:, pl.ds((1 - c) * h, h), :], land_ref, send_sem.at[0], recv_sem.at[0], sib)
        cp.wait_send()
        cp.wait_recv()

    hbm = pl.BlockSpec(memory_space=pltpu.HBM)
    sem = pl.BlockSpec(memory_space=pltpu.SEMAPHORE)
    return pl.pallas_call(
        body, name=name,
        out_shape=(pltpu.HBM(g_thru.shape, g_thru.dtype), pltpu.HBM(land_thru.shape, land_thru.dtype)),
        in_specs=(hbm, hbm, sem, sem, ANY), out_specs=(hbm, hbm), input_output_aliases={0: 0, 1: 1},
        compiler_params=pltpu.CompilerParams(has_side_effects=pltpu.SideEffectType.DATAFLOW_SIDE_EFFECTING))(
            g_thru, land_thru, send_sem, recv_sem, after)


def _join_start(parts, name):
    n = len(parts)

    def body(*refs):
        srcs, (send_sems, recv_sems), token = refs[:n], refs[n:n + 2], refs[-1]
        x, y, c, me, chips, sib = _place()
        for a, src in enumerate(srcs):
            h = src.shape[0] // 2
            mine = src.at[pl.ds(c * h, h)]
            _rcopy(mine, mine, send_sems.at[a], recv_sems.at[a], sib).start()
        token[...] = jnp.zeros_like(token)

    hbm = pl.BlockSpec(memory_space=pltpu.HBM)
    sem = pl.BlockSpec(memory_space=pltpu.SEMAPHORE)
    outs = pl.pallas_call(
        body, name=name,
        out_shape=(pltpu.SemaphoreType.DMA((n,)), pltpu.SemaphoreType.DMA((n,)))
        + tuple(pltpu.HBM(p.shape, p.dtype) for p in parts) + (jax.ShapeDtypeStruct((8, LANES), F32),),
        in_specs=(hbm,) * n, out_specs=(sem, sem) + (hbm,) * n + (pl.BlockSpec(memory_space=pltpu.VMEM),),
        input_output_aliases={a: 2 + a for a in range(n)},
        compiler_params=pltpu.CompilerParams(has_side_effects=pltpu.SideEffectType.DATAFLOW_SIDE_EFFECTING))(
            *[pltpu.with_memory_space_constraint(p, pltpu.HBM) for p in parts])
    return outs[0], outs[1], list(outs[2:2 + n]), outs[-1]


def _join_wait(send_sems, recv_sems, parts, after, name):
    n = len(parts)

    def body(*refs):
        srcs, (send_sems, recv_sems) = refs[:n], refs[n:n + 2]
        x, y, c, me, chips, sib = _place()
        for a, src in enumerate(srcs):
            h = src.shape[0] // 2
            mine, theirs = src.at[pl.ds(c * h, h)], src.at[pl.ds((1 - c) * h, h)]
            _rcopy(mine, theirs, send_sems.at[a], recv_sems.at[a], sib).wait_send()
            _rcopy(theirs, theirs, send_sems.at[a], recv_sems.at[a], sib).wait_recv()

    hbm = pl.BlockSpec(memory_space=pltpu.HBM)
    sem = pl.BlockSpec(memory_space=pltpu.SEMAPHORE)
    return pl.pallas_call(
        body, name=name, out_shape=tuple(pltpu.HBM(p.shape, p.dtype) for p in parts),
        in_specs=(hbm,) * n + (sem, sem, ANY), out_specs=(hbm,) * n, input_output_aliases={a: a for a in range(n)},
        compiler_params=pltpu.CompilerParams(has_side_effects=pltpu.SideEffectType.DATAFLOW_SIDE_EFFECTING))(
            *parts, send_sems, recv_sems, after)


def _pad_rows(a, rows):
    return jnp.pad(a, ((0, rows - a.shape[0]), (0, 0)))


def _stack_rows(parts, multiple):
    padded = [_pad_rows(p, -(-p.shape[0] // 8) * 8) for p in parts]
    starts, at = [], 0
    for p in padded:
        starts.append(at)
        at += p.shape[0]
    total = -(-at // multiple) * multiple
    if total > at:
        padded.append(jnp.zeros((total - at, parts[0].shape[1]), parts[0].dtype))
    return jnp.concatenate(padded, axis=0), starts


def kernel(x, ln_pre_even, w_in_even, pool_w, pool_scale, w_out_even, ln_post_even, ln_pre_odd, w_in_odd, sconv_w, dconv_w, dconv_b, cnorm_g, cnorm_b, w_out_odd, ln_post_odd, loss_target, m_ln_pre_even, m_w_in_even, m_pool_w, m_pool_scale, m_w_out_even, m_ln_post_even, m_ln_pre_odd, m_w_in_odd, m_sconv_w, m_dconv_w, m_dconv_b, m_cnorm_g, m_cnorm_b, m_w_out_odd, m_ln_post_odd, v_ln_pre_even, v_w_in_even, v_pool_w, v_pool_scale, v_w_out_even, v_ln_post_even, v_ln_pre_odd, v_w_in_odd, v_sconv_w, v_dconv_w, v_dconv_b, v_cnorm_g, v_cnorm_b, v_w_out_odd, v_ln_post_odd):
    _, s, d = x.shape
    half = d // 2
    cw = half // N_CHIPS
    ng, q, gd = pool_w.shape[1:]
    k3, k31 = sconv_w.shape[1], dconv_w.shape[1]
    x2d, tgt = x[0], loss_target[0]
    me = 2 * lax.axis_index("x") + lax.axis_index("y")
    core = lax.axis_index("c")
    c_arr = jnp.reshape(core, (1,)).astype(jnp.int32)
    me_arr = jnp.reshape(me, (1,)).astype(jnp.int32)
    mc_arr = jnp.stack([me, core]).astype(jnp.int32)

    shards = [w_in_even[0], w_out_even[0], w_in_odd[0], w_out_odd[0]]
    slabs = [_cast_bf16_own_slab(w, me_arr, f"cast_w{n}") for n, w in enumerate(shards)]
    pool_w_b = _cast_bf16(pool_w[0].reshape(ng * q, gd), "cast_pool_w").reshape(ng, q, gd)
    pack_w, at_w = _stack_rows([sconv_w[0], dconv_w[0], dconv_b, cnorm_g, cnorm_b], 8)
    pack_d, at_d = _stack_rows([ln_pre_odd, ln_post_odd], 8)
    win_e, pool_w_f, pack_w_f, pack_d_f = _gather_weights(slabs[:1], pool_w_b, pack_w, pack_d, "gather_first")
    sconv_f = pack_w_f[at_w[0]:at_w[0] + k3]
    dconv_f = pack_w_f[at_w[1]:at_w[1] + k31]
    dconv_b_f, cnorm_g_f, cnorm_b_f = (pack_w_f[at_w[n]:at_w[n] + 1] for n in (2, 3, 4))
    ln_pre_odd_f = pack_d_f[at_d[0]:at_d[0] + 1]
    ln_post_odd_f = pack_d_f[at_d[1]:at_d[1] + 1]

    h0 = _rms_fwd(x2d, ln_pre_even, "rms_pre_even")
    plans = _Multi([_GatherPlan([slabs[1]], at=(0.6, 0.88)), _GatherPlan([slabs[2]], (0, 1, 4), at=(0.6, 0.88))])
    p_e, extra = _mm_nn(h0, win_e, "proj_in_even", plans)
    (wout_e,), (win_o,) = plans.results(extra)
    wout_e = wout_e.reshape(d, d)
    att, ltot, (win_o,) = _sba_fwd(p_e, half, "sba_fwd", _GatherPlan([win_o], (1, 4, 4), at=(0.69, 0.94)))
    y_e = _even_mix_fwd(p_e, att, pool_w_f, pool_scale, d, "even_mix_fwd")
    o_e, x1, h1 = _mm_out_even(y_e, wout_e, x2d, ln_post_even, ln_pre_odd_f, "proj_out_even")
    p_o, (wout_o,) = _mm_nn(h1, win_o, "proj_in_odd", _GatherPlan([slabs[3]]))
    wout_o = wout_o.reshape(d, d)
    y_o, s3, d1 = _odd_mix_fwd(p_o, sconv_f, dconv_f, dconv_b_f, cnorm_g_f, cnorm_b_f, d, "odd_mix_fwd")
    do_o, dx2, loss_blk, dln_post_odd = _mm_out_odd(y_o, wout_o, x1, ln_post_odd_f, tgt, "proj_out_odd_loss")

    dy_o = _mm_nt(do_o, wout_o, "dy_odd")
    g_wout_o = _mm_tn(y_o, do_o, 1, "dw_out_odd")[0].reshape(N_CHIPS, d // N_CHIPS, d)
    (dbc, dgate_o, ds3, dd1, dcnorm_g, dcnorm_b, ddconv_b), (got,) = _odd_bwd_rows(
        p_o, s3, d1, dy_o, cnorm_g_f, cnorm_b_f, d, "odd_bwd_rows", _SwapPlan([g_wout_o]))
    h_wout_o = _half_add(g_wout_o, got, c_arr, "half_add_out_odd")
    dhc, dcc, dga, dgb, dsconv, ddconv = _odd_bwd_conv(p_o, ds3, dd1, sconv_f, dconv_f, d, "odd_bwd_conv")
    dp_o = jnp.concatenate([dhc, dbc, dcc, dga, dgb, dgate_o], axis=1)
    g_win_o, (s_wout_o,) = _mm_tn(h1, dp_o, N_CHIPS, "dw_in_odd", _ScatterPlan([h_wout_o]))
    (dx1, dln_pre_odd, do_e, dln_post_even), (got,) = _mm_in_bwd(
        dp_o, win_o, x1, ln_pre_odd_f, dx2, (o_e, ln_post_even), "dx_odd", _SwapPlan([g_win_o]))
    h_win_o = _half_add(g_win_o, got, c_arr, "half_add_in_odd")

    dy_e = _mm_nt(do_e, wout_e, "dy_even")
    g_wout_e = _mm_tn(y_e, do_e, 1, "dw_out_even")[0].reshape(N_CHIPS, d // N_CHIPS, d)
    (datt, du, dgate_e, dpool_scale, dpool_w), (got,) = _even_mix_bwd(
        p_e, att, dy_e, pool_w_f, pool_scale, d, "even_mix_bwd", _SwapPlan([g_wout_e]))
    h_wout_e = _half_add(g_wout_e, got, c_arr, "half_add_out_even")
    two = lambda v: v.reshape(2, half)
    small_parts = [dpool_scale, two(dln_post_even), two(dln_pre_odd), two(dln_post_odd),
                   dsconv, ddconv, ddconv_b, dcnorm_g, dcnorm_b, dpool_w.reshape(gd, half)]
    small, at_s = _stack_rows(small_parts, 16)
    plans = _Multi([_ScatterPlan([h_win_o]), _SendWholePlan([small])])
    dq, dk, dv, extra = _sba_bwd(p_e, ltot, datt, half, "sba_bwd", plans)
    (s_win_o,), (small1,) = plans.results(extra)
    small2 = _add2(small, small1, "small_add")
    dp_e = jnp.concatenate([dq, dk, dv, du, dgate_e], axis=1)
    plans = _Multi([_ScatterPlan([h_wout_e]), _ShareHalfPlan([small2])])
    g_win_e, extra = _mm_tn(h0, dp_e, N_CHIPS, "dw_in_even", plans)
    (s_wout_e,), (small_got,) = plans.results(extra)
    swap = _swap_start(g_win_e, "swap_in_even_start")
    pairs = [(h_wout_e, s_wout_e), (h_win_o, s_win_o), (h_wout_o, s_wout_o)]
    parts = []
    for n, (h, r) in enumerate(pairs):
        parts.append(_sum_chips(h, r, mc_arr, f"sum_chips{n + 1}", after=parts[-1] if parts else swap[4]))
    g_win_e, got = _swap_wait(*swap[:4], parts[-1], "swap_in_even_wait")
    parts.append(_sum_chips_ordered(small2, small_got, mc_arr, "small_sum"))
    join_sems = _join_start(parts, "join_first_start")
    h_win_e = _half_add(g_win_e, got, c_arr, "half_add_in_even")
    send_sems, recv_sems, h_win_e, landing, token = _scatter_start(h_win_e, "scatter_in_even_start")
    (grad_x, dln_pre_even), _ = _mm_in_bwd(dp_e, win_e, x2d, ln_pre_even + token[0:1, 0:1], dx1, None, "dx_even")

    last, at_l = _stack_rows([two(dln_pre_even), jnp.pad(loss_blk[0:1], ((0, 0), (0, half - LANES)))], 16)
    (last1,) = _swap_with_sibling([], [last], "swap_last")
    last2 = _add2(last, last1, "last_add")
    (last_got,) = _scatter_to_chips([], last2, "scatter_last")
    last_sum = _sum_chips_ordered(last2, last_got, mc_arr, "last_sum")
    h_win_e, s_win_e = _scatter_wait(send_sems, recv_sems, h_win_e, landing, last_sum, "scatter_in_even_wait")
    last_sems = _join_start([_sum_chips(h_win_e, s_win_e, mc_arr, "sum_chips0"), last_sum], "join_last_start")
    gw_out_e, gw_in_o, gw_out_o, red = _join_wait(*join_sems[:3], last_sems[3], "join_first_wait")

    def rows(n, cnt):
        return red[at_s[n]:at_s[n] + cnt]

    def mine(a, width):
        return lax.dynamic_slice_in_dim(a, me * width, width, axis=1)

    quarter = d // N_CHIPS
    g_small = {
        "pool_scale": rows(0, 1),
        "ln_post_even": rows(1, 2).reshape(1, d),
        "ln_pre_odd": mine(rows(2, 2).reshape(1, d), quarter),
        "ln_post_odd": mine(rows(3, 2).reshape(1, d), quarter),
        "sconv_w": mine(rows(4, k3), cw),
        "dconv_w": mine(rows(5, k31), cw),
        "dconv_b": mine(rows(6, 1), cw),
        "cnorm_g": mine(rows(7, 1), cw),
        "cnorm_b": mine(rows(8, 1), cw),
        "pool_w": lax.dynamic_slice_in_dim(rows(9, gd).reshape(ng, gd, gd), me * q, q, axis=1).reshape(ng * q, gd),
    }
    w2d = {
        "ln_pre_even": ln_pre_even, "w_in_even": w_in_even[0], "pool_w": pool_w[0].reshape(ng * q, gd),
        "pool_scale": pool_scale, "w_out_even": w_out_even[0], "ln_post_even": ln_post_even, "ln_pre_odd": ln_pre_odd,
        "w_in_odd": w_in_odd[0], "sconv_w": sconv_w[0], "dconv_w": dconv_w[0], "dconv_b": dconv_b, "cnorm_g": cnorm_g,
        "cnorm_b": cnorm_b, "w_out_odd": w_out_odd[0], "ln_post_odd": ln_post_odd,
    }
    moments = {
        "ln_pre_even": (m_ln_pre_even, v_ln_pre_even), "w_in_even": (m_w_in_even, v_w_in_even),
        "pool_w": (m_pool_w, v_pool_w), "pool_scale": (m_pool_scale, v_pool_scale),
        "w_out_even": (m_w_out_even, v_w_out_even), "ln_post_even": (m_ln_post_even, v_ln_post_even),
        "ln_pre_odd": (m_ln_pre_odd, v_ln_pre_odd), "w_in_odd": (m_w_in_odd, v_w_in_odd),
        "sconv_w": (m_sconv_w, v_sconv_w), "dconv_w": (m_dconv_w, v_dconv_w), "dconv_b": (m_dconv_b, v_dconv_b),
        "cnorm_g": (m_cnorm_g, v_cnorm_g), "cnorm_b": (m_cnorm_b, v_cnorm_b),
        "w_out_odd": (m_w_out_odd, v_w_out_odd), "ln_post_odd": (m_ln_post_odd, v_ln_post_odd),
    }
    def update(name, g):
        m_in, v_in = moments[name]
        w = w2d[name]
        return _adamw(w, g, m_in.reshape(w.shape), v_in.reshape(w.shape), "adamw_" + name)[0]

    updates = {"w_in_odd": update("w_in_odd", gw_in_o)}
    gw_in_e, red_last = _join_wait(*last_sems[:3], updates["w_in_odd"][1], "join_last_wait")
    loss = red_last[at_l[1], 0]
    g_small["ln_pre_even"] = red_last[at_l[0]:at_l[0] + 2].reshape(1, d)
    for name, g in dict(g_small, w_in_even=gw_in_e, w_out_even=gw_out_e, w_out_odd=gw_out_o).items():
        updates[name] = update(name, g)
    outs = [[u.reshape(moments[name][0].shape) for u in updates[name]] for name in w2d]
    grads_out, deltas, new_m, new_v = zip(*outs)
    return (loss, grad_x.reshape(x.shape), *grads_out, *deltas, *new_m, *new_v)
```

```python
import functools
import math

import jax
import jax.numpy as jnp
from jax import lax
from jax.experimental import pallas as pl
from jax.experimental.pallas import tpu as pltpu

F32 = jnp.float32
BF16 = jnp.bfloat16
EPS = 1e-6
N_CHIPS = 4
VMEM_LIMIT_V7X = 56 << 20
HEAD_DIM = 128
ATT_BLOCK = 256
POOL_WINDOWS = (2, 4, 8, 16)
ROW_TILE = 256
POOL_HALO = 16
CONV_HALO = 32
LANES = 128
ADAM_LR, ADAM_B1, ADAM_B2, ADAM_EPS, ADAM_WD, ADAM_STEP = 0.001, 0.9, 0.999, 1e-08, 0.01, 10
MESH_ID = pl.DeviceIdType.MESH
ANY = pl.BlockSpec(memory_space=pl.ANY)


def _cp(*sem):
    return pltpu.CompilerParams(dimension_semantics=sem or None, vmem_limit_bytes=VMEM_LIMIT_V7X)


def _pick_tile(n, cap):
    best = None
    for t in range(LANES, min(n, cap) + 1, LANES):
        if n % t == 0:
            best = t
    assert best is not None, (n, cap)
    return best


def _sigmoid(x):
    return 1.0 / (1.0 + jnp.exp(-x))


def _silu(x):
    return x * _sigmoid(x)


def _dsilu(x):
    s = _sigmoid(x)
    return s * (1.0 + x * (1.0 - s))


def _log_sigmoid(z):
    return jnp.minimum(z, 0.0) - jnp.log(1.0 + jnp.exp(-jnp.abs(z)))


def _rms_stats(x):
    r = lax.rsqrt(jnp.mean(x * x, axis=-1, keepdims=True) + EPS)
    return x * r, r


def _rms_bwd(dh, xhat, r, g):
    dxh = dh * g
    dx = r * (dxh - xhat * jnp.mean(dxh * xhat, axis=-1, keepdims=True))
    return dx, jnp.sum(dh * xhat, axis=0, keepdims=True)


def _acc_rows(ref, first, val):
    @pl.when(first)
    def _():
        ref[...] = val

    @pl.when(jnp.logical_not(first))
    def _():
        ref[...] += val


def _rcopy(src, dst, ssem, rsem, dev):
    return pltpu.make_async_remote_copy(src_ref=src, dst_ref=dst, send_sem=ssem, recv_sem=rsem,
                                        device_id=dev, device_id_type=MESH_ID)


def _place():
    x, y, c = lax.axis_index("x"), lax.axis_index("y"), lax.axis_index("c")
    chips = [(1 - x, y), (x, 1 - y), (1 - x, 1 - y)]
    return x, y, c, 2 * x + y, chips, (x, y, 1 - c)


class _GatherPlan:
    PER_ARRAY = 7

    def __init__(self, arrays, part=(0, 1, 1), at=(0.5, 0.8)):
        self.operands = list(arrays)
        self.out_shapes = [jax.ShapeDtypeStruct(a.shape, a.dtype) for a in arrays]
        self.aliases = {i: i for i in range(len(arrays))}
        self.nsems = self.PER_ARRAY * len(arrays)
        self.base = 0
        self.halves = [a.shape[1] // 2 for a in arrays]
        self.part = part
        self.at = at

    def schedule(self):
        return [(0.0, self.start), (self.at[0], self.relay), (self.at[1], self.relay_far)]

    def _rows(self, ref, a, chip, half, quarter=None):
        lo, hi, n = self.part
        h = self.halves[a]
        first, size = half * h + lo * h // n, (hi - lo) * h // n
        if quarter is not None:
            first, size = first + quarter * (size // 2), size // 2
        return ref.at[chip, pl.ds(first, size)]

    def _copy(self, src, dst, a, n, ssem, rsem, dev):
        return _rcopy(src, dst, ssem.at[self.base + self.PER_ARRAY * a + n], rsem.at[self.base + self.PER_ARRAY * a + n], dev)

    def _own(self, ins, outs, ssem, rsem):
        x, y, c, me, chips, sib = _place()
        return [self._copy(self._rows(ins[a], a, me, c), self._rows(outs[a], a, me, c), a, k, ssem, rsem, (*chips[k], c))
                for a in range(len(ins)) for k in (0, 1)]

    def _relays(self, outs, ssem, rsem, a, k):
        x, y, c, me, chips, sib = _place()
        chip = 2 * chips[k][0] + chips[k][1]
        whole, quarter = self._rows(outs[a], a, chip, c), self._rows(outs[a], a, chip, c, k)
        return (self._copy(whole, whole, a, k, ssem, rsem, (*chips[k], c)),
                self._copy(quarter, quarter, a, 2 + k, ssem, rsem, (*chips[1 - k], c)),
                self._copy(whole, whole, a, 4 + k, ssem, rsem, sib))

    def _far(self, outs, ssem, rsem, a):
        x, y, c, me, chips, sib = _place()
        chip = 2 * chips[2][0] + chips[2][1]
        whole = self._rows(outs[a], a, chip, c)
        got = [self._copy(q, q, a, 2 + k, ssem, rsem, (*chips[1 - k], c))
               for k, q in enumerate([self._rows(outs[a], a, chip, c, 0), self._rows(outs[a], a, chip, c, 1)])]
        return got, self._copy(whole, whole, a, 6, ssem, rsem, sib)

    def start(self, ins, outs, ssem, rsem):
        for cp in self._own(ins, outs, ssem, rsem):
            cp.start()

    def relay(self, ins, outs, ssem, rsem):
        for a in range(len(outs)):
            for k in (0, 1):
                landed, onward, to_sibling = self._relays(outs, ssem, rsem, a, k)
                landed.wait_recv()
                onward.start()
                to_sibling.start()

    def relay_far(self, ins, outs, ssem, rsem):
        for a in range(len(outs)):
            got, to_sibling = self._far(outs, ssem, rsem, a)
            for cp in got:
                cp.wait_recv()
            to_sibling.start()

    def finish(self, ins, outs, ssem, rsem):
        x, y, c, me, chips, sib = _place()
        for a in range(len(outs)):
            for k in range(3):
                ref = self._rows(outs[a], a, 2 * chips[k][0] + chips[k][1], 1 - c)
                self._copy(ref, ref, a, 4 + k, ssem, rsem, sib).wait_recv()
        for cp in self._own(ins, outs, ssem, rsem):
            cp.wait_send()
        for a in range(len(outs)):
            for k in (0, 1):
                _, onward, to_sibling = self._relays(outs, ssem, rsem, a, k)
                onward.wait_send()
                to_sibling.wait_send()
            self._far(outs, ssem, rsem, a)[1].wait_send()


class _ScatterPlan:
    def __init__(self, arrays, part=(0, 1, 1), into=None):
        self.n = len(arrays)
        self.operands = list(arrays) + list(into or [])
        self.out_shapes = [jax.ShapeDtypeStruct((3,) + a.shape[1:], a.dtype) for a in arrays]
        self.aliases = {self.n + i: i for i in range(self.n)} if into else {}
        self.nsems = 3 * self.n
        self.base = 0
        self.part = part

    def _copies(self, ins, outs, ssem, rsem):
        x, y, c, me, chips, sib = _place()
        lo, hi, n = self.part
        out = []
        for a in range(self.n):
            h = ins[a].shape[1]
            rows = pl.ds(lo * h // n, (hi - lo) * h // n)
            for k, chip in enumerate(chips):
                out.append(_rcopy(ins[a].at[2 * chip[0] + chip[1], rows], outs[a].at[k, rows],
                                  ssem.at[self.base + 3 * a + k], rsem.at[self.base + 3 * a + k], (*chip, c)))
        return out

    def schedule(self):
        return [(0.0, self.start)]

    def start(self, ins, outs, ssem, rsem):
        for cp in self._copies(ins, outs, ssem, rsem):
            cp.start()

    def finish(self, ins, outs, ssem, rsem):
        cps = self._copies(ins, outs, ssem, rsem)
        for cp in cps:
            cp.wait_recv()
        for cp in cps:
            cp.wait_send()


class _ShareHalfPlan(_ScatterPlan):
    def __init__(self, arrays):
        super().__init__(arrays)
        self.out_shapes = [jax.ShapeDtypeStruct((3, a.shape[0] // 2, a.shape[1]), a.dtype) for a in arrays]

    def _copies(self, ins, outs, ssem, rsem):
        x, y, c, me, chips, sib = _place()
        out = []
        for a in range(self.n):
            rh = ins[a].shape[0] // 2
            for k, chip in enumerate(chips):
                out.append(_rcopy(ins[a].at[pl.ds(c * rh, rh)], outs[a].at[k],
                                  ssem.at[self.base + 3 * a + k], rsem.at[self.base + 3 * a + k], (*chip, c)))
        return out


class _SwapPlan:
    def __init__(self, grads):
        self.operands = list(grads)
        self.out_shapes = [jax.ShapeDtypeStruct((g.shape[0], g.shape[1] // 2, g.shape[2]), g.dtype) for g in grads]
        self.aliases = {}
        self.nsems = len(grads)
        self.base = 0

    def _copies(self, ins, outs, ssem, rsem):
        x, y, c, me, chips, sib = _place()
        out = []
        for a, src in enumerate(ins):
            h = src.shape[1] // 2
            out.append(_rcopy(src.at[:, pl.ds((1 - c) * h, h), :], outs[a], ssem.at[self.base + a], rsem.at[self.base + a], sib))
        return out

    def schedule(self):
        return [(0.0, self.start)]

    def start(self, ins, outs, ssem, rsem):
        for cp in self._copies(ins, outs, ssem, rsem):
            cp.start()

    def finish(self, ins, outs, ssem, rsem):
        cps = self._copies(ins, outs, ssem, rsem)
        for cp in cps:
            cp.wait_recv()
        for cp in cps:
            cp.wait_send()


class _SendWholePlan(_SwapPlan):
    def __init__(self, arrays):
        self.operands = list(arrays)
        self.out_shapes = [jax.ShapeDtypeStruct(a.shape, a.dtype) for a in arrays]
        self.aliases = {}
        self.nsems = len(arrays)
        self.base = 0

    def _copies(self, ins, outs, ssem, rsem):
        x, y, c, me, chips, sib = _place()
        return [_rcopy(src, outs[a], ssem.at[self.base + a], rsem.at[self.base + a], sib) for a, src in enumerate(ins)]


class _Multi:
    def __init__(self, plans):
        self.plans = plans
        self.operands, self.out_shapes, self.aliases, self.nsems = [], [], {}, 0
        self.spans = []
        for p in plans:
            ni, no = len(self.operands), len(self.out_shapes)
            self.spans.append((ni, ni + len(p.operands), no, no + len(p.out_shapes)))
            self.aliases.update({ni + i: no + j for i, j in p.aliases.items()})
            p.base = self.nsems
            self.nsems += p.nsems
            self.operands += p.operands
            self.out_shapes += p.out_shapes

    def schedule(self):
        def bound(fn, span):
            i0, i1, o0, o1 = span
            return lambda ins, outs, ssem, rsem: fn(ins[i0:i1], outs[o0:o1], ssem, rsem)

        stages = [(at, bound(fn, span)) for p, span in zip(self.plans, self.spans) for at, fn in p.schedule()]
        return sorted(stages, key=lambda s: s[0])

    def finish(self, ins, outs, ssem, rsem):
        for p, (i0, i1, o0, o1) in zip(self.plans, self.spans):
            p.finish(ins[i0:i1], outs[o0:o1], ssem, rsem)

    def results(self, extra):
        return [list(extra[o0:o1]) for (_, _, o0, o1) in self.spans]


class _Host:
    def __init__(self, comm, in_specs, out_specs, out_shape, scratch):
        self.comm = comm
        self.n_in, self.n_out = len(in_specs), len(out_specs)
        self.in_specs, self.out_specs, self.out_shape, self.scratch = list(in_specs), list(out_specs), list(out_shape), list(scratch)
        self.aliases = {}
        self.args = []
        if comm is not None:
            self.in_specs += [ANY] * len(comm.operands)
            self.out_specs += [ANY] * len(comm.out_shapes)
            self.out_shape += comm.out_shapes
            self.scratch += [pltpu.SemaphoreType.DMA((comm.nsems,)), pltpu.SemaphoreType.DMA((comm.nsems,))]
            self.aliases = {self.n_in + i: self.n_out + j for i, j in comm.aliases.items()}
            self.args = list(comm.operands)

    def split(self, refs):
        nc = len(self.args)
        nco = len(self.out_shape) - self.n_out
        ins, p = refs[:self.n_in], self.n_in + nc
        outs, rest = refs[p:p + self.n_out], refs[p + self.n_out + nco:]
        self._cargs = None
        if self.comm is not None:
            self._cargs = (refs[self.n_in:p], refs[p + self.n_out:p + self.n_out + nco], rest[-2], rest[-1])
            rest = rest[:-2]
        return ins, outs, rest

    def before(self, step, total):
        if self.comm is None:
            return

        for at, stage in self.comm.schedule():
            pl.when(step == min(total - 1, int(at * total)))(functools.partial(stage, *self._cargs))

    def after(self, step, total):
        if self.comm is None:
            return

        @pl.when(step == total - 1)
        def _():
            self.comm.finish(*self._cargs)

    def results(self, outs):
        return outs[:self.n_out], outs[self.n_out:]


def _cast_bf16(x, name):
    r, c = x.shape
    tr = ROW_TILE if r % ROW_TILE == 0 else r

    def body(x_ref, o_ref):
        o_ref[...] = x_ref[...].astype(BF16)

    return pl.pallas_call(
        body, name=name, grid=(r // tr,),
        in_specs=[pl.BlockSpec((tr, c), lambda i: (i, 0))],
        out_specs=pl.BlockSpec((tr, c), lambda i: (i, 0)),
        out_shape=jax.ShapeDtypeStruct((r, c), BF16), compiler_params=_cp("parallel"))(x)


def _cast_bf16_own_slab(x, me_arr, name):
    r, c = x.shape
    tr = ROW_TILE if r % ROW_TILE == 0 else r

    def body(me_ref, x_ref, o_ref):
        o_ref[...] = x_ref[...].astype(BF16)

    return pl.pallas_call(
        body, name=name,
        grid_spec=pltpu.PrefetchScalarGridSpec(
            num_scalar_prefetch=1, grid=(r // tr,),
            in_specs=[pl.BlockSpec((tr, c), lambda i, me: (i, 0))],
            out_specs=pl.BlockSpec((None, tr, c), lambda i, me: (me[0], i, 0))),
        out_shape=jax.ShapeDtypeStruct((N_CHIPS, r, c), BF16), compiler_params=_cp("parallel"))(me_arr, x)


def _rms_fwd(x, g, name):
    s, d = x.shape

    def body(x_ref, g_ref, h_ref):
        xhat, _ = _rms_stats(x_ref[...])
        h_ref[...] = (xhat * g_ref[...]).astype(BF16)

    return pl.pallas_call(
        body, name=name, grid=(s // ROW_TILE,),
        in_specs=[pl.BlockSpec((ROW_TILE, d), lambda i: (i, 0)), pl.BlockSpec((1, d), lambda i: (0, 0))],
        out_specs=pl.BlockSpec((ROW_TILE, d), lambda i: (i, 0)),
        out_shape=jax.ShapeDtypeStruct((s, d), BF16), compiler_params=_cp("parallel"))(x, g)


def _mm_nn(a, w3, name, comm=None):
    m, k = a.shape
    nsh, _, ns = w3.shape
    tm = 512 if m % 512 == 0 else ROW_TILE
    tn = _pick_tile(ns, 1024)
    per = ns // tn
    grid = (nsh * per, m // tm)
    host = _Host(comm,
                 [pl.BlockSpec((tm, k), lambda n, i: (i, 0)), pl.BlockSpec((None, k, tn), lambda n, i: (n // per, 0, n % per))],
                 [pl.BlockSpec((tm, tn), lambda n, i: (i, n))], [jax.ShapeDtypeStruct((m, nsh * ns), F32)], [])

    def body(*refs):
        (a_ref, w_ref), (o_ref,), _ = host.split(refs)
        step = pl.program_id(0) * grid[1] + pl.program_id(1)
        host.before(step, grid[0] * grid[1])
        o_ref[...] = jnp.dot(a_ref[...], w_ref[...], preferred_element_type=F32)
        host.after(step, grid[0] * grid[1])

    outs = pl.pallas_call(
        body, name=name, grid=grid, in_specs=host.in_specs, out_specs=host.out_specs, out_shape=host.out_shape,
        scratch_shapes=host.scratch, input_output_aliases=host.aliases,
        compiler_params=_cp("arbitrary", "arbitrary"))(a, w3, *host.args)
    (out,), extra = host.results(outs)
    return out, extra


def _mm_nt(a, b, name):
    m, k = a.shape
    n = b.shape[0]
    tm = 512 if m % 512 == 0 else ROW_TILE

    def body(a_ref, b_ref, o_ref):
        o_ref[...] = lax.dot_general(a_ref[...], b_ref[...], (((1,), (1,)), ((), ())), preferred_element_type=F32)

    return pl.pallas_call(
        body, name=name, grid=(m // tm,),
        in_specs=[pl.BlockSpec((tm, k), lambda i: (i, 0)), pl.BlockSpec((n, k), lambda i: (0, 0))],
        out_specs=pl.BlockSpec((tm, n), lambda i: (i, 0)),
        out_shape=jax.ShapeDtypeStruct((m, n), F32), compiler_params=_cp("parallel"))(a, b)


def _mm_tn(a, b, nsh, name, comm=None):
    s, m = a.shape
    n = b.shape[1]
    ns = n // nsh
    tm = 512 if m % 512 == 0 else ROW_TILE
    tn = _pick_tile(ns, 1024)
    per = ns // tn
    grid = (nsh * per, m // tm)
    host = _Host(comm, [pl.BlockSpec((s, tm), lambda j, i: (0, i)), pl.BlockSpec((s, tn), lambda j, i: (0, j))],
                 [pl.BlockSpec((None, tm, tn), lambda j, i: (j // per, i, j % per))],
                 [jax.ShapeDtypeStruct((nsh, m, ns), BF16)], [])

    def body(*refs):
        (a_ref, b_ref), (o_ref,), _ = host.split(refs)
        step = pl.program_id(0) * grid[1] + pl.program_id(1)
        host.before(step, grid[0] * grid[1])
        o_ref[...] = lax.dot_general(a_ref[...], b_ref[...], (((0,), (0,)), ((), ())),
                                     preferred_element_type=F32).astype(BF16)
        host.after(step, grid[0] * grid[1])

    outs = pl.pallas_call(
        body, name=name, grid=grid, in_specs=host.in_specs, out_specs=host.out_specs, out_shape=host.out_shape,
        scratch_shapes=host.scratch, input_output_aliases=host.aliases,
        compiler_params=_cp("arbitrary", "arbitrary"))(a, b, *host.args)
    (out,), extra = host.results(outs)
    return out, extra


def _tri(n, rel):
    row = lax.broadcasted_iota(jnp.int32, (2 * n, n), 0)
    col = lax.broadcasted_iota(jnp.int32, (2 * n, n), 1)
    return jnp.where(rel(jnp.where(row >= n, row - n, row), col), 1.0, 0.0).astype(BF16)


def _dot_split(x, tri2):
    hi = x.astype(BF16)
    lo = (x - hi.astype(F32)).astype(BF16)
    return jnp.dot(jnp.concatenate([hi, lo], axis=1), tri2, preferred_element_type=F32)


def _nt(a, b):
    return lax.dot_general(a, b, (((1,), (1,)), ((), ())), preferred_element_type=F32)


def _tn(a, b):
    return lax.dot_general(a, b, (((0,), (0,)), ((), ())), preferred_element_type=F32)


def _heads_per_step(nh):
    return max(h for h in (1, 2, 4) if nh % h == 0)


def _sba_fwd(p, sbw, name, comm=None):
    s = p.shape[0]
    nh = sbw // HEAD_DIM
    hp = _heads_per_step(nh)
    ngrp, hw = nh // hp, hp * HEAD_DIM
    blk = ATT_BLOCK
    nq = s // blk
    scale = 1.0 / math.sqrt(HEAD_DIM)
    host = _Host(comm,
                 [pl.BlockSpec((blk, hw), lambda g, i: (i, g)),
                  pl.BlockSpec((s, hw), lambda g, i: (0, ngrp + g)),
                  pl.BlockSpec((s, hw), lambda g, i: (0, 2 * ngrp + g))],
                 [pl.BlockSpec((blk, hw), lambda g, i: (i, g))] * 2,
                 [jax.ShapeDtypeStruct((s, sbw), F32)] * 2,
                 [pltpu.VMEM((s, hw), BF16)] * 2)

    def body(*refs):
        (q_ref, k_ref, v_ref), (o_ref, lt_ref), (kb_ref, vb_ref) = host.split(refs)
        i = pl.program_id(1)
        step = pl.program_id(0) * nq + i
        host.before(step, ngrp * nq)

        @pl.when(i == 0)
        def _():
            kb_ref[...] = k_ref[...].astype(BF16)
            vb_ref[...] = v_ref[...].astype(BF16)

        heads = [slice(h * HEAD_DIM, (h + 1) * HEAD_DIM) for h in range(hp)]
        qs = [q_ref[:, hd].astype(BF16) for hd in heads]
        later = _tri(blk, lambda r, c: r > c)
        causal = lax.broadcasted_iota(jnp.int32, (blk, blk), 1) < lax.broadcasted_iota(jnp.int32, (blk, blk), 0)

        def key_block(j, carry, diagonal):
            rows = pl.ds(pl.multiple_of(j * blk, blk), blk)
            hs = range(hp)
            z = [_nt(qs[h], kb_ref[rows, heads[h]]) * scale for h in hs]
            ls = [_log_sigmoid(z[h]) for h in hs]
            lm = [jnp.where(causal, ls[h] - z[h], 0.0) if diagonal else ls[h] - z[h] for h in hs]
            stay = [_dot_split(lm[h], later) for h in hs]
            w = [jnp.exp(ls[h] + stay[h] + carry[h][1]) for h in hs]
            if diagonal:
                w = [jnp.where(causal, w[h], 0.0) for h in hs]
            acc = [carry[h][0] + jnp.dot(w[h].astype(BF16), vb_ref[rows, heads[h]], preferred_element_type=F32) for h in hs]
            return tuple((acc[h], carry[h][1] + jnp.sum(lm[h], axis=1, keepdims=True)) for h in hs)

        init = tuple((jnp.zeros((blk, HEAD_DIM), F32), jnp.zeros((blk, 1), F32)) for _ in heads)
        carry = key_block(i, init, True)
        carry = lax.fori_loop(0, i, lambda n, c: key_block(i - 1 - n, c, False), carry)
        for h, hd in enumerate(heads):
            o_ref[:, hd] = carry[h][0]
            lt_ref[:, hd] = jnp.broadcast_to(carry[h][1], (blk, HEAD_DIM))
        host.after(step, ngrp * nq)

    outs = pl.pallas_call(
        body, name=name, grid=(ngrp, nq), in_specs=host.in_specs, out_specs=host.out_specs, out_shape=host.out_shape,
        scratch_shapes=host.scratch, input_output_aliases=host.aliases,
        compiler_params=_cp("arbitrary", "arbitrary"))(p, p, p, *host.args)
    (out, ltot), extra = host.results(outs)
    return out, ltot, extra


def _sba_bwd(p, ltot, dout, sbw, name, comm=None):
    s = p.shape[0]
    nh = sbw // HEAD_DIM
    hp = _heads_per_step(nh)
    ngrp, hw = nh // hp, hp * HEAD_DIM
    blk = ATT_BLOCK
    nq = s // blk
    scale = 1.0 / math.sqrt(HEAD_DIM)
    blk_spec = pl.BlockSpec((blk, hw), lambda g, i: (i, g))
    col_spec = pl.BlockSpec((s, hw), lambda g, i: (0, g))
    host = _Host(comm,
                 [blk_spec, pl.BlockSpec((s, hw), lambda g, i: (0, ngrp + g)),
                  pl.BlockSpec((s, hw), lambda g, i: (0, 2 * ngrp + g)), blk_spec, blk_spec],
                 [blk_spec, col_spec, col_spec], [jax.ShapeDtypeStruct((s, sbw), BF16)] * 3,
                 [pltpu.VMEM((s, hw), BF16)] * 2 + [pltpu.VMEM((s, hw), F32)] * 2)

    def body(*refs):
        (q_ref, k_ref, v_ref, lt_ref, do_ref), (dq_ref, dk_ref, dv_ref), (kb_ref, vb_ref, dka_ref, dva_ref) = host.split(refs)
        i = pl.program_id(1)
        step = pl.program_id(0) * nq + i
        host.before(step, ngrp * nq)

        @pl.when(i == 0)
        def _():
            kb_ref[...] = k_ref[...].astype(BF16)
            vb_ref[...] = v_ref[...].astype(BF16)
            dka_ref[...] = jnp.zeros_like(dka_ref)
            dva_ref[...] = jnp.zeros_like(dva_ref)

        heads = [slice(h * HEAD_DIM, (h + 1) * HEAD_DIM) for h in range(hp)]
        qs = [q_ref[:, hd].astype(BF16) for hd in heads]
        dos = [do_ref[:, hd].astype(BF16) for hd in heads]
        ltots = [lt_ref[:, h * HEAD_DIM:h * HEAD_DIM + 1] for h in range(hp)]
        upto = _tri(blk, lambda r, c: r <= c)
        before = _tri(blk, lambda r, c: r < c)
        causal = lax.broadcasted_iota(jnp.int32, (blk, blk), 1) < lax.broadcasted_iota(jnp.int32, (blk, blk), 0)

        def key_block(j, carry, diagonal):
            rows = pl.ds(pl.multiple_of(j * blk, blk), blk)
            hs = range(hp)
            kj = [kb_ref[rows, heads[h]] for h in hs]
            vj = [vb_ref[rows, heads[h]] for h in hs]
            z = [_nt(qs[h], kj[h]) * scale for h in hs]
            dw = [_nt(dos[h], vj[h]) for h in hs]
            ls = [_log_sigmoid(z[h]) for h in hs]
            lm = [jnp.where(causal, ls[h] - z[h], 0.0) if diagonal else ls[h] - z[h] for h in hs]
            stay = [ltots[h] - carry[h][1] - _dot_split(lm[h], upto) for h in hs]
            w = [jnp.exp(ls[h] + stay[h]) for h in hs]
            if diagonal:
                w = [jnp.where(causal, w[h], 0.0) for h in hs]
            da = [dw[h] * w[h] for h in hs]
            sig = [jnp.exp(ls[h]) for h in hs]
            chain = [sig[h] * (carry[h][2] + _dot_split(da[h], before)) for h in hs]
            if diagonal:
                chain = [jnp.where(causal, chain[h], 0.0) for h in hs]
            dzb = [((da[h] * (1.0 - sig[h]) - chain[h]) * scale).astype(BF16) for h in hs]
            dq = [carry[h][0] + jnp.dot(dzb[h], kj[h], preferred_element_type=F32) for h in hs]
            for h in hs:
                dka_ref[rows, heads[h]] += _tn(dzb[h], qs[h])
            for h in hs:
                dva_ref[rows, heads[h]] += _tn(w[h].astype(BF16), dos[h])
            return tuple((dq[h], carry[h][1] + jnp.sum(lm[h], axis=1, keepdims=True),
                          carry[h][2] + jnp.sum(da[h], axis=1, keepdims=True)) for h in hs)

        zero = jnp.zeros((blk, 1), F32)
        init = tuple((jnp.zeros((blk, HEAD_DIM), F32), zero, zero) for _ in heads)
        carry = lax.fori_loop(0, i, lambda j, c: key_block(j, c, False), init)
        carry = key_block(i, carry, True)
        for h, hd in enumerate(heads):
            dq_ref[:, hd] = carry[h][0].astype(BF16)

        @pl.when(i == nq - 1)
        def _():
            dk_ref[...] = dka_ref[...].astype(BF16)
            dv_ref[...] = dva_ref[...].astype(BF16)

        host.after(step, ngrp * nq)

    outs = pl.pallas_call(
        body, name=name, grid=(ngrp, nq), in_specs=host.in_specs, out_specs=host.out_specs, out_shape=host.out_shape,
        scratch_shapes=host.scratch, input_output_aliases=host.aliases,
        compiler_params=_cp("arbitrary", "arbitrary"))(p, p, p, ltot, dout, *host.args)
    (dq, dk, dv), extra = host.results(outs)
    return dq, dk, dv, extra


def _pool_groups(pad_ref, tile, row0, gd, halo):
    row = row0 + lax.broadcasted_iota(jnp.int32, (tile, 1), 0)
    out = []
    for gi, win in enumerate(POOL_WINDOWS):
        cs = slice(gi * gd, (gi + 1) * gd)
        tok = pad_ref[halo:halo + tile, cs]
        acc = tok
        for j in range(1, win):
            acc = acc + pad_ref[halo - j:halo - j + tile, cs]
        cnt = jnp.minimum(win, row + 1).astype(F32)
        out.append(acc / cnt - tok)
    return out


def _even_mix_fwd(p, att, pool_w, pool_scale, d, name):
    s = p.shape[0]
    half = d // 2
    gd = half // len(POOL_WINDOWS)
    t, hb = ROW_TILE, POOL_HALO

    def body(u_ref, uh_ref, g_ref, a_ref, pw_ref, sc_ref, y_ref, pad_ref):
        i = pl.program_id(0)
        pad_ref[0:hb, :] = jnp.where(i > 0, uh_ref[...], 0.0)
        pad_ref[hb:, :] = u_ref[...]
        pooled = _pool_groups(pad_ref, t, i * t, gd, hb)
        for gi in range(len(POOL_WINDOWS)):
            cs = slice(gi * gd, (gi + 1) * gd)
            po = jnp.dot(pooled[gi].astype(BF16), pw_ref[gi], preferred_element_type=F32) * sc_ref[:, cs]
            y_ref[:, half + gi * gd:half + (gi + 1) * gd] = (po * _silu(g_ref[:, half + gi * gd:half + (gi + 1) * gd])).astype(BF16)
        y_ref[:, :half] = (a_ref[...] * _silu(g_ref[:, :half])).astype(BF16)

    return pl.pallas_call(
        body, name=name, grid=(s // t,),
        in_specs=[pl.BlockSpec((t, half), lambda i: (i, 3)),
                  pl.BlockSpec((hb, half), lambda i: (jnp.maximum(i * (t // hb) - 1, 0), 3)),
                  pl.BlockSpec((t, d), lambda i: (i, 2)),
                  pl.BlockSpec((t, half), lambda i: (i, 0)),
                  pl.BlockSpec(pool_w.shape, lambda i: (0, 0, 0)),
                  pl.BlockSpec((1, half), lambda i: (0, 0))],
        out_specs=pl.BlockSpec((t, d), lambda i: (i, 0)),
        out_shape=jax.ShapeDtypeStruct((s, d), BF16),
        scratch_shapes=[pltpu.VMEM((hb + t, half), F32)],
        compiler_params=_cp("parallel"))(p, p, p, att, pool_w, pool_scale)


def _even_mix_bwd(p, att, dy, pool_w, pool_scale, d, name, comm=None):
    s = p.shape[0]
    half = d // 2
    ng = len(POOL_WINDOWS)
    gd = half // ng
    t, hb = ROW_TILE, POOL_HALO
    nt = s // t
    host = _Host(
        comm,
        [pl.BlockSpec((t, half), lambda i: (i, 3)),
         pl.BlockSpec((hb, half), lambda i: (jnp.maximum(i * (t // hb) - 1, 0), 3)),
         pl.BlockSpec((t, d), lambda i: (i, 2)),
         pl.BlockSpec((hb, half), lambda i: (jnp.minimum((i + 1) * (t // hb), s // hb - 1), 5)),
         pl.BlockSpec((t, half), lambda i: (i, 0)),
         pl.BlockSpec((t, d), lambda i: (i, 0)),
         pl.BlockSpec((hb, half), lambda i: (jnp.minimum((i + 1) * (t // hb), s // hb - 1), 1)),
         pl.BlockSpec(pool_w.shape, lambda i: (0, 0, 0)),
         pl.BlockSpec((1, half), lambda i: (0, 0))],
        [pl.BlockSpec((t, half), lambda i: (i, 0)),
         pl.BlockSpec((t, half), lambda i: (i, 0)),
         pl.BlockSpec((t, d), lambda i: (i, 0)),
         pl.BlockSpec((1, half), lambda i: (0, 0)),
         pl.BlockSpec((ng, gd, gd), lambda i: (0, 0, 0))],
        [jax.ShapeDtypeStruct((s, half), F32), jax.ShapeDtypeStruct((s, half), BF16),
         jax.ShapeDtypeStruct((s, d), BF16), jax.ShapeDtypeStruct((1, half), F32),
         jax.ShapeDtypeStruct((ng, gd, gd), F32)],
        [pltpu.VMEM((hb + t, half), F32), pltpu.VMEM((t + hb, half), F32)])

    def body(*refs):
        ((u_ref, uh_ref, g_ref, gh_ref, a_ref, dy_ref, dyh_ref, pw_ref, sc_ref),
         (da_ref, du_ref, dg_ref, dsc_ref, dpw_ref), (pad_ref, dn_ref)) = host.split(refs)
        i = pl.program_id(0)
        host.before(i, nt)
        first = i == 0
        pad_ref[0:hb, :] = jnp.where(i > 0, uh_ref[...], 0.0)
        pad_ref[hb:, :] = u_ref[...]
        pooled = _pool_groups(pad_ref, t, i * t, gd, hb)
        g1 = g_ref[:, :half]
        dy1 = dy_ref[:, :half]
        da_ref[...] = dy1 * _silu(g1)
        dg_ref[:, :half] = (dy1 * a_ref[...] * _dsilu(g1)).astype(BF16)
        row = i * t + lax.broadcasted_iota(jnp.int32, (t + hb, 1), 0)
        for gi, win in enumerate(POOL_WINDOWS):
            cs = slice(gi * gd, (gi + 1) * gd)
            cs2 = slice(half + gi * gd, half + (gi + 1) * gd)
            w = pw_ref[gi]
            pb = pooled[gi].astype(BF16)
            zp = jnp.dot(pb, w, preferred_element_type=F32)
            g2 = g_ref[:, cs2]
            dy2 = dy_ref[:, cs2]
            dg_ref[:, cs2] = (dy2 * zp * sc_ref[:, cs] * _dsilu(g2)).astype(BF16)
            dpo = dy2 * _silu(g2)
            _acc_rows(dsc_ref.at[:, cs], first, jnp.sum(dpo * zp, axis=0, keepdims=True))
            dz = (dpo * sc_ref[:, cs]).astype(BF16)
            _acc_rows(dpw_ref.at[gi], first, _tn(pb, dz))
            dzh = jnp.where(i < nt - 1, dyh_ref[:, cs] * _silu(gh_ref[:, cs]) * sc_ref[:, cs], 0.0).astype(BF16)
            dpool = _nt(dz, w)
            dpool_h = _nt(dzh, w)
            cnt = jnp.minimum(win, row + 1).astype(F32)
            dn_ref[0:t, cs] = dpool / cnt[0:t]
            dn_ref[t:, cs] = dpool_h / cnt[t:]
            acc = dn_ref[0:t, cs]
            for j in range(1, win):
                acc = acc + dn_ref[j:j + t, cs]
            du_ref[:, cs] = (acc - dpool).astype(BF16)
        host.after(i, nt)

    outs = pl.pallas_call(
        body, name=name, grid=(nt,), in_specs=host.in_specs, out_specs=host.out_specs, out_shape=host.out_shape,
        scratch_shapes=host.scratch, input_output_aliases=host.aliases,
        compiler_params=_cp("arbitrary"))(p, p, p, p, att, dy, dy, pool_w, pool_scale, *host.args)
    return host.results(outs)


def _mm_out_even(y, w, x, g_post, g_pre_next, name):
    s, k = y.shape
    d = w.shape[1]
    t = ROW_TILE

    def body(y_ref, w_ref, x_ref, gp_ref, gn_ref, o_ref, x1_ref, h1_ref):
        for r0 in range(0, t, t // 2):
            rows = slice(r0, r0 + t // 2)
            o = jnp.dot(y_ref[rows, :], w_ref[...], preferred_element_type=F32)
            o_ref[rows, :] = o
            ohat, _ = _rms_stats(o)
            x1 = x_ref[rows, :] + ohat * gp_ref[...]
            x1_ref[rows, :] = x1
            xhat, _ = _rms_stats(x1)
            h1_ref[rows, :] = (xhat * gn_ref[...]).astype(BF16)

    row = lambda c: pl.BlockSpec((t, c), lambda i: (i, 0))
    vec = pl.BlockSpec((1, d), lambda i: (0, 0))
    return pl.pallas_call(
        body, name=name, grid=(s // t,),
        in_specs=[row(k), pl.BlockSpec((k, d), lambda i: (0, 0)), row(d), vec, vec],
        out_specs=[row(d), row(d), row(d)],
        out_shape=[jax.ShapeDtypeStruct((s, d), F32), jax.ShapeDtypeStruct((s, d), F32),
                   jax.ShapeDtypeStruct((s, d), BF16)],
        compiler_params=_cp("parallel"))(y, w, x, g_post, g_pre_next)


def _mm_out_odd(y, w, x1, g_post, target, name):
    s, k = y.shape
    d = w.shape[1]
    t = ROW_TILE

    def body(y_ref, w_ref, x_ref, gp_ref, tg_ref, do_ref, dx_ref, loss_ref, dgp_ref):
        first = pl.program_id(0) == 0
        gp = gp_ref[...]
        part = dgp = None
        for r0 in range(0, t, t // 2):
            rows = slice(r0, r0 + t // 2)
            o = jnp.dot(y_ref[rows, :], w_ref[...], preferred_element_type=F32)
            ohat, r = _rms_stats(o)
            diff = x_ref[rows, :] + ohat * gp - tg_ref[rows, :]
            part_half = 0.5 * jnp.sum(jnp.mean(diff * diff, axis=-1, keepdims=True), axis=0, keepdims=True)
            dx2 = diff * (1.0 / d)
            dx_ref[rows, :] = dx2
            do, dgp_half = _rms_bwd(dx2, ohat, r, gp)
            do_ref[rows, :] = do.astype(BF16)
            part = part_half if part is None else part + part_half
            dgp = dgp_half if dgp is None else dgp + dgp_half
        _acc_rows(loss_ref, first, jnp.broadcast_to(part, loss_ref.shape))
        _acc_rows(dgp_ref, first, dgp)

    row = lambda c: pl.BlockSpec((t, c), lambda i: (i, 0))
    vec = pl.BlockSpec((1, d), lambda i: (0, 0))
    return pl.pallas_call(
        body, name=name, grid=(s // t,),
        in_specs=[row(k), pl.BlockSpec((k, d), lambda i: (0, 0)), row(d), vec, row(d)],
        out_specs=[row(d), row(d), pl.BlockSpec((8, LANES), lambda i: (0, 0)), vec],
        out_shape=[jax.ShapeDtypeStruct((s, d), BF16), jax.ShapeDtypeStruct((s, d), F32),
                   jax.ShapeDtypeStruct((8, LANES), F32), jax.ShapeDtypeStruct((1, d), F32)],
        compiler_params=_cp("arbitrary"))(y, w, x1, g_post, target)


def _layer_norm(d1, cg, cb):
    mu = jnp.mean(d1, axis=-1, keepdims=True)
    cen = d1 - mu
    rstd = lax.rsqrt(jnp.mean(cen * cen, axis=-1, keepdims=True) + EPS)
    n = cen * rstd
    return n, rstd, n * cg + cb


SUBLANES = 8
ROW_STRIP = 64
GATHER_PIECES = 8
CONV_ROWS = 64


def _make_shifts(pad_ref, cs, sh_ref):
    rows = sh_ref.shape[1]
    for r in range(1, SUBLANES):
        sh_ref[r - 1] = pad_ref[r:r + rows, cs]


def _by_shift(taps, base, sign=1):
    return sorted(range(taps), key=lambda k: ((sign * (base + k)) % SUBLANES, k))


def _window(pad_ref, cs, sh_ref, off, t):
    m, r = divmod(off, SUBLANES)
    if r == 0:
        return pad_ref[SUBLANES * m:SUBLANES * m + t, cs]
    return sh_ref[r - 1, SUBLANES * m:SUBLANES * m + t, :]


def _odd_mix_fwd(p, sconv_w, dconv_w, dconv_b, cnorm_g, cnorm_b, d, name):
    s = p.shape[0]
    w = d // 2
    k3, k31 = sconv_w.shape[0], dconv_w.shape[0]
    t, hb = ROW_TILE, CONV_HALO
    assert hb >= k31 - 1 and w % LANES == 0

    def body(p_ref, ph_ref, w3_ref, w31_ref, b31_ref, cg_ref, cb_ref, y_ref, s3_ref, d1_ref, mpad, dpad, sh_ref):
        i = pl.program_id(0)
        mpad[0:hb, :] = jnp.where(i > 0, ph_ref[:, 2 * w:3 * w] * ph_ref[:, 0:w], 0.0)
        mpad[hb:, :] = p_ref[:, 2 * w:3 * w] * p_ref[:, 0:w]
        dpad[0:hb, :] = jnp.where(i > 0, ph_ref[:, 3 * w:4 * w] * _sigmoid(ph_ref[:, 4 * w:5 * w]), 0.0)
        dpad[hb:, :] = p_ref[:, 3 * w:4 * w] * _sigmoid(p_ref[:, 4 * w:5 * w])
        for c0 in range(0, w, LANES):
            cs = slice(c0, c0 + LANES)
            acc = jnp.zeros((t, LANES), F32)
            for kk in range(k3):
                acc = acc + w3_ref[kk:kk + 1, cs] * mpad[hb - (k3 - 1) + kk:hb - (k3 - 1) + kk + t, cs]
            s3_ref[:, cs] = acc
            _make_shifts(dpad, cs, sh_ref)
            for r0 in range(0, t, CONV_ROWS):
                acc = jnp.zeros((CONV_ROWS, LANES), F32)
                for kk in _by_shift(k31, hb - (k31 - 1)):
                    acc = acc + w31_ref[kk:kk + 1, cs] * _window(dpad, cs, sh_ref, hb - (k31 - 1) + kk + r0, CONV_ROWS)
                d1_ref[r0:r0 + CONV_ROWS, cs] = acc + b31_ref[:, cs]
        _, _, d2 = _layer_norm(d1_ref[...], cg_ref[...], cb_ref[...])
        y_ref[:, :w] = (p_ref[:, w:2 * w] * s3_ref[...] * _silu(p_ref[:, 5 * w:6 * w])).astype(BF16)
        y_ref[:, w:] = (_silu(d2) * _silu(p_ref[:, 6 * w:7 * w])).astype(BF16)

    row = lambda c: pl.BlockSpec((t, c), lambda i: (i, 0))
    full = lambda a: pl.BlockSpec(a.shape, lambda i: (0, 0))
    return pl.pallas_call(
        body, name=name, grid=(s // t,),
        in_specs=[row(7 * w),
                  pl.BlockSpec((hb, 5 * w), lambda i: (jnp.maximum(i * (t // hb) - 1, 0), 0)),
                  full(sconv_w), full(dconv_w), full(dconv_b), full(cnorm_g), full(cnorm_b)],
        out_specs=[row(d), row(w), row(w)],
        out_shape=[jax.ShapeDtypeStruct((s, d), BF16), jax.ShapeDtypeStruct((s, w), F32),
                   jax.ShapeDtypeStruct((s, w), F32)],
        scratch_shapes=[pltpu.VMEM((hb + t, w), F32)] * 2 + [pltpu.VMEM((SUBLANES - 1, hb + t - SUBLANES, LANES), F32)],
        compiler_params=_cp("parallel"))(p, p, sconv_w, dconv_w, dconv_b, cnorm_g, cnorm_b)


def _odd_bwd_rows(p, s3, d1, dy, cnorm_g, cnorm_b, d, name, comm=None):
    s = p.shape[0]
    w = d // 2
    t = ROW_TILE
    col = lambda j: pl.BlockSpec((t, w), lambda i: (i, j))
    row = lambda c: pl.BlockSpec((t, c), lambda i: (i, 0))
    vec = pl.BlockSpec((1, w), lambda i: (0, 0))
    host = _Host(comm, [col(1), col(5), col(6), row(w), row(w), row(d), vec, vec],
                 [row(w), row(d), row(w), row(w), vec, vec, vec],
                 [jax.ShapeDtypeStruct((s, w), BF16), jax.ShapeDtypeStruct((s, d), BF16),
                  jax.ShapeDtypeStruct((s, w), F32), jax.ShapeDtypeStruct((s, w), F32)] + [jax.ShapeDtypeStruct((1, w), F32)] * 3, [])

    def body(*refs):
        ((bc_ref, g1_ref, g2_ref, s3_ref, d1_ref, dy_ref, cg_ref, cb_ref),
         (dbc_ref, dg_ref, ds3_ref, dd1_ref, dcg_ref, dcb_ref, db_ref), _) = host.split(refs)
        step = pl.program_id(0)
        host.before(step, s // t)
        first = step == 0

        def strip(j, sums):
            rows = slice(j * ROW_STRIP, (j + 1) * ROW_STRIP)
            g1, g2 = g1_ref[rows, :], g2_ref[rows, :]
            bc, s3v = bc_ref[rows, :], s3_ref[rows, :]
            dy1, dy2 = dy_ref[rows, :w], dy_ref[rows, w:]
            n, rstd, d2 = _layer_norm(d1_ref[rows, :], cg_ref[...], cb_ref[...])
            dg_ref[rows, :w] = (dy1 * bc * s3v * _dsilu(g1)).astype(BF16)
            dg_ref[rows, w:] = (dy2 * _silu(d2) * _dsilu(g2)).astype(BF16)
            dco = dy1 * _silu(g1)
            dbc_ref[rows, :] = (dco * s3v).astype(BF16)
            ds3_ref[rows, :] = dco * bc
            dd2 = dy2 * _silu(g2) * _dsilu(d2)
            dn = dd2 * cg_ref[...]
            dd1 = rstd * (dn - jnp.mean(dn, axis=-1, keepdims=True) - n * jnp.mean(dn * n, axis=-1, keepdims=True))
            dd1_ref[rows, :] = dd1
            dcb, dcg, db = sums
            return (dcb + jnp.sum(dd2, axis=0, keepdims=True), dcg + jnp.sum(dd2 * n, axis=0, keepdims=True),
                    db + jnp.sum(dd1, axis=0, keepdims=True))

        zero = jnp.zeros((1, w), F32)
        sums = (zero, zero, zero)
        for j in range(t // ROW_STRIP):
            sums = strip(j, sums)
        dcb, dcg, db = sums
        _acc_rows(dcb_ref, first, dcb)
        _acc_rows(dcg_ref, first, dcg)
        _acc_rows(db_ref, first, db)
        host.after(step, s // t)

    outs = pl.pallas_call(
        body, name=name, grid=(s // t,), in_specs=host.in_specs, out_specs=host.out_specs, out_shape=host.out_shape,
        scratch_shapes=host.scratch, input_output_aliases=host.aliases,
        compiler_params=_cp("arbitrary"))(p, p, p, s3, d1, dy, cnorm_g, cnorm_b, *host.args)
    return host.results(outs)


def _odd_bwd_conv(p, ds3, dd1, sconv_w, dconv_w, d, name):
    s = p.shape[0]
    w = d // 2
    k3, k31 = sconv_w.shape[0], dconv_w.shape[0]
    t, hb, ha = ROW_TILE, CONV_HALO, 8
    nt = s // t
    assert hb >= k31 - 1 and ha >= k3 - 1

    def body(hc_ref, cc_ref, ga_ref, gb_ref, hch_ref, cch_ref, gah_ref, gbh_ref, ds3_ref, ds3h_ref, dd1_ref, dd1h_ref,
             w3_ref, w31_ref, dhc_ref, dcc_ref, dga_ref, dgb_ref, dw3_ref, dw31_ref, mpad, dpad, s3pad, d1pad, sh_ref):
        i = pl.program_id(0)
        first = i == 0
        last = i == nt - 1
        mpad[0:hb, :] = jnp.where(i > 0, cch_ref[...] * hch_ref[...], 0.0)
        mpad[hb:, :] = cc_ref[...] * hc_ref[...]
        dpad[0:hb, :] = jnp.where(i > 0, gah_ref[...] * _sigmoid(gbh_ref[...]), 0.0)
        dpad[hb:, :] = ga_ref[...] * _sigmoid(gb_ref[...])
        s3pad[0:t, :] = ds3_ref[...]
        s3pad[t:, :] = jnp.where(last, 0.0, ds3h_ref[...])
        d1pad[0:t, :] = dd1_ref[...]
        d1pad[t:, :] = jnp.where(last, 0.0, dd1h_ref[...])

        @pl.when(first)
        def _():
            dw3_ref[...] = jnp.zeros_like(dw3_ref)
            dw31_ref[...] = jnp.zeros_like(dw31_ref)

        def fold(v):
            return jnp.sum(v.reshape(v.shape[0] // SUBLANES, SUBLANES, LANES), axis=0)

        groups = range(0, t, CONV_ROWS)
        for c0 in range(0, w, LANES):
            cs = slice(c0, c0 + LANES)
            ds3v = s3pad[0:t, cs]
            dm = jnp.zeros((t, LANES), F32)
            for kk in range(k3):
                dm = dm + w3_ref[kk:kk + 1, cs] * s3pad[k3 - 1 - kk:k3 - 1 - kk + t, cs]
                off = hb - (k3 - 1) + kk
                dw3_ref[SUBLANES * kk:SUBLANES * (kk + 1), cs] += fold(ds3v * mpad[off:off + t, cs])
            dcc_ref[:, cs] = (dm * hc_ref[:, cs]).astype(BF16)
            dhc_ref[:, cs] = (dm * cc_ref[:, cs]).astype(BF16)
            _make_shifts(d1pad, cs, sh_ref)
            for r0 in groups:
                rows = slice(r0, r0 + CONV_ROWS)
                dd0 = jnp.zeros((CONV_ROWS, LANES), F32)
                for kk in _by_shift(k31, -(k31 - 1), -1):
                    dd0 = dd0 + w31_ref[kk:kk + 1, cs] * _window(d1pad, cs, sh_ref, k31 - 1 - kk + r0, CONV_ROWS)
                sgb = _sigmoid(gb_ref[rows, cs])
                dga_ref[rows, cs] = (dd0 * sgb).astype(BF16)
                dgb_ref[rows, cs] = (dd0 * ga_ref[rows, cs] * sgb * (1.0 - sgb)).astype(BF16)
            _make_shifts(dpad, cs, sh_ref)
            for kk in _by_shift(k31, hb - (k31 - 1)):
                part = jnp.zeros((SUBLANES, LANES), F32)
                for r0 in groups:
                    part = part + fold(d1pad[r0:r0 + CONV_ROWS, cs]
                                       * _window(dpad, cs, sh_ref, hb - (k31 - 1) + kk + r0, CONV_ROWS))
                dw31_ref[SUBLANES * kk:SUBLANES * (kk + 1), cs] += part

    col = lambda j: pl.BlockSpec((t, w), lambda i: (i, j))
    pre = lambda j: pl.BlockSpec((hb, w), lambda i: (jnp.maximum(i * (t // hb) - 1, 0), j))
    row = pl.BlockSpec((t, w), lambda i: (i, 0))
    post = lambda h: pl.BlockSpec((h, w), lambda i: (jnp.minimum((i + 1) * (t // h), s // h - 1), 0))
    full = lambda a: pl.BlockSpec(a.shape, lambda i: (0, 0))
    dhc, dcc, dga, dgb, dw3, dw31 = pl.pallas_call(
        body, name=name, grid=(nt,),
        in_specs=[col(0), col(2), col(3), col(4), pre(0), pre(2), pre(3), pre(4),
                  row, post(ha), row, post(hb), full(sconv_w), full(dconv_w)],
        out_specs=[row, row, row, row, pl.BlockSpec((SUBLANES * k3, w), lambda i: (0, 0)),
                   pl.BlockSpec((SUBLANES * k31, w), lambda i: (0, 0))],
        out_shape=[jax.ShapeDtypeStruct((s, w), BF16)] * 4
        + [jax.ShapeDtypeStruct((SUBLANES * k3, w), F32), jax.ShapeDtypeStruct((SUBLANES * k31, w), F32)],
        scratch_shapes=[pltpu.VMEM((hb + t, w), F32)] * 2 + [pltpu.VMEM((t + ha, w), F32), pltpu.VMEM((t + hb, w), F32),
                                                             pltpu.VMEM((SUBLANES - 1, hb + t - SUBLANES, LANES), F32)],
        compiler_params=_cp("arbitrary"))(p, p, p, p, p, p, p, p, ds3, ds3, dd1, dd1, sconv_w, dconv_w)
    return dhc, dcc, dga, dgb, jnp.sum(dw3.reshape(k3, SUBLANES, w), axis=1), jnp.sum(dw31.reshape(k31, SUBLANES, w), axis=1)


def _mm_in_bwd(dp, w3, x, g_pre, dres, post, name, comm=None):
    s = dp.shape[0]
    nsh, d, ns = w3.shape
    t = 512 if s % 512 == 0 else ROW_TILE
    nt = s // t
    ks = 2 if (ns // 2) % LANES == 0 else 1
    nk, kw = nsh * ks, ns // ks
    chunk = 128
    nchunk = t // chunk
    row = pl.BlockSpec((t, d), lambda i, k: (i, 0))
    vec = pl.BlockSpec((1, d), lambda i, k: (0, 0))
    rowwise = [x, dres] + ([post[0]] if post is not None else [])
    in_specs = [pl.BlockSpec((t, kw), lambda i, k: (i, k)), pl.BlockSpec((None, d, kw), lambda i, k: (k // ks, 0, k % ks)), vec]
    out_specs = [row, vec]
    out_shape = [jax.ShapeDtypeStruct((s, d), F32), jax.ShapeDtypeStruct((1, d), F32)]
    args = [dp, w3, g_pre]
    if post is not None:
        in_specs += [vec]
        out_specs += [row, vec]
        out_shape += [jax.ShapeDtypeStruct((s, d), BF16), jax.ShapeDtypeStruct((1, d), F32)]
        args += [post[1]]
    n_blocked = len(in_specs)
    in_specs += [ANY] * len(rowwise)
    args += rowwise
    host = _Host(comm, in_specs, out_specs, out_shape,
                 [pltpu.VMEM((t, d), F32), pltpu.VMEM((len(rowwise), 2, chunk, d), F32), pltpu.SemaphoreType.DMA((len(rowwise), 2))])

    def body(*refs):
        ins, outs, (acc_ref, buf_ref, sem_ref) = host.split(refs)
        dp_ref, w_ref, g_ref = ins[:3]
        hbm = ins[n_blocked:]
        dx_ref, dg_ref = outs[:2]
        tile = pl.program_id(0)
        kk = pl.program_id(1)
        first = tile == 0
        step = tile * nk + kk
        host.before(step, nt * nk)
        part = _nt(dp_ref[...], w_ref[...])

        @pl.when(kk == 0)
        def _():
            acc_ref[...] = part

        @pl.when(kk > 0)
        def _():
            acc_ref[...] += part

        def fetch(ci, slot):
            return [pltpu.make_async_copy(src.at[pl.ds(tile * t + ci * chunk, chunk)], buf_ref.at[n, slot], sem_ref.at[n, slot])
                    for n, src in enumerate(hbm)]

        @pl.when(kk == nk - 1)
        def _():
            dg = dgp = None
            for cp in fetch(0, 0):
                cp.start()
            for ci in range(nchunk):
                slot = ci % 2
                if ci + 1 < nchunk:
                    for cp in fetch(ci + 1, 1 - slot):
                        cp.start()
                for cp in fetch(ci, slot):
                    cp.wait()
                rows = slice(ci * chunk, (ci + 1) * chunk)
                xhat, r = _rms_stats(buf_ref[0, slot])
                dxn, dg_part = _rms_bwd(acc_ref[rows, :], xhat, r, g_ref[...])
                dx = buf_ref[1, slot] + dxn
                dx_ref[rows, :] = dx
                dg = dg_part if dg is None else dg + dg_part
                if post is not None:
                    ohat, ro = _rms_stats(buf_ref[2, slot])
                    do, dgp_part = _rms_bwd(dx, ohat, ro, ins[3][...])
                    outs[2][rows, :] = do.astype(BF16)
                    dgp = dgp_part if dgp is None else dgp + dgp_part
            _acc_rows(dg_ref, first, dg)
            if post is not None:
                _acc_rows(outs[3], first, dgp)

        host.after(step, nt * nk)

    res = pl.pallas_call(
        body, name=name, grid=(nt, nk), in_specs=host.in_specs, out_specs=host.out_specs, out_shape=host.out_shape,
        scratch_shapes=host.scratch, input_output_aliases=host.aliases,
        compiler_params=_cp("arbitrary", "arbitrary"))(*args, *host.args)
    return host.results(res)


def _half_add(g, r1, c_arr, name):
    nsh, rows, ns = g.shape
    h = rows // 2
    tr = min(ROW_TILE, h)
    per = h // tr

    def body(c_ref, g_ref, r_ref, o_ref):
        o_ref[...] = (g_ref[...].astype(F32) + r_ref[...].astype(F32)).astype(BF16)

    spec = pl.BlockSpec((None, tr, ns), lambda s, r, c: (s, r, 0))
    return pl.pallas_call(
        body, name=name,
        grid_spec=pltpu.PrefetchScalarGridSpec(
            num_scalar_prefetch=1, grid=(nsh, per),
            in_specs=[pl.BlockSpec((None, tr, ns), lambda s, r, c: (s, c[0] * per + r, 0)), spec], out_specs=spec),
        out_shape=jax.ShapeDtypeStruct((nsh, h, ns), BF16), compiler_params=_cp("parallel", "parallel"))(c_arr, g, r1)


def _sum_chips(hh, r2, mc_arr, name, after=None):
    _, h, ns = hh.shape
    tr = min(ROW_TILE, h)
    per = h // tr

    def body(mc_ref, h_ref, a_ref, b_ref, c_ref, *rest):
        rest[-1][...] = ((h_ref[...].astype(F32) + a_ref[...].astype(F32)) + b_ref[...].astype(F32)) + c_ref[...].astype(F32)

    got = lambda k: pl.BlockSpec((None, tr, ns), lambda r, mc: (k, r, 0))
    ordering = [] if after is None else [after]
    return pl.pallas_call(
        body, name=name,
        grid_spec=pltpu.PrefetchScalarGridSpec(
            num_scalar_prefetch=1, grid=(per,),
            in_specs=[pl.BlockSpec((None, tr, ns), lambda r, mc: (mc[0], r, 0)), got(0), got(1), got(2)] + [ANY] * len(ordering),
            out_specs=pl.BlockSpec((tr, ns), lambda r, mc: (mc[1] * per + r, 0))),
        out_shape=jax.ShapeDtypeStruct((2 * h, ns), F32), compiler_params=_cp("parallel"))(mc_arr, hh, r2, r2, r2, *ordering)


def _add2(a, b, name):
    def body(a_ref, b_ref, o_ref):
        o_ref[...] = a_ref[...] + b_ref[...]

    return pl.pallas_call(body, name=name, out_shape=jax.ShapeDtypeStruct(a.shape, a.dtype), compiler_params=_cp())(a, b)


def _sum_chips_ordered(s2, r2, mc_arr, name):
    rows, w = s2.shape
    rh = rows // 2

    def body(mc_ref, s_ref, a_ref, b_ref, c_ref, o_ref):
        me = mc_ref[0]
        acc = None
        for j in range(N_CHIPS):
            rel = jnp.bitwise_xor(me, j)
            v = jnp.where(rel == 0, s_ref[...], jnp.where(rel == 2, a_ref[...], jnp.where(rel == 1, b_ref[...], c_ref[...])))
            acc = v if acc is None else acc + v
        o_ref[...] = acc

    got = lambda k: pl.BlockSpec((None, rh, w), lambda i, mc: (k, 0, 0))
    return pl.pallas_call(
        body, name=name,
        grid_spec=pltpu.PrefetchScalarGridSpec(
            num_scalar_prefetch=1, grid=(1,),
            in_specs=[pl.BlockSpec((rh, w), lambda i, mc: (mc[1], 0)), got(0), got(1), got(2)],
            out_specs=pl.BlockSpec((rh, w), lambda i, mc: (mc[1], 0))),
        out_shape=jax.ShapeDtypeStruct((rows, w), F32), compiler_params=_cp("arbitrary"))(mc_arr, s2, r2, r2, r2)


def _adamw(w, g, m, v, name, comm=None):
    r, c = w.shape
    tr = ROW_TILE if r % ROW_TILE == 0 else r
    c1 = 1.0 / (1.0 - ADAM_B1 ** ADAM_STEP)
    c2 = 1.0 / (1.0 - ADAM_B2 ** ADAM_STEP)
    spec = pl.BlockSpec((tr, c), lambda i: (i, 0))
    host = _Host(comm, [spec] * 4, [spec] * 4, [jax.ShapeDtypeStruct((r, c), F32)] * 4, [])

    def body(*refs):
        (w_ref, g_ref, m_ref, v_ref), (go_ref, d_ref, nm_ref, nv_ref), _ = host.split(refs)
        step = pl.program_id(0)
        host.before(step, r // tr)
        gv = g_ref[...]
        go_ref[...] = gv
        nm = ADAM_B1 * m_ref[...] + (1.0 - ADAM_B1) * gv
        nv = ADAM_B2 * v_ref[...] + (1.0 - ADAM_B2) * (gv * gv)
        nm_ref[...] = nm
        nv_ref[...] = nv
        d_ref[...] = -ADAM_LR * ((nm * c1) / (jnp.sqrt(nv * c2) + ADAM_EPS) + ADAM_WD * w_ref[...])
        host.after(step, r // tr)

    outs = pl.pallas_call(
        body, name=name, grid=(r // tr,), in_specs=host.in_specs, out_specs=host.out_specs, out_shape=host.out_shape,
        scratch_shapes=host.scratch, input_output_aliases=host.aliases,
        compiler_params=_cp("arbitrary"))(w, g, m, v, *host.args)
    return host.results(outs)


def _gather_weights(bigs, pool_w, pack_w, pack_d, name):
    nb = len(bigs)
    smalls = [pool_w, pack_w, pack_d]
    q, cw, cd = pool_w.shape[1], pack_w.shape[1], pack_d.shape[1]
    pieces = [_GatherPlan(bigs, (j, j + 1, GATHER_PIECES)) for j in range(GATHER_PIECES)]
    for j, piece in enumerate(pieces):
        piece.base = 9 + j * piece.nsems

    def body(*refs):
        srcs, dsts = refs[:nb + 3], refs[nb + 3:2 * (nb + 3)]
        ssem, rsem, lsem = refs[2 * (nb + 3):]
        x, y, c, me, chips, sib = _place()

        def small_dst(n, chip):
            if n == 0:
                return dsts[nb].at[:, pl.ds(chip * q, q), :]
            return dsts[nb + n].at[:, pl.ds(chip * (cw if n == 1 else cd), cw if n == 1 else cd)]

        local = [pltpu.make_async_copy(srcs[nb + n], small_dst(n, me), lsem.at[n]) for n in range(3)]
        for cp in local:
            cp.start()
        sends = []
        for n in range(3):
            for k, chip in enumerate(chips):
                cp = _rcopy(srcs[nb + n], small_dst(n, me), ssem.at[3 * n + k], rsem.at[3 * n + k], (*chip, c))
                cp.start()
                sends.append(cp)
        big = (srcs[:nb], dsts[:nb], ssem, rsem)
        for stage in ("start", "relay", "relay_far", "finish"):
            for piece in pieces:
                getattr(piece, stage)(*big)
        for n in range(3):
            for k, chip in enumerate(chips):
                ref = small_dst(n, 2 * chip[0] + chip[1])
                _rcopy(ref, ref, ssem.at[3 * n + k], rsem.at[3 * n + k], (*chip, c)).wait_recv()
        for cp in sends:
            cp.wait_send()
        for cp in local:
            cp.wait()

    nsem = 9 + sum(piece.nsems for piece in pieces)
    out_shape = [jax.ShapeDtypeStruct(b.shape, b.dtype) for b in bigs]
    out_shape += [jax.ShapeDtypeStruct((pool_w.shape[0], N_CHIPS * q, pool_w.shape[2]), pool_w.dtype),
                  jax.ShapeDtypeStruct((pack_w.shape[0], N_CHIPS * cw), pack_w.dtype),
                  jax.ShapeDtypeStruct((pack_d.shape[0], N_CHIPS * cd), pack_d.dtype)]
    return pl.pallas_call(
        body, name=name, in_specs=[ANY] * (nb + 3), out_specs=[ANY] * (nb + 3), out_shape=out_shape,
        input_output_aliases={a: a for a in range(nb)},
        scratch_shapes=[pltpu.SemaphoreType.DMA((nsem,)), pltpu.SemaphoreType.DMA((nsem,)), pltpu.SemaphoreType.DMA((3,))],
        compiler_params=pltpu.CompilerParams(has_side_effects=True))(*bigs, *smalls)


def _swap_with_sibling(grads, wholes, name):
    n, nw = len(grads), len(wholes)
    halves = [g.shape[1] // 2 for g in grads]

    def body(*refs):
        srcs, dsts = refs[:n + nw], refs[n + nw:2 * (n + nw)]
        ssem, rsem = refs[2 * (n + nw):]
        x, y, c, me, chips, sib = _place()
        cps = [_rcopy(srcs[a].at[:, pl.ds((1 - c) * halves[a], halves[a]), :], dsts[a], ssem.at[a], rsem.at[a], sib)
               for a in range(n)]
        cps += [_rcopy(srcs[a], dsts[a], ssem.at[a], rsem.at[a], sib) for a in range(n, n + nw)]
        for cp in cps:
            cp.start()
        for cp in cps:
            cp.wait_recv()
        for cp in cps:
            cp.wait_send()

    out_shape = [jax.ShapeDtypeStruct((g.shape[0], h, g.shape[2]), g.dtype) for g, h in zip(grads, halves)]
    out_shape += [jax.ShapeDtypeStruct(w.shape, w.dtype) for w in wholes]
    return pl.pallas_call(
        body, name=name, in_specs=[ANY] * (n + nw), out_specs=[ANY] * (n + nw), out_shape=out_shape,
        scratch_shapes=[pltpu.SemaphoreType.DMA((n + nw,)), pltpu.SemaphoreType.DMA((n + nw,))],
        compiler_params=pltpu.CompilerParams(has_side_effects=True))(*grads, *wholes)


def _scatter_to_chips(halves_in, small, name):
    n = len(halves_in)
    rh = small.shape[0] // 2

    def body(*refs):
        srcs, dsts = refs[:n + 1], refs[n + 1:2 * (n + 1)]
        ssem, rsem = refs[2 * (n + 1):]
        x, y, c, me, chips, sib = _place()
        cps = []
        for a in range(n + 1):
            for k, chip in enumerate(chips):
                src = srcs[a].at[2 * chip[0] + chip[1]] if a < n else srcs[a].at[pl.ds(c * rh, rh)]
                cps.append(_rcopy(src, dsts[a].at[k], ssem.at[3 * a + k], rsem.at[3 * a + k], (*chip, c)))
        for cp in cps:
            cp.start()
        for cp in cps:
            cp.wait_recv()
        for cp in cps:
            cp.wait_send()

    out_shape = [jax.ShapeDtypeStruct((3,) + h.shape[1:], h.dtype) for h in halves_in]
    out_shape.append(jax.ShapeDtypeStruct((3, rh, small.shape[1]), small.dtype))
    return pl.pallas_call(
        body, name=name, in_specs=[ANY] * (n + 1), out_specs=[ANY] * (n + 1), out_shape=out_shape,
        scratch_shapes=[pltpu.SemaphoreType.DMA((3 * (n + 1),)), pltpu.SemaphoreType.DMA((3 * (n + 1),))],
        compiler_params=pltpu.CompilerParams(has_side_effects=True))(*halves_in, small)


def _join_halves(parts, name):
    n = len(parts)

    def body(*refs):
        srcs, dsts = refs[:n], refs[n:2 * n]
        ssem, rsem = refs[2 * n:]
        x, y, c, me, chips, sib = _place()
        cps = []
        for a in range(n):
            h = srcs[a].shape[0] // 2
            cps.append(_rcopy(srcs[a].at[pl.ds(c * h, h)], dsts[a].at[pl.ds(c * h, h)], ssem.at[a], rsem.at[a], sib))
        for cp in cps:
            cp.start()
        for a in range(n):
            h = srcs[a].shape[0] // 2
            theirs = dsts[a].at[pl.ds((1 - c) * h, h)]
            _rcopy(theirs, theirs, ssem.at[a], rsem.at[a], sib).wait_recv()
        for cp in cps:
            cp.wait_send()

    out_shape = [jax.ShapeDtypeStruct(p.shape, p.dtype) for p in parts]
    return pl.pallas_call(
        body, name=name, in_specs=[ANY] * n, out_specs=[ANY] * n, out_shape=out_shape,
        input_output_aliases={a: a for a in range(n)},
        scratch_shapes=[pltpu.SemaphoreType.DMA((n,)), pltpu.SemaphoreType.DMA((n,))],
        compiler_params=pltpu.CompilerParams(has_side_effects=True))(*parts)


def _scatter_start(h, name):
    land = (3,) + h.shape[1:]

    def body(h_ref, land_ref, send_sems, recv_sems, h_thru, land_thru, token):
        x, y, c, me, chips, sib = _place()
        for k, chip in enumerate(chips):
            _rcopy(h_ref.at[2 * chip[0] + chip[1]], land_ref.at[k], send_sems.at[k], recv_sems.at[k], (*chip, c)).start()
        token[...] = jnp.zeros_like(token)

    hbm = pl.BlockSpec(memory_space=pltpu.HBM)
    sem = pl.BlockSpec(memory_space=pltpu.SEMAPHORE)
    return pl.pallas_call(
        body, name=name,
        out_shape=(pltpu.SemaphoreType.DMA((3,)), pltpu.SemaphoreType.DMA((3,)), pltpu.HBM(h.shape, h.dtype),
                   pltpu.HBM(land, h.dtype), jax.ShapeDtypeStruct((8, LANES), F32)),
        in_specs=(hbm, hbm), out_specs=(sem, sem, hbm, hbm, pl.BlockSpec(memory_space=pltpu.VMEM)),
        input_output_aliases={0: 2, 1: 3},
        compiler_params=pltpu.CompilerParams(has_side_effects=pltpu.SideEffectType.DATAFLOW_SIDE_EFFECTING))(
            pltpu.with_memory_space_constraint(h, pltpu.HBM),
            pltpu.with_memory_space_constraint(lax.empty(land, h.dtype), pltpu.HBM))


def _scatter_wait(send_sems, recv_sems, h_thru, land_thru, after, name):
    def body(h_ref, land_ref, send_sems, recv_sems, after_ref, h_dead, got_ref):
        x, y, c, me, chips, sib = _place()
        for k, chip in enumerate(chips):
            cp = _rcopy(h_ref.at[2 * chip[0] + chip[1]], land_ref.at[k], send_sems.at[k], recv_sems.at[k], (*chip, c))
            cp.wait_send()
            cp.wait_recv()

    hbm = pl.BlockSpec(memory_space=pltpu.HBM)
    sem = pl.BlockSpec(memory_space=pltpu.SEMAPHORE)
    return pl.pallas_call(
        body, name=name,
        out_shape=(pltpu.HBM(h_thru.shape, h_thru.dtype), pltpu.HBM(land_thru.shape, land_thru.dtype)),
        in_specs=(hbm, hbm, sem, sem, ANY), out_specs=(hbm, hbm), input_output_aliases={0: 0, 1: 1},
        compiler_params=pltpu.CompilerParams(has_side_effects=pltpu.SideEffectType.DATAFLOW_SIDE_EFFECTING))(
            h_thru, land_thru, send_sems, recv_sems, after)


def _swap_start(g, name):
    h = g.shape[1] // 2
    land = (g.shape[0], h, g.shape[2])

    def body(g_ref, land_ref, send_sem, recv_sem, g_thru, land_thru, token):
        x, y, c, me, chips, sib = _place()
        _rcopy(g_ref.at[:, pl.ds((1 - c) * h, h), :], land_ref, send_sem.at[0], recv_sem.at[0], sib).start()
        token[...] = jnp.zeros_like(token)

    hbm = pl.BlockSpec(memory_space=pltpu.HBM)
    sem = pl.BlockSpec(memory_space=pltpu.SEMAPHORE)
    return pl.pallas_call(
        body, name=name,
        out_shape=(pltpu.SemaphoreType.DMA((1,)), pltpu.SemaphoreType.DMA((1,)), pltpu.HBM(g.shape, g.dtype),
                   pltpu.HBM(land, g.dtype), jax.ShapeDtypeStruct((8, LANES), F32)),
        in_specs=(hbm, hbm), out_specs=(sem, sem, hbm, hbm, pl.BlockSpec(memory_space=pltpu.VMEM)),
        input_output_aliases={0: 2, 1: 3},
        compiler_params=pltpu.CompilerParams(has_side_effects=pltpu.SideEffectType.DATAFLOW_SIDE_EFFECTING))(
            pltpu.with_memory_space_constraint(g, pltpu.HBM),
            pltpu.with_memory_space_constraint(lax.empty(land, g.dtype), pltpu.HBM))


def _swap_wait(send_sem, recv_sem, g_thru, land_thru, after, name):
    h = g_thru.shape[1] // 2

    def body(g_ref, land_ref, send_sem, recv_sem, after_ref, g_dead, got_ref):
        x, y, c, me, chips, sib = _place()
        cp = _rcopy(g_ref.at[:, pl.ds((1 - c) * h, h), :], land_ref, send_sem.at[0], recv_sem.at[0], sib)
        cp.wait_send()
        cp.wait_recv()

    hbm = pl.BlockSpec(memory_space=pltpu.HBM)
    sem = pl.BlockSpec(memory_space=pltpu.SEMAPHORE)
    return pl.pallas_call(
        body, name=name,
        out_shape=(pltpu.HBM(g_thru.shape, g_thru.dtype), pltpu.HBM(land_thru.shape, land_thru.dtype)),
        in_specs=(hbm, hbm, sem, sem, ANY), out_specs=(hbm, hbm), input_output_aliases={0: 0, 1: 1},
        compiler_params=pltpu.CompilerParams(has_side_effects=pltpu.SideEffectType.DATAFLOW_SIDE_EFFECTING))(
            g_thru, land_thru, send_sem, recv_sem, after)


def _share_half_start(small, name):
    rh = small.shape[0] // 2
    land = (3, rh, small.shape[1])

    def body(s_ref, land_ref, send_sems, recv_sems, s_thru, land_thru, token):
        x, y, c, me, chips, sib = _place()
        for k, chip in enumerate(chips):
            _rcopy(s_ref.at[pl.ds(c * rh, rh)], land_ref.at[k], send_sems.at[k], recv_sems.at[k], (*chip, c)).start()
        token[...] = jnp.zeros_like(token)

    hbm = pl.BlockSpec(memory_space=pltpu.HBM)
    sem = pl.BlockSpec(memory_space=pltpu.SEMAPHORE)
    return pl.pallas_call(
        body, name=name,
        out_shape=(pltpu.SemaphoreType.DMA((3,)), pltpu.SemaphoreType.DMA((3,)), pltpu.HBM(small.shape, small.dtype),
                   pltpu.HBM(land, small.dtype), jax.ShapeDtypeStruct((8, LANES), F32)),
        in_specs=(hbm, hbm), out_specs=(sem, sem, hbm, hbm, pl.BlockSpec(memory_space=pltpu.VMEM)),
        input_output_aliases={0: 2, 1: 3},
        compiler_params=pltpu.CompilerParams(has_side_effects=pltpu.SideEffectType.DATAFLOW_SIDE_EFFECTING))(
            pltpu.with_memory_space_constraint(small, pltpu.HBM),
            pltpu.with_memory_space_constraint(lax.empty(land, small.dtype), pltpu.HBM))


def _share_half_wait(send_sems, recv_sems, s_thru, land_thru, after, name):
    rh = s_thru.shape[0] // 2

    def body(s_ref, land_ref, send_sems, recv_sems, after_ref, s_dead, got_ref):
        x, y, c, me, chips, sib = _place()
        for k, chip in enumerate(chips):
            cp = _rcopy(s_ref.at[pl.ds(c * rh, rh)], land_ref.at[k], send_sems.at[k], recv_sems.at[k], (*chip, c))
            cp.wait_send()
            cp.wait_recv()

    hbm = pl.BlockSpec(memory_space=pltpu.HBM)
    sem = pl.BlockSpec(memory_space=pltpu.SEMAPHORE)
    return pl.pallas_call(
        body, name=name,
        out_shape=(pltpu.HBM(s_thru.shape, s_thru.dtype), pltpu.HBM(land_thru.shape, land_thru.dtype)),
        in_specs=(hbm, hbm, sem, sem, ANY), out_specs=(hbm, hbm), input_output_aliases={0: 0, 1: 1},
        compiler_params=pltpu.CompilerParams(has_side_effects=pltpu.SideEffectType.DATAFLOW_SIDE_EFFECTING))(
            s_thru, land_thru, send_sems, recv_sems, after)


def _join_start(parts, name):
    n = len(parts)

    def body(*refs):
        srcs, (send_sems, recv_sems), token = refs[:n], refs[n:n + 2], refs[-1]
        x, y, c, me, chips, sib = _place()
        for a, src in enumerate(srcs):
            h = src.shape[0] // 2
            mine = src.at[pl.ds(c * h, h)]
            _rcopy(mine, mine, send_sems.at[a], recv_sems.at[a], sib).start()
        token[...] = jnp.zeros_like(token)

    hbm = pl.BlockSpec(memory_space=pltpu.HBM)
    sem = pl.BlockSpec(memory_space=pltpu.SEMAPHORE)
    outs = pl.pallas_call(
        body, name=name,
        out_shape=(pltpu.SemaphoreType.DMA((n,)), pltpu.SemaphoreType.DMA((n,)))
        + tuple(pltpu.HBM(p.shape, p.dtype) for p in parts) + (jax.ShapeDtypeStruct((8, LANES), F32),),
        in_specs=(hbm,) * n, out_specs=(sem, sem) + (hbm,) * n + (pl.BlockSpec(memory_space=pltpu.VMEM),),
        input_output_aliases={a: 2 + a for a in range(n)},
        compiler_params=pltpu.CompilerParams(has_side_effects=pltpu.SideEffectType.DATAFLOW_SIDE_EFFECTING))(
            *[pltpu.with_memory_space_constraint(p, pltpu.HBM) for p in parts])
    return outs[0], outs[1], list(outs[2:2 + n]), outs[-1]


def _join_wait(send_sems, recv_sems, parts, after, name):
    n = len(parts)

    def body(*refs):
        srcs, (send_sems, recv_sems) = refs[:n], refs[n:n + 2]
        x, y, c, me, chips, sib = _place()
        for a, src in enumerate(srcs):
            h = src.shape[0] // 2
            mine, theirs = src.at[pl.ds(c * h, h)], src.at[pl.ds((1 - c) * h, h)]
            _rcopy(mine, theirs, send_sems.at[a], recv_sems.at[a], sib).wait_send()
            _rcopy(theirs, theirs, send_sems.at[a], recv_sems.at[a], sib).wait_recv()

    hbm = pl.BlockSpec(memory_space=pltpu.HBM)
    sem = pl.BlockSpec(memory_space=pltpu.SEMAPHORE)
    return pl.pallas_call(
        body, name=name, out_shape=tuple(pltpu.HBM(p.shape, p.dtype) for p in parts),
        in_specs=(hbm,) * n + (sem, sem, ANY), out_specs=(hbm,) * n, input_output_aliases={a: a for a in range(n)},
        compiler_params=pltpu.CompilerParams(has_side_effects=pltpu.SideEffectType.DATAFLOW_SIDE_EFFECTING))(
            *parts, send_sems, recv_sems, after)


def _pad_rows(a, rows):
    return jnp.pad(a, ((0, rows - a.shape[0]), (0, 0)))


def _stack_rows(parts, multiple):
    padded = [_pad_rows(p, -(-p.shape[0] // 8) * 8) for p in parts]
    starts, at = [], 0
    for p in padded:
        starts.append(at)
        at += p.shape[0]
    total = -(-at // multiple) * multiple
    if total > at:
        padded.append(jnp.zeros((total - at, parts[0].shape[1]), parts[0].dtype))
    return jnp.concatenate(padded, axis=0), starts


def kernel(x, ln_pre_even, w_in_even, pool_w, pool_scale, w_out_even, ln_post_even, ln_pre_odd, w_in_odd, sconv_w, dconv_w, dconv_b, cnorm_g, cnorm_b, w_out_odd, ln_post_odd, loss_target, m_ln_pre_even, m_w_in_even, m_pool_w, m_pool_scale, m_w_out_even, m_ln_post_even, m_ln_pre_odd, m_w_in_odd, m_sconv_w, m_dconv_w, m_dconv_b, m_cnorm_g, m_cnorm_b, m_w_out_odd, m_ln_post_odd, v_ln_pre_even, v_w_in_even, v_pool_w, v_pool_scale, v_w_out_even, v_ln_post_even, v_ln_pre_odd, v_w_in_odd, v_sconv_w, v_dconv_w, v_dconv_b, v_cnorm_g, v_cnorm_b, v_w_out_odd, v_ln_post_odd):
    _, s, d = x.shape
    half = d // 2
    cw = half // N_CHIPS
    ng, q, gd = pool_w.shape[1:]
    k3, k31 = sconv_w.shape[1], dconv_w.shape[1]
    x2d, tgt = x[0], loss_target[0]
    me = 2 * lax.axis_index("x") + lax.axis_index("y")
    core = lax.axis_index("c")
    c_arr = jnp.reshape(core, (1,)).astype(jnp.int32)
    me_arr = jnp.reshape(me, (1,)).astype(jnp.int32)
    mc_arr = jnp.stack([me, core]).astype(jnp.int32)

    shards = [w_in_even[0], w_out_even[0], w_in_odd[0], w_out_odd[0]]
    slabs = [_cast_bf16_own_slab(w, me_arr, f"cast_w{n}") for n, w in enumerate(shards)]
    pool_w_b = _cast_bf16(pool_w[0].reshape(ng * q, gd), "cast_pool_w").reshape(ng, q, gd)
    pack_w, at_w = _stack_rows([sconv_w[0], dconv_w[0], dconv_b, cnorm_g, cnorm_b], 8)
    pack_d, at_d = _stack_rows([ln_pre_odd, ln_post_odd], 8)
    win_e, pool_w_f, pack_w_f, pack_d_f = _gather_weights(slabs[:1], pool_w_b, pack_w, pack_d, "gather_first")
    sconv_f = pack_w_f[at_w[0]:at_w[0] + k3]
    dconv_f = pack_w_f[at_w[1]:at_w[1] + k31]
    dconv_b_f, cnorm_g_f, cnorm_b_f = (pack_w_f[at_w[n]:at_w[n] + 1] for n in (2, 3, 4))
    ln_pre_odd_f = pack_d_f[at_d[0]:at_d[0] + 1]
    ln_post_odd_f = pack_d_f[at_d[1]:at_d[1] + 1]

    h0 = _rms_fwd(x2d, ln_pre_even, "rms_pre_even")
    plans = _Multi([_GatherPlan([slabs[1]], at=(0.6, 0.88)), _GatherPlan([slabs[2]], (0, 1, 4), at=(0.6, 0.88))])
    p_e, extra = _mm_nn(h0, win_e, "proj_in_even", plans)
    (wout_e,), (win_o,) = plans.results(extra)
    wout_e = wout_e.reshape(d, d)
    att, ltot, (win_o,) = _sba_fwd(p_e, half, "sba_fwd", _GatherPlan([win_o], (1, 4, 4), at=(0.69, 0.94)))
    y_e = _even_mix_fwd(p_e, att, pool_w_f, pool_scale, d, "even_mix_fwd")
    o_e, x1, h1 = _mm_out_even(y_e, wout_e, x2d, ln_post_even, ln_pre_odd_f, "proj_out_even")
    p_o, (wout_o,) = _mm_nn(h1, win_o, "proj_in_odd", _GatherPlan([slabs[3]]))
    wout_o = wout_o.reshape(d, d)
    y_o, s3, d1 = _odd_mix_fwd(p_o, sconv_f, dconv_f, dconv_b_f, cnorm_g_f, cnorm_b_f, d, "odd_mix_fwd")
    do_o, dx2, loss_blk, dln_post_odd = _mm_out_odd(y_o, wout_o, x1, ln_post_odd_f, tgt, "proj_out_odd_loss")

    dy_o = _mm_nt(do_o, wout_o, "dy_odd")
    g_wout_o = _mm_tn(y_o, do_o, 1, "dw_out_odd")[0].reshape(N_CHIPS, d // N_CHIPS, d)
    (dbc, dgate_o, ds3, dd1, dcnorm_g, dcnorm_b, ddconv_b), (got,) = _odd_bwd_rows(
        p_o, s3, d1, dy_o, cnorm_g_f, cnorm_b_f, d, "odd_bwd_rows", _SwapPlan([g_wout_o]))
    h_wout_o = _half_add(g_wout_o, got, c_arr, "half_add_out_odd")
    dhc, dcc, dga, dgb, dsconv, ddconv = _odd_bwd_conv(p_o, ds3, dd1, sconv_f, dconv_f, d, "odd_bwd_conv")
    dp_o = jnp.concatenate([dhc, dbc, dcc, dga, dgb, dgate_o], axis=1)
    g_win_o, (s_wout_o,) = _mm_tn(h1, dp_o, N_CHIPS, "dw_in_odd", _ScatterPlan([h_wout_o]))
    (dx1, dln_pre_odd, do_e, dln_post_even), (got,) = _mm_in_bwd(
        dp_o, win_o, x1, ln_pre_odd_f, dx2, (o_e, ln_post_even), "dx_odd", _SwapPlan([g_win_o]))
    h_win_o = _half_add(g_win_o, got, c_arr, "half_add_in_odd")

    dy_e = _mm_nt(do_e, wout_e, "dy_even")
    g_wout_e = _mm_tn(y_e, do_e, 1, "dw_out_even")[0].reshape(N_CHIPS, d // N_CHIPS, d)
    (datt, du, dgate_e, dpool_scale, dpool_w), (got,) = _even_mix_bwd(
        p_e, att, dy_e, pool_w_f, pool_scale, d, "even_mix_bwd", _SwapPlan([g_wout_e]))
    h_wout_e = _half_add(g_wout_e, got, c_arr, "half_add_out_even")
    two = lambda v: v.reshape(2, half)
    small_parts = [dpool_scale, two(dln_post_even), two(dln_pre_odd), two(dln_post_odd),
                   dsconv, ddconv, ddconv_b, dcnorm_g, dcnorm_b, dpool_w.reshape(gd, half)]
    small, at_s = _stack_rows(small_parts, 16)
    plans = _Multi([_ScatterPlan([h_win_o]), _SendWholePlan([small])])
    dq, dk, dv, extra = _sba_bwd(p_e, ltot, datt, half, "sba_bwd", plans)
    (s_win_o,), (small1,) = plans.results(extra)
    small2 = _add2(small, small1, "small_add")
    dp_e = jnp.concatenate([dq, dk, dv, du, dgate_e], axis=1)
    plans = _Multi([_ScatterPlan([h_wout_e]), _ShareHalfPlan([small2])])
    g_win_e, extra = _mm_tn(h0, dp_e, N_CHIPS, "dw_in_even", plans)
    (s_wout_e,), (small_got,) = plans.results(extra)
    swap = _swap_start(g_win_e, "swap_in_even_start")
    pairs = [(h_wout_e, s_wout_e), (h_win_o, s_win_o), (h_wout_o, s_wout_o)]
    parts = []
    for n, (h, r) in enumerate(pairs):
        parts.append(_sum_chips(h, r, mc_arr, f"sum_chips{n + 1}", after=parts[-1] if parts else swap[4]))
    g_win_e, got = _swap_wait(*swap[:4], parts[-1], "swap_in_even_wait")
    parts.append(_sum_chips_ordered(small2, small_got, mc_arr, "small_sum"))
    join_sems = _join_start(parts, "join_first_start")
    h_win_e = _half_add(g_win_e, got, c_arr, "half_add_in_even")
    send_sems, recv_sems, h_win_e, landing, token = _scatter_start(h_win_e, "scatter_in_even_start")
    (grad_x, dln_pre_even), _ = _mm_in_bwd(dp_e, win_e, x2d, ln_pre_even + token[0:1, 0:1], dx1, None, "dx_even")

    last, at_l = _stack_rows([two(dln_pre_even), jnp.pad(loss_blk[0:1], ((0, 0), (0, half - LANES)))], 16)
    (last1,) = _swap_with_sibling([], [last], "swap_last")
    last2 = _add2(last, last1, "last_add")
    share = _share_half_start(last2, "share_last_start")
    gw_out_e, gw_in_o, gw_out_o, red = _join_wait(*join_sems[:3], share[4], "join_first_wait")

    def rows(n, cnt):
        return red[at_s[n]:at_s[n] + cnt]

    def mine(a, width):
        return lax.dynamic_slice_in_dim(a, me * width, width, axis=1)

    quarter = d // N_CHIPS
    g_small = {
        "pool_scale": rows(0, 1),
        "ln_post_even": rows(1, 2).reshape(1, d),
        "ln_pre_odd": mine(rows(2, 2).reshape(1, d), quarter),
        "ln_post_odd": mine(rows(3, 2).reshape(1, d), quarter),
        "sconv_w": mine(rows(4, k3), cw),
        "dconv_w": mine(rows(5, k31), cw),
        "dconv_b": mine(rows(6, 1), cw),
        "cnorm_g": mine(rows(7, 1), cw),
        "cnorm_b": mine(rows(8, 1), cw),
        "pool_w": lax.dynamic_slice_in_dim(rows(9, gd).reshape(ng, gd, gd), me * q, q, axis=1).reshape(ng * q, gd),
    }
    w2d = {
        "ln_pre_even": ln_pre_even, "w_in_even": w_in_even[0], "pool_w": pool_w[0].reshape(ng * q, gd),
        "pool_scale": pool_scale, "w_out_even": w_out_even[0], "ln_post_even": ln_post_even, "ln_pre_odd": ln_pre_odd,
        "w_in_odd": w_in_odd[0], "sconv_w": sconv_w[0], "dconv_w": dconv_w[0], "dconv_b": dconv_b, "cnorm_g": cnorm_g,
        "cnorm_b": cnorm_b, "w_out_odd": w_out_odd[0], "ln_post_odd": ln_post_odd,
    }
    moments = {
        "ln_pre_even": (m_ln_pre_even, v_ln_pre_even), "w_in_even": (m_w_in_even, v_w_in_even),
        "pool_w": (m_pool_w, v_pool_w), "pool_scale": (m_pool_scale, v_pool_scale),
        "w_out_even": (m_w_out_even, v_w_out_even), "ln_post_even": (m_ln_post_even, v_ln_post_even),
        "ln_pre_odd": (m_ln_pre_odd, v_ln_pre_odd), "w_in_odd": (m_w_in_odd, v_w_in_odd),
        "sconv_w": (m_sconv_w, v_sconv_w), "dconv_w": (m_dconv_w, v_dconv_w), "dconv_b": (m_dconv_b, v_dconv_b),
        "cnorm_g": (m_cnorm_g, v_cnorm_g), "cnorm_b": (m_cnorm_b, v_cnorm_b),
        "w_out_odd": (m_w_out_odd, v_w_out_odd), "ln_post_odd": (m_ln_post_odd, v_ln_post_odd),
    }
    def update(name, g):
        m_in, v_in = moments[name]
        w = w2d[name]
        return _adamw(w, g, m_in.reshape(w.shape), v_in.reshape(w.shape), "adamw_" + name)[0]

    updates = {name: update(name, g) for name, g in (("w_in_odd", gw_in_o), ("w_out_even", gw_out_e), ("w_out_odd", gw_out_o))}
    last2, last_got = _share_half_wait(*share[:4], updates["w_out_odd"][1], "share_last_wait")
    last_sum = _sum_chips_ordered(last2, last_got, mc_arr, "last_sum")
    h_win_e, s_win_e = _scatter_wait(send_sems, recv_sems, h_win_e, landing, last_sum, "scatter_in_even_wait")
    last_sems = _join_start([_sum_chips(h_win_e, s_win_e, mc_arr, "sum_chips0"), last_sum], "join_last_start")
    for name, g in g_small.items():
        updates[name] = update(name, g)
    gw_in_e, red_last = _join_wait(*last_sems[:3], updates["pool_w"][1], "join_last_wait")
    loss = red_last[at_l[1], 0]
    updates["ln_pre_even"] = update("ln_pre_even", red_last[at_l[0]:at_l[0] + 2].reshape(1, d))
    updates["w_in_even"] = update("w_in_even", gw_in_e)
    outs = [[u.reshape(moments[name][0].shape) for u in updates[name]] for name in w2d]
    grads_out, deltas, new_m, new_v = zip(*outs)
    return (loss, grad_x.reshape(x.shape), *grads_out, *deltas, *new_m, *new_v)
```

```python
import functools
import math

import jax
import jax.numpy as jnp
from jax import lax
from jax.experimental import pallas as pl
from jax.experimental.pallas import tpu as pltpu

F32 = jnp.float32
BF16 = jnp.bfloat16
EPS = 1e-6
N_CHIPS = 4
VMEM_LIMIT_V7X = 56 << 20
HEAD_DIM = 128
ATT_BLOCK = 256
POOL_WINDOWS = (2, 4, 8, 16)
ROW_TILE = 256
POOL_HALO = 16
CONV_HALO = 32
LANES = 128
ADAM_LR, ADAM_B1, ADAM_B2, ADAM_EPS, ADAM_WD, ADAM_STEP = 0.001, 0.9, 0.999, 1e-08, 0.01, 10
MESH_ID = pl.DeviceIdType.MESH
ANY = pl.BlockSpec(memory_space=pl.ANY)


def _cp(*sem):
    return pltpu.CompilerParams(dimension_semantics=sem or None, vmem_limit_bytes=VMEM_LIMIT_V7X)


def _pick_tile(n, cap):
    best = None
    for t in range(LANES, min(n, cap) + 1, LANES):
        if n % t == 0:
            best = t
    assert best is not None, (n, cap)
    return best


def _sigmoid(x):
    return 1.0 / (1.0 + jnp.exp(-x))


def _silu(x):
    return x * _sigmoid(x)


def _dsilu(x):
    s = _sigmoid(x)
    return s * (1.0 + x * (1.0 - s))


def _log_sigmoid(z):
    return jnp.minimum(z, 0.0) - jnp.log(1.0 + jnp.exp(-jnp.abs(z)))


def _rms_stats(x):
    r = lax.rsqrt(jnp.mean(x * x, axis=-1, keepdims=True) + EPS)
    return x * r, r


def _rms_bwd(dh, xhat, r, g):
    dxh = dh * g
    dx = r * (dxh - xhat * jnp.mean(dxh * xhat, axis=-1, keepdims=True))
    return dx, jnp.sum(dh * xhat, axis=0, keepdims=True)


def _acc_rows(ref, first, val):
    @pl.when(first)
    def _():
        ref[...] = val

    @pl.when(jnp.logical_not(first))
    def _():
        ref[...] += val


def _rcopy(src, dst, ssem, rsem, dev):
    return pltpu.make_async_remote_copy(src_ref=src, dst_ref=dst, send_sem=ssem, recv_sem=rsem,
                                        device_id=dev, device_id_type=MESH_ID)


def _place():
    x, y, c = lax.axis_index("x"), lax.axis_index("y"), lax.axis_index("c")
    chips = [(1 - x, y), (x, 1 - y), (1 - x, 1 - y)]
    return x, y, c, 2 * x + y, chips, (x, y, 1 - c)


class _GatherPlan:
    PER_ARRAY = 7

    def __init__(self, arrays, part=(0, 1, 1), at=(0.5, 0.8)):
        self.operands = list(arrays)
        self.out_shapes = [jax.ShapeDtypeStruct(a.shape, a.dtype) for a in arrays]
        self.aliases = {i: i for i in range(len(arrays))}
        self.nsems = self.PER_ARRAY * len(arrays)
        self.base = 0
        self.halves = [a.shape[1] // 2 for a in arrays]
        self.part = part
        self.at = at

    def schedule(self):
        return [(0.0, self.start), (self.at[0], self.relay), (self.at[1], self.relay_far)]

    def _rows(self, ref, a, chip, half, quarter=None):
        lo, hi, n = self.part
        h = self.halves[a]
        first, size = half * h + lo * h // n, (hi - lo) * h // n
        if quarter is not None:
            first, size = first + quarter * (size // 2), size // 2
        return ref.at[chip, pl.ds(first, size)]

    def _copy(self, src, dst, a, n, ssem, rsem, dev):
        return _rcopy(src, dst, ssem.at[self.base + self.PER_ARRAY * a + n], rsem.at[self.base + self.PER_ARRAY * a + n], dev)

    def _own(self, ins, outs, ssem, rsem):
        x, y, c, me, chips, sib = _place()
        return [self._copy(self._rows(ins[a], a, me, c), self._rows(outs[a], a, me, c), a, k, ssem, rsem, (*chips[k], c))
                for a in range(len(ins)) for k in (0, 1)]

    def _relays(self, outs, ssem, rsem, a, k):
        x, y, c, me, chips, sib = _place()
        chip = 2 * chips[k][0] + chips[k][1]
        whole, quarter = self._rows(outs[a], a, chip, c), self._rows(outs[a], a, chip, c, k)
        return (self._copy(whole, whole, a, k, ssem, rsem, (*chips[k], c)),
                self._copy(quarter, quarter, a, 2 + k, ssem, rsem, (*chips[1 - k], c)),
                self._copy(whole, whole, a, 4 + k, ssem, rsem, sib))

    def _far(self, outs, ssem, rsem, a):
        x, y, c, me, chips, sib = _place()
        chip = 2 * chips[2][0] + chips[2][1]
        whole = self._rows(outs[a], a, chip, c)
        got = [self._copy(q, q, a, 2 + k, ssem, rsem, (*chips[1 - k], c))
               for k, q in enumerate([self._rows(outs[a], a, chip, c, 0), self._rows(outs[a], a, chip, c, 1)])]
        return got, self._copy(whole, whole, a, 6, ssem, rsem, sib)

    def start(self, ins, outs, ssem, rsem):
        for cp in self._own(ins, outs, ssem, rsem):
            cp.start()

    def relay(self, ins, outs, ssem, rsem):
        for a in range(len(outs)):
            for k in (0, 1):
                landed, onward, to_sibling = self._relays(outs, ssem, rsem, a, k)
                landed.wait_recv()
                onward.start()
                to_sibling.start()

    def relay_far(self, ins, outs, ssem, rsem):
        for a in range(len(outs)):
            got, to_sibling = self._far(outs, ssem, rsem, a)
            for cp in got:
                cp.wait_recv()
            to_sibling.start()

    def finish(self, ins, outs, ssem, rsem):
        x, y, c, me, chips, sib = _place()
        for a in range(len(outs)):
            for k in range(3):
                ref = self._rows(outs[a], a, 2 * chips[k][0] + chips[k][1], 1 - c)
                self._copy(ref, ref, a, 4 + k, ssem, rsem, sib).wait_recv()
        for cp in self._own(ins, outs, ssem, rsem):
            cp.wait_send()
        for a in range(len(outs)):
            for k in (0, 1):
                _, onward, to_sibling = self._relays(outs, ssem, rsem, a, k)
                onward.wait_send()
                to_sibling.wait_send()
            self._far(outs, ssem, rsem, a)[1].wait_send()


class _ScatterPlan:
    def __init__(self, arrays, part=(0, 1, 1), into=None):
        self.n = len(arrays)
        self.operands = list(arrays) + list(into or [])
        self.out_shapes = [jax.ShapeDtypeStruct((3,) + a.shape[1:], a.dtype) for a in arrays]
        self.aliases = {self.n + i: i for i in range(self.n)} if into else {}
        self.nsems = 3 * self.n
        self.base = 0
        self.part = part

    def _copies(self, ins, outs, ssem, rsem):
        x, y, c, me, chips, sib = _place()
        lo, hi, n = self.part
        out = []
        for a in range(self.n):
            h = ins[a].shape[1]
            rows = pl.ds(lo * h // n, (hi - lo) * h // n)
            for k, chip in enumerate(chips):
                out.append(_rcopy(ins[a].at[2 * chip[0] + chip[1], rows], outs[a].at[k, rows],
                                  ssem.at[self.base + 3 * a + k], rsem.at[self.base + 3 * a + k], (*chip, c)))
        return out

    def schedule(self):
        return [(0.0, self.start)]

    def start(self, ins, outs, ssem, rsem):
        for cp in self._copies(ins, outs, ssem, rsem):
            cp.start()

    def finish(self, ins, outs, ssem, rsem):
        cps = self._copies(ins, outs, ssem, rsem)
        for cp in cps:
            cp.wait_recv()
        for cp in cps:
            cp.wait_send()


class _ShareHalfPlan(_ScatterPlan):
    def __init__(self, arrays):
        super().__init__(arrays)
        self.out_shapes = [jax.ShapeDtypeStruct((3, a.shape[0] // 2, a.shape[1]), a.dtype) for a in arrays]

    def _copies(self, ins, outs, ssem, rsem):
        x, y, c, me, chips, sib = _place()
        out = []
        for a in range(self.n):
            rh = ins[a].shape[0] // 2
            for k, chip in enumerate(chips):
                out.append(_rcopy(ins[a].at[pl.ds(c * rh, rh)], outs[a].at[k],
                                  ssem.at[self.base + 3 * a + k], rsem.at[self.base + 3 * a + k], (*chip, c)))
        return out


class _SwapPlan:
    def __init__(self, grads):
        self.operands = list(grads)
        self.out_shapes = [jax.ShapeDtypeStruct((g.shape[0], g.shape[1] // 2, g.shape[2]), g.dtype) for g in grads]
        self.aliases = {}
        self.nsems = len(grads)
        self.base = 0

    def _copies(self, ins, outs, ssem, rsem):
        x, y, c, me, chips, sib = _place()
        out = []
        for a, src in enumerate(ins):
            h = src.shape[1] // 2
            out.append(_rcopy(src.at[:, pl.ds((1 - c) * h, h), :], outs[a], ssem.at[self.base + a], rsem.at[self.base + a], sib))
        return out

    def schedule(self):
        return [(0.0, self.start)]

    def start(self, ins, outs, ssem, rsem):
        for cp in self._copies(ins, outs, ssem, rsem):
            cp.start()

    def finish(self, ins, outs, ssem, rsem):
        cps = self._copies(ins, outs, ssem, rsem)
        for cp in cps:
            cp.wait_recv()
        for cp in cps:
            cp.wait_send()


class _SendWholePlan(_SwapPlan):
    def __init__(self, arrays):
        self.operands = list(arrays)
        self.out_shapes = [jax.ShapeDtypeStruct(a.shape, a.dtype) for a in arrays]
        self.aliases = {}
        self.nsems = len(arrays)
        self.base = 0

    def _copies(self, ins, outs, ssem, rsem):
        x, y, c, me, chips, sib = _place()
        return [_rcopy(src, outs[a], ssem.at[self.base + a], rsem.at[self.base + a], sib) for a, src in enumerate(ins)]


class _GatherPieces:
    def __init__(self, arrays, n, at):
        self.pieces = [_GatherPlan(arrays, (j, j + 1, n), at) for j in range(n)]
        self.operands, self.out_shapes, self.aliases = self.pieces[0].operands, self.pieces[0].out_shapes, self.pieces[0].aliases
        self.nsems = sum(p.nsems for p in self.pieces)
        self.at = at
        self.base = 0

    @property
    def base(self):
        return self.pieces[0].base

    @base.setter
    def base(self, value):
        for j, p in enumerate(self.pieces):
            p.base = value + j * p.nsems

    def schedule(self):
        return [(0.0, self.start), (self.at[0], self.relay), (self.at[1], self.relay_far)]

    def _each(self, what, *a):
        for p in self.pieces:
            getattr(p, what)(*a)

    def start(self, *a):
        self._each("start", *a)

    def relay(self, *a):
        self._each("relay", *a)

    def relay_far(self, *a):
        self._each("relay_far", *a)

    def finish(self, *a):
        self._each("finish", *a)


class _SmallGatherPlan:
    def __init__(self, arrays, widths):
        self.operands = list(arrays)
        self.out_shapes = [jax.ShapeDtypeStruct(a.shape, a.dtype) for a in arrays]
        self.aliases = {i: i for i in range(3)}
        self.nsems = 9
        self.base = 0
        self.widths = widths

    def _part(self, ref, n, chip):
        w = self.widths[n]
        return ref.at[:, pl.ds(chip * w, w), :] if n == 0 else ref.at[:, pl.ds(chip * w, w)]

    def _copies(self, ins, outs, ssem, rsem, own):
        x, y, c, me, chips, sib = _place()
        out = []
        for n in range(3):
            for k, chip in enumerate(chips):
                which = me if own else 2 * chip[0] + chip[1]
                out.append(_rcopy(self._part(ins[n], n, which), self._part(outs[n], n, which),
                                  ssem.at[self.base + 3 * n + k], rsem.at[self.base + 3 * n + k], (*chip, c)))
        return out

    def schedule(self):
        return [(0.0, self.start)]

    def start(self, ins, outs, ssem, rsem):
        for cp in self._copies(ins, outs, ssem, rsem, True):
            cp.start()

    def finish(self, ins, outs, ssem, rsem):
        for cp in self._copies(ins, outs, ssem, rsem, False):
            cp.wait_recv()
        for cp in self._copies(ins, outs, ssem, rsem, True):
            cp.wait_send()


class _Multi:
    def __init__(self, plans):
        self.plans = plans
        self.operands, self.out_shapes, self.aliases, self.nsems = [], [], {}, 0
        self.spans = []
        for p in plans:
            ni, no = len(self.operands), len(self.out_shapes)
            self.spans.append((ni, ni + len(p.operands), no, no + len(p.out_shapes)))
            self.aliases.update({ni + i: no + j for i, j in p.aliases.items()})
            p.base = self.nsems
            self.nsems += p.nsems
            self.operands += p.operands
            self.out_shapes += p.out_shapes

    def schedule(self):
        def bound(fn, span):
            i0, i1, o0, o1 = span
            return lambda ins, outs, ssem, rsem: fn(ins[i0:i1], outs[o0:o1], ssem, rsem)

        stages = [(at, bound(fn, span)) for p, span in zip(self.plans, self.spans) for at, fn in p.schedule()]
        return sorted(stages, key=lambda s: s[0])

    def finish(self, ins, outs, ssem, rsem):
        for p, (i0, i1, o0, o1) in zip(self.plans, self.spans):
            p.finish(ins[i0:i1], outs[o0:o1], ssem, rsem)

    def results(self, extra):
        return [list(extra[o0:o1]) for (_, _, o0, o1) in self.spans]


class _Host:
    def __init__(self, comm, in_specs, out_specs, out_shape, scratch, prefetch=0):
        self.comm = comm
        self.n_in, self.n_out = len(in_specs), len(out_specs)
        self.in_specs, self.out_specs, self.out_shape, self.scratch = list(in_specs), list(out_specs), list(out_shape), list(scratch)
        self.aliases = {}
        self.args = []
        if comm is not None:
            self.in_specs += [ANY] * len(comm.operands)
            self.out_specs += [ANY] * len(comm.out_shapes)
            self.out_shape += comm.out_shapes
            self.scratch += [pltpu.SemaphoreType.DMA((comm.nsems,)), pltpu.SemaphoreType.DMA((comm.nsems,))]
            self.aliases = {prefetch + self.n_in + i: self.n_out + j for i, j in comm.aliases.items()}
            self.args = list(comm.operands)

    def split(self, refs):
        nc = len(self.args)
        nco = len(self.out_shape) - self.n_out
        ins, p = refs[:self.n_in], self.n_in + nc
        outs, rest = refs[p:p + self.n_out], refs[p + self.n_out + nco:]
        self._cargs = None
        if self.comm is not None:
            self._cargs = (refs[self.n_in:p], refs[p + self.n_out:p + self.n_out + nco], rest[-2], rest[-1])
            rest = rest[:-2]
        return ins, outs, rest

    def before(self, step, total):
        if self.comm is None:
            return

        for at, stage in self.comm.schedule():
            pl.when(step == min(total - 1, int(at * total)))(functools.partial(stage, *self._cargs))

    def after(self, step, total):
        if self.comm is None:
            return

        @pl.when(step == total - 1)
        def _():
            self.comm.finish(*self._cargs)

    def results(self, outs):
        return outs[:self.n_out], outs[self.n_out:]


def _cast_bf16(x, name):
    r, c = x.shape
    tr = ROW_TILE if r % ROW_TILE == 0 else r

    def body(x_ref, o_ref):
        o_ref[...] = x_ref[...].astype(BF16)

    return pl.pallas_call(
        body, name=name, grid=(r // tr,),
        in_specs=[pl.BlockSpec((tr, c), lambda i: (i, 0))],
        out_specs=pl.BlockSpec((tr, c), lambda i: (i, 0)),
        out_shape=jax.ShapeDtypeStruct((r, c), BF16), compiler_params=_cp("parallel"))(x)


def _cast_bf16_own_slab(x, me_arr, name):
    r, c = x.shape
    tr = ROW_TILE if r % ROW_TILE == 0 else r

    def body(me_ref, x_ref, o_ref):
        o_ref[...] = x_ref[...].astype(BF16)

    return pl.pallas_call(
        body, name=name,
        grid_spec=pltpu.PrefetchScalarGridSpec(
            num_scalar_prefetch=1, grid=(r // tr,),
            in_specs=[pl.BlockSpec((tr, c), lambda i, me: (i, 0))],
            out_specs=pl.BlockSpec((None, tr, c), lambda i, me: (me[0], i, 0))),
        out_shape=jax.ShapeDtypeStruct((N_CHIPS, r, c), BF16), compiler_params=_cp("parallel"))(me_arr, x)


def _prep(x, g, shards, me_arr, name, comm):
    s, d = x.shape
    steps = s // ROW_TILE
    tiles = [(w.shape[0] // steps, w.shape[1]) for w in shards]
    assert all(w.shape[0] % steps == 0 for w in shards)
    in_specs = [pl.BlockSpec((ROW_TILE, d), lambda i, me: (i, 0)), pl.BlockSpec((1, d), lambda i, me: (0, 0))]
    in_specs += [pl.BlockSpec(t, lambda i, me: (i, 0)) for t in tiles]
    out_specs = [pl.BlockSpec((ROW_TILE, d), lambda i, me: (i, 0))]
    out_specs += [pl.BlockSpec((None,) + t, lambda i, me: (me[0], i, 0)) for t in tiles]
    out_shape = [jax.ShapeDtypeStruct((s, d), BF16)] + [jax.ShapeDtypeStruct((N_CHIPS,) + w.shape, BF16) for w in shards]
    host = _Host(comm, in_specs, out_specs, out_shape, [], prefetch=1)

    def body(me_ref, *refs):
        (x_ref, g_ref, *w_refs), (h_ref, *slab_refs), _ = host.split(refs)
        step = pl.program_id(0)
        host.before(step, steps)
        xhat, _ = _rms_stats(x_ref[...])
        h_ref[...] = (xhat * g_ref[...]).astype(BF16)
        for w_ref, slab_ref in zip(w_refs, slab_refs):
            slab_ref[...] = w_ref[...].astype(BF16)
        host.after(step, steps)

    outs = pl.pallas_call(
        body, name=name,
        grid_spec=pltpu.PrefetchScalarGridSpec(num_scalar_prefetch=1, grid=(steps,), in_specs=host.in_specs,
                                               out_specs=host.out_specs, scratch_shapes=host.scratch),
        out_shape=host.out_shape, input_output_aliases=host.aliases,
        compiler_params=_cp("arbitrary"))(me_arr, x, g, *shards, *host.args)
    (h, *slabs), extra = host.results(outs)
    return h, slabs, extra


def _rms_fwd(x, g, name):
    s, d = x.shape

    def body(x_ref, g_ref, h_ref):
        xhat, _ = _rms_stats(x_ref[...])
        h_ref[...] = (xhat * g_ref[...]).astype(BF16)

    return pl.pallas_call(
        body, name=name, grid=(s // ROW_TILE,),
        in_specs=[pl.BlockSpec((ROW_TILE, d), lambda i: (i, 0)), pl.BlockSpec((1, d), lambda i: (0, 0))],
        out_specs=pl.BlockSpec((ROW_TILE, d), lambda i: (i, 0)),
        out_shape=jax.ShapeDtypeStruct((s, d), BF16), compiler_params=_cp("parallel"))(x, g)


def _mm_nn(a, w3, name, comm=None):
    m, k = a.shape
    nsh, _, ns = w3.shape
    tm = 512 if m % 512 == 0 else ROW_TILE
    tn = _pick_tile(ns, 1024)
    per = ns // tn
    grid = (nsh * per, m // tm)
    host = _Host(comm,
                 [pl.BlockSpec((tm, k), lambda n, i: (i, 0)), pl.BlockSpec((None, k, tn), lambda n, i: (n // per, 0, n % per))],
                 [pl.BlockSpec((tm, tn), lambda n, i: (i, n))], [jax.ShapeDtypeStruct((m, nsh * ns), F32)], [])

    def body(*refs):
        (a_ref, w_ref), (o_ref,), _ = host.split(refs)
        step = pl.program_id(0) * grid[1] + pl.program_id(1)
        host.before(step, grid[0] * grid[1])
        o_ref[...] = jnp.dot(a_ref[...], w_ref[...], preferred_element_type=F32)
        host.after(step, grid[0] * grid[1])

    outs = pl.pallas_call(
        body, name=name, grid=grid, in_specs=host.in_specs, out_specs=host.out_specs, out_shape=host.out_shape,
        scratch_shapes=host.scratch, input_output_aliases=host.aliases,
        compiler_params=_cp("arbitrary", "arbitrary"))(a, w3, *host.args)
    (out,), extra = host.results(outs)
    return out, extra


def _mm_nt(a, b, name):
    m, k = a.shape
    n = b.shape[0]
    tm = 512 if m % 512 == 0 else ROW_TILE

    def body(a_ref, b_ref, o_ref):
        o_ref[...] = lax.dot_general(a_ref[...], b_ref[...], (((1,), (1,)), ((), ())), preferred_element_type=F32)

    return pl.pallas_call(
        body, name=name, grid=(m // tm,),
        in_specs=[pl.BlockSpec((tm, k), lambda i: (i, 0)), pl.BlockSpec((n, k), lambda i: (0, 0))],
        out_specs=pl.BlockSpec((tm, n), lambda i: (i, 0)),
        out_shape=jax.ShapeDtypeStruct((m, n), F32), compiler_params=_cp("parallel"))(a, b)


def _mm_tn(a, b, nsh, name, comm=None):
    s, m = a.shape
    n = b.shape[1]
    ns = n // nsh
    tm = 512 if m % 512 == 0 else ROW_TILE
    tn = _pick_tile(ns, 1024)
    per = ns // tn
    grid = (nsh * per, m // tm)
    host = _Host(comm, [pl.BlockSpec((s, tm), lambda j, i: (0, i)), pl.BlockSpec((s, tn), lambda j, i: (0, j))],
                 [pl.BlockSpec((None, tm, tn), lambda j, i: (j // per, i, j % per))],
                 [jax.ShapeDtypeStruct((nsh, m, ns), BF16)], [])

    def body(*refs):
        (a_ref, b_ref), (o_ref,), _ = host.split(refs)
        step = pl.program_id(0) * grid[1] + pl.program_id(1)
        host.before(step, grid[0] * grid[1])
        o_ref[...] = lax.dot_general(a_ref[...], b_ref[...], (((0,), (0,)), ((), ())),
                                     preferred_element_type=F32).astype(BF16)
        host.after(step, grid[0] * grid[1])

    outs = pl.pallas_call(
        body, name=name, grid=grid, in_specs=host.in_specs, out_specs=host.out_specs, out_shape=host.out_shape,
        scratch_shapes=host.scratch, input_output_aliases=host.aliases,
        compiler_params=_cp("arbitrary", "arbitrary"))(a, b, *host.args)
    (out,), extra = host.results(outs)
    return out, extra


def _tri(n, rel):
    row = lax.broadcasted_iota(jnp.int32, (2 * n, n), 0)
    col = lax.broadcasted_iota(jnp.int32, (2 * n, n), 1)
    return jnp.where(rel(jnp.where(row >= n, row - n, row), col), 1.0, 0.0).astype(BF16)


def _dot_split(x, tri2):
    hi = x.astype(BF16)
    lo = (x - hi.astype(F32)).astype(BF16)
    return jnp.dot(jnp.concatenate([hi, lo], axis=1), tri2, preferred_element_type=F32)


def _nt(a, b):
    return lax.dot_general(a, b, (((1,), (1,)), ((), ())), preferred_element_type=F32)


def _tn(a, b):
    return lax.dot_general(a, b, (((0,), (0,)), ((), ())), preferred_element_type=F32)


def _heads_per_step(nh):
    return max(h for h in (1, 2, 4) if nh % h == 0)


def _sba_fwd(p, sbw, name, comm=None):
    s = p.shape[0]
    nh = sbw // HEAD_DIM
    hp = _heads_per_step(nh)
    ngrp, hw = nh // hp, hp * HEAD_DIM
    blk = ATT_BLOCK
    nq = s // blk
    scale = 1.0 / math.sqrt(HEAD_DIM)
    host = _Host(comm,
                 [pl.BlockSpec((blk, hw), lambda g, i: (i, g)),
                  pl.BlockSpec((s, hw), lambda g, i: (0, ngrp + g)),
                  pl.BlockSpec((s, hw), lambda g, i: (0, 2 * ngrp + g))],
                 [pl.BlockSpec((blk, hw), lambda g, i: (i, g))] * 2,
                 [jax.ShapeDtypeStruct((s, sbw), F32)] * 2,
                 [pltpu.VMEM((s, hw), BF16)] * 2)

    def body(*refs):
        (q_ref, k_ref, v_ref), (o_ref, lt_ref), (kb_ref, vb_ref) = host.split(refs)
        i = pl.program_id(1)
        step = pl.program_id(0) * nq + i
        host.before(step, ngrp * nq)

        @pl.when(i == 0)
        def _():
            kb_ref[...] = k_ref[...].astype(BF16)
            vb_ref[...] = v_ref[...].astype(BF16)

        heads = [slice(h * HEAD_DIM, (h + 1) * HEAD_DIM) for h in range(hp)]
        qs = [q_ref[:, hd].astype(BF16) for hd in heads]
        later = _tri(blk, lambda r, c: r > c)
        causal = lax.broadcasted_iota(jnp.int32, (blk, blk), 1) < lax.broadcasted_iota(jnp.int32, (blk, blk), 0)

        def key_block(j, carry, diagonal):
            rows = pl.ds(pl.multiple_of(j * blk, blk), blk)
            hs = range(hp)
            z = [_nt(qs[h], kb_ref[rows, heads[h]]) * scale for h in hs]
            ls = [_log_sigmoid(z[h]) for h in hs]
            lm = [jnp.where(causal, ls[h] - z[h], 0.0) if diagonal else ls[h] - z[h] for h in hs]
            stay = [_dot_split(lm[h], later) for h in hs]
            w = [jnp.exp(ls[h] + stay[h] + carry[h][1]) for h in hs]
            if diagonal:
                w = [jnp.where(causal, w[h], 0.0) for h in hs]
            acc = [carry[h][0] + jnp.dot(w[h].astype(BF16), vb_ref[rows, heads[h]], preferred_element_type=F32) for h in hs]
            return tuple((acc[h], carry[h][1] + jnp.sum(lm[h], axis=1, keepdims=True)) for h in hs)

        init = tuple((jnp.zeros((blk, HEAD_DIM), F32), jnp.zeros((blk, 1), F32)) for _ in heads)
        carry = key_block(i, init, True)
        carry = lax.fori_loop(0, i, lambda n, c: key_block(i - 1 - n, c, False), carry)
        for h, hd in enumerate(heads):
            o_ref[:, hd] = carry[h][0]
            lt_ref[:, hd] = jnp.broadcast_to(carry[h][1], (blk, HEAD_DIM))
        host.after(step, ngrp * nq)

    outs = pl.pallas_call(
        body, name=name, grid=(ngrp, nq), in_specs=host.in_specs, out_specs=host.out_specs, out_shape=host.out_shape,
        scratch_shapes=host.scratch, input_output_aliases=host.aliases,
        compiler_params=_cp("arbitrary", "arbitrary"))(p, p, p, *host.args)
    (out, ltot), extra = host.results(outs)
    return out, ltot, extra


def _sba_bwd(p, ltot, dout, sbw, name, comm=None):
    s = p.shape[0]
    nh = sbw // HEAD_DIM
    hp = _heads_per_step(nh)
    ngrp, hw = nh // hp, hp * HEAD_DIM
    blk = ATT_BLOCK
    nq = s // blk
    scale = 1.0 / math.sqrt(HEAD_DIM)
    blk_spec = pl.BlockSpec((blk, hw), lambda g, i: (i, g))
    col_spec = pl.BlockSpec((s, hw), lambda g, i: (0, g))
    host = _Host(comm,
                 [blk_spec, pl.BlockSpec((s, hw), lambda g, i: (0, ngrp + g)),
                  pl.BlockSpec((s, hw), lambda g, i: (0, 2 * ngrp + g)), blk_spec, blk_spec],
                 [blk_spec, col_spec, col_spec], [jax.ShapeDtypeStruct((s, sbw), BF16)] * 3,
                 [pltpu.VMEM((s, hw), BF16)] * 2 + [pltpu.VMEM((s, hw), F32)] * 2)

    def body(*refs):
        (q_ref, k_ref, v_ref, lt_ref, do_ref), (dq_ref, dk_ref, dv_ref), (kb_ref, vb_ref, dka_ref, dva_ref) = host.split(refs)
        i = pl.program_id(1)
        step = pl.program_id(0) * nq + i
        host.before(step, ngrp * nq)

        @pl.when(i == 0)
        def _():
            kb_ref[...] = k_ref[...].astype(BF16)
            vb_ref[...] = v_ref[...].astype(BF16)
            dka_ref[...] = jnp.zeros_like(dka_ref)
            dva_ref[...] = jnp.zeros_like(dva_ref)

        heads = [slice(h * HEAD_DIM, (h + 1) * HEAD_DIM) for h in range(hp)]
        qs = [q_ref[:, hd].astype(BF16) for hd in heads]
        dos = [do_ref[:, hd].astype(BF16) for hd in heads]
        ltots = [lt_ref[:, h * HEAD_DIM:h * HEAD_DIM + 1] for h in range(hp)]
        upto = _tri(blk, lambda r, c: r <= c)
        before = _tri(blk, lambda r, c: r < c)
        causal = lax.broadcasted_iota(jnp.int32, (blk, blk), 1) < lax.broadcasted_iota(jnp.int32, (blk, blk), 0)

        def key_block(j, carry, diagonal):
            rows = pl.ds(pl.multiple_of(j * blk, blk), blk)
            hs = range(hp)
            kj = [kb_ref[rows, heads[h]] for h in hs]
            vj = [vb_ref[rows, heads[h]] for h in hs]
            z = [_nt(qs[h], kj[h]) * scale for h in hs]
            dw = [_nt(dos[h], vj[h]) for h in hs]
            ls = [_log_sigmoid(z[h]) for h in hs]
            lm = [jnp.where(causal, ls[h] - z[h], 0.0) if diagonal else ls[h] - z[h] for h in hs]
            stay = [ltots[h] - carry[h][1] - _dot_split(lm[h], upto) for h in hs]
            w = [jnp.exp(ls[h] + stay[h]) for h in hs]
            if diagonal:
                w = [jnp.where(causal, w[h], 0.0) for h in hs]
            da = [dw[h] * w[h] for h in hs]
            sig = [jnp.exp(ls[h]) for h in hs]
            chain = [sig[h] * (carry[h][2] + _dot_split(da[h], before)) for h in hs]
            if diagonal:
                chain = [jnp.where(causal, chain[h], 0.0) for h in hs]
            dzb = [((da[h] * (1.0 - sig[h]) - chain[h]) * scale).astype(BF16) for h in hs]
            dq = [carry[h][0] + jnp.dot(dzb[h], kj[h], preferred_element_type=F32) for h in hs]
            for h in hs:
                dka_ref[rows, heads[h]] += _tn(dzb[h], qs[h])
            for h in hs:
                dva_ref[rows, heads[h]] += _tn(w[h].astype(BF16), dos[h])
            return tuple((dq[h], carry[h][1] + jnp.sum(lm[h], axis=1, keepdims=True),
                          carry[h][2] + jnp.sum(da[h], axis=1, keepdims=True)) for h in hs)

        zero = jnp.zeros((blk, 1), F32)
        init = tuple((jnp.zeros((blk, HEAD_DIM), F32), zero, zero) for _ in heads)
        carry = lax.fori_loop(0, i, lambda j, c: key_block(j, c, False), init)
        carry = key_block(i, carry, True)
        for h, hd in enumerate(heads):
            dq_ref[:, hd] = carry[h][0].astype(BF16)

        @pl.when(i == nq - 1)
        def _():
            dk_ref[...] = dka_ref[...].astype(BF16)
            dv_ref[...] = dva_ref[...].astype(BF16)

        host.after(step, ngrp * nq)

    outs = pl.pallas_call(
        body, name=name, grid=(ngrp, nq), in_specs=host.in_specs, out_specs=host.out_specs, out_shape=host.out_shape,
        scratch_shapes=host.scratch, input_output_aliases=host.aliases,
        compiler_params=_cp("arbitrary", "arbitrary"))(p, p, p, ltot, dout, *host.args)
    (dq, dk, dv), extra = host.results(outs)
    return dq, dk, dv, extra


def _pool_groups(pad_ref, tile, row0, gd, halo):
    row = row0 + lax.broadcasted_iota(jnp.int32, (tile, 1), 0)
    out = []
    for gi, win in enumerate(POOL_WINDOWS):
        cs = slice(gi * gd, (gi + 1) * gd)
        tok = pad_ref[halo:halo + tile, cs]
        acc = tok
        for j in range(1, win):
            acc = acc + pad_ref[halo - j:halo - j + tile, cs]
        cnt = jnp.minimum(win, row + 1).astype(F32)
        out.append(acc / cnt - tok)
    return out


def _even_mix_fwd(p, att, pool_w, pool_scale, d, name):
    s = p.shape[0]
    half = d // 2
    gd = half // len(POOL_WINDOWS)
    t, hb = ROW_TILE, POOL_HALO

    def body(u_ref, uh_ref, g_ref, a_ref, pw_ref, sc_ref, y_ref, pad_ref):
        i = pl.program_id(0)
        pad_ref[0:hb, :] = jnp.where(i > 0, uh_ref[...], 0.0)
        pad_ref[hb:, :] = u_ref[...]
        pooled = _pool_groups(pad_ref, t, i * t, gd, hb)
        for gi in range(len(POOL_WINDOWS)):
            cs = slice(gi * gd, (gi + 1) * gd)
            po = jnp.dot(pooled[gi].astype(BF16), pw_ref[gi], preferred_element_type=F32) * sc_ref[:, cs]
            y_ref[:, half + gi * gd:half + (gi + 1) * gd] = (po * _silu(g_ref[:, half + gi * gd:half + (gi + 1) * gd])).astype(BF16)
        y_ref[:, :half] = (a_ref[...] * _silu(g_ref[:, :half])).astype(BF16)

    return pl.pallas_call(
        body, name=name, grid=(s // t,),
        in_specs=[pl.BlockSpec((t, half), lambda i: (i, 3)),
                  pl.BlockSpec((hb, half), lambda i: (jnp.maximum(i * (t // hb) - 1, 0), 3)),
                  pl.BlockSpec((t, d), lambda i: (i, 2)),
                  pl.BlockSpec((t, half), lambda i: (i, 0)),
                  pl.BlockSpec(pool_w.shape, lambda i: (0, 0, 0)),
                  pl.BlockSpec((1, half), lambda i: (0, 0))],
        out_specs=pl.BlockSpec((t, d), lambda i: (i, 0)),
        out_shape=jax.ShapeDtypeStruct((s, d), BF16),
        scratch_shapes=[pltpu.VMEM((hb + t, half), F32)],
        compiler_params=_cp("parallel"))(p, p, p, att, pool_w, pool_scale)


def _even_mix_bwd(p, att, dy, pool_w, pool_scale, d, name, comm=None):
    s = p.shape[0]
    half = d // 2
    ng = len(POOL_WINDOWS)
    gd = half // ng
    t, hb = ROW_TILE, POOL_HALO
    nt = s // t
    host = _Host(
        comm,
        [pl.BlockSpec((t, half), lambda i: (i, 3)),
         pl.BlockSpec((hb, half), lambda i: (jnp.maximum(i * (t // hb) - 1, 0), 3)),
         pl.BlockSpec((t, d), lambda i: (i, 2)),
         pl.BlockSpec((hb, half), lambda i: (jnp.minimum((i + 1) * (t // hb), s // hb - 1), 5)),
         pl.BlockSpec((t, half), lambda i: (i, 0)),
         pl.BlockSpec((t, d), lambda i: (i, 0)),
         pl.BlockSpec((hb, half), lambda i: (jnp.minimum((i + 1) * (t // hb), s // hb - 1), 1)),
         pl.BlockSpec(pool_w.shape, lambda i: (0, 0, 0)),
         pl.BlockSpec((1, half), lambda i: (0, 0))],
        [pl.BlockSpec((t, half), lambda i: (i, 0)),
         pl.BlockSpec((t, half), lambda i: (i, 0)),
         pl.BlockSpec((t, d), lambda i: (i, 0)),
         pl.BlockSpec((1, half), lambda i: (0, 0)),
         pl.BlockSpec((ng, gd, gd), lambda i: (0, 0, 0))],
        [jax.ShapeDtypeStruct((s, half), F32), jax.ShapeDtypeStruct((s, half), BF16),
         jax.ShapeDtypeStruct((s, d), BF16), jax.ShapeDtypeStruct((1, half), F32),
         jax.ShapeDtypeStruct((ng, gd, gd), F32)],
        [pltpu.VMEM((hb + t, half), F32), pltpu.VMEM((t + hb, half), F32)])

    def body(*refs):
        ((u_ref, uh_ref, g_ref, gh_ref, a_ref, dy_ref, dyh_ref, pw_ref, sc_ref),
         (da_ref, du_ref, dg_ref, dsc_ref, dpw_ref), (pad_ref, dn_ref)) = host.split(refs)
        i = pl.program_id(0)
        host.before(i, nt)
        first = i == 0
        pad_ref[0:hb, :] = jnp.where(i > 0, uh_ref[...], 0.0)
        pad_ref[hb:, :] = u_ref[...]
        pooled = _pool_groups(pad_ref, t, i * t, gd, hb)
        g1 = g_ref[:, :half]
        dy1 = dy_ref[:, :half]
        da_ref[...] = dy1 * _silu(g1)
        dg_ref[:, :half] = (dy1 * a_ref[...] * _dsilu(g1)).astype(BF16)
        row = i * t + lax.broadcasted_iota(jnp.int32, (t + hb, 1), 0)
        for gi, win in enumerate(POOL_WINDOWS):
            cs = slice(gi * gd, (gi + 1) * gd)
            cs2 = slice(half + gi * gd, half + (gi + 1) * gd)
            w = pw_ref[gi]
            pb = pooled[gi].astype(BF16)
            zp = jnp.dot(pb, w, preferred_element_type=F32)
            g2 = g_ref[:, cs2]
            dy2 = dy_ref[:, cs2]
            dg_ref[:, cs2] = (dy2 * zp * sc_ref[:, cs] * _dsilu(g2)).astype(BF16)
            dpo = dy2 * _silu(g2)
            _acc_rows(dsc_ref.at[:, cs], first, jnp.sum(dpo * zp, axis=0, keepdims=True))
            dz = (dpo * sc_ref[:, cs]).astype(BF16)
            _acc_rows(dpw_ref.at[gi], first, _tn(pb, dz))
            dzh = jnp.where(i < nt - 1, dyh_ref[:, cs] * _silu(gh_ref[:, cs]) * sc_ref[:, cs], 0.0).astype(BF16)
            dpool = _nt(dz, w)
            dpool_h = _nt(dzh, w)
            cnt = jnp.minimum(win, row + 1).astype(F32)
            dn_ref[0:t, cs] = dpool / cnt[0:t]
            dn_ref[t:, cs] = dpool_h / cnt[t:]
            acc = dn_ref[0:t, cs]
            for j in range(1, win):
                acc = acc + dn_ref[j:j + t, cs]
            du_ref[:, cs] = (acc - dpool).astype(BF16)
        host.after(i, nt)

    outs = pl.pallas_call(
        body, name=name, grid=(nt,), in_specs=host.in_specs, out_specs=host.out_specs, out_shape=host.out_shape,
        scratch_shapes=host.scratch, input_output_aliases=host.aliases,
        compiler_params=_cp("arbitrary"))(p, p, p, p, att, dy, dy, pool_w, pool_scale, *host.args)
    return host.results(outs)


def _mm_out_even(y, w, x, g_post, g_pre_next, name):
    s, k = y.shape
    d = w.shape[1]
    t = ROW_TILE

    def body(y_ref, w_ref, x_ref, gp_ref, gn_ref, o_ref, x1_ref, h1_ref):
        for r0 in range(0, t, t // 2):
            rows = slice(r0, r0 + t // 2)
            o = jnp.dot(y_ref[rows, :], w_ref[...], preferred_element_type=F32)
            o_ref[rows, :] = o
            ohat, _ = _rms_stats(o)
            x1 = x_ref[rows, :] + ohat * gp_ref[...]
            x1_ref[rows, :] = x1
            xhat, _ = _rms_stats(x1)
            h1_ref[rows, :] = (xhat * gn_ref[...]).astype(BF16)

    row = lambda c: pl.BlockSpec((t, c), lambda i: (i, 0))
    vec = pl.BlockSpec((1, d), lambda i: (0, 0))
    return pl.pallas_call(
        body, name=name, grid=(s // t,),
        in_specs=[row(k), pl.BlockSpec((k, d), lambda i: (0, 0)), row(d), vec, vec],
        out_specs=[row(d), row(d), row(d)],
        out_shape=[jax.ShapeDtypeStruct((s, d), F32), jax.ShapeDtypeStruct((s, d), F32),
                   jax.ShapeDtypeStruct((s, d), BF16)],
        compiler_params=_cp("parallel"))(y, w, x, g_post, g_pre_next)


def _mm_out_odd(y, w, x1, g_post, target, name):
    s, k = y.shape
    d = w.shape[1]
    t = ROW_TILE

    def body(y_ref, w_ref, x_ref, gp_ref, tg_ref, do_ref, dx_ref, loss_ref, dgp_ref):
        first = pl.program_id(0) == 0
        gp = gp_ref[...]
        part = dgp = None
        for r0 in range(0, t, t // 2):
            rows = slice(r0, r0 + t // 2)
            o = jnp.dot(y_ref[rows, :], w_ref[...], preferred_element_type=F32)
            ohat, r = _rms_stats(o)
            diff = x_ref[rows, :] + ohat * gp - tg_ref[rows, :]
            part_half = 0.5 * jnp.sum(jnp.mean(diff * diff, axis=-1, keepdims=True), axis=0, keepdims=True)
            dx2 = diff * (1.0 / d)
            dx_ref[rows, :] = dx2
            do, dgp_half = _rms_bwd(dx2, ohat, r, gp)
            do_ref[rows, :] = do.astype(BF16)
            part = part_half if part is None else part + part_half
            dgp = dgp_half if dgp is None else dgp + dgp_half
        _acc_rows(loss_ref, first, jnp.broadcast_to(part, loss_ref.shape))
        _acc_rows(dgp_ref, first, dgp)

    row = lambda c: pl.BlockSpec((t, c), lambda i: (i, 0))
    vec = pl.BlockSpec((1, d), lambda i: (0, 0))
    return pl.pallas_call(
        body, name=name, grid=(s // t,),
        in_specs=[row(k), pl.BlockSpec((k, d), lambda i: (0, 0)), row(d), vec, row(d)],
        out_specs=[row(d), row(d), pl.BlockSpec((8, LANES), lambda i: (0, 0)), vec],
        out_shape=[jax.ShapeDtypeStruct((s, d), BF16), jax.ShapeDtypeStruct((s, d), F32),
                   jax.ShapeDtypeStruct((8, LANES), F32), jax.ShapeDtypeStruct((1, d), F32)],
        compiler_params=_cp("arbitrary"))(y, w, x1, g_post, target)


def _layer_norm(d1, cg, cb):
    mu = jnp.mean(d1, axis=-1, keepdims=True)
    cen = d1 - mu
    rstd = lax.rsqrt(jnp.mean(cen * cen, axis=-1, keepdims=True) + EPS)
    n = cen * rstd
    return n, rstd, n * cg + cb


SUBLANES = 8
ROW_STRIP = 64
GATHER_PIECES = 8
CONV_ROWS = 64


def _make_shifts(pad_ref, cs, sh_ref):
    rows = sh_ref.shape[1]
    for r in range(1, SUBLANES):
        sh_ref[r - 1] = pad_ref[r:r + rows, cs]


def _by_shift(taps, base, sign=1):
    return sorted(range(taps), key=lambda k: ((sign * (base + k)) % SUBLANES, k))


def _window(pad_ref, cs, sh_ref, off, t):
    m, r = divmod(off, SUBLANES)
    if r == 0:
        return pad_ref[SUBLANES * m:SUBLANES * m + t, cs]
    return sh_ref[r - 1, SUBLANES * m:SUBLANES * m + t, :]


def _odd_mix_fwd(p, sconv_w, dconv_w, dconv_b, cnorm_g, cnorm_b, d, name):
    s = p.shape[0]
    w = d // 2
    k3, k31 = sconv_w.shape[0], dconv_w.shape[0]
    t, hb = ROW_TILE, CONV_HALO
    assert hb >= k31 - 1 and w % LANES == 0

    def body(p_ref, ph_ref, w3_ref, w31_ref, b31_ref, cg_ref, cb_ref, y_ref, s3_ref, d1_ref, mpad, dpad, sh_ref):
        i = pl.program_id(0)
        mpad[0:hb, :] = jnp.where(i > 0, ph_ref[:, 2 * w:3 * w] * ph_ref[:, 0:w], 0.0)
        mpad[hb:, :] = p_ref[:, 2 * w:3 * w] * p_ref[:, 0:w]
        dpad[0:hb, :] = jnp.where(i > 0, ph_ref[:, 3 * w:4 * w] * _sigmoid(ph_ref[:, 4 * w:5 * w]), 0.0)
        dpad[hb:, :] = p_ref[:, 3 * w:4 * w] * _sigmoid(p_ref[:, 4 * w:5 * w])
        for c0 in range(0, w, LANES):
            cs = slice(c0, c0 + LANES)
            acc = jnp.zeros((t, LANES), F32)
            for kk in range(k3):
                acc = acc + w3_ref[kk:kk + 1, cs] * mpad[hb - (k3 - 1) + kk:hb - (k3 - 1) + kk + t, cs]
            s3_ref[:, cs] = acc
            _make_shifts(dpad, cs, sh_ref)
            for r0 in range(0, t, CONV_ROWS):
                acc = jnp.zeros((CONV_ROWS, LANES), F32)
                for kk in _by_shift(k31, hb - (k31 - 1)):
                    acc = acc + w31_ref[kk:kk + 1, cs] * _window(dpad, cs, sh_ref, hb - (k31 - 1) + kk + r0, CONV_ROWS)
                d1_ref[r0:r0 + CONV_ROWS, cs] = acc + b31_ref[:, cs]
        _, _, d2 = _layer_norm(d1_ref[...], cg_ref[...], cb_ref[...])
        y_ref[:, :w] = (p_ref[:, w:2 * w] * s3_ref[...] * _silu(p_ref[:, 5 * w:6 * w])).astype(BF16)
        y_ref[:, w:] = (_silu(d2) * _silu(p_ref[:, 6 * w:7 * w])).astype(BF16)

    row = lambda c: pl.BlockSpec((t, c), lambda i: (i, 0))
    full = lambda a: pl.BlockSpec(a.shape, lambda i: (0, 0))
    return pl.pallas_call(
        body, name=name, grid=(s // t,),
        in_specs=[row(7 * w),
                  pl.BlockSpec((hb, 5 * w), lambda i: (jnp.maximum(i * (t // hb) - 1, 0), 0)),
                  full(sconv_w), full(dconv_w), full(dconv_b), full(cnorm_g), full(cnorm_b)],
        out_specs=[row(d), row(w), row(w)],
        out_shape=[jax.ShapeDtypeStruct((s, d), BF16), jax.ShapeDtypeStruct((s, w), F32),
                   jax.ShapeDtypeStruct((s, w), F32)],
        scratch_shapes=[pltpu.VMEM((hb + t, w), F32)] * 2 + [pltpu.VMEM((SUBLANES - 1, hb + t - SUBLANES, LANES), F32)],
        compiler_params=_cp("parallel"))(p, p, sconv_w, dconv_w, dconv_b, cnorm_g, cnorm_b)


def _odd_bwd_rows(p, s3, d1, dy, cnorm_g, cnorm_b, d, name, comm=None):
    s = p.shape[0]
    w = d // 2
    t = ROW_TILE
    col = lambda j: pl.BlockSpec((t, w), lambda i: (i, j))
    row = lambda c: pl.BlockSpec((t, c), lambda i: (i, 0))
    vec = pl.BlockSpec((1, w), lambda i: (0, 0))
    host = _Host(comm, [col(1), col(5), col(6), row(w), row(w), row(d), vec, vec],
                 [row(w), row(d), row(w), row(w), vec, vec, vec],
                 [jax.ShapeDtypeStruct((s, w), BF16), jax.ShapeDtypeStruct((s, d), BF16),
                  jax.ShapeDtypeStruct((s, w), F32), jax.ShapeDtypeStruct((s, w), F32)] + [jax.ShapeDtypeStruct((1, w), F32)] * 3, [])

    def body(*refs):
        ((bc_ref, g1_ref, g2_ref, s3_ref, d1_ref, dy_ref, cg_ref, cb_ref),
         (dbc_ref, dg_ref, ds3_ref, dd1_ref, dcg_ref, dcb_ref, db_ref), _) = host.split(refs)
        step = pl.program_id(0)
        host.before(step, s // t)
        first = step == 0

        def strip(j, sums):
            rows = slice(j * ROW_STRIP, (j + 1) * ROW_STRIP)
            g1, g2 = g1_ref[rows, :], g2_ref[rows, :]
            bc, s3v = bc_ref[rows, :], s3_ref[rows, :]
            dy1, dy2 = dy_ref[rows, :w], dy_ref[rows, w:]
            n, rstd, d2 = _layer_norm(d1_ref[rows, :], cg_ref[...], cb_ref[...])
            dg_ref[rows, :w] = (dy1 * bc * s3v * _dsilu(g1)).astype(BF16)
            dg_ref[rows, w:] = (dy2 * _silu(d2) * _dsilu(g2)).astype(BF16)
            dco = dy1 * _silu(g1)
            dbc_ref[rows, :] = (dco * s3v).astype(BF16)
            ds3_ref[rows, :] = dco * bc
            dd2 = dy2 * _silu(g2) * _dsilu(d2)
            dn = dd2 * cg_ref[...]
            dd1 = rstd * (dn - jnp.mean(dn, axis=-1, keepdims=True) - n * jnp.mean(dn * n, axis=-1, keepdims=True))
            dd1_ref[rows, :] = dd1
            dcb, dcg, db = sums
            return (dcb + jnp.sum(dd2, axis=0, keepdims=True), dcg + jnp.sum(dd2 * n, axis=0, keepdims=True),
                    db + jnp.sum(dd1, axis=0, keepdims=True))

        zero = jnp.zeros((1, w), F32)
        sums = (zero, zero, zero)
        for j in range(t // ROW_STRIP):
            sums = strip(j, sums)
        dcb, dcg, db = sums
        _acc_rows(dcb_ref, first, dcb)
        _acc_rows(dcg_ref, first, dcg)
        _acc_rows(db_ref, first, db)
        host.after(step, s // t)

    outs = pl.pallas_call(
        body, name=name, grid=(s // t,), in_specs=host.in_specs, out_specs=host.out_specs, out_shape=host.out_shape,
        scratch_shapes=host.scratch, input_output_aliases=host.aliases,
        compiler_params=_cp("arbitrary"))(p, p, p, s3, d1, dy, cnorm_g, cnorm_b, *host.args)
    return host.results(outs)


def _odd_bwd_conv(p, ds3, dd1, sconv_w, dconv_w, d, name):
    s = p.shape[0]
    w = d // 2
    k3, k31 = sconv_w.shape[0], dconv_w.shape[0]
    t, hb, ha = ROW_TILE, CONV_HALO, 8
    nt = s // t
    assert hb >= k31 - 1 and ha >= k3 - 1

    def body(hc_ref, cc_ref, ga_ref, gb_ref, hch_ref, cch_ref, gah_ref, gbh_ref, ds3_ref, ds3h_ref, dd1_ref, dd1h_ref,
             w3_ref, w31_ref, dhc_ref, dcc_ref, dga_ref, dgb_ref, dw3_ref, dw31_ref, mpad, dpad, s3pad, d1pad, sh_ref):
        i = pl.program_id(0)
        first = i == 0
        last = i == nt - 1
        mpad[0:hb, :] = jnp.where(i > 0, cch_ref[...] * hch_ref[...], 0.0)
        mpad[hb:, :] = cc_ref[...] * hc_ref[...]
        dpad[0:hb, :] = jnp.where(i > 0, gah_ref[...] * _sigmoid(gbh_ref[...]), 0.0)
        dpad[hb:, :] = ga_ref[...] * _sigmoid(gb_ref[...])
        s3pad[0:t, :] = ds3_ref[...]
        s3pad[t:, :] = jnp.where(last, 0.0, ds3h_ref[...])
        d1pad[0:t, :] = dd1_ref[...]
        d1pad[t:, :] = jnp.where(last, 0.0, dd1h_ref[...])

        @pl.when(first)
        def _():
            dw3_ref[...] = jnp.zeros_like(dw3_ref)
            dw31_ref[...] = jnp.zeros_like(dw31_ref)

        def fold(v):
            return jnp.sum(v.reshape(v.shape[0] // SUBLANES, SUBLANES, LANES), axis=0)

        groups = range(0, t, CONV_ROWS)
        for c0 in range(0, w, LANES):
            cs = slice(c0, c0 + LANES)
            ds3v = s3pad[0:t, cs]
            dm = jnp.zeros((t, LANES), F32)
            for kk in range(k3):
                dm = dm + w3_ref[kk:kk + 1, cs] * s3pad[k3 - 1 - kk:k3 - 1 - kk + t, cs]
                off = hb - (k3 - 1) + kk
                dw3_ref[SUBLANES * kk:SUBLANES * (kk + 1), cs] += fold(ds3v * mpad[off:off + t, cs])
            dcc_ref[:, cs] = (dm * hc_ref[:, cs]).astype(BF16)
            dhc_ref[:, cs] = (dm * cc_ref[:, cs]).astype(BF16)
            _make_shifts(d1pad, cs, sh_ref)
            for r0 in groups:
                rows = slice(r0, r0 + CONV_ROWS)
                dd0 = jnp.zeros((CONV_ROWS, LANES), F32)
                for kk in _by_shift(k31, -(k31 - 1), -1):
                    dd0 = dd0 + w31_ref[kk:kk + 1, cs] * _window(d1pad, cs, sh_ref, k31 - 1 - kk + r0, CONV_ROWS)
                sgb = _sigmoid(gb_ref[rows, cs])
                dga_ref[rows, cs] = (dd0 * sgb).astype(BF16)
                dgb_ref[rows, cs] = (dd0 * ga_ref[rows, cs] * sgb * (1.0 - sgb)).astype(BF16)
            _make_shifts(dpad, cs, sh_ref)
            for kk in _by_shift(k31, hb - (k31 - 1)):
                part = jnp.zeros((SUBLANES, LANES), F32)
                for r0 in groups:
                    part = part + fold(d1pad[r0:r0 + CONV_ROWS, cs]
                                       * _window(dpad, cs, sh_ref, hb - (k31 - 1) + kk + r0, CONV_ROWS))
                dw31_ref[SUBLANES * kk:SUBLANES * (kk + 1), cs] += part

    col = lambda j: pl.BlockSpec((t, w), lambda i: (i, j))
    pre = lambda j: pl.BlockSpec((hb, w), lambda i: (jnp.maximum(i * (t // hb) - 1, 0), j))
    row = pl.BlockSpec((t, w), lambda i: (i, 0))
    post = lambda h: pl.BlockSpec((h, w), lambda i: (jnp.minimum((i + 1) * (t // h), s // h - 1), 0))
    full = lambda a: pl.BlockSpec(a.shape, lambda i: (0, 0))
    dhc, dcc, dga, dgb, dw3, dw31 = pl.pallas_call(
        body, name=name, grid=(nt,),
        in_specs=[col(0), col(2), col(3), col(4), pre(0), pre(2), pre(3), pre(4),
                  row, post(ha), row, post(hb), full(sconv_w), full(dconv_w)],
        out_specs=[row, row, row, row, pl.BlockSpec((SUBLANES * k3, w), lambda i: (0, 0)),
                   pl.BlockSpec((SUBLANES * k31, w), lambda i: (0, 0))],
        out_shape=[jax.ShapeDtypeStruct((s, w), BF16)] * 4
        + [jax.ShapeDtypeStruct((SUBLANES * k3, w), F32), jax.ShapeDtypeStruct((SUBLANES * k31, w), F32)],
        scratch_shapes=[pltpu.VMEM((hb + t, w), F32)] * 2 + [pltpu.VMEM((t + ha, w), F32), pltpu.VMEM((t + hb, w), F32),
                                                             pltpu.VMEM((SUBLANES - 1, hb + t - SUBLANES, LANES), F32)],
        compiler_params=_cp("arbitrary"))(p, p, p, p, p, p, p, p, ds3, ds3, dd1, dd1, sconv_w, dconv_w)
    return dhc, dcc, dga, dgb, jnp.sum(dw3.reshape(k3, SUBLANES, w), axis=1), jnp.sum(dw31.reshape(k31, SUBLANES, w), axis=1)


def _mm_in_bwd(dp, w3, x, g_pre, dres, post, name, comm=None):
    s = dp.shape[0]
    nsh, d, ns = w3.shape
    t = 512 if s % 512 == 0 else ROW_TILE
    nt = s // t
    ks = 2 if (ns // 2) % LANES == 0 else 1
    nk, kw = nsh * ks, ns // ks
    chunk = 128
    nchunk = t // chunk
    row = pl.BlockSpec((t, d), lambda i, k: (i, 0))
    vec = pl.BlockSpec((1, d), lambda i, k: (0, 0))
    rowwise = [x, dres] + ([post[0]] if post is not None else [])
    in_specs = [pl.BlockSpec((t, kw), lambda i, k: (i, k)), pl.BlockSpec((None, d, kw), lambda i, k: (k // ks, 0, k % ks)), vec]
    out_specs = [row, vec]
    out_shape = [jax.ShapeDtypeStruct((s, d), F32), jax.ShapeDtypeStruct((1, d), F32)]
    args = [dp, w3, g_pre]
    if post is not None:
        in_specs += [vec]
        out_specs += [row, vec]
        out_shape += [jax.ShapeDtypeStruct((s, d), BF16), jax.ShapeDtypeStruct((1, d), F32)]
        args += [post[1]]
    n_blocked = len(in_specs)
    in_specs += [ANY] * len(rowwise)
    args += rowwise
    host = _Host(comm, in_specs, out_specs, out_shape,
                 [pltpu.VMEM((t, d), F32), pltpu.VMEM((len(rowwise), 2, chunk, d), F32), pltpu.SemaphoreType.DMA((len(rowwise), 2))])

    def body(*refs):
        ins, outs, (acc_ref, buf_ref, sem_ref) = host.split(refs)
        dp_ref, w_ref, g_ref = ins[:3]
        hbm = ins[n_blocked:]
        dx_ref, dg_ref = outs[:2]
        tile = pl.program_id(0)
        kk = pl.program_id(1)
        first = tile == 0
        step = tile * nk + kk
        host.before(step, nt * nk)
        part = _nt(dp_ref[...], w_ref[...])

        @pl.when(kk == 0)
        def _():
            acc_ref[...] = part

        @pl.when(kk > 0)
        def _():
            acc_ref[...] += part

        def fetch(ci, slot):
            return [pltpu.make_async_copy(src.at[pl.ds(tile * t + ci * chunk, chunk)], buf_ref.at[n, slot], sem_ref.at[n, slot])
                    for n, src in enumerate(hbm)]

        @pl.when(kk == nk - 1)
        def _():
            dg = dgp = None
            for cp in fetch(0, 0):
                cp.start()
            for ci in range(nchunk):
                slot = ci % 2
                if ci + 1 < nchunk:
                    for cp in fetch(ci + 1, 1 - slot):
                        cp.start()
                for cp in fetch(ci, slot):
                    cp.wait()
                rows = slice(ci * chunk, (ci + 1) * chunk)
                xhat, r = _rms_stats(buf_ref[0, slot])
                dxn, dg_part = _rms_bwd(acc_ref[rows, :], xhat, r, g_ref[...])
                dx = buf_ref[1, slot] + dxn
                dx_ref[rows, :] = dx
                dg = dg_part if dg is None else dg + dg_part
                if post is not None:
                    ohat, ro = _rms_stats(buf_ref[2, slot])
                    do, dgp_part = _rms_bwd(dx, ohat, ro, ins[3][...])
                    outs[2][rows, :] = do.astype(BF16)
                    dgp = dgp_part if dgp is None else dgp + dgp_part
            _acc_rows(dg_ref, first, dg)
            if post is not None:
                _acc_rows(outs[3], first, dgp)

        host.after(step, nt * nk)

    res = pl.pallas_call(
        body, name=name, grid=(nt, nk), in_specs=host.in_specs, out_specs=host.out_specs, out_shape=host.out_shape,
        scratch_shapes=host.scratch, input_output_aliases=host.aliases,
        compiler_params=_cp("arbitrary", "arbitrary"))(*args, *host.args)
    return host.results(res)


def _half_add(g, r1, c_arr, name):
    nsh, rows, ns = g.shape
    h = rows // 2
    tr = min(ROW_TILE, h)
    per = h // tr

    def body(c_ref, g_ref, r_ref, o_ref):
        o_ref[...] = (g_ref[...].astype(F32) + r_ref[...].astype(F32)).astype(BF16)

    spec = pl.BlockSpec((None, tr, ns), lambda s, r, c: (s, r, 0))
    return pl.pallas_call(
        body, name=name,
        grid_spec=pltpu.PrefetchScalarGridSpec(
            num_scalar_prefetch=1, grid=(nsh, per),
            in_specs=[pl.BlockSpec((None, tr, ns), lambda s, r, c: (s, c[0] * per + r, 0)), spec], out_specs=spec),
        out_shape=jax.ShapeDtypeStruct((nsh, h, ns), BF16), compiler_params=_cp("parallel", "parallel"))(c_arr, g, r1)


def _sum_chips(hh, r2, mc_arr, name, after=None):
    _, h, ns = hh.shape
    tr = min(ROW_TILE, h)
    per = h // tr

    def body(mc_ref, h_ref, a_ref, b_ref, c_ref, *rest):
        rest[-1][...] = ((h_ref[...].astype(F32) + a_ref[...].astype(F32)) + b_ref[...].astype(F32)) + c_ref[...].astype(F32)

    got = lambda k: pl.BlockSpec((None, tr, ns), lambda r, mc: (k, r, 0))
    ordering = [] if after is None else [after]
    return pl.pallas_call(
        body, name=name,
        grid_spec=pltpu.PrefetchScalarGridSpec(
            num_scalar_prefetch=1, grid=(per,),
            in_specs=[pl.BlockSpec((None, tr, ns), lambda r, mc: (mc[0], r, 0)), got(0), got(1), got(2)] + [ANY] * len(ordering),
            out_specs=pl.BlockSpec((tr, ns), lambda r, mc: (mc[1] * per + r, 0))),
        out_shape=jax.ShapeDtypeStruct((2 * h, ns), F32), compiler_params=_cp("parallel"))(mc_arr, hh, r2, r2, r2, *ordering)


def _add2(a, b, name):
    def body(a_ref, b_ref, o_ref):
        o_ref[...] = a_ref[...] + b_ref[...]

    return pl.pallas_call(body, name=name, out_shape=jax.ShapeDtypeStruct(a.shape, a.dtype), compiler_params=_cp())(a, b)


def _sum_chips_ordered(s2, r2, mc_arr, name):
    rows, w = s2.shape
    rh = rows // 2

    def body(mc_ref, s_ref, a_ref, b_ref, c_ref, o_ref):
        me = mc_ref[0]
        acc = None
        for j in range(N_CHIPS):
            rel = jnp.bitwise_xor(me, j)
            v = jnp.where(rel == 0, s_ref[...], jnp.where(rel == 2, a_ref[...], jnp.where(rel == 1, b_ref[...], c_ref[...])))
            acc = v if acc is None else acc + v
        o_ref[...] = acc

    got = lambda k: pl.BlockSpec((None, rh, w), lambda i, mc: (k, 0, 0))
    return pl.pallas_call(
        body, name=name,
        grid_spec=pltpu.PrefetchScalarGridSpec(
            num_scalar_prefetch=1, grid=(1,),
            in_specs=[pl.BlockSpec((rh, w), lambda i, mc: (mc[1], 0)), got(0), got(1), got(2)],
            out_specs=pl.BlockSpec((rh, w), lambda i, mc: (mc[1], 0))),
        out_shape=jax.ShapeDtypeStruct((rows, w), F32), compiler_params=_cp("arbitrary"))(mc_arr, s2, r2, r2, r2)


def _adamw(w, g, m, v, name, comm=None):
    r, c = w.shape
    tr = ROW_TILE if r % ROW_TILE == 0 else r
    c1 = 1.0 / (1.0 - ADAM_B1 ** ADAM_STEP)
    c2 = 1.0 / (1.0 - ADAM_B2 ** ADAM_STEP)
    spec = pl.BlockSpec((tr, c), lambda i: (i, 0))
    host = _Host(comm, [spec] * 4, [spec] * 4, [jax.ShapeDtypeStruct((r, c), F32)] * 4, [])

    def body(*refs):
        (w_ref, g_ref, m_ref, v_ref), (go_ref, d_ref, nm_ref, nv_ref), _ = host.split(refs)
        step = pl.program_id(0)
        host.before(step, r // tr)
        gv = g_ref[...]
        go_ref[...] = gv
        nm = ADAM_B1 * m_ref[...] + (1.0 - ADAM_B1) * gv
        nv = ADAM_B2 * v_ref[...] + (1.0 - ADAM_B2) * (gv * gv)
        nm_ref[...] = nm
        nv_ref[...] = nv
        d_ref[...] = -ADAM_LR * ((nm * c1) / (jnp.sqrt(nv * c2) + ADAM_EPS) + ADAM_WD * w_ref[...])
        host.after(step, r // tr)

    outs = pl.pallas_call(
        body, name=name, grid=(r // tr,), in_specs=host.in_specs, out_specs=host.out_specs, out_shape=host.out_shape,
        scratch_shapes=host.scratch, input_output_aliases=host.aliases,
        compiler_params=_cp("arbitrary"))(w, g, m, v, *host.args)
    return host.results(outs)


def _gather_weights(bigs, pool_w, pack_w, pack_d, name):
    nb = len(bigs)
    smalls = [pool_w, pack_w, pack_d]
    q, cw, cd = pool_w.shape[1], pack_w.shape[1], pack_d.shape[1]
    pieces = [_GatherPlan(bigs, (j, j + 1, GATHER_PIECES)) for j in range(GATHER_PIECES)]
    for j, piece in enumerate(pieces):
        piece.base = 9 + j * piece.nsems

    def body(*refs):
        srcs, dsts = refs[:nb + 3], refs[nb + 3:2 * (nb + 3)]
        ssem, rsem, lsem = refs[2 * (nb + 3):]
        x, y, c, me, chips, sib = _place()

        def small_dst(n, chip):
            if n == 0:
                return dsts[nb].at[:, pl.ds(chip * q, q), :]
            return dsts[nb + n].at[:, pl.ds(chip * (cw if n == 1 else cd), cw if n == 1 else cd)]

        local = [pltpu.make_async_copy(srcs[nb + n], small_dst(n, me), lsem.at[n]) for n in range(3)]
        for cp in local:
            cp.start()
        sends = []
        for n in range(3):
            for k, chip in enumerate(chips):
                cp = _rcopy(srcs[nb + n], small_dst(n, me), ssem.at[3 * n + k], rsem.at[3 * n + k], (*chip, c))
                cp.start()
                sends.append(cp)
        big = (srcs[:nb], dsts[:nb], ssem, rsem)
        for stage in ("start", "relay", "relay_far", "finish"):
            for piece in pieces:
                getattr(piece, stage)(*big)
        for n in range(3):
            for k, chip in enumerate(chips):
                ref = small_dst(n, 2 * chip[0] + chip[1])
                _rcopy(ref, ref, ssem.at[3 * n + k], rsem.at[3 * n + k], (*chip, c)).wait_recv()
        for cp in sends:
            cp.wait_send()
        for cp in local:
            cp.wait()

    nsem = 9 + sum(piece.nsems for piece in pieces)
    out_shape = [jax.ShapeDtypeStruct(b.shape, b.dtype) for b in bigs]
    out_shape += [jax.ShapeDtypeStruct((pool_w.shape[0], N_CHIPS * q, pool_w.shape[2]), pool_w.dtype),
                  jax.ShapeDtypeStruct((pack_w.shape[0], N_CHIPS * cw), pack_w.dtype),
                  jax.ShapeDtypeStruct((pack_d.shape[0], N_CHIPS * cd), pack_d.dtype)]
    return pl.pallas_call(
        body, name=name, in_specs=[ANY] * (nb + 3), out_specs=[ANY] * (nb + 3), out_shape=out_shape,
        input_output_aliases={a: a for a in range(nb)},
        scratch_shapes=[pltpu.SemaphoreType.DMA((nsem,)), pltpu.SemaphoreType.DMA((nsem,)), pltpu.SemaphoreType.DMA((3,))],
        compiler_params=pltpu.CompilerParams(has_side_effects=True))(*bigs, *smalls)


def _swap_with_sibling(grads, wholes, name):
    n, nw = len(grads), len(wholes)
    halves = [g.shape[1] // 2 for g in grads]

    def body(*refs):
        srcs, dsts = refs[:n + nw], refs[n + nw:2 * (n + nw)]
        ssem, rsem = refs[2 * (n + nw):]
        x, y, c, me, chips, sib = _place()
        cps = [_rcopy(srcs[a].at[:, pl.ds((1 - c) * halves[a], halves[a]), :], dsts[a], ssem.at[a], rsem.at[a], sib)
               for a in range(n)]
        cps += [_rcopy(srcs[a], dsts[a], ssem.at[a], rsem.at[a], sib) for a in range(n, n + nw)]
        for cp in cps:
            cp.start()
        for cp in cps:
            cp.wait_recv()
        for cp in cps:
            cp.wait_send()

    out_shape = [jax.ShapeDtypeStruct((g.shape[0], h, g.shape[2]), g.dtype) for g, h in zip(grads, halves)]
    out_shape += [jax.ShapeDtypeStruct(w.shape, w.dtype) for w in wholes]
    return pl.pallas_call(
        body, name=name, in_specs=[ANY] * (n + nw), out_specs=[ANY] * (n + nw), out_shape=out_shape,
        scratch_shapes=[pltpu.SemaphoreType.DMA((n + nw,)), pltpu.SemaphoreType.DMA((n + nw,))],
        compiler_params=pltpu.CompilerParams(has_side_effects=True))(*grads, *wholes)


def _scatter_to_chips(halves_in, small, name):
    n = len(halves_in)
    rh = small.shape[0] // 2

    def body(*refs):
        srcs, dsts = refs[:n + 1], refs[n + 1:2 * (n + 1)]
        ssem, rsem = refs[2 * (n + 1):]
        x, y, c, me, chips, sib = _place()
        cps = []
        for a in range(n + 1):
            for k, chip in enumerate(chips):
                src = srcs[a].at[2 * chip[0] + chip[1]] if a < n else srcs[a].at[pl.ds(c * rh, rh)]
                cps.append(_rcopy(src, dsts[a].at[k], ssem.at[3 * a + k], rsem.at[3 * a + k], (*chip, c)))
        for cp in cps:
            cp.start()
        for cp in cps:
            cp.wait_recv()
        for cp in cps:
            cp.wait_send()

    out_shape = [jax.ShapeDtypeStruct((3,) + h.shape[1:], h.dtype) for h in halves_in]
    out_shape.append(jax.ShapeDtypeStruct((3, rh, small.shape[1]), small.dtype))
    return pl.pallas_call(
        body, name=name, in_specs=[ANY] * (n + 1), out_specs=[ANY] * (n + 1), out_shape=out_shape,
        scratch_shapes=[pltpu.SemaphoreType.DMA((3 * (n + 1),)), pltpu.SemaphoreType.DMA((3 * (n + 1),))],
        compiler_params=pltpu.CompilerParams(has_side_effects=True))(*halves_in, small)


def _join_halves(parts, name):
    n = len(parts)

    def body(*refs):
        srcs, dsts = refs[:n], refs[n:2 * n]
        ssem, rsem = refs[2 * n:]
        x, y, c, me, chips, sib = _place()
        cps = []
        for a in range(n):
            h = srcs[a].shape[0] // 2
            cps.append(_rcopy(srcs[a].at[pl.ds(c * h, h)], dsts[a].at[pl.ds(c * h, h)], ssem.at[a], rsem.at[a], sib))
        for cp in cps:
            cp.start()
        for a in range(n):
            h = srcs[a].shape[0] // 2
            theirs = dsts[a].at[pl.ds((1 - c) * h, h)]
            _rcopy(theirs, theirs, ssem.at[a], rsem.at[a], sib).wait_recv()
        for cp in cps:
            cp.wait_send()

    out_shape = [jax.ShapeDtypeStruct(p.shape, p.dtype) for p in parts]
    return pl.pallas_call(
        body, name=name, in_specs=[ANY] * n, out_specs=[ANY] * n, out_shape=out_shape,
        input_output_aliases={a: a for a in range(n)},
        scratch_shapes=[pltpu.SemaphoreType.DMA((n,)), pltpu.SemaphoreType.DMA((n,))],
        compiler_params=pltpu.CompilerParams(has_side_effects=True))(*parts)


def _scatter_start(h, name):
    land = (3,) + h.shape[1:]

    def body(h_ref, land_ref, send_sems, recv_sems, h_thru, land_thru, token):
        x, y, c, me, chips, sib = _place()
        for k, chip in enumerate(chips):
            _rcopy(h_ref.at[2 * chip[0] + chip[1]], land_ref.at[k], send_sems.at[k], recv_sems.at[k], (*chip, c)).start()
        token[...] = jnp.zeros_like(token)

    hbm = pl.BlockSpec(memory_space=pltpu.HBM)
    sem = pl.BlockSpec(memory_space=pltpu.SEMAPHORE)
    return pl.pallas_call(
        body, name=name,
        out_shape=(pltpu.SemaphoreType.DMA((3,)), pltpu.SemaphoreType.DMA((3,)), pltpu.HBM(h.shape, h.dtype),
                   pltpu.HBM(land, h.dtype), jax.ShapeDtypeStruct((8, LANES), F32)),
        in_specs=(hbm, hbm), out_specs=(sem, sem, hbm, hbm, pl.BlockSpec(memory_space=pltpu.VMEM)),
        input_output_aliases={0: 2, 1: 3},
        compiler_params=pltpu.CompilerParams(has_side_effects=pltpu.SideEffectType.DATAFLOW_SIDE_EFFECTING))(
            pltpu.with_memory_space_constraint(h, pltpu.HBM),
            pltpu.with_memory_space_constraint(lax.empty(land, h.dtype), pltpu.HBM))


def _scatter_wait(send_sems, recv_sems, h_thru, land_thru, after, name):
    def body(h_ref, land_ref, send_sems, recv_sems, after_ref, h_dead, got_ref):
        x, y, c, me, chips, sib = _place()
        for k, chip in enumerate(chips):
            cp = _rcopy(h_ref.at[2 * chip[0] + chip[1]], land_ref.at[k], send_sems.at[k], recv_sems.at[k], (*chip, c))
            cp.wait_send()
            cp.wait_recv()

    hbm = pl.BlockSpec(memory_space=pltpu.HBM)
    sem = pl.BlockSpec(memory_space=pltpu.SEMAPHORE)
    return pl.pallas_call(
        body, name=name,
        out_shape=(pltpu.HBM(h_thru.shape, h_thru.dtype), pltpu.HBM(land_thru.shape, land_thru.dtype)),
        in_specs=(hbm, hbm, sem, sem, ANY), out_specs=(hbm, hbm), input_output_aliases={0: 0, 1: 1},
        compiler_params=pltpu.CompilerParams(has_side_effects=pltpu.SideEffectType.DATAFLOW_SIDE_EFFECTING))(
            h_thru, land_thru, send_sems, recv_sems, after)


def _swap_start(g, name):
    h = g.shape[1] // 2
    land = (g.shape[0], h, g.shape[2])

    def body(g_ref, land_ref, send_sem, recv_sem, g_thru, land_thru, token):
        x, y, c, me, chips, sib = _place()
        _rcopy(g_ref.at[:, pl.ds((1 - c) * h, h), :], land_ref, send_sem.at[0], recv_sem.at[0], sib).start()
        token[...] = jnp.zeros_like(token)

    hbm = pl.BlockSpec(memory_space=pltpu.HBM)
    sem = pl.BlockSpec(memory_space=pltpu.SEMAPHORE)
    return pl.pallas_call(
        body, name=name,
        out_shape=(pltpu.SemaphoreType.DMA((1,)), pltpu.SemaphoreType.DMA((1,)), pltpu.HBM(g.shape, g.dtype),
                   pltpu.HBM(land, g.dtype), jax.ShapeDtypeStruct((8, LANES), F32)),
        in_specs=(hbm, hbm), out_specs=(sem, sem, hbm, hbm, pl.BlockSpec(memory_space=pltpu.VMEM)),
        input_output_aliases={0: 2, 1: 3},
        compiler_params=pltpu.CompilerParams(has_side_effects=pltpu.SideEffectType.DATAFLOW_SIDE_EFFECTING))(
            pltpu.with_memory_space_constraint(g, pltpu.HBM),
            pltpu.with_memory_space_constraint(lax.empty(land, g.dtype), pltpu.HBM))


def _swap_wait(send_sem, recv_sem, g_thru, land_thru, after, name):
    h = g_thru.shape[1] // 2

    def body(g_ref, land_ref, send_sem, recv_sem, after_ref, g_dead, got_ref):
        x, y, c, me, chips, sib = _place()
        cp = _rcopy(g_ref.at[:, pl.ds((1 - c) * h, h), :], land_ref, send_sem.at[0], recv_sem.at[0], sib)
        cp.wait_send()
        cp.wait_recv()

    hbm = pl.BlockSpec(memory_space=pltpu.HBM)
    sem = pl.BlockSpec(memory_space=pltpu.SEMAPHORE)
    return pl.pallas_call(
        body, name=name,
        out_shape=(pltpu.HBM(g_thru.shape, g_thru.dtype), pltpu.HBM(land_thru.shape, land_thru.dtype)),
        in_specs=(hbm, hbm, sem, sem, ANY), out_specs=(hbm, hbm), input_output_aliases={0: 0, 1: 1},
        compiler_params=pltpu.CompilerParams(has_side_effects=pltpu.SideEffectType.DATAFLOW_SIDE_EFFECTING))(
            g_thru, land_thru, send_sem, recv_sem, after)


def _share_half_start(small, name):
    rh = small.shape[0] // 2
    land = (3, rh, small.shape[1])

    def body(s_ref, land_ref, send_sems, recv_sems, s_thru, land_thru, token):
        x, y, c, me, chips, sib = _place()
        for k, chip in enumerate(chips):
            _rcopy(s_ref.at[pl.ds(c * rh, rh)], land_ref.at[k], send_sems.at[k], recv_sems.at[k], (*chip, c)).start()
        token[...] = jnp.zeros_like(token)

    hbm = pl.BlockSpec(memory_space=pltpu.HBM)
    sem = pl.BlockSpec(memory_space=pltpu.SEMAPHORE)
    return pl.pallas_call(
        body, name=name,
        out_shape=(pltpu.SemaphoreType.DMA((3,)), pltpu.SemaphoreType.DMA((3,)), pltpu.HBM(small.shape, small.dtype),
                   pltpu.HBM(land, small.dtype), jax.ShapeDtypeStruct((8, LANES), F32)),
        in_specs=(hbm, hbm), out_specs=(sem, sem, hbm, hbm, pl.BlockSpec(memory_space=pltpu.VMEM)),
        input_output_aliases={0: 2, 1: 3},
        compiler_params=pltpu.CompilerParams(has_side_effects=pltpu.SideEffectType.DATAFLOW_SIDE_EFFECTING))(
            pltpu.with_memory_space_constraint(small, pltpu.HBM),
            pltpu.with_memory_space_constraint(lax.empty(land, small.dtype), pltpu.HBM))


def _share_half_wait(send_sems, recv_sems, s_thru, land_thru, after, name):
    rh = s_thru.shape[0] // 2

    def body(s_ref, land_ref, send_sems, recv_sems, after_ref, s_dead, got_ref):
        x, y, c, me, chips, sib = _place()
        for k, chip in enumerate(chips):
            cp = _rcopy(s_ref.at[pl.ds(c * rh, rh)], land_ref.at[k], send_sems.at[k], recv_sems.at[k], (*chip, c))
            cp.wait_send()
            cp.wait_recv()

    hbm = pl.BlockSpec(memory_space=pltpu.HBM)
    sem = pl.BlockSpec(memory_space=pltpu.SEMAPHORE)
    return pl.pallas_call(
        body, name=name,
        out_shape=(pltpu.HBM(s_thru.shape, s_thru.dtype), pltpu.HBM(land_thru.shape, land_thru.dtype)),
        in_specs=(hbm, hbm, sem, sem, ANY), out_specs=(hbm, hbm), input_output_aliases={0: 0, 1: 1},
        compiler_params=pltpu.CompilerParams(has_side_effects=pltpu.SideEffectType.DATAFLOW_SIDE_EFFECTING))(
            s_thru, land_thru, send_sems, recv_sems, after)


def _join_start(parts, name):
    n = len(parts)

    def body(*refs):
        srcs, (send_sems, recv_sems), token = refs[:n], refs[n:n + 2], refs[-1]
        x, y, c, me, chips, sib = _place()
        for a, src in enumerate(srcs):
            h = src.shape[0] // 2
            mine = src.at[pl.ds(c * h, h)]
            _rcopy(mine, mine, send_sems.at[a], recv_sems.at[a], sib).start()
        token[...] = jnp.zeros_like(token)

    hbm = pl.BlockSpec(memory_space=pltpu.HBM)
    sem = pl.BlockSpec(memory_space=pltpu.SEMAPHORE)
    outs = pl.pallas_call(
        body, name=name,
        out_shape=(pltpu.SemaphoreType.DMA((n,)), pltpu.SemaphoreType.DMA((n,)))
        + tuple(pltpu.HBM(p.shape, p.dtype) for p in parts) + (jax.ShapeDtypeStruct((8, LANES), F32),),
        in_specs=(hbm,) * n, out_specs=(sem, sem) + (hbm,) * n + (pl.BlockSpec(memory_space=pltpu.VMEM),),
        input_output_aliases={a: 2 + a for a in range(n)},
        compiler_params=pltpu.CompilerParams(has_side_effects=pltpu.SideEffectType.DATAFLOW_SIDE_EFFECTING))(
            *[pltpu.with_memory_space_constraint(p, pltpu.HBM) for p in parts])
    return outs[0], outs[1], list(outs[2:2 + n]), outs[-1]


def _join_wait(send_sems, recv_sems, parts, after, name):
    n = len(parts)

    def body(*refs):
        srcs, (send_sems, recv_sems) = refs[:n], refs[n:n + 2]
        x, y, c, me, chips, sib = _place()
        for a, src in enumerate(srcs):
            h = src.shape[0] // 2
            mine, theirs = src.at[pl.ds(c * h, h)], src.at[pl.ds((1 - c) * h, h)]
            _rcopy(mine, theirs, send_sems.at[a], recv_sems.at[a], sib).wait_send()
            _rcopy(theirs, theirs, send_sems.at[a], recv_sems.at[a], sib).wait_recv()

    hbm = pl.BlockSpec(memory_space=pltpu.HBM)
    sem = pl.BlockSpec(memory_space=pltpu.SEMAPHORE)
    return pl.pallas_call(
        body, name=name, out_shape=tuple(pltpu.HBM(p.shape, p.dtype) for p in parts),
        in_specs=(hbm,) * n + (sem, sem, ANY), out_specs=(hbm,) * n, input_output_aliases={a: a for a in range(n)},
        compiler_params=pltpu.CompilerParams(has_side_effects=pltpu.SideEffectType.DATAFLOW_SIDE_EFFECTING))(
            *parts, send_sems, recv_sems, after)


def _pad_rows(a, rows):
    return jnp.pad(a, ((0, rows - a.shape[0]), (0, 0)))


def _stack_rows(parts, multiple):
    padded = [_pad_rows(p, -(-p.shape[0] // 8) * 8) for p in parts]
    starts, at = [], 0
    for p in padded:
        starts.append(at)
        at += p.shape[0]
    total = -(-at // multiple) * multiple
    if total > at:
        padded.append(jnp.zeros((total - at, parts[0].shape[1]), parts[0].dtype))
    return jnp.concatenate(padded, axis=0), starts


def kernel(x, ln_pre_even, w_in_even, pool_w, pool_scale, w_out_even, ln_post_even, ln_pre_odd, w_in_odd, sconv_w, dconv_w, dconv_b, cnorm_g, cnorm_b, w_out_odd, ln_post_odd, loss_target, m_ln_pre_even, m_w_in_even, m_pool_w, m_pool_scale, m_w_out_even, m_ln_post_even, m_ln_pre_odd, m_w_in_odd, m_sconv_w, m_dconv_w, m_dconv_b, m_cnorm_g, m_cnorm_b, m_w_out_odd, m_ln_post_odd, v_ln_pre_even, v_w_in_even, v_pool_w, v_pool_scale, v_w_out_even, v_ln_post_even, v_ln_pre_odd, v_w_in_odd, v_sconv_w, v_dconv_w, v_dconv_b, v_cnorm_g, v_cnorm_b, v_w_out_odd, v_ln_post_odd):
    _, s, d = x.shape
    half = d // 2
    cw = half // N_CHIPS
    ng, q, gd = pool_w.shape[1:]
    k3, k31 = sconv_w.shape[1], dconv_w.shape[1]
    x2d, tgt = x[0], loss_target[0]
    me = 2 * lax.axis_index("x") + lax.axis_index("y")
    core = lax.axis_index("c")
    c_arr = jnp.reshape(core, (1,)).astype(jnp.int32)
    me_arr = jnp.reshape(me, (1,)).astype(jnp.int32)
    mc_arr = jnp.stack([me, core]).astype(jnp.int32)

    shards = [w_in_even[0], w_out_even[0], w_in_odd[0], w_out_odd[0]]
    pool_w_b = _cast_bf16(pool_w[0].reshape(ng * q, gd), "cast_pool_w").reshape(ng, q, gd)
    pack_w, at_w = _stack_rows([sconv_w[0], dconv_w[0], dconv_b, cnorm_g, cnorm_b], 8)
    pack_d, at_d = _stack_rows([ln_pre_odd, ln_post_odd], 8)
    placed = [lax.dynamic_update_slice(jnp.zeros((ng, N_CHIPS * q, gd), BF16), pool_w_b, (0, me * q, 0)),
              lax.dynamic_update_slice(jnp.zeros((pack_w.shape[0], N_CHIPS * cw), F32), pack_w, (0, me * cw)),
              lax.dynamic_update_slice(jnp.zeros((pack_d.shape[0], d), F32), pack_d, (0, me * (d // N_CHIPS)))]
    plans = _Multi([_GatherPieces([_cast_bf16_own_slab(shards[0], me_arr, "cast_w0")], GATHER_PIECES, (0.3, 0.6)),
                    _SmallGatherPlan(placed, (q, cw, d // N_CHIPS))])
    h0, others, extra = _prep(x2d, ln_pre_even, shards[1:], me_arr, "prep_and_gather_first", plans)
    (win_e,), (pool_w_f, pack_w_f, pack_d_f) = plans.results(extra)
    slabs = [None] + others
    sconv_f = pack_w_f[at_w[0]:at_w[0] + k3]
    dconv_f = pack_w_f[at_w[1]:at_w[1] + k31]
    dconv_b_f, cnorm_g_f, cnorm_b_f = (pack_w_f[at_w[n]:at_w[n] + 1] for n in (2, 3, 4))
    ln_pre_odd_f = pack_d_f[at_d[0]:at_d[0] + 1]
    ln_post_odd_f = pack_d_f[at_d[1]:at_d[1] + 1]

    plans = _Multi([_GatherPlan([slabs[1]], at=(0.6, 0.88)), _GatherPlan([slabs[2]], (0, 1, 4), at=(0.6, 0.88))])
    p_e, extra = _mm_nn(h0, win_e, "proj_in_even", plans)
    (wout_e,), (win_o,) = plans.results(extra)
    wout_e = wout_e.reshape(d, d)
    att, ltot, (win_o,) = _sba_fwd(p_e, half, "sba_fwd", _GatherPlan([win_o], (1, 4, 4), at=(0.69, 0.94)))
    y_e = _even_mix_fwd(p_e, att, pool_w_f, pool_scale, d, "even_mix_fwd")
    o_e, x1, h1 = _mm_out_even(y_e, wout_e, x2d, ln_post_even, ln_pre_odd_f, "proj_out_even")
    p_o, (wout_o,) = _mm_nn(h1, win_o, "proj_in_odd", _GatherPlan([slabs[3]]))
    wout_o = wout_o.reshape(d, d)
    y_o, s3, d1 = _odd_mix_fwd(p_o, sconv_f, dconv_f, dconv_b_f, cnorm_g_f, cnorm_b_f, d, "odd_mix_fwd")
    do_o, dx2, loss_blk, dln_post_odd = _mm_out_odd(y_o, wout_o, x1, ln_post_odd_f, tgt, "proj_out_odd_loss")

    dy_o = _mm_nt(do_o, wout_o, "dy_odd")
    g_wout_o = _mm_tn(y_o, do_o, 1, "dw_out_odd")[0].reshape(N_CHIPS, d // N_CHIPS, d)
    (dbc, dgate_o, ds3, dd1, dcnorm_g, dcnorm_b, ddconv_b), (got,) = _odd_bwd_rows(
        p_o, s3, d1, dy_o, cnorm_g_f, cnorm_b_f, d, "odd_bwd_rows", _SwapPlan([g_wout_o]))
    h_wout_o = _half_add(g_wout_o, got, c_arr, "half_add_out_odd")
    dhc, dcc, dga, dgb, dsconv, ddconv = _odd_bwd_conv(p_o, ds3, dd1, sconv_f, dconv_f, d, "odd_bwd_conv")
    dp_o = jnp.concatenate([dhc, dbc, dcc, dga, dgb, dgate_o], axis=1)
    g_win_o, (s_wout_o,) = _mm_tn(h1, dp_o, N_CHIPS, "dw_in_odd", _ScatterPlan([h_wout_o]))
    (dx1, dln_pre_odd, do_e, dln_post_even), (got,) = _mm_in_bwd(
        dp_o, win_o, x1, ln_pre_odd_f, dx2, (o_e, ln_post_even), "dx_odd", _SwapPlan([g_win_o]))
    h_win_o = _half_add(g_win_o, got, c_arr, "half_add_in_odd")

    dy_e = _mm_nt(do_e, wout_e, "dy_even")
    g_wout_e = _mm_tn(y_e, do_e, 1, "dw_out_even")[0].reshape(N_CHIPS, d // N_CHIPS, d)
    (datt, du, dgate_e, dpool_scale, dpool_w), (got,) = _even_mix_bwd(
        p_e, att, dy_e, pool_w_f, pool_scale, d, "even_mix_bwd", _SwapPlan([g_wout_e]))
    h_wout_e = _half_add(g_wout_e, got, c_arr, "half_add_out_even")
    two = lambda v: v.reshape(2, half)
    small_parts = [dpool_scale, two(dln_post_even), two(dln_pre_odd), two(dln_post_odd),
                   dsconv, ddconv, ddconv_b, dcnorm_g, dcnorm_b, dpool_w.reshape(gd, half)]
    small, at_s = _stack_rows(small_parts, 16)
    plans = _Multi([_ScatterPlan([h_win_o]), _SendWholePlan([small])])
    dq, dk, dv, extra = _sba_bwd(p_e, ltot, datt, half, "sba_bwd", plans)
    (s_win_o,), (small1,) = plans.results(extra)
    small2 = _add2(small, small1, "small_add")
    dp_e = jnp.concatenate([dq, dk, dv, du, dgate_e], axis=1)
    plans = _Multi([_ScatterPlan([h_wout_e]), _ShareHalfPlan([small2])])
    g_win_e, extra = _mm_tn(h0, dp_e, N_CHIPS, "dw_in_even", plans)
    (s_wout_e,), (small_got,) = plans.results(extra)
    swap = _swap_start(g_win_e, "swap_in_even_start")
    pairs = [(h_wout_e, s_wout_e), (h_win_o, s_win_o), (h_wout_o, s_wout_o)]
    parts = []
    for n, (h, r) in enumerate(pairs):
        parts.append(_sum_chips(h, r, mc_arr, f"sum_chips{n + 1}", after=parts[-1] if parts else swap[4]))
    g_win_e, got = _swap_wait(*swap[:4], parts[-1], "swap_in_even_wait")
    parts.append(_sum_chips_ordered(small2, small_got, mc_arr, "small_sum"))
    join_sems = _join_start(parts, "join_first_start")
    h_win_e = _half_add(g_win_e, got, c_arr, "half_add_in_even")
    send_sems, recv_sems, h_win_e, landing, token = _scatter_start(h_win_e, "scatter_in_even_start")
    (grad_x, dln_pre_even), _ = _mm_in_bwd(dp_e, win_e, x2d, ln_pre_even + token[0:1, 0:1], dx1, None, "dx_even")

    last, at_l = _stack_rows([two(dln_pre_even), jnp.pad(loss_blk[0:1], ((0, 0), (0, half - LANES)))], 16)
    (last1,) = _swap_with_sibling([], [last], "swap_last")
    last2 = _add2(last, last1, "last_add")
    share = _share_half_start(last2, "share_last_start")
    gw_out_e, gw_in_o, gw_out_o, red = _join_wait(*join_sems[:3], share[4], "join_first_wait")

    def rows(n, cnt):
        return red[at_s[n]:at_s[n] + cnt]

    def mine(a, width):
        return lax.dynamic_slice_in_dim(a, me * width, width, axis=1)

    quarter = d // N_CHIPS
    g_small = {
        "pool_scale": rows(0, 1),
        "ln_post_even": rows(1, 2).reshape(1, d),
        "ln_pre_odd": mine(rows(2, 2).reshape(1, d), quarter),
        "ln_post_odd": mine(rows(3, 2).reshape(1, d), quarter),
        "sconv_w": mine(rows(4, k3), cw),
        "dconv_w": mine(rows(5, k31), cw),
        "dconv_b": mine(rows(6, 1), cw),
        "cnorm_g": mine(rows(7, 1), cw),
        "cnorm_b": mine(rows(8, 1), cw),
        "pool_w": lax.dynamic_slice_in_dim(rows(9, gd).reshape(ng, gd, gd), me * q, q, axis=1).reshape(ng * q, gd),
    }
    w2d = {
        "ln_pre_even": ln_pre_even, "w_in_even": w_in_even[0], "pool_w": pool_w[0].reshape(ng * q, gd),
        "pool_scale": pool_scale, "w_out_even": w_out_even[0], "ln_post_even": ln_post_even, "ln_pre_odd": ln_pre_odd,
        "w_in_odd": w_in_odd[0], "sconv_w": sconv_w[0], "dconv_w": dconv_w[0], "dconv_b": dconv_b, "cnorm_g": cnorm_g,
        "cnorm_b": cnorm_b, "w_out_odd": w_out_odd[0], "ln_post_odd": ln_post_odd,
    }
    moments = {
        "ln_pre_even": (m_ln_pre_even, v_ln_pre_even), "w_in_even": (m_w_in_even, v_w_in_even),
        "pool_w": (m_pool_w, v_pool_w), "pool_scale": (m_pool_scale, v_pool_scale),
        "w_out_even": (m_w_out_even, v_w_out_even), "ln_post_even": (m_ln_post_even, v_ln_post_even),
        "ln_pre_odd": (m_ln_pre_odd, v_ln_pre_odd), "w_in_odd": (m_w_in_odd, v_w_in_odd),
        "sconv_w": (m_sconv_w, v_sconv_w), "dconv_w": (m_dconv_w, v_dconv_w), "dconv_b": (m_dconv_b, v_dconv_b),
        "cnorm_g": (m_cnorm_g, v_cnorm_g), "cnorm_b": (m_cnorm_b, v_cnorm_b),
        "w_out_odd": (m_w_out_odd, v_w_out_odd), "ln_post_odd": (m_ln_post_odd, v_ln_post_odd),
    }
    def update(name, g):
        m_in, v_in = moments[name]
        w = w2d[name]
        return _adamw(w, g, m_in.reshape(w.shape), v_in.reshape(w.shape), "adamw_" + name)[0]

    updates = {name: update(name, g) for name, g in (("w_in_odd", gw_in_o), ("w_out_even", gw_out_e), ("w_out_odd", gw_out_o))}
    last2, last_got = _share_half_wait(*share[:4], updates["w_out_odd"][1], "share_last_wait")
    last_sum = _sum_chips_ordered(last2, last_got, mc_arr, "last_sum")
    h_win_e, s_win_e = _scatter_wait(send_sems, recv_sems, h_win_e, landing, last_sum, "scatter_in_even_wait")
    last_sems = _join_start([_sum_chips(h_win_e, s_win_e, mc_arr, "sum_chips0"), last_sum], "join_last_start")
    for name, g in g_small.items():
        updates[name] = update(name, g)
    gw_in_e, red_last = _join_wait(*last_sems[:3], updates["pool_w"][1], "join_last_wait")
    loss = red_last[at_l[1], 0]
    updates["ln_pre_even"] = update("ln_pre_even", red_last[at_l[0]:at_l[0] + 2].reshape(1, d))
    updates["w_in_even"] = update("w_in_even", gw_in_e)
    outs = [[u.reshape(moments[name][0].shape) for u in updates[name]] for name in w2d]
    grads_out, deltas, new_m, new_v = zip(*outs)
    return (loss, grad_x.reshape(x.shape), *grads_out, *deltas, *new_m, *new_v)
```

```python
import functools
import math

import jax
import jax.numpy as jnp
from jax import lax
from jax.experimental import pallas as pl
from jax.experimental.pallas import tpu as pltpu

F32 = jnp.float32
BF16 = jnp.bfloat16
EPS = 1e-6
N_CHIPS = 4
VMEM_LIMIT_V7X = 56 << 20
HEAD_DIM = 128
ATT_BLOCK = 256
POOL_WINDOWS = (2, 4, 8, 16)
ROW_TILE = 256
POOL_HALO = 16
CONV_HALO = 32
LANES = 128
ADAM_LR, ADAM_B1, ADAM_B2, ADAM_EPS, ADAM_WD, ADAM_STEP = 0.001, 0.9, 0.999, 1e-08, 0.01, 10
MESH_ID = pl.DeviceIdType.MESH
ANY = pl.BlockSpec(memory_space=pl.ANY)


def _cp(*sem):
    return pltpu.CompilerParams(dimension_semantics=sem or None, vmem_limit_bytes=VMEM_LIMIT_V7X)


def _pick_tile(n, cap):
    best = None
    for t in range(LANES, min(n, cap) + 1, LANES):
        if n % t == 0:
            best = t
    assert best is not None, (n, cap)
    return best


def _sigmoid(x):
    return 1.0 / (1.0 + jnp.exp(-x))


def _silu(x):
    return x * _sigmoid(x)


def _dsilu(x):
    s = _sigmoid(x)
    return s * (1.0 + x * (1.0 - s))


def _log_sigmoid(z):
    return jnp.minimum(z, 0.0) - jnp.log(1.0 + jnp.exp(-jnp.abs(z)))


def _rms_stats(x):
    r = lax.rsqrt(jnp.mean(x * x, axis=-1, keepdims=True) + EPS)
    return x * r, r


def _rms_bwd(dh, xhat, r, g):
    dxh = dh * g
    dx = r * (dxh - xhat * jnp.mean(dxh * xhat, axis=-1, keepdims=True))
    return dx, jnp.sum(dh * xhat, axis=0, keepdims=True)


def _acc_rows(ref, first, val):
    @pl.when(first)
    def _():
        ref[...] = val

    @pl.when(jnp.logical_not(first))
    def _():
        ref[...] += val


def _rcopy(src, dst, ssem, rsem, dev):
    return pltpu.make_async_remote_copy(src_ref=src, dst_ref=dst, send_sem=ssem, recv_sem=rsem,
                                        device_id=dev, device_id_type=MESH_ID)


def _place():
    x, y, c = lax.axis_index("x"), lax.axis_index("y"), lax.axis_index("c")
    chips = [(1 - x, y), (x, 1 - y), (1 - x, 1 - y)]
    return x, y, c, 2 * x + y, chips, (x, y, 1 - c)


class _GatherPlan:
    PER_ARRAY = 7

    def __init__(self, arrays, part=(0, 1, 1), at=(0.5, 0.8)):
        self.operands = list(arrays)
        self.out_shapes = [jax.ShapeDtypeStruct(a.shape, a.dtype) for a in arrays]
        self.aliases = {i: i for i in range(len(arrays))}
        self.nsems = self.PER_ARRAY * len(arrays)
        self.base = 0
        self.halves = [a.shape[1] // 2 for a in arrays]
        self.part = part
        self.at = at

    def schedule(self):
        return [(0.0, self.start), (self.at[0], self.relay), (self.at[1], self.relay_far)]

    def _rows(self, ref, a, chip, half, quarter=None):
        lo, hi, n = self.part
        h = self.halves[a]
        first, size = half * h + lo * h // n, (hi - lo) * h // n
        if quarter is not None:
            first, size = first + quarter * (size // 2), size // 2
        return ref.at[chip, pl.ds(first, size)]

    def _copy(self, src, dst, a, n, ssem, rsem, dev):
        return _rcopy(src, dst, ssem.at[self.base + self.PER_ARRAY * a + n], rsem.at[self.base + self.PER_ARRAY * a + n], dev)

    def _own(self, ins, outs, ssem, rsem):
        x, y, c, me, chips, sib = _place()
        return [self._copy(self._rows(ins[a], a, me, c), self._rows(outs[a], a, me, c), a, k, ssem, rsem, (*chips[k], c))
                for a in range(len(ins)) for k in (0, 1)]

    def _relays(self, outs, ssem, rsem, a, k):
        x, y, c, me, chips, sib = _place()
        chip = 2 * chips[k][0] + chips[k][1]
        whole, quarter = self._rows(outs[a], a, chip, c), self._rows(outs[a], a, chip, c, k)
        return (self._copy(whole, whole, a, k, ssem, rsem, (*chips[k], c)),
                self._copy(quarter, quarter, a, 2 + k, ssem, rsem, (*chips[1 - k], c)),
                self._copy(whole, whole, a, 4 + k, ssem, rsem, sib))

    def _far(self, outs, ssem, rsem, a):
        x, y, c, me, chips, sib = _place()
        chip = 2 * chips[2][0] + chips[2][1]
        whole = self._rows(outs[a], a, chip, c)
        got = [self._copy(q, q, a, 2 + k, ssem, rsem, (*chips[1 - k], c))
               for k, q in enumerate([self._rows(outs[a], a, chip, c, 0), self._rows(outs[a], a, chip, c, 1)])]
        return got, self._copy(whole, whole, a, 6, ssem, rsem, sib)

    def start(self, ins, outs, ssem, rsem):
        for cp in self._own(ins, outs, ssem, rsem):
            cp.start()

    def relay(self, ins, outs, ssem, rsem):
        for a in range(len(outs)):
            for k in (0, 1):
                landed, onward, to_sibling = self._relays(outs, ssem, rsem, a, k)
                landed.wait_recv()
                onward.start()
                to_sibling.start()

    def relay_far(self, ins, outs, ssem, rsem):
        for a in range(len(outs)):
            got, to_sibling = self._far(outs, ssem, rsem, a)
            for cp in got:
                cp.wait_recv()
            to_sibling.start()

    def finish(self, ins, outs, ssem, rsem):
        x, y, c, me, chips, sib = _place()
        for a in range(len(outs)):
            for k in range(3):
                ref = self._rows(outs[a], a, 2 * chips[k][0] + chips[k][1], 1 - c)
                self._copy(ref, ref, a, 4 + k, ssem, rsem, sib).wait_recv()
        for cp in self._own(ins, outs, ssem, rsem):
            cp.wait_send()
        for a in range(len(outs)):
            for k in (0, 1):
                _, onward, to_sibling = self._relays(outs, ssem, rsem, a, k)
                onward.wait_send()
                to_sibling.wait_send()
            self._far(outs, ssem, rsem, a)[1].wait_send()


class _ScatterPlan:
    def __init__(self, arrays, part=(0, 1, 1), into=None):
        self.n = len(arrays)
        self.operands = list(arrays) + list(into or [])
        self.out_shapes = [jax.ShapeDtypeStruct((3,) + a.shape[1:], a.dtype) for a in arrays]
        self.aliases = {self.n + i: i for i in range(self.n)} if into else {}
        self.nsems = 3 * self.n
        self.base = 0
        self.part = part

    def _copies(self, ins, outs, ssem, rsem):
        x, y, c, me, chips, sib = _place()
        lo, hi, n = self.part
        out = []
        for a in range(self.n):
            h = ins[a].shape[1]
            rows = pl.ds(lo * h // n, (hi - lo) * h // n)
            for k, chip in enumerate(chips):
                out.append(_rcopy(ins[a].at[2 * chip[0] + chip[1], rows], outs[a].at[k, rows],
                                  ssem.at[self.base + 3 * a + k], rsem.at[self.base + 3 * a + k], (*chip, c)))
        return out

    def schedule(self):
        return [(0.0, self.start)]

    def start(self, ins, outs, ssem, rsem):
        for cp in self._copies(ins, outs, ssem, rsem):
            cp.start()

    def finish(self, ins, outs, ssem, rsem):
        cps = self._copies(ins, outs, ssem, rsem)
        for cp in cps:
            cp.wait_recv()
        for cp in cps:
            cp.wait_send()


class _ShareHalfPlan(_ScatterPlan):
    def __init__(self, arrays):
        super().__init__(arrays)
        self.out_shapes = [jax.ShapeDtypeStruct((3, a.shape[0] // 2, a.shape[1]), a.dtype) for a in arrays]

    def _copies(self, ins, outs, ssem, rsem):
        x, y, c, me, chips, sib = _place()
        out = []
        for a in range(self.n):
            rh = ins[a].shape[0] // 2
            for k, chip in enumerate(chips):
                out.append(_rcopy(ins[a].at[pl.ds(c * rh, rh)], outs[a].at[k],
                                  ssem.at[self.base + 3 * a + k], rsem.at[self.base + 3 * a + k], (*chip, c)))
        return out


class _SwapPlan:
    def __init__(self, grads):
        self.operands = list(grads)
        self.out_shapes = [jax.ShapeDtypeStruct((g.shape[0], g.shape[1] // 2, g.shape[2]), g.dtype) for g in grads]
        self.aliases = {}
        self.nsems = len(grads)
        self.base = 0

    def _copies(self, ins, outs, ssem, rsem):
        x, y, c, me, chips, sib = _place()
        out = []
        for a, src in enumerate(ins):
            h = src.shape[1] // 2
            out.append(_rcopy(src.at[:, pl.ds((1 - c) * h, h), :], outs[a], ssem.at[self.base + a], rsem.at[self.base + a], sib))
        return out

    def schedule(self):
        return [(0.0, self.start)]

    def start(self, ins, outs, ssem, rsem):
        for cp in self._copies(ins, outs, ssem, rsem):
            cp.start()

    def finish(self, ins, outs, ssem, rsem):
        cps = self._copies(ins, outs, ssem, rsem)
        for cp in cps:
            cp.wait_recv()
        for cp in cps:
            cp.wait_send()


class _SendWholePlan(_SwapPlan):
    def __init__(self, arrays):
        self.operands = list(arrays)
        self.out_shapes = [jax.ShapeDtypeStruct(a.shape, a.dtype) for a in arrays]
        self.aliases = {}
        self.nsems = len(arrays)
        self.base = 0

    def _copies(self, ins, outs, ssem, rsem):
        x, y, c, me, chips, sib = _place()
        return [_rcopy(src, outs[a], ssem.at[self.base + a], rsem.at[self.base + a], sib) for a, src in enumerate(ins)]


class _GatherPieces:
    def __init__(self, arrays, n, at):
        self.pieces = [_GatherPlan(arrays, (j, j + 1, n), at) for j in range(n)]
        self.operands, self.out_shapes, self.aliases = self.pieces[0].operands, self.pieces[0].out_shapes, self.pieces[0].aliases
        self.nsems = sum(p.nsems for p in self.pieces)
        self.at = at
        self.base = 0

    @property
    def base(self):
        return self.pieces[0].base

    @base.setter
    def base(self, value):
        for j, p in enumerate(self.pieces):
            p.base = value + j * p.nsems

    def schedule(self):
        return [(0.0, self.start), (self.at[0], self.relay), (self.at[1], self.relay_far)]

    def _each(self, what, *a):
        for p in self.pieces:
            getattr(p, what)(*a)

    def start(self, *a):
        self._each("start", *a)

    def relay(self, *a):
        self._each("relay", *a)

    def relay_far(self, *a):
        self._each("relay_far", *a)

    def finish(self, *a):
        self._each("finish", *a)


class _SmallGatherPlan:
    def __init__(self, arrays, widths):
        self.operands = list(arrays)
        self.out_shapes = [jax.ShapeDtypeStruct(a.shape, a.dtype) for a in arrays]
        self.aliases = {i: i for i in range(3)}
        self.nsems = 9
        self.base = 0
        self.widths = widths

    def _part(self, ref, n, chip):
        w = self.widths[n]
        return ref.at[:, pl.ds(chip * w, w), :] if n == 0 else ref.at[:, pl.ds(chip * w, w)]

    def _copies(self, ins, outs, ssem, rsem, own):
        x, y, c, me, chips, sib = _place()
        out = []
        for n in range(3):
            for k, chip in enumerate(chips):
                which = me if own else 2 * chip[0] + chip[1]
                out.append(_rcopy(self._part(ins[n], n, which), self._part(outs[n], n, which),
                                  ssem.at[self.base + 3 * n + k], rsem.at[self.base + 3 * n + k], (*chip, c)))
        return out

    def schedule(self):
        return [(0.0, self.start)]

    def start(self, ins, outs, ssem, rsem):
        for cp in self._copies(ins, outs, ssem, rsem, True):
            cp.start()

    def finish(self, ins, outs, ssem, rsem):
        for cp in self._copies(ins, outs, ssem, rsem, False):
            cp.wait_recv()
        for cp in self._copies(ins, outs, ssem, rsem, True):
            cp.wait_send()


class _Multi:
    def __init__(self, plans):
        self.plans = plans
        self.operands, self.out_shapes, self.aliases, self.nsems = [], [], {}, 0
        self.spans = []
        for p in plans:
            ni, no = len(self.operands), len(self.out_shapes)
            self.spans.append((ni, ni + len(p.operands), no, no + len(p.out_shapes)))
            self.aliases.update({ni + i: no + j for i, j in p.aliases.items()})
            p.base = self.nsems
            self.nsems += p.nsems
            self.operands += p.operands
            self.out_shapes += p.out_shapes

    def schedule(self):
        def bound(fn, span):
            i0, i1, o0, o1 = span
            return lambda ins, outs, ssem, rsem: fn(ins[i0:i1], outs[o0:o1], ssem, rsem)

        stages = [(at, bound(fn, span)) for p, span in zip(self.plans, self.spans) for at, fn in p.schedule()]
        return sorted(stages, key=lambda s: s[0])

    def finish(self, ins, outs, ssem, rsem):
        for p, (i0, i1, o0, o1) in zip(self.plans, self.spans):
            p.finish(ins[i0:i1], outs[o0:o1], ssem, rsem)

    def results(self, extra):
        return [list(extra[o0:o1]) for (_, _, o0, o1) in self.spans]


class _Host:
    def __init__(self, comm, in_specs, out_specs, out_shape, scratch, prefetch=0):
        self.comm = comm
        self.n_in, self.n_out = len(in_specs), len(out_specs)
        self.in_specs, self.out_specs, self.out_shape, self.scratch = list(in_specs), list(out_specs), list(out_shape), list(scratch)
        self.aliases = {}
        self.args = []
        if comm is not None:
            self.in_specs += [ANY] * len(comm.operands)
            self.out_specs += [ANY] * len(comm.out_shapes)
            self.out_shape += comm.out_shapes
            self.scratch += [pltpu.SemaphoreType.DMA((comm.nsems,)), pltpu.SemaphoreType.DMA((comm.nsems,))]
            self.aliases = {prefetch + self.n_in + i: self.n_out + j for i, j in comm.aliases.items()}
            self.args = list(comm.operands)

    def split(self, refs):
        nc = len(self.args)
        nco = len(self.out_shape) - self.n_out
        ins, p = refs[:self.n_in], self.n_in + nc
        outs, rest = refs[p:p + self.n_out], refs[p + self.n_out + nco:]
        self._cargs = None
        if self.comm is not None:
            self._cargs = (refs[self.n_in:p], refs[p + self.n_out:p + self.n_out + nco], rest[-2], rest[-1])
            rest = rest[:-2]
        return ins, outs, rest

    def before(self, step, total):
        if self.comm is None:
            return

        for at, stage in self.comm.schedule():
            pl.when(step == min(total - 1, int(at * total)))(functools.partial(stage, *self._cargs))

    def after(self, step, total):
        if self.comm is None:
            return

        @pl.when(step == total - 1)
        def _():
            self.comm.finish(*self._cargs)

    def results(self, outs):
        return outs[:self.n_out], outs[self.n_out:]


def _cast_bf16(x, name):
    r, c = x.shape
    tr = ROW_TILE if r % ROW_TILE == 0 else r

    def body(x_ref, o_ref):
        o_ref[...] = x_ref[...].astype(BF16)

    return pl.pallas_call(
        body, name=name, grid=(r // tr,),
        in_specs=[pl.BlockSpec((tr, c), lambda i: (i, 0))],
        out_specs=pl.BlockSpec((tr, c), lambda i: (i, 0)),
        out_shape=jax.ShapeDtypeStruct((r, c), BF16), compiler_params=_cp("parallel"))(x)


def _cast_bf16_own_slab(x, me_arr, name):
    r, c = x.shape
    tr = ROW_TILE if r % ROW_TILE == 0 else r

    def body(me_ref, x_ref, o_ref):
        o_ref[...] = x_ref[...].astype(BF16)

    return pl.pallas_call(
        body, name=name,
        grid_spec=pltpu.PrefetchScalarGridSpec(
            num_scalar_prefetch=1, grid=(r // tr,),
            in_specs=[pl.BlockSpec((tr, c), lambda i, me: (i, 0))],
            out_specs=pl.BlockSpec((None, tr, c), lambda i, me: (me[0], i, 0))),
        out_shape=jax.ShapeDtypeStruct((N_CHIPS, r, c), BF16), compiler_params=_cp("parallel"))(me_arr, x)


def _prep(x, g, shards, me_arr, name, comm):
    s, d = x.shape
    steps = s // ROW_TILE
    tiles = [(w.shape[0] // steps, w.shape[1]) for w in shards]
    assert all(w.shape[0] % steps == 0 for w in shards)
    in_specs = [pl.BlockSpec((ROW_TILE, d), lambda i, me: (i, 0)), pl.BlockSpec((1, d), lambda i, me: (0, 0))]
    in_specs += [pl.BlockSpec(t, lambda i, me: (i, 0)) for t in tiles]
    out_specs = [pl.BlockSpec((ROW_TILE, d), lambda i, me: (i, 0))]
    out_specs += [pl.BlockSpec((None,) + t, lambda i, me: (me[0], i, 0)) for t in tiles]
    out_shape = [jax.ShapeDtypeStruct((s, d), BF16)] + [jax.ShapeDtypeStruct((N_CHIPS,) + w.shape, BF16) for w in shards]
    host = _Host(comm, in_specs, out_specs, out_shape, [], prefetch=1)

    def body(me_ref, *refs):
        (x_ref, g_ref, *w_refs), (h_ref, *slab_refs), _ = host.split(refs)
        step = pl.program_id(0)
        host.before(step, steps)
        xhat, _ = _rms_stats(x_ref[...])
        h_ref[...] = (xhat * g_ref[...]).astype(BF16)
        for w_ref, slab_ref in zip(w_refs, slab_refs):
            slab_ref[...] = w_ref[...].astype(BF16)
        host.after(step, steps)

    outs = pl.pallas_call(
        body, name=name,
        grid_spec=pltpu.PrefetchScalarGridSpec(num_scalar_prefetch=1, grid=(steps,), in_specs=host.in_specs,
                                               out_specs=host.out_specs, scratch_shapes=host.scratch),
        out_shape=host.out_shape, input_output_aliases=host.aliases,
        compiler_params=_cp("arbitrary"))(me_arr, x, g, *shards, *host.args)
    (h, *slabs), extra = host.results(outs)
    return h, slabs, extra


def _rms_fwd(x, g, name):
    s, d = x.shape

    def body(x_ref, g_ref, h_ref):
        xhat, _ = _rms_stats(x_ref[...])
        h_ref[...] = (xhat * g_ref[...]).astype(BF16)

    return pl.pallas_call(
        body, name=name, grid=(s // ROW_TILE,),
        in_specs=[pl.BlockSpec((ROW_TILE, d), lambda i: (i, 0)), pl.BlockSpec((1, d), lambda i: (0, 0))],
        out_specs=pl.BlockSpec((ROW_TILE, d), lambda i: (i, 0)),
        out_shape=jax.ShapeDtypeStruct((s, d), BF16), compiler_params=_cp("parallel"))(x, g)


def _mm_nn(a, w3, name, comm=None):
    m, k = a.shape
    nsh, _, ns = w3.shape
    tm = 512 if m % 512 == 0 else ROW_TILE
    tn = _pick_tile(ns, 1024)
    per = ns // tn
    grid = (nsh * per, m // tm)
    host = _Host(comm,
                 [pl.BlockSpec((tm, k), lambda n, i: (i, 0)), pl.BlockSpec((None, k, tn), lambda n, i: (n // per, 0, n % per))],
                 [pl.BlockSpec((tm, tn), lambda n, i: (i, n))], [jax.ShapeDtypeStruct((m, nsh * ns), F32)], [])

    def body(*refs):
        (a_ref, w_ref), (o_ref,), _ = host.split(refs)
        step = pl.program_id(0) * grid[1] + pl.program_id(1)
        host.before(step, grid[0] * grid[1])
        o_ref[...] = jnp.dot(a_ref[...], w_ref[...], preferred_element_type=F32)
        host.after(step, grid[0] * grid[1])

    outs = pl.pallas_call(
        body, name=name, grid=grid, in_specs=host.in_specs, out_specs=host.out_specs, out_shape=host.out_shape,
        scratch_shapes=host.scratch, input_output_aliases=host.aliases,
        compiler_params=_cp("arbitrary", "arbitrary"))(a, w3, *host.args)
    (out,), extra = host.results(outs)
    return out, extra


def _mm_nt(a, b, name):
    m, k = a.shape
    n = b.shape[0]
    tm = 512 if m % 512 == 0 else ROW_TILE

    def body(a_ref, b_ref, o_ref):
        o_ref[...] = lax.dot_general(a_ref[...], b_ref[...], (((1,), (1,)), ((), ())), preferred_element_type=F32)

    return pl.pallas_call(
        body, name=name, grid=(m // tm,),
        in_specs=[pl.BlockSpec((tm, k), lambda i: (i, 0)), pl.BlockSpec((n, k), lambda i: (0, 0))],
        out_specs=pl.BlockSpec((tm, n), lambda i: (i, 0)),
        out_shape=jax.ShapeDtypeStruct((m, n), F32), compiler_params=_cp("parallel"))(a, b)


def _mm_tn(a, b, nsh, name, comm=None):
    s, m = a.shape
    n = b.shape[1]
    ns = n // nsh
    tm = 512 if m % 512 == 0 else ROW_TILE
    tn = _pick_tile(ns, 1024)
    per = ns // tn
    grid = (nsh * per, m // tm)
    host = _Host(comm, [pl.BlockSpec((s, tm), lambda j, i: (0, i)), pl.BlockSpec((s, tn), lambda j, i: (0, j))],
                 [pl.BlockSpec((None, tm, tn), lambda j, i: (j // per, i, j % per))],
                 [jax.ShapeDtypeStruct((nsh, m, ns), BF16)], [])

    def body(*refs):
        (a_ref, b_ref), (o_ref,), _ = host.split(refs)
        step = pl.program_id(0) * grid[1] + pl.program_id(1)
        host.before(step, grid[0] * grid[1])
        o_ref[...] = lax.dot_general(a_ref[...], b_ref[...], (((0,), (0,)), ((), ())),
                                     preferred_element_type=F32).astype(BF16)
        host.after(step, grid[0] * grid[1])

    outs = pl.pallas_call(
        body, name=name, grid=grid, in_specs=host.in_specs, out_specs=host.out_specs, out_shape=host.out_shape,
        scratch_shapes=host.scratch, input_output_aliases=host.aliases,
        compiler_params=_cp("arbitrary", "arbitrary"))(a, b, *host.args)
    (out,), extra = host.results(outs)
    return out, extra


def _tri(n, rel):
    row = lax.broadcasted_iota(jnp.int32, (2 * n, n), 0)
    col = lax.broadcasted_iota(jnp.int32, (2 * n, n), 1)
    return jnp.where(rel(jnp.where(row >= n, row - n, row), col), 1.0, 0.0).astype(BF16)


def _dot_split(x, tri2):
    hi = x.astype(BF16)
    lo = (x - hi.astype(F32)).astype(BF16)
    return jnp.dot(jnp.concatenate([hi, lo], axis=1), tri2, preferred_element_type=F32)


def _nt(a, b):
    return lax.dot_general(a, b, (((1,), (1,)), ((), ())), preferred_element_type=F32)


def _tn(a, b):
    return lax.dot_general(a, b, (((0,), (0,)), ((), ())), preferred_element_type=F32)


def _heads_per_step(nh):
    return max(h for h in (1, 2, 4) if nh % h == 0)


def _sba_fwd(p, sbw, name, comm=None):
    s = p.shape[0]
    nh = sbw // HEAD_DIM
    hp = _heads_per_step(nh)
    ngrp, hw = nh // hp, hp * HEAD_DIM
    blk = ATT_BLOCK
    nq = s // blk
    scale = 1.0 / math.sqrt(HEAD_DIM)
    host = _Host(comm,
                 [pl.BlockSpec((blk, hw), lambda g, i: (i, g)),
                  pl.BlockSpec((s, hw), lambda g, i: (0, ngrp + g)),
                  pl.BlockSpec((s, hw), lambda g, i: (0, 2 * ngrp + g))],
                 [pl.BlockSpec((blk, hw), lambda g, i: (i, g))] * 2,
                 [jax.ShapeDtypeStruct((s, sbw), F32)] * 2,
                 [pltpu.VMEM((s, hw), BF16)] * 2)

    def body(*refs):
        (q_ref, k_ref, v_ref), (o_ref, lt_ref), (kb_ref, vb_ref) = host.split(refs)
        i = pl.program_id(1)
        step = pl.program_id(0) * nq + i
        host.before(step, ngrp * nq)

        @pl.when(i == 0)
        def _():
            kb_ref[...] = k_ref[...].astype(BF16)
            vb_ref[...] = v_ref[...].astype(BF16)

        heads = [slice(h * HEAD_DIM, (h + 1) * HEAD_DIM) for h in range(hp)]
        qs = [q_ref[:, hd].astype(BF16) for hd in heads]
        later = _tri(blk, lambda r, c: r > c)
        causal = lax.broadcasted_iota(jnp.int32, (blk, blk), 1) < lax.broadcasted_iota(jnp.int32, (blk, blk), 0)

        def key_block(j, carry, diagonal):
            rows = pl.ds(pl.multiple_of(j * blk, blk), blk)
            hs = range(hp)
            z = [_nt(qs[h], kb_ref[rows, heads[h]]) * scale for h in hs]
            ls = [_log_sigmoid(z[h]) for h in hs]
            lm = [jnp.where(causal, ls[h] - z[h], 0.0) if diagonal else ls[h] - z[h] for h in hs]
            stay = [_dot_split(lm[h], later) for h in hs]
            w = [jnp.exp(ls[h] + stay[h] + carry[h][1]) for h in hs]
            if diagonal:
                w = [jnp.where(causal, w[h], 0.0) for h in hs]
            acc = [carry[h][0] + jnp.dot(w[h].astype(BF16), vb_ref[rows, heads[h]], preferred_element_type=F32) for h in hs]
            return tuple((acc[h], carry[h][1] + jnp.sum(lm[h], axis=1, keepdims=True)) for h in hs)

        init = tuple((jnp.zeros((blk, HEAD_DIM), F32), jnp.zeros((blk, 1), F32)) for _ in heads)
        carry = key_block(i, init, True)
        carry = lax.fori_loop(0, i, lambda n, c: key_block(i - 1 - n, c, False), carry)
        for h, hd in enumerate(heads):
            o_ref[:, hd] = carry[h][0]
            lt_ref[:, hd] = jnp.broadcast_to(carry[h][1], (blk, HEAD_DIM))
        host.after(step, ngrp * nq)

    outs = pl.pallas_call(
        body, name=name, grid=(ngrp, nq), in_specs=host.in_specs, out_specs=host.out_specs, out_shape=host.out_shape,
        scratch_shapes=host.scratch, input_output_aliases=host.aliases,
        compiler_params=_cp("arbitrary", "arbitrary"))(p, p, p, *host.args)
    (out, ltot), extra = host.results(outs)
    return out, ltot, extra


def _sba_bwd(p, ltot, dout, sbw, name, comm=None):
    s = p.shape[0]
    nh = sbw // HEAD_DIM
    hp = _heads_per_step(nh)
    ngrp, hw = nh // hp, hp * HEAD_DIM
    blk = ATT_BLOCK
    nq = s // blk
    scale = 1.0 / math.sqrt(HEAD_DIM)
    blk_spec = pl.BlockSpec((blk, hw), lambda g, i: (i, g))
    col_spec = pl.BlockSpec((s, hw), lambda g, i: (0, g))
    host = _Host(comm,
                 [blk_spec, pl.BlockSpec((s, hw), lambda g, i: (0, ngrp + g)),
                  pl.BlockSpec((s, hw), lambda g, i: (0, 2 * ngrp + g)), blk_spec, blk_spec],
                 [blk_spec, col_spec, col_spec], [jax.ShapeDtypeStruct((s, sbw), BF16)] * 3,
                 [pltpu.VMEM((s, hw), BF16)] * 2 + [pltpu.VMEM((s, hw), F32)] * 2)

    def body(*refs):
        (q_ref, k_ref, v_ref, lt_ref, do_ref), (dq_ref, dk_ref, dv_ref), (kb_ref, vb_ref, dka_ref, dva_ref) = host.split(refs)
        i = pl.program_id(1)
        step = pl.program_id(0) * nq + i
        host.before(step, ngrp * nq)

        @pl.when(i == 0)
        def _():
            kb_ref[...] = k_ref[...].astype(BF16)
            vb_ref[...] = v_ref[...].astype(BF16)
            dka_ref[...] = jnp.zeros_like(dka_ref)
            dva_ref[...] = jnp.zeros_like(dva_ref)

        heads = [slice(h * HEAD_DIM, (h + 1) * HEAD_DIM) for h in range(hp)]
        qs = [q_ref[:, hd].astype(BF16) for hd in heads]
        dos = [do_ref[:, hd].astype(BF16) for hd in heads]
        ltots = [lt_ref[:, h * HEAD_DIM:h * HEAD_DIM + 1] for h in range(hp)]
        upto = _tri(blk, lambda r, c: r <= c)
        before = _tri(blk, lambda r, c: r < c)
        causal = lax.broadcasted_iota(jnp.int32, (blk, blk), 1) < lax.broadcasted_iota(jnp.int32, (blk, blk), 0)

        def key_block(j, carry, diagonal):
            rows = pl.ds(pl.multiple_of(j * blk, blk), blk)
            hs = range(hp)
            kj = [kb_ref[rows, heads[h]] for h in hs]
            vj = [vb_ref[rows, heads[h]] for h in hs]
            z = [_nt(qs[h], kj[h]) * scale for h in hs]
            dw = [_nt(dos[h], vj[h]) for h in hs]
            ls = [_log_sigmoid(z[h]) for h in hs]
            lm = [jnp.where(causal, ls[h] - z[h], 0.0) if diagonal else ls[h] - z[h] for h in hs]
            stay = [ltots[h] - carry[h][1] - _dot_split(lm[h], upto) for h in hs]
            w = [jnp.exp(ls[h] + stay[h]) for h in hs]
            if diagonal:
                w = [jnp.where(causal, w[h], 0.0) for h in hs]
            da = [dw[h] * w[h] for h in hs]
            sig = [jnp.exp(ls[h]) for h in hs]
            chain = [sig[h] * (carry[h][2] + _dot_split(da[h], before)) for h in hs]
            if diagonal:
                chain = [jnp.where(causal, chain[h], 0.0) for h in hs]
            dzb = [((da[h] * (1.0 - sig[h]) - chain[h]) * scale).astype(BF16) for h in hs]
            dq = [carry[h][0] + jnp.dot(dzb[h], kj[h], preferred_element_type=F32) for h in hs]
            for h in hs:
                dka_ref[rows, heads[h]] += _tn(dzb[h], qs[h])
            for h in hs:
                dva_ref[rows, heads[h]] += _tn(w[h].astype(BF16), dos[h])
            return tuple((dq[h], carry[h][1] + jnp.sum(lm[h], axis=1, keepdims=True),
                          carry[h][2] + jnp.sum(da[h], axis=1, keepdims=True)) for h in hs)

        zero = jnp.zeros((blk, 1), F32)
        init = tuple((jnp.zeros((blk, HEAD_DIM), F32), zero, zero) for _ in heads)
        carry = lax.fori_loop(0, i, lambda j, c: key_block(j, c, False), init)
        carry = key_block(i, carry, True)
        for h, hd in enumerate(heads):
            dq_ref[:, hd] = carry[h][0].astype(BF16)

        @pl.when(i == nq - 1)
        def _():
            dk_ref[...] = dka_ref[...].astype(BF16)
            dv_ref[...] = dva_ref[...].astype(BF16)

        host.after(step, ngrp * nq)

    outs = pl.pallas_call(
        body, name=name, grid=(ngrp, nq), in_specs=host.in_specs, out_specs=host.out_specs, out_shape=host.out_shape,
        scratch_shapes=host.scratch, input_output_aliases=host.aliases,
        compiler_params=_cp("arbitrary", "arbitrary"))(p, p, p, ltot, dout, *host.args)
    (dq, dk, dv), extra = host.results(outs)
    return dq, dk, dv, extra


def _pool_groups(pad_ref, tile, row0, gd, halo):
    row = row0 + lax.broadcasted_iota(jnp.int32, (tile, 1), 0)
    out = []
    for gi, win in enumerate(POOL_WINDOWS):
        cs = slice(gi * gd, (gi + 1) * gd)
        tok = pad_ref[halo:halo + tile, cs]
        acc = tok
        for j in range(1, win):
            acc = acc + pad_ref[halo - j:halo - j + tile, cs]
        cnt = jnp.minimum(win, row + 1).astype(F32)
        out.append(acc / cnt - tok)
    return out


def _even_mix_fwd(p, att, pool_w, pool_scale, d, name):
    s = p.shape[0]
    half = d // 2
    gd = half // len(POOL_WINDOWS)
    t, hb = ROW_TILE, POOL_HALO

    def body(u_ref, uh_ref, g_ref, a_ref, pw_ref, sc_ref, y_ref, pad_ref):
        i = pl.program_id(0)
        pad_ref[0:hb, :] = jnp.where(i > 0, uh_ref[...], 0.0)
        pad_ref[hb:, :] = u_ref[...]
        pooled = _pool_groups(pad_ref, t, i * t, gd, hb)
        for gi in range(len(POOL_WINDOWS)):
            cs = slice(gi * gd, (gi + 1) * gd)
            po = jnp.dot(pooled[gi].astype(BF16), pw_ref[gi], preferred_element_type=F32) * sc_ref[:, cs]
            y_ref[:, half + gi * gd:half + (gi + 1) * gd] = (po * _silu(g_ref[:, half + gi * gd:half + (gi + 1) * gd])).astype(BF16)
        y_ref[:, :half] = (a_ref[...] * _silu(g_ref[:, :half])).astype(BF16)

    return pl.pallas_call(
        body, name=name, grid=(s // t,),
        in_specs=[pl.BlockSpec((t, half), lambda i: (i, 3)),
                  pl.BlockSpec((hb, half), lambda i: (jnp.maximum(i * (t // hb) - 1, 0), 3)),
                  pl.BlockSpec((t, d), lambda i: (i, 2)),
                  pl.BlockSpec((t, half), lambda i: (i, 0)),
                  pl.BlockSpec(pool_w.shape, lambda i: (0, 0, 0)),
                  pl.BlockSpec((1, half), lambda i: (0, 0))],
        out_specs=pl.BlockSpec((t, d), lambda i: (i, 0)),
        out_shape=jax.ShapeDtypeStruct((s, d), BF16),
        scratch_shapes=[pltpu.VMEM((hb + t, half), F32)],
        compiler_params=_cp("parallel"))(p, p, p, att, pool_w, pool_scale)


def _even_mix_bwd(p, att, dy, pool_w, pool_scale, d, name, comm=None):
    s = p.shape[0]
    half = d // 2
    ng = len(POOL_WINDOWS)
    gd = half // ng
    t, hb = ROW_TILE, POOL_HALO
    nt = s // t
    host = _Host(
        comm,
        [pl.BlockSpec((t, half), lambda i: (i, 3)),
         pl.BlockSpec((hb, half), lambda i: (jnp.maximum(i * (t // hb) - 1, 0), 3)),
         pl.BlockSpec((t, d), lambda i: (i, 2)),
         pl.BlockSpec((hb, half), lambda i: (jnp.minimum((i + 1) * (t // hb), s // hb - 1), 5)),
         pl.BlockSpec((t, half), lambda i: (i, 0)),
         pl.BlockSpec((t, d), lambda i: (i, 0)),
         pl.BlockSpec((hb, half), lambda i: (jnp.minimum((i + 1) * (t // hb), s // hb - 1), 1)),
         pl.BlockSpec(pool_w.shape, lambda i: (0, 0, 0)),
         pl.BlockSpec((1, half), lambda i: (0, 0))],
        [pl.BlockSpec((t, half), lambda i: (i, 0)),
         pl.BlockSpec((t, half), lambda i: (i, 0)),
         pl.BlockSpec((t, d), lambda i: (i, 0)),
         pl.BlockSpec((1, half), lambda i: (0, 0)),
         pl.BlockSpec((ng, gd, gd), lambda i: (0, 0, 0))],
        [jax.ShapeDtypeStruct((s, half), F32), jax.ShapeDtypeStruct((s, half), BF16),
         jax.ShapeDtypeStruct((s, d), BF16), jax.ShapeDtypeStruct((1, half), F32),
         jax.ShapeDtypeStruct((ng, gd, gd), F32)],
        [pltpu.VMEM((hb + t, half), F32), pltpu.VMEM((t + hb, half), F32)])

    def body(*refs):
        ((u_ref, uh_ref, g_ref, gh_ref, a_ref, dy_ref, dyh_ref, pw_ref, sc_ref),
         (da_ref, du_ref, dg_ref, dsc_ref, dpw_ref), (pad_ref, dn_ref)) = host.split(refs)
        i = pl.program_id(0)
        host.before(i, nt)
        first = i == 0
        pad_ref[0:hb, :] = jnp.where(i > 0, uh_ref[...], 0.0)
        pad_ref[hb:, :] = u_ref[...]
        pooled = _pool_groups(pad_ref, t, i * t, gd, hb)
        g1 = g_ref[:, :half]
        dy1 = dy_ref[:, :half]
        da_ref[...] = dy1 * _silu(g1)
        dg_ref[:, :half] = (dy1 * a_ref[...] * _dsilu(g1)).astype(BF16)
        row = i * t + lax.broadcasted_iota(jnp.int32, (t + hb, 1), 0)
        for gi, win in enumerate(POOL_WINDOWS):
            cs = slice(gi * gd, (gi + 1) * gd)
            cs2 = slice(half + gi * gd, half + (gi + 1) * gd)
            w = pw_ref[gi]
            pb = pooled[gi].astype(BF16)
            zp = jnp.dot(pb, w, preferred_element_type=F32)
            g2 = g_ref[:, cs2]
            dy2 = dy_ref[:, cs2]
            dg_ref[:, cs2] = (dy2 * zp * sc_ref[:, cs] * _dsilu(g2)).astype(BF16)
            dpo = dy2 * _silu(g2)
            _acc_rows(dsc_ref.at[:, cs], first, jnp.sum(dpo * zp, axis=0, keepdims=True))
            dz = (dpo * sc_ref[:, cs]).astype(BF16)
            _acc_rows(dpw_ref.at[gi], first, _tn(pb, dz))
            dzh = jnp.where(i < nt - 1, dyh_ref[:, cs] * _silu(gh_ref[:, cs]) * sc_ref[:, cs], 0.0).astype(BF16)
            dpool = _nt(dz, w)
            dpool_h = _nt(dzh, w)
            cnt = jnp.minimum(win, row + 1).astype(F32)
            dn_ref[0:t, cs] = dpool / cnt[0:t]
            dn_ref[t:, cs] = dpool_h / cnt[t:]
            acc = dn_ref[0:t, cs]
            for j in range(1, win):
                acc = acc + dn_ref[j:j + t, cs]
            du_ref[:, cs] = (acc - dpool).astype(BF16)
        host.after(i, nt)

    outs = pl.pallas_call(
        body, name=name, grid=(nt,), in_specs=host.in_specs, out_specs=host.out_specs, out_shape=host.out_shape,
        scratch_shapes=host.scratch, input_output_aliases=host.aliases,
        compiler_params=_cp("arbitrary"))(p, p, p, p, att, dy, dy, pool_w, pool_scale, *host.args)
    return host.results(outs)


def _mm_out_even(y, w, x, g_post, g_pre_next, name):
    s, k = y.shape
    d = w.shape[1]
    t = ROW_TILE

    def body(y_ref, w_ref, x_ref, gp_ref, gn_ref, o_ref, x1_ref, h1_ref):
        for r0 in range(0, t, t // 2):
            rows = slice(r0, r0 + t // 2)
            o = jnp.dot(y_ref[rows, :], w_ref[...], preferred_element_type=F32)
            o_ref[rows, :] = o
            ohat, _ = _rms_stats(o)
            x1 = x_ref[rows, :] + ohat * gp_ref[...]
            x1_ref[rows, :] = x1
            xhat, _ = _rms_stats(x1)
            h1_ref[rows, :] = (xhat * gn_ref[...]).astype(BF16)

    row = lambda c: pl.BlockSpec((t, c), lambda i: (i, 0))
    vec = pl.BlockSpec((1, d), lambda i: (0, 0))
    return pl.pallas_call(
        body, name=name, grid=(s // t,),
        in_specs=[row(k), pl.BlockSpec((k, d), lambda i: (0, 0)), row(d), vec, vec],
        out_specs=[row(d), row(d), row(d)],
        out_shape=[jax.ShapeDtypeStruct((s, d), F32), jax.ShapeDtypeStruct((s, d), F32),
                   jax.ShapeDtypeStruct((s, d), BF16)],
        compiler_params=_cp("parallel"))(y, w, x, g_post, g_pre_next)


def _mm_out_odd(y, w, x1, g_post, target, name):
    s, k = y.shape
    d = w.shape[1]
    t = ROW_TILE

    def body(y_ref, w_ref, x_ref, gp_ref, tg_ref, do_ref, dx_ref, loss_ref, dgp_ref):
        first = pl.program_id(0) == 0
        gp = gp_ref[...]
        part = dgp = None
        for r0 in range(0, t, t // 2):
            rows = slice(r0, r0 + t // 2)
            o = jnp.dot(y_ref[rows, :], w_ref[...], preferred_element_type=F32)
            ohat, r = _rms_stats(o)
            diff = x_ref[rows, :] + ohat * gp - tg_ref[rows, :]
            part_half = 0.5 * jnp.sum(jnp.mean(diff * diff, axis=-1, keepdims=True), axis=0, keepdims=True)
            dx2 = diff * (1.0 / d)
            dx_ref[rows, :] = dx2
            do, dgp_half = _rms_bwd(dx2, ohat, r, gp)
            do_ref[rows, :] = do.astype(BF16)
            part = part_half if part is None else part + part_half
            dgp = dgp_half if dgp is None else dgp + dgp_half
        _acc_rows(loss_ref, first, jnp.broadcast_to(part, loss_ref.shape))
        _acc_rows(dgp_ref, first, dgp)

    row = lambda c: pl.BlockSpec((t, c), lambda i: (i, 0))
    vec = pl.BlockSpec((1, d), lambda i: (0, 0))
    return pl.pallas_call(
        body, name=name, grid=(s // t,),
        in_specs=[row(k), pl.BlockSpec((k, d), lambda i: (0, 0)), row(d), vec, row(d)],
        out_specs=[row(d), row(d), pl.BlockSpec((8, LANES), lambda i: (0, 0)), vec],
        out_shape=[jax.ShapeDtypeStruct((s, d), BF16), jax.ShapeDtypeStruct((s, d), F32),
                   jax.ShapeDtypeStruct((8, LANES), F32), jax.ShapeDtypeStruct((1, d), F32)],
        compiler_params=_cp("arbitrary"))(y, w, x1, g_post, target)


def _layer_norm(d1, cg, cb):
    mu = jnp.mean(d1, axis=-1, keepdims=True)
    cen = d1 - mu
    rstd = lax.rsqrt(jnp.mean(cen * cen, axis=-1, keepdims=True) + EPS)
    n = cen * rstd
    return n, rstd, n * cg + cb


SUBLANES = 8
ROW_STRIP = 64
GATHER_PIECES = 8
CONV_ROWS = 64


def _make_shifts(pad_ref, cs, sh_ref):
    rows = sh_ref.shape[1]
    for r in range(1, SUBLANES):
        sh_ref[r - 1] = pad_ref[r:r + rows, cs]


def _by_shift(taps, base, sign=1):
    return sorted(range(taps), key=lambda k: ((sign * (base + k)) % SUBLANES, k))


def _window(pad_ref, cs, sh_ref, off, t):
    m, r = divmod(off, SUBLANES)
    if r == 0:
        return pad_ref[SUBLANES * m:SUBLANES * m + t, cs]
    return sh_ref[r - 1, SUBLANES * m:SUBLANES * m + t, :]


def _odd_mix_fwd(p, sconv_w, dconv_w, dconv_b, cnorm_g, cnorm_b, d, name):
    s = p.shape[0]
    w = d // 2
    k3, k31 = sconv_w.shape[0], dconv_w.shape[0]
    t, hb = ROW_TILE, CONV_HALO
    assert hb >= k31 - 1 and w % LANES == 0

    def body(p_ref, ph_ref, w3_ref, w31_ref, b31_ref, cg_ref, cb_ref, y_ref, s3_ref, d1_ref, mpad, dpad, sh_ref):
        i = pl.program_id(0)
        mpad[0:hb, :] = jnp.where(i > 0, ph_ref[:, 2 * w:3 * w] * ph_ref[:, 0:w], 0.0)
        mpad[hb:, :] = p_ref[:, 2 * w:3 * w] * p_ref[:, 0:w]
        dpad[0:hb, :] = jnp.where(i > 0, ph_ref[:, 3 * w:4 * w] * _sigmoid(ph_ref[:, 4 * w:5 * w]), 0.0)
        dpad[hb:, :] = p_ref[:, 3 * w:4 * w] * _sigmoid(p_ref[:, 4 * w:5 * w])
        for c0 in range(0, w, LANES):
            cs = slice(c0, c0 + LANES)
            acc = jnp.zeros((t, LANES), F32)
            for kk in range(k3):
                acc = acc + w3_ref[kk:kk + 1, cs] * mpad[hb - (k3 - 1) + kk:hb - (k3 - 1) + kk + t, cs]
            s3_ref[:, cs] = acc
            _make_shifts(dpad, cs, sh_ref)
            for r0 in range(0, t, CONV_ROWS):
                acc = jnp.zeros((CONV_ROWS, LANES), F32)
                for kk in _by_shift(k31, hb - (k31 - 1)):
                    acc = acc + w31_ref[kk:kk + 1, cs] * _window(dpad, cs, sh_ref, hb - (k31 - 1) + kk + r0, CONV_ROWS)
                d1_ref[r0:r0 + CONV_ROWS, cs] = acc + b31_ref[:, cs]
        _, _, d2 = _layer_norm(d1_ref[...], cg_ref[...], cb_ref[...])
        y_ref[:, :w] = (p_ref[:, w:2 * w] * s3_ref[...] * _silu(p_ref[:, 5 * w:6 * w])).astype(BF16)
        y_ref[:, w:] = (_silu(d2) * _silu(p_ref[:, 6 * w:7 * w])).astype(BF16)

    row = lambda c: pl.BlockSpec((t, c), lambda i: (i, 0))
    full = lambda a: pl.BlockSpec(a.shape, lambda i: (0, 0))
    return pl.pallas_call(
        body, name=name, grid=(s // t,),
        in_specs=[row(7 * w),
                  pl.BlockSpec((hb, 5 * w), lambda i: (jnp.maximum(i * (t // hb) - 1, 0), 0)),
                  full(sconv_w), full(dconv_w), full(dconv_b), full(cnorm_g), full(cnorm_b)],
        out_specs=[row(d), row(w), row(w)],
        out_shape=[jax.ShapeDtypeStruct((s, d), BF16), jax.ShapeDtypeStruct((s, w), F32),
                   jax.ShapeDtypeStruct((s, w), F32)],
        scratch_shapes=[pltpu.VMEM((hb + t, w), F32)] * 2 + [pltpu.VMEM((SUBLANES - 1, hb + t - SUBLANES, LANES), F32)],
        compiler_params=_cp("parallel"))(p, p, sconv_w, dconv_w, dconv_b, cnorm_g, cnorm_b)


def _odd_bwd_rows(p, s3, d1, dy, cnorm_g, cnorm_b, d, name, comm=None):
    s = p.shape[0]
    w = d // 2
    t = ROW_TILE
    col = lambda j: pl.BlockSpec((t, w), lambda i: (i, j))
    row = lambda c: pl.BlockSpec((t, c), lambda i: (i, 0))
    vec = pl.BlockSpec((1, w), lambda i: (0, 0))
    host = _Host(comm, [col(1), col(5), col(6), row(w), row(w), row(d), vec, vec],
                 [row(w), row(d), row(w), row(w), vec, vec, vec],
                 [jax.ShapeDtypeStruct((s, w), BF16), jax.ShapeDtypeStruct((s, d), BF16),
                  jax.ShapeDtypeStruct((s, w), F32), jax.ShapeDtypeStruct((s, w), F32)] + [jax.ShapeDtypeStruct((1, w), F32)] * 3, [])

    def body(*refs):
        ((bc_ref, g1_ref, g2_ref, s3_ref, d1_ref, dy_ref, cg_ref, cb_ref),
         (dbc_ref, dg_ref, ds3_ref, dd1_ref, dcg_ref, dcb_ref, db_ref), _) = host.split(refs)
        step = pl.program_id(0)
        host.before(step, s // t)
        first = step == 0

        def strip(j, sums):
            rows = slice(j * ROW_STRIP, (j + 1) * ROW_STRIP)
            g1, g2 = g1_ref[rows, :], g2_ref[rows, :]
            bc, s3v = bc_ref[rows, :], s3_ref[rows, :]
            dy1, dy2 = dy_ref[rows, :w], dy_ref[rows, w:]
            n, rstd, d2 = _layer_norm(d1_ref[rows, :], cg_ref[...], cb_ref[...])
            dg_ref[rows, :w] = (dy1 * bc * s3v * _dsilu(g1)).astype(BF16)
            dg_ref[rows, w:] = (dy2 * _silu(d2) * _dsilu(g2)).astype(BF16)
            dco = dy1 * _silu(g1)
            dbc_ref[rows, :] = (dco * s3v).astype(BF16)
            ds3_ref[rows, :] = dco * bc
            dd2 = dy2 * _silu(g2) * _dsilu(d2)
            dn = dd2 * cg_ref[...]
            dd1 = rstd * (dn - jnp.mean(dn, axis=-1, keepdims=True) - n * jnp.mean(dn * n, axis=-1, keepdims=True))
            dd1_ref[rows, :] = dd1
            dcb, dcg, db = sums
            return (dcb + jnp.sum(dd2, axis=0, keepdims=True), dcg + jnp.sum(dd2 * n, axis=0, keepdims=True),
                    db + jnp.sum(dd1, axis=0, keepdims=True))

        zero = jnp.zeros((1, w), F32)
        sums = (zero, zero, zero)
        for j in range(t // ROW_STRIP):
            sums = strip(j, sums)
        dcb, dcg, db = sums
        _acc_rows(dcb_ref, first, dcb)
        _acc_rows(dcg_ref, first, dcg)
        _acc_rows(db_ref, first, db)
        host.after(step, s // t)

    outs = pl.pallas_call(
        body, name=name, grid=(s // t,), in_specs=host.in_specs, out_specs=host.out_specs, out_shape=host.out_shape,
        scratch_shapes=host.scratch, input_output_aliases=host.aliases,
        compiler_params=_cp("arbitrary"))(p, p, p, s3, d1, dy, cnorm_g, cnorm_b, *host.args)
    return host.results(outs)


def _odd_bwd_conv(p, ds3, dd1, sconv_w, dconv_w, d, name):
    s = p.shape[0]
    w = d // 2
    k3, k31 = sconv_w.shape[0], dconv_w.shape[0]
    t, hb, ha = ROW_TILE, CONV_HALO, 8
    nt = s // t
    assert hb >= k31 - 1 and ha >= k3 - 1

    def body(hc_ref, cc_ref, ga_ref, gb_ref, hch_ref, cch_ref, gah_ref, gbh_ref, ds3_ref, ds3h_ref, dd1_ref, dd1h_ref,
             w3_ref, w31_ref, dhc_ref, dcc_ref, dga_ref, dgb_ref, dw3_ref, dw31_ref, mpad, dpad, s3pad, d1pad, sh_ref):
        i = pl.program_id(0)
        first = i == 0
        last = i == nt - 1
        mpad[0:hb, :] = jnp.where(i > 0, cch_ref[...] * hch_ref[...], 0.0)
        mpad[hb:, :] = cc_ref[...] * hc_ref[...]
        dpad[0:hb, :] = jnp.where(i > 0, gah_ref[...] * _sigmoid(gbh_ref[...]), 0.0)
        dpad[hb:, :] = ga_ref[...] * _sigmoid(gb_ref[...])
        s3pad[0:t, :] = ds3_ref[...]
        s3pad[t:, :] = jnp.where(last, 0.0, ds3h_ref[...])
        d1pad[0:t, :] = dd1_ref[...]
        d1pad[t:, :] = jnp.where(last, 0.0, dd1h_ref[...])

        @pl.when(first)
        def _():
            dw3_ref[...] = jnp.zeros_like(dw3_ref)
            dw31_ref[...] = jnp.zeros_like(dw31_ref)

        def fold(v):
            return jnp.sum(v.reshape(v.shape[0] // SUBLANES, SUBLANES, LANES), axis=0)

        groups = range(0, t, CONV_ROWS)
        for c0 in range(0, w, LANES):
            cs = slice(c0, c0 + LANES)
            ds3v = s3pad[0:t, cs]
            dm = jnp.zeros((t, LANES), F32)
            for kk in range(k3):
                dm = dm + w3_ref[kk:kk + 1, cs] * s3pad[k3 - 1 - kk:k3 - 1 - kk + t, cs]
                off = hb - (k3 - 1) + kk
                dw3_ref[SUBLANES * kk:SUBLANES * (kk + 1), cs] += fold(ds3v * mpad[off:off + t, cs])
            dcc_ref[:, cs] = (dm * hc_ref[:, cs]).astype(BF16)
            dhc_ref[:, cs] = (dm * cc_ref[:, cs]).astype(BF16)
            _make_shifts(d1pad, cs, sh_ref)
            for r0 in groups:
                rows = slice(r0, r0 + CONV_ROWS)
                dd0 = jnp.zeros((CONV_ROWS, LANES), F32)
                for kk in _by_shift(k31, -(k31 - 1), -1):
                    dd0 = dd0 + w31_ref[kk:kk + 1, cs] * _window(d1pad, cs, sh_ref, k31 - 1 - kk + r0, CONV_ROWS)
                sgb = _sigmoid(gb_ref[rows, cs])
                dga_ref[rows, cs] = (dd0 * sgb).astype(BF16)
                dgb_ref[rows, cs] = (dd0 * ga_ref[rows, cs] * sgb * (1.0 - sgb)).astype(BF16)
            _make_shifts(dpad, cs, sh_ref)
            for kk in _by_shift(k31, hb - (k31 - 1)):
                part = jnp.zeros((SUBLANES, LANES), F32)
                for r0 in groups:
                    part = part + fold(d1pad[r0:r0 + CONV_ROWS, cs]
                                       * _window(dpad, cs, sh_ref, hb - (k31 - 1) + kk + r0, CONV_ROWS))
                dw31_ref[SUBLANES * kk:SUBLANES * (kk + 1), cs] += part

    col = lambda j: pl.BlockSpec((t, w), lambda i: (i, j))
    pre = lambda j: pl.BlockSpec((hb, w), lambda i: (jnp.maximum(i * (t // hb) - 1, 0), j))
    row = pl.BlockSpec((t, w), lambda i: (i, 0))
    post = lambda h: pl.BlockSpec((h, w), lambda i: (jnp.minimum((i + 1) * (t // h), s // h - 1), 0))
    full = lambda a: pl.BlockSpec(a.shape, lambda i: (0, 0))
    dhc, dcc, dga, dgb, dw3, dw31 = pl.pallas_call(
        body, name=name, grid=(nt,),
        in_specs=[col(0), col(2), col(3), col(4), pre(0), pre(2), pre(3), pre(4),
                  row, post(ha), row, post(hb), full(sconv_w), full(dconv_w)],
        out_specs=[row, row, row, row, pl.BlockSpec((SUBLANES * k3, w), lambda i: (0, 0)),
                   pl.BlockSpec((SUBLANES * k31, w), lambda i: (0, 0))],
        out_shape=[jax.ShapeDtypeStruct((s, w), BF16)] * 4
        + [jax.ShapeDtypeStruct((SUBLANES * k3, w), F32), jax.ShapeDtypeStruct((SUBLANES * k31, w), F32)],
        scratch_shapes=[pltpu.VMEM((hb + t, w), F32)] * 2 + [pltpu.VMEM((t + ha, w), F32), pltpu.VMEM((t + hb, w), F32),
                                                             pltpu.VMEM((SUBLANES - 1, hb + t - SUBLANES, LANES), F32)],
        compiler_params=_cp("arbitrary"))(p, p, p, p, p, p, p, p, ds3, ds3, dd1, dd1, sconv_w, dconv_w)
    return dhc, dcc, dga, dgb, jnp.sum(dw3.reshape(k3, SUBLANES, w), axis=1), jnp.sum(dw31.reshape(k31, SUBLANES, w), axis=1)


def _mm_in_bwd(dp, w3, x, g_pre, dres, post, name, comm=None):
    s = dp.shape[0]
    nsh, d, ns = w3.shape
    t = 512 if s % 512 == 0 else ROW_TILE
    nt = s // t
    ks = 2 if (ns // 2) % LANES == 0 else 1
    nk, kw = nsh * ks, ns // ks
    chunk = 128
    nchunk = t // chunk
    row = pl.BlockSpec((t, d), lambda i, k: (i, 0))
    vec = pl.BlockSpec((1, d), lambda i, k: (0, 0))
    rowwise = [x, dres] + ([post[0]] if post is not None else [])
    in_specs = [pl.BlockSpec((t, kw), lambda i, k: (i, k)), pl.BlockSpec((None, d, kw), lambda i, k: (k // ks, 0, k % ks)), vec]
    out_specs = [row, vec]
    out_shape = [jax.ShapeDtypeStruct((s, d), F32), jax.ShapeDtypeStruct((1, d), F32)]
    args = [dp, w3, g_pre]
    if post is not None:
        in_specs += [vec]
        out_specs += [row, vec]
        out_shape += [jax.ShapeDtypeStruct((s, d), BF16), jax.ShapeDtypeStruct((1, d), F32)]
        args += [post[1]]
    n_blocked = len(in_specs)
    in_specs += [ANY] * len(rowwise)
    args += rowwise
    host = _Host(comm, in_specs, out_specs, out_shape,
                 [pltpu.VMEM((t, d), F32), pltpu.VMEM((len(rowwise), 2, chunk, d), F32), pltpu.SemaphoreType.DMA((len(rowwise), 2))])

    def body(*refs):
        ins, outs, (acc_ref, buf_ref, sem_ref) = host.split(refs)
        dp_ref, w_ref, g_ref = ins[:3]
        hbm = ins[n_blocked:]
        dx_ref, dg_ref = outs[:2]
        tile = pl.program_id(0)
        kk = pl.program_id(1)
        first = tile == 0
        step = tile * nk + kk
        host.before(step, nt * nk)
        part = _nt(dp_ref[...], w_ref[...])

        @pl.when(kk == 0)
        def _():
            acc_ref[...] = part

        @pl.when(kk > 0)
        def _():
            acc_ref[...] += part

        def fetch(ci, slot):
            return [pltpu.make_async_copy(src.at[pl.ds(tile * t + ci * chunk, chunk)], buf_ref.at[n, slot], sem_ref.at[n, slot])
                    for n, src in enumerate(hbm)]

        @pl.when(kk == nk - 1)
        def _():
            dg = dgp = None
            for cp in fetch(0, 0):
                cp.start()
            for ci in range(nchunk):
                slot = ci % 2
                if ci + 1 < nchunk:
                    for cp in fetch(ci + 1, 1 - slot):
                        cp.start()
                for cp in fetch(ci, slot):
                    cp.wait()
                rows = slice(ci * chunk, (ci + 1) * chunk)
                xhat, r = _rms_stats(buf_ref[0, slot])
                dxn, dg_part = _rms_bwd(acc_ref[rows, :], xhat, r, g_ref[...])
                dx = buf_ref[1, slot] + dxn
                dx_ref[rows, :] = dx
                dg = dg_part if dg is None else dg + dg_part
                if post is not None:
                    ohat, ro = _rms_stats(buf_ref[2, slot])
                    do, dgp_part = _rms_bwd(dx, ohat, ro, ins[3][...])
                    outs[2][rows, :] = do.astype(BF16)
                    dgp = dgp_part if dgp is None else dgp + dgp_part
            _acc_rows(dg_ref, first, dg)
            if post is not None:
                _acc_rows(outs[3], first, dgp)

        host.after(step, nt * nk)

    res = pl.pallas_call(
        body, name=name, grid=(nt, nk), in_specs=host.in_specs, out_specs=host.out_specs, out_shape=host.out_shape,
        scratch_shapes=host.scratch, input_output_aliases=host.aliases,
        compiler_params=_cp("arbitrary", "arbitrary"))(*args, *host.args)
    return host.results(res)


def _half_add(g, r1, c_arr, name, after=None):
    nsh, rows, ns = g.shape
    h = rows // 2
    tr = min(ROW_TILE, h)
    per = h // tr

    def body(c_ref, g_ref, r_ref, *rest):
        rest[-1][...] = (g_ref[...].astype(F32) + r_ref[...].astype(F32)).astype(BF16)

    spec = pl.BlockSpec((None, tr, ns), lambda s, r, c: (s, r, 0))
    ordering = [] if after is None else [after]
    return pl.pallas_call(
        body, name=name,
        grid_spec=pltpu.PrefetchScalarGridSpec(
            num_scalar_prefetch=1, grid=(nsh, per),
            in_specs=[pl.BlockSpec((None, tr, ns), lambda s, r, c: (s, c[0] * per + r, 0)), spec] + [ANY] * len(ordering),
            out_specs=spec),
        out_shape=jax.ShapeDtypeStruct((nsh, h, ns), BF16), compiler_params=_cp("parallel", "parallel"))(c_arr, g, r1, *ordering)


def _sum_chips(hh, r2, mc_arr, name, after=None):
    _, h, ns = hh.shape
    tr = min(ROW_TILE, h)
    per = h // tr

    def body(mc_ref, h_ref, a_ref, b_ref, c_ref, *rest):
        rest[-1][...] = ((h_ref[...].astype(F32) + a_ref[...].astype(F32)) + b_ref[...].astype(F32)) + c_ref[...].astype(F32)

    got = lambda k: pl.BlockSpec((None, tr, ns), lambda r, mc: (k, r, 0))
    ordering = [] if after is None else [after]
    return pl.pallas_call(
        body, name=name,
        grid_spec=pltpu.PrefetchScalarGridSpec(
            num_scalar_prefetch=1, grid=(per,),
            in_specs=[pl.BlockSpec((None, tr, ns), lambda r, mc: (mc[0], r, 0)), got(0), got(1), got(2)] + [ANY] * len(ordering),
            out_specs=pl.BlockSpec((tr, ns), lambda r, mc: (mc[1] * per + r, 0))),
        out_shape=jax.ShapeDtypeStruct((2 * h, ns), F32), compiler_params=_cp("parallel"))(mc_arr, hh, r2, r2, r2, *ordering)


def _add2(a, b, name):
    def body(a_ref, b_ref, o_ref):
        o_ref[...] = a_ref[...] + b_ref[...]

    return pl.pallas_call(body, name=name, out_shape=jax.ShapeDtypeStruct(a.shape, a.dtype), compiler_params=_cp())(a, b)


def _sum_chips_ordered(s2, r2, mc_arr, name):
    rows, w = s2.shape
    rh = rows // 2

    def body(mc_ref, s_ref, a_ref, b_ref, c_ref, o_ref):
        me = mc_ref[0]
        acc = None
        for j in range(N_CHIPS):
            rel = jnp.bitwise_xor(me, j)
            v = jnp.where(rel == 0, s_ref[...], jnp.where(rel == 2, a_ref[...], jnp.where(rel == 1, b_ref[...], c_ref[...])))
            acc = v if acc is None else acc + v
        o_ref[...] = acc

    got = lambda k: pl.BlockSpec((None, rh, w), lambda i, mc: (k, 0, 0))
    return pl.pallas_call(
        body, name=name,
        grid_spec=pltpu.PrefetchScalarGridSpec(
            num_scalar_prefetch=1, grid=(1,),
            in_specs=[pl.BlockSpec((rh, w), lambda i, mc: (mc[1], 0)), got(0), got(1), got(2)],
            out_specs=pl.BlockSpec((rh, w), lambda i, mc: (mc[1], 0))),
        out_shape=jax.ShapeDtypeStruct((rows, w), F32), compiler_params=_cp("arbitrary"))(mc_arr, s2, r2, r2, r2)


def _adamw(w, g, m, v, name, comm=None):
    r, c = w.shape
    tr = ROW_TILE if r % ROW_TILE == 0 else r
    c1 = 1.0 / (1.0 - ADAM_B1 ** ADAM_STEP)
    c2 = 1.0 / (1.0 - ADAM_B2 ** ADAM_STEP)
    spec = pl.BlockSpec((tr, c), lambda i: (i, 0))
    host = _Host(comm, [spec] * 4, [spec] * 4, [jax.ShapeDtypeStruct((r, c), F32)] * 4, [])

    def body(*refs):
        (w_ref, g_ref, m_ref, v_ref), (go_ref, d_ref, nm_ref, nv_ref), _ = host.split(refs)
        step = pl.program_id(0)
        host.before(step, r // tr)
        gv = g_ref[...]
        go_ref[...] = gv
        nm = ADAM_B1 * m_ref[...] + (1.0 - ADAM_B1) * gv
        nv = ADAM_B2 * v_ref[...] + (1.0 - ADAM_B2) * (gv * gv)
        nm_ref[...] = nm
        nv_ref[...] = nv
        d_ref[...] = -ADAM_LR * ((nm * c1) / (jnp.sqrt(nv * c2) + ADAM_EPS) + ADAM_WD * w_ref[...])
        host.after(step, r // tr)

    outs = pl.pallas_call(
        body, name=name, grid=(r // tr,), in_specs=host.in_specs, out_specs=host.out_specs, out_shape=host.out_shape,
        scratch_shapes=host.scratch, input_output_aliases=host.aliases,
        compiler_params=_cp("arbitrary"))(w, g, m, v, *host.args)
    return host.results(outs)


def _gather_weights(bigs, pool_w, pack_w, pack_d, name):
    nb = len(bigs)
    smalls = [pool_w, pack_w, pack_d]
    q, cw, cd = pool_w.shape[1], pack_w.shape[1], pack_d.shape[1]
    pieces = [_GatherPlan(bigs, (j, j + 1, GATHER_PIECES)) for j in range(GATHER_PIECES)]
    for j, piece in enumerate(pieces):
        piece.base = 9 + j * piece.nsems

    def body(*refs):
        srcs, dsts = refs[:nb + 3], refs[nb + 3:2 * (nb + 3)]
        ssem, rsem, lsem = refs[2 * (nb + 3):]
        x, y, c, me, chips, sib = _place()

        def small_dst(n, chip):
            if n == 0:
                return dsts[nb].at[:, pl.ds(chip * q, q), :]
            return dsts[nb + n].at[:, pl.ds(chip * (cw if n == 1 else cd), cw if n == 1 else cd)]

        local = [pltpu.make_async_copy(srcs[nb + n], small_dst(n, me), lsem.at[n]) for n in range(3)]
        for cp in local:
            cp.start()
        sends = []
        for n in range(3):
            for k, chip in enumerate(chips):
                cp = _rcopy(srcs[nb + n], small_dst(n, me), ssem.at[3 * n + k], rsem.at[3 * n + k], (*chip, c))
                cp.start()
                sends.append(cp)
        big = (srcs[:nb], dsts[:nb], ssem, rsem)
        for stage in ("start", "relay", "relay_far", "finish"):
            for piece in pieces:
                getattr(piece, stage)(*big)
        for n in range(3):
            for k, chip in enumerate(chips):
                ref = small_dst(n, 2 * chip[0] + chip[1])
                _rcopy(ref, ref, ssem.at[3 * n + k], rsem.at[3 * n + k], (*chip, c)).wait_recv()
        for cp in sends:
            cp.wait_send()
        for cp in local:
            cp.wait()

    nsem = 9 + sum(piece.nsems for piece in pieces)
    out_shape = [jax.ShapeDtypeStruct(b.shape, b.dtype) for b in bigs]
    out_shape += [jax.ShapeDtypeStruct((pool_w.shape[0], N_CHIPS * q, pool_w.shape[2]), pool_w.dtype),
                  jax.ShapeDtypeStruct((pack_w.shape[0], N_CHIPS * cw), pack_w.dtype),
                  jax.ShapeDtypeStruct((pack_d.shape[0], N_CHIPS * cd), pack_d.dtype)]
    return pl.pallas_call(
        body, name=name, in_specs=[ANY] * (nb + 3), out_specs=[ANY] * (nb + 3), out_shape=out_shape,
        input_output_aliases={a: a for a in range(nb)},
        scratch_shapes=[pltpu.SemaphoreType.DMA((nsem,)), pltpu.SemaphoreType.DMA((nsem,)), pltpu.SemaphoreType.DMA((3,))],
        compiler_params=pltpu.CompilerParams(has_side_effects=True))(*bigs, *smalls)


def _swap_with_sibling(grads, wholes, name):
    n, nw = len(grads), len(wholes)
    halves = [g.shape[1] // 2 for g in grads]

    def body(*refs):
        srcs, dsts = refs[:n + nw], refs[n + nw:2 * (n + nw)]
        ssem, rsem = refs[2 * (n + nw):]
        x, y, c, me, chips, sib = _place()
        cps = [_rcopy(srcs[a].at[:, pl.ds((1 - c) * halves[a], halves[a]), :], dsts[a], ssem.at[a], rsem.at[a], sib)
               for a in range(n)]
        cps += [_rcopy(srcs[a], dsts[a], ssem.at[a], rsem.at[a], sib) for a in range(n, n + nw)]
        for cp in cps:
            cp.start()
        for cp in cps:
            cp.wait_recv()
        for cp in cps:
            cp.wait_send()

    out_shape = [jax.ShapeDtypeStruct((g.shape[0], h, g.shape[2]), g.dtype) for g, h in zip(grads, halves)]
    out_shape += [jax.ShapeDtypeStruct(w.shape, w.dtype) for w in wholes]
    return pl.pallas_call(
        body, name=name, in_specs=[ANY] * (n + nw), out_specs=[ANY] * (n + nw), out_shape=out_shape,
        scratch_shapes=[pltpu.SemaphoreType.DMA((n + nw,)), pltpu.SemaphoreType.DMA((n + nw,))],
        compiler_params=pltpu.CompilerParams(has_side_effects=True))(*grads, *wholes)


def _scatter_to_chips(halves_in, small, name):
    n = len(halves_in)
    rh = small.shape[0] // 2

    def body(*refs):
        srcs, dsts = refs[:n + 1], refs[n + 1:2 * (n + 1)]
        ssem, rsem = refs[2 * (n + 1):]
        x, y, c, me, chips, sib = _place()
        cps = []
        for a in range(n + 1):
            for k, chip in enumerate(chips):
                src = srcs[a].at[2 * chip[0] + chip[1]] if a < n else srcs[a].at[pl.ds(c * rh, rh)]
                cps.append(_rcopy(src, dsts[a].at[k], ssem.at[3 * a + k], rsem.at[3 * a + k], (*chip, c)))
        for cp in cps:
            cp.start()
        for cp in cps:
            cp.wait_recv()
        for cp in cps:
            cp.wait_send()

    out_shape = [jax.ShapeDtypeStruct((3,) + h.shape[1:], h.dtype) for h in halves_in]
    out_shape.append(jax.ShapeDtypeStruct((3, rh, small.shape[1]), small.dtype))
    return pl.pallas_call(
        body, name=name, in_specs=[ANY] * (n + 1), out_specs=[ANY] * (n + 1), out_shape=out_shape,
        scratch_shapes=[pltpu.SemaphoreType.DMA((3 * (n + 1),)), pltpu.SemaphoreType.DMA((3 * (n + 1),))],
        compiler_params=pltpu.CompilerParams(has_side_effects=True))(*halves_in, small)


def _join_halves(parts, name):
    n = len(parts)

    def body(*refs):
        srcs, dsts = refs[:n], refs[n:2 * n]
        ssem, rsem = refs[2 * n:]
        x, y, c, me, chips, sib = _place()
        cps = []
        for a in range(n):
            h = srcs[a].shape[0] // 2
            cps.append(_rcopy(srcs[a].at[pl.ds(c * h, h)], dsts[a].at[pl.ds(c * h, h)], ssem.at[a], rsem.at[a], sib))
        for cp in cps:
            cp.start()
        for a in range(n):
            h = srcs[a].shape[0] // 2
            theirs = dsts[a].at[pl.ds((1 - c) * h, h)]
            _rcopy(theirs, theirs, ssem.at[a], rsem.at[a], sib).wait_recv()
        for cp in cps:
            cp.wait_send()

    out_shape = [jax.ShapeDtypeStruct(p.shape, p.dtype) for p in parts]
    return pl.pallas_call(
        body, name=name, in_specs=[ANY] * n, out_specs=[ANY] * n, out_shape=out_shape,
        input_output_aliases={a: a for a in range(n)},
        scratch_shapes=[pltpu.SemaphoreType.DMA((n,)), pltpu.SemaphoreType.DMA((n,))],
        compiler_params=pltpu.CompilerParams(has_side_effects=True))(*parts)


def _scatter_start(h, name):
    land = (3,) + h.shape[1:]

    def body(h_ref, land_ref, send_sems, recv_sems, h_thru, land_thru, token):
        x, y, c, me, chips, sib = _place()
        for k, chip in enumerate(chips):
            _rcopy(h_ref.at[2 * chip[0] + chip[1]], land_ref.at[k], send_sems.at[k], recv_sems.at[k], (*chip, c)).start()
        token[...] = jnp.zeros_like(token)

    hbm = pl.BlockSpec(memory_space=pltpu.HBM)
    sem = pl.BlockSpec(memory_space=pltpu.SEMAPHORE)
    return pl.pallas_call(
        body, name=name,
        out_shape=(pltpu.SemaphoreType.DMA((3,)), pltpu.SemaphoreType.DMA((3,)), pltpu.HBM(h.shape, h.dtype),
                   pltpu.HBM(land, h.dtype), jax.ShapeDtypeStruct((8, LANES), F32)),
        in_specs=(hbm, hbm), out_specs=(sem, sem, hbm, hbm, pl.BlockSpec(memory_space=pltpu.VMEM)),
        input_output_aliases={0: 2, 1: 3},
        compiler_params=pltpu.CompilerParams(has_side_effects=pltpu.SideEffectType.DATAFLOW_SIDE_EFFECTING))(
            pltpu.with_memory_space_constraint(h, pltpu.HBM),
            pltpu.with_memory_space_constraint(lax.empty(land, h.dtype), pltpu.HBM))


def _scatter_wait(send_sems, recv_sems, h_thru, land_thru, after, name):
    def body(h_ref, land_ref, send_sems, recv_sems, after_ref, h_dead, got_ref):
        x, y, c, me, chips, sib = _place()
        for k, chip in enumerate(chips):
            cp = _rcopy(h_ref.at[2 * chip[0] + chip[1]], land_ref.at[k], send_sems.at[k], recv_sems.at[k], (*chip, c))
            cp.wait_send()
            cp.wait_recv()

    hbm = pl.BlockSpec(memory_space=pltpu.HBM)
    sem = pl.BlockSpec(memory_space=pltpu.SEMAPHORE)
    return pl.pallas_call(
        body, name=name,
        out_shape=(pltpu.HBM(h_thru.shape, h_thru.dtype), pltpu.HBM(land_thru.shape, land_thru.dtype)),
        in_specs=(hbm, hbm, sem, sem, ANY), out_specs=(hbm, hbm), input_output_aliases={0: 0, 1: 1},
        compiler_params=pltpu.CompilerParams(has_side_effects=pltpu.SideEffectType.DATAFLOW_SIDE_EFFECTING))(
            h_thru, land_thru, send_sems, recv_sems, after)


def _swap_start(g, name):
    h = g.shape[1] // 2
    land = (g.shape[0], h, g.shape[2])

    def body(g_ref, land_ref, send_sem, recv_sem, g_thru, land_thru, token):
        x, y, c, me, chips, sib = _place()
        _rcopy(g_ref.at[:, pl.ds((1 - c) * h, h), :], land_ref, send_sem.at[0], recv_sem.at[0], sib).start()
        token[...] = jnp.zeros_like(token)

    hbm = pl.BlockSpec(memory_space=pltpu.HBM)
    sem = pl.BlockSpec(memory_space=pltpu.SEMAPHORE)
    return pl.pallas_call(
        body, name=name,
        out_shape=(pltpu.SemaphoreType.DMA((1,)), pltpu.SemaphoreType.DMA((1,)), pltpu.HBM(g.shape, g.dtype),
                   pltpu.HBM(land, g.dtype), jax.ShapeDtypeStruct((8, LANES), F32)),
        in_specs=(hbm, hbm), out_specs=(sem, sem, hbm, hbm, pl.BlockSpec(memory_space=pltpu.VMEM)),
        input_output_aliases={0: 2, 1: 3},
        compiler_params=pltpu.CompilerParams(has_side_effects=pltpu.SideEffectType.DATAFLOW_SIDE_EFFECTING))(
            pltpu.with_memory_space_constraint(g, pltpu.HBM),
            pltpu.with_memory_space_constraint(lax.empty(land, g.dtype), pltpu.HBM))


def _swap_wait(send_sem, recv_sem, g_thru, land_thru, after, name):
    h = g_thru.shape[1] // 2

    def body(g_ref, land_ref, send_sem, recv_sem, after_ref, g_dead, got_ref):
        x, y, c, me, chips, sib = _place()
        cp = _rcopy(g_ref.at[:, pl.ds((1 - c) * h, h), :], land_ref, send_sem.at[0], recv_sem.at[0], sib)
        cp.wait_send()
        cp.wait_recv()

    hbm = pl.BlockSpec(memory_space=pltpu.HBM)
    sem = pl.BlockSpec(memory_space=pltpu.SEMAPHORE)
    return pl.pallas_call(
        body, name=name,
        out_shape=(pltpu.HBM(g_thru.shape, g_thru.dtype), pltpu.HBM(land_thru.shape, land_thru.dtype)),
        in_specs=(hbm, hbm, sem, sem, ANY), out_specs=(hbm, hbm), input_output_aliases={0: 0, 1: 1},
        compiler_params=pltpu.CompilerParams(has_side_effects=pltpu.SideEffectType.DATAFLOW_SIDE_EFFECTING))(
            g_thru, land_thru, send_sem, recv_sem, after)


def _share_half_start(small, name):
    rh = small.shape[0] // 2
    land = (3, rh, small.shape[1])

    def body(s_ref, land_ref, send_sems, recv_sems, s_thru, land_thru, token):
        x, y, c, me, chips, sib = _place()
        for k, chip in enumerate(chips):
            _rcopy(s_ref.at[pl.ds(c * rh, rh)], land_ref.at[k], send_sems.at[k], recv_sems.at[k], (*chip, c)).start()
        token[...] = jnp.zeros_like(token)

    hbm = pl.BlockSpec(memory_space=pltpu.HBM)
    sem = pl.BlockSpec(memory_space=pltpu.SEMAPHORE)
    return pl.pallas_call(
        body, name=name,
        out_shape=(pltpu.SemaphoreType.DMA((3,)), pltpu.SemaphoreType.DMA((3,)), pltpu.HBM(small.shape, small.dtype),
                   pltpu.HBM(land, small.dtype), jax.ShapeDtypeStruct((8, LANES), F32)),
        in_specs=(hbm, hbm), out_specs=(sem, sem, hbm, hbm, pl.BlockSpec(memory_space=pltpu.VMEM)),
        input_output_aliases={0: 2, 1: 3},
        compiler_params=pltpu.CompilerParams(has_side_effects=pltpu.SideEffectType.DATAFLOW_SIDE_EFFECTING))(
            pltpu.with_memory_space_constraint(small, pltpu.HBM),
            pltpu.with_memory_space_constraint(lax.empty(land, small.dtype), pltpu.HBM))


def _share_half_wait(send_sems, recv_sems, s_thru, land_thru, after, name):
    rh = s_thru.shape[0] // 2

    def body(s_ref, land_ref, send_sems, recv_sems, after_ref, s_dead, got_ref):
        x, y, c, me, chips, sib = _place()
        for k, chip in enumerate(chips):
            cp = _rcopy(s_ref.at[pl.ds(c * rh, rh)], land_ref.at[k], send_sems.at[k], recv_sems.at[k], (*chip, c))
            cp.wait_send()
            cp.wait_recv()

    hbm = pl.BlockSpec(memory_space=pltpu.HBM)
    sem = pl.BlockSpec(memory_space=pltpu.SEMAPHORE)
    return pl.pallas_call(
        body, name=name,
        out_shape=(pltpu.HBM(s_thru.shape, s_thru.dtype), pltpu.HBM(land_thru.shape, land_thru.dtype)),
        in_specs=(hbm, hbm, sem, sem, ANY), out_specs=(hbm, hbm), input_output_aliases={0: 0, 1: 1},
        compiler_params=pltpu.CompilerParams(has_side_effects=pltpu.SideEffectType.DATAFLOW_SIDE_EFFECTING))(
            s_thru, land_thru, send_sems, recv_sems, after)


def _join_start(parts, name):
    n = len(parts)

    def body(*refs):
        srcs, (send_sems, recv_sems), token = refs[:n], refs[n:n + 2], refs[-1]
        x, y, c, me, chips, sib = _place()
        for a, src in enumerate(srcs):
            h = src.shape[0] // 2
            mine = src.at[pl.ds(c * h, h)]
            _rcopy(mine, mine, send_sems.at[a], recv_sems.at[a], sib).start()
        token[...] = jnp.zeros_like(token)

    hbm = pl.BlockSpec(memory_space=pltpu.HBM)
    sem = pl.BlockSpec(memory_space=pltpu.SEMAPHORE)
    outs = pl.pallas_call(
        body, name=name,
        out_shape=(pltpu.SemaphoreType.DMA((n,)), pltpu.SemaphoreType.DMA((n,)))
        + tuple(pltpu.HBM(p.shape, p.dtype) for p in parts) + (jax.ShapeDtypeStruct((8, LANES), F32),),
        in_specs=(hbm,) * n, out_specs=(sem, sem) + (hbm,) * n + (pl.BlockSpec(memory_space=pltpu.VMEM),),
        input_output_aliases={a: 2 + a for a in range(n)},
        compiler_params=pltpu.CompilerParams(has_side_effects=pltpu.SideEffectType.DATAFLOW_SIDE_EFFECTING))(
            *[pltpu.with_memory_space_constraint(p, pltpu.HBM) for p in parts])
    return outs[0], outs[1], list(outs[2:2 + n]), outs[-1]


def _join_wait(send_sems, recv_sems, parts, after, name):
    n = len(parts)

    def body(*refs):
        srcs, (send_sems, recv_sems) = refs[:n], refs[n:n + 2]
        x, y, c, me, chips, sib = _place()
        for a, src in enumerate(srcs):
            h = src.shape[0] // 2
            mine, theirs = src.at[pl.ds(c * h, h)], src.at[pl.ds((1 - c) * h, h)]
            _rcopy(mine, theirs, send_sems.at[a], recv_sems.at[a], sib).wait_send()
            _rcopy(theirs, theirs, send_sems.at[a], recv_sems.at[a], sib).wait_recv()

    hbm = pl.BlockSpec(memory_space=pltpu.HBM)
    sem = pl.BlockSpec(memory_space=pltpu.SEMAPHORE)
    return pl.pallas_call(
        body, name=name, out_shape=tuple(pltpu.HBM(p.shape, p.dtype) for p in parts),
        in_specs=(hbm,) * n + (sem, sem, ANY), out_specs=(hbm,) * n, input_output_aliases={a: a for a in range(n)},
        compiler_params=pltpu.CompilerParams(has_side_effects=pltpu.SideEffectType.DATAFLOW_SIDE_EFFECTING))(
            *parts, send_sems, recv_sems, after)


def _pad_rows(a, rows):
    return jnp.pad(a, ((0, rows - a.shape[0]), (0, 0)))


def _stack_rows(parts, multiple):
    padded = [_pad_rows(p, -(-p.shape[0] // 8) * 8) for p in parts]
    starts, at = [], 0
    for p in padded:
        starts.append(at)
        at += p.shape[0]
    total = -(-at // multiple) * multiple
    if total > at:
        padded.append(jnp.zeros((total - at, parts[0].shape[1]), parts[0].dtype))
    return jnp.concatenate(padded, axis=0), starts


def kernel(x, ln_pre_even, w_in_even, pool_w, pool_scale, w_out_even, ln_post_even, ln_pre_odd, w_in_odd, sconv_w, dconv_w, dconv_b, cnorm_g, cnorm_b, w_out_odd, ln_post_odd, loss_target, m_ln_pre_even, m_w_in_even, m_pool_w, m_pool_scale, m_w_out_even, m_ln_post_even, m_ln_pre_odd, m_w_in_odd, m_sconv_w, m_dconv_w, m_dconv_b, m_cnorm_g, m_cnorm_b, m_w_out_odd, m_ln_post_odd, v_ln_pre_even, v_w_in_even, v_pool_w, v_pool_scale, v_w_out_even, v_ln_post_even, v_ln_pre_odd, v_w_in_odd, v_sconv_w, v_dconv_w, v_dconv_b, v_cnorm_g, v_cnorm_b, v_w_out_odd, v_ln_post_odd):
    _, s, d = x.shape
    half = d // 2
    cw = half // N_CHIPS
    ng, q, gd = pool_w.shape[1:]
    k3, k31 = sconv_w.shape[1], dconv_w.shape[1]
    x2d, tgt = x[0], loss_target[0]
    me = 2 * lax.axis_index("x") + lax.axis_index("y")
    core = lax.axis_index("c")
    c_arr = jnp.reshape(core, (1,)).astype(jnp.int32)
    me_arr = jnp.reshape(me, (1,)).astype(jnp.int32)
    mc_arr = jnp.stack([me, core]).astype(jnp.int32)

    shards = [w_in_even[0], w_out_even[0], w_in_odd[0], w_out_odd[0]]
    pool_w_b = _cast_bf16(pool_w[0].reshape(ng * q, gd), "cast_pool_w").reshape(ng, q, gd)
    pack_w, at_w = _stack_rows([sconv_w[0], dconv_w[0], dconv_b, cnorm_g, cnorm_b], 8)
    pack_d, at_d = _stack_rows([ln_pre_odd, ln_post_odd], 8)
    placed = [lax.dynamic_update_slice(jnp.zeros((ng, N_CHIPS * q, gd), BF16), pool_w_b, (0, me * q, 0)),
              lax.dynamic_update_slice(jnp.zeros((pack_w.shape[0], N_CHIPS * cw), F32), pack_w, (0, me * cw)),
              lax.dynamic_update_slice(jnp.zeros((pack_d.shape[0], d), F32), pack_d, (0, me * (d // N_CHIPS)))]
    plans = _Multi([_GatherPieces([_cast_bf16_own_slab(shards[0], me_arr, "cast_w0")], GATHER_PIECES, (0.3, 0.6)),
                    _SmallGatherPlan(placed, (q, cw, d // N_CHIPS))])
    h0, others, extra = _prep(x2d, ln_pre_even, shards[1:], me_arr, "prep_and_gather_first", plans)
    (win_e,), (pool_w_f, pack_w_f, pack_d_f) = plans.results(extra)
    slabs = [None] + others
    sconv_f = pack_w_f[at_w[0]:at_w[0] + k3]
    dconv_f = pack_w_f[at_w[1]:at_w[1] + k31]
    dconv_b_f, cnorm_g_f, cnorm_b_f = (pack_w_f[at_w[n]:at_w[n] + 1] for n in (2, 3, 4))
    ln_pre_odd_f = pack_d_f[at_d[0]:at_d[0] + 1]
    ln_post_odd_f = pack_d_f[at_d[1]:at_d[1] + 1]

    plans = _Multi([_GatherPlan([slabs[1]], at=(0.6, 0.88)), _GatherPlan([slabs[2]], (0, 1, 4), at=(0.6, 0.88))])
    p_e, extra = _mm_nn(h0, win_e, "proj_in_even", plans)
    (wout_e,), (win_o,) = plans.results(extra)
    wout_e = wout_e.reshape(d, d)
    att, ltot, (win_o,) = _sba_fwd(p_e, half, "sba_fwd", _GatherPlan([win_o], (1, 4, 4), at=(0.69, 0.94)))
    y_e = _even_mix_fwd(p_e, att, pool_w_f, pool_scale, d, "even_mix_fwd")
    o_e, x1, h1 = _mm_out_even(y_e, wout_e, x2d, ln_post_even, ln_pre_odd_f, "proj_out_even")
    p_o, (wout_o,) = _mm_nn(h1, win_o, "proj_in_odd", _GatherPlan([slabs[3]]))
    wout_o = wout_o.reshape(d, d)
    y_o, s3, d1 = _odd_mix_fwd(p_o, sconv_f, dconv_f, dconv_b_f, cnorm_g_f, cnorm_b_f, d, "odd_mix_fwd")
    do_o, dx2, loss_blk, dln_post_odd = _mm_out_odd(y_o, wout_o, x1, ln_post_odd_f, tgt, "proj_out_odd_loss")

    dy_o = _mm_nt(do_o, wout_o, "dy_odd")
    g_wout_o = _mm_tn(y_o, do_o, 1, "dw_out_odd")[0].reshape(N_CHIPS, d // N_CHIPS, d)
    (dbc, dgate_o, ds3, dd1, dcnorm_g, dcnorm_b, ddconv_b), (got,) = _odd_bwd_rows(
        p_o, s3, d1, dy_o, cnorm_g_f, cnorm_b_f, d, "odd_bwd_rows", _SwapPlan([g_wout_o]))
    h_wout_o = _half_add(g_wout_o, got, c_arr, "half_add_out_odd")
    dhc, dcc, dga, dgb, dsconv, ddconv = _odd_bwd_conv(p_o, ds3, dd1, sconv_f, dconv_f, d, "odd_bwd_conv")
    dp_o = jnp.concatenate([dhc, dbc, dcc, dga, dgb, dgate_o], axis=1)
    g_win_o, (s_wout_o,) = _mm_tn(h1, dp_o, N_CHIPS, "dw_in_odd", _ScatterPlan([h_wout_o]))
    (dx1, dln_pre_odd, do_e, dln_post_even), (got,) = _mm_in_bwd(
        dp_o, win_o, x1, ln_pre_odd_f, dx2, (o_e, ln_post_even), "dx_odd", _SwapPlan([g_win_o]))
    h_win_o = _half_add(g_win_o, got, c_arr, "half_add_in_odd")

    dy_e = _mm_nt(do_e, wout_e, "dy_even")
    g_wout_e = _mm_tn(y_e, do_e, 1, "dw_out_even")[0].reshape(N_CHIPS, d // N_CHIPS, d)
    (datt, du, dgate_e, dpool_scale, dpool_w), (got,) = _even_mix_bwd(
        p_e, att, dy_e, pool_w_f, pool_scale, d, "even_mix_bwd", _SwapPlan([g_wout_e]))
    h_wout_e = _half_add(g_wout_e, got, c_arr, "half_add_out_even")
    two = lambda v: v.reshape(2, half)
    small_parts = [dpool_scale, two(dln_post_even), two(dln_pre_odd), two(dln_post_odd),
                   dsconv, ddconv, ddconv_b, dcnorm_g, dcnorm_b, dpool_w.reshape(gd, half)]
    small, at_s = _stack_rows(small_parts, 16)
    plans = _Multi([_ScatterPlan([h_win_o]), _SendWholePlan([small])])
    dq, dk, dv, extra = _sba_bwd(p_e, ltot, datt, half, "sba_bwd", plans)
    (s_win_o,), (small1,) = plans.results(extra)
    small2 = _add2(small, small1, "small_add")
    dp_e = jnp.concatenate([dq, dk, dv, du, dgate_e], axis=1)
    plans = _Multi([_ScatterPlan([h_wout_e]), _ShareHalfPlan([small2])])
    g_win_e, extra = _mm_tn(h0, dp_e, N_CHIPS, "dw_in_even", plans)
    (s_wout_e,), (small_got,) = plans.results(extra)
    swap = _swap_start(g_win_e, "swap_in_even_start")
    pairs = [(h_wout_e, s_wout_e), (h_win_o, s_win_o), (h_wout_o, s_wout_o)]
    parts = []
    for n, (h, r) in enumerate(pairs):
        parts.append(_sum_chips(h, r, mc_arr, f"sum_chips{n + 1}", after=parts[-1] if parts else swap[4]))
    g_win_e, got = _swap_wait(*swap[:4], parts[-1], "swap_in_even_wait")
    parts.append(_sum_chips_ordered(small2, small_got, mc_arr, "small_sum"))
    join_sems = _join_start(parts, "join_first_start")
    h_win_e = _half_add(g_win_e, got, c_arr, "half_add_in_even", after=join_sems[3])
    send_sems, recv_sems, h_win_e, landing, token = _scatter_start(h_win_e, "scatter_in_even_start")
    (grad_x, dln_pre_even), _ = _mm_in_bwd(dp_e, win_e, x2d, ln_pre_even + token[0:1, 0:1], dx1, None, "dx_even")

    last, at_l = _stack_rows([two(dln_pre_even), jnp.pad(loss_blk[0:1], ((0, 0), (0, half - LANES)))], 16)
    (last1,) = _swap_with_sibling([], [last], "swap_last")
    last2 = _add2(last, last1, "last_add")
    share = _share_half_start(last2, "share_last_start")
    gw_out_e, gw_in_o, gw_out_o, red = _join_wait(*join_sems[:3], share[4], "join_first_wait")

    def rows(n, cnt):
        return red[at_s[n]:at_s[n] + cnt]

    def mine(a, width):
        return lax.dynamic_slice_in_dim(a, me * width, width, axis=1)

    quarter = d // N_CHIPS
    g_small = {
        "pool_scale": rows(0, 1),
        "ln_post_even": rows(1, 2).reshape(1, d),
        "ln_pre_odd": mine(rows(2, 2).reshape(1, d), quarter),
        "ln_post_odd": mine(rows(3, 2).reshape(1, d), quarter),
        "sconv_w": mine(rows(4, k3), cw),
        "dconv_w": mine(rows(5, k31), cw),
        "dconv_b": mine(rows(6, 1), cw),
        "cnorm_g": mine(rows(7, 1), cw),
        "cnorm_b": mine(rows(8, 1), cw),
        "pool_w": lax.dynamic_slice_in_dim(rows(9, gd).reshape(ng, gd, gd), me * q, q, axis=1).reshape(ng * q, gd),
    }
    w2d = {
        "ln_pre_even": ln_pre_even, "w_in_even": w_in_even[0], "pool_w": pool_w[0].reshape(ng * q, gd),
        "pool_scale": pool_scale, "w_out_even": w_out_even[0], "ln_post_even": ln_post_even, "ln_pre_odd": ln_pre_odd,
        "w_in_odd": w_in_odd[0], "sconv_w": sconv_w[0], "dconv_w": dconv_w[0], "dconv_b": dconv_b, "cnorm_g": cnorm_g,
        "cnorm_b": cnorm_b, "w_out_odd": w_out_odd[0], "ln_post_odd": ln_post_odd,
    }
    moments = {
        "ln_pre_even": (m_ln_pre_even, v_ln_pre_even), "w_in_even": (m_w_in_even, v_w_in_even),
        "pool_w": (m_pool_w, v_pool_w), "pool_scale": (m_pool_scale, v_pool_scale),
        "w_out_even": (m_w_out_even, v_w_out_even), "ln_post_even": (m_ln_post_even, v_ln_post_even),
        "ln_pre_odd": (m_ln_pre_odd, v_ln_pre_odd), "w_in_odd": (m_w_in_odd, v_w_in_odd),
        "sconv_w": (m_sconv_w, v_sconv_w), "dconv_w": (m_dconv_w, v_dconv_w), "dconv_b": (m_dconv_b, v_dconv_b),
        "cnorm_g": (m_cnorm_g, v_cnorm_g), "cnorm_b": (m_cnorm_b, v_cnorm_b),
        "w_out_odd": (m_w_out_odd, v_w_out_odd), "ln_post_odd": (m_ln_post_odd, v_ln_post_odd),
    }
    def update(name, g):
        m_in, v_in = moments[name]
        w = w2d[name]
        return _adamw(w, g, m_in.reshape(w.shape), v_in.reshape(w.shape), "adamw_" + name)[0]

    updates = {name: update(name, g) for name, g in (("w_in_odd", gw_in_o), ("w_out_even", gw_out_e), ("w_out_odd", gw_out_o))}
    last2, last_got = _share_half_wait(*share[:4], updates["w_out_odd"][1], "share_last_wait")
    last_sum = _sum_chips_ordered(last2, last_got, mc_arr, "last_sum")
    h_win_e, s_win_e = _scatter_wait(send_sems, recv_sems, h_win_e, landing, last_sum, "scatter_in_even_wait")
    last_sems = _join_start([_sum_chips(h_win_e, s_win_e, mc_arr, "sum_chips0"), last_sum], "join_last_start")
    for name, g in g_small.items():
        updates[name] = update(name, g)
    gw_in_e, red_last = _join_wait(*last_sems[:3], updates["pool_w"][1], "join_last_wait")
    loss = red_last[at_l[1], 0]
    updates["ln_pre_even"] = update("ln_pre_even", red_last[at_l[0]:at_l[0] + 2].reshape(1, d))
    updates["w_in_even"] = update("w_in_even", gw_in_e)
    outs = [[u.reshape(moments[name][0].shape) for u in updates[name]] for name in w2d]
    grads_out, deltas, new_m, new_v = zip(*outs)
    return (loss, grad_x.reshape(x.shape), *grads_out, *deltas, *new_m, *new_v)
```

```python
import functools
import math

import jax
import jax.numpy as jnp
from jax import lax
from jax.experimental import pallas as pl
from jax.experimental.pallas import tpu as pltpu

F32 = jnp.float32
BF16 = jnp.bfloat16
EPS = 1e-6
N_CHIPS = 4
VMEM_LIMIT_V7X = 56 << 20
HEAD_DIM = 128
ATT_BLOCK = 256
POOL_WINDOWS = (2, 4, 8, 16)
ROW_TILE = 256
POOL_HALO = 16
CONV_HALO = 32
LANES = 128
ADAM_LR, ADAM_B1, ADAM_B2, ADAM_EPS, ADAM_WD, ADAM_STEP = 0.001, 0.9, 0.999, 1e-08, 0.01, 10
MESH_ID = pl.DeviceIdType.MESH
ANY = pl.BlockSpec(memory_space=pl.ANY)


def _cp(*sem):
    return pltpu.CompilerParams(dimension_semantics=sem or None, vmem_limit_bytes=VMEM_LIMIT_V7X)


def _pick_tile(n, cap):
    best = None
    for t in range(LANES, min(n, cap) + 1, LANES):
        if n % t == 0:
            best = t
    assert best is not None, (n, cap)
    return best


def _sigmoid(x):
    return 1.0 / (1.0 + jnp.exp(-x))


def _silu(x):
    return x * _sigmoid(x)


def _dsilu(x):
    s = _sigmoid(x)
    return s * (1.0 + x * (1.0 - s))


def _log_sigmoid(z):
    return jnp.minimum(z, 0.0) - jnp.log(1.0 + jnp.exp(-jnp.abs(z)))


def _rms_stats(x):
    r = lax.rsqrt(jnp.mean(x * x, axis=-1, keepdims=True) + EPS)
    return x * r, r


def _rms_bwd(dh, xhat, r, g):
    dxh = dh * g
    dx = r * (dxh - xhat * jnp.mean(dxh * xhat, axis=-1, keepdims=True))
    return dx, jnp.sum(dh * xhat, axis=0, keepdims=True)


def _acc_rows(ref, first, val):
    @pl.when(first)
    def _():
        ref[...] = val

    @pl.when(jnp.logical_not(first))
    def _():
        ref[...] += val


def _rcopy(src, dst, ssem, rsem, dev):
    return pltpu.make_async_remote_copy(src_ref=src, dst_ref=dst, send_sem=ssem, recv_sem=rsem,
                                        device_id=dev, device_id_type=MESH_ID)


def _place():
    x, y, c = lax.axis_index("x"), lax.axis_index("y"), lax.axis_index("c")
    chips = [(1 - x, y), (x, 1 - y), (1 - x, 1 - y)]
    return x, y, c, 2 * x + y, chips, (x, y, 1 - c)


class _GatherPlan:
    PER_ARRAY = 7

    def __init__(self, arrays, part=(0, 1, 1), at=(0.5, 0.8)):
        self.operands = list(arrays)
        self.out_shapes = [jax.ShapeDtypeStruct(a.shape, a.dtype) for a in arrays]
        self.aliases = {i: i for i in range(len(arrays))}
        self.nsems = self.PER_ARRAY * len(arrays)
        self.base = 0
        self.halves = [a.shape[1] // 2 for a in arrays]
        self.part = part
        self.at = at

    def schedule(self):
        return [(0.0, self.start), (self.at[0], self.relay), (self.at[1], self.relay_far)]

    def _rows(self, ref, a, chip, half, quarter=None):
        lo, hi, n = self.part
        h = self.halves[a]
        first, size = half * h + lo * h // n, (hi - lo) * h // n
        if quarter is not None:
            first, size = first + quarter * (size // 2), size // 2
        return ref.at[chip, pl.ds(first, size)]

    def _copy(self, src, dst, a, n, ssem, rsem, dev):
        return _rcopy(src, dst, ssem.at[self.base + self.PER_ARRAY * a + n], rsem.at[self.base + self.PER_ARRAY * a + n], dev)

    def _own(self, ins, outs, ssem, rsem):
        x, y, c, me, chips, sib = _place()
        return [self._copy(self._rows(ins[a], a, me, c), self._rows(outs[a], a, me, c), a, k, ssem, rsem, (*chips[k], c))
                for a in range(len(ins)) for k in (0, 1)]

    def _relays(self, outs, ssem, rsem, a, k):
        x, y, c, me, chips, sib = _place()
        chip = 2 * chips[k][0] + chips[k][1]
        whole, quarter = self._rows(outs[a], a, chip, c), self._rows(outs[a], a, chip, c, k)
        return (self._copy(whole, whole, a, k, ssem, rsem, (*chips[k], c)),
                self._copy(quarter, quarter, a, 2 + k, ssem, rsem, (*chips[1 - k], c)),
                self._copy(whole, whole, a, 4 + k, ssem, rsem, sib))

    def _far(self, outs, ssem, rsem, a):
        x, y, c, me, chips, sib = _place()
        chip = 2 * chips[2][0] + chips[2][1]
        whole = self._rows(outs[a], a, chip, c)
        got = [self._copy(q, q, a, 2 + k, ssem, rsem, (*chips[1 - k], c))
               for k, q in enumerate([self._rows(outs[a], a, chip, c, 0), self._rows(outs[a], a, chip, c, 1)])]
        return got, self._copy(whole, whole, a, 6, ssem, rsem, sib)

    def start(self, ins, outs, ssem, rsem):
        for cp in self._own(ins, outs, ssem, rsem):
            cp.start()

    def relay(self, ins, outs, ssem, rsem):
        for a in range(len(outs)):
            for k in (0, 1):
                landed, onward, to_sibling = self._relays(outs, ssem, rsem, a, k)
                landed.wait_recv()
                onward.start()
                to_sibling.start()

    def relay_far(self, ins, outs, ssem, rsem):
        for a in range(len(outs)):
            got, to_sibling = self._far(outs, ssem, rsem, a)
            for cp in got:
                cp.wait_recv()
            to_sibling.start()

    def finish(self, ins, outs, ssem, rsem):
        x, y, c, me, chips, sib = _place()
        for a in range(len(outs)):
            for k in range(3):
                ref = self._rows(outs[a], a, 2 * chips[k][0] + chips[k][1], 1 - c)
                self._copy(ref, ref, a, 4 + k, ssem, rsem, sib).wait_recv()
        for cp in self._own(ins, outs, ssem, rsem):
            cp.wait_send()
        for a in range(len(outs)):
            for k in (0, 1):
                _, onward, to_sibling = self._relays(outs, ssem, rsem, a, k)
                onward.wait_send()
                to_sibling.wait_send()
            self._far(outs, ssem, rsem, a)[1].wait_send()


class _ScatterPlan:
    def __init__(self, arrays, part=(0, 1, 1), into=None):
        self.n = len(arrays)
        self.operands = list(arrays) + list(into or [])
        self.out_shapes = [jax.ShapeDtypeStruct((3,) + a.shape[1:], a.dtype) for a in arrays]
        self.aliases = {self.n + i: i for i in range(self.n)} if into else {}
        self.nsems = 3 * self.n
        self.base = 0
        self.part = part

    def _copies(self, ins, outs, ssem, rsem):
        x, y, c, me, chips, sib = _place()
        lo, hi, n = self.part
        out = []
        for a in range(self.n):
            h = ins[a].shape[1]
            rows = pl.ds(lo * h // n, (hi - lo) * h // n)
            for k, chip in enumerate(chips):
                out.append(_rcopy(ins[a].at[2 * chip[0] + chip[1], rows], outs[a].at[k, rows],
                                  ssem.at[self.base + 3 * a + k], rsem.at[self.base + 3 * a + k], (*chip, c)))
        return out

    def schedule(self):
        return [(0.0, self.start)]

    def start(self, ins, outs, ssem, rsem):
        for cp in self._copies(ins, outs, ssem, rsem):
            cp.start()

    def finish(self, ins, outs, ssem, rsem):
        cps = self._copies(ins, outs, ssem, rsem)
        for cp in cps:
            cp.wait_recv()
        for cp in cps:
            cp.wait_send()


class _ShareHalfPlan(_ScatterPlan):
    def __init__(self, arrays):
        super().__init__(arrays)
        self.out_shapes = [jax.ShapeDtypeStruct((3, a.shape[0] // 2, a.shape[1]), a.dtype) for a in arrays]

    def _copies(self, ins, outs, ssem, rsem):
        x, y, c, me, chips, sib = _place()
        out = []
        for a in range(self.n):
            rh = ins[a].shape[0] // 2
            for k, chip in enumerate(chips):
                out.append(_rcopy(ins[a].at[pl.ds(c * rh, rh)], outs[a].at[k],
                                  ssem.at[self.base + 3 * a + k], rsem.at[self.base + 3 * a + k], (*chip, c)))
        return out


class _SwapPlan:
    def __init__(self, grads):
        self.operands = list(grads)
        self.out_shapes = [jax.ShapeDtypeStruct((g.shape[0], g.shape[1] // 2, g.shape[2]), g.dtype) for g in grads]
        self.aliases = {}
        self.nsems = len(grads)
        self.base = 0

    def _copies(self, ins, outs, ssem, rsem):
        x, y, c, me, chips, sib = _place()
        out = []
        for a, src in enumerate(ins):
            h = src.shape[1] // 2
            out.append(_rcopy(src.at[:, pl.ds((1 - c) * h, h), :], outs[a], ssem.at[self.base + a], rsem.at[self.base + a], sib))
        return out

    def schedule(self):
        return [(0.0, self.start)]

    def start(self, ins, outs, ssem, rsem):
        for cp in self._copies(ins, outs, ssem, rsem):
            cp.start()

    def finish(self, ins, outs, ssem, rsem):
        cps = self._copies(ins, outs, ssem, rsem)
        for cp in cps:
            cp.wait_recv()
        for cp in cps:
            cp.wait_send()


class _SendWholePlan(_SwapPlan):
    def __init__(self, arrays):
        self.operands = list(arrays)
        self.out_shapes = [jax.ShapeDtypeStruct(a.shape, a.dtype) for a in arrays]
        self.aliases = {}
        self.nsems = len(arrays)
        self.base = 0

    def _copies(self, ins, outs, ssem, rsem):
        x, y, c, me, chips, sib = _place()
        return [_rcopy(src, outs[a], ssem.at[self.base + a], rsem.at[self.base + a], sib) for a, src in enumerate(ins)]


class _GatherPieces:
    def __init__(self, arrays, n, at):
        self.pieces = [_GatherPlan(arrays, (j, j + 1, n), at) for j in range(n)]
        self.operands, self.out_shapes, self.aliases = self.pieces[0].operands, self.pieces[0].out_shapes, self.pieces[0].aliases
        self.nsems = sum(p.nsems for p in self.pieces)
        self.at = at
        self.base = 0

    @property
    def base(self):
        return self.pieces[0].base

    @base.setter
    def base(self, value):
        for j, p in enumerate(self.pieces):
            p.base = value + j * p.nsems

    def schedule(self):
        return [(0.0, self.start), (self.at[0], self.relay), (self.at[1], self.relay_far)]

    def _each(self, what, *a):
        for p in self.pieces:
            getattr(p, what)(*a)

    def start(self, *a):
        self._each("start", *a)

    def relay(self, *a):
        self._each("relay", *a)

    def relay_far(self, *a):
        self._each("relay_far", *a)

    def finish(self, *a):
        self._each("finish", *a)


class _SmallGatherPlan:
    def __init__(self, arrays, widths):
        self.operands = list(arrays)
        self.out_shapes = [jax.ShapeDtypeStruct(a.shape, a.dtype) for a in arrays]
        self.aliases = {i: i for i in range(3)}
        self.nsems = 9
        self.base = 0
        self.widths = widths

    def _part(self, ref, n, chip):
        w = self.widths[n]
        return ref.at[:, pl.ds(chip * w, w), :] if n == 0 else ref.at[:, pl.ds(chip * w, w)]

    def _copies(self, ins, outs, ssem, rsem, own):
        x, y, c, me, chips, sib = _place()
        out = []
        for n in range(3):
            for k, chip in enumerate(chips):
                which = me if own else 2 * chip[0] + chip[1]
                out.append(_rcopy(self._part(ins[n], n, which), self._part(outs[n], n, which),
                                  ssem.at[self.base + 3 * n + k], rsem.at[self.base + 3 * n + k], (*chip, c)))
        return out

    def schedule(self):
        return [(0.0, self.start)]

    def start(self, ins, outs, ssem, rsem):
        for cp in self._copies(ins, outs, ssem, rsem, True):
            cp.start()

    def finish(self, ins, outs, ssem, rsem):
        for cp in self._copies(ins, outs, ssem, rsem, False):
            cp.wait_recv()
        for cp in self._copies(ins, outs, ssem, rsem, True):
            cp.wait_send()


class _Multi:
    def __init__(self, plans):
        self.plans = plans
        self.operands, self.out_shapes, self.aliases, self.nsems = [], [], {}, 0
        self.spans = []
        for p in plans:
            ni, no = len(self.operands), len(self.out_shapes)
            self.spans.append((ni, ni + len(p.operands), no, no + len(p.out_shapes)))
            self.aliases.update({ni + i: no + j for i, j in p.aliases.items()})
            p.base = self.nsems
            self.nsems += p.nsems
            self.operands += p.operands
            self.out_shapes += p.out_shapes

    def schedule(self):
        def bound(fn, span):
            i0, i1, o0, o1 = span
            return lambda ins, outs, ssem, rsem: fn(ins[i0:i1], outs[o0:o1], ssem, rsem)

        stages = [(at, bound(fn, span)) for p, span in zip(self.plans, self.spans) for at, fn in p.schedule()]
        return sorted(stages, key=lambda s: s[0])

    def finish(self, ins, outs, ssem, rsem):
        for p, (i0, i1, o0, o1) in zip(self.plans, self.spans):
            p.finish(ins[i0:i1], outs[o0:o1], ssem, rsem)

    def results(self, extra):
        return [list(extra[o0:o1]) for (_, _, o0, o1) in self.spans]


class _Host:
    def __init__(self, comm, in_specs, out_specs, out_shape, scratch, prefetch=0):
        self.comm = comm
        self.n_in, self.n_out = len(in_specs), len(out_specs)
        self.in_specs, self.out_specs, self.out_shape, self.scratch = list(in_specs), list(out_specs), list(out_shape), list(scratch)
        self.aliases = {}
        self.args = []
        if comm is not None:
            self.in_specs += [ANY] * len(comm.operands)
            self.out_specs += [ANY] * len(comm.out_shapes)
            self.out_shape += comm.out_shapes
            self.scratch += [pltpu.SemaphoreType.DMA((comm.nsems,)), pltpu.SemaphoreType.DMA((comm.nsems,))]
            self.aliases = {prefetch + self.n_in + i: self.n_out + j for i, j in comm.aliases.items()}
            self.args = list(comm.operands)

    def split(self, refs):
        nc = len(self.args)
        nco = len(self.out_shape) - self.n_out
        ins, p = refs[:self.n_in], self.n_in + nc
        outs, rest = refs[p:p + self.n_out], refs[p + self.n_out + nco:]
        self._cargs = None
        if self.comm is not None:
            self._cargs = (refs[self.n_in:p], refs[p + self.n_out:p + self.n_out + nco], rest[-2], rest[-1])
            rest = rest[:-2]
        return ins, outs, rest

    def before(self, step, total):
        if self.comm is None:
            return

        for at, stage in self.comm.schedule():
            pl.when(step == min(total - 1, int(at * total)))(functools.partial(stage, *self._cargs))

    def after(self, step, total):
        if self.comm is None:
            return

        @pl.when(step == total - 1)
        def _():
            self.comm.finish(*self._cargs)

    def results(self, outs):
        return outs[:self.n_out], outs[self.n_out:]


def _cast_bf16(x, name):
    r, c = x.shape
    tr = ROW_TILE if r % ROW_TILE == 0 else r

    def body(x_ref, o_ref):
        o_ref[...] = x_ref[...].astype(BF16)

    return pl.pallas_call(
        body, name=name, grid=(r // tr,),
        in_specs=[pl.BlockSpec((tr, c), lambda i: (i, 0))],
        out_specs=pl.BlockSpec((tr, c), lambda i: (i, 0)),
        out_shape=jax.ShapeDtypeStruct((r, c), BF16), compiler_params=_cp("parallel"))(x)


def _cast_bf16_own_slab(x, me_arr, name):
    r, c = x.shape
    tr = ROW_TILE if r % ROW_TILE == 0 else r

    def body(me_ref, x_ref, o_ref):
        o_ref[...] = x_ref[...].astype(BF16)

    return pl.pallas_call(
        body, name=name,
        grid_spec=pltpu.PrefetchScalarGridSpec(
            num_scalar_prefetch=1, grid=(r // tr,),
            in_specs=[pl.BlockSpec((tr, c), lambda i, me: (i, 0))],
            out_specs=pl.BlockSpec((None, tr, c), lambda i, me: (me[0], i, 0))),
        out_shape=jax.ShapeDtypeStruct((N_CHIPS, r, c), BF16), compiler_params=_cp("parallel"))(me_arr, x)


def _prep(x, g, shards, me_arr, name, comm):
    s, d = x.shape
    steps = s // ROW_TILE
    tiles = [(w.shape[0] // steps, w.shape[1]) for w in shards]
    assert all(w.shape[0] % steps == 0 for w in shards)
    in_specs = [pl.BlockSpec((ROW_TILE, d), lambda i, me: (i, 0)), pl.BlockSpec((1, d), lambda i, me: (0, 0))]
    in_specs += [pl.BlockSpec(t, lambda i, me: (i, 0)) for t in tiles]
    out_specs = [pl.BlockSpec((ROW_TILE, d), lambda i, me: (i, 0))]
    out_specs += [pl.BlockSpec((None,) + t, lambda i, me: (me[0], i, 0)) for t in tiles]
    out_shape = [jax.ShapeDtypeStruct((s, d), BF16)] + [jax.ShapeDtypeStruct((N_CHIPS,) + w.shape, BF16) for w in shards]
    host = _Host(comm, in_specs, out_specs, out_shape, [], prefetch=1)

    def body(me_ref, *refs):
        (x_ref, g_ref, *w_refs), (h_ref, *slab_refs), _ = host.split(refs)
        step = pl.program_id(0)
        host.before(step, steps)
        xhat, _ = _rms_stats(x_ref[...])
        h_ref[...] = (xhat * g_ref[...]).astype(BF16)
        for w_ref, slab_ref in zip(w_refs, slab_refs):
            slab_ref[...] = w_ref[...].astype(BF16)
        host.after(step, steps)

    outs = pl.pallas_call(
        body, name=name,
        grid_spec=pltpu.PrefetchScalarGridSpec(num_scalar_prefetch=1, grid=(steps,), in_specs=host.in_specs,
                                               out_specs=host.out_specs, scratch_shapes=host.scratch),
        out_shape=host.out_shape, input_output_aliases=host.aliases,
        compiler_params=_cp("arbitrary"))(me_arr, x, g, *shards, *host.args)
    (h, *slabs), extra = host.results(outs)
    return h, slabs, extra


def _mm_nn(a, w3, name, comm=None):
    m, k = a.shape
    nsh, _, ns = w3.shape
    tm = 512 if m % 512 == 0 else ROW_TILE
    tn = _pick_tile(ns, 1024)
    per = ns // tn
    grid = (nsh * per, m // tm)
    host = _Host(comm,
                 [pl.BlockSpec((tm, k), lambda n, i: (i, 0)), pl.BlockSpec((None, k, tn), lambda n, i: (n // per, 0, n % per))],
                 [pl.BlockSpec((tm, tn), lambda n, i: (i, n))], [jax.ShapeDtypeStruct((m, nsh * ns), F32)], [])

    def body(*refs):
        (a_ref, w_ref), (o_ref,), _ = host.split(refs)
        step = pl.program_id(0) * grid[1] + pl.program_id(1)
        host.before(step, grid[0] * grid[1])
        o_ref[...] = jnp.dot(a_ref[...], w_ref[...], preferred_element_type=F32)
        host.after(step, grid[0] * grid[1])

    outs = pl.pallas_call(
        body, name=name, grid=grid, in_specs=host.in_specs, out_specs=host.out_specs, out_shape=host.out_shape,
        scratch_shapes=host.scratch, input_output_aliases=host.aliases,
        compiler_params=_cp("arbitrary", "arbitrary"))(a, w3, *host.args)
    (out,), extra = host.results(outs)
    return out, extra


def _mm_nt(a, b, name):
    m, k = a.shape
    n = b.shape[0]
    tm = 512 if m % 512 == 0 else ROW_TILE

    def body(a_ref, b_ref, o_ref):
        o_ref[...] = lax.dot_general(a_ref[...], b_ref[...], (((1,), (1,)), ((), ())), preferred_element_type=F32)

    return pl.pallas_call(
        body, name=name, grid=(m // tm,),
        in_specs=[pl.BlockSpec((tm, k), lambda i: (i, 0)), pl.BlockSpec((n, k), lambda i: (0, 0))],
        out_specs=pl.BlockSpec((tm, n), lambda i: (i, 0)),
        out_shape=jax.ShapeDtypeStruct((m, n), F32), compiler_params=_cp("parallel"))(a, b)


def _mm_tn(a, b, nsh, name, comm=None):
    s, m = a.shape
    n = b.shape[1]
    ns = n // nsh
    tm = 512 if m % 512 == 0 else ROW_TILE
    tn = _pick_tile(ns, 1024)
    per = ns // tn
    grid = (nsh * per, m // tm)
    host = _Host(comm, [pl.BlockSpec((s, tm), lambda j, i: (0, i)), pl.BlockSpec((s, tn), lambda j, i: (0, j))],
                 [pl.BlockSpec((None, tm, tn), lambda j, i: (j // per, i, j % per))],
                 [jax.ShapeDtypeStruct((nsh, m, ns), BF16)], [])

    def body(*refs):
        (a_ref, b_ref), (o_ref,), _ = host.split(refs)
        step = pl.program_id(0) * grid[1] + pl.program_id(1)
        host.before(step, grid[0] * grid[1])
        o_ref[...] = lax.dot_general(a_ref[...], b_ref[...], (((0,), (0,)), ((), ())),
                                     preferred_element_type=F32).astype(BF16)
        host.after(step, grid[0] * grid[1])

    outs = pl.pallas_call(
        body, name=name, grid=grid, in_specs=host.in_specs, out_specs=host.out_specs, out_shape=host.out_shape,
        scratch_shapes=host.scratch, input_output_aliases=host.aliases,
        compiler_params=_cp("arbitrary", "arbitrary"))(a, b, *host.args)
    (out,), extra = host.results(outs)
    return out, extra


def _tri(n, rel):
    row = lax.broadcasted_iota(jnp.int32, (2 * n, n), 0)
    col = lax.broadcasted_iota(jnp.int32, (2 * n, n), 1)
    return jnp.where(rel(jnp.where(row >= n, row - n, row), col), 1.0, 0.0).astype(BF16)


def _dot_split(x, tri2):
    hi = x.astype(BF16)
    lo = (x - hi.astype(F32)).astype(BF16)
    return jnp.dot(jnp.concatenate([hi, lo], axis=1), tri2, preferred_element_type=F32)


def _nt(a, b):
    return lax.dot_general(a, b, (((1,), (1,)), ((), ())), preferred_element_type=F32)


def _tn(a, b):
    return lax.dot_general(a, b, (((0,), (0,)), ((), ())), preferred_element_type=F32)


def _heads_per_step(nh):
    return max(h for h in (1, 2, 4) if nh % h == 0)


def _sba_fwd(p, sbw, name, comm=None):
    s = p.shape[0]
    nh = sbw // HEAD_DIM
    hp = _heads_per_step(nh)
    ngrp, hw = nh // hp, hp * HEAD_DIM
    blk = ATT_BLOCK
    nq = s // blk
    scale = 1.0 / math.sqrt(HEAD_DIM)
    host = _Host(comm,
                 [pl.BlockSpec((blk, hw), lambda g, i: (i, g)),
                  pl.BlockSpec((s, hw), lambda g, i: (0, ngrp + g)),
                  pl.BlockSpec((s, hw), lambda g, i: (0, 2 * ngrp + g))],
                 [pl.BlockSpec((blk, hw), lambda g, i: (i, g))] * 2,
                 [jax.ShapeDtypeStruct((s, sbw), F32)] * 2,
                 [pltpu.VMEM((s, hw), BF16)] * 2)

    def body(*refs):
        (q_ref, k_ref, v_ref), (o_ref, lt_ref), (kb_ref, vb_ref) = host.split(refs)
        i = pl.program_id(1)
        step = pl.program_id(0) * nq + i
        host.before(step, ngrp * nq)

        @pl.when(i == 0)
        def _():
            kb_ref[...] = k_ref[...].astype(BF16)
            vb_ref[...] = v_ref[...].astype(BF16)

        heads = [slice(h * HEAD_DIM, (h + 1) * HEAD_DIM) for h in range(hp)]
        qs = [q_ref[:, hd].astype(BF16) for hd in heads]
        later = _tri(blk, lambda r, c: r > c)
        causal = lax.broadcasted_iota(jnp.int32, (blk, blk), 1) < lax.broadcasted_iota(jnp.int32, (blk, blk), 0)

        def key_block(j, carry, diagonal):
            rows = pl.ds(pl.multiple_of(j * blk, blk), blk)
            hs = range(hp)
            z = [_nt(qs[h], kb_ref[rows, heads[h]]) * scale for h in hs]
            ls = [_log_sigmoid(z[h]) for h in hs]
            lm = [jnp.where(causal, ls[h] - z[h], 0.0) if diagonal else ls[h] - z[h] for h in hs]
            stay = [_dot_split(lm[h], later) for h in hs]
            w = [jnp.exp(ls[h] + stay[h] + carry[h][1]) for h in hs]
            if diagonal:
                w = [jnp.where(causal, w[h], 0.0) for h in hs]
            acc = [carry[h][0] + jnp.dot(w[h].astype(BF16), vb_ref[rows, heads[h]], preferred_element_type=F32) for h in hs]
            return tuple((acc[h], carry[h][1] + jnp.sum(lm[h], axis=1, keepdims=True)) for h in hs)

        init = tuple((jnp.zeros((blk, HEAD_DIM), F32), jnp.zeros((blk, 1), F32)) for _ in heads)
        carry = key_block(i, init, True)
        carry = lax.fori_loop(0, i, lambda n, c: key_block(i - 1 - n, c, False), carry)
        for h, hd in enumerate(heads):
            o_ref[:, hd] = carry[h][0]
            lt_ref[:, hd] = jnp.broadcast_to(carry[h][1], (blk, HEAD_DIM))
        host.after(step, ngrp * nq)

    outs = pl.pallas_call(
        body, name=name, grid=(ngrp, nq), in_specs=host.in_specs, out_specs=host.out_specs, out_shape=host.out_shape,
        scratch_shapes=host.scratch, input_output_aliases=host.aliases,
        compiler_params=_cp("arbitrary", "arbitrary"))(p, p, p, *host.args)
    (out, ltot), extra = host.results(outs)
    return out, ltot, extra


def _sba_bwd(p, ltot, dout, sbw, name, comm=None):
    s = p.shape[0]
    nh = sbw // HEAD_DIM
    hp = _heads_per_step(nh)
    ngrp, hw = nh // hp, hp * HEAD_DIM
    blk = ATT_BLOCK
    nq = s // blk
    scale = 1.0 / math.sqrt(HEAD_DIM)
    blk_spec = pl.BlockSpec((blk, hw), lambda g, i: (i, g))
    col_spec = pl.BlockSpec((s, hw), lambda g, i: (0, g))
    host = _Host(comm,
                 [blk_spec, pl.BlockSpec((s, hw), lambda g, i: (0, ngrp + g)),
                  pl.BlockSpec((s, hw), lambda g, i: (0, 2 * ngrp + g)), blk_spec, blk_spec],
                 [blk_spec, col_spec, col_spec], [jax.ShapeDtypeStruct((s, sbw), BF16)] * 3,
                 [pltpu.VMEM((s, hw), BF16)] * 2 + [pltpu.VMEM((s, hw), F32)] * 2)

    def body(*refs):
        (q_ref, k_ref, v_ref, lt_ref, do_ref), (dq_ref, dk_ref, dv_ref), (kb_ref, vb_ref, dka_ref, dva_ref) = host.split(refs)
        i = pl.program_id(1)
        step = pl.program_id(0) * nq + i
        host.before(step, ngrp * nq)

        @pl.when(i == 0)
        def _():
            kb_ref[...] = k_ref[...].astype(BF16)
            vb_ref[...] = v_ref[...].astype(BF16)
            dka_ref[...] = jnp.zeros_like(dka_ref)
            dva_ref[...] = jnp.zeros_like(dva_ref)

        heads = [slice(h * HEAD_DIM, (h + 1) * HEAD_DIM) for h in range(hp)]
        qs = [q_ref[:, hd].astype(BF16) for hd in heads]
        dos = [do_ref[:, hd].astype(BF16) for hd in heads]
        ltots = [lt_ref[:, h * HEAD_DIM:h * HEAD_DIM + 1] for h in range(hp)]
        upto = _tri(blk, lambda r, c: r <= c)
        before = _tri(blk, lambda r, c: r < c)
        causal = lax.broadcasted_iota(jnp.int32, (blk, blk), 1) < lax.broadcasted_iota(jnp.int32, (blk, blk), 0)

        def key_block(j, carry, diagonal):
            rows = pl.ds(pl.multiple_of(j * blk, blk), blk)
            hs = range(hp)
            kj = [kb_ref[rows, heads[h]] for h in hs]
            vj = [vb_ref[rows, heads[h]] for h in hs]
            z = [_nt(qs[h], kj[h]) * scale for h in hs]
            dw = [_nt(dos[h], vj[h]) for h in hs]
            ls = [_log_sigmoid(z[h]) for h in hs]
            lm = [jnp.where(causal, ls[h] - z[h], 0.0) if diagonal else ls[h] - z[h] for h in hs]
            stay = [ltots[h] - carry[h][1] - _dot_split(lm[h], upto) for h in hs]
            w = [jnp.exp(ls[h] + stay[h]) for h in hs]
            if diagonal:
                w = [jnp.where(causal, w[h], 0.0) for h in hs]
            da = [dw[h] * w[h] for h in hs]
            sig = [jnp.exp(ls[h]) for h in hs]
            chain = [sig[h] * (carry[h][2] + _dot_split(da[h], before)) for h in hs]
            if diagonal:
                chain = [jnp.where(causal, chain[h], 0.0) for h in hs]
            dzb = [((da[h] * (1.0 - sig[h]) - chain[h]) * scale).astype(BF16) for h in hs]
            dq = [carry[h][0] + jnp.dot(dzb[h], kj[h], preferred_element_type=F32) for h in hs]
            for h in hs:
                dka_ref[rows, heads[h]] += _tn(dzb[h], qs[h])
            for h in hs:
                dva_ref[rows, heads[h]] += _tn(w[h].astype(BF16), dos[h])
            return tuple((dq[h], carry[h][1] + jnp.sum(lm[h], axis=1, keepdims=True),
                          carry[h][2] + jnp.sum(da[h], axis=1, keepdims=True)) for h in hs)

        zero = jnp.zeros((blk, 1), F32)
        init = tuple((jnp.zeros((blk, HEAD_DIM), F32), zero, zero) for _ in heads)
        carry = lax.fori_loop(0, i, lambda j, c: key_block(j, c, False), init)
        carry = key_block(i, carry, True)
        for h, hd in enumerate(heads):
            dq_ref[:, hd] = carry[h][0].astype(BF16)

        @pl.when(i == nq - 1)
        def _():
            dk_ref[...] = dka_ref[...].astype(BF16)
            dv_ref[...] = dva_ref[...].astype(BF16)

        host.after(step, ngrp * nq)

    outs = pl.pallas_call(
        body, name=name, grid=(ngrp, nq), in_specs=host.in_specs, out_specs=host.out_specs, out_shape=host.out_shape,
        scratch_shapes=host.scratch, input_output_aliases=host.aliases,
        compiler_params=_cp("arbitrary", "arbitrary"))(p, p, p, ltot, dout, *host.args)
    (dq, dk, dv), extra = host.results(outs)
    return dq, dk, dv, extra


def _pool_groups(pad_ref, tile, row0, gd, halo):
    row = row0 + lax.broadcasted_iota(jnp.int32, (tile, 1), 0)
    out = []
    for gi, win in enumerate(POOL_WINDOWS):
        cs = slice(gi * gd, (gi + 1) * gd)
        tok = pad_ref[halo:halo + tile, cs]
        acc = tok
        for j in range(1, win):
            acc = acc + pad_ref[halo - j:halo - j + tile, cs]
        cnt = jnp.minimum(win, row + 1).astype(F32)
        out.append(acc / cnt - tok)
    return out


def _even_mix_fwd(p, att, pool_w, pool_scale, d, name):
    s = p.shape[0]
    half = d // 2
    gd = half // len(POOL_WINDOWS)
    t, hb = ROW_TILE, POOL_HALO

    def body(u_ref, uh_ref, g_ref, a_ref, pw_ref, sc_ref, y_ref, pad_ref):
        i = pl.program_id(0)
        pad_ref[0:hb, :] = jnp.where(i > 0, uh_ref[...], 0.0)
        pad_ref[hb:, :] = u_ref[...]
        pooled = _pool_groups(pad_ref, t, i * t, gd, hb)
        for gi in range(len(POOL_WINDOWS)):
            cs = slice(gi * gd, (gi + 1) * gd)
            po = jnp.dot(pooled[gi].astype(BF16), pw_ref[gi], preferred_element_type=F32) * sc_ref[:, cs]
            y_ref[:, half + gi * gd:half + (gi + 1) * gd] = (po * _silu(g_ref[:, half + gi * gd:half + (gi + 1) * gd])).astype(BF16)
        y_ref[:, :half] = (a_ref[...] * _silu(g_ref[:, :half])).astype(BF16)

    return pl.pallas_call(
        body, name=name, grid=(s // t,),
        in_specs=[pl.BlockSpec((t, half), lambda i: (i, 3)),
                  pl.BlockSpec((hb, half), lambda i: (jnp.maximum(i * (t // hb) - 1, 0), 3)),
                  pl.BlockSpec((t, d), lambda i: (i, 2)),
                  pl.BlockSpec((t, half), lambda i: (i, 0)),
                  pl.BlockSpec(pool_w.shape, lambda i: (0, 0, 0)),
                  pl.BlockSpec((1, half), lambda i: (0, 0))],
        out_specs=pl.BlockSpec((t, d), lambda i: (i, 0)),
        out_shape=jax.ShapeDtypeStruct((s, d), BF16),
        scratch_shapes=[pltpu.VMEM((hb + t, half), F32)],
        compiler_params=_cp("parallel"))(p, p, p, att, pool_w, pool_scale)


def _even_mix_bwd(p, att, dy, pool_w, pool_scale, d, name, comm=None):
    s = p.shape[0]
    half = d // 2
    ng = len(POOL_WINDOWS)
    gd = half // ng
    t, hb = ROW_TILE, POOL_HALO
    nt = s // t
    host = _Host(
        comm,
        [pl.BlockSpec((t, half), lambda i: (i, 3)),
         pl.BlockSpec((hb, half), lambda i: (jnp.maximum(i * (t // hb) - 1, 0), 3)),
         pl.BlockSpec((t, d), lambda i: (i, 2)),
         pl.BlockSpec((hb, half), lambda i: (jnp.minimum((i + 1) * (t // hb), s // hb - 1), 5)),
         pl.BlockSpec((t, half), lambda i: (i, 0)),
         pl.BlockSpec((t, d), lambda i: (i, 0)),
         pl.BlockSpec((hb, half), lambda i: (jnp.minimum((i + 1) * (t // hb), s // hb - 1), 1)),
         pl.BlockSpec(pool_w.shape, lambda i: (0, 0, 0)),
         pl.BlockSpec((1, half), lambda i: (0, 0))],
        [pl.BlockSpec((t, half), lambda i: (i, 0)),
         pl.BlockSpec((t, half), lambda i: (i, 0)),
         pl.BlockSpec((t, d), lambda i: (i, 0)),
         pl.BlockSpec((1, half), lambda i: (0, 0)),
         pl.BlockSpec((ng, gd, gd), lambda i: (0, 0, 0))],
        [jax.ShapeDtypeStruct((s, half), F32), jax.ShapeDtypeStruct((s, half), BF16),
         jax.ShapeDtypeStruct((s, d), BF16), jax.ShapeDtypeStruct((1, half), F32),
         jax.ShapeDtypeStruct((ng, gd, gd), F32)],
        [pltpu.VMEM((hb + t, half), F32), pltpu.VMEM((t + hb, half), F32)])

    def body(*refs):
        ((u_ref, uh_ref, g_ref, gh_ref, a_ref, dy_ref, dyh_ref, pw_ref, sc_ref),
         (da_ref, du_ref, dg_ref, dsc_ref, dpw_ref), (pad_ref, dn_ref)) = host.split(refs)
        i = pl.program_id(0)
        host.before(i, nt)
        first = i == 0
        pad_ref[0:hb, :] = jnp.where(i > 0, uh_ref[...], 0.0)
        pad_ref[hb:, :] = u_ref[...]
        pooled = _pool_groups(pad_ref, t, i * t, gd, hb)
        g1 = g_ref[:, :half]
        dy1 = dy_ref[:, :half]
        da_ref[...] = dy1 * _silu(g1)
        dg_ref[:, :half] = (dy1 * a_ref[...] * _dsilu(g1)).astype(BF16)
        row = i * t + lax.broadcasted_iota(jnp.int32, (t + hb, 1), 0)
        for gi, win in enumerate(POOL_WINDOWS):
            cs = slice(gi * gd, (gi + 1) * gd)
            cs2 = slice(half + gi * gd, half + (gi + 1) * gd)
            w = pw_ref[gi]
            pb = pooled[gi].astype(BF16)
            zp = jnp.dot(pb, w, preferred_element_type=F32)
            g2 = g_ref[:, cs2]
            dy2 = dy_ref[:, cs2]
            dg_ref[:, cs2] = (dy2 * zp * sc_ref[:, cs] * _dsilu(g2)).astype(BF16)
            dpo = dy2 * _silu(g2)
            _acc_rows(dsc_ref.at[:, cs], first, jnp.sum(dpo * zp, axis=0, keepdims=True))
            dz = (dpo * sc_ref[:, cs]).astype(BF16)
            _acc_rows(dpw_ref.at[gi], first, _tn(pb, dz))
            dzh = jnp.where(i < nt - 1, dyh_ref[:, cs] * _silu(gh_ref[:, cs]) * sc_ref[:, cs], 0.0).astype(BF16)
            dpool = _nt(dz, w)
            dpool_h = _nt(dzh, w)
            cnt = jnp.minimum(win, row + 1).astype(F32)
            dn_ref[0:t, cs] = dpool / cnt[0:t]
            dn_ref[t:, cs] = dpool_h / cnt[t:]
            acc = dn_ref[0:t, cs]
            for j in range(1, win):
                acc = acc + dn_ref[j:j + t, cs]
            du_ref[:, cs] = (acc - dpool).astype(BF16)
        host.after(i, nt)

    outs = pl.pallas_call(
        body, name=name, grid=(nt,), in_specs=host.in_specs, out_specs=host.out_specs, out_shape=host.out_shape,
        scratch_shapes=host.scratch, input_output_aliases=host.aliases,
        compiler_params=_cp("arbitrary"))(p, p, p, p, att, dy, dy, pool_w, pool_scale, *host.args)
    return host.results(outs)


def _mm_out_even(y, w, x, g_post, g_pre_next, name):
    s, k = y.shape
    d = w.shape[1]
    t = ROW_TILE

    def body(y_ref, w_ref, x_ref, gp_ref, gn_ref, o_ref, x1_ref, h1_ref):
        for r0 in range(0, t, t // 2):
            rows = slice(r0, r0 + t // 2)
            o = jnp.dot(y_ref[rows, :], w_ref[...], preferred_element_type=F32)
            o_ref[rows, :] = o
            ohat, _ = _rms_stats(o)
            x1 = x_ref[rows, :] + ohat * gp_ref[...]
            x1_ref[rows, :] = x1
            xhat, _ = _rms_stats(x1)
            h1_ref[rows, :] = (xhat * gn_ref[...]).astype(BF16)

    row = lambda c: pl.BlockSpec((t, c), lambda i: (i, 0))
    vec = pl.BlockSpec((1, d), lambda i: (0, 0))
    return pl.pallas_call(
        body, name=name, grid=(s // t,),
        in_specs=[row(k), pl.BlockSpec((k, d), lambda i: (0, 0)), row(d), vec, vec],
        out_specs=[row(d), row(d), row(d)],
        out_shape=[jax.ShapeDtypeStruct((s, d), F32), jax.ShapeDtypeStruct((s, d), F32),
                   jax.ShapeDtypeStruct((s, d), BF16)],
        compiler_params=_cp("parallel"))(y, w, x, g_post, g_pre_next)


def _mm_out_odd(y, w, x1, g_post, target, name):
    s, k = y.shape
    d = w.shape[1]
    t = ROW_TILE

    def body(y_ref, w_ref, x_ref, gp_ref, tg_ref, do_ref, dx_ref, loss_ref, dgp_ref):
        first = pl.program_id(0) == 0
        gp = gp_ref[...]
        part = dgp = None
        for r0 in range(0, t, t // 2):
            rows = slice(r0, r0 + t // 2)
            o = jnp.dot(y_ref[rows, :], w_ref[...], preferred_element_type=F32)
            ohat, r = _rms_stats(o)
            diff = x_ref[rows, :] + ohat * gp - tg_ref[rows, :]
            part_half = 0.5 * jnp.sum(jnp.mean(diff * diff, axis=-1, keepdims=True), axis=0, keepdims=True)
            dx2 = diff * (1.0 / d)
            dx_ref[rows, :] = dx2
            do, dgp_half = _rms_bwd(dx2, ohat, r, gp)
            do_ref[rows, :] = do.astype(BF16)
            part = part_half if part is None else part + part_half
            dgp = dgp_half if dgp is None else dgp + dgp_half
        _acc_rows(loss_ref, first, jnp.broadcast_to(part, loss_ref.shape))
        _acc_rows(dgp_ref, first, dgp)

    row = lambda c: pl.BlockSpec((t, c), lambda i: (i, 0))
    vec = pl.BlockSpec((1, d), lambda i: (0, 0))
    return pl.pallas_call(
        body, name=name, grid=(s // t,),
        in_specs=[row(k), pl.BlockSpec((k, d), lambda i: (0, 0)), row(d), vec, row(d)],
        out_specs=[row(d), row(d), pl.BlockSpec((8, LANES), lambda i: (0, 0)), vec],
        out_shape=[jax.ShapeDtypeStruct((s, d), BF16), jax.ShapeDtypeStruct((s, d), F32),
                   jax.ShapeDtypeStruct((8, LANES), F32), jax.ShapeDtypeStruct((1, d), F32)],
        compiler_params=_cp("arbitrary"))(y, w, x1, g_post, target)


def _layer_norm(d1, cg, cb):
    mu = jnp.mean(d1, axis=-1, keepdims=True)
    cen = d1 - mu
    rstd = lax.rsqrt(jnp.mean(cen * cen, axis=-1, keepdims=True) + EPS)
    n = cen * rstd
    return n, rstd, n * cg + cb


SUBLANES = 8
ROW_STRIP = 64
GATHER_PIECES = 8
CONV_ROWS = 64


def _make_shifts(pad_ref, cs, sh_ref):
    rows = sh_ref.shape[1]
    for r in range(1, SUBLANES):
        sh_ref[r - 1] = pad_ref[r:r + rows, cs]


def _by_shift(taps, base, sign=1):
    return sorted(range(taps), key=lambda k: ((sign * (base + k)) % SUBLANES, k))


def _window(pad_ref, cs, sh_ref, off, t):
    m, r = divmod(off, SUBLANES)
    if r == 0:
        return pad_ref[SUBLANES * m:SUBLANES * m + t, cs]
    return sh_ref[r - 1, SUBLANES * m:SUBLANES * m + t, :]


def _odd_mix_fwd(p, sconv_w, dconv_w, dconv_b, cnorm_g, cnorm_b, d, name):
    s = p.shape[0]
    w = d // 2
    k3, k31 = sconv_w.shape[0], dconv_w.shape[0]
    t, hb = ROW_TILE, CONV_HALO
    assert hb >= k31 - 1 and w % LANES == 0

    def body(p_ref, ph_ref, w3_ref, w31_ref, b31_ref, cg_ref, cb_ref, y_ref, s3_ref, d1_ref, mpad, dpad, sh_ref):
        i = pl.program_id(0)
        mpad[0:hb, :] = jnp.where(i > 0, ph_ref[:, 2 * w:3 * w] * ph_ref[:, 0:w], 0.0)
        mpad[hb:, :] = p_ref[:, 2 * w:3 * w] * p_ref[:, 0:w]
        dpad[0:hb, :] = jnp.where(i > 0, ph_ref[:, 3 * w:4 * w] * _sigmoid(ph_ref[:, 4 * w:5 * w]), 0.0)
        dpad[hb:, :] = p_ref[:, 3 * w:4 * w] * _sigmoid(p_ref[:, 4 * w:5 * w])
        for c0 in range(0, w, LANES):
            cs = slice(c0, c0 + LANES)
            acc = jnp.zeros((t, LANES), F32)
            for kk in range(k3):
                acc = acc + w3_ref[kk:kk + 1, cs] * mpad[hb - (k3 - 1) + kk:hb - (k3 - 1) + kk + t, cs]
            s3_ref[:, cs] = acc
            _make_shifts(dpad, cs, sh_ref)
            for r0 in range(0, t, CONV_ROWS):
                acc = jnp.zeros((CONV_ROWS, LANES), F32)
                for kk in _by_shift(k31, hb - (k31 - 1)):
                    acc = acc + w31_ref[kk:kk + 1, cs] * _window(dpad, cs, sh_ref, hb - (k31 - 1) + kk + r0, CONV_ROWS)
                d1_ref[r0:r0 + CONV_ROWS, cs] = acc + b31_ref[:, cs]
        _, _, d2 = _layer_norm(d1_ref[...], cg_ref[...], cb_ref[...])
        y_ref[:, :w] = (p_ref[:, w:2 * w] * s3_ref[...] * _silu(p_ref[:, 5 * w:6 * w])).astype(BF16)
        y_ref[:, w:] = (_silu(d2) * _silu(p_ref[:, 6 * w:7 * w])).astype(BF16)

    row = lambda c: pl.BlockSpec((t, c), lambda i: (i, 0))
    full = lambda a: pl.BlockSpec(a.shape, lambda i: (0, 0))
    return pl.pallas_call(
        body, name=name, grid=(s // t,),
        in_specs=[row(7 * w),
                  pl.BlockSpec((hb, 5 * w), lambda i: (jnp.maximum(i * (t // hb) - 1, 0), 0)),
                  full(sconv_w), full(dconv_w), full(dconv_b), full(cnorm_g), full(cnorm_b)],
        out_specs=[row(d), row(w), row(w)],
        out_shape=[jax.ShapeDtypeStruct((s, d), BF16), jax.ShapeDtypeStruct((s, w), F32),
                   jax.ShapeDtypeStruct((s, w), F32)],
        scratch_shapes=[pltpu.VMEM((hb + t, w), F32)] * 2 + [pltpu.VMEM((SUBLANES - 1, hb + t - SUBLANES, LANES), F32)],
        compiler_params=_cp("parallel"))(p, p, sconv_w, dconv_w, dconv_b, cnorm_g, cnorm_b)


def _odd_bwd_rows(p, s3, d1, dy, cnorm_g, cnorm_b, d, name, comm=None):
    s = p.shape[0]
    w = d // 2
    t = ROW_TILE
    col = lambda j: pl.BlockSpec((t, w), lambda i: (i, j))
    row = lambda c: pl.BlockSpec((t, c), lambda i: (i, 0))
    vec = pl.BlockSpec((1, w), lambda i: (0, 0))
    host = _Host(comm, [col(1), col(5), col(6), row(w), row(w), row(d), vec, vec],
                 [row(w), row(d), row(w), row(w), vec, vec, vec],
                 [jax.ShapeDtypeStruct((s, w), BF16), jax.ShapeDtypeStruct((s, d), BF16),
                  jax.ShapeDtypeStruct((s, w), F32), jax.ShapeDtypeStruct((s, w), F32)] + [jax.ShapeDtypeStruct((1, w), F32)] * 3, [])

    def body(*refs):
        ((bc_ref, g1_ref, g2_ref, s3_ref, d1_ref, dy_ref, cg_ref, cb_ref),
         (dbc_ref, dg_ref, ds3_ref, dd1_ref, dcg_ref, dcb_ref, db_ref), _) = host.split(refs)
        step = pl.program_id(0)
        host.before(step, s // t)
        first = step == 0

        def strip(j, sums):
            rows = slice(j * ROW_STRIP, (j + 1) * ROW_STRIP)
            g1, g2 = g1_ref[rows, :], g2_ref[rows, :]
            bc, s3v = bc_ref[rows, :], s3_ref[rows, :]
            dy1, dy2 = dy_ref[rows, :w], dy_ref[rows, w:]
            n, rstd, d2 = _layer_norm(d1_ref[rows, :], cg_ref[...], cb_ref[...])
            dg_ref[rows, :w] = (dy1 * bc * s3v * _dsilu(g1)).astype(BF16)
            dg_ref[rows, w:] = (dy2 * _silu(d2) * _dsilu(g2)).astype(BF16)
            dco = dy1 * _silu(g1)
            dbc_ref[rows, :] = (dco * s3v).astype(BF16)
            ds3_ref[rows, :] = dco * bc
            dd2 = dy2 * _silu(g2) * _dsilu(d2)
            dn = dd2 * cg_ref[...]
            dd1 = rstd * (dn - jnp.mean(dn, axis=-1, keepdims=True) - n * jnp.mean(dn * n, axis=-1, keepdims=True))
            dd1_ref[rows, :] = dd1
            dcb, dcg, db = sums
            return (dcb + jnp.sum(dd2, axis=0, keepdims=True), dcg + jnp.sum(dd2 * n, axis=0, keepdims=True),
                    db + jnp.sum(dd1, axis=0, keepdims=True))

        zero = jnp.zeros((1, w), F32)
        sums = (zero, zero, zero)
        for j in range(t // ROW_STRIP):
            sums = strip(j, sums)
        dcb, dcg, db = sums
        _acc_rows(dcb_ref, first, dcb)
        _acc_rows(dcg_ref, first, dcg)
        _acc_rows(db_ref, first, db)
        host.after(step, s // t)

    outs = pl.pallas_call(
        body, name=name, grid=(s // t,), in_specs=host.in_specs, out_specs=host.out_specs, out_shape=host.out_shape,
        scratch_shapes=host.scratch, input_output_aliases=host.aliases,
        compiler_params=_cp("arbitrary"))(p, p, p, s3, d1, dy, cnorm_g, cnorm_b, *host.args)
    return host.results(outs)


def _odd_bwd_conv(p, ds3, dd1, sconv_w, dconv_w, d, name):
    s = p.shape[0]
    w = d // 2
    k3, k31 = sconv_w.shape[0], dconv_w.shape[0]
    t, hb, ha = ROW_TILE, CONV_HALO, 8
    nt = s // t
    assert hb >= k31 - 1 and ha >= k3 - 1

    def body(hc_ref, cc_ref, ga_ref, gb_ref, hch_ref, cch_ref, gah_ref, gbh_ref, ds3_ref, ds3h_ref, dd1_ref, dd1h_ref,
             w3_ref, w31_ref, dhc_ref, dcc_ref, dga_ref, dgb_ref, dw3_ref, dw31_ref, mpad, dpad, s3pad, d1pad, sh_ref):
        i = pl.program_id(0)
        first = i == 0
        last = i == nt - 1
        mpad[0:hb, :] = jnp.where(i > 0, cch_ref[...] * hch_ref[...], 0.0)
        mpad[hb:, :] = cc_ref[...] * hc_ref[...]
        dpad[0:hb, :] = jnp.where(i > 0, gah_ref[...] * _sigmoid(gbh_ref[...]), 0.0)
        dpad[hb:, :] = ga_ref[...] * _sigmoid(gb_ref[...])
        s3pad[0:t, :] = ds3_ref[...]
        s3pad[t:, :] = jnp.where(last, 0.0, ds3h_ref[...])
        d1pad[0:t, :] = dd1_ref[...]
        d1pad[t:, :] = jnp.where(last, 0.0, dd1h_ref[...])

        @pl.when(first)
        def _():
            dw3_ref[...] = jnp.zeros_like(dw3_ref)
            dw31_ref[...] = jnp.zeros_like(dw31_ref)

        def fold(v):
            return jnp.sum(v.reshape(v.shape[0] // SUBLANES, SUBLANES, LANES), axis=0)

        groups = range(0, t, CONV_ROWS)
        for c0 in range(0, w, LANES):
            cs = slice(c0, c0 + LANES)
            ds3v = s3pad[0:t, cs]
            dm = jnp.zeros((t, LANES), F32)
            for kk in range(k3):
                dm = dm + w3_ref[kk:kk + 1, cs] * s3pad[k3 - 1 - kk:k3 - 1 - kk + t, cs]
                off = hb - (k3 - 1) + kk
                dw3_ref[SUBLANES * kk:SUBLANES * (kk + 1), cs] += fold(ds3v * mpad[off:off + t, cs])
            dcc_ref[:, cs] = (dm * hc_ref[:, cs]).astype(BF16)
            dhc_ref[:, cs] = (dm * cc_ref[:, cs]).astype(BF16)
            _make_shifts(d1pad, cs, sh_ref)
            for r0 in groups:
                rows = slice(r0, r0 + CONV_ROWS)
                dd0 = jnp.zeros((CONV_ROWS, LANES), F32)
                for kk in _by_shift(k31, -(k31 - 1), -1):
                    dd0 = dd0 + w31_ref[kk:kk + 1, cs] * _window(d1pad, cs, sh_ref, k31 - 1 - kk + r0, CONV_ROWS)
                sgb = _sigmoid(gb_ref[rows, cs])
                dga_ref[rows, cs] = (dd0 * sgb).astype(BF16)
                dgb_ref[rows, cs] = (dd0 * ga_ref[rows, cs] * sgb * (1.0 - sgb)).astype(BF16)
            _make_shifts(dpad, cs, sh_ref)
            for kk in _by_shift(k31, hb - (k31 - 1)):
                part = jnp.zeros((SUBLANES, LANES), F32)
                for r0 in groups:
                    part = part + fold(d1pad[r0:r0 + CONV_ROWS, cs]
                                       * _window(dpad, cs, sh_ref, hb - (k31 - 1) + kk + r0, CONV_ROWS))
                dw31_ref[SUBLANES * kk:SUBLANES * (kk + 1), cs] += part

    col = lambda j: pl.BlockSpec((t, w), lambda i: (i, j))
    pre = lambda j: pl.BlockSpec((hb, w), lambda i: (jnp.maximum(i * (t // hb) - 1, 0), j))
    row = pl.BlockSpec((t, w), lambda i: (i, 0))
    post = lambda h: pl.BlockSpec((h, w), lambda i: (jnp.minimum((i + 1) * (t // h), s // h - 1), 0))
    full = lambda a: pl.BlockSpec(a.shape, lambda i: (0, 0))
    dhc, dcc, dga, dgb, dw3, dw31 = pl.pallas_call(
        body, name=name, grid=(nt,),
        in_specs=[col(0), col(2), col(3), col(4), pre(0), pre(2), pre(3), pre(4),
                  row, post(ha), row, post(hb), full(sconv_w), full(dconv_w)],
        out_specs=[row, row, row, row, pl.BlockSpec((SUBLANES * k3, w), lambda i: (0, 0)),
                   pl.BlockSpec((SUBLANES * k31, w), lambda i: (0, 0))],
        out_shape=[jax.ShapeDtypeStruct((s, w), BF16)] * 4
        + [jax.ShapeDtypeStruct((SUBLANES * k3, w), F32), jax.ShapeDtypeStruct((SUBLANES * k31, w), F32)],
        scratch_shapes=[pltpu.VMEM((hb + t, w), F32)] * 2 + [pltpu.VMEM((t + ha, w), F32), pltpu.VMEM((t + hb, w), F32),
                                                             pltpu.VMEM((SUBLANES - 1, hb + t - SUBLANES, LANES), F32)],
        compiler_params=_cp("arbitrary"))(p, p, p, p, p, p, p, p, ds3, ds3, dd1, dd1, sconv_w, dconv_w)
    return dhc, dcc, dga, dgb, jnp.sum(dw3.reshape(k3, SUBLANES, w), axis=1), jnp.sum(dw31.reshape(k31, SUBLANES, w), axis=1)


def _mm_in_bwd(dp, w3, x, g_pre, dres, post, name, comm=None):
    s = dp.shape[0]
    nsh, d, ns = w3.shape
    t = 512 if s % 512 == 0 else ROW_TILE
    nt = s // t
    ks = 2 if (ns // 2) % LANES == 0 else 1
    nk, kw = nsh * ks, ns // ks
    chunk = 128
    nchunk = t // chunk
    row = pl.BlockSpec((t, d), lambda i, k: (i, 0))
    vec = pl.BlockSpec((1, d), lambda i, k: (0, 0))
    rowwise = [x, dres] + ([post[0]] if post is not None else [])
    in_specs = [pl.BlockSpec((t, kw), lambda i, k: (i, k)), pl.BlockSpec((None, d, kw), lambda i, k: (k // ks, 0, k % ks)), vec]
    out_specs = [row, vec]
    out_shape = [jax.ShapeDtypeStruct((s, d), F32), jax.ShapeDtypeStruct((1, d), F32)]
    args = [dp, w3, g_pre]
    if post is not None:
        in_specs += [vec]
        out_specs += [row, vec]
        out_shape += [jax.ShapeDtypeStruct((s, d), BF16), jax.ShapeDtypeStruct((1, d), F32)]
        args += [post[1]]
    n_blocked = len(in_specs)
    in_specs += [ANY] * len(rowwise)
    args += rowwise
    host = _Host(comm, in_specs, out_specs, out_shape,
                 [pltpu.VMEM((t, d), F32), pltpu.VMEM((len(rowwise), 2, chunk, d), F32), pltpu.SemaphoreType.DMA((len(rowwise), 2))])

    def body(*refs):
        ins, outs, (acc_ref, buf_ref, sem_ref) = host.split(refs)
        dp_ref, w_ref, g_ref = ins[:3]
        hbm = ins[n_blocked:]
        dx_ref, dg_ref = outs[:2]
        tile = pl.program_id(0)
        kk = pl.program_id(1)
        first = tile == 0
        step = tile * nk + kk
        host.before(step, nt * nk)
        part = _nt(dp_ref[...], w_ref[...])

        @pl.when(kk == 0)
        def _():
            acc_ref[...] = part

        @pl.when(kk > 0)
        def _():
            acc_ref[...] += part

        def fetch(ci, slot):
            return [pltpu.make_async_copy(src.at[pl.ds(tile * t + ci * chunk, chunk)], buf_ref.at[n, slot], sem_ref.at[n, slot])
                    for n, src in enumerate(hbm)]

        @pl.when(kk == nk - 1)
        def _():
            dg = dgp = None
            for cp in fetch(0, 0):
                cp.start()
            for ci in range(nchunk):
                slot = ci % 2
                if ci + 1 < nchunk:
                    for cp in fetch(ci + 1, 1 - slot):
                        cp.start()
                for cp in fetch(ci, slot):
                    cp.wait()
                rows = slice(ci * chunk, (ci + 1) * chunk)
                xhat, r = _rms_stats(buf_ref[0, slot])
                dxn, dg_part = _rms_bwd(acc_ref[rows, :], xhat, r, g_ref[...])
                dx = buf_ref[1, slot] + dxn
                dx_ref[rows, :] = dx
                dg = dg_part if dg is None else dg + dg_part
                if post is not None:
                    ohat, ro = _rms_stats(buf_ref[2, slot])
                    do, dgp_part = _rms_bwd(dx, ohat, ro, ins[3][...])
                    outs[2][rows, :] = do.astype(BF16)
                    dgp = dgp_part if dgp is None else dgp + dgp_part
            _acc_rows(dg_ref, first, dg)
            if post is not None:
                _acc_rows(outs[3], first, dgp)

        host.after(step, nt * nk)

    res = pl.pallas_call(
        body, name=name, grid=(nt, nk), in_specs=host.in_specs, out_specs=host.out_specs, out_shape=host.out_shape,
        scratch_shapes=host.scratch, input_output_aliases=host.aliases,
        compiler_params=_cp("arbitrary", "arbitrary"))(*args, *host.args)
    return host.results(res)


def _half_add(g, r1, c_arr, name, after=None):
    nsh, rows, ns = g.shape
    h = rows // 2
    tr = min(ROW_TILE, h)
    per = h // tr

    def body(c_ref, g_ref, r_ref, *rest):
        rest[-1][...] = (g_ref[...].astype(F32) + r_ref[...].astype(F32)).astype(BF16)

    spec = pl.BlockSpec((None, tr, ns), lambda s, r, c: (s, r, 0))
    ordering = [] if after is None else [after]
    return pl.pallas_call(
        body, name=name,
        grid_spec=pltpu.PrefetchScalarGridSpec(
            num_scalar_prefetch=1, grid=(nsh, per),
            in_specs=[pl.BlockSpec((None, tr, ns), lambda s, r, c: (s, c[0] * per + r, 0)), spec] + [ANY] * len(ordering),
            out_specs=spec),
        out_shape=jax.ShapeDtypeStruct((nsh, h, ns), BF16), compiler_params=_cp("parallel", "parallel"))(c_arr, g, r1, *ordering)


def _sum_chips(hh, r2, mc_arr, name, after=None):
    _, h, ns = hh.shape
    tr = min(ROW_TILE, h)
    per = h // tr

    def body(mc_ref, h_ref, a_ref, b_ref, c_ref, *rest):
        rest[-1][...] = ((h_ref[...].astype(F32) + a_ref[...].astype(F32)) + b_ref[...].astype(F32)) + c_ref[...].astype(F32)

    got = lambda k: pl.BlockSpec((None, tr, ns), lambda r, mc: (k, r, 0))
    ordering = [] if after is None else [after]
    return pl.pallas_call(
        body, name=name,
        grid_spec=pltpu.PrefetchScalarGridSpec(
            num_scalar_prefetch=1, grid=(per,),
            in_specs=[pl.BlockSpec((None, tr, ns), lambda r, mc: (mc[0], r, 0)), got(0), got(1), got(2)] + [ANY] * len(ordering),
            out_specs=pl.BlockSpec((tr, ns), lambda r, mc: (mc[1] * per + r, 0))),
        out_shape=jax.ShapeDtypeStruct((2 * h, ns), F32), compiler_params=_cp("parallel"))(mc_arr, hh, r2, r2, r2, *ordering)


def _add2(a, b, name):
    def body(a_ref, b_ref, o_ref):
        o_ref[...] = a_ref[...] + b_ref[...]

    return pl.pallas_call(body, name=name, out_shape=jax.ShapeDtypeStruct(a.shape, a.dtype), compiler_params=_cp())(a, b)


def _sum_chips_ordered(s2, r2, mc_arr, name):
    rows, w = s2.shape
    rh = rows // 2

    def body(mc_ref, s_ref, a_ref, b_ref, c_ref, o_ref):
        me = mc_ref[0]
        acc = None
        for j in range(N_CHIPS):
            rel = jnp.bitwise_xor(me, j)
            v = jnp.where(rel == 0, s_ref[...], jnp.where(rel == 2, a_ref[...], jnp.where(rel == 1, b_ref[...], c_ref[...])))
            acc = v if acc is None else acc + v
        o_ref[...] = acc

    got = lambda k: pl.BlockSpec((None, rh, w), lambda i, mc: (k, 0, 0))
    return pl.pallas_call(
        body, name=name,
        grid_spec=pltpu.PrefetchScalarGridSpec(
            num_scalar_prefetch=1, grid=(1,),
            in_specs=[pl.BlockSpec((rh, w), lambda i, mc: (mc[1], 0)), got(0), got(1), got(2)],
            out_specs=pl.BlockSpec((rh, w), lambda i, mc: (mc[1], 0))),
        out_shape=jax.ShapeDtypeStruct((rows, w), F32), compiler_params=_cp("arbitrary"))(mc_arr, s2, r2, r2, r2)


def _adamw(w, g, m, v, name, comm=None):
    r, c = w.shape
    tr = ROW_TILE if r % ROW_TILE == 0 else r
    c1 = 1.0 / (1.0 - ADAM_B1 ** ADAM_STEP)
    c2 = 1.0 / (1.0 - ADAM_B2 ** ADAM_STEP)
    spec = pl.BlockSpec((tr, c), lambda i: (i, 0))
    host = _Host(comm, [spec] * 4, [spec] * 4, [jax.ShapeDtypeStruct((r, c), F32)] * 4, [])

    def body(*refs):
        (w_ref, g_ref, m_ref, v_ref), (go_ref, d_ref, nm_ref, nv_ref), _ = host.split(refs)
        step = pl.program_id(0)
        host.before(step, r // tr)
        gv = g_ref[...]
        go_ref[...] = gv
        nm = ADAM_B1 * m_ref[...] + (1.0 - ADAM_B1) * gv
        nv = ADAM_B2 * v_ref[...] + (1.0 - ADAM_B2) * (gv * gv)
        nm_ref[...] = nm
        nv_ref[...] = nv
        d_ref[...] = -ADAM_LR * ((nm * c1) / (jnp.sqrt(nv * c2) + ADAM_EPS) + ADAM_WD * w_ref[...])
        host.after(step, r // tr)

    outs = pl.pallas_call(
        body, name=name, grid=(r // tr,), in_specs=host.in_specs, out_specs=host.out_specs, out_shape=host.out_shape,
        scratch_shapes=host.scratch, input_output_aliases=host.aliases,
        compiler_params=_cp("arbitrary"))(w, g, m, v, *host.args)
    return host.results(outs)


def _swap_with_sibling(grads, wholes, name):
    n, nw = len(grads), len(wholes)
    halves = [g.shape[1] // 2 for g in grads]

    def body(*refs):
        srcs, dsts = refs[:n + nw], refs[n + nw:2 * (n + nw)]
        ssem, rsem = refs[2 * (n + nw):]
        x, y, c, me, chips, sib = _place()
        cps = [_rcopy(srcs[a].at[:, pl.ds((1 - c) * halves[a], halves[a]), :], dsts[a], ssem.at[a], rsem.at[a], sib)
               for a in range(n)]
        cps += [_rcopy(srcs[a], dsts[a], ssem.at[a], rsem.at[a], sib) for a in range(n, n + nw)]
        for cp in cps:
            cp.start()
        for cp in cps:
            cp.wait_recv()
        for cp in cps:
            cp.wait_send()

    out_shape = [jax.ShapeDtypeStruct((g.shape[0], h, g.shape[2]), g.dtype) for g, h in zip(grads, halves)]
    out_shape += [jax.ShapeDtypeStruct(w.shape, w.dtype) for w in wholes]
    return pl.pallas_call(
        body, name=name, in_specs=[ANY] * (n + nw), out_specs=[ANY] * (n + nw), out_shape=out_shape,
        scratch_shapes=[pltpu.SemaphoreType.DMA((n + nw,)), pltpu.SemaphoreType.DMA((n + nw,))],
        compiler_params=pltpu.CompilerParams(has_side_effects=True))(*grads, *wholes)


def _scatter_start(h, name):
    land = (3,) + h.shape[1:]

    def body(h_ref, land_ref, send_sems, recv_sems, h_thru, land_thru, token):
        x, y, c, me, chips, sib = _place()
        for k, chip in enumerate(chips):
            _rcopy(h_ref.at[2 * chip[0] + chip[1]], land_ref.at[k], send_sems.at[k], recv_sems.at[k], (*chip, c)).start()
        token[...] = jnp.zeros_like(token)

    hbm = pl.BlockSpec(memory_space=pltpu.HBM)
    sem = pl.BlockSpec(memory_space=pltpu.SEMAPHORE)
    return pl.pallas_call(
        body, name=name,
        out_shape=(pltpu.SemaphoreType.DMA((3,)), pltpu.SemaphoreType.DMA((3,)), pltpu.HBM(h.shape, h.dtype),
                   pltpu.HBM(land, h.dtype), jax.ShapeDtypeStruct((8, LANES), F32)),
        in_specs=(hbm, hbm), out_specs=(sem, sem, hbm, hbm, pl.BlockSpec(memory_space=pltpu.VMEM)),
        input_output_aliases={0: 2, 1: 3},
        compiler_params=pltpu.CompilerParams(has_side_effects=pltpu.SideEffectType.DATAFLOW_SIDE_EFFECTING))(
            pltpu.with_memory_space_constraint(h, pltpu.HBM),
            pltpu.with_memory_space_constraint(lax.empty(land, h.dtype), pltpu.HBM))


def _scatter_wait(send_sems, recv_sems, h_thru, land_thru, after, name):
    def body(h_ref, land_ref, send_sems, recv_sems, after_ref, h_dead, got_ref):
        x, y, c, me, chips, sib = _place()
        for k, chip in enumerate(chips):
            cp = _rcopy(h_ref.at[2 * chip[0] + chip[1]], land_ref.at[k], send_sems.at[k], recv_sems.at[k], (*chip, c))
            cp.wait_send()
            cp.wait_recv()

    hbm = pl.BlockSpec(memory_space=pltpu.HBM)
    sem = pl.BlockSpec(memory_space=pltpu.SEMAPHORE)
    return pl.pallas_call(
        body, name=name,
        out_shape=(pltpu.HBM(h_thru.shape, h_thru.dtype), pltpu.HBM(land_thru.shape, land_thru.dtype)),
        in_specs=(hbm, hbm, sem, sem, ANY), out_specs=(hbm, hbm), input_output_aliases={0: 0, 1: 1},
        compiler_params=pltpu.CompilerParams(has_side_effects=pltpu.SideEffectType.DATAFLOW_SIDE_EFFECTING))(
            h_thru, land_thru, send_sems, recv_sems, after)


def _swap_start(g, name):
    h = g.shape[1] // 2
    land = (g.shape[0], h, g.shape[2])

    def body(g_ref, land_ref, send_sem, recv_sem, g_thru, land_thru, token):
        x, y, c, me, chips, sib = _place()
        _rcopy(g_ref.at[:, pl.ds((1 - c) * h, h), :], land_ref, send_sem.at[0], recv_sem.at[0], sib).start()
        token[...] = jnp.zeros_like(token)

    hbm = pl.BlockSpec(memory_space=pltpu.HBM)
    sem = pl.BlockSpec(memory_space=pltpu.SEMAPHORE)
    return pl.pallas_call(
        body, name=name,
        out_shape=(pltpu.SemaphoreType.DMA((1,)), pltpu.SemaphoreType.DMA((1,)), pltpu.HBM(g.shape, g.dtype),
                   pltpu.HBM(land, g.dtype), jax.ShapeDtypeStruct((8, LANES), F32)),
        in_specs=(hbm, hbm), out_specs=(sem, sem, hbm, hbm, pl.BlockSpec(memory_space=pltpu.VMEM)),
        input_output_aliases={0: 2, 1: 3},
        compiler_params=pltpu.CompilerParams(has_side_effects=pltpu.SideEffectType.DATAFLOW_SIDE_EFFECTING))(
            pltpu.with_memory_space_constraint(g, pltpu.HBM),
            pltpu.with_memory_space_constraint(lax.empty(land, g.dtype), pltpu.HBM))


def _swap_wait(send_sem, recv_sem, g_thru, land_thru, after, name):
    h = g_thru.shape[1] // 2

    def body(g_ref, land_ref, send_sem, recv_sem, after_ref, g_dead, got_ref):
        x, y, c, me, chips, sib = _place()
        cp = _rcopy(g_ref.at[:, pl.ds((1 - c) * h, h), :], land_ref, send_sem.at[0], recv_sem.at[0], sib)
        cp.wait_send()
        cp.wait_recv()

    hbm = pl.BlockSpec(memory_space=pltpu.HBM)
    sem = pl.BlockSpec(memory_space=pltpu.SEMAPHORE)
    return pl.pallas_call(
        body, name=name,
        out_shape=(pltpu.HBM(g_thru.shape, g_thru.dtype), pltpu.HBM(land_thru.shape, land_thru.dtype)),
        in_specs=(hbm, hbm, sem, sem, ANY), out_specs=(hbm, hbm), input_output_aliases={0: 0, 1: 1},
        compiler_params=pltpu.CompilerParams(has_side_effects=pltpu.SideEffectType.DATAFLOW_SIDE_EFFECTING))(
            g_thru, land_thru, send_sem, recv_sem, after)


def _share_half_start(small, name):
    rh = small.shape[0] // 2
    land = (3, rh, small.shape[1])

    def body(s_ref, land_ref, send_sems, recv_sems, s_thru, land_thru, token):
        x, y, c, me, chips, sib = _place()
        for k, chip in enumerate(chips):
            _rcopy(s_ref.at[pl.ds(c * rh, rh)], land_ref.at[k], send_sems.at[k], recv_sems.at[k], (*chip, c)).start()
        token[...] = jnp.zeros_like(token)

    hbm = pl.BlockSpec(memory_space=pltpu.HBM)
    sem = pl.BlockSpec(memory_space=pltpu.SEMAPHORE)
    return pl.pallas_call(
        body, name=name,
        out_shape=(pltpu.SemaphoreType.DMA((3,)), pltpu.SemaphoreType.DMA((3,)), pltpu.HBM(small.shape, small.dtype),
                   pltpu.HBM(land, small.dtype), jax.ShapeDtypeStruct((8, LANES), F32)),
        in_specs=(hbm, hbm), out_specs=(sem, sem, hbm, hbm, pl.BlockSpec(memory_space=pltpu.VMEM)),
        input_output_aliases={0: 2, 1: 3},
        compiler_params=pltpu.CompilerParams(has_side_effects=pltpu.SideEffectType.DATAFLOW_SIDE_EFFECTING))(
            pltpu.with_memory_space_constraint(small, pltpu.HBM),
            pltpu.with_memory_space_constraint(lax.empty(land, small.dtype), pltpu.HBM))


def _share_half_wait(send_sems, recv_sems, s_thru, land_thru, after, name):
    rh = s_thru.shape[0] // 2

    def body(s_ref, land_ref, send_sems, recv_sems, after_ref, s_dead, got_ref):
        x, y, c, me, chips, sib = _place()
        for k, chip in enumerate(chips):
            cp = _rcopy(s_ref.at[pl.ds(c * rh, rh)], land_ref.at[k], send_sems.at[k], recv_sems.at[k], (*chip, c))
            cp.wait_send()
            cp.wait_recv()

    hbm = pl.BlockSpec(memory_space=pltpu.HBM)
    sem = pl.BlockSpec(memory_space=pltpu.SEMAPHORE)
    return pl.pallas_call(
        body, name=name,
        out_shape=(pltpu.HBM(s_thru.shape, s_thru.dtype), pltpu.HBM(land_thru.shape, land_thru.dtype)),
        in_specs=(hbm, hbm, sem, sem, ANY), out_specs=(hbm, hbm), input_output_aliases={0: 0, 1: 1},
        compiler_params=pltpu.CompilerParams(has_side_effects=pltpu.SideEffectType.DATAFLOW_SIDE_EFFECTING))(
            s_thru, land_thru, send_sems, recv_sems, after)


def _join_start(parts, name):
    n = len(parts)

    def body(*refs):
        srcs, (send_sems, recv_sems), token = refs[:n], refs[n:n + 2], refs[-1]
        x, y, c, me, chips, sib = _place()
        for a, src in enumerate(srcs):
            h = src.shape[0] // 2
            mine = src.at[pl.ds(c * h, h)]
            _rcopy(mine, mine, send_sems.at[a], recv_sems.at[a], sib).start()
        token[...] = jnp.zeros_like(token)

    hbm = pl.BlockSpec(memory_space=pltpu.HBM)
    sem = pl.BlockSpec(memory_space=pltpu.SEMAPHORE)
    outs = pl.pallas_call(
        body, name=name,
        out_shape=(pltpu.SemaphoreType.DMA((n,)), pltpu.SemaphoreType.DMA((n,)))
        + tuple(pltpu.HBM(p.shape, p.dtype) for p in parts) + (jax.ShapeDtypeStruct((8, LANES), F32),),
        in_specs=(hbm,) * n, out_specs=(sem, sem) + (hbm,) * n + (pl.BlockSpec(memory_space=pltpu.VMEM),),
        input_output_aliases={a: 2 + a for a in range(n)},
        compiler_params=pltpu.CompilerParams(has_side_effects=pltpu.SideEffectType.DATAFLOW_SIDE_EFFECTING))(
            *[pltpu.with_memory_space_constraint(p, pltpu.HBM) for p in parts])
    return outs[0], outs[1], list(outs[2:2 + n]), outs[-1]


def _join_wait(send_sems, recv_sems, parts, after, name):
    n = len(parts)

    def body(*refs):
        srcs, (send_sems, recv_sems) = refs[:n], refs[n:n + 2]
        x, y, c, me, chips, sib = _place()
        for a, src in enumerate(srcs):
            h = src.shape[0] // 2
            mine, theirs = src.at[pl.ds(c * h, h)], src.at[pl.ds((1 - c) * h, h)]
            _rcopy(mine, theirs, send_sems.at[a], recv_sems.at[a], sib).wait_send()
            _rcopy(theirs, theirs, send_sems.at[a], recv_sems.at[a], sib).wait_recv()

    hbm = pl.BlockSpec(memory_space=pltpu.HBM)
    sem = pl.BlockSpec(memory_space=pltpu.SEMAPHORE)
    return pl.pallas_call(
        body, name=name, out_shape=tuple(pltpu.HBM(p.shape, p.dtype) for p in parts),
        in_specs=(hbm,) * n + (sem, sem, ANY), out_specs=(hbm,) * n, input_output_aliases={a: a for a in range(n)},
        compiler_params=pltpu.CompilerParams(has_side_effects=pltpu.SideEffectType.DATAFLOW_SIDE_EFFECTING))(
            *parts, send_sems, recv_sems, after)


def _pad_rows(a, rows):
    return jnp.pad(a, ((0, rows - a.shape[0]), (0, 0)))


def _stack_rows(parts, multiple):
    padded = [_pad_rows(p, -(-p.shape[0] // 8) * 8) for p in parts]
    starts, at = [], 0
    for p in padded:
        starts.append(at)
        at += p.shape[0]
    total = -(-at // multiple) * multiple
    if total > at:
        padded.append(jnp.zeros((total - at, parts[0].shape[1]), parts[0].dtype))
    return jnp.concatenate(padded, axis=0), starts


def kernel(x, ln_pre_even, w_in_even, pool_w, pool_scale, w_out_even, ln_post_even, ln_pre_odd, w_in_odd, sconv_w, dconv_w, dconv_b, cnorm_g, cnorm_b, w_out_odd, ln_post_odd, loss_target, m_ln_pre_even, m_w_in_even, m_pool_w, m_pool_scale, m_w_out_even, m_ln_post_even, m_ln_pre_odd, m_w_in_odd, m_sconv_w, m_dconv_w, m_dconv_b, m_cnorm_g, m_cnorm_b, m_w_out_odd, m_ln_post_odd, v_ln_pre_even, v_w_in_even, v_pool_w, v_pool_scale, v_w_out_even, v_ln_post_even, v_ln_pre_odd, v_w_in_odd, v_sconv_w, v_dconv_w, v_dconv_b, v_cnorm_g, v_cnorm_b, v_w_out_odd, v_ln_post_odd):
    _, s, d = x.shape
    half = d // 2
    cw = half // N_CHIPS
    ng, q, gd = pool_w.shape[1:]
    k3, k31 = sconv_w.shape[1], dconv_w.shape[1]
    x2d, tgt = x[0], loss_target[0]
    me = 2 * lax.axis_index("x") + lax.axis_index("y")
    core = lax.axis_index("c")
    c_arr = jnp.reshape(core, (1,)).astype(jnp.int32)
    me_arr = jnp.reshape(me, (1,)).astype(jnp.int32)
    mc_arr = jnp.stack([me, core]).astype(jnp.int32)

    shards = [w_in_even[0], w_out_even[0], w_in_odd[0], w_out_odd[0]]
    pool_w_b = _cast_bf16(pool_w[0].reshape(ng * q, gd), "cast_pool_w").reshape(ng, q, gd)
    pack_w, at_w = _stack_rows([sconv_w[0], dconv_w[0], dconv_b, cnorm_g, cnorm_b], 8)
    pack_d, at_d = _stack_rows([ln_pre_odd, ln_post_odd], 8)
    placed = [lax.dynamic_update_slice(jnp.zeros((ng, N_CHIPS * q, gd), BF16), pool_w_b, (0, me * q, 0)),
              lax.dynamic_update_slice(jnp.zeros((pack_w.shape[0], N_CHIPS * cw), F32), pack_w, (0, me * cw)),
              lax.dynamic_update_slice(jnp.zeros((pack_d.shape[0], d), F32), pack_d, (0, me * (d // N_CHIPS)))]
    plans = _Multi([_GatherPieces([_cast_bf16_own_slab(shards[0], me_arr, "cast_w0")], GATHER_PIECES, (0.3, 0.9)),
                    _SmallGatherPlan(placed, (q, cw, d // N_CHIPS))])
    h0, others, extra = _prep(x2d, ln_pre_even, shards[1:], me_arr, "prep_and_gather_first", plans)
    (win_e,), (pool_w_f, pack_w_f, pack_d_f) = plans.results(extra)
    slabs = [None] + others
    sconv_f = pack_w_f[at_w[0]:at_w[0] + k3]
    dconv_f = pack_w_f[at_w[1]:at_w[1] + k31]
    dconv_b_f, cnorm_g_f, cnorm_b_f = (pack_w_f[at_w[n]:at_w[n] + 1] for n in (2, 3, 4))
    ln_pre_odd_f = pack_d_f[at_d[0]:at_d[0] + 1]
    ln_post_odd_f = pack_d_f[at_d[1]:at_d[1] + 1]

    plans = _Multi([_GatherPlan([slabs[1]], at=(0.6, 0.88)), _GatherPlan([slabs[2]], (0, 1, 4), at=(0.6, 0.88))])
    p_e, extra = _mm_nn(h0, win_e, "proj_in_even", plans)
    (wout_e,), (win_o,) = plans.results(extra)
    wout_e = wout_e.reshape(d, d)
    att, ltot, (win_o,) = _sba_fwd(p_e, half, "sba_fwd", _GatherPlan([win_o], (1, 4, 4), at=(0.69, 0.94)))
    y_e = _even_mix_fwd(p_e, att, pool_w_f, pool_scale, d, "even_mix_fwd")
    o_e, x1, h1 = _mm_out_even(y_e, wout_e, x2d, ln_post_even, ln_pre_odd_f, "proj_out_even")
    p_o, (wout_o,) = _mm_nn(h1, win_o, "proj_in_odd", _GatherPlan([slabs[3]]))
    wout_o = wout_o.reshape(d, d)
    y_o, s3, d1 = _odd_mix_fwd(p_o, sconv_f, dconv_f, dconv_b_f, cnorm_g_f, cnorm_b_f, d, "odd_mix_fwd")
    do_o, dx2, loss_blk, dln_post_odd = _mm_out_odd(y_o, wout_o, x1, ln_post_odd_f, tgt, "proj_out_odd_loss")

    dy_o = _mm_nt(do_o, wout_o, "dy_odd")
    g_wout_o = _mm_tn(y_o, do_o, 1, "dw_out_odd")[0].reshape(N_CHIPS, d // N_CHIPS, d)
    (dbc, dgate_o, ds3, dd1, dcnorm_g, dcnorm_b, ddconv_b), (got,) = _odd_bwd_rows(
        p_o, s3, d1, dy_o, cnorm_g_f, cnorm_b_f, d, "odd_bwd_rows", _SwapPlan([g_wout_o]))
    h_wout_o = _half_add(g_wout_o, got, c_arr, "half_add_out_odd")
    dhc, dcc, dga, dgb, dsconv, ddconv = _odd_bwd_conv(p_o, ds3, dd1, sconv_f, dconv_f, d, "odd_bwd_conv")
    dp_o = jnp.concatenate([dhc, dbc, dcc, dga, dgb, dgate_o], axis=1)
    g_win_o, (s_wout_o,) = _mm_tn(h1, dp_o, N_CHIPS, "dw_in_odd", _ScatterPlan([h_wout_o]))
    (dx1, dln_pre_odd, do_e, dln_post_even), (got,) = _mm_in_bwd(
        dp_o, win_o, x1, ln_pre_odd_f, dx2, (o_e, ln_post_even), "dx_odd", _SwapPlan([g_win_o]))
    h_win_o = _half_add(g_win_o, got, c_arr, "half_add_in_odd")

    dy_e = _mm_nt(do_e, wout_e, "dy_even")
    g_wout_e = _mm_tn(y_e, do_e, 1, "dw_out_even")[0].reshape(N_CHIPS, d // N_CHIPS, d)
    (datt, du, dgate_e, dpool_scale, dpool_w), (got,) = _even_mix_bwd(
        p_e, att, dy_e, pool_w_f, pool_scale, d, "even_mix_bwd", _SwapPlan([g_wout_e]))
    h_wout_e = _half_add(g_wout_e, got, c_arr, "half_add_out_even")
    two = lambda v: v.reshape(2, half)
    small_parts = [dpool_scale, two(dln_post_even), two(dln_pre_odd), two(dln_post_odd),
                   dsconv, ddconv, ddconv_b, dcnorm_g, dcnorm_b, dpool_w.reshape(gd, half)]
    small, at_s = _stack_rows(small_parts, 16)
    plans = _Multi([_ScatterPlan([h_win_o]), _SendWholePlan([small])])
    dq, dk, dv, extra = _sba_bwd(p_e, ltot, datt, half, "sba_bwd", plans)
    (s_win_o,), (small1,) = plans.results(extra)
    small2 = _add2(small, small1, "small_add")
    dp_e = jnp.concatenate([dq, dk, dv, du, dgate_e], axis=1)
    plans = _Multi([_ScatterPlan([h_wout_e]), _ShareHalfPlan([small2])])
    g_win_e, extra = _mm_tn(h0, dp_e, N_CHIPS, "dw_in_even", plans)
    (s_wout_e,), (small_got,) = plans.results(extra)
    swap = _swap_start(g_win_e, "swap_in_even_start")
    pairs = [(h_wout_e, s_wout_e), (h_win_o, s_win_o), (h_wout_o, s_wout_o)]
    parts = []
    for n, (h, r) in enumerate(pairs):
        parts.append(_sum_chips(h, r, mc_arr, f"sum_chips{n + 1}", after=parts[-1] if parts else swap[4]))
    g_win_e, got = _swap_wait(*swap[:4], parts[-1], "swap_in_even_wait")
    parts.append(_sum_chips_ordered(small2, small_got, mc_arr, "small_sum"))
    join_sems = _join_start(parts, "join_first_start")
    h_win_e = _half_add(g_win_e, got, c_arr, "half_add_in_even", after=join_sems[3])
    send_sems, recv_sems, h_win_e, landing, token = _scatter_start(h_win_e, "scatter_in_even_start")
    (grad_x, dln_pre_even), _ = _mm_in_bwd(dp_e, win_e, x2d, ln_pre_even + token[0:1, 0:1], dx1, None, "dx_even")

    last, at_l = _stack_rows([two(dln_pre_even), jnp.pad(loss_blk[0:1], ((0, 0), (0, half - LANES)))], 16)
    (last1,) = _swap_with_sibling([], [last], "swap_last")
    last2 = _add2(last, last1, "last_add")
    share = _share_half_start(last2, "share_last_start")
    gw_out_e, gw_in_o, gw_out_o, red = _join_wait(*join_sems[:3], share[4], "join_first_wait")

    def rows(n, cnt):
        return red[at_s[n]:at_s[n] + cnt]

    def mine(a, width):
        return lax.dynamic_slice_in_dim(a, me * width, width, axis=1)

    quarter = d // N_CHIPS
    g_small = {
        "pool_scale": rows(0, 1),
        "ln_post_even": rows(1, 2).reshape(1, d),
        "ln_pre_odd": mine(rows(2, 2).reshape(1, d), quarter),
        "ln_post_odd": mine(rows(3, 2).reshape(1, d), quarter),
        "sconv_w": mine(rows(4, k3), cw),
        "dconv_w": mine(rows(5, k31), cw),
        "dconv_b": mine(rows(6, 1), cw),
        "cnorm_g": mine(rows(7, 1), cw),
        "cnorm_b": mine(rows(8, 1), cw),
        "pool_w": lax.dynamic_slice_in_dim(rows(9, gd).reshape(ng, gd, gd), me * q, q, axis=1).reshape(ng * q, gd),
    }
    w2d = {
        "ln_pre_even": ln_pre_even, "w_in_even": w_in_even[0], "pool_w": pool_w[0].reshape(ng * q, gd),
        "pool_scale": pool_scale, "w_out_even": w_out_even[0], "ln_post_even": ln_post_even, "ln_pre_odd": ln_pre_odd,
        "w_in_odd": w_in_odd[0], "sconv_w": sconv_w[0], "dconv_w": dconv_w[0], "dconv_b": dconv_b, "cnorm_g": cnorm_g,
        "cnorm_b": cnorm_b, "w_out_odd": w_out_odd[0], "ln_post_odd": ln_post_odd,
    }
    moments = {
        "ln_pre_even": (m_ln_pre_even, v_ln_pre_even), "w_in_even": (m_w_in_even, v_w_in_even),
        "pool_w": (m_pool_w, v_pool_w), "pool_scale": (m_pool_scale, v_pool_scale),
        "w_out_even": (m_w_out_even, v_w_out_even), "ln_post_even": (m_ln_post_even, v_ln_post_even),
        "ln_pre_odd": (m_ln_pre_odd, v_ln_pre_odd), "w_in_odd": (m_w_in_odd, v_w_in_odd),
        "sconv_w": (m_sconv_w, v_sconv_w), "dconv_w": (m_dconv_w, v_dconv_w), "dconv_b": (m_dconv_b, v_dconv_b),
        "cnorm_g": (m_cnorm_g, v_cnorm_g), "cnorm_b": (m_cnorm_b, v_cnorm_b),
        "w_out_odd": (m_w_out_odd, v_w_out_odd), "ln_post_odd": (m_ln_post_odd, v_ln_post_odd),
    }
    def update(name, g):
        m_in, v_in = moments[name]
        w = w2d[name]
        return _adamw(w, g, m_in.reshape(w.shape), v_in.reshape(w.shape), "adamw_" + name)[0]

    updates = {name: update(name, g) for name, g in (("w_in_odd", gw_in_o), ("w_out_even", gw_out_e), ("w_out_odd", gw_out_o))}
    last2, last_got = _share_half_wait(*share[:4], updates["w_out_odd"][1], "share_last_wait")
    last_sum = _sum_chips_ordered(last2, last_got, mc_arr, "last_sum")
    h_win_e, s_win_e = _scatter_wait(send_sems, recv_sems, h_win_e, landing, last_sum, "scatter_in_even_wait")
    last_sems = _join_start([_sum_chips(h_win_e, s_win_e, mc_arr, "sum_chips0"), last_sum], "join_last_start")
    for name, g in g_small.items():
        updates[name] = update(name, g)
    gw_in_e, red_last = _join_wait(*last_sems[:3], updates["pool_w"][1], "join_last_wait")
    loss = red_last[at_l[1], 0]
    updates["ln_pre_even"] = update("ln_pre_even", red_last[at_l[0]:at_l[0] + 2].reshape(1, d))
    updates["w_in_even"] = update("w_in_even", gw_in_e)
    outs = [[u.reshape(moments[name][0].shape) for u in updates[name]] for name in w2d]
    grads_out, deltas, new_m, new_v = zip(*outs)
    return (loss, grad_x.reshape(x.shape), *grads_out, *deltas, *new_m, *new_v)
```

```python
import functools
import math

import jax
import jax.numpy as jnp
from jax import lax
from jax.experimental import pallas as pl
from jax.experimental.pallas import tpu as pltpu

F32 = jnp.float32
BF16 = jnp.bfloat16
EPS = 1e-6
N_CHIPS = 4
VMEM_LIMIT_V7X = 56 << 20
HEAD_DIM = 128
ATT_BLOCK = 256
POOL_WINDOWS = (2, 4, 8, 16)
ROW_TILE = 256
POOL_HALO = 16
CONV_HALO = 32
LANES = 128
ADAM_LR, ADAM_B1, ADAM_B2, ADAM_EPS, ADAM_WD, ADAM_STEP = 0.001, 0.9, 0.999, 1e-08, 0.01, 10
MESH_ID = pl.DeviceIdType.MESH
ANY = pl.BlockSpec(memory_space=pl.ANY)


def _cp(*sem):
    return pltpu.CompilerParams(dimension_semantics=sem or None, vmem_limit_bytes=VMEM_LIMIT_V7X)


def _pick_tile(n, cap):
    best = None
    for t in range(LANES, min(n, cap) + 1, LANES):
        if n % t == 0:
            best = t
    assert best is not None, (n, cap)
    return best


def _sigmoid(x):
    return 1.0 / (1.0 + jnp.exp(-x))


def _silu(x):
    return x * _sigmoid(x)


def _dsilu(x):
    s = _sigmoid(x)
    return s * (1.0 + x * (1.0 - s))


def _log_sigmoid(z):
    return jnp.minimum(z, 0.0) - jnp.log(1.0 + jnp.exp(-jnp.abs(z)))


def _rms_stats(x):
    r = lax.rsqrt(jnp.mean(x * x, axis=-1, keepdims=True) + EPS)
    return x * r, r


def _rms_bwd(dh, xhat, r, g):
    dxh = dh * g
    dx = r * (dxh - xhat * jnp.mean(dxh * xhat, axis=-1, keepdims=True))
    return dx, jnp.sum(dh * xhat, axis=0, keepdims=True)


def _acc_rows(ref, first, val):
    @pl.when(first)
    def _():
        ref[...] = val

    @pl.when(jnp.logical_not(first))
    def _():
        ref[...] += val


def _rcopy(src, dst, ssem, rsem, dev):
    return pltpu.make_async_remote_copy(src_ref=src, dst_ref=dst, send_sem=ssem, recv_sem=rsem,
                                        device_id=dev, device_id_type=MESH_ID)


def _place():
    x, y, c = lax.axis_index("x"), lax.axis_index("y"), lax.axis_index("c")
    chips = [(1 - x, y), (x, 1 - y), (1 - x, 1 - y)]
    return x, y, c, 2 * x + y, chips, (x, y, 1 - c)


class _GatherPlan:
    PER_ARRAY = 7

    def __init__(self, arrays, part=(0, 1, 1), at=(0.5, 0.8)):
        self.operands = list(arrays)
        self.out_shapes = [jax.ShapeDtypeStruct(a.shape, a.dtype) for a in arrays]
        self.aliases = {i: i for i in range(len(arrays))}
        self.nsems = self.PER_ARRAY * len(arrays)
        self.base = 0
        self.halves = [a.shape[1] // 2 for a in arrays]
        self.part = part
        self.at = at

    def schedule(self):
        return [(0.0, self.start), (self.at[0], self.relay), (self.at[1], self.relay_far)]

    def _rows(self, ref, a, chip, half, quarter=None):
        lo, hi, n = self.part
        h = self.halves[a]
        first, size = half * h + lo * h // n, (hi - lo) * h // n
        if quarter is not None:
            first, size = first + quarter * (size // 2), size // 2
        return ref.at[chip, pl.ds(first, size)]

    def _copy(self, src, dst, a, n, ssem, rsem, dev):
        return _rcopy(src, dst, ssem.at[self.base + self.PER_ARRAY * a + n], rsem.at[self.base + self.PER_ARRAY * a + n], dev)

    def _own(self, ins, outs, ssem, rsem):
        x, y, c, me, chips, sib = _place()
        return [self._copy(self._rows(ins[a], a, me, c), self._rows(outs[a], a, me, c), a, k, ssem, rsem, (*chips[k], c))
                for a in range(len(ins)) for k in (0, 1)]

    def _relays(self, outs, ssem, rsem, a, k):
        x, y, c, me, chips, sib = _place()
        chip = 2 * chips[k][0] + chips[k][1]
        whole, quarter = self._rows(outs[a], a, chip, c), self._rows(outs[a], a, chip, c, k)
        return (self._copy(whole, whole, a, k, ssem, rsem, (*chips[k], c)),
                self._copy(quarter, quarter, a, 2 + k, ssem, rsem, (*chips[1 - k], c)),
                self._copy(whole, whole, a, 4 + k, ssem, rsem, sib))

    def _far(self, outs, ssem, rsem, a):
        x, y, c, me, chips, sib = _place()
        chip = 2 * chips[2][0] + chips[2][1]
        whole = self._rows(outs[a], a, chip, c)
        got = [self._copy(q, q, a, 2 + k, ssem, rsem, (*chips[1 - k], c))
               for k, q in enumerate([self._rows(outs[a], a, chip, c, 0), self._rows(outs[a], a, chip, c, 1)])]
        return got, self._copy(whole, whole, a, 6, ssem, rsem, sib)

    def start(self, ins, outs, ssem, rsem):
        for cp in self._own(ins, outs, ssem, rsem):
            cp.start()

    def relay(self, ins, outs, ssem, rsem):
        for a in range(len(outs)):
            for k in (0, 1):
                landed, onward, to_sibling = self._relays(outs, ssem, rsem, a, k)
                landed.wait_recv()
                onward.start()
                to_sibling.start()

    def relay_far(self, ins, outs, ssem, rsem):
        for a in range(len(outs)):
            got, to_sibling = self._far(outs, ssem, rsem, a)
            for cp in got:
                cp.wait_recv()
            to_sibling.start()

    def finish(self, ins, outs, ssem, rsem):
        x, y, c, me, chips, sib = _place()
        for a in range(len(outs)):
            for k in range(3):
                ref = self._rows(outs[a], a, 2 * chips[k][0] + chips[k][1], 1 - c)
                self._copy(ref, ref, a, 4 + k, ssem, rsem, sib).wait_recv()
        for cp in self._own(ins, outs, ssem, rsem):
            cp.wait_send()
        for a in range(len(outs)):
            for k in (0, 1):
                _, onward, to_sibling = self._relays(outs, ssem, rsem, a, k)
                onward.wait_send()
                to_sibling.wait_send()
            self._far(outs, ssem, rsem, a)[1].wait_send()


class _ScatterPlan:
    def __init__(self, arrays, part=(0, 1, 1), into=None):
        self.n = len(arrays)
        self.operands = list(arrays) + list(into or [])
        self.out_shapes = [jax.ShapeDtypeStruct((3,) + a.shape[1:], a.dtype) for a in arrays]
        self.aliases = {self.n + i: i for i in range(self.n)} if into else {}
        self.nsems = 3 * self.n
        self.base = 0
        self.part = part

    def _copies(self, ins, outs, ssem, rsem):
        x, y, c, me, chips, sib = _place()
        lo, hi, n = self.part
        out = []
        for a in range(self.n):
            h = ins[a].shape[1]
            rows = pl.ds(lo * h // n, (hi - lo) * h // n)
            for k, chip in enumerate(chips):
                out.append(_rcopy(ins[a].at[2 * chip[0] + chip[1], rows], outs[a].at[k, rows],
                                  ssem.at[self.base + 3 * a + k], rsem.at[self.base + 3 * a + k], (*chip, c)))
        return out

    def schedule(self):
        return [(0.0, self.start)]

    def start(self, ins, outs, ssem, rsem):
        for cp in self._copies(ins, outs, ssem, rsem):
            cp.start()

    def finish(self, ins, outs, ssem, rsem):
        cps = self._copies(ins, outs, ssem, rsem)
        for cp in cps:
            cp.wait_recv()
        for cp in cps:
            cp.wait_send()


class _ShareHalfPlan(_ScatterPlan):
    def __init__(self, arrays):
        super().__init__(arrays)
        self.out_shapes = [jax.ShapeDtypeStruct((3, a.shape[0] // 2, a.shape[1]), a.dtype) for a in arrays]

    def _copies(self, ins, outs, ssem, rsem):
        x, y, c, me, chips, sib = _place()
        out = []
        for a in range(self.n):
            rh = ins[a].shape[0] // 2
            for k, chip in enumerate(chips):
                out.append(_rcopy(ins[a].at[pl.ds(c * rh, rh)], outs[a].at[k],
                                  ssem.at[self.base + 3 * a + k], rsem.at[self.base + 3 * a + k], (*chip, c)))
        return out


class _SwapPlan:
    def __init__(self, grads):
        self.operands = list(grads)
        self.out_shapes = [jax.ShapeDtypeStruct((g.shape[0], g.shape[1] // 2, g.shape[2]), g.dtype) for g in grads]
        self.aliases = {}
        self.nsems = len(grads)
        self.base = 0

    def _copies(self, ins, outs, ssem, rsem):
        x, y, c, me, chips, sib = _place()
        out = []
        for a, src in enumerate(ins):
            h = src.shape[1] // 2
            out.append(_rcopy(src.at[:, pl.ds((1 - c) * h, h), :], outs[a], ssem.at[self.base + a], rsem.at[self.base + a], sib))
        return out

    def schedule(self):
        return [(0.0, self.start)]

    def start(self, ins, outs, ssem, rsem):
        for cp in self._copies(ins, outs, ssem, rsem):
            cp.start()

    def finish(self, ins, outs, ssem, rsem):
        cps = self._copies(ins, outs, ssem, rsem)
        for cp in cps:
            cp.wait_recv()
        for cp in cps:
            cp.wait_send()


class _SendWholePlan(_SwapPlan):
    def __init__(self, arrays):
        self.operands = list(arrays)
        self.out_shapes = [jax.ShapeDtypeStruct(a.shape, a.dtype) for a in arrays]
        self.aliases = {}
        self.nsems = len(arrays)
        self.base = 0

    def _copies(self, ins, outs, ssem, rsem):
        x, y, c, me, chips, sib = _place()
        return [_rcopy(src, outs[a], ssem.at[self.base + a], rsem.at[self.base + a], sib) for a, src in enumerate(ins)]


class _GatherPieces:
    def __init__(self, arrays, n, at):
        self.pieces = [_GatherPlan(arrays, (j, j + 1, n), at) for j in range(n)]
        self.operands, self.out_shapes, self.aliases = self.pieces[0].operands, self.pieces[0].out_shapes, self.pieces[0].aliases
        self.nsems = sum(p.nsems for p in self.pieces)
        self.at = at
        self.base = 0

    @property
    def base(self):
        return self.pieces[0].base

    @base.setter
    def base(self, value):
        for j, p in enumerate(self.pieces):
            p.base = value + j * p.nsems

    def schedule(self):
        return [(0.0, self.start), (self.at[0], self.relay), (self.at[1], self.relay_far)]

    def _each(self, what, *a):
        for p in self.pieces:
            getattr(p, what)(*a)

    def start(self, *a):
        self._each("start", *a)

    def relay(self, *a):
        self._each("relay", *a)

    def relay_far(self, *a):
        self._each("relay_far", *a)

    def finish(self, *a):
        self._each("finish", *a)


class _SmallGatherPlan:
    def __init__(self, arrays, widths):
        self.operands = list(arrays)
        self.out_shapes = [jax.ShapeDtypeStruct(a.shape, a.dtype) for a in arrays]
        self.aliases = {i: i for i in range(3)}
        self.nsems = 9
        self.base = 0
        self.widths = widths

    def _part(self, ref, n, chip):
        w = self.widths[n]
        return ref.at[:, pl.ds(chip * w, w), :] if n == 0 else ref.at[:, pl.ds(chip * w, w)]

    def _copies(self, ins, outs, ssem, rsem, own):
        x, y, c, me, chips, sib = _place()
        out = []
        for n in range(3):
            for k, chip in enumerate(chips):
                which = me if own else 2 * chip[0] + chip[1]
                out.append(_rcopy(self._part(ins[n], n, which), self._part(outs[n], n, which),
                                  ssem.at[self.base + 3 * n + k], rsem.at[self.base + 3 * n + k], (*chip, c)))
        return out

    def schedule(self):
        return [(0.0, self.start)]

    def start(self, ins, outs, ssem, rsem):
        for cp in self._copies(ins, outs, ssem, rsem, True):
            cp.start()

    def finish(self, ins, outs, ssem, rsem):
        for cp in self._copies(ins, outs, ssem, rsem, False):
            cp.wait_recv()
        for cp in self._copies(ins, outs, ssem, rsem, True):
            cp.wait_send()


class _Multi:
    def __init__(self, plans):
        self.plans = plans
        self.operands, self.out_shapes, self.aliases, self.nsems = [], [], {}, 0
        self.spans = []
        for p in plans:
            ni, no = len(self.operands), len(self.out_shapes)
            self.spans.append((ni, ni + len(p.operands), no, no + len(p.out_shapes)))
            self.aliases.update({ni + i: no + j for i, j in p.aliases.items()})
            p.base = self.nsems
            self.nsems += p.nsems
            self.operands += p.operands
            self.out_shapes += p.out_shapes

    def schedule(self):
        def bound(fn, span):
            i0, i1, o0, o1 = span
            return lambda ins, outs, ssem, rsem: fn(ins[i0:i1], outs[o0:o1], ssem, rsem)

        stages = [(at, bound(fn, span)) for p, span in zip(self.plans, self.spans) for at, fn in p.schedule()]
        return sorted(stages, key=lambda s: s[0])

    def finish(self, ins, outs, ssem, rsem):
        for p, (i0, i1, o0, o1) in zip(self.plans, self.spans):
            p.finish(ins[i0:i1], outs[o0:o1], ssem, rsem)

    def results(self, extra):
        return [list(extra[o0:o1]) for (_, _, o0, o1) in self.spans]


class _Host:
    def __init__(self, comm, in_specs, out_specs, out_shape, scratch, prefetch=0):
        self.comm = comm
        self.n_in, self.n_out = len(in_specs), len(out_specs)
        self.in_specs, self.out_specs, self.out_shape, self.scratch = list(in_specs), list(out_specs), list(out_shape), list(scratch)
        self.aliases = {}
        self.args = []
        if comm is not None:
            self.in_specs += [ANY] * len(comm.operands)
            self.out_specs += [ANY] * len(comm.out_shapes)
            self.out_shape += comm.out_shapes
            self.scratch += [pltpu.SemaphoreType.DMA((comm.nsems,)), pltpu.SemaphoreType.DMA((comm.nsems,))]
            self.aliases = {prefetch + self.n_in + i: self.n_out + j for i, j in comm.aliases.items()}
            self.args = list(comm.operands)

    def split(self, refs):
        nc = len(self.args)
        nco = len(self.out_shape) - self.n_out
        ins, p = refs[:self.n_in], self.n_in + nc
        outs, rest = refs[p:p + self.n_out], refs[p + self.n_out + nco:]
        self._cargs = None
        if self.comm is not None:
            self._cargs = (refs[self.n_in:p], refs[p + self.n_out:p + self.n_out + nco], rest[-2], rest[-1])
            rest = rest[:-2]
        return ins, outs, rest

    def before(self, step, total):
        if self.comm is None:
            return

        for at, stage in self.comm.schedule():
            pl.when(step == min(total - 1, int(at * total)))(functools.partial(stage, *self._cargs))

    def after(self, step, total):
        if self.comm is None:
            return

        @pl.when(step == total - 1)
        def _():
            self.comm.finish(*self._cargs)

    def results(self, outs):
        return outs[:self.n_out], outs[self.n_out:]


def _cast_bf16(x, name):
    r, c = x.shape
    tr = ROW_TILE if r % ROW_TILE == 0 else r

    def body(x_ref, o_ref):
        o_ref[...] = x_ref[...].astype(BF16)

    return pl.pallas_call(
        body, name=name, grid=(r // tr,),
        in_specs=[pl.BlockSpec((tr, c), lambda i: (i, 0))],
        out_specs=pl.BlockSpec((tr, c), lambda i: (i, 0)),
        out_shape=jax.ShapeDtypeStruct((r, c), BF16), compiler_params=_cp("parallel"))(x)


def _cast_bf16_own_slab(x, me_arr, name):
    r, c = x.shape
    tr = ROW_TILE if r % ROW_TILE == 0 else r

    def body(me_ref, x_ref, o_ref):
        o_ref[...] = x_ref[...].astype(BF16)

    return pl.pallas_call(
        body, name=name,
        grid_spec=pltpu.PrefetchScalarGridSpec(
            num_scalar_prefetch=1, grid=(r // tr,),
            in_specs=[pl.BlockSpec((tr, c), lambda i, me: (i, 0))],
            out_specs=pl.BlockSpec((None, tr, c), lambda i, me: (me[0], i, 0))),
        out_shape=jax.ShapeDtypeStruct((N_CHIPS, r, c), BF16), compiler_params=_cp("parallel"))(me_arr, x)


def _prep(x, g, shards, me_arr, name, comm):
    s, d = x.shape
    steps = s // ROW_TILE
    tiles = [(w.shape[0] // steps, w.shape[1]) for w in shards]
    assert all(w.shape[0] % steps == 0 for w in shards)
    in_specs = [pl.BlockSpec((ROW_TILE, d), lambda i, me: (i, 0)), pl.BlockSpec((1, d), lambda i, me: (0, 0))]
    in_specs += [pl.BlockSpec(t, lambda i, me: (i, 0)) for t in tiles]
    out_specs = [pl.BlockSpec((ROW_TILE, d), lambda i, me: (i, 0))]
    out_specs += [pl.BlockSpec((None,) + t, lambda i, me: (me[0], i, 0)) for t in tiles]
    out_shape = [jax.ShapeDtypeStruct((s, d), BF16)] + [jax.ShapeDtypeStruct((N_CHIPS,) + w.shape, BF16) for w in shards]
    host = _Host(comm, in_specs, out_specs, out_shape, [], prefetch=1)

    def body(me_ref, *refs):
        (x_ref, g_ref, *w_refs), (h_ref, *slab_refs), _ = host.split(refs)
        step = pl.program_id(0)
        host.before(step, steps)
        xhat, _ = _rms_stats(x_ref[...])
        h_ref[...] = (xhat * g_ref[...]).astype(BF16)
        for w_ref, slab_ref in zip(w_refs, slab_refs):
            slab_ref[...] = w_ref[...].astype(BF16)
        host.after(step, steps)

    outs = pl.pallas_call(
        body, name=name,
        grid_spec=pltpu.PrefetchScalarGridSpec(num_scalar_prefetch=1, grid=(steps,), in_specs=host.in_specs,
                                               out_specs=host.out_specs, scratch_shapes=host.scratch),
        out_shape=host.out_shape, input_output_aliases=host.aliases,
        compiler_params=_cp("arbitrary"))(me_arr, x, g, *shards, *host.args)
    (h, *slabs), extra = host.results(outs)
    return h, slabs, extra


def _mm_nn(a, w3, name, comm=None):
    m, k = a.shape
    nsh, _, ns = w3.shape
    tm = 512 if m % 512 == 0 else ROW_TILE
    tn = _pick_tile(ns, 1024)
    per = ns // tn
    grid = (nsh * per, m // tm)
    host = _Host(comm,
                 [pl.BlockSpec((tm, k), lambda n, i: (i, 0)), pl.BlockSpec((None, k, tn), lambda n, i: (n // per, 0, n % per))],
                 [pl.BlockSpec((tm, tn), lambda n, i: (i, n))], [jax.ShapeDtypeStruct((m, nsh * ns), F32)], [])

    def body(*refs):
        (a_ref, w_ref), (o_ref,), _ = host.split(refs)
        step = pl.program_id(0) * grid[1] + pl.program_id(1)
        host.before(step, grid[0] * grid[1])
        o_ref[...] = jnp.dot(a_ref[...], w_ref[...], preferred_element_type=F32)
        host.after(step, grid[0] * grid[1])

    outs = pl.pallas_call(
        body, name=name, grid=grid, in_specs=host.in_specs, out_specs=host.out_specs, out_shape=host.out_shape,
        scratch_shapes=host.scratch, input_output_aliases=host.aliases,
        compiler_params=_cp("arbitrary", "arbitrary"))(a, w3, *host.args)
    (out,), extra = host.results(outs)
    return out, extra


def _mm_nt(a, b, name):
    m, k = a.shape
    n = b.shape[0]
    tm = 512 if m % 512 == 0 else ROW_TILE

    def body(a_ref, b_ref, o_ref):
        o_ref[...] = lax.dot_general(a_ref[...], b_ref[...], (((1,), (1,)), ((), ())), preferred_element_type=F32)

    return pl.pallas_call(
        body, name=name, grid=(m // tm,),
        in_specs=[pl.BlockSpec((tm, k), lambda i: (i, 0)), pl.BlockSpec((n, k), lambda i: (0, 0))],
        out_specs=pl.BlockSpec((tm, n), lambda i: (i, 0)),
        out_shape=jax.ShapeDtypeStruct((m, n), F32), compiler_params=_cp("parallel"))(a, b)


def _mm_tn(a, b, nsh, name, comm=None):
    s, m = a.shape
    n = b.shape[1]
    ns = n // nsh
    tm = 512 if m % 512 == 0 else ROW_TILE
    tn = _pick_tile(ns, 1024)
    per = ns // tn
    grid = (nsh * per, m // tm)
    host = _Host(comm, [pl.BlockSpec((s, tm), lambda j, i: (0, i)), pl.BlockSpec((s, tn), lambda j, i: (0, j))],
                 [pl.BlockSpec((None, tm, tn), lambda j, i: (j // per, i, j % per))],
                 [jax.ShapeDtypeStruct((nsh, m, ns), BF16)], [])

    def body(*refs):
        (a_ref, b_ref), (o_ref,), _ = host.split(refs)
        step = pl.program_id(0) * grid[1] + pl.program_id(1)
        host.before(step, grid[0] * grid[1])
        o_ref[...] = lax.dot_general(a_ref[...], b_ref[...], (((0,), (0,)), ((), ())),
                                     preferred_element_type=F32).astype(BF16)
        host.after(step, grid[0] * grid[1])

    outs = pl.pallas_call(
        body, name=name, grid=grid, in_specs=host.in_specs, out_specs=host.out_specs, out_shape=host.out_shape,
        scratch_shapes=host.scratch, input_output_aliases=host.aliases,
        compiler_params=_cp("arbitrary", "arbitrary"))(a, b, *host.args)
    (out,), extra = host.results(outs)
    return out, extra


def _tri(n, rel):
    row = lax.broadcasted_iota(jnp.int32, (2 * n, n), 0)
    col = lax.broadcasted_iota(jnp.int32, (2 * n, n), 1)
    return jnp.where(rel(jnp.where(row >= n, row - n, row), col), 1.0, 0.0).astype(BF16)


def _dot_split(x, tri2):
    hi = x.astype(BF16)
    lo = (x - hi.astype(F32)).astype(BF16)
    return jnp.dot(jnp.concatenate([hi, lo], axis=1), tri2, preferred_element_type=F32)


def _nt(a, b):
    return lax.dot_general(a, b, (((1,), (1,)), ((), ())), preferred_element_type=F32)


def _tn(a, b):
    return lax.dot_general(a, b, (((0,), (0,)), ((), ())), preferred_element_type=F32)


def _heads_per_step(nh):
    return max(h for h in (1, 2, 4) if nh % h == 0)


def _sba_fwd(p, sbw, name, comm=None):
    s = p.shape[0]
    nh = sbw // HEAD_DIM
    hp = _heads_per_step(nh)
    ngrp, hw = nh // hp, hp * HEAD_DIM
    blk = ATT_BLOCK
    nq = s // blk
    scale = 1.0 / math.sqrt(HEAD_DIM)
    host = _Host(comm,
                 [pl.BlockSpec((blk, hw), lambda g, i: (i, g)),
                  pl.BlockSpec((s, hw), lambda g, i: (0, ngrp + g)),
                  pl.BlockSpec((s, hw), lambda g, i: (0, 2 * ngrp + g))],
                 [pl.BlockSpec((blk, hw), lambda g, i: (i, g))] * 2,
                 [jax.ShapeDtypeStruct((s, sbw), F32)] * 2,
                 [pltpu.VMEM((s, hw), BF16)] * 2)

    def body(*refs):
        (q_ref, k_ref, v_ref), (o_ref, lt_ref), (kb_ref, vb_ref) = host.split(refs)
        i = pl.program_id(1)
        step = pl.program_id(0) * nq + i
        host.before(step, ngrp * nq)

        @pl.when(i == 0)
        def _():
            kb_ref[...] = k_ref[...].astype(BF16)
            vb_ref[...] = v_ref[...].astype(BF16)

        heads = [slice(h * HEAD_DIM, (h + 1) * HEAD_DIM) for h in range(hp)]
        qs = [q_ref[:, hd].astype(BF16) for hd in heads]
        later = _tri(blk, lambda r, c: r > c)
        causal = lax.broadcasted_iota(jnp.int32, (blk, blk), 1) < lax.broadcasted_iota(jnp.int32, (blk, blk), 0)

        def key_block(j, carry, diagonal):
            rows = pl.ds(pl.multiple_of(j * blk, blk), blk)
            hs = range(hp)
            z = [_nt(qs[h], kb_ref[rows, heads[h]]) * scale for h in hs]
            ls = [_log_sigmoid(z[h]) for h in hs]
            lm = [jnp.where(causal, ls[h] - z[h], 0.0) if diagonal else ls[h] - z[h] for h in hs]
            stay = [_dot_split(lm[h], later) for h in hs]
            w = [jnp.exp(ls[h] + stay[h] + carry[h][1]) for h in hs]
            if diagonal:
                w = [jnp.where(causal, w[h], 0.0) for h in hs]
            acc = [carry[h][0] + jnp.dot(w[h].astype(BF16), vb_ref[rows, heads[h]], preferred_element_type=F32) for h in hs]
            return tuple((acc[h], carry[h][1] + jnp.sum(lm[h], axis=1, keepdims=True)) for h in hs)

        init = tuple((jnp.zeros((blk, HEAD_DIM), F32), jnp.zeros((blk, 1), F32)) for _ in heads)
        carry = key_block(i, init, True)
        carry = lax.fori_loop(0, i, lambda n, c: key_block(i - 1 - n, c, False), carry)
        for h, hd in enumerate(heads):
            o_ref[:, hd] = carry[h][0]
            lt_ref[:, hd] = jnp.broadcast_to(carry[h][1], (blk, HEAD_DIM))
        host.after(step, ngrp * nq)

    outs = pl.pallas_call(
        body, name=name, grid=(ngrp, nq), in_specs=host.in_specs, out_specs=host.out_specs, out_shape=host.out_shape,
        scratch_shapes=host.scratch, input_output_aliases=host.aliases,
        compiler_params=_cp("arbitrary", "arbitrary"))(p, p, p, *host.args)
    (out, ltot), extra = host.results(outs)
    return out, ltot, extra


def _sba_bwd(p, ltot, dout, sbw, name, comm=None):
    s = p.shape[0]
    nh = sbw // HEAD_DIM
    hp = _heads_per_step(nh)
    ngrp, hw = nh // hp, hp * HEAD_DIM
    blk = ATT_BLOCK
    nq = s // blk
    scale = 1.0 / math.sqrt(HEAD_DIM)
    blk_spec = pl.BlockSpec((blk, hw), lambda g, i: (i, g))
    col_spec = pl.BlockSpec((s, hw), lambda g, i: (0, g))
    host = _Host(comm,
                 [blk_spec, pl.BlockSpec((s, hw), lambda g, i: (0, ngrp + g)),
                  pl.BlockSpec((s, hw), lambda g, i: (0, 2 * ngrp + g)), blk_spec, blk_spec],
                 [blk_spec, col_spec, col_spec], [jax.ShapeDtypeStruct((s, sbw), BF16)] * 3,
                 [pltpu.VMEM((s, hw), BF16)] * 2 + [pltpu.VMEM((s, hw), F32)] * 2)

    def body(*refs):
        (q_ref, k_ref, v_ref, lt_ref, do_ref), (dq_ref, dk_ref, dv_ref), (kb_ref, vb_ref, dka_ref, dva_ref) = host.split(refs)
        i = pl.program_id(1)
        step = pl.program_id(0) * nq + i
        host.before(step, ngrp * nq)

        @pl.when(i == 0)
        def _():
            kb_ref[...] = k_ref[...].astype(BF16)
            vb_ref[...] = v_ref[...].astype(BF16)
            dka_ref[...] = jnp.zeros_like(dka_ref)
            dva_ref[...] = jnp.zeros_like(dva_ref)

        heads = [slice(h * HEAD_DIM, (h + 1) * HEAD_DIM) for h in range(hp)]
        qs = [q_ref[:, hd].astype(BF16) for hd in heads]
        dos = [do_ref[:, hd].astype(BF16) for hd in heads]
        ltots = [lt_ref[:, h * HEAD_DIM:h * HEAD_DIM + 1] for h in range(hp)]
        upto = _tri(blk, lambda r, c: r <= c)
        before = _tri(blk, lambda r, c: r < c)
        causal = lax.broadcasted_iota(jnp.int32, (blk, blk), 1) < lax.broadcasted_iota(jnp.int32, (blk, blk), 0)

        def key_block(j, carry, diagonal):
            rows = pl.ds(pl.multiple_of(j * blk, blk), blk)
            hs = range(hp)
            kj = [kb_ref[rows, heads[h]] for h in hs]
            vj = [vb_ref[rows, heads[h]] for h in hs]
            z = [_nt(qs[h], kj[h]) * scale for h in hs]
            dw = [_nt(dos[h], vj[h]) for h in hs]
            ls = [_log_sigmoid(z[h]) for h in hs]
            lm = [jnp.where(causal, ls[h] - z[h], 0.0) if diagonal else ls[h] - z[h] for h in hs]
            stay = [ltots[h] - carry[h][1] - _dot_split(lm[h], upto) for h in hs]
            w = [jnp.exp(ls[h] + stay[h]) for h in hs]
            if diagonal:
                w = [jnp.where(causal, w[h], 0.0) for h in hs]
            da = [dw[h] * w[h] for h in hs]
            sig = [jnp.exp(ls[h]) for h in hs]
            chain = [sig[h] * (carry[h][2] + _dot_split(da[h], before)) for h in hs]
            if diagonal:
                chain = [jnp.where(causal, chain[h], 0.0) for h in hs]
            dzb = [((da[h] * (1.0 - sig[h]) - chain[h]) * scale).astype(BF16) for h in hs]
            dq = [carry[h][0] + jnp.dot(dzb[h], kj[h], preferred_element_type=F32) for h in hs]
            for h in hs:
                dka_ref[rows, heads[h]] += _tn(dzb[h], qs[h])
            for h in hs:
                dva_ref[rows, heads[h]] += _tn(w[h].astype(BF16), dos[h])
            return tuple((dq[h], carry[h][1] + jnp.sum(lm[h], axis=1, keepdims=True),
                          carry[h][2] + jnp.sum(da[h], axis=1, keepdims=True)) for h in hs)

        zero = jnp.zeros((blk, 1), F32)
        init = tuple((jnp.zeros((blk, HEAD_DIM), F32), zero, zero) for _ in heads)
        carry = lax.fori_loop(0, i, lambda j, c: key_block(j, c, False), init)
        carry = key_block(i, carry, True)
        for h, hd in enumerate(heads):
            dq_ref[:, hd] = carry[h][0].astype(BF16)

        @pl.when(i == nq - 1)
        def _():
            dk_ref[...] = dka_ref[...].astype(BF16)
            dv_ref[...] = dva_ref[...].astype(BF16)

        host.after(step, ngrp * nq)

    outs = pl.pallas_call(
        body, name=name, grid=(ngrp, nq), in_specs=host.in_specs, out_specs=host.out_specs, out_shape=host.out_shape,
        scratch_shapes=host.scratch, input_output_aliases=host.aliases,
        compiler_params=_cp("arbitrary", "arbitrary"))(p, p, p, ltot, dout, *host.args)
    (dq, dk, dv), extra = host.results(outs)
    return dq, dk, dv, extra


def _pool_groups(pad_ref, tile, row0, gd, halo):
    row = row0 + lax.broadcasted_iota(jnp.int32, (tile, 1), 0)
    out = []
    for gi, win in enumerate(POOL_WINDOWS):
        cs = slice(gi * gd, (gi + 1) * gd)
        tok = pad_ref[halo:halo + tile, cs]
        acc = tok
        for j in range(1, win):
            acc = acc + pad_ref[halo - j:halo - j + tile, cs]
        cnt = jnp.minimum(win, row + 1).astype(F32)
        out.append(acc / cnt - tok)
    return out


def _even_mix_fwd(p, att, pool_w, pool_scale, d, name):
    s = p.shape[0]
    half = d // 2
    gd = half // len(POOL_WINDOWS)
    t, hb = ROW_TILE, POOL_HALO

    def body(u_ref, uh_ref, g_ref, a_ref, pw_ref, sc_ref, y_ref, pad_ref):
        i = pl.program_id(0)
        pad_ref[0:hb, :] = jnp.where(i > 0, uh_ref[...], 0.0)
        pad_ref[hb:, :] = u_ref[...]
        pooled = _pool_groups(pad_ref, t, i * t, gd, hb)
        for gi in range(len(POOL_WINDOWS)):
            cs = slice(gi * gd, (gi + 1) * gd)
            po = jnp.dot(pooled[gi].astype(BF16), pw_ref[gi], preferred_element_type=F32) * sc_ref[:, cs]
            y_ref[:, half + gi * gd:half + (gi + 1) * gd] = (po * _silu(g_ref[:, half + gi * gd:half + (gi + 1) * gd])).astype(BF16)
        y_ref[:, :half] = (a_ref[...] * _silu(g_ref[:, :half])).astype(BF16)

    return pl.pallas_call(
        body, name=name, grid=(s // t,),
        in_specs=[pl.BlockSpec((t, half), lambda i: (i, 3)),
                  pl.BlockSpec((hb, half), lambda i: (jnp.maximum(i * (t // hb) - 1, 0), 3)),
                  pl.BlockSpec((t, d), lambda i: (i, 2)),
                  pl.BlockSpec((t, half), lambda i: (i, 0)),
                  pl.BlockSpec(pool_w.shape, lambda i: (0, 0, 0)),
                  pl.BlockSpec((1, half), lambda i: (0, 0))],
        out_specs=pl.BlockSpec((t, d), lambda i: (i, 0)),
        out_shape=jax.ShapeDtypeStruct((s, d), BF16),
        scratch_shapes=[pltpu.VMEM((hb + t, half), F32)],
        compiler_params=_cp("parallel"))(p, p, p, att, pool_w, pool_scale)


def _even_mix_bwd(p, att, dy, pool_w, pool_scale, d, name, comm=None):
    s = p.shape[0]
    half = d // 2
    ng = len(POOL_WINDOWS)
    gd = half // ng
    t, hb = ROW_TILE, POOL_HALO
    nt = s // t
    host = _Host(
        comm,
        [pl.BlockSpec((t, half), lambda i: (i, 3)),
         pl.BlockSpec((hb, half), lambda i: (jnp.maximum(i * (t // hb) - 1, 0), 3)),
         pl.BlockSpec((t, d), lambda i: (i, 2)),
         pl.BlockSpec((hb, half), lambda i: (jnp.minimum((i + 1) * (t // hb), s // hb - 1), 5)),
         pl.BlockSpec((t, half), lambda i: (i, 0)),
         pl.BlockSpec((t, d), lambda i: (i, 0)),
         pl.BlockSpec((hb, half), lambda i: (jnp.minimum((i + 1) * (t // hb), s // hb - 1), 1)),
         pl.BlockSpec(pool_w.shape, lambda i: (0, 0, 0)),
         pl.BlockSpec((1, half), lambda i: (0, 0))],
        [pl.BlockSpec((t, half), lambda i: (i, 0)),
         pl.BlockSpec((t, half), lambda i: (i, 0)),
         pl.BlockSpec((t, d), lambda i: (i, 0)),
         pl.BlockSpec((1, half), lambda i: (0, 0)),
         pl.BlockSpec((ng, gd, gd), lambda i: (0, 0, 0))],
        [jax.ShapeDtypeStruct((s, half), F32), jax.ShapeDtypeStruct((s, half), BF16),
         jax.ShapeDtypeStruct((s, d), BF16), jax.ShapeDtypeStruct((1, half), F32),
         jax.ShapeDtypeStruct((ng, gd, gd), F32)],
        [pltpu.VMEM((hb + t, half), F32), pltpu.VMEM((t + hb, half), F32)])

    def body(*refs):
        ((u_ref, uh_ref, g_ref, gh_ref, a_ref, dy_ref, dyh_ref, pw_ref, sc_ref),
         (da_ref, du_ref, dg_ref, dsc_ref, dpw_ref), (pad_ref, dn_ref)) = host.split(refs)
        i = pl.program_id(0)
        host.before(i, nt)
        first = i == 0
        pad_ref[0:hb, :] = jnp.where(i > 0, uh_ref[...], 0.0)
        pad_ref[hb:, :] = u_ref[...]
        pooled = _pool_groups(pad_ref, t, i * t, gd, hb)
        g1 = g_ref[:, :half]
        dy1 = dy_ref[:, :half]
        da_ref[...] = dy1 * _silu(g1)
        dg_ref[:, :half] = (dy1 * a_ref[...] * _dsilu(g1)).astype(BF16)
        row = i * t + lax.broadcasted_iota(jnp.int32, (t + hb, 1), 0)
        for gi, win in enumerate(POOL_WINDOWS):
            cs = slice(gi * gd, (gi + 1) * gd)
            cs2 = slice(half + gi * gd, half + (gi + 1) * gd)
            w = pw_ref[gi]
            pb = pooled[gi].astype(BF16)
            zp = jnp.dot(pb, w, preferred_element_type=F32)
            g2 = g_ref[:, cs2]
            dy2 = dy_ref[:, cs2]
            dg_ref[:, cs2] = (dy2 * zp * sc_ref[:, cs] * _dsilu(g2)).astype(BF16)
            dpo = dy2 * _silu(g2)
            _acc_rows(dsc_ref.at[:, cs], first, jnp.sum(dpo * zp, axis=0, keepdims=True))
            dz = (dpo * sc_ref[:, cs]).astype(BF16)
            _acc_rows(dpw_ref.at[gi], first, _tn(pb, dz))
            dzh = jnp.where(i < nt - 1, dyh_ref[:, cs] * _silu(gh_ref[:, cs]) * sc_ref[:, cs], 0.0).astype(BF16)
            dpool = _nt(dz, w)
            dpool_h = _nt(dzh, w)
            cnt = jnp.minimum(win, row + 1).astype(F32)
            dn_ref[0:t, cs] = dpool / cnt[0:t]
            dn_ref[t:, cs] = dpool_h / cnt[t:]
            acc = dn_ref[0:t, cs]
            for j in range(1, win):
                acc = acc + dn_ref[j:j + t, cs]
            du_ref[:, cs] = (acc - dpool).astype(BF16)
        host.after(i, nt)

    outs = pl.pallas_call(
        body, name=name, grid=(nt,), in_specs=host.in_specs, out_specs=host.out_specs, out_shape=host.out_shape,
        scratch_shapes=host.scratch, input_output_aliases=host.aliases,
        compiler_params=_cp("arbitrary"))(p, p, p, p, att, dy, dy, pool_w, pool_scale, *host.args)
    return host.results(outs)


def _mm_out_even(y, w, x, g_post, g_pre_next, name):
    s, k = y.shape
    d = w.shape[1]
    t = ROW_TILE

    def body(y_ref, w_ref, x_ref, gp_ref, gn_ref, o_ref, x1_ref, h1_ref):
        for r0 in range(0, t, t // 2):
            rows = slice(r0, r0 + t // 2)
            o = jnp.dot(y_ref[rows, :], w_ref[...], preferred_element_type=F32)
            o_ref[rows, :] = o
            ohat, _ = _rms_stats(o)
            x1 = x_ref[rows, :] + ohat * gp_ref[...]
            x1_ref[rows, :] = x1
            xhat, _ = _rms_stats(x1)
            h1_ref[rows, :] = (xhat * gn_ref[...]).astype(BF16)

    row = lambda c: pl.BlockSpec((t, c), lambda i: (i, 0))
    vec = pl.BlockSpec((1, d), lambda i: (0, 0))
    return pl.pallas_call(
        body, name=name, grid=(s // t,),
        in_specs=[row(k), pl.BlockSpec((k, d), lambda i: (0, 0)), row(d), vec, vec],
        out_specs=[row(d), row(d), row(d)],
        out_shape=[jax.ShapeDtypeStruct((s, d), F32), jax.ShapeDtypeStruct((s, d), F32),
                   jax.ShapeDtypeStruct((s, d), BF16)],
        compiler_params=_cp("parallel"))(y, w, x, g_post, g_pre_next)


def _mm_out_odd(y, w, x1, g_post, target, name):
    s, k = y.shape
    d = w.shape[1]
    t = ROW_TILE

    def body(y_ref, w_ref, x_ref, gp_ref, tg_ref, do_ref, dx_ref, loss_ref, dgp_ref):
        first = pl.program_id(0) == 0
        gp = gp_ref[...]
        part = dgp = None
        for r0 in range(0, t, t // 2):
            rows = slice(r0, r0 + t // 2)
            o = jnp.dot(y_ref[rows, :], w_ref[...], preferred_element_type=F32)
            ohat, r = _rms_stats(o)
            diff = x_ref[rows, :] + ohat * gp - tg_ref[rows, :]
            part_half = 0.5 * jnp.sum(jnp.mean(diff * diff, axis=-1, keepdims=True), axis=0, keepdims=True)
            dx2 = diff * (1.0 / d)
            dx_ref[rows, :] = dx2
            do, dgp_half = _rms_bwd(dx2, ohat, r, gp)
            do_ref[rows, :] = do.astype(BF16)
            part = part_half if part is None else part + part_half
            dgp = dgp_half if dgp is None else dgp + dgp_half
        _acc_rows(loss_ref, first, jnp.broadcast_to(part, loss_ref.shape))
        _acc_rows(dgp_ref, first, dgp)

    row = lambda c: pl.BlockSpec((t, c), lambda i: (i, 0))
    vec = pl.BlockSpec((1, d), lambda i: (0, 0))
    return pl.pallas_call(
        body, name=name, grid=(s // t,),
        in_specs=[row(k), pl.BlockSpec((k, d), lambda i: (0, 0)), row(d), vec, row(d)],
        out_specs=[row(d), row(d), pl.BlockSpec((8, LANES), lambda i: (0, 0)), vec],
        out_shape=[jax.ShapeDtypeStruct((s, d), BF16), jax.ShapeDtypeStruct((s, d), F32),
                   jax.ShapeDtypeStruct((8, LANES), F32), jax.ShapeDtypeStruct((1, d), F32)],
        compiler_params=_cp("arbitrary"))(y, w, x1, g_post, target)


def _layer_norm(d1, cg, cb):
    mu = jnp.mean(d1, axis=-1, keepdims=True)
    cen = d1 - mu
    rstd = lax.rsqrt(jnp.mean(cen * cen, axis=-1, keepdims=True) + EPS)
    n = cen * rstd
    return n, rstd, n * cg + cb


SUBLANES = 8
ROW_STRIP = 64
GATHER_PIECES = 8
CONV_ROWS = 64


def _make_shifts(pad_ref, cs, sh_ref):
    rows = sh_ref.shape[1]
    for r in range(1, SUBLANES):
        sh_ref[r - 1] = pad_ref[r:r + rows, cs]


def _by_shift(taps, base, sign=1):
    return sorted(range(taps), key=lambda k: ((sign * (base + k)) % SUBLANES, k))


def _window(pad_ref, cs, sh_ref, off, t):
    m, r = divmod(off, SUBLANES)
    if r == 0:
        return pad_ref[SUBLANES * m:SUBLANES * m + t, cs]
    return sh_ref[r - 1, SUBLANES * m:SUBLANES * m + t, :]


def _odd_mix_fwd(p, sconv_w, dconv_w, dconv_b, cnorm_g, cnorm_b, d, name):
    s = p.shape[0]
    w = d // 2
    k3, k31 = sconv_w.shape[0], dconv_w.shape[0]
    t, hb = ROW_TILE, CONV_HALO
    assert hb >= k31 - 1 and w % LANES == 0

    def body(p_ref, ph_ref, w3_ref, w31_ref, b31_ref, cg_ref, cb_ref, y_ref, s3_ref, d1_ref, mpad, dpad, sh_ref):
        i = pl.program_id(0)
        mpad[0:hb, :] = jnp.where(i > 0, ph_ref[:, 2 * w:3 * w] * ph_ref[:, 0:w], 0.0)
        mpad[hb:, :] = p_ref[:, 2 * w:3 * w] * p_ref[:, 0:w]
        dpad[0:hb, :] = jnp.where(i > 0, ph_ref[:, 3 * w:4 * w] * _sigmoid(ph_ref[:, 4 * w:5 * w]), 0.0)
        dpad[hb:, :] = p_ref[:, 3 * w:4 * w] * _sigmoid(p_ref[:, 4 * w:5 * w])
        for c0 in range(0, w, LANES):
            cs = slice(c0, c0 + LANES)
            acc = jnp.zeros((t, LANES), F32)
            for kk in range(k3):
                acc = acc + w3_ref[kk:kk + 1, cs] * mpad[hb - (k3 - 1) + kk:hb - (k3 - 1) + kk + t, cs]
            s3_ref[:, cs] = acc
            _make_shifts(dpad, cs, sh_ref)
            for r0 in range(0, t, CONV_ROWS):
                acc = jnp.zeros((CONV_ROWS, LANES), F32)
                for kk in _by_shift(k31, hb - (k31 - 1)):
                    acc = acc + w31_ref[kk:kk + 1, cs] * _window(dpad, cs, sh_ref, hb - (k31 - 1) + kk + r0, CONV_ROWS)
                d1_ref[r0:r0 + CONV_ROWS, cs] = acc + b31_ref[:, cs]
        _, _, d2 = _layer_norm(d1_ref[...], cg_ref[...], cb_ref[...])
        y_ref[:, :w] = (p_ref[:, w:2 * w] * s3_ref[...] * _silu(p_ref[:, 5 * w:6 * w])).astype(BF16)
        y_ref[:, w:] = (_silu(d2) * _silu(p_ref[:, 6 * w:7 * w])).astype(BF16)

    row = lambda c: pl.BlockSpec((t, c), lambda i: (i, 0))
    full = lambda a: pl.BlockSpec(a.shape, lambda i: (0, 0))
    return pl.pallas_call(
        body, name=name, grid=(s // t,),
        in_specs=[row(7 * w),
                  pl.BlockSpec((hb, 5 * w), lambda i: (jnp.maximum(i * (t // hb) - 1, 0), 0)),
                  full(sconv_w), full(dconv_w), full(dconv_b), full(cnorm_g), full(cnorm_b)],
        out_specs=[row(d), row(w), row(w)],
        out_shape=[jax.ShapeDtypeStruct((s, d), BF16), jax.ShapeDtypeStruct((s, w), F32),
                   jax.ShapeDtypeStruct((s, w), F32)],
        scratch_shapes=[pltpu.VMEM((hb + t, w), F32)] * 2 + [pltpu.VMEM((SUBLANES - 1, hb + t - SUBLANES, LANES), F32)],
        compiler_params=_cp("parallel"))(p, p, sconv_w, dconv_w, dconv_b, cnorm_g, cnorm_b)


def _odd_bwd_rows(p, s3, d1, dy, cnorm_g, cnorm_b, d, name, comm=None):
    s = p.shape[0]
    w = d // 2
    t = ROW_TILE
    col = lambda j: pl.BlockSpec((t, w), lambda i: (i, j))
    row = lambda c: pl.BlockSpec((t, c), lambda i: (i, 0))
    vec = pl.BlockSpec((1, w), lambda i: (0, 0))
    host = _Host(comm, [col(1), col(5), col(6), row(w), row(w), row(d), vec, vec],
                 [row(w), row(d), row(w), row(w), vec, vec, vec],
                 [jax.ShapeDtypeStruct((s, w), BF16), jax.ShapeDtypeStruct((s, d), BF16),
                  jax.ShapeDtypeStruct((s, w), F32), jax.ShapeDtypeStruct((s, w), F32)] + [jax.ShapeDtypeStruct((1, w), F32)] * 3, [])

    def body(*refs):
        ((bc_ref, g1_ref, g2_ref, s3_ref, d1_ref, dy_ref, cg_ref, cb_ref),
         (dbc_ref, dg_ref, ds3_ref, dd1_ref, dcg_ref, dcb_ref, db_ref), _) = host.split(refs)
        step = pl.program_id(0)
        host.before(step, s // t)
        first = step == 0

        def strip(j, sums):
            rows = slice(j * ROW_STRIP, (j + 1) * ROW_STRIP)
            g1, g2 = g1_ref[rows, :], g2_ref[rows, :]
            bc, s3v = bc_ref[rows, :], s3_ref[rows, :]
            dy1, dy2 = dy_ref[rows, :w], dy_ref[rows, w:]
            n, rstd, d2 = _layer_norm(d1_ref[rows, :], cg_ref[...], cb_ref[...])
            dg_ref[rows, :w] = (dy1 * bc * s3v * _dsilu(g1)).astype(BF16)
            dg_ref[rows, w:] = (dy2 * _silu(d2) * _dsilu(g2)).astype(BF16)
            dco = dy1 * _silu(g1)
            dbc_ref[rows, :] = (dco * s3v).astype(BF16)
            ds3_ref[rows, :] = dco * bc
            dd2 = dy2 * _silu(g2) * _dsilu(d2)
            dn = dd2 * cg_ref[...]
            dd1 = rstd * (dn - jnp.mean(dn, axis=-1, keepdims=True) - n * jnp.mean(dn * n, axis=-1, keepdims=True))
            dd1_ref[rows, :] = dd1
            dcb, dcg, db = sums
            return (dcb + jnp.sum(dd2, axis=0, keepdims=True), dcg + jnp.sum(dd2 * n, axis=0, keepdims=True),
                    db + jnp.sum(dd1, axis=0, keepdims=True))

        zero = jnp.zeros((1, w), F32)
        sums = (zero, zero, zero)
        for j in range(t // ROW_STRIP):
            sums = strip(j, sums)
        dcb, dcg, db = sums
        _acc_rows(dcb_ref, first, dcb)
        _acc_rows(dcg_ref, first, dcg)
        _acc_rows(db_ref, first, db)
        host.after(step, s // t)

    outs = pl.pallas_call(
        body, name=name, grid=(s // t,), in_specs=host.in_specs, out_specs=host.out_specs, out_shape=host.out_shape,
        scratch_shapes=host.scratch, input_output_aliases=host.aliases,
        compiler_params=_cp("arbitrary"))(p, p, p, s3, d1, dy, cnorm_g, cnorm_b, *host.args)
    return host.results(outs)


def _odd_bwd_conv(p, ds3, dd1, sconv_w, dconv_w, d, name):
    s = p.shape[0]
    w = d // 2
    k3, k31 = sconv_w.shape[0], dconv_w.shape[0]
    t, hb, ha = ROW_TILE, CONV_HALO, 8
    nt = s // t
    assert hb >= k31 - 1 and ha >= k3 - 1

    def body(hc_ref, cc_ref, ga_ref, gb_ref, hch_ref, cch_ref, gah_ref, gbh_ref, ds3_ref, ds3h_ref, dd1_ref, dd1h_ref,
             w3_ref, w31_ref, dhc_ref, dcc_ref, dga_ref, dgb_ref, dw3_ref, dw31_ref, mpad, dpad, s3pad, d1pad, sh_ref):
        i = pl.program_id(0)
        first = i == 0
        last = i == nt - 1
        mpad[0:hb, :] = jnp.where(i > 0, cch_ref[...] * hch_ref[...], 0.0)
        mpad[hb:, :] = cc_ref[...] * hc_ref[...]
        dpad[0:hb, :] = jnp.where(i > 0, gah_ref[...] * _sigmoid(gbh_ref[...]), 0.0)
        dpad[hb:, :] = ga_ref[...] * _sigmoid(gb_ref[...])
        s3pad[0:t, :] = ds3_ref[...]
        s3pad[t:, :] = jnp.where(last, 0.0, ds3h_ref[...])
        d1pad[0:t, :] = dd1_ref[...]
        d1pad[t:, :] = jnp.where(last, 0.0, dd1h_ref[...])

        @pl.when(first)
        def _():
            dw3_ref[...] = jnp.zeros_like(dw3_ref)
            dw31_ref[...] = jnp.zeros_like(dw31_ref)

        def fold(v):
            return jnp.sum(v.reshape(v.shape[0] // SUBLANES, SUBLANES, LANES), axis=0)

        groups = range(0, t, CONV_ROWS)
        for c0 in range(0, w, LANES):
            cs = slice(c0, c0 + LANES)
            ds3v = s3pad[0:t, cs]
            dm = jnp.zeros((t, LANES), F32)
            for kk in range(k3):
                dm = dm + w3_ref[kk:kk + 1, cs] * s3pad[k3 - 1 - kk:k3 - 1 - kk + t, cs]
                off = hb - (k3 - 1) + kk
                dw3_ref[SUBLANES * kk:SUBLANES * (kk + 1), cs] += fold(ds3v * mpad[off:off + t, cs])
            dcc_ref[:, cs] = (dm * hc_ref[:, cs]).astype(BF16)
            dhc_ref[:, cs] = (dm * cc_ref[:, cs]).astype(BF16)
            _make_shifts(d1pad, cs, sh_ref)
            for r0 in groups:
                rows = slice(r0, r0 + CONV_ROWS)
                dd0 = jnp.zeros((CONV_ROWS, LANES), F32)
                for kk in _by_shift(k31, -(k31 - 1), -1):
                    dd0 = dd0 + w31_ref[kk:kk + 1, cs] * _window(d1pad, cs, sh_ref, k31 - 1 - kk + r0, CONV_ROWS)
                sgb = _sigmoid(gb_ref[rows, cs])
                dga_ref[rows, cs] = (dd0 * sgb).astype(BF16)
                dgb_ref[rows, cs] = (dd0 * ga_ref[rows, cs] * sgb * (1.0 - sgb)).astype(BF16)
            _make_shifts(dpad, cs, sh_ref)
            for kk in _by_shift(k31, hb - (k31 - 1)):
                part = jnp.zeros((SUBLANES, LANES), F32)
                for r0 in groups:
                    part = part + fold(d1pad[r0:r0 + CONV_ROWS, cs]
                                       * _window(dpad, cs, sh_ref, hb - (k31 - 1) + kk + r0, CONV_ROWS))
                dw31_ref[SUBLANES * kk:SUBLANES * (kk + 1), cs] += part

    col = lambda j: pl.BlockSpec((t, w), lambda i: (i, j))
    pre = lambda j: pl.BlockSpec((hb, w), lambda i: (jnp.maximum(i * (t // hb) - 1, 0), j))
    row = pl.BlockSpec((t, w), lambda i: (i, 0))
    post = lambda h: pl.BlockSpec((h, w), lambda i: (jnp.minimum((i + 1) * (t // h), s // h - 1), 0))
    full = lambda a: pl.BlockSpec(a.shape, lambda i: (0, 0))
    dhc, dcc, dga, dgb, dw3, dw31 = pl.pallas_call(
        body, name=name, grid=(nt,),
        in_specs=[col(0), col(2), col(3), col(4), pre(0), pre(2), pre(3), pre(4),
                  row, post(ha), row, post(hb), full(sconv_w), full(dconv_w)],
        out_specs=[row, row, row, row, pl.BlockSpec((SUBLANES * k3, w), lambda i: (0, 0)),
                   pl.BlockSpec((SUBLANES * k31, w), lambda i: (0, 0))],
        out_shape=[jax.ShapeDtypeStruct((s, w), BF16)] * 4
        + [jax.ShapeDtypeStruct((SUBLANES * k3, w), F32), jax.ShapeDtypeStruct((SUBLANES * k31, w), F32)],
        scratch_shapes=[pltpu.VMEM((hb + t, w), F32)] * 2 + [pltpu.VMEM((t + ha, w), F32), pltpu.VMEM((t + hb, w), F32),
                                                             pltpu.VMEM((SUBLANES - 1, hb + t - SUBLANES, LANES), F32)],
        compiler_params=_cp("arbitrary"))(p, p, p, p, p, p, p, p, ds3, ds3, dd1, dd1, sconv_w, dconv_w)
    return dhc, dcc, dga, dgb, jnp.sum(dw3.reshape(k3, SUBLANES, w), axis=1), jnp.sum(dw31.reshape(k31, SUBLANES, w), axis=1)


def _mm_in_bwd(dp, w3, x, g_pre, dres, post, name, comm=None):
    s = dp.shape[0]
    nsh, d, ns = w3.shape
    t = 512 if s % 512 == 0 else ROW_TILE
    nt = s // t
    ks = 2 if (ns // 2) % LANES == 0 else 1
    nk, kw = nsh * ks, ns // ks
    chunk = 128
    nchunk = t // chunk
    row = pl.BlockSpec((t, d), lambda i, k: (i, 0))
    vec = pl.BlockSpec((1, d), lambda i, k: (0, 0))
    rowwise = [x, dres] + ([post[0]] if post is not None else [])
    in_specs = [pl.BlockSpec((t, kw), lambda i, k: (i, k)), pl.BlockSpec((None, d, kw), lambda i, k: (k // ks, 0, k % ks)), vec]
    out_specs = [row, vec]
    out_shape = [jax.ShapeDtypeStruct((s, d), F32), jax.ShapeDtypeStruct((1, d), F32)]
    args = [dp, w3, g_pre]
    if post is not None:
        in_specs += [vec]
        out_specs += [row, vec]
        out_shape += [jax.ShapeDtypeStruct((s, d), BF16), jax.ShapeDtypeStruct((1, d), F32)]
        args += [post[1]]
    n_blocked = len(in_specs)
    in_specs += [ANY] * len(rowwise)
    args += rowwise
    host = _Host(comm, in_specs, out_specs, out_shape,
                 [pltpu.VMEM((t, d), F32), pltpu.VMEM((len(rowwise), 2, chunk, d), F32), pltpu.SemaphoreType.DMA((len(rowwise), 2))])

    def body(*refs):
        ins, outs, (acc_ref, buf_ref, sem_ref) = host.split(refs)
        dp_ref, w_ref, g_ref = ins[:3]
        hbm = ins[n_blocked:]
        dx_ref, dg_ref = outs[:2]
        tile = pl.program_id(0)
        kk = pl.program_id(1)
        first = tile == 0
        step = tile * nk + kk
        host.before(step, nt * nk)
        part = _nt(dp_ref[...], w_ref[...])

        @pl.when(kk == 0)
        def _():
            acc_ref[...] = part

        @pl.when(kk > 0)
        def _():
            acc_ref[...] += part

        def fetch(ci, slot):
            return [pltpu.make_async_copy(src.at[pl.ds(tile * t + ci * chunk, chunk)], buf_ref.at[n, slot], sem_ref.at[n, slot])
                    for n, src in enumerate(hbm)]

        @pl.when(kk == nk - 1)
        def _():
            dg = dgp = None
            for cp in fetch(0, 0):
                cp.start()
            for ci in range(nchunk):
                slot = ci % 2
                if ci + 1 < nchunk:
                    for cp in fetch(ci + 1, 1 - slot):
                        cp.start()
                for cp in fetch(ci, slot):
                    cp.wait()
                rows = slice(ci * chunk, (ci + 1) * chunk)
                xhat, r = _rms_stats(buf_ref[0, slot])
                dxn, dg_part = _rms_bwd(acc_ref[rows, :], xhat, r, g_ref[...])
                dx = buf_ref[1, slot] + dxn
                dx_ref[rows, :] = dx
                dg = dg_part if dg is None else dg + dg_part
                if post is not None:
                    ohat, ro = _rms_stats(buf_ref[2, slot])
                    do, dgp_part = _rms_bwd(dx, ohat, ro, ins[3][...])
                    outs[2][rows, :] = do.astype(BF16)
                    dgp = dgp_part if dgp is None else dgp + dgp_part
            _acc_rows(dg_ref, first, dg)
            if post is not None:
                _acc_rows(outs[3], first, dgp)

        host.after(step, nt * nk)

    res = pl.pallas_call(
        body, name=name, grid=(nt, nk), in_specs=host.in_specs, out_specs=host.out_specs, out_shape=host.out_shape,
        scratch_shapes=host.scratch, input_output_aliases=host.aliases,
        compiler_params=_cp("arbitrary", "arbitrary"))(*args, *host.args)
    return host.results(res)


def _half_add(g, r1, c_arr, name, after=None):
    nsh, rows, ns = g.shape
    h = rows // 2
    tr = min(ROW_TILE, h)
    per = h // tr

    def body(c_ref, g_ref, r_ref, *rest):
        rest[-1][...] = (g_ref[...].astype(F32) + r_ref[...].astype(F32)).astype(BF16)

    spec = pl.BlockSpec((None, tr, ns), lambda s, r, c: (s, r, 0))
    ordering = [] if after is None else [after]
    return pl.pallas_call(
        body, name=name,
        grid_spec=pltpu.PrefetchScalarGridSpec(
            num_scalar_prefetch=1, grid=(nsh, per),
            in_specs=[pl.BlockSpec((None, tr, ns), lambda s, r, c: (s, c[0] * per + r, 0)), spec] + [ANY] * len(ordering),
            out_specs=spec),
        out_shape=jax.ShapeDtypeStruct((nsh, h, ns), BF16), compiler_params=_cp("parallel", "parallel"))(c_arr, g, r1, *ordering)


def _sum_chips(hh, r2, mc_arr, name, after=None):
    _, h, ns = hh.shape
    tr = min(ROW_TILE, h)
    per = h // tr

    def body(mc_ref, h_ref, a_ref, b_ref, c_ref, *rest):
        rest[-1][...] = ((h_ref[...].astype(F32) + a_ref[...].astype(F32)) + b_ref[...].astype(F32)) + c_ref[...].astype(F32)

    got = lambda k: pl.BlockSpec((None, tr, ns), lambda r, mc: (k, r, 0))
    ordering = [] if after is None else [after]
    return pl.pallas_call(
        body, name=name,
        grid_spec=pltpu.PrefetchScalarGridSpec(
            num_scalar_prefetch=1, grid=(per,),
            in_specs=[pl.BlockSpec((None, tr, ns), lambda r, mc: (mc[0], r, 0)), got(0), got(1), got(2)] + [ANY] * len(ordering),
            out_specs=pl.BlockSpec((tr, ns), lambda r, mc: (mc[1] * per + r, 0))),
        out_shape=jax.ShapeDtypeStruct((2 * h, ns), F32), compiler_params=_cp("parallel"))(mc_arr, hh, r2, r2, r2, *ordering)


def _add2(a, b, name):
    def body(a_ref, b_ref, o_ref):
        o_ref[...] = a_ref[...] + b_ref[...]

    return pl.pallas_call(body, name=name, out_shape=jax.ShapeDtypeStruct(a.shape, a.dtype), compiler_params=_cp())(a, b)


def _sum_chips_ordered(s2, r2, mc_arr, name):
    rows, w = s2.shape
    rh = rows // 2

    def body(mc_ref, s_ref, a_ref, b_ref, c_ref, o_ref):
        me = mc_ref[0]
        acc = None
        for j in range(N_CHIPS):
            rel = jnp.bitwise_xor(me, j)
            v = jnp.where(rel == 0, s_ref[...], jnp.where(rel == 2, a_ref[...], jnp.where(rel == 1, b_ref[...], c_ref[...])))
            acc = v if acc is None else acc + v
        o_ref[...] = acc

    got = lambda k: pl.BlockSpec((None, rh, w), lambda i, mc: (k, 0, 0))
    return pl.pallas_call(
        body, name=name,
        grid_spec=pltpu.PrefetchScalarGridSpec(
            num_scalar_prefetch=1, grid=(1,),
            in_specs=[pl.BlockSpec((rh, w), lambda i, mc: (mc[1], 0)), got(0), got(1), got(2)],
            out_specs=pl.BlockSpec((rh, w), lambda i, mc: (mc[1], 0))),
        out_shape=jax.ShapeDtypeStruct((rows, w), F32), compiler_params=_cp("arbitrary"))(mc_arr, s2, r2, r2, r2)


def _adamw(w, g, m, v, name, comm=None):
    r, c = w.shape
    tr = ROW_TILE if r % ROW_TILE == 0 else r
    c1 = 1.0 / (1.0 - ADAM_B1 ** ADAM_STEP)
    c2 = 1.0 / (1.0 - ADAM_B2 ** ADAM_STEP)
    spec = pl.BlockSpec((tr, c), lambda i: (i, 0))
    host = _Host(comm, [spec] * 4, [spec] * 4, [jax.ShapeDtypeStruct((r, c), F32)] * 4, [])

    def body(*refs):
        (w_ref, g_ref, m_ref, v_ref), (go_ref, d_ref, nm_ref, nv_ref), _ = host.split(refs)
        step = pl.program_id(0)
        host.before(step, r // tr)
        gv = g_ref[...]
        go_ref[...] = gv
        nm = ADAM_B1 * m_ref[...] + (1.0 - ADAM_B1) * gv
        nv = ADAM_B2 * v_ref[...] + (1.0 - ADAM_B2) * (gv * gv)
        nm_ref[...] = nm
        nv_ref[...] = nv
        d_ref[...] = -ADAM_LR * ((nm * c1) / (jnp.sqrt(nv * c2) + ADAM_EPS) + ADAM_WD * w_ref[...])
        host.after(step, r // tr)

    outs = pl.pallas_call(
        body, name=name, grid=(r // tr,), in_specs=host.in_specs, out_specs=host.out_specs, out_shape=host.out_shape,
        scratch_shapes=host.scratch, input_output_aliases=host.aliases,
        compiler_params=_cp("arbitrary"))(w, g, m, v, *host.args)
    return host.results(outs)


def _swap_with_sibling(grads, wholes, name):
    n, nw = len(grads), len(wholes)
    halves = [g.shape[1] // 2 for g in grads]

    def body(*refs):
        srcs, dsts = refs[:n + nw], refs[n + nw:2 * (n + nw)]
        ssem, rsem = refs[2 * (n + nw):]
        x, y, c, me, chips, sib = _place()
        cps = [_rcopy(srcs[a].at[:, pl.ds((1 - c) * halves[a], halves[a]), :], dsts[a], ssem.at[a], rsem.at[a], sib)
               for a in range(n)]
        cps += [_rcopy(srcs[a], dsts[a], ssem.at[a], rsem.at[a], sib) for a in range(n, n + nw)]
        for cp in cps:
            cp.start()
        for cp in cps:
            cp.wait_recv()
        for cp in cps:
            cp.wait_send()

    out_shape = [jax.ShapeDtypeStruct((g.shape[0], h, g.shape[2]), g.dtype) for g, h in zip(grads, halves)]
    out_shape += [jax.ShapeDtypeStruct(w.shape, w.dtype) for w in wholes]
    return pl.pallas_call(
        body, name=name, in_specs=[ANY] * (n + nw), out_specs=[ANY] * (n + nw), out_shape=out_shape,
        scratch_shapes=[pltpu.SemaphoreType.DMA((n + nw,)), pltpu.SemaphoreType.DMA((n + nw,))],
        compiler_params=pltpu.CompilerParams(has_side_effects=True))(*grads, *wholes)


def _scatter_start(h, name):
    land = (3,) + h.shape[1:]

    def body(h_ref, land_ref, send_sems, recv_sems, h_thru, land_thru, token):
        x, y, c, me, chips, sib = _place()
        for k, chip in enumerate(chips):
            _rcopy(h_ref.at[2 * chip[0] + chip[1]], land_ref.at[k], send_sems.at[k], recv_sems.at[k], (*chip, c)).start()
        token[...] = jnp.zeros_like(token)

    hbm = pl.BlockSpec(memory_space=pltpu.HBM)
    sem = pl.BlockSpec(memory_space=pltpu.SEMAPHORE)
    return pl.pallas_call(
        body, name=name,
        out_shape=(pltpu.SemaphoreType.DMA((3,)), pltpu.SemaphoreType.DMA((3,)), pltpu.HBM(h.shape, h.dtype),
                   pltpu.HBM(land, h.dtype), jax.ShapeDtypeStruct((8, LANES), F32)),
        in_specs=(hbm, hbm), out_specs=(sem, sem, hbm, hbm, pl.BlockSpec(memory_space=pltpu.VMEM)),
        input_output_aliases={0: 2, 1: 3},
        compiler_params=pltpu.CompilerParams(has_side_effects=pltpu.SideEffectType.DATAFLOW_SIDE_EFFECTING))(
            pltpu.with_memory_space_constraint(h, pltpu.HBM),
            pltpu.with_memory_space_constraint(lax.empty(land, h.dtype), pltpu.HBM))


def _scatter_wait(send_sems, recv_sems, h_thru, land_thru, after, name):
    def body(h_ref, land_ref, send_sems, recv_sems, after_ref, h_dead, got_ref):
        x, y, c, me, chips, sib = _place()
        for k, chip in enumerate(chips):
            cp = _rcopy(h_ref.at[2 * chip[0] + chip[1]], land_ref.at[k], send_sems.at[k], recv_sems.at[k], (*chip, c))
            cp.wait_send()
            cp.wait_recv()

    hbm = pl.BlockSpec(memory_space=pltpu.HBM)
    sem = pl.BlockSpec(memory_space=pltpu.SEMAPHORE)
    return pl.pallas_call(
        body, name=name,
        out_shape=(pltpu.HBM(h_thru.shape, h_thru.dtype), pltpu.HBM(land_thru.shape, land_thru.dtype)),
        in_specs=(hbm, hbm, sem, sem, ANY), out_specs=(hbm, hbm), input_output_aliases={0: 0, 1: 1},
        compiler_params=pltpu.CompilerParams(has_side_effects=pltpu.SideEffectType.DATAFLOW_SIDE_EFFECTING))(
            h_thru, land_thru, send_sems, recv_sems, after)


def _swap_start(g, name):
    h = g.shape[1] // 2
    land = (g.shape[0], h, g.shape[2])

    def body(g_ref, land_ref, send_sem, recv_sem, g_thru, land_thru, token):
        x, y, c, me, chips, sib = _place()
        _rcopy(g_ref.at[:, pl.ds((1 - c) * h, h), :], land_ref, send_sem.at[0], recv_sem.at[0], sib).start()
        token[...] = jnp.zeros_like(token)

    hbm = pl.BlockSpec(memory_space=pltpu.HBM)
    sem = pl.BlockSpec(memory_space=pltpu.SEMAPHORE)
    return pl.pallas_call(
        body, name=name,
        out_shape=(pltpu.SemaphoreType.DMA((1,)), pltpu.SemaphoreType.DMA((1,)), pltpu.HBM(g.shape, g.dtype),
                   pltpu.HBM(land, g.dtype), jax.ShapeDtypeStruct((8, LANES), F32)),
        in_specs=(hbm, hbm), out_specs=(sem, sem, hbm, hbm, pl.BlockSpec(memory_space=pltpu.VMEM)),
        input_output_aliases={0: 2, 1: 3},
        compiler_params=pltpu.CompilerParams(has_side_effects=pltpu.SideEffectType.DATAFLOW_SIDE_EFFECTING))(
            pltpu.with_memory_space_constraint(g, pltpu.HBM),
            pltpu.with_memory_space_constraint(lax.empty(land, g.dtype), pltpu.HBM))


def _swap_wait(send_sem, recv_sem, g_thru, land_thru, after, name):
    h = g_thru.shape[1] // 2

    def body(g_ref, land_ref, send_sem, recv_sem, after_ref, g_dead, got_ref):
        x, y, c, me, chips, sib = _place()
        cp = _rcopy(g_ref.at[:, pl.ds((1 - c) * h, h), :], land_ref, send_sem.at[0], recv_sem.at[0], sib)
        cp.wait_send()
        cp.wait_recv()

    hbm = pl.BlockSpec(memory_space=pltpu.HBM)
    sem = pl.BlockSpec(memory_space=pltpu.SEMAPHORE)
    return pl.pallas_call(
        body, name=name,
        out_shape=(pltpu.HBM(g_thru.shape, g_thru.dtype), pltpu.HBM(land_thru.shape, land_thru.dtype)),
        in_specs=(hbm, hbm, sem, sem, ANY), out_specs=(hbm, hbm), input_output_aliases={0: 0, 1: 1},
        compiler_params=pltpu.CompilerParams(has_side_effects=pltpu.SideEffectType.DATAFLOW_SIDE_EFFECTING))(
            g_thru, land_thru, send_sem, recv_sem, after)


def _share_half_start(small, name):
    rh = small.shape[0] // 2
    land = (3, rh, small.shape[1])

    def body(s_ref, land_ref, send_sems, recv_sems, s_thru, land_thru, token):
        x, y, c, me, chips, sib = _place()
        for k, chip in enumerate(chips):
            _rcopy(s_ref.at[pl.ds(c * rh, rh)], land_ref.at[k], send_sems.at[k], recv_sems.at[k], (*chip, c)).start()
        token[...] = jnp.zeros_like(token)

    hbm = pl.BlockSpec(memory_space=pltpu.HBM)
    sem = pl.BlockSpec(memory_space=pltpu.SEMAPHORE)
    return pl.pallas_call(
        body, name=name,
        out_shape=(pltpu.SemaphoreType.DMA((3,)), pltpu.SemaphoreType.DMA((3,)), pltpu.HBM(small.shape, small.dtype),
                   pltpu.HBM(land, small.dtype), jax.ShapeDtypeStruct((8, LANES), F32)),
        in_specs=(hbm, hbm), out_specs=(sem, sem, hbm, hbm, pl.BlockSpec(memory_space=pltpu.VMEM)),
        input_output_aliases={0: 2, 1: 3},
        compiler_params=pltpu.CompilerParams(has_side_effects=pltpu.SideEffectType.DATAFLOW_SIDE_EFFECTING))(
            pltpu.with_memory_space_constraint(small, pltpu.HBM),
            pltpu.with_memory_space_constraint(lax.empty(land, small.dtype), pltpu.HBM))


def _share_half_wait(send_sems, recv_sems, s_thru, land_thru, after, name):
    rh = s_thru.shape[0] // 2

    def body(s_ref, land_ref, send_sems, recv_sems, after_ref, s_dead, got_ref):
        x, y, c, me, chips, sib = _place()
        for k, chip in enumerate(chips):
            cp = _rcopy(s_ref.at[pl.ds(c * rh, rh)], land_ref.at[k], send_sems.at[k], recv_sems.at[k], (*chip, c))
            cp.wait_send()
            cp.wait_recv()

    hbm = pl.BlockSpec(memory_space=pltpu.HBM)
    sem = pl.BlockSpec(memory_space=pltpu.SEMAPHORE)
    return pl.pallas_call(
        body, name=name,
        out_shape=(pltpu.HBM(s_thru.shape, s_thru.dtype), pltpu.HBM(land_thru.shape, land_thru.dtype)),
        in_specs=(hbm, hbm, sem, sem, ANY), out_specs=(hbm, hbm), input_output_aliases={0: 0, 1: 1},
        compiler_params=pltpu.CompilerParams(has_side_effects=pltpu.SideEffectType.DATAFLOW_SIDE_EFFECTING))(
            s_thru, land_thru, send_sems, recv_sems, after)


def _join_start(parts, name):
    n = len(parts)

    def body(*refs):
        srcs, (send_sems, recv_sems), token = refs[:n], refs[n:n + 2], refs[-1]
        x, y, c, me, chips, sib = _place()
        for a, src in enumerate(srcs):
            h = src.shape[0] // 2
            mine = src.at[pl.ds(c * h, h)]
            _rcopy(mine, mine, send_sems.at[a], recv_sems.at[a], sib).start()
        token[...] = jnp.zeros_like(token)

    hbm = pl.BlockSpec(memory_space=pltpu.HBM)
    sem = pl.BlockSpec(memory_space=pltpu.SEMAPHORE)
    outs = pl.pallas_call(
        body, name=name,
        out_shape=(pltpu.SemaphoreType.DMA((n,)), pltpu.SemaphoreType.DMA((n,)))
        + tuple(pltpu.HBM(p.shape, p.dtype) for p in parts) + (jax.ShapeDtypeStruct((8, LANES), F32),),
        in_specs=(hbm,) * n, out_specs=(sem, sem) + (hbm,) * n + (pl.BlockSpec(memory_space=pltpu.VMEM),),
        input_output_aliases={a: 2 + a for a in range(n)},
        compiler_params=pltpu.CompilerParams(has_side_effects=pltpu.SideEffectType.DATAFLOW_SIDE_EFFECTING))(
            *[pltpu.with_memory_space_constraint(p, pltpu.HBM) for p in parts])
    return outs[0], outs[1], list(outs[2:2 + n]), outs[-1]


def _join_wait(send_sems, recv_sems, parts, after, name):
    n = len(parts)

    def body(*refs):
        srcs, (send_sems, recv_sems) = refs[:n], refs[n:n + 2]
        x, y, c, me, chips, sib = _place()
        for a, src in enumerate(srcs):
            h = src.shape[0] // 2
            mine, theirs = src.at[pl.ds(c * h, h)], src.at[pl.ds((1 - c) * h, h)]
            _rcopy(mine, theirs, send_sems.at[a], recv_sems.at[a], sib).wait_send()
            _rcopy(theirs, theirs, send_sems.at[a], recv_sems.at[a], sib).wait_recv()

    hbm = pl.BlockSpec(memory_space=pltpu.HBM)
    sem = pl.BlockSpec(memory_space=pltpu.SEMAPHORE)
    return pl.pallas_call(
        body, name=name, out_shape=tuple(pltpu.HBM(p.shape, p.dtype) for p in parts),
        in_specs=(hbm,) * n + (sem, sem, ANY), out_specs=(hbm,) * n, input_output_aliases={a: a for a in range(n)},
        compiler_params=pltpu.CompilerParams(has_side_effects=pltpu.SideEffectType.DATAFLOW_SIDE_EFFECTING))(
            *parts, send_sems, recv_sems, after)


def _pad_rows(a, rows):
    return jnp.pad(a, ((0, rows - a.shape[0]), (0, 0)))


def _stack_rows(parts, multiple):
    padded = [_pad_rows(p, -(-p.shape[0] // 8) * 8) for p in parts]
    starts, at = [], 0
    for p in padded:
        starts.append(at)
        at += p.shape[0]
    total = -(-at // multiple) * multiple
    if total > at:
        padded.append(jnp.zeros((total - at, parts[0].shape[1]), parts[0].dtype))
    return jnp.concatenate(padded, axis=0), starts


def kernel(x, ln_pre_even, w_in_even, pool_w, pool_scale, w_out_even, ln_post_even, ln_pre_odd, w_in_odd, sconv_w, dconv_w, dconv_b, cnorm_g, cnorm_b, w_out_odd, ln_post_odd, loss_target, m_ln_pre_even, m_w_in_even, m_pool_w, m_pool_scale, m_w_out_even, m_ln_post_even, m_ln_pre_odd, m_w_in_odd, m_sconv_w, m_dconv_w, m_dconv_b, m_cnorm_g, m_cnorm_b, m_w_out_odd, m_ln_post_odd, v_ln_pre_even, v_w_in_even, v_pool_w, v_pool_scale, v_w_out_even, v_ln_post_even, v_ln_pre_odd, v_w_in_odd, v_sconv_w, v_dconv_w, v_dconv_b, v_cnorm_g, v_cnorm_b, v_w_out_odd, v_ln_post_odd):
    _, s, d = x.shape
    half = d // 2
    cw = half // N_CHIPS
    ng, q, gd = pool_w.shape[1:]
    k3, k31 = sconv_w.shape[1], dconv_w.shape[1]
    x2d, tgt = x[0], loss_target[0]
    me = 2 * lax.axis_index("x") + lax.axis_index("y")
    core = lax.axis_index("c")
    c_arr = jnp.reshape(core, (1,)).astype(jnp.int32)
    me_arr = jnp.reshape(me, (1,)).astype(jnp.int32)
    mc_arr = jnp.stack([me, core]).astype(jnp.int32)

    shards = [w_in_even[0], w_out_even[0], w_in_odd[0], w_out_odd[0]]
    pool_w_b = _cast_bf16(pool_w[0].reshape(ng * q, gd), "cast_pool_w").reshape(ng, q, gd)
    pack_w, at_w = _stack_rows([sconv_w[0], dconv_w[0], dconv_b, cnorm_g, cnorm_b], 8)
    pack_d, at_d = _stack_rows([ln_pre_odd, ln_post_odd], 8)
    placed = [lax.dynamic_update_slice(jnp.zeros((ng, N_CHIPS * q, gd), BF16), pool_w_b, (0, me * q, 0)),
              lax.dynamic_update_slice(jnp.zeros((pack_w.shape[0], N_CHIPS * cw), F32), pack_w, (0, me * cw)),
              lax.dynamic_update_slice(jnp.zeros((pack_d.shape[0], d), F32), pack_d, (0, me * (d // N_CHIPS)))]
    plans = _Multi([_GatherPieces([_cast_bf16_own_slab(shards[0], me_arr, "cast_w0")], GATHER_PIECES, (0.3, 0.9)),
                    _SmallGatherPlan(placed, (q, cw, d // N_CHIPS))])
    h0, others, extra = _prep(x2d, ln_pre_even, shards[1:], me_arr, "prep_and_gather_first", plans)
    (win_e,), (pool_w_f, pack_w_f, pack_d_f) = plans.results(extra)
    slabs = [None] + others
    sconv_f = pack_w_f[at_w[0]:at_w[0] + k3]
    dconv_f = pack_w_f[at_w[1]:at_w[1] + k31]
    dconv_b_f, cnorm_g_f, cnorm_b_f = (pack_w_f[at_w[n]:at_w[n] + 1] for n in (2, 3, 4))
    ln_pre_odd_f = pack_d_f[at_d[0]:at_d[0] + 1]
    ln_post_odd_f = pack_d_f[at_d[1]:at_d[1] + 1]

    plans = _Multi([_GatherPlan([slabs[1]], at=(0.6, 0.97)), _GatherPlan([slabs[2]], (0, 1, 4), at=(0.6, 0.97))])
    p_e, extra = _mm_nn(h0, win_e, "proj_in_even", plans)
    (wout_e,), (win_o,) = plans.results(extra)
    wout_e = wout_e.reshape(d, d)
    att, ltot, (win_o,) = _sba_fwd(p_e, half, "sba_fwd", _GatherPlan([win_o], (1, 4, 4), at=(0.69, 0.97)))
    y_e = _even_mix_fwd(p_e, att, pool_w_f, pool_scale, d, "even_mix_fwd")
    o_e, x1, h1 = _mm_out_even(y_e, wout_e, x2d, ln_post_even, ln_pre_odd_f, "proj_out_even")
    p_o, (wout_o,) = _mm_nn(h1, win_o, "proj_in_odd", _GatherPlan([slabs[3]]))
    wout_o = wout_o.reshape(d, d)
    y_o, s3, d1 = _odd_mix_fwd(p_o, sconv_f, dconv_f, dconv_b_f, cnorm_g_f, cnorm_b_f, d, "odd_mix_fwd")
    do_o, dx2, loss_blk, dln_post_odd = _mm_out_odd(y_o, wout_o, x1, ln_post_odd_f, tgt, "proj_out_odd_loss")

    dy_o = _mm_nt(do_o, wout_o, "dy_odd")
    g_wout_o = _mm_tn(y_o, do_o, 1, "dw_out_odd")[0].reshape(N_CHIPS, d // N_CHIPS, d)
    (dbc, dgate_o, ds3, dd1, dcnorm_g, dcnorm_b, ddconv_b), (got,) = _odd_bwd_rows(
        p_o, s3, d1, dy_o, cnorm_g_f, cnorm_b_f, d, "odd_bwd_rows", _SwapPlan([g_wout_o]))
    h_wout_o = _half_add(g_wout_o, got, c_arr, "half_add_out_odd")
    dhc, dcc, dga, dgb, dsconv, ddconv = _odd_bwd_conv(p_o, ds3, dd1, sconv_f, dconv_f, d, "odd_bwd_conv")
    dp_o = jnp.concatenate([dhc, dbc, dcc, dga, dgb, dgate_o], axis=1)
    g_win_o, (s_wout_o,) = _mm_tn(h1, dp_o, N_CHIPS, "dw_in_odd", _ScatterPlan([h_wout_o]))
    (dx1, dln_pre_odd, do_e, dln_post_even), (got,) = _mm_in_bwd(
        dp_o, win_o, x1, ln_pre_odd_f, dx2, (o_e, ln_post_even), "dx_odd", _SwapPlan([g_win_o]))
    h_win_o = _half_add(g_win_o, got, c_arr, "half_add_in_odd")

    dy_e = _mm_nt(do_e, wout_e, "dy_even")
    g_wout_e = _mm_tn(y_e, do_e, 1, "dw_out_even")[0].reshape(N_CHIPS, d // N_CHIPS, d)
    (datt, du, dgate_e, dpool_scale, dpool_w), (got,) = _even_mix_bwd(
        p_e, att, dy_e, pool_w_f, pool_scale, d, "even_mix_bwd", _SwapPlan([g_wout_e]))
    h_wout_e = _half_add(g_wout_e, got, c_arr, "half_add_out_even")
    two = lambda v: v.reshape(2, half)
    small_parts = [dpool_scale, two(dln_post_even), two(dln_pre_odd), two(dln_post_odd),
                   dsconv, ddconv, ddconv_b, dcnorm_g, dcnorm_b, dpool_w.reshape(gd, half)]
    small, at_s = _stack_rows(small_parts, 16)
    plans = _Multi([_ScatterPlan([h_win_o]), _SendWholePlan([small])])
    dq, dk, dv, extra = _sba_bwd(p_e, ltot, datt, half, "sba_bwd", plans)
    (s_win_o,), (small1,) = plans.results(extra)
    small2 = _add2(small, small1, "small_add")
    dp_e = jnp.concatenate([dq, dk, dv, du, dgate_e], axis=1)
    plans = _Multi([_ScatterPlan([h_wout_e]), _ShareHalfPlan([small2])])
    g_win_e, extra = _mm_tn(h0, dp_e, N_CHIPS, "dw_in_even", plans)
    (s_wout_e,), (small_got,) = plans.results(extra)
    swap = _swap_start(g_win_e, "swap_in_even_start")
    pairs = [(h_wout_e, s_wout_e), (h_win_o, s_win_o), (h_wout_o, s_wout_o)]
    parts = []
    for n, (h, r) in enumerate(pairs):
        parts.append(_sum_chips(h, r, mc_arr, f"sum_chips{n + 1}", after=parts[-1] if parts else swap[4]))
    g_win_e, got = _swap_wait(*swap[:4], parts[-1], "swap_in_even_wait")
    parts.append(_sum_chips_ordered(small2, small_got, mc_arr, "small_sum"))
    join_sems = _join_start(parts, "join_first_start")
    h_win_e = _half_add(g_win_e, got, c_arr, "half_add_in_even", after=join_sems[3])
    send_sems, recv_sems, h_win_e, landing, token = _scatter_start(h_win_e, "scatter_in_even_start")
    (grad_x, dln_pre_even), _ = _mm_in_bwd(dp_e, win_e, x2d, ln_pre_even + token[0:1, 0:1], dx1, None, "dx_even")

    last, at_l = _stack_rows([two(dln_pre_even), jnp.pad(loss_blk[0:1], ((0, 0), (0, half - LANES)))], 16)
    (last1,) = _swap_with_sibling([], [last], "swap_last")
    last2 = _add2(last, last1, "last_add")
    share = _share_half_start(last2, "share_last_start")
    gw_out_e, gw_in_o, gw_out_o, red = _join_wait(*join_sems[:3], share[4], "join_first_wait")

    def rows(n, cnt):
        return red[at_s[n]:at_s[n] + cnt]

    def mine(a, width):
        return lax.dynamic_slice_in_dim(a, me * width, width, axis=1)

    quarter = d // N_CHIPS
    g_small = {
        "pool_scale": rows(0, 1),
        "ln_post_even": rows(1, 2).reshape(1, d),
        "ln_pre_odd": mine(rows(2, 2).reshape(1, d), quarter),
        "ln_post_odd": mine(rows(3, 2).reshape(1, d), quarter),
        "sconv_w": mine(rows(4, k3), cw),
        "dconv_w": mine(rows(5, k31), cw),
        "dconv_b": mine(rows(6, 1), cw),
        "cnorm_g": mine(rows(7, 1), cw),
        "cnorm_b": mine(rows(8, 1), cw),
        "pool_w": lax.dynamic_slice_in_dim(rows(9, gd).reshape(ng, gd, gd), me * q, q, axis=1).reshape(ng * q, gd),
    }
    w2d = {
        "ln_pre_even": ln_pre_even, "w_in_even": w_in_even[0], "pool_w": pool_w[0].reshape(ng * q, gd),
        "pool_scale": pool_scale, "w_out_even": w_out_even[0], "ln_post_even": ln_post_even, "ln_pre_odd": ln_pre_odd,
        "w_in_odd": w_in_odd[0], "sconv_w": sconv_w[0], "dconv_w": dconv_w[0], "dconv_b": dconv_b, "cnorm_g": cnorm_g,
        "cnorm_b": cnorm_b, "w_out_odd": w_out_odd[0], "ln_post_odd": ln_post_odd,
    }
    moments = {
        "ln_pre_even": (m_ln_pre_even, v_ln_pre_even), "w_in_even": (m_w_in_even, v_w_in_even),
        "pool_w": (m_pool_w, v_pool_w), "pool_scale": (m_pool_scale, v_pool_scale),
        "w_out_even": (m_w_out_even, v_w_out_even), "ln_post_even": (m_ln_post_even, v_ln_post_even),
        "ln_pre_odd": (m_ln_pre_odd, v_ln_pre_odd), "w_in_odd": (m_w_in_odd, v_w_in_odd),
        "sconv_w": (m_sconv_w, v_sconv_w), "dconv_w": (m_dconv_w, v_dconv_w), "dconv_b": (m_dconv_b, v_dconv_b),
        "cnorm_g": (m_cnorm_g, v_cnorm_g), "cnorm_b": (m_cnorm_b, v_cnorm_b),
        "w_out_odd": (m_w_out_odd, v_w_out_odd), "ln_post_odd": (m_ln_post_odd, v_ln_post_odd),
    }
    def update(name, g):
        m_in, v_in = moments[name]
        w = w2d[name]
        return _adamw(w, g, m_in.reshape(w.shape), v_in.reshape(w.shape), "adamw_" + name)[0]

    updates = {name: update(name, g) for name, g in (("w_in_odd", gw_in_o), ("w_out_even", gw_out_e), ("w_out_odd", gw_out_o))}
    last2, last_got = _share_half_wait(*share[:4], updates["w_out_odd"][1], "share_last_wait")
    last_sum = _sum_chips_ordered(last2, last_got, mc_arr, "last_sum")
    h_win_e, s_win_e = _scatter_wait(send_sems, recv_sems, h_win_e, landing, last_sum, "scatter_in_even_wait")
    last_sems = _join_start([_sum_chips(h_win_e, s_win_e, mc_arr, "sum_chips0"), last_sum], "join_last_start")
    for name, g in g_small.items():
        updates[name] = update(name, g)
    gw_in_e, red_last = _join_wait(*last_sems[:3], updates["pool_w"][1], "join_last_wait")
    loss = red_last[at_l[1], 0]
    updates["ln_pre_even"] = update("ln_pre_even", red_last[at_l[0]:at_l[0] + 2].reshape(1, d))
    updates["w_in_even"] = update("w_in_even", gw_in_e)
    outs = [[u.reshape(moments[name][0].shape) for u in updates[name]] for name in w2d]
    grads_out, deltas, new_m, new_v = zip(*outs)
    return (loss, grad_x.reshape(x.shape), *grads_out, *deltas, *new_m, *new_v)
```

```python
import functools
import math

import jax
import jax.numpy as jnp
from jax import lax
from jax.experimental import pallas as pl
from jax.experimental.pallas import tpu as pltpu

F32 = jnp.float32
BF16 = jnp.bfloat16
EPS = 1e-6
N_CHIPS = 4
VMEM_LIMIT_V7X = 56 << 20
HEAD_DIM = 128
ATT_BLOCK = 256
POOL_WINDOWS = (2, 4, 8, 16)
ROW_TILE = 256
POOL_HALO = 16
CONV_HALO = 32
LANES = 128
ADAM_LR, ADAM_B1, ADAM_B2, ADAM_EPS, ADAM_WD, ADAM_STEP = 0.001, 0.9, 0.999, 1e-08, 0.01, 10
MESH_ID = pl.DeviceIdType.MESH
ANY = pl.BlockSpec(memory_space=pl.ANY)


def _cp(*sem):
    return pltpu.CompilerParams(dimension_semantics=sem or None, vmem_limit_bytes=VMEM_LIMIT_V7X)


def _pick_tile(n, cap):
    best = None
    for t in range(LANES, min(n, cap) + 1, LANES):
        if n % t == 0:
            best = t
    assert best is not None, (n, cap)
    return best


def _sigmoid(x):
    return 1.0 / (1.0 + jnp.exp(-x))


def _silu(x):
    return x * _sigmoid(x)


def _dsilu(x):
    s = _sigmoid(x)
    return s * (1.0 + x * (1.0 - s))


def _log_sigmoid(z):
    return jnp.minimum(z, 0.0) - jnp.log(1.0 + jnp.exp(-jnp.abs(z)))


def _rms_stats(x):
    r = lax.rsqrt(jnp.mean(x * x, axis=-1, keepdims=True) + EPS)
    return x * r, r


def _rms_bwd(dh, xhat, r, g):
    dxh = dh * g
    dx = r * (dxh - xhat * jnp.mean(dxh * xhat, axis=-1, keepdims=True))
    return dx, jnp.sum(dh * xhat, axis=0, keepdims=True)


def _acc_rows(ref, first, val):
    @pl.when(first)
    def _():
        ref[...] = val

    @pl.when(jnp.logical_not(first))
    def _():
        ref[...] += val


def _rcopy(src, dst, ssem, rsem, dev):
    return pltpu.make_async_remote_copy(src_ref=src, dst_ref=dst, send_sem=ssem, recv_sem=rsem,
                                        device_id=dev, device_id_type=MESH_ID)


def _place():
    x, y, c = lax.axis_index("x"), lax.axis_index("y"), lax.axis_index("c")
    chips = [(1 - x, y), (x, 1 - y), (1 - x, 1 - y)]
    return x, y, c, 2 * x + y, chips, (x, y, 1 - c)


class _GatherPlan:
    PER_ARRAY = 7

    def __init__(self, arrays, part=(0, 1, 1), at=(0.5, 0.8)):
        self.operands = list(arrays)
        self.out_shapes = [jax.ShapeDtypeStruct(a.shape, a.dtype) for a in arrays]
        self.aliases = {i: i for i in range(len(arrays))}
        self.nsems = self.PER_ARRAY * len(arrays)
        self.base = 0
        self.halves = [a.shape[1] // 2 for a in arrays]
        self.part = part
        self.at = at

    def schedule(self):
        return [(0.0, self.start), (self.at[0], self.relay), (self.at[1], self.relay_far)]

    def _rows(self, ref, a, chip, half, quarter=None):
        lo, hi, n = self.part
        h = self.halves[a]
        first, size = half * h + lo * h // n, (hi - lo) * h // n
        if quarter is not None:
            first, size = first + quarter * (size // 2), size // 2
        return ref.at[chip, pl.ds(first, size)]

    def _copy(self, src, dst, a, n, ssem, rsem, dev):
        return _rcopy(src, dst, ssem.at[self.base + self.PER_ARRAY * a + n], rsem.at[self.base + self.PER_ARRAY * a + n], dev)

    def _own(self, ins, outs, ssem, rsem):
        x, y, c, me, chips, sib = _place()
        return [self._copy(self._rows(ins[a], a, me, c), self._rows(outs[a], a, me, c), a, k, ssem, rsem, (*chips[k], c))
                for a in range(len(ins)) for k in (0, 1)]

    def _relays(self, outs, ssem, rsem, a, k):
        x, y, c, me, chips, sib = _place()
        chip = 2 * chips[k][0] + chips[k][1]
        whole, quarter = self._rows(outs[a], a, chip, c), self._rows(outs[a], a, chip, c, k)
        return (self._copy(whole, whole, a, k, ssem, rsem, (*chips[k], c)),
                self._copy(quarter, quarter, a, 2 + k, ssem, rsem, (*chips[1 - k], c)),
                self._copy(whole, whole, a, 4 + k, ssem, rsem, sib))

    def _far(self, outs, ssem, rsem, a):
        x, y, c, me, chips, sib = _place()
        chip = 2 * chips[2][0] + chips[2][1]
        whole = self._rows(outs[a], a, chip, c)
        got = [self._copy(q, q, a, 2 + k, ssem, rsem, (*chips[1 - k], c))
               for k, q in enumerate([self._rows(outs[a], a, chip, c, 0), self._rows(outs[a], a, chip, c, 1)])]
        return got, self._copy(whole, whole, a, 6, ssem, rsem, sib)

    def start(self, ins, outs, ssem, rsem):
        for cp in self._own(ins, outs, ssem, rsem):
            cp.start()

    def relay(self, ins, outs, ssem, rsem):
        for a in range(len(outs)):
            for k in (0, 1):
                landed, onward, to_sibling = self._relays(outs, ssem, rsem, a, k)
                landed.wait_recv()
                onward.start()
                to_sibling.start()

    def relay_far(self, ins, outs, ssem, rsem):
        for a in range(len(outs)):
            got, to_sibling = self._far(outs, ssem, rsem, a)
            for cp in got:
                cp.wait_recv()
            to_sibling.start()

    def finish(self, ins, outs, ssem, rsem):
        x, y, c, me, chips, sib = _place()
        for a in range(len(outs)):
            for k in range(3):
                ref = self._rows(outs[a], a, 2 * chips[k][0] + chips[k][1], 1 - c)
                self._copy(ref, ref, a, 4 + k, ssem, rsem, sib).wait_recv()
        for cp in self._own(ins, outs, ssem, rsem):
            cp.wait_send()
        for a in range(len(outs)):
            for k in (0, 1):
                _, onward, to_sibling = self._relays(outs, ssem, rsem, a, k)
                onward.wait_send()
                to_sibling.wait_send()
            self._far(outs, ssem, rsem, a)[1].wait_send()


class _ScatterPlan:
    def __init__(self, arrays, part=(0, 1, 1), into=None):
        self.n = len(arrays)
        self.operands = list(arrays) + list(into or [])
        self.out_shapes = [jax.ShapeDtypeStruct((3,) + a.shape[1:], a.dtype) for a in arrays]
        self.aliases = {self.n + i: i for i in range(self.n)} if into else {}
        self.nsems = 3 * self.n
        self.base = 0
        self.part = part

    def _copies(self, ins, outs, ssem, rsem):
        x, y, c, me, chips, sib = _place()
        lo, hi, n = self.part
        out = []
        for a in range(self.n):
            h = ins[a].shape[1]
            rows = pl.ds(lo * h // n, (hi - lo) * h // n)
            for k, chip in enumerate(chips):
                out.append(_rcopy(ins[a].at[2 * chip[0] + chip[1], rows], outs[a].at[k, rows],
                                  ssem.at[self.base + 3 * a + k], rsem.at[self.base + 3 * a + k], (*chip, c)))
        return out

    def schedule(self):
        return [(0.0, self.start)]

    def start(self, ins, outs, ssem, rsem):
        for cp in self._copies(ins, outs, ssem, rsem):
            cp.start()

    def finish(self, ins, outs, ssem, rsem):
        cps = self._copies(ins, outs, ssem, rsem)
        for cp in cps:
            cp.wait_recv()
        for cp in cps:
            cp.wait_send()


class _ShareHalfPlan(_ScatterPlan):
    def __init__(self, arrays):
        super().__init__(arrays)
        self.out_shapes = [jax.ShapeDtypeStruct((3, a.shape[0] // 2, a.shape[1]), a.dtype) for a in arrays]

    def _copies(self, ins, outs, ssem, rsem):
        x, y, c, me, chips, sib = _place()
        out = []
        for a in range(self.n):
            rh = ins[a].shape[0] // 2
            for k, chip in enumerate(chips):
                out.append(_rcopy(ins[a].at[pl.ds(c * rh, rh)], outs[a].at[k],
                                  ssem.at[self.base + 3 * a + k], rsem.at[self.base + 3 * a + k], (*chip, c)))
        return out


class _SwapPlan:
    def __init__(self, grads):
        self.operands = list(grads)
        self.out_shapes = [jax.ShapeDtypeStruct((g.shape[0], g.shape[1] // 2, g.shape[2]), g.dtype) for g in grads]
        self.aliases = {}
        self.nsems = len(grads)
        self.base = 0

    def _copies(self, ins, outs, ssem, rsem):
        x, y, c, me, chips, sib = _place()
        out = []
        for a, src in enumerate(ins):
            h = src.shape[1] // 2
            out.append(_rcopy(src.at[:, pl.ds((1 - c) * h, h), :], outs[a], ssem.at[self.base + a], rsem.at[self.base + a], sib))
        return out

    def schedule(self):
        return [(0.0, self.start)]

    def start(self, ins, outs, ssem, rsem):
        for cp in self._copies(ins, outs, ssem, rsem):
            cp.start()

    def finish(self, ins, outs, ssem, rsem):
        cps = self._copies(ins, outs, ssem, rsem)
        for cp in cps:
            cp.wait_recv()
        for cp in cps:
            cp.wait_send()


class _SendWholePlan(_SwapPlan):
    def __init__(self, arrays):
        self.operands = list(arrays)
        self.out_shapes = [jax.ShapeDtypeStruct(a.shape, a.dtype) for a in arrays]
        self.aliases = {}
        self.nsems = len(arrays)
        self.base = 0

    def _copies(self, ins, outs, ssem, rsem):
        x, y, c, me, chips, sib = _place()
        return [_rcopy(src, outs[a], ssem.at[self.base + a], rsem.at[self.base + a], sib) for a, src in enumerate(ins)]


class _GatherPieces:
    def __init__(self, arrays, n, at):
        self.pieces = [_GatherPlan(arrays, (j, j + 1, n), at) for j in range(n)]
        self.operands, self.out_shapes, self.aliases = self.pieces[0].operands, self.pieces[0].out_shapes, self.pieces[0].aliases
        self.nsems = sum(p.nsems for p in self.pieces)
        self.at = at
        self.base = 0

    @property
    def base(self):
        return self.pieces[0].base

    @base.setter
    def base(self, value):
        for j, p in enumerate(self.pieces):
            p.base = value + j * p.nsems

    def schedule(self):
        return [(0.0, self.start), (self.at[0], self.relay), (self.at[1], self.relay_far)]

    def _each(self, what, *a):
        for p in self.pieces:
            getattr(p, what)(*a)

    def start(self, *a):
        self._each("start", *a)

    def relay(self, *a):
        self._each("relay", *a)

    def relay_far(self, *a):
        self._each("relay_far", *a)

    def finish(self, *a):
        self._each("finish", *a)


class _SmallGatherPlan:
    def __init__(self, arrays, widths):
        self.operands = list(arrays)
        self.out_shapes = [jax.ShapeDtypeStruct(a.shape, a.dtype) for a in arrays]
        self.aliases = {i: i for i in range(3)}
        self.nsems = 9
        self.base = 0
        self.widths = widths

    def _part(self, ref, n, chip):
        w = self.widths[n]
        return ref.at[:, pl.ds(chip * w, w), :] if n == 0 else ref.at[:, pl.ds(chip * w, w)]

    def _copies(self, ins, outs, ssem, rsem, own):
        x, y, c, me, chips, sib = _place()
        out = []
        for n in range(3):
            for k, chip in enumerate(chips):
                which = me if own else 2 * chip[0] + chip[1]
                out.append(_rcopy(self._part(ins[n], n, which), self._part(outs[n], n, which),
                                  ssem.at[self.base + 3 * n + k], rsem.at[self.base + 3 * n + k], (*chip, c)))
        return out

    def schedule(self):
        return [(0.0, self.start)]

    def start(self, ins, outs, ssem, rsem):
        for cp in self._copies(ins, outs, ssem, rsem, True):
            cp.start()

    def finish(self, ins, outs, ssem, rsem):
        for cp in self._copies(ins, outs, ssem, rsem, False):
            cp.wait_recv()
        for cp in self._copies(ins, outs, ssem, rsem, True):
            cp.wait_send()


class _Multi:
    def __init__(self, plans):
        self.plans = plans
        self.operands, self.out_shapes, self.aliases, self.nsems = [], [], {}, 0
        self.spans = []
        for p in plans:
            ni, no = len(self.operands), len(self.out_shapes)
            self.spans.append((ni, ni + len(p.operands), no, no + len(p.out_shapes)))
            self.aliases.update({ni + i: no + j for i, j in p.aliases.items()})
            p.base = self.nsems
            self.nsems += p.nsems
            self.operands += p.operands
            self.out_shapes += p.out_shapes

    def schedule(self):
        def bound(fn, span):
            i0, i1, o0, o1 = span
            return lambda ins, outs, ssem, rsem: fn(ins[i0:i1], outs[o0:o1], ssem, rsem)

        stages = [(at, bound(fn, span)) for p, span in zip(self.plans, self.spans) for at, fn in p.schedule()]
        return sorted(stages, key=lambda s: s[0])

    def finish(self, ins, outs, ssem, rsem):
        for p, (i0, i1, o0, o1) in zip(self.plans, self.spans):
            p.finish(ins[i0:i1], outs[o0:o1], ssem, rsem)

    def results(self, extra):
        return [list(extra[o0:o1]) for (_, _, o0, o1) in self.spans]


class _Host:
    def __init__(self, comm, in_specs, out_specs, out_shape, scratch, prefetch=0):
        self.comm = comm
        self.n_in, self.n_out = len(in_specs), len(out_specs)
        self.in_specs, self.out_specs, self.out_shape, self.scratch = list(in_specs), list(out_specs), list(out_shape), list(scratch)
        self.aliases = {}
        self.args = []
        if comm is not None:
            self.in_specs += [ANY] * len(comm.operands)
            self.out_specs += [ANY] * len(comm.out_shapes)
            self.out_shape += comm.out_shapes
            self.scratch += [pltpu.SemaphoreType.DMA((comm.nsems,)), pltpu.SemaphoreType.DMA((comm.nsems,))]
            self.aliases = {prefetch + self.n_in + i: self.n_out + j for i, j in comm.aliases.items()}
            self.args = list(comm.operands)

    def split(self, refs):
        nc = len(self.args)
        nco = len(self.out_shape) - self.n_out
        ins, p = refs[:self.n_in], self.n_in + nc
        outs, rest = refs[p:p + self.n_out], refs[p + self.n_out + nco:]
        self._cargs = None
        if self.comm is not None:
            self._cargs = (refs[self.n_in:p], refs[p + self.n_out:p + self.n_out + nco], rest[-2], rest[-1])
            rest = rest[:-2]
        return ins, outs, rest

    def before(self, step, total):
        if self.comm is None:
            return

        for at, stage in self.comm.schedule():
            pl.when(step == min(total - 1, int(at * total)))(functools.partial(stage, *self._cargs))

    def after(self, step, total):
        if self.comm is None:
            return

        @pl.when(step == total - 1)
        def _():
            self.comm.finish(*self._cargs)

    def results(self, outs):
        return outs[:self.n_out], outs[self.n_out:]


def _cast_bf16(x, name):
    r, c = x.shape
    tr = ROW_TILE if r % ROW_TILE == 0 else r

    def body(x_ref, o_ref):
        o_ref[...] = x_ref[...].astype(BF16)

    return pl.pallas_call(
        body, name=name, grid=(r // tr,),
        in_specs=[pl.BlockSpec((tr, c), lambda i: (i, 0))],
        out_specs=pl.BlockSpec((tr, c), lambda i: (i, 0)),
        out_shape=jax.ShapeDtypeStruct((r, c), BF16), compiler_params=_cp("parallel"))(x)


def _cast_bf16_own_slab(x, me_arr, name):
    r, c = x.shape
    tr = ROW_TILE if r % ROW_TILE == 0 else r

    def body(me_ref, x_ref, o_ref):
        o_ref[...] = x_ref[...].astype(BF16)

    return pl.pallas_call(
        body, name=name,
        grid_spec=pltpu.PrefetchScalarGridSpec(
            num_scalar_prefetch=1, grid=(r // tr,),
            in_specs=[pl.BlockSpec((tr, c), lambda i, me: (i, 0))],
            out_specs=pl.BlockSpec((None, tr, c), lambda i, me: (me[0], i, 0))),
        out_shape=jax.ShapeDtypeStruct((N_CHIPS, r, c), BF16), compiler_params=_cp("parallel"))(me_arr, x)


def _prep(x, g, shards, me_arr, name, comm):
    s, d = x.shape
    steps = s // ROW_TILE
    tiles = [(w.shape[0] // steps, w.shape[1]) for w in shards]
    assert all(w.shape[0] % steps == 0 for w in shards)
    in_specs = [pl.BlockSpec((ROW_TILE, d), lambda i, me: (i, 0)), pl.BlockSpec((1, d), lambda i, me: (0, 0))]
    in_specs += [pl.BlockSpec(t, lambda i, me: (i, 0)) for t in tiles]
    out_specs = [pl.BlockSpec((ROW_TILE, d), lambda i, me: (i, 0))]
    out_specs += [pl.BlockSpec((None,) + t, lambda i, me: (me[0], i, 0)) for t in tiles]
    out_shape = [jax.ShapeDtypeStruct((s, d), BF16)] + [jax.ShapeDtypeStruct((N_CHIPS,) + w.shape, BF16) for w in shards]
    host = _Host(comm, in_specs, out_specs, out_shape, [], prefetch=1)

    def body(me_ref, *refs):
        (x_ref, g_ref, *w_refs), (h_ref, *slab_refs), _ = host.split(refs)
        step = pl.program_id(0)
        host.before(step, steps)
        xhat, _ = _rms_stats(x_ref[...])
        h_ref[...] = (xhat * g_ref[...]).astype(BF16)
        for w_ref, slab_ref in zip(w_refs, slab_refs):
            slab_ref[...] = w_ref[...].astype(BF16)
        host.after(step, steps)

    outs = pl.pallas_call(
        body, name=name,
        grid_spec=pltpu.PrefetchScalarGridSpec(num_scalar_prefetch=1, grid=(steps,), in_specs=host.in_specs,
                                               out_specs=host.out_specs, scratch_shapes=host.scratch),
        out_shape=host.out_shape, input_output_aliases=host.aliases,
        compiler_params=_cp("arbitrary"))(me_arr, x, g, *shards, *host.args)
    (h, *slabs), extra = host.results(outs)
    return h, slabs, extra


def _mm_nn(a, w3, name, comm=None):
    m, k = a.shape
    nsh, _, ns = w3.shape
    tm = 512 if m % 512 == 0 else ROW_TILE
    tn = _pick_tile(ns, 1024)
    per = ns // tn
    grid = (nsh * per, m // tm)
    host = _Host(comm,
                 [pl.BlockSpec((tm, k), lambda n, i: (i, 0)), pl.BlockSpec((None, k, tn), lambda n, i: (n // per, 0, n % per))],
                 [pl.BlockSpec((tm, tn), lambda n, i: (i, n))], [jax.ShapeDtypeStruct((m, nsh * ns), F32)], [])

    def body(*refs):
        (a_ref, w_ref), (o_ref,), _ = host.split(refs)
        step = pl.program_id(0) * grid[1] + pl.program_id(1)
        host.before(step, grid[0] * grid[1])
        o_ref[...] = jnp.dot(a_ref[...], w_ref[...], preferred_element_type=F32)
        host.after(step, grid[0] * grid[1])

    outs = pl.pallas_call(
        body, name=name, grid=grid, in_specs=host.in_specs, out_specs=host.out_specs, out_shape=host.out_shape,
        scratch_shapes=host.scratch, input_output_aliases=host.aliases,
        compiler_params=_cp("arbitrary", "arbitrary"))(a, w3, *host.args)
    (out,), extra = host.results(outs)
    return out, extra


def _mm_nt(a, b, name):
    m, k = a.shape
    n = b.shape[0]
    tm = 512 if m % 512 == 0 else ROW_TILE

    def body(a_ref, b_ref, o_ref):
        o_ref[...] = lax.dot_general(a_ref[...], b_ref[...], (((1,), (1,)), ((), ())), preferred_element_type=F32)

    return pl.pallas_call(
        body, name=name, grid=(m // tm,),
        in_specs=[pl.BlockSpec((tm, k), lambda i: (i, 0)), pl.BlockSpec((n, k), lambda i: (0, 0))],
        out_specs=pl.BlockSpec((tm, n), lambda i: (i, 0)),
        out_shape=jax.ShapeDtypeStruct((m, n), F32), compiler_params=_cp("parallel"))(a, b)


def _mm_tn(a, b, nsh, name, comm=None):
    s, m = a.shape
    n = b.shape[1]
    ns = n // nsh
    tm = 512 if m % 512 == 0 else ROW_TILE
    tn = _pick_tile(ns, 1024)
    per = ns // tn
    grid = (nsh * per, m // tm)
    host = _Host(comm, [pl.BlockSpec((s, tm), lambda j, i: (0, i)), pl.BlockSpec((s, tn), lambda j, i: (0, j))],
                 [pl.BlockSpec((None, tm, tn), lambda j, i: (j // per, i, j % per))],
                 [jax.ShapeDtypeStruct((nsh, m, ns), BF16)], [])

    def body(*refs):
        (a_ref, b_ref), (o_ref,), _ = host.split(refs)
        step = pl.program_id(0) * grid[1] + pl.program_id(1)
        host.before(step, grid[0] * grid[1])
        o_ref[...] = lax.dot_general(a_ref[...], b_ref[...], (((0,), (0,)), ((), ())),
                                     preferred_element_type=F32).astype(BF16)
        host.after(step, grid[0] * grid[1])

    outs = pl.pallas_call(
        body, name=name, grid=grid, in_specs=host.in_specs, out_specs=host.out_specs, out_shape=host.out_shape,
        scratch_shapes=host.scratch, input_output_aliases=host.aliases,
        compiler_params=_cp("arbitrary", "arbitrary"))(a, b, *host.args)
    (out,), extra = host.results(outs)
    return out, extra


def _tri(n, rel):
    row = lax.broadcasted_iota(jnp.int32, (2 * n, n), 0)
    col = lax.broadcasted_iota(jnp.int32, (2 * n, n), 1)
    return jnp.where(rel(jnp.where(row >= n, row - n, row), col), 1.0, 0.0).astype(BF16)


def _dot_split(x, tri2):
    hi = x.astype(BF16)
    lo = (x - hi.astype(F32)).astype(BF16)
    return jnp.dot(jnp.concatenate([hi, lo], axis=1), tri2, preferred_element_type=F32)


def _nt(a, b):
    return lax.dot_general(a, b, (((1,), (1,)), ((), ())), preferred_element_type=F32)


def _tn(a, b):
    return lax.dot_general(a, b, (((0,), (0,)), ((), ())), preferred_element_type=F32)


def _heads_per_step(nh):
    return max(h for h in (1, 2, 4) if nh % h == 0)


def _sba_fwd(p, sbw, name, comm=None):
    s = p.shape[0]
    nh = sbw // HEAD_DIM
    hp = _heads_per_step(nh)
    ngrp, hw = nh // hp, hp * HEAD_DIM
    blk = ATT_BLOCK
    nq = s // blk
    scale = 1.0 / math.sqrt(HEAD_DIM)
    host = _Host(comm,
                 [pl.BlockSpec((blk, hw), lambda g, i: (i, g)),
                  pl.BlockSpec((s, hw), lambda g, i: (0, ngrp + g)),
                  pl.BlockSpec((s, hw), lambda g, i: (0, 2 * ngrp + g))],
                 [pl.BlockSpec((blk, hw), lambda g, i: (i, g))] * 2,
                 [jax.ShapeDtypeStruct((s, sbw), F32)] * 2,
                 [pltpu.VMEM((s, hw), BF16)] * 2)

    def body(*refs):
        (q_ref, k_ref, v_ref), (o_ref, lt_ref), (kb_ref, vb_ref) = host.split(refs)
        i = pl.program_id(1)
        step = pl.program_id(0) * nq + i
        host.before(step, ngrp * nq)

        @pl.when(i == 0)
        def _():
            kb_ref[...] = k_ref[...].astype(BF16)
            vb_ref[...] = v_ref[...].astype(BF16)

        heads = [slice(h * HEAD_DIM, (h + 1) * HEAD_DIM) for h in range(hp)]
        qs = [q_ref[:, hd].astype(BF16) for hd in heads]
        later = _tri(blk, lambda r, c: r > c)
        causal = lax.broadcasted_iota(jnp.int32, (blk, blk), 1) < lax.broadcasted_iota(jnp.int32, (blk, blk), 0)

        def key_block(j, carry, diagonal):
            rows = pl.ds(pl.multiple_of(j * blk, blk), blk)
            hs = range(hp)
            z = [_nt(qs[h], kb_ref[rows, heads[h]]) * scale for h in hs]
            ls = [_log_sigmoid(z[h]) for h in hs]
            lm = [jnp.where(causal, ls[h] - z[h], 0.0) if diagonal else ls[h] - z[h] for h in hs]
            stay = [_dot_split(lm[h], later) for h in hs]
            w = [jnp.exp(ls[h] + stay[h] + carry[h][1]) for h in hs]
            if diagonal:
                w = [jnp.where(causal, w[h], 0.0) for h in hs]
            acc = [carry[h][0] + jnp.dot(w[h].astype(BF16), vb_ref[rows, heads[h]], preferred_element_type=F32) for h in hs]
            return tuple((acc[h], carry[h][1] + jnp.sum(lm[h], axis=1, keepdims=True)) for h in hs)

        init = tuple((jnp.zeros((blk, HEAD_DIM), F32), jnp.zeros((blk, 1), F32)) for _ in heads)
        carry = key_block(i, init, True)
        carry = lax.fori_loop(0, i, lambda n, c: key_block(i - 1 - n, c, False), carry)
        for h, hd in enumerate(heads):
            o_ref[:, hd] = carry[h][0]
            lt_ref[:, hd] = jnp.broadcast_to(carry[h][1], (blk, HEAD_DIM))
        host.after(step, ngrp * nq)

    outs = pl.pallas_call(
        body, name=name, grid=(ngrp, nq), in_specs=host.in_specs, out_specs=host.out_specs, out_shape=host.out_shape,
        scratch_shapes=host.scratch, input_output_aliases=host.aliases,
        compiler_params=_cp("arbitrary", "arbitrary"))(p, p, p, *host.args)
    (out, ltot), extra = host.results(outs)
    return out, ltot, extra


def _sba_bwd(p, ltot, dout, sbw, name, comm=None):
    s = p.shape[0]
    nh = sbw // HEAD_DIM
    hp = _heads_per_step(nh)
    ngrp, hw = nh // hp, hp * HEAD_DIM
    blk = ATT_BLOCK
    nq = s // blk
    scale = 1.0 / math.sqrt(HEAD_DIM)
    blk_spec = pl.BlockSpec((blk, hw), lambda g, i: (i, g))
    col_spec = pl.BlockSpec((s, hw), lambda g, i: (0, g))
    host = _Host(comm,
                 [blk_spec, pl.BlockSpec((s, hw), lambda g, i: (0, ngrp + g)),
                  pl.BlockSpec((s, hw), lambda g, i: (0, 2 * ngrp + g)), blk_spec, blk_spec],
                 [blk_spec, col_spec, col_spec], [jax.ShapeDtypeStruct((s, sbw), BF16)] * 3,
                 [pltpu.VMEM((s, hw), BF16)] * 2 + [pltpu.VMEM((s, hw), F32)] * 2)

    def body(*refs):
        (q_ref, k_ref, v_ref, lt_ref, do_ref), (dq_ref, dk_ref, dv_ref), (kb_ref, vb_ref, dka_ref, dva_ref) = host.split(refs)
        i = pl.program_id(1)
        step = pl.program_id(0) * nq + i
        host.before(step, ngrp * nq)

        @pl.when(i == 0)
        def _():
            kb_ref[...] = k_ref[...].astype(BF16)
            vb_ref[...] = v_ref[...].astype(BF16)
            dka_ref[...] = jnp.zeros_like(dka_ref)
            dva_ref[...] = jnp.zeros_like(dva_ref)

        heads = [slice(h * HEAD_DIM, (h + 1) * HEAD_DIM) for h in range(hp)]
        qs = [q_ref[:, hd].astype(BF16) for hd in heads]
        dos = [do_ref[:, hd].astype(BF16) for hd in heads]
        ltots = [lt_ref[:, h * HEAD_DIM:h * HEAD_DIM + 1] for h in range(hp)]
        upto = _tri(blk, lambda r, c: r <= c)
        before = _tri(blk, lambda r, c: r < c)
        causal = lax.broadcasted_iota(jnp.int32, (blk, blk), 1) < lax.broadcasted_iota(jnp.int32, (blk, blk), 0)

        def key_block(j, carry, diagonal):
            rows = pl.ds(pl.multiple_of(j * blk, blk), blk)
            hs = range(hp)
            kj = [kb_ref[rows, heads[h]] for h in hs]
            vj = [vb_ref[rows, heads[h]] for h in hs]
            z = [_nt(qs[h], kj[h]) * scale for h in hs]
            dw = [_nt(dos[h], vj[h]) for h in hs]
            ls = [_log_sigmoid(z[h]) for h in hs]
            lm = [jnp.where(causal, ls[h] - z[h], 0.0) if diagonal else ls[h] - z[h] for h in hs]
            stay = [ltots[h] - carry[h][1] - _dot_split(lm[h], upto) for h in hs]
            w = [jnp.exp(ls[h] + stay[h]) for h in hs]
            if diagonal:
                w = [jnp.where(causal, w[h], 0.0) for h in hs]
            da = [dw[h] * w[h] for h in hs]
            sig = [jnp.exp(ls[h]) for h in hs]
            chain = [sig[h] * (carry[h][2] + _dot_split(da[h], before)) for h in hs]
            if diagonal:
                chain = [jnp.where(causal, chain[h], 0.0) for h in hs]
            dzb = [((da[h] * (1.0 - sig[h]) - chain[h]) * scale).astype(BF16) for h in hs]
            dq = [carry[h][0] + jnp.dot(dzb[h], kj[h], preferred_element_type=F32) for h in hs]
            for h in hs:
                dka_ref[rows, heads[h]] += _tn(dzb[h], qs[h])
            for h in hs:
                dva_ref[rows, heads[h]] += _tn(w[h].astype(BF16), dos[h])
            return tuple((dq[h], carry[h][1] + jnp.sum(lm[h], axis=1, keepdims=True),
                          carry[h][2] + jnp.sum(da[h], axis=1, keepdims=True)) for h in hs)

        zero = jnp.zeros((blk, 1), F32)
        init = tuple((jnp.zeros((blk, HEAD_DIM), F32), zero, zero) for _ in heads)
        carry = lax.fori_loop(0, i, lambda j, c: key_block(j, c, False), init)
        carry = key_block(i, carry, True)
        for h, hd in enumerate(heads):
            dq_ref[:, hd] = carry[h][0].astype(BF16)

        @pl.when(i == nq - 1)
        def _():
            dk_ref[...] = dka_ref[...].astype(BF16)
            dv_ref[...] = dva_ref[...].astype(BF16)

        host.after(step, ngrp * nq)

    outs = pl.pallas_call(
        body, name=name, grid=(ngrp, nq), in_specs=host.in_specs, out_specs=host.out_specs, out_shape=host.out_shape,
        scratch_shapes=host.scratch, input_output_aliases=host.aliases,
        compiler_params=_cp("arbitrary", "arbitrary"))(p, p, p, ltot, dout, *host.args)
    (dq, dk, dv), extra = host.results(outs)
    return dq, dk, dv, extra


def _pool_groups(pad_ref, tile, row0, gd, halo):
    row = row0 + lax.broadcasted_iota(jnp.int32, (tile, 1), 0)
    out = []
    for gi, win in enumerate(POOL_WINDOWS):
        cs = slice(gi * gd, (gi + 1) * gd)
        tok = pad_ref[halo:halo + tile, cs]
        acc = tok
        for j in range(1, win):
            acc = acc + pad_ref[halo - j:halo - j + tile, cs]
        cnt = jnp.minimum(win, row + 1).astype(F32)
        out.append(acc / cnt - tok)
    return out


def _even_mix_fwd(p, att, pool_w, pool_scale, d, name):
    s = p.shape[0]
    half = d // 2
    gd = half // len(POOL_WINDOWS)
    t, hb = ROW_TILE, POOL_HALO

    def body(u_ref, uh_ref, g_ref, a_ref, pw_ref, sc_ref, y_ref, pad_ref):
        i = pl.program_id(0)
        pad_ref[0:hb, :] = jnp.where(i > 0, uh_ref[...], 0.0)
        pad_ref[hb:, :] = u_ref[...]
        pooled = _pool_groups(pad_ref, t, i * t, gd, hb)
        for gi in range(len(POOL_WINDOWS)):
            cs = slice(gi * gd, (gi + 1) * gd)
            po = jnp.dot(pooled[gi].astype(BF16), pw_ref[gi], preferred_element_type=F32) * sc_ref[:, cs]
            y_ref[:, half + gi * gd:half + (gi + 1) * gd] = (po * _silu(g_ref[:, half + gi * gd:half + (gi + 1) * gd])).astype(BF16)
        y_ref[:, :half] = (a_ref[...] * _silu(g_ref[:, :half])).astype(BF16)

    return pl.pallas_call(
        body, name=name, grid=(s // t,),
        in_specs=[pl.BlockSpec((t, half), lambda i: (i, 3)),
                  pl.BlockSpec((hb, half), lambda i: (jnp.maximum(i * (t // hb) - 1, 0), 3)),
                  pl.BlockSpec((t, d), lambda i: (i, 2)),
                  pl.BlockSpec((t, half), lambda i: (i, 0)),
                  pl.BlockSpec(pool_w.shape, lambda i: (0, 0, 0)),
                  pl.BlockSpec((1, half), lambda i: (0, 0))],
        out_specs=pl.BlockSpec((t, d), lambda i: (i, 0)),
        out_shape=jax.ShapeDtypeStruct((s, d), BF16),
        scratch_shapes=[pltpu.VMEM((hb + t, half), F32)],
        compiler_params=_cp("parallel"))(p, p, p, att, pool_w, pool_scale)


def _even_mix_bwd(p, att, dy, pool_w, pool_scale, d, name, comm=None):
    s = p.shape[0]
    half = d // 2
    ng = len(POOL_WINDOWS)
    gd = half // ng
    t, hb = ROW_TILE, POOL_HALO
    nt = s // t
    host = _Host(
        comm,
        [pl.BlockSpec((t, half), lambda i: (i, 3)),
         pl.BlockSpec((hb, half), lambda i: (jnp.maximum(i * (t // hb) - 1, 0), 3)),
         pl.BlockSpec((t, d), lambda i: (i, 2)),
         pl.BlockSpec((hb, half), lambda i: (jnp.minimum((i + 1) * (t // hb), s // hb - 1), 5)),
         pl.BlockSpec((t, half), lambda i: (i, 0)),
         pl.BlockSpec((t, d), lambda i: (i, 0)),
         pl.BlockSpec((hb, half), lambda i: (jnp.minimum((i + 1) * (t // hb), s // hb - 1), 1)),
         pl.BlockSpec(pool_w.shape, lambda i: (0, 0, 0)),
         pl.BlockSpec((1, half), lambda i: (0, 0))],
        [pl.BlockSpec((t, half), lambda i: (i, 0)),
         pl.BlockSpec((t, half), lambda i: (i, 0)),
         pl.BlockSpec((t, d), lambda i: (i, 0)),
         pl.BlockSpec((1, half), lambda i: (0, 0)),
         pl.BlockSpec((ng, gd, gd), lambda i: (0, 0, 0))],
        [jax.ShapeDtypeStruct((s, half), F32), jax.ShapeDtypeStruct((s, half), BF16),
         jax.ShapeDtypeStruct((s, d), BF16), jax.ShapeDtypeStruct((1, half), F32),
         jax.ShapeDtypeStruct((ng, gd, gd), F32)],
        [pltpu.VMEM((hb + t, half), F32), pltpu.VMEM((t + hb, half), F32)])

    def body(*refs):
        ((u_ref, uh_ref, g_ref, gh_ref, a_ref, dy_ref, dyh_ref, pw_ref, sc_ref),
         (da_ref, du_ref, dg_ref, dsc_ref, dpw_ref), (pad_ref, dn_ref)) = host.split(refs)
        i = pl.program_id(0)
        host.before(i, nt)
        first = i == 0
        pad_ref[0:hb, :] = jnp.where(i > 0, uh_ref[...], 0.0)
        pad_ref[hb:, :] = u_ref[...]
        pooled = _pool_groups(pad_ref, t, i * t, gd, hb)
        g1 = g_ref[:, :half]
        dy1 = dy_ref[:, :half]
        da_ref[...] = dy1 * _silu(g1)
        dg_ref[:, :half] = (dy1 * a_ref[...] * _dsilu(g1)).astype(BF16)
        row = i * t + lax.broadcasted_iota(jnp.int32, (t + hb, 1), 0)
        for gi, win in enumerate(POOL_WINDOWS):
            cs = slice(gi * gd, (gi + 1) * gd)
            cs2 = slice(half + gi * gd, half + (gi + 1) * gd)
            w = pw_ref[gi]
            pb = pooled[gi].astype(BF16)
            zp = jnp.dot(pb, w, preferred_element_type=F32)
            g2 = g_ref[:, cs2]
            dy2 = dy_ref[:, cs2]
            dg_ref[:, cs2] = (dy2 * zp * sc_ref[:, cs] * _dsilu(g2)).astype(BF16)
            dpo = dy2 * _silu(g2)
            _acc_rows(dsc_ref.at[:, cs], first, jnp.sum(dpo * zp, axis=0, keepdims=True))
            dz = (dpo * sc_ref[:, cs]).astype(BF16)
            _acc_rows(dpw_ref.at[gi], first, _tn(pb, dz))
            dzh = jnp.where(i < nt - 1, dyh_ref[:, cs] * _silu(gh_ref[:, cs]) * sc_ref[:, cs], 0.0).astype(BF16)
            dpool = _nt(dz, w)
            dpool_h = _nt(dzh, w)
            cnt = jnp.minimum(win, row + 1).astype(F32)
            dn_ref[0:t, cs] = dpool / cnt[0:t]
            dn_ref[t:, cs] = dpool_h / cnt[t:]
            acc = dn_ref[0:t, cs]
            for j in range(1, win):
                acc = acc + dn_ref[j:j + t, cs]
            du_ref[:, cs] = (acc - dpool).astype(BF16)
        host.after(i, nt)

    outs = pl.pallas_call(
        body, name=name, grid=(nt,), in_specs=host.in_specs, out_specs=host.out_specs, out_shape=host.out_shape,
        scratch_shapes=host.scratch, input_output_aliases=host.aliases,
        compiler_params=_cp("arbitrary"))(p, p, p, p, att, dy, dy, pool_w, pool_scale, *host.args)
    return host.results(outs)


def _mm_out_even(y, w, x, g_post, g_pre_next, name):
    s, k = y.shape
    d = w.shape[1]
    t = ROW_TILE

    def body(y_ref, w_ref, x_ref, gp_ref, gn_ref, o_ref, x1_ref, h1_ref):
        for r0 in range(0, t, t // 2):
            rows = slice(r0, r0 + t // 2)
            o = jnp.dot(y_ref[rows, :], w_ref[...], preferred_element_type=F32)
            o_ref[rows, :] = o
            ohat, _ = _rms_stats(o)
            x1 = x_ref[rows, :] + ohat * gp_ref[...]
            x1_ref[rows, :] = x1
            xhat, _ = _rms_stats(x1)
            h1_ref[rows, :] = (xhat * gn_ref[...]).astype(BF16)

    row = lambda c: pl.BlockSpec((t, c), lambda i: (i, 0))
    vec = pl.BlockSpec((1, d), lambda i: (0, 0))
    return pl.pallas_call(
        body, name=name, grid=(s // t,),
        in_specs=[row(k), pl.BlockSpec((k, d), lambda i: (0, 0)), row(d), vec, vec],
        out_specs=[row(d), row(d), row(d)],
        out_shape=[jax.ShapeDtypeStruct((s, d), F32), jax.ShapeDtypeStruct((s, d), F32),
                   jax.ShapeDtypeStruct((s, d), BF16)],
        compiler_params=_cp("parallel"))(y, w, x, g_post, g_pre_next)


def _mm_out_odd(y, w, x1, g_post, target, name):
    s, k = y.shape
    d = w.shape[1]
    t = ROW_TILE

    def body(y_ref, w_ref, x_ref, gp_ref, tg_ref, do_ref, dx_ref, loss_ref, dgp_ref):
        first = pl.program_id(0) == 0
        gp = gp_ref[...]
        part = dgp = None
        for r0 in range(0, t, t // 2):
            rows = slice(r0, r0 + t // 2)
            o = jnp.dot(y_ref[rows, :], w_ref[...], preferred_element_type=F32)
            ohat, r = _rms_stats(o)
            diff = x_ref[rows, :] + ohat * gp - tg_ref[rows, :]
            part_half = 0.5 * jnp.sum(jnp.mean(diff * diff, axis=-1, keepdims=True), axis=0, keepdims=True)
            dx2 = diff * (1.0 / d)
            dx_ref[rows, :] = dx2
            do, dgp_half = _rms_bwd(dx2, ohat, r, gp)
            do_ref[rows, :] = do.astype(BF16)
            part = part_half if part is None else part + part_half
            dgp = dgp_half if dgp is None else dgp + dgp_half
        _acc_rows(loss_ref, first, jnp.broadcast_to(part, loss_ref.shape))
        _acc_rows(dgp_ref, first, dgp)

    row = lambda c: pl.BlockSpec((t, c), lambda i: (i, 0))
    vec = pl.BlockSpec((1, d), lambda i: (0, 0))
    return pl.pallas_call(
        body, name=name, grid=(s // t,),
        in_specs=[row(k), pl.BlockSpec((k, d), lambda i: (0, 0)), row(d), vec, row(d)],
        out_specs=[row(d), row(d), pl.BlockSpec((8, LANES), lambda i: (0, 0)), vec],
        out_shape=[jax.ShapeDtypeStruct((s, d), BF16), jax.ShapeDtypeStruct((s, d), F32),
                   jax.ShapeDtypeStruct((8, LANES), F32), jax.ShapeDtypeStruct((1, d), F32)],
        compiler_params=_cp("arbitrary"))(y, w, x1, g_post, target)


def _layer_norm(d1, cg, cb):
    mu = jnp.mean(d1, axis=-1, keepdims=True)
    cen = d1 - mu
    rstd = lax.rsqrt(jnp.mean(cen * cen, axis=-1, keepdims=True) + EPS)
    n = cen * rstd
    return n, rstd, n * cg + cb


SUBLANES = 8
ROW_STRIP = 64
GATHER_PIECES = 8
CONV_ROWS = 64


def _make_shifts(pad_ref, cs, sh_ref):
    rows = sh_ref.shape[1]
    for r in range(1, SUBLANES):
        sh_ref[r - 1] = pad_ref[r:r + rows, cs]


def _by_shift(taps, base, sign=1):
    return sorted(range(taps), key=lambda k: ((sign * (base + k)) % SUBLANES, k))


def _window(pad_ref, cs, sh_ref, off, t):
    m, r = divmod(off, SUBLANES)
    if r == 0:
        return pad_ref[SUBLANES * m:SUBLANES * m + t, cs]
    return sh_ref[r - 1, SUBLANES * m:SUBLANES * m + t, :]


def _odd_mix_fwd(p, sconv_w, dconv_w, dconv_b, cnorm_g, cnorm_b, d, name):
    s = p.shape[0]
    w = d // 2
    k3, k31 = sconv_w.shape[0], dconv_w.shape[0]
    t, hb = ROW_TILE, CONV_HALO
    assert hb >= k31 - 1 and w % LANES == 0

    def body(p_ref, ph_ref, w3_ref, w31_ref, b31_ref, cg_ref, cb_ref, y_ref, s3_ref, d1_ref, mpad, dpad, sh_ref):
        i = pl.program_id(0)
        mpad[0:hb, :] = jnp.where(i > 0, ph_ref[:, 2 * w:3 * w] * ph_ref[:, 0:w], 0.0)
        mpad[hb:, :] = p_ref[:, 2 * w:3 * w] * p_ref[:, 0:w]
        dpad[0:hb, :] = jnp.where(i > 0, ph_ref[:, 3 * w:4 * w] * _sigmoid(ph_ref[:, 4 * w:5 * w]), 0.0)
        dpad[hb:, :] = p_ref[:, 3 * w:4 * w] * _sigmoid(p_ref[:, 4 * w:5 * w])
        for c0 in range(0, w, LANES):
            cs = slice(c0, c0 + LANES)
            acc = jnp.zeros((t, LANES), F32)
            for kk in range(k3):
                acc = acc + w3_ref[kk:kk + 1, cs] * mpad[hb - (k3 - 1) + kk:hb - (k3 - 1) + kk + t, cs]
            s3_ref[:, cs] = acc
            _make_shifts(dpad, cs, sh_ref)
            for r0 in range(0, t, CONV_ROWS):
                acc = jnp.zeros((CONV_ROWS, LANES), F32)
                for kk in _by_shift(k31, hb - (k31 - 1)):
                    acc = acc + w31_ref[kk:kk + 1, cs] * _window(dpad, cs, sh_ref, hb - (k31 - 1) + kk + r0, CONV_ROWS)
                d1_ref[r0:r0 + CONV_ROWS, cs] = acc + b31_ref[:, cs]
        _, _, d2 = _layer_norm(d1_ref[...], cg_ref[...], cb_ref[...])
        y_ref[:, :w] = (p_ref[:, w:2 * w] * s3_ref[...] * _silu(p_ref[:, 5 * w:6 * w])).astype(BF16)
        y_ref[:, w:] = (_silu(d2) * _silu(p_ref[:, 6 * w:7 * w])).astype(BF16)

    row = lambda c: pl.BlockSpec((t, c), lambda i: (i, 0))
    full = lambda a: pl.BlockSpec(a.shape, lambda i: (0, 0))
    return pl.pallas_call(
        body, name=name, grid=(s // t,),
        in_specs=[row(7 * w),
                  pl.BlockSpec((hb, 5 * w), lambda i: (jnp.maximum(i * (t // hb) - 1, 0), 0)),
                  full(sconv_w), full(dconv_w), full(dconv_b), full(cnorm_g), full(cnorm_b)],
        out_specs=[row(d), row(w), row(w)],
        out_shape=[jax.ShapeDtypeStruct((s, d), BF16), jax.ShapeDtypeStruct((s, w), F32),
                   jax.ShapeDtypeStruct((s, w), F32)],
        scratch_shapes=[pltpu.VMEM((hb + t, w), F32)] * 2 + [pltpu.VMEM((SUBLANES - 1, hb + t - SUBLANES, LANES), F32)],
        compiler_params=_cp("parallel"))(p, p, sconv_w, dconv_w, dconv_b, cnorm_g, cnorm_b)


def _odd_bwd_rows(p, s3, d1, dy, cnorm_g, cnorm_b, d, name, comm=None):
    s = p.shape[0]
    w = d // 2
    t = ROW_TILE
    col = lambda j: pl.BlockSpec((t, w), lambda i: (i, j))
    row = lambda c: pl.BlockSpec((t, c), lambda i: (i, 0))
    vec = pl.BlockSpec((1, w), lambda i: (0, 0))
    host = _Host(comm, [col(1), col(5), col(6), row(w), row(w), row(d), vec, vec],
                 [row(w), row(d), row(w), row(w), vec, vec, vec],
                 [jax.ShapeDtypeStruct((s, w), BF16), jax.ShapeDtypeStruct((s, d), BF16),
                  jax.ShapeDtypeStruct((s, w), F32), jax.ShapeDtypeStruct((s, w), F32)] + [jax.ShapeDtypeStruct((1, w), F32)] * 3, [])

    def body(*refs):
        ((bc_ref, g1_ref, g2_ref, s3_ref, d1_ref, dy_ref, cg_ref, cb_ref),
         (dbc_ref, dg_ref, ds3_ref, dd1_ref, dcg_ref, dcb_ref, db_ref), _) = host.split(refs)
        step = pl.program_id(0)
        host.before(step, s // t)
        first = step == 0

        def strip(j, sums):
            rows = slice(j * ROW_STRIP, (j + 1) * ROW_STRIP)
            g1, g2 = g1_ref[rows, :], g2_ref[rows, :]
            bc, s3v = bc_ref[rows, :], s3_ref[rows, :]
            dy1, dy2 = dy_ref[rows, :w], dy_ref[rows, w:]
            n, rstd, d2 = _layer_norm(d1_ref[rows, :], cg_ref[...], cb_ref[...])
            dg_ref[rows, :w] = (dy1 * bc * s3v * _dsilu(g1)).astype(BF16)
            dg_ref[rows, w:] = (dy2 * _silu(d2) * _dsilu(g2)).astype(BF16)
            dco = dy1 * _silu(g1)
            dbc_ref[rows, :] = (dco * s3v).astype(BF16)
            ds3_ref[rows, :] = dco * bc
            dd2 = dy2 * _silu(g2) * _dsilu(d2)
            dn = dd2 * cg_ref[...]
            dd1 = rstd * (dn - jnp.mean(dn, axis=-1, keepdims=True) - n * jnp.mean(dn * n, axis=-1, keepdims=True))
            dd1_ref[rows, :] = dd1
            dcb, dcg, db = sums
            return (dcb + jnp.sum(dd2, axis=0, keepdims=True), dcg + jnp.sum(dd2 * n, axis=0, keepdims=True),
                    db + jnp.sum(dd1, axis=0, keepdims=True))

        zero = jnp.zeros((1, w), F32)
        sums = (zero, zero, zero)
        for j in range(t // ROW_STRIP):
            sums = strip(j, sums)
        dcb, dcg, db = sums
        _acc_rows(dcb_ref, first, dcb)
        _acc_rows(dcg_ref, first, dcg)
        _acc_rows(db_ref, first, db)
        host.after(step, s // t)

    outs = pl.pallas_call(
        body, name=name, grid=(s // t,), in_specs=host.in_specs, out_specs=host.out_specs, out_shape=host.out_shape,
        scratch_shapes=host.scratch, input_output_aliases=host.aliases,
        compiler_params=_cp("arbitrary"))(p, p, p, s3, d1, dy, cnorm_g, cnorm_b, *host.args)
    return host.results(outs)


def _odd_bwd_conv(p, ds3, dd1, sconv_w, dconv_w, d, name):
    s = p.shape[0]
    w = d // 2
    k3, k31 = sconv_w.shape[0], dconv_w.shape[0]
    t, hb, ha = ROW_TILE, CONV_HALO, 8
    nt = s // t
    assert hb >= k31 - 1 and ha >= k3 - 1

    def body(hc_ref, cc_ref, ga_ref, gb_ref, hch_ref, cch_ref, gah_ref, gbh_ref, ds3_ref, ds3h_ref, dd1_ref, dd1h_ref,
             w3_ref, w31_ref, dhc_ref, dcc_ref, dga_ref, dgb_ref, dw3_ref, dw31_ref, mpad, dpad, s3pad, d1pad, sh_ref):
        i = pl.program_id(0)
        first = i == 0
        last = i == nt - 1
        mpad[0:hb, :] = jnp.where(i > 0, cch_ref[...] * hch_ref[...], 0.0)
        mpad[hb:, :] = cc_ref[...] * hc_ref[...]
        dpad[0:hb, :] = jnp.where(i > 0, gah_ref[...] * _sigmoid(gbh_ref[...]), 0.0)
        dpad[hb:, :] = ga_ref[...] * _sigmoid(gb_ref[...])
        s3pad[0:t, :] = ds3_ref[...]
        s3pad[t:, :] = jnp.where(last, 0.0, ds3h_ref[...])
        d1pad[0:t, :] = dd1_ref[...]
        d1pad[t:, :] = jnp.where(last, 0.0, dd1h_ref[...])

        @pl.when(first)
        def _():
            dw3_ref[...] = jnp.zeros_like(dw3_ref)
            dw31_ref[...] = jnp.zeros_like(dw31_ref)

        def fold(v):
            return jnp.sum(v.reshape(v.shape[0] // SUBLANES, SUBLANES, LANES), axis=0)

        groups = range(0, t, CONV_ROWS)
        for c0 in range(0, w, LANES):
            cs = slice(c0, c0 + LANES)
            ds3v = s3pad[0:t, cs]
            dm = jnp.zeros((t, LANES), F32)
            for kk in range(k3):
                dm = dm + w3_ref[kk:kk + 1, cs] * s3pad[k3 - 1 - kk:k3 - 1 - kk + t, cs]
                off = hb - (k3 - 1) + kk
                dw3_ref[SUBLANES * kk:SUBLANES * (kk + 1), cs] += fold(ds3v * mpad[off:off + t, cs])
            dcc_ref[:, cs] = (dm * hc_ref[:, cs]).astype(BF16)
            dhc_ref[:, cs] = (dm * cc_ref[:, cs]).astype(BF16)
            _make_shifts(d1pad, cs, sh_ref)
            for r0 in groups:
                rows = slice(r0, r0 + CONV_ROWS)
                dd0 = jnp.zeros((CONV_ROWS, LANES), F32)
                for kk in _by_shift(k31, -(k31 - 1), -1):
                    dd0 = dd0 + w31_ref[kk:kk + 1, cs] * _window(d1pad, cs, sh_ref, k31 - 1 - kk + r0, CONV_ROWS)
                sgb = _sigmoid(gb_ref[rows, cs])
                dga_ref[rows, cs] = (dd0 * sgb).astype(BF16)
                dgb_ref[rows, cs] = (dd0 * ga_ref[rows, cs] * sgb * (1.0 - sgb)).astype(BF16)
            _make_shifts(dpad, cs, sh_ref)
            for kk in _by_shift(k31, hb - (k31 - 1)):
                part = jnp.zeros((SUBLANES, LANES), F32)
                for r0 in groups:
                    part = part + fold(d1pad[r0:r0 + CONV_ROWS, cs]
                                       * _window(dpad, cs, sh_ref, hb - (k31 - 1) + kk + r0, CONV_ROWS))
                dw31_ref[SUBLANES * kk:SUBLANES * (kk + 1), cs] += part

    col = lambda j: pl.BlockSpec((t, w), lambda i: (i, j))
    pre = lambda j: pl.BlockSpec((hb, w), lambda i: (jnp.maximum(i * (t // hb) - 1, 0), j))
    row = pl.BlockSpec((t, w), lambda i: (i, 0))
    post = lambda h: pl.BlockSpec((h, w), lambda i: (jnp.minimum((i + 1) * (t // h), s // h - 1), 0))
    full = lambda a: pl.BlockSpec(a.shape, lambda i: (0, 0))
    dhc, dcc, dga, dgb, dw3, dw31 = pl.pallas_call(
        body, name=name, grid=(nt,),
        in_specs=[col(0), col(2), col(3), col(4), pre(0), pre(2), pre(3), pre(4),
                  row, post(ha), row, post(hb), full(sconv_w), full(dconv_w)],
        out_specs=[row, row, row, row, pl.BlockSpec((SUBLANES * k3, w), lambda i: (0, 0)),
                   pl.BlockSpec((SUBLANES * k31, w), lambda i: (0, 0))],
        out_shape=[jax.ShapeDtypeStruct((s, w), BF16)] * 4
        + [jax.ShapeDtypeStruct((SUBLANES * k3, w), F32), jax.ShapeDtypeStruct((SUBLANES * k31, w), F32)],
        scratch_shapes=[pltpu.VMEM((hb + t, w), F32)] * 2 + [pltpu.VMEM((t + ha, w), F32), pltpu.VMEM((t + hb, w), F32),
                                                             pltpu.VMEM((SUBLANES - 1, hb + t - SUBLANES, LANES), F32)],
        compiler_params=_cp("arbitrary"))(p, p, p, p, p, p, p, p, ds3, ds3, dd1, dd1, sconv_w, dconv_w)
    return dhc, dcc, dga, dgb, jnp.sum(dw3.reshape(k3, SUBLANES, w), axis=1), jnp.sum(dw31.reshape(k31, SUBLANES, w), axis=1)


def _mm_in_bwd(dp, w3, x, g_pre, dres, post, name, comm=None):
    s = dp.shape[0]
    nsh, d, ns = w3.shape
    t = 512 if s % 512 == 0 else ROW_TILE
    nt = s // t
    ks = 2 if (ns // 2) % LANES == 0 else 1
    nk, kw = nsh * ks, ns // ks
    chunk = 128
    nchunk = t // chunk
    row = pl.BlockSpec((t, d), lambda i, k: (i, 0))
    vec = pl.BlockSpec((1, d), lambda i, k: (0, 0))
    rowwise = [x, dres] + ([post[0]] if post is not None else [])
    in_specs = [pl.BlockSpec((t, kw), lambda i, k: (i, k)), pl.BlockSpec((None, d, kw), lambda i, k: (k // ks, 0, k % ks)), vec]
    out_specs = [row, vec]
    out_shape = [jax.ShapeDtypeStruct((s, d), F32), jax.ShapeDtypeStruct((1, d), F32)]
    args = [dp, w3, g_pre]
    if post is not None:
        in_specs += [vec]
        out_specs += [row, vec]
        out_shape += [jax.ShapeDtypeStruct((s, d), BF16), jax.ShapeDtypeStruct((1, d), F32)]
        args += [post[1]]
    n_blocked = len(in_specs)
    in_specs += [ANY] * len(rowwise)
    args += rowwise
    host = _Host(comm, in_specs, out_specs, out_shape,
                 [pltpu.VMEM((t, d), F32), pltpu.VMEM((len(rowwise), 2, chunk, d), F32), pltpu.SemaphoreType.DMA((len(rowwise), 2))])

    def body(*refs):
        ins, outs, (acc_ref, buf_ref, sem_ref) = host.split(refs)
        dp_ref, w_ref, g_ref = ins[:3]
        hbm = ins[n_blocked:]
        dx_ref, dg_ref = outs[:2]
        tile = pl.program_id(0)
        kk = pl.program_id(1)
        first = tile == 0
        step = tile * nk + kk
        host.before(step, nt * nk)
        part = _nt(dp_ref[...], w_ref[...])

        @pl.when(kk == 0)
        def _():
            acc_ref[...] = part

        @pl.when(kk > 0)
        def _():
            acc_ref[...] += part

        def fetch(ci, slot):
            return [pltpu.make_async_copy(src.at[pl.ds(tile * t + ci * chunk, chunk)], buf_ref.at[n, slot], sem_ref.at[n, slot])
                    for n, src in enumerate(hbm)]

        @pl.when(kk == nk - 1)
        def _():
            dg = dgp = None
            for cp in fetch(0, 0):
                cp.start()
            for ci in range(nchunk):
                slot = ci % 2
                if ci + 1 < nchunk:
                    for cp in fetch(ci + 1, 1 - slot):
                        cp.start()
                for cp in fetch(ci, slot):
                    cp.wait()
                rows = slice(ci * chunk, (ci + 1) * chunk)
                xhat, r = _rms_stats(buf_ref[0, slot])
                dxn, dg_part = _rms_bwd(acc_ref[rows, :], xhat, r, g_ref[...])
                dx = buf_ref[1, slot] + dxn
                dx_ref[rows, :] = dx
                dg = dg_part if dg is None else dg + dg_part
                if post is not None:
                    ohat, ro = _rms_stats(buf_ref[2, slot])
                    do, dgp_part = _rms_bwd(dx, ohat, ro, ins[3][...])
                    outs[2][rows, :] = do.astype(BF16)
                    dgp = dgp_part if dgp is None else dgp + dgp_part
            _acc_rows(dg_ref, first, dg)
            if post is not None:
                _acc_rows(outs[3], first, dgp)

        host.after(step, nt * nk)

    res = pl.pallas_call(
        body, name=name, grid=(nt, nk), in_specs=host.in_specs, out_specs=host.out_specs, out_shape=host.out_shape,
        scratch_shapes=host.scratch, input_output_aliases=host.aliases,
        compiler_params=_cp("arbitrary", "arbitrary"))(*args, *host.args)
    return host.results(res)


def _half_add(g, r1, c_arr, name, after=None):
    nsh, rows, ns = g.shape
    h = rows // 2
    tr = min(ROW_TILE, h)
    per = h // tr

    def body(c_ref, g_ref, r_ref, *rest):
        rest[-1][...] = (g_ref[...].astype(F32) + r_ref[...].astype(F32)).astype(BF16)

    spec = pl.BlockSpec((None, tr, ns), lambda s, r, c: (s, r, 0))
    ordering = [] if after is None else [after]
    return pl.pallas_call(
        body, name=name,
        grid_spec=pltpu.PrefetchScalarGridSpec(
            num_scalar_prefetch=1, grid=(nsh, per),
            in_specs=[pl.BlockSpec((None, tr, ns), lambda s, r, c: (s, c[0] * per + r, 0)), spec] + [ANY] * len(ordering),
            out_specs=spec),
        out_shape=jax.ShapeDtypeStruct((nsh, h, ns), BF16), compiler_params=_cp("parallel", "parallel"))(c_arr, g, r1, *ordering)


def _sum_chips(hh, r2, mc_arr, name, after=None):
    _, h, ns = hh.shape
    tr = min(ROW_TILE, h)
    per = h // tr

    def body(mc_ref, h_ref, a_ref, b_ref, c_ref, *rest):
        rest[-1][...] = ((h_ref[...].astype(F32) + a_ref[...].astype(F32)) + b_ref[...].astype(F32)) + c_ref[...].astype(F32)

    got = lambda k: pl.BlockSpec((None, tr, ns), lambda r, mc: (k, r, 0))
    ordering = [] if after is None else [after]
    return pl.pallas_call(
        body, name=name,
        grid_spec=pltpu.PrefetchScalarGridSpec(
            num_scalar_prefetch=1, grid=(per,),
            in_specs=[pl.BlockSpec((None, tr, ns), lambda r, mc: (mc[0], r, 0)), got(0), got(1), got(2)] + [ANY] * len(ordering),
            out_specs=pl.BlockSpec((tr, ns), lambda r, mc: (mc[1] * per + r, 0))),
        out_shape=jax.ShapeDtypeStruct((2 * h, ns), F32), compiler_params=_cp("parallel"))(mc_arr, hh, r2, r2, r2, *ordering)


def _add2(a, b, name):
    def body(a_ref, b_ref, o_ref):
        o_ref[...] = a_ref[...] + b_ref[...]

    return pl.pallas_call(body, name=name, out_shape=jax.ShapeDtypeStruct(a.shape, a.dtype), compiler_params=_cp())(a, b)


def _sum_chips_ordered(s2, r2, mc_arr, name):
    rows, w = s2.shape
    rh = rows // 2

    def body(mc_ref, s_ref, a_ref, b_ref, c_ref, o_ref):
        me = mc_ref[0]
        acc = None
        for j in range(N_CHIPS):
            rel = jnp.bitwise_xor(me, j)
            v = jnp.where(rel == 0, s_ref[...], jnp.where(rel == 2, a_ref[...], jnp.where(rel == 1, b_ref[...], c_ref[...])))
            acc = v if acc is None else acc + v
        o_ref[...] = acc

    got = lambda k: pl.BlockSpec((None, rh, w), lambda i, mc: (k, 0, 0))
    return pl.pallas_call(
        body, name=name,
        grid_spec=pltpu.PrefetchScalarGridSpec(
            num_scalar_prefetch=1, grid=(1,),
            in_specs=[pl.BlockSpec((rh, w), lambda i, mc: (mc[1], 0)), got(0), got(1), got(2)],
            out_specs=pl.BlockSpec((rh, w), lambda i, mc: (mc[1], 0))),
        out_shape=jax.ShapeDtypeStruct((rows, w), F32), compiler_params=_cp("arbitrary"))(mc_arr, s2, r2, r2, r2)


def _adamw(w, g, m, v, name, comm=None):
    r, c = w.shape
    tr = ROW_TILE if r % ROW_TILE == 0 else r
    c1 = 1.0 / (1.0 - ADAM_B1 ** ADAM_STEP)
    c2 = 1.0 / (1.0 - ADAM_B2 ** ADAM_STEP)
    spec = pl.BlockSpec((tr, c), lambda i: (i, 0))
    host = _Host(comm, [spec] * 4, [spec] * 4, [jax.ShapeDtypeStruct((r, c), F32)] * 4, [])

    def body(*refs):
        (w_ref, g_ref, m_ref, v_ref), (go_ref, d_ref, nm_ref, nv_ref), _ = host.split(refs)
        step = pl.program_id(0)
        host.before(step, r // tr)
        gv = g_ref[...]
        go_ref[...] = gv
        nm = ADAM_B1 * m_ref[...] + (1.0 - ADAM_B1) * gv
        nv = ADAM_B2 * v_ref[...] + (1.0 - ADAM_B2) * (gv * gv)
        nm_ref[...] = nm
        nv_ref[...] = nv
        d_ref[...] = -ADAM_LR * ((nm * c1) / (jnp.sqrt(nv * c2) + ADAM_EPS) + ADAM_WD * w_ref[...])
        host.after(step, r // tr)

    outs = pl.pallas_call(
        body, name=name, grid=(r // tr,), in_specs=host.in_specs, out_specs=host.out_specs, out_shape=host.out_shape,
        scratch_shapes=host.scratch, input_output_aliases=host.aliases,
        compiler_params=_cp("arbitrary"))(w, g, m, v, *host.args)
    return host.results(outs)


def _swap_with_sibling(grads, wholes, name):
    n, nw = len(grads), len(wholes)
    halves = [g.shape[1] // 2 for g in grads]

    def body(*refs):
        srcs, dsts = refs[:n + nw], refs[n + nw:2 * (n + nw)]
        ssem, rsem = refs[2 * (n + nw):]
        x, y, c, me, chips, sib = _place()
        cps = [_rcopy(srcs[a].at[:, pl.ds((1 - c) * halves[a], halves[a]), :], dsts[a], ssem.at[a], rsem.at[a], sib)
               for a in range(n)]
        cps += [_rcopy(srcs[a], dsts[a], ssem.at[a], rsem.at[a], sib) for a in range(n, n + nw)]
        for cp in cps:
            cp.start()
        for cp in cps:
            cp.wait_recv()
        for cp in cps:
            cp.wait_send()

    out_shape = [jax.ShapeDtypeStruct((g.shape[0], h, g.shape[2]), g.dtype) for g, h in zip(grads, halves)]
    out_shape += [jax.ShapeDtypeStruct(w.shape, w.dtype) for w in wholes]
    return pl.pallas_call(
        body, name=name, in_specs=[ANY] * (n + nw), out_specs=[ANY] * (n + nw), out_shape=out_shape,
        scratch_shapes=[pltpu.SemaphoreType.DMA((n + nw,)), pltpu.SemaphoreType.DMA((n + nw,))],
        compiler_params=pltpu.CompilerParams(has_side_effects=True))(*grads, *wholes)


def _scatter_start(h, name):
    land = (3,) + h.shape[1:]

    def body(h_ref, land_ref, send_sems, recv_sems, h_thru, land_thru, token):
        x, y, c, me, chips, sib = _place()
        for k, chip in enumerate(chips):
            _rcopy(h_ref.at[2 * chip[0] + chip[1]], land_ref.at[k], send_sems.at[k], recv_sems.at[k], (*chip, c)).start()
        token[...] = jnp.zeros_like(token)

    hbm = pl.BlockSpec(memory_space=pltpu.HBM)
    sem = pl.BlockSpec(memory_space=pltpu.SEMAPHORE)
    return pl.pallas_call(
        body, name=name,
        out_shape=(pltpu.SemaphoreType.DMA((3,)), pltpu.SemaphoreType.DMA((3,)), pltpu.HBM(h.shape, h.dtype),
                   pltpu.HBM(land, h.dtype), jax.ShapeDtypeStruct((8, LANES), F32)),
        in_specs=(hbm, hbm), out_specs=(sem, sem, hbm, hbm, pl.BlockSpec(memory_space=pltpu.VMEM)),
        input_output_aliases={0: 2, 1: 3},
        compiler_params=pltpu.CompilerParams(has_side_effects=pltpu.SideEffectType.DATAFLOW_SIDE_EFFECTING))(
            pltpu.with_memory_space_constraint(h, pltpu.HBM),
            pltpu.with_memory_space_constraint(lax.empty(land, h.dtype), pltpu.HBM))


def _scatter_wait(send_sems, recv_sems, h_thru, land_thru, after, name):
    def body(h_ref, land_ref, send_sems, recv_sems, after_ref, h_dead, got_ref):
        x, y, c, me, chips, sib = _place()
        for k, chip in enumerate(chips):
            cp = _rcopy(h_ref.at[2 * chip[0] + chip[1]], land_ref.at[k], send_sems.at[k], recv_sems.at[k], (*chip, c))
            cp.wait_send()
            cp.wait_recv()

    hbm = pl.BlockSpec(memory_space=pltpu.HBM)
    sem = pl.BlockSpec(memory_space=pltpu.SEMAPHORE)
    return pl.pallas_call(
        body, name=name,
        out_shape=(pltpu.HBM(h_thru.shape, h_thru.dtype), pltpu.HBM(land_thru.shape, land_thru.dtype)),
        in_specs=(hbm, hbm, sem, sem, ANY), out_specs=(hbm, hbm), input_output_aliases={0: 0, 1: 1},
        compiler_params=pltpu.CompilerParams(has_side_effects=pltpu.SideEffectType.DATAFLOW_SIDE_EFFECTING))(
            h_thru, land_thru, send_sems, recv_sems, after)


def _swap_start(g, name):
    h = g.shape[1] // 2
    land = (g.shape[0], h, g.shape[2])

    def body(g_ref, land_ref, send_sem, recv_sem, g_thru, land_thru, token):
        x, y, c, me, chips, sib = _place()
        _rcopy(g_ref.at[:, pl.ds((1 - c) * h, h), :], land_ref, send_sem.at[0], recv_sem.at[0], sib).start()
        token[...] = jnp.zeros_like(token)

    hbm = pl.BlockSpec(memory_space=pltpu.HBM)
    sem = pl.BlockSpec(memory_space=pltpu.SEMAPHORE)
    return pl.pallas_call(
        body, name=name,
        out_shape=(pltpu.SemaphoreType.DMA((1,)), pltpu.SemaphoreType.DMA((1,)), pltpu.HBM(g.shape, g.dtype),
                   pltpu.HBM(land, g.dtype), jax.ShapeDtypeStruct((8, LANES), F32)),
        in_specs=(hbm, hbm), out_specs=(sem, sem, hbm, hbm, pl.BlockSpec(memory_space=pltpu.VMEM)),
        input_output_aliases={0: 2, 1: 3},
        compiler_params=pltpu.CompilerParams(has_side_effects=pltpu.SideEffectType.DATAFLOW_SIDE_EFFECTING))(
            pltpu.with_memory_space_constraint(g, pltpu.HBM),
            pltpu.with_memory_space_constraint(lax.empty(land, g.dtype), pltpu.HBM))


def _swap_wait(send_sem, recv_sem, g_thru, land_thru, after, name):
    h = g_thru.shape[1] // 2

    def body(g_ref, land_ref, send_sem, recv_sem, after_ref, g_dead, got_ref):
        x, y, c, me, chips, sib = _place()
        cp = _rcopy(g_ref.at[:, pl.ds((1 - c) * h, h), :], land_ref, send_sem.at[0], recv_sem.at[0], sib)
        cp.wait_send()
        cp.wait_recv()

    hbm = pl.BlockSpec(memory_space=pltpu.HBM)
    sem = pl.BlockSpec(memory_space=pltpu.SEMAPHORE)
    return pl.pallas_call(
        body, name=name,
        out_shape=(pltpu.HBM(g_thru.shape, g_thru.dtype), pltpu.HBM(land_thru.shape, land_thru.dtype)),
        in_specs=(hbm, hbm, sem, sem, ANY), out_specs=(hbm, hbm), input_output_aliases={0: 0, 1: 1},
        compiler_params=pltpu.CompilerParams(has_side_effects=pltpu.SideEffectType.DATAFLOW_SIDE_EFFECTING))(
            g_thru, land_thru, send_sem, recv_sem, after)


def _share_half_start(small, name):
    rh = small.shape[0] // 2
    land = (3, rh, small.shape[1])

    def body(s_ref, land_ref, send_sems, recv_sems, s_thru, land_thru, token):
        x, y, c, me, chips, sib = _place()
        for k, chip in enumerate(chips):
            _rcopy(s_ref.at[pl.ds(c * rh, rh)], land_ref.at[k], send_sems.at[k], recv_sems.at[k], (*chip, c)).start()
        token[...] = jnp.zeros_like(token)

    hbm = pl.BlockSpec(memory_space=pltpu.HBM)
    sem = pl.BlockSpec(memory_space=pltpu.SEMAPHORE)
    return pl.pallas_call(
        body, name=name,
        out_shape=(pltpu.SemaphoreType.DMA((3,)), pltpu.SemaphoreType.DMA((3,)), pltpu.HBM(small.shape, small.dtype),
                   pltpu.HBM(land, small.dtype), jax.ShapeDtypeStruct((8, LANES), F32)),
        in_specs=(hbm, hbm), out_specs=(sem, sem, hbm, hbm, pl.BlockSpec(memory_space=pltpu.VMEM)),
        input_output_aliases={0: 2, 1: 3},
        compiler_params=pltpu.CompilerParams(has_side_effects=pltpu.SideEffectType.DATAFLOW_SIDE_EFFECTING))(
            pltpu.with_memory_space_constraint(small, pltpu.HBM),
            pltpu.with_memory_space_constraint(lax.empty(land, small.dtype), pltpu.HBM))


def _share_half_wait(send_sems, recv_sems, s_thru, land_thru, after, name):
    rh = s_thru.shape[0] // 2

    def body(s_ref, land_ref, send_sems, recv_sems, after_ref, s_dead, got_ref):
        x, y, c, me, chips, sib = _place()
        for k, chip in enumerate(chips):
            cp = _rcopy(s_ref.at[pl.ds(c * rh, rh)], land_ref.at[k], send_sems.at[k], recv_sems.at[k], (*chip, c))
            cp.wait_send()
            cp.wait_recv()

    hbm = pl.BlockSpec(memory_space=pltpu.HBM)
    sem = pl.BlockSpec(memory_space=pltpu.SEMAPHORE)
    return pl.pallas_call(
        body, name=name,
        out_shape=(pltpu.HBM(s_thru.shape, s_thru.dtype), pltpu.HBM(land_thru.shape, land_thru.dtype)),
        in_specs=(hbm, hbm, sem, sem, ANY), out_specs=(hbm, hbm), input_output_aliases={0: 0, 1: 1},
        compiler_params=pltpu.CompilerParams(has_side_effects=pltpu.SideEffectType.DATAFLOW_SIDE_EFFECTING))(
            s_thru, land_thru, send_sems, recv_sems, after)


def _join_start(parts, name):
    n = len(parts)

    def body(*refs):
        srcs, (send_sems, recv_sems), token = refs[:n], refs[n:n + 2], refs[-1]
        x, y, c, me, chips, sib = _place()
        for a, src in enumerate(srcs):
            h = src.shape[0] // 2
            mine = src.at[pl.ds(c * h, h)]
            _rcopy(mine, mine, send_sems.at[a], recv_sems.at[a], sib).start()
        token[...] = jnp.zeros_like(token)

    hbm = pl.BlockSpec(memory_space=pltpu.HBM)
    sem = pl.BlockSpec(memory_space=pltpu.SEMAPHORE)
    outs = pl.pallas_call(
        body, name=name,
        out_shape=(pltpu.SemaphoreType.DMA((n,)), pltpu.SemaphoreType.DMA((n,)))
        + tuple(pltpu.HBM(p.shape, p.dtype) for p in parts) + (jax.ShapeDtypeStruct((8, LANES), F32),),
        in_specs=(hbm,) * n, out_specs=(sem, sem) + (hbm,) * n + (pl.BlockSpec(memory_space=pltpu.VMEM),),
        input_output_aliases={a: 2 + a for a in range(n)},
        compiler_params=pltpu.CompilerParams(has_side_effects=pltpu.SideEffectType.DATAFLOW_SIDE_EFFECTING))(
            *[pltpu.with_memory_space_constraint(p, pltpu.HBM) for p in parts])
    return outs[0], outs[1], list(outs[2:2 + n]), outs[-1]


def _join_wait(send_sems, recv_sems, parts, after, name):
    n = len(parts)

    def body(*refs):
        srcs, (send_sems, recv_sems) = refs[:n], refs[n:n + 2]
        x, y, c, me, chips, sib = _place()
        for a, src in enumerate(srcs):
            h = src.shape[0] // 2
            mine, theirs = src.at[pl.ds(c * h, h)], src.at[pl.ds((1 - c) * h, h)]
            _rcopy(mine, theirs, send_sems.at[a], recv_sems.at[a], sib).wait_send()
            _rcopy(theirs, theirs, send_sems.at[a], recv_sems.at[a], sib).wait_recv()

    hbm = pl.BlockSpec(memory_space=pltpu.HBM)
    sem = pl.BlockSpec(memory_space=pltpu.SEMAPHORE)
    return pl.pallas_call(
        body, name=name, out_shape=tuple(pltpu.HBM(p.shape, p.dtype) for p in parts),
        in_specs=(hbm,) * n + (sem, sem, ANY), out_specs=(hbm,) * n, input_output_aliases={a: a for a in range(n)},
        compiler_params=pltpu.CompilerParams(has_side_effects=pltpu.SideEffectType.DATAFLOW_SIDE_EFFECTING))(
            *parts, send_sems, recv_sems, after)


def _pad_rows(a, rows):
    return jnp.pad(a, ((0, rows - a.shape[0]), (0, 0)))


def _stack_rows(parts, multiple):
    padded = [_pad_rows(p, -(-p.shape[0] // 8) * 8) for p in parts]
    starts, at = [], 0
    for p in padded:
        starts.append(at)
        at += p.shape[0]
    total = -(-at // multiple) * multiple
    if total > at:
        padded.append(jnp.zeros((total - at, parts[0].shape[1]), parts[0].dtype))
    return jnp.concatenate(padded, axis=0), starts


def kernel(x, ln_pre_even, w_in_even, pool_w, pool_scale, w_out_even, ln_post_even, ln_pre_odd, w_in_odd, sconv_w, dconv_w, dconv_b, cnorm_g, cnorm_b, w_out_odd, ln_post_odd, loss_target, m_ln_pre_even, m_w_in_even, m_pool_w, m_pool_scale, m_w_out_even, m_ln_post_even, m_ln_pre_odd, m_w_in_odd, m_sconv_w, m_dconv_w, m_dconv_b, m_cnorm_g, m_cnorm_b, m_w_out_odd, m_ln_post_odd, v_ln_pre_even, v_w_in_even, v_pool_w, v_pool_scale, v_w_out_even, v_ln_post_even, v_ln_pre_odd, v_w_in_odd, v_sconv_w, v_dconv_w, v_dconv_b, v_cnorm_g, v_cnorm_b, v_w_out_odd, v_ln_post_odd):
    _, s, d = x.shape
    half = d // 2
    cw = half // N_CHIPS
    ng, q, gd = pool_w.shape[1:]
    k3, k31 = sconv_w.shape[1], dconv_w.shape[1]
    x2d, tgt = x[0], loss_target[0]
    me = 2 * lax.axis_index("x") + lax.axis_index("y")
    core = lax.axis_index("c")
    c_arr = jnp.reshape(core, (1,)).astype(jnp.int32)
    me_arr = jnp.reshape(me, (1,)).astype(jnp.int32)
    mc_arr = jnp.stack([me, core]).astype(jnp.int32)

    shards = [w_in_even[0], w_out_even[0], w_in_odd[0], w_out_odd[0]]
    pool_w_b = _cast_bf16(pool_w[0].reshape(ng * q, gd), "cast_pool_w").reshape(ng, q, gd)
    pack_w, at_w = _stack_rows([sconv_w[0], dconv_w[0], dconv_b, cnorm_g, cnorm_b], 8)
    pack_d, at_d = _stack_rows([ln_pre_odd, ln_post_odd], 8)
    placed = [lax.dynamic_update_slice(jnp.zeros((ng, N_CHIPS * q, gd), BF16), pool_w_b, (0, me * q, 0)),
              lax.dynamic_update_slice(jnp.zeros((pack_w.shape[0], N_CHIPS * cw), F32), pack_w, (0, me * cw)),
              lax.dynamic_update_slice(jnp.zeros((pack_d.shape[0], d), F32), pack_d, (0, me * (d // N_CHIPS)))]
    plans = _Multi([_GatherPieces([_cast_bf16_own_slab(shards[0], me_arr, "cast_w0")], GATHER_PIECES, (0.3, 0.9)),
                    _SmallGatherPlan(placed, (q, cw, d // N_CHIPS))])
    h0, others, extra = _prep(x2d, ln_pre_even, shards[1:], me_arr, "prep_and_gather_first", plans)
    (win_e,), (pool_w_f, pack_w_f, pack_d_f) = plans.results(extra)
    slabs = [None] + others
    sconv_f = pack_w_f[at_w[0]:at_w[0] + k3]
    dconv_f = pack_w_f[at_w[1]:at_w[1] + k31]
    dconv_b_f, cnorm_g_f, cnorm_b_f = (pack_w_f[at_w[n]:at_w[n] + 1] for n in (2, 3, 4))
    ln_pre_odd_f = pack_d_f[at_d[0]:at_d[0] + 1]
    ln_post_odd_f = pack_d_f[at_d[1]:at_d[1] + 1]

    plans = _Multi([_GatherPlan([slabs[1]], at=(0.6, 0.88)), _GatherPlan([slabs[2]], (0, 1, 4), at=(0.6, 0.88))])
    p_e, extra = _mm_nn(h0, win_e, "proj_in_even", plans)
    (wout_e,), (win_o,) = plans.results(extra)
    wout_e = wout_e.reshape(d, d)
    att, ltot, (win_o,) = _sba_fwd(p_e, half, "sba_fwd", _GatherPlan([win_o], (1, 4, 4), at=(0.69, 0.94)))
    y_e = _even_mix_fwd(p_e, att, pool_w_f, pool_scale, d, "even_mix_fwd")
    o_e, x1, h1 = _mm_out_even(y_e, wout_e, x2d, ln_post_even, ln_pre_odd_f, "proj_out_even")
    p_o, (wout_o,) = _mm_nn(h1, win_o, "proj_in_odd", _GatherPlan([slabs[3]]))
    wout_o = wout_o.reshape(d, d)
    y_o, s3, d1 = _odd_mix_fwd(p_o, sconv_f, dconv_f, dconv_b_f, cnorm_g_f, cnorm_b_f, d, "odd_mix_fwd")
    do_o, dx2, loss_blk, dln_post_odd = _mm_out_odd(y_o, wout_o, x1, ln_post_odd_f, tgt, "proj_out_odd_loss")

    dy_o = _mm_nt(do_o, wout_o, "dy_odd")
    g_wout_o = _mm_tn(y_o, do_o, 1, "dw_out_odd")[0].reshape(N_CHIPS, d // N_CHIPS, d)
    (dbc, dgate_o, ds3, dd1, dcnorm_g, dcnorm_b, ddconv_b), (got,) = _odd_bwd_rows(
        p_o, s3, d1, dy_o, cnorm_g_f, cnorm_b_f, d, "odd_bwd_rows", _SwapPlan([g_wout_o]))
    h_wout_o = _half_add(g_wout_o, got, c_arr, "half_add_out_odd")
    dhc, dcc, dga, dgb, dsconv, ddconv = _odd_bwd_conv(p_o, ds3, dd1, sconv_f, dconv_f, d, "odd_bwd_conv")
    dp_o = jnp.concatenate([dhc, dbc, dcc, dga, dgb, dgate_o], axis=1)
    g_win_o, (s_wout_o,) = _mm_tn(h1, dp_o, N_CHIPS, "dw_in_odd", _ScatterPlan([h_wout_o]))
    (dx1, dln_pre_odd, do_e, dln_post_even), (got,) = _mm_in_bwd(
        dp_o, win_o, x1, ln_pre_odd_f, dx2, (o_e, ln_post_even), "dx_odd", _SwapPlan([g_win_o]))
    h_win_o = _half_add(g_win_o, got, c_arr, "half_add_in_odd")

    dy_e = _mm_nt(do_e, wout_e, "dy_even")
    g_wout_e = _mm_tn(y_e, do_e, 1, "dw_out_even")[0].reshape(N_CHIPS, d // N_CHIPS, d)
    (datt, du, dgate_e, dpool_scale, dpool_w), (got,) = _even_mix_bwd(
        p_e, att, dy_e, pool_w_f, pool_scale, d, "even_mix_bwd", _SwapPlan([g_wout_e]))
    h_wout_e = _half_add(g_wout_e, got, c_arr, "half_add_out_even")
    two = lambda v: v.reshape(2, half)
    small_parts = [dpool_scale, two(dln_post_even), two(dln_pre_odd), two(dln_post_odd),
                   dsconv, ddconv, ddconv_b, dcnorm_g, dcnorm_b, dpool_w.reshape(gd, half)]
    small, at_s = _stack_rows(small_parts, 16)
    plans = _Multi([_ScatterPlan([h_win_o, h_wout_e]), _SendWholePlan([small])])
    dq, dk, dv, extra = _sba_bwd(p_e, ltot, datt, half, "sba_bwd", plans)
    (s_win_o, s_wout_e), (small1,) = plans.results(extra)
    small2 = _add2(small, small1, "small_add")
    dp_e = jnp.concatenate([dq, dk, dv, du, dgate_e], axis=1)
    g_win_e, (small_got,) = _mm_tn(h0, dp_e, N_CHIPS, "dw_in_even", _ShareHalfPlan([small2]))
    swap = _swap_start(g_win_e, "swap_in_even_start")
    pairs = [(h_wout_e, s_wout_e), (h_win_o, s_win_o), (h_wout_o, s_wout_o)]
    parts = []
    for n, (h, r) in enumerate(pairs):
        parts.append(_sum_chips(h, r, mc_arr, f"sum_chips{n + 1}", after=parts[-1] if parts else swap[4]))
    g_win_e, got = _swap_wait(*swap[:4], parts[-1], "swap_in_even_wait")
    parts.append(_sum_chips_ordered(small2, small_got, mc_arr, "small_sum"))
    join_sems = _join_start(parts, "join_first_start")
    h_win_e = _half_add(g_win_e, got, c_arr, "half_add_in_even", after=join_sems[3])
    send_sems, recv_sems, h_win_e, landing, token = _scatter_start(h_win_e, "scatter_in_even_start")
    (grad_x, dln_pre_even), _ = _mm_in_bwd(dp_e, win_e, x2d, ln_pre_even + token[0:1, 0:1], dx1, None, "dx_even")

    last, at_l = _stack_rows([two(dln_pre_even), jnp.pad(loss_blk[0:1], ((0, 0), (0, half - LANES)))], 16)
    (last1,) = _swap_with_sibling([], [last], "swap_last")
    last2 = _add2(last, last1, "last_add")
    share = _share_half_start(last2, "share_last_start")
    gw_out_e, gw_in_o, gw_out_o, red = _join_wait(*join_sems[:3], share[4], "join_first_wait")

    def rows(n, cnt):
        return red[at_s[n]:at_s[n] + cnt]

    def mine(a, width):
        return lax.dynamic_slice_in_dim(a, me * width, width, axis=1)

    quarter = d // N_CHIPS
    g_small = {
        "pool_scale": rows(0, 1),
        "ln_post_even": rows(1, 2).reshape(1, d),
        "ln_pre_odd": mine(rows(2, 2).reshape(1, d), quarter),
        "ln_post_odd": mine(rows(3, 2).reshape(1, d), quarter),
        "sconv_w": mine(rows(4, k3), cw),
        "dconv_w": mine(rows(5, k31), cw),
        "dconv_b": mine(rows(6, 1), cw),
        "cnorm_g": mine(rows(7, 1), cw),
        "cnorm_b": mine(rows(8, 1), cw),
        "pool_w": lax.dynamic_slice_in_dim(rows(9, gd).reshape(ng, gd, gd), me * q, q, axis=1).reshape(ng * q, gd),
    }
    w2d = {
        "ln_pre_even": ln_pre_even, "w_in_even": w_in_even[0], "pool_w": pool_w[0].reshape(ng * q, gd),
        "pool_scale": pool_scale, "w_out_even": w_out_even[0], "ln_post_even": ln_post_even, "ln_pre_odd": ln_pre_odd,
        "w_in_odd": w_in_odd[0], "sconv_w": sconv_w[0], "dconv_w": dconv_w[0], "dconv_b": dconv_b, "cnorm_g": cnorm_g,
        "cnorm_b": cnorm_b, "w_out_odd": w_out_odd[0], "ln_post_odd": ln_post_odd,
    }
    moments = {
        "ln_pre_even": (m_ln_pre_even, v_ln_pre_even), "w_in_even": (m_w_in_even, v_w_in_even),
        "pool_w": (m_pool_w, v_pool_w), "pool_scale": (m_pool_scale, v_pool_scale),
        "w_out_even": (m_w_out_even, v_w_out_even), "ln_post_even": (m_ln_post_even, v_ln_post_even),
        "ln_pre_odd": (m_ln_pre_odd, v_ln_pre_odd), "w_in_odd": (m_w_in_odd, v_w_in_odd),
        "sconv_w": (m_sconv_w, v_sconv_w), "dconv_w": (m_dconv_w, v_dconv_w), "dconv_b": (m_dconv_b, v_dconv_b),
        "cnorm_g": (m_cnorm_g, v_cnorm_g), "cnorm_b": (m_cnorm_b, v_cnorm_b),
        "w_out_odd": (m_w_out_odd, v_w_out_odd), "ln_post_odd": (m_ln_post_odd, v_ln_post_odd),
    }
    def update(name, g):
        m_in, v_in = moments[name]
        w = w2d[name]
        return _adamw(w, g, m_in.reshape(w.shape), v_in.reshape(w.shape), "adamw_" + name)[0]

    updates = {name: update(name, g) for name, g in (("w_in_odd", gw_in_o), ("w_out_even", gw_out_e), ("w_out_odd", gw_out_o))}
    last2, last_got = _share_half_wait(*share[:4], updates["w_out_odd"][1], "share_last_wait")
    last_sum = _sum_chips_ordered(last2, last_got, mc_arr, "last_sum")
    h_win_e, s_win_e = _scatter_wait(send_sems, recv_sems, h_win_e, landing, last_sum, "scatter_in_even_wait")
    last_sems = _join_start([_sum_chips(h_win_e, s_win_e, mc_arr, "sum_chips0"), last_sum], "join_last_start")
    for name, g in g_small.items():
        updates[name] = update(name, g)
    gw_in_e, red_last = _join_wait(*last_sems[:3], updates["pool_w"][1], "join_last_wait")
    loss = red_last[at_l[1], 0]
    updates["ln_pre_even"] = update("ln_pre_even", red_last[at_l[0]:at_l[0] + 2].reshape(1, d))
    updates["w_in_even"] = update("w_in_even", gw_in_e)
    outs = [[u.reshape(moments[name][0].shape) for u in updates[name]] for name in w2d]
    grads_out, deltas, new_m, new_v = zip(*outs)
    return (loss, grad_x.reshape(x.shape), *grads_out, *deltas, *new_m, *new_v)
```

```python
import functools
import math

import jax
import jax.numpy as jnp
from jax import lax
from jax.experimental import pallas as pl
from jax.experimental.pallas import tpu as pltpu

F32 = jnp.float32
BF16 = jnp.bfloat16
EPS = 1e-6
N_CHIPS = 4
VMEM_LIMIT_V7X = 56 << 20
HEAD_DIM = 128
ATT_BLOCK = 256
POOL_WINDOWS = (2, 4, 8, 16)
ROW_TILE = 256
POOL_HALO = 16
CONV_HALO = 32
LANES = 128
ADAM_LR, ADAM_B1, ADAM_B2, ADAM_EPS, ADAM_WD, ADAM_STEP = 0.001, 0.9, 0.999, 1e-08, 0.01, 10
MESH_ID = pl.DeviceIdType.MESH
ANY = pl.BlockSpec(memory_space=pl.ANY)


def _cp(*sem):
    return pltpu.CompilerParams(dimension_semantics=sem or None, vmem_limit_bytes=VMEM_LIMIT_V7X)


def _pick_tile(n, cap):
    best = None
    for t in range(LANES, min(n, cap) + 1, LANES):
        if n % t == 0:
            best = t
    assert best is not None, (n, cap)
    return best


def _sigmoid(x):
    return 1.0 / (1.0 + jnp.exp(-x))


def _silu(x):
    return x * _sigmoid(x)


def _dsilu(x):
    s = _sigmoid(x)
    return s * (1.0 + x * (1.0 - s))


def _log_sigmoid(z):
    return jnp.minimum(z, 0.0) - jnp.log(1.0 + jnp.exp(-jnp.abs(z)))


def _rms_stats(x):
    r = lax.rsqrt(jnp.mean(x * x, axis=-1, keepdims=True) + EPS)
    return x * r, r


def _rms_bwd(dh, xhat, r, g):
    dxh = dh * g
    dx = r * (dxh - xhat * jnp.mean(dxh * xhat, axis=-1, keepdims=True))
    return dx, jnp.sum(dh * xhat, axis=0, keepdims=True)


def _acc_rows(ref, first, val):
    @pl.when(first)
    def _():
        ref[...] = val

    @pl.when(jnp.logical_not(first))
    def _():
        ref[...] += val


def _rcopy(src, dst, ssem, rsem, dev):
    return pltpu.make_async_remote_copy(src_ref=src, dst_ref=dst, send_sem=ssem, recv_sem=rsem,
                                        device_id=dev, device_id_type=MESH_ID)


def _place():
    x, y, c = lax.axis_index("x"), lax.axis_index("y"), lax.axis_index("c")
    chips = [(1 - x, y), (x, 1 - y), (1 - x, 1 - y)]
    return x, y, c, 2 * x + y, chips, (x, y, 1 - c)


class _GatherPlan:
    PER_ARRAY = 7

    def __init__(self, arrays, part=(0, 1, 1), at=(0.5, 0.8)):
        self.operands = list(arrays)
        self.out_shapes = [jax.ShapeDtypeStruct(a.shape, a.dtype) for a in arrays]
        self.aliases = {i: i for i in range(len(arrays))}
        self.nsems = self.PER_ARRAY * len(arrays)
        self.base = 0
        self.halves = [a.shape[1] // 2 for a in arrays]
        self.part = part
        self.at = at

    def schedule(self):
        return [(0.0, self.start), (self.at[0], self.relay), (self.at[1], self.relay_far)]

    def _rows(self, ref, a, chip, half, quarter=None):
        lo, hi, n = self.part
        h = self.halves[a]
        first, size = half * h + lo * h // n, (hi - lo) * h // n
        if quarter is not None:
            first, size = first + quarter * (size // 2), size // 2
        return ref.at[chip, pl.ds(first, size)]

    def _copy(self, src, dst, a, n, ssem, rsem, dev):
        return _rcopy(src, dst, ssem.at[self.base + self.PER_ARRAY * a + n], rsem.at[self.base + self.PER_ARRAY * a + n], dev)

    def _own(self, ins, outs, ssem, rsem):
        x, y, c, me, chips, sib = _place()
        return [self._copy(self._rows(ins[a], a, me, c), self._rows(outs[a], a, me, c), a, k, ssem, rsem, (*chips[k], c))
                for a in range(len(ins)) for k in (0, 1)]

    def _relays(self, outs, ssem, rsem, a, k):
        x, y, c, me, chips, sib = _place()
        chip = 2 * chips[k][0] + chips[k][1]
        whole, quarter = self._rows(outs[a], a, chip, c), self._rows(outs[a], a, chip, c, k)
        return (self._copy(whole, whole, a, k, ssem, rsem, (*chips[k], c)),
                self._copy(quarter, quarter, a, 2 + k, ssem, rsem, (*chips[1 - k], c)),
                self._copy(whole, whole, a, 4 + k, ssem, rsem, sib))

    def _far(self, outs, ssem, rsem, a):
        x, y, c, me, chips, sib = _place()
        chip = 2 * chips[2][0] + chips[2][1]
        whole = self._rows(outs[a], a, chip, c)
        got = [self._copy(q, q, a, 2 + k, ssem, rsem, (*chips[1 - k], c))
               for k, q in enumerate([self._rows(outs[a], a, chip, c, 0), self._rows(outs[a], a, chip, c, 1)])]
        return got, self._copy(whole, whole, a, 6, ssem, rsem, sib)

    def start(self, ins, outs, ssem, rsem):
        for cp in self._own(ins, outs, ssem, rsem):
            cp.start()

    def relay(self, ins, outs, ssem, rsem):
        for a in range(len(outs)):
            for k in (0, 1):
                landed, onward, to_sibling = self._relays(outs, ssem, rsem, a, k)
                landed.wait_recv()
                onward.start()
                to_sibling.start()

    def relay_far(self, ins, outs, ssem, rsem):
        for a in range(len(outs)):
            got, to_sibling = self._far(outs, ssem, rsem, a)
            for cp in got:
                cp.wait_recv()
            to_sibling.start()

    def finish(self, ins, outs, ssem, rsem):
        x, y, c, me, chips, sib = _place()
        for a in range(len(outs)):
            for k in range(3):
                ref = self._rows(outs[a], a, 2 * chips[k][0] + chips[k][1], 1 - c)
                self._copy(ref, ref, a, 4 + k, ssem, rsem, sib).wait_recv()
        for cp in self._own(ins, outs, ssem, rsem):
            cp.wait_send()
        for a in range(len(outs)):
            for k in (0, 1):
                _, onward, to_sibling = self._relays(outs, ssem, rsem, a, k)
                onward.wait_send()
                to_sibling.wait_send()
            self._far(outs, ssem, rsem, a)[1].wait_send()


class _ScatterPlan:
    def __init__(self, arrays, part=(0, 1, 1), into=None):
        self.n = len(arrays)
        self.operands = list(arrays) + list(into or [])
        self.out_shapes = [jax.ShapeDtypeStruct((3,) + a.shape[1:], a.dtype) for a in arrays]
        self.aliases = {self.n + i: i for i in range(self.n)} if into else {}
        self.nsems = 3 * self.n
        self.base = 0
        self.part = part

    def _copies(self, ins, outs, ssem, rsem):
        x, y, c, me, chips, sib = _place()
        lo, hi, n = self.part
        out = []
        for a in range(self.n):
            h = ins[a].shape[1]
            rows = pl.ds(lo * h // n, (hi - lo) * h // n)
            for k, chip in enumerate(chips):
                out.append(_rcopy(ins[a].at[2 * chip[0] + chip[1], rows], outs[a].at[k, rows],
                                  ssem.at[self.base + 3 * a + k], rsem.at[self.base + 3 * a + k], (*chip, c)))
        return out

    def schedule(self):
        return [(0.0, self.start)]

    def start(self, ins, outs, ssem, rsem):
        for cp in self._copies(ins, outs, ssem, rsem):
            cp.start()

    def finish(self, ins, outs, ssem, rsem):
        cps = self._copies(ins, outs, ssem, rsem)
        for cp in cps:
            cp.wait_recv()
        for cp in cps:
            cp.wait_send()


class _ShareHalfPlan(_ScatterPlan):
    def __init__(self, arrays):
        super().__init__(arrays)
        self.out_shapes = [jax.ShapeDtypeStruct((3, a.shape[0] // 2, a.shape[1]), a.dtype) for a in arrays]

    def _copies(self, ins, outs, ssem, rsem):
        x, y, c, me, chips, sib = _place()
        out = []
        for a in range(self.n):
            rh = ins[a].shape[0] // 2
            for k, chip in enumerate(chips):
                out.append(_rcopy(ins[a].at[pl.ds(c * rh, rh)], outs[a].at[k],
                                  ssem.at[self.base + 3 * a + k], rsem.at[self.base + 3 * a + k], (*chip, c)))
        return out


class _SwapPlan:
    def __init__(self, grads):
        self.operands = list(grads)
        self.out_shapes = [jax.ShapeDtypeStruct((g.shape[0], g.shape[1] // 2, g.shape[2]), g.dtype) for g in grads]
        self.aliases = {}
        self.nsems = len(grads)
        self.base = 0

    def _copies(self, ins, outs, ssem, rsem):
        x, y, c, me, chips, sib = _place()
        out = []
        for a, src in enumerate(ins):
            h = src.shape[1] // 2
            out.append(_rcopy(src.at[:, pl.ds((1 - c) * h, h), :], outs[a], ssem.at[self.base + a], rsem.at[self.base + a], sib))
        return out

    def schedule(self):
        return [(0.0, self.start)]

    def start(self, ins, outs, ssem, rsem):
        for cp in self._copies(ins, outs, ssem, rsem):
            cp.start()

    def finish(self, ins, outs, ssem, rsem):
        cps = self._copies(ins, outs, ssem, rsem)
        for cp in cps:
            cp.wait_recv()
        for cp in cps:
            cp.wait_send()


class _SendWholePlan(_SwapPlan):
    def __init__(self, arrays):
        self.operands = list(arrays)
        self.out_shapes = [jax.ShapeDtypeStruct(a.shape, a.dtype) for a in arrays]
        self.aliases = {}
        self.nsems = len(arrays)
        self.base = 0

    def _copies(self, ins, outs, ssem, rsem):
        x, y, c, me, chips, sib = _place()
        return [_rcopy(src, outs[a], ssem.at[self.base + a], rsem.at[self.base + a], sib) for a, src in enumerate(ins)]


class _GatherPieces:
    def __init__(self, arrays, n, at):
        self.pieces = [_GatherPlan(arrays, (j, j + 1, n), at) for j in range(n)]
        self.operands, self.out_shapes, self.aliases = self.pieces[0].operands, self.pieces[0].out_shapes, self.pieces[0].aliases
        self.nsems = sum(p.nsems for p in self.pieces)
        self.at = at
        self.base = 0

    @property
    def base(self):
        return self.pieces[0].base

    @base.setter
    def base(self, value):
        for j, p in enumerate(self.pieces):
            p.base = value + j * p.nsems

    def schedule(self):
        return [(0.0, self.start), (self.at[0], self.relay), (self.at[1], self.relay_far)]

    def _each(self, what, *a):
        for p in self.pieces:
            getattr(p, what)(*a)

    def start(self, *a):
        self._each("start", *a)

    def relay(self, *a):
        self._each("relay", *a)

    def relay_far(self, *a):
        self._each("relay_far", *a)

    def finish(self, *a):
        self._each("finish", *a)


class _SmallGatherPlan:
    def __init__(self, arrays, widths):
        self.operands = list(arrays)
        self.out_shapes = [jax.ShapeDtypeStruct(a.shape, a.dtype) for a in arrays]
        self.aliases = {i: i for i in range(3)}
        self.nsems = 9
        self.base = 0
        self.widths = widths

    def _part(self, ref, n, chip):
        w = self.widths[n]
        return ref.at[:, pl.ds(chip * w, w), :] if n == 0 else ref.at[:, pl.ds(chip * w, w)]

    def _copies(self, ins, outs, ssem, rsem, own):
        x, y, c, me, chips, sib = _place()
        out = []
        for n in range(3):
            for k, chip in enumerate(chips):
                which = me if own else 2 * chip[0] + chip[1]
                out.append(_rcopy(self._part(ins[n], n, which), self._part(outs[n], n, which),
                                  ssem.at[self.base + 3 * n + k], rsem.at[self.base + 3 * n + k], (*chip, c)))
        return out

    def schedule(self):
        return [(0.0, self.start)]

    def start(self, ins, outs, ssem, rsem):
        for cp in self._copies(ins, outs, ssem, rsem, True):
            cp.start()

    def finish(self, ins, outs, ssem, rsem):
        for cp in self._copies(ins, outs, ssem, rsem, False):
            cp.wait_recv()
        for cp in self._copies(ins, outs, ssem, rsem, True):
            cp.wait_send()


class _Multi:
    def __init__(self, plans):
        self.plans = plans
        self.operands, self.out_shapes, self.aliases, self.nsems = [], [], {}, 0
        self.spans = []
        for p in plans:
            ni, no = len(self.operands), len(self.out_shapes)
            self.spans.append((ni, ni + len(p.operands), no, no + len(p.out_shapes)))
            self.aliases.update({ni + i: no + j for i, j in p.aliases.items()})
            p.base = self.nsems
            self.nsems += p.nsems
            self.operands += p.operands
            self.out_shapes += p.out_shapes

    def schedule(self):
        def bound(fn, span):
            i0, i1, o0, o1 = span
            return lambda ins, outs, ssem, rsem: fn(ins[i0:i1], outs[o0:o1], ssem, rsem)

        stages = [(at, bound(fn, span)) for p, span in zip(self.plans, self.spans) for at, fn in p.schedule()]
        return sorted(stages, key=lambda s: s[0])

    def finish(self, ins, outs, ssem, rsem):
        for p, (i0, i1, o0, o1) in zip(self.plans, self.spans):
            p.finish(ins[i0:i1], outs[o0:o1], ssem, rsem)

    def results(self, extra):
        return [list(extra[o0:o1]) for (_, _, o0, o1) in self.spans]


class _Host:
    def __init__(self, comm, in_specs, out_specs, out_shape, scratch, prefetch=0):
        self.comm = comm
        self.n_in, self.n_out = len(in_specs), len(out_specs)
        self.in_specs, self.out_specs, self.out_shape, self.scratch = list(in_specs), list(out_specs), list(out_shape), list(scratch)
        self.aliases = {}
        self.args = []
        if comm is not None:
            self.in_specs += [ANY] * len(comm.operands)
            self.out_specs += [ANY] * len(comm.out_shapes)
            self.out_shape += comm.out_shapes
            self.scratch += [pltpu.SemaphoreType.DMA((comm.nsems,)), pltpu.SemaphoreType.DMA((comm.nsems,))]
            self.aliases = {prefetch + self.n_in + i: self.n_out + j for i, j in comm.aliases.items()}
            self.args = list(comm.operands)

    def split(self, refs):
        nc = len(self.args)
        nco = len(self.out_shape) - self.n_out
        ins, p = refs[:self.n_in], self.n_in + nc
        outs, rest = refs[p:p + self.n_out], refs[p + self.n_out + nco:]
        self._cargs = None
        if self.comm is not None:
            self._cargs = (refs[self.n_in:p], refs[p + self.n_out:p + self.n_out + nco], rest[-2], rest[-1])
            rest = rest[:-2]
        return ins, outs, rest

    def before(self, step, total):
        if self.comm is None:
            return

        for at, stage in self.comm.schedule():
            pl.when(step == min(total - 1, int(at * total)))(functools.partial(stage, *self._cargs))

    def after(self, step, total):
        if self.comm is None:
            return

        @pl.when(step == total - 1)
        def _():
            self.comm.finish(*self._cargs)

    def results(self, outs):
        return outs[:self.n_out], outs[self.n_out:]


def _cast_bf16(x, name):
    r, c = x.shape
    tr = ROW_TILE if r % ROW_TILE == 0 else r

    def body(x_ref, o_ref):
        o_ref[...] = x_ref[...].astype(BF16)

    return pl.pallas_call(
        body, name=name, grid=(r // tr,),
        in_specs=[pl.BlockSpec((tr, c), lambda i: (i, 0))],
        out_specs=pl.BlockSpec((tr, c), lambda i: (i, 0)),
        out_shape=jax.ShapeDtypeStruct((r, c), BF16), compiler_params=_cp("parallel"))(x)


def _cast_bf16_own_slab(x, me_arr, name):
    r, c = x.shape
    tr = ROW_TILE if r % ROW_TILE == 0 else r

    def body(me_ref, x_ref, o_ref):
        o_ref[...] = x_ref[...].astype(BF16)

    return pl.pallas_call(
        body, name=name,
        grid_spec=pltpu.PrefetchScalarGridSpec(
            num_scalar_prefetch=1, grid=(r // tr,),
            in_specs=[pl.BlockSpec((tr, c), lambda i, me: (i, 0))],
            out_specs=pl.BlockSpec((None, tr, c), lambda i, me: (me[0], i, 0))),
        out_shape=jax.ShapeDtypeStruct((N_CHIPS, r, c), BF16), compiler_params=_cp("parallel"))(me_arr, x)


def _prep(x, g, shards, me_arr, name, comm):
    s, d = x.shape
    steps = s // ROW_TILE
    tiles = [(w.shape[0] // steps, w.shape[1]) for w in shards]
    assert all(w.shape[0] % steps == 0 for w in shards)
    in_specs = [pl.BlockSpec((ROW_TILE, d), lambda i, me: (i, 0)), pl.BlockSpec((1, d), lambda i, me: (0, 0))]
    in_specs += [pl.BlockSpec(t, lambda i, me: (i, 0)) for t in tiles]
    out_specs = [pl.BlockSpec((ROW_TILE, d), lambda i, me: (i, 0))]
    out_specs += [pl.BlockSpec((None,) + t, lambda i, me: (me[0], i, 0)) for t in tiles]
    out_shape = [jax.ShapeDtypeStruct((s, d), BF16)] + [jax.ShapeDtypeStruct((N_CHIPS,) + w.shape, BF16) for w in shards]
    host = _Host(comm, in_specs, out_specs, out_shape, [], prefetch=1)

    def body(me_ref, *refs):
        (x_ref, g_ref, *w_refs), (h_ref, *slab_refs), _ = host.split(refs)
        step = pl.program_id(0)
        host.before(step, steps)
        xhat, _ = _rms_stats(x_ref[...])
        h_ref[...] = (xhat * g_ref[...]).astype(BF16)
        for w_ref, slab_ref in zip(w_refs, slab_refs):
            slab_ref[...] = w_ref[...].astype(BF16)
        host.after(step, steps)

    outs = pl.pallas_call(
        body, name=name,
        grid_spec=pltpu.PrefetchScalarGridSpec(num_scalar_prefetch=1, grid=(steps,), in_specs=host.in_specs,
                                               out_specs=host.out_specs, scratch_shapes=host.scratch),
        out_shape=host.out_shape, input_output_aliases=host.aliases,
        compiler_params=_cp("arbitrary"))(me_arr, x, g, *shards, *host.args)
    (h, *slabs), extra = host.results(outs)
    return h, slabs, extra


def _mm_nn(a, w3, name, comm=None):
    m, k = a.shape
    nsh, _, ns = w3.shape
    tm = 512 if m % 512 == 0 else ROW_TILE
    tn = _pick_tile(ns, 1024)
    per = ns // tn
    grid = (nsh * per, m // tm)
    host = _Host(comm,
                 [pl.BlockSpec((tm, k), lambda n, i: (i, 0)), pl.BlockSpec((None, k, tn), lambda n, i: (n // per, 0, n % per))],
                 [pl.BlockSpec((tm, tn), lambda n, i: (i, n))], [jax.ShapeDtypeStruct((m, nsh * ns), F32)], [])

    def body(*refs):
        (a_ref, w_ref), (o_ref,), _ = host.split(refs)
        step = pl.program_id(0) * grid[1] + pl.program_id(1)
        host.before(step, grid[0] * grid[1])
        o_ref[...] = jnp.dot(a_ref[...], w_ref[...], preferred_element_type=F32)
        host.after(step, grid[0] * grid[1])

    outs = pl.pallas_call(
        body, name=name, grid=grid, in_specs=host.in_specs, out_specs=host.out_specs, out_shape=host.out_shape,
        scratch_shapes=host.scratch, input_output_aliases=host.aliases,
        compiler_params=_cp("arbitrary", "arbitrary"))(a, w3, *host.args)
    (out,), extra = host.results(outs)
    return out, extra


def _mm_nt(a, b, name):
    m, k = a.shape
    n = b.shape[0]
    tm = 512 if m % 512 == 0 else ROW_TILE

    def body(a_ref, b_ref, o_ref):
        o_ref[...] = lax.dot_general(a_ref[...], b_ref[...], (((1,), (1,)), ((), ())), preferred_element_type=F32)

    return pl.pallas_call(
        body, name=name, grid=(m // tm,),
        in_specs=[pl.BlockSpec((tm, k), lambda i: (i, 0)), pl.BlockSpec((n, k), lambda i: (0, 0))],
        out_specs=pl.BlockSpec((tm, n), lambda i: (i, 0)),
        out_shape=jax.ShapeDtypeStruct((m, n), F32), compiler_params=_cp("parallel"))(a, b)


def _mm_tn(a, b, nsh, name, comm=None):
    s, m = a.shape
    n = b.shape[1]
    ns = n // nsh
    tm = 512 if m % 512 == 0 else ROW_TILE
    tn = _pick_tile(ns, 1024)
    per = ns // tn
    grid = (nsh * per, m // tm)
    host = _Host(comm, [pl.BlockSpec((s, tm), lambda j, i: (0, i)), pl.BlockSpec((s, tn), lambda j, i: (0, j))],
                 [pl.BlockSpec((None, tm, tn), lambda j, i: (j // per, i, j % per))],
                 [jax.ShapeDtypeStruct((nsh, m, ns), BF16)], [])

    def body(*refs):
        (a_ref, b_ref), (o_ref,), _ = host.split(refs)
        step = pl.program_id(0) * grid[1] + pl.program_id(1)
        host.before(step, grid[0] * grid[1])
        o_ref[...] = lax.dot_general(a_ref[...], b_ref[...], (((0,), (0,)), ((), ())),
                                     preferred_element_type=F32).astype(BF16)
        host.after(step, grid[0] * grid[1])

    outs = pl.pallas_call(
        body, name=name, grid=grid, in_specs=host.in_specs, out_specs=host.out_specs, out_shape=host.out_shape,
        scratch_shapes=host.scratch, input_output_aliases=host.aliases,
        compiler_params=_cp("arbitrary", "arbitrary"))(a, b, *host.args)
    (out,), extra = host.results(outs)
    return out, extra


def _tri(n, rel):
    row = lax.broadcasted_iota(jnp.int32, (2 * n, n), 0)
    col = lax.broadcasted_iota(jnp.int32, (2 * n, n), 1)
    return jnp.where(rel(jnp.where(row >= n, row - n, row), col), 1.0, 0.0).astype(BF16)


def _dot_split(x, tri2):
    hi = x.astype(BF16)
    lo = (x - hi.astype(F32)).astype(BF16)
    return jnp.dot(jnp.concatenate([hi, lo], axis=1), tri2, preferred_element_type=F32)


def _nt(a, b):
    return lax.dot_general(a, b, (((1,), (1,)), ((), ())), preferred_element_type=F32)


def _tn(a, b):
    return lax.dot_general(a, b, (((0,), (0,)), ((), ())), preferred_element_type=F32)


def _heads_per_step(nh):
    return max(h for h in (1, 2, 4) if nh % h == 0)


def _sba_fwd(p, sbw, name, comm=None):
    s = p.shape[0]
    nh = sbw // HEAD_DIM
    hp = _heads_per_step(nh)
    ngrp, hw = nh // hp, hp * HEAD_DIM
    blk = ATT_BLOCK
    nq = s // blk
    scale = 1.0 / math.sqrt(HEAD_DIM)
    host = _Host(comm,
                 [pl.BlockSpec((blk, hw), lambda g, i: (i, g)),
                  pl.BlockSpec((s, hw), lambda g, i: (0, ngrp + g)),
                  pl.BlockSpec((s, hw), lambda g, i: (0, 2 * ngrp + g))],
                 [pl.BlockSpec((blk, hw), lambda g, i: (i, g))] * 2,
                 [jax.ShapeDtypeStruct((s, sbw), F32)] * 2,
                 [pltpu.VMEM((s, hw), BF16)] * 2)

    def body(*refs):
        (q_ref, k_ref, v_ref), (o_ref, lt_ref), (kb_ref, vb_ref) = host.split(refs)
        i = pl.program_id(1)
        step = pl.program_id(0) * nq + i
        host.before(step, ngrp * nq)

        @pl.when(i == 0)
        def _():
            kb_ref[...] = k_ref[...].astype(BF16)
            vb_ref[...] = v_ref[...].astype(BF16)

        heads = [slice(h * HEAD_DIM, (h + 1) * HEAD_DIM) for h in range(hp)]
        qs = [q_ref[:, hd].astype(BF16) for hd in heads]
        later = _tri(blk, lambda r, c: r > c)
        causal = lax.broadcasted_iota(jnp.int32, (blk, blk), 1) < lax.broadcasted_iota(jnp.int32, (blk, blk), 0)

        def key_block(j, carry, diagonal):
            rows = pl.ds(pl.multiple_of(j * blk, blk), blk)
            hs = range(hp)
            z = [_nt(qs[h], kb_ref[rows, heads[h]]) * scale for h in hs]
            ls = [_log_sigmoid(z[h]) for h in hs]
            lm = [jnp.where(causal, ls[h] - z[h], 0.0) if diagonal else ls[h] - z[h] for h in hs]
            stay = [_dot_split(lm[h], later) for h in hs]
            w = [jnp.exp(ls[h] + stay[h] + carry[h][1]) for h in hs]
            if diagonal:
                w = [jnp.where(causal, w[h], 0.0) for h in hs]
            acc = [carry[h][0] + jnp.dot(w[h].astype(BF16), vb_ref[rows, heads[h]], preferred_element_type=F32) for h in hs]
            return tuple((acc[h], carry[h][1] + jnp.sum(lm[h], axis=1, keepdims=True)) for h in hs)

        init = tuple((jnp.zeros((blk, HEAD_DIM), F32), jnp.zeros((blk, 1), F32)) for _ in heads)
        carry = key_block(i, init, True)
        carry = lax.fori_loop(0, i, lambda n, c: key_block(i - 1 - n, c, False), carry)
        for h, hd in enumerate(heads):
            o_ref[:, hd] = carry[h][0]
            lt_ref[:, hd] = jnp.broadcast_to(carry[h][1], (blk, HEAD_DIM))
        host.after(step, ngrp * nq)

    outs = pl.pallas_call(
        body, name=name, grid=(ngrp, nq), in_specs=host.in_specs, out_specs=host.out_specs, out_shape=host.out_shape,
        scratch_shapes=host.scratch, input_output_aliases=host.aliases,
        compiler_params=_cp("arbitrary", "arbitrary"))(p, p, p, *host.args)
    (out, ltot), extra = host.results(outs)
    return out, ltot, extra


def _sba_bwd(p, ltot, dout, sbw, name, comm=None):
    s = p.shape[0]
    nh = sbw // HEAD_DIM
    hp = _heads_per_step(nh)
    ngrp, hw = nh // hp, hp * HEAD_DIM
    blk = ATT_BLOCK
    nq = s // blk
    scale = 1.0 / math.sqrt(HEAD_DIM)
    blk_spec = pl.BlockSpec((blk, hw), lambda g, i: (i, g))
    col_spec = pl.BlockSpec((s, hw), lambda g, i: (0, g))
    host = _Host(comm,
                 [blk_spec, pl.BlockSpec((s, hw), lambda g, i: (0, ngrp + g)),
                  pl.BlockSpec((s, hw), lambda g, i: (0, 2 * ngrp + g)), blk_spec, blk_spec],
                 [blk_spec, col_spec, col_spec], [jax.ShapeDtypeStruct((s, sbw), BF16)] * 3,
                 [pltpu.VMEM((s, hw), BF16)] * 2 + [pltpu.VMEM((s, hw), F32)] * 2)

    def body(*refs):
        (q_ref, k_ref, v_ref, lt_ref, do_ref), (dq_ref, dk_ref, dv_ref), (kb_ref, vb_ref, dka_ref, dva_ref) = host.split(refs)
        i = pl.program_id(1)
        step = pl.program_id(0) * nq + i
        host.before(step, ngrp * nq)

        @pl.when(i == 0)
        def _():
            kb_ref[...] = k_ref[...].astype(BF16)
            vb_ref[...] = v_ref[...].astype(BF16)
            dka_ref[...] = jnp.zeros_like(dka_ref)
            dva_ref[...] = jnp.zeros_like(dva_ref)

        heads = [slice(h * HEAD_DIM, (h + 1) * HEAD_DIM) for h in range(hp)]
        qs = [q_ref[:, hd].astype(BF16) for hd in heads]
        dos = [do_ref[:, hd].astype(BF16) for hd in heads]
        ltots = [lt_ref[:, h * HEAD_DIM:h * HEAD_DIM + 1] for h in range(hp)]
        upto = _tri(blk, lambda r, c: r <= c)
        before = _tri(blk, lambda r, c: r < c)
        causal = lax.broadcasted_iota(jnp.int32, (blk, blk), 1) < lax.broadcasted_iota(jnp.int32, (blk, blk), 0)

        def key_block(j, carry, diagonal):
            rows = pl.ds(pl.multiple_of(j * blk, blk), blk)
            hs = range(hp)
            kj = [kb_ref[rows, heads[h]] for h in hs]
            vj = [vb_ref[rows, heads[h]] for h in hs]
            z = [_nt(qs[h], kj[h]) * scale for h in hs]
            dw = [_nt(dos[h], vj[h]) for h in hs]
            ls = [_log_sigmoid(z[h]) for h in hs]
            lm = [jnp.where(causal, ls[h] - z[h], 0.0) if diagonal else ls[h] - z[h] for h in hs]
            stay = [ltots[h] - carry[h][1] - _dot_split(lm[h], upto) for h in hs]
            w = [jnp.exp(ls[h] + stay[h]) for h in hs]
            if diagonal:
                w = [jnp.where(causal, w[h], 0.0) for h in hs]
            da = [dw[h] * w[h] for h in hs]
            sig = [jnp.exp(ls[h]) for h in hs]
            chain = [sig[h] * (carry[h][2] + _dot_split(da[h], before)) for h in hs]
            if diagonal:
                chain = [jnp.where(causal, chain[h], 0.0) for h in hs]
            dzb = [((da[h] * (1.0 - sig[h]) - chain[h]) * scale).astype(BF16) for h in hs]
            dq = [carry[h][0] + jnp.dot(dzb[h], kj[h], preferred_element_type=F32) for h in hs]
            for h in hs:
                dka_ref[rows, heads[h]] += _tn(dzb[h], qs[h])
            for h in hs:
                dva_ref[rows, heads[h]] += _tn(w[h].astype(BF16), dos[h])
            return tuple((dq[h], carry[h][1] + jnp.sum(lm[h], axis=1, keepdims=True),
                          carry[h][2] + jnp.sum(da[h], axis=1, keepdims=True)) for h in hs)

        zero = jnp.zeros((blk, 1), F32)
        init = tuple((jnp.zeros((blk, HEAD_DIM), F32), zero, zero) for _ in heads)
        carry = lax.fori_loop(0, i, lambda j, c: key_block(j, c, False), init)
        carry = key_block(i, carry, True)
        for h, hd in enumerate(heads):
            dq_ref[:, hd] = carry[h][0].astype(BF16)

        @pl.when(i == nq - 1)
        def _():
            dk_ref[...] = dka_ref[...].astype(BF16)
            dv_ref[...] = dva_ref[...].astype(BF16)

        host.after(step, ngrp * nq)

    outs = pl.pallas_call(
        body, name=name, grid=(ngrp, nq), in_specs=host.in_specs, out_specs=host.out_specs, out_shape=host.out_shape,
        scratch_shapes=host.scratch, input_output_aliases=host.aliases,
        compiler_params=_cp("arbitrary", "arbitrary"))(p, p, p, ltot, dout, *host.args)
    (dq, dk, dv), extra = host.results(outs)
    return dq, dk, dv, extra


def _pool_groups(pad_ref, tile, row0, gd, halo):
    row = row0 + lax.broadcasted_iota(jnp.int32, (tile, 1), 0)
    out = []
    for gi, win in enumerate(POOL_WINDOWS):
        cs = slice(gi * gd, (gi + 1) * gd)
        tok = pad_ref[halo:halo + tile, cs]
        acc = tok
        for j in range(1, win):
            acc = acc + pad_ref[halo - j:halo - j + tile, cs]
        cnt = jnp.minimum(win, row + 1).astype(F32)
        out.append(acc / cnt - tok)
    return out


def _even_mix_fwd(p, att, pool_w, pool_scale, d, name):
    s = p.shape[0]
    half = d // 2
    gd = half // len(POOL_WINDOWS)
    t, hb = ROW_TILE, POOL_HALO

    def body(u_ref, uh_ref, g_ref, a_ref, pw_ref, sc_ref, y_ref, pad_ref):
        i = pl.program_id(0)
        pad_ref[0:hb, :] = jnp.where(i > 0, uh_ref[...], 0.0)
        pad_ref[hb:, :] = u_ref[...]
        pooled = _pool_groups(pad_ref, t, i * t, gd, hb)
        for gi in range(len(POOL_WINDOWS)):
            cs = slice(gi * gd, (gi + 1) * gd)
            po = jnp.dot(pooled[gi].astype(BF16), pw_ref[gi], preferred_element_type=F32) * sc_ref[:, cs]
            y_ref[:, half + gi * gd:half + (gi + 1) * gd] = (po * _silu(g_ref[:, half + gi * gd:half + (gi + 1) * gd])).astype(BF16)
        y_ref[:, :half] = (a_ref[...] * _silu(g_ref[:, :half])).astype(BF16)

    return pl.pallas_call(
        body, name=name, grid=(s // t,),
        in_specs=[pl.BlockSpec((t, half), lambda i: (i, 3)),
                  pl.BlockSpec((hb, half), lambda i: (jnp.maximum(i * (t // hb) - 1, 0), 3)),
                  pl.BlockSpec((t, d), lambda i: (i, 2)),
                  pl.BlockSpec((t, half), lambda i: (i, 0)),
                  pl.BlockSpec(pool_w.shape, lambda i: (0, 0, 0)),
                  pl.BlockSpec((1, half), lambda i: (0, 0))],
        out_specs=pl.BlockSpec((t, d), lambda i: (i, 0)),
        out_shape=jax.ShapeDtypeStruct((s, d), BF16),
        scratch_shapes=[pltpu.VMEM((hb + t, half), F32)],
        compiler_params=_cp("parallel"))(p, p, p, att, pool_w, pool_scale)


def _even_mix_bwd(p, att, dy, pool_w, pool_scale, d, name, comm=None):
    s = p.shape[0]
    half = d // 2
    ng = len(POOL_WINDOWS)
    gd = half // ng
    t, hb = ROW_TILE, POOL_HALO
    nt = s // t
    host = _Host(
        comm,
        [pl.BlockSpec((t, half), lambda i: (i, 3)),
         pl.BlockSpec((hb, half), lambda i: (jnp.maximum(i * (t // hb) - 1, 0), 3)),
         pl.BlockSpec((t, d), lambda i: (i, 2)),
         pl.BlockSpec((hb, half), lambda i: (jnp.minimum((i + 1) * (t // hb), s // hb - 1), 5)),
         pl.BlockSpec((t, half), lambda i: (i, 0)),
         pl.BlockSpec((t, d), lambda i: (i, 0)),
         pl.BlockSpec((hb, half), lambda i: (jnp.minimum((i + 1) * (t // hb), s // hb - 1), 1)),
         pl.BlockSpec(pool_w.shape, lambda i: (0, 0, 0)),
         pl.BlockSpec((1, half), lambda i: (0, 0))],
        [pl.BlockSpec((t, half), lambda i: (i, 0)),
         pl.BlockSpec((t, half), lambda i: (i, 0)),
         pl.BlockSpec((t, d), lambda i: (i, 0)),
         pl.BlockSpec((1, half), lambda i: (0, 0)),
         pl.BlockSpec((ng, gd, gd), lambda i: (0, 0, 0))],
        [jax.ShapeDtypeStruct((s, half), F32), jax.ShapeDtypeStruct((s, half), BF16),
         jax.ShapeDtypeStruct((s, d), BF16), jax.ShapeDtypeStruct((1, half), F32),
         jax.ShapeDtypeStruct((ng, gd, gd), F32)],
        [pltpu.VMEM((hb + t, half), F32), pltpu.VMEM((t + hb, half), F32)])

    def body(*refs):
        ((u_ref, uh_ref, g_ref, gh_ref, a_ref, dy_ref, dyh_ref, pw_ref, sc_ref),
         (da_ref, du_ref, dg_ref, dsc_ref, dpw_ref), (pad_ref, dn_ref)) = host.split(refs)
        i = pl.program_id(0)
        host.before(i, nt)
        first = i == 0
        pad_ref[0:hb, :] = jnp.where(i > 0, uh_ref[...], 0.0)
        pad_ref[hb:, :] = u_ref[...]
        pooled = _pool_groups(pad_ref, t, i * t, gd, hb)
        g1 = g_ref[:, :half]
        dy1 = dy_ref[:, :half]
        da_ref[...] = dy1 * _silu(g1)
        dg_ref[:, :half] = (dy1 * a_ref[...] * _dsilu(g1)).astype(BF16)
        row = i * t + lax.broadcasted_iota(jnp.int32, (t + hb, 1), 0)
        for gi, win in enumerate(POOL_WINDOWS):
            cs = slice(gi * gd, (gi + 1) * gd)
            cs2 = slice(half + gi * gd, half + (gi + 1) * gd)
            w = pw_ref[gi]
            pb = pooled[gi].astype(BF16)
            zp = jnp.dot(pb, w, preferred_element_type=F32)
            g2 = g_ref[:, cs2]
            dy2 = dy_ref[:, cs2]
            dg_ref[:, cs2] = (dy2 * zp * sc_ref[:, cs] * _dsilu(g2)).astype(BF16)
            dpo = dy2 * _silu(g2)
            _acc_rows(dsc_ref.at[:, cs], first, jnp.sum(dpo * zp, axis=0, keepdims=True))
            dz = (dpo * sc_ref[:, cs]).astype(BF16)
            _acc_rows(dpw_ref.at[gi], first, _tn(pb, dz))
            dzh = jnp.where(i < nt - 1, dyh_ref[:, cs] * _silu(gh_ref[:, cs]) * sc_ref[:, cs], 0.0).astype(BF16)
            dpool = _nt(dz, w)
            dpool_h = _nt(dzh, w)
            cnt = jnp.minimum(win, row + 1).astype(F32)
            dn_ref[0:t, cs] = dpool / cnt[0:t]
            dn_ref[t:, cs] = dpool_h / cnt[t:]
            acc = dn_ref[0:t, cs]
            for j in range(1, win):
                acc = acc + dn_ref[j:j + t, cs]
            du_ref[:, cs] = (acc - dpool).astype(BF16)
        host.after(i, nt)

    outs = pl.pallas_call(
        body, name=name, grid=(nt,), in_specs=host.in_specs, out_specs=host.out_specs, out_shape=host.out_shape,
        scratch_shapes=host.scratch, input_output_aliases=host.aliases,
        compiler_params=_cp("arbitrary"))(p, p, p, p, att, dy, dy, pool_w, pool_scale, *host.args)
    return host.results(outs)


def _mm_out_even(y, w, x, g_post, g_pre_next, name):
    s, k = y.shape
    d = w.shape[1]
    t = ROW_TILE

    def body(y_ref, w_ref, x_ref, gp_ref, gn_ref, o_ref, x1_ref, h1_ref):
        for r0 in range(0, t, t // 2):
            rows = slice(r0, r0 + t // 2)
            o = jnp.dot(y_ref[rows, :], w_ref[...], preferred_element_type=F32)
            o_ref[rows, :] = o
            ohat, _ = _rms_stats(o)
            x1 = x_ref[rows, :] + ohat * gp_ref[...]
            x1_ref[rows, :] = x1
            xhat, _ = _rms_stats(x1)
            h1_ref[rows, :] = (xhat * gn_ref[...]).astype(BF16)

    row = lambda c: pl.BlockSpec((t, c), lambda i: (i, 0))
    vec = pl.BlockSpec((1, d), lambda i: (0, 0))
    return pl.pallas_call(
        body, name=name, grid=(s // t,),
        in_specs=[row(k), pl.BlockSpec((k, d), lambda i: (0, 0)), row(d), vec, vec],
        out_specs=[row(d), row(d), row(d)],
        out_shape=[jax.ShapeDtypeStruct((s, d), F32), jax.ShapeDtypeStruct((s, d), F32),
                   jax.ShapeDtypeStruct((s, d), BF16)],
        compiler_params=_cp("parallel"))(y, w, x, g_post, g_pre_next)


def _mm_out_odd(y, w, x1, g_post, target, name):
    s, k = y.shape
    d = w.shape[1]
    t = ROW_TILE

    def body(y_ref, w_ref, x_ref, gp_ref, tg_ref, do_ref, dx_ref, loss_ref, dgp_ref):
        first = pl.program_id(0) == 0
        gp = gp_ref[...]
        part = dgp = None
        for r0 in range(0, t, t // 2):
            rows = slice(r0, r0 + t // 2)
            o = jnp.dot(y_ref[rows, :], w_ref[...], preferred_element_type=F32)
            ohat, r = _rms_stats(o)
            diff = x_ref[rows, :] + ohat * gp - tg_ref[rows, :]
            part_half = 0.5 * jnp.sum(jnp.mean(diff * diff, axis=-1, keepdims=True), axis=0, keepdims=True)
            dx2 = diff * (1.0 / d)
            dx_ref[rows, :] = dx2
            do, dgp_half = _rms_bwd(dx2, ohat, r, gp)
            do_ref[rows, :] = do.astype(BF16)
            part = part_half if part is None else part + part_half
            dgp = dgp_half if dgp is None else dgp + dgp_half
        _acc_rows(loss_ref, first, jnp.broadcast_to(part, loss_ref.shape))
        _acc_rows(dgp_ref, first, dgp)

    row = lambda c: pl.BlockSpec((t, c), lambda i: (i, 0))
    vec = pl.BlockSpec((1, d), lambda i: (0, 0))
    return pl.pallas_call(
        body, name=name, grid=(s // t,),
        in_specs=[row(k), pl.BlockSpec((k, d), lambda i: (0, 0)), row(d), vec, row(d)],
        out_specs=[row(d), row(d), pl.BlockSpec((8, LANES), lambda i: (0, 0)), vec],
        out_shape=[jax.ShapeDtypeStruct((s, d), BF16), jax.ShapeDtypeStruct((s, d), F32),
                   jax.ShapeDtypeStruct((8, LANES), F32), jax.ShapeDtypeStruct((1, d), F32)],
        compiler_params=_cp("arbitrary"))(y, w, x1, g_post, target)


def _layer_norm(d1, cg, cb):
    mu = jnp.mean(d1, axis=-1, keepdims=True)
    cen = d1 - mu
    rstd = lax.rsqrt(jnp.mean(cen * cen, axis=-1, keepdims=True) + EPS)
    n = cen * rstd
    return n, rstd, n * cg + cb


SUBLANES = 8
ROW_STRIP = 64
GATHER_PIECES = 8
CONV_ROWS = 64


def _make_shifts(pad_ref, cs, sh_ref):
    rows = sh_ref.shape[1]
    for r in range(1, SUBLANES):
        sh_ref[r - 1] = pad_ref[r:r + rows, cs]


def _by_shift(taps, base, sign=1):
    return sorted(range(taps), key=lambda k: ((sign * (base + k)) % SUBLANES, k))


def _window(pad_ref, cs, sh_ref, off, t):
    m, r = divmod(off, SUBLANES)
    if r == 0:
        return pad_ref[SUBLANES * m:SUBLANES * m + t, cs]
    return sh_ref[r - 1, SUBLANES * m:SUBLANES * m + t, :]


def _odd_mix_fwd(p, sconv_w, dconv_w, dconv_b, cnorm_g, cnorm_b, d, name):
    s = p.shape[0]
    w = d // 2
    k3, k31 = sconv_w.shape[0], dconv_w.shape[0]
    t, hb = ROW_TILE, CONV_HALO
    assert hb >= k31 - 1 and w % LANES == 0

    def body(p_ref, ph_ref, w3_ref, w31_ref, b31_ref, cg_ref, cb_ref, y_ref, s3_ref, d1_ref, mpad, dpad, sh_ref):
        i = pl.program_id(0)
        mpad[0:hb, :] = jnp.where(i > 0, ph_ref[:, 2 * w:3 * w] * ph_ref[:, 0:w], 0.0)
        mpad[hb:, :] = p_ref[:, 2 * w:3 * w] * p_ref[:, 0:w]
        dpad[0:hb, :] = jnp.where(i > 0, ph_ref[:, 3 * w:4 * w] * _sigmoid(ph_ref[:, 4 * w:5 * w]), 0.0)
        dpad[hb:, :] = p_ref[:, 3 * w:4 * w] * _sigmoid(p_ref[:, 4 * w:5 * w])
        for c0 in range(0, w, LANES):
            cs = slice(c0, c0 + LANES)
            acc = jnp.zeros((t, LANES), F32)
            for kk in range(k3):
                acc = acc + w3_ref[kk:kk + 1, cs] * mpad[hb - (k3 - 1) + kk:hb - (k3 - 1) + kk + t, cs]
            s3_ref[:, cs] = acc
            _make_shifts(dpad, cs, sh_ref)
            for r0 in range(0, t, CONV_ROWS):
                acc = jnp.zeros((CONV_ROWS, LANES), F32)
                for kk in _by_shift(k31, hb - (k31 - 1)):
                    acc = acc + w31_ref[kk:kk + 1, cs] * _window(dpad, cs, sh_ref, hb - (k31 - 1) + kk + r0, CONV_ROWS)
                d1_ref[r0:r0 + CONV_ROWS, cs] = acc + b31_ref[:, cs]
        _, _, d2 = _layer_norm(d1_ref[...], cg_ref[...], cb_ref[...])
        y_ref[:, :w] = (p_ref[:, w:2 * w] * s3_ref[...] * _silu(p_ref[:, 5 * w:6 * w])).astype(BF16)
        y_ref[:, w:] = (_silu(d2) * _silu(p_ref[:, 6 * w:7 * w])).astype(BF16)

    row = lambda c: pl.BlockSpec((t, c), lambda i: (i, 0))
    full = lambda a: pl.BlockSpec(a.shape, lambda i: (0, 0))
    return pl.pallas_call(
        body, name=name, grid=(s // t,),
        in_specs=[row(7 * w),
                  pl.BlockSpec((hb, 5 * w), lambda i: (jnp.maximum(i * (t // hb) - 1, 0), 0)),
                  full(sconv_w), full(dconv_w), full(dconv_b), full(cnorm_g), full(cnorm_b)],
        out_specs=[row(d), row(w), row(w)],
        out_shape=[jax.ShapeDtypeStruct((s, d), BF16), jax.ShapeDtypeStruct((s, w), F32),
                   jax.ShapeDtypeStruct((s, w), F32)],
        scratch_shapes=[pltpu.VMEM((hb + t, w), F32)] * 2 + [pltpu.VMEM((SUBLANES - 1, hb + t - SUBLANES, LANES), F32)],
        compiler_params=_cp("parallel"))(p, p, sconv_w, dconv_w, dconv_b, cnorm_g, cnorm_b)


def _odd_bwd_rows(p, s3, d1, dy, cnorm_g, cnorm_b, d, name, comm=None):
    s = p.shape[0]
    w = d // 2
    t = ROW_TILE
    col = lambda j: pl.BlockSpec((t, w), lambda i: (i, j))
    row = lambda c: pl.BlockSpec((t, c), lambda i: (i, 0))
    vec = pl.BlockSpec((1, w), lambda i: (0, 0))
    host = _Host(comm, [col(1), col(5), col(6), row(w), row(w), row(d), vec, vec],
                 [row(w), row(d), row(w), row(w), vec, vec, vec],
                 [jax.ShapeDtypeStruct((s, w), BF16), jax.ShapeDtypeStruct((s, d), BF16),
                  jax.ShapeDtypeStruct((s, w), F32), jax.ShapeDtypeStruct((s, w), F32)] + [jax.ShapeDtypeStruct((1, w), F32)] * 3, [])

    def body(*refs):
        ((bc_ref, g1_ref, g2_ref, s3_ref, d1_ref, dy_ref, cg_ref, cb_ref),
         (dbc_ref, dg_ref, ds3_ref, dd1_ref, dcg_ref, dcb_ref, db_ref), _) = host.split(refs)
        step = pl.program_id(0)
        host.before(step, s // t)
        first = step == 0

        def strip(j, sums):
            rows = slice(j * ROW_STRIP, (j + 1) * ROW_STRIP)
            g1, g2 = g1_ref[rows, :], g2_ref[rows, :]
            bc, s3v = bc_ref[rows, :], s3_ref[rows, :]
            dy1, dy2 = dy_ref[rows, :w], dy_ref[rows, w:]
            n, rstd, d2 = _layer_norm(d1_ref[rows, :], cg_ref[...], cb_ref[...])
            dg_ref[rows, :w] = (dy1 * bc * s3v * _dsilu(g1)).astype(BF16)
            dg_ref[rows, w:] = (dy2 * _silu(d2) * _dsilu(g2)).astype(BF16)
            dco = dy1 * _silu(g1)
            dbc_ref[rows, :] = (dco * s3v).astype(BF16)
            ds3_ref[rows, :] = dco * bc
            dd2 = dy2 * _silu(g2) * _dsilu(d2)
            dn = dd2 * cg_ref[...]
            dd1 = rstd * (dn - jnp.mean(dn, axis=-1, keepdims=True) - n * jnp.mean(dn * n, axis=-1, keepdims=True))
            dd1_ref[rows, :] = dd1
            dcb, dcg, db = sums
            return (dcb + jnp.sum(dd2, axis=0, keepdims=True), dcg + jnp.sum(dd2 * n, axis=0, keepdims=True),
                    db + jnp.sum(dd1, axis=0, keepdims=True))

        zero = jnp.zeros((1, w), F32)
        sums = (zero, zero, zero)
        for j in range(t // ROW_STRIP):
            sums = strip(j, sums)
        dcb, dcg, db = sums
        _acc_rows(dcb_ref, first, dcb)
        _acc_rows(dcg_ref, first, dcg)
        _acc_rows(db_ref, first, db)
        host.after(step, s // t)

    outs = pl.pallas_call(
        body, name=name, grid=(s // t,), in_specs=host.in_specs, out_specs=host.out_specs, out_shape=host.out_shape,
        scratch_shapes=host.scratch, input_output_aliases=host.aliases,
        compiler_params=_cp("arbitrary"))(p, p, p, s3, d1, dy, cnorm_g, cnorm_b, *host.args)
    return host.results(outs)


def _odd_bwd_conv(p, ds3, dd1, sconv_w, dconv_w, d, name):
    s = p.shape[0]
    w = d // 2
    k3, k31 = sconv_w.shape[0], dconv_w.shape[0]
    t, hb, ha = ROW_TILE, CONV_HALO, 8
    nt = s // t
    assert hb >= k31 - 1 and ha >= k3 - 1

    def body(hc_ref, cc_ref, ga_ref, gb_ref, hch_ref, cch_ref, gah_ref, gbh_ref, ds3_ref, ds3h_ref, dd1_ref, dd1h_ref,
             w3_ref, w31_ref, dhc_ref, dcc_ref, dga_ref, dgb_ref, dw3_ref, dw31_ref, mpad, dpad, s3pad, d1pad, sh_ref):
        i = pl.program_id(0)
        first = i == 0
        last = i == nt - 1
        mpad[0:hb, :] = jnp.where(i > 0, cch_ref[...] * hch_ref[...], 0.0)
        mpad[hb:, :] = cc_ref[...] * hc_ref[...]
        dpad[0:hb, :] = jnp.where(i > 0, gah_ref[...] * _sigmoid(gbh_ref[...]), 0.0)
        dpad[hb:, :] = ga_ref[...] * _sigmoid(gb_ref[...])
        s3pad[0:t, :] = ds3_ref[...]
        s3pad[t:, :] = jnp.where(last, 0.0, ds3h_ref[...])
        d1pad[0:t, :] = dd1_ref[...]
        d1pad[t:, :] = jnp.where(last, 0.0, dd1h_ref[...])

        @pl.when(first)
        def _():
            dw3_ref[...] = jnp.zeros_like(dw3_ref)
            dw31_ref[...] = jnp.zeros_like(dw31_ref)

        def fold(v):
            return jnp.sum(v.reshape(v.shape[0] // SUBLANES, SUBLANES, LANES), axis=0)

        groups = range(0, t, CONV_ROWS)
        for c0 in range(0, w, LANES):
            cs = slice(c0, c0 + LANES)
            ds3v = s3pad[0:t, cs]
            dm = jnp.zeros((t, LANES), F32)
            for kk in range(k3):
                dm = dm + w3_ref[kk:kk + 1, cs] * s3pad[k3 - 1 - kk:k3 - 1 - kk + t, cs]
                off = hb - (k3 - 1) + kk
                dw3_ref[SUBLANES * kk:SUBLANES * (kk + 1), cs] += fold(ds3v * mpad[off:off + t, cs])
            dcc_ref[:, cs] = (dm * hc_ref[:, cs]).astype(BF16)
            dhc_ref[:, cs] = (dm * cc_ref[:, cs]).astype(BF16)
            _make_shifts(d1pad, cs, sh_ref)
            for r0 in groups:
                rows = slice(r0, r0 + CONV_ROWS)
                dd0 = jnp.zeros((CONV_ROWS, LANES), F32)
                for kk in _by_shift(k31, -(k31 - 1), -1):
                    dd0 = dd0 + w31_ref[kk:kk + 1, cs] * _window(d1pad, cs, sh_ref, k31 - 1 - kk + r0, CONV_ROWS)
                sgb = _sigmoid(gb_ref[rows, cs])
                dga_ref[rows, cs] = (dd0 * sgb).astype(BF16)
                dgb_ref[rows, cs] = (dd0 * ga_ref[rows, cs] * sgb * (1.0 - sgb)).astype(BF16)
            _make_shifts(dpad, cs, sh_ref)
            for kk in _by_shift(k31, hb - (k31 - 1)):
                part = jnp.zeros((SUBLANES, LANES), F32)
                for r0 in groups:
                    part = part + fold(d1pad[r0:r0 + CONV_ROWS, cs]
                                       * _window(dpad, cs, sh_ref, hb - (k31 - 1) + kk + r0, CONV_ROWS))
                dw31_ref[SUBLANES * kk:SUBLANES * (kk + 1), cs] += part

    col = lambda j: pl.BlockSpec((t, w), lambda i: (i, j))
    pre = lambda j: pl.BlockSpec((hb, w), lambda i: (jnp.maximum(i * (t // hb) - 1, 0), j))
    row = pl.BlockSpec((t, w), lambda i: (i, 0))
    post = lambda h: pl.BlockSpec((h, w), lambda i: (jnp.minimum((i + 1) * (t // h), s // h - 1), 0))
    full = lambda a: pl.BlockSpec(a.shape, lambda i: (0, 0))
    dhc, dcc, dga, dgb, dw3, dw31 = pl.pallas_call(
        body, name=name, grid=(nt,),
        in_specs=[col(0), col(2), col(3), col(4), pre(0), pre(2), pre(3), pre(4),
                  row, post(ha), row, post(hb), full(sconv_w), full(dconv_w)],
        out_specs=[row, row, row, row, pl.BlockSpec((SUBLANES * k3, w), lambda i: (0, 0)),
                   pl.BlockSpec((SUBLANES * k31, w), lambda i: (0, 0))],
        out_shape=[jax.ShapeDtypeStruct((s, w), BF16)] * 4
        + [jax.ShapeDtypeStruct((SUBLANES * k3, w), F32), jax.ShapeDtypeStruct((SUBLANES * k31, w), F32)],
        scratch_shapes=[pltpu.VMEM((hb + t, w), F32)] * 2 + [pltpu.VMEM((t + ha, w), F32), pltpu.VMEM((t + hb, w), F32),
                                                             pltpu.VMEM((SUBLANES - 1, hb + t - SUBLANES, LANES), F32)],
        compiler_params=_cp("arbitrary"))(p, p, p, p, p, p, p, p, ds3, ds3, dd1, dd1, sconv_w, dconv_w)
    return dhc, dcc, dga, dgb, jnp.sum(dw3.reshape(k3, SUBLANES, w), axis=1), jnp.sum(dw31.reshape(k31, SUBLANES, w), axis=1)


def _mm_in_bwd(dp, w3, x, g_pre, dres, post, name, comm=None):
    s = dp.shape[0]
    nsh, d, ns = w3.shape
    t = 512 if s % 512 == 0 else ROW_TILE
    nt = s // t
    ks = 2 if (ns // 2) % LANES == 0 else 1
    nk, kw = nsh * ks, ns // ks
    chunk = 128
    nchunk = t // chunk
    row = pl.BlockSpec((t, d), lambda i, k: (i, 0))
    vec = pl.BlockSpec((1, d), lambda i, k: (0, 0))
    rowwise = [x, dres] + ([post[0]] if post is not None else [])
    in_specs = [pl.BlockSpec((t, kw), lambda i, k: (i, k)), pl.BlockSpec((None, d, kw), lambda i, k: (k // ks, 0, k % ks)), vec]
    out_specs = [row, vec]
    out_shape = [jax.ShapeDtypeStruct((s, d), F32), jax.ShapeDtypeStruct((1, d), F32)]
    args = [dp, w3, g_pre]
    if post is not None:
        in_specs += [vec]
        out_specs += [row, vec]
        out_shape += [jax.ShapeDtypeStruct((s, d), BF16), jax.ShapeDtypeStruct((1, d), F32)]
        args += [post[1]]
    n_blocked = len(in_specs)
    in_specs += [ANY] * len(rowwise)
    args += rowwise
    host = _Host(comm, in_specs, out_specs, out_shape,
                 [pltpu.VMEM((t, d), F32), pltpu.VMEM((len(rowwise), 2, chunk, d), F32), pltpu.SemaphoreType.DMA((len(rowwise), 2))])

    def body(*refs):
        ins, outs, (acc_ref, buf_ref, sem_ref) = host.split(refs)
        dp_ref, w_ref, g_ref = ins[:3]
        hbm = ins[n_blocked:]
        dx_ref, dg_ref = outs[:2]
        tile = pl.program_id(0)
        kk = pl.program_id(1)
        first = tile == 0
        step = tile * nk + kk
        host.before(step, nt * nk)
        part = _nt(dp_ref[...], w_ref[...])

        @pl.when(kk == 0)
        def _():
            acc_ref[...] = part

        @pl.when(kk > 0)
        def _():
            acc_ref[...] += part

        def fetch(ci, slot):
            return [pltpu.make_async_copy(src.at[pl.ds(tile * t + ci * chunk, chunk)], buf_ref.at[n, slot], sem_ref.at[n, slot])
                    for n, src in enumerate(hbm)]

        @pl.when(kk == nk - 1)
        def _():
            dg = dgp = None
            for cp in fetch(0, 0):
                cp.start()
            for ci in range(nchunk):
                slot = ci % 2
                if ci + 1 < nchunk:
                    for cp in fetch(ci + 1, 1 - slot):
                        cp.start()
                for cp in fetch(ci, slot):
                    cp.wait()
                rows = slice(ci * chunk, (ci + 1) * chunk)
                xhat, r = _rms_stats(buf_ref[0, slot])
                dxn, dg_part = _rms_bwd(acc_ref[rows, :], xhat, r, g_ref[...])
                dx = buf_ref[1, slot] + dxn
                dx_ref[rows, :] = dx
                dg = dg_part if dg is None else dg + dg_part
                if post is not None:
                    ohat, ro = _rms_stats(buf_ref[2, slot])
                    do, dgp_part = _rms_bwd(dx, ohat, ro, ins[3][...])
                    outs[2][rows, :] = do.astype(BF16)
                    dgp = dgp_part if dgp is None else dgp + dgp_part
            _acc_rows(dg_ref, first, dg)
            if post is not None:
                _acc_rows(outs[3], first, dgp)

        host.after(step, nt * nk)

    res = pl.pallas_call(
        body, name=name, grid=(nt, nk), in_specs=host.in_specs, out_specs=host.out_specs, out_shape=host.out_shape,
        scratch_shapes=host.scratch, input_output_aliases=host.aliases,
        compiler_params=_cp("arbitrary", "arbitrary"))(*args, *host.args)
    return host.results(res)


def _half_add(g, r1, c_arr, name, after=None):
    nsh, rows, ns = g.shape
    h = rows // 2
    tr = min(ROW_TILE, h)
    per = h // tr

    def body(c_ref, g_ref, r_ref, *rest):
        rest[-1][...] = (g_ref[...].astype(F32) + r_ref[...].astype(F32)).astype(BF16)

    spec = pl.BlockSpec((None, tr, ns), lambda s, r, c: (s, r, 0))
    ordering = [] if after is None else [after]
    return pl.pallas_call(
        body, name=name,
        grid_spec=pltpu.PrefetchScalarGridSpec(
            num_scalar_prefetch=1, grid=(nsh, per),
            in_specs=[pl.BlockSpec((None, tr, ns), lambda s, r, c: (s, c[0] * per + r, 0)), spec] + [ANY] * len(ordering),
            out_specs=spec),
        out_shape=jax.ShapeDtypeStruct((nsh, h, ns), BF16), compiler_params=_cp("parallel", "parallel"))(c_arr, g, r1, *ordering)


def _sum_chips(hh, r2, mc_arr, name, after=None):
    _, h, ns = hh.shape
    tr = min(ROW_TILE, h)
    per = h // tr

    def body(mc_ref, h_ref, a_ref, b_ref, c_ref, *rest):
        rest[-1][...] = ((h_ref[...].astype(F32) + a_ref[...].astype(F32)) + b_ref[...].astype(F32)) + c_ref[...].astype(F32)

    got = lambda k: pl.BlockSpec((None, tr, ns), lambda r, mc: (k, r, 0))
    ordering = [] if after is None else [after]
    return pl.pallas_call(
        body, name=name,
        grid_spec=pltpu.PrefetchScalarGridSpec(
            num_scalar_prefetch=1, grid=(per,),
            in_specs=[pl.BlockSpec((None, tr, ns), lambda r, mc: (mc[0], r, 0)), got(0), got(1), got(2)] + [ANY] * len(ordering),
            out_specs=pl.BlockSpec((tr, ns), lambda r, mc: (mc[1] * per + r, 0))),
        out_shape=jax.ShapeDtypeStruct((2 * h, ns), F32), compiler_params=_cp("parallel"))(mc_arr, hh, r2, r2, r2, *ordering)


def _add2(a, b, name):
    def body(a_ref, b_ref, o_ref):
        o_ref[...] = a_ref[...] + b_ref[...]

    return pl.pallas_call(body, name=name, out_shape=jax.ShapeDtypeStruct(a.shape, a.dtype), compiler_params=_cp())(a, b)


def _sum_chips_ordered(s2, r2, mc_arr, name):
    rows, w = s2.shape
    rh = rows // 2

    def body(mc_ref, s_ref, a_ref, b_ref, c_ref, o_ref):
        me = mc_ref[0]
        acc = None
        for j in range(N_CHIPS):
            rel = jnp.bitwise_xor(me, j)
            v = jnp.where(rel == 0, s_ref[...], jnp.where(rel == 2, a_ref[...], jnp.where(rel == 1, b_ref[...], c_ref[...])))
            acc = v if acc is None else acc + v
        o_ref[...] = acc

    got = lambda k: pl.BlockSpec((None, rh, w), lambda i, mc: (k, 0, 0))
    return pl.pallas_call(
        body, name=name,
        grid_spec=pltpu.PrefetchScalarGridSpec(
            num_scalar_prefetch=1, grid=(1,),
            in_specs=[pl.BlockSpec((rh, w), lambda i, mc: (mc[1], 0)), got(0), got(1), got(2)],
            out_specs=pl.BlockSpec((rh, w), lambda i, mc: (mc[1], 0))),
        out_shape=jax.ShapeDtypeStruct((rows, w), F32), compiler_params=_cp("arbitrary"))(mc_arr, s2, r2, r2, r2)


def _adamw(w, g, m, v, name, comm=None):
    r, c = w.shape
    tr = ROW_TILE if r % ROW_TILE == 0 else r
    c1 = 1.0 / (1.0 - ADAM_B1 ** ADAM_STEP)
    c2 = 1.0 / (1.0 - ADAM_B2 ** ADAM_STEP)
    spec = pl.BlockSpec((tr, c), lambda i: (i, 0))
    host = _Host(comm, [spec] * 4, [spec] * 4, [jax.ShapeDtypeStruct((r, c), F32)] * 4, [])

    def body(*refs):
        (w_ref, g_ref, m_ref, v_ref), (go_ref, d_ref, nm_ref, nv_ref), _ = host.split(refs)
        step = pl.program_id(0)
        host.before(step, r // tr)
        gv = g_ref[...]
        go_ref[...] = gv
        nm = ADAM_B1 * m_ref[...] + (1.0 - ADAM_B1) * gv
        nv = ADAM_B2 * v_ref[...] + (1.0 - ADAM_B2) * (gv * gv)
        nm_ref[...] = nm
        nv_ref[...] = nv
        d_ref[...] = -ADAM_LR * ((nm * c1) / (jnp.sqrt(nv * c2) + ADAM_EPS) + ADAM_WD * w_ref[...])
        host.after(step, r // tr)

    outs = pl.pallas_call(
        body, name=name, grid=(r // tr,), in_specs=host.in_specs, out_specs=host.out_specs, out_shape=host.out_shape,
        scratch_shapes=host.scratch, input_output_aliases=host.aliases,
        compiler_params=_cp("arbitrary"))(w, g, m, v, *host.args)
    return host.results(outs)


def _swap_with_sibling(grads, wholes, name):
    n, nw = len(grads), len(wholes)
    halves = [g.shape[1] // 2 for g in grads]

    def body(*refs):
        srcs, dsts = refs[:n + nw], refs[n + nw:2 * (n + nw)]
        ssem, rsem = refs[2 * (n + nw):]
        x, y, c, me, chips, sib = _place()
        cps = [_rcopy(srcs[a].at[:, pl.ds((1 - c) * halves[a], halves[a]), :], dsts[a], ssem.at[a], rsem.at[a], sib)
               for a in range(n)]
        cps += [_rcopy(srcs[a], dsts[a], ssem.at[a], rsem.at[a], sib) for a in range(n, n + nw)]
        for cp in cps:
            cp.start()
        for cp in cps:
            cp.wait_recv()
        for cp in cps:
            cp.wait_send()

    out_shape = [jax.ShapeDtypeStruct((g.shape[0], h, g.shape[2]), g.dtype) for g, h in zip(grads, halves)]
    out_shape += [jax.ShapeDtypeStruct(w.shape, w.dtype) for w in wholes]
    return pl.pallas_call(
        body, name=name, in_specs=[ANY] * (n + nw), out_specs=[ANY] * (n + nw), out_shape=out_shape,
        scratch_shapes=[pltpu.SemaphoreType.DMA((n + nw,)), pltpu.SemaphoreType.DMA((n + nw,))],
        compiler_params=pltpu.CompilerParams(has_side_effects=True))(*grads, *wholes)


def _scatter_start(h, name):
    land = (3,) + h.shape[1:]

    def body(h_ref, land_ref, send_sems, recv_sems, h_thru, land_thru, token):
        x, y, c, me, chips, sib = _place()
        for k, chip in enumerate(chips):
            _rcopy(h_ref.at[2 * chip[0] + chip[1]], land_ref.at[k], send_sems.at[k], recv_sems.at[k], (*chip, c)).start()
        token[...] = jnp.zeros_like(token)

    hbm = pl.BlockSpec(memory_space=pltpu.HBM)
    sem = pl.BlockSpec(memory_space=pltpu.SEMAPHORE)
    return pl.pallas_call(
        body, name=name,
        out_shape=(pltpu.SemaphoreType.DMA((3,)), pltpu.SemaphoreType.DMA((3,)), pltpu.HBM(h.shape, h.dtype),
                   pltpu.HBM(land, h.dtype), jax.ShapeDtypeStruct((8, LANES), F32)),
        in_specs=(hbm, hbm), out_specs=(sem, sem, hbm, hbm, pl.BlockSpec(memory_space=pltpu.VMEM)),
        input_output_aliases={0: 2, 1: 3},
        compiler_params=pltpu.CompilerParams(has_side_effects=pltpu.SideEffectType.DATAFLOW_SIDE_EFFECTING))(
            pltpu.with_memory_space_constraint(h, pltpu.HBM),
            pltpu.with_memory_space_constraint(lax.empty(land, h.dtype), pltpu.HBM))


def _scatter_wait(send_sems, recv_sems, h_thru, land_thru, after, name):
    def body(h_ref, land_ref, send_sems, recv_sems, after_ref, h_dead, got_ref):
        x, y, c, me, chips, sib = _place()
        for k, chip in enumerate(chips):
            cp = _rcopy(h_ref.at[2 * chip[0] + chip[1]], land_ref.at[k], send_sems.at[k], recv_sems.at[k], (*chip, c))
            cp.wait_send()
            cp.wait_recv()

    hbm = pl.BlockSpec(memory_space=pltpu.HBM)
    sem = pl.BlockSpec(memory_space=pltpu.SEMAPHORE)
    return pl.pallas_call(
        body, name=name,
        out_shape=(pltpu.HBM(h_thru.shape, h_thru.dtype), pltpu.HBM(land_thru.shape, land_thru.dtype)),
        in_specs=(hbm, hbm, sem, sem, ANY), out_specs=(hbm, hbm), input_output_aliases={0: 0, 1: 1},
        compiler_params=pltpu.CompilerParams(has_side_effects=pltpu.SideEffectType.DATAFLOW_SIDE_EFFECTING))(
            h_thru, land_thru, send_sems, recv_sems, after)


def _swap_start(g, name):
    h = g.shape[1] // 2
    land = (g.shape[0], h, g.shape[2])

    def body(g_ref, land_ref, send_sem, recv_sem, g_thru, land_thru, token):
        x, y, c, me, chips, sib = _place()
        _rcopy(g_ref.at[:, pl.ds((1 - c) * h, h), :], land_ref, send_sem.at[0], recv_sem.at[0], sib).start()
        token[...] = jnp.zeros_like(token)

    hbm = pl.BlockSpec(memory_space=pltpu.HBM)
    sem = pl.BlockSpec(memory_space=pltpu.SEMAPHORE)
    return pl.pallas_call(
        body, name=name,
        out_shape=(pltpu.SemaphoreType.DMA((1,)), pltpu.SemaphoreType.DMA((1,)), pltpu.HBM(g.shape, g.dtype),
                   pltpu.HBM(land, g.dtype), jax.ShapeDtypeStruct((8, LANES), F32)),
        in_specs=(hbm, hbm), out_specs=(sem, sem, hbm, hbm, pl.BlockSpec(memory_space=pltpu.VMEM)),
        input_output_aliases={0: 2, 1: 3},
        compiler_params=pltpu.CompilerParams(has_side_effects=pltpu.SideEffectType.DATAFLOW_SIDE_EFFECTING))(
            pltpu.with_memory_space_constraint(g, pltpu.HBM),
            pltpu.with_memory_space_constraint(lax.empty(land, g.dtype), pltpu.HBM))


def _swap_wait(send_sem, recv_sem, g_thru, land_thru, after, name):
    h = g_thru.shape[1] // 2

    def body(g_ref, land_ref, send_sem, recv_sem, after_ref, g_dead, got_ref):
        x, y, c, me, chips, sib = _place()
        cp = _rcopy(g_ref.at[:, pl.ds((1 - c) * h, h), :], land_ref, send_sem.at[0], recv_sem.at[0], sib)
        cp.wait_send()
        cp.wait_recv()

    hbm = pl.BlockSpec(memory_space=pltpu.HBM)
    sem = pl.BlockSpec(memory_space=pltpu.SEMAPHORE)
    return pl.pallas_call(
        body, name=name,
        out_shape=(pltpu.HBM(g_thru.shape, g_thru.dtype), pltpu.HBM(land_thru.shape, land_thru.dtype)),
        in_specs=(hbm, hbm, sem, sem, ANY), out_specs=(hbm, hbm), input_output_aliases={0: 0, 1: 1},
        compiler_params=pltpu.CompilerParams(has_side_effects=pltpu.SideEffectType.DATAFLOW_SIDE_EFFECTING))(
            g_thru, land_thru, send_sem, recv_sem, after)


def _share_half_start(small, name):
    rh = small.shape[0] // 2
    land = (3, rh, small.shape[1])

    def body(s_ref, land_ref, send_sems, recv_sems, s_thru, land_thru, token):
        x, y, c, me, chips, sib = _place()
        for k, chip in enumerate(chips):
            _rcopy(s_ref.at[pl.ds(c * rh, rh)], land_ref.at[k], send_sems.at[k], recv_sems.at[k], (*chip, c)).start()
        token[...] = jnp.zeros_like(token)

    hbm = pl.BlockSpec(memory_space=pltpu.HBM)
    sem = pl.BlockSpec(memory_space=pltpu.SEMAPHORE)
    return pl.pallas_call(
        body, name=name,
        out_shape=(pltpu.SemaphoreType.DMA((3,)), pltpu.SemaphoreType.DMA((3,)), pltpu.HBM(small.shape, small.dtype),
                   pltpu.HBM(land, small.dtype), jax.ShapeDtypeStruct((8, LANES), F32)),
        in_specs=(hbm, hbm), out_specs=(sem, sem, hbm, hbm, pl.BlockSpec(memory_space=pltpu.VMEM)),
        input_output_aliases={0: 2, 1: 3},
        compiler_params=pltpu.CompilerParams(has_side_effects=pltpu.SideEffectType.DATAFLOW_SIDE_EFFECTING))(
            pltpu.with_memory_space_constraint(small, pltpu.HBM),
            pltpu.with_memory_space_constraint(lax.empty(land, small.dtype), pltpu.HBM))


def _share_half_wait(send_sems, recv_sems, s_thru, land_thru, after, name):
    rh = s_thru.shape[0] // 2

    def body(s_ref, land_ref, send_sems, recv_sems, after_ref, s_dead, got_ref):
        x, y, c, me, chips, sib = _place()
        for k, chip in enumerate(chips):
            cp = _rcopy(s_ref.at[pl.ds(c * rh, rh)], land_ref.at[k], send_sems.at[k], recv_sems.at[k], (*chip, c))
            cp.wait_send()
            cp.wait_recv()

    hbm = pl.BlockSpec(memory_space=pltpu.HBM)
    sem = pl.BlockSpec(memory_space=pltpu.SEMAPHORE)
    return pl.pallas_call(
        body, name=name,
        out_shape=(pltpu.HBM(s_thru.shape, s_thru.dtype), pltpu.HBM(land_thru.shape, land_thru.dtype)),
        in_specs=(hbm, hbm, sem, sem, ANY), out_specs=(hbm, hbm), input_output_aliases={0: 0, 1: 1},
        compiler_params=pltpu.CompilerParams(has_side_effects=pltpu.SideEffectType.DATAFLOW_SIDE_EFFECTING))(
            s_thru, land_thru, send_sems, recv_sems, after)


def _join_start(parts, name):
    n = len(parts)

    def body(*refs):
        srcs, (send_sems, recv_sems), token = refs[:n], refs[n:n + 2], refs[-1]
        x, y, c, me, chips, sib = _place()
        for a, src in enumerate(srcs):
            h = src.shape[0] // 2
            mine = src.at[pl.ds(c * h, h)]
            _rcopy(mine, mine, send_sems.at[a], recv_sems.at[a], sib).start()
        token[...] = jnp.zeros_like(token)

    hbm = pl.BlockSpec(memory_space=pltpu.HBM)
    sem = pl.BlockSpec(memory_space=pltpu.SEMAPHORE)
    outs = pl.pallas_call(
        body, name=name,
        out_shape=(pltpu.SemaphoreType.DMA((n,)), pltpu.SemaphoreType.DMA((n,)))
        + tuple(pltpu.HBM(p.shape, p.dtype) for p in parts) + (jax.ShapeDtypeStruct((8, LANES), F32),),
        in_specs=(hbm,) * n, out_specs=(sem, sem) + (hbm,) * n + (pl.BlockSpec(memory_space=pltpu.VMEM),),
        input_output_aliases={a: 2 + a for a in range(n)},
        compiler_params=pltpu.CompilerParams(has_side_effects=pltpu.SideEffectType.DATAFLOW_SIDE_EFFECTING))(
            *[pltpu.with_memory_space_constraint(p, pltpu.HBM) for p in parts])
    return outs[0], outs[1], list(outs[2:2 + n]), outs[-1]


def _join_wait(send_sems, recv_sems, parts, after, name):
    n = len(parts)

    def body(*refs):
        srcs, (send_sems, recv_sems) = refs[:n], refs[n:n + 2]
        x, y, c, me, chips, sib = _place()
        for a, src in enumerate(srcs):
            h = src.shape[0] // 2
            mine, theirs = src.at[pl.ds(c * h, h)], src.at[pl.ds((1 - c) * h, h)]
            _rcopy(mine, theirs, send_sems.at[a], recv_sems.at[a], sib).wait_send()
            _rcopy(theirs, theirs, send_sems.at[a], recv_sems.at[a], sib).wait_recv()

    hbm = pl.BlockSpec(memory_space=pltpu.HBM)
    sem = pl.BlockSpec(memory_space=pltpu.SEMAPHORE)
    return pl.pallas_call(
        body, name=name, out_shape=tuple(pltpu.HBM(p.shape, p.dtype) for p in parts),
        in_specs=(hbm,) * n + (sem, sem, ANY), out_specs=(hbm,) * n, input_output_aliases={a: a for a in range(n)},
        compiler_params=pltpu.CompilerParams(has_side_effects=pltpu.SideEffectType.DATAFLOW_SIDE_EFFECTING))(
            *parts, send_sems, recv_sems, after)


def _pad_rows(a, rows):
    return jnp.pad(a, ((0, rows - a.shape[0]), (0, 0)))


def _stack_rows(parts, multiple):
    padded = [_pad_rows(p, -(-p.shape[0] // 8) * 8) for p in parts]
    starts, at = [], 0
    for p in padded:
        starts.append(at)
        at += p.shape[0]
    total = -(-at // multiple) * multiple
    if total > at:
        padded.append(jnp.zeros((total - at, parts[0].shape[1]), parts[0].dtype))
    return jnp.concatenate(padded, axis=0), starts


def kernel(x, ln_pre_even, w_in_even, pool_w, pool_scale, w_out_even, ln_post_even, ln_pre_odd, w_in_odd, sconv_w, dconv_w, dconv_b, cnorm_g, cnorm_b, w_out_odd, ln_post_odd, loss_target, m_ln_pre_even, m_w_in_even, m_pool_w, m_pool_scale, m_w_out_even, m_ln_post_even, m_ln_pre_odd, m_w_in_odd, m_sconv_w, m_dconv_w, m_dconv_b, m_cnorm_g, m_cnorm_b, m_w_out_odd, m_ln_post_odd, v_ln_pre_even, v_w_in_even, v_pool_w, v_pool_scale, v_w_out_even, v_ln_post_even, v_ln_pre_odd, v_w_in_odd, v_sconv_w, v_dconv_w, v_dconv_b, v_cnorm_g, v_cnorm_b, v_w_out_odd, v_ln_post_odd):
    _, s, d = x.shape
    half = d // 2
    cw = half // N_CHIPS
    ng, q, gd = pool_w.shape[1:]
    k3, k31 = sconv_w.shape[1], dconv_w.shape[1]
    x2d, tgt = x[0], loss_target[0]
    me = 2 * lax.axis_index("x") + lax.axis_index("y")
    core = lax.axis_index("c")
    c_arr = jnp.reshape(core, (1,)).astype(jnp.int32)
    me_arr = jnp.reshape(me, (1,)).astype(jnp.int32)
    mc_arr = jnp.stack([me, core]).astype(jnp.int32)

    shards = [w_in_even[0], w_out_even[0], w_in_odd[0], w_out_odd[0]]
    pool_w_b = _cast_bf16(pool_w[0].reshape(ng * q, gd), "cast_pool_w").reshape(ng, q, gd)
    pack_w, at_w = _stack_rows([sconv_w[0], dconv_w[0], dconv_b, cnorm_g, cnorm_b], 8)
    pack_d, at_d = _stack_rows([ln_pre_odd, ln_post_odd], 8)
    placed = [lax.dynamic_update_slice(jnp.zeros((ng, N_CHIPS * q, gd), BF16), pool_w_b, (0, me * q, 0)),
              lax.dynamic_update_slice(jnp.zeros((pack_w.shape[0], N_CHIPS * cw), F32), pack_w, (0, me * cw)),
              lax.dynamic_update_slice(jnp.zeros((pack_d.shape[0], d), F32), pack_d, (0, me * (d // N_CHIPS)))]
    plans = _Multi([_GatherPieces([_cast_bf16_own_slab(shards[0], me_arr, "cast_w0")], GATHER_PIECES, (0.3, 0.9)),
                    _SmallGatherPlan(placed, (q, cw, d // N_CHIPS))])
    h0, others, extra = _prep(x2d, ln_pre_even, shards[1:], me_arr, "prep_and_gather_first", plans)
    (win_e,), (pool_w_f, pack_w_f, pack_d_f) = plans.results(extra)
    slabs = [None] + others
    sconv_f = pack_w_f[at_w[0]:at_w[0] + k3]
    dconv_f = pack_w_f[at_w[1]:at_w[1] + k31]
    dconv_b_f, cnorm_g_f, cnorm_b_f = (pack_w_f[at_w[n]:at_w[n] + 1] for n in (2, 3, 4))
    ln_pre_odd_f = pack_d_f[at_d[0]:at_d[0] + 1]
    ln_post_odd_f = pack_d_f[at_d[1]:at_d[1] + 1]

    plans = _Multi([_GatherPlan([slabs[1]], at=(0.6, 0.88)), _GatherPlan([slabs[2]], (0, 3, 8), at=(0.6, 0.88))])
    p_e, extra = _mm_nn(h0, win_e, "proj_in_even", plans)
    (wout_e,), (win_o,) = plans.results(extra)
    wout_e = wout_e.reshape(d, d)
    att, ltot, (win_o,) = _sba_fwd(p_e, half, "sba_fwd", _GatherPlan([win_o], (3, 8, 8), at=(0.69, 0.94)))
    y_e = _even_mix_fwd(p_e, att, pool_w_f, pool_scale, d, "even_mix_fwd")
    o_e, x1, h1 = _mm_out_even(y_e, wout_e, x2d, ln_post_even, ln_pre_odd_f, "proj_out_even")
    p_o, (wout_o,) = _mm_nn(h1, win_o, "proj_in_odd", _GatherPlan([slabs[3]]))
    wout_o = wout_o.reshape(d, d)
    y_o, s3, d1 = _odd_mix_fwd(p_o, sconv_f, dconv_f, dconv_b_f, cnorm_g_f, cnorm_b_f, d, "odd_mix_fwd")
    do_o, dx2, loss_blk, dln_post_odd = _mm_out_odd(y_o, wout_o, x1, ln_post_odd_f, tgt, "proj_out_odd_loss")

    dy_o = _mm_nt(do_o, wout_o, "dy_odd")
    g_wout_o = _mm_tn(y_o, do_o, 1, "dw_out_odd")[0].reshape(N_CHIPS, d // N_CHIPS, d)
    (dbc, dgate_o, ds3, dd1, dcnorm_g, dcnorm_b, ddconv_b), (got,) = _odd_bwd_rows(
        p_o, s3, d1, dy_o, cnorm_g_f, cnorm_b_f, d, "odd_bwd_rows", _SwapPlan([g_wout_o]))
    h_wout_o = _half_add(g_wout_o, got, c_arr, "half_add_out_odd")
    dhc, dcc, dga, dgb, dsconv, ddconv = _odd_bwd_conv(p_o, ds3, dd1, sconv_f, dconv_f, d, "odd_bwd_conv")
    dp_o = jnp.concatenate([dhc, dbc, dcc, dga, dgb, dgate_o], axis=1)
    g_win_o, (s_wout_o,) = _mm_tn(h1, dp_o, N_CHIPS, "dw_in_odd", _ScatterPlan([h_wout_o]))
    (dx1, dln_pre_odd, do_e, dln_post_even), (got,) = _mm_in_bwd(
        dp_o, win_o, x1, ln_pre_odd_f, dx2, (o_e, ln_post_even), "dx_odd", _SwapPlan([g_win_o]))
    h_win_o = _half_add(g_win_o, got, c_arr, "half_add_in_odd")

    dy_e = _mm_nt(do_e, wout_e, "dy_even")
    g_wout_e = _mm_tn(y_e, do_e, 1, "dw_out_even")[0].reshape(N_CHIPS, d // N_CHIPS, d)
    (datt, du, dgate_e, dpool_scale, dpool_w), (got,) = _even_mix_bwd(
        p_e, att, dy_e, pool_w_f, pool_scale, d, "even_mix_bwd", _SwapPlan([g_wout_e]))
    h_wout_e = _half_add(g_wout_e, got, c_arr, "half_add_out_even")
    two = lambda v: v.reshape(2, half)
    small_parts = [dpool_scale, two(dln_post_even), two(dln_pre_odd), two(dln_post_odd),
                   dsconv, ddconv, ddconv_b, dcnorm_g, dcnorm_b, dpool_w.reshape(gd, half)]
    small, at_s = _stack_rows(small_parts, 16)
    plans = _Multi([_ScatterPlan([h_win_o]), _SendWholePlan([small])])
    dq, dk, dv, extra = _sba_bwd(p_e, ltot, datt, half, "sba_bwd", plans)
    (s_win_o,), (small1,) = plans.results(extra)
    small2 = _add2(small, small1, "small_add")
    dp_e = jnp.concatenate([dq, dk, dv, du, dgate_e], axis=1)
    plans = _Multi([_ScatterPlan([h_wout_e]), _ShareHalfPlan([small2])])
    g_win_e, extra = _mm_tn(h0, dp_e, N_CHIPS, "dw_in_even", plans)
    (s_wout_e,), (small_got,) = plans.results(extra)
    swap = _swap_start(g_win_e, "swap_in_even_start")
    pairs = [(h_wout_e, s_wout_e), (h_win_o, s_win_o), (h_wout_o, s_wout_o)]
    parts = []
    for n, (h, r) in enumerate(pairs):
        parts.append(_sum_chips(h, r, mc_arr, f"sum_chips{n + 1}", after=parts[-1] if parts else swap[4]))
    g_win_e, got = _swap_wait(*swap[:4], parts[-1], "swap_in_even_wait")
    parts.append(_sum_chips_ordered(small2, small_got, mc_arr, "small_sum"))
    join_sems = _join_start(parts, "join_first_start")
    h_win_e = _half_add(g_win_e, got, c_arr, "half_add_in_even", after=join_sems[3])
    send_sems, recv_sems, h_win_e, landing, token = _scatter_start(h_win_e, "scatter_in_even_start")
    (grad_x, dln_pre_even), _ = _mm_in_bwd(dp_e, win_e, x2d, ln_pre_even + token[0:1, 0:1], dx1, None, "dx_even")

    last, at_l = _stack_rows([two(dln_pre_even), jnp.pad(loss_blk[0:1], ((0, 0), (0, half - LANES)))], 16)
    (last1,) = _swap_with_sibling([], [last], "swap_last")
    last2 = _add2(last, last1, "last_add")
    share = _share_half_start(last2, "share_last_start")
    gw_out_e, gw_in_o, gw_out_o, red = _join_wait(*join_sems[:3], share[4], "join_first_wait")

    def rows(n, cnt):
        return red[at_s[n]:at_s[n] + cnt]

    def mine(a, width):
        return lax.dynamic_slice_in_dim(a, me * width, width, axis=1)

    quarter = d // N_CHIPS
    g_small = {
        "pool_scale": rows(0, 1),
        "ln_post_even": rows(1, 2).reshape(1, d),
        "ln_pre_odd": mine(rows(2, 2).reshape(1, d), quarter),
        "ln_post_odd": mine(rows(3, 2).reshape(1, d), quarter),
        "sconv_w": mine(rows(4, k3), cw),
        "dconv_w": mine(rows(5, k31), cw),
        "dconv_b": mine(rows(6, 1), cw),
        "cnorm_g": mine(rows(7, 1), cw),
        "cnorm_b": mine(rows(8, 1), cw),
        "pool_w": lax.dynamic_slice_in_dim(rows(9, gd).reshape(ng, gd, gd), me * q, q, axis=1).reshape(ng * q, gd),
    }
    w2d = {
        "ln_pre_even": ln_pre_even, "w_in_even": w_in_even[0], "pool_w": pool_w[0].reshape(ng * q, gd),
        "pool_scale": pool_scale, "w_out_even": w_out_even[0], "ln_post_even": ln_post_even, "ln_pre_odd": ln_pre_odd,
        "w_in_odd": w_in_odd[0], "sconv_w": sconv_w[0], "dconv_w": dconv_w[0], "dconv_b": dconv_b, "cnorm_g": cnorm_g,
        "cnorm_b": cnorm_b, "w_out_odd": w_out_odd[0], "ln_post_odd": ln_post_odd,
    }
    moments = {
        "ln_pre_even": (m_ln_pre_even, v_ln_pre_even), "w_in_even": (m_w_in_even, v_w_in_even),
        "pool_w": (m_pool_w, v_pool_w), "pool_scale": (m_pool_scale, v_pool_scale),
        "w_out_even": (m_w_out_even, v_w_out_even), "ln_post_even": (m_ln_post_even, v_ln_post_even),
        "ln_pre_odd": (m_ln_pre_odd, v_ln_pre_odd), "w_in_odd": (m_w_in_odd, v_w_in_odd),
        "sconv_w": (m_sconv_w, v_sconv_w), "dconv_w": (m_dconv_w, v_dconv_w), "dconv_b": (m_dconv_b, v_dconv_b),
        "cnorm_g": (m_cnorm_g, v_cnorm_g), "cnorm_b": (m_cnorm_b, v_cnorm_b),
        "w_out_odd": (m_w_out_odd, v_w_out_odd), "ln_post_odd": (m_ln_post_odd, v_ln_post_odd),
    }
    def update(name, g):
        m_in, v_in = moments[name]
        w = w2d[name]
        return _adamw(w, g, m_in.reshape(w.shape), v_in.reshape(w.shape), "adamw_" + name)[0]

    updates = {name: update(name, g) for name, g in (("w_in_odd", gw_in_o), ("w_out_even", gw_out_e), ("w_out_odd", gw_out_o))}
    last2, last_got = _share_half_wait(*share[:4], updates["w_out_odd"][1], "share_last_wait")
    last_sum = _sum_chips_ordered(last2, last_got, mc_arr, "last_sum")
    h_win_e, s_win_e = _scatter_wait(send_sems, recv_sems, h_win_e, landing, last_sum, "scatter_in_even_wait")
    last_sems = _join_start([_sum_chips(h_win_e, s_win_e, mc_arr, "sum_chips0"), last_sum], "join_last_start")
    for name, g in g_small.items():
        updates[name] = update(name, g)
    gw_in_e, red_last = _join_wait(*last_sems[:3], updates["pool_w"][1], "join_last_wait")
    loss = red_last[at_l[1], 0]
    updates["ln_pre_even"] = update("ln_pre_even", red_last[at_l[0]:at_l[0] + 2].reshape(1, d))
    updates["w_in_even"] = update("w_in_even", gw_in_e)
    outs = [[u.reshape(moments[name][0].shape) for u in updates[name]] for name in w2d]
    grads_out, deltas, new_m, new_v = zip(*outs)
    return (loss, grad_x.reshape(x.shape), *grads_out, *deltas, *new_m, *new_v)
```

```python
import functools
import math

import jax
import jax.numpy as jnp
from jax import lax
from jax.experimental import pallas as pl
from jax.experimental.pallas import tpu as pltpu

F32 = jnp.float32
BF16 = jnp.bfloat16
EPS = 1e-6
N_CHIPS = 4
VMEM_LIMIT_V7X = 56 << 20
HEAD_DIM = 128
ATT_BLOCK = 256
POOL_WINDOWS = (2, 4, 8, 16)
ROW_TILE = 256
POOL_HALO = 16
CONV_HALO = 32
LANES = 128
ADAM_LR, ADAM_B1, ADAM_B2, ADAM_EPS, ADAM_WD, ADAM_STEP = 0.001, 0.9, 0.999, 1e-08, 0.01, 10
MESH_ID = pl.DeviceIdType.MESH
ANY = pl.BlockSpec(memory_space=pl.ANY)


def _cp(*sem):
    return pltpu.CompilerParams(dimension_semantics=sem or None, vmem_limit_bytes=VMEM_LIMIT_V7X)


def _pick_tile(n, cap):
    best = None
    for t in range(LANES, min(n, cap) + 1, LANES):
        if n % t == 0:
            best = t
    assert best is not None, (n, cap)
    return best


def _sigmoid(x):
    return 1.0 / (1.0 + jnp.exp(-x))


def _silu(x):
    return x * _sigmoid(x)


def _dsilu(x):
    s = _sigmoid(x)
    return s * (1.0 + x * (1.0 - s))


def _log_sigmoid(z):
    return jnp.minimum(z, 0.0) - jnp.log(1.0 + jnp.exp(-jnp.abs(z)))


def _rms_stats(x):
    r = lax.rsqrt(jnp.mean(x * x, axis=-1, keepdims=True) + EPS)
    return x * r, r


def _rms_bwd(dh, xhat, r, g):
    dxh = dh * g
    dx = r * (dxh - xhat * jnp.mean(dxh * xhat, axis=-1, keepdims=True))
    return dx, jnp.sum(dh * xhat, axis=0, keepdims=True)


def _acc_rows(ref, first, val):
    @pl.when(first)
    def _():
        ref[...] = val

    @pl.when(jnp.logical_not(first))
    def _():
        ref[...] += val


def _rcopy(src, dst, ssem, rsem, dev):
    return pltpu.make_async_remote_copy(src_ref=src, dst_ref=dst, send_sem=ssem, recv_sem=rsem,
                                        device_id=dev, device_id_type=MESH_ID)


def _place():
    x, y, c = lax.axis_index("x"), lax.axis_index("y"), lax.axis_index("c")
    chips = [(1 - x, y), (x, 1 - y), (1 - x, 1 - y)]
    return x, y, c, 2 * x + y, chips, (x, y, 1 - c)


class _GatherPlan:
    PER_ARRAY = 7

    def __init__(self, arrays, part=(0, 1, 1), at=(0.5, 0.8)):
        self.operands = list(arrays)
        self.out_shapes = [jax.ShapeDtypeStruct(a.shape, a.dtype) for a in arrays]
        self.aliases = {i: i for i in range(len(arrays))}
        self.nsems = self.PER_ARRAY * len(arrays)
        self.base = 0
        self.halves = [a.shape[1] // 2 for a in arrays]
        self.part = part
        self.at = at

    def schedule(self):
        return [(0.0, self.start), (self.at[0], self.relay), (self.at[1], self.relay_far)]

    def _rows(self, ref, a, chip, half, quarter=None):
        lo, hi, n = self.part
        h = self.halves[a]
        first, size = half * h + lo * h // n, (hi - lo) * h // n
        if quarter is not None:
            first, size = first + quarter * (size // 2), size // 2
        return ref.at[chip, pl.ds(first, size)]

    def _copy(self, src, dst, a, n, ssem, rsem, dev):
        return _rcopy(src, dst, ssem.at[self.base + self.PER_ARRAY * a + n], rsem.at[self.base + self.PER_ARRAY * a + n], dev)

    def _own(self, ins, outs, ssem, rsem):
        x, y, c, me, chips, sib = _place()
        return [self._copy(self._rows(ins[a], a, me, c), self._rows(outs[a], a, me, c), a, k, ssem, rsem, (*chips[k], c))
                for a in range(len(ins)) for k in (0, 1)]

    def _relays(self, outs, ssem, rsem, a, k):
        x, y, c, me, chips, sib = _place()
        chip = 2 * chips[k][0] + chips[k][1]
        whole, quarter = self._rows(outs[a], a, chip, c), self._rows(outs[a], a, chip, c, k)
        return (self._copy(whole, whole, a, k, ssem, rsem, (*chips[k], c)),
                self._copy(quarter, quarter, a, 2 + k, ssem, rsem, (*chips[1 - k], c)),
                self._copy(whole, whole, a, 4 + k, ssem, rsem, sib))

    def _far(self, outs, ssem, rsem, a):
        x, y, c, me, chips, sib = _place()
        chip = 2 * chips[2][0] + chips[2][1]
        whole = self._rows(outs[a], a, chip, c)
        got = [self._copy(q, q, a, 2 + k, ssem, rsem, (*chips[1 - k], c))
               for k, q in enumerate([self._rows(outs[a], a, chip, c, 0), self._rows(outs[a], a, chip, c, 1)])]
        return got, self._copy(whole, whole, a, 6, ssem, rsem, sib)

    def start(self, ins, outs, ssem, rsem):
        for cp in self._own(ins, outs, ssem, rsem):
            cp.start()

    def relay(self, ins, outs, ssem, rsem):
        for a in range(len(outs)):
            for k in (0, 1):
                landed, onward, to_sibling = self._relays(outs, ssem, rsem, a, k)
                landed.wait_recv()
                onward.start()
                to_sibling.start()

    def relay_far(self, ins, outs, ssem, rsem):
        for a in range(len(outs)):
            got, to_sibling = self._far(outs, ssem, rsem, a)
            for cp in got:
                cp.wait_recv()
            to_sibling.start()

    def finish(self, ins, outs, ssem, rsem):
        x, y, c, me, chips, sib = _place()
        for a in range(len(outs)):
            for k in range(3):
                ref = self._rows(outs[a], a, 2 * chips[k][0] + chips[k][1], 1 - c)
                self._copy(ref, ref, a, 4 + k, ssem, rsem, sib).wait_recv()
        for cp in self._own(ins, outs, ssem, rsem):
            cp.wait_send()
        for a in range(len(outs)):
            for k in (0, 1):
                _, onward, to_sibling = self._relays(outs, ssem, rsem, a, k)
                onward.wait_send()
                to_sibling.wait_send()
            self._far(outs, ssem, rsem, a)[1].wait_send()


class _ScatterPlan:
    def __init__(self, arrays, part=(0, 1, 1), into=None):
        self.n = len(arrays)
        self.operands = list(arrays) + list(into or [])
        self.out_shapes = [jax.ShapeDtypeStruct((3,) + a.shape[1:], a.dtype) for a in arrays]
        self.aliases = {self.n + i: i for i in range(self.n)} if into else {}
        self.nsems = 3 * self.n
        self.base = 0
        self.part = part

    def _copies(self, ins, outs, ssem, rsem):
        x, y, c, me, chips, sib = _place()
        lo, hi, n = self.part
        out = []
        for a in range(self.n):
            h = ins[a].shape[1]
            rows = pl.ds(lo * h // n, (hi - lo) * h // n)
            for k, chip in enumerate(chips):
                out.append(_rcopy(ins[a].at[2 * chip[0] + chip[1], rows], outs[a].at[k, rows],
                                  ssem.at[self.base + 3 * a + k], rsem.at[self.base + 3 * a + k], (*chip, c)))
        return out

    def schedule(self):
        return [(0.0, self.start)]

    def start(self, ins, outs, ssem, rsem):
        for cp in self._copies(ins, outs, ssem, rsem):
            cp.start()

    def finish(self, ins, outs, ssem, rsem):
        cps = self._copies(ins, outs, ssem, rsem)
        for cp in cps:
            cp.wait_recv()
        for cp in cps:
            cp.wait_send()


class _ShareHalfPlan(_ScatterPlan):
    def __init__(self, arrays):
        super().__init__(arrays)
        self.out_shapes = [jax.ShapeDtypeStruct((3, a.shape[0] // 2, a.shape[1]), a.dtype) for a in arrays]

    def _copies(self, ins, outs, ssem, rsem):
        x, y, c, me, chips, sib = _place()
        out = []
        for a in range(self.n):
            rh = ins[a].shape[0] // 2
            for k, chip in enumerate(chips):
                out.append(_rcopy(ins[a].at[pl.ds(c * rh, rh)], outs[a].at[k],
                                  ssem.at[self.base + 3 * a + k], rsem.at[self.base + 3 * a + k], (*chip, c)))
        return out


class _SwapPlan:
    def __init__(self, grads):
        self.operands = list(grads)
        self.out_shapes = [jax.ShapeDtypeStruct((g.shape[0], g.shape[1] // 2, g.shape[2]), g.dtype) for g in grads]
        self.aliases = {}
        self.nsems = len(grads)
        self.base = 0

    def _copies(self, ins, outs, ssem, rsem):
        x, y, c, me, chips, sib = _place()
        out = []
        for a, src in enumerate(ins):
            h = src.shape[1] // 2
            out.append(_rcopy(src.at[:, pl.ds((1 - c) * h, h), :], outs[a], ssem.at[self.base + a], rsem.at[self.base + a], sib))
        return out

    def schedule(self):
        return [(0.0, self.start)]

    def start(self, ins, outs, ssem, rsem):
        for cp in self._copies(ins, outs, ssem, rsem):
            cp.start()

    def finish(self, ins, outs, ssem, rsem):
        cps = self._copies(ins, outs, ssem, rsem)
        for cp in cps:
            cp.wait_recv()
        for cp in cps:
            cp.wait_send()


class _SendWholePlan(_SwapPlan):
    def __init__(self, arrays):
        self.operands = list(arrays)
        self.out_shapes = [jax.ShapeDtypeStruct(a.shape, a.dtype) for a in arrays]
        self.aliases = {}
        self.nsems = len(arrays)
        self.base = 0

    def _copies(self, ins, outs, ssem, rsem):
        x, y, c, me, chips, sib = _place()
        return [_rcopy(src, outs[a], ssem.at[self.base + a], rsem.at[self.base + a], sib) for a, src in enumerate(ins)]


class _GatherPieces:
    def __init__(self, arrays, n, at):
        self.pieces = [_GatherPlan(arrays, (j, j + 1, n), at) for j in range(n)]
        self.operands, self.out_shapes, self.aliases = self.pieces[0].operands, self.pieces[0].out_shapes, self.pieces[0].aliases
        self.nsems = sum(p.nsems for p in self.pieces)
        self.at = at
        self.base = 0

    @property
    def base(self):
        return self.pieces[0].base

    @base.setter
    def base(self, value):
        for j, p in enumerate(self.pieces):
            p.base = value + j * p.nsems

    def schedule(self):
        return [(0.0, self.start), (self.at[0], self.relay), (self.at[1], self.relay_far)]

    def _each(self, what, *a):
        for p in self.pieces:
            getattr(p, what)(*a)

    def start(self, *a):
        self._each("start", *a)

    def relay(self, *a):
        self._each("relay", *a)

    def relay_far(self, *a):
        self._each("relay_far", *a)

    def finish(self, *a):
        self._each("finish", *a)


class _SmallGatherPlan:
    def __init__(self, arrays, widths):
        self.operands = list(arrays)
        self.out_shapes = [jax.ShapeDtypeStruct(a.shape, a.dtype) for a in arrays]
        self.aliases = {i: i for i in range(3)}
        self.nsems = 9
        self.base = 0
        self.widths = widths

    def _part(self, ref, n, chip):
        w = self.widths[n]
        return ref.at[:, pl.ds(chip * w, w), :] if n == 0 else ref.at[:, pl.ds(chip * w, w)]

    def _copies(self, ins, outs, ssem, rsem, own):
        x, y, c, me, chips, sib = _place()
        out = []
        for n in range(3):
            for k, chip in enumerate(chips):
                which = me if own else 2 * chip[0] + chip[1]
                out.append(_rcopy(self._part(ins[n], n, which), self._part(outs[n], n, which),
                                  ssem.at[self.base + 3 * n + k], rsem.at[self.base + 3 * n + k], (*chip, c)))
        return out

    def schedule(self):
        return [(0.0, self.start)]

    def start(self, ins, outs, ssem, rsem):
        for cp in self._copies(ins, outs, ssem, rsem, True):
            cp.start()

    def finish(self, ins, outs, ssem, rsem):
        for cp in self._copies(ins, outs, ssem, rsem, False):
            cp.wait_recv()
        for cp in self._copies(ins, outs, ssem, rsem, True):
            cp.wait_send()


class _Multi:
    def __init__(self, plans):
        self.plans = plans
        self.operands, self.out_shapes, self.aliases, self.nsems = [], [], {}, 0
        self.spans = []
        for p in plans:
            ni, no = len(self.operands), len(self.out_shapes)
            self.spans.append((ni, ni + len(p.operands), no, no + len(p.out_shapes)))
            self.aliases.update({ni + i: no + j for i, j in p.aliases.items()})
            p.base = self.nsems
            self.nsems += p.nsems
            self.operands += p.operands
            self.out_shapes += p.out_shapes

    def schedule(self):
        def bound(fn, span):
            i0, i1, o0, o1 = span
            return lambda ins, outs, ssem, rsem: fn(ins[i0:i1], outs[o0:o1], ssem, rsem)

        stages = [(at, bound(fn, span)) for p, span in zip(self.plans, self.spans) for at, fn in p.schedule()]
        return sorted(stages, key=lambda s: s[0])

    def finish(self, ins, outs, ssem, rsem):
        for p, (i0, i1, o0, o1) in zip(self.plans, self.spans):
            p.finish(ins[i0:i1], outs[o0:o1], ssem, rsem)

    def results(self, extra):
        return [list(extra[o0:o1]) for (_, _, o0, o1) in self.spans]


class _Host:
    def __init__(self, comm, in_specs, out_specs, out_shape, scratch, prefetch=0):
        self.comm = comm
        self.n_in, self.n_out = len(in_specs), len(out_specs)
        self.in_specs, self.out_specs, self.out_shape, self.scratch = list(in_specs), list(out_specs), list(out_shape), list(scratch)
        self.aliases = {}
        self.args = []
        if comm is not None:
            self.in_specs += [ANY] * len(comm.operands)
            self.out_specs += [ANY] * len(comm.out_shapes)
            self.out_shape += comm.out_shapes
            self.scratch += [pltpu.SemaphoreType.DMA((comm.nsems,)), pltpu.SemaphoreType.DMA((comm.nsems,))]
            self.aliases = {prefetch + self.n_in + i: self.n_out + j for i, j in comm.aliases.items()}
            self.args = list(comm.operands)

    def split(self, refs):
        nc = len(self.args)
        nco = len(self.out_shape) - self.n_out
        ins, p = refs[:self.n_in], self.n_in + nc
        outs, rest = refs[p:p + self.n_out], refs[p + self.n_out + nco:]
        self._cargs = None
        if self.comm is not None:
            self._cargs = (refs[self.n_in:p], refs[p + self.n_out:p + self.n_out + nco], rest[-2], rest[-1])
            rest = rest[:-2]
        return ins, outs, rest

    def before(self, step, total):
        if self.comm is None:
            return

        for at, stage in self.comm.schedule():
            pl.when(step == min(total - 1, int(at * total)))(functools.partial(stage, *self._cargs))

    def after(self, step, total):
        if self.comm is None:
            return

        @pl.when(step == total - 1)
        def _():
            self.comm.finish(*self._cargs)

    def results(self, outs):
        return outs[:self.n_out], outs[self.n_out:]


def _cast_bf16(x, name):
    r, c = x.shape
    tr = ROW_TILE if r % ROW_TILE == 0 else r

    def body(x_ref, o_ref):
        o_ref[...] = x_ref[...].astype(BF16)

    return pl.pallas_call(
        body, name=name, grid=(r // tr,),
        in_specs=[pl.BlockSpec((tr, c), lambda i: (i, 0))],
        out_specs=pl.BlockSpec((tr, c), lambda i: (i, 0)),
        out_shape=jax.ShapeDtypeStruct((r, c), BF16), compiler_params=_cp("parallel"))(x)


def _cast_bf16_own_slab(x, me_arr, name):
    r, c = x.shape
    tr = ROW_TILE if r % ROW_TILE == 0 else r

    def body(me_ref, x_ref, o_ref):
        o_ref[...] = x_ref[...].astype(BF16)

    return pl.pallas_call(
        body, name=name,
        grid_spec=pltpu.PrefetchScalarGridSpec(
            num_scalar_prefetch=1, grid=(r // tr,),
            in_specs=[pl.BlockSpec((tr, c), lambda i, me: (i, 0))],
            out_specs=pl.BlockSpec((None, tr, c), lambda i, me: (me[0], i, 0))),
        out_shape=jax.ShapeDtypeStruct((N_CHIPS, r, c), BF16), compiler_params=_cp("parallel"))(me_arr, x)


def _prep(x, g, shards, me_arr, name, comm):
    s, d = x.shape
    steps = s // ROW_TILE
    tiles = [(w.shape[0] // steps, w.shape[1]) for w in shards]
    assert all(w.shape[0] % steps == 0 for w in shards)
    in_specs = [pl.BlockSpec((ROW_TILE, d), lambda i, me: (i, 0)), pl.BlockSpec((1, d), lambda i, me: (0, 0))]
    in_specs += [pl.BlockSpec(t, lambda i, me: (i, 0)) for t in tiles]
    out_specs = [pl.BlockSpec((ROW_TILE, d), lambda i, me: (i, 0))]
    out_specs += [pl.BlockSpec((None,) + t, lambda i, me: (me[0], i, 0)) for t in tiles]
    out_shape = [jax.ShapeDtypeStruct((s, d), BF16)] + [jax.ShapeDtypeStruct((N_CHIPS,) + w.shape, BF16) for w in shards]
    host = _Host(comm, in_specs, out_specs, out_shape, [], prefetch=1)

    def body(me_ref, *refs):
        (x_ref, g_ref, *w_refs), (h_ref, *slab_refs), _ = host.split(refs)
        step = pl.program_id(0)
        host.before(step, steps)
        xhat, _ = _rms_stats(x_ref[...])
        h_ref[...] = (xhat * g_ref[...]).astype(BF16)
        for w_ref, slab_ref in zip(w_refs, slab_refs):
            slab_ref[...] = w_ref[...].astype(BF16)
        host.after(step, steps)

    outs = pl.pallas_call(
        body, name=name,
        grid_spec=pltpu.PrefetchScalarGridSpec(num_scalar_prefetch=1, grid=(steps,), in_specs=host.in_specs,
                                               out_specs=host.out_specs, scratch_shapes=host.scratch),
        out_shape=host.out_shape, input_output_aliases=host.aliases,
        compiler_params=_cp("arbitrary"))(me_arr, x, g, *shards, *host.args)
    (h, *slabs), extra = host.results(outs)
    return h, slabs, extra


def _mm_nn(a, w3, name, comm=None):
    m, k = a.shape
    nsh, _, ns = w3.shape
    tm = 512 if m % 512 == 0 else ROW_TILE
    tn = _pick_tile(ns, 1024)
    per = ns // tn
    grid = (nsh * per, m // tm)
    host = _Host(comm,
                 [pl.BlockSpec((tm, k), lambda n, i: (i, 0)), pl.BlockSpec((None, k, tn), lambda n, i: (n // per, 0, n % per))],
                 [pl.BlockSpec((tm, tn), lambda n, i: (i, n))], [jax.ShapeDtypeStruct((m, nsh * ns), F32)], [])

    def body(*refs):
        (a_ref, w_ref), (o_ref,), _ = host.split(refs)
        step = pl.program_id(0) * grid[1] + pl.program_id(1)
        host.before(step, grid[0] * grid[1])
        o_ref[...] = jnp.dot(a_ref[...], w_ref[...], preferred_element_type=F32)
        host.after(step, grid[0] * grid[1])

    outs = pl.pallas_call(
        body, name=name, grid=grid, in_specs=host.in_specs, out_specs=host.out_specs, out_shape=host.out_shape,
        scratch_shapes=host.scratch, input_output_aliases=host.aliases,
        compiler_params=_cp("arbitrary", "arbitrary"))(a, w3, *host.args)
    (out,), extra = host.results(outs)
    return out, extra


def _mm_nt(a, b, name):
    m, k = a.shape
    n = b.shape[0]
    tm = 512 if m % 512 == 0 else ROW_TILE

    def body(a_ref, b_ref, o_ref):
        o_ref[...] = lax.dot_general(a_ref[...], b_ref[...], (((1,), (1,)), ((), ())), preferred_element_type=F32)

    return pl.pallas_call(
        body, name=name, grid=(m // tm,),
        in_specs=[pl.BlockSpec((tm, k), lambda i: (i, 0)), pl.BlockSpec((n, k), lambda i: (0, 0))],
        out_specs=pl.BlockSpec((tm, n), lambda i: (i, 0)),
        out_shape=jax.ShapeDtypeStruct((m, n), F32), compiler_params=_cp("parallel"))(a, b)


def _mm_tn(a, b, nsh, name, comm=None):
    s, m = a.shape
    n = b.shape[1]
    ns = n // nsh
    tm = 512 if m % 512 == 0 else ROW_TILE
    tn = _pick_tile(ns, 1024)
    per = ns // tn
    grid = (nsh * per, m // tm)
    host = _Host(comm, [pl.BlockSpec((s, tm), lambda j, i: (0, i)), pl.BlockSpec((s, tn), lambda j, i: (0, j))],
                 [pl.BlockSpec((None, tm, tn), lambda j, i: (j // per, i, j % per))],
                 [jax.ShapeDtypeStruct((nsh, m, ns), BF16)], [])

    def body(*refs):
        (a_ref, b_ref), (o_ref,), _ = host.split(refs)
        step = pl.program_id(0) * grid[1] + pl.program_id(1)
        host.before(step, grid[0] * grid[1])
        o_ref[...] = lax.dot_general(a_ref[...], b_ref[...], (((0,), (0,)), ((), ())),
                                     preferred_element_type=F32).astype(BF16)
        host.after(step, grid[0] * grid[1])

    outs = pl.pallas_call(
        body, name=name, grid=grid, in_specs=host.in_specs, out_specs=host.out_specs, out_shape=host.out_shape,
        scratch_shapes=host.scratch, input_output_aliases=host.aliases,
        compiler_params=_cp("arbitrary", "arbitrary"))(a, b, *host.args)
    (out,), extra = host.results(outs)
    return out, extra


def _tri(n, rel):
    row = lax.broadcasted_iota(jnp.int32, (2 * n, n), 0)
    col = lax.broadcasted_iota(jnp.int32, (2 * n, n), 1)
    return jnp.where(rel(jnp.where(row >= n, row - n, row), col), 1.0, 0.0).astype(BF16)


def _dot_split(x, tri2):
    hi = x.astype(BF16)
    lo = (x - hi.astype(F32)).astype(BF16)
    return jnp.dot(jnp.concatenate([hi, lo], axis=1), tri2, preferred_element_type=F32)


def _nt(a, b):
    return lax.dot_general(a, b, (((1,), (1,)), ((), ())), preferred_element_type=F32)


def _tn(a, b):
    return lax.dot_general(a, b, (((0,), (0,)), ((), ())), preferred_element_type=F32)


def _heads_per_step(nh):
    return max(h for h in (1, 2, 4) if nh % h == 0)


def _sba_fwd(p, sbw, name, comm=None):
    s = p.shape[0]
    nh = sbw // HEAD_DIM
    hp = _heads_per_step(nh)
    ngrp, hw = nh // hp, hp * HEAD_DIM
    blk = ATT_BLOCK
    nq = s // blk
    scale = 1.0 / math.sqrt(HEAD_DIM)
    host = _Host(comm,
                 [pl.BlockSpec((blk, hw), lambda g, i: (i, g)),
                  pl.BlockSpec((s, hw), lambda g, i: (0, ngrp + g)),
                  pl.BlockSpec((s, hw), lambda g, i: (0, 2 * ngrp + g))],
                 [pl.BlockSpec((blk, hw), lambda g, i: (i, g))] * 2,
                 [jax.ShapeDtypeStruct((s, sbw), F32)] * 2,
                 [pltpu.VMEM((s, hw), BF16)] * 2)

    def body(*refs):
        (q_ref, k_ref, v_ref), (o_ref, lt_ref), (kb_ref, vb_ref) = host.split(refs)
        i = pl.program_id(1)
        step = pl.program_id(0) * nq + i
        host.before(step, ngrp * nq)

        @pl.when(i == 0)
        def _():
            kb_ref[...] = k_ref[...].astype(BF16)
            vb_ref[...] = v_ref[...].astype(BF16)

        heads = [slice(h * HEAD_DIM, (h + 1) * HEAD_DIM) for h in range(hp)]
        qs = [q_ref[:, hd].astype(BF16) for hd in heads]
        later = _tri(blk, lambda r, c: r > c)
        causal = lax.broadcasted_iota(jnp.int32, (blk, blk), 1) < lax.broadcasted_iota(jnp.int32, (blk, blk), 0)

        def key_block(j, carry, diagonal):
            rows = pl.ds(pl.multiple_of(j * blk, blk), blk)
            hs = range(hp)
            z = [_nt(qs[h], kb_ref[rows, heads[h]]) * scale for h in hs]
            ls = [_log_sigmoid(z[h]) for h in hs]
            lm = [jnp.where(causal, ls[h] - z[h], 0.0) if diagonal else ls[h] - z[h] for h in hs]
            stay = [_dot_split(lm[h], later) for h in hs]
            w = [jnp.exp(ls[h] + stay[h] + carry[h][1]) for h in hs]
            if diagonal:
                w = [jnp.where(causal, w[h], 0.0) for h in hs]
            acc = [carry[h][0] + jnp.dot(w[h].astype(BF16), vb_ref[rows, heads[h]], preferred_element_type=F32) for h in hs]
            return tuple((acc[h], carry[h][1] + jnp.sum(lm[h], axis=1, keepdims=True)) for h in hs)

        init = tuple((jnp.zeros((blk, HEAD_DIM), F32), jnp.zeros((blk, 1), F32)) for _ in heads)
        carry = key_block(i, init, True)
        carry = lax.fori_loop(0, i, lambda n, c: key_block(i - 1 - n, c, False), carry)
        for h, hd in enumerate(heads):
            o_ref[:, hd] = carry[h][0]
            lt_ref[:, hd] = jnp.broadcast_to(carry[h][1], (blk, HEAD_DIM))
        host.after(step, ngrp * nq)

    outs = pl.pallas_call(
        body, name=name, grid=(ngrp, nq), in_specs=host.in_specs, out_specs=host.out_specs, out_shape=host.out_shape,
        scratch_shapes=host.scratch, input_output_aliases=host.aliases,
        compiler_params=_cp("arbitrary", "arbitrary"))(p, p, p, *host.args)
    (out, ltot), extra = host.results(outs)
    return out, ltot, extra


def _sba_bwd(p, ltot, dout, sbw, name, comm=None):
    s = p.shape[0]
    nh = sbw // HEAD_DIM
    hp = _heads_per_step(nh)
    ngrp, hw = nh // hp, hp * HEAD_DIM
    blk = ATT_BLOCK
    nq = s // blk
    scale = 1.0 / math.sqrt(HEAD_DIM)
    blk_spec = pl.BlockSpec((blk, hw), lambda g, i: (i, g))
    col_spec = pl.BlockSpec((s, hw), lambda g, i: (0, g))
    host = _Host(comm,
                 [blk_spec, pl.BlockSpec((s, hw), lambda g, i: (0, ngrp + g)),
                  pl.BlockSpec((s, hw), lambda g, i: (0, 2 * ngrp + g)), blk_spec, blk_spec],
                 [blk_spec, col_spec, col_spec], [jax.ShapeDtypeStruct((s, sbw), BF16)] * 3,
                 [pltpu.VMEM((s, hw), BF16)] * 2 + [pltpu.VMEM((s, hw), F32)] * 2)

    def body(*refs):
        (q_ref, k_ref, v_ref, lt_ref, do_ref), (dq_ref, dk_ref, dv_ref), (kb_ref, vb_ref, dka_ref, dva_ref) = host.split(refs)
        i = pl.program_id(1)
        step = pl.program_id(0) * nq + i
        host.before(step, ngrp * nq)

        @pl.when(i == 0)
        def _():
            kb_ref[...] = k_ref[...].astype(BF16)
            vb_ref[...] = v_ref[...].astype(BF16)
            dka_ref[...] = jnp.zeros_like(dka_ref)
            dva_ref[...] = jnp.zeros_like(dva_ref)

        heads = [slice(h * HEAD_DIM, (h + 1) * HEAD_DIM) for h in range(hp)]
        qs = [q_ref[:, hd].astype(BF16) for hd in heads]
        dos = [do_ref[:, hd].astype(BF16) for hd in heads]
        ltots = [lt_ref[:, h * HEAD_DIM:h * HEAD_DIM + 1] for h in range(hp)]
        upto = _tri(blk, lambda r, c: r <= c)
        before = _tri(blk, lambda r, c: r < c)
        causal = lax.broadcasted_iota(jnp.int32, (blk, blk), 1) < lax.broadcasted_iota(jnp.int32, (blk, blk), 0)

        def key_block(j, carry, diagonal):
            rows = pl.ds(pl.multiple_of(j * blk, blk), blk)
            hs = range(hp)
            kj = [kb_ref[rows, heads[h]] for h in hs]
            vj = [vb_ref[rows, heads[h]] for h in hs]
            z = [_nt(qs[h], kj[h]) * scale for h in hs]
            dw = [_nt(dos[h], vj[h]) for h in hs]
            ls = [_log_sigmoid(z[h]) for h in hs]
            lm = [jnp.where(causal, ls[h] - z[h], 0.0) if diagonal else ls[h] - z[h] for h in hs]
            stay = [ltots[h] - carry[h][1] - _dot_split(lm[h], upto) for h in hs]
            w = [jnp.exp(ls[h] + stay[h]) for h in hs]
            if diagonal:
                w = [jnp.where(causal, w[h], 0.0) for h in hs]
            da = [dw[h] * w[h] for h in hs]
            sig = [jnp.exp(ls[h]) for h in hs]
            chain = [sig[h] * (carry[h][2] + _dot_split(da[h], before)) for h in hs]
            if diagonal:
                chain = [jnp.where(causal, chain[h], 0.0) for h in hs]
            dzb = [((da[h] * (1.0 - sig[h]) - chain[h]) * scale).astype(BF16) for h in hs]
            dq = [carry[h][0] + jnp.dot(dzb[h], kj[h], preferred_element_type=F32) for h in hs]
            for h in hs:
                dka_ref[rows, heads[h]] += _tn(dzb[h], qs[h])
            for h in hs:
                dva_ref[rows, heads[h]] += _tn(w[h].astype(BF16), dos[h])
            return tuple((dq[h], carry[h][1] + jnp.sum(lm[h], axis=1, keepdims=True),
                          carry[h][2] + jnp.sum(da[h], axis=1, keepdims=True)) for h in hs)

        zero = jnp.zeros((blk, 1), F32)
        init = tuple((jnp.zeros((blk, HEAD_DIM), F32), zero, zero) for _ in heads)
        carry = lax.fori_loop(0, i, lambda j, c: key_block(j, c, False), init)
        carry = key_block(i, carry, True)
        for h, hd in enumerate(heads):
            dq_ref[:, hd] = carry[h][0].astype(BF16)

        @pl.when(i == nq - 1)
        def _():
            dk_ref[...] = dka_ref[...].astype(BF16)
            dv_ref[...] = dva_ref[...].astype(BF16)

        host.after(step, ngrp * nq)

    outs = pl.pallas_call(
        body, name=name, grid=(ngrp, nq), in_specs=host.in_specs, out_specs=host.out_specs, out_shape=host.out_shape,
        scratch_shapes=host.scratch, input_output_aliases=host.aliases,
        compiler_params=_cp("arbitrary", "arbitrary"))(p, p, p, ltot, dout, *host.args)
    (dq, dk, dv), extra = host.results(outs)
    return dq, dk, dv, extra


def _pool_groups(pad_ref, tile, row0, gd, halo):
    row = row0 + lax.broadcasted_iota(jnp.int32, (tile, 1), 0)
    out = []
    for gi, win in enumerate(POOL_WINDOWS):
        cs = slice(gi * gd, (gi + 1) * gd)
        tok = pad_ref[halo:halo + tile, cs]
        acc = tok
        for j in range(1, win):
            acc = acc + pad_ref[halo - j:halo - j + tile, cs]
        cnt = jnp.minimum(win, row + 1).astype(F32)
        out.append(acc / cnt - tok)
    return out


def _even_mix_fwd(p, att, pool_w, pool_scale, d, name):
    s = p.shape[0]
    half = d // 2
    gd = half // len(POOL_WINDOWS)
    t, hb = ROW_TILE, POOL_HALO

    def body(u_ref, uh_ref, g_ref, a_ref, pw_ref, sc_ref, y_ref, pad_ref):
        i = pl.program_id(0)
        pad_ref[0:hb, :] = jnp.where(i > 0, uh_ref[...], 0.0)
        pad_ref[hb:, :] = u_ref[...]
        pooled = _pool_groups(pad_ref, t, i * t, gd, hb)
        for gi in range(len(POOL_WINDOWS)):
            cs = slice(gi * gd, (gi + 1) * gd)
            po = jnp.dot(pooled[gi].astype(BF16), pw_ref[gi], preferred_element_type=F32) * sc_ref[:, cs]
            y_ref[:, half + gi * gd:half + (gi + 1) * gd] = (po * _silu(g_ref[:, half + gi * gd:half + (gi + 1) * gd])).astype(BF16)
        y_ref[:, :half] = (a_ref[...] * _silu(g_ref[:, :half])).astype(BF16)

    return pl.pallas_call(
        body, name=name, grid=(s // t,),
        in_specs=[pl.BlockSpec((t, half), lambda i: (i, 3)),
                  pl.BlockSpec((hb, half), lambda i: (jnp.maximum(i * (t // hb) - 1, 0), 3)),
                  pl.BlockSpec((t, d), lambda i: (i, 2)),
                  pl.BlockSpec((t, half), lambda i: (i, 0)),
                  pl.BlockSpec(pool_w.shape, lambda i: (0, 0, 0)),
                  pl.BlockSpec((1, half), lambda i: (0, 0))],
        out_specs=pl.BlockSpec((t, d), lambda i: (i, 0)),
        out_shape=jax.ShapeDtypeStruct((s, d), BF16),
        scratch_shapes=[pltpu.VMEM((hb + t, half), F32)],
        compiler_params=_cp("parallel"))(p, p, p, att, pool_w, pool_scale)


def _even_mix_bwd(p, att, dy, pool_w, pool_scale, d, name, comm=None):
    s = p.shape[0]
    half = d // 2
    ng = len(POOL_WINDOWS)
    gd = half // ng
    t, hb = ROW_TILE, POOL_HALO
    nt = s // t
    host = _Host(
        comm,
        [pl.BlockSpec((t, half), lambda i: (i, 3)),
         pl.BlockSpec((hb, half), lambda i: (jnp.maximum(i * (t // hb) - 1, 0), 3)),
         pl.BlockSpec((t, d), lambda i: (i, 2)),
         pl.BlockSpec((hb, half), lambda i: (jnp.minimum((i + 1) * (t // hb), s // hb - 1), 5)),
         pl.BlockSpec((t, half), lambda i: (i, 0)),
         pl.BlockSpec((t, d), lambda i: (i, 0)),
         pl.BlockSpec((hb, half), lambda i: (jnp.minimum((i + 1) * (t // hb), s // hb - 1), 1)),
         pl.BlockSpec(pool_w.shape, lambda i: (0, 0, 0)),
         pl.BlockSpec((1, half), lambda i: (0, 0))],
        [pl.BlockSpec((t, half), lambda i: (i, 0)),
         pl.BlockSpec((t, half), lambda i: (i, 0)),
         pl.BlockSpec((t, d), lambda i: (i, 0)),
         pl.BlockSpec((1, half), lambda i: (0, 0)),
         pl.BlockSpec((ng, gd, gd), lambda i: (0, 0, 0))],
        [jax.ShapeDtypeStruct((s, half), F32), jax.ShapeDtypeStruct((s, half), BF16),
         jax.ShapeDtypeStruct((s, d), BF16), jax.ShapeDtypeStruct((1, half), F32),
         jax.ShapeDtypeStruct((ng, gd, gd), F32)],
        [pltpu.VMEM((hb + t, half), F32), pltpu.VMEM((t + hb, half), F32)])

    def body(*refs):
        ((u_ref, uh_ref, g_ref, gh_ref, a_ref, dy_ref, dyh_ref, pw_ref, sc_ref),
         (da_ref, du_ref, dg_ref, dsc_ref, dpw_ref), (pad_ref, dn_ref)) = host.split(refs)
        i = pl.program_id(0)
        host.before(i, nt)
        first = i == 0
        pad_ref[0:hb, :] = jnp.where(i > 0, uh_ref[...], 0.0)
        pad_ref[hb:, :] = u_ref[...]
        pooled = _pool_groups(pad_ref, t, i * t, gd, hb)
        g1 = g_ref[:, :half]
        dy1 = dy_ref[:, :half]
        da_ref[...] = dy1 * _silu(g1)
        dg_ref[:, :half] = (dy1 * a_ref[...] * _dsilu(g1)).astype(BF16)
        row = i * t + lax.broadcasted_iota(jnp.int32, (t + hb, 1), 0)
        for gi, win in enumerate(POOL_WINDOWS):
            cs = slice(gi * gd, (gi + 1) * gd)
            cs2 = slice(half + gi * gd, half + (gi + 1) * gd)
            w = pw_ref[gi]
            pb = pooled[gi].astype(BF16)
            zp = jnp.dot(pb, w, preferred_element_type=F32)
            g2 = g_ref[:, cs2]
            dy2 = dy_ref[:, cs2]
            dg_ref[:, cs2] = (dy2 * zp * sc_ref[:, cs] * _dsilu(g2)).astype(BF16)
            dpo = dy2 * _silu(g2)
            _acc_rows(dsc_ref.at[:, cs], first, jnp.sum(dpo * zp, axis=0, keepdims=True))
            dz = (dpo * sc_ref[:, cs]).astype(BF16)
            _acc_rows(dpw_ref.at[gi], first, _tn(pb, dz))
            dzh = jnp.where(i < nt - 1, dyh_ref[:, cs] * _silu(gh_ref[:, cs]) * sc_ref[:, cs], 0.0).astype(BF16)
            dpool = _nt(dz, w)
            dpool_h = _nt(dzh, w)
            cnt = jnp.minimum(win, row + 1).astype(F32)
            dn_ref[0:t, cs] = dpool / cnt[0:t]
            dn_ref[t:, cs] = dpool_h / cnt[t:]
            acc = dn_ref[0:t, cs]
            for j in range(1, win):
                acc = acc + dn_ref[j:j + t, cs]
            du_ref[:, cs] = (acc - dpool).astype(BF16)
        host.after(i, nt)

    outs = pl.pallas_call(
        body, name=name, grid=(nt,), in_specs=host.in_specs, out_specs=host.out_specs, out_shape=host.out_shape,
        scratch_shapes=host.scratch, input_output_aliases=host.aliases,
        compiler_params=_cp("arbitrary"))(p, p, p, p, att, dy, dy, pool_w, pool_scale, *host.args)
    return host.results(outs)


def _mm_out_even(y, w, x, g_post, g_pre_next, name):
    s, k = y.shape
    d = w.shape[1]
    t = ROW_TILE

    def body(y_ref, w_ref, x_ref, gp_ref, gn_ref, o_ref, x1_ref, h1_ref):
        for r0 in range(0, t, t // 2):
            rows = slice(r0, r0 + t // 2)
            o = jnp.dot(y_ref[rows, :], w_ref[...], preferred_element_type=F32)
            o_ref[rows, :] = o
            ohat, _ = _rms_stats(o)
            x1 = x_ref[rows, :] + ohat * gp_ref[...]
            x1_ref[rows, :] = x1
            xhat, _ = _rms_stats(x1)
            h1_ref[rows, :] = (xhat * gn_ref[...]).astype(BF16)

    row = lambda c: pl.BlockSpec((t, c), lambda i: (i, 0))
    vec = pl.BlockSpec((1, d), lambda i: (0, 0))
    return pl.pallas_call(
        body, name=name, grid=(s // t,),
        in_specs=[row(k), pl.BlockSpec((k, d), lambda i: (0, 0)), row(d), vec, vec],
        out_specs=[row(d), row(d), row(d)],
        out_shape=[jax.ShapeDtypeStruct((s, d), F32), jax.ShapeDtypeStruct((s, d), F32),
                   jax.ShapeDtypeStruct((s, d), BF16)],
        compiler_params=_cp("parallel"))(y, w, x, g_post, g_pre_next)


def _mm_out_odd(y, w, x1, g_post, target, name):
    s, k = y.shape
    d = w.shape[1]
    t = ROW_TILE

    def body(y_ref, w_ref, x_ref, gp_ref, tg_ref, do_ref, dx_ref, loss_ref, dgp_ref):
        first = pl.program_id(0) == 0
        gp = gp_ref[...]
        part = dgp = None
        for r0 in range(0, t, t // 2):
            rows = slice(r0, r0 + t // 2)
            o = jnp.dot(y_ref[rows, :], w_ref[...], preferred_element_type=F32)
            ohat, r = _rms_stats(o)
            diff = x_ref[rows, :] + ohat * gp - tg_ref[rows, :]
            part_half = 0.5 * jnp.sum(jnp.mean(diff * diff, axis=-1, keepdims=True), axis=0, keepdims=True)
            dx2 = diff * (1.0 / d)
            dx_ref[rows, :] = dx2
            do, dgp_half = _rms_bwd(dx2, ohat, r, gp)
            do_ref[rows, :] = do.astype(BF16)
            part = part_half if part is None else part + part_half
            dgp = dgp_half if dgp is None else dgp + dgp_half
        _acc_rows(loss_ref, first, jnp.broadcast_to(part, loss_ref.shape))
        _acc_rows(dgp_ref, first, dgp)

    row = lambda c: pl.BlockSpec((t, c), lambda i: (i, 0))
    vec = pl.BlockSpec((1, d), lambda i: (0, 0))
    return pl.pallas_call(
        body, name=name, grid=(s // t,),
        in_specs=[row(k), pl.BlockSpec((k, d), lambda i: (0, 0)), row(d), vec, row(d)],
        out_specs=[row(d), row(d), pl.BlockSpec((8, LANES), lambda i: (0, 0)), vec],
        out_shape=[jax.ShapeDtypeStruct((s, d), BF16), jax.ShapeDtypeStruct((s, d), F32),
                   jax.ShapeDtypeStruct((8, LANES), F32), jax.ShapeDtypeStruct((1, d), F32)],
        compiler_params=_cp("arbitrary"))(y, w, x1, g_post, target)


def _layer_norm(d1, cg, cb):
    mu = jnp.mean(d1, axis=-1, keepdims=True)
    cen = d1 - mu
    rstd = lax.rsqrt(jnp.mean(cen * cen, axis=-1, keepdims=True) + EPS)
    n = cen * rstd
    return n, rstd, n * cg + cb


SUBLANES = 8
ROW_STRIP = 64
GATHER_PIECES = 4
CONV_ROWS = 64


def _make_shifts(pad_ref, cs, sh_ref):
    rows = sh_ref.shape[1]
    for r in range(1, SUBLANES):
        sh_ref[r - 1] = pad_ref[r:r + rows, cs]


def _by_shift(taps, base, sign=1):
    return sorted(range(taps), key=lambda k: ((sign * (base + k)) % SUBLANES, k))


def _window(pad_ref, cs, sh_ref, off, t):
    m, r = divmod(off, SUBLANES)
    if r == 0:
        return pad_ref[SUBLANES * m:SUBLANES * m + t, cs]
    return sh_ref[r - 1, SUBLANES * m:SUBLANES * m + t, :]


def _odd_mix_fwd(p, sconv_w, dconv_w, dconv_b, cnorm_g, cnorm_b, d, name):
    s = p.shape[0]
    w = d // 2
    k3, k31 = sconv_w.shape[0], dconv_w.shape[0]
    t, hb = ROW_TILE, CONV_HALO
    assert hb >= k31 - 1 and w % LANES == 0

    def body(p_ref, ph_ref, w3_ref, w31_ref, b31_ref, cg_ref, cb_ref, y_ref, s3_ref, d1_ref, mpad, dpad, sh_ref):
        i = pl.program_id(0)
        mpad[0:hb, :] = jnp.where(i > 0, ph_ref[:, 2 * w:3 * w] * ph_ref[:, 0:w], 0.0)
        mpad[hb:, :] = p_ref[:, 2 * w:3 * w] * p_ref[:, 0:w]
        dpad[0:hb, :] = jnp.where(i > 0, ph_ref[:, 3 * w:4 * w] * _sigmoid(ph_ref[:, 4 * w:5 * w]), 0.0)
        dpad[hb:, :] = p_ref[:, 3 * w:4 * w] * _sigmoid(p_ref[:, 4 * w:5 * w])
        for c0 in range(0, w, LANES):
            cs = slice(c0, c0 + LANES)
            acc = jnp.zeros((t, LANES), F32)
            for kk in range(k3):
                acc = acc + w3_ref[kk:kk + 1, cs] * mpad[hb - (k3 - 1) + kk:hb - (k3 - 1) + kk + t, cs]
            s3_ref[:, cs] = acc
            _make_shifts(dpad, cs, sh_ref)
            for r0 in range(0, t, CONV_ROWS):
                acc = jnp.zeros((CONV_ROWS, LANES), F32)
                for kk in _by_shift(k31, hb - (k31 - 1)):
                    acc = acc + w31_ref[kk:kk + 1, cs] * _window(dpad, cs, sh_ref, hb - (k31 - 1) + kk + r0, CONV_ROWS)
                d1_ref[r0:r0 + CONV_ROWS, cs] = acc + b31_ref[:, cs]
        _, _, d2 = _layer_norm(d1_ref[...], cg_ref[...], cb_ref[...])
        y_ref[:, :w] = (p_ref[:, w:2 * w] * s3_ref[...] * _silu(p_ref[:, 5 * w:6 * w])).astype(BF16)
        y_ref[:, w:] = (_silu(d2) * _silu(p_ref[:, 6 * w:7 * w])).astype(BF16)

    row = lambda c: pl.BlockSpec((t, c), lambda i: (i, 0))
    full = lambda a: pl.BlockSpec(a.shape, lambda i: (0, 0))
    return pl.pallas_call(
        body, name=name, grid=(s // t,),
        in_specs=[row(7 * w),
                  pl.BlockSpec((hb, 5 * w), lambda i: (jnp.maximum(i * (t // hb) - 1, 0), 0)),
                  full(sconv_w), full(dconv_w), full(dconv_b), full(cnorm_g), full(cnorm_b)],
        out_specs=[row(d), row(w), row(w)],
        out_shape=[jax.ShapeDtypeStruct((s, d), BF16), jax.ShapeDtypeStruct((s, w), F32),
                   jax.ShapeDtypeStruct((s, w), F32)],
        scratch_shapes=[pltpu.VMEM((hb + t, w), F32)] * 2 + [pltpu.VMEM((SUBLANES - 1, hb + t - SUBLANES, LANES), F32)],
        compiler_params=_cp("parallel"))(p, p, sconv_w, dconv_w, dconv_b, cnorm_g, cnorm_b)


def _odd_bwd_rows(p, s3, d1, dy, cnorm_g, cnorm_b, d, name, comm=None):
    s = p.shape[0]
    w = d // 2
    t = ROW_TILE
    col = lambda j: pl.BlockSpec((t, w), lambda i: (i, j))
    row = lambda c: pl.BlockSpec((t, c), lambda i: (i, 0))
    vec = pl.BlockSpec((1, w), lambda i: (0, 0))
    host = _Host(comm, [col(1), col(5), col(6), row(w), row(w), row(d), vec, vec],
                 [row(w), row(d), row(w), row(w), vec, vec, vec],
                 [jax.ShapeDtypeStruct((s, w), BF16), jax.ShapeDtypeStruct((s, d), BF16),
                  jax.ShapeDtypeStruct((s, w), F32), jax.ShapeDtypeStruct((s, w), F32)] + [jax.ShapeDtypeStruct((1, w), F32)] * 3, [])

    def body(*refs):
        ((bc_ref, g1_ref, g2_ref, s3_ref, d1_ref, dy_ref, cg_ref, cb_ref),
         (dbc_ref, dg_ref, ds3_ref, dd1_ref, dcg_ref, dcb_ref, db_ref), _) = host.split(refs)
        step = pl.program_id(0)
        host.before(step, s // t)
        first = step == 0

        def strip(j, sums):
            rows = slice(j * ROW_STRIP, (j + 1) * ROW_STRIP)
            g1, g2 = g1_ref[rows, :], g2_ref[rows, :]
            bc, s3v = bc_ref[rows, :], s3_ref[rows, :]
            dy1, dy2 = dy_ref[rows, :w], dy_ref[rows, w:]
            n, rstd, d2 = _layer_norm(d1_ref[rows, :], cg_ref[...], cb_ref[...])
            dg_ref[rows, :w] = (dy1 * bc * s3v * _dsilu(g1)).astype(BF16)
            dg_ref[rows, w:] = (dy2 * _silu(d2) * _dsilu(g2)).astype(BF16)
            dco = dy1 * _silu(g1)
            dbc_ref[rows, :] = (dco * s3v).astype(BF16)
            ds3_ref[rows, :] = dco * bc
            dd2 = dy2 * _silu(g2) * _dsilu(d2)
            dn = dd2 * cg_ref[...]
            dd1 = rstd * (dn - jnp.mean(dn, axis=-1, keepdims=True) - n * jnp.mean(dn * n, axis=-1, keepdims=True))
            dd1_ref[rows, :] = dd1
            dcb, dcg, db = sums
            return (dcb + jnp.sum(dd2, axis=0, keepdims=True), dcg + jnp.sum(dd2 * n, axis=0, keepdims=True),
                    db + jnp.sum(dd1, axis=0, keepdims=True))

        zero = jnp.zeros((1, w), F32)
        sums = (zero, zero, zero)
        for j in range(t // ROW_STRIP):
            sums = strip(j, sums)
        dcb, dcg, db = sums
        _acc_rows(dcb_ref, first, dcb)
        _acc_rows(dcg_ref, first, dcg)
        _acc_rows(db_ref, first, db)
        host.after(step, s // t)

    outs = pl.pallas_call(
        body, name=name, grid=(s // t,), in_specs=host.in_specs, out_specs=host.out_specs, out_shape=host.out_shape,
        scratch_shapes=host.scratch, input_output_aliases=host.aliases,
        compiler_params=_cp("arbitrary"))(p, p, p, s3, d1, dy, cnorm_g, cnorm_b, *host.args)
    return host.results(outs)


def _odd_bwd_conv(p, ds3, dd1, sconv_w, dconv_w, d, name):
    s = p.shape[0]
    w = d // 2
    k3, k31 = sconv_w.shape[0], dconv_w.shape[0]
    t, hb, ha = ROW_TILE, CONV_HALO, 8
    nt = s // t
    assert hb >= k31 - 1 and ha >= k3 - 1

    def body(hc_ref, cc_ref, ga_ref, gb_ref, hch_ref, cch_ref, gah_ref, gbh_ref, ds3_ref, ds3h_ref, dd1_ref, dd1h_ref,
             w3_ref, w31_ref, dhc_ref, dcc_ref, dga_ref, dgb_ref, dw3_ref, dw31_ref, mpad, dpad, s3pad, d1pad, sh_ref):
        i = pl.program_id(0)
        first = i == 0
        last = i == nt - 1
        mpad[0:hb, :] = jnp.where(i > 0, cch_ref[...] * hch_ref[...], 0.0)
        mpad[hb:, :] = cc_ref[...] * hc_ref[...]
        dpad[0:hb, :] = jnp.where(i > 0, gah_ref[...] * _sigmoid(gbh_ref[...]), 0.0)
        dpad[hb:, :] = ga_ref[...] * _sigmoid(gb_ref[...])
        s3pad[0:t, :] = ds3_ref[...]
        s3pad[t:, :] = jnp.where(last, 0.0, ds3h_ref[...])
        d1pad[0:t, :] = dd1_ref[...]
        d1pad[t:, :] = jnp.where(last, 0.0, dd1h_ref[...])

        @pl.when(first)
        def _():
            dw3_ref[...] = jnp.zeros_like(dw3_ref)
            dw31_ref[...] = jnp.zeros_like(dw31_ref)

        def fold(v):
            return jnp.sum(v.reshape(v.shape[0] // SUBLANES, SUBLANES, LANES), axis=0)

        groups = range(0, t, CONV_ROWS)
        for c0 in range(0, w, LANES):
            cs = slice(c0, c0 + LANES)
            ds3v = s3pad[0:t, cs]
            dm = jnp.zeros((t, LANES), F32)
            for kk in range(k3):
                dm = dm + w3_ref[kk:kk + 1, cs] * s3pad[k3 - 1 - kk:k3 - 1 - kk + t, cs]
                off = hb - (k3 - 1) + kk
                dw3_ref[SUBLANES * kk:SUBLANES * (kk + 1), cs] += fold(ds3v * mpad[off:off + t, cs])
            dcc_ref[:, cs] = (dm * hc_ref[:, cs]).astype(BF16)
            dhc_ref[:, cs] = (dm * cc_ref[:, cs]).astype(BF16)
            _make_shifts(d1pad, cs, sh_ref)
            for r0 in groups:
                rows = slice(r0, r0 + CONV_ROWS)
                dd0 = jnp.zeros((CONV_ROWS, LANES), F32)
                for kk in _by_shift(k31, -(k31 - 1), -1):
                    dd0 = dd0 + w31_ref[kk:kk + 1, cs] * _window(d1pad, cs, sh_ref, k31 - 1 - kk + r0, CONV_ROWS)
                sgb = _sigmoid(gb_ref[rows, cs])
                dga_ref[rows, cs] = (dd0 * sgb).astype(BF16)
                dgb_ref[rows, cs] = (dd0 * ga_ref[rows, cs] * sgb * (1.0 - sgb)).astype(BF16)
            _make_shifts(dpad, cs, sh_ref)
            for kk in _by_shift(k31, hb - (k31 - 1)):
                part = jnp.zeros((SUBLANES, LANES), F32)
                for r0 in groups:
                    part = part + fold(d1pad[r0:r0 + CONV_ROWS, cs]
                                       * _window(dpad, cs, sh_ref, hb - (k31 - 1) + kk + r0, CONV_ROWS))
                dw31_ref[SUBLANES * kk:SUBLANES * (kk + 1), cs] += part

    col = lambda j: pl.BlockSpec((t, w), lambda i: (i, j))
    pre = lambda j: pl.BlockSpec((hb, w), lambda i: (jnp.maximum(i * (t // hb) - 1, 0), j))
    row = pl.BlockSpec((t, w), lambda i: (i, 0))
    post = lambda h: pl.BlockSpec((h, w), lambda i: (jnp.minimum((i + 1) * (t // h), s // h - 1), 0))
    full = lambda a: pl.BlockSpec(a.shape, lambda i: (0, 0))
    dhc, dcc, dga, dgb, dw3, dw31 = pl.pallas_call(
        body, name=name, grid=(nt,),
        in_specs=[col(0), col(2), col(3), col(4), pre(0), pre(2), pre(3), pre(4),
                  row, post(ha), row, post(hb), full(sconv_w), full(dconv_w)],
        out_specs=[row, row, row, row, pl.BlockSpec((SUBLANES * k3, w), lambda i: (0, 0)),
                   pl.BlockSpec((SUBLANES * k31, w), lambda i: (0, 0))],
        out_shape=[jax.ShapeDtypeStruct((s, w), BF16)] * 4
        + [jax.ShapeDtypeStruct((SUBLANES * k3, w), F32), jax.ShapeDtypeStruct((SUBLANES * k31, w), F32)],
        scratch_shapes=[pltpu.VMEM((hb + t, w), F32)] * 2 + [pltpu.VMEM((t + ha, w), F32), pltpu.VMEM((t + hb, w), F32),
                                                             pltpu.VMEM((SUBLANES - 1, hb + t - SUBLANES, LANES), F32)],
        compiler_params=_cp("arbitrary"))(p, p, p, p, p, p, p, p, ds3, ds3, dd1, dd1, sconv_w, dconv_w)
    return dhc, dcc, dga, dgb, jnp.sum(dw3.reshape(k3, SUBLANES, w), axis=1), jnp.sum(dw31.reshape(k31, SUBLANES, w), axis=1)


def _mm_in_bwd(dp, w3, x, g_pre, dres, post, name, comm=None):
    s = dp.shape[0]
    nsh, d, ns = w3.shape
    t = 512 if s % 512 == 0 else ROW_TILE
    nt = s // t
    ks = 2 if (ns // 2) % LANES == 0 else 1
    nk, kw = nsh * ks, ns // ks
    chunk = 128
    nchunk = t // chunk
    row = pl.BlockSpec((t, d), lambda i, k: (i, 0))
    vec = pl.BlockSpec((1, d), lambda i, k: (0, 0))
    rowwise = [x, dres] + ([post[0]] if post is not None else [])
    in_specs = [pl.BlockSpec((t, kw), lambda i, k: (i, k)), pl.BlockSpec((None, d, kw), lambda i, k: (k // ks, 0, k % ks)), vec]
    out_specs = [row, vec]
    out_shape = [jax.ShapeDtypeStruct((s, d), F32), jax.ShapeDtypeStruct((1, d), F32)]
    args = [dp, w3, g_pre]
    if post is not None:
        in_specs += [vec]
        out_specs += [row, vec]
        out_shape += [jax.ShapeDtypeStruct((s, d), BF16), jax.ShapeDtypeStruct((1, d), F32)]
        args += [post[1]]
    n_blocked = len(in_specs)
    in_specs += [ANY] * len(rowwise)
    args += rowwise
    host = _Host(comm, in_specs, out_specs, out_shape,
                 [pltpu.VMEM((t, d), F32), pltpu.VMEM((len(rowwise), 2, chunk, d), F32), pltpu.SemaphoreType.DMA((len(rowwise), 2))])

    def body(*refs):
        ins, outs, (acc_ref, buf_ref, sem_ref) = host.split(refs)
        dp_ref, w_ref, g_ref = ins[:3]
        hbm = ins[n_blocked:]
        dx_ref, dg_ref = outs[:2]
        tile = pl.program_id(0)
        kk = pl.program_id(1)
        first = tile == 0
        step = tile * nk + kk
        host.before(step, nt * nk)
        part = _nt(dp_ref[...], w_ref[...])

        @pl.when(kk == 0)
        def _():
            acc_ref[...] = part

        @pl.when(kk > 0)
        def _():
            acc_ref[...] += part

        def fetch(ci, slot):
            return [pltpu.make_async_copy(src.at[pl.ds(tile * t + ci * chunk, chunk)], buf_ref.at[n, slot], sem_ref.at[n, slot])
                    for n, src in enumerate(hbm)]

        @pl.when(kk == nk - 1)
        def _():
            dg = dgp = None
            for cp in fetch(0, 0):
                cp.start()
            for ci in range(nchunk):
                slot = ci % 2
                if ci + 1 < nchunk:
                    for cp in fetch(ci + 1, 1 - slot):
                        cp.start()
                for cp in fetch(ci, slot):
                    cp.wait()
                rows = slice(ci * chunk, (ci + 1) * chunk)
                xhat, r = _rms_stats(buf_ref[0, slot])
                dxn, dg_part = _rms_bwd(acc_ref[rows, :], xhat, r, g_ref[...])
                dx = buf_ref[1, slot] + dxn
                dx_ref[rows, :] = dx
                dg = dg_part if dg is None else dg + dg_part
                if post is not None:
                    ohat, ro = _rms_stats(buf_ref[2, slot])
                    do, dgp_part = _rms_bwd(dx, ohat, ro, ins[3][...])
                    outs[2][rows, :] = do.astype(BF16)
                    dgp = dgp_part if dgp is None else dgp + dgp_part
            _acc_rows(dg_ref, first, dg)
            if post is not None:
                _acc_rows(outs[3], first, dgp)

        host.after(step, nt * nk)

    res = pl.pallas_call(
        body, name=name, grid=(nt, nk), in_specs=host.in_specs, out_specs=host.out_specs, out_shape=host.out_shape,
        scratch_shapes=host.scratch, input_output_aliases=host.aliases,
        compiler_params=_cp("arbitrary", "arbitrary"))(*args, *host.args)
    return host.results(res)


def _half_add(g, r1, c_arr, name, after=None):
    nsh, rows, ns = g.shape
    h = rows // 2
    tr = min(ROW_TILE, h)
    per = h // tr

    def body(c_ref, g_ref, r_ref, *rest):
        rest[-1][...] = (g_ref[...].astype(F32) + r_ref[...].astype(F32)).astype(BF16)

    spec = pl.BlockSpec((None, tr, ns), lambda s, r, c: (s, r, 0))
    ordering = [] if after is None else [after]
    return pl.pallas_call(
        body, name=name,
        grid_spec=pltpu.PrefetchScalarGridSpec(
            num_scalar_prefetch=1, grid=(nsh, per),
            in_specs=[pl.BlockSpec((None, tr, ns), lambda s, r, c: (s, c[0] * per + r, 0)), spec] + [ANY] * len(ordering),
            out_specs=spec),
        out_shape=jax.ShapeDtypeStruct((nsh, h, ns), BF16), compiler_params=_cp("parallel", "parallel"))(c_arr, g, r1, *ordering)


def _sum_chips(hh, r2, mc_arr, name, after=None):
    _, h, ns = hh.shape
    tr = min(ROW_TILE, h)
    per = h // tr

    def body(mc_ref, h_ref, a_ref, b_ref, c_ref, *rest):
        rest[-1][...] = ((h_ref[...].astype(F32) + a_ref[...].astype(F32)) + b_ref[...].astype(F32)) + c_ref[...].astype(F32)

    got = lambda k: pl.BlockSpec((None, tr, ns), lambda r, mc: (k, r, 0))
    ordering = [] if after is None else [after]
    return pl.pallas_call(
        body, name=name,
        grid_spec=pltpu.PrefetchScalarGridSpec(
            num_scalar_prefetch=1, grid=(per,),
            in_specs=[pl.BlockSpec((None, tr, ns), lambda r, mc: (mc[0], r, 0)), got(0), got(1), got(2)] + [ANY] * len(ordering),
            out_specs=pl.BlockSpec((tr, ns), lambda r, mc: (mc[1] * per + r, 0))),
        out_shape=jax.ShapeDtypeStruct((2 * h, ns), F32), compiler_params=_cp("parallel"))(mc_arr, hh, r2, r2, r2, *ordering)


def _add2(a, b, name):
    def body(a_ref, b_ref, o_ref):
        o_ref[...] = a_ref[...] + b_ref[...]

    return pl.pallas_call(body, name=name, out_shape=jax.ShapeDtypeStruct(a.shape, a.dtype), compiler_params=_cp())(a, b)


def _sum_chips_ordered(s2, r2, mc_arr, name):
    rows, w = s2.shape
    rh = rows // 2

    def body(mc_ref, s_ref, a_ref, b_ref, c_ref, o_ref):
        me = mc_ref[0]
        acc = None
        for j in range(N_CHIPS):
            rel = jnp.bitwise_xor(me, j)
            v = jnp.where(rel == 0, s_ref[...], jnp.where(rel == 2, a_ref[...], jnp.where(rel == 1, b_ref[...], c_ref[...])))
            acc = v if acc is None else acc + v
        o_ref[...] = acc

    got = lambda k: pl.BlockSpec((None, rh, w), lambda i, mc: (k, 0, 0))
    return pl.pallas_call(
        body, name=name,
        grid_spec=pltpu.PrefetchScalarGridSpec(
            num_scalar_prefetch=1, grid=(1,),
            in_specs=[pl.BlockSpec((rh, w), lambda i, mc: (mc[1], 0)), got(0), got(1), got(2)],
            out_specs=pl.BlockSpec((rh, w), lambda i, mc: (mc[1], 0))),
        out_shape=jax.ShapeDtypeStruct((rows, w), F32), compiler_params=_cp("arbitrary"))(mc_arr, s2, r2, r2, r2)


def _adamw(w, g, m, v, name, comm=None):
    r, c = w.shape
    tr = ROW_TILE if r % ROW_TILE == 0 else r
    c1 = 1.0 / (1.0 - ADAM_B1 ** ADAM_STEP)
    c2 = 1.0 / (1.0 - ADAM_B2 ** ADAM_STEP)
    spec = pl.BlockSpec((tr, c), lambda i: (i, 0))
    host = _Host(comm, [spec] * 4, [spec] * 4, [jax.ShapeDtypeStruct((r, c), F32)] * 4, [])

    def body(*refs):
        (w_ref, g_ref, m_ref, v_ref), (go_ref, d_ref, nm_ref, nv_ref), _ = host.split(refs)
        step = pl.program_id(0)
        host.before(step, r // tr)
        gv = g_ref[...]
        go_ref[...] = gv
        nm = ADAM_B1 * m_ref[...] + (1.0 - ADAM_B1) * gv
        nv = ADAM_B2 * v_ref[...] + (1.0 - ADAM_B2) * (gv * gv)
        nm_ref[...] = nm
        nv_ref[...] = nv
        d_ref[...] = -ADAM_LR * ((nm * c1) / (jnp.sqrt(nv * c2) + ADAM_EPS) + ADAM_WD * w_ref[...])
        host.after(step, r // tr)

    outs = pl.pallas_call(
        body, name=name, grid=(r // tr,), in_specs=host.in_specs, out_specs=host.out_specs, out_shape=host.out_shape,
        scratch_shapes=host.scratch, input_output_aliases=host.aliases,
        compiler_params=_cp("arbitrary"))(w, g, m, v, *host.args)
    return host.results(outs)


def _swap_with_sibling(grads, wholes, name):
    n, nw = len(grads), len(wholes)
    halves = [g.shape[1] // 2 for g in grads]

    def body(*refs):
        srcs, dsts = refs[:n + nw], refs[n + nw:2 * (n + nw)]
        ssem, rsem = refs[2 * (n + nw):]
        x, y, c, me, chips, sib = _place()
        cps = [_rcopy(srcs[a].at[:, pl.ds((1 - c) * halves[a], halves[a]), :], dsts[a], ssem.at[a], rsem.at[a], sib)
               for a in range(n)]
        cps += [_rcopy(srcs[a], dsts[a], ssem.at[a], rsem.at[a], sib) for a in range(n, n + nw)]
        for cp in cps:
            cp.start()
        for cp in cps:
            cp.wait_recv()
        for cp in cps:
            cp.wait_send()

    out_shape = [jax.ShapeDtypeStruct((g.shape[0], h, g.shape[2]), g.dtype) for g, h in zip(grads, halves)]
    out_shape += [jax.ShapeDtypeStruct(w.shape, w.dtype) for w in wholes]
    return pl.pallas_call(
        body, name=name, in_specs=[ANY] * (n + nw), out_specs=[ANY] * (n + nw), out_shape=out_shape,
        scratch_shapes=[pltpu.SemaphoreType.DMA((n + nw,)), pltpu.SemaphoreType.DMA((n + nw,))],
        compiler_params=pltpu.CompilerParams(has_side_effects=True))(*grads, *wholes)


def _scatter_start(h, name):
    land = (3,) + h.shape[1:]

    def body(h_ref, land_ref, send_sems, recv_sems, h_thru, land_thru, token):
        x, y, c, me, chips, sib = _place()
        for k, chip in enumerate(chips):
            _rcopy(h_ref.at[2 * chip[0] + chip[1]], land_ref.at[k], send_sems.at[k], recv_sems.at[k], (*chip, c)).start()
        token[...] = jnp.zeros_like(token)

    hbm = pl.BlockSpec(memory_space=pltpu.HBM)
    sem = pl.BlockSpec(memory_space=pltpu.SEMAPHORE)
    return pl.pallas_call(
        body, name=name,
        out_shape=(pltpu.SemaphoreType.DMA((3,)), pltpu.SemaphoreType.DMA((3,)), pltpu.HBM(h.shape, h.dtype),
                   pltpu.HBM(land, h.dtype), jax.ShapeDtypeStruct((8, LANES), F32)),
        in_specs=(hbm, hbm), out_specs=(sem, sem, hbm, hbm, pl.BlockSpec(memory_space=pltpu.VMEM)),
        input_output_aliases={0: 2, 1: 3},
        compiler_params=pltpu.CompilerParams(has_side_effects=pltpu.SideEffectType.DATAFLOW_SIDE_EFFECTING))(
            pltpu.with_memory_space_constraint(h, pltpu.HBM),
            pltpu.with_memory_space_constraint(lax.empty(land, h.dtype), pltpu.HBM))


def _scatter_wait(send_sems, recv_sems, h_thru, land_thru, after, name):
    def body(h_ref, land_ref, send_sems, recv_sems, after_ref, h_dead, got_ref):
        x, y, c, me, chips, sib = _place()
        for k, chip in enumerate(chips):
            cp = _rcopy(h_ref.at[2 * chip[0] + chip[1]], land_ref.at[k], send_sems.at[k], recv_sems.at[k], (*chip, c))
            cp.wait_send()
            cp.wait_recv()

    hbm = pl.BlockSpec(memory_space=pltpu.HBM)
    sem = pl.BlockSpec(memory_space=pltpu.SEMAPHORE)
    return pl.pallas_call(
        body, name=name,
        out_shape=(pltpu.HBM(h_thru.shape, h_thru.dtype), pltpu.HBM(land_thru.shape, land_thru.dtype)),
        in_specs=(hbm, hbm, sem, sem, ANY), out_specs=(hbm, hbm), input_output_aliases={0: 0, 1: 1},
        compiler_params=pltpu.CompilerParams(has_side_effects=pltpu.SideEffectType.DATAFLOW_SIDE_EFFECTING))(
            h_thru, land_thru, send_sems, recv_sems, after)


def _swap_start(g, name):
    h = g.shape[1] // 2
    land = (g.shape[0], h, g.shape[2])

    def body(g_ref, land_ref, send_sem, recv_sem, g_thru, land_thru, token):
        x, y, c, me, chips, sib = _place()
        _rcopy(g_ref.at[:, pl.ds((1 - c) * h, h), :], land_ref, send_sem.at[0], recv_sem.at[0], sib).start()
        token[...] = jnp.zeros_like(token)

    hbm = pl.BlockSpec(memory_space=pltpu.HBM)
    sem = pl.BlockSpec(memory_space=pltpu.SEMAPHORE)
    return pl.pallas_call(
        body, name=name,
        out_shape=(pltpu.SemaphoreType.DMA((1,)), pltpu.SemaphoreType.DMA((1,)), pltpu.HBM(g.shape, g.dtype),
                   pltpu.HBM(land, g.dtype), jax.ShapeDtypeStruct((8, LANES), F32)),
        in_specs=(hbm, hbm), out_specs=(sem, sem, hbm, hbm, pl.BlockSpec(memory_space=pltpu.VMEM)),
        input_output_aliases={0: 2, 1: 3},
        compiler_params=pltpu.CompilerParams(has_side_effects=pltpu.SideEffectType.DATAFLOW_SIDE_EFFECTING))(
            pltpu.with_memory_space_constraint(g, pltpu.HBM),
            pltpu.with_memory_space_constraint(lax.empty(land, g.dtype), pltpu.HBM))


def _swap_wait(send_sem, recv_sem, g_thru, land_thru, after, name):
    h = g_thru.shape[1] // 2

    def body(g_ref, land_ref, send_sem, recv_sem, after_ref, g_dead, got_ref):
        x, y, c, me, chips, sib = _place()
        cp = _rcopy(g_ref.at[:, pl.ds((1 - c) * h, h), :], land_ref, send_sem.at[0], recv_sem.at[0], sib)
        cp.wait_send()
        cp.wait_recv()

    hbm = pl.BlockSpec(memory_space=pltpu.HBM)
    sem = pl.BlockSpec(memory_space=pltpu.SEMAPHORE)
    return pl.pallas_call(
        body, name=name,
        out_shape=(pltpu.HBM(g_thru.shape, g_thru.dtype), pltpu.HBM(land_thru.shape, land_thru.dtype)),
        in_specs=(hbm, hbm, sem, sem, ANY), out_specs=(hbm, hbm), input_output_aliases={0: 0, 1: 1},
        compiler_params=pltpu.CompilerParams(has_side_effects=pltpu.SideEffectType.DATAFLOW_SIDE_EFFECTING))(
            g_thru, land_thru, send_sem, recv_sem, after)


def _share_half_start(small, name):
    rh = small.shape[0] // 2
    land = (3, rh, small.shape[1])

    def body(s_ref, land_ref, send_sems, recv_sems, s_thru, land_thru, token):
        x, y, c, me, chips, sib = _place()
        for k, chip in enumerate(chips):
            _rcopy(s_ref.at[pl.ds(c * rh, rh)], land_ref.at[k], send_sems.at[k], recv_sems.at[k], (*chip, c)).start()
        token[...] = jnp.zeros_like(token)

    hbm = pl.BlockSpec(memory_space=pltpu.HBM)
    sem = pl.BlockSpec(memory_space=pltpu.SEMAPHORE)
    return pl.pallas_call(
        body, name=name,
        out_shape=(pltpu.SemaphoreType.DMA((3,)), pltpu.SemaphoreType.DMA((3,)), pltpu.HBM(small.shape, small.dtype),
                   pltpu.HBM(land, small.dtype), jax.ShapeDtypeStruct((8, LANES), F32)),
        in_specs=(hbm, hbm), out_specs=(sem, sem, hbm, hbm, pl.BlockSpec(memory_space=pltpu.VMEM)),
        input_output_aliases={0: 2, 1: 3},
        compiler_params=pltpu.CompilerParams(has_side_effects=pltpu.SideEffectType.DATAFLOW_SIDE_EFFECTING))(
            pltpu.with_memory_space_constraint(small, pltpu.HBM),
            pltpu.with_memory_space_constraint(lax.empty(land, small.dtype), pltpu.HBM))


def _share_half_wait(send_sems, recv_sems, s_thru, land_thru, after, name):
    rh = s_thru.shape[0] // 2

    def body(s_ref, land_ref, send_sems, recv_sems, after_ref, s_dead, got_ref):
        x, y, c, me, chips, sib = _place()
        for k, chip in enumerate(chips):
            cp = _rcopy(s_ref.at[pl.ds(c * rh, rh)], land_ref.at[k], send_sems.at[k], recv_sems.at[k], (*chip, c))
            cp.wait_send()
            cp.wait_recv()

    hbm = pl.BlockSpec(memory_space=pltpu.HBM)
    sem = pl.BlockSpec(memory_space=pltpu.SEMAPHORE)
    return pl.pallas_call(
        body, name=name,
        out_shape=(pltpu.HBM(s_thru.shape, s_thru.dtype), pltpu.HBM(land_thru.shape, land_thru.dtype)),
        in_specs=(hbm, hbm, sem, sem, ANY), out_specs=(hbm, hbm), input_output_aliases={0: 0, 1: 1},
        compiler_params=pltpu.CompilerParams(has_side_effects=pltpu.SideEffectType.DATAFLOW_SIDE_EFFECTING))(
            s_thru, land_thru, send_sems, recv_sems, after)


def _join_start(parts, name):
    n = len(parts)

    def body(*refs):
        srcs, (send_sems, recv_sems), token = refs[:n], refs[n:n + 2], refs[-1]
        x, y, c, me, chips, sib = _place()
        for a, src in enumerate(srcs):
            h = src.shape[0] // 2
            mine = src.at[pl.ds(c * h, h)]
            _rcopy(mine, mine, send_sems.at[a], recv_sems.at[a], sib).start()
        token[...] = jnp.zeros_like(token)

    hbm = pl.BlockSpec(memory_space=pltpu.HBM)
    sem = pl.BlockSpec(memory_space=pltpu.SEMAPHORE)
    outs = pl.pallas_call(
        body, name=name,
        out_shape=(pltpu.SemaphoreType.DMA((n,)), pltpu.SemaphoreType.DMA((n,)))
        + tuple(pltpu.HBM(p.shape, p.dtype) for p in parts) + (jax.ShapeDtypeStruct((8, LANES), F32),),
        in_specs=(hbm,) * n, out_specs=(sem, sem) + (hbm,) * n + (pl.BlockSpec(memory_space=pltpu.VMEM),),
        input_output_aliases={a: 2 + a for a in range(n)},
        compiler_params=pltpu.CompilerParams(has_side_effects=pltpu.SideEffectType.DATAFLOW_SIDE_EFFECTING))(
            *[pltpu.with_memory_space_constraint(p, pltpu.HBM) for p in parts])
    return outs[0], outs[1], list(outs[2:2 + n]), outs[-1]


def _join_wait(send_sems, recv_sems, parts, after, name):
    n = len(parts)

    def body(*refs):
        srcs, (send_sems, recv_sems) = refs[:n], refs[n:n + 2]
        x, y, c, me, chips, sib = _place()
        for a, src in enumerate(srcs):
            h = src.shape[0] // 2
            mine, theirs = src.at[pl.ds(c * h, h)], src.at[pl.ds((1 - c) * h, h)]
            _rcopy(mine, theirs, send_sems.at[a], recv_sems.at[a], sib).wait_send()
            _rcopy(theirs, theirs, send_sems.at[a], recv_sems.at[a], sib).wait_recv()

    hbm = pl.BlockSpec(memory_space=pltpu.HBM)
    sem = pl.BlockSpec(memory_space=pltpu.SEMAPHORE)
    return pl.pallas_call(
        body, name=name, out_shape=tuple(pltpu.HBM(p.shape, p.dtype) for p in parts),
        in_specs=(hbm,) * n + (sem, sem, ANY), out_specs=(hbm,) * n, input_output_aliases={a: a for a in range(n)},
        compiler_params=pltpu.CompilerParams(has_side_effects=pltpu.SideEffectType.DATAFLOW_SIDE_EFFECTING))(
            *parts, send_sems, recv_sems, after)


def _pad_rows(a, rows):
    return jnp.pad(a, ((0, rows - a.shape[0]), (0, 0)))


def _stack_rows(parts, multiple):
    padded = [_pad_rows(p, -(-p.shape[0] // 8) * 8) for p in parts]
    starts, at = [], 0
    for p in padded:
        starts.append(at)
        at += p.shape[0]
    total = -(-at // multiple) * multiple
    if total > at:
        padded.append(jnp.zeros((total - at, parts[0].shape[1]), parts[0].dtype))
    return jnp.concatenate(padded, axis=0), starts


def kernel(x, ln_pre_even, w_in_even, pool_w, pool_scale, w_out_even, ln_post_even, ln_pre_odd, w_in_odd, sconv_w, dconv_w, dconv_b, cnorm_g, cnorm_b, w_out_odd, ln_post_odd, loss_target, m_ln_pre_even, m_w_in_even, m_pool_w, m_pool_scale, m_w_out_even, m_ln_post_even, m_ln_pre_odd, m_w_in_odd, m_sconv_w, m_dconv_w, m_dconv_b, m_cnorm_g, m_cnorm_b, m_w_out_odd, m_ln_post_odd, v_ln_pre_even, v_w_in_even, v_pool_w, v_pool_scale, v_w_out_even, v_ln_post_even, v_ln_pre_odd, v_w_in_odd, v_sconv_w, v_dconv_w, v_dconv_b, v_cnorm_g, v_cnorm_b, v_w_out_odd, v_ln_post_odd):
    _, s, d = x.shape
    half = d // 2
    cw = half // N_CHIPS
    ng, q, gd = pool_w.shape[1:]
    k3, k31 = sconv_w.shape[1], dconv_w.shape[1]
    x2d, tgt = x[0], loss_target[0]
    me = 2 * lax.axis_index("x") + lax.axis_index("y")
    core = lax.axis_index("c")
    c_arr = jnp.reshape(core, (1,)).astype(jnp.int32)
    me_arr = jnp.reshape(me, (1,)).astype(jnp.int32)
    mc_arr = jnp.stack([me, core]).astype(jnp.int32)

    shards = [w_in_even[0], w_out_even[0], w_in_odd[0], w_out_odd[0]]
    pool_w_b = _cast_bf16(pool_w[0].reshape(ng * q, gd), "cast_pool_w").reshape(ng, q, gd)
    pack_w, at_w = _stack_rows([sconv_w[0], dconv_w[0], dconv_b, cnorm_g, cnorm_b], 8)
    pack_d, at_d = _stack_rows([ln_pre_odd, ln_post_odd], 8)
    placed = [lax.dynamic_update_slice(jnp.zeros((ng, N_CHIPS * q, gd), BF16), pool_w_b, (0, me * q, 0)),
              lax.dynamic_update_slice(jnp.zeros((pack_w.shape[0], N_CHIPS * cw), F32), pack_w, (0, me * cw)),
              lax.dynamic_update_slice(jnp.zeros((pack_d.shape[0], d), F32), pack_d, (0, me * (d // N_CHIPS)))]
    plans = _Multi([_GatherPieces([_cast_bf16_own_slab(shards[0], me_arr, "cast_w0")], GATHER_PIECES, (0.3, 0.9)),
                    _SmallGatherPlan(placed, (q, cw, d // N_CHIPS))])
    h0, others, extra = _prep(x2d, ln_pre_even, shards[1:], me_arr, "prep_and_gather_first", plans)
    (win_e,), (pool_w_f, pack_w_f, pack_d_f) = plans.results(extra)
    slabs = [None] + others
    sconv_f = pack_w_f[at_w[0]:at_w[0] + k3]
    dconv_f = pack_w_f[at_w[1]:at_w[1] + k31]
    dconv_b_f, cnorm_g_f, cnorm_b_f = (pack_w_f[at_w[n]:at_w[n] + 1] for n in (2, 3, 4))
    ln_pre_odd_f = pack_d_f[at_d[0]:at_d[0] + 1]
    ln_post_odd_f = pack_d_f[at_d[1]:at_d[1] + 1]

    plans = _Multi([_GatherPlan([slabs[1]], at=(0.6, 0.88)), _GatherPlan([slabs[2]], (0, 1, 4), at=(0.6, 0.88))])
    p_e, extra = _mm_nn(h0, win_e, "proj_in_even", plans)
    (wout_e,), (win_o,) = plans.results(extra)
    wout_e = wout_e.reshape(d, d)
    att, ltot, (win_o,) = _sba_fwd(p_e, half, "sba_fwd", _GatherPlan([win_o], (1, 4, 4), at=(0.69, 0.94)))
    y_e = _even_mix_fwd(p_e, att, pool_w_f, pool_scale, d, "even_mix_fwd")
    o_e, x1, h1 = _mm_out_even(y_e, wout_e, x2d, ln_post_even, ln_pre_odd_f, "proj_out_even")
    p_o, (wout_o,) = _mm_nn(h1, win_o, "proj_in_odd", _GatherPlan([slabs[3]]))
    wout_o = wout_o.reshape(d, d)
    y_o, s3, d1 = _odd_mix_fwd(p_o, sconv_f, dconv_f, dconv_b_f, cnorm_g_f, cnorm_b_f, d, "odd_mix_fwd")
    do_o, dx2, loss_blk, dln_post_odd = _mm_out_odd(y_o, wout_o, x1, ln_post_odd_f, tgt, "proj_out_odd_loss")

    dy_o = _mm_nt(do_o, wout_o, "dy_odd")
    g_wout_o = _mm_tn(y_o, do_o, 1, "dw_out_odd")[0].reshape(N_CHIPS, d // N_CHIPS, d)
    (dbc, dgate_o, ds3, dd1, dcnorm_g, dcnorm_b, ddconv_b), (got,) = _odd_bwd_rows(
        p_o, s3, d1, dy_o, cnorm_g_f, cnorm_b_f, d, "odd_bwd_rows", _SwapPlan([g_wout_o]))
    h_wout_o = _half_add(g_wout_o, got, c_arr, "half_add_out_odd")
    dhc, dcc, dga, dgb, dsconv, ddconv = _odd_bwd_conv(p_o, ds3, dd1, sconv_f, dconv_f, d, "odd_bwd_conv")
    dp_o = jnp.concatenate([dhc, dbc, dcc, dga, dgb, dgate_o], axis=1)
    g_win_o, (s_wout_o,) = _mm_tn(h1, dp_o, N_CHIPS, "dw_in_odd", _ScatterPlan([h_wout_o]))
    (dx1, dln_pre_odd, do_e, dln_post_even), (got,) = _mm_in_bwd(
        dp_o, win_o, x1, ln_pre_odd_f, dx2, (o_e, ln_post_even), "dx_odd", _SwapPlan([g_win_o]))
    h_win_o = _half_add(g_win_o, got, c_arr, "half_add_in_odd")

    dy_e = _mm_nt(do_e, wout_e, "dy_even")
    g_wout_e = _mm_tn(y_e, do_e, 1, "dw_out_even")[0].reshape(N_CHIPS, d // N_CHIPS, d)
    (datt, du, dgate_e, dpool_scale, dpool_w), (got,) = _even_mix_bwd(
        p_e, att, dy_e, pool_w_f, pool_scale, d, "even_mix_bwd", _SwapPlan([g_wout_e]))
    h_wout_e = _half_add(g_wout_e, got, c_arr, "half_add_out_even")
    two = lambda v: v.reshape(2, half)
    small_parts = [dpool_scale, two(dln_post_even), two(dln_pre_odd), two(dln_post_odd),
                   dsconv, ddconv, ddconv_b, dcnorm_g, dcnorm_b, dpool_w.reshape(gd, half)]
    small, at_s = _stack_rows(small_parts, 16)
    plans = _Multi([_ScatterPlan([h_win_o]), _SendWholePlan([small])])
    dq, dk, dv, extra = _sba_bwd(p_e, ltot, datt, half, "sba_bwd", plans)
    (s_win_o,), (small1,) = plans.results(extra)
    small2 = _add2(small, small1, "small_add")
    dp_e = jnp.concatenate([dq, dk, dv, du, dgate_e], axis=1)
    plans = _Multi([_ScatterPlan([h_wout_e]), _ShareHalfPlan([small2])])
    g_win_e, extra = _mm_tn(h0, dp_e, N_CHIPS, "dw_in_even", plans)
    (s_wout_e,), (small_got,) = plans.results(extra)
    swap = _swap_start(g_win_e, "swap_in_even_start")
    pairs = [(h_wout_e, s_wout_e), (h_win_o, s_win_o), (h_wout_o, s_wout_o)]
    parts = []
    for n, (h, r) in enumerate(pairs):
        parts.append(_sum_chips(h, r, mc_arr, f"sum_chips{n + 1}", after=parts[-1] if parts else swap[4]))
    g_win_e, got = _swap_wait(*swap[:4], parts[-1], "swap_in_even_wait")
    parts.append(_sum_chips_ordered(small2, small_got, mc_arr, "small_sum"))
    join_sems = _join_start(parts, "join_first_start")
    h_win_e = _half_add(g_win_e, got, c_arr, "half_add_in_even", after=join_sems[3])
    send_sems, recv_sems, h_win_e, landing, token = _scatter_start(h_win_e, "scatter_in_even_start")
    (grad_x, dln_pre_even), _ = _mm_in_bwd(dp_e, win_e, x2d, ln_pre_even + token[0:1, 0:1], dx1, None, "dx_even")

    last, at_l = _stack_rows([two(dln_pre_even), jnp.pad(loss_blk[0:1], ((0, 0), (0, half - LANES)))], 16)
    (last1,) = _swap_with_sibling([], [last], "swap_last")
    last2 = _add2(last, last1, "last_add")
    share = _share_half_start(last2, "share_last_start")
    gw_out_e, gw_in_o, gw_out_o, red = _join_wait(*join_sems[:3], share[4], "join_first_wait")

    def rows(n, cnt):
        return red[at_s[n]:at_s[n] + cnt]

    def mine(a, width):
        return lax.dynamic_slice_in_dim(a, me * width, width, axis=1)

    quarter = d // N_CHIPS
    g_small = {
        "pool_scale": rows(0, 1),
        "ln_post_even": rows(1, 2).reshape(1, d),
        "ln_pre_odd": mine(rows(2, 2).reshape(1, d), quarter),
        "ln_post_odd": mine(rows(3, 2).reshape(1, d), quarter),
        "sconv_w": mine(rows(4, k3), cw),
        "dconv_w": mine(rows(5, k31), cw),
        "dconv_b": mine(rows(6, 1), cw),
        "cnorm_g": mine(rows(7, 1), cw),
        "cnorm_b": mine(rows(8, 1), cw),
        "pool_w": lax.dynamic_slice_in_dim(rows(9, gd).reshape(ng, gd, gd), me * q, q, axis=1).reshape(ng * q, gd),
    }
    w2d = {
        "ln_pre_even": ln_pre_even, "w_in_even": w_in_even[0], "pool_w": pool_w[0].reshape(ng * q, gd),
        "pool_scale": pool_scale, "w_out_even": w_out_even[0], "ln_post_even": ln_post_even, "ln_pre_odd": ln_pre_odd,
        "w_in_odd": w_in_odd[0], "sconv_w": sconv_w[0], "dconv_w": dconv_w[0], "dconv_b": dconv_b, "cnorm_g": cnorm_g,
        "cnorm_b": cnorm_b, "w_out_odd": w_out_odd[0], "ln_post_odd": ln_post_odd,
    }
    moments = {
        "ln_pre_even": (m_ln_pre_even, v_ln_pre_even), "w_in_even": (m_w_in_even, v_w_in_even),
        "pool_w": (m_pool_w, v_pool_w), "pool_scale": (m_pool_scale, v_pool_scale),
        "w_out_even": (m_w_out_even, v_w_out_even), "ln_post_even": (m_ln_post_even, v_ln_post_even),
        "ln_pre_odd": (m_ln_pre_odd, v_ln_pre_odd), "w_in_odd": (m_w_in_odd, v_w_in_odd),
        "sconv_w": (m_sconv_w, v_sconv_w), "dconv_w": (m_dconv_w, v_dconv_w), "dconv_b": (m_dconv_b, v_dconv_b),
        "cnorm_g": (m_cnorm_g, v_cnorm_g), "cnorm_b": (m_cnorm_b, v_cnorm_b),
        "w_out_odd": (m_w_out_odd, v_w_out_odd), "ln_post_odd": (m_ln_post_odd, v_ln_post_odd),
    }
    def update(name, g):
        m_in, v_in = moments[name]
        w = w2d[name]
        return _adamw(w, g, m_in.reshape(w.shape), v_in.reshape(w.shape), "adamw_" + name)[0]

    updates = {name: update(name, g) for name, g in (("w_in_odd", gw_in_o), ("w_out_even", gw_out_e), ("w_out_odd", gw_out_o))}
    last2, last_got = _share_half_wait(*share[:4], updates["w_out_odd"][1], "share_last_wait")
    last_sum = _sum_chips_ordered(last2, last_got, mc_arr, "last_sum")
    h_win_e, s_win_e = _scatter_wait(send_sems, recv_sems, h_win_e, landing, last_sum, "scatter_in_even_wait")
    last_sems = _join_start([_sum_chips(h_win_e, s_win_e, mc_arr, "sum_chips0"), last_sum], "join_last_start")
    for name, g in g_small.items():
        updates[name] = update(name, g)
    gw_in_e, red_last = _join_wait(*last_sems[:3], updates["pool_w"][1], "join_last_wait")
    loss = red_last[at_l[1], 0]
    updates["ln_pre_even"] = update("ln_pre_even", red_last[at_l[0]:at_l[0] + 2].reshape(1, d))
    updates["w_in_even"] = update("w_in_even", gw_in_e)
    outs = [[u.reshape(moments[name][0].shape) for u in updates[name]] for name in w2d]
    grads_out, deltas, new_m, new_v = zip(*outs)
    return (loss, grad_x.reshape(x.shape), *grads_out, *deltas, *new_m, *new_v)
```

```python
import functools
import math

import jax
import jax.numpy as jnp
from jax import lax
from jax.experimental import pallas as pl
from jax.experimental.pallas import tpu as pltpu

F32 = jnp.float32
BF16 = jnp.bfloat16
EPS = 1e-6
N_CHIPS = 4
VMEM_LIMIT_V7X = 56 << 20
HEAD_DIM = 128
ATT_BLOCK = 256
POOL_WINDOWS = (2, 4, 8, 16)
ROW_TILE = 256
POOL_HALO = 16
CONV_HALO = 32
LANES = 128
ADAM_LR, ADAM_B1, ADAM_B2, ADAM_EPS, ADAM_WD, ADAM_STEP = 0.001, 0.9, 0.999, 1e-08, 0.01, 10
MESH_ID = pl.DeviceIdType.MESH
ANY = pl.BlockSpec(memory_space=pl.ANY)


def _cp(*sem):
    return pltpu.CompilerParams(dimension_semantics=sem or None, vmem_limit_bytes=VMEM_LIMIT_V7X)


def _pick_tile(n, cap):
    best = None
    for t in range(LANES, min(n, cap) + 1, LANES):
        if n % t == 0:
            best = t
    assert best is not None, (n, cap)
    return best


def _sigmoid(x):
    return 1.0 / (1.0 + jnp.exp(-x))


def _silu(x):
    return x * _sigmoid(x)


def _dsilu(x):
    s = _sigmoid(x)
    return s * (1.0 + x * (1.0 - s))


def _log_sigmoid(z):
    return jnp.minimum(z, 0.0) - jnp.log(1.0 + jnp.exp(-jnp.abs(z)))


def _rms_stats(x):
    r = lax.rsqrt(jnp.mean(x * x, axis=-1, keepdims=True) + EPS)
    return x * r, r


def _rms_bwd(dh, xhat, r, g):
    dxh = dh * g
    dx = r * (dxh - xhat * jnp.mean(dxh * xhat, axis=-1, keepdims=True))
    return dx, jnp.sum(dh * xhat, axis=0, keepdims=True)


def _acc_rows(ref, first, val):
    @pl.when(first)
    def _():
        ref[...] = val

    @pl.when(jnp.logical_not(first))
    def _():
        ref[...] += val


def _rcopy(src, dst, ssem, rsem, dev):
    return pltpu.make_async_remote_copy(src_ref=src, dst_ref=dst, send_sem=ssem, recv_sem=rsem,
                                        device_id=dev, device_id_type=MESH_ID)


def _place():
    x, y, c = lax.axis_index("x"), lax.axis_index("y"), lax.axis_index("c")
    chips = [(1 - x, y), (x, 1 - y), (1 - x, 1 - y)]
    return x, y, c, 2 * x + y, chips, (x, y, 1 - c)


class _GatherPlan:
    PER_ARRAY = 7

    def __init__(self, arrays, part=(0, 1, 1), at=(0.5, 0.8)):
        self.operands = list(arrays)
        self.out_shapes = [jax.ShapeDtypeStruct(a.shape, a.dtype) for a in arrays]
        self.aliases = {i: i for i in range(len(arrays))}
        self.nsems = self.PER_ARRAY * len(arrays)
        self.base = 0
        self.halves = [a.shape[1] // 2 for a in arrays]
        self.part = part
        self.at = at

    def schedule(self):
        return [(0.0, self.start), (self.at[0], self.relay), (self.at[1], self.relay_far)]

    def _rows(self, ref, a, chip, half, quarter=None):
        lo, hi, n = self.part
        h = self.halves[a]
        first, size = half * h + lo * h // n, (hi - lo) * h // n
        if quarter is not None:
            first, size = first + quarter * (size // 2), size // 2
        return ref.at[chip, pl.ds(first, size)]

    def _copy(self, src, dst, a, n, ssem, rsem, dev):
        return _rcopy(src, dst, ssem.at[self.base + self.PER_ARRAY * a + n], rsem.at[self.base + self.PER_ARRAY * a + n], dev)

    def _own(self, ins, outs, ssem, rsem):
        x, y, c, me, chips, sib = _place()
        return [self._copy(self._rows(ins[a], a, me, c), self._rows(outs[a], a, me, c), a, k, ssem, rsem, (*chips[k], c))
                for a in range(len(ins)) for k in (0, 1)]

    def _relays(self, outs, ssem, rsem, a, k):
        x, y, c, me, chips, sib = _place()
        chip = 2 * chips[k][0] + chips[k][1]
        whole, quarter = self._rows(outs[a], a, chip, c), self._rows(outs[a], a, chip, c, k)
        return (self._copy(whole, whole, a, k, ssem, rsem, (*chips[k], c)),
                self._copy(quarter, quarter, a, 2 + k, ssem, rsem, (*chips[1 - k], c)),
                self._copy(whole, whole, a, 4 + k, ssem, rsem, sib))

    def _far(self, outs, ssem, rsem, a):
        x, y, c, me, chips, sib = _place()
        chip = 2 * chips[2][0] + chips[2][1]
        whole = self._rows(outs[a], a, chip, c)
        got = [self._copy(q, q, a, 2 + k, ssem, rsem, (*chips[1 - k], c))
               for k, q in enumerate([self._rows(outs[a], a, chip, c, 0), self._rows(outs[a], a, chip, c, 1)])]
        return got, self._copy(whole, whole, a, 6, ssem, rsem, sib)

    def start(self, ins, outs, ssem, rsem):
        for cp in self._own(ins, outs, ssem, rsem):
            cp.start()

    def relay(self, ins, outs, ssem, rsem):
        for a in range(len(outs)):
            for k in (0, 1):
                landed, onward, to_sibling = self._relays(outs, ssem, rsem, a, k)
                landed.wait_recv()
                onward.start()
                to_sibling.start()

    def relay_far(self, ins, outs, ssem, rsem):
        for a in range(len(outs)):
            got, to_sibling = self._far(outs, ssem, rsem, a)
            for cp in got:
                cp.wait_recv()
            to_sibling.start()

    def finish(self, ins, outs, ssem, rsem):
        x, y, c, me, chips, sib = _place()
        for a in range(len(outs)):
            for k in range(3):
                ref = self._rows(outs[a], a, 2 * chips[k][0] + chips[k][1], 1 - c)
                self._copy(ref, ref, a, 4 + k, ssem, rsem, sib).wait_recv()
        for cp in self._own(ins, outs, ssem, rsem):
            cp.wait_send()
        for a in range(len(outs)):
            for k in (0, 1):
                _, onward, to_sibling = self._relays(outs, ssem, rsem, a, k)
                onward.wait_send()
                to_sibling.wait_send()
            self._far(outs, ssem, rsem, a)[1].wait_send()


class _ScatterPlan:
    def __init__(self, arrays, part=(0, 1, 1), into=None):
        self.n = len(arrays)
        self.operands = list(arrays) + list(into or [])
        self.out_shapes = [jax.ShapeDtypeStruct((3,) + a.shape[1:], a.dtype) for a in arrays]
        self.aliases = {self.n + i: i for i in range(self.n)} if into else {}
        self.nsems = 3 * self.n
        self.base = 0
        self.part = part

    def _copies(self, ins, outs, ssem, rsem):
        x, y, c, me, chips, sib = _place()
        lo, hi, n = self.part
        out = []
        for a in range(self.n):
            h = ins[a].shape[1]
            rows = pl.ds(lo * h // n, (hi - lo) * h // n)
            for k, chip in enumerate(chips):
                out.append(_rcopy(ins[a].at[2 * chip[0] + chip[1], rows], outs[a].at[k, rows],
                                  ssem.at[self.base + 3 * a + k], rsem.at[self.base + 3 * a + k], (*chip, c)))
        return out

    def schedule(self):
        return [(0.0, self.start)]

    def start(self, ins, outs, ssem, rsem):
        for cp in self._copies(ins, outs, ssem, rsem):
            cp.start()

    def finish(self, ins, outs, ssem, rsem):
        cps = self._copies(ins, outs, ssem, rsem)
        for cp in cps:
            cp.wait_recv()
        for cp in cps:
            cp.wait_send()


class _ShareHalfPlan(_ScatterPlan):
    def __init__(self, arrays):
        super().__init__(arrays)
        self.out_shapes = [jax.ShapeDtypeStruct((3, a.shape[0] // 2, a.shape[1]), a.dtype) for a in arrays]

    def _copies(self, ins, outs, ssem, rsem):
        x, y, c, me, chips, sib = _place()
        out = []
        for a in range(self.n):
            rh = ins[a].shape[0] // 2
            for k, chip in enumerate(chips):
                out.append(_rcopy(ins[a].at[pl.ds(c * rh, rh)], outs[a].at[k],
                                  ssem.at[self.base + 3 * a + k], rsem.at[self.base + 3 * a + k], (*chip, c)))
        return out


class _SwapPlan:
    def __init__(self, grads):
        self.operands = list(grads)
        self.out_shapes = [jax.ShapeDtypeStruct((g.shape[0], g.shape[1] // 2, g.shape[2]), g.dtype) for g in grads]
        self.aliases = {}
        self.nsems = len(grads)
        self.base = 0

    def _copies(self, ins, outs, ssem, rsem):
        x, y, c, me, chips, sib = _place()
        out = []
        for a, src in enumerate(ins):
            h = src.shape[1] // 2
            out.append(_rcopy(src.at[:, pl.ds((1 - c) * h, h), :], outs[a], ssem.at[self.base + a], rsem.at[self.base + a], sib))
        return out

    def schedule(self):
        return [(0.0, self.start)]

    def start(self, ins, outs, ssem, rsem):
        for cp in self._copies(ins, outs, ssem, rsem):
            cp.start()

    def finish(self, ins, outs, ssem, rsem):
        cps = self._copies(ins, outs, ssem, rsem)
        for cp in cps:
            cp.wait_recv()
        for cp in cps:
            cp.wait_send()


class _SendWholePlan(_SwapPlan):
    def __init__(self, arrays):
        self.operands = list(arrays)
        self.out_shapes = [jax.ShapeDtypeStruct(a.shape, a.dtype) for a in arrays]
        self.aliases = {}
        self.nsems = len(arrays)
        self.base = 0

    def _copies(self, ins, outs, ssem, rsem):
        x, y, c, me, chips, sib = _place()
        return [_rcopy(src, outs[a], ssem.at[self.base + a], rsem.at[self.base + a], sib) for a, src in enumerate(ins)]


class _GatherPieces:
    def __init__(self, arrays, n, at):
        self.pieces = [_GatherPlan(arrays, (j, j + 1, n), at) for j in range(n)]
        self.operands, self.out_shapes, self.aliases = self.pieces[0].operands, self.pieces[0].out_shapes, self.pieces[0].aliases
        self.nsems = sum(p.nsems for p in self.pieces)
        self.at = at
        self.base = 0

    @property
    def base(self):
        return self.pieces[0].base

    @base.setter
    def base(self, value):
        for j, p in enumerate(self.pieces):
            p.base = value + j * p.nsems

    def schedule(self):
        return [(0.0, self.start), (self.at[0], self.relay), (self.at[1], self.relay_far)]

    def _each(self, what, *a):
        for p in self.pieces:
            getattr(p, what)(*a)

    def start(self, *a):
        self._each("start", *a)

    def relay(self, *a):
        self._each("relay", *a)

    def relay_far(self, *a):
        self._each("relay_far", *a)

    def finish(self, *a):
        self._each("finish", *a)


class _SmallGatherPlan:
    def __init__(self, arrays, widths):
        self.operands = list(arrays)
        self.out_shapes = [jax.ShapeDtypeStruct(a.shape, a.dtype) for a in arrays]
        self.aliases = {i: i for i in range(3)}
        self.nsems = 9
        self.base = 0
        self.widths = widths

    def _part(self, ref, n, chip):
        w = self.widths[n]
        return ref.at[:, pl.ds(chip * w, w), :] if n == 0 else ref.at[:, pl.ds(chip * w, w)]

    def _copies(self, ins, outs, ssem, rsem, own):
        x, y, c, me, chips, sib = _place()
        out = []
        for n in range(3):
            for k, chip in enumerate(chips):
                which = me if own else 2 * chip[0] + chip[1]
                out.append(_rcopy(self._part(ins[n], n, which), self._part(outs[n], n, which),
                                  ssem.at[self.base + 3 * n + k], rsem.at[self.base + 3 * n + k], (*chip, c)))
        return out

    def schedule(self):
        return [(0.0, self.start)]

    def start(self, ins, outs, ssem, rsem):
        for cp in self._copies(ins, outs, ssem, rsem, True):
            cp.start()

    def finish(self, ins, outs, ssem, rsem):
        for cp in self._copies(ins, outs, ssem, rsem, False):
            cp.wait_recv()
        for cp in self._copies(ins, outs, ssem, rsem, True):
            cp.wait_send()


class _Multi:
    def __init__(self, plans):
        self.plans = plans
        self.operands, self.out_shapes, self.aliases, self.nsems = [], [], {}, 0
        self.spans = []
        for p in plans:
            ni, no = len(self.operands), len(self.out_shapes)
            self.spans.append((ni, ni + len(p.operands), no, no + len(p.out_shapes)))
            self.aliases.update({ni + i: no + j for i, j in p.aliases.items()})
            p.base = self.nsems
            self.nsems += p.nsems
            self.operands += p.operands
            self.out_shapes += p.out_shapes

    def schedule(self):
        def bound(fn, span):
            i0, i1, o0, o1 = span
            return lambda ins, outs, ssem, rsem: fn(ins[i0:i1], outs[o0:o1], ssem, rsem)

        stages = [(at, bound(fn, span)) for p, span in zip(self.plans, self.spans) for at, fn in p.schedule()]
        return sorted(stages, key=lambda s: s[0])

    def finish(self, ins, outs, ssem, rsem):
        for p, (i0, i1, o0, o1) in zip(self.plans, self.spans):
            p.finish(ins[i0:i1], outs[o0:o1], ssem, rsem)

    def results(self, extra):
        return [list(extra[o0:o1]) for (_, _, o0, o1) in self.spans]


class _Host:
    def __init__(self, comm, in_specs, out_specs, out_shape, scratch, prefetch=0):
        self.comm = comm
        self.n_in, self.n_out = len(in_specs), len(out_specs)
        self.in_specs, self.out_specs, self.out_shape, self.scratch = list(in_specs), list(out_specs), list(out_shape), list(scratch)
        self.aliases = {}
        self.args = []
        if comm is not None:
            self.in_specs += [ANY] * len(comm.operands)
            self.out_specs += [ANY] * len(comm.out_shapes)
            self.out_shape += comm.out_shapes
            self.scratch += [pltpu.SemaphoreType.DMA((comm.nsems,)), pltpu.SemaphoreType.DMA((comm.nsems,))]
            self.aliases = {prefetch + self.n_in + i: self.n_out + j for i, j in comm.aliases.items()}
            self.args = list(comm.operands)

    def split(self, refs):
        nc = len(self.args)
        nco = len(self.out_shape) - self.n_out
        ins, p = refs[:self.n_in], self.n_in + nc
        outs, rest = refs[p:p + self.n_out], refs[p + self.n_out + nco:]
        self._cargs = None
        if self.comm is not None:
            self._cargs = (refs[self.n_in:p], refs[p + self.n_out:p + self.n_out + nco], rest[-2], rest[-1])
            rest = rest[:-2]
        return ins, outs, rest

    def before(self, step, total):
        if self.comm is None:
            return

        for at, stage in self.comm.schedule():
            pl.when(step == min(total - 1, int(at * total)))(functools.partial(stage, *self._cargs))

    def after(self, step, total):
        if self.comm is None:
            return

        @pl.when(step == total - 1)
        def _():
            self.comm.finish(*self._cargs)

    def results(self, outs):
        return outs[:self.n_out], outs[self.n_out:]


def _cast_bf16(x, name):
    r, c = x.shape
    tr = ROW_TILE if r % ROW_TILE == 0 else r

    def body(x_ref, o_ref):
        o_ref[...] = x_ref[...].astype(BF16)

    return pl.pallas_call(
        body, name=name, grid=(r // tr,),
        in_specs=[pl.BlockSpec((tr, c), lambda i: (i, 0))],
        out_specs=pl.BlockSpec((tr, c), lambda i: (i, 0)),
        out_shape=jax.ShapeDtypeStruct((r, c), BF16), compiler_params=_cp("parallel"))(x)


def _cast_bf16_own_slab(x, me_arr, name):
    r, c = x.shape
    tr = ROW_TILE if r % ROW_TILE == 0 else r

    def body(me_ref, x_ref, o_ref):
        o_ref[...] = x_ref[...].astype(BF16)

    return pl.pallas_call(
        body, name=name,
        grid_spec=pltpu.PrefetchScalarGridSpec(
            num_scalar_prefetch=1, grid=(r // tr,),
            in_specs=[pl.BlockSpec((tr, c), lambda i, me: (i, 0))],
            out_specs=pl.BlockSpec((None, tr, c), lambda i, me: (me[0], i, 0))),
        out_shape=jax.ShapeDtypeStruct((N_CHIPS, r, c), BF16), compiler_params=_cp("parallel"))(me_arr, x)


def _prep(x, g, shards, me_arr, name, comm):
    s, d = x.shape
    steps = s // ROW_TILE
    tiles = [(w.shape[0] // steps, w.shape[1]) for w in shards]
    assert all(w.shape[0] % steps == 0 for w in shards)
    in_specs = [pl.BlockSpec((ROW_TILE, d), lambda i, me: (i, 0)), pl.BlockSpec((1, d), lambda i, me: (0, 0))]
    in_specs += [pl.BlockSpec(t, lambda i, me: (i, 0)) for t in tiles]
    out_specs = [pl.BlockSpec((ROW_TILE, d), lambda i, me: (i, 0))]
    out_specs += [pl.BlockSpec((None,) + t, lambda i, me: (me[0], i, 0)) for t in tiles]
    out_shape = [jax.ShapeDtypeStruct((s, d), BF16)] + [jax.ShapeDtypeStruct((N_CHIPS,) + w.shape, BF16) for w in shards]
    host = _Host(comm, in_specs, out_specs, out_shape, [], prefetch=1)

    def body(me_ref, *refs):
        (x_ref, g_ref, *w_refs), (h_ref, *slab_refs), _ = host.split(refs)
        step = pl.program_id(0)
        host.before(step, steps)
        xhat, _ = _rms_stats(x_ref[...])
        h_ref[...] = (xhat * g_ref[...]).astype(BF16)
        for w_ref, slab_ref in zip(w_refs, slab_refs):
            slab_ref[...] = w_ref[...].astype(BF16)
        host.after(step, steps)

    outs = pl.pallas_call(
        body, name=name,
        grid_spec=pltpu.PrefetchScalarGridSpec(num_scalar_prefetch=1, grid=(steps,), in_specs=host.in_specs,
                                               out_specs=host.out_specs, scratch_shapes=host.scratch),
        out_shape=host.out_shape, input_output_aliases=host.aliases,
        compiler_params=_cp("arbitrary"))(me_arr, x, g, *shards, *host.args)
    (h, *slabs), extra = host.results(outs)
    return h, slabs, extra


def _mm_nn(a, w3, name, comm=None):
    m, k = a.shape
    nsh, _, ns = w3.shape
    tm = 512 if m % 512 == 0 else ROW_TILE
    tn = _pick_tile(ns, 1024)
    per = ns // tn
    grid = (nsh * per, m // tm)
    host = _Host(comm,
                 [pl.BlockSpec((tm, k), lambda n, i: (i, 0)), pl.BlockSpec((None, k, tn), lambda n, i: (n // per, 0, n % per))],
                 [pl.BlockSpec((tm, tn), lambda n, i: (i, n))], [jax.ShapeDtypeStruct((m, nsh * ns), F32)], [])

    def body(*refs):
        (a_ref, w_ref), (o_ref,), _ = host.split(refs)
        step = pl.program_id(0) * grid[1] + pl.program_id(1)
        host.before(step, grid[0] * grid[1])
        o_ref[...] = jnp.dot(a_ref[...], w_ref[...], preferred_element_type=F32)
        host.after(step, grid[0] * grid[1])

    outs = pl.pallas_call(
        body, name=name, grid=grid, in_specs=host.in_specs, out_specs=host.out_specs, out_shape=host.out_shape,
        scratch_shapes=host.scratch, input_output_aliases=host.aliases,
        compiler_params=_cp("arbitrary", "arbitrary"))(a, w3, *host.args)
    (out,), extra = host.results(outs)
    return out, extra


def _mm_nt(a, b, name):
    m, k = a.shape
    n = b.shape[0]
    tm = 512 if m % 512 == 0 else ROW_TILE

    def body(a_ref, b_ref, o_ref):
        o_ref[...] = lax.dot_general(a_ref[...], b_ref[...], (((1,), (1,)), ((), ())), preferred_element_type=F32)

    return pl.pallas_call(
        body, name=name, grid=(m // tm,),
        in_specs=[pl.BlockSpec((tm, k), lambda i: (i, 0)), pl.BlockSpec((n, k), lambda i: (0, 0))],
        out_specs=pl.BlockSpec((tm, n), lambda i: (i, 0)),
        out_shape=jax.ShapeDtypeStruct((m, n), F32), compiler_params=_cp("parallel"))(a, b)


def _mm_tn(a, b, nsh, name, comm=None):
    s, m = a.shape
    n = b.shape[1]
    ns = n // nsh
    tm = 512 if m % 512 == 0 else ROW_TILE
    tn = _pick_tile(ns, 1024)
    per = ns // tn
    grid = (nsh * per, m // tm)
    host = _Host(comm, [pl.BlockSpec((s, tm), lambda j, i: (0, i)), pl.BlockSpec((s, tn), lambda j, i: (0, j))],
                 [pl.BlockSpec((None, tm, tn), lambda j, i: (j // per, i, j % per))],
                 [jax.ShapeDtypeStruct((nsh, m, ns), BF16)], [])

    def body(*refs):
        (a_ref, b_ref), (o_ref,), _ = host.split(refs)
        step = pl.program_id(0) * grid[1] + pl.program_id(1)
        host.before(step, grid[0] * grid[1])
        o_ref[...] = lax.dot_general(a_ref[...], b_ref[...], (((0,), (0,)), ((), ())),
                                     preferred_element_type=F32).astype(BF16)
        host.after(step, grid[0] * grid[1])

    outs = pl.pallas_call(
        body, name=name, grid=grid, in_specs=host.in_specs, out_specs=host.out_specs, out_shape=host.out_shape,
        scratch_shapes=host.scratch, input_output_aliases=host.aliases,
        compiler_params=_cp("arbitrary", "arbitrary"))(a, b, *host.args)
    (out,), extra = host.results(outs)
    return out, extra


def _tri(n, rel):
    row = lax.broadcasted_iota(jnp.int32, (2 * n, n), 0)
    col = lax.broadcasted_iota(jnp.int32, (2 * n, n), 1)
    return jnp.where(rel(jnp.where(row >= n, row - n, row), col), 1.0, 0.0).astype(BF16)


def _dot_split(x, tri2):
    hi = x.astype(BF16)
    lo = (x - hi.astype(F32)).astype(BF16)
    return jnp.dot(jnp.concatenate([hi, lo], axis=1), tri2, preferred_element_type=F32)


def _nt(a, b):
    return lax.dot_general(a, b, (((1,), (1,)), ((), ())), preferred_element_type=F32)


def _tn(a, b):
    return lax.dot_general(a, b, (((0,), (0,)), ((), ())), preferred_element_type=F32)


def _heads_per_step(nh):
    return max(h for h in (1, 2, 4) if nh % h == 0)


def _sba_fwd(p, sbw, name, comm=None):
    s = p.shape[0]
    nh = sbw // HEAD_DIM
    hp = _heads_per_step(nh)
    ngrp, hw = nh // hp, hp * HEAD_DIM
    blk = ATT_BLOCK
    nq = s // blk
    scale = 1.0 / math.sqrt(HEAD_DIM)
    host = _Host(comm,
                 [pl.BlockSpec((blk, hw), lambda g, i: (i, g)),
                  pl.BlockSpec((s, hw), lambda g, i: (0, ngrp + g)),
                  pl.BlockSpec((s, hw), lambda g, i: (0, 2 * ngrp + g))],
                 [pl.BlockSpec((blk, hw), lambda g, i: (i, g))] * 2,
                 [jax.ShapeDtypeStruct((s, sbw), F32)] * 2,
                 [pltpu.VMEM((s, hw), BF16)] * 2)

    def body(*refs):
        (q_ref, k_ref, v_ref), (o_ref, lt_ref), (kb_ref, vb_ref) = host.split(refs)
        i = pl.program_id(1)
        step = pl.program_id(0) * nq + i
        host.before(step, ngrp * nq)

        @pl.when(i == 0)
        def _():
            kb_ref[...] = k_ref[...].astype(BF16)
            vb_ref[...] = v_ref[...].astype(BF16)

        heads = [slice(h * HEAD_DIM, (h + 1) * HEAD_DIM) for h in range(hp)]
        qs = [q_ref[:, hd].astype(BF16) for hd in heads]
        later = _tri(blk, lambda r, c: r > c)
        causal = lax.broadcasted_iota(jnp.int32, (blk, blk), 1) < lax.broadcasted_iota(jnp.int32, (blk, blk), 0)

        def key_block(j, carry, diagonal):
            rows = pl.ds(pl.multiple_of(j * blk, blk), blk)
            hs = range(hp)
            z = [_nt(qs[h], kb_ref[rows, heads[h]]) * scale for h in hs]
            ls = [_log_sigmoid(z[h]) for h in hs]
            lm = [jnp.where(causal, ls[h] - z[h], 0.0) if diagonal else ls[h] - z[h] for h in hs]
            stay = [_dot_split(lm[h], later) for h in hs]
            w = [jnp.exp(ls[h] + stay[h] + carry[h][1]) for h in hs]
            if diagonal:
                w = [jnp.where(causal, w[h], 0.0) for h in hs]
            acc = [carry[h][0] + jnp.dot(w[h].astype(BF16), vb_ref[rows, heads[h]], preferred_element_type=F32) for h in hs]
            return tuple((acc[h], carry[h][1] + jnp.sum(lm[h], axis=1, keepdims=True)) for h in hs)

        init = tuple((jnp.zeros((blk, HEAD_DIM), F32), jnp.zeros((blk, 1), F32)) for _ in heads)
        carry = key_block(i, init, True)
        carry = lax.fori_loop(0, i, lambda n, c: key_block(i - 1 - n, c, False), carry)
        for h, hd in enumerate(heads):
            o_ref[:, hd] = carry[h][0]
            lt_ref[:, hd] = jnp.broadcast_to(carry[h][1], (blk, HEAD_DIM))
        host.after(step, ngrp * nq)

    outs = pl.pallas_call(
        body, name=name, grid=(ngrp, nq), in_specs=host.in_specs, out_specs=host.out_specs, out_shape=host.out_shape,
        scratch_shapes=host.scratch, input_output_aliases=host.aliases,
        compiler_params=_cp("arbitrary", "arbitrary"))(p, p, p, *host.args)
    (out, ltot), extra = host.results(outs)
    return out, ltot, extra


def _sba_bwd(p, ltot, dout, sbw, name, comm=None):
    s = p.shape[0]
    nh = sbw // HEAD_DIM
    hp = _heads_per_step(nh)
    ngrp, hw = nh // hp, hp * HEAD_DIM
    blk = ATT_BLOCK
    nq = s // blk
    scale = 1.0 / math.sqrt(HEAD_DIM)
    blk_spec = pl.BlockSpec((blk, hw), lambda g, i: (i, g))
    col_spec = pl.BlockSpec((s, hw), lambda g, i: (0, g))
    host = _Host(comm,
                 [blk_spec, pl.BlockSpec((s, hw), lambda g, i: (0, ngrp + g)),
                  pl.BlockSpec((s, hw), lambda g, i: (0, 2 * ngrp + g)), blk_spec, blk_spec],
                 [blk_spec, col_spec, col_spec], [jax.ShapeDtypeStruct((s, sbw), BF16)] * 3,
                 [pltpu.VMEM((s, hw), BF16)] * 2 + [pltpu.VMEM((s, hw), F32)] * 2)

    def body(*refs):
        (q_ref, k_ref, v_ref, lt_ref, do_ref), (dq_ref, dk_ref, dv_ref), (kb_ref, vb_ref, dka_ref, dva_ref) = host.split(refs)
        i = pl.program_id(1)
        step = pl.program_id(0) * nq + i
        host.before(step, ngrp * nq)

        @pl.when(i == 0)
        def _():
            kb_ref[...] = k_ref[...].astype(BF16)
            vb_ref[...] = v_ref[...].astype(BF16)
            dka_ref[...] = jnp.zeros_like(dka_ref)
            dva_ref[...] = jnp.zeros_like(dva_ref)

        heads = [slice(h * HEAD_DIM, (h + 1) * HEAD_DIM) for h in range(hp)]
        qs = [q_ref[:, hd].astype(BF16) for hd in heads]
        dos = [do_ref[:, hd].astype(BF16) for hd in heads]
        ltots = [lt_ref[:, h * HEAD_DIM:h * HEAD_DIM + 1] for h in range(hp)]
        upto = _tri(blk, lambda r, c: r <= c)
        before = _tri(blk, lambda r, c: r < c)
        causal = lax.broadcasted_iota(jnp.int32, (blk, blk), 1) < lax.broadcasted_iota(jnp.int32, (blk, blk), 0)

        def key_block(j, carry, diagonal):
            rows = pl.ds(pl.multiple_of(j * blk, blk), blk)
            hs = range(hp)
            kj = [kb_ref[rows, heads[h]] for h in hs]
            vj = [vb_ref[rows, heads[h]] for h in hs]
            z = [_nt(qs[h], kj[h]) * scale for h in hs]
            dw = [_nt(dos[h], vj[h]) for h in hs]
            ls = [_log_sigmoid(z[h]) for h in hs]
            lm = [jnp.where(causal, ls[h] - z[h], 0.0) if diagonal else ls[h] - z[h] for h in hs]
            stay = [ltots[h] - carry[h][1] - _dot_split(lm[h], upto) for h in hs]
            w = [jnp.exp(ls[h] + stay[h]) for h in hs]
            if diagonal:
                w = [jnp.where(causal, w[h], 0.0) for h in hs]
            da = [dw[h] * w[h] for h in hs]
            sig = [jnp.exp(ls[h]) for h in hs]
            chain = [sig[h] * (carry[h][2] + _dot_split(da[h], before)) for h in hs]
            if diagonal:
                chain = [jnp.where(causal, chain[h], 0.0) for h in hs]
            dzb = [((da[h] * (1.0 - sig[h]) - chain[h]) * scale).astype(BF16) for h in hs]
            dq = [carry[h][0] + jnp.dot(dzb[h], kj[h], preferred_element_type=F32) for h in hs]
            for h in hs:
                dka_ref[rows, heads[h]] += _tn(dzb[h], qs[h])
            for h in hs:
                dva_ref[rows, heads[h]] += _tn(w[h].astype(BF16), dos[h])
            return tuple((dq[h], carry[h][1] + jnp.sum(lm[h], axis=1, keepdims=True),
                          carry[h][2] + jnp.sum(da[h], axis=1, keepdims=True)) for h in hs)

        zero = jnp.zeros((blk, 1), F32)
        init = tuple((jnp.zeros((blk, HEAD_DIM), F32), zero, zero) for _ in heads)
        carry = lax.fori_loop(0, i, lambda j, c: key_block(j, c, False), init)
        carry = key_block(i, carry, True)
        for h, hd in enumerate(heads):
            dq_ref[:, hd] = carry[h][0].astype(BF16)

        @pl.when(i == nq - 1)
        def _():
            dk_ref[...] = dka_ref[...].astype(BF16)
            dv_ref[...] = dva_ref[...].astype(BF16)

        host.after(step, ngrp * nq)

    outs = pl.pallas_call(
        body, name=name, grid=(ngrp, nq), in_specs=host.in_specs, out_specs=host.out_specs, out_shape=host.out_shape,
        scratch_shapes=host.scratch, input_output_aliases=host.aliases,
        compiler_params=_cp("arbitrary", "arbitrary"))(p, p, p, ltot, dout, *host.args)
    (dq, dk, dv), extra = host.results(outs)
    return dq, dk, dv, extra


def _pool_groups(pad_ref, tile, row0, gd, halo):
    row = row0 + lax.broadcasted_iota(jnp.int32, (tile, 1), 0)
    out = []
    for gi, win in enumerate(POOL_WINDOWS):
        cs = slice(gi * gd, (gi + 1) * gd)
        tok = pad_ref[halo:halo + tile, cs]
        acc = tok
        for j in range(1, win):
            acc = acc + pad_ref[halo - j:halo - j + tile, cs]
        cnt = jnp.minimum(win, row + 1).astype(F32)
        out.append(acc / cnt - tok)
    return out


def _even_mix_fwd(p, att, pool_w, pool_scale, d, name):
    s = p.shape[0]
    half = d // 2
    gd = half // len(POOL_WINDOWS)
    t, hb = ROW_TILE, POOL_HALO

    def body(u_ref, uh_ref, g_ref, a_ref, pw_ref, sc_ref, y_ref, pad_ref):
        i = pl.program_id(0)
        pad_ref[0:hb, :] = jnp.where(i > 0, uh_ref[...], 0.0)
        pad_ref[hb:, :] = u_ref[...]
        pooled = _pool_groups(pad_ref, t, i * t, gd, hb)
        for gi in range(len(POOL_WINDOWS)):
            cs = slice(gi * gd, (gi + 1) * gd)
            po = jnp.dot(pooled[gi].astype(BF16), pw_ref[gi], preferred_element_type=F32) * sc_ref[:, cs]
            y_ref[:, half + gi * gd:half + (gi + 1) * gd] = (po * _silu(g_ref[:, half + gi * gd:half + (gi + 1) * gd])).astype(BF16)
        y_ref[:, :half] = (a_ref[...] * _silu(g_ref[:, :half])).astype(BF16)

    return pl.pallas_call(
        body, name=name, grid=(s // t,),
        in_specs=[pl.BlockSpec((t, half), lambda i: (i, 3)),
                  pl.BlockSpec((hb, half), lambda i: (jnp.maximum(i * (t // hb) - 1, 0), 3)),
                  pl.BlockSpec((t, d), lambda i: (i, 2)),
                  pl.BlockSpec((t, half), lambda i: (i, 0)),
                  pl.BlockSpec(pool_w.shape, lambda i: (0, 0, 0)),
                  pl.BlockSpec((1, half), lambda i: (0, 0))],
        out_specs=pl.BlockSpec((t, d), lambda i: (i, 0)),
        out_shape=jax.ShapeDtypeStruct((s, d), BF16),
        scratch_shapes=[pltpu.VMEM((hb + t, half), F32)],
        compiler_params=_cp("parallel"))(p, p, p, att, pool_w, pool_scale)


def _even_mix_bwd(p, att, dy, pool_w, pool_scale, d, name, comm=None):
    s = p.shape[0]
    half = d // 2
    ng = len(POOL_WINDOWS)
    gd = half // ng
    t, hb = ROW_TILE, POOL_HALO
    nt = s // t
    host = _Host(
        comm,
        [pl.BlockSpec((t, half), lambda i: (i, 3)),
         pl.BlockSpec((hb, half), lambda i: (jnp.maximum(i * (t // hb) - 1, 0), 3)),
         pl.BlockSpec((t, d), lambda i: (i, 2)),
         pl.BlockSpec((hb, half), lambda i: (jnp.minimum((i + 1) * (t // hb), s // hb - 1), 5)),
         pl.BlockSpec((t, half), lambda i: (i, 0)),
         pl.BlockSpec((t, d), lambda i: (i, 0)),
         pl.BlockSpec((hb, half), lambda i: (jnp.minimum((i + 1) * (t // hb), s // hb - 1), 1)),
         pl.BlockSpec(pool_w.shape, lambda i: (0, 0, 0)),
         pl.BlockSpec((1, half), lambda i: (0, 0))],
        [pl.BlockSpec((t, half), lambda i: (i, 0)),
         pl.BlockSpec((t, half), lambda i: (i, 0)),
         pl.BlockSpec((t, d), lambda i: (i, 0)),
         pl.BlockSpec((1, half), lambda i: (0, 0)),
         pl.BlockSpec((ng, gd, gd), lambda i: (0, 0, 0))],
        [jax.ShapeDtypeStruct((s, half), F32), jax.ShapeDtypeStruct((s, half), BF16),
         jax.ShapeDtypeStruct((s, d), BF16), jax.ShapeDtypeStruct((1, half), F32),
         jax.ShapeDtypeStruct((ng, gd, gd), F32)],
        [pltpu.VMEM((hb + t, half), F32), pltpu.VMEM((t + hb, half), F32)])

    def body(*refs):
        ((u_ref, uh_ref, g_ref, gh_ref, a_ref, dy_ref, dyh_ref, pw_ref, sc_ref),
         (da_ref, du_ref, dg_ref, dsc_ref, dpw_ref), (pad_ref, dn_ref)) = host.split(refs)
        i = pl.program_id(0)
        host.before(i, nt)
        first = i == 0
        pad_ref[0:hb, :] = jnp.where(i > 0, uh_ref[...], 0.0)
        pad_ref[hb:, :] = u_ref[...]
        pooled = _pool_groups(pad_ref, t, i * t, gd, hb)
        g1 = g_ref[:, :half]
        dy1 = dy_ref[:, :half]
        da_ref[...] = dy1 * _silu(g1)
        dg_ref[:, :half] = (dy1 * a_ref[...] * _dsilu(g1)).astype(BF16)
        row = i * t + lax.broadcasted_iota(jnp.int32, (t + hb, 1), 0)
        for gi, win in enumerate(POOL_WINDOWS):
            cs = slice(gi * gd, (gi + 1) * gd)
            cs2 = slice(half + gi * gd, half + (gi + 1) * gd)
            w = pw_ref[gi]
            pb = pooled[gi].astype(BF16)
            zp = jnp.dot(pb, w, preferred_element_type=F32)
            g2 = g_ref[:, cs2]
            dy2 = dy_ref[:, cs2]
            dg_ref[:, cs2] = (dy2 * zp * sc_ref[:, cs] * _dsilu(g2)).astype(BF16)
            dpo = dy2 * _silu(g2)
            _acc_rows(dsc_ref.at[:, cs], first, jnp.sum(dpo * zp, axis=0, keepdims=True))
            dz = (dpo * sc_ref[:, cs]).astype(BF16)
            _acc_rows(dpw_ref.at[gi], first, _tn(pb, dz))
            dzh = jnp.where(i < nt - 1, dyh_ref[:, cs] * _silu(gh_ref[:, cs]) * sc_ref[:, cs], 0.0).astype(BF16)
            dpool = _nt(dz, w)
            dpool_h = _nt(dzh, w)
            cnt = jnp.minimum(win, row + 1).astype(F32)
            dn_ref[0:t, cs] = dpool / cnt[0:t]
            dn_ref[t:, cs] = dpool_h / cnt[t:]
            acc = dn_ref[0:t, cs]
            for j in range(1, win):
                acc = acc + dn_ref[j:j + t, cs]
            du_ref[:, cs] = (acc - dpool).astype(BF16)
        host.after(i, nt)

    outs = pl.pallas_call(
        body, name=name, grid=(nt,), in_specs=host.in_specs, out_specs=host.out_specs, out_shape=host.out_shape,
        scratch_shapes=host.scratch, input_output_aliases=host.aliases,
        compiler_params=_cp("arbitrary"))(p, p, p, p, att, dy, dy, pool_w, pool_scale, *host.args)
    return host.results(outs)


def _mm_out_even(y, w, x, g_post, g_pre_next, name):
    s, k = y.shape
    d = w.shape[1]
    t = ROW_TILE

    def body(y_ref, w_ref, x_ref, gp_ref, gn_ref, o_ref, x1_ref, h1_ref):
        for r0 in range(0, t, t // 2):
            rows = slice(r0, r0 + t // 2)
            o = jnp.dot(y_ref[rows, :], w_ref[...], preferred_element_type=F32)
            o_ref[rows, :] = o
            ohat, _ = _rms_stats(o)
            x1 = x_ref[rows, :] + ohat * gp_ref[...]
            x1_ref[rows, :] = x1
            xhat, _ = _rms_stats(x1)
            h1_ref[rows, :] = (xhat * gn_ref[...]).astype(BF16)

    row = lambda c: pl.BlockSpec((t, c), lambda i: (i, 0))
    vec = pl.BlockSpec((1, d), lambda i: (0, 0))
    return pl.pallas_call(
        body, name=name, grid=(s // t,),
        in_specs=[row(k), pl.BlockSpec((k, d), lambda i: (0, 0)), row(d), vec, vec],
        out_specs=[row(d), row(d), row(d)],
        out_shape=[jax.ShapeDtypeStruct((s, d), F32), jax.ShapeDtypeStruct((s, d), F32),
                   jax.ShapeDtypeStruct((s, d), BF16)],
        compiler_params=_cp("parallel"))(y, w, x, g_post, g_pre_next)


def _mm_out_odd(y, w, x1, g_post, target, name):
    s, k = y.shape
    d = w.shape[1]
    t = ROW_TILE

    def body(y_ref, w_ref, x_ref, gp_ref, tg_ref, do_ref, dx_ref, loss_ref, dgp_ref):
        first = pl.program_id(0) == 0
        gp = gp_ref[...]
        part = dgp = None
        for r0 in range(0, t, t // 2):
            rows = slice(r0, r0 + t // 2)
            o = jnp.dot(y_ref[rows, :], w_ref[...], preferred_element_type=F32)
            ohat, r = _rms_stats(o)
            diff = x_ref[rows, :] + ohat * gp - tg_ref[rows, :]
            part_half = 0.5 * jnp.sum(jnp.mean(diff * diff, axis=-1, keepdims=True), axis=0, keepdims=True)
            dx2 = diff * (1.0 / d)
            dx_ref[rows, :] = dx2
            do, dgp_half = _rms_bwd(dx2, ohat, r, gp)
            do_ref[rows, :] = do.astype(BF16)
            part = part_half if part is None else part + part_half
            dgp = dgp_half if dgp is None else dgp + dgp_half
        _acc_rows(loss_ref, first, jnp.broadcast_to(part, loss_ref.shape))
        _acc_rows(dgp_ref, first, dgp)

    row = lambda c: pl.BlockSpec((t, c), lambda i: (i, 0))
    vec = pl.BlockSpec((1, d), lambda i: (0, 0))
    return pl.pallas_call(
        body, name=name, grid=(s // t,),
        in_specs=[row(k), pl.BlockSpec((k, d), lambda i: (0, 0)), row(d), vec, row(d)],
        out_specs=[row(d), row(d), pl.BlockSpec((8, LANES), lambda i: (0, 0)), vec],
        out_shape=[jax.ShapeDtypeStruct((s, d), BF16), jax.ShapeDtypeStruct((s, d), F32),
                   jax.ShapeDtypeStruct((8, LANES), F32), jax.ShapeDtypeStruct((1, d), F32)],
        compiler_params=_cp("arbitrary"))(y, w, x1, g_post, target)


def _layer_norm(d1, cg, cb):
    mu = jnp.mean(d1, axis=-1, keepdims=True)
    cen = d1 - mu
    rstd = lax.rsqrt(jnp.mean(cen * cen, axis=-1, keepdims=True) + EPS)
    n = cen * rstd
    return n, rstd, n * cg + cb


SUBLANES = 8
ROW_STRIP = 64
GATHER_PIECES = 8
CONV_ROWS = 64


def _make_shifts(pad_ref, cs, sh_ref):
    rows = sh_ref.shape[1]
    for r in range(1, SUBLANES):
        sh_ref[r - 1] = pad_ref[r:r + rows, cs]


def _by_shift(taps, base, sign=1):
    return sorted(range(taps), key=lambda k: ((sign * (base + k)) % SUBLANES, k))


def _window(pad_ref, cs, sh_ref, off, t):
    m, r = divmod(off, SUBLANES)
    if r == 0:
        return pad_ref[SUBLANES * m:SUBLANES * m + t, cs]
    return sh_ref[r - 1, SUBLANES * m:SUBLANES * m + t, :]


def _odd_mix_fwd(p, sconv_w, dconv_w, dconv_b, cnorm_g, cnorm_b, d, name):
    s = p.shape[0]
    w = d // 2
    k3, k31 = sconv_w.shape[0], dconv_w.shape[0]
    t, hb = ROW_TILE, CONV_HALO
    assert hb >= k31 - 1 and w % LANES == 0

    def body(p_ref, ph_ref, w3_ref, w31_ref, b31_ref, cg_ref, cb_ref, y_ref, s3_ref, d1_ref, mpad, dpad, sh_ref):
        i = pl.program_id(0)
        mpad[0:hb, :] = jnp.where(i > 0, ph_ref[:, 2 * w:3 * w] * ph_ref[:, 0:w], 0.0)
        mpad[hb:, :] = p_ref[:, 2 * w:3 * w] * p_ref[:, 0:w]
        dpad[0:hb, :] = jnp.where(i > 0, ph_ref[:, 3 * w:4 * w] * _sigmoid(ph_ref[:, 4 * w:5 * w]), 0.0)
        dpad[hb:, :] = p_ref[:, 3 * w:4 * w] * _sigmoid(p_ref[:, 4 * w:5 * w])
        for c0 in range(0, w, LANES):
            cs = slice(c0, c0 + LANES)
            acc = jnp.zeros((t, LANES), F32)
            for kk in range(k3):
                acc = acc + w3_ref[kk:kk + 1, cs] * mpad[hb - (k3 - 1) + kk:hb - (k3 - 1) + kk + t, cs]
            s3_ref[:, cs] = acc
            _make_shifts(dpad, cs, sh_ref)
            for r0 in range(0, t, CONV_ROWS):
                acc = jnp.zeros((CONV_ROWS, LANES), F32)
                for kk in _by_shift(k31, hb - (k31 - 1)):
                    acc = acc + w31_ref[kk:kk + 1, cs] * _window(dpad, cs, sh_ref, hb - (k31 - 1) + kk + r0, CONV_ROWS)
                d1_ref[r0:r0 + CONV_ROWS, cs] = acc + b31_ref[:, cs]
        _, _, d2 = _layer_norm(d1_ref[...], cg_ref[...], cb_ref[...])
        y_ref[:, :w] = (p_ref[:, w:2 * w] * s3_ref[...] * _silu(p_ref[:, 5 * w:6 * w])).astype(BF16)
        y_ref[:, w:] = (_silu(d2) * _silu(p_ref[:, 6 * w:7 * w])).astype(BF16)

    row = lambda c: pl.BlockSpec((t, c), lambda i: (i, 0))
    full = lambda a: pl.BlockSpec(a.shape, lambda i: (0, 0))
    return pl.pallas_call(
        body, name=name, grid=(s // t,),
        in_specs=[row(7 * w),
                  pl.BlockSpec((hb, 5 * w), lambda i: (jnp.maximum(i * (t // hb) - 1, 0), 0)),
                  full(sconv_w), full(dconv_w), full(dconv_b), full(cnorm_g), full(cnorm_b)],
        out_specs=[row(d), row(w), row(w)],
        out_shape=[jax.ShapeDtypeStruct((s, d), BF16), jax.ShapeDtypeStruct((s, w), F32),
                   jax.ShapeDtypeStruct((s, w), F32)],
        scratch_shapes=[pltpu.VMEM((hb + t, w), F32)] * 2 + [pltpu.VMEM((SUBLANES - 1, hb + t - SUBLANES, LANES), F32)],
        compiler_params=_cp("parallel"))(p, p, sconv_w, dconv_w, dconv_b, cnorm_g, cnorm_b)


def _odd_bwd_rows(p, s3, d1, dy, cnorm_g, cnorm_b, d, name, comm=None):
    s = p.shape[0]
    w = d // 2
    t = ROW_TILE
    col = lambda j: pl.BlockSpec((t, w), lambda i: (i, j))
    row = lambda c: pl.BlockSpec((t, c), lambda i: (i, 0))
    vec = pl.BlockSpec((1, w), lambda i: (0, 0))
    host = _Host(comm, [col(1), col(5), col(6), row(w), row(w), row(d), vec, vec],
                 [row(w), row(d), row(w), row(w), vec, vec, vec],
                 [jax.ShapeDtypeStruct((s, w), BF16), jax.ShapeDtypeStruct((s, d), BF16),
                  jax.ShapeDtypeStruct((s, w), F32), jax.ShapeDtypeStruct((s, w), F32)] + [jax.ShapeDtypeStruct((1, w), F32)] * 3, [])

    def body(*refs):
        ((bc_ref, g1_ref, g2_ref, s3_ref, d1_ref, dy_ref, cg_ref, cb_ref),
         (dbc_ref, dg_ref, ds3_ref, dd1_ref, dcg_ref, dcb_ref, db_ref), _) = host.split(refs)
        step = pl.program_id(0)
        host.before(step, s // t)
        first = step == 0

        def strip(j, sums):
            rows = slice(j * ROW_STRIP, (j + 1) * ROW_STRIP)
            g1, g2 = g1_ref[rows, :], g2_ref[rows, :]
            bc, s3v = bc_ref[rows, :], s3_ref[rows, :]
            dy1, dy2 = dy_ref[rows, :w], dy_ref[rows, w:]
            n, rstd, d2 = _layer_norm(d1_ref[rows, :], cg_ref[...], cb_ref[...])
            dg_ref[rows, :w] = (dy1 * bc * s3v * _dsilu(g1)).astype(BF16)
            dg_ref[rows, w:] = (dy2 * _silu(d2) * _dsilu(g2)).astype(BF16)
            dco = dy1 * _silu(g1)
            dbc_ref[rows, :] = (dco * s3v).astype(BF16)
            ds3_ref[rows, :] = dco * bc
            dd2 = dy2 * _silu(g2) * _dsilu(d2)
            dn = dd2 * cg_ref[...]
            dd1 = rstd * (dn - jnp.mean(dn, axis=-1, keepdims=True) - n * jnp.mean(dn * n, axis=-1, keepdims=True))
            dd1_ref[rows, :] = dd1
            dcb, dcg, db = sums
            return (dcb + jnp.sum(dd2, axis=0, keepdims=True), dcg + jnp.sum(dd2 * n, axis=0, keepdims=True),
                    db + jnp.sum(dd1, axis=0, keepdims=True))

        zero = jnp.zeros((1, w), F32)
        sums = (zero, zero, zero)
        for j in range(t // ROW_STRIP):
            sums = strip(j, sums)
        dcb, dcg, db = sums
        _acc_rows(dcb_ref, first, dcb)
        _acc_rows(dcg_ref, first, dcg)
        _acc_rows(db_ref, first, db)
        host.after(step, s // t)

    outs = pl.pallas_call(
        body, name=name, grid=(s // t,), in_specs=host.in_specs, out_specs=host.out_specs, out_shape=host.out_shape,
        scratch_shapes=host.scratch, input_output_aliases=host.aliases,
        compiler_params=_cp("arbitrary"))(p, p, p, s3, d1, dy, cnorm_g, cnorm_b, *host.args)
    return host.results(outs)


def _odd_bwd_conv(p, ds3, dd1, dbc, dgate, sconv_w, dconv_w, d, name):
    s = p.shape[0]
    w = d // 2
    k3, k31 = sconv_w.shape[0], dconv_w.shape[0]
    t, hb, ha = ROW_TILE, CONV_HALO, 8
    nt = s // t
    assert hb >= k31 - 1 and ha >= k3 - 1

    def body(hc_ref, cc_ref, ga_ref, gb_ref, hch_ref, cch_ref, gah_ref, gbh_ref, ds3_ref, ds3h_ref, dd1_ref, dd1h_ref,
             dbc_ref, dgate_ref, w3_ref, w31_ref, dp_ref, dw3_ref, dw31_ref, mpad, dpad, s3pad, d1pad, sh_ref):
        i = pl.program_id(0)
        first = i == 0
        last = i == nt - 1
        dhc_ref, dcc_ref, dga_ref, dgb_ref = (dp_ref.at[:, pl.ds(j * w, w)] for j in (0, 2, 3, 4))
        dp_ref[:, w:2 * w] = dbc_ref[...]
        dp_ref[:, 5 * w:] = dgate_ref[...]
        mpad[0:hb, :] = jnp.where(i > 0, cch_ref[...] * hch_ref[...], 0.0)
        mpad[hb:, :] = cc_ref[...] * hc_ref[...]
        dpad[0:hb, :] = jnp.where(i > 0, gah_ref[...] * _sigmoid(gbh_ref[...]), 0.0)
        dpad[hb:, :] = ga_ref[...] * _sigmoid(gb_ref[...])
        s3pad[0:t, :] = ds3_ref[...]
        s3pad[t:, :] = jnp.where(last, 0.0, ds3h_ref[...])
        d1pad[0:t, :] = dd1_ref[...]
        d1pad[t:, :] = jnp.where(last, 0.0, dd1h_ref[...])

        @pl.when(first)
        def _():
            dw3_ref[...] = jnp.zeros_like(dw3_ref)
            dw31_ref[...] = jnp.zeros_like(dw31_ref)

        def fold(v):
            return jnp.sum(v.reshape(v.shape[0] // SUBLANES, SUBLANES, LANES), axis=0)

        groups = range(0, t, CONV_ROWS)
        for c0 in range(0, w, LANES):
            cs = slice(c0, c0 + LANES)
            ds3v = s3pad[0:t, cs]
            dm = jnp.zeros((t, LANES), F32)
            for kk in range(k3):
                dm = dm + w3_ref[kk:kk + 1, cs] * s3pad[k3 - 1 - kk:k3 - 1 - kk + t, cs]
                off = hb - (k3 - 1) + kk
                dw3_ref[SUBLANES * kk:SUBLANES * (kk + 1), cs] += fold(ds3v * mpad[off:off + t, cs])
            dcc_ref[:, cs] = (dm * hc_ref[:, cs]).astype(BF16)
            dhc_ref[:, cs] = (dm * cc_ref[:, cs]).astype(BF16)
            _make_shifts(d1pad, cs, sh_ref)
            for r0 in groups:
                rows = slice(r0, r0 + CONV_ROWS)
                dd0 = jnp.zeros((CONV_ROWS, LANES), F32)
                for kk in _by_shift(k31, -(k31 - 1), -1):
                    dd0 = dd0 + w31_ref[kk:kk + 1, cs] * _window(d1pad, cs, sh_ref, k31 - 1 - kk + r0, CONV_ROWS)
                sgb = _sigmoid(gb_ref[rows, cs])
                dga_ref[rows, cs] = (dd0 * sgb).astype(BF16)
                dgb_ref[rows, cs] = (dd0 * ga_ref[rows, cs] * sgb * (1.0 - sgb)).astype(BF16)
            _make_shifts(dpad, cs, sh_ref)
            for kk in _by_shift(k31, hb - (k31 - 1)):
                part = jnp.zeros((SUBLANES, LANES), F32)
                for r0 in groups:
                    part = part + fold(d1pad[r0:r0 + CONV_ROWS, cs]
                                       * _window(dpad, cs, sh_ref, hb - (k31 - 1) + kk + r0, CONV_ROWS))
                dw31_ref[SUBLANES * kk:SUBLANES * (kk + 1), cs] += part

    col = lambda j: pl.BlockSpec((t, w), lambda i: (i, j))
    pre = lambda j: pl.BlockSpec((hb, w), lambda i: (jnp.maximum(i * (t // hb) - 1, 0), j))
    row = pl.BlockSpec((t, w), lambda i: (i, 0))
    post = lambda h: pl.BlockSpec((h, w), lambda i: (jnp.minimum((i + 1) * (t // h), s // h - 1), 0))
    full = lambda a: pl.BlockSpec(a.shape, lambda i: (0, 0))
    dp, dw3, dw31 = pl.pallas_call(
        body, name=name, grid=(nt,),
        in_specs=[col(0), col(2), col(3), col(4), pre(0), pre(2), pre(3), pre(4),
                  row, post(ha), row, post(hb), row, pl.BlockSpec((t, 2 * w), lambda i: (i, 0)), full(sconv_w), full(dconv_w)],
        out_specs=[pl.BlockSpec((t, 7 * w), lambda i: (i, 0)), pl.BlockSpec((SUBLANES * k3, w), lambda i: (0, 0)),
                   pl.BlockSpec((SUBLANES * k31, w), lambda i: (0, 0))],
        out_shape=[jax.ShapeDtypeStruct((s, 7 * w), BF16),
                   jax.ShapeDtypeStruct((SUBLANES * k3, w), F32), jax.ShapeDtypeStruct((SUBLANES * k31, w), F32)],
        scratch_shapes=[pltpu.VMEM((hb + t, w), F32)] * 2 + [pltpu.VMEM((t + ha, w), F32), pltpu.VMEM((t + hb, w), F32),
                                                             pltpu.VMEM((SUBLANES - 1, hb + t - SUBLANES, LANES), F32)],
        compiler_params=_cp("arbitrary"))(p, p, p, p, p, p, p, p, ds3, ds3, dd1, dd1, dbc, dgate, sconv_w, dconv_w)
    return dp, jnp.sum(dw3.reshape(k3, SUBLANES, w), axis=1), jnp.sum(dw31.reshape(k31, SUBLANES, w), axis=1)


def _mm_in_bwd(dp, w3, x, g_pre, dres, post, name, comm=None):
    s = dp.shape[0]
    nsh, d, ns = w3.shape
    t = 512 if s % 512 == 0 else ROW_TILE
    nt = s // t
    ks = 2 if (ns // 2) % LANES == 0 else 1
    nk, kw = nsh * ks, ns // ks
    chunk = 128
    nchunk = t // chunk
    row = pl.BlockSpec((t, d), lambda i, k: (i, 0))
    vec = pl.BlockSpec((1, d), lambda i, k: (0, 0))
    rowwise = [x, dres] + ([post[0]] if post is not None else [])
    in_specs = [pl.BlockSpec((t, kw), lambda i, k: (i, k)), pl.BlockSpec((None, d, kw), lambda i, k: (k // ks, 0, k % ks)), vec]
    out_specs = [row, vec]
    out_shape = [jax.ShapeDtypeStruct((s, d), F32), jax.ShapeDtypeStruct((1, d), F32)]
    args = [dp, w3, g_pre]
    if post is not None:
        in_specs += [vec]
        out_specs += [row, vec]
        out_shape += [jax.ShapeDtypeStruct((s, d), BF16), jax.ShapeDtypeStruct((1, d), F32)]
        args += [post[1]]
    n_blocked = len(in_specs)
    in_specs += [ANY] * len(rowwise)
    args += rowwise
    host = _Host(comm, in_specs, out_specs, out_shape,
                 [pltpu.VMEM((t, d), F32), pltpu.VMEM((len(rowwise), 2, chunk, d), F32), pltpu.SemaphoreType.DMA((len(rowwise), 2))])

    def body(*refs):
        ins, outs, (acc_ref, buf_ref, sem_ref) = host.split(refs)
        dp_ref, w_ref, g_ref = ins[:3]
        hbm = ins[n_blocked:]
        dx_ref, dg_ref = outs[:2]
        tile = pl.program_id(0)
        kk = pl.program_id(1)
        first = tile == 0
        step = tile * nk + kk
        host.before(step, nt * nk)
        part = _nt(dp_ref[...], w_ref[...])

        @pl.when(kk == 0)
        def _():
            acc_ref[...] = part

        @pl.when(kk > 0)
        def _():
            acc_ref[...] += part

        def fetch(ci, slot):
            return [pltpu.make_async_copy(src.at[pl.ds(tile * t + ci * chunk, chunk)], buf_ref.at[n, slot], sem_ref.at[n, slot])
                    for n, src in enumerate(hbm)]

        @pl.when(kk == nk - 1)
        def _():
            dg = dgp = None
            for cp in fetch(0, 0):
                cp.start()
            for ci in range(nchunk):
                slot = ci % 2
                if ci + 1 < nchunk:
                    for cp in fetch(ci + 1, 1 - slot):
                        cp.start()
                for cp in fetch(ci, slot):
                    cp.wait()
                rows = slice(ci * chunk, (ci + 1) * chunk)
                xhat, r = _rms_stats(buf_ref[0, slot])
                dxn, dg_part = _rms_bwd(acc_ref[rows, :], xhat, r, g_ref[...])
                dx = buf_ref[1, slot] + dxn
                dx_ref[rows, :] = dx
                dg = dg_part if dg is None else dg + dg_part
                if post is not None:
                    ohat, ro = _rms_stats(buf_ref[2, slot])
                    do, dgp_part = _rms_bwd(dx, ohat, ro, ins[3][...])
                    outs[2][rows, :] = do.astype(BF16)
                    dgp = dgp_part if dgp is None else dgp + dgp_part
            _acc_rows(dg_ref, first, dg)
            if post is not None:
                _acc_rows(outs[3], first, dgp)

        host.after(step, nt * nk)

    res = pl.pallas_call(
        body, name=name, grid=(nt, nk), in_specs=host.in_specs, out_specs=host.out_specs, out_shape=host.out_shape,
        scratch_shapes=host.scratch, input_output_aliases=host.aliases,
        compiler_params=_cp("arbitrary", "arbitrary"))(*args, *host.args)
    return host.results(res)


def _half_add(g, r1, c_arr, name, after=None):
    nsh, rows, ns = g.shape
    h = rows // 2
    tr = min(ROW_TILE, h)
    per = h // tr

    def body(c_ref, g_ref, r_ref, *rest):
        rest[-1][...] = (g_ref[...].astype(F32) + r_ref[...].astype(F32)).astype(BF16)

    spec = pl.BlockSpec((None, tr, ns), lambda s, r, c: (s, r, 0))
    ordering = [] if after is None else [after]
    return pl.pallas_call(
        body, name=name,
        grid_spec=pltpu.PrefetchScalarGridSpec(
            num_scalar_prefetch=1, grid=(nsh, per),
            in_specs=[pl.BlockSpec((None, tr, ns), lambda s, r, c: (s, c[0] * per + r, 0)), spec] + [ANY] * len(ordering),
            out_specs=spec),
        out_shape=jax.ShapeDtypeStruct((nsh, h, ns), BF16), compiler_params=_cp("parallel", "parallel"))(c_arr, g, r1, *ordering)


def _sum_chips(hh, r2, mc_arr, name, after=None):
    _, h, ns = hh.shape
    tr = min(ROW_TILE, h)
    per = h // tr

    def body(mc_ref, h_ref, a_ref, b_ref, c_ref, *rest):
        rest[-1][...] = ((h_ref[...].astype(F32) + a_ref[...].astype(F32)) + b_ref[...].astype(F32)) + c_ref[...].astype(F32)

    got = lambda k: pl.BlockSpec((None, tr, ns), lambda r, mc: (k, r, 0))
    ordering = [] if after is None else [after]
    return pl.pallas_call(
        body, name=name,
        grid_spec=pltpu.PrefetchScalarGridSpec(
            num_scalar_prefetch=1, grid=(per,),
            in_specs=[pl.BlockSpec((None, tr, ns), lambda r, mc: (mc[0], r, 0)), got(0), got(1), got(2)] + [ANY] * len(ordering),
            out_specs=pl.BlockSpec((tr, ns), lambda r, mc: (mc[1] * per + r, 0))),
        out_shape=jax.ShapeDtypeStruct((2 * h, ns), F32), compiler_params=_cp("parallel"))(mc_arr, hh, r2, r2, r2, *ordering)


def _add2(a, b, name):
    def body(a_ref, b_ref, o_ref):
        o_ref[...] = a_ref[...] + b_ref[...]

    return pl.pallas_call(body, name=name, out_shape=jax.ShapeDtypeStruct(a.shape, a.dtype), compiler_params=_cp())(a, b)


def _sum_chips_ordered(s2, r2, mc_arr, name):
    rows, w = s2.shape
    rh = rows // 2

    def body(mc_ref, s_ref, a_ref, b_ref, c_ref, o_ref):
        me = mc_ref[0]
        acc = None
        for j in range(N_CHIPS):
            rel = jnp.bitwise_xor(me, j)
            v = jnp.where(rel == 0, s_ref[...], jnp.where(rel == 2, a_ref[...], jnp.where(rel == 1, b_ref[...], c_ref[...])))
            acc = v if acc is None else acc + v
        o_ref[...] = acc

    got = lambda k: pl.BlockSpec((None, rh, w), lambda i, mc: (k, 0, 0))
    return pl.pallas_call(
        body, name=name,
        grid_spec=pltpu.PrefetchScalarGridSpec(
            num_scalar_prefetch=1, grid=(1,),
            in_specs=[pl.BlockSpec((rh, w), lambda i, mc: (mc[1], 0)), got(0), got(1), got(2)],
            out_specs=pl.BlockSpec((rh, w), lambda i, mc: (mc[1], 0))),
        out_shape=jax.ShapeDtypeStruct((rows, w), F32), compiler_params=_cp("arbitrary"))(mc_arr, s2, r2, r2, r2)


def _adamw(w, g, m, v, name, comm=None):
    r, c = w.shape
    tr = ROW_TILE if r % ROW_TILE == 0 else r
    c1 = 1.0 / (1.0 - ADAM_B1 ** ADAM_STEP)
    c2 = 1.0 / (1.0 - ADAM_B2 ** ADAM_STEP)
    spec = pl.BlockSpec((tr, c), lambda i: (i, 0))
    host = _Host(comm, [spec] * 4, [spec] * 4, [jax.ShapeDtypeStruct((r, c), F32)] * 4, [])

    def body(*refs):
        (w_ref, g_ref, m_ref, v_ref), (go_ref, d_ref, nm_ref, nv_ref), _ = host.split(refs)
        step = pl.program_id(0)
        host.before(step, r // tr)
        gv = g_ref[...]
        go_ref[...] = gv
        nm = ADAM_B1 * m_ref[...] + (1.0 - ADAM_B1) * gv
        nv = ADAM_B2 * v_ref[...] + (1.0 - ADAM_B2) * (gv * gv)
        nm_ref[...] = nm
        nv_ref[...] = nv
        d_ref[...] = -ADAM_LR * ((nm * c1) / (jnp.sqrt(nv * c2) + ADAM_EPS) + ADAM_WD * w_ref[...])
        host.after(step, r // tr)

    outs = pl.pallas_call(
        body, name=name, grid=(r // tr,), in_specs=host.in_specs, out_specs=host.out_specs, out_shape=host.out_shape,
        scratch_shapes=host.scratch, input_output_aliases=host.aliases,
        compiler_params=_cp("arbitrary"))(w, g, m, v, *host.args)
    return host.results(outs)


def _swap_with_sibling(grads, wholes, name):
    n, nw = len(grads), len(wholes)
    halves = [g.shape[1] // 2 for g in grads]

    def body(*refs):
        srcs, dsts = refs[:n + nw], refs[n + nw:2 * (n + nw)]
        ssem, rsem = refs[2 * (n + nw):]
        x, y, c, me, chips, sib = _place()
        cps = [_rcopy(srcs[a].at[:, pl.ds((1 - c) * halves[a], halves[a]), :], dsts[a], ssem.at[a], rsem.at[a], sib)
               for a in range(n)]
        cps += [_rcopy(srcs[a], dsts[a], ssem.at[a], rsem.at[a], sib) for a in range(n, n + nw)]
        for cp in cps:
            cp.start()
        for cp in cps:
            cp.wait_recv()
        for cp in cps:
            cp.wait_send()

    out_shape = [jax.ShapeDtypeStruct((g.shape[0], h, g.shape[2]), g.dtype) for g, h in zip(grads, halves)]
    out_shape += [jax.ShapeDtypeStruct(w.shape, w.dtype) for w in wholes]
    return pl.pallas_call(
        body, name=name, in_specs=[ANY] * (n + nw), out_specs=[ANY] * (n + nw), out_shape=out_shape,
        scratch_shapes=[pltpu.SemaphoreType.DMA((n + nw,)), pltpu.SemaphoreType.DMA((n + nw,))],
        compiler_params=pltpu.CompilerParams(has_side_effects=True))(*grads, *wholes)


def _scatter_start(h, name):
    land = (3,) + h.shape[1:]

    def body(h_ref, land_ref, send_sems, recv_sems, h_thru, land_thru, token):
        x, y, c, me, chips, sib = _place()
        for k, chip in enumerate(chips):
            _rcopy(h_ref.at[2 * chip[0] + chip[1]], land_ref.at[k], send_sems.at[k], recv_sems.at[k], (*chip, c)).start()
        token[...] = jnp.zeros_like(token)

    hbm = pl.BlockSpec(memory_space=pltpu.HBM)
    sem = pl.BlockSpec(memory_space=pltpu.SEMAPHORE)
    return pl.pallas_call(
        body, name=name,
        out_shape=(pltpu.SemaphoreType.DMA((3,)), pltpu.SemaphoreType.DMA((3,)), pltpu.HBM(h.shape, h.dtype),
                   pltpu.HBM(land, h.dtype), jax.ShapeDtypeStruct((8, LANES), F32)),
        in_specs=(hbm, hbm), out_specs=(sem, sem, hbm, hbm, pl.BlockSpec(memory_space=pltpu.VMEM)),
        input_output_aliases={0: 2, 1: 3},
        compiler_params=pltpu.CompilerParams(has_side_effects=pltpu.SideEffectType.DATAFLOW_SIDE_EFFECTING))(
            pltpu.with_memory_space_constraint(h, pltpu.HBM),
            pltpu.with_memory_space_constraint(lax.empty(land, h.dtype), pltpu.HBM))


def _scatter_wait(send_sems, recv_sems, h_thru, land_thru, after, name):
    def body(h_ref, land_ref, send_sems, recv_sems, after_ref, h_dead, got_ref):
        x, y, c, me, chips, sib = _place()
        for k, chip in enumerate(chips):
            cp = _rcopy(h_ref.at[2 * chip[0] + chip[1]], land_ref.at[k], send_sems.at[k], recv_sems.at[k], (*chip, c))
            cp.wait_send()
            cp.wait_recv()

    hbm = pl.BlockSpec(memory_space=pltpu.HBM)
    sem = pl.BlockSpec(memory_space=pltpu.SEMAPHORE)
    return pl.pallas_call(
        body, name=name,
        out_shape=(pltpu.HBM(h_thru.shape, h_thru.dtype), pltpu.HBM(land_thru.shape, land_thru.dtype)),
        in_specs=(hbm, hbm, sem, sem, ANY), out_specs=(hbm, hbm), input_output_aliases={0: 0, 1: 1},
        compiler_params=pltpu.CompilerParams(has_side_effects=pltpu.SideEffectType.DATAFLOW_SIDE_EFFECTING))(
            h_thru, land_thru, send_sems, recv_sems, after)


def _swap_start(g, name):
    h = g.shape[1] // 2
    land = (g.shape[0], h, g.shape[2])

    def body(g_ref, land_ref, send_sem, recv_sem, g_thru, land_thru, token):
        x, y, c, me, chips, sib = _place()
        _rcopy(g_ref.at[:, pl.ds((1 - c) * h, h), :], land_ref, send_sem.at[0], recv_sem.at[0], sib).start()
        token[...] = jnp.zeros_like(token)

    hbm = pl.BlockSpec(memory_space=pltpu.HBM)
    sem = pl.BlockSpec(memory_space=pltpu.SEMAPHORE)
    return pl.pallas_call(
        body, name=name,
        out_shape=(pltpu.SemaphoreType.DMA((1,)), pltpu.SemaphoreType.DMA((1,)), pltpu.HBM(g.shape, g.dtype),
                   pltpu.HBM(land, g.dtype), jax.ShapeDtypeStruct((8, LANES), F32)),
        in_specs=(hbm, hbm), out_specs=(sem, sem, hbm, hbm, pl.BlockSpec(memory_space=pltpu.VMEM)),
        input_output_aliases={0: 2, 1: 3},
        compiler_params=pltpu.CompilerParams(has_side_effects=pltpu.SideEffectType.DATAFLOW_SIDE_EFFECTING))(
            pltpu.with_memory_space_constraint(g, pltpu.HBM),
            pltpu.with_memory_space_constraint(lax.empty(land, g.dtype), pltpu.HBM))


def _swap_wait(send_sem, recv_sem, g_thru, land_thru, after, name):
    h = g_thru.shape[1] // 2

    def body(g_ref, land_ref, send_sem, recv_sem, after_ref, g_dead, got_ref):
        x, y, c, me, chips, sib = _place()
        cp = _rcopy(g_ref.at[:, pl.ds((1 - c) * h, h), :], land_ref, send_sem.at[0], recv_sem.at[0], sib)
        cp.wait_send()
        cp.wait_recv()

    hbm = pl.BlockSpec(memory_space=pltpu.HBM)
    sem = pl.BlockSpec(memory_space=pltpu.SEMAPHORE)
    return pl.pallas_call(
        body, name=name,
        out_shape=(pltpu.HBM(g_thru.shape, g_thru.dtype), pltpu.HBM(land_thru.shape, land_thru.dtype)),
        in_specs=(hbm, hbm, sem, sem, ANY), out_specs=(hbm, hbm), input_output_aliases={0: 0, 1: 1},
        compiler_params=pltpu.CompilerParams(has_side_effects=pltpu.SideEffectType.DATAFLOW_SIDE_EFFECTING))(
            g_thru, land_thru, send_sem, recv_sem, after)


def _share_half_start(small, name):
    rh = small.shape[0] // 2
    land = (3, rh, small.shape[1])

    def body(s_ref, land_ref, send_sems, recv_sems, s_thru, land_thru, token):
        x, y, c, me, chips, sib = _place()
        for k, chip in enumerate(chips):
            _rcopy(s_ref.at[pl.ds(c * rh, rh)], land_ref.at[k], send_sems.at[k], recv_sems.at[k], (*chip, c)).start()
        token[...] = jnp.zeros_like(token)

    hbm = pl.BlockSpec(memory_space=pltpu.HBM)
    sem = pl.BlockSpec(memory_space=pltpu.SEMAPHORE)
    return pl.pallas_call(
        body, name=name,
        out_shape=(pltpu.SemaphoreType.DMA((3,)), pltpu.SemaphoreType.DMA((3,)), pltpu.HBM(small.shape, small.dtype),
                   pltpu.HBM(land, small.dtype), jax.ShapeDtypeStruct((8, LANES), F32)),
        in_specs=(hbm, hbm), out_specs=(sem, sem, hbm, hbm, pl.BlockSpec(memory_space=pltpu.VMEM)),
        input_output_aliases={0: 2, 1: 3},
        compiler_params=pltpu.CompilerParams(has_side_effects=pltpu.SideEffectType.DATAFLOW_SIDE_EFFECTING))(
            pltpu.with_memory_space_constraint(small, pltpu.HBM),
            pltpu.with_memory_space_constraint(lax.empty(land, small.dtype), pltpu.HBM))


def _share_half_wait(send_sems, recv_sems, s_thru, land_thru, after, name):
    rh = s_thru.shape[0] // 2

    def body(s_ref, land_ref, send_sems, recv_sems, after_ref, s_dead, got_ref):
        x, y, c, me, chips, sib = _place()
        for k, chip in enumerate(chips):
            cp = _rcopy(s_ref.at[pl.ds(c * rh, rh)], land_ref.at[k], send_sems.at[k], recv_sems.at[k], (*chip, c))
            cp.wait_send()
            cp.wait_recv()

    hbm = pl.BlockSpec(memory_space=pltpu.HBM)
    sem = pl.BlockSpec(memory_space=pltpu.SEMAPHORE)
    return pl.pallas_call(
        body, name=name,
        out_shape=(pltpu.HBM(s_thru.shape, s_thru.dtype), pltpu.HBM(land_thru.shape, land_thru.dtype)),
        in_specs=(hbm, hbm, sem, sem, ANY), out_specs=(hbm, hbm), input_output_aliases={0: 0, 1: 1},
        compiler_params=pltpu.CompilerParams(has_side_effects=pltpu.SideEffectType.DATAFLOW_SIDE_EFFECTING))(
            s_thru, land_thru, send_sems, recv_sems, after)


def _join_start(parts, name):
    n = len(parts)

    def body(*refs):
        srcs, (send_sems, recv_sems), token = refs[:n], refs[n:n + 2], refs[-1]
        x, y, c, me, chips, sib = _place()
        for a, src in enumerate(srcs):
            h = src.shape[0] // 2
            mine = src.at[pl.ds(c * h, h)]
            _rcopy(mine, mine, send_sems.at[a], recv_sems.at[a], sib).start()
        token[...] = jnp.zeros_like(token)

    hbm = pl.BlockSpec(memory_space=pltpu.HBM)
    sem = pl.BlockSpec(memory_space=pltpu.SEMAPHORE)
    outs = pl.pallas_call(
        body, name=name,
        out_shape=(pltpu.SemaphoreType.DMA((n,)), pltpu.SemaphoreType.DMA((n,)))
        + tuple(pltpu.HBM(p.shape, p.dtype) for p in parts) + (jax.ShapeDtypeStruct((8, LANES), F32),),
        in_specs=(hbm,) * n, out_specs=(sem, sem) + (hbm,) * n + (pl.BlockSpec(memory_space=pltpu.VMEM),),
        input_output_aliases={a: 2 + a for a in range(n)},
        compiler_params=pltpu.CompilerParams(has_side_effects=pltpu.SideEffectType.DATAFLOW_SIDE_EFFECTING))(
            *[pltpu.with_memory_space_constraint(p, pltpu.HBM) for p in parts])
    return outs[0], outs[1], list(outs[2:2 + n]), outs[-1]


def _join_wait(send_sems, recv_sems, parts, after, name):
    n = len(parts)

    def body(*refs):
        srcs, (send_sems, recv_sems) = refs[:n], refs[n:n + 2]
        x, y, c, me, chips, sib = _place()
        for a, src in enumerate(srcs):
            h = src.shape[0] // 2
            mine, theirs = src.at[pl.ds(c * h, h)], src.at[pl.ds((1 - c) * h, h)]
            _rcopy(mine, theirs, send_sems.at[a], recv_sems.at[a], sib).wait_send()
            _rcopy(theirs, theirs, send_sems.at[a], recv_sems.at[a], sib).wait_recv()

    hbm = pl.BlockSpec(memory_space=pltpu.HBM)
    sem = pl.BlockSpec(memory_space=pltpu.SEMAPHORE)
    return pl.pallas_call(
        body, name=name, out_shape=tuple(pltpu.HBM(p.shape, p.dtype) for p in parts),
        in_specs=(hbm,) * n + (sem, sem, ANY), out_specs=(hbm,) * n, input_output_aliases={a: a for a in range(n)},
        compiler_params=pltpu.CompilerParams(has_side_effects=pltpu.SideEffectType.DATAFLOW_SIDE_EFFECTING))(
            *parts, send_sems, recv_sems, after)


def _pad_rows(a, rows):
    return jnp.pad(a, ((0, rows - a.shape[0]), (0, 0)))


def _stack_rows(parts, multiple):
    padded = [_pad_rows(p, -(-p.shape[0] // 8) * 8) for p in parts]
    starts, at = [], 0
    for p in padded:
        starts.append(at)
        at += p.shape[0]
    total = -(-at // multiple) * multiple
    if total > at:
        padded.append(jnp.zeros((total - at, parts[0].shape[1]), parts[0].dtype))
    return jnp.concatenate(padded, axis=0), starts


def kernel(x, ln_pre_even, w_in_even, pool_w, pool_scale, w_out_even, ln_post_even, ln_pre_odd, w_in_odd, sconv_w, dconv_w, dconv_b, cnorm_g, cnorm_b, w_out_odd, ln_post_odd, loss_target, m_ln_pre_even, m_w_in_even, m_pool_w, m_pool_scale, m_w_out_even, m_ln_post_even, m_ln_pre_odd, m_w_in_odd, m_sconv_w, m_dconv_w, m_dconv_b, m_cnorm_g, m_cnorm_b, m_w_out_odd, m_ln_post_odd, v_ln_pre_even, v_w_in_even, v_pool_w, v_pool_scale, v_w_out_even, v_ln_post_even, v_ln_pre_odd, v_w_in_odd, v_sconv_w, v_dconv_w, v_dconv_b, v_cnorm_g, v_cnorm_b, v_w_out_odd, v_ln_post_odd):
    _, s, d = x.shape
    half = d // 2
    cw = half // N_CHIPS
    ng, q, gd = pool_w.shape[1:]
    k3, k31 = sconv_w.shape[1], dconv_w.shape[1]
    x2d, tgt = x[0], loss_target[0]
    me = 2 * lax.axis_index("x") + lax.axis_index("y")
    core = lax.axis_index("c")
    c_arr = jnp.reshape(core, (1,)).astype(jnp.int32)
    me_arr = jnp.reshape(me, (1,)).astype(jnp.int32)
    mc_arr = jnp.stack([me, core]).astype(jnp.int32)

    shards = [w_in_even[0], w_out_even[0], w_in_odd[0], w_out_odd[0]]
    pool_w_b = _cast_bf16(pool_w[0].reshape(ng * q, gd), "cast_pool_w").reshape(ng, q, gd)
    pack_w, at_w = _stack_rows([sconv_w[0], dconv_w[0], dconv_b, cnorm_g, cnorm_b], 8)
    pack_d, at_d = _stack_rows([ln_pre_odd, ln_post_odd], 8)
    placed = [lax.dynamic_update_slice(jnp.zeros((ng, N_CHIPS * q, gd), BF16), pool_w_b, (0, me * q, 0)),
              lax.dynamic_update_slice(jnp.zeros((pack_w.shape[0], N_CHIPS * cw), F32), pack_w, (0, me * cw)),
              lax.dynamic_update_slice(jnp.zeros((pack_d.shape[0], d), F32), pack_d, (0, me * (d // N_CHIPS)))]
    plans = _Multi([_GatherPieces([_cast_bf16_own_slab(shards[0], me_arr, "cast_w0")], GATHER_PIECES, (0.3, 0.9)),
                    _SmallGatherPlan(placed, (q, cw, d // N_CHIPS))])
    h0, others, extra = _prep(x2d, ln_pre_even, shards[1:], me_arr, "prep_and_gather_first", plans)
    (win_e,), (pool_w_f, pack_w_f, pack_d_f) = plans.results(extra)
    slabs = [None] + others
    sconv_f = pack_w_f[at_w[0]:at_w[0] + k3]
    dconv_f = pack_w_f[at_w[1]:at_w[1] + k31]
    dconv_b_f, cnorm_g_f, cnorm_b_f = (pack_w_f[at_w[n]:at_w[n] + 1] for n in (2, 3, 4))
    ln_pre_odd_f = pack_d_f[at_d[0]:at_d[0] + 1]
    ln_post_odd_f = pack_d_f[at_d[1]:at_d[1] + 1]

    plans = _Multi([_GatherPlan([slabs[1]], at=(0.6, 0.88)), _GatherPlan([slabs[2]], (0, 1, 4), at=(0.6, 0.88))])
    p_e, extra = _mm_nn(h0, win_e, "proj_in_even", plans)
    (wout_e,), (win_o,) = plans.results(extra)
    wout_e = wout_e.reshape(d, d)
    att, ltot, (win_o,) = _sba_fwd(p_e, half, "sba_fwd", _GatherPlan([win_o], (1, 4, 4), at=(0.69, 0.94)))
    y_e = _even_mix_fwd(p_e, att, pool_w_f, pool_scale, d, "even_mix_fwd")
    o_e, x1, h1 = _mm_out_even(y_e, wout_e, x2d, ln_post_even, ln_pre_odd_f, "proj_out_even")
    p_o, (wout_o,) = _mm_nn(h1, win_o, "proj_in_odd", _GatherPlan([slabs[3]]))
    wout_o = wout_o.reshape(d, d)
    y_o, s3, d1 = _odd_mix_fwd(p_o, sconv_f, dconv_f, dconv_b_f, cnorm_g_f, cnorm_b_f, d, "odd_mix_fwd")
    do_o, dx2, loss_blk, dln_post_odd = _mm_out_odd(y_o, wout_o, x1, ln_post_odd_f, tgt, "proj_out_odd_loss")

    dy_o = _mm_nt(do_o, wout_o, "dy_odd")
    g_wout_o = _mm_tn(y_o, do_o, 1, "dw_out_odd")[0].reshape(N_CHIPS, d // N_CHIPS, d)
    (dbc, dgate_o, ds3, dd1, dcnorm_g, dcnorm_b, ddconv_b), (got,) = _odd_bwd_rows(
        p_o, s3, d1, dy_o, cnorm_g_f, cnorm_b_f, d, "odd_bwd_rows", _SwapPlan([g_wout_o]))
    h_wout_o = _half_add(g_wout_o, got, c_arr, "half_add_out_odd")
    dp_o, dsconv, ddconv = _odd_bwd_conv(p_o, ds3, dd1, dbc, dgate_o, sconv_f, dconv_f, d, "odd_bwd_conv")
    g_win_o, (s_wout_o,) = _mm_tn(h1, dp_o, N_CHIPS, "dw_in_odd", _ScatterPlan([h_wout_o]))
    (dx1, dln_pre_odd, do_e, dln_post_even), (got,) = _mm_in_bwd(
        dp_o, win_o, x1, ln_pre_odd_f, dx2, (o_e, ln_post_even), "dx_odd", _SwapPlan([g_win_o]))
    h_win_o = _half_add(g_win_o, got, c_arr, "half_add_in_odd")

    dy_e = _mm_nt(do_e, wout_e, "dy_even")
    g_wout_e = _mm_tn(y_e, do_e, 1, "dw_out_even")[0].reshape(N_CHIPS, d // N_CHIPS, d)
    (datt, du, dgate_e, dpool_scale, dpool_w), (got,) = _even_mix_bwd(
        p_e, att, dy_e, pool_w_f, pool_scale, d, "even_mix_bwd", _SwapPlan([g_wout_e]))
    h_wout_e = _half_add(g_wout_e, got, c_arr, "half_add_out_even")
    two = lambda v: v.reshape(2, half)
    small_parts = [dpool_scale, two(dln_post_even), two(dln_pre_odd), two(dln_post_odd),
                   dsconv, ddconv, ddconv_b, dcnorm_g, dcnorm_b, dpool_w.reshape(gd, half)]
    small, at_s = _stack_rows(small_parts, 16)
    plans = _Multi([_ScatterPlan([h_win_o]), _SendWholePlan([small])])
    dq, dk, dv, extra = _sba_bwd(p_e, ltot, datt, half, "sba_bwd", plans)
    (s_win_o,), (small1,) = plans.results(extra)
    small2 = _add2(small, small1, "small_add")
    dp_e = jnp.concatenate([dq, dk, dv, du, dgate_e], axis=1)
    plans = _Multi([_ScatterPlan([h_wout_e]), _ShareHalfPlan([small2])])
    g_win_e, extra = _mm_tn(h0, dp_e, N_CHIPS, "dw_in_even", plans)
    (s_wout_e,), (small_got,) = plans.results(extra)
    swap = _swap_start(g_win_e, "swap_in_even_start")
    pairs = [(h_wout_e, s_wout_e), (h_win_o, s_win_o), (h_wout_o, s_wout_o)]
    parts = []
    for n, (h, r) in enumerate(pairs):
        parts.append(_sum_chips(h, r, mc_arr, f"sum_chips{n + 1}", after=parts[-1] if parts else swap[4]))
    g_win_e, got = _swap_wait(*swap[:4], parts[-1], "swap_in_even_wait")
    parts.append(_sum_chips_ordered(small2, small_got, mc_arr, "small_sum"))
    join_sems = _join_start(parts, "join_first_start")
    h_win_e = _half_add(g_win_e, got, c_arr, "half_add_in_even", after=join_sems[3])
    send_sems, recv_sems, h_win_e, landing, token = _scatter_start(h_win_e, "scatter_in_even_start")
    (grad_x, dln_pre_even), _ = _mm_in_bwd(dp_e, win_e, x2d, ln_pre_even + token[0:1, 0:1], dx1, None, "dx_even")

    last, at_l = _stack_rows([two(dln_pre_even), jnp.pad(loss_blk[0:1], ((0, 0), (0, half - LANES)))], 16)
    (last1,) = _swap_with_sibling([], [last], "swap_last")
    last2 = _add2(last, last1, "last_add")
    share = _share_half_start(last2, "share_last_start")
    gw_out_e, gw_in_o, gw_out_o, red = _join_wait(*join_sems[:3], share[4], "join_first_wait")

    def rows(n, cnt):
        return red[at_s[n]:at_s[n] + cnt]

    def mine(a, width):
        return lax.dynamic_slice_in_dim(a, me * width, width, axis=1)

    quarter = d // N_CHIPS
    g_small = {
        "pool_scale": rows(0, 1),
        "ln_post_even": rows(1, 2).reshape(1, d),
        "ln_pre_odd": mine(rows(2, 2).reshape(1, d), quarter),
        "ln_post_odd": mine(rows(3, 2).reshape(1, d), quarter),
        "sconv_w": mine(rows(4, k3), cw),
        "dconv_w": mine(rows(5, k31), cw),
        "dconv_b": mine(rows(6, 1), cw),
        "cnorm_g": mine(rows(7, 1), cw),
        "cnorm_b": mine(rows(8, 1), cw),
        "pool_w": lax.dynamic_slice_in_dim(rows(9, gd).reshape(ng, gd, gd), me * q, q, axis=1).reshape(ng * q, gd),
    }
    w2d = {
        "ln_pre_even": ln_pre_even, "w_in_even": w_in_even[0], "pool_w": pool_w[0].reshape(ng * q, gd),
        "pool_scale": pool_scale, "w_out_even": w_out_even[0], "ln_post_even": ln_post_even, "ln_pre_odd": ln_pre_odd,
        "w_in_odd": w_in_odd[0], "sconv_w": sconv_w[0], "dconv_w": dconv_w[0], "dconv_b": dconv_b, "cnorm_g": cnorm_g,
        "cnorm_b": cnorm_b, "w_out_odd": w_out_odd[0], "ln_post_odd": ln_post_odd,
    }
    moments = {
        "ln_pre_even": (m_ln_pre_even, v_ln_pre_even), "w_in_even": (m_w_in_even, v_w_in_even),
        "pool_w": (m_pool_w, v_pool_w), "pool_scale": (m_pool_scale, v_pool_scale),
        "w_out_even": (m_w_out_even, v_w_out_even), "ln_post_even": (m_ln_post_even, v_ln_post_even),
        "ln_pre_odd": (m_ln_pre_odd, v_ln_pre_odd), "w_in_odd": (m_w_in_odd, v_w_in_odd),
        "sconv_w": (m_sconv_w, v_sconv_w), "dconv_w": (m_dconv_w, v_dconv_w), "dconv_b": (m_dconv_b, v_dconv_b),
        "cnorm_g": (m_cnorm_g, v_cnorm_g), "cnorm_b": (m_cnorm_b, v_cnorm_b),
        "w_out_odd": (m_w_out_odd, v_w_out_odd), "ln_post_odd": (m_ln_post_odd, v_ln_post_odd),
    }
    def update(name, g):
        m_in, v_in = moments[name]
        w = w2d[name]
        return _adamw(w, g, m_in.reshape(w.shape), v_in.reshape(w.shape), "adamw_" + name)[0]

    updates = {name: update(name, g) for name, g in (("w_in_odd", gw_in_o), ("w_out_even", gw_out_e), ("w_out_odd", gw_out_o))}
    last2, last_got = _share_half_wait(*share[:4], updates["w_out_odd"][1], "share_last_wait")
    last_sum = _sum_chips_ordered(last2, last_got, mc_arr, "last_sum")
    h_win_e, s_win_e = _scatter_wait(send_sems, recv_sems, h_win_e, landing, last_sum, "scatter_in_even_wait")
    last_sems = _join_start([_sum_chips(h_win_e, s_win_e, mc_arr, "sum_chips0"), last_sum], "join_last_start")
    for name, g in g_small.items():
        updates[name] = update(name, g)
    gw_in_e, red_last = _join_wait(*last_sems[:3], updates["pool_w"][1], "join_last_wait")
    loss = red_last[at_l[1], 0]
    updates["ln_pre_even"] = update("ln_pre_even", red_last[at_l[0]:at_l[0] + 2].reshape(1, d))
    updates["w_in_even"] = update("w_in_even", gw_in_e)
    outs = [[u.reshape(moments[name][0].shape) for u in updates[name]] for name in w2d]
    grads_out, deltas, new_m, new_v = zip(*outs)
    return (loss, grad_x.reshape(x.shape), *grads_out, *deltas, *new_m, *new_v)
```

```python
import functools
import math

import jax
import jax.numpy as jnp
from jax import lax
from jax.experimental import pallas as pl
from jax.experimental.pallas import tpu as pltpu

F32 = jnp.float32
BF16 = jnp.bfloat16
EPS = 1e-6
N_CHIPS = 4
VMEM_LIMIT_V7X = 56 << 20
HEAD_DIM = 128
ATT_BLOCK = 256
POOL_WINDOWS = (2, 4, 8, 16)
ROW_TILE = 256
POOL_HALO = 16
CONV_HALO = 32
LANES = 128
ADAM_LR, ADAM_B1, ADAM_B2, ADAM_EPS, ADAM_WD, ADAM_STEP = 0.001, 0.9, 0.999, 1e-08, 0.01, 10
MESH_ID = pl.DeviceIdType.MESH
ANY = pl.BlockSpec(memory_space=pl.ANY)


def _cp(*sem):
    return pltpu.CompilerParams(dimension_semantics=sem or None, vmem_limit_bytes=VMEM_LIMIT_V7X)


def _pick_tile(n, cap):
    best = None
    for t in range(LANES, min(n, cap) + 1, LANES):
        if n % t == 0:
            best = t
    assert best is not None, (n, cap)
    return best


def _sigmoid(x):
    return 1.0 / (1.0 + jnp.exp(-x))


def _silu(x):
    return x * _sigmoid(x)


def _dsilu(x):
    s = _sigmoid(x)
    return s * (1.0 + x * (1.0 - s))


def _log_sigmoid(z):
    return jnp.minimum(z, 0.0) - jnp.log(1.0 + jnp.exp(-jnp.abs(z)))


def _rms_stats(x):
    r = lax.rsqrt(jnp.mean(x * x, axis=-1, keepdims=True) + EPS)
    return x * r, r


def _rms_bwd(dh, xhat, r, g):
    dxh = dh * g
    dx = r * (dxh - xhat * jnp.mean(dxh * xhat, axis=-1, keepdims=True))
    return dx, jnp.sum(dh * xhat, axis=0, keepdims=True)


def _acc_rows(ref, first, val):
    @pl.when(first)
    def _():
        ref[...] = val

    @pl.when(jnp.logical_not(first))
    def _():
        ref[...] += val


def _rcopy(src, dst, ssem, rsem, dev):
    return pltpu.make_async_remote_copy(src_ref=src, dst_ref=dst, send_sem=ssem, recv_sem=rsem,
                                        device_id=dev, device_id_type=MESH_ID)


def _place():
    x, y, c = lax.axis_index("x"), lax.axis_index("y"), lax.axis_index("c")
    chips = [(1 - x, y), (x, 1 - y), (1 - x, 1 - y)]
    return x, y, c, 2 * x + y, chips, (x, y, 1 - c)


class _GatherPlan:
    PER_ARRAY = 7

    def __init__(self, arrays, part=(0, 1, 1), at=(0.5, 0.8)):
        self.operands = list(arrays)
        self.out_shapes = [jax.ShapeDtypeStruct(a.shape, a.dtype) for a in arrays]
        self.aliases = {i: i for i in range(len(arrays))}
        self.nsems = self.PER_ARRAY * len(arrays)
        self.base = 0
        self.halves = [a.shape[1] // 2 for a in arrays]
        self.part = part
        self.at = at

    def schedule(self):
        return [(0.0, self.start), (self.at[0], self.relay), (self.at[1], self.relay_far)]

    def _rows(self, ref, a, chip, half, quarter=None):
        lo, hi, n = self.part
        h = self.halves[a]
        first, size = half * h + lo * h // n, (hi - lo) * h // n
        if quarter is not None:
            first, size = first + quarter * (size // 2), size // 2
        return ref.at[chip, pl.ds(first, size)]

    def _copy(self, src, dst, a, n, ssem, rsem, dev):
        return _rcopy(src, dst, ssem.at[self.base + self.PER_ARRAY * a + n], rsem.at[self.base + self.PER_ARRAY * a + n], dev)

    def _own(self, ins, outs, ssem, rsem):
        x, y, c, me, chips, sib = _place()
        return [self._copy(self._rows(ins[a], a, me, c), self._rows(outs[a], a, me, c), a, k, ssem, rsem, (*chips[k], c))
                for a in range(len(ins)) for k in (0, 1)]

    def _relays(self, outs, ssem, rsem, a, k):
        x, y, c, me, chips, sib = _place()
        chip = 2 * chips[k][0] + chips[k][1]
        whole, quarter = self._rows(outs[a], a, chip, c), self._rows(outs[a], a, chip, c, k)
        return (self._copy(whole, whole, a, k, ssem, rsem, (*chips[k], c)),
                self._copy(quarter, quarter, a, 2 + k, ssem, rsem, (*chips[1 - k], c)),
                self._copy(whole, whole, a, 4 + k, ssem, rsem, sib))

    def _far(self, outs, ssem, rsem, a):
        x, y, c, me, chips, sib = _place()
        chip = 2 * chips[2][0] + chips[2][1]
        whole = self._rows(outs[a], a, chip, c)
        got = [self._copy(q, q, a, 2 + k, ssem, rsem, (*chips[1 - k], c))
               for k, q in enumerate([self._rows(outs[a], a, chip, c, 0), self._rows(outs[a], a, chip, c, 1)])]
        return got, self._copy(whole, whole, a, 6, ssem, rsem, sib)

    def start(self, ins, outs, ssem, rsem):
        for cp in self._own(ins, outs, ssem, rsem):
            cp.start()

    def relay(self, ins, outs, ssem, rsem):
        for a in range(len(outs)):
            for k in (0, 1):
                landed, onward, to_sibling = self._relays(outs, ssem, rsem, a, k)
                landed.wait_recv()
                onward.start()
                to_sibling.start()

    def relay_far(self, ins, outs, ssem, rsem):
        for a in range(len(outs)):
            got, to_sibling = self._far(outs, ssem, rsem, a)
            for cp in got:
                cp.wait_recv()
            to_sibling.start()

    def finish(self, ins, outs, ssem, rsem):
        x, y, c, me, chips, sib = _place()
        for a in range(len(outs)):
            for k in range(3):
                ref = self._rows(outs[a], a, 2 * chips[k][0] + chips[k][1], 1 - c)
                self._copy(ref, ref, a, 4 + k, ssem, rsem, sib).wait_recv()
        for cp in self._own(ins, outs, ssem, rsem):
            cp.wait_send()
        for a in range(len(outs)):
            for k in (0, 1):
                _, onward, to_sibling = self._relays(outs, ssem, rsem, a, k)
                onward.wait_send()
                to_sibling.wait_send()
            self._far(outs, ssem, rsem, a)[1].wait_send()


class _ScatterPlan:
    def __init__(self, arrays, part=(0, 1, 1), into=None):
        self.n = len(arrays)
        self.operands = list(arrays) + list(into or [])
        self.out_shapes = [jax.ShapeDtypeStruct((3,) + a.shape[1:], a.dtype) for a in arrays]
        self.aliases = {self.n + i: i for i in range(self.n)} if into else {}
        self.nsems = 3 * self.n
        self.base = 0
        self.part = part

    def _copies(self, ins, outs, ssem, rsem):
        x, y, c, me, chips, sib = _place()
        lo, hi, n = self.part
        out = []
        for a in range(self.n):
            h = ins[a].shape[1]
            rows = pl.ds(lo * h // n, (hi - lo) * h // n)
            for k, chip in enumerate(chips):
                out.append(_rcopy(ins[a].at[2 * chip[0] + chip[1], rows], outs[a].at[k, rows],
                                  ssem.at[self.base + 3 * a + k], rsem.at[self.base + 3 * a + k], (*chip, c)))
        return out

    def schedule(self):
        return [(0.0, self.start)]

    def start(self, ins, outs, ssem, rsem):
        for cp in self._copies(ins, outs, ssem, rsem):
            cp.start()

    def finish(self, ins, outs, ssem, rsem):
        cps = self._copies(ins, outs, ssem, rsem)
        for cp in cps:
            cp.wait_recv()
        for cp in cps:
            cp.wait_send()


class _ShareHalfPlan(_ScatterPlan):
    def __init__(self, arrays):
        super().__init__(arrays)
        self.out_shapes = [jax.ShapeDtypeStruct((3, a.shape[0] // 2, a.shape[1]), a.dtype) for a in arrays]

    def _copies(self, ins, outs, ssem, rsem):
        x, y, c, me, chips, sib = _place()
        out = []
        for a in range(self.n):
            rh = ins[a].shape[0] // 2
            for k, chip in enumerate(chips):
                out.append(_rcopy(ins[a].at[pl.ds(c * rh, rh)], outs[a].at[k],
                                  ssem.at[self.base + 3 * a + k], rsem.at[self.base + 3 * a + k], (*chip, c)))
        return out


class _SwapPlan:
    def __init__(self, grads):
        self.operands = list(grads)
        self.out_shapes = [jax.ShapeDtypeStruct((g.shape[0], g.shape[1] // 2, g.shape[2]), g.dtype) for g in grads]
        self.aliases = {}
        self.nsems = len(grads)
        self.base = 0

    def _copies(self, ins, outs, ssem, rsem):
        x, y, c, me, chips, sib = _place()
        out = []
        for a, src in enumerate(ins):
            h = src.shape[1] // 2
            out.append(_rcopy(src.at[:, pl.ds((1 - c) * h, h), :], outs[a], ssem.at[self.base + a], rsem.at[self.base + a], sib))
        return out

    def schedule(self):
        return [(0.0, self.start)]

    def start(self, ins, outs, ssem, rsem):
        for cp in self._copies(ins, outs, ssem, rsem):
            cp.start()

    def finish(self, ins, outs, ssem, rsem):
        cps = self._copies(ins, outs, ssem, rsem)
        for cp in cps:
            cp.wait_recv()
        for cp in cps:
            cp.wait_send()


class _SendWholePlan(_SwapPlan):
    def __init__(self, arrays):
        self.operands = list(arrays)
        self.out_shapes = [jax.ShapeDtypeStruct(a.shape, a.dtype) for a in arrays]
        self.aliases = {}
        self.nsems = len(arrays)
        self.base = 0

    def _copies(self, ins, outs, ssem, rsem):
        x, y, c, me, chips, sib = _place()
        return [_rcopy(src, outs[a], ssem.at[self.base + a], rsem.at[self.base + a], sib) for a, src in enumerate(ins)]


class _GatherPieces:
    def __init__(self, arrays, n, at):
        self.pieces = [_GatherPlan(arrays, (j, j + 1, n), at) for j in range(n)]
        self.operands, self.out_shapes, self.aliases = self.pieces[0].operands, self.pieces[0].out_shapes, self.pieces[0].aliases
        self.nsems = sum(p.nsems for p in self.pieces)
        self.at = at
        self.base = 0

    @property
    def base(self):
        return self.pieces[0].base

    @base.setter
    def base(self, value):
        for j, p in enumerate(self.pieces):
            p.base = value + j * p.nsems

    def schedule(self):
        return [(0.0, self.start), (self.at[0], self.relay), (self.at[1], self.relay_far)]

    def _each(self, what, *a):
        for p in self.pieces:
            getattr(p, what)(*a)

    def start(self, *a):
        self._each("start", *a)

    def relay(self, *a):
        self._each("relay", *a)

    def relay_far(self, *a):
        self._each("relay_far", *a)

    def finish(self, *a):
        self._each("finish", *a)


class _SmallGatherPlan:
    def __init__(self, arrays, widths):
        self.operands = list(arrays)
        self.out_shapes = [jax.ShapeDtypeStruct(a.shape, a.dtype) for a in arrays]
        self.aliases = {i: i for i in range(3)}
        self.nsems = 9
        self.base = 0
        self.widths = widths

    def _part(self, ref, n, chip):
        w = self.widths[n]
        return ref.at[:, pl.ds(chip * w, w), :] if n == 0 else ref.at[:, pl.ds(chip * w, w)]

    def _copies(self, ins, outs, ssem, rsem, own):
        x, y, c, me, chips, sib = _place()
        out = []
        for n in range(3):
            for k, chip in enumerate(chips):
                which = me if own else 2 * chip[0] + chip[1]
                out.append(_rcopy(self._part(ins[n], n, which), self._part(outs[n], n, which),
                                  ssem.at[self.base + 3 * n + k], rsem.at[self.base + 3 * n + k], (*chip, c)))
        return out

    def schedule(self):
        return [(0.0, self.start)]

    def start(self, ins, outs, ssem, rsem):
        for cp in self._copies(ins, outs, ssem, rsem, True):
            cp.start()

    def finish(self, ins, outs, ssem, rsem):
        for cp in self._copies(ins, outs, ssem, rsem, False):
            cp.wait_recv()
        for cp in self._copies(ins, outs, ssem, rsem, True):
            cp.wait_send()


class _Multi:
    def __init__(self, plans):
        self.plans = plans
        self.operands, self.out_shapes, self.aliases, self.nsems = [], [], {}, 0
        self.spans = []
        for p in plans:
            ni, no = len(self.operands), len(self.out_shapes)
            self.spans.append((ni, ni + len(p.operands), no, no + len(p.out_shapes)))
            self.aliases.update({ni + i: no + j for i, j in p.aliases.items()})
            p.base = self.nsems
            self.nsems += p.nsems
            self.operands += p.operands
            self.out_shapes += p.out_shapes

    def schedule(self):
        def bound(fn, span):
            i0, i1, o0, o1 = span
            return lambda ins, outs, ssem, rsem: fn(ins[i0:i1], outs[o0:o1], ssem, rsem)

        stages = [(at, bound(fn, span)) for p, span in zip(self.plans, self.spans) for at, fn in p.schedule()]
        return sorted(stages, key=lambda s: s[0])

    def finish(self, ins, outs, ssem, rsem):
        for p, (i0, i1, o0, o1) in zip(self.plans, self.spans):
            p.finish(ins[i0:i1], outs[o0:o1], ssem, rsem)

    def results(self, extra):
        return [list(extra[o0:o1]) for (_, _, o0, o1) in self.spans]


class _Host:
    def __init__(self, comm, in_specs, out_specs, out_shape, scratch, prefetch=0):
        self.comm = comm
        self.n_in, self.n_out = len(in_specs), len(out_specs)
        self.in_specs, self.out_specs, self.out_shape, self.scratch = list(in_specs), list(out_specs), list(out_shape), list(scratch)
        self.aliases = {}
        self.args = []
        if comm is not None:
            self.in_specs += [ANY] * len(comm.operands)
            self.out_specs += [ANY] * len(comm.out_shapes)
            self.out_shape += comm.out_shapes
            self.scratch += [pltpu.SemaphoreType.DMA((comm.nsems,)), pltpu.SemaphoreType.DMA((comm.nsems,))]
            self.aliases = {prefetch + self.n_in + i: self.n_out + j for i, j in comm.aliases.items()}
            self.args = list(comm.operands)

    def split(self, refs):
        nc = len(self.args)
        nco = len(self.out_shape) - self.n_out
        ins, p = refs[:self.n_in], self.n_in + nc
        outs, rest = refs[p:p + self.n_out], refs[p + self.n_out + nco:]
        self._cargs = None
        if self.comm is not None:
            self._cargs = (refs[self.n_in:p], refs[p + self.n_out:p + self.n_out + nco], rest[-2], rest[-1])
            rest = rest[:-2]
        return ins, outs, rest

    def before(self, step, total):
        if self.comm is None:
            return

        for at, stage in self.comm.schedule():
            pl.when(step == min(total - 1, int(at * total)))(functools.partial(stage, *self._cargs))

    def after(self, step, total):
        if self.comm is None:
            return

        @pl.when(step == total - 1)
        def _():
            self.comm.finish(*self._cargs)

    def results(self, outs):
        return outs[:self.n_out], outs[self.n_out:]


def _cast_bf16(x, name):
    r, c = x.shape
    tr = ROW_TILE if r % ROW_TILE == 0 else r

    def body(x_ref, o_ref):
        o_ref[...] = x_ref[...].astype(BF16)

    return pl.pallas_call(
        body, name=name, grid=(r // tr,),
        in_specs=[pl.BlockSpec((tr, c), lambda i: (i, 0))],
        out_specs=pl.BlockSpec((tr, c), lambda i: (i, 0)),
        out_shape=jax.ShapeDtypeStruct((r, c), BF16), compiler_params=_cp("parallel"))(x)


def _cast_bf16_own_slab(x, me_arr, name):
    r, c = x.shape
    tr = ROW_TILE if r % ROW_TILE == 0 else r

    def body(me_ref, x_ref, o_ref):
        o_ref[...] = x_ref[...].astype(BF16)

    return pl.pallas_call(
        body, name=name,
        grid_spec=pltpu.PrefetchScalarGridSpec(
            num_scalar_prefetch=1, grid=(r // tr,),
            in_specs=[pl.BlockSpec((tr, c), lambda i, me: (i, 0))],
            out_specs=pl.BlockSpec((None, tr, c), lambda i, me: (me[0], i, 0))),
        out_shape=jax.ShapeDtypeStruct((N_CHIPS, r, c), BF16), compiler_params=_cp("parallel"))(me_arr, x)


def _prep(x, g, shards, me_arr, name, comm):
    s, d = x.shape
    steps = s // ROW_TILE
    tiles = [(w.shape[0] // steps, w.shape[1]) for w in shards]
    assert all(w.shape[0] % steps == 0 for w in shards)
    in_specs = [pl.BlockSpec((ROW_TILE, d), lambda i, me: (i, 0)), pl.BlockSpec((1, d), lambda i, me: (0, 0))]
    in_specs += [pl.BlockSpec(t, lambda i, me: (i, 0)) for t in tiles]
    out_specs = [pl.BlockSpec((ROW_TILE, d), lambda i, me: (i, 0))]
    out_specs += [pl.BlockSpec((None,) + t, lambda i, me: (me[0], i, 0)) for t in tiles]
    out_shape = [jax.ShapeDtypeStruct((s, d), BF16)] + [jax.ShapeDtypeStruct((N_CHIPS,) + w.shape, BF16) for w in shards]
    host = _Host(comm, in_specs, out_specs, out_shape, [], prefetch=1)

    def body(me_ref, *refs):
        (x_ref, g_ref, *w_refs), (h_ref, *slab_refs), _ = host.split(refs)
        step = pl.program_id(0)
        host.before(step, steps)
        xhat, _ = _rms_stats(x_ref[...])
        h_ref[...] = (xhat * g_ref[...]).astype(BF16)
        for w_ref, slab_ref in zip(w_refs, slab_refs):
            slab_ref[...] = w_ref[...].astype(BF16)
        host.after(step, steps)

    outs = pl.pallas_call(
        body, name=name,
        grid_spec=pltpu.PrefetchScalarGridSpec(num_scalar_prefetch=1, grid=(steps,), in_specs=host.in_specs,
                                               out_specs=host.out_specs, scratch_shapes=host.scratch),
        out_shape=host.out_shape, input_output_aliases=host.aliases,
        compiler_params=_cp("arbitrary"))(me_arr, x, g, *shards, *host.args)
    (h, *slabs), extra = host.results(outs)
    return h, slabs, extra


def _assemble(parts, name):
    s = parts[0].shape[0]
    widths = [p.shape[1] for p in parts]
    starts = [sum(widths[:k]) for k in range(len(parts))]

    def body(*refs):
        for src, at, width in zip(refs[:-1], starts, widths):
            refs[-1][:, at:at + width] = src[...]

    return pl.pallas_call(
        body, name=name, grid=(s // ROW_TILE,),
        in_specs=[pl.BlockSpec((ROW_TILE, width), lambda i: (i, 0)) for width in widths],
        out_specs=pl.BlockSpec((ROW_TILE, sum(widths)), lambda i: (i, 0)),
        out_shape=jax.ShapeDtypeStruct((s, sum(widths)), parts[0].dtype), compiler_params=_cp("parallel"))(*parts)


def _mm_nn(a, w3, name, comm=None):
    m, k = a.shape
    nsh, _, ns = w3.shape
    tm = 512 if m % 512 == 0 else ROW_TILE
    tn = _pick_tile(ns, 1024)
    per = ns // tn
    grid = (nsh * per, m // tm)
    host = _Host(comm,
                 [pl.BlockSpec((tm, k), lambda n, i: (i, 0)), pl.BlockSpec((None, k, tn), lambda n, i: (n // per, 0, n % per))],
                 [pl.BlockSpec((tm, tn), lambda n, i: (i, n))], [jax.ShapeDtypeStruct((m, nsh * ns), F32)], [])

    def body(*refs):
        (a_ref, w_ref), (o_ref,), _ = host.split(refs)
        step = pl.program_id(0) * grid[1] + pl.program_id(1)
        host.before(step, grid[0] * grid[1])
        o_ref[...] = jnp.dot(a_ref[...], w_ref[...], preferred_element_type=F32)
        host.after(step, grid[0] * grid[1])

    outs = pl.pallas_call(
        body, name=name, grid=grid, in_specs=host.in_specs, out_specs=host.out_specs, out_shape=host.out_shape,
        scratch_shapes=host.scratch, input_output_aliases=host.aliases,
        compiler_params=_cp("arbitrary", "arbitrary"))(a, w3, *host.args)
    (out,), extra = host.results(outs)
    return out, extra


def _mm_nt(a, b, name):
    m, k = a.shape
    n = b.shape[0]
    tm = 512 if m % 512 == 0 else ROW_TILE

    def body(a_ref, b_ref, o_ref):
        o_ref[...] = lax.dot_general(a_ref[...], b_ref[...], (((1,), (1,)), ((), ())), preferred_element_type=F32)

    return pl.pallas_call(
        body, name=name, grid=(m // tm,),
        in_specs=[pl.BlockSpec((tm, k), lambda i: (i, 0)), pl.BlockSpec((n, k), lambda i: (0, 0))],
        out_specs=pl.BlockSpec((tm, n), lambda i: (i, 0)),
        out_shape=jax.ShapeDtypeStruct((m, n), F32), compiler_params=_cp("parallel"))(a, b)


def _mm_tn(a, b, nsh, name, comm=None):
    s, m = a.shape
    n = b.shape[1]
    ns = n // nsh
    tm = 512 if m % 512 == 0 else ROW_TILE
    tn = _pick_tile(ns, 1024)
    per = ns // tn
    grid = (nsh * per, m // tm)
    host = _Host(comm, [pl.BlockSpec((s, tm), lambda j, i: (0, i)), pl.BlockSpec((s, tn), lambda j, i: (0, j))],
                 [pl.BlockSpec((None, tm, tn), lambda j, i: (j // per, i, j % per))],
                 [jax.ShapeDtypeStruct((nsh, m, ns), BF16)], [])

    def body(*refs):
        (a_ref, b_ref), (o_ref,), _ = host.split(refs)
        step = pl.program_id(0) * grid[1] + pl.program_id(1)
        host.before(step, grid[0] * grid[1])
        o_ref[...] = lax.dot_general(a_ref[...], b_ref[...], (((0,), (0,)), ((), ())),
                                     preferred_element_type=F32).astype(BF16)
        host.after(step, grid[0] * grid[1])

    outs = pl.pallas_call(
        body, name=name, grid=grid, in_specs=host.in_specs, out_specs=host.out_specs, out_shape=host.out_shape,
        scratch_shapes=host.scratch, input_output_aliases=host.aliases,
        compiler_params=_cp("arbitrary", "arbitrary"))(a, b, *host.args)
    (out,), extra = host.results(outs)
    return out, extra


def _tri(n, rel):
    row = lax.broadcasted_iota(jnp.int32, (2 * n, n), 0)
    col = lax.broadcasted_iota(jnp.int32, (2 * n, n), 1)
    return jnp.where(rel(jnp.where(row >= n, row - n, row), col), 1.0, 0.0).astype(BF16)


def _dot_split(x, tri2):
    hi = x.astype(BF16)
    lo = (x - hi.astype(F32)).astype(BF16)
    return jnp.dot(jnp.concatenate([hi, lo], axis=1), tri2, preferred_element_type=F32)


def _nt(a, b):
    return lax.dot_general(a, b, (((1,), (1,)), ((), ())), preferred_element_type=F32)


def _tn(a, b):
    return lax.dot_general(a, b, (((0,), (0,)), ((), ())), preferred_element_type=F32)


def _heads_per_step(nh):
    return max(h for h in (1, 2, 4) if nh % h == 0)


def _sba_fwd(p, sbw, name, comm=None):
    s = p.shape[0]
    nh = sbw // HEAD_DIM
    hp = _heads_per_step(nh)
    ngrp, hw = nh // hp, hp * HEAD_DIM
    blk = ATT_BLOCK
    nq = s // blk
    scale = 1.0 / math.sqrt(HEAD_DIM)
    host = _Host(comm,
                 [pl.BlockSpec((blk, hw), lambda g, i: (i, g)),
                  pl.BlockSpec((s, hw), lambda g, i: (0, ngrp + g)),
                  pl.BlockSpec((s, hw), lambda g, i: (0, 2 * ngrp + g))],
                 [pl.BlockSpec((blk, hw), lambda g, i: (i, g))] * 2,
                 [jax.ShapeDtypeStruct((s, sbw), F32)] * 2,
                 [pltpu.VMEM((s, hw), BF16)] * 2)

    def body(*refs):
        (q_ref, k_ref, v_ref), (o_ref, lt_ref), (kb_ref, vb_ref) = host.split(refs)
        i = pl.program_id(1)
        step = pl.program_id(0) * nq + i
        host.before(step, ngrp * nq)

        @pl.when(i == 0)
        def _():
            kb_ref[...] = k_ref[...].astype(BF16)
            vb_ref[...] = v_ref[...].astype(BF16)

        heads = [slice(h * HEAD_DIM, (h + 1) * HEAD_DIM) for h in range(hp)]
        qs = [q_ref[:, hd].astype(BF16) for hd in heads]
        later = _tri(blk, lambda r, c: r > c)
        causal = lax.broadcasted_iota(jnp.int32, (blk, blk), 1) < lax.broadcasted_iota(jnp.int32, (blk, blk), 0)

        def key_block(j, carry, diagonal):
            rows = pl.ds(pl.multiple_of(j * blk, blk), blk)
            hs = range(hp)
            z = [_nt(qs[h], kb_ref[rows, heads[h]]) * scale for h in hs]
            ls = [_log_sigmoid(z[h]) for h in hs]
            lm = [jnp.where(causal, ls[h] - z[h], 0.0) if diagonal else ls[h] - z[h] for h in hs]
            stay = [_dot_split(lm[h], later) for h in hs]
            w = [jnp.exp(ls[h] + stay[h] + carry[h][1]) for h in hs]
            if diagonal:
                w = [jnp.where(causal, w[h], 0.0) for h in hs]
            acc = [carry[h][0] + jnp.dot(w[h].astype(BF16), vb_ref[rows, heads[h]], preferred_element_type=F32) for h in hs]
            return tuple((acc[h], carry[h][1] + jnp.sum(lm[h], axis=1, keepdims=True)) for h in hs)

        init = tuple((jnp.zeros((blk, HEAD_DIM), F32), jnp.zeros((blk, 1), F32)) for _ in heads)
        carry = key_block(i, init, True)
        carry = lax.fori_loop(0, i, lambda n, c: key_block(i - 1 - n, c, False), carry)
        for h, hd in enumerate(heads):
            o_ref[:, hd] = carry[h][0]
            lt_ref[:, hd] = jnp.broadcast_to(carry[h][1], (blk, HEAD_DIM))
        host.after(step, ngrp * nq)

    outs = pl.pallas_call(
        body, name=name, grid=(ngrp, nq), in_specs=host.in_specs, out_specs=host.out_specs, out_shape=host.out_shape,
        scratch_shapes=host.scratch, input_output_aliases=host.aliases,
        compiler_params=_cp("arbitrary", "arbitrary"))(p, p, p, *host.args)
    (out, ltot), extra = host.results(outs)
    return out, ltot, extra


def _sba_bwd(p, ltot, dout, sbw, name, comm=None):
    s = p.shape[0]
    nh = sbw // HEAD_DIM
    hp = _heads_per_step(nh)
    ngrp, hw = nh // hp, hp * HEAD_DIM
    blk = ATT_BLOCK
    nq = s // blk
    scale = 1.0 / math.sqrt(HEAD_DIM)
    blk_spec = pl.BlockSpec((blk, hw), lambda g, i: (i, g))
    col_spec = pl.BlockSpec((s, hw), lambda g, i: (0, g))
    host = _Host(comm,
                 [blk_spec, pl.BlockSpec((s, hw), lambda g, i: (0, ngrp + g)),
                  pl.BlockSpec((s, hw), lambda g, i: (0, 2 * ngrp + g)), blk_spec, blk_spec],
                 [blk_spec, col_spec, col_spec], [jax.ShapeDtypeStruct((s, sbw), BF16)] * 3,
                 [pltpu.VMEM((s, hw), BF16)] * 2 + [pltpu.VMEM((s, hw), F32)] * 2)

    def body(*refs):
        (q_ref, k_ref, v_ref, lt_ref, do_ref), (dq_ref, dk_ref, dv_ref), (kb_ref, vb_ref, dka_ref, dva_ref) = host.split(refs)
        i = pl.program_id(1)
        step = pl.program_id(0) * nq + i
        host.before(step, ngrp * nq)

        @pl.when(i == 0)
        def _():
            kb_ref[...] = k_ref[...].astype(BF16)
            vb_ref[...] = v_ref[...].astype(BF16)
            dka_ref[...] = jnp.zeros_like(dka_ref)
            dva_ref[...] = jnp.zeros_like(dva_ref)

        heads = [slice(h * HEAD_DIM, (h + 1) * HEAD_DIM) for h in range(hp)]
        qs = [q_ref[:, hd].astype(BF16) for hd in heads]
        dos = [do_ref[:, hd].astype(BF16) for hd in heads]
        ltots = [lt_ref[:, h * HEAD_DIM:h * HEAD_DIM + 1] for h in range(hp)]
        upto = _tri(blk, lambda r, c: r <= c)
        before = _tri(blk, lambda r, c: r < c)
        causal = lax.broadcasted_iota(jnp.int32, (blk, blk), 1) < lax.broadcasted_iota(jnp.int32, (blk, blk), 0)

        def key_block(j, carry, diagonal):
            rows = pl.ds(pl.multiple_of(j * blk, blk), blk)
            hs = range(hp)
            kj = [kb_ref[rows, heads[h]] for h in hs]
            vj = [vb_ref[rows, heads[h]] for h in hs]
            z = [_nt(qs[h], kj[h]) * scale for h in hs]
            dw = [_nt(dos[h], vj[h]) for h in hs]
            ls = [_log_sigmoid(z[h]) for h in hs]
            lm = [jnp.where(causal, ls[h] - z[h], 0.0) if diagonal else ls[h] - z[h] for h in hs]
            stay = [ltots[h] - carry[h][1] - _dot_split(lm[h], upto) for h in hs]
            w = [jnp.exp(ls[h] + stay[h]) for h in hs]
            if diagonal:
                w = [jnp.where(causal, w[h], 0.0) for h in hs]
            da = [dw[h] * w[h] for h in hs]
            sig = [jnp.exp(ls[h]) for h in hs]
            chain = [sig[h] * (carry[h][2] + _dot_split(da[h], before)) for h in hs]
            if diagonal:
                chain = [jnp.where(causal, chain[h], 0.0) for h in hs]
            dzb = [((da[h] * (1.0 - sig[h]) - chain[h]) * scale).astype(BF16) for h in hs]
            dq = [carry[h][0] + jnp.dot(dzb[h], kj[h], preferred_element_type=F32) for h in hs]
            for h in hs:
                dka_ref[rows, heads[h]] += _tn(dzb[h], qs[h])
            for h in hs:
                dva_ref[rows, heads[h]] += _tn(w[h].astype(BF16), dos[h])
            return tuple((dq[h], carry[h][1] + jnp.sum(lm[h], axis=1, keepdims=True),
                          carry[h][2] + jnp.sum(da[h], axis=1, keepdims=True)) for h in hs)

        zero = jnp.zeros((blk, 1), F32)
        init = tuple((jnp.zeros((blk, HEAD_DIM), F32), zero, zero) for _ in heads)
        carry = lax.fori_loop(0, i, lambda j, c: key_block(j, c, False), init)
        carry = key_block(i, carry, True)
        for h, hd in enumerate(heads):
            dq_ref[:, hd] = carry[h][0].astype(BF16)

        @pl.when(i == nq - 1)
        def _():
            dk_ref[...] = dka_ref[...].astype(BF16)
            dv_ref[...] = dva_ref[...].astype(BF16)

        host.after(step, ngrp * nq)

    outs = pl.pallas_call(
        body, name=name, grid=(ngrp, nq), in_specs=host.in_specs, out_specs=host.out_specs, out_shape=host.out_shape,
        scratch_shapes=host.scratch, input_output_aliases=host.aliases,
        compiler_params=_cp("arbitrary", "arbitrary"))(p, p, p, ltot, dout, *host.args)
    (dq, dk, dv), extra = host.results(outs)
    return dq, dk, dv, extra


def _pool_groups(pad_ref, tile, row0, gd, halo):
    row = row0 + lax.broadcasted_iota(jnp.int32, (tile, 1), 0)
    out = []
    for gi, win in enumerate(POOL_WINDOWS):
        cs = slice(gi * gd, (gi + 1) * gd)
        tok = pad_ref[halo:halo + tile, cs]
        acc = tok
        for j in range(1, win):
            acc = acc + pad_ref[halo - j:halo - j + tile, cs]
        cnt = jnp.minimum(win, row + 1).astype(F32)
        out.append(acc / cnt - tok)
    return out


def _even_mix_fwd(p, att, pool_w, pool_scale, d, name):
    s = p.shape[0]
    half = d // 2
    gd = half // len(POOL_WINDOWS)
    t, hb = ROW_TILE, POOL_HALO

    def body(u_ref, uh_ref, g_ref, a_ref, pw_ref, sc_ref, y_ref, pad_ref):
        i = pl.program_id(0)
        pad_ref[0:hb, :] = jnp.where(i > 0, uh_ref[...], 0.0)
        pad_ref[hb:, :] = u_ref[...]
        pooled = _pool_groups(pad_ref, t, i * t, gd, hb)
        for gi in range(len(POOL_WINDOWS)):
            cs = slice(gi * gd, (gi + 1) * gd)
            po = jnp.dot(pooled[gi].astype(BF16), pw_ref[gi], preferred_element_type=F32) * sc_ref[:, cs]
            y_ref[:, half + gi * gd:half + (gi + 1) * gd] = (po * _silu(g_ref[:, half + gi * gd:half + (gi + 1) * gd])).astype(BF16)
        y_ref[:, :half] = (a_ref[...] * _silu(g_ref[:, :half])).astype(BF16)

    return pl.pallas_call(
        body, name=name, grid=(s // t,),
        in_specs=[pl.BlockSpec((t, half), lambda i: (i, 3)),
                  pl.BlockSpec((hb, half), lambda i: (jnp.maximum(i * (t // hb) - 1, 0), 3)),
                  pl.BlockSpec((t, d), lambda i: (i, 2)),
                  pl.BlockSpec((t, half), lambda i: (i, 0)),
                  pl.BlockSpec(pool_w.shape, lambda i: (0, 0, 0)),
                  pl.BlockSpec((1, half), lambda i: (0, 0))],
        out_specs=pl.BlockSpec((t, d), lambda i: (i, 0)),
        out_shape=jax.ShapeDtypeStruct((s, d), BF16),
        scratch_shapes=[pltpu.VMEM((hb + t, half), F32)],
        compiler_params=_cp("parallel"))(p, p, p, att, pool_w, pool_scale)


def _even_mix_bwd(p, att, dy, pool_w, pool_scale, d, name, comm=None):
    s = p.shape[0]
    half = d // 2
    ng = len(POOL_WINDOWS)
    gd = half // ng
    t, hb = ROW_TILE, POOL_HALO
    nt = s // t
    host = _Host(
        comm,
        [pl.BlockSpec((t, half), lambda i: (i, 3)),
         pl.BlockSpec((hb, half), lambda i: (jnp.maximum(i * (t // hb) - 1, 0), 3)),
         pl.BlockSpec((t, d), lambda i: (i, 2)),
         pl.BlockSpec((hb, half), lambda i: (jnp.minimum((i + 1) * (t // hb), s // hb - 1), 5)),
         pl.BlockSpec((t, half), lambda i: (i, 0)),
         pl.BlockSpec((t, d), lambda i: (i, 0)),
         pl.BlockSpec((hb, half), lambda i: (jnp.minimum((i + 1) * (t // hb), s // hb - 1), 1)),
         pl.BlockSpec(pool_w.shape, lambda i: (0, 0, 0)),
         pl.BlockSpec((1, half), lambda i: (0, 0))],
        [pl.BlockSpec((t, half), lambda i: (i, 0)),
         pl.BlockSpec((t, half), lambda i: (i, 0)),
         pl.BlockSpec((t, d), lambda i: (i, 0)),
         pl.BlockSpec((1, half), lambda i: (0, 0)),
         pl.BlockSpec((ng, gd, gd), lambda i: (0, 0, 0))],
        [jax.ShapeDtypeStruct((s, half), F32), jax.ShapeDtypeStruct((s, half), BF16),
         jax.ShapeDtypeStruct((s, d), BF16), jax.ShapeDtypeStruct((1, half), F32),
         jax.ShapeDtypeStruct((ng, gd, gd), F32)],
        [pltpu.VMEM((hb + t, half), F32), pltpu.VMEM((t + hb, half), F32)])

    def body(*refs):
        ((u_ref, uh_ref, g_ref, gh_ref, a_ref, dy_ref, dyh_ref, pw_ref, sc_ref),
         (da_ref, du_ref, dg_ref, dsc_ref, dpw_ref), (pad_ref, dn_ref)) = host.split(refs)
        i = pl.program_id(0)
        host.before(i, nt)
        first = i == 0
        pad_ref[0:hb, :] = jnp.where(i > 0, uh_ref[...], 0.0)
        pad_ref[hb:, :] = u_ref[...]
        pooled = _pool_groups(pad_ref, t, i * t, gd, hb)
        g1 = g_ref[:, :half]
        dy1 = dy_ref[:, :half]
        da_ref[...] = dy1 * _silu(g1)
        dg_ref[:, :half] = (dy1 * a_ref[...] * _dsilu(g1)).astype(BF16)
        row = i * t + lax.broadcasted_iota(jnp.int32, (t + hb, 1), 0)
        for gi, win in enumerate(POOL_WINDOWS):
            cs = slice(gi * gd, (gi + 1) * gd)
            cs2 = slice(half + gi * gd, half + (gi + 1) * gd)
            w = pw_ref[gi]
            pb = pooled[gi].astype(BF16)
            zp = jnp.dot(pb, w, preferred_element_type=F32)
            g2 = g_ref[:, cs2]
            dy2 = dy_ref[:, cs2]
            dg_ref[:, cs2] = (dy2 * zp * sc_ref[:, cs] * _dsilu(g2)).astype(BF16)
            dpo = dy2 * _silu(g2)
            _acc_rows(dsc_ref.at[:, cs], first, jnp.sum(dpo * zp, axis=0, keepdims=True))
            dz = (dpo * sc_ref[:, cs]).astype(BF16)
            _acc_rows(dpw_ref.at[gi], first, _tn(pb, dz))
            dzh = jnp.where(i < nt - 1, dyh_ref[:, cs] * _silu(gh_ref[:, cs]) * sc_ref[:, cs], 0.0).astype(BF16)
            dpool = _nt(dz, w)
            dpool_h = _nt(dzh, w)
            cnt = jnp.minimum(win, row + 1).astype(F32)
            dn_ref[0:t, cs] = dpool / cnt[0:t]
            dn_ref[t:, cs] = dpool_h / cnt[t:]
            acc = dn_ref[0:t, cs]
            for j in range(1, win):
                acc = acc + dn_ref[j:j + t, cs]
            du_ref[:, cs] = (acc - dpool).astype(BF16)
        host.after(i, nt)

    outs = pl.pallas_call(
        body, name=name, grid=(nt,), in_specs=host.in_specs, out_specs=host.out_specs, out_shape=host.out_shape,
        scratch_shapes=host.scratch, input_output_aliases=host.aliases,
        compiler_params=_cp("arbitrary"))(p, p, p, p, att, dy, dy, pool_w, pool_scale, *host.args)
    return host.results(outs)


def _mm_out_even(y, w, x, g_post, g_pre_next, name):
    s, k = y.shape
    d = w.shape[1]
    t = ROW_TILE

    def body(y_ref, w_ref, x_ref, gp_ref, gn_ref, o_ref, x1_ref, h1_ref):
        for r0 in range(0, t, t // 2):
            rows = slice(r0, r0 + t // 2)
            o = jnp.dot(y_ref[rows, :], w_ref[...], preferred_element_type=F32)
            o_ref[rows, :] = o
            ohat, _ = _rms_stats(o)
            x1 = x_ref[rows, :] + ohat * gp_ref[...]
            x1_ref[rows, :] = x1
            xhat, _ = _rms_stats(x1)
            h1_ref[rows, :] = (xhat * gn_ref[...]).astype(BF16)

    row = lambda c: pl.BlockSpec((t, c), lambda i: (i, 0))
    vec = pl.BlockSpec((1, d), lambda i: (0, 0))
    return pl.pallas_call(
        body, name=name, grid=(s // t,),
        in_specs=[row(k), pl.BlockSpec((k, d), lambda i: (0, 0)), row(d), vec, vec],
        out_specs=[row(d), row(d), row(d)],
        out_shape=[jax.ShapeDtypeStruct((s, d), F32), jax.ShapeDtypeStruct((s, d), F32),
                   jax.ShapeDtypeStruct((s, d), BF16)],
        compiler_params=_cp("parallel"))(y, w, x, g_post, g_pre_next)


def _mm_out_odd(y, w, x1, g_post, target, name):
    s, k = y.shape
    d = w.shape[1]
    t = ROW_TILE

    def body(y_ref, w_ref, x_ref, gp_ref, tg_ref, do_ref, dx_ref, loss_ref, dgp_ref):
        first = pl.program_id(0) == 0
        gp = gp_ref[...]
        part = dgp = None
        for r0 in range(0, t, t // 2):
            rows = slice(r0, r0 + t // 2)
            o = jnp.dot(y_ref[rows, :], w_ref[...], preferred_element_type=F32)
            ohat, r = _rms_stats(o)
            diff = x_ref[rows, :] + ohat * gp - tg_ref[rows, :]
            part_half = 0.5 * jnp.sum(jnp.mean(diff * diff, axis=-1, keepdims=True), axis=0, keepdims=True)
            dx2 = diff * (1.0 / d)
            dx_ref[rows, :] = dx2
            do, dgp_half = _rms_bwd(dx2, ohat, r, gp)
            do_ref[rows, :] = do.astype(BF16)
            part = part_half if part is None else part + part_half
            dgp = dgp_half if dgp is None else dgp + dgp_half
        _acc_rows(loss_ref, first, jnp.broadcast_to(part, loss_ref.shape))
        _acc_rows(dgp_ref, first, dgp)

    row = lambda c: pl.BlockSpec((t, c), lambda i: (i, 0))
    vec = pl.BlockSpec((1, d), lambda i: (0, 0))
    return pl.pallas_call(
        body, name=name, grid=(s // t,),
        in_specs=[row(k), pl.BlockSpec((k, d), lambda i: (0, 0)), row(d), vec, row(d)],
        out_specs=[row(d), row(d), pl.BlockSpec((8, LANES), lambda i: (0, 0)), vec],
        out_shape=[jax.ShapeDtypeStruct((s, d), BF16), jax.ShapeDtypeStruct((s, d), F32),
                   jax.ShapeDtypeStruct((8, LANES), F32), jax.ShapeDtypeStruct((1, d), F32)],
        compiler_params=_cp("arbitrary"))(y, w, x1, g_post, target)


def _layer_norm(d1, cg, cb):
    mu = jnp.mean(d1, axis=-1, keepdims=True)
    cen = d1 - mu
    rstd = lax.rsqrt(jnp.mean(cen * cen, axis=-1, keepdims=True) + EPS)
    n = cen * rstd
    return n, rstd, n * cg + cb


SUBLANES = 8
ROW_STRIP = 64
GATHER_PIECES = 8
CONV_ROWS = 64


def _make_shifts(pad_ref, cs, sh_ref):
    rows = sh_ref.shape[1]
    for r in range(1, SUBLANES):
        sh_ref[r - 1] = pad_ref[r:r + rows, cs]


def _by_shift(taps, base, sign=1):
    return sorted(range(taps), key=lambda k: ((sign * (base + k)) % SUBLANES, k))


def _window(pad_ref, cs, sh_ref, off, t):
    m, r = divmod(off, SUBLANES)
    if r == 0:
        return pad_ref[SUBLANES * m:SUBLANES * m + t, cs]
    return sh_ref[r - 1, SUBLANES * m:SUBLANES * m + t, :]


def _odd_mix_fwd(p, sconv_w, dconv_w, dconv_b, cnorm_g, cnorm_b, d, name):
    s = p.shape[0]
    w = d // 2
    k3, k31 = sconv_w.shape[0], dconv_w.shape[0]
    t, hb = ROW_TILE, CONV_HALO
    assert hb >= k31 - 1 and w % LANES == 0

    def body(p_ref, ph_ref, w3_ref, w31_ref, b31_ref, cg_ref, cb_ref, y_ref, s3_ref, d1_ref, mpad, dpad, sh_ref):
        i = pl.program_id(0)
        mpad[0:hb, :] = jnp.where(i > 0, ph_ref[:, 2 * w:3 * w] * ph_ref[:, 0:w], 0.0)
        mpad[hb:, :] = p_ref[:, 2 * w:3 * w] * p_ref[:, 0:w]
        dpad[0:hb, :] = jnp.where(i > 0, ph_ref[:, 3 * w:4 * w] * _sigmoid(ph_ref[:, 4 * w:5 * w]), 0.0)
        dpad[hb:, :] = p_ref[:, 3 * w:4 * w] * _sigmoid(p_ref[:, 4 * w:5 * w])
        for c0 in range(0, w, LANES):
            cs = slice(c0, c0 + LANES)
            acc = jnp.zeros((t, LANES), F32)
            for kk in range(k3):
                acc = acc + w3_ref[kk:kk + 1, cs] * mpad[hb - (k3 - 1) + kk:hb - (k3 - 1) + kk + t, cs]
            s3_ref[:, cs] = acc
            _make_shifts(dpad, cs, sh_ref)
            for r0 in range(0, t, CONV_ROWS):
                acc = jnp.zeros((CONV_ROWS, LANES), F32)
                for kk in _by_shift(k31, hb - (k31 - 1)):
                    acc = acc + w31_ref[kk:kk + 1, cs] * _window(dpad, cs, sh_ref, hb - (k31 - 1) + kk + r0, CONV_ROWS)
                d1_ref[r0:r0 + CONV_ROWS, cs] = acc + b31_ref[:, cs]
        _, _, d2 = _layer_norm(d1_ref[...], cg_ref[...], cb_ref[...])
        y_ref[:, :w] = (p_ref[:, w:2 * w] * s3_ref[...] * _silu(p_ref[:, 5 * w:6 * w])).astype(BF16)
        y_ref[:, w:] = (_silu(d2) * _silu(p_ref[:, 6 * w:7 * w])).astype(BF16)

    row = lambda c: pl.BlockSpec((t, c), lambda i: (i, 0))
    full = lambda a: pl.BlockSpec(a.shape, lambda i: (0, 0))
    return pl.pallas_call(
        body, name=name, grid=(s // t,),
        in_specs=[row(7 * w),
                  pl.BlockSpec((hb, 5 * w), lambda i: (jnp.maximum(i * (t // hb) - 1, 0), 0)),
                  full(sconv_w), full(dconv_w), full(dconv_b), full(cnorm_g), full(cnorm_b)],
        out_specs=[row(d), row(w), row(w)],
        out_shape=[jax.ShapeDtypeStruct((s, d), BF16), jax.ShapeDtypeStruct((s, w), F32),
                   jax.ShapeDtypeStruct((s, w), F32)],
        scratch_shapes=[pltpu.VMEM((hb + t, w), F32)] * 2 + [pltpu.VMEM((SUBLANES - 1, hb + t - SUBLANES, LANES), F32)],
        compiler_params=_cp("parallel"))(p, p, sconv_w, dconv_w, dconv_b, cnorm_g, cnorm_b)


def _odd_bwd_rows(p, s3, d1, dy, cnorm_g, cnorm_b, d, name, comm=None):
    s = p.shape[0]
    w = d // 2
    t = ROW_TILE
    col = lambda j: pl.BlockSpec((t, w), lambda i: (i, j))
    row = lambda c: pl.BlockSpec((t, c), lambda i: (i, 0))
    vec = pl.BlockSpec((1, w), lambda i: (0, 0))
    host = _Host(comm, [col(1), col(5), col(6), row(w), row(w), row(d), vec, vec],
                 [row(w), row(d), row(w), row(w), vec, vec, vec],
                 [jax.ShapeDtypeStruct((s, w), BF16), jax.ShapeDtypeStruct((s, d), BF16),
                  jax.ShapeDtypeStruct((s, w), F32), jax.ShapeDtypeStruct((s, w), F32)] + [jax.ShapeDtypeStruct((1, w), F32)] * 3, [])

    def body(*refs):
        ((bc_ref, g1_ref, g2_ref, s3_ref, d1_ref, dy_ref, cg_ref, cb_ref),
         (dbc_ref, dg_ref, ds3_ref, dd1_ref, dcg_ref, dcb_ref, db_ref), _) = host.split(refs)
        step = pl.program_id(0)
        host.before(step, s // t)
        first = step == 0

        def strip(j, sums):
            rows = slice(j * ROW_STRIP, (j + 1) * ROW_STRIP)
            g1, g2 = g1_ref[rows, :], g2_ref[rows, :]
            bc, s3v = bc_ref[rows, :], s3_ref[rows, :]
            dy1, dy2 = dy_ref[rows, :w], dy_ref[rows, w:]
            n, rstd, d2 = _layer_norm(d1_ref[rows, :], cg_ref[...], cb_ref[...])
            dg_ref[rows, :w] = (dy1 * bc * s3v * _dsilu(g1)).astype(BF16)
            dg_ref[rows, w:] = (dy2 * _silu(d2) * _dsilu(g2)).astype(BF16)
            dco = dy1 * _silu(g1)
            dbc_ref[rows, :] = (dco * s3v).astype(BF16)
            ds3_ref[rows, :] = dco * bc
            dd2 = dy2 * _silu(g2) * _dsilu(d2)
            dn = dd2 * cg_ref[...]
            dd1 = rstd * (dn - jnp.mean(dn, axis=-1, keepdims=True) - n * jnp.mean(dn * n, axis=-1, keepdims=True))
            dd1_ref[rows, :] = dd1
            dcb, dcg, db = sums
            return (dcb + jnp.sum(dd2, axis=0, keepdims=True), dcg + jnp.sum(dd2 * n, axis=0, keepdims=True),
                    db + jnp.sum(dd1, axis=0, keepdims=True))

        zero = jnp.zeros((1, w), F32)
        sums = (zero, zero, zero)
        for j in range(t // ROW_STRIP):
            sums = strip(j, sums)
        dcb, dcg, db = sums
        _acc_rows(dcb_ref, first, dcb)
        _acc_rows(dcg_ref, first, dcg)
        _acc_rows(db_ref, first, db)
        host.after(step, s // t)

    outs = pl.pallas_call(
        body, name=name, grid=(s // t,), in_specs=host.in_specs, out_specs=host.out_specs, out_shape=host.out_shape,
        scratch_shapes=host.scratch, input_output_aliases=host.aliases,
        compiler_params=_cp("arbitrary"))(p, p, p, s3, d1, dy, cnorm_g, cnorm_b, *host.args)
    return host.results(outs)


def _odd_bwd_conv(p, ds3, dd1, dbc, dgate, sconv_w, dconv_w, d, name):
    s = p.shape[0]
    w = d // 2
    k3, k31 = sconv_w.shape[0], dconv_w.shape[0]
    t, hb, ha = ROW_TILE, CONV_HALO, 8
    nt = s // t
    assert hb >= k31 - 1 and ha >= k3 - 1

    def body(hc_ref, cc_ref, ga_ref, gb_ref, hch_ref, cch_ref, gah_ref, gbh_ref, ds3_ref, ds3h_ref, dd1_ref, dd1h_ref,
             dbc_ref, dgate_ref, w3_ref, w31_ref, dp_ref, dw3_ref, dw31_ref, mpad, dpad, s3pad, d1pad, sh_ref):
        i = pl.program_id(0)
        first = i == 0
        last = i == nt - 1
        dhc_ref, dcc_ref, dga_ref, dgb_ref = (dp_ref.at[:, pl.ds(j * w, w)] for j in (0, 2, 3, 4))
        dp_ref[:, w:2 * w] = dbc_ref[...]
        dp_ref[:, 5 * w:] = dgate_ref[...]
        mpad[0:hb, :] = jnp.where(i > 0, cch_ref[...] * hch_ref[...], 0.0)
        mpad[hb:, :] = cc_ref[...] * hc_ref[...]
        dpad[0:hb, :] = jnp.where(i > 0, gah_ref[...] * _sigmoid(gbh_ref[...]), 0.0)
        dpad[hb:, :] = ga_ref[...] * _sigmoid(gb_ref[...])
        s3pad[0:t, :] = ds3_ref[...]
        s3pad[t:, :] = jnp.where(last, 0.0, ds3h_ref[...])
        d1pad[0:t, :] = dd1_ref[...]
        d1pad[t:, :] = jnp.where(last, 0.0, dd1h_ref[...])

        @pl.when(first)
        def _():
            dw3_ref[...] = jnp.zeros_like(dw3_ref)
            dw31_ref[...] = jnp.zeros_like(dw31_ref)

        def fold(v):
            return jnp.sum(v.reshape(v.shape[0] // SUBLANES, SUBLANES, LANES), axis=0)

        groups = range(0, t, CONV_ROWS)
        for c0 in range(0, w, LANES):
            cs = slice(c0, c0 + LANES)
            ds3v = s3pad[0:t, cs]
            dm = jnp.zeros((t, LANES), F32)
            for kk in range(k3):
                dm = dm + w3_ref[kk:kk + 1, cs] * s3pad[k3 - 1 - kk:k3 - 1 - kk + t, cs]
                off = hb - (k3 - 1) + kk
                dw3_ref[SUBLANES * kk:SUBLANES * (kk + 1), cs] += fold(ds3v * mpad[off:off + t, cs])
            dcc_ref[:, cs] = (dm * hc_ref[:, cs]).astype(BF16)
            dhc_ref[:, cs] = (dm * cc_ref[:, cs]).astype(BF16)
            _make_shifts(d1pad, cs, sh_ref)
            for r0 in groups:
                rows = slice(r0, r0 + CONV_ROWS)
                dd0 = jnp.zeros((CONV_ROWS, LANES), F32)
                for kk in _by_shift(k31, -(k31 - 1), -1):
                    dd0 = dd0 + w31_ref[kk:kk + 1, cs] * _window(d1pad, cs, sh_ref, k31 - 1 - kk + r0, CONV_ROWS)
                sgb = _sigmoid(gb_ref[rows, cs])
                dga_ref[rows, cs] = (dd0 * sgb).astype(BF16)
                dgb_ref[rows, cs] = (dd0 * ga_ref[rows, cs] * sgb * (1.0 - sgb)).astype(BF16)
            _make_shifts(dpad, cs, sh_ref)
            for kk in _by_shift(k31, hb - (k31 - 1)):
                part = jnp.zeros((SUBLANES, LANES), F32)
                for r0 in groups:
                    part = part + fold(d1pad[r0:r0 + CONV_ROWS, cs]
                                       * _window(dpad, cs, sh_ref, hb - (k31 - 1) + kk + r0, CONV_ROWS))
                dw31_ref[SUBLANES * kk:SUBLANES * (kk + 1), cs] += part

    col = lambda j: pl.BlockSpec((t, w), lambda i: (i, j))
    pre = lambda j: pl.BlockSpec((hb, w), lambda i: (jnp.maximum(i * (t // hb) - 1, 0), j))
    row = pl.BlockSpec((t, w), lambda i: (i, 0))
    post = lambda h: pl.BlockSpec((h, w), lambda i: (jnp.minimum((i + 1) * (t // h), s // h - 1), 0))
    full = lambda a: pl.BlockSpec(a.shape, lambda i: (0, 0))
    dp, dw3, dw31 = pl.pallas_call(
        body, name=name, grid=(nt,),
        in_specs=[col(0), col(2), col(3), col(4), pre(0), pre(2), pre(3), pre(4),
                  row, post(ha), row, post(hb), row, pl.BlockSpec((t, 2 * w), lambda i: (i, 0)), full(sconv_w), full(dconv_w)],
        out_specs=[pl.BlockSpec((t, 7 * w), lambda i: (i, 0)), pl.BlockSpec((SUBLANES * k3, w), lambda i: (0, 0)),
                   pl.BlockSpec((SUBLANES * k31, w), lambda i: (0, 0))],
        out_shape=[jax.ShapeDtypeStruct((s, 7 * w), BF16),
                   jax.ShapeDtypeStruct((SUBLANES * k3, w), F32), jax.ShapeDtypeStruct((SUBLANES * k31, w), F32)],
        scratch_shapes=[pltpu.VMEM((hb + t, w), F32)] * 2 + [pltpu.VMEM((t + ha, w), F32), pltpu.VMEM((t + hb, w), F32),
                                                             pltpu.VMEM((SUBLANES - 1, hb + t - SUBLANES, LANES), F32)],
        compiler_params=_cp("arbitrary"))(p, p, p, p, p, p, p, p, ds3, ds3, dd1, dd1, dbc, dgate, sconv_w, dconv_w)
    return dp, jnp.sum(dw3.reshape(k3, SUBLANES, w), axis=1), jnp.sum(dw31.reshape(k31, SUBLANES, w), axis=1)


def _mm_in_bwd(dp, w3, x, g_pre, dres, post, name, comm=None):
    s = dp.shape[0]
    nsh, d, ns = w3.shape
    t = 512 if s % 512 == 0 else ROW_TILE
    nt = s // t
    ks = 2 if (ns // 2) % LANES == 0 else 1
    nk, kw = nsh * ks, ns // ks
    chunk = 128
    nchunk = t // chunk
    row = pl.BlockSpec((t, d), lambda i, k: (i, 0))
    vec = pl.BlockSpec((1, d), lambda i, k: (0, 0))
    rowwise = [x, dres] + ([post[0]] if post is not None else [])
    in_specs = [pl.BlockSpec((t, kw), lambda i, k: (i, k)), pl.BlockSpec((None, d, kw), lambda i, k: (k // ks, 0, k % ks)), vec]
    out_specs = [row, vec]
    out_shape = [jax.ShapeDtypeStruct((s, d), F32), jax.ShapeDtypeStruct((1, d), F32)]
    args = [dp, w3, g_pre]
    if post is not None:
        in_specs += [vec]
        out_specs += [row, vec]
        out_shape += [jax.ShapeDtypeStruct((s, d), BF16), jax.ShapeDtypeStruct((1, d), F32)]
        args += [post[1]]
    n_blocked = len(in_specs)
    in_specs += [ANY] * len(rowwise)
    args += rowwise
    host = _Host(comm, in_specs, out_specs, out_shape,
                 [pltpu.VMEM((t, d), F32), pltpu.VMEM((len(rowwise), 2, chunk, d), F32), pltpu.SemaphoreType.DMA((len(rowwise), 2))])

    def body(*refs):
        ins, outs, (acc_ref, buf_ref, sem_ref) = host.split(refs)
        dp_ref, w_ref, g_ref = ins[:3]
        hbm = ins[n_blocked:]
        dx_ref, dg_ref = outs[:2]
        tile = pl.program_id(0)
        kk = pl.program_id(1)
        first = tile == 0
        step = tile * nk + kk
        host.before(step, nt * nk)
        part = _nt(dp_ref[...], w_ref[...])

        @pl.when(kk == 0)
        def _():
            acc_ref[...] = part

        @pl.when(kk > 0)
        def _():
            acc_ref[...] += part

        def fetch(ci, slot):
            return [pltpu.make_async_copy(src.at[pl.ds(tile * t + ci * chunk, chunk)], buf_ref.at[n, slot], sem_ref.at[n, slot])
                    for n, src in enumerate(hbm)]

        @pl.when(kk == nk - 1)
        def _():
            dg = dgp = None
            for cp in fetch(0, 0):
                cp.start()
            for ci in range(nchunk):
                slot = ci % 2
                if ci + 1 < nchunk:
                    for cp in fetch(ci + 1, 1 - slot):
                        cp.start()
                for cp in fetch(ci, slot):
                    cp.wait()
                rows = slice(ci * chunk, (ci + 1) * chunk)
                xhat, r = _rms_stats(buf_ref[0, slot])
                dxn, dg_part = _rms_bwd(acc_ref[rows, :], xhat, r, g_ref[...])
                dx = buf_ref[1, slot] + dxn
                dx_ref[rows, :] = dx
                dg = dg_part if dg is None else dg + dg_part
                if post is not None:
                    ohat, ro = _rms_stats(buf_ref[2, slot])
                    do, dgp_part = _rms_bwd(dx, ohat, ro, ins[3][...])
                    outs[2][rows, :] = do.astype(BF16)
                    dgp = dgp_part if dgp is None else dgp + dgp_part
            _acc_rows(dg_ref, first, dg)
            if post is not None:
                _acc_rows(outs[3], first, dgp)

        host.after(step, nt * nk)

    res = pl.pallas_call(
        body, name=name, grid=(nt, nk), in_specs=host.in_specs, out_specs=host.out_specs, out_shape=host.out_shape,
        scratch_shapes=host.scratch, input_output_aliases=host.aliases,
        compiler_params=_cp("arbitrary", "arbitrary"))(*args, *host.args)
    return host.results(res)


def _half_add(g, r1, c_arr, name, after=None):
    nsh, rows, ns = g.shape
    h = rows // 2
    tr = min(ROW_TILE, h)
    per = h // tr

    def body(c_ref, g_ref, r_ref, *rest):
        rest[-1][...] = (g_ref[...].astype(F32) + r_ref[...].astype(F32)).astype(BF16)

    spec = pl.BlockSpec((None, tr, ns), lambda s, r, c: (s, r, 0))
    ordering = [] if after is None else [after]
    return pl.pallas_call(
        body, name=name,
        grid_spec=pltpu.PrefetchScalarGridSpec(
            num_scalar_prefetch=1, grid=(nsh, per),
            in_specs=[pl.BlockSpec((None, tr, ns), lambda s, r, c: (s, c[0] * per + r, 0)), spec] + [ANY] * len(ordering),
            out_specs=spec),
        out_shape=jax.ShapeDtypeStruct((nsh, h, ns), BF16), compiler_params=_cp("parallel", "parallel"))(c_arr, g, r1, *ordering)


def _sum_chips(hh, r2, mc_arr, name, after=None):
    _, h, ns = hh.shape
    tr = min(ROW_TILE, h)
    per = h // tr

    def body(mc_ref, h_ref, a_ref, b_ref, c_ref, *rest):
        rest[-1][...] = ((h_ref[...].astype(F32) + a_ref[...].astype(F32)) + b_ref[...].astype(F32)) + c_ref[...].astype(F32)

    got = lambda k: pl.BlockSpec((None, tr, ns), lambda r, mc: (k, r, 0))
    ordering = [] if after is None else [after]
    return pl.pallas_call(
        body, name=name,
        grid_spec=pltpu.PrefetchScalarGridSpec(
            num_scalar_prefetch=1, grid=(per,),
            in_specs=[pl.BlockSpec((None, tr, ns), lambda r, mc: (mc[0], r, 0)), got(0), got(1), got(2)] + [ANY] * len(ordering),
            out_specs=pl.BlockSpec((tr, ns), lambda r, mc: (mc[1] * per + r, 0))),
        out_shape=jax.ShapeDtypeStruct((2 * h, ns), F32), compiler_params=_cp("parallel"))(mc_arr, hh, r2, r2, r2, *ordering)


def _add2(a, b, name):
    def body(a_ref, b_ref, o_ref):
        o_ref[...] = a_ref[...] + b_ref[...]

    return pl.pallas_call(body, name=name, out_shape=jax.ShapeDtypeStruct(a.shape, a.dtype), compiler_params=_cp())(a, b)


def _sum_chips_ordered(s2, r2, mc_arr, name):
    rows, w = s2.shape
    rh = rows // 2

    def body(mc_ref, s_ref, a_ref, b_ref, c_ref, o_ref):
        me = mc_ref[0]
        acc = None
        for j in range(N_CHIPS):
            rel = jnp.bitwise_xor(me, j)
            v = jnp.where(rel == 0, s_ref[...], jnp.where(rel == 2, a_ref[...], jnp.where(rel == 1, b_ref[...], c_ref[...])))
            acc = v if acc is None else acc + v
        o_ref[...] = acc

    got = lambda k: pl.BlockSpec((None, rh, w), lambda i, mc: (k, 0, 0))
    return pl.pallas_call(
        body, name=name,
        grid_spec=pltpu.PrefetchScalarGridSpec(
            num_scalar_prefetch=1, grid=(1,),
            in_specs=[pl.BlockSpec((rh, w), lambda i, mc: (mc[1], 0)), got(0), got(1), got(2)],
            out_specs=pl.BlockSpec((rh, w), lambda i, mc: (mc[1], 0))),
        out_shape=jax.ShapeDtypeStruct((rows, w), F32), compiler_params=_cp("arbitrary"))(mc_arr, s2, r2, r2, r2)


def _adamw(w, g, m, v, name, comm=None):
    r, c = w.shape
    tr = ROW_TILE if r % ROW_TILE == 0 else r
    c1 = 1.0 / (1.0 - ADAM_B1 ** ADAM_STEP)
    c2 = 1.0 / (1.0 - ADAM_B2 ** ADAM_STEP)
    spec = pl.BlockSpec((tr, c), lambda i: (i, 0))
    host = _Host(comm, [spec] * 4, [spec] * 4, [jax.ShapeDtypeStruct((r, c), F32)] * 4, [])

    def body(*refs):
        (w_ref, g_ref, m_ref, v_ref), (go_ref, d_ref, nm_ref, nv_ref), _ = host.split(refs)
        step = pl.program_id(0)
        host.before(step, r // tr)
        gv = g_ref[...]
        go_ref[...] = gv
        nm = ADAM_B1 * m_ref[...] + (1.0 - ADAM_B1) * gv
        nv = ADAM_B2 * v_ref[...] + (1.0 - ADAM_B2) * (gv * gv)
        nm_ref[...] = nm
        nv_ref[...] = nv
        d_ref[...] = -ADAM_LR * ((nm * c1) / (jnp.sqrt(nv * c2) + ADAM_EPS) + ADAM_WD * w_ref[...])
        host.after(step, r // tr)

    outs = pl.pallas_call(
        body, name=name, grid=(r // tr,), in_specs=host.in_specs, out_specs=host.out_specs, out_shape=host.out_shape,
        scratch_shapes=host.scratch, input_output_aliases=host.aliases,
        compiler_params=_cp("arbitrary"))(w, g, m, v, *host.args)
    return host.results(outs)


def _swap_with_sibling(grads, wholes, name):
    n, nw = len(grads), len(wholes)
    halves = [g.shape[1] // 2 for g in grads]

    def body(*refs):
        srcs, dsts = refs[:n + nw], refs[n + nw:2 * (n + nw)]
        ssem, rsem = refs[2 * (n + nw):]
        x, y, c, me, chips, sib = _place()
        cps = [_rcopy(srcs[a].at[:, pl.ds((1 - c) * halves[a], halves[a]), :], dsts[a], ssem.at[a], rsem.at[a], sib)
               for a in range(n)]
        cps += [_rcopy(srcs[a], dsts[a], ssem.at[a], rsem.at[a], sib) for a in range(n, n + nw)]
        for cp in cps:
            cp.start()
        for cp in cps:
            cp.wait_recv()
        for cp in cps:
            cp.wait_send()

    out_shape = [jax.ShapeDtypeStruct((g.shape[0], h, g.shape[2]), g.dtype) for g, h in zip(grads, halves)]
    out_shape += [jax.ShapeDtypeStruct(w.shape, w.dtype) for w in wholes]
    return pl.pallas_call(
        body, name=name, in_specs=[ANY] * (n + nw), out_specs=[ANY] * (n + nw), out_shape=out_shape,
        scratch_shapes=[pltpu.SemaphoreType.DMA((n + nw,)), pltpu.SemaphoreType.DMA((n + nw,))],
        compiler_params=pltpu.CompilerParams(has_side_effects=True))(*grads, *wholes)


def _scatter_start(h, name):
    land = (3,) + h.shape[1:]

    def body(h_ref, land_ref, send_sems, recv_sems, h_thru, land_thru, token):
        x, y, c, me, chips, sib = _place()
        for k, chip in enumerate(chips):
            _rcopy(h_ref.at[2 * chip[0] + chip[1]], land_ref.at[k], send_sems.at[k], recv_sems.at[k], (*chip, c)).start()
        token[...] = jnp.zeros_like(token)

    hbm = pl.BlockSpec(memory_space=pltpu.HBM)
    sem = pl.BlockSpec(memory_space=pltpu.SEMAPHORE)
    return pl.pallas_call(
        body, name=name,
        out_shape=(pltpu.SemaphoreType.DMA((3,)), pltpu.SemaphoreType.DMA((3,)), pltpu.HBM(h.shape, h.dtype),
                   pltpu.HBM(land, h.dtype), jax.ShapeDtypeStruct((8, LANES), F32)),
        in_specs=(hbm, hbm), out_specs=(sem, sem, hbm, hbm, pl.BlockSpec(memory_space=pltpu.VMEM)),
        input_output_aliases={0: 2, 1: 3},
        compiler_params=pltpu.CompilerParams(has_side_effects=pltpu.SideEffectType.DATAFLOW_SIDE_EFFECTING))(
            pltpu.with_memory_space_constraint(h, pltpu.HBM),
            pltpu.with_memory_space_constraint(lax.empty(land, h.dtype), pltpu.HBM))


def _scatter_wait(send_sems, recv_sems, h_thru, land_thru, after, name):
    def body(h_ref, land_ref, send_sems, recv_sems, after_ref, h_dead, got_ref):
        x, y, c, me, chips, sib = _place()
        for k, chip in enumerate(chips):
            cp = _rcopy(h_ref.at[2 * chip[0] + chip[1]], land_ref.at[k], send_sems.at[k], recv_sems.at[k], (*chip, c))
            cp.wait_send()
            cp.wait_recv()

    hbm = pl.BlockSpec(memory_space=pltpu.HBM)
    sem = pl.BlockSpec(memory_space=pltpu.SEMAPHORE)
    return pl.pallas_call(
        body, name=name,
        out_shape=(pltpu.HBM(h_thru.shape, h_thru.dtype), pltpu.HBM(land_thru.shape, land_thru.dtype)),
        in_specs=(hbm, hbm, sem, sem, ANY), out_specs=(hbm, hbm), input_output_aliases={0: 0, 1: 1},
        compiler_params=pltpu.CompilerParams(has_side_effects=pltpu.SideEffectType.DATAFLOW_SIDE_EFFECTING))(
            h_thru, land_thru, send_sems, recv_sems, after)


def _swap_start(g, name):
    h = g.shape[1] // 2
    land = (g.shape[0], h, g.shape[2])

    def body(g_ref, land_ref, send_sem, recv_sem, g_thru, land_thru, token):
        x, y, c, me, chips, sib = _place()
        _rcopy(g_ref.at[:, pl.ds((1 - c) * h, h), :], land_ref, send_sem.at[0], recv_sem.at[0], sib).start()
        token[...] = jnp.zeros_like(token)

    hbm = pl.BlockSpec(memory_space=pltpu.HBM)
    sem = pl.BlockSpec(memory_space=pltpu.SEMAPHORE)
    return pl.pallas_call(
        body, name=name,
        out_shape=(pltpu.SemaphoreType.DMA((1,)), pltpu.SemaphoreType.DMA((1,)), pltpu.HBM(g.shape, g.dtype),
                   pltpu.HBM(land, g.dtype), jax.ShapeDtypeStruct((8, LANES), F32)),
        in_specs=(hbm, hbm), out_specs=(sem, sem, hbm, hbm, pl.BlockSpec(memory_space=pltpu.VMEM)),
        input_output_aliases={0: 2, 1: 3},
        compiler_params=pltpu.CompilerParams(has_side_effects=pltpu.SideEffectType.DATAFLOW_SIDE_EFFECTING))(
            pltpu.with_memory_space_constraint(g, pltpu.HBM),
            pltpu.with_memory_space_constraint(lax.empty(land, g.dtype), pltpu.HBM))


def _swap_wait(send_sem, recv_sem, g_thru, land_thru, after, name):
    h = g_thru.shape[1] // 2

    def body(g_ref, land_ref, send_sem, recv_sem, after_ref, g_dead, got_ref):
        x, y, c, me, chips, sib = _place()
        cp = _rcopy(g_ref.at[:, pl.ds((1 - c) * h, h), :], land_ref, send_sem.at[0], recv_sem.at[0], sib)
        cp.wait_send()
        cp.wait_recv()

    hbm = pl.BlockSpec(memory_space=pltpu.HBM)
    sem = pl.BlockSpec(memory_space=pltpu.SEMAPHORE)
    return pl.pallas_call(
        body, name=name,
        out_shape=(pltpu.HBM(g_thru.shape, g_thru.dtype), pltpu.HBM(land_thru.shape, land_thru.dtype)),
        in_specs=(hbm, hbm, sem, sem, ANY), out_specs=(hbm, hbm), input_output_aliases={0: 0, 1: 1},
        compiler_params=pltpu.CompilerParams(has_side_effects=pltpu.SideEffectType.DATAFLOW_SIDE_EFFECTING))(
            g_thru, land_thru, send_sem, recv_sem, after)


def _share_half_start(small, name):
    rh = small.shape[0] // 2
    land = (3, rh, small.shape[1])

    def body(s_ref, land_ref, send_sems, recv_sems, s_thru, land_thru, token):
        x, y, c, me, chips, sib = _place()
        for k, chip in enumerate(chips):
            _rcopy(s_ref.at[pl.ds(c * rh, rh)], land_ref.at[k], send_sems.at[k], recv_sems.at[k], (*chip, c)).start()
        token[...] = jnp.zeros_like(token)

    hbm = pl.BlockSpec(memory_space=pltpu.HBM)
    sem = pl.BlockSpec(memory_space=pltpu.SEMAPHORE)
    return pl.pallas_call(
        body, name=name,
        out_shape=(pltpu.SemaphoreType.DMA((3,)), pltpu.SemaphoreType.DMA((3,)), pltpu.HBM(small.shape, small.dtype),
                   pltpu.HBM(land, small.dtype), jax.ShapeDtypeStruct((8, LANES), F32)),
        in_specs=(hbm, hbm), out_specs=(sem, sem, hbm, hbm, pl.BlockSpec(memory_space=pltpu.VMEM)),
        input_output_aliases={0: 2, 1: 3},
        compiler_params=pltpu.CompilerParams(has_side_effects=pltpu.SideEffectType.DATAFLOW_SIDE_EFFECTING))(
            pltpu.with_memory_space_constraint(small, pltpu.HBM),
            pltpu.with_memory_space_constraint(lax.empty(land, small.dtype), pltpu.HBM))


def _share_half_wait(send_sems, recv_sems, s_thru, land_thru, after, name):
    rh = s_thru.shape[0] // 2

    def body(s_ref, land_ref, send_sems, recv_sems, after_ref, s_dead, got_ref):
        x, y, c, me, chips, sib = _place()
        for k, chip in enumerate(chips):
            cp = _rcopy(s_ref.at[pl.ds(c * rh, rh)], land_ref.at[k], send_sems.at[k], recv_sems.at[k], (*chip, c))
            cp.wait_send()
            cp.wait_recv()

    hbm = pl.BlockSpec(memory_space=pltpu.HBM)
    sem = pl.BlockSpec(memory_space=pltpu.SEMAPHORE)
    return pl.pallas_call(
        body, name=name,
        out_shape=(pltpu.HBM(s_thru.shape, s_thru.dtype), pltpu.HBM(land_thru.shape, land_thru.dtype)),
        in_specs=(hbm, hbm, sem, sem, ANY), out_specs=(hbm, hbm), input_output_aliases={0: 0, 1: 1},
        compiler_params=pltpu.CompilerParams(has_side_effects=pltpu.SideEffectType.DATAFLOW_SIDE_EFFECTING))(
            s_thru, land_thru, send_sems, recv_sems, after)


def _join_start(parts, name):
    n = len(parts)

    def body(*refs):
        srcs, (send_sems, recv_sems), token = refs[:n], refs[n:n + 2], refs[-1]
        x, y, c, me, chips, sib = _place()
        for a, src in enumerate(srcs):
            h = src.shape[0] // 2
            mine = src.at[pl.ds(c * h, h)]
            _rcopy(mine, mine, send_sems.at[a], recv_sems.at[a], sib).start()
        token[...] = jnp.zeros_like(token)

    hbm = pl.BlockSpec(memory_space=pltpu.HBM)
    sem = pl.BlockSpec(memory_space=pltpu.SEMAPHORE)
    outs = pl.pallas_call(
        body, name=name,
        out_shape=(pltpu.SemaphoreType.DMA((n,)), pltpu.SemaphoreType.DMA((n,)))
        + tuple(pltpu.HBM(p.shape, p.dtype) for p in parts) + (jax.ShapeDtypeStruct((8, LANES), F32),),
        in_specs=(hbm,) * n, out_specs=(sem, sem) + (hbm,) * n + (pl.BlockSpec(memory_space=pltpu.VMEM),),
        input_output_aliases={a: 2 + a for a in range(n)},
        compiler_params=pltpu.CompilerParams(has_side_effects=pltpu.SideEffectType.DATAFLOW_SIDE_EFFECTING))(
            *[pltpu.with_memory_space_constraint(p, pltpu.HBM) for p in parts])
    return outs[0], outs[1], list(outs[2:2 + n]), outs[-1]


def _join_wait(send_sems, recv_sems, parts, after, name):
    n = len(parts)

    def body(*refs):
        srcs, (send_sems, recv_sems) = refs[:n], refs[n:n + 2]
        x, y, c, me, chips, sib = _place()
        for a, src in enumerate(srcs):
            h = src.shape[0] // 2
            mine, theirs = src.at[pl.ds(c * h, h)], src.at[pl.ds((1 - c) * h, h)]
            _rcopy(mine, theirs, send_sems.at[a], recv_sems.at[a], sib).wait_send()
            _rcopy(theirs, theirs, send_sems.at[a], recv_sems.at[a], sib).wait_recv()

    hbm = pl.BlockSpec(memory_space=pltpu.HBM)
    sem = pl.BlockSpec(memory_space=pltpu.SEMAPHORE)
    return pl.pallas_call(
        body, name=name, out_shape=tuple(pltpu.HBM(p.shape, p.dtype) for p in parts),
        in_specs=(hbm,) * n + (sem, sem, ANY), out_specs=(hbm,) * n, input_output_aliases={a: a for a in range(n)},
        compiler_params=pltpu.CompilerParams(has_side_effects=pltpu.SideEffectType.DATAFLOW_SIDE_EFFECTING))(
            *parts, send_sems, recv_sems, after)


def _pad_rows(a, rows):
    return jnp.pad(a, ((0, rows - a.shape[0]), (0, 0)))


def _stack_rows(parts, multiple):
    padded = [_pad_rows(p, -(-p.shape[0] // 8) * 8) for p in parts]
    starts, at = [], 0
    for p in padded:
        starts.append(at)
        at += p.shape[0]
    total = -(-at // multiple) * multiple
    if total > at:
        padded.append(jnp.zeros((total - at, parts[0].shape[1]), parts[0].dtype))
    return jnp.concatenate(padded, axis=0), starts


def kernel(x, ln_pre_even, w_in_even, pool_w, pool_scale, w_out_even, ln_post_even, ln_pre_odd, w_in_odd, sconv_w, dconv_w, dconv_b, cnorm_g, cnorm_b, w_out_odd, ln_post_odd, loss_target, m_ln_pre_even, m_w_in_even, m_pool_w, m_pool_scale, m_w_out_even, m_ln_post_even, m_ln_pre_odd, m_w_in_odd, m_sconv_w, m_dconv_w, m_dconv_b, m_cnorm_g, m_cnorm_b, m_w_out_odd, m_ln_post_odd, v_ln_pre_even, v_w_in_even, v_pool_w, v_pool_scale, v_w_out_even, v_ln_post_even, v_ln_pre_odd, v_w_in_odd, v_sconv_w, v_dconv_w, v_dconv_b, v_cnorm_g, v_cnorm_b, v_w_out_odd, v_ln_post_odd):
    _, s, d = x.shape
    half = d // 2
    cw = half // N_CHIPS
    ng, q, gd = pool_w.shape[1:]
    k3, k31 = sconv_w.shape[1], dconv_w.shape[1]
    x2d, tgt = x[0], loss_target[0]
    me = 2 * lax.axis_index("x") + lax.axis_index("y")
    core = lax.axis_index("c")
    c_arr = jnp.reshape(core, (1,)).astype(jnp.int32)
    me_arr = jnp.reshape(me, (1,)).astype(jnp.int32)
    mc_arr = jnp.stack([me, core]).astype(jnp.int32)

    shards = [w_in_even[0], w_out_even[0], w_in_odd[0], w_out_odd[0]]
    pool_w_b = _cast_bf16(pool_w[0].reshape(ng * q, gd), "cast_pool_w").reshape(ng, q, gd)
    pack_w, at_w = _stack_rows([sconv_w[0], dconv_w[0], dconv_b, cnorm_g, cnorm_b], 8)
    pack_d, at_d = _stack_rows([ln_pre_odd, ln_post_odd], 8)
    placed = [lax.dynamic_update_slice(jnp.zeros((ng, N_CHIPS * q, gd), BF16), pool_w_b, (0, me * q, 0)),
              lax.dynamic_update_slice(jnp.zeros((pack_w.shape[0], N_CHIPS * cw), F32), pack_w, (0, me * cw)),
              lax.dynamic_update_slice(jnp.zeros((pack_d.shape[0], d), F32), pack_d, (0, me * (d // N_CHIPS)))]
    plans = _Multi([_GatherPieces([_cast_bf16_own_slab(shards[0], me_arr, "cast_w0")], GATHER_PIECES, (0.3, 0.9)),
                    _SmallGatherPlan(placed, (q, cw, d // N_CHIPS))])
    h0, others, extra = _prep(x2d, ln_pre_even, shards[1:], me_arr, "prep_and_gather_first", plans)
    (win_e,), (pool_w_f, pack_w_f, pack_d_f) = plans.results(extra)
    slabs = [None] + others
    sconv_f = pack_w_f[at_w[0]:at_w[0] + k3]
    dconv_f = pack_w_f[at_w[1]:at_w[1] + k31]
    dconv_b_f, cnorm_g_f, cnorm_b_f = (pack_w_f[at_w[n]:at_w[n] + 1] for n in (2, 3, 4))
    ln_pre_odd_f = pack_d_f[at_d[0]:at_d[0] + 1]
    ln_post_odd_f = pack_d_f[at_d[1]:at_d[1] + 1]

    plans = _Multi([_GatherPlan([slabs[1]], at=(0.6, 0.88)), _GatherPlan([slabs[2]], (0, 1, 4), at=(0.6, 0.88))])
    p_e, extra = _mm_nn(h0, win_e, "proj_in_even", plans)
    (wout_e,), (win_o,) = plans.results(extra)
    wout_e = wout_e.reshape(d, d)
    att, ltot, (win_o,) = _sba_fwd(p_e, half, "sba_fwd", _GatherPlan([win_o], (1, 4, 4), at=(0.69, 0.94)))
    y_e = _even_mix_fwd(p_e, att, pool_w_f, pool_scale, d, "even_mix_fwd")
    o_e, x1, h1 = _mm_out_even(y_e, wout_e, x2d, ln_post_even, ln_pre_odd_f, "proj_out_even")
    p_o, (wout_o,) = _mm_nn(h1, win_o, "proj_in_odd", _GatherPlan([slabs[3]]))
    wout_o = wout_o.reshape(d, d)
    y_o, s3, d1 = _odd_mix_fwd(p_o, sconv_f, dconv_f, dconv_b_f, cnorm_g_f, cnorm_b_f, d, "odd_mix_fwd")
    do_o, dx2, loss_blk, dln_post_odd = _mm_out_odd(y_o, wout_o, x1, ln_post_odd_f, tgt, "proj_out_odd_loss")

    dy_o = _mm_nt(do_o, wout_o, "dy_odd")
    g_wout_o = _mm_tn(y_o, do_o, 1, "dw_out_odd")[0].reshape(N_CHIPS, d // N_CHIPS, d)
    (dbc, dgate_o, ds3, dd1, dcnorm_g, dcnorm_b, ddconv_b), (got,) = _odd_bwd_rows(
        p_o, s3, d1, dy_o, cnorm_g_f, cnorm_b_f, d, "odd_bwd_rows", _SwapPlan([g_wout_o]))
    h_wout_o = _half_add(g_wout_o, got, c_arr, "half_add_out_odd")
    dp_o, dsconv, ddconv = _odd_bwd_conv(p_o, ds3, dd1, dbc, dgate_o, sconv_f, dconv_f, d, "odd_bwd_conv")
    g_win_o, (s_wout_o,) = _mm_tn(h1, dp_o, N_CHIPS, "dw_in_odd", _ScatterPlan([h_wout_o]))
    (dx1, dln_pre_odd, do_e, dln_post_even), (got,) = _mm_in_bwd(
        dp_o, win_o, x1, ln_pre_odd_f, dx2, (o_e, ln_post_even), "dx_odd", _SwapPlan([g_win_o]))
    h_win_o = _half_add(g_win_o, got, c_arr, "half_add_in_odd")

    dy_e = _mm_nt(do_e, wout_e, "dy_even")
    g_wout_e = _mm_tn(y_e, do_e, 1, "dw_out_even")[0].reshape(N_CHIPS, d // N_CHIPS, d)
    (datt, du, dgate_e, dpool_scale, dpool_w), (got,) = _even_mix_bwd(
        p_e, att, dy_e, pool_w_f, pool_scale, d, "even_mix_bwd", _SwapPlan([g_wout_e]))
    h_wout_e = _half_add(g_wout_e, got, c_arr, "half_add_out_even")
    two = lambda v: v.reshape(2, half)
    small_parts = [dpool_scale, two(dln_post_even), two(dln_pre_odd), two(dln_post_odd),
                   dsconv, ddconv, ddconv_b, dcnorm_g, dcnorm_b, dpool_w.reshape(gd, half)]
    small, at_s = _stack_rows(small_parts, 16)
    plans = _Multi([_ScatterPlan([h_win_o]), _SendWholePlan([small])])
    dq, dk, dv, extra = _sba_bwd(p_e, ltot, datt, half, "sba_bwd", plans)
    (s_win_o,), (small1,) = plans.results(extra)
    small2 = _add2(small, small1, "small_add")
    dp_e = _assemble([dq, dk, dv, du, dgate_e], "assemble_dp_even")
    plans = _Multi([_ScatterPlan([h_wout_e]), _ShareHalfPlan([small2])])
    g_win_e, extra = _mm_tn(h0, dp_e, N_CHIPS, "dw_in_even", plans)
    (s_wout_e,), (small_got,) = plans.results(extra)
    swap = _swap_start(g_win_e, "swap_in_even_start")
    pairs = [(h_wout_e, s_wout_e), (h_win_o, s_win_o), (h_wout_o, s_wout_o)]
    parts = []
    for n, (h, r) in enumerate(pairs):
        parts.append(_sum_chips(h, r, mc_arr, f"sum_chips{n + 1}", after=parts[-1] if parts else swap[4]))
    g_win_e, got = _swap_wait(*swap[:4], parts[-1], "swap_in_even_wait")
    parts.append(_sum_chips_ordered(small2, small_got, mc_arr, "small_sum"))
    join_sems = _join_start(parts, "join_first_start")
    h_win_e = _half_add(g_win_e, got, c_arr, "half_add_in_even", after=join_sems[3])
    send_sems, recv_sems, h_win_e, landing, token = _scatter_start(h_win_e, "scatter_in_even_start")
    (grad_x, dln_pre_even), _ = _mm_in_bwd(dp_e, win_e, x2d, ln_pre_even + token[0:1, 0:1], dx1, None, "dx_even")

    last, at_l = _stack_rows([two(dln_pre_even), jnp.pad(loss_blk[0:1], ((0, 0), (0, half - LANES)))], 16)
    (last1,) = _swap_with_sibling([], [last], "swap_last")
    last2 = _add2(last, last1, "last_add")
    share = _share_half_start(last2, "share_last_start")
    gw_out_e, gw_in_o, gw_out_o, red = _join_wait(*join_sems[:3], share[4], "join_first_wait")

    def rows(n, cnt):
        return red[at_s[n]:at_s[n] + cnt]

    def mine(a, width):
        return lax.dynamic_slice_in_dim(a, me * width, width, axis=1)

    quarter = d // N_CHIPS
    g_small = {
        "pool_scale": rows(0, 1),
        "ln_post_even": rows(1, 2).reshape(1, d),
        "ln_pre_odd": mine(rows(2, 2).reshape(1, d), quarter),
        "ln_post_odd": mine(rows(3, 2).reshape(1, d), quarter),
        "sconv_w": mine(rows(4, k3), cw),
        "dconv_w": mine(rows(5, k31), cw),
        "dconv_b": mine(rows(6, 1), cw),
        "cnorm_g": mine(rows(7, 1), cw),
        "cnorm_b": mine(rows(8, 1), cw),
        "pool_w": lax.dynamic_slice_in_dim(rows(9, gd).reshape(ng, gd, gd), me * q, q, axis=1).reshape(ng * q, gd),
    }
    w2d = {
        "ln_pre_even": ln_pre_even, "w_in_even": w_in_even[0], "pool_w": pool_w[0].reshape(ng * q, gd),
        "pool_scale": pool_scale, "w_out_even": w_out_even[0], "ln_post_even": ln_post_even, "ln_pre_odd": ln_pre_odd,
        "w_in_odd": w_in_odd[0], "sconv_w": sconv_w[0], "dconv_w": dconv_w[0], "dconv_b": dconv_b, "cnorm_g": cnorm_g,
        "cnorm_b": cnorm_b, "w_out_odd": w_out_odd[0], "ln_post_odd": ln_post_odd,
    }
    moments = {
        "ln_pre_even": (m_ln_pre_even, v_ln_pre_even), "w_in_even": (m_w_in_even, v_w_in_even),
        "pool_w": (m_pool_w, v_pool_w), "pool_scale": (m_pool_scale, v_pool_scale),
        "w_out_even": (m_w_out_even, v_w_out_even), "ln_post_even": (m_ln_post_even, v_ln_post_even),
        "ln_pre_odd": (m_ln_pre_odd, v_ln_pre_odd), "w_in_odd": (m_w_in_odd, v_w_in_odd),
        "sconv_w": (m_sconv_w, v_sconv_w), "dconv_w": (m_dconv_w, v_dconv_w), "dconv_b": (m_dconv_b, v_dconv_b),
        "cnorm_g": (m_cnorm_g, v_cnorm_g), "cnorm_b": (m_cnorm_b, v_cnorm_b),
        "w_out_odd": (m_w_out_odd, v_w_out_odd), "ln_post_odd": (m_ln_post_odd, v_ln_post_odd),
    }
    def update(name, g):
        m_in, v_in = moments[name]
        w = w2d[name]
        return _adamw(w, g, m_in.reshape(w.shape), v_in.reshape(w.shape), "adamw_" + name)[0]

    updates = {name: update(name, g) for name, g in (("w_in_odd", gw_in_o), ("w_out_even", gw_out_e), ("w_out_odd", gw_out_o))}
    last2, last_got = _share_half_wait(*share[:4], updates["w_out_odd"][1], "share_last_wait")
    last_sum = _sum_chips_ordered(last2, last_got, mc_arr, "last_sum")
    h_win_e, s_win_e = _scatter_wait(send_sems, recv_sems, h_win_e, landing, last_sum, "scatter_in_even_wait")
    last_sems = _join_start([_sum_chips(h_win_e, s_win_e, mc_arr, "sum_chips0"), last_sum], "join_last_start")
    for name, g in g_small.items():
        updates[name] = update(name, g)
    gw_in_e, red_last = _join_wait(*last_sems[:3], updates["pool_w"][1], "join_last_wait")
    loss = red_last[at_l[1], 0]
    updates["ln_pre_even"] = update("ln_pre_even", red_last[at_l[0]:at_l[0] + 2].reshape(1, d))
    updates["w_in_even"] = update("w_in_even", gw_in_e)
    outs = [[u.reshape(moments[name][0].shape) for u in updates[name]] for name in w2d]
    grads_out, deltas, new_m, new_v = zip(*outs)
    return (loss, grad_x.reshape(x.shape), *grads_out, *deltas, *new_m, *new_v)
```
